```python
import math
import jax, jax.numpy as jnp
from jax import lax
import numpy as np

D_MODEL = 1024
BATCH = 16
SEQ = 2048
DEPTH = 2

SSD_WIDTH = D_MODEL
SSD_HEAD_DIM = 64
SSD_HEADS = SSD_WIDTH // SSD_HEAD_DIM
SSD_GROUPS = 2
SSD_STATE = 128
SSD_CONV = 4
SSD_CHUNK = 128
SSD_CONV_DIM = SSD_WIDTH + 2 * SSD_GROUPS * SSD_STATE
ATTN_HEAD_DIM = 64
ATTN_WIDTH = D_MODEL // 2
ATTN_Q_HEADS = ATTN_WIDTH // ATTN_HEAD_DIM
ATTN_KV_HEADS = 2
WINDOW = 128
CONF_WIDTH = D_MODEL // 2
CONF_KERNEL = 31
MIX_WIDTH = SSD_WIDTH + ATTN_WIDTH + CONF_WIDTH
IN_SPLITS = (MIX_WIDTH, SSD_CONV_DIM, SSD_HEADS, ATTN_Q_HEADS * ATTN_HEAD_DIM,
             ATTN_KV_HEADS * ATTN_HEAD_DIM, ATTN_KV_HEADS * ATTN_HEAD_DIM, 2 * CONF_WIDTH)
D_IN_PROJ = sum(IN_SPLITS)
EPS = 1e-5

kernel_name = "hybrid_ssd_swa_conformer_parallel_heads"


def _split(a, sizes):
    idx = np.cumsum(sizes)[:-1].tolist()
    return jnp.split(a, idx, axis=-1)


def rmsnorm(x, w):
    xf = x.astype(jnp.float32)
    y = xf * lax.rsqrt(jnp.mean(xf * xf, axis=-1, keepdims=True) + EPS)
    return (y * w.astype(jnp.float32)).astype(x.dtype)


def gated_group_rmsnorm(y, z, w, groups):
    g = (y * jax.nn.silu(z)).astype(jnp.float32)
    shp = g.shape
    g = g.reshape(shp[:-1] + (groups, shp[-1] // groups))
    g = g * lax.rsqrt(jnp.mean(g * g, axis=-1, keepdims=True) + EPS)
    return (g.reshape(shp) * w.astype(jnp.float32)).astype(y.dtype)


def layernorm(x, w, b):
    xf = x.astype(jnp.float32)
    mu = jnp.mean(xf, axis=-1, keepdims=True)
    xc = xf - mu
    y = xc * lax.rsqrt(jnp.mean(xc * xc, axis=-1, keepdims=True) + EPS)
    return (y * w.astype(jnp.float32) + b.astype(jnp.float32)).astype(x.dtype)


def causal_depthwise_conv(x, w, b):
    K, C = w.shape
    y = lax.conv_general_dilated(x, w[:, None, :].astype(x.dtype), window_strides=(1,),
                                 padding=[(K - 1, 0)], dimension_numbers=('NWC', 'WIO', 'NWC'),
                                 feature_group_count=C)
    return y + b.astype(x.dtype)


def ssd_chunked(x, dt, A, B, C):
    b, l, h, p = x.shape
    g, n = B.shape[2], B.shape[3]
    r = h // g
    nc = l // SSD_CHUNK
    Q = SSD_CHUNK
    x = x.reshape(b, nc, Q, g, r, p)
    dt = dt.reshape(b, nc, Q, g, r)
    B = B.reshape(b, nc, Q, g, n)
    C = C.reshape(b, nc, Q, g, n)
    a = dt * A.reshape(g, r)
    a_cs = jnp.cumsum(a, axis=2)
    xdt = x * dt[..., None]
    seg = a_cs[:, :, :, None] - a_cs[:, :, None, :]
    causal = jnp.tril(jnp.ones((Q, Q), dtype=bool))[None, None, :, :, None, None]
    L = jnp.exp(jnp.where(causal, seg, -jnp.inf))
    cb = jnp.einsum('bcign,bcjgn->bcijg', C, B)
    M = cb[..., None] * L
    y_diag = jnp.einsum('bcijgr,bcjgrp->bcigrp', M, xdt)
    decay_to_end = jnp.exp(a_cs[:, :, -1:] - a_cs)
    states = jnp.einsum('bcjgn,bcjgrp->bcgrpn', B, xdt * decay_to_end[..., None])
    chunk_decay = jnp.exp(a_cs[:, :, -1])

    def step(S, inp):
        st, dec = inp
        return dec[..., None, None] * S + st, S

    S0 = jnp.zeros((b, g, r, p, n), jnp.float32)
    _, prev = lax.scan(step, S0, (jnp.moveaxis(states, 1, 0), jnp.moveaxis(chunk_decay, 1, 0)))
    prev = jnp.moveaxis(prev, 0, 1)
    y_off = jnp.einsum('bcign,bcgrpn->bcigrp', C, prev) * jnp.exp(a_cs)[..., None]
    return (y_diag + y_off).reshape(b, l, h, p)


def swa_gqa_sinks(q, k, v, sinks):
    b, l, g, r, d = q.shape
    W = WINDOW
    nb = l // W
    qb = q.reshape(b, nb, W, g, r, d)
    kb = k.reshape(b, nb, W, g, d)
    vb = v.reshape(b, nb, W, g, d)
    pad = ((0, 0), (1, 0), (0, 0), (0, 0), (0, 0))
    kk = jnp.concatenate([jnp.pad(kb, pad)[:, :-1], kb], axis=2)
    vv = jnp.concatenate([jnp.pad(vb, pad)[:, :-1], vb], axis=2)
    s = jnp.einsum('bnqgrd,bnkgd->bngrqk', qb, kk,
                   preferred_element_type=jnp.float32) * (d ** -0.5)
    qi = jnp.arange(W)[:, None]
    kj = jnp.arange(2 * W)[None, :] - W
    rel = qi - kj
    band = (rel >= 0) & (rel < W)
    mask = band[None] & ((jnp.arange(nb)[:, None, None] > 0) | (kj[None] >= 0))
    s = jnp.where(mask[None, :, None, None], s, -jnp.inf)
    sk = sinks.astype(jnp.float32).reshape(g, r)[None, None, :, :, None, None]
    lse = jnp.logaddexp(jax.nn.logsumexp(s, axis=-1, keepdims=True), sk)
    pr = jnp.exp(s - lse)
    o = jnp.einsum('bngrqk,bnkgd->bnqgrd', pr.astype(v.dtype), vv)
    return o.reshape(b, l, g * r * d)


def hybrid_mixer(h, w_in, conv_w, conv_b, dt_bias, a_log, d_skip, ssd_norm_w, sinks,
                 dw_w, dw_b, ln_w, ln_b, w_out):
    b, l, _ = h.shape
    proj = h @ w_in
    z, xbc, dt, q, k, v, conf = _split(proj, IN_SPLITS)
    z_ssd, z_attn, z_conf = _split(z, (SSD_WIDTH, ATTN_WIDTH, CONF_WIDTH))
    xbc = jax.nn.silu(causal_depthwise_conv(xbc, conv_w, conv_b))
    xs, Bs, Cs = _split(xbc, (SSD_WIDTH, SSD_GROUPS * SSD_STATE, SSD_GROUPS * SSD_STATE))
    xs = xs.reshape(b, l, SSD_HEADS, SSD_HEAD_DIM).astype(jnp.float32)
    Bs = Bs.reshape(b, l, SSD_GROUPS, SSD_STATE).astype(jnp.float32)
    Cs = Cs.reshape(b, l, SSD_GROUPS, SSD_STATE).astype(jnp.float32)
    dtp = jax.nn.softplus(dt.astype(jnp.float32) + dt_bias.astype(jnp.float32))
    A = -jnp.exp(a_log.astype(jnp.float32))
    y = ssd_chunked(xs, dtp, A, Bs, Cs) + d_skip.astype(jnp.float32)[:, None] * xs
    y_ssd = gated_group_rmsnorm(y.reshape(b, l, SSD_WIDTH).astype(h.dtype), z_ssd,
                                ssd_norm_w, SSD_GROUPS)
    rep = ATTN_Q_HEADS // ATTN_KV_HEADS
    qh = q.reshape(b, l, ATTN_KV_HEADS, rep, ATTN_HEAD_DIM)
    kh = k.reshape(b, l, ATTN_KV_HEADS, ATTN_HEAD_DIM)
    vh = v.reshape(b, l, ATTN_KV_HEADS, ATTN_HEAD_DIM)
    y_attn = swa_gqa_sinks(qh, kh, vh, sinks) * jax.nn.silu(z_attn)
    ca, cg = _split(conf, (CONF_WIDTH, CONF_WIDTH))
    c = ca * jax.nn.sigmoid(cg)
    c = causal_depthwise_conv(c, dw_w, dw_b)
    c = jax.nn.silu(layernorm(c, ln_w, ln_b))
    y_conf = c * jax.nn.silu(z_conf)
    return jnp.concatenate([y_ssd, y_attn, y_conf], axis=-1) @ w_out


def _fwd_setup_inputs(seed: int = 0) -> dict:
    key = jax.random.key(seed)
    ks = jax.random.split(key, 20)
    f32 = jnp.float32
    nrm = lambda k, s, sc: jax.random.normal(k, s, f32) * sc
    x = jax.random.normal(ks[0], (BATCH, SEQ, D_MODEL), f32)
    norm_w = 1.0 + nrm(ks[1], (DEPTH, D_MODEL), 0.02)
    w_in = nrm(ks[2], (DEPTH, D_MODEL, D_IN_PROJ), D_MODEL ** -0.5)
    ssd_conv_w = nrm(ks[3], (DEPTH, SSD_CONV, SSD_CONV_DIM), SSD_CONV ** -0.5)
    ssd_conv_b = nrm(ks[4], (DEPTH, SSD_CONV_DIM), 0.02)
    dt0 = jnp.exp(jax.random.uniform(ks[5], (DEPTH, SSD_HEADS), f32,
                                     math.log(1e-3), math.log(1e-1)))
    ssd_dt_bias = dt0 + jnp.log(-jnp.expm1(-dt0))
    ssd_a_log = jnp.log(jax.random.uniform(ks[6], (DEPTH, SSD_HEADS), f32, 1.0, 16.0))
    ssd_d = 1.0 + nrm(ks[7], (DEPTH, SSD_HEADS), 0.1)
    ssd_norm_w = 1.0 + nrm(ks[8], (DEPTH, SSD_WIDTH), 0.02)
    attn_sinks = nrm(ks[9], (DEPTH, ATTN_Q_HEADS), 1.0)
    conf_dw_w = nrm(ks[10], (DEPTH, CONF_KERNEL, CONF_WIDTH), CONF_KERNEL ** -0.5)
    conf_dw_b = nrm(ks[11], (DEPTH, CONF_WIDTH), 0.02)
    conf_ln_w = 1.0 + nrm(ks[12], (DEPTH, CONF_WIDTH), 0.02)
    conf_ln_b = nrm(ks[13], (DEPTH, CONF_WIDTH), 0.02)
    w_out = nrm(ks[14], (DEPTH, MIX_WIDTH, D_MODEL), MIX_WIDTH ** -0.5)
    final_norm_w = 1.0 + nrm(ks[15], (D_MODEL,), 0.02)
    return {"x": x, "norm_w": norm_w, "w_in": w_in, "ssd_conv_w": ssd_conv_w,
            "ssd_conv_b": ssd_conv_b, "ssd_dt_bias": ssd_dt_bias, "ssd_a_log": ssd_a_log,
            "ssd_d": ssd_d, "ssd_norm_w": ssd_norm_w, "attn_sinks": attn_sinks,
            "conf_dw_w": conf_dw_w, "conf_dw_b": conf_dw_b, "conf_ln_w": conf_ln_w,
            "conf_ln_b": conf_ln_b, "w_out": w_out, "final_norm_w": final_norm_w}


def _fwd_reference(x, norm_w, w_in, ssd_conv_w, ssd_conv_b, ssd_dt_bias, ssd_a_log, ssd_d,
              ssd_norm_w, attn_sinks, conf_dw_w, conf_dw_b, conf_ln_w, conf_ln_b, w_out,
              final_norm_w):
    for i in range(DEPTH):
        h = rmsnorm(x, norm_w[i])
        x = x + hybrid_mixer(h, w_in[i], ssd_conv_w[i], ssd_conv_b[i], ssd_dt_bias[i],
                             ssd_a_log[i], ssd_d[i], ssd_norm_w[i], attn_sinks[i],
                             conf_dw_w[i], conf_dw_b[i], conf_ln_w[i], conf_ln_b[i], w_out[i])
    return rmsnorm(x, final_norm_w)


import jax as _jax
import jax.numpy as _jnp

TWIN_FORMAT = 'train_step'
FWD_PARAMS = ['x', 'norm_w', 'w_in', 'ssd_conv_w', 'ssd_conv_b', 'ssd_dt_bias', 'ssd_a_log', 'ssd_d', 'ssd_norm_w', 'attn_sinks', 'conf_dw_w', 'conf_dw_b', 'conf_ln_w', 'conf_ln_b', 'w_out', 'final_norm_w']
TWIN_WEIGHTS = ['norm_w', 'w_in', 'ssd_conv_w', 'ssd_conv_b', 'ssd_dt_bias', 'ssd_a_log', 'ssd_d', 'ssd_norm_w', 'attn_sinks', 'conf_dw_w', 'conf_dw_b', 'conf_ln_w', 'conf_ln_b', 'w_out', 'final_norm_w']
TWIN_DIFF_INPUT = 'x'
TWIN_INPUTS = ['x', 'norm_w', 'w_in', 'ssd_conv_w', 'ssd_conv_b', 'ssd_dt_bias', 'ssd_a_log', 'ssd_d', 'ssd_norm_w', 'attn_sinks', 'conf_dw_w', 'conf_dw_b', 'conf_ln_w', 'conf_ln_b', 'w_out', 'final_norm_w', 'loss_target', 'm_norm_w', 'm_w_in', 'm_ssd_conv_w', 'm_ssd_conv_b', 'm_ssd_dt_bias', 'm_ssd_a_log', 'm_ssd_d', 'm_ssd_norm_w', 'm_attn_sinks', 'm_conf_dw_w', 'm_conf_dw_b', 'm_conf_ln_w', 'm_conf_ln_b', 'm_w_out', 'm_final_norm_w', 'v_norm_w', 'v_w_in', 'v_ssd_conv_w', 'v_ssd_conv_b', 'v_ssd_dt_bias', 'v_ssd_a_log', 'v_ssd_d', 'v_ssd_norm_w', 'v_attn_sinks', 'v_conf_dw_w', 'v_conf_dw_b', 'v_conf_ln_w', 'v_conf_ln_b', 'v_w_out', 'v_final_norm_w']
TWIN_OUTPUTS = ['loss', 'grad_x', 'grad_norm_w', 'grad_w_in', 'grad_ssd_conv_w', 'grad_ssd_conv_b', 'grad_ssd_dt_bias', 'grad_ssd_a_log', 'grad_ssd_d', 'grad_ssd_norm_w', 'grad_attn_sinks', 'grad_conf_dw_w', 'grad_conf_dw_b', 'grad_conf_ln_w', 'grad_conf_ln_b', 'grad_w_out', 'grad_final_norm_w', 'delta_norm_w', 'delta_w_in', 'delta_ssd_conv_w', 'delta_ssd_conv_b', 'delta_ssd_dt_bias', 'delta_ssd_a_log', 'delta_ssd_d', 'delta_ssd_norm_w', 'delta_attn_sinks', 'delta_conf_dw_w', 'delta_conf_dw_b', 'delta_conf_ln_w', 'delta_conf_ln_b', 'delta_w_out', 'delta_final_norm_w', 'new_m_norm_w', 'new_m_w_in', 'new_m_ssd_conv_w', 'new_m_ssd_conv_b', 'new_m_ssd_dt_bias', 'new_m_ssd_a_log', 'new_m_ssd_d', 'new_m_ssd_norm_w', 'new_m_attn_sinks', 'new_m_conf_dw_w', 'new_m_conf_dw_b', 'new_m_conf_ln_w', 'new_m_conf_ln_b', 'new_m_w_out', 'new_m_final_norm_w', 'new_v_norm_w', 'new_v_w_in', 'new_v_ssd_conv_w', 'new_v_ssd_conv_b', 'new_v_ssd_dt_bias', 'new_v_ssd_a_log', 'new_v_ssd_d', 'new_v_ssd_norm_w', 'new_v_attn_sinks', 'new_v_conf_dw_w', 'new_v_conf_dw_b', 'new_v_conf_ln_w', 'new_v_conf_ln_b', 'new_v_w_out', 'new_v_final_norm_w']
TWIN_LEAF_KINDS = {'loss': 'loss', 'grad_x': 'grad_x', 'grad_norm_w': 'grad_w', 'grad_w_in': 'grad_w', 'grad_ssd_conv_w': 'grad_w', 'grad_ssd_conv_b': 'grad_w', 'grad_ssd_dt_bias': 'grad_w', 'grad_ssd_a_log': 'grad_w', 'grad_ssd_d': 'grad_w', 'grad_ssd_norm_w': 'grad_w', 'grad_attn_sinks': 'grad_w', 'grad_conf_dw_w': 'grad_w', 'grad_conf_dw_b': 'grad_w', 'grad_conf_ln_w': 'grad_w', 'grad_conf_ln_b': 'grad_w', 'grad_w_out': 'grad_w', 'grad_final_norm_w': 'grad_w', 'delta_norm_w': 'delta_w', 'delta_w_in': 'delta_w', 'delta_ssd_conv_w': 'delta_w', 'delta_ssd_conv_b': 'delta_w', 'delta_ssd_dt_bias': 'delta_w', 'delta_ssd_a_log': 'delta_w', 'delta_ssd_d': 'delta_w', 'delta_ssd_norm_w': 'delta_w', 'delta_attn_sinks': 'delta_w', 'delta_conf_dw_w': 'delta_w', 'delta_conf_dw_b': 'delta_w', 'delta_conf_ln_w': 'delta_w', 'delta_conf_ln_b': 'delta_w', 'delta_w_out': 'delta_w', 'delta_final_norm_w': 'delta_w', 'new_m_norm_w': 'new_m', 'new_m_w_in': 'new_m', 'new_m_ssd_conv_w': 'new_m', 'new_m_ssd_conv_b': 'new_m', 'new_m_ssd_dt_bias': 'new_m', 'new_m_ssd_a_log': 'new_m', 'new_m_ssd_d': 'new_m', 'new_m_ssd_norm_w': 'new_m', 'new_m_attn_sinks': 'new_m', 'new_m_conf_dw_w': 'new_m', 'new_m_conf_dw_b': 'new_m', 'new_m_conf_ln_w': 'new_m', 'new_m_conf_ln_b': 'new_m', 'new_m_w_out': 'new_m', 'new_m_final_norm_w': 'new_m', 'new_v_norm_w': 'new_v', 'new_v_w_in': 'new_v', 'new_v_ssd_conv_w': 'new_v', 'new_v_ssd_conv_b': 'new_v', 'new_v_ssd_dt_bias': 'new_v', 'new_v_ssd_a_log': 'new_v', 'new_v_ssd_d': 'new_v', 'new_v_ssd_norm_w': 'new_v', 'new_v_attn_sinks': 'new_v', 'new_v_conf_dw_w': 'new_v', 'new_v_conf_dw_b': 'new_v', 'new_v_conf_ln_w': 'new_v', 'new_v_conf_ln_b': 'new_v', 'new_v_w_out': 'new_v', 'new_v_final_norm_w': 'new_v'}


def _forward(args):
    return _fwd_reference(*[args[k] for k in FWD_PARAMS])


def _output_shape():
    out = _jax.eval_shape(lambda: _forward(_fwd_setup_inputs(0)))
    return out.shape, out.dtype

N_MICROBATCH = 1
ADAM_LR = 0.001
ADAM_B1 = 0.9
ADAM_B2 = 0.999
ADAM_EPS = 1e-08
ADAM_WD = 0.01
ADAM_STEP = 10
PER_EXAMPLE_BATCH_AXIS = {'x': 0, 'loss_target': 0}
SHARED_INPUTS = []
_WEIGHT_DTYPES = {'norm_w': _jnp.float32, 'w_in': _jnp.float32, 'ssd_conv_w': _jnp.float32, 'ssd_conv_b': _jnp.float32, 'ssd_dt_bias': _jnp.float32, 'ssd_a_log': _jnp.float32, 'ssd_d': _jnp.float32, 'ssd_norm_w': _jnp.float32, 'attn_sinks': _jnp.float32, 'conf_dw_w': _jnp.float32, 'conf_dw_b': _jnp.float32, 'conf_ln_w': _jnp.float32, 'conf_ln_b': _jnp.float32, 'w_out': _jnp.float32, 'final_norm_w': _jnp.float32}
MOMENT_SCALE = {'norm_w': 1.646917e-01, 'w_in': 7.039067e-02, 'ssd_conv_w': 9.145070e-02, 'ssd_conv_b': 1.231953e-01, 'ssd_dt_bias': 2.207949e-01, 'ssd_a_log': 2.519799e-01, 'ssd_d': 6.250658e-01, 'ssd_norm_w': 1.030664e-01, 'attn_sinks': 1.066887e-02, 'conf_dw_w': 3.792795e-02, 'conf_dw_b': 8.050204e-02, 'conf_ln_w': 4.430599e-02, 'conf_ln_b': 3.734898e-02, 'w_out': 1.062584e-01, 'final_norm_w': 3.197652e+01}


def _to_microbatches(a, axis):
    t = _jnp.moveaxis(a, axis, 0)
    t = t.reshape((N_MICROBATCH, t.shape[0] // N_MICROBATCH) + t.shape[1:])
    return _jnp.moveaxis(t, 1, axis + 1)


def setup_inputs(seed: int = 0) -> dict:
    inp = _fwd_setup_inputs(seed)
    key = _jax.random.fold_in(_jax.random.key(seed), 7919)
    shape, _ = _output_shape()
    out = dict(inp)
    out["loss_target"] = _jax.random.normal(_jax.random.fold_in(key, 0), shape, _jnp.float32)
    for i, name in enumerate(TWIN_WEIGHTS):
        w = inp[name].astype(_jnp.float32)
        if MOMENT_SCALE is None:
            s = _jnp.sqrt(_jnp.mean(_jnp.square(w)) + 1e-30)
        else:
            s = MOMENT_SCALE[name]
        km, kv = _jax.random.split(_jax.random.fold_in(key, i + 1))
        out[name] = w
        out["m_" + name] = s * _jax.random.normal(km, w.shape, _jnp.float32)
        out["v_" + name] = (s * s) * _jax.random.uniform(kv, w.shape, _jnp.float32, 0.5, 1.5)
    if N_MICROBATCH > 1:
        for name, axis in PER_EXAMPLE_BATCH_AXIS.items():
            out[name] = _to_microbatches(out[name], axis)
    return {'x': out['x'], 'norm_w': out['norm_w'], 'w_in': out['w_in'], 'ssd_conv_w': out['ssd_conv_w'], 'ssd_conv_b': out['ssd_conv_b'], 'ssd_dt_bias': out['ssd_dt_bias'], 'ssd_a_log': out['ssd_a_log'], 'ssd_d': out['ssd_d'], 'ssd_norm_w': out['ssd_norm_w'], 'attn_sinks': out['attn_sinks'], 'conf_dw_w': out['conf_dw_w'], 'conf_dw_b': out['conf_dw_b'], 'conf_ln_w': out['conf_ln_w'], 'conf_ln_b': out['conf_ln_b'], 'w_out': out['w_out'], 'final_norm_w': out['final_norm_w'], 'loss_target': out['loss_target'], 'm_norm_w': out['m_norm_w'], 'm_w_in': out['m_w_in'], 'm_ssd_conv_w': out['m_ssd_conv_w'], 'm_ssd_conv_b': out['m_ssd_conv_b'], 'm_ssd_dt_bias': out['m_ssd_dt_bias'], 'm_ssd_a_log': out['m_ssd_a_log'], 'm_ssd_d': out['m_ssd_d'], 'm_ssd_norm_w': out['m_ssd_norm_w'], 'm_attn_sinks': out['m_attn_sinks'], 'm_conf_dw_w': out['m_conf_dw_w'], 'm_conf_dw_b': out['m_conf_dw_b'], 'm_conf_ln_w': out['m_conf_ln_w'], 'm_conf_ln_b': out['m_conf_ln_b'], 'm_w_out': out['m_w_out'], 'm_final_norm_w': out['m_final_norm_w'], 'v_norm_w': out['v_norm_w'], 'v_w_in': out['v_w_in'], 'v_ssd_conv_w': out['v_ssd_conv_w'], 'v_ssd_conv_b': out['v_ssd_conv_b'], 'v_ssd_dt_bias': out['v_ssd_dt_bias'], 'v_ssd_a_log': out['v_ssd_a_log'], 'v_ssd_d': out['v_ssd_d'], 'v_ssd_norm_w': out['v_ssd_norm_w'], 'v_attn_sinks': out['v_attn_sinks'], 'v_conf_dw_w': out['v_conf_dw_w'], 'v_conf_dw_b': out['v_conf_dw_b'], 'v_conf_ln_w': out['v_conf_ln_w'], 'v_conf_ln_b': out['v_conf_ln_b'], 'v_w_out': out['v_w_out'], 'v_final_norm_w': out['v_final_norm_w']}


def _loss(weights, diff, rest, loss_target):
    with _jax.named_scope("forward"):
        args = {**rest, TWIN_DIFF_INPUT: diff, **{k: w.astype(_WEIGHT_DTYPES[k]) for k, w in weights.items()}}
        y = _forward(args)
    with _jax.named_scope("loss_head"):
        err = _jnp.square(y.astype(_jnp.float32) - loss_target)
        return 0.5 * _jnp.sum(_jnp.mean(err, axis=-1)) if err.ndim else 0.5 * err


def _adamw(w, g, m, v):
    m = ADAM_B1 * m + (1.0 - ADAM_B1) * g
    v = ADAM_B2 * v + (1.0 - ADAM_B2) * _jnp.square(g)
    m_hat = m / (1.0 - ADAM_B1 ** ADAM_STEP)
    v_hat = v / (1.0 - ADAM_B2 ** ADAM_STEP)
    delta = -ADAM_LR * (m_hat / (_jnp.sqrt(v_hat) + ADAM_EPS) + ADAM_WD * w)
    return delta, m, v


def reference(x, norm_w, w_in, ssd_conv_w, ssd_conv_b, ssd_dt_bias, ssd_a_log, ssd_d, ssd_norm_w, attn_sinks, conf_dw_w, conf_dw_b, conf_ln_w, conf_ln_b, w_out, final_norm_w, loss_target, m_norm_w, m_w_in, m_ssd_conv_w, m_ssd_conv_b, m_ssd_dt_bias, m_ssd_a_log, m_ssd_d, m_ssd_norm_w, m_attn_sinks, m_conf_dw_w, m_conf_dw_b, m_conf_ln_w, m_conf_ln_b, m_w_out, m_final_norm_w, v_norm_w, v_w_in, v_ssd_conv_w, v_ssd_conv_b, v_ssd_dt_bias, v_ssd_a_log, v_ssd_d, v_ssd_norm_w, v_attn_sinks, v_conf_dw_w, v_conf_dw_b, v_conf_ln_w, v_conf_ln_b, v_w_out, v_final_norm_w):
    given = dict(x=x, norm_w=norm_w, w_in=w_in, ssd_conv_w=ssd_conv_w, ssd_conv_b=ssd_conv_b, ssd_dt_bias=ssd_dt_bias, ssd_a_log=ssd_a_log, ssd_d=ssd_d, ssd_norm_w=ssd_norm_w, attn_sinks=attn_sinks, conf_dw_w=conf_dw_w, conf_dw_b=conf_dw_b, conf_ln_w=conf_ln_w, conf_ln_b=conf_ln_b, w_out=w_out, final_norm_w=final_norm_w, loss_target=loss_target, m_norm_w=m_norm_w, m_w_in=m_w_in, m_ssd_conv_w=m_ssd_conv_w, m_ssd_conv_b=m_ssd_conv_b, m_ssd_dt_bias=m_ssd_dt_bias, m_ssd_a_log=m_ssd_a_log, m_ssd_d=m_ssd_d, m_ssd_norm_w=m_ssd_norm_w, m_attn_sinks=m_attn_sinks, m_conf_dw_w=m_conf_dw_w, m_conf_dw_b=m_conf_dw_b, m_conf_ln_w=m_conf_ln_w, m_conf_ln_b=m_conf_ln_b, m_w_out=m_w_out, m_final_norm_w=m_final_norm_w, v_norm_w=v_norm_w, v_w_in=v_w_in, v_ssd_conv_w=v_ssd_conv_w, v_ssd_conv_b=v_ssd_conv_b, v_ssd_dt_bias=v_ssd_dt_bias, v_ssd_a_log=v_ssd_a_log, v_ssd_d=v_ssd_d, v_ssd_norm_w=v_ssd_norm_w, v_attn_sinks=v_attn_sinks, v_conf_dw_w=v_conf_dw_w, v_conf_dw_b=v_conf_dw_b, v_conf_ln_w=v_conf_ln_w, v_conf_ln_b=v_conf_ln_b, v_w_out=v_w_out, v_final_norm_w=v_final_norm_w)
    weights = {n: given[n] for n in TWIN_WEIGHTS}
    shared = {n: given[n] for n in SHARED_INPUTS}
    per_example = {n: given[n] for n in ['x']}
    grad_fn = _jax.value_and_grad(_loss, argnums=(0, 1))

    def one_microbatch(ex, loss_target):
        ex = dict(ex)
        diff = ex.pop(TWIN_DIFF_INPUT)
        return grad_fn(weights, diff, {**shared, **ex}, loss_target)

    if N_MICROBATCH == 1:
        loss, (grad_w, grad_x) = one_microbatch(per_example, given["loss_target"])
    else:
        def body(carry, xs):
            loss_sum, grad_sum = carry
            l_k, (gw_k, gx_k) = one_microbatch(xs[0], xs[1])
            with _jax.named_scope("update"):
                return (loss_sum + l_k, _jax.tree.map(_jnp.add, grad_sum, gw_k)), gx_k

        init = (_jnp.zeros((), _jnp.float32), _jax.tree.map(_jnp.zeros_like, weights))
        (loss, grad_w), grad_x = _jax.lax.scan(body, init, (per_example, given["loss_target"]))
    with _jax.named_scope("update"):
        delta_w, new_m, new_v = {}, {}, {}
        for n in TWIN_WEIGHTS:
            delta_w[n], new_m[n], new_v[n] = _adamw(weights[n], grad_w[n], given["m_" + n], given["v_" + n])
    return (loss, grad_x, *[grad_w[n] for n in TWIN_WEIGHTS], *[delta_w[n] for n in TWIN_WEIGHTS],
            *[new_m[n] for n in TWIN_WEIGHTS], *[new_v[n] for n in TWIN_WEIGHTS])
```

```python
import functools
import math

import jax
import jax.numpy as jnp
import numpy as np
from jax import lax
from jax.experimental import pallas as pl
from jax.experimental.pallas import tpu as pltpu

F32 = jnp.float32
BF16 = jnp.bfloat16
MXU_DTYPE = BF16

D_MODEL = 1024
DEPTH = 2
SSD_HEADS = 16
SSD_HEAD_DIM = 64
SSD_STATE = 128
SSD_CONV = 4
CHUNK = 128
SSD_CONV_DIM = 1536
ATTN_HEAD_DIM = 64
ATTN_Q_HEADS = 8
WINDOW = 128
CONF_WIDTH = 512
CONF_KERNEL = 31
MIX_WIDTH = 2048
D_IN_PROJ = 5392
EPS = 1e-5

ADAM_LR = 0.001
ADAM_B1 = 0.9
ADAM_B2 = 0.999
ADAM_EPS = 1e-08
ADAM_WD = 0.01
ADAM_STEP = 10

LANES = 128
SUBLANES = 8
VMEM_LIMIT = 48 * 1024 * 1024

NP = 5632
OFF_Z, OFF_XBC, OFF_Q, OFF_K, OFF_V, OFF_DT, OFF_CONF = 0, 2048, 3584, 4096, 4224, 4352, 4608
REF_OFF_XBC, REF_OFF_DT, REF_OFF_Q, REF_OFF_K, REF_OFF_V, REF_OFF_CONF = 2048, 3584, 3600, 4112, 4240, 4368

NN = (((1,), (0,)), ((), ()))
NT = (((1,), (1,)), ((), ()))
TN = (((0,), (0,)), ((), ()))


def _params(sem):
    return pltpu.CompilerParams(dimension_semantics=sem, vmem_limit_bytes=VMEM_LIMIT)


def _dot(a, b, dims=NN):
    return lax.dot_general(a.astype(MXU_DTYPE), b.astype(MXU_DTYPE), dims, preferred_element_type=F32)


def _split_bf16(a, passes):
    pieces = []
    r = a
    for _ in range(passes):
        p = r.astype(BF16)
        pieces.append(p)
        r = r - p.astype(F32)
    return pieces


def _xdot(a, sel, dims=NN, passes=3):
    out = None
    for p in _split_bf16(a, passes):
        t = lax.dot_general(p, sel, dims, preferred_element_type=F32)
        out = t if out is None else out + t
    return out


def _xdot_r(sel, b, dims=NN, passes=3):
    out = None
    for p in _split_bf16(b, passes):
        t = lax.dot_general(sel, p, dims, preferred_element_type=F32)
        out = t if out is None else out + t
    return out


def _sigmoid(x):
    return 1.0 / (1.0 + jnp.exp(-x))


def _silu(x):
    return x * _sigmoid(x)


def _dsilu(x):
    s = _sigmoid(x)
    return s * (1.0 + x * (1.0 - s))


def _softplus(x):
    return jnp.maximum(x, 0.0) + jnp.log(1.0 + jnp.exp(-jnp.abs(x)))


def _rowsum8(x):
    r, c = x.shape
    return jnp.sum(x.reshape(r // SUBLANES, SUBLANES, c), axis=0)


def _iota(shape, dim):
    return lax.broadcasted_iota(jnp.int32, shape, dim)


def _matmul(a, b, form, out_dtype, tm, tn, tk, name, residual=None):
    if form == "nn":
        (m, k), n = a.shape, b.shape[1]
    elif form == "nt":
        (m, k), n = a.shape, b.shape[0]
    else:
        (k, m), n = a.shape, b.shape[1]
    tm, tn, tk = min(tm, m), min(tn, n), min(tk, k)
    assert m % tm == 0 and n % tn == 0 and k % tk == 0, (name, m, n, k, tm, tn, tk)
    if form == "nn":
        a_spec = pl.BlockSpec((tm, tk), lambda i, j, s: (i, s))
        b_spec = pl.BlockSpec((tk, tn), lambda i, j, s: (s, j))
        dims = NN
    elif form == "nt":
        (m, k), n = a.shape, b.shape[0]
        a_spec = pl.BlockSpec((tm, tk), lambda i, j, s: (i, s))
        b_spec = pl.BlockSpec((tn, tk), lambda i, j, s: (j, s))
        dims = NT
    else:
        (k, m), n = a.shape, b.shape[1]
        a_spec = pl.BlockSpec((tk, tm), lambda i, j, s: (s, i))
        b_spec = pl.BlockSpec((tk, tn), lambda i, j, s: (s, j))
        dims = TN
    nk = k // tk
    has_res = residual is not None

    def body(a_ref, b_ref, *rest):
        if has_res:
            r_ref, o_ref, acc = rest
        else:
            o_ref, acc = rest
        s = pl.program_id(2)

        @pl.when(s == 0)
        def _():
            acc[...] = jnp.zeros_like(acc)

        acc[...] += _dot(a_ref[...], b_ref[...], dims)

        @pl.when(s == nk - 1)
        def _():
            o = acc[...]
            if has_res:
                o = o + r_ref[...]
            o_ref[...] = o.astype(out_dtype)

    in_specs = [a_spec, b_spec]
    args = [a, b]
    if has_res:
        in_specs.append(pl.BlockSpec((tm, tn), lambda i, j, s: (i, j)))
        args.append(residual)
    return pl.pallas_call(
        body, name=name,
        out_shape=jax.ShapeDtypeStruct((m, n), out_dtype),
        grid=(m // tm, n // tn, nk),
        in_specs=in_specs,
        out_specs=pl.BlockSpec((tm, tn), lambda i, j, s: (i, j)),
        scratch_shapes=[pltpu.VMEM((tm, tn), F32)],
        compiler_params=_params(("parallel", "parallel", "arbitrary")),
    )(*args)


ROW_TILE = 256


def _rmsnorm_fwd(x, w, name):
    t, d = x.shape
    tm = ROW_TILE

    def body(x_ref, w_ref, o_ref):
        xv = x_ref[...]
        rstd = lax.rsqrt(jnp.mean(xv * xv, axis=-1, keepdims=True) + EPS)
        o_ref[...] = (xv * rstd * w_ref[...]).astype(o_ref.dtype)

    return pl.pallas_call(
        body, name=name,
        out_shape=jax.ShapeDtypeStruct((t, d), MXU_DTYPE),
        grid=(t // tm,),
        in_specs=[pl.BlockSpec((tm, d), lambda i: (i, 0)), pl.BlockSpec((1, d), lambda i: (0, 0))],
        out_specs=pl.BlockSpec((tm, d), lambda i: (i, 0)),
        compiler_params=_params(("parallel",)),
    )(x, w)


def _rmsnorm_bwd(dh, x, w, dres, name):
    t, d = x.shape
    tm = ROW_TILE
    nt = t // tm

    def body(dh_ref, x_ref, w_ref, dr_ref, dx_ref, dw_ref, acc):
        i = pl.program_id(0)

        @pl.when(i == 0)
        def _():
            acc[...] = jnp.zeros_like(acc)

        xv = x_ref[...]
        rstd = lax.rsqrt(jnp.mean(xv * xv, axis=-1, keepdims=True) + EPS)
        xh = xv * rstd
        dhv = dh_ref[...]
        g = dhv * w_ref[...]
        dx_ref[...] = dr_ref[...] + rstd * (g - xh * jnp.mean(g * xh, axis=-1, keepdims=True))
        acc[...] += _rowsum8(dhv * xh)

        @pl.when(i == nt - 1)
        def _():
            dw_ref[...] = jnp.sum(acc[...], axis=0, keepdims=True)

    row = pl.BlockSpec((tm, d), lambda i: (i, 0))
    vec = pl.BlockSpec((1, d), lambda i: (0, 0))
    return pl.pallas_call(
        body, name=name,
        out_shape=(jax.ShapeDtypeStruct((t, d), F32), jax.ShapeDtypeStruct((1, d), F32)),
        grid=(nt,),
        in_specs=[row, row, vec, row],
        out_specs=(row, vec),
        scratch_shapes=[pltpu.VMEM((SUBLANES, d), F32)],
        compiler_params=_params(("arbitrary",)),
    )(dh, x, w, dres)


def _loss_head(xf, target, w, name):
    t, d = xf.shape
    tm = ROW_TILE
    nt = t // tm

    def body(x_ref, t_ref, w_ref, loss_ref, dx_ref, dw_ref, lacc, wacc):
        i = pl.program_id(0)

        @pl.when(i == 0)
        def _():
            lacc[...] = jnp.zeros_like(lacc)
            wacc[...] = jnp.zeros_like(wacc)

        xv = x_ref[...]
        rstd = lax.rsqrt(jnp.mean(xv * xv, axis=-1, keepdims=True) + EPS)
        xh = xv * rstd
        err = xh * w_ref[...] - t_ref[...]
        lacc[...] += jnp.sum(err * err)
        dy = err * (1.0 / d)
        g = dy * w_ref[...]
        dx_ref[...] = rstd * (g - xh * jnp.mean(g * xh, axis=-1, keepdims=True))
        wacc[...] += _rowsum8(dy * xh)

        @pl.when(i == nt - 1)
        def _():
            loss_ref[...] = lacc[...] * (0.5 / d)
            dw_ref[...] = jnp.sum(wacc[...], axis=0, keepdims=True)

    row = pl.BlockSpec((tm, d), lambda i: (i, 0))
    vec = pl.BlockSpec((1, d), lambda i: (0, 0))
    return pl.pallas_call(
        body, name=name,
        out_shape=(jax.ShapeDtypeStruct((SUBLANES, LANES), F32), jax.ShapeDtypeStruct((t, d), F32),
                   jax.ShapeDtypeStruct((1, d), F32)),
        grid=(nt,),
        in_specs=[row, row, vec],
        out_specs=(pl.BlockSpec((SUBLANES, LANES), lambda i: (0, 0)), row, vec),
        scratch_shapes=[pltpu.VMEM((SUBLANES, LANES), F32), pltpu.VMEM((SUBLANES, d), F32)],
        compiler_params=_params(("arbitrary",)),
    )(xf, target, w)


def _glu_fwd(proj, name):
    t = proj.shape[0]
    tm, cw = ROW_TILE, CONF_WIDTH

    def body(a_ref, g_ref, o_ref):
        o_ref[...] = a_ref[...] * _sigmoid(g_ref[...])

    return pl.pallas_call(
        body, name=name,
        out_shape=jax.ShapeDtypeStruct((t, cw), F32),
        grid=(t // tm,),
        in_specs=[pl.BlockSpec((tm, cw), lambda i: (i, OFF_CONF // cw)),
                  pl.BlockSpec((tm, cw), lambda i: (i, OFF_CONF // cw + 1))],
        out_specs=pl.BlockSpec((tm, cw), lambda i: (i, 0)),
        compiler_params=_params(("parallel",)),
    )(proj, proj)


def _glu_bwd(dc0, proj, name):
    t = proj.shape[0]
    tm, cw = ROW_TILE, CONF_WIDTH

    def body(d_ref, a_ref, g_ref, o_ref):
        s = _sigmoid(g_ref[...])
        dv = d_ref[...]
        o_ref[:, :cw] = dv * s
        o_ref[:, cw:] = dv * a_ref[...] * s * (1.0 - s)

    return pl.pallas_call(
        body, name=name,
        out_shape=jax.ShapeDtypeStruct((t, 2 * cw), F32),
        grid=(t // tm,),
        in_specs=[pl.BlockSpec((tm, cw), lambda i: (i, 0)),
                  pl.BlockSpec((tm, cw), lambda i: (i, OFF_CONF // cw)),
                  pl.BlockSpec((tm, cw), lambda i: (i, OFF_CONF // cw + 1))],
        out_specs=pl.BlockSpec((tm, 2 * cw), lambda i: (i, 0)),
        compiler_params=_params(("parallel",)),
    )(dc0, proj, proj)


def _conf_post_fwd(c1, proj, ln_w, ln_b, name):
    t = c1.shape[0]
    tm, cw = ROW_TILE, CONF_WIDTH

    def body(c_ref, z_ref, w_ref, b_ref, o_ref):
        cv = c_ref[...]
        xc = cv - jnp.mean(cv, axis=-1, keepdims=True)
        rstd = lax.rsqrt(jnp.mean(xc * xc, axis=-1, keepdims=True) + EPS)
        c2 = xc * rstd * w_ref[...] + b_ref[...]
        o_ref[...] = (_silu(c2) * _silu(z_ref[...])).astype(o_ref.dtype)

    vec = pl.BlockSpec((1, cw), lambda i: (0, 0))
    return pl.pallas_call(
        body, name=name,
        out_shape=jax.ShapeDtypeStruct((t, cw), MXU_DTYPE),
        grid=(t // tm,),
        in_specs=[pl.BlockSpec((tm, cw), lambda i: (i, 0)),
                  pl.BlockSpec((tm, cw), lambda i: (i, (OFF_Z + 1536) // cw)), vec, vec],
        out_specs=pl.BlockSpec((tm, cw), lambda i: (i, 0)),
        compiler_params=_params(("parallel",)),
    )(c1, proj, ln_w, ln_b)


def _conf_post_bwd(dycat, c1, proj, ln_w, ln_b, name):
    t = c1.shape[0]
    tm, cw = ROW_TILE, CONF_WIDTH
    nt = t // tm

    def body(dy_ref, c_ref, z_ref, w_ref, b_ref, dc_ref, dz_ref, dw_ref, db_ref, wacc, bacc):
        i = pl.program_id(0)

        @pl.when(i == 0)
        def _():
            wacc[...] = jnp.zeros_like(wacc)
            bacc[...] = jnp.zeros_like(bacc)

        cv = c_ref[...]
        xc = cv - jnp.mean(cv, axis=-1, keepdims=True)
        rstd = lax.rsqrt(jnp.mean(xc * xc, axis=-1, keepdims=True) + EPS)
        xh = xc * rstd
        c2 = xh * w_ref[...] + b_ref[...]
        zv = z_ref[...]
        dy = dy_ref[...]
        dz_ref[...] = dy * _silu(c2) * _dsilu(zv)
        dc2 = dy * _silu(zv) * _dsilu(c2)
        bacc[...] += _rowsum8(dc2)
        wacc[...] += _rowsum8(dc2 * xh)
        dxh = dc2 * w_ref[...]
        dc_ref[...] = rstd * (dxh - jnp.mean(dxh, axis=-1, keepdims=True)
                              - xh * jnp.mean(dxh * xh, axis=-1, keepdims=True))

        @pl.when(i == nt - 1)
        def _():
            dw_ref[...] = jnp.sum(wacc[...], axis=0, keepdims=True)
            db_ref[...] = jnp.sum(bacc[...], axis=0, keepdims=True)

    row = pl.BlockSpec((tm, cw), lambda i: (i, 0))
    vec = pl.BlockSpec((1, cw), lambda i: (0, 0))
    return pl.pallas_call(
        body, name=name,
        out_shape=(jax.ShapeDtypeStruct((t, cw), F32), jax.ShapeDtypeStruct((t, cw), F32),
                   jax.ShapeDtypeStruct((1, cw), F32), jax.ShapeDtypeStruct((1, cw), F32)),
        grid=(nt,),
        in_specs=[pl.BlockSpec((tm, cw), lambda i: (i, 1536 // cw)), row,
                  pl.BlockSpec((tm, cw), lambda i: (i, (OFF_Z + 1536) // cw)), vec, vec],
        out_specs=(row, row, vec, vec),
        scratch_shapes=[pltpu.VMEM((SUBLANES, cw), F32), pltpu.VMEM((SUBLANES, cw), F32)],
        compiler_params=_params(("arbitrary",)),
    )(dycat, c1, proj, ln_w, ln_b)


GN_WIDTH = 512


def _gated_norm_fwd(y, proj, w, name):
    t = y.shape[0]
    tm, cw = ROW_TILE, GN_WIDTH

    def body(y_ref, z_ref, w_ref, o_ref):
        g = y_ref[...] * _silu(z_ref[...])
        rstd = lax.rsqrt(jnp.mean(g * g, axis=-1, keepdims=True) + EPS)
        o_ref[...] = (g * rstd * w_ref[...]).astype(o_ref.dtype)

    blk = pl.BlockSpec((tm, cw), lambda i, j: (i, j))
    return pl.pallas_call(
        body, name=name,
        out_shape=jax.ShapeDtypeStruct(y.shape, MXU_DTYPE),
        grid=(t // tm, y.shape[1] // cw),
        in_specs=[blk, blk, pl.BlockSpec((1, cw), lambda i, j: (0, j))],
        out_specs=blk,
        compiler_params=_params(("parallel", "parallel")),
    )(y, proj, w)


def _gated_norm_bwd(dycat, y, proj, w, name):
    t = y.shape[0]
    tm, cw = ROW_TILE, GN_WIDTH
    nt = t // tm

    def body(do_ref, y_ref, z_ref, w_ref, dy_ref, dz_ref, dw_ref, acc):
        i = pl.program_id(1)

        @pl.when(i == 0)
        def _():
            acc[...] = jnp.zeros_like(acc)

        yv, zv, dov = y_ref[...], z_ref[...], do_ref[...]
        sz = _silu(zv)
        g = yv * sz
        rstd = lax.rsqrt(jnp.mean(g * g, axis=-1, keepdims=True) + EPS)
        gh = g * rstd
        acc[...] += _rowsum8(dov * gh)
        dgn = dov * w_ref[...]
        dg = rstd * (dgn - gh * jnp.mean(dgn * gh, axis=-1, keepdims=True))
        dy_ref[...] = dg * sz
        dz_ref[...] = dg * yv * _dsilu(zv)

        @pl.when(i == nt - 1)
        def _():
            dw_ref[...] = jnp.sum(acc[...], axis=0, keepdims=True)

    blk = pl.BlockSpec((tm, cw), lambda j, i: (i, j))
    vec = pl.BlockSpec((1, cw), lambda j, i: (0, j))
    return pl.pallas_call(
        body, name=name,
        out_shape=(jax.ShapeDtypeStruct(y.shape, F32), jax.ShapeDtypeStruct(y.shape, F32),
                   jax.ShapeDtypeStruct((1, y.shape[1]), F32)),
        grid=(y.shape[1] // cw, nt),
        in_specs=[blk, blk, blk, vec],
        out_specs=(blk, blk, vec),
        scratch_shapes=[pltpu.VMEM((SUBLANES, cw), F32)],
        compiler_params=_params(("parallel", "arbitrary")),
    )(dycat, y, proj, w)


CONV_TILE = 256
CONV_COLS = 128


def _conv_halo(k):
    return SUBLANES if k - 1 <= SUBLANES else 32


def _conv_fwd(src, col0, width, w, bias, k, seq, name):
    t = src.shape[0]
    tm, cw, halo = CONV_TILE, CONV_COLS, _conv_halo(k)
    p = k - 1
    cb0 = col0 // cw
    kp = w.shape[0]

    def body(x_ref, h_ref, w_ref, b_ref, o_ref, ext):
        i = pl.program_id(0)
        seq_start = (i * tm) % seq == 0
        ext[halo:, :] = x_ref[...]
        ext[:halo, :] = jnp.where(seq_start, 0.0, h_ref[...])
        acc = jnp.zeros((tm, cw), F32) + b_ref[...]
        for j in range(k):
            acc = acc + w_ref[j:j + 1, :] * ext[halo - p + j:halo - p + j + tm, :]
        o_ref[...] = acc

    return pl.pallas_call(
        body, name=name,
        out_shape=jax.ShapeDtypeStruct((t, width), F32),
        grid=(t // tm, width // cw),
        in_specs=[pl.BlockSpec((tm, cw), lambda i, j: (i, cb0 + j)),
                  pl.BlockSpec((halo, cw), lambda i, j: (jnp.maximum(i * (tm // halo) - 1, 0), cb0 + j)),
                  pl.BlockSpec((kp, cw), lambda i, j: (0, j)),
                  pl.BlockSpec((1, cw), lambda i, j: (0, j))],
        out_specs=pl.BlockSpec((tm, cw), lambda i, j: (i, j)),
        scratch_shapes=[pltpu.VMEM((halo + tm, cw), F32)],
        compiler_params=_params(("parallel", "parallel")),
    )(src, src, w, bias)


def _conv_bwd(dy, src, col0, width, w, k, seq, name):
    t = src.shape[0]
    tm, cw, halo = CONV_TILE, CONV_COLS, _conv_halo(k)
    p = k - 1
    cb0 = col0 // cw
    kp = w.shape[0]
    nt = t // tm
    last_halo = t // halo - 1

    def body(dy_ref, dn_ref, x_ref, xp_ref, w_ref, dx_ref, dw_ref, db_ref, dyext, xext, wacc, bacc):
        i = pl.program_id(1)

        @pl.when(i == 0)
        def _():
            wacc[...] = jnp.zeros_like(wacc)
            bacc[...] = jnp.zeros_like(bacc)

        seq_start = (i * tm) % seq == 0
        seq_end = ((i + 1) * tm) % seq == 0
        dyv = dy_ref[...]
        dyext[:tm, :] = dyv
        dyext[tm:, :] = jnp.where(seq_end, 0.0, dn_ref[...])
        xext[halo:, :] = x_ref[...]
        xext[:halo, :] = jnp.where(seq_start, 0.0, xp_ref[...])
        acc = jnp.zeros((tm, cw), F32)
        for j in range(k):
            acc = acc + w_ref[j:j + 1, :] * dyext[p - j:p - j + tm, :]
            wacc[j] += _rowsum8(dyv * xext[halo - p + j:halo - p + j + tm, :])
        dx_ref[...] = acc
        bacc[...] += _rowsum8(dyv)

        @pl.when(i == nt - 1)
        def _():
            dw_ref[...] = jnp.zeros_like(dw_ref)
            for j in range(k):
                dw_ref[j:j + 1, :] = jnp.sum(wacc[j], axis=0, keepdims=True)
            db_ref[...] = jnp.sum(bacc[...], axis=0, keepdims=True)

    return pl.pallas_call(
        body, name=name,
        out_shape=(jax.ShapeDtypeStruct((t, width), F32), jax.ShapeDtypeStruct((kp, width), F32),
                   jax.ShapeDtypeStruct((1, width), F32)),
        grid=(width // cw, nt),
        in_specs=[pl.BlockSpec((tm, cw), lambda j, i: (i, j)),
                  pl.BlockSpec((halo, cw), lambda j, i: (jnp.minimum((i + 1) * (tm // halo), last_halo), j)),
                  pl.BlockSpec((tm, cw), lambda j, i: (i, cb0 + j)),
                  pl.BlockSpec((halo, cw), lambda j, i: (jnp.maximum(i * (tm // halo) - 1, 0), cb0 + j)),
                  pl.BlockSpec((kp, cw), lambda j, i: (0, j))],
        out_specs=(pl.BlockSpec((tm, cw), lambda j, i: (i, j)),
                   pl.BlockSpec((kp, cw), lambda j, i: (0, j)),
                   pl.BlockSpec((1, cw), lambda j, i: (0, j))),
        scratch_shapes=[pltpu.VMEM((tm + halo, cw), F32), pltpu.VMEM((halo + tm, cw), F32),
                        pltpu.VMEM((kp, SUBLANES, cw), F32), pltpu.VMEM((SUBLANES, cw), F32)],
        compiler_params=_params(("parallel", "arbitrary")),
    )(dy, dy, src, src, w)


def _head_dup(g):
    r, c = _iota((LANES, LANES), 0), _iota((LANES, LANES), 1)
    return (r == g * ATTN_HEAD_DIM + (c & (ATTN_HEAD_DIM - 1))).astype(BF16)


def _half_mask(half):
    lane = _iota((1, LANES), 1)
    return ((lane >= half * ATTN_HEAD_DIM) & (lane < (half + 1) * ATTN_HEAD_DIM)).astype(F32)


def _band_mask(first_block):
    w = WINDOW
    qi = _iota((w, 2 * w), 0)
    kj = _iota((w, 2 * w), 1) - w
    rel = qi - kj
    return (rel >= 0) & (rel < w) & (jnp.logical_not(first_block) | (kj >= 0))


def _lane_pick(x, h):
    return jnp.sum(jnp.where(_iota(x.shape, 1) == h, x, 0.0), axis=1, keepdims=True)


def _attn_specs(nb, rev):
    w = WINDOW

    def blk(i):
        return nb - 1 - i if rev else i

    def row(b, i):
        return b * nb + blk(i)

    def prow(b, i):
        return b * nb + jnp.maximum(blk(i) - 1, 0)

    q = pl.BlockSpec((w, 512), lambda b, i: (row(b, i), OFF_Q // 512))
    kc = pl.BlockSpec((w, 128), lambda b, i: (row(b, i), OFF_K // 128))
    kp = pl.BlockSpec((w, 128), lambda b, i: (prow(b, i), OFF_K // 128))
    vc = pl.BlockSpec((w, 128), lambda b, i: (row(b, i), OFF_V // 128))
    vp = pl.BlockSpec((w, 128), lambda b, i: (prow(b, i), OFF_V // 128))
    z = pl.BlockSpec((w, 512), lambda b, i: (row(b, i), (OFF_Z + 1024) // 512))
    return q, kc, kp, vc, vp, z, row


def _attn_fwd(proj, sinks, nbatch, name):
    t = proj.shape[0]
    w = WINDOW
    nb = t // nbatch // w
    scale = ATTN_HEAD_DIM ** -0.5
    q_s, kc_s, kp_s, vc_s, vp_s, z_s, row = _attn_specs(nb, False)

    def body(q_ref, kc_ref, kp_ref, vc_ref, vp_ref, z_ref, sk_ref, y_ref, o_ref, lse_ref):
        first = pl.program_id(1) == 0
        mask = _band_mask(first)
        kk = jnp.concatenate([kp_ref[...], kc_ref[...]], axis=0).astype(MXU_DTYPE)
        vv = jnp.concatenate([vp_ref[...], vc_ref[...]], axis=0).astype(MXU_DTYPE)
        sk = sk_ref[...]
        lse_all = jnp.zeros((w, LANES), F32)
        lane = _iota((w, LANES), 1)
        for g in range(2):
            dup = _head_dup(g)
            kkd = _dot(kk, dup).astype(MXU_DTYPE)
            vvd = _dot(vv, dup)
            for jj in range(2):
                j = 2 * g + jj
                qp = q_ref[:, j * LANES:(j + 1) * LANES]
                op = jnp.zeros((w, LANES), F32)
                for half in range(2):
                    h = 2 * j + half
                    hm = _half_mask(half)
                    s = _dot(qp * hm, kkd, NT) * scale
                    s = jnp.where(mask, s, -1e30)
                    skh = _lane_pick(sk, h)
                    m = jnp.maximum(jnp.max(s, axis=1, keepdims=True), skh)
                    den = jnp.sum(jnp.exp(s - m), axis=1, keepdims=True) + jnp.exp(skh - m)
                    lse = m + jnp.log(den)
                    pr = jnp.exp(s - lse)
                    op = op + _dot(pr, vvd * hm)
                    lse_all = jnp.where(lane == h, lse, lse_all)
                o_ref[:, j * LANES:(j + 1) * LANES] = op
                y_ref[:, j * LANES:(j + 1) * LANES] = (
                    op * _silu(z_ref[:, j * LANES:(j + 1) * LANES])).astype(y_ref.dtype)
        lse_ref[...] = lse_all

    return pl.pallas_call(
        body, name=name,
        out_shape=(jax.ShapeDtypeStruct((t, 512), MXU_DTYPE), jax.ShapeDtypeStruct((t, 512), F32),
                   jax.ShapeDtypeStruct((t, LANES), F32)),
        grid=(nbatch, nb),
        in_specs=[q_s, kc_s, kp_s, vc_s, vp_s, z_s, pl.BlockSpec((1, LANES), lambda b, i: (0, 0))],
        out_specs=(pl.BlockSpec((w, 512), lambda b, i: (row(b, i), 0)),
                   pl.BlockSpec((w, 512), lambda b, i: (row(b, i), 0)),
                   pl.BlockSpec((w, LANES), lambda b, i: (row(b, i), 0))),
        compiler_params=_params(("parallel", "parallel")),
    )(proj, proj, proj, proj, proj, proj, sinks)


def _attn_bwd(dycat, proj, o, lse, sinks, nbatch, name):
    t = proj.shape[0]
    w = WINDOW
    nb = t // nbatch // w
    scale = ATTN_HEAD_DIM ** -0.5
    q_s, kc_s, kp_s, vc_s, vp_s, z_s, row = _attn_specs(nb, True)

    def body(dy_ref, q_ref, kc_ref, kp_ref, vc_ref, vp_ref, z_ref, o_ref, lse_ref, sk_ref,
             dqkv_ref, dz_ref, dsk_ref, kcarry, vcarry, sacc):
        b, i = pl.program_id(0), pl.program_id(1)

        @pl.when((b == 0) & (i == 0))
        def _():
            sacc[...] = jnp.zeros_like(sacc)

        @pl.when(i == 0)
        def _():
            kcarry[...] = jnp.zeros_like(kcarry)
            vcarry[...] = jnp.zeros_like(vcarry)

        first = i == nb - 1
        mask = _band_mask(first)
        kk = jnp.concatenate([kp_ref[...], kc_ref[...]], axis=0).astype(MXU_DTYPE)
        vv = jnp.concatenate([vp_ref[...], vc_ref[...]], axis=0).astype(MXU_DTYPE)
        sk = sk_ref[...]
        lse_all = lse_ref[...]
        lane1 = _iota((1, LANES), 1)
        dkk = jnp.zeros((2 * w, LANES), F32)
        dvv = jnp.zeros((2 * w, LANES), F32)
        dsk = jnp.zeros((1, LANES), F32)
        for g in range(2):
            dup = _head_dup(g)
            kkd = _dot(kk, dup).astype(MXU_DTYPE)
            vvd = _dot(vv, dup).astype(MXU_DTYPE)
            dkd = jnp.zeros((2 * w, LANES), F32)
            dvd = jnp.zeros((2 * w, LANES), F32)
            for jj in range(2):
                j = 2 * g + jj
                cols = slice(j * LANES, (j + 1) * LANES)
                qp, zp, ov, dy = q_ref[:, cols], z_ref[:, cols], o_ref[:, cols], dy_ref[:, cols]
                dz_ref[:, cols] = dy * ov * _dsilu(zp)
                do = dy * _silu(zp)
                dq = jnp.zeros((w, LANES), F32)
                for half in range(2):
                    h = 2 * j + half
                    hm = _half_mask(half)
                    qh = qp * hm
                    doh = do * hm
                    delta = jnp.sum(doh * ov, axis=1, keepdims=True)
                    lse_h = _lane_pick(lse_all, h)
                    s = _dot(qh, kkd, NT) * scale
                    s = jnp.where(mask, s, -1e30)
                    pr = jnp.exp(s - lse_h)
                    dp = _dot(doh, vvd, NT)
                    ds = pr * (dp - delta)
                    dq = dq + _dot(ds, kkd) * hm * scale
                    dkd = dkd + _dot(ds, qh, TN) * scale
                    dvd = dvd + _dot(pr, doh, TN)
                    psink = jnp.exp(_lane_pick(sk, h) - lse_h)
                    dsk = dsk - jnp.where(lane1 == h, jnp.sum(psink * delta), 0.0)
                dqkv_ref[:, cols] = dq
            dkk = dkk + _xdot(dkd, dup, NT, passes=2)
            dvv = dvv + _xdot(dvd, dup, NT, passes=2)
        dqkv_ref[:, 512:640] = dkk[w:, :] + kcarry[...]
        dqkv_ref[:, 640:768] = dvv[w:, :] + vcarry[...]
        kcarry[...] = dkk[:w, :]
        vcarry[...] = dvv[:w, :]
        sacc[...] += dsk

        @pl.when((b == nbatch - 1) & (i == nb - 1))
        def _():
            dsk_ref[...] = sacc[...]

    return pl.pallas_call(
        body, name=name,
        out_shape=(jax.ShapeDtypeStruct((t, 768), F32), jax.ShapeDtypeStruct((t, 512), F32),
                   jax.ShapeDtypeStruct((1, LANES), F32)),
        grid=(nbatch, nb),
        in_specs=[pl.BlockSpec((w, 512), lambda b, i: (row(b, i), 1024 // 512)),
                  q_s, kc_s, kp_s, vc_s, vp_s, z_s,
                  pl.BlockSpec((w, 512), lambda b, i: (row(b, i), 0)),
                  pl.BlockSpec((w, LANES), lambda b, i: (row(b, i), 0)),
                  pl.BlockSpec((1, LANES), lambda b, i: (0, 0))],
        out_specs=(pl.BlockSpec((w, 768), lambda b, i: (row(b, i), 0)),
                   pl.BlockSpec((w, 512), lambda b, i: (row(b, i), 0)),
                   pl.BlockSpec((1, LANES), lambda b, i: (0, 0))),
        scratch_shapes=[pltpu.VMEM((w, LANES), F32), pltpu.VMEM((w, LANES), F32),
                        pltpu.VMEM((1, LANES), F32)],
        compiler_params=_params(("arbitrary", "arbitrary")),
    )(dycat, proj, proj, proj, proj, proj, proj, o, lse, sinks)


SSD_WIDTH = SSD_HEADS * SSD_HEAD_DIM
GROUP_ROWS = SSD_WIDTH // 2


def _expand_mat():
    r, c = _iota((LANES, SSD_WIDTH), 0), _iota((LANES, SSD_WIDTH), 1)
    return (r == lax.shift_right_logical(c, 6)).astype(BF16)


def _expand_mat_t():
    r, c = _iota((SSD_WIDTH, LANES), 0), _iota((SSD_WIDTH, LANES), 1)
    return (c == lax.shift_right_logical(r, 6)).astype(BF16)


def _ssd_common(u_ref, dt_ref, dtb_ref, a_ref):
    q = CHUNK
    act = _silu(u_ref[...])
    xs = act[:, :SSD_WIDTH]
    bm = act[:, SSD_WIDTH:SSD_WIDTH + 256]
    cm = act[:, SSD_WIDTH + 256:]
    dtp = _softplus(dt_ref[...] + dtb_ref[...])
    a = dtp * a_ref[...]
    tril = (_iota((q, q), 0) >= _iota((q, q), 1)).astype(BF16)
    acs = _xdot_r(tril, a)
    acs_t = acs.T
    e = _expand_mat()
    dt_x = _xdot(dtp, e)
    ea = jnp.exp(_xdot(acs, e))
    a_end = jnp.sum(jnp.where(_iota(acs.shape, 0) == q - 1, acs, 0.0), axis=0, keepdims=True)
    dec = jnp.exp(_xdot(a_end - acs, e))
    a_end_col = jnp.broadcast_to(_lane_pick(acs_t, q - 1), (LANES, LANES))
    s_scale = jnp.exp(_xdot_r(_expand_mat_t(), a_end_col))
    return act, xs, bm, cm, dtp, acs, acs_t, dt_x, ea, dec, s_scale, tril


def _decay_mat(acs, acs_t, h):
    q = CHUNK
    col = _lane_pick(acs, h)
    rowv = jnp.sum(jnp.where(_iota(acs_t.shape, 0) == h, acs_t, 0.0), axis=0, keepdims=True)
    causal = _iota((q, q), 0) >= _iota((q, q), 1)
    return jnp.exp(jnp.where(causal, col - rowv, -1e30))


def _ssd_fwd(u, proj, dtb, a_neg, d_x, nbatch, name):
    t = u.shape[0]
    q = CHUNK
    nc = t // nbatch // q

    def body(u_ref, dt_ref, dtb_ref, a_ref, dx_ref, y_ref, st_ref, state):
        c = pl.program_id(1)

        @pl.when(c == 0)
        def _():
            state[...] = jnp.zeros_like(state)

        st_ref[...] = state[...]
        act, xs, bm, cm, dtp, acs, acs_t, dt_x, ea, dec, s_scale, _ = _ssd_common(u_ref, dt_ref, dtb_ref, a_ref)
        xdt = xs * dt_x
        xdec = xdt * dec
        lo, hi = _half_mask(0), _half_mask(1)
        for g in range(2):
            bg = bm[:, g * LANES:(g + 1) * LANES]
            cg = cm[:, g * LANES:(g + 1) * LANES]
            rows = slice(g * GROUP_ROWS, (g + 1) * GROUP_ROWS)
            sg = state[rows, :]
            cb = _dot(cg, bg, NT)
            yoff = _dot(cg, sg, NT)
            for j in range(4):
                pj = g * 4 + j
                cols = slice(pj * LANES, (pj + 1) * LANES)
                xp = xdt[:, cols]
                m0 = cb * _decay_mat(acs, acs_t, 2 * pj)
                m1 = cb * _decay_mat(acs, acs_t, 2 * pj + 1)
                yp = _dot(m0, xp * lo) + _dot(m1, xp * hi)
                yp = yp + yoff[:, j * LANES:(j + 1) * LANES] * ea[:, cols]
                y_ref[:, cols] = yp + dx_ref[:, cols] * xs[:, cols]
            state[rows, :] = s_scale[rows, :] * sg + _dot(xdec[:, rows], bg, TN)

    vec = pl.BlockSpec((1, LANES), lambda b, c: (0, 0))
    return pl.pallas_call(
        body, name=name,
        out_shape=(jax.ShapeDtypeStruct((t, SSD_WIDTH), F32),
                   jax.ShapeDtypeStruct((nbatch * nc * SSD_WIDTH, SSD_STATE), F32)),
        grid=(nbatch, nc),
        in_specs=[pl.BlockSpec((q, SSD_CONV_DIM), lambda b, c: (b * nc + c, 0)),
                  pl.BlockSpec((q, LANES), lambda b, c: (b * nc + c, OFF_DT // LANES)),
                  vec, vec, pl.BlockSpec((1, SSD_WIDTH), lambda b, c: (0, 0))],
        out_specs=(pl.BlockSpec((q, SSD_WIDTH), lambda b, c: (b * nc + c, 0)),
                   pl.BlockSpec((SSD_WIDTH, SSD_STATE), lambda b, c: (b * nc + c, 0))),
        scratch_shapes=[pltpu.VMEM((SSD_WIDTH, SSD_STATE), F32)],
        compiler_params=_params(("parallel", "arbitrary")),
    )(u, proj, dtb, a_neg, d_x)


def _ssd_bwd(dy, u, proj, states, dtb, a_neg, d_x, nbatch, name):
    t = u.shape[0]
    q = CHUNK
    nc = t // nbatch // q

    def body(dy_ref, u_ref, dt_ref, st_ref, dtb_ref, a_ref, dx_ref,
             du_ref, ddt_ref, dal_ref, dd_ref, dtbg_ref, dstate, acc_a, acc_d, acc_b):
        b, c = pl.program_id(0), pl.program_id(1)

        @pl.when((b == 0) & (c == 0))
        def _():
            acc_a[...] = jnp.zeros_like(acc_a)
            acc_d[...] = jnp.zeros_like(acc_d)
            acc_b[...] = jnp.zeros_like(acc_b)

        @pl.when(c == 0)
        def _():
            dstate[...] = jnp.zeros_like(dstate)

        act, xs, bm, cm, dtp, acs, acs_t, dt_x, ea, dec, s_scale, tril = _ssd_common(
            u_ref, dt_ref, dtb_ref, a_ref)
        xdt = xs * dt_x
        xdec = xdt * dec
        dyv = dy_ref[...]
        dye = dyv * ea
        lo, hi = _half_mask(0), _half_mask(1)
        et = _expand_mat_t()
        dxdt_parts, db_parts, dc_parts, dxst_parts, yoff_parts = [], [], [], [], []
        end_sum = jnp.zeros((LANES, LANES), F32)
        dal_diag = jnp.zeros((q, LANES), F32)
        lane_q = _iota((q, LANES), 1)
        for g in range(2):
            bg = bm[:, g * LANES:(g + 1) * LANES]
            cg = cm[:, g * LANES:(g + 1) * LANES]
            rows = slice(g * GROUP_ROWS, (g + 1) * GROUP_ROWS)
            sg = st_ref[rows, :]
            dsg = dstate[rows, :]
            cb = _dot(cg, bg, NT)
            yoff_parts.append(_dot(cg, sg, NT))
            dcb = jnp.zeros((q, q), F32)
            parts = []
            for j in range(4):
                pj = g * 4 + j
                cols = slice(pj * LANES, (pj + 1) * LANES)
                xp = xdt[:, cols]
                dy0, dy1 = dyv[:, cols] * lo, dyv[:, cols] * hi
                l0 = _decay_mat(acs, acs_t, 2 * pj)
                l1 = _decay_mat(acs, acs_t, 2 * pj + 1)
                g0, g1 = _dot(dy0, xp, NT), _dot(dy1, xp, NT)
                m0, m1 = cb * l0, cb * l1
                dcb = dcb + g0 * l0 + g1 * l1
                parts.append(_dot(m0, dy0, TN) + _dot(m1, dy1, TN))
                for hh, wmat in enumerate((g0 * m0, g1 * m1)):
                    sel = (lane_q == 2 * pj + hh).astype(F32)
                    dal_diag = dal_diag + _dot(wmat, sel) - _dot(wmat, sel, TN)
            dxst = _dot(bg, dsg, NT) * dec[:, rows]
            dxst_parts.append(dxst)
            dxdt_parts.append(jnp.concatenate(parts, axis=1) + dxst)
            dc_parts.append(_dot(dcb, bg) + _dot(dye[:, rows], sg))
            db_parts.append(_dot(dcb, cg, TN) + _dot(xdec[:, rows], dsg))
            s_next = s_scale[rows, :] * sg + _dot(xdec[:, rows], bg, TN)
            end_sum = end_sum + _xdot(dsg * s_next, et[rows, :], TN, passes=2)
            dstate[rows, :] = _dot(dye[:, rows], cg, TN) + s_scale[rows, :] * dsg
        dxdt = jnp.concatenate(dxdt_parts, axis=1)
        dxv = dx_ref[...]
        yoff = jnp.concatenate(yoff_parts, axis=1) * ea
        dalpha = dal_diag + _xdot(dyv * yoff - xdt * jnp.concatenate(dxst_parts, axis=1), et)
        end_row = jnp.sum(end_sum, axis=0, keepdims=True)
        dalpha = dalpha + jnp.where(_iota((q, LANES), 0) == q - 1, end_row, 0.0)
        da = _xdot_r(tril, dalpha, TN)
        ddtp = da * a_ref[...] + _xdot(dxdt * xs, et)
        acc_a[...] += _rowsum8(da * dtp)
        acc_d[...] += _rowsum8(_xdot(dyv * xs, et))
        ddt_raw = ddtp * _sigmoid(dt_ref[...] + dtb_ref[...])
        acc_b[...] += _rowsum8(ddt_raw)
        ddt_ref[...] = ddt_raw
        dxs = dxdt * dt_x + dxv * dyv
        dact = jnp.concatenate([dxs] + db_parts + dc_parts, axis=1)
        du_ref[...] = dact * _dsilu(u_ref[...])

        @pl.when((b == nbatch - 1) & (c == nc - 1))
        def _():
            dal_ref[...] = jnp.sum(acc_a[...], axis=0, keepdims=True) * a_ref[...]
            dd_ref[...] = jnp.sum(acc_d[...], axis=0, keepdims=True)
            dtbg_ref[...] = jnp.sum(acc_b[...], axis=0, keepdims=True)

    def rowblk(b, c):
        return b * nc + (nc - 1 - c)

    vec = pl.BlockSpec((1, LANES), lambda b, c: (0, 0))
    wide = pl.BlockSpec((q, SSD_WIDTH), lambda b, c: (rowblk(b, c), 0))
    return pl.pallas_call(
        body, name=name,
        out_shape=(jax.ShapeDtypeStruct((t, SSD_CONV_DIM), F32), jax.ShapeDtypeStruct((t, LANES), F32),
                   jax.ShapeDtypeStruct((1, LANES), F32), jax.ShapeDtypeStruct((1, LANES), F32),
                   jax.ShapeDtypeStruct((1, LANES), F32)),
        grid=(nbatch, nc),
        in_specs=[wide,
                  pl.BlockSpec((q, SSD_CONV_DIM), lambda b, c: (rowblk(b, c), 0)),
                  pl.BlockSpec((q, LANES), lambda b, c: (rowblk(b, c), OFF_DT // LANES)),
                  pl.BlockSpec((SSD_WIDTH, SSD_STATE), lambda b, c: (rowblk(b, c), 0)),
                  vec, vec, pl.BlockSpec((1, SSD_WIDTH), lambda b, c: (0, 0))],
        out_specs=(pl.BlockSpec((q, SSD_CONV_DIM), lambda b, c: (rowblk(b, c), 0)),
                   pl.BlockSpec((q, LANES), lambda b, c: (rowblk(b, c), 0)),
                   vec, vec, vec),
        scratch_shapes=[pltpu.VMEM((SSD_WIDTH, SSD_STATE), F32), pltpu.VMEM((SUBLANES, LANES), F32),
                        pltpu.VMEM((SUBLANES, LANES), F32), pltpu.VMEM((SUBLANES, LANES), F32)],
        compiler_params=_params(("arbitrary", "arbitrary")),
    )(dy, u, proj, states, dtb, a_neg, d_x)


def _pad_rows(w, rows):
    return jnp.concatenate([w, jnp.zeros((rows - w.shape[0], w.shape[1]), w.dtype)], axis=0)


def _pad_lanes(v):
    return jnp.concatenate([v, jnp.zeros((LANES - v.shape[0],), v.dtype)]).reshape(1, LANES)


def _to_padded_cols(w):
    z = jnp.zeros(w.shape[:-1] + (OFF_CONF - OFF_DT - 16,), w.dtype)
    return jnp.concatenate([w[..., :REF_OFF_DT], w[..., REF_OFF_Q:REF_OFF_CONF], w[..., REF_OFF_DT:REF_OFF_Q], z,
                            w[..., REF_OFF_CONF:]], axis=-1)


def _from_padded_cols(w):
    return jnp.concatenate([w[..., :OFF_Q], w[..., OFF_DT:OFF_DT + 16], w[..., OFF_Q:OFF_DT], w[..., OFF_CONF:]],
                           axis=-1)


def _layer_params(li, w_in_p, w_out, conv_w, dw_w, small):
    return dict(
        w_in_p=w_in_p, w_out=w_out,
        conv_w=_pad_rows(conv_w, SUBLANES), dw_w=_pad_rows(dw_w, 32),
        norm_w=small["norm_w"][li].reshape(1, -1),
        conv_b=small["ssd_conv_b"][li].reshape(1, -1),
        dtb=_pad_lanes(small["ssd_dt_bias"][li]),
        a_neg=_pad_lanes(-jnp.exp(small["ssd_a_log"][li])),
        d_x=jnp.repeat(small["ssd_d"][li], SSD_HEAD_DIM).reshape(1, -1),
        ssd_norm_w=small["ssd_norm_w"][li].reshape(1, -1),
        sinks=_pad_lanes(small["attn_sinks"][li]),
        dw_b=small["conf_dw_b"][li].reshape(1, -1),
        ln_w=small["conf_ln_w"][li].reshape(1, -1),
        ln_b=small["conf_ln_b"][li].reshape(1, -1),
    )


def _layer_fwd(x, p, nbatch, seq, tag):
    h = _rmsnorm_fwd(x, p["norm_w"], name=f"rmsnorm_fwd_{tag}")
    proj = _matmul(h, p["w_in_p"], "nn", F32, 512, 512, 1024, name=f"proj_fwd_{tag}")
    u = _conv_fwd(proj, OFF_XBC, SSD_CONV_DIM, p["conv_w"], p["conv_b"], SSD_CONV, seq, name=f"ssd_conv_fwd_{tag}")
    y, states = _ssd_fwd(u, proj, p["dtb"], p["a_neg"], p["d_x"], nbatch, name=f"ssd_fwd_{tag}")
    y_ssd = _gated_norm_fwd(y, proj, p["ssd_norm_w"], name=f"gated_norm_fwd_{tag}")
    y_attn, o, lse = _attn_fwd(proj, p["sinks"], nbatch, name=f"attn_fwd_{tag}")
    c0 = _glu_fwd(proj, name=f"glu_fwd_{tag}")
    c1 = _conv_fwd(c0, 0, CONF_WIDTH, p["dw_w"], p["dw_b"], CONF_KERNEL, seq, name=f"conf_conv_fwd_{tag}")
    y_conf = _conf_post_fwd(c1, proj, p["ln_w"], p["ln_b"], name=f"conf_post_fwd_{tag}")
    ycat = jnp.concatenate([y_ssd, y_attn, y_conf], axis=1)
    x_new = _matmul(ycat, p["w_out"], "nn", F32, 512, 512, 1024, name=f"out_fwd_{tag}", residual=x)
    return x_new, dict(x=x, h=h, proj=proj, u=u, y=y, states=states, o=o, lse=lse, c0=c0, c1=c1, ycat=ycat)


def _layer_bwd(dx_out, p, s, nbatch, seq, tag):
    proj = s["proj"]
    dycat = _matmul(dx_out, p["w_out"], "nt", F32, 512, 512, 1024, name=f"out_bwd_dy_{tag}")
    dw_out = _matmul(s["ycat"], dx_out, "tn", F32, 512, 512, 1024, name=f"out_bwd_dw_{tag}")
    dc1, dz_conf, dln_w, dln_b = _conf_post_bwd(dycat, s["c1"], proj, p["ln_w"], p["ln_b"],
                                                name=f"conf_post_bwd_{tag}")
    dc0, ddw_w, ddw_b = _conv_bwd(dc1, s["c0"], 0, CONF_WIDTH, p["dw_w"], CONF_KERNEL, seq,
                                  name=f"conf_conv_bwd_{tag}")
    dconf = _glu_bwd(dc0, proj, name=f"glu_bwd_{tag}")
    dqkv, dz_attn, dsinks = _attn_bwd(dycat, proj, s["o"], s["lse"], p["sinks"], nbatch, name=f"attn_bwd_{tag}")
    dy, dz_ssd, dssd_norm_w = _gated_norm_bwd(dycat, s["y"], proj, p["ssd_norm_w"], name=f"gated_norm_bwd_{tag}")
    du, ddt, da_log, dd, ddtb = _ssd_bwd(dy, s["u"], proj, s["states"], p["dtb"], p["a_neg"], p["d_x"],
                                         nbatch, name=f"ssd_bwd_{tag}")
    dxbc, dconv_w, dconv_b = _conv_bwd(du, proj, OFF_XBC, SSD_CONV_DIM, p["conv_w"], SSD_CONV, seq,
                                       name=f"ssd_conv_bwd_{tag}")
    t = proj.shape[0]
    dproj = jnp.concatenate([dz_ssd, dz_attn, dz_conf, dxbc, dqkv, ddt, jnp.zeros((t, LANES), F32), dconf],
                            axis=1).astype(MXU_DTYPE)
    dh = _matmul(dproj, p["w_in_p"], "nt", F32, 512, 512, 512, name=f"proj_bwd_dh_{tag}")
    dw_in_p = _matmul(s["h"], dproj, "tn", F32, 512, 512, 1024, name=f"proj_bwd_dw_{tag}")
    dx_in, dnorm_w = _rmsnorm_bwd(dh, s["x"], p["norm_w"], dx_out, name=f"rmsnorm_bwd_{tag}")
    grads = dict(
        norm_w=dnorm_w[0], w_in_p=dw_in_p, ssd_conv_w=dconv_w[:SSD_CONV], ssd_conv_b=dconv_b[0],
        ssd_dt_bias=ddtb[0, :SSD_HEADS], ssd_a_log=da_log[0, :SSD_HEADS], ssd_d=dd[0, :SSD_HEADS],
        ssd_norm_w=dssd_norm_w[0], attn_sinks=dsinks[0, :ATTN_Q_HEADS], conf_dw_w=ddw_w[:CONF_KERNEL],
        conf_dw_b=ddw_b[0], conf_ln_w=dln_w[0], conf_ln_b=dln_b[0], w_out=dw_out)
    return dx_in, grads


def _local_step(x, target, layer_params, final_norm_w):
    nbatch, seq, d = x.shape
    xt = x.reshape(nbatch * seq, d)
    saved = []
    for li, p in enumerate(layer_params):
        xt, s = _layer_fwd(xt, p, nbatch, seq, f"l{li}")
        saved.append(s)
    loss, dx, dfinal = _loss_head(xt, target.reshape(nbatch * seq, d), final_norm_w.reshape(1, d), name="loss_head")
    grads = [None] * len(layer_params)
    for li in reversed(range(len(layer_params))):
        dx, grads[li] = _layer_bwd(dx, layer_params[li], saved[li], nbatch, seq, f"l{li}")
    return loss[0, 0], dx.reshape(nbatch, seq, d), grads, dfinal[0]


MESH = pl.DeviceIdType.MESH
N_CHIPS = 4
ANY = pl.BlockSpec(memory_space=pl.ANY)


def _mesh_pos():
    return lax.axis_index("x"), lax.axis_index("y"), lax.axis_index("c")


def _other_chips(x, y):
    return [(1 - x, y), (x, 1 - y), (1 - x, 1 - y)]


def _gather_weights(big, small, name):
    nbig, nsmall = len(big), len(small)
    n_ici = 3 * (nbig + nsmall)
    n_fwd = 3 * nbig

    def body(*refs):
        ins = refs[:nbig + nsmall]
        outs = refs[nbig + nsmall:2 * (nbig + nsmall)]
        send_sems, recv_sems, local_sems = refs[2 * (nbig + nsmall):]
        x, y, c = _mesh_pos()
        me = 2 * x + y
        sibling = (x, y, 1 - c)
        chips = _other_chips(x, y)
        local = []
        for a in range(nbig + nsmall):
            cp = pltpu.make_async_copy(ins[a], outs[a].at[me], local_sems.at[a])
            cp.start()
            local.append(cp)

        def ici(a, j, origin, dest):
            if a < nbig:
                src = ins[a].at[c] if origin is None else outs[a].at[origin, c]
                dst = outs[a].at[me if origin is None else origin, c]
            else:
                src = ins[a] if origin is None else outs[a].at[origin]
                dst = outs[a].at[me if origin is None else origin]
            k = a * 3 + j
            return pltpu.make_async_remote_copy(src_ref=src, dst_ref=dst, send_sem=send_sems.at[k],
                                                recv_sem=recv_sems.at[k], device_id=dest, device_id_type=MESH)

        def fwd(a, j, origin, half):
            k = n_ici + a * 3 + j
            ref = outs[a].at[origin, half]
            return pltpu.make_async_remote_copy(src_ref=ref, dst_ref=ref, send_sem=send_sems.at[k],
                                                recv_sem=recv_sems.at[k], device_id=sibling, device_id_type=MESH)

        sends = []
        for j, (px, py) in enumerate(chips):
            for a in range(nbig + nsmall):
                cp = ici(a, j, None, (px, py, c))
                cp.start()
                sends.append(cp)
        for j, (px, py) in enumerate(chips):
            origin = 2 * px + py
            for a in range(nbig):
                ici(a, j, origin, (px, py, c)).wait_recv()
                cp = fwd(a, j, origin, c)
                cp.start()
                sends.append(cp)
        for j, (px, py) in enumerate(chips):
            origin = 2 * px + py
            for a in range(nbig, nbig + nsmall):
                ici(a, j, origin, (px, py, c)).wait_recv()
            for a in range(nbig):
                fwd(a, j, origin, 1 - c).wait_recv()
        for cp in sends:
            cp.wait_send()
        for cp in local:
            cp.wait()

    out_shape = tuple(jax.ShapeDtypeStruct((N_CHIPS,) + a.shape, a.dtype) for a in list(big) + list(small))
    return pl.pallas_call(
        body, name=name, out_shape=out_shape,
        in_specs=[ANY] * (nbig + nsmall), out_specs=tuple([ANY] * (nbig + nsmall)),
        scratch_shapes=[pltpu.SemaphoreType.DMA((n_ici + n_fwd,)), pltpu.SemaphoreType.DMA((n_ici + n_fwd,)),
                        pltpu.SemaphoreType.DMA((nbig + nsmall,))],
    )(*big, *small)


def _pair_swap_halves(arrs, name):
    n = len(arrs)

    def body(*refs):
        ins, outs = refs[:n], refs[n:2 * n]
        send_sems, recv_sems = refs[2 * n:]
        x, y, c = _mesh_pos()
        cps = [pltpu.make_async_remote_copy(src_ref=ins[a].at[1 - c], dst_ref=outs[a], send_sem=send_sems.at[a],
                                            recv_sem=recv_sems.at[a], device_id=(x, y, 1 - c), device_id_type=MESH)
               for a in range(n)]
        for cp in cps:
            cp.start()
        for cp in cps:
            cp.wait()

    return pl.pallas_call(
        body, name=name, out_shape=tuple(jax.ShapeDtypeStruct(a.shape[1:], a.dtype) for a in arrs),
        in_specs=[ANY] * n, out_specs=tuple([ANY] * n),
        scratch_shapes=[pltpu.SemaphoreType.DMA((n,)), pltpu.SemaphoreType.DMA((n,))],
    )(*arrs)


def _chip_scatter(arrs, name):
    n = len(arrs)

    def body(*refs):
        ins, outs = refs[:n], refs[n:2 * n]
        send_sems, recv_sems, local_sems = refs[2 * n:]
        x, y, c = _mesh_pos()
        me = 2 * x + y
        local = [pltpu.make_async_copy(ins[a].at[me], outs[a].at[me], local_sems.at[a]) for a in range(n)]
        for cp in local:
            cp.start()
        cps = []
        for j, (px, py) in enumerate(_other_chips(x, y)):
            for a in range(n):
                cps.append(pltpu.make_async_remote_copy(
                    src_ref=ins[a].at[2 * px + py], dst_ref=outs[a].at[me], send_sem=send_sems.at[a * 3 + j],
                    recv_sem=recv_sems.at[a * 3 + j], device_id=(px, py, c), device_id_type=MESH))
        for cp in cps:
            cp.start()
        for cp in cps:
            cp.wait()
        for cp in local:
            cp.wait()

    return pl.pallas_call(
        body, name=name, out_shape=tuple(jax.ShapeDtypeStruct(a.shape, a.dtype) for a in arrs),
        in_specs=[ANY] * n, out_specs=tuple([ANY] * n),
        scratch_shapes=[pltpu.SemaphoreType.DMA((3 * n,)), pltpu.SemaphoreType.DMA((3 * n,)),
                        pltpu.SemaphoreType.DMA((n,))],
    )(*arrs)


def _pair_gather(arrs, name):
    n = len(arrs)

    def body(*refs):
        ins, outs = refs[:n], refs[n:2 * n]
        send_sems, recv_sems, local_sems = refs[2 * n:]
        x, y, c = _mesh_pos()
        local = [pltpu.make_async_copy(ins[a], outs[a].at[c], local_sems.at[a]) for a in range(n)]
        cps = [pltpu.make_async_remote_copy(src_ref=ins[a], dst_ref=outs[a].at[c], send_sem=send_sems.at[a],
                                            recv_sem=recv_sems.at[a], device_id=(x, y, 1 - c), device_id_type=MESH)
               for a in range(n)]
        for cp in local + cps:
            cp.start()
        for cp in cps + local:
            cp.wait()

    return pl.pallas_call(
        body, name=name, out_shape=tuple(jax.ShapeDtypeStruct((2,) + a.shape, a.dtype) for a in arrs),
        in_specs=[ANY] * n, out_specs=tuple([ANY] * n),
        scratch_shapes=[pltpu.SemaphoreType.DMA((n,)), pltpu.SemaphoreType.DMA((n,)), pltpu.SemaphoreType.DMA((n,))],
    )(*arrs)


N_DEV = 8


def _allreduce_small(pack, name):
    r = pack.shape[0]

    def body(p_ref, o_ref, land, send_sems, recv_sems):
        x, y, c = _mesh_pos()
        me = 4 * x + 2 * y + c
        cps = []
        for k in range(1, N_DEV):
            peer = (x ^ (k >> 2), y ^ ((k >> 1) & 1), c ^ (k & 1))
            cps.append(pltpu.make_async_remote_copy(src_ref=p_ref, dst_ref=land.at[me], send_sem=send_sems.at[k - 1],
                                                    recv_sem=recv_sems.at[k - 1], device_id=peer, device_id_type=MESH))
        for cp in cps:
            cp.start()
        land[me] = p_ref[...]
        for cp in cps:
            cp.wait()
        total = land[0]
        for d in range(1, N_DEV):
            total = total + land[d]
        o_ref[...] = total

    vm = pl.BlockSpec(memory_space=pltpu.VMEM)
    return pl.pallas_call(
        body, name=name, out_shape=jax.ShapeDtypeStruct(pack.shape, F32),
        in_specs=[vm], out_specs=vm,
        scratch_shapes=[pltpu.VMEM((N_DEV, r, LANES), F32), pltpu.SemaphoreType.DMA((N_DEV - 1,)),
                        pltpu.SemaphoreType.DMA((N_DEV - 1,))],
    )(pack)


BIG_ROWS = 128


def _cast_halves(w, name):
    nl, r, cdim = w.shape
    tr = BIG_ROWS
    per = r // 2 // tr

    def body(w_ref, o_ref):
        o_ref[...] = w_ref[...].astype(o_ref.dtype)

    return pl.pallas_call(
        body, name=name, out_shape=jax.ShapeDtypeStruct((2, nl, r // 2, cdim), MXU_DTYPE),
        grid=(nl, 2, per),
        in_specs=[pl.BlockSpec((None, tr, cdim), lambda l, h, i: (l, h * per + i, 0))],
        out_specs=pl.BlockSpec((None, None, tr, cdim), lambda l, h, i: (h, l, i, 0)),
        compiler_params=_params(("parallel", "parallel", "parallel")),
    )(w)


def _sum_lead(parts, name, select=None):
    if isinstance(parts, (list, tuple)):
        k = len(parts)
        shape = parts[0].shape
    else:
        k = parts.shape[0]
        shape = parts.shape[1:]
    lead = int(np.prod(shape[:-2]))
    r, cdim = shape[-2], shape[-1]
    tr = BIG_ROWS

    def body(*refs):
        o_ref = refs[-1]
        if isinstance(parts, (list, tuple)):
            total = refs[0][...]
            for a in range(1, k):
                total = total + refs[a][...]
        else:
            total = refs[0][0]
            for a in range(1, k):
                total = total + refs[0][a]
        o_ref[...] = total

    if isinstance(parts, (list, tuple)):
        args = [p.reshape(lead, r, cdim) for p in parts]
        in_specs = [pl.BlockSpec((None, tr, cdim), lambda l, i: (l, i, 0))] * k
    else:
        args = [parts.reshape(k, lead, r, cdim)]
        in_specs = [pl.BlockSpec((k, None, tr, cdim), lambda l, i: (0, l, i, 0))]
    out = pl.pallas_call(
        body, name=name, out_shape=jax.ShapeDtypeStruct((lead, r, cdim), F32),
        grid=(lead, r // tr), in_specs=in_specs,
        out_specs=pl.BlockSpec((None, tr, cdim), lambda l, i: (l, i, 0)),
        compiler_params=_params(("parallel", "parallel")),
    )(*args)
    return out.reshape(shape)


def _adam_math(w, g, m, v):
    m2 = ADAM_B1 * m + (1.0 - ADAM_B1) * g
    v2 = ADAM_B2 * v + (1.0 - ADAM_B2) * (g * g)
    m_hat = m2 / (1.0 - ADAM_B1 ** ADAM_STEP)
    v_hat = v2 / (1.0 - ADAM_B2 ** ADAM_STEP)
    delta = -ADAM_LR * (m_hat / (jnp.sqrt(v_hat) + ADAM_EPS) + ADAM_WD * w)
    return delta, m2, v2


def _adam_big(w, g_halves, m, v, name):
    nl, r, cdim = w.shape
    tr = BIG_ROWS
    per = r // 2 // tr

    def body(w_ref, g_ref, m_ref, v_ref, go_ref, d_ref, mo_ref, vo_ref):
        g = g_ref[...]
        delta, m2, v2 = _adam_math(w_ref[...], g, m_ref[...], v_ref[...])
        go_ref[...] = g
        d_ref[...] = delta
        mo_ref[...] = m2
        vo_ref[...] = v2

    full = pl.BlockSpec((None, tr, cdim), lambda l, h, i: (l, h * per + i, 0))
    half = pl.BlockSpec((None, None, tr, cdim), lambda l, h, i: (h, l, i, 0))
    shp = jax.ShapeDtypeStruct(w.shape, F32)
    return pl.pallas_call(
        body, name=name, out_shape=(shp, shp, shp, shp),
        grid=(nl, 2, per), in_specs=[full, half, full, full], out_specs=(full, full, full, full),
        compiler_params=_params(("parallel", "parallel", "parallel")),
    )(w, g_halves, m, v)


def _adam_small(w, g, m, v, name):
    def body(w_ref, g_ref, m_ref, v_ref, d_ref, mo_ref, vo_ref):
        delta, m2, v2 = _adam_math(w_ref[...], g_ref[...], m_ref[...], v_ref[...])
        d_ref[...] = delta
        mo_ref[...] = m2
        vo_ref[...] = v2

    shp = jax.ShapeDtypeStruct(w.shape, F32)
    vm = pl.BlockSpec(memory_space=pltpu.VMEM)
    return pl.pallas_call(body, name=name, out_shape=(shp, shp, shp), in_specs=[vm] * 4, out_specs=(vm, vm, vm))(
        w, g, m, v)


def _pack(arrays):
    rows = []
    for a in arrays:
        flat = a.reshape(-1)
        pad = (-flat.shape[0]) % LANES
        if pad:
            flat = jnp.concatenate([flat, jnp.zeros((pad,), flat.dtype)])
        rows.append(flat.reshape(-1, LANES))
    out = jnp.concatenate(rows, axis=0)
    pad = (-out.shape[0]) % SUBLANES
    if pad:
        out = jnp.concatenate([out, jnp.zeros((pad, LANES), out.dtype)], axis=0)
    return out


def _unpack(pack, shapes):
    outs, row = [], 0
    for shp in shapes:
        n = int(np.prod(shp))
        nrows = -(-n // LANES)
        outs.append(pack[row:row + nrows].reshape(-1)[:n].reshape(shp))
        row += nrows
    return outs


SMALL = ["norm_w", "ssd_conv_b", "ssd_dt_bias", "ssd_a_log", "ssd_d", "ssd_norm_w", "attn_sinks",
         "conf_dw_b", "conf_ln_w", "conf_ln_b"]
WEIGHTS = ["norm_w", "w_in", "ssd_conv_w", "ssd_conv_b", "ssd_dt_bias", "ssd_a_log", "ssd_d", "ssd_norm_w",
           "attn_sinks", "conf_dw_w", "conf_dw_b", "conf_ln_w", "conf_ln_b", "w_out", "final_norm_w"]


def kernel(x, norm_w, w_in, ssd_conv_w, ssd_conv_b, ssd_dt_bias, ssd_a_log, ssd_d, ssd_norm_w, attn_sinks, conf_dw_w, conf_dw_b, conf_ln_w, conf_ln_b, w_out, final_norm_w, loss_target, m_norm_w, m_w_in, m_ssd_conv_w, m_ssd_conv_b, m_ssd_dt_bias, m_ssd_a_log, m_ssd_d, m_ssd_norm_w, m_attn_sinks, m_conf_dw_w, m_conf_dw_b, m_conf_ln_w, m_conf_ln_b, m_w_out, m_final_norm_w, v_norm_w, v_w_in, v_ssd_conv_w, v_ssd_conv_b, v_ssd_dt_bias, v_ssd_a_log, v_ssd_d, v_ssd_norm_w, v_attn_sinks, v_conf_dw_w, v_conf_dw_b, v_conf_ln_w, v_conf_ln_b, v_w_out, v_final_norm_w):
    w = dict(norm_w=norm_w, w_in=w_in, ssd_conv_w=ssd_conv_w, ssd_conv_b=ssd_conv_b, ssd_dt_bias=ssd_dt_bias,
             ssd_a_log=ssd_a_log, ssd_d=ssd_d, ssd_norm_w=ssd_norm_w, attn_sinks=attn_sinks, conf_dw_w=conf_dw_w,
             conf_dw_b=conf_dw_b, conf_ln_w=conf_ln_w, conf_ln_b=conf_ln_b, w_out=w_out, final_norm_w=final_norm_w)
    m = dict(norm_w=m_norm_w, w_in=m_w_in, ssd_conv_w=m_ssd_conv_w, ssd_conv_b=m_ssd_conv_b,
             ssd_dt_bias=m_ssd_dt_bias, ssd_a_log=m_ssd_a_log, ssd_d=m_ssd_d, ssd_norm_w=m_ssd_norm_w,
             attn_sinks=m_attn_sinks, conf_dw_w=m_conf_dw_w, conf_dw_b=m_conf_dw_b, conf_ln_w=m_conf_ln_w,
             conf_ln_b=m_conf_ln_b, w_out=m_w_out, final_norm_w=m_final_norm_w)
    v = dict(norm_w=v_norm_w, w_in=v_w_in, ssd_conv_w=v_ssd_conv_w, ssd_conv_b=v_ssd_conv_b,
             ssd_dt_bias=v_ssd_dt_bias, ssd_a_log=v_ssd_a_log, ssd_d=v_ssd_d, ssd_norm_w=v_ssd_norm_w,
             attn_sinks=v_attn_sinks, conf_dw_w=v_conf_dw_w, conf_dw_b=v_conf_dw_b, conf_ln_w=v_conf_ln_w,
             conf_ln_b=v_conf_ln_b, w_out=v_w_out, final_norm_w=v_final_norm_w)
    depth = w_in.shape[0]
    me = 2 * lax.axis_index("x") + lax.axis_index("y")

    g_in, g_out, g_conv, g_dw = _gather_weights(
        [_cast_halves(w_in, name="cast_w_in"), _cast_halves(w_out, name="cast_w_out")],
        [ssd_conv_w, conf_dw_w], name="gather_weights")
    layer_params = []
    for li in range(depth):
        w_in_full = jnp.concatenate(
            [jnp.concatenate([g_in[p, 0, li], g_in[p, 1, li]], axis=0) for p in range(N_CHIPS)], axis=1)
        w_out_full = jnp.concatenate(
            [jnp.concatenate([g_out[p, 0, li], g_out[p, 1, li]], axis=0) for p in range(N_CHIPS)], axis=0)
        conv_full = jnp.concatenate([g_conv[p, li] for p in range(N_CHIPS)], axis=1)
        dw_full = jnp.concatenate([g_dw[p, li] for p in range(N_CHIPS)], axis=1)
        layer_params.append(_layer_params(li, _to_padded_cols(w_in_full), w_out_full, conv_full, dw_full, w))

    loss, grad_x, grads, dfinal = _local_step(x, loss_target, layer_params, final_norm_w)

    small_list = [grads[li][n] for li in range(depth) for n in SMALL]
    small_list += [grads[li][n] for li in range(depth) for n in ("ssd_conv_w", "conf_dw_w")]
    small_list += [dfinal, loss.reshape(1)]
    small_shapes = [a.shape for a in small_list]
    reduced = _unpack(_allreduce_small(_pack(small_list), name="allreduce_small"), small_shapes)
    ns = len(SMALL)
    g = {n: jnp.stack([reduced[li * ns + i] for li in range(depth)]) for i, n in enumerate(SMALL)}
    conv_w_cols, dw_w_cols = ssd_conv_w.shape[2], conf_dw_w.shape[2]
    g["ssd_conv_w"] = jnp.stack([lax.dynamic_slice_in_dim(reduced[depth * ns + 2 * li], me * conv_w_cols,
                                                          conv_w_cols, axis=1) for li in range(depth)])
    g["conf_dw_w"] = jnp.stack([lax.dynamic_slice_in_dim(reduced[depth * ns + 2 * li + 1], me * dw_w_cols,
                                                         dw_w_cols, axis=1) for li in range(depth)])
    g["final_norm_w"] = reduced[-2]
    loss_total = reduced[-1][0]

    cols = w_in.shape[2]
    rows_out = w_out.shape[1]
    p_in = jnp.stack([_from_padded_cols(grads[li]["w_in_p"]).reshape(2, D_MODEL // 2, N_CHIPS, cols)
                      for li in range(depth)]).transpose(1, 3, 0, 2, 4)
    p_out = jnp.stack([grads[li]["w_out"].reshape(N_CHIPS, 2, rows_out // 2, D_MODEL)
                       for li in range(depth)]).transpose(2, 1, 0, 3, 4)
    c = lax.axis_index("c")
    sib_in, sib_out = _pair_swap_halves([p_in, p_out], name="grad_pair_swap")
    mine_in = lax.dynamic_index_in_dim(p_in, c, axis=0, keepdims=False)
    mine_out = lax.dynamic_index_in_dim(p_out, c, axis=0, keepdims=False)
    s_in = _sum_lead([mine_in, sib_in], name="grad_pair_sum_in")
    s_out = _sum_lead([mine_out, sib_out], name="grad_pair_sum_out")
    r_in, r_out = _chip_scatter([s_in, s_out], name="grad_chip_scatter")
    t_in = _sum_lead(r_in, name="grad_chip_sum_in")
    t_out = _sum_lead(r_out, name="grad_chip_sum_out")
    gh_in, gh_out = _pair_gather([t_in, t_out], name="grad_pair_gather")

    outs_g, outs_d, outs_m, outs_v = {}, {}, {}, {}
    outs_g["w_in"], outs_d["w_in"], outs_m["w_in"], outs_v["w_in"] = _adam_big(
        w_in, gh_in, m_w_in, v_w_in, name="adam_w_in")
    outs_g["w_out"], outs_d["w_out"], outs_m["w_out"], outs_v["w_out"] = _adam_big(
        w_out, gh_out, m_w_out, v_w_out, name="adam_w_out")
    small_names = [n for n in WEIGHTS if n not in ("w_in", "w_out")]
    d_p, m_p, v_p = _adam_small(_pack([w[n] for n in small_names]), _pack([g[n] for n in small_names]),
                                _pack([m[n] for n in small_names]), _pack([v[n] for n in small_names]),
                                name="adam_small")
    shapes = [w[n].shape for n in small_names]
    for n, dn, mn, vn in zip(small_names, _unpack(d_p, shapes), _unpack(m_p, shapes), _unpack(v_p, shapes)):
        outs_g[n], outs_d[n], outs_m[n], outs_v[n] = g[n], dn, mn, vn
    return (loss_total, grad_x, *[outs_g[n] for n in WEIGHTS], *[outs_d[n] for n in WEIGHTS],
            *[outs_m[n] for n in WEIGHTS], *[outs_v[n] for n in WEIGHTS])
```

```python
import functools
import math

import jax
import jax.numpy as jnp
import numpy as np
from jax import lax
from jax.experimental import pallas as pl
from jax.experimental.pallas import tpu as pltpu

F32 = jnp.float32
BF16 = jnp.bfloat16
MXU_DTYPE = BF16

D_MODEL = 1024
DEPTH = 2
SSD_HEADS = 16
SSD_HEAD_DIM = 64
SSD_STATE = 128
SSD_CONV = 4
CHUNK = 128
SSD_CONV_DIM = 1536
ATTN_HEAD_DIM = 64
ATTN_Q_HEADS = 8
WINDOW = 128
CONF_WIDTH = 512
CONF_KERNEL = 31
MIX_WIDTH = 2048
D_IN_PROJ = 5392
EPS = 1e-5

ADAM_LR = 0.001
ADAM_B1 = 0.9
ADAM_B2 = 0.999
ADAM_EPS = 1e-08
ADAM_WD = 0.01
ADAM_STEP = 10

LANES = 128
SUBLANES = 8
VMEM_LIMIT = 48 * 1024 * 1024

NP = 5632
OFF_Z, OFF_XBC, OFF_Q, OFF_K, OFF_V, OFF_DT, OFF_CONF = 0, 2048, 3584, 4096, 4224, 4352, 4608
REF_OFF_XBC, REF_OFF_DT, REF_OFF_Q, REF_OFF_K, REF_OFF_V, REF_OFF_CONF = 2048, 3584, 3600, 4112, 4240, 4368

NN = (((1,), (0,)), ((), ()))
NT = (((1,), (1,)), ((), ()))
TN = (((0,), (0,)), ((), ()))


def _params(sem):
    return pltpu.CompilerParams(dimension_semantics=sem, vmem_limit_bytes=VMEM_LIMIT)


def _dot(a, b, dims=NN):
    return lax.dot_general(a.astype(MXU_DTYPE), b.astype(MXU_DTYPE), dims, preferred_element_type=F32)


def _split_bf16(a, passes):
    pieces = []
    r = a
    for _ in range(passes):
        p = r.astype(BF16)
        pieces.append(p)
        r = r - p.astype(F32)
    return pieces


def _xdot(a, sel, dims=NN, passes=3):
    out = None
    for p in _split_bf16(a, passes):
        t = lax.dot_general(p, sel, dims, preferred_element_type=F32)
        out = t if out is None else out + t
    return out


def _xdot_r(sel, b, dims=NN, passes=3):
    out = None
    for p in _split_bf16(b, passes):
        t = lax.dot_general(sel, p, dims, preferred_element_type=F32)
        out = t if out is None else out + t
    return out


def _sigmoid(x):
    return 1.0 / (1.0 + jnp.exp(-x))


def _silu(x):
    return x * _sigmoid(x)


def _dsilu(x):
    s = _sigmoid(x)
    return s * (1.0 + x * (1.0 - s))


def _softplus(x):
    return jnp.maximum(x, 0.0) + jnp.log(1.0 + jnp.exp(-jnp.abs(x)))


def _rowsum8(x):
    r, c = x.shape
    return jnp.sum(x.reshape(r // SUBLANES, SUBLANES, c), axis=0)


def _iota(shape, dim):
    return lax.broadcasted_iota(jnp.int32, shape, dim)


def _matmul(a, b, form, out_dtype, tm, tn, tk, name, residual=None):
    if form == "nn":
        (m, k), n = a.shape, b.shape[1]
    elif form == "nt":
        (m, k), n = a.shape, b.shape[0]
    else:
        (k, m), n = a.shape, b.shape[1]
    tm, tn, tk = min(tm, m), min(tn, n), min(tk, k)
    assert m % tm == 0 and n % tn == 0 and k % tk == 0, (name, m, n, k, tm, tn, tk)
    if form == "nn":
        a_spec = pl.BlockSpec((tm, tk), lambda i, j, s: (i, s))
        b_spec = pl.BlockSpec((tk, tn), lambda i, j, s: (s, j))
        dims = NN
    elif form == "nt":
        (m, k), n = a.shape, b.shape[0]
        a_spec = pl.BlockSpec((tm, tk), lambda i, j, s: (i, s))
        b_spec = pl.BlockSpec((tn, tk), lambda i, j, s: (j, s))
        dims = NT
    else:
        (k, m), n = a.shape, b.shape[1]
        a_spec = pl.BlockSpec((tk, tm), lambda i, j, s: (s, i))
        b_spec = pl.BlockSpec((tk, tn), lambda i, j, s: (s, j))
        dims = TN
    nk = k // tk
    has_res = residual is not None

    def body_single(a_ref, b_ref, *rest):
        o = _dot(a_ref[...], b_ref[...], dims)
        if has_res:
            o = o + rest[0][...]
        rest[-1][...] = o.astype(out_dtype)

    def body(a_ref, b_ref, *rest):
        if has_res:
            r_ref, o_ref, acc = rest
        else:
            o_ref, acc = rest
        s = pl.program_id(2)

        @pl.when(s == 0)
        def _():
            acc[...] = jnp.zeros_like(acc)

        acc[...] += _dot(a_ref[...], b_ref[...], dims)

        @pl.when(s == nk - 1)
        def _():
            o = acc[...]
            if has_res:
                o = o + r_ref[...]
            o_ref[...] = o.astype(out_dtype)

    in_specs = [a_spec, b_spec]
    args = [a, b]
    if has_res:
        in_specs.append(pl.BlockSpec((tm, tn), lambda i, j, s: (i, j)))
        args.append(residual)
    return pl.pallas_call(
        body_single if nk == 1 else body, name=name,
        out_shape=jax.ShapeDtypeStruct((m, n), out_dtype),
        grid=(m // tm, n // tn, nk),
        in_specs=in_specs,
        out_specs=pl.BlockSpec((tm, tn), lambda i, j, s: (i, j)),
        scratch_shapes=[] if nk == 1 else [pltpu.VMEM((tm, tn), F32)],
        compiler_params=_params(("parallel", "parallel", "arbitrary")),
    )(*args)


ROW_TILE = 256


def _rmsnorm_fwd(x, w, name):
    t, d = x.shape
    tm = ROW_TILE

    def body(x_ref, w_ref, o_ref, ot_ref):
        xv = x_ref[...]
        rstd = lax.rsqrt(jnp.mean(xv * xv, axis=-1, keepdims=True) + EPS)
        h = xv * rstd * w_ref[...]
        o_ref[...] = h.astype(o_ref.dtype)
        ot_ref[...] = h.T.astype(ot_ref.dtype)

    return pl.pallas_call(
        body, name=name,
        out_shape=(jax.ShapeDtypeStruct((t, d), MXU_DTYPE), jax.ShapeDtypeStruct((d, t), MXU_DTYPE)),
        grid=(t // tm,),
        in_specs=[pl.BlockSpec((tm, d), lambda i: (i, 0)), pl.BlockSpec((1, d), lambda i: (0, 0))],
        out_specs=(pl.BlockSpec((tm, d), lambda i: (i, 0)), pl.BlockSpec((d, tm), lambda i: (0, i))),
        compiler_params=_params(("parallel",)),
    )(x, w)


def _rmsnorm_bwd(dh, x, w, dres, name):
    t, d = x.shape
    tm = ROW_TILE
    nt = t // tm

    def body(dh_ref, x_ref, w_ref, dr_ref, dx_ref, dw_ref, acc):
        i = pl.program_id(0)

        @pl.when(i == 0)
        def _():
            acc[...] = jnp.zeros_like(acc)

        xv = x_ref[...]
        rstd = lax.rsqrt(jnp.mean(xv * xv, axis=-1, keepdims=True) + EPS)
        xh = xv * rstd
        dhv = dh_ref[...]
        g = dhv * w_ref[...]
        dx_ref[...] = dr_ref[...] + rstd * (g - xh * jnp.mean(g * xh, axis=-1, keepdims=True))
        acc[...] += _rowsum8(dhv * xh)

        @pl.when(i == nt - 1)
        def _():
            dw_ref[...] = jnp.sum(acc[...], axis=0, keepdims=True)

    row = pl.BlockSpec((tm, d), lambda i: (i, 0))
    vec = pl.BlockSpec((1, d), lambda i: (0, 0))
    return pl.pallas_call(
        body, name=name,
        out_shape=(jax.ShapeDtypeStruct((t, d), F32), jax.ShapeDtypeStruct((1, d), F32)),
        grid=(nt,),
        in_specs=[row, row, vec, row],
        out_specs=(row, vec),
        scratch_shapes=[pltpu.VMEM((SUBLANES, d), F32)],
        compiler_params=_params(("arbitrary",)),
    )(dh, x, w, dres)


def _loss_head(xf, target, w, name):
    t, d = xf.shape
    tm = ROW_TILE
    nt = t // tm

    def body(x_ref, t_ref, w_ref, loss_ref, dx_ref, dw_ref, lacc, wacc):
        i = pl.program_id(0)

        @pl.when(i == 0)
        def _():
            lacc[...] = jnp.zeros_like(lacc)
            wacc[...] = jnp.zeros_like(wacc)

        xv = x_ref[...]
        rstd = lax.rsqrt(jnp.mean(xv * xv, axis=-1, keepdims=True) + EPS)
        xh = xv * rstd
        err = xh * w_ref[...] - t_ref[...]
        lacc[...] += jnp.sum(err * err)
        dy = err * (1.0 / d)
        g = dy * w_ref[...]
        dx_ref[...] = rstd * (g - xh * jnp.mean(g * xh, axis=-1, keepdims=True))
        wacc[...] += _rowsum8(dy * xh)

        @pl.when(i == nt - 1)
        def _():
            loss_ref[...] = lacc[...] * (0.5 / d)
            dw_ref[...] = jnp.sum(wacc[...], axis=0, keepdims=True)

    row = pl.BlockSpec((tm, d), lambda i: (i, 0))
    vec = pl.BlockSpec((1, d), lambda i: (0, 0))
    return pl.pallas_call(
        body, name=name,
        out_shape=(jax.ShapeDtypeStruct((SUBLANES, LANES), F32), jax.ShapeDtypeStruct((t, d), F32),
                   jax.ShapeDtypeStruct((1, d), F32)),
        grid=(nt,),
        in_specs=[row, row, vec],
        out_specs=(pl.BlockSpec((SUBLANES, LANES), lambda i: (0, 0)), row, vec),
        scratch_shapes=[pltpu.VMEM((SUBLANES, LANES), F32), pltpu.VMEM((SUBLANES, d), F32)],
        compiler_params=_params(("arbitrary",)),
    )(xf, target, w)


def _glu_fwd(proj, name):
    t = proj.shape[0]
    tm, cw = ROW_TILE, CONF_WIDTH

    def body(a_ref, g_ref, o_ref):
        o_ref[...] = a_ref[...] * _sigmoid(g_ref[...])

    return pl.pallas_call(
        body, name=name,
        out_shape=jax.ShapeDtypeStruct((t, cw), F32),
        grid=(t // tm,),
        in_specs=[pl.BlockSpec((tm, cw), lambda i: (i, OFF_CONF // cw)),
                  pl.BlockSpec((tm, cw), lambda i: (i, OFF_CONF // cw + 1))],
        out_specs=pl.BlockSpec((tm, cw), lambda i: (i, 0)),
        compiler_params=_params(("parallel",)),
    )(proj, proj)


def _glu_bwd(dc0, proj, name):
    t = proj.shape[0]
    tm, cw = ROW_TILE, CONF_WIDTH

    def body(d_ref, a_ref, g_ref, o_ref):
        s = _sigmoid(g_ref[...])
        dv = d_ref[...]
        o_ref[:, :cw] = dv * s
        o_ref[:, cw:] = dv * a_ref[...] * s * (1.0 - s)

    return pl.pallas_call(
        body, name=name,
        out_shape=jax.ShapeDtypeStruct((t, 2 * cw), F32),
        grid=(t // tm,),
        in_specs=[pl.BlockSpec((tm, cw), lambda i: (i, 0)),
                  pl.BlockSpec((tm, cw), lambda i: (i, OFF_CONF // cw)),
                  pl.BlockSpec((tm, cw), lambda i: (i, OFF_CONF // cw + 1))],
        out_specs=pl.BlockSpec((tm, 2 * cw), lambda i: (i, 0)),
        compiler_params=_params(("parallel",)),
    )(dc0, proj, proj)


def _conf_post_fwd(c1, proj, ln_w, ln_b, name):
    t = c1.shape[0]
    tm, cw = ROW_TILE, CONF_WIDTH

    def body(c_ref, z_ref, w_ref, b_ref, o_ref):
        cv = c_ref[...]
        xc = cv - jnp.mean(cv, axis=-1, keepdims=True)
        rstd = lax.rsqrt(jnp.mean(xc * xc, axis=-1, keepdims=True) + EPS)
        c2 = xc * rstd * w_ref[...] + b_ref[...]
        o_ref[...] = (_silu(c2) * _silu(z_ref[...])).astype(o_ref.dtype)

    vec = pl.BlockSpec((1, cw), lambda i: (0, 0))
    return pl.pallas_call(
        body, name=name,
        out_shape=jax.ShapeDtypeStruct((t, cw), MXU_DTYPE),
        grid=(t // tm,),
        in_specs=[pl.BlockSpec((tm, cw), lambda i: (i, 0)),
                  pl.BlockSpec((tm, cw), lambda i: (i, (OFF_Z + 1536) // cw)), vec, vec],
        out_specs=pl.BlockSpec((tm, cw), lambda i: (i, 0)),
        compiler_params=_params(("parallel",)),
    )(c1, proj, ln_w, ln_b)


def _conf_post_bwd(dycat, c1, proj, ln_w, ln_b, name):
    t = c1.shape[0]
    tm, cw = ROW_TILE, CONF_WIDTH
    nt = t // tm

    def body(dy_ref, c_ref, z_ref, w_ref, b_ref, dc_ref, dz_ref, dw_ref, db_ref, wacc, bacc):
        i = pl.program_id(0)

        @pl.when(i == 0)
        def _():
            wacc[...] = jnp.zeros_like(wacc)
            bacc[...] = jnp.zeros_like(bacc)

        cv = c_ref[...]
        xc = cv - jnp.mean(cv, axis=-1, keepdims=True)
        rstd = lax.rsqrt(jnp.mean(xc * xc, axis=-1, keepdims=True) + EPS)
        xh = xc * rstd
        c2 = xh * w_ref[...] + b_ref[...]
        zv = z_ref[...]
        dy = dy_ref[...]
        dz_ref[...] = dy * _silu(c2) * _dsilu(zv)
        dc2 = dy * _silu(zv) * _dsilu(c2)
        bacc[...] += _rowsum8(dc2)
        wacc[...] += _rowsum8(dc2 * xh)
        dxh = dc2 * w_ref[...]
        dc_ref[...] = rstd * (dxh - jnp.mean(dxh, axis=-1, keepdims=True)
                              - xh * jnp.mean(dxh * xh, axis=-1, keepdims=True))

        @pl.when(i == nt - 1)
        def _():
            dw_ref[...] = jnp.sum(wacc[...], axis=0, keepdims=True)
            db_ref[...] = jnp.sum(bacc[...], axis=0, keepdims=True)

    row = pl.BlockSpec((tm, cw), lambda i: (i, 0))
    vec = pl.BlockSpec((1, cw), lambda i: (0, 0))
    return pl.pallas_call(
        body, name=name,
        out_shape=(jax.ShapeDtypeStruct((t, cw), F32), jax.ShapeDtypeStruct((t, cw), F32),
                   jax.ShapeDtypeStruct((1, cw), F32), jax.ShapeDtypeStruct((1, cw), F32)),
        grid=(nt,),
        in_specs=[pl.BlockSpec((tm, cw), lambda i: (i, 1536 // cw)), row,
                  pl.BlockSpec((tm, cw), lambda i: (i, (OFF_Z + 1536) // cw)), vec, vec],
        out_specs=(row, row, vec, vec),
        scratch_shapes=[pltpu.VMEM((SUBLANES, cw), F32), pltpu.VMEM((SUBLANES, cw), F32)],
        compiler_params=_params(("arbitrary",)),
    )(dycat, c1, proj, ln_w, ln_b)


GN_WIDTH = 512


def _gated_norm_fwd(y, proj, w, name):
    t = y.shape[0]
    tm, cw = ROW_TILE, GN_WIDTH

    def body(y_ref, z_ref, w_ref, o_ref):
        g = y_ref[...] * _silu(z_ref[...])
        rstd = lax.rsqrt(jnp.mean(g * g, axis=-1, keepdims=True) + EPS)
        o_ref[...] = (g * rstd * w_ref[...]).astype(o_ref.dtype)

    blk = pl.BlockSpec((tm, cw), lambda i, j: (i, j))
    return pl.pallas_call(
        body, name=name,
        out_shape=jax.ShapeDtypeStruct(y.shape, MXU_DTYPE),
        grid=(t // tm, y.shape[1] // cw),
        in_specs=[blk, blk, pl.BlockSpec((1, cw), lambda i, j: (0, j))],
        out_specs=blk,
        compiler_params=_params(("parallel", "parallel")),
    )(y, proj, w)


def _gated_norm_bwd(dycat, y, proj, w, name):
    t = y.shape[0]
    tm, cw = ROW_TILE, GN_WIDTH
    nt = t // tm

    def body(do_ref, y_ref, z_ref, w_ref, dy_ref, dz_ref, dw_ref, acc):
        i = pl.program_id(1)

        @pl.when(i == 0)
        def _():
            acc[...] = jnp.zeros_like(acc)

        yv, zv, dov = y_ref[...], z_ref[...], do_ref[...]
        sz = _silu(zv)
        g = yv * sz
        rstd = lax.rsqrt(jnp.mean(g * g, axis=-1, keepdims=True) + EPS)
        gh = g * rstd
        acc[...] += _rowsum8(dov * gh)
        dgn = dov * w_ref[...]
        dg = rstd * (dgn - gh * jnp.mean(dgn * gh, axis=-1, keepdims=True))
        dy_ref[...] = dg * sz
        dz_ref[...] = dg * yv * _dsilu(zv)

        @pl.when(i == nt - 1)
        def _():
            dw_ref[...] = jnp.sum(acc[...], axis=0, keepdims=True)

    blk = pl.BlockSpec((tm, cw), lambda j, i: (i, j))
    vec = pl.BlockSpec((1, cw), lambda j, i: (0, j))
    return pl.pallas_call(
        body, name=name,
        out_shape=(jax.ShapeDtypeStruct(y.shape, F32), jax.ShapeDtypeStruct(y.shape, F32),
                   jax.ShapeDtypeStruct((1, y.shape[1]), F32)),
        grid=(y.shape[1] // cw, nt),
        in_specs=[blk, blk, blk, vec],
        out_specs=(blk, blk, vec),
        scratch_shapes=[pltpu.VMEM((SUBLANES, cw), F32)],
        compiler_params=_params(("parallel", "arbitrary")),
    )(dycat, y, proj, w)


CONV_TILE = 512
CONV_COLS = 512
CONV_SUB_ROWS = 256
CONV_SUB_COLS = LANES


def _conv_halo(k):
    return SUBLANES if k - 1 <= SUBLANES else 32


def _conv_subtiles(tm, cw):
    return [(r0, c0) for r0 in range(0, tm, CONV_SUB_ROWS) for c0 in range(0, cw, CONV_SUB_COLS)]


def _conv_fwd(src, col0, width, w, bias, k, seq, name):
    t = src.shape[0]
    tm, cw, halo = CONV_TILE, CONV_COLS, _conv_halo(k)
    sr, sc = CONV_SUB_ROWS, CONV_SUB_COLS
    p = k - 1
    cb0 = col0 // cw
    kp = w.shape[0]

    def body(x_ref, h_ref, w_ref, b_ref, o_ref, ext):
        i = pl.program_id(0)
        seq_start = (i * tm) % seq == 0
        ext[halo:, :] = x_ref[...]
        ext[:halo, :] = jnp.where(seq_start, 0.0, h_ref[...])
        for r0, c0 in _conv_subtiles(tm, cw):
            cs = slice(c0, c0 + sc)
            acc = jnp.zeros((sr, sc), F32) + b_ref[:, cs]
            for j in range(k):
                acc = acc + w_ref[j:j + 1, cs] * ext[r0 + halo - p + j:r0 + halo - p + j + sr, cs]
            o_ref[r0:r0 + sr, cs] = acc

    return pl.pallas_call(
        body, name=name,
        out_shape=jax.ShapeDtypeStruct((t, width), F32),
        grid=(t // tm, width // cw),
        in_specs=[pl.BlockSpec((tm, cw), lambda i, j: (i, cb0 + j)),
                  pl.BlockSpec((halo, cw), lambda i, j: (jnp.maximum(i * (tm // halo) - 1, 0), cb0 + j)),
                  pl.BlockSpec((kp, cw), lambda i, j: (0, j)),
                  pl.BlockSpec((1, cw), lambda i, j: (0, j))],
        out_specs=pl.BlockSpec((tm, cw), lambda i, j: (i, j)),
        scratch_shapes=[pltpu.VMEM((halo + tm, cw), F32)],
        compiler_params=_params(("parallel", "parallel")),
    )(src, src, w, bias)


def _conv_bwd(dy, src, col0, width, w, k, seq, name):
    t = src.shape[0]
    tm, cw, halo = CONV_TILE, CONV_COLS, _conv_halo(k)
    sr, sc = CONV_SUB_ROWS, CONV_SUB_COLS
    p = k - 1
    cb0 = col0 // cw
    kp = w.shape[0]
    nt = t // tm
    last_halo = t // halo - 1

    def body(dy_ref, dn_ref, x_ref, xp_ref, w_ref, dx_ref, dw_ref, db_ref, dyext, xext, wacc, bacc):
        i = pl.program_id(1)

        @pl.when(i == 0)
        def _():
            wacc[...] = jnp.zeros_like(wacc)
            bacc[...] = jnp.zeros_like(bacc)

        seq_start = (i * tm) % seq == 0
        seq_end = ((i + 1) * tm) % seq == 0
        dyext[:tm, :] = dy_ref[...]
        dyext[tm:, :] = jnp.where(seq_end, 0.0, dn_ref[...])
        xext[halo:, :] = x_ref[...]
        xext[:halo, :] = jnp.where(seq_start, 0.0, xp_ref[...])
        for r0, c0 in _conv_subtiles(tm, cw):
            cs = slice(c0, c0 + sc)
            dyv = dy_ref[r0:r0 + sr, cs]
            acc = jnp.zeros((sr, sc), F32)
            for j in range(k):
                acc = acc + w_ref[j:j + 1, cs] * dyext[r0 + p - j:r0 + p - j + sr, cs]
                wacc[j, :, cs] += _rowsum8(dyv * xext[r0 + halo - p + j:r0 + halo - p + j + sr, cs])
            dx_ref[r0:r0 + sr, cs] = acc
            bacc[:, cs] += _rowsum8(dyv)

        @pl.when(i == nt - 1)
        def _():
            dw_ref[...] = jnp.zeros_like(dw_ref)
            for j in range(k):
                dw_ref[j:j + 1, :] = jnp.sum(wacc[j], axis=0, keepdims=True)
            db_ref[...] = jnp.sum(bacc[...], axis=0, keepdims=True)

    return pl.pallas_call(
        body, name=name,
        out_shape=(jax.ShapeDtypeStruct((t, width), F32), jax.ShapeDtypeStruct((kp, width), F32),
                   jax.ShapeDtypeStruct((1, width), F32)),
        grid=(width // cw, nt),
        in_specs=[pl.BlockSpec((tm, cw), lambda j, i: (i, j)),
                  pl.BlockSpec((halo, cw), lambda j, i: (jnp.minimum((i + 1) * (tm // halo), last_halo), j)),
                  pl.BlockSpec((tm, cw), lambda j, i: (i, cb0 + j)),
                  pl.BlockSpec((halo, cw), lambda j, i: (jnp.maximum(i * (tm // halo) - 1, 0), cb0 + j)),
                  pl.BlockSpec((kp, cw), lambda j, i: (0, j))],
        out_specs=(pl.BlockSpec((tm, cw), lambda j, i: (i, j)),
                   pl.BlockSpec((kp, cw), lambda j, i: (0, j)),
                   pl.BlockSpec((1, cw), lambda j, i: (0, j))),
        scratch_shapes=[pltpu.VMEM((tm + halo, cw), F32), pltpu.VMEM((halo + tm, cw), F32),
                        pltpu.VMEM((kp, SUBLANES, cw), F32), pltpu.VMEM((SUBLANES, cw), F32)],
        compiler_params=_params(("parallel", "arbitrary")),
    )(dy, dy, src, src, w)


def _head_dup(g):
    r, c = _iota((LANES, LANES), 0), _iota((LANES, LANES), 1)
    return (r == g * ATTN_HEAD_DIM + (c & (ATTN_HEAD_DIM - 1))).astype(BF16)


def _half_mask(half):
    lane = _iota((1, LANES), 1)
    return ((lane >= half * ATTN_HEAD_DIM) & (lane < (half + 1) * ATTN_HEAD_DIM)).astype(F32)


def _band_mask(first_block):
    w = WINDOW
    qi = _iota((w, 2 * w), 0)
    kj = _iota((w, 2 * w), 1) - w
    rel = qi - kj
    return (rel >= 0) & (rel < w) & (jnp.logical_not(first_block) | (kj >= 0))


def _lane_pick(x, h):
    return jnp.sum(jnp.where(_iota(x.shape, 1) == h, x, 0.0), axis=1, keepdims=True)


def _attn_specs(nb, rev):
    w = WINDOW

    def blk(i):
        return nb - 1 - i if rev else i

    def row(b, i):
        return b * nb + blk(i)

    def prow(b, i):
        return b * nb + jnp.maximum(blk(i) - 1, 0)

    q = pl.BlockSpec((w, 512), lambda b, i: (row(b, i), OFF_Q // 512))
    kc = pl.BlockSpec((w, 128), lambda b, i: (row(b, i), OFF_K // 128))
    kp = pl.BlockSpec((w, 128), lambda b, i: (prow(b, i), OFF_K // 128))
    vc = pl.BlockSpec((w, 128), lambda b, i: (row(b, i), OFF_V // 128))
    vp = pl.BlockSpec((w, 128), lambda b, i: (prow(b, i), OFF_V // 128))
    z = pl.BlockSpec((w, 512), lambda b, i: (row(b, i), (OFF_Z + 1024) // 512))
    return q, kc, kp, vc, vp, z, row


def _attn_fwd(proj, sinks, nbatch, name):
    t = proj.shape[0]
    w = WINDOW
    nb = t // nbatch // w
    scale = ATTN_HEAD_DIM ** -0.5
    q_s, kc_s, kp_s, vc_s, vp_s, z_s, row = _attn_specs(nb, False)

    def body(q_ref, kc_ref, kp_ref, vc_ref, vp_ref, z_ref, sk_ref, y_ref, o_ref, lse_ref):
        first = pl.program_id(1) == 0
        mask = _band_mask(first)
        kk = jnp.concatenate([kp_ref[...], kc_ref[...]], axis=0).astype(MXU_DTYPE)
        vv = jnp.concatenate([vp_ref[...], vc_ref[...]], axis=0).astype(MXU_DTYPE)
        sk = sk_ref[...]
        lse_all = jnp.zeros((w, LANES), F32)
        lane = _iota((w, LANES), 1)
        for g in range(2):
            dup = _head_dup(g)
            kkd = _dot(kk, dup).astype(MXU_DTYPE)
            vvd = _dot(vv, dup)
            for jj in range(2):
                j = 2 * g + jj
                qp = q_ref[:, j * LANES:(j + 1) * LANES]
                op = jnp.zeros((w, LANES), F32)
                for half in range(2):
                    h = 2 * j + half
                    hm = _half_mask(half)
                    s = _dot(qp * hm, kkd, NT) * scale
                    s = jnp.where(mask, s, -1e30)
                    skh = _lane_pick(sk, h)
                    m = jnp.maximum(jnp.max(s, axis=1, keepdims=True), skh)
                    den = jnp.sum(jnp.exp(s - m), axis=1, keepdims=True) + jnp.exp(skh - m)
                    lse = m + jnp.log(den)
                    pr = jnp.exp(s - lse)
                    op = op + _dot(pr, vvd * hm)
                    lse_all = jnp.where(lane == h, lse, lse_all)
                o_ref[:, j * LANES:(j + 1) * LANES] = op
                y_ref[:, j * LANES:(j + 1) * LANES] = (
                    op * _silu(z_ref[:, j * LANES:(j + 1) * LANES])).astype(y_ref.dtype)
        lse_ref[...] = lse_all

    return pl.pallas_call(
        body, name=name,
        out_shape=(jax.ShapeDtypeStruct((t, 512), MXU_DTYPE), jax.ShapeDtypeStruct((t, 512), F32),
                   jax.ShapeDtypeStruct((t, LANES), F32)),
        grid=(nbatch, nb),
        in_specs=[q_s, kc_s, kp_s, vc_s, vp_s, z_s, pl.BlockSpec((1, LANES), lambda b, i: (0, 0))],
        out_specs=(pl.BlockSpec((w, 512), lambda b, i: (row(b, i), 0)),
                   pl.BlockSpec((w, 512), lambda b, i: (row(b, i), 0)),
                   pl.BlockSpec((w, LANES), lambda b, i: (row(b, i), 0))),
        compiler_params=_params(("parallel", "parallel")),
    )(proj, proj, proj, proj, proj, proj, sinks)


def _attn_bwd(dycat, proj, o, lse, sinks, nbatch, name):
    t = proj.shape[0]
    w = WINDOW
    nb = t // nbatch // w
    scale = ATTN_HEAD_DIM ** -0.5
    q_s, kc_s, kp_s, vc_s, vp_s, z_s, row = _attn_specs(nb, True)

    def body(dy_ref, q_ref, kc_ref, kp_ref, vc_ref, vp_ref, z_ref, o_ref, lse_ref, sk_ref,
             dqkv_ref, dz_ref, dsk_ref, kcarry, vcarry, sacc):
        b, i = pl.program_id(0), pl.program_id(1)

        @pl.when((b == 0) & (i == 0))
        def _():
            sacc[...] = jnp.zeros_like(sacc)

        @pl.when(i == 0)
        def _():
            kcarry[...] = jnp.zeros_like(kcarry)
            vcarry[...] = jnp.zeros_like(vcarry)

        first = i == nb - 1
        mask = _band_mask(first)
        kk = jnp.concatenate([kp_ref[...], kc_ref[...]], axis=0).astype(MXU_DTYPE)
        vv = jnp.concatenate([vp_ref[...], vc_ref[...]], axis=0).astype(MXU_DTYPE)
        sk = sk_ref[...]
        lse_all = lse_ref[...]
        lane1 = _iota((1, LANES), 1)
        dkk = jnp.zeros((2 * w, LANES), F32)
        dvv = jnp.zeros((2 * w, LANES), F32)
        dsk = jnp.zeros((1, LANES), F32)
        for g in range(2):
            dup = _head_dup(g)
            kkd = _dot(kk, dup).astype(MXU_DTYPE)
            vvd = _dot(vv, dup).astype(MXU_DTYPE)
            dkd = jnp.zeros((2 * w, LANES), F32)
            dvd = jnp.zeros((2 * w, LANES), F32)
            for jj in range(2):
                j = 2 * g + jj
                cols = slice(j * LANES, (j + 1) * LANES)
                qp, zp, ov, dy = q_ref[:, cols], z_ref[:, cols], o_ref[:, cols], dy_ref[:, cols]
                dz_ref[:, cols] = dy * ov * _dsilu(zp)
                do = dy * _silu(zp)
                dq = jnp.zeros((w, LANES), F32)
                for half in range(2):
                    h = 2 * j + half
                    hm = _half_mask(half)
                    qh = qp * hm
                    doh = do * hm
                    delta = jnp.sum(doh * ov, axis=1, keepdims=True)
                    lse_h = _lane_pick(lse_all, h)
                    s = _dot(qh, kkd, NT) * scale
                    s = jnp.where(mask, s, -1e30)
                    pr = jnp.exp(s - lse_h)
                    dp = _dot(doh, vvd, NT)
                    ds = pr * (dp - delta)
                    dq = dq + _dot(ds, kkd) * hm * scale
                    dkd = dkd + _dot(ds, qh, TN) * scale
                    dvd = dvd + _dot(pr, doh, TN)
                    psink = jnp.exp(_lane_pick(sk, h) - lse_h)
                    dsk = dsk - jnp.where(lane1 == h, jnp.sum(psink * delta), 0.0)
                dqkv_ref[:, cols] = dq
            dkk = dkk + _xdot(dkd, dup, NT, passes=2)
            dvv = dvv + _xdot(dvd, dup, NT, passes=2)
        dqkv_ref[:, 512:640] = dkk[w:, :] + kcarry[...]
        dqkv_ref[:, 640:768] = dvv[w:, :] + vcarry[...]
        kcarry[...] = dkk[:w, :]
        vcarry[...] = dvv[:w, :]
        sacc[...] += dsk

        @pl.when((b == nbatch - 1) & (i == nb - 1))
        def _():
            dsk_ref[...] = sacc[...]

    return pl.pallas_call(
        body, name=name,
        out_shape=(jax.ShapeDtypeStruct((t, 768), F32), jax.ShapeDtypeStruct((t, 512), F32),
                   jax.ShapeDtypeStruct((1, LANES), F32)),
        grid=(nbatch, nb),
        in_specs=[pl.BlockSpec((w, 512), lambda b, i: (row(b, i), 1024 // 512)),
                  q_s, kc_s, kp_s, vc_s, vp_s, z_s,
                  pl.BlockSpec((w, 512), lambda b, i: (row(b, i), 0)),
                  pl.BlockSpec((w, LANES), lambda b, i: (row(b, i), 0)),
                  pl.BlockSpec((1, LANES), lambda b, i: (0, 0))],
        out_specs=(pl.BlockSpec((w, 768), lambda b, i: (row(b, i), 0)),
                   pl.BlockSpec((w, 512), lambda b, i: (row(b, i), 0)),
                   pl.BlockSpec((1, LANES), lambda b, i: (0, 0))),
        scratch_shapes=[pltpu.VMEM((w, LANES), F32), pltpu.VMEM((w, LANES), F32),
                        pltpu.VMEM((1, LANES), F32)],
        compiler_params=_params(("arbitrary", "arbitrary")),
    )(dycat, proj, proj, proj, proj, proj, proj, o, lse, sinks)


SSD_WIDTH = SSD_HEADS * SSD_HEAD_DIM
GROUP_ROWS = SSD_WIDTH // 2


def _expand_mat():
    r, c = _iota((LANES, SSD_WIDTH), 0), _iota((LANES, SSD_WIDTH), 1)
    return (r == lax.shift_right_logical(c, 6)).astype(BF16)


def _expand_mat_t():
    r, c = _iota((SSD_WIDTH, LANES), 0), _iota((SSD_WIDTH, LANES), 1)
    return (c == lax.shift_right_logical(r, 6)).astype(BF16)


def _ssd_common(u_ref, dt_ref, dtb_ref, a_ref):
    q = CHUNK
    act = _silu(u_ref[...])
    xs = act[:, :SSD_WIDTH]
    bm = act[:, SSD_WIDTH:SSD_WIDTH + 256]
    cm = act[:, SSD_WIDTH + 256:]
    dtp = _softplus(dt_ref[...] + dtb_ref[...])
    a = dtp * a_ref[...]
    tril = (_iota((q, q), 0) >= _iota((q, q), 1)).astype(BF16)
    acs = _xdot_r(tril, a)
    acs_t = acs.T
    e = _expand_mat()
    dt_x = _xdot(dtp, e)
    ea = jnp.exp(_xdot(acs, e))
    a_end = jnp.sum(jnp.where(_iota(acs.shape, 0) == q - 1, acs, 0.0), axis=0, keepdims=True)
    dec = jnp.exp(_xdot(a_end - acs, e))
    a_end_col = jnp.broadcast_to(_lane_pick(acs_t, q - 1), (LANES, LANES))
    s_scale = jnp.exp(_xdot_r(_expand_mat_t(), a_end_col))
    return act, xs, bm, cm, dtp, acs, acs_t, dt_x, ea, dec, s_scale, tril


def _decay_mat(acs, acs_t, h):
    q = CHUNK
    col = _lane_pick(acs, h)
    rowv = jnp.sum(jnp.where(_iota(acs_t.shape, 0) == h, acs_t, 0.0), axis=0, keepdims=True)
    causal = _iota((q, q), 0) >= _iota((q, q), 1)
    return jnp.exp(jnp.where(causal, col - rowv, -1e30))


def _ssd_fwd(u, proj, dtb, a_neg, d_x, nbatch, name):
    t = u.shape[0]
    q = CHUNK
    nc = t // nbatch // q

    def body(u_ref, dt_ref, dtb_ref, a_ref, dx_ref, y_ref, st_ref, state):
        c = pl.program_id(1)

        @pl.when(c == 0)
        def _():
            state[...] = jnp.zeros_like(state)

        st_ref[...] = state[...]
        act, xs, bm, cm, dtp, acs, acs_t, dt_x, ea, dec, s_scale, _ = _ssd_common(u_ref, dt_ref, dtb_ref, a_ref)
        xdt = xs * dt_x
        xdec = xdt * dec
        lo, hi = _half_mask(0), _half_mask(1)
        for g in range(2):
            bg = bm[:, g * LANES:(g + 1) * LANES]
            cg = cm[:, g * LANES:(g + 1) * LANES]
            rows = slice(g * GROUP_ROWS, (g + 1) * GROUP_ROWS)
            sg = state[rows, :]
            cb = _dot(cg, bg, NT)
            yoff = _dot(cg, sg, NT)
            for j in range(4):
                pj = g * 4 + j
                cols = slice(pj * LANES, (pj + 1) * LANES)
                xp = xdt[:, cols]
                m0 = cb * _decay_mat(acs, acs_t, 2 * pj)
                m1 = cb * _decay_mat(acs, acs_t, 2 * pj + 1)
                yp = _dot(m0, xp * lo) + _dot(m1, xp * hi)
                yp = yp + yoff[:, j * LANES:(j + 1) * LANES] * ea[:, cols]
                y_ref[:, cols] = yp + dx_ref[:, cols] * xs[:, cols]
            state[rows, :] = s_scale[rows, :] * sg + _dot(xdec[:, rows], bg, TN)

    vec = pl.BlockSpec((1, LANES), lambda b, c: (0, 0))
    return pl.pallas_call(
        body, name=name,
        out_shape=(jax.ShapeDtypeStruct((t, SSD_WIDTH), F32),
                   jax.ShapeDtypeStruct((nbatch * nc * SSD_WIDTH, SSD_STATE), F32)),
        grid=(nbatch, nc),
        in_specs=[pl.BlockSpec((q, SSD_CONV_DIM), lambda b, c: (b * nc + c, 0)),
                  pl.BlockSpec((q, LANES), lambda b, c: (b * nc + c, OFF_DT // LANES)),
                  vec, vec, pl.BlockSpec((1, SSD_WIDTH), lambda b, c: (0, 0))],
        out_specs=(pl.BlockSpec((q, SSD_WIDTH), lambda b, c: (b * nc + c, 0)),
                   pl.BlockSpec((SSD_WIDTH, SSD_STATE), lambda b, c: (b * nc + c, 0))),
        scratch_shapes=[pltpu.VMEM((SSD_WIDTH, SSD_STATE), F32)],
        compiler_params=_params(("parallel", "arbitrary")),
    )(u, proj, dtb, a_neg, d_x)


def _ssd_bwd(dy, u, proj, states, dtb, a_neg, d_x, nbatch, name):
    t = u.shape[0]
    q = CHUNK
    nc = t // nbatch // q

    def body(dy_ref, u_ref, dt_ref, st_ref, dtb_ref, a_ref, dx_ref,
             du_ref, ddt_ref, dal_ref, dd_ref, dtbg_ref, dstate, acc_a, acc_d, acc_b):
        b, c = pl.program_id(0), pl.program_id(1)

        @pl.when((b == 0) & (c == 0))
        def _():
            acc_a[...] = jnp.zeros_like(acc_a)
            acc_d[...] = jnp.zeros_like(acc_d)
            acc_b[...] = jnp.zeros_like(acc_b)

        @pl.when(c == 0)
        def _():
            dstate[...] = jnp.zeros_like(dstate)

        act, xs, bm, cm, dtp, acs, acs_t, dt_x, ea, dec, s_scale, tril = _ssd_common(
            u_ref, dt_ref, dtb_ref, a_ref)
        xdt = xs * dt_x
        xdec = xdt * dec
        dyv = dy_ref[...]
        dye = dyv * ea
        lo, hi = _half_mask(0), _half_mask(1)
        et = _expand_mat_t()
        dxdt_parts, db_parts, dc_parts, dxst_parts, yoff_parts = [], [], [], [], []
        end_sum = jnp.zeros((LANES, LANES), F32)
        dal_diag = jnp.zeros((q, LANES), F32)
        lane_q = _iota((q, LANES), 1)
        for g in range(2):
            bg = bm[:, g * LANES:(g + 1) * LANES]
            cg = cm[:, g * LANES:(g + 1) * LANES]
            rows = slice(g * GROUP_ROWS, (g + 1) * GROUP_ROWS)
            sg = st_ref[rows, :]
            dsg = dstate[rows, :]
            cb = _dot(cg, bg, NT)
            yoff_parts.append(_dot(cg, sg, NT))
            dcb = jnp.zeros((q, q), F32)
            parts = []
            for j in range(4):
                pj = g * 4 + j
                cols = slice(pj * LANES, (pj + 1) * LANES)
                xp = xdt[:, cols]
                dy0, dy1 = dyv[:, cols] * lo, dyv[:, cols] * hi
                l0 = _decay_mat(acs, acs_t, 2 * pj)
                l1 = _decay_mat(acs, acs_t, 2 * pj + 1)
                g0, g1 = _dot(dy0, xp, NT), _dot(dy1, xp, NT)
                m0, m1 = cb * l0, cb * l1
                dcb = dcb + g0 * l0 + g1 * l1
                parts.append(_dot(m0, dy0, TN) + _dot(m1, dy1, TN))
                for hh, wmat in enumerate((g0 * m0, g1 * m1)):
                    sel = (lane_q == 2 * pj + hh).astype(F32)
                    dal_diag = dal_diag + _dot(wmat, sel) - _dot(wmat, sel, TN)
            dxst = _dot(bg, dsg, NT) * dec[:, rows]
            dxst_parts.append(dxst)
            dxdt_parts.append(jnp.concatenate(parts, axis=1) + dxst)
            dc_parts.append(_dot(dcb, bg) + _dot(dye[:, rows], sg))
            db_parts.append(_dot(dcb, cg, TN) + _dot(xdec[:, rows], dsg))
            s_next = s_scale[rows, :] * sg + _dot(xdec[:, rows], bg, TN)
            end_sum = end_sum + _xdot(dsg * s_next, et[rows, :], TN, passes=2)
            dstate[rows, :] = _dot(dye[:, rows], cg, TN) + s_scale[rows, :] * dsg
        dxdt = jnp.concatenate(dxdt_parts, axis=1)
        dxv = dx_ref[...]
        yoff = jnp.concatenate(yoff_parts, axis=1) * ea
        dalpha = dal_diag + _xdot(dyv * yoff - xdt * jnp.concatenate(dxst_parts, axis=1), et)
        end_row = jnp.sum(end_sum, axis=0, keepdims=True)
        dalpha = dalpha + jnp.where(_iota((q, LANES), 0) == q - 1, end_row, 0.0)
        da = _xdot_r(tril, dalpha, TN)
        ddtp = da * a_ref[...] + _xdot(dxdt * xs, et)
        acc_a[...] += _rowsum8(da * dtp)
        acc_d[...] += _rowsum8(_xdot(dyv * xs, et))
        ddt_raw = ddtp * _sigmoid(dt_ref[...] + dtb_ref[...])
        acc_b[...] += _rowsum8(ddt_raw)
        ddt_ref[...] = ddt_raw
        dxs = dxdt * dt_x + dxv * dyv
        dact = jnp.concatenate([dxs] + db_parts + dc_parts, axis=1)
        du_ref[...] = dact * _dsilu(u_ref[...])

        @pl.when((b == nbatch - 1) & (c == nc - 1))
        def _():
            dal_ref[...] = jnp.sum(acc_a[...], axis=0, keepdims=True) * a_ref[...]
            dd_ref[...] = jnp.sum(acc_d[...], axis=0, keepdims=True)
            dtbg_ref[...] = jnp.sum(acc_b[...], axis=0, keepdims=True)

    def rowblk(b, c):
        return b * nc + (nc - 1 - c)

    vec = pl.BlockSpec((1, LANES), lambda b, c: (0, 0))
    wide = pl.BlockSpec((q, SSD_WIDTH), lambda b, c: (rowblk(b, c), 0))
    return pl.pallas_call(
        body, name=name,
        out_shape=(jax.ShapeDtypeStruct((t, SSD_CONV_DIM), F32), jax.ShapeDtypeStruct((t, LANES), F32),
                   jax.ShapeDtypeStruct((1, LANES), F32), jax.ShapeDtypeStruct((1, LANES), F32),
                   jax.ShapeDtypeStruct((1, LANES), F32)),
        grid=(nbatch, nc),
        in_specs=[wide,
                  pl.BlockSpec((q, SSD_CONV_DIM), lambda b, c: (rowblk(b, c), 0)),
                  pl.BlockSpec((q, LANES), lambda b, c: (rowblk(b, c), OFF_DT // LANES)),
                  pl.BlockSpec((SSD_WIDTH, SSD_STATE), lambda b, c: (rowblk(b, c), 0)),
                  vec, vec, pl.BlockSpec((1, SSD_WIDTH), lambda b, c: (0, 0))],
        out_specs=(pl.BlockSpec((q, SSD_CONV_DIM), lambda b, c: (rowblk(b, c), 0)),
                   pl.BlockSpec((q, LANES), lambda b, c: (rowblk(b, c), 0)),
                   vec, vec, vec),
        scratch_shapes=[pltpu.VMEM((SSD_WIDTH, SSD_STATE), F32), pltpu.VMEM((SUBLANES, LANES), F32),
                        pltpu.VMEM((SUBLANES, LANES), F32), pltpu.VMEM((SUBLANES, LANES), F32)],
        compiler_params=_params(("arbitrary", "arbitrary")),
    )(dy, u, proj, states, dtb, a_neg, d_x)


def _pad_rows(w, rows):
    return jnp.concatenate([w, jnp.zeros((rows - w.shape[0], w.shape[1]), w.dtype)], axis=0)


def _pad_lanes(v):
    return jnp.concatenate([v, jnp.zeros((LANES - v.shape[0],), v.dtype)]).reshape(1, LANES)


def _to_padded_cols(w):
    z = jnp.zeros(w.shape[:-1] + (OFF_CONF - OFF_DT - 16,), w.dtype)
    return jnp.concatenate([w[..., :REF_OFF_DT], w[..., REF_OFF_Q:REF_OFF_CONF], w[..., REF_OFF_DT:REF_OFF_Q], z,
                            w[..., REF_OFF_CONF:]], axis=-1)


def _from_padded_cols(w):
    return jnp.concatenate([w[..., :OFF_Q], w[..., OFF_DT:OFF_DT + 16], w[..., OFF_Q:OFF_DT], w[..., OFF_CONF:]],
                           axis=-1)


def _layer_params(li, w_in_p, w_out, conv_w, dw_w, small):
    return dict(
        w_in_p=w_in_p, w_out=w_out,
        conv_w=_pad_rows(conv_w, SUBLANES), dw_w=_pad_rows(dw_w, 32),
        norm_w=small["norm_w"][li].reshape(1, -1),
        conv_b=small["ssd_conv_b"][li].reshape(1, -1),
        dtb=_pad_lanes(small["ssd_dt_bias"][li]),
        a_neg=_pad_lanes(-jnp.exp(small["ssd_a_log"][li])),
        d_x=jnp.repeat(small["ssd_d"][li], SSD_HEAD_DIM).reshape(1, -1),
        ssd_norm_w=small["ssd_norm_w"][li].reshape(1, -1),
        sinks=_pad_lanes(small["attn_sinks"][li]),
        dw_b=small["conf_dw_b"][li].reshape(1, -1),
        ln_w=small["conf_ln_w"][li].reshape(1, -1),
        ln_b=small["conf_ln_b"][li].reshape(1, -1),
    )


def _layer_fwd(x, p, nbatch, seq, tag):
    h, h_t = _rmsnorm_fwd(x, p["norm_w"], name=f"rmsnorm_fwd_{tag}")
    proj = _matmul(h, p["w_in_p"], "nn", F32, 1024, 512, 1024, name=f"proj_fwd_{tag}")
    u = _conv_fwd(proj, OFF_XBC, SSD_CONV_DIM, p["conv_w"], p["conv_b"], SSD_CONV, seq, name=f"ssd_conv_fwd_{tag}")
    y, states = _ssd_fwd(u, proj, p["dtb"], p["a_neg"], p["d_x"], nbatch, name=f"ssd_fwd_{tag}")
    y_ssd = _gated_norm_fwd(y, proj, p["ssd_norm_w"], name=f"gated_norm_fwd_{tag}")
    y_attn, o, lse = _attn_fwd(proj, p["sinks"], nbatch, name=f"attn_fwd_{tag}")
    c0 = _glu_fwd(proj, name=f"glu_fwd_{tag}")
    c1 = _conv_fwd(c0, 0, CONF_WIDTH, p["dw_w"], p["dw_b"], CONF_KERNEL, seq, name=f"conf_conv_fwd_{tag}")
    y_conf = _conf_post_fwd(c1, proj, p["ln_w"], p["ln_b"], name=f"conf_post_fwd_{tag}")
    ycat = jnp.concatenate([y_ssd, y_attn, y_conf], axis=1)
    x_new = _matmul(ycat, p["w_out"], "nn", F32, 1024, 512, 2048, name=f"out_fwd_{tag}", residual=x)
    return x_new, dict(x=x, h_t=h_t, proj=proj, u=u, y=y, states=states, o=o, lse=lse, c0=c0, c1=c1, ycat=ycat)


def _layer_bwd(dx_out, p, s, nbatch, seq, tag):
    proj = s["proj"]
    dycat = _matmul(dx_out, p["w_out"], "nt", F32, 1024, 1024, 1024, name=f"out_bwd_dy_{tag}")
    dw_out = _matmul(s["ycat"], dx_out, "tn", F32, 1024, 1024, 1024, name=f"out_bwd_dw_{tag}")
    dc1, dz_conf, dln_w, dln_b = _conf_post_bwd(dycat, s["c1"], proj, p["ln_w"], p["ln_b"],
                                                name=f"conf_post_bwd_{tag}")
    dc0, ddw_w, ddw_b = _conv_bwd(dc1, s["c0"], 0, CONF_WIDTH, p["dw_w"], CONF_KERNEL, seq,
                                  name=f"conf_conv_bwd_{tag}")
    dconf = _glu_bwd(dc0, proj, name=f"glu_bwd_{tag}")
    dqkv, dz_attn, dsinks = _attn_bwd(dycat, proj, s["o"], s["lse"], p["sinks"], nbatch, name=f"attn_bwd_{tag}")
    dy, dz_ssd, dssd_norm_w = _gated_norm_bwd(dycat, s["y"], proj, p["ssd_norm_w"], name=f"gated_norm_bwd_{tag}")
    du, ddt, da_log, dd, ddtb = _ssd_bwd(dy, s["u"], proj, s["states"], p["dtb"], p["a_neg"], p["d_x"],
                                         nbatch, name=f"ssd_bwd_{tag}")
    dxbc, dconv_w, dconv_b = _conv_bwd(du, proj, OFF_XBC, SSD_CONV_DIM, p["conv_w"], SSD_CONV, seq,
                                       name=f"ssd_conv_bwd_{tag}")
    t = proj.shape[0]
    dproj = jnp.concatenate([dz_ssd, dz_attn, dz_conf, dxbc, dqkv, ddt, jnp.zeros((t, LANES), F32), dconf],
                            axis=1).astype(MXU_DTYPE)
    dh = _matmul(dproj, p["w_in_p"], "nt", F32, 1024, 1024, 1408, name=f"proj_bwd_dh_{tag}")
    dw_in_p = _matmul(s["h_t"], dproj, "nn", F32, 1024, 512, 4096, name=f"proj_bwd_dw_{tag}")
    dx_in, dnorm_w = _rmsnorm_bwd(dh, s["x"], p["norm_w"], dx_out, name=f"rmsnorm_bwd_{tag}")
    grads = dict(
        norm_w=dnorm_w[0], w_in_p=dw_in_p, ssd_conv_w=dconv_w[:SSD_CONV], ssd_conv_b=dconv_b[0],
        ssd_dt_bias=ddtb[0, :SSD_HEADS], ssd_a_log=da_log[0, :SSD_HEADS], ssd_d=dd[0, :SSD_HEADS],
        ssd_norm_w=dssd_norm_w[0], attn_sinks=dsinks[0, :ATTN_Q_HEADS], conf_dw_w=ddw_w[:CONF_KERNEL],
        conf_dw_b=ddw_b[0], conf_ln_w=dln_w[0], conf_ln_b=dln_b[0], w_out=dw_out)
    return dx_in, grads


def _local_step(x, target, layer_params, final_norm_w):
    nbatch, seq, d = x.shape
    xt = x.reshape(nbatch * seq, d)
    saved = []
    for li, p in enumerate(layer_params):
        xt, s = _layer_fwd(xt, p, nbatch, seq, f"l{li}")
        saved.append(s)
    loss, dx, dfinal = _loss_head(xt, target.reshape(nbatch * seq, d), final_norm_w.reshape(1, d), name="loss_head")
    grads = [None] * len(layer_params)
    for li in reversed(range(len(layer_params))):
        dx, grads[li] = _layer_bwd(dx, layer_params[li], saved[li], nbatch, seq, f"l{li}")
    return loss[0, 0], dx.reshape(nbatch, seq, d), grads, dfinal[0]


MESH = pl.DeviceIdType.MESH
N_CHIPS = 4
ANY = pl.BlockSpec(memory_space=pl.ANY)


def _mesh_pos():
    return lax.axis_index("x"), lax.axis_index("y"), lax.axis_index("c")


def _other_chips(x, y):
    return [(1 - x, y), (x, 1 - y), (1 - x, 1 - y)]


def _gather_weights(big, small, name):
    nbig, nsmall = len(big), len(small)
    n_ici = 3 * (nbig + nsmall)
    n_fwd = 3 * nbig

    def body(*refs):
        ins = refs[:nbig + nsmall]
        outs = refs[nbig + nsmall:2 * (nbig + nsmall)]
        send_sems, recv_sems = refs[2 * (nbig + nsmall):]
        x, y, c = _mesh_pos()
        me = 2 * x + y
        sibling = (x, y, 1 - c)
        chips = _other_chips(x, y)

        def ici(a, j, origin, dest):
            if a < nbig:
                src = ins[a].at[c] if origin is None else outs[a].at[origin, c]
                dst = outs[a].at[me if origin is None else origin, c]
            else:
                src = ins[a] if origin is None else outs[a].at[origin]
                dst = outs[a].at[me if origin is None else origin]
            k = a * 3 + j
            return pltpu.make_async_remote_copy(src_ref=src, dst_ref=dst, send_sem=send_sems.at[k],
                                                recv_sem=recv_sems.at[k], device_id=dest, device_id_type=MESH)

        def fwd(a, j, origin, half):
            k = n_ici + a * 3 + j
            ref = outs[a].at[origin, half]
            return pltpu.make_async_remote_copy(src_ref=ref, dst_ref=ref, send_sem=send_sems.at[k],
                                                recv_sem=recv_sems.at[k], device_id=sibling, device_id_type=MESH)

        sends = []
        for j, (px, py) in enumerate(chips):
            for a in range(nbig + nsmall):
                cp = ici(a, j, None, (px, py, c))
                cp.start()
                sends.append(cp)
        for j, (px, py) in enumerate(chips):
            origin = 2 * px + py
            for a in range(nbig):
                ici(a, j, origin, (px, py, c)).wait_recv()
                cp = fwd(a, j, origin, c)
                cp.start()
                sends.append(cp)
        for j, (px, py) in enumerate(chips):
            origin = 2 * px + py
            for a in range(nbig, nbig + nsmall):
                ici(a, j, origin, (px, py, c)).wait_recv()
            for a in range(nbig):
                fwd(a, j, origin, 1 - c).wait_recv()
        for cp in sends:
            cp.wait_send()

    out_shape = tuple(jax.ShapeDtypeStruct((N_CHIPS,) + a.shape, a.dtype) for a in list(big) + list(small))
    return pl.pallas_call(
        body, name=name, out_shape=out_shape,
        in_specs=[ANY] * (nbig + nsmall), out_specs=tuple([ANY] * (nbig + nsmall)),
        scratch_shapes=[pltpu.SemaphoreType.DMA((n_ici + n_fwd,)), pltpu.SemaphoreType.DMA((n_ici + n_fwd,))],
    )(*big, *small)


def _pair_swap_halves(arrs, name):
    n = len(arrs)

    def body(*refs):
        ins, outs = refs[:n], refs[n:2 * n]
        send_sems, recv_sems = refs[2 * n:]
        x, y, c = _mesh_pos()
        cps = [pltpu.make_async_remote_copy(src_ref=ins[a].at[1 - c], dst_ref=outs[a], send_sem=send_sems.at[a],
                                            recv_sem=recv_sems.at[a], device_id=(x, y, 1 - c), device_id_type=MESH)
               for a in range(n)]
        for cp in cps:
            cp.start()
        for cp in cps:
            cp.wait()

    return pl.pallas_call(
        body, name=name, out_shape=tuple(jax.ShapeDtypeStruct(a.shape[1:], a.dtype) for a in arrs),
        in_specs=[ANY] * n, out_specs=tuple([ANY] * n),
        scratch_shapes=[pltpu.SemaphoreType.DMA((n,)), pltpu.SemaphoreType.DMA((n,))],
    )(*arrs)


def _chip_scatter(arrs, name):
    n = len(arrs)

    def body(*refs):
        ins, outs = refs[:n], refs[n:2 * n]
        send_sems, recv_sems = refs[2 * n:]
        x, y, c = _mesh_pos()
        me = 2 * x + y
        cps = []
        for j, (px, py) in enumerate(_other_chips(x, y)):
            for a in range(n):
                cps.append(pltpu.make_async_remote_copy(
                    src_ref=ins[a].at[2 * px + py], dst_ref=outs[a].at[me], send_sem=send_sems.at[a * 3 + j],
                    recv_sem=recv_sems.at[a * 3 + j], device_id=(px, py, c), device_id_type=MESH))
        for cp in cps:
            cp.start()
        for cp in cps:
            cp.wait()

    return pl.pallas_call(
        body, name=name, out_shape=tuple(jax.ShapeDtypeStruct(a.shape, a.dtype) for a in arrs),
        in_specs=[ANY] * n, out_specs=tuple([ANY] * n),
        scratch_shapes=[pltpu.SemaphoreType.DMA((3 * n,)), pltpu.SemaphoreType.DMA((3 * n,))],
    )(*arrs)


def _pair_gather(arrs, name):
    n = len(arrs)

    def body(*refs):
        ins, outs = refs[:n], refs[n:2 * n]
        send_sems, recv_sems = refs[2 * n:]
        x, y, c = _mesh_pos()
        cps = [pltpu.make_async_remote_copy(src_ref=ins[a], dst_ref=outs[a].at[c], send_sem=send_sems.at[a],
                                            recv_sem=recv_sems.at[a], device_id=(x, y, 1 - c), device_id_type=MESH)
               for a in range(n)]
        for cp in cps:
            cp.start()
        for cp in cps:
            cp.wait()

    return pl.pallas_call(
        body, name=name, out_shape=tuple(jax.ShapeDtypeStruct((2,) + a.shape, a.dtype) for a in arrs),
        in_specs=[ANY] * n, out_specs=tuple([ANY] * n),
        scratch_shapes=[pltpu.SemaphoreType.DMA((n,)), pltpu.SemaphoreType.DMA((n,))],
    )(*arrs)


N_DEV = 8


def _allreduce_small(pack, name):
    r = pack.shape[0]

    def body(p_ref, o_ref, land, send_sems, recv_sems):
        x, y, c = _mesh_pos()
        me = 4 * x + 2 * y + c
        cps = []
        for k in range(1, N_DEV):
            peer = (x ^ (k >> 2), y ^ ((k >> 1) & 1), c ^ (k & 1))
            cps.append(pltpu.make_async_remote_copy(src_ref=p_ref, dst_ref=land.at[me], send_sem=send_sems.at[k - 1],
                                                    recv_sem=recv_sems.at[k - 1], device_id=peer, device_id_type=MESH))
        for cp in cps:
            cp.start()
        land[me] = p_ref[...]
        for cp in cps:
            cp.wait()
        total = land[0]
        for d in range(1, N_DEV):
            total = total + land[d]
        o_ref[...] = total

    vm = pl.BlockSpec(memory_space=pltpu.VMEM)
    return pl.pallas_call(
        body, name=name, out_shape=jax.ShapeDtypeStruct(pack.shape, F32),
        in_specs=[vm], out_specs=vm,
        scratch_shapes=[pltpu.VMEM((N_DEV, r, LANES), F32), pltpu.SemaphoreType.DMA((N_DEV - 1,)),
                        pltpu.SemaphoreType.DMA((N_DEV - 1,))],
    )(pack)


BIG_ROWS = 128


def _cast_halves(w, name):
    nl, r, cdim = w.shape
    tr = BIG_ROWS
    per = r // 2 // tr

    def body(w_ref, o_ref):
        o_ref[...] = w_ref[...].astype(o_ref.dtype)

    return pl.pallas_call(
        body, name=name, out_shape=jax.ShapeDtypeStruct((2, nl, r // 2, cdim), MXU_DTYPE),
        grid=(nl, 2, per),
        in_specs=[pl.BlockSpec((None, tr, cdim), lambda l, h, i: (l, h * per + i, 0))],
        out_specs=pl.BlockSpec((None, None, tr, cdim), lambda l, h, i: (h, l, i, 0)),
        compiler_params=_params(("parallel", "parallel", "parallel")),
    )(w)


def _sum_lead(parts, name, select=None):
    if isinstance(parts, (list, tuple)):
        k = len(parts)
        shape = parts[0].shape
    else:
        k = parts.shape[0]
        shape = parts.shape[1:]
    lead = int(np.prod(shape[:-2]))
    r, cdim = shape[-2], shape[-1]
    tr = BIG_ROWS

    def body(*refs):
        o_ref = refs[-1]
        if isinstance(parts, (list, tuple)):
            total = refs[0][...]
            for a in range(1, k):
                total = total + refs[a][...]
        else:
            total = refs[0][0]
            for a in range(1, k):
                total = total + refs[0][a]
        o_ref[...] = total

    if isinstance(parts, (list, tuple)):
        args = [p.reshape(lead, r, cdim) for p in parts]
        in_specs = [pl.BlockSpec((None, tr, cdim), lambda l, i: (l, i, 0))] * k
    else:
        args = [parts.reshape(k, lead, r, cdim)]
        in_specs = [pl.BlockSpec((k, None, tr, cdim), lambda l, i: (0, l, i, 0))]
    out = pl.pallas_call(
        body, name=name, out_shape=jax.ShapeDtypeStruct((lead, r, cdim), F32),
        grid=(lead, r // tr), in_specs=in_specs,
        out_specs=pl.BlockSpec((None, tr, cdim), lambda l, i: (l, i, 0)),
        compiler_params=_params(("parallel", "parallel")),
    )(*args)
    return out.reshape(shape)


def _adam_math(w, g, m, v):
    m2 = ADAM_B1 * m + (1.0 - ADAM_B1) * g
    v2 = ADAM_B2 * v + (1.0 - ADAM_B2) * (g * g)
    m_hat = m2 / (1.0 - ADAM_B1 ** ADAM_STEP)
    v_hat = v2 / (1.0 - ADAM_B2 ** ADAM_STEP)
    delta = -ADAM_LR * (m_hat / (jnp.sqrt(v_hat) + ADAM_EPS) + ADAM_WD * w)
    return delta, m2, v2


def _adam_big(w, g_halves, m, v, name):
    nl, r, cdim = w.shape
    tr = BIG_ROWS
    per = r // 2 // tr

    def body(w_ref, g_ref, m_ref, v_ref, go_ref, d_ref, mo_ref, vo_ref):
        g = g_ref[...]
        delta, m2, v2 = _adam_math(w_ref[...], g, m_ref[...], v_ref[...])
        go_ref[...] = g
        d_ref[...] = delta
        mo_ref[...] = m2
        vo_ref[...] = v2

    full = pl.BlockSpec((None, tr, cdim), lambda l, h, i: (l, h * per + i, 0))
    half = pl.BlockSpec((None, None, tr, cdim), lambda l, h, i: (h, l, i, 0))
    shp = jax.ShapeDtypeStruct(w.shape, F32)
    return pl.pallas_call(
        body, name=name, out_shape=(shp, shp, shp, shp),
        grid=(nl, 2, per), in_specs=[full, half, full, full], out_specs=(full, full, full, full),
        compiler_params=_params(("parallel", "parallel", "parallel")),
    )(w, g_halves, m, v)


def _adam_small(w, g, m, v, name):
    def body(w_ref, g_ref, m_ref, v_ref, d_ref, mo_ref, vo_ref):
        delta, m2, v2 = _adam_math(w_ref[...], g_ref[...], m_ref[...], v_ref[...])
        d_ref[...] = delta
        mo_ref[...] = m2
        vo_ref[...] = v2

    shp = jax.ShapeDtypeStruct(w.shape, F32)
    vm = pl.BlockSpec(memory_space=pltpu.VMEM)
    return pl.pallas_call(body, name=name, out_shape=(shp, shp, shp), in_specs=[vm] * 4, out_specs=(vm, vm, vm))(
        w, g, m, v)


def _pack(arrays):
    rows = []
    for a in arrays:
        flat = a.reshape(-1)
        pad = (-flat.shape[0]) % LANES
        if pad:
            flat = jnp.concatenate([flat, jnp.zeros((pad,), flat.dtype)])
        rows.append(flat.reshape(-1, LANES))
    out = jnp.concatenate(rows, axis=0)
    pad = (-out.shape[0]) % SUBLANES
    if pad:
        out = jnp.concatenate([out, jnp.zeros((pad, LANES), out.dtype)], axis=0)
    return out


def _unpack(pack, shapes):
    outs, row = [], 0
    for shp in shapes:
        n = int(np.prod(shp))
        nrows = -(-n // LANES)
        outs.append(pack[row:row + nrows].reshape(-1)[:n].reshape(shp))
        row += nrows
    return outs


SMALL = ["norm_w", "ssd_conv_b", "ssd_dt_bias", "ssd_a_log", "ssd_d", "ssd_norm_w", "attn_sinks",
         "conf_dw_b", "conf_ln_w", "conf_ln_b"]
WEIGHTS = ["norm_w", "w_in", "ssd_conv_w", "ssd_conv_b", "ssd_dt_bias", "ssd_a_log", "ssd_d", "ssd_norm_w",
           "attn_sinks", "conf_dw_w", "conf_dw_b", "conf_ln_w", "conf_ln_b", "w_out", "final_norm_w"]


def kernel(x, norm_w, w_in, ssd_conv_w, ssd_conv_b, ssd_dt_bias, ssd_a_log, ssd_d, ssd_norm_w, attn_sinks, conf_dw_w, conf_dw_b, conf_ln_w, conf_ln_b, w_out, final_norm_w, loss_target, m_norm_w, m_w_in, m_ssd_conv_w, m_ssd_conv_b, m_ssd_dt_bias, m_ssd_a_log, m_ssd_d, m_ssd_norm_w, m_attn_sinks, m_conf_dw_w, m_conf_dw_b, m_conf_ln_w, m_conf_ln_b, m_w_out, m_final_norm_w, v_norm_w, v_w_in, v_ssd_conv_w, v_ssd_conv_b, v_ssd_dt_bias, v_ssd_a_log, v_ssd_d, v_ssd_norm_w, v_attn_sinks, v_conf_dw_w, v_conf_dw_b, v_conf_ln_w, v_conf_ln_b, v_w_out, v_final_norm_w):
    w = dict(norm_w=norm_w, w_in=w_in, ssd_conv_w=ssd_conv_w, ssd_conv_b=ssd_conv_b, ssd_dt_bias=ssd_dt_bias,
             ssd_a_log=ssd_a_log, ssd_d=ssd_d, ssd_norm_w=ssd_norm_w, attn_sinks=attn_sinks, conf_dw_w=conf_dw_w,
             conf_dw_b=conf_dw_b, conf_ln_w=conf_ln_w, conf_ln_b=conf_ln_b, w_out=w_out, final_norm_w=final_norm_w)
    m = dict(norm_w=m_norm_w, w_in=m_w_in, ssd_conv_w=m_ssd_conv_w, ssd_conv_b=m_ssd_conv_b,
             ssd_dt_bias=m_ssd_dt_bias, ssd_a_log=m_ssd_a_log, ssd_d=m_ssd_d, ssd_norm_w=m_ssd_norm_w,
             attn_sinks=m_attn_sinks, conf_dw_w=m_conf_dw_w, conf_dw_b=m_conf_dw_b, conf_ln_w=m_conf_ln_w,
             conf_ln_b=m_conf_ln_b, w_out=m_w_out, final_norm_w=m_final_norm_w)
    v = dict(norm_w=v_norm_w, w_in=v_w_in, ssd_conv_w=v_ssd_conv_w, ssd_conv_b=v_ssd_conv_b,
             ssd_dt_bias=v_ssd_dt_bias, ssd_a_log=v_ssd_a_log, ssd_d=v_ssd_d, ssd_norm_w=v_ssd_norm_w,
             attn_sinks=v_attn_sinks, conf_dw_w=v_conf_dw_w, conf_dw_b=v_conf_dw_b, conf_ln_w=v_conf_ln_w,
             conf_ln_b=v_conf_ln_b, w_out=v_w_out, final_norm_w=v_final_norm_w)
    depth = w_in.shape[0]
    me = 2 * lax.axis_index("x") + lax.axis_index("y")

    own = [_cast_halves(w_in, name="cast_w_in"), _cast_halves(w_out, name="cast_w_out"), ssd_conv_w, conf_dw_w]
    gathered = _gather_weights(own[:2], own[2:], name="gather_weights")
    g_in, g_out, g_conv, g_dw = [lax.dynamic_update_index_in_dim(g_all, mine, me, 0)
                                 for g_all, mine in zip(gathered, own)]
    layer_params = []
    for li in range(depth):
        w_in_full = jnp.concatenate(
            [jnp.concatenate([g_in[p, 0, li], g_in[p, 1, li]], axis=0) for p in range(N_CHIPS)], axis=1)
        w_out_full = jnp.concatenate(
            [jnp.concatenate([g_out[p, 0, li], g_out[p, 1, li]], axis=0) for p in range(N_CHIPS)], axis=0)
        conv_full = jnp.concatenate([g_conv[p, li] for p in range(N_CHIPS)], axis=1)
        dw_full = jnp.concatenate([g_dw[p, li] for p in range(N_CHIPS)], axis=1)
        layer_params.append(_layer_params(li, _to_padded_cols(w_in_full), w_out_full, conv_full, dw_full, w))

    loss, grad_x, grads, dfinal = _local_step(x, loss_target, layer_params, final_norm_w)

    small_list = [grads[li][n] for li in range(depth) for n in SMALL]
    small_list += [grads[li][n] for li in range(depth) for n in ("ssd_conv_w", "conf_dw_w")]
    small_list += [dfinal, loss.reshape(1)]
    small_shapes = [a.shape for a in small_list]
    reduced = _unpack(_allreduce_small(_pack(small_list), name="allreduce_small"), small_shapes)
    ns = len(SMALL)
    g = {n: jnp.stack([reduced[li * ns + i] for li in range(depth)]) for i, n in enumerate(SMALL)}
    conv_w_cols, dw_w_cols = ssd_conv_w.shape[2], conf_dw_w.shape[2]
    g["ssd_conv_w"] = jnp.stack([lax.dynamic_slice_in_dim(reduced[depth * ns + 2 * li], me * conv_w_cols,
                                                          conv_w_cols, axis=1) for li in range(depth)])
    g["conf_dw_w"] = jnp.stack([lax.dynamic_slice_in_dim(reduced[depth * ns + 2 * li + 1], me * dw_w_cols,
                                                         dw_w_cols, axis=1) for li in range(depth)])
    g["final_norm_w"] = reduced[-2]
    loss_total = reduced[-1][0]

    cols = w_in.shape[2]
    rows_out = w_out.shape[1]
    p_in = jnp.stack([_from_padded_cols(grads[li]["w_in_p"]).reshape(2, D_MODEL // 2, N_CHIPS, cols)
                      for li in range(depth)]).transpose(1, 3, 0, 2, 4)
    p_out = jnp.stack([grads[li]["w_out"].reshape(N_CHIPS, 2, rows_out // 2, D_MODEL)
                       for li in range(depth)]).transpose(2, 1, 0, 3, 4)
    c = lax.axis_index("c")
    sib_in, sib_out = _pair_swap_halves([p_in, p_out], name="grad_pair_swap")
    mine_in = lax.dynamic_index_in_dim(p_in, c, axis=0, keepdims=False)
    mine_out = lax.dynamic_index_in_dim(p_out, c, axis=0, keepdims=False)
    s_in = _sum_lead([mine_in, sib_in], name="grad_pair_sum_in")
    s_out = _sum_lead([mine_out, sib_out], name="grad_pair_sum_out")
    r_in, r_out = _chip_scatter([s_in, s_out], name="grad_chip_scatter")
    r_in = lax.dynamic_update_index_in_dim(r_in, lax.dynamic_index_in_dim(s_in, me, 0, keepdims=False), me, 0)
    r_out = lax.dynamic_update_index_in_dim(r_out, lax.dynamic_index_in_dim(s_out, me, 0, keepdims=False), me, 0)
    t_in = _sum_lead(r_in, name="grad_chip_sum_in")
    t_out = _sum_lead(r_out, name="grad_chip_sum_out")
    gh_in, gh_out = _pair_gather([t_in, t_out], name="grad_pair_gather")
    gh_in = lax.dynamic_update_index_in_dim(gh_in, t_in, c, 0)
    gh_out = lax.dynamic_update_index_in_dim(gh_out, t_out, c, 0)

    outs_g, outs_d, outs_m, outs_v = {}, {}, {}, {}
    outs_g["w_in"], outs_d["w_in"], outs_m["w_in"], outs_v["w_in"] = _adam_big(
        w_in, gh_in, m_w_in, v_w_in, name="adam_w_in")
    outs_g["w_out"], outs_d["w_out"], outs_m["w_out"], outs_v["w_out"] = _adam_big(
        w_out, gh_out, m_w_out, v_w_out, name="adam_w_out")
    small_names = [n for n in WEIGHTS if n not in ("w_in", "w_out")]
    d_p, m_p, v_p = _adam_small(_pack([w[n] for n in small_names]), _pack([g[n] for n in small_names]),
                                _pack([m[n] for n in small_names]), _pack([v[n] for n in small_names]),
                                name="adam_small")
    shapes = [w[n].shape for n in small_names]
    for n, dn, mn, vn in zip(small_names, _unpack(d_p, shapes), _unpack(m_p, shapes), _unpack(v_p, shapes)):
        outs_g[n], outs_d[n], outs_m[n], outs_v[n] = g[n], dn, mn, vn
    return (loss_total, grad_x, *[outs_g[n] for n in WEIGHTS], *[outs_d[n] for n in WEIGHTS],
            *[outs_m[n] for n in WEIGHTS], *[outs_v[n] for n in WEIGHTS])
```

```python
import functools
import math

import jax
import jax.numpy as jnp
import numpy as np
from jax import lax
from jax.experimental import pallas as pl
from jax.experimental.pallas import tpu as pltpu

F32 = jnp.float32
BF16 = jnp.bfloat16
MXU_DTYPE = BF16

D_MODEL = 1024
DEPTH = 2
SSD_HEADS = 16
SSD_HEAD_DIM = 64
SSD_STATE = 128
SSD_CONV = 4
CHUNK = 128
SSD_CONV_DIM = 1536
ATTN_HEAD_DIM = 64
ATTN_Q_HEADS = 8
WINDOW = 128
CONF_WIDTH = 512
CONF_KERNEL = 31
MIX_WIDTH = 2048
D_IN_PROJ = 5392
EPS = 1e-5

ADAM_LR = 0.001
ADAM_B1 = 0.9
ADAM_B2 = 0.999
ADAM_EPS = 1e-08
ADAM_WD = 0.01
ADAM_STEP = 10

LANES = 128
SUBLANES = 8
VMEM_LIMIT = 48 * 1024 * 1024

NP = 5632
OFF_Z, OFF_XBC, OFF_Q, OFF_K, OFF_V, OFF_DT, OFF_CONF = 0, 2048, 3584, 4096, 4224, 4352, 4608
REF_OFF_XBC, REF_OFF_DT, REF_OFF_Q, REF_OFF_K, REF_OFF_V, REF_OFF_CONF = 2048, 3584, 3600, 4112, 4240, 4368

NN = (((1,), (0,)), ((), ()))
NT = (((1,), (1,)), ((), ()))
TN = (((0,), (0,)), ((), ()))


def _params(sem):
    return pltpu.CompilerParams(dimension_semantics=sem, vmem_limit_bytes=VMEM_LIMIT)


def _dot(a, b, dims=NN):
    return lax.dot_general(a.astype(MXU_DTYPE), b.astype(MXU_DTYPE), dims, preferred_element_type=F32)


def _split_bf16(a, passes):
    pieces = []
    r = a
    for _ in range(passes):
        p = r.astype(BF16)
        pieces.append(p)
        r = r - p.astype(F32)
    return pieces


def _xdot(a, sel, dims=NN, passes=3):
    out = None
    for p in _split_bf16(a, passes):
        t = lax.dot_general(p, sel, dims, preferred_element_type=F32)
        out = t if out is None else out + t
    return out


def _xdot_r(sel, b, dims=NN, passes=3):
    out = None
    for p in _split_bf16(b, passes):
        t = lax.dot_general(sel, p, dims, preferred_element_type=F32)
        out = t if out is None else out + t
    return out


def _sigmoid(x):
    return 1.0 / (1.0 + jnp.exp(-x))


def _silu(x):
    return x * _sigmoid(x)


def _dsilu(x):
    s = _sigmoid(x)
    return s * (1.0 + x * (1.0 - s))


def _softplus(x):
    return jnp.maximum(x, 0.0) + jnp.log(1.0 + jnp.exp(-jnp.abs(x)))


def _rowsum8(x):
    r, c = x.shape
    return jnp.sum(x.reshape(r // SUBLANES, SUBLANES, c), axis=0)


def _iota(shape, dim):
    return lax.broadcasted_iota(jnp.int32, shape, dim)


def _matmul(a, b, form, out_dtype, tm, tn, tk, name, residual=None):
    if form == "nn":
        (m, k), n = a.shape, b.shape[1]
    elif form == "nt":
        (m, k), n = a.shape, b.shape[0]
    else:
        (k, m), n = a.shape, b.shape[1]
    tm, tn, tk = min(tm, m), min(tn, n), min(tk, k)
    assert m % tm == 0 and n % tn == 0 and k % tk == 0, (name, m, n, k, tm, tn, tk)
    if form == "nn":
        a_spec = pl.BlockSpec((tm, tk), lambda i, j, s: (i, s))
        b_spec = pl.BlockSpec((tk, tn), lambda i, j, s: (s, j))
        dims = NN
    elif form == "nt":
        (m, k), n = a.shape, b.shape[0]
        a_spec = pl.BlockSpec((tm, tk), lambda i, j, s: (i, s))
        b_spec = pl.BlockSpec((tn, tk), lambda i, j, s: (j, s))
        dims = NT
    else:
        (k, m), n = a.shape, b.shape[1]
        a_spec = pl.BlockSpec((tk, tm), lambda i, j, s: (s, i))
        b_spec = pl.BlockSpec((tk, tn), lambda i, j, s: (s, j))
        dims = TN
    nk = k // tk
    has_res = residual is not None

    def body_single(a_ref, b_ref, *rest):
        o = _dot(a_ref[...], b_ref[...], dims)
        if has_res:
            o = o + rest[0][...]
        rest[-1][...] = o.astype(out_dtype)

    def body(a_ref, b_ref, *rest):
        if has_res:
            r_ref, o_ref, acc = rest
        else:
            o_ref, acc = rest
        s = pl.program_id(2)

        @pl.when(s == 0)
        def _():
            acc[...] = jnp.zeros_like(acc)

        acc[...] += _dot(a_ref[...], b_ref[...], dims)

        @pl.when(s == nk - 1)
        def _():
            o = acc[...]
            if has_res:
                o = o + r_ref[...]
            o_ref[...] = o.astype(out_dtype)

    in_specs = [a_spec, b_spec]
    args = [a, b]
    if has_res:
        in_specs.append(pl.BlockSpec((tm, tn), lambda i, j, s: (i, j)))
        args.append(residual)
    return pl.pallas_call(
        body_single if nk == 1 else body, name=name,
        out_shape=jax.ShapeDtypeStruct((m, n), out_dtype),
        grid=(m // tm, n // tn, nk),
        in_specs=in_specs,
        out_specs=pl.BlockSpec((tm, tn), lambda i, j, s: (i, j)),
        scratch_shapes=[] if nk == 1 else [pltpu.VMEM((tm, tn), F32)],
        compiler_params=_params(("parallel", "parallel", "arbitrary")),
    )(*args)


ROW_TILE = 256


def _rmsnorm_fwd(x, w, name):
    t, d = x.shape
    tm = ROW_TILE

    def body(x_ref, w_ref, o_ref, ot_ref):
        xv = x_ref[...]
        rstd = lax.rsqrt(jnp.mean(xv * xv, axis=-1, keepdims=True) + EPS)
        h = xv * rstd * w_ref[...]
        o_ref[...] = h.astype(o_ref.dtype)
        ot_ref[...] = h.T.astype(ot_ref.dtype)

    return pl.pallas_call(
        body, name=name,
        out_shape=(jax.ShapeDtypeStruct((t, d), MXU_DTYPE), jax.ShapeDtypeStruct((d, t), MXU_DTYPE)),
        grid=(t // tm,),
        in_specs=[pl.BlockSpec((tm, d), lambda i: (i, 0)), pl.BlockSpec((1, d), lambda i: (0, 0))],
        out_specs=(pl.BlockSpec((tm, d), lambda i: (i, 0)), pl.BlockSpec((d, tm), lambda i: (0, i))),
        compiler_params=_params(("parallel",)),
    )(x, w)


def _rmsnorm_bwd(dh, x, w, dres, name):
    t, d = x.shape
    tm = ROW_TILE
    nt = t // tm

    def body(dh_ref, x_ref, w_ref, dr_ref, dx_ref, dw_ref, acc):
        i = pl.program_id(0)

        @pl.when(i == 0)
        def _():
            acc[...] = jnp.zeros_like(acc)

        xv = x_ref[...]
        rstd = lax.rsqrt(jnp.mean(xv * xv, axis=-1, keepdims=True) + EPS)
        xh = xv * rstd
        dhv = dh_ref[...]
        g = dhv * w_ref[...]
        dx_ref[...] = dr_ref[...] + rstd * (g - xh * jnp.mean(g * xh, axis=-1, keepdims=True))
        acc[...] += _rowsum8(dhv * xh)

        @pl.when(i == nt - 1)
        def _():
            dw_ref[...] = jnp.sum(acc[...], axis=0, keepdims=True)

    row = pl.BlockSpec((tm, d), lambda i: (i, 0))
    vec = pl.BlockSpec((1, d), lambda i: (0, 0))
    return pl.pallas_call(
        body, name=name,
        out_shape=(jax.ShapeDtypeStruct((t, d), F32), jax.ShapeDtypeStruct((1, d), F32)),
        grid=(nt,),
        in_specs=[row, row, vec, row],
        out_specs=(row, vec),
        scratch_shapes=[pltpu.VMEM((SUBLANES, d), F32)],
        compiler_params=_params(("arbitrary",)),
    )(dh, x, w, dres)


def _loss_head(xf, target, w, name):
    t, d = xf.shape
    tm = ROW_TILE
    nt = t // tm

    def body(x_ref, t_ref, w_ref, loss_ref, dx_ref, dw_ref, lacc, wacc):
        i = pl.program_id(0)

        @pl.when(i == 0)
        def _():
            lacc[...] = jnp.zeros_like(lacc)
            wacc[...] = jnp.zeros_like(wacc)

        xv = x_ref[...]
        rstd = lax.rsqrt(jnp.mean(xv * xv, axis=-1, keepdims=True) + EPS)
        xh = xv * rstd
        err = xh * w_ref[...] - t_ref[...]
        lacc[...] += jnp.sum(err * err)
        dy = err * (1.0 / d)
        g = dy * w_ref[...]
        dx_ref[...] = rstd * (g - xh * jnp.mean(g * xh, axis=-1, keepdims=True))
        wacc[...] += _rowsum8(dy * xh)

        @pl.when(i == nt - 1)
        def _():
            loss_ref[...] = lacc[...] * (0.5 / d)
            dw_ref[...] = jnp.sum(wacc[...], axis=0, keepdims=True)

    row = pl.BlockSpec((tm, d), lambda i: (i, 0))
    vec = pl.BlockSpec((1, d), lambda i: (0, 0))
    return pl.pallas_call(
        body, name=name,
        out_shape=(jax.ShapeDtypeStruct((SUBLANES, LANES), F32), jax.ShapeDtypeStruct((t, d), F32),
                   jax.ShapeDtypeStruct((1, d), F32)),
        grid=(nt,),
        in_specs=[row, row, vec],
        out_specs=(pl.BlockSpec((SUBLANES, LANES), lambda i: (0, 0)), row, vec),
        scratch_shapes=[pltpu.VMEM((SUBLANES, LANES), F32), pltpu.VMEM((SUBLANES, d), F32)],
        compiler_params=_params(("arbitrary",)),
    )(xf, target, w)


def _glu_fwd(proj, name):
    t = proj.shape[0]
    tm, cw = ROW_TILE, CONF_WIDTH

    def body(a_ref, g_ref, o_ref):
        o_ref[...] = a_ref[...] * _sigmoid(g_ref[...])

    return pl.pallas_call(
        body, name=name,
        out_shape=jax.ShapeDtypeStruct((t, cw), F32),
        grid=(t // tm,),
        in_specs=[pl.BlockSpec((tm, cw), lambda i: (i, OFF_CONF // cw)),
                  pl.BlockSpec((tm, cw), lambda i: (i, OFF_CONF // cw + 1))],
        out_specs=pl.BlockSpec((tm, cw), lambda i: (i, 0)),
        compiler_params=_params(("parallel",)),
    )(proj, proj)


def _glu_bwd(dc0, proj, name):
    t = proj.shape[0]
    tm, cw = ROW_TILE, CONF_WIDTH

    def body(d_ref, a_ref, g_ref, o_ref):
        s = _sigmoid(g_ref[...])
        dv = d_ref[...]
        o_ref[:, :cw] = dv * s
        o_ref[:, cw:] = dv * a_ref[...] * s * (1.0 - s)

    return pl.pallas_call(
        body, name=name,
        out_shape=jax.ShapeDtypeStruct((t, 2 * cw), F32),
        grid=(t // tm,),
        in_specs=[pl.BlockSpec((tm, cw), lambda i: (i, 0)),
                  pl.BlockSpec((tm, cw), lambda i: (i, OFF_CONF // cw)),
                  pl.BlockSpec((tm, cw), lambda i: (i, OFF_CONF // cw + 1))],
        out_specs=pl.BlockSpec((tm, 2 * cw), lambda i: (i, 0)),
        compiler_params=_params(("parallel",)),
    )(dc0, proj, proj)


def _conf_post_fwd(c1, proj, ln_w, ln_b, name):
    t = c1.shape[0]
    tm, cw = ROW_TILE, CONF_WIDTH

    def body(c_ref, z_ref, w_ref, b_ref, o_ref):
        cv = c_ref[...]
        xc = cv - jnp.mean(cv, axis=-1, keepdims=True)
        rstd = lax.rsqrt(jnp.mean(xc * xc, axis=-1, keepdims=True) + EPS)
        c2 = xc * rstd * w_ref[...] + b_ref[...]
        o_ref[...] = (_silu(c2) * _silu(z_ref[...])).astype(o_ref.dtype)

    vec = pl.BlockSpec((1, cw), lambda i: (0, 0))
    return pl.pallas_call(
        body, name=name,
        out_shape=jax.ShapeDtypeStruct((t, cw), MXU_DTYPE),
        grid=(t // tm,),
        in_specs=[pl.BlockSpec((tm, cw), lambda i: (i, 0)),
                  pl.BlockSpec((tm, cw), lambda i: (i, (OFF_Z + 1536) // cw)), vec, vec],
        out_specs=pl.BlockSpec((tm, cw), lambda i: (i, 0)),
        compiler_params=_params(("parallel",)),
    )(c1, proj, ln_w, ln_b)


def _conf_post_bwd(dycat, c1, proj, ln_w, ln_b, name):
    t = c1.shape[0]
    tm, cw = ROW_TILE, CONF_WIDTH
    nt = t // tm

    def body(dy_ref, c_ref, z_ref, w_ref, b_ref, dc_ref, dz_ref, dw_ref, db_ref, wacc, bacc):
        i = pl.program_id(0)

        @pl.when(i == 0)
        def _():
            wacc[...] = jnp.zeros_like(wacc)
            bacc[...] = jnp.zeros_like(bacc)

        cv = c_ref[...]
        xc = cv - jnp.mean(cv, axis=-1, keepdims=True)
        rstd = lax.rsqrt(jnp.mean(xc * xc, axis=-1, keepdims=True) + EPS)
        xh = xc * rstd
        c2 = xh * w_ref[...] + b_ref[...]
        zv = z_ref[...]
        dy = dy_ref[...]
        dz_ref[...] = dy * _silu(c2) * _dsilu(zv)
        dc2 = dy * _silu(zv) * _dsilu(c2)
        bacc[...] += _rowsum8(dc2)
        wacc[...] += _rowsum8(dc2 * xh)
        dxh = dc2 * w_ref[...]
        dc_ref[...] = rstd * (dxh - jnp.mean(dxh, axis=-1, keepdims=True)
                              - xh * jnp.mean(dxh * xh, axis=-1, keepdims=True))

        @pl.when(i == nt - 1)
        def _():
            dw_ref[...] = jnp.sum(wacc[...], axis=0, keepdims=True)
            db_ref[...] = jnp.sum(bacc[...], axis=0, keepdims=True)

    row = pl.BlockSpec((tm, cw), lambda i: (i, 0))
    vec = pl.BlockSpec((1, cw), lambda i: (0, 0))
    return pl.pallas_call(
        body, name=name,
        out_shape=(jax.ShapeDtypeStruct((t, cw), F32), jax.ShapeDtypeStruct((t, cw), F32),
                   jax.ShapeDtypeStruct((1, cw), F32), jax.ShapeDtypeStruct((1, cw), F32)),
        grid=(nt,),
        in_specs=[pl.BlockSpec((tm, cw), lambda i: (i, 1536 // cw)), row,
                  pl.BlockSpec((tm, cw), lambda i: (i, (OFF_Z + 1536) // cw)), vec, vec],
        out_specs=(row, row, vec, vec),
        scratch_shapes=[pltpu.VMEM((SUBLANES, cw), F32), pltpu.VMEM((SUBLANES, cw), F32)],
        compiler_params=_params(("arbitrary",)),
    )(dycat, c1, proj, ln_w, ln_b)


GN_WIDTH = 512


def _gated_norm_fwd(y, proj, w, name):
    t = y.shape[0]
    tm, cw = ROW_TILE, GN_WIDTH

    def body(y_ref, z_ref, w_ref, o_ref):
        g = y_ref[...] * _silu(z_ref[...])
        rstd = lax.rsqrt(jnp.mean(g * g, axis=-1, keepdims=True) + EPS)
        o_ref[...] = (g * rstd * w_ref[...]).astype(o_ref.dtype)

    blk = pl.BlockSpec((tm, cw), lambda i, j: (i, j))
    return pl.pallas_call(
        body, name=name,
        out_shape=jax.ShapeDtypeStruct(y.shape, MXU_DTYPE),
        grid=(t // tm, y.shape[1] // cw),
        in_specs=[blk, blk, pl.BlockSpec((1, cw), lambda i, j: (0, j))],
        out_specs=blk,
        compiler_params=_params(("parallel", "parallel")),
    )(y, proj, w)


def _gated_norm_bwd(dycat, y, proj, w, name):
    t = y.shape[0]
    tm, cw = ROW_TILE, GN_WIDTH
    nt = t // tm

    def body(do_ref, y_ref, z_ref, w_ref, dy_ref, dz_ref, dw_ref, acc):
        i = pl.program_id(1)

        @pl.when(i == 0)
        def _():
            acc[...] = jnp.zeros_like(acc)

        yv, zv, dov = y_ref[...], z_ref[...], do_ref[...]
        sz = _silu(zv)
        g = yv * sz
        rstd = lax.rsqrt(jnp.mean(g * g, axis=-1, keepdims=True) + EPS)
        gh = g * rstd
        acc[...] += _rowsum8(dov * gh)
        dgn = dov * w_ref[...]
        dg = rstd * (dgn - gh * jnp.mean(dgn * gh, axis=-1, keepdims=True))
        dy_ref[...] = dg * sz
        dz_ref[...] = dg * yv * _dsilu(zv)

        @pl.when(i == nt - 1)
        def _():
            dw_ref[...] = jnp.sum(acc[...], axis=0, keepdims=True)

    blk = pl.BlockSpec((tm, cw), lambda j, i: (i, j))
    vec = pl.BlockSpec((1, cw), lambda j, i: (0, j))
    return pl.pallas_call(
        body, name=name,
        out_shape=(jax.ShapeDtypeStruct(y.shape, F32), jax.ShapeDtypeStruct(y.shape, F32),
                   jax.ShapeDtypeStruct((1, y.shape[1]), F32)),
        grid=(y.shape[1] // cw, nt),
        in_specs=[blk, blk, blk, vec],
        out_specs=(blk, blk, vec),
        scratch_shapes=[pltpu.VMEM((SUBLANES, cw), F32)],
        compiler_params=_params(("parallel", "arbitrary")),
    )(dycat, y, proj, w)


CONV_TILE = 512
CONV_COLS = 512
CONV_SUB_ROWS = 256
CONV_SUB_COLS = LANES


def _conv_halo(k):
    return SUBLANES if k - 1 <= SUBLANES else 32


def _conv_subtiles(tm, cw):
    return [(r0, c0) for r0 in range(0, tm, CONV_SUB_ROWS) for c0 in range(0, cw, CONV_SUB_COLS)]


def _conv_use_shifted(k):
    return k > SUBLANES


def _conv_shift_scratch(k, rows, cw):
    return [pltpu.VMEM((SUBLANES - 1, rows - SUBLANES, cw), F32)] if _conv_use_shifted(k) else []


def _conv_fill_shifted(ext, sh):
    n = sh.shape[1]
    for b in range(1, SUBLANES):
        sh[b - 1] = ext[b:b + n, :]


def _conv_rows(ext, sh, start, rows, cs):
    b = start % SUBLANES
    if b == 0 or not sh:
        return ext[start:start + rows, cs]
    return sh[0][b - 1, start - b:start - b + rows, cs]


def _conv_fwd(src, col0, width, w, bias, k, seq, name):
    t = src.shape[0]
    tm, cw, halo = CONV_TILE, CONV_COLS, _conv_halo(k)
    sr, sc = CONV_SUB_ROWS, CONV_SUB_COLS
    p = k - 1
    cb0 = col0 // cw
    kp = w.shape[0]

    shifted = _conv_use_shifted(k)

    def body(x_ref, h_ref, w_ref, b_ref, o_ref, ext, *sh):
        i = pl.program_id(0)
        seq_start = (i * tm) % seq == 0
        ext[halo:, :] = x_ref[...]
        ext[:halo, :] = jnp.where(seq_start, 0.0, h_ref[...])
        if shifted:
            _conv_fill_shifted(ext, sh[0])
        for r0, c0 in _conv_subtiles(tm, cw):
            cs = slice(c0, c0 + sc)
            acc = jnp.zeros((sr, sc), F32) + b_ref[:, cs]
            for j in range(k):
                acc = acc + w_ref[j:j + 1, cs] * _conv_rows(ext, sh, r0 + halo - p + j, sr, cs)
            o_ref[r0:r0 + sr, cs] = acc

    return pl.pallas_call(
        body, name=name,
        out_shape=jax.ShapeDtypeStruct((t, width), F32),
        grid=(t // tm, width // cw),
        in_specs=[pl.BlockSpec((tm, cw), lambda i, j: (i, cb0 + j)),
                  pl.BlockSpec((halo, cw), lambda i, j: (jnp.maximum(i * (tm // halo) - 1, 0), cb0 + j)),
                  pl.BlockSpec((kp, cw), lambda i, j: (0, j)),
                  pl.BlockSpec((1, cw), lambda i, j: (0, j))],
        out_specs=pl.BlockSpec((tm, cw), lambda i, j: (i, j)),
        scratch_shapes=[pltpu.VMEM((halo + tm, cw), F32)] + _conv_shift_scratch(k, halo + tm, cw),
        compiler_params=_params(("parallel", "parallel")),
    )(src, src, w, bias)


def _conv_bwd(dy, src, col0, width, w, k, seq, name):
    t = src.shape[0]
    tm, cw, halo = CONV_TILE, CONV_COLS, _conv_halo(k)
    sr, sc = CONV_SUB_ROWS, CONV_SUB_COLS
    p = k - 1
    cb0 = col0 // cw
    kp = w.shape[0]
    nt = t // tm
    last_halo = t // halo - 1

    shifted = _conv_use_shifted(k)

    def body(dy_ref, dn_ref, x_ref, xp_ref, w_ref, dx_ref, dw_ref, db_ref, dyext, xext, wacc, bacc, *sh):
        i = pl.program_id(1)
        dysh, xsh = (sh[:1], sh[1:]) if shifted else ((), ())

        @pl.when(i == 0)
        def _():
            wacc[...] = jnp.zeros_like(wacc)
            bacc[...] = jnp.zeros_like(bacc)

        seq_start = (i * tm) % seq == 0
        seq_end = ((i + 1) * tm) % seq == 0
        dyext[:tm, :] = dy_ref[...]
        dyext[tm:, :] = jnp.where(seq_end, 0.0, dn_ref[...])
        xext[halo:, :] = x_ref[...]
        xext[:halo, :] = jnp.where(seq_start, 0.0, xp_ref[...])
        if shifted:
            _conv_fill_shifted(dyext, dysh[0])
            _conv_fill_shifted(xext, xsh[0])
        for r0, c0 in _conv_subtiles(tm, cw):
            cs = slice(c0, c0 + sc)
            dyv = dy_ref[r0:r0 + sr, cs]
            acc = jnp.zeros((sr, sc), F32)
            for j in range(k):
                acc = acc + w_ref[j:j + 1, cs] * _conv_rows(dyext, dysh, r0 + p - j, sr, cs)
                wacc[j, :, cs] += _rowsum8(dyv * _conv_rows(xext, xsh, r0 + halo - p + j, sr, cs))
            dx_ref[r0:r0 + sr, cs] = acc
            bacc[:, cs] += _rowsum8(dyv)

        @pl.when(i == nt - 1)
        def _():
            dw_ref[...] = jnp.zeros_like(dw_ref)
            for j in range(k):
                dw_ref[j:j + 1, :] = jnp.sum(wacc[j], axis=0, keepdims=True)
            db_ref[...] = jnp.sum(bacc[...], axis=0, keepdims=True)

    return pl.pallas_call(
        body, name=name,
        out_shape=(jax.ShapeDtypeStruct((t, width), F32), jax.ShapeDtypeStruct((kp, width), F32),
                   jax.ShapeDtypeStruct((1, width), F32)),
        grid=(width // cw, nt),
        in_specs=[pl.BlockSpec((tm, cw), lambda j, i: (i, j)),
                  pl.BlockSpec((halo, cw), lambda j, i: (jnp.minimum((i + 1) * (tm // halo), last_halo), j)),
                  pl.BlockSpec((tm, cw), lambda j, i: (i, cb0 + j)),
                  pl.BlockSpec((halo, cw), lambda j, i: (jnp.maximum(i * (tm // halo) - 1, 0), cb0 + j)),
                  pl.BlockSpec((kp, cw), lambda j, i: (0, j))],
        out_specs=(pl.BlockSpec((tm, cw), lambda j, i: (i, j)),
                   pl.BlockSpec((kp, cw), lambda j, i: (0, j)),
                   pl.BlockSpec((1, cw), lambda j, i: (0, j))),
        scratch_shapes=[pltpu.VMEM((tm + halo, cw), F32), pltpu.VMEM((halo + tm, cw), F32),
                        pltpu.VMEM((kp, SUBLANES, cw), F32), pltpu.VMEM((SUBLANES, cw), F32)]
        + 2 * _conv_shift_scratch(k, halo + tm, cw),
        compiler_params=_params(("parallel", "arbitrary")),
    )(dy, dy, src, src, w)


def _head_dup(g):
    r, c = _iota((LANES, LANES), 0), _iota((LANES, LANES), 1)
    return (r == g * ATTN_HEAD_DIM + (c & (ATTN_HEAD_DIM - 1))).astype(BF16)


def _half_mask(half):
    lane = _iota((1, LANES), 1)
    return ((lane >= half * ATTN_HEAD_DIM) & (lane < (half + 1) * ATTN_HEAD_DIM)).astype(F32)


def _band_mask(first_block):
    w = WINDOW
    qi = _iota((w, 2 * w), 0)
    kj = _iota((w, 2 * w), 1) - w
    rel = qi - kj
    return (rel >= 0) & (rel < w) & (jnp.logical_not(first_block) | (kj >= 0))


def _lane_pick(x, h):
    return jnp.sum(jnp.where(_iota(x.shape, 1) == h, x, 0.0), axis=1, keepdims=True)


def _attn_specs(nb, rev):
    w = WINDOW

    def blk(i):
        return nb - 1 - i if rev else i

    def row(b, i):
        return b * nb + blk(i)

    def prow(b, i):
        return b * nb + jnp.maximum(blk(i) - 1, 0)

    q = pl.BlockSpec((w, 512), lambda b, i: (row(b, i), OFF_Q // 512))
    kc = pl.BlockSpec((w, 128), lambda b, i: (row(b, i), OFF_K // 128))
    kp = pl.BlockSpec((w, 128), lambda b, i: (prow(b, i), OFF_K // 128))
    vc = pl.BlockSpec((w, 128), lambda b, i: (row(b, i), OFF_V // 128))
    vp = pl.BlockSpec((w, 128), lambda b, i: (prow(b, i), OFF_V // 128))
    z = pl.BlockSpec((w, 512), lambda b, i: (row(b, i), (OFF_Z + 1024) // 512))
    return q, kc, kp, vc, vp, z, row


def _attn_fwd(proj, sinks, nbatch, name):
    t = proj.shape[0]
    w = WINDOW
    nb = t // nbatch // w
    scale = ATTN_HEAD_DIM ** -0.5
    q_s, kc_s, kp_s, vc_s, vp_s, z_s, row = _attn_specs(nb, False)

    def body(q_ref, kc_ref, kp_ref, vc_ref, vp_ref, z_ref, sk_ref, y_ref, o_ref, lse_ref):
        first = pl.program_id(1) == 0
        mask = _band_mask(first)
        kk = jnp.concatenate([kp_ref[...], kc_ref[...]], axis=0).astype(MXU_DTYPE)
        vv = jnp.concatenate([vp_ref[...], vc_ref[...]], axis=0).astype(MXU_DTYPE)
        sk = sk_ref[...]
        lse_all = jnp.zeros((w, LANES), F32)
        lane = _iota((w, LANES), 1)
        for g in range(2):
            dup = _head_dup(g)
            kkd = _dot(kk, dup).astype(MXU_DTYPE)
            vvd = _dot(vv, dup)
            for jj in range(2):
                j = 2 * g + jj
                qp = q_ref[:, j * LANES:(j + 1) * LANES]
                op = jnp.zeros((w, LANES), F32)
                for half in range(2):
                    h = 2 * j + half
                    hm = _half_mask(half)
                    s = _dot(qp * hm, kkd, NT) * scale
                    s = jnp.where(mask, s, -1e30)
                    skh = _lane_pick(sk, h)
                    m = jnp.maximum(jnp.max(s, axis=1, keepdims=True), skh)
                    den = jnp.sum(jnp.exp(s - m), axis=1, keepdims=True) + jnp.exp(skh - m)
                    lse = m + jnp.log(den)
                    pr = jnp.exp(s - lse)
                    op = op + _dot(pr, vvd * hm)
                    lse_all = jnp.where(lane == h, lse, lse_all)
                o_ref[:, j * LANES:(j + 1) * LANES] = op
                y_ref[:, j * LANES:(j + 1) * LANES] = (
                    op * _silu(z_ref[:, j * LANES:(j + 1) * LANES])).astype(y_ref.dtype)
        lse_ref[...] = lse_all

    return pl.pallas_call(
        body, name=name,
        out_shape=(jax.ShapeDtypeStruct((t, 512), MXU_DTYPE), jax.ShapeDtypeStruct((t, 512), F32),
                   jax.ShapeDtypeStruct((t, LANES), F32)),
        grid=(nbatch, nb),
        in_specs=[q_s, kc_s, kp_s, vc_s, vp_s, z_s, pl.BlockSpec((1, LANES), lambda b, i: (0, 0))],
        out_specs=(pl.BlockSpec((w, 512), lambda b, i: (row(b, i), 0)),
                   pl.BlockSpec((w, 512), lambda b, i: (row(b, i), 0)),
                   pl.BlockSpec((w, LANES), lambda b, i: (row(b, i), 0))),
        compiler_params=_params(("parallel", "parallel")),
    )(proj, proj, proj, proj, proj, proj, sinks)


def _attn_bwd(dycat, proj, o, lse, sinks, nbatch, name):
    t = proj.shape[0]
    w = WINDOW
    nb = t // nbatch // w
    scale = ATTN_HEAD_DIM ** -0.5
    q_s, kc_s, kp_s, vc_s, vp_s, z_s, row = _attn_specs(nb, True)

    def body(dy_ref, q_ref, kc_ref, kp_ref, vc_ref, vp_ref, z_ref, o_ref, lse_ref, sk_ref,
             dqkv_ref, dz_ref, dsk_ref, kcarry, vcarry, sacc):
        b, i = pl.program_id(0), pl.program_id(1)

        @pl.when((b == 0) & (i == 0))
        def _():
            sacc[...] = jnp.zeros_like(sacc)

        @pl.when(i == 0)
        def _():
            kcarry[...] = jnp.zeros_like(kcarry)
            vcarry[...] = jnp.zeros_like(vcarry)

        first = i == nb - 1
        mask = _band_mask(first)
        kk = jnp.concatenate([kp_ref[...], kc_ref[...]], axis=0).astype(MXU_DTYPE)
        vv = jnp.concatenate([vp_ref[...], vc_ref[...]], axis=0).astype(MXU_DTYPE)
        sk = sk_ref[...]
        lse_all = lse_ref[...]
        lane1 = _iota((1, LANES), 1)
        dkk = jnp.zeros((2 * w, LANES), F32)
        dvv = jnp.zeros((2 * w, LANES), F32)
        dsk = jnp.zeros((1, LANES), F32)
        for g in range(2):
            dup = _head_dup(g)
            kkd = _dot(kk, dup).astype(MXU_DTYPE)
            vvd = _dot(vv, dup).astype(MXU_DTYPE)
            dkd = jnp.zeros((2 * w, LANES), F32)
            dvd = jnp.zeros((2 * w, LANES), F32)
            for jj in range(2):
                j = 2 * g + jj
                cols = slice(j * LANES, (j + 1) * LANES)
                qp, zp, ov, dy = q_ref[:, cols], z_ref[:, cols], o_ref[:, cols], dy_ref[:, cols]
                dz_ref[:, cols] = dy * ov * _dsilu(zp)
                do = dy * _silu(zp)
                dq = jnp.zeros((w, LANES), F32)
                for half in range(2):
                    h = 2 * j + half
                    hm = _half_mask(half)
                    qh = qp * hm
                    doh = do * hm
                    delta = jnp.sum(doh * ov, axis=1, keepdims=True)
                    lse_h = _lane_pick(lse_all, h)
                    s = _dot(qh, kkd, NT) * scale
                    s = jnp.where(mask, s, -1e30)
                    pr = jnp.exp(s - lse_h)
                    dp = _dot(doh, vvd, NT)
                    ds = pr * (dp - delta)
                    dq = dq + _dot(ds, kkd) * hm * scale
                    dkd = dkd + _dot(ds, qh, TN) * scale
                    dvd = dvd + _dot(pr, doh, TN)
                    psink = jnp.exp(_lane_pick(sk, h) - lse_h)
                    dsk = dsk - jnp.where(lane1 == h, jnp.sum(psink * delta), 0.0)
                dqkv_ref[:, cols] = dq
            dkk = dkk + _xdot(dkd, dup, NT, passes=2)
            dvv = dvv + _xdot(dvd, dup, NT, passes=2)
        dqkv_ref[:, 512:640] = dkk[w:, :] + kcarry[...]
        dqkv_ref[:, 640:768] = dvv[w:, :] + vcarry[...]
        kcarry[...] = dkk[:w, :]
        vcarry[...] = dvv[:w, :]
        sacc[...] += dsk

        @pl.when((b == nbatch - 1) & (i == nb - 1))
        def _():
            dsk_ref[...] = sacc[...]

    return pl.pallas_call(
        body, name=name,
        out_shape=(jax.ShapeDtypeStruct((t, 768), F32), jax.ShapeDtypeStruct((t, 512), F32),
                   jax.ShapeDtypeStruct((1, LANES), F32)),
        grid=(nbatch, nb),
        in_specs=[pl.BlockSpec((w, 512), lambda b, i: (row(b, i), 1024 // 512)),
                  q_s, kc_s, kp_s, vc_s, vp_s, z_s,
                  pl.BlockSpec((w, 512), lambda b, i: (row(b, i), 0)),
                  pl.BlockSpec((w, LANES), lambda b, i: (row(b, i), 0)),
                  pl.BlockSpec((1, LANES), lambda b, i: (0, 0))],
        out_specs=(pl.BlockSpec((w, 768), lambda b, i: (row(b, i), 0)),
                   pl.BlockSpec((w, 512), lambda b, i: (row(b, i), 0)),
                   pl.BlockSpec((1, LANES), lambda b, i: (0, 0))),
        scratch_shapes=[pltpu.VMEM((w, LANES), F32), pltpu.VMEM((w, LANES), F32),
                        pltpu.VMEM((1, LANES), F32)],
        compiler_params=_params(("arbitrary", "arbitrary")),
    )(dycat, proj, proj, proj, proj, proj, proj, o, lse, sinks)


SSD_WIDTH = SSD_HEADS * SSD_HEAD_DIM
GROUP_ROWS = SSD_WIDTH // 2


def _expand_mat():
    r, c = _iota((LANES, SSD_WIDTH), 0), _iota((LANES, SSD_WIDTH), 1)
    return (r == lax.shift_right_logical(c, 6)).astype(BF16)


def _expand_mat_t():
    r, c = _iota((SSD_WIDTH, LANES), 0), _iota((SSD_WIDTH, LANES), 1)
    return (c == lax.shift_right_logical(r, 6)).astype(BF16)


def _ssd_common(u_ref, dt_ref, dtb_ref, a_ref):
    q = CHUNK
    act = _silu(u_ref[...])
    xs = act[:, :SSD_WIDTH]
    bm = act[:, SSD_WIDTH:SSD_WIDTH + 256]
    cm = act[:, SSD_WIDTH + 256:]
    dtp = _softplus(dt_ref[...] + dtb_ref[...])
    a = dtp * a_ref[...]
    tril = (_iota((q, q), 0) >= _iota((q, q), 1)).astype(BF16)
    acs = _xdot_r(tril, a)
    acs_t = acs.T
    e = _expand_mat()
    dt_x = _xdot(dtp, e)
    ea = jnp.exp(_xdot(acs, e))
    a_end = jnp.sum(jnp.where(_iota(acs.shape, 0) == q - 1, acs, 0.0), axis=0, keepdims=True)
    dec = jnp.exp(_xdot(a_end - acs, e))
    a_end_col = jnp.broadcast_to(_lane_pick(acs_t, q - 1), (LANES, LANES))
    s_scale = jnp.exp(_xdot_r(_expand_mat_t(), a_end_col))
    return act, xs, bm, cm, dtp, acs, acs_t, dt_x, ea, dec, s_scale, tril


def _decay_mat(acs, acs_t, h):
    q = CHUNK
    col = _lane_pick(acs, h)
    rowv = jnp.sum(jnp.where(_iota(acs_t.shape, 0) == h, acs_t, 0.0), axis=0, keepdims=True)
    causal = _iota((q, q), 0) >= _iota((q, q), 1)
    return jnp.exp(jnp.where(causal, col - rowv, -1e30))


def _ssd_fwd(u, proj, dtb, a_neg, d_x, nbatch, name):
    t = u.shape[0]
    q = CHUNK
    nc = t // nbatch // q

    def body(u_ref, dt_ref, dtb_ref, a_ref, dx_ref, y_ref, st_ref, state):
        c = pl.program_id(1)

        @pl.when(c == 0)
        def _():
            state[...] = jnp.zeros_like(state)

        st_ref[...] = state[...]
        act, xs, bm, cm, dtp, acs, acs_t, dt_x, ea, dec, s_scale, _ = _ssd_common(u_ref, dt_ref, dtb_ref, a_ref)
        xdt = xs * dt_x
        xdec = xdt * dec
        lo, hi = _half_mask(0), _half_mask(1)
        for g in range(2):
            bg = bm[:, g * LANES:(g + 1) * LANES]
            cg = cm[:, g * LANES:(g + 1) * LANES]
            rows = slice(g * GROUP_ROWS, (g + 1) * GROUP_ROWS)
            sg = state[rows, :]
            cb = _dot(cg, bg, NT)
            yoff = _dot(cg, sg, NT)
            for j in range(4):
                pj = g * 4 + j
                cols = slice(pj * LANES, (pj + 1) * LANES)
                xp = xdt[:, cols]
                m0 = cb * _decay_mat(acs, acs_t, 2 * pj)
                m1 = cb * _decay_mat(acs, acs_t, 2 * pj + 1)
                yp = _dot(m0, xp * lo) + _dot(m1, xp * hi)
                yp = yp + yoff[:, j * LANES:(j + 1) * LANES] * ea[:, cols]
                y_ref[:, cols] = yp + dx_ref[:, cols] * xs[:, cols]
            state[rows, :] = s_scale[rows, :] * sg + _dot(xdec[:, rows], bg, TN)

    vec = pl.BlockSpec((1, LANES), lambda b, c: (0, 0))
    return pl.pallas_call(
        body, name=name,
        out_shape=(jax.ShapeDtypeStruct((t, SSD_WIDTH), F32),
                   jax.ShapeDtypeStruct((nbatch * nc * SSD_WIDTH, SSD_STATE), F32)),
        grid=(nbatch, nc),
        in_specs=[pl.BlockSpec((q, SSD_CONV_DIM), lambda b, c: (b * nc + c, 0)),
                  pl.BlockSpec((q, LANES), lambda b, c: (b * nc + c, OFF_DT // LANES)),
                  vec, vec, pl.BlockSpec((1, SSD_WIDTH), lambda b, c: (0, 0))],
        out_specs=(pl.BlockSpec((q, SSD_WIDTH), lambda b, c: (b * nc + c, 0)),
                   pl.BlockSpec((SSD_WIDTH, SSD_STATE), lambda b, c: (b * nc + c, 0))),
        scratch_shapes=[pltpu.VMEM((SSD_WIDTH, SSD_STATE), F32)],
        compiler_params=_params(("parallel", "arbitrary")),
    )(u, proj, dtb, a_neg, d_x)


def _ssd_bwd(dy, u, proj, states, dtb, a_neg, d_x, nbatch, name):
    t = u.shape[0]
    q = CHUNK
    nc = t // nbatch // q

    def body(dy_ref, u_ref, dt_ref, st_ref, dtb_ref, a_ref, dx_ref,
             du_ref, ddt_ref, dal_ref, dd_ref, dtbg_ref, dstate, acc_a, acc_d, acc_b):
        b, c = pl.program_id(0), pl.program_id(1)

        @pl.when((b == 0) & (c == 0))
        def _():
            acc_a[...] = jnp.zeros_like(acc_a)
            acc_d[...] = jnp.zeros_like(acc_d)
            acc_b[...] = jnp.zeros_like(acc_b)

        @pl.when(c == 0)
        def _():
            dstate[...] = jnp.zeros_like(dstate)

        act, xs, bm, cm, dtp, acs, acs_t, dt_x, ea, dec, s_scale, tril = _ssd_common(
            u_ref, dt_ref, dtb_ref, a_ref)
        xdt = xs * dt_x
        xdec = xdt * dec
        dyv = dy_ref[...]
        dye = dyv * ea
        lo, hi = _half_mask(0), _half_mask(1)
        et = _expand_mat_t()
        dxdt_parts, db_parts, dc_parts, dxst_parts, yoff_parts = [], [], [], [], []
        end_sum = jnp.zeros((LANES, LANES), F32)
        dal_diag = jnp.zeros((q, LANES), F32)
        lane_q = _iota((q, LANES), 1)
        for g in range(2):
            bg = bm[:, g * LANES:(g + 1) * LANES]
            cg = cm[:, g * LANES:(g + 1) * LANES]
            rows = slice(g * GROUP_ROWS, (g + 1) * GROUP_ROWS)
            sg = st_ref[rows, :]
            dsg = dstate[rows, :]
            cb = _dot(cg, bg, NT)
            yoff_parts.append(_dot(cg, sg, NT))
            dcb = jnp.zeros((q, q), F32)
            parts = []
            for j in range(4):
                pj = g * 4 + j
                cols = slice(pj * LANES, (pj + 1) * LANES)
                xp = xdt[:, cols]
                dy0, dy1 = dyv[:, cols] * lo, dyv[:, cols] * hi
                l0 = _decay_mat(acs, acs_t, 2 * pj)
                l1 = _decay_mat(acs, acs_t, 2 * pj + 1)
                g0, g1 = _dot(dy0, xp, NT), _dot(dy1, xp, NT)
                m0, m1 = cb * l0, cb * l1
                dcb = dcb + g0 * l0 + g1 * l1
                parts.append(_dot(m0, dy0, TN) + _dot(m1, dy1, TN))
                for hh, wmat in enumerate((g0 * m0, g1 * m1)):
                    sel = (lane_q == 2 * pj + hh).astype(F32)
                    dal_diag = dal_diag + _dot(wmat, sel) - _dot(wmat, sel, TN)
            dxst = _dot(bg, dsg, NT) * dec[:, rows]
            dxst_parts.append(dxst)
            dxdt_parts.append(jnp.concatenate(parts, axis=1) + dxst)
            dc_parts.append(_dot(dcb, bg) + _dot(dye[:, rows], sg))
            db_parts.append(_dot(dcb, cg, TN) + _dot(xdec[:, rows], dsg))
            s_next = s_scale[rows, :] * sg + _dot(xdec[:, rows], bg, TN)
            end_sum = end_sum + _xdot(dsg * s_next, et[rows, :], TN, passes=2)
            dstate[rows, :] = _dot(dye[:, rows], cg, TN) + s_scale[rows, :] * dsg
        dxdt = jnp.concatenate(dxdt_parts, axis=1)
        dxv = dx_ref[...]
        yoff = jnp.concatenate(yoff_parts, axis=1) * ea
        dalpha = dal_diag + _xdot(dyv * yoff - xdt * jnp.concatenate(dxst_parts, axis=1), et)
        end_row = jnp.sum(end_sum, axis=0, keepdims=True)
        dalpha = dalpha + jnp.where(_iota((q, LANES), 0) == q - 1, end_row, 0.0)
        da = _xdot_r(tril, dalpha, TN)
        ddtp = da * a_ref[...] + _xdot(dxdt * xs, et)
        acc_a[...] += _rowsum8(da * dtp)
        acc_d[...] += _rowsum8(_xdot(dyv * xs, et))
        ddt_raw = ddtp * _sigmoid(dt_ref[...] + dtb_ref[...])
        acc_b[...] += _rowsum8(ddt_raw)
        ddt_ref[...] = ddt_raw
        dxs = dxdt * dt_x + dxv * dyv
        dact = jnp.concatenate([dxs] + db_parts + dc_parts, axis=1)
        du_ref[...] = dact * _dsilu(u_ref[...])

        @pl.when((b == nbatch - 1) & (c == nc - 1))
        def _():
            dal_ref[...] = jnp.sum(acc_a[...], axis=0, keepdims=True) * a_ref[...]
            dd_ref[...] = jnp.sum(acc_d[...], axis=0, keepdims=True)
            dtbg_ref[...] = jnp.sum(acc_b[...], axis=0, keepdims=True)

    def rowblk(b, c):
        return b * nc + (nc - 1 - c)

    vec = pl.BlockSpec((1, LANES), lambda b, c: (0, 0))
    wide = pl.BlockSpec((q, SSD_WIDTH), lambda b, c: (rowblk(b, c), 0))
    return pl.pallas_call(
        body, name=name,
        out_shape=(jax.ShapeDtypeStruct((t, SSD_CONV_DIM), F32), jax.ShapeDtypeStruct((t, LANES), F32),
                   jax.ShapeDtypeStruct((1, LANES), F32), jax.ShapeDtypeStruct((1, LANES), F32),
                   jax.ShapeDtypeStruct((1, LANES), F32)),
        grid=(nbatch, nc),
        in_specs=[wide,
                  pl.BlockSpec((q, SSD_CONV_DIM), lambda b, c: (rowblk(b, c), 0)),
                  pl.BlockSpec((q, LANES), lambda b, c: (rowblk(b, c), OFF_DT // LANES)),
                  pl.BlockSpec((SSD_WIDTH, SSD_STATE), lambda b, c: (rowblk(b, c), 0)),
                  vec, vec, pl.BlockSpec((1, SSD_WIDTH), lambda b, c: (0, 0))],
        out_specs=(pl.BlockSpec((q, SSD_CONV_DIM), lambda b, c: (rowblk(b, c), 0)),
                   pl.BlockSpec((q, LANES), lambda b, c: (rowblk(b, c), 0)),
                   vec, vec, vec),
        scratch_shapes=[pltpu.VMEM((SSD_WIDTH, SSD_STATE), F32), pltpu.VMEM((SUBLANES, LANES), F32),
                        pltpu.VMEM((SUBLANES, LANES), F32), pltpu.VMEM((SUBLANES, LANES), F32)],
        compiler_params=_params(("arbitrary", "arbitrary")),
    )(dy, u, proj, states, dtb, a_neg, d_x)


def _pad_rows(w, rows):
    return jnp.concatenate([w, jnp.zeros((rows - w.shape[0], w.shape[1]), w.dtype)], axis=0)


def _pad_lanes(v):
    return jnp.concatenate([v, jnp.zeros((LANES - v.shape[0],), v.dtype)]).reshape(1, LANES)


SECTIONS = ((0, REF_OFF_DT, OFF_Z), (REF_OFF_DT, REF_OFF_Q, OFF_DT), (REF_OFF_Q, REF_OFF_CONF, OFF_Q),
            (REF_OFF_CONF, D_IN_PROJ, OFF_CONF))


def _padded_from_chips(pieces):
    cols = pieces[0].shape[-1]
    lead = pieces[0].shape[:-1]
    parts, pos = [], 0
    for lo, hi, start in sorted(SECTIONS, key=lambda s: s[2]):
        if start > pos:
            parts.append(jnp.zeros(lead + (start - pos,), pieces[0].dtype))
        pos = start + hi - lo
        while lo < hi:
            p = lo // cols
            end = min(hi, (p + 1) * cols)
            parts.append(pieces[p][..., lo - p * cols:end - p * cols])
            lo = end
    if pos < NP:
        parts.append(jnp.zeros(lead + (NP - pos,), pieces[0].dtype))
    return jnp.concatenate(parts, axis=-1)


def _chip_part_from_padded(wp, p, cols):
    lo, hi = p * cols, (p + 1) * cols
    parts = []
    for rs, re, start in SECTIONS:
        a, b = max(lo, rs), min(hi, re)
        if a < b:
            parts.append(wp[..., start + a - rs:start + b - rs])
    return jnp.concatenate(parts, axis=-1)


def _layer_params(li, w_in_p, w_out, conv_w, dw_w, small):
    return dict(
        w_in_p=w_in_p, w_out=w_out,
        conv_w=_pad_rows(conv_w, SUBLANES), dw_w=_pad_rows(dw_w, 32),
        norm_w=small["norm_w"][li].reshape(1, -1),
        conv_b=small["ssd_conv_b"][li].reshape(1, -1),
        dtb=_pad_lanes(small["ssd_dt_bias"][li]),
        a_neg=_pad_lanes(-jnp.exp(small["ssd_a_log"][li])),
        d_x=jnp.repeat(small["ssd_d"][li], SSD_HEAD_DIM).reshape(1, -1),
        ssd_norm_w=small["ssd_norm_w"][li].reshape(1, -1),
        sinks=_pad_lanes(small["attn_sinks"][li]),
        dw_b=small["conf_dw_b"][li].reshape(1, -1),
        ln_w=small["conf_ln_w"][li].reshape(1, -1),
        ln_b=small["conf_ln_b"][li].reshape(1, -1),
    )


def _layer_fwd(x, p, nbatch, seq, tag):
    h, h_t = _rmsnorm_fwd(x, p["norm_w"], name=f"rmsnorm_fwd_{tag}")
    proj = _matmul(h, p["w_in_p"], "nn", F32, 1024, 512, 1024, name=f"proj_fwd_{tag}")
    u = _conv_fwd(proj, OFF_XBC, SSD_CONV_DIM, p["conv_w"], p["conv_b"], SSD_CONV, seq, name=f"ssd_conv_fwd_{tag}")
    y, states = _ssd_fwd(u, proj, p["dtb"], p["a_neg"], p["d_x"], nbatch, name=f"ssd_fwd_{tag}")
    y_ssd = _gated_norm_fwd(y, proj, p["ssd_norm_w"], name=f"gated_norm_fwd_{tag}")
    y_attn, o, lse = _attn_fwd(proj, p["sinks"], nbatch, name=f"attn_fwd_{tag}")
    c0 = _glu_fwd(proj, name=f"glu_fwd_{tag}")
    c1 = _conv_fwd(c0, 0, CONF_WIDTH, p["dw_w"], p["dw_b"], CONF_KERNEL, seq, name=f"conf_conv_fwd_{tag}")
    y_conf = _conf_post_fwd(c1, proj, p["ln_w"], p["ln_b"], name=f"conf_post_fwd_{tag}")
    ycat = jnp.concatenate([y_ssd, y_attn, y_conf], axis=1)
    x_new = _matmul(ycat, p["w_out"], "nn", F32, 1024, 512, 2048, name=f"out_fwd_{tag}", residual=x)
    return x_new, dict(x=x, h_t=h_t, proj=proj, u=u, y=y, states=states, o=o, lse=lse, c0=c0, c1=c1, ycat=ycat)


def _layer_bwd(dx_out, p, s, nbatch, seq, tag):
    proj = s["proj"]
    dycat = _matmul(dx_out, p["w_out"], "nt", F32, 1024, 1024, 1024, name=f"out_bwd_dy_{tag}")
    dw_out = _matmul(s["ycat"], dx_out, "tn", F32, 1024, 1024, 1024, name=f"out_bwd_dw_{tag}")
    dc1, dz_conf, dln_w, dln_b = _conf_post_bwd(dycat, s["c1"], proj, p["ln_w"], p["ln_b"],
                                                name=f"conf_post_bwd_{tag}")
    dc0, ddw_w, ddw_b = _conv_bwd(dc1, s["c0"], 0, CONF_WIDTH, p["dw_w"], CONF_KERNEL, seq,
                                  name=f"conf_conv_bwd_{tag}")
    dconf = _glu_bwd(dc0, proj, name=f"glu_bwd_{tag}")
    dqkv, dz_attn, dsinks = _attn_bwd(dycat, proj, s["o"], s["lse"], p["sinks"], nbatch, name=f"attn_bwd_{tag}")
    dy, dz_ssd, dssd_norm_w = _gated_norm_bwd(dycat, s["y"], proj, p["ssd_norm_w"], name=f"gated_norm_bwd_{tag}")
    du, ddt, da_log, dd, ddtb = _ssd_bwd(dy, s["u"], proj, s["states"], p["dtb"], p["a_neg"], p["d_x"],
                                         nbatch, name=f"ssd_bwd_{tag}")
    dxbc, dconv_w, dconv_b = _conv_bwd(du, proj, OFF_XBC, SSD_CONV_DIM, p["conv_w"], SSD_CONV, seq,
                                       name=f"ssd_conv_bwd_{tag}")
    t = proj.shape[0]
    dproj = jnp.concatenate([dz_ssd, dz_attn, dz_conf, dxbc, dqkv, ddt, jnp.zeros((t, LANES), F32), dconf],
                            axis=1).astype(MXU_DTYPE)
    dh = _matmul(dproj, p["w_in_p"], "nt", F32, 1024, 1024, 1408, name=f"proj_bwd_dh_{tag}")
    dw_in_p = _matmul(s["h_t"], dproj, "nn", F32, 1024, 512, 4096, name=f"proj_bwd_dw_{tag}")
    dx_in, dnorm_w = _rmsnorm_bwd(dh, s["x"], p["norm_w"], dx_out, name=f"rmsnorm_bwd_{tag}")
    grads = dict(
        norm_w=dnorm_w[0], w_in_p=dw_in_p, ssd_conv_w=dconv_w[:SSD_CONV], ssd_conv_b=dconv_b[0],
        ssd_dt_bias=ddtb[0, :SSD_HEADS], ssd_a_log=da_log[0, :SSD_HEADS], ssd_d=dd[0, :SSD_HEADS],
        ssd_norm_w=dssd_norm_w[0], attn_sinks=dsinks[0, :ATTN_Q_HEADS], conf_dw_w=ddw_w[:CONF_KERNEL],
        conf_dw_b=ddw_b[0], conf_ln_w=dln_w[0], conf_ln_b=dln_b[0], w_out=dw_out)
    return dx_in, grads


def _local_step(x, target, layer_params, final_norm_w):
    nbatch, seq, d = x.shape
    xt = x.reshape(nbatch * seq, d)
    saved = []
    for li, p in enumerate(layer_params):
        xt, s = _layer_fwd(xt, p, nbatch, seq, f"l{li}")
        saved.append(s)
    loss, dx, dfinal = _loss_head(xt, target.reshape(nbatch * seq, d), final_norm_w.reshape(1, d), name="loss_head")
    grads = [None] * len(layer_params)
    for li in reversed(range(len(layer_params))):
        dx, grads[li] = _layer_bwd(dx, layer_params[li], saved[li], nbatch, seq, f"l{li}")
    return loss[0, 0], dx.reshape(nbatch, seq, d), grads, dfinal[0]


MESH = pl.DeviceIdType.MESH
N_CHIPS = 4
ANY = pl.BlockSpec(memory_space=pl.ANY)


def _mesh_pos():
    return lax.axis_index("x"), lax.axis_index("y"), lax.axis_index("c")


def _other_chips(x, y):
    return [(1 - x, y), (x, 1 - y), (1 - x, 1 - y)]


def _gather_weights(big, small, name):
    nbig, nsmall = len(big), len(small)
    n_ici = 3 * (nbig + nsmall)
    n_fwd = 3 * nbig

    def body(*refs):
        ins = refs[:nbig + nsmall]
        outs = refs[nbig + nsmall:2 * (nbig + nsmall)]
        send_sems, recv_sems = refs[2 * (nbig + nsmall):]
        x, y, c = _mesh_pos()
        me = 2 * x + y
        sibling = (x, y, 1 - c)
        chips = _other_chips(x, y)

        def ici(a, j, origin, dest):
            if a < nbig:
                src = ins[a].at[c] if origin is None else outs[a].at[origin, c]
                dst = outs[a].at[me if origin is None else origin, c]
            else:
                src = ins[a] if origin is None else outs[a].at[origin]
                dst = outs[a].at[me if origin is None else origin]
            k = a * 3 + j
            return pltpu.make_async_remote_copy(src_ref=src, dst_ref=dst, send_sem=send_sems.at[k],
                                                recv_sem=recv_sems.at[k], device_id=dest, device_id_type=MESH)

        def fwd(a, j, origin, half):
            k = n_ici + a * 3 + j
            ref = outs[a].at[origin, half]
            return pltpu.make_async_remote_copy(src_ref=ref, dst_ref=ref, send_sem=send_sems.at[k],
                                                recv_sem=recv_sems.at[k], device_id=sibling, device_id_type=MESH)

        sends = []
        for j, (px, py) in enumerate(chips):
            for a in range(nbig + nsmall):
                cp = ici(a, j, None, (px, py, c))
                cp.start()
                sends.append(cp)
        for j, (px, py) in enumerate(chips):
            origin = 2 * px + py
            for a in range(nbig):
                ici(a, j, origin, (px, py, c)).wait_recv()
                cp = fwd(a, j, origin, c)
                cp.start()
                sends.append(cp)
        for j, (px, py) in enumerate(chips):
            origin = 2 * px + py
            for a in range(nbig, nbig + nsmall):
                ici(a, j, origin, (px, py, c)).wait_recv()
            for a in range(nbig):
                fwd(a, j, origin, 1 - c).wait_recv()
        for cp in sends:
            cp.wait_send()

    out_shape = tuple(jax.ShapeDtypeStruct((N_CHIPS,) + a.shape, a.dtype) for a in list(big) + list(small))
    return pl.pallas_call(
        body, name=name, out_shape=out_shape,
        in_specs=[ANY] * (nbig + nsmall), out_specs=tuple([ANY] * (nbig + nsmall)),
        scratch_shapes=[pltpu.SemaphoreType.DMA((n_ici + n_fwd,)), pltpu.SemaphoreType.DMA((n_ici + n_fwd,))],
    )(*big, *small)


def _pair_swap_halves(arrs, name):
    n = len(arrs)

    def body(*refs):
        ins, outs = refs[:n], refs[n:2 * n]
        send_sems, recv_sems = refs[2 * n:]
        x, y, c = _mesh_pos()
        cps = [pltpu.make_async_remote_copy(src_ref=ins[a].at[1 - c], dst_ref=outs[a], send_sem=send_sems.at[a],
                                            recv_sem=recv_sems.at[a], device_id=(x, y, 1 - c), device_id_type=MESH)
               for a in range(n)]
        for cp in cps:
            cp.start()
        for cp in cps:
            cp.wait()

    return pl.pallas_call(
        body, name=name, out_shape=tuple(jax.ShapeDtypeStruct(a.shape[1:], a.dtype) for a in arrs),
        in_specs=[ANY] * n, out_specs=tuple([ANY] * n),
        scratch_shapes=[pltpu.SemaphoreType.DMA((n,)), pltpu.SemaphoreType.DMA((n,))],
    )(*arrs)


def _chip_scatter(arrs, name):
    n = len(arrs)

    def body(*refs):
        ins, outs = refs[:n], refs[n:2 * n]
        send_sems, recv_sems = refs[2 * n:]
        x, y, c = _mesh_pos()
        me = 2 * x + y
        cps = []
        for j, (px, py) in enumerate(_other_chips(x, y)):
            for a in range(n):
                cps.append(pltpu.make_async_remote_copy(
                    src_ref=ins[a].at[2 * px + py], dst_ref=outs[a].at[me], send_sem=send_sems.at[a * 3 + j],
                    recv_sem=recv_sems.at[a * 3 + j], device_id=(px, py, c), device_id_type=MESH))
        for cp in cps:
            cp.start()
        for cp in cps:
            cp.wait()

    return pl.pallas_call(
        body, name=name, out_shape=tuple(jax.ShapeDtypeStruct(a.shape, a.dtype) for a in arrs),
        in_specs=[ANY] * n, out_specs=tuple([ANY] * n),
        scratch_shapes=[pltpu.SemaphoreType.DMA((3 * n,)), pltpu.SemaphoreType.DMA((3 * n,))],
    )(*arrs)


def _pair_gather(arrs, name):
    n = len(arrs)

    def body(*refs):
        ins, outs = refs[:n], refs[n:2 * n]
        send_sems, recv_sems = refs[2 * n:]
        x, y, c = _mesh_pos()
        cps = [pltpu.make_async_remote_copy(src_ref=ins[a], dst_ref=outs[a].at[c], send_sem=send_sems.at[a],
                                            recv_sem=recv_sems.at[a], device_id=(x, y, 1 - c), device_id_type=MESH)
               for a in range(n)]
        for cp in cps:
            cp.start()
        for cp in cps:
            cp.wait()

    return pl.pallas_call(
        body, name=name, out_shape=tuple(jax.ShapeDtypeStruct((2,) + a.shape, a.dtype) for a in arrs),
        in_specs=[ANY] * n, out_specs=tuple([ANY] * n),
        scratch_shapes=[pltpu.SemaphoreType.DMA((n,)), pltpu.SemaphoreType.DMA((n,))],
    )(*arrs)


N_DEV = 8


def _allreduce_small(pack, name):
    r = pack.shape[0]

    def body(p_ref, o_ref, land, send_sems, recv_sems):
        x, y, c = _mesh_pos()
        me = 4 * x + 2 * y + c
        cps = []
        for k in range(1, N_DEV):
            peer = (x ^ (k >> 2), y ^ ((k >> 1) & 1), c ^ (k & 1))
            cps.append(pltpu.make_async_remote_copy(src_ref=p_ref, dst_ref=land.at[me], send_sem=send_sems.at[k - 1],
                                                    recv_sem=recv_sems.at[k - 1], device_id=peer, device_id_type=MESH))
        for cp in cps:
            cp.start()
        land[me] = p_ref[...]
        for cp in cps:
            cp.wait()
        total = land[0]
        for d in range(1, N_DEV):
            total = total + land[d]
        o_ref[...] = total

    vm = pl.BlockSpec(memory_space=pltpu.VMEM)
    return pl.pallas_call(
        body, name=name, out_shape=jax.ShapeDtypeStruct(pack.shape, F32),
        in_specs=[vm], out_specs=vm,
        scratch_shapes=[pltpu.VMEM((N_DEV, r, LANES), F32), pltpu.SemaphoreType.DMA((N_DEV - 1,)),
                        pltpu.SemaphoreType.DMA((N_DEV - 1,))],
    )(pack)


BIG_ROWS = 128


def _cast_mxu(w, name):
    nl, r, cdim = w.shape
    tr = BIG_ROWS

    def body(w_ref, o_ref):
        o_ref[...] = w_ref[...].astype(o_ref.dtype)

    blk = pl.BlockSpec((None, tr, cdim), lambda l, i: (l, i, 0))
    return pl.pallas_call(
        body, name=name, out_shape=jax.ShapeDtypeStruct(w.shape, MXU_DTYPE),
        grid=(nl, r // tr), in_specs=[blk], out_specs=blk,
        compiler_params=_params(("parallel", "parallel")),
    )(w)


def _pair_sum(parts, sib, which, out_dtype, name):
    _, k, r, cdim = parts.shape
    tr = BIG_ROWS

    def body(sel_ref, p_ref, s_ref, o_ref):
        o_ref[...] = (p_ref[...] + s_ref[...]).astype(o_ref.dtype)

    grid_spec = pltpu.PrefetchScalarGridSpec(
        num_scalar_prefetch=1, grid=(k, r // tr),
        in_specs=[pl.BlockSpec((None, None, tr, cdim), lambda l, i, sel: (sel[0], l, i, 0)),
                  pl.BlockSpec((None, tr, cdim), lambda l, i, sel: (l, i, 0))],
        out_specs=pl.BlockSpec((None, tr, cdim), lambda l, i, sel: (l, i, 0)))
    return pl.pallas_call(
        body, name=name, out_shape=jax.ShapeDtypeStruct((k, r, cdim), out_dtype), grid_spec=grid_spec,
        compiler_params=_params(("parallel", "parallel")),
    )(which.reshape(1).astype(jnp.int32), parts, sib)


def _sum_lead(parts, name):
    k, r, cdim = parts.shape
    tr = BIG_ROWS

    def body(p_ref, o_ref):
        total = p_ref[0].astype(F32)
        for a in range(1, k):
            total = total + p_ref[a].astype(F32)
        o_ref[...] = total

    return pl.pallas_call(
        body, name=name, out_shape=jax.ShapeDtypeStruct((r, cdim), F32),
        grid=(r // tr,), in_specs=[pl.BlockSpec((k, tr, cdim), lambda i: (0, i, 0))],
        out_specs=pl.BlockSpec((tr, cdim), lambda i: (i, 0)),
        compiler_params=_params(("parallel",)),
    )(parts)


def _adam_math(w, g, m, v):
    m2 = ADAM_B1 * m + (1.0 - ADAM_B1) * g
    v2 = ADAM_B2 * v + (1.0 - ADAM_B2) * (g * g)
    m_hat = m2 / (1.0 - ADAM_B1 ** ADAM_STEP)
    v_hat = v2 / (1.0 - ADAM_B2 ** ADAM_STEP)
    delta = -ADAM_LR * (m_hat / (jnp.sqrt(v_hat) + ADAM_EPS) + ADAM_WD * w)
    return delta, m2, v2


def _adam_big(w, g, m, v, name):
    nl, r, cdim = w.shape
    tr = BIG_ROWS

    def body(w_ref, g_ref, m_ref, v_ref, d_ref, mo_ref, vo_ref):
        delta, m2, v2 = _adam_math(w_ref[...], g_ref[...], m_ref[...], v_ref[...])
        d_ref[...] = delta
        mo_ref[...] = m2
        vo_ref[...] = v2

    blk = pl.BlockSpec((None, tr, cdim), lambda l, i: (l, i, 0))
    shp = jax.ShapeDtypeStruct(w.shape, F32)
    return pl.pallas_call(
        body, name=name, out_shape=(shp, shp, shp),
        grid=(nl, r // tr), in_specs=[blk] * 4, out_specs=(blk, blk, blk),
        compiler_params=_params(("parallel", "parallel")),
    )(w, g, m, v)


def _adam_small(w, g, m, v, name):
    def body(w_ref, g_ref, m_ref, v_ref, d_ref, mo_ref, vo_ref):
        delta, m2, v2 = _adam_math(w_ref[...], g_ref[...], m_ref[...], v_ref[...])
        d_ref[...] = delta
        mo_ref[...] = m2
        vo_ref[...] = v2

    shp = jax.ShapeDtypeStruct(w.shape, F32)
    vm = pl.BlockSpec(memory_space=pltpu.VMEM)
    return pl.pallas_call(body, name=name, out_shape=(shp, shp, shp), in_specs=[vm] * 4, out_specs=(vm, vm, vm))(
        w, g, m, v)


def _pack(arrays):
    rows = []
    for a in arrays:
        flat = a.reshape(-1)
        pad = (-flat.shape[0]) % LANES
        if pad:
            flat = jnp.concatenate([flat, jnp.zeros((pad,), flat.dtype)])
        rows.append(flat.reshape(-1, LANES))
    out = jnp.concatenate(rows, axis=0)
    pad = (-out.shape[0]) % SUBLANES
    if pad:
        out = jnp.concatenate([out, jnp.zeros((pad, LANES), out.dtype)], axis=0)
    return out


def _unpack(pack, shapes):
    outs, row = [], 0
    for shp in shapes:
        n = int(np.prod(shp))
        nrows = -(-n // LANES)
        outs.append(pack[row:row + nrows].reshape(-1)[:n].reshape(shp))
        row += nrows
    return outs


SMALL = ["norm_w", "ssd_conv_b", "ssd_dt_bias", "ssd_a_log", "ssd_d", "ssd_norm_w", "attn_sinks",
         "conf_dw_b", "conf_ln_w", "conf_ln_b"]
WEIGHTS = ["norm_w", "w_in", "ssd_conv_w", "ssd_conv_b", "ssd_dt_bias", "ssd_a_log", "ssd_d", "ssd_norm_w",
           "attn_sinks", "conf_dw_w", "conf_dw_b", "conf_ln_w", "conf_ln_b", "w_out", "final_norm_w"]


def kernel(x, norm_w, w_in, ssd_conv_w, ssd_conv_b, ssd_dt_bias, ssd_a_log, ssd_d, ssd_norm_w, attn_sinks, conf_dw_w, conf_dw_b, conf_ln_w, conf_ln_b, w_out, final_norm_w, loss_target, m_norm_w, m_w_in, m_ssd_conv_w, m_ssd_conv_b, m_ssd_dt_bias, m_ssd_a_log, m_ssd_d, m_ssd_norm_w, m_attn_sinks, m_conf_dw_w, m_conf_dw_b, m_conf_ln_w, m_conf_ln_b, m_w_out, m_final_norm_w, v_norm_w, v_w_in, v_ssd_conv_w, v_ssd_conv_b, v_ssd_dt_bias, v_ssd_a_log, v_ssd_d, v_ssd_norm_w, v_attn_sinks, v_conf_dw_w, v_conf_dw_b, v_conf_ln_w, v_conf_ln_b, v_w_out, v_final_norm_w):
    w = dict(norm_w=norm_w, w_in=w_in, ssd_conv_w=ssd_conv_w, ssd_conv_b=ssd_conv_b, ssd_dt_bias=ssd_dt_bias,
             ssd_a_log=ssd_a_log, ssd_d=ssd_d, ssd_norm_w=ssd_norm_w, attn_sinks=attn_sinks, conf_dw_w=conf_dw_w,
             conf_dw_b=conf_dw_b, conf_ln_w=conf_ln_w, conf_ln_b=conf_ln_b, w_out=w_out, final_norm_w=final_norm_w)
    m = dict(norm_w=m_norm_w, w_in=m_w_in, ssd_conv_w=m_ssd_conv_w, ssd_conv_b=m_ssd_conv_b,
             ssd_dt_bias=m_ssd_dt_bias, ssd_a_log=m_ssd_a_log, ssd_d=m_ssd_d, ssd_norm_w=m_ssd_norm_w,
             attn_sinks=m_attn_sinks, conf_dw_w=m_conf_dw_w, conf_dw_b=m_conf_dw_b, conf_ln_w=m_conf_ln_w,
             conf_ln_b=m_conf_ln_b, w_out=m_w_out, final_norm_w=m_final_norm_w)
    v = dict(norm_w=v_norm_w, w_in=v_w_in, ssd_conv_w=v_ssd_conv_w, ssd_conv_b=v_ssd_conv_b,
             ssd_dt_bias=v_ssd_dt_bias, ssd_a_log=v_ssd_a_log, ssd_d=v_ssd_d, ssd_norm_w=v_ssd_norm_w,
             attn_sinks=v_attn_sinks, conf_dw_w=v_conf_dw_w, conf_dw_b=v_conf_dw_b, conf_ln_w=v_conf_ln_w,
             conf_ln_b=v_conf_ln_b, w_out=v_w_out, final_norm_w=v_final_norm_w)
    depth = w_in.shape[0]
    me = 2 * lax.axis_index("x") + lax.axis_index("y")

    assert depth == 2
    own = [_cast_mxu(w_in, name="cast_w_in"), _cast_mxu(w_out, name="cast_w_out"), ssd_conv_w, conf_dw_w]
    gathered = _gather_weights(own[:2], own[2:], name="gather_weights")
    g_in, g_out, g_conv, g_dw = [lax.dynamic_update_index_in_dim(g_all, mine, me, 0)
                                 for g_all, mine in zip(gathered, own)]
    layer_params = []
    for li in range(depth):
        w_in_p = _padded_from_chips([g_in[p, li] for p in range(N_CHIPS)])
        w_out_full = jnp.concatenate([g_out[p, li] for p in range(N_CHIPS)], axis=0)
        conv_full = jnp.concatenate([g_conv[p, li] for p in range(N_CHIPS)], axis=1)
        dw_full = jnp.concatenate([g_dw[p, li] for p in range(N_CHIPS)], axis=1)
        layer_params.append(_layer_params(li, w_in_p, w_out_full, conv_full, dw_full, w))

    loss, grad_x, grads, dfinal = _local_step(x, loss_target, layer_params, final_norm_w)

    small_list = [grads[li][n] for li in range(depth) for n in SMALL]
    small_list += [grads[li][n] for li in range(depth) for n in ("ssd_conv_w", "conf_dw_w")]
    small_list += [dfinal, loss.reshape(1)]
    small_shapes = [a.shape for a in small_list]
    reduced = _unpack(_allreduce_small(_pack(small_list), name="allreduce_small"), small_shapes)
    ns = len(SMALL)
    g = {n: jnp.stack([reduced[li * ns + i] for li in range(depth)]) for i, n in enumerate(SMALL)}
    conv_w_cols, dw_w_cols = ssd_conv_w.shape[2], conf_dw_w.shape[2]
    g["ssd_conv_w"] = jnp.stack([lax.dynamic_slice_in_dim(reduced[depth * ns + 2 * li], me * conv_w_cols,
                                                          conv_w_cols, axis=1) for li in range(depth)])
    g["conf_dw_w"] = jnp.stack([lax.dynamic_slice_in_dim(reduced[depth * ns + 2 * li + 1], me * dw_w_cols,
                                                         dw_w_cols, axis=1) for li in range(depth)])
    g["final_norm_w"] = reduced[-2]
    loss_total = reduced[-1][0]

    cols = w_in.shape[2]
    rows_out = w_out.shape[1]
    p_in = jnp.stack([jnp.stack([_chip_part_from_padded(grads[li]["w_in_p"], p, cols) for p in range(N_CHIPS)])
                      for li in range(depth)])
    p_out = jnp.stack([grads[li]["w_out"].reshape(N_CHIPS, rows_out, D_MODEL) for li in range(depth)])
    c = lax.axis_index("c")
    sib_in, sib_out = _pair_swap_halves([p_in, p_out], name="grad_pair_swap")
    s_in = _pair_sum(p_in, sib_in, c, MXU_DTYPE, name="grad_pair_sum_in")
    s_out = _pair_sum(p_out, sib_out, c, MXU_DTYPE, name="grad_pair_sum_out")
    r_in, r_out = _chip_scatter([s_in, s_out], name="grad_chip_scatter")
    r_in = lax.dynamic_update_index_in_dim(r_in, lax.dynamic_index_in_dim(s_in, me, 0, keepdims=False), me, 0)
    r_out = lax.dynamic_update_index_in_dim(r_out, lax.dynamic_index_in_dim(s_out, me, 0, keepdims=False), me, 0)
    t_in = _sum_lead(r_in, name="grad_chip_sum_in")
    t_out = _sum_lead(r_out, name="grad_chip_sum_out")
    g_w_in, g_w_out = _pair_gather([t_in, t_out], name="grad_pair_gather")
    g_w_in = lax.dynamic_update_index_in_dim(g_w_in, t_in, c, 0)
    g_w_out = lax.dynamic_update_index_in_dim(g_w_out, t_out, c, 0)

    outs_g, outs_d, outs_m, outs_v = {"w_in": g_w_in, "w_out": g_w_out}, {}, {}, {}
    outs_d["w_in"], outs_m["w_in"], outs_v["w_in"] = _adam_big(w_in, g_w_in, m_w_in, v_w_in, name="adam_w_in")
    outs_d["w_out"], outs_m["w_out"], outs_v["w_out"] = _adam_big(w_out, g_w_out, m_w_out, v_w_out,
                                                                  name="adam_w_out")
    small_names = [n for n in WEIGHTS if n not in ("w_in", "w_out")]
    d_p, m_p, v_p = _adam_small(_pack([w[n] for n in small_names]), _pack([g[n] for n in small_names]),
                                _pack([m[n] for n in small_names]), _pack([v[n] for n in small_names]),
                                name="adam_small")
    shapes = [w[n].shape for n in small_names]
    for n, dn, mn, vn in zip(small_names, _unpack(d_p, shapes), _unpack(m_p, shapes), _unpack(v_p, shapes)):
        outs_g[n], outs_d[n], outs_m[n], outs_v[n] = g[n], dn, mn, vn
    return (loss_total, grad_x, *[outs_g[n] for n in WEIGHTS], *[outs_d[n] for n in WEIGHTS],
            *[outs_m[n] for n in WEIGHTS], *[outs_v[n] for n in WEIGHTS])
```

```python
import functools
import math

import jax
import jax.numpy as jnp
import numpy as np
from jax import lax
from jax.experimental import pallas as pl
from jax.experimental.pallas import tpu as pltpu

F32 = jnp.float32
BF16 = jnp.bfloat16
MXU_DTYPE = BF16

D_MODEL = 1024
DEPTH = 2
SSD_HEADS = 16
SSD_HEAD_DIM = 64
SSD_STATE = 128
SSD_CONV = 4
CHUNK = 128
SSD_CONV_DIM = 1536
ATTN_HEAD_DIM = 64
ATTN_Q_HEADS = 8
WINDOW = 128
CONF_WIDTH = 512
CONF_KERNEL = 31
MIX_WIDTH = 2048
D_IN_PROJ = 5392
EPS = 1e-5

ADAM_LR = 0.001
ADAM_B1 = 0.9
ADAM_B2 = 0.999
ADAM_EPS = 1e-08
ADAM_WD = 0.01
ADAM_STEP = 10

LANES = 128
SUBLANES = 8
VMEM_LIMIT = 48 * 1024 * 1024

NP = 5632
OFF_ZA, OFF_Q, OFF_K, OFF_V, OFF_DT = 0, 512, 1024, 1152, 1280
ATTN_GROUP = 1536
OFF_XBC = 1536
OFF_CONF, OFF_ZC = 3072, 4096
OFF_ZS = 4608
SECTIONS = ((0, 1024, OFF_ZS), (1024, 1536, OFF_ZA), (1536, 2048, OFF_ZC), (2048, 3584, OFF_XBC),
            (3584, 3600, OFF_DT), (3600, 4368, OFF_Q), (4368, 5392, OFF_CONF))

YCAT_ATTN, YCAT_CONF = 1024, 1536
ANY = pl.BlockSpec(memory_space=pl.ANY)

NN = (((1,), (0,)), ((), ()))
NT = (((1,), (1,)), ((), ()))
TN = (((0,), (0,)), ((), ()))


def _params(sem):
    return pltpu.CompilerParams(dimension_semantics=sem, vmem_limit_bytes=VMEM_LIMIT)


def _dot(a, b, dims=NN):
    return lax.dot_general(a.astype(MXU_DTYPE), b.astype(MXU_DTYPE), dims, preferred_element_type=F32)


def _split_bf16(a, passes):
    pieces = []
    r = a
    for _ in range(passes):
        p = r.astype(BF16)
        pieces.append(p)
        r = r - p.astype(F32)
    return pieces


def _xdot(a, sel, dims=NN, passes=3):
    out = None
    for p in _split_bf16(a, passes):
        t = lax.dot_general(p, sel, dims, preferred_element_type=F32)
        out = t if out is None else out + t
    return out


def _xdot_r(sel, b, dims=NN, passes=3):
    out = None
    for p in _split_bf16(b, passes):
        t = lax.dot_general(sel, p, dims, preferred_element_type=F32)
        out = t if out is None else out + t
    return out


def _sigmoid(x):
    return 1.0 / (1.0 + jnp.exp(-x))


def _silu(x):
    return x * _sigmoid(x)


def _dsilu(x):
    s = _sigmoid(x)
    return s * (1.0 + x * (1.0 - s))


def _softplus(x):
    return jnp.maximum(x, 0.0) + jnp.log(1.0 + jnp.exp(-jnp.abs(x)))


def _rowsum8(x):
    r, c = x.shape
    return jnp.sum(x.reshape(r // SUBLANES, SUBLANES, c), axis=0)


def _iota(shape, dim):
    return lax.broadcasted_iota(jnp.int32, shape, dim)


def _matmul(a, b, form, out_dtype, tm, tn, tk, name, residual=None):
    if form == "nn":
        (m, k), n = a.shape, b.shape[1]
    elif form == "nt":
        (m, k), n = a.shape, b.shape[0]
    else:
        (k, m), n = a.shape, b.shape[1]
    tm, tn, tk = min(tm, m), min(tn, n), min(tk, k)
    assert m % tm == 0 and n % tn == 0 and k % tk == 0, (name, m, n, k, tm, tn, tk)
    if form == "nn":
        a_spec = pl.BlockSpec((tm, tk), lambda i, j, s: (i, s))
        b_spec = pl.BlockSpec((tk, tn), lambda i, j, s: (s, j))
        dims = NN
    elif form == "nt":
        (m, k), n = a.shape, b.shape[0]
        a_spec = pl.BlockSpec((tm, tk), lambda i, j, s: (i, s))
        b_spec = pl.BlockSpec((tn, tk), lambda i, j, s: (j, s))
        dims = NT
    else:
        (k, m), n = a.shape, b.shape[1]
        a_spec = pl.BlockSpec((tk, tm), lambda i, j, s: (s, i))
        b_spec = pl.BlockSpec((tk, tn), lambda i, j, s: (s, j))
        dims = TN
    nk = k // tk
    has_res = residual is not None

    def body_single(a_ref, b_ref, *rest):
        o = _dot(a_ref[...], b_ref[...], dims)
        if has_res:
            o = o + rest[0][...]
        rest[-1][...] = o.astype(out_dtype)

    def body(a_ref, b_ref, *rest):
        if has_res:
            r_ref, o_ref, acc = rest
        else:
            o_ref, acc = rest
        s = pl.program_id(2)

        @pl.when(s == 0)
        def _():
            acc[...] = jnp.zeros_like(acc)

        acc[...] += _dot(a_ref[...], b_ref[...], dims)

        @pl.when(s == nk - 1)
        def _():
            o = acc[...]
            if has_res:
                o = o + r_ref[...]
            o_ref[...] = o.astype(out_dtype)

    in_specs = [a_spec, b_spec]
    args = [a, b]
    if has_res:
        in_specs.append(pl.BlockSpec((tm, tn), lambda i, j, s: (i, j)))
        args.append(residual)
    return pl.pallas_call(
        body_single if nk == 1 else body, name=name,
        out_shape=jax.ShapeDtypeStruct((m, n), out_dtype),
        grid=(m // tm, n // tn, nk),
        in_specs=in_specs,
        out_specs=pl.BlockSpec((tm, tn), lambda i, j, s: (i, j)),
        scratch_shapes=[] if nk == 1 else [pltpu.VMEM((tm, tn), F32)],
        compiler_params=_params(("parallel", "parallel", "arbitrary")),
    )(*args)


ROW_TILE = 256


def _rmsnorm_fwd(x, w, name):
    t, d = x.shape
    tm = ROW_TILE

    def body(x_ref, w_ref, o_ref, ot_ref):
        xv = x_ref[...]
        rstd = lax.rsqrt(jnp.mean(xv * xv, axis=-1, keepdims=True) + EPS)
        h = xv * rstd * w_ref[...]
        o_ref[...] = h.astype(o_ref.dtype)
        ot_ref[...] = h.T.astype(ot_ref.dtype)

    return pl.pallas_call(
        body, name=name,
        out_shape=(jax.ShapeDtypeStruct((t, d), MXU_DTYPE), jax.ShapeDtypeStruct((d, t), MXU_DTYPE)),
        grid=(t // tm,),
        in_specs=[pl.BlockSpec((tm, d), lambda i: (i, 0)), pl.BlockSpec((1, d), lambda i: (0, 0))],
        out_specs=(pl.BlockSpec((tm, d), lambda i: (i, 0)), pl.BlockSpec((d, tm), lambda i: (0, i))),
        compiler_params=_params(("parallel",)),
    )(x, w)


def _rmsnorm_bwd(dh, x, w, dres, name):
    t, d = x.shape
    tm = ROW_TILE
    nt = t // tm

    def body(dh_ref, x_ref, w_ref, dr_ref, dx_ref, dw_ref, acc):
        i = pl.program_id(0)

        @pl.when(i == 0)
        def _():
            acc[...] = jnp.zeros_like(acc)

        xv = x_ref[...]
        rstd = lax.rsqrt(jnp.mean(xv * xv, axis=-1, keepdims=True) + EPS)
        xh = xv * rstd
        dhv = dh_ref[...]
        g = dhv * w_ref[...]
        dx_ref[...] = dr_ref[...] + rstd * (g - xh * jnp.mean(g * xh, axis=-1, keepdims=True))
        acc[...] += _rowsum8(dhv * xh)

        @pl.when(i == nt - 1)
        def _():
            dw_ref[...] = jnp.sum(acc[...], axis=0, keepdims=True)

    row = pl.BlockSpec((tm, d), lambda i: (i, 0))
    vec = pl.BlockSpec((1, d), lambda i: (0, 0))
    return pl.pallas_call(
        body, name=name,
        out_shape=(jax.ShapeDtypeStruct((t, d), F32), jax.ShapeDtypeStruct((1, d), F32)),
        grid=(nt,),
        in_specs=[row, row, vec, row],
        out_specs=(row, vec),
        scratch_shapes=[pltpu.VMEM((SUBLANES, d), F32)],
        compiler_params=_params(("arbitrary",)),
    )(dh, x, w, dres)


def _loss_head(xf, target, w, name):
    t, d = xf.shape
    tm = ROW_TILE
    nt = t // tm

    def body(x_ref, t_ref, w_ref, loss_ref, dx_ref, dw_ref, lacc, wacc):
        i = pl.program_id(0)

        @pl.when(i == 0)
        def _():
            lacc[...] = jnp.zeros_like(lacc)
            wacc[...] = jnp.zeros_like(wacc)

        xv = x_ref[...]
        rstd = lax.rsqrt(jnp.mean(xv * xv, axis=-1, keepdims=True) + EPS)
        xh = xv * rstd
        err = xh * w_ref[...] - t_ref[...]
        lacc[...] += jnp.sum(err * err)
        dy = err * (1.0 / d)
        g = dy * w_ref[...]
        dx_ref[...] = rstd * (g - xh * jnp.mean(g * xh, axis=-1, keepdims=True))
        wacc[...] += _rowsum8(dy * xh)

        @pl.when(i == nt - 1)
        def _():
            loss_ref[...] = lacc[...] * (0.5 / d)
            dw_ref[...] = jnp.sum(wacc[...], axis=0, keepdims=True)

    row = pl.BlockSpec((tm, d), lambda i: (i, 0))
    vec = pl.BlockSpec((1, d), lambda i: (0, 0))
    return pl.pallas_call(
        body, name=name,
        out_shape=(jax.ShapeDtypeStruct((SUBLANES, LANES), F32), jax.ShapeDtypeStruct((t, d), F32),
                   jax.ShapeDtypeStruct((1, d), F32)),
        grid=(nt,),
        in_specs=[row, row, vec],
        out_specs=(pl.BlockSpec((SUBLANES, LANES), lambda i: (0, 0)), row, vec),
        scratch_shapes=[pltpu.VMEM((SUBLANES, LANES), F32), pltpu.VMEM((SUBLANES, d), F32)],
        compiler_params=_params(("arbitrary",)),
    )(xf, target, w)


def _glu_fwd(proj, name):
    t = proj.shape[0]
    tm, cw = ROW_TILE, CONF_WIDTH

    def body(a_ref, g_ref, o_ref):
        o_ref[...] = a_ref[...] * _sigmoid(g_ref[...])

    return pl.pallas_call(
        body, name=name,
        out_shape=jax.ShapeDtypeStruct((t, cw), F32),
        grid=(t // tm,),
        in_specs=[pl.BlockSpec((tm, cw), lambda i: (i, OFF_CONF // cw)),
                  pl.BlockSpec((tm, cw), lambda i: (i, OFF_CONF // cw + 1))],
        out_specs=pl.BlockSpec((tm, cw), lambda i: (i, 0)),
        compiler_params=_params(("parallel",)),
    )(proj, proj)


def _glu_bwd(dc0, proj, dproj, name):
    t = proj.shape[0]
    tm, cw = ROW_TILE, CONF_WIDTH

    def body(d_ref, a_ref, g_ref, _, o_ref):
        s = _sigmoid(g_ref[...])
        dv = d_ref[...]
        o_ref[:, :cw] = (dv * s).astype(o_ref.dtype)
        o_ref[:, cw:] = (dv * a_ref[...] * s * (1.0 - s)).astype(o_ref.dtype)

    return pl.pallas_call(
        body, name=name,
        out_shape=jax.ShapeDtypeStruct(dproj.shape, dproj.dtype),
        grid=(t // tm,),
        in_specs=[pl.BlockSpec((tm, cw), lambda i: (i, 0)),
                  pl.BlockSpec((tm, cw), lambda i: (i, OFF_CONF // cw)),
                  pl.BlockSpec((tm, cw), lambda i: (i, OFF_CONF // cw + 1)), ANY],
        out_specs=pl.BlockSpec((tm, 2 * cw), lambda i: (i, OFF_CONF // (2 * cw))),
        input_output_aliases={3: 0},
        compiler_params=_params(("parallel",)),
    )(dc0, proj, proj, dproj)


def _conf_post_fwd(c1, proj, ln_w, ln_b, ycat, name):
    t = c1.shape[0]
    tm, cw = ROW_TILE, CONF_WIDTH

    def body(c_ref, z_ref, w_ref, b_ref, _, o_ref):
        cv = c_ref[...]
        xc = cv - jnp.mean(cv, axis=-1, keepdims=True)
        rstd = lax.rsqrt(jnp.mean(xc * xc, axis=-1, keepdims=True) + EPS)
        c2 = xc * rstd * w_ref[...] + b_ref[...]
        o_ref[...] = (_silu(c2) * _silu(z_ref[...])).astype(o_ref.dtype)

    vec = pl.BlockSpec((1, cw), lambda i: (0, 0))
    return pl.pallas_call(
        body, name=name,
        out_shape=jax.ShapeDtypeStruct(ycat.shape, ycat.dtype),
        grid=(t // tm,),
        in_specs=[pl.BlockSpec((tm, cw), lambda i: (i, 0)),
                  pl.BlockSpec((tm, cw), lambda i: (i, OFF_ZC // cw)), vec, vec, ANY],
        out_specs=pl.BlockSpec((tm, cw), lambda i: (i, YCAT_CONF // cw)),
        input_output_aliases={4: 0},
        compiler_params=_params(("parallel",)),
    )(c1, proj, ln_w, ln_b, ycat)


def _conf_post_bwd(dycat, c1, proj, ln_w, ln_b, dproj, name):
    t = c1.shape[0]
    tm, cw = ROW_TILE, CONF_WIDTH
    nt = t // tm

    def body(dy_ref, c_ref, z_ref, w_ref, b_ref, _, dc_ref, dz_ref, dw_ref, db_ref, wacc, bacc):
        i = pl.program_id(0)

        @pl.when(i == 0)
        def _():
            wacc[...] = jnp.zeros_like(wacc)
            bacc[...] = jnp.zeros_like(bacc)

        cv = c_ref[...]
        xc = cv - jnp.mean(cv, axis=-1, keepdims=True)
        rstd = lax.rsqrt(jnp.mean(xc * xc, axis=-1, keepdims=True) + EPS)
        xh = xc * rstd
        c2 = xh * w_ref[...] + b_ref[...]
        zv = z_ref[...]
        dy = dy_ref[...]
        dz_ref[...] = (dy * _silu(c2) * _dsilu(zv)).astype(dz_ref.dtype)
        dc2 = dy * _silu(zv) * _dsilu(c2)
        bacc[...] += _rowsum8(dc2)
        wacc[...] += _rowsum8(dc2 * xh)
        dxh = dc2 * w_ref[...]
        dc_ref[...] = rstd * (dxh - jnp.mean(dxh, axis=-1, keepdims=True)
                              - xh * jnp.mean(dxh * xh, axis=-1, keepdims=True))

        @pl.when(i == nt - 1)
        def _():
            dw_ref[...] = jnp.sum(wacc[...], axis=0, keepdims=True)
            db_ref[...] = jnp.sum(bacc[...], axis=0, keepdims=True)

    row = pl.BlockSpec((tm, cw), lambda i: (i, 0))
    vec = pl.BlockSpec((1, cw), lambda i: (0, 0))
    return pl.pallas_call(
        body, name=name,
        out_shape=(jax.ShapeDtypeStruct((t, cw), F32), jax.ShapeDtypeStruct(dproj.shape, dproj.dtype),
                   jax.ShapeDtypeStruct((1, cw), F32), jax.ShapeDtypeStruct((1, cw), F32)),
        grid=(nt,),
        in_specs=[pl.BlockSpec((tm, cw), lambda i: (i, YCAT_CONF // cw)), row,
                  pl.BlockSpec((tm, cw), lambda i: (i, OFF_ZC // cw)), vec, vec, ANY],
        out_specs=(row, pl.BlockSpec((tm, cw), lambda i: (i, OFF_ZC // cw)), vec, vec),
        input_output_aliases={5: 1},
        scratch_shapes=[pltpu.VMEM((SUBLANES, cw), F32), pltpu.VMEM((SUBLANES, cw), F32)],
        compiler_params=_params(("arbitrary",)),
    )(dycat, c1, proj, ln_w, ln_b, dproj)


GN_WIDTH = 512


def _gated_norm_fwd(y, proj, w, ycat, name):
    t = y.shape[0]
    tm, cw = ROW_TILE, GN_WIDTH

    def body(y_ref, z_ref, w_ref, _, o_ref):
        g = y_ref[...] * _silu(z_ref[...])
        rstd = lax.rsqrt(jnp.mean(g * g, axis=-1, keepdims=True) + EPS)
        o_ref[...] = (g * rstd * w_ref[...]).astype(o_ref.dtype)

    blk = pl.BlockSpec((tm, cw), lambda i, j: (i, j))
    return pl.pallas_call(
        body, name=name,
        out_shape=jax.ShapeDtypeStruct(ycat.shape, ycat.dtype),
        grid=(t // tm, y.shape[1] // cw),
        in_specs=[blk, pl.BlockSpec((tm, cw), lambda i, j: (i, OFF_ZS // cw + j)),
                  pl.BlockSpec((1, cw), lambda i, j: (0, j)), ANY],
        out_specs=blk,
        input_output_aliases={3: 0},
        compiler_params=_params(("parallel", "parallel")),
    )(y, proj, w, ycat)


def _gated_norm_bwd(dycat, y, proj, w, dproj, name):
    t = y.shape[0]
    tm, cw = ROW_TILE, GN_WIDTH
    nt = t // tm

    def body(do_ref, y_ref, z_ref, w_ref, _, dy_ref, dz_ref, dw_ref, acc):
        i = pl.program_id(1)

        @pl.when(i == 0)
        def _():
            acc[...] = jnp.zeros_like(acc)

        yv, zv, dov = y_ref[...], z_ref[...], do_ref[...]
        sz = _silu(zv)
        g = yv * sz
        rstd = lax.rsqrt(jnp.mean(g * g, axis=-1, keepdims=True) + EPS)
        gh = g * rstd
        acc[...] += _rowsum8(dov * gh)
        dgn = dov * w_ref[...]
        dg = rstd * (dgn - gh * jnp.mean(dgn * gh, axis=-1, keepdims=True))
        dy_ref[...] = dg * sz
        dz_ref[...] = (dg * yv * _dsilu(zv)).astype(dz_ref.dtype)

        @pl.when(i == nt - 1)
        def _():
            dw_ref[...] = jnp.sum(acc[...], axis=0, keepdims=True)

    blk = pl.BlockSpec((tm, cw), lambda j, i: (i, j))
    zblk = pl.BlockSpec((tm, cw), lambda j, i: (i, OFF_ZS // cw + j))
    vec = pl.BlockSpec((1, cw), lambda j, i: (0, j))
    return pl.pallas_call(
        body, name=name,
        out_shape=(jax.ShapeDtypeStruct(y.shape, F32), jax.ShapeDtypeStruct(dproj.shape, dproj.dtype),
                   jax.ShapeDtypeStruct((1, y.shape[1]), F32)),
        grid=(y.shape[1] // cw, nt),
        in_specs=[blk, blk, zblk, vec, ANY],
        out_specs=(blk, zblk, vec),
        input_output_aliases={4: 1},
        scratch_shapes=[pltpu.VMEM((SUBLANES, cw), F32)],
        compiler_params=_params(("parallel", "arbitrary")),
    )(dycat, y, proj, w, dproj)


CONV_TILE = 512
CONV_COLS = 512
CONV_SUB_ROWS = 128
CONV_SUB_COLS = LANES


def _conv_halo(k):
    return SUBLANES if k - 1 <= SUBLANES else 32


def _conv_subtiles(tm, cw):
    return [(r0, c0) for r0 in range(0, tm, CONV_SUB_ROWS) for c0 in range(0, cw, CONV_SUB_COLS)]


def _conv_use_shifted(k):
    return k > SUBLANES


def _conv_shift_scratch(k, rows, cw):
    return [pltpu.VMEM((SUBLANES - 1, rows - SUBLANES, cw), F32)] if _conv_use_shifted(k) else []


def _conv_fill_shifted(ext, sh):
    n = sh.shape[1]
    for b in range(1, SUBLANES):
        sh[b - 1] = ext[b:b + n, :]


def _conv_rows(ext, sh, start, rows, cs):
    b = start % SUBLANES
    if b == 0 or not sh:
        return ext[start:start + rows, cs]
    return sh[0][b - 1, start - b:start - b + rows, cs]


def _conv_fwd(src, col0, width, w, bias, k, seq, name):
    t = src.shape[0]
    tm, cw, halo = CONV_TILE, CONV_COLS, _conv_halo(k)
    sr, sc = CONV_SUB_ROWS, CONV_SUB_COLS
    p = k - 1
    cb0 = col0 // cw
    kp = w.shape[0]

    shifted = _conv_use_shifted(k)

    def body(x_ref, h_ref, w_ref, b_ref, o_ref, ext, *sh):
        i = pl.program_id(0)
        seq_start = (i * tm) % seq == 0
        ext[halo:, :] = x_ref[...]
        ext[:halo, :] = jnp.where(seq_start, 0.0, h_ref[...])
        if shifted:
            _conv_fill_shifted(ext, sh[0])
        for r0, c0 in _conv_subtiles(tm, cw):
            cs = slice(c0, c0 + sc)
            acc = jnp.zeros((sr, sc), F32) + b_ref[:, cs]
            for j in range(k):
                acc = acc + w_ref[j:j + 1, cs] * _conv_rows(ext, sh, r0 + halo - p + j, sr, cs)
            o_ref[r0:r0 + sr, cs] = acc

    return pl.pallas_call(
        body, name=name,
        out_shape=jax.ShapeDtypeStruct((t, width), F32),
        grid=(t // tm, width // cw),
        in_specs=[pl.BlockSpec((tm, cw), lambda i, j: (i, cb0 + j)),
                  pl.BlockSpec((halo, cw), lambda i, j: (jnp.maximum(i * (tm // halo) - 1, 0), cb0 + j)),
                  pl.BlockSpec((kp, cw), lambda i, j: (0, j)),
                  pl.BlockSpec((1, cw), lambda i, j: (0, j))],
        out_specs=pl.BlockSpec((tm, cw), lambda i, j: (i, j)),
        scratch_shapes=[pltpu.VMEM((halo + tm, cw), F32)] + _conv_shift_scratch(k, halo + tm, cw),
        compiler_params=_params(("parallel", "parallel")),
    )(src, src, w, bias)


def _conv_bwd(dy, src, col0, width, w, k, seq, name, into=None):
    t = src.shape[0]
    tm, cw, halo = CONV_TILE, CONV_COLS, _conv_halo(k)
    sr, sc = CONV_SUB_ROWS, CONV_SUB_COLS
    p = k - 1
    cb0 = col0 // cw
    kp = w.shape[0]
    nt = t // tm
    last_halo = t // halo - 1

    shifted = _conv_use_shifted(k)

    def body(dy_ref, dn_ref, x_ref, xp_ref, w_ref, *rest):
        if into is not None:
            rest = rest[1:]
        dx_ref, dw_ref, db_ref, dyext, xext, wacc, bacc = rest[:7]
        sh = rest[7:]
        i = pl.program_id(1)
        dysh, xsh = (sh[:1], sh[1:]) if shifted else ((), ())

        @pl.when(i == 0)
        def _():
            wacc[...] = jnp.zeros_like(wacc)
            bacc[...] = jnp.zeros_like(bacc)

        seq_start = (i * tm) % seq == 0
        seq_end = ((i + 1) * tm) % seq == 0
        dyext[:tm, :] = dy_ref[...]
        dyext[tm:, :] = jnp.where(seq_end, 0.0, dn_ref[...])
        xext[halo:, :] = x_ref[...]
        xext[:halo, :] = jnp.where(seq_start, 0.0, xp_ref[...])
        if shifted:
            _conv_fill_shifted(dyext, dysh[0])
            _conv_fill_shifted(xext, xsh[0])
        for r0, c0 in _conv_subtiles(tm, cw):
            cs = slice(c0, c0 + sc)
            dyv = dy_ref[r0:r0 + sr, cs]
            acc = jnp.zeros((sr, sc), F32)
            for j in range(k):
                acc = acc + w_ref[j:j + 1, cs] * _conv_rows(dyext, dysh, r0 + p - j, sr, cs)
                wacc[j, :, cs] += _rowsum8(dyv * _conv_rows(xext, xsh, r0 + halo - p + j, sr, cs))
            dx_ref[r0:r0 + sr, cs] = acc.astype(dx_ref.dtype)
            bacc[:, cs] += _rowsum8(dyv)

        @pl.when(i == nt - 1)
        def _():
            dw_ref[...] = jnp.zeros_like(dw_ref)
            for j in range(k):
                dw_ref[j:j + 1, :] = jnp.sum(wacc[j], axis=0, keepdims=True)
            db_ref[...] = jnp.sum(bacc[...], axis=0, keepdims=True)

    if into is None:
        dx_shape = jax.ShapeDtypeStruct((t, width), F32)
        dx_spec = pl.BlockSpec((tm, cw), lambda j, i: (i, j))
        extra_specs, extra_args, aliases = [], [], {}
    else:
        dx_shape = jax.ShapeDtypeStruct(into.shape, into.dtype)
        dx_spec = pl.BlockSpec((tm, cw), lambda j, i: (i, cb0 + j))
        extra_specs, extra_args, aliases = [ANY], [into], {5: 0}
    return pl.pallas_call(
        body, name=name,
        out_shape=(dx_shape, jax.ShapeDtypeStruct((kp, width), F32), jax.ShapeDtypeStruct((1, width), F32)),
        grid=(width // cw, nt),
        in_specs=[pl.BlockSpec((tm, cw), lambda j, i: (i, j)),
                  pl.BlockSpec((halo, cw), lambda j, i: (jnp.minimum((i + 1) * (tm // halo), last_halo), j)),
                  pl.BlockSpec((tm, cw), lambda j, i: (i, cb0 + j)),
                  pl.BlockSpec((halo, cw), lambda j, i: (jnp.maximum(i * (tm // halo) - 1, 0), cb0 + j)),
                  pl.BlockSpec((kp, cw), lambda j, i: (0, j))] + extra_specs,
        out_specs=(dx_spec,
                   pl.BlockSpec((kp, cw), lambda j, i: (0, j)),
                   pl.BlockSpec((1, cw), lambda j, i: (0, j))),
        input_output_aliases=aliases,
        scratch_shapes=[pltpu.VMEM((tm + halo, cw), F32), pltpu.VMEM((halo + tm, cw), F32),
                        pltpu.VMEM((kp, SUBLANES, cw), F32), pltpu.VMEM((SUBLANES, cw), F32)]
        + 2 * _conv_shift_scratch(k, halo + tm, cw),
        compiler_params=_params(("parallel", "arbitrary")),
    )(dy, dy, src, src, w, *extra_args)


def _head_dup(g):
    r, c = _iota((LANES, LANES), 0), _iota((LANES, LANES), 1)
    return (r == g * ATTN_HEAD_DIM + (c & (ATTN_HEAD_DIM - 1))).astype(BF16)


def _half_mask(half):
    lane = _iota((1, LANES), 1)
    return ((lane >= half * ATTN_HEAD_DIM) & (lane < (half + 1) * ATTN_HEAD_DIM)).astype(F32)


def _band_mask(first_block):
    w = WINDOW
    qi = _iota((w, 2 * w), 0)
    kj = _iota((w, 2 * w), 1) - w
    rel = qi - kj
    return (rel >= 0) & (rel < w) & (jnp.logical_not(first_block) | (kj >= 0))


def _lane_pick(x, h):
    return jnp.sum(jnp.where(_iota(x.shape, 1) == h, x, 0.0), axis=1, keepdims=True)


def _attn_specs(nb, rev):
    w = WINDOW

    def blk(i):
        return nb - 1 - i if rev else i

    def row(b, i):
        return b * nb + blk(i)

    def prow(b, i):
        return b * nb + jnp.maximum(blk(i) - 1, 0)

    q = pl.BlockSpec((w, 512), lambda b, i: (row(b, i), OFF_Q // 512))
    kc = pl.BlockSpec((w, 128), lambda b, i: (row(b, i), OFF_K // 128))
    kp = pl.BlockSpec((w, 128), lambda b, i: (prow(b, i), OFF_K // 128))
    vc = pl.BlockSpec((w, 128), lambda b, i: (row(b, i), OFF_V // 128))
    vp = pl.BlockSpec((w, 128), lambda b, i: (prow(b, i), OFF_V // 128))
    z = pl.BlockSpec((w, 512), lambda b, i: (row(b, i), OFF_ZA // 512))
    return q, kc, kp, vc, vp, z, row


def _attn_fwd(proj, sinks, ycat, nbatch, name):
    t = proj.shape[0]
    w = WINDOW
    nb = t // nbatch // w
    scale = ATTN_HEAD_DIM ** -0.5
    q_s, kc_s, kp_s, vc_s, vp_s, z_s, row = _attn_specs(nb, False)

    def body(q_ref, kc_ref, kp_ref, vc_ref, vp_ref, z_ref, sk_ref, _, y_ref, o_ref, lse_ref):
        first = pl.program_id(1) == 0
        mask = _band_mask(first)
        kk = jnp.concatenate([kp_ref[...], kc_ref[...]], axis=0).astype(MXU_DTYPE)
        vv = jnp.concatenate([vp_ref[...], vc_ref[...]], axis=0).astype(MXU_DTYPE)
        sk = sk_ref[...]
        lse_all = jnp.zeros((w, LANES), F32)
        lane = _iota((w, LANES), 1)
        for g in range(2):
            dup = _head_dup(g)
            kkd = _dot(kk, dup).astype(MXU_DTYPE)
            vvd = _dot(vv, dup)
            for jj in range(2):
                j = 2 * g + jj
                qp = q_ref[:, j * LANES:(j + 1) * LANES]
                op = jnp.zeros((w, LANES), F32)
                for half in range(2):
                    h = 2 * j + half
                    hm = _half_mask(half)
                    s = _dot(qp * hm, kkd, NT) * scale
                    s = jnp.where(mask, s, -1e30)
                    skh = _lane_pick(sk, h)
                    m = jnp.maximum(jnp.max(s, axis=1, keepdims=True), skh)
                    den = jnp.sum(jnp.exp(s - m), axis=1, keepdims=True) + jnp.exp(skh - m)
                    lse = m + jnp.log(den)
                    pr = jnp.exp(s - lse)
                    op = op + _dot(pr, vvd * hm)
                    lse_all = jnp.where(lane == h, lse, lse_all)
                o_ref[:, j * LANES:(j + 1) * LANES] = op
                y_ref[:, j * LANES:(j + 1) * LANES] = (
                    op * _silu(z_ref[:, j * LANES:(j + 1) * LANES])).astype(y_ref.dtype)
        lse_ref[...] = lse_all

    return pl.pallas_call(
        body, name=name,
        out_shape=(jax.ShapeDtypeStruct(ycat.shape, ycat.dtype), jax.ShapeDtypeStruct((t, 512), F32),
                   jax.ShapeDtypeStruct((t, LANES), F32)),
        grid=(nbatch, nb),
        in_specs=[q_s, kc_s, kp_s, vc_s, vp_s, z_s, pl.BlockSpec((1, LANES), lambda b, i: (0, 0)), ANY],
        out_specs=(pl.BlockSpec((w, 512), lambda b, i: (row(b, i), YCAT_ATTN // 512)),
                   pl.BlockSpec((w, 512), lambda b, i: (row(b, i), 0)),
                   pl.BlockSpec((w, LANES), lambda b, i: (row(b, i), 0))),
        input_output_aliases={7: 0},
        compiler_params=_params(("parallel", "parallel")),
    )(proj, proj, proj, proj, proj, proj, sinks, ycat)


def _attn_bwd(dycat, proj, o, lse, sinks, dproj, nbatch, name):
    t = proj.shape[0]
    w = WINDOW
    nb = t // nbatch // w
    scale = ATTN_HEAD_DIM ** -0.5
    q_s, kc_s, kp_s, vc_s, vp_s, z_s, row = _attn_specs(nb, True)

    def body(dy_ref, q_ref, kc_ref, kp_ref, vc_ref, vp_ref, z_ref, o_ref, lse_ref, sk_ref, _,
             grp_ref, dsk_ref, kcarry, vcarry, sacc):
        b, i = pl.program_id(0), pl.program_id(1)

        @pl.when((b == 0) & (i == 0))
        def _():
            sacc[...] = jnp.zeros_like(sacc)

        @pl.when(i == 0)
        def _():
            kcarry[...] = jnp.zeros_like(kcarry)
            vcarry[...] = jnp.zeros_like(vcarry)

        first = i == nb - 1
        mask = _band_mask(first)
        kk = jnp.concatenate([kp_ref[...], kc_ref[...]], axis=0).astype(MXU_DTYPE)
        vv = jnp.concatenate([vp_ref[...], vc_ref[...]], axis=0).astype(MXU_DTYPE)
        sk = sk_ref[...]
        lse_all = lse_ref[...]
        lane1 = _iota((1, LANES), 1)
        dkk = jnp.zeros((2 * w, LANES), F32)
        dvv = jnp.zeros((2 * w, LANES), F32)
        dsk = jnp.zeros((1, LANES), F32)
        for g in range(2):
            dup = _head_dup(g)
            kkd = _dot(kk, dup).astype(MXU_DTYPE)
            vvd = _dot(vv, dup).astype(MXU_DTYPE)
            dkd = jnp.zeros((2 * w, LANES), F32)
            dvd = jnp.zeros((2 * w, LANES), F32)
            for jj in range(2):
                j = 2 * g + jj
                cols = slice(j * LANES, (j + 1) * LANES)
                qp, zp, ov, dy = q_ref[:, cols], z_ref[:, cols], o_ref[:, cols], dy_ref[:, cols]
                grp_ref[:, OFF_ZA + j * LANES:OFF_ZA + (j + 1) * LANES] = (dy * ov * _dsilu(zp)).astype(grp_ref.dtype)
                do = dy * _silu(zp)
                dq = jnp.zeros((w, LANES), F32)
                for half in range(2):
                    h = 2 * j + half
                    hm = _half_mask(half)
                    qh = qp * hm
                    doh = do * hm
                    delta = jnp.sum(doh * ov, axis=1, keepdims=True)
                    lse_h = _lane_pick(lse_all, h)
                    s = _dot(qh, kkd, NT) * scale
                    s = jnp.where(mask, s, -1e30)
                    pr = jnp.exp(s - lse_h)
                    dp = _dot(doh, vvd, NT)
                    ds = pr * (dp - delta)
                    dq = dq + _dot(ds, kkd) * hm * scale
                    dkd = dkd + _dot(ds, qh, TN) * scale
                    dvd = dvd + _dot(pr, doh, TN)
                    psink = jnp.exp(_lane_pick(sk, h) - lse_h)
                    dsk = dsk - jnp.where(lane1 == h, jnp.sum(psink * delta), 0.0)
                grp_ref[:, OFF_Q + j * LANES:OFF_Q + (j + 1) * LANES] = dq.astype(grp_ref.dtype)
            dkk = dkk + _xdot(dkd, dup, NT, passes=2)
            dvv = dvv + _xdot(dvd, dup, NT, passes=2)
        grp_ref[:, OFF_K:OFF_K + LANES] = (dkk[w:, :] + kcarry[...]).astype(grp_ref.dtype)
        grp_ref[:, OFF_V:OFF_V + LANES] = (dvv[w:, :] + vcarry[...]).astype(grp_ref.dtype)
        grp_ref[:, OFF_DT:] = jnp.zeros((w, ATTN_GROUP - OFF_DT), grp_ref.dtype)
        kcarry[...] = dkk[:w, :]
        vcarry[...] = dvv[:w, :]
        sacc[...] += dsk

        @pl.when((b == nbatch - 1) & (i == nb - 1))
        def _():
            dsk_ref[...] = sacc[...]

    return pl.pallas_call(
        body, name=name,
        out_shape=(jax.ShapeDtypeStruct(dproj.shape, dproj.dtype), jax.ShapeDtypeStruct((1, LANES), F32)),
        grid=(nbatch, nb),
        in_specs=[pl.BlockSpec((w, 512), lambda b, i: (row(b, i), YCAT_ATTN // 512)),
                  q_s, kc_s, kp_s, vc_s, vp_s, z_s,
                  pl.BlockSpec((w, 512), lambda b, i: (row(b, i), 0)),
                  pl.BlockSpec((w, LANES), lambda b, i: (row(b, i), 0)),
                  pl.BlockSpec((1, LANES), lambda b, i: (0, 0)), ANY],
        out_specs=(pl.BlockSpec((w, ATTN_GROUP), lambda b, i: (row(b, i), 0)),
                   pl.BlockSpec((1, LANES), lambda b, i: (0, 0))),
        input_output_aliases={10: 0},
        scratch_shapes=[pltpu.VMEM((w, LANES), F32), pltpu.VMEM((w, LANES), F32),
                        pltpu.VMEM((1, LANES), F32)],
        compiler_params=_params(("arbitrary", "arbitrary")),
    )(dycat, proj, proj, proj, proj, proj, proj, o, lse, sinks, dproj)


SSD_WIDTH = SSD_HEADS * SSD_HEAD_DIM
GROUP_ROWS = SSD_WIDTH // 2


def _expand_mat():
    r, c = _iota((LANES, SSD_WIDTH), 0), _iota((LANES, SSD_WIDTH), 1)
    return (r == lax.shift_right_logical(c, 6)).astype(BF16)


def _expand_mat_t():
    r, c = _iota((SSD_WIDTH, LANES), 0), _iota((SSD_WIDTH, LANES), 1)
    return (c == lax.shift_right_logical(r, 6)).astype(BF16)


def _ssd_common(u_ref, dt_ref, dtb_ref, a_ref):
    q = CHUNK
    act = _silu(u_ref[...])
    xs = act[:, :SSD_WIDTH]
    bm = act[:, SSD_WIDTH:SSD_WIDTH + 256]
    cm = act[:, SSD_WIDTH + 256:]
    dtp = _softplus(dt_ref[...] + dtb_ref[...])
    a = dtp * a_ref[...]
    tril = (_iota((q, q), 0) >= _iota((q, q), 1)).astype(BF16)
    acs = _xdot_r(tril, a)
    acs_t = acs.T
    e = _expand_mat()
    dt_x = _xdot(dtp, e)
    ea = jnp.exp(_xdot(acs, e))
    a_end = jnp.sum(jnp.where(_iota(acs.shape, 0) == q - 1, acs, 0.0), axis=0, keepdims=True)
    dec = jnp.exp(_xdot(a_end - acs, e))
    a_end_col = jnp.broadcast_to(_lane_pick(acs_t, q - 1), (LANES, LANES))
    s_scale = jnp.exp(_xdot_r(_expand_mat_t(), a_end_col))
    return act, xs, bm, cm, dtp, acs, acs_t, dt_x, ea, dec, s_scale, tril


def _decay_mat(acs, acs_t, h):
    q = CHUNK
    col = _lane_pick(acs, h)
    rowv = jnp.sum(jnp.where(_iota(acs_t.shape, 0) == h, acs_t, 0.0), axis=0, keepdims=True)
    causal = _iota((q, q), 0) >= _iota((q, q), 1)
    return jnp.exp(jnp.where(causal, col - rowv, -1e30))


def _ssd_fwd(u, proj, dtb, a_neg, d_x, nbatch, name):
    t = u.shape[0]
    q = CHUNK
    nc = t // nbatch // q

    def body(u_ref, dt_ref, dtb_ref, a_ref, dx_ref, y_ref, st_ref, state):
        c = pl.program_id(1)

        @pl.when(c == 0)
        def _():
            state[...] = jnp.zeros_like(state)

        st_ref[...] = state[...]
        act, xs, bm, cm, dtp, acs, acs_t, dt_x, ea, dec, s_scale, _ = _ssd_common(u_ref, dt_ref, dtb_ref, a_ref)
        xdt = xs * dt_x
        xdec = xdt * dec
        lo, hi = _half_mask(0), _half_mask(1)
        for g in range(2):
            bg = bm[:, g * LANES:(g + 1) * LANES]
            cg = cm[:, g * LANES:(g + 1) * LANES]
            rows = slice(g * GROUP_ROWS, (g + 1) * GROUP_ROWS)
            sg = state[rows, :]
            cb = _dot(cg, bg, NT)
            yoff = _dot(cg, sg, NT)
            for j in range(4):
                pj = g * 4 + j
                cols = slice(pj * LANES, (pj + 1) * LANES)
                xp = xdt[:, cols]
                m0 = cb * _decay_mat(acs, acs_t, 2 * pj)
                m1 = cb * _decay_mat(acs, acs_t, 2 * pj + 1)
                yp = _dot(m0, xp * lo) + _dot(m1, xp * hi)
                yp = yp + yoff[:, j * LANES:(j + 1) * LANES] * ea[:, cols]
                y_ref[:, cols] = yp + dx_ref[:, cols] * xs[:, cols]
            state[rows, :] = s_scale[rows, :] * sg + _dot(xdec[:, rows], bg, TN)

    vec = pl.BlockSpec((1, LANES), lambda b, c: (0, 0))
    return pl.pallas_call(
        body, name=name,
        out_shape=(jax.ShapeDtypeStruct((t, SSD_WIDTH), F32),
                   jax.ShapeDtypeStruct((nbatch * nc * SSD_WIDTH, SSD_STATE), F32)),
        grid=(nbatch, nc),
        in_specs=[pl.BlockSpec((q, SSD_CONV_DIM), lambda b, c: (b * nc + c, 0)),
                  pl.BlockSpec((q, LANES), lambda b, c: (b * nc + c, OFF_DT // LANES)),
                  vec, vec, pl.BlockSpec((1, SSD_WIDTH), lambda b, c: (0, 0))],
        out_specs=(pl.BlockSpec((q, SSD_WIDTH), lambda b, c: (b * nc + c, 0)),
                   pl.BlockSpec((SSD_WIDTH, SSD_STATE), lambda b, c: (b * nc + c, 0))),
        scratch_shapes=[pltpu.VMEM((SSD_WIDTH, SSD_STATE), F32)],
        compiler_params=_params(("parallel", "arbitrary")),
    )(u, proj, dtb, a_neg, d_x)


def _ssd_bwd(dy, u, proj, states, dtb, a_neg, d_x, dproj, nbatch, name):
    t = u.shape[0]
    q = CHUNK
    nc = t // nbatch // q

    def body(dy_ref, u_ref, dt_ref, st_ref, dtb_ref, a_ref, dx_ref, _,
             du_ref, ddt_ref, dal_ref, dd_ref, dtbg_ref, dstate, acc_a, acc_d, acc_b):
        b, c = pl.program_id(0), pl.program_id(1)

        @pl.when((b == 0) & (c == 0))
        def _():
            acc_a[...] = jnp.zeros_like(acc_a)
            acc_d[...] = jnp.zeros_like(acc_d)
            acc_b[...] = jnp.zeros_like(acc_b)

        @pl.when(c == 0)
        def _():
            dstate[...] = jnp.zeros_like(dstate)

        act, xs, bm, cm, dtp, acs, acs_t, dt_x, ea, dec, s_scale, tril = _ssd_common(
            u_ref, dt_ref, dtb_ref, a_ref)
        xdt = xs * dt_x
        xdec = xdt * dec
        dyv = dy_ref[...]
        dye = dyv * ea
        lo, hi = _half_mask(0), _half_mask(1)
        et = _expand_mat_t()
        dxdt_parts, db_parts, dc_parts, dxst_parts, yoff_parts = [], [], [], [], []
        end_sum = jnp.zeros((LANES, LANES), F32)
        dal_diag = jnp.zeros((q, LANES), F32)
        lane_q = _iota((q, LANES), 1)
        for g in range(2):
            bg = bm[:, g * LANES:(g + 1) * LANES]
            cg = cm[:, g * LANES:(g + 1) * LANES]
            rows = slice(g * GROUP_ROWS, (g + 1) * GROUP_ROWS)
            sg = st_ref[rows, :]
            dsg = dstate[rows, :]
            cb = _dot(cg, bg, NT)
            yoff_parts.append(_dot(cg, sg, NT))
            dcb = jnp.zeros((q, q), F32)
            parts = []
            for j in range(4):
                pj = g * 4 + j
                cols = slice(pj * LANES, (pj + 1) * LANES)
                xp = xdt[:, cols]
                dy0, dy1 = dyv[:, cols] * lo, dyv[:, cols] * hi
                l0 = _decay_mat(acs, acs_t, 2 * pj)
                l1 = _decay_mat(acs, acs_t, 2 * pj + 1)
                g0, g1 = _dot(dy0, xp, NT), _dot(dy1, xp, NT)
                m0, m1 = cb * l0, cb * l1
                dcb = dcb + g0 * l0 + g1 * l1
                parts.append(_dot(m0, dy0, TN) + _dot(m1, dy1, TN))
                for hh, wmat in enumerate((g0 * m0, g1 * m1)):
                    sel = (lane_q == 2 * pj + hh).astype(F32)
                    dal_diag = dal_diag + _dot(wmat, sel) - _dot(wmat, sel, TN)
            dxst = _dot(bg, dsg, NT) * dec[:, rows]
            dxst_parts.append(dxst)
            dxdt_parts.append(jnp.concatenate(parts, axis=1) + dxst)
            dc_parts.append(_dot(dcb, bg) + _dot(dye[:, rows], sg))
            db_parts.append(_dot(dcb, cg, TN) + _dot(xdec[:, rows], dsg))
            s_next = s_scale[rows, :] * sg + _dot(xdec[:, rows], bg, TN)
            end_sum = end_sum + _xdot(dsg * s_next, et[rows, :], TN, passes=2)
            dstate[rows, :] = _dot(dye[:, rows], cg, TN) + s_scale[rows, :] * dsg
        dxdt = jnp.concatenate(dxdt_parts, axis=1)
        dxv = dx_ref[...]
        yoff = jnp.concatenate(yoff_parts, axis=1) * ea
        dalpha = dal_diag + _xdot(dyv * yoff - xdt * jnp.concatenate(dxst_parts, axis=1), et)
        end_row = jnp.sum(end_sum, axis=0, keepdims=True)
        dalpha = dalpha + jnp.where(_iota((q, LANES), 0) == q - 1, end_row, 0.0)
        da = _xdot_r(tril, dalpha, TN)
        ddtp = da * a_ref[...] + _xdot(dxdt * xs, et)
        acc_a[...] += _rowsum8(da * dtp)
        acc_d[...] += _rowsum8(_xdot(dyv * xs, et))
        ddt_raw = ddtp * _sigmoid(dt_ref[...] + dtb_ref[...])
        acc_b[...] += _rowsum8(ddt_raw)
        ddt_ref[...] = ddt_raw.astype(ddt_ref.dtype)
        dxs = dxdt * dt_x + dxv * dyv
        dact = jnp.concatenate([dxs] + db_parts + dc_parts, axis=1)
        du_ref[...] = dact * _dsilu(u_ref[...])

        @pl.when((b == nbatch - 1) & (c == nc - 1))
        def _():
            dal_ref[...] = jnp.sum(acc_a[...], axis=0, keepdims=True) * a_ref[...]
            dd_ref[...] = jnp.sum(acc_d[...], axis=0, keepdims=True)
            dtbg_ref[...] = jnp.sum(acc_b[...], axis=0, keepdims=True)

    def rowblk(b, c):
        return b * nc + (nc - 1 - c)

    vec = pl.BlockSpec((1, LANES), lambda b, c: (0, 0))
    wide = pl.BlockSpec((q, SSD_WIDTH), lambda b, c: (rowblk(b, c), 0))
    return pl.pallas_call(
        body, name=name,
        out_shape=(jax.ShapeDtypeStruct((t, SSD_CONV_DIM), F32), jax.ShapeDtypeStruct(dproj.shape, dproj.dtype),
                   jax.ShapeDtypeStruct((1, LANES), F32), jax.ShapeDtypeStruct((1, LANES), F32),
                   jax.ShapeDtypeStruct((1, LANES), F32)),
        grid=(nbatch, nc),
        in_specs=[wide,
                  pl.BlockSpec((q, SSD_CONV_DIM), lambda b, c: (rowblk(b, c), 0)),
                  pl.BlockSpec((q, LANES), lambda b, c: (rowblk(b, c), OFF_DT // LANES)),
                  pl.BlockSpec((SSD_WIDTH, SSD_STATE), lambda b, c: (rowblk(b, c), 0)),
                  vec, vec, pl.BlockSpec((1, SSD_WIDTH), lambda b, c: (0, 0)), ANY],
        out_specs=(pl.BlockSpec((q, SSD_CONV_DIM), lambda b, c: (rowblk(b, c), 0)),
                   pl.BlockSpec((q, LANES), lambda b, c: (rowblk(b, c), OFF_DT // LANES)),
                   vec, vec, vec),
        input_output_aliases={7: 1},
        scratch_shapes=[pltpu.VMEM((SSD_WIDTH, SSD_STATE), F32), pltpu.VMEM((SUBLANES, LANES), F32),
                        pltpu.VMEM((SUBLANES, LANES), F32), pltpu.VMEM((SUBLANES, LANES), F32)],
        compiler_params=_params(("arbitrary", "arbitrary")),
    )(dy, u, proj, states, dtb, a_neg, d_x, dproj)


def _pad_rows(w, rows):
    return jnp.concatenate([w, jnp.zeros((rows - w.shape[0], w.shape[1]), w.dtype)], axis=0)


def _pad_lanes(v):
    return jnp.concatenate([v, jnp.zeros((LANES - v.shape[0],), v.dtype)]).reshape(1, LANES)


def _padded_from_chips(pieces):
    cols = pieces[0].shape[-1]
    lead = pieces[0].shape[:-1]
    parts, pos = [], 0
    for lo, hi, start in sorted(SECTIONS, key=lambda s: s[2]):
        if start > pos:
            parts.append(jnp.zeros(lead + (start - pos,), pieces[0].dtype))
        pos = start + hi - lo
        while lo < hi:
            p = lo // cols
            end = min(hi, (p + 1) * cols)
            parts.append(pieces[p][..., lo - p * cols:end - p * cols])
            lo = end
    if pos < NP:
        parts.append(jnp.zeros(lead + (NP - pos,), pieces[0].dtype))
    return jnp.concatenate(parts, axis=-1)


def _chip_part_from_padded(wp, p, cols):
    lo, hi = p * cols, (p + 1) * cols
    parts = []
    for rs, re, start in SECTIONS:
        a, b = max(lo, rs), min(hi, re)
        if a < b:
            parts.append(wp[..., start + a - rs:start + b - rs])
    return jnp.concatenate(parts, axis=-1)


def _layer_params(li, w_in_p, w_out, conv_w, dw_w, small):
    return dict(
        w_in_p=w_in_p, w_out=w_out,
        conv_w=_pad_rows(conv_w, SUBLANES), dw_w=_pad_rows(dw_w, 32),
        norm_w=small["norm_w"][li].reshape(1, -1),
        conv_b=small["ssd_conv_b"][li].reshape(1, -1),
        dtb=_pad_lanes(small["ssd_dt_bias"][li]),
        a_neg=_pad_lanes(-jnp.exp(small["ssd_a_log"][li])),
        d_x=jnp.repeat(small["ssd_d"][li], SSD_HEAD_DIM).reshape(1, -1),
        ssd_norm_w=small["ssd_norm_w"][li].reshape(1, -1),
        sinks=_pad_lanes(small["attn_sinks"][li]),
        dw_b=small["conf_dw_b"][li].reshape(1, -1),
        ln_w=small["conf_ln_w"][li].reshape(1, -1),
        ln_b=small["conf_ln_b"][li].reshape(1, -1),
    )


def _layer_fwd(x, p, nbatch, seq, tag):
    h, h_t = _rmsnorm_fwd(x, p["norm_w"], name=f"rmsnorm_fwd_{tag}")
    proj = _matmul(h, p["w_in_p"], "nn", F32, 1024, 512, 1024, name=f"proj_fwd_{tag}")
    u = _conv_fwd(proj, OFF_XBC, SSD_CONV_DIM, p["conv_w"], p["conv_b"], SSD_CONV, seq, name=f"ssd_conv_fwd_{tag}")
    y, states = _ssd_fwd(u, proj, p["dtb"], p["a_neg"], p["d_x"], nbatch, name=f"ssd_fwd_{tag}")
    ycat = lax.empty((x.shape[0], MIX_WIDTH), MXU_DTYPE)
    ycat = _gated_norm_fwd(y, proj, p["ssd_norm_w"], ycat, name=f"gated_norm_fwd_{tag}")
    ycat, o, lse = _attn_fwd(proj, p["sinks"], ycat, nbatch, name=f"attn_fwd_{tag}")
    c0 = _glu_fwd(proj, name=f"glu_fwd_{tag}")
    c1 = _conv_fwd(c0, 0, CONF_WIDTH, p["dw_w"], p["dw_b"], CONF_KERNEL, seq, name=f"conf_conv_fwd_{tag}")
    ycat = _conf_post_fwd(c1, proj, p["ln_w"], p["ln_b"], ycat, name=f"conf_post_fwd_{tag}")
    x_new = _matmul(ycat, p["w_out"], "nn", F32, 1024, 512, 2048, name=f"out_fwd_{tag}", residual=x)
    return x_new, dict(x=x, h_t=h_t, proj=proj, u=u, y=y, states=states, o=o, lse=lse, c0=c0, c1=c1, ycat=ycat)


def _layer_bwd(dx_out, p, s, nbatch, seq, tag):
    proj = s["proj"]
    dycat = _matmul(dx_out, p["w_out"], "nt", F32, 1024, 1024, 1024, name=f"out_bwd_dy_{tag}")
    dw_out = _matmul(s["ycat"], dx_out, "tn", F32, 1024, 1024, 1024, name=f"out_bwd_dw_{tag}")
    dproj = lax.empty(proj.shape, MXU_DTYPE)
    dproj, dsinks = _attn_bwd(dycat, proj, s["o"], s["lse"], p["sinks"], dproj, nbatch, name=f"attn_bwd_{tag}")
    dc1, dproj, dln_w, dln_b = _conf_post_bwd(dycat, s["c1"], proj, p["ln_w"], p["ln_b"], dproj,
                                              name=f"conf_post_bwd_{tag}")
    dc0, ddw_w, ddw_b = _conv_bwd(dc1, s["c0"], 0, CONF_WIDTH, p["dw_w"], CONF_KERNEL, seq,
                                  name=f"conf_conv_bwd_{tag}")
    dproj = _glu_bwd(dc0, proj, dproj, name=f"glu_bwd_{tag}")
    dy, dproj, dssd_norm_w = _gated_norm_bwd(dycat, s["y"], proj, p["ssd_norm_w"], dproj,
                                             name=f"gated_norm_bwd_{tag}")
    du, dproj, da_log, dd, ddtb = _ssd_bwd(dy, s["u"], proj, s["states"], p["dtb"], p["a_neg"], p["d_x"], dproj,
                                           nbatch, name=f"ssd_bwd_{tag}")
    dproj, dconv_w, dconv_b = _conv_bwd(du, proj, OFF_XBC, SSD_CONV_DIM, p["conv_w"], SSD_CONV, seq,
                                        name=f"ssd_conv_bwd_{tag}", into=dproj)
    dh = _matmul(dproj, p["w_in_p"], "nt", F32, 1024, 1024, 1408, name=f"proj_bwd_dh_{tag}")
    dw_in_p = _matmul(s["h_t"], dproj, "nn", F32, 1024, 512, 4096, name=f"proj_bwd_dw_{tag}")
    dx_in, dnorm_w = _rmsnorm_bwd(dh, s["x"], p["norm_w"], dx_out, name=f"rmsnorm_bwd_{tag}")
    grads = dict(
        norm_w=dnorm_w[0], w_in_p=dw_in_p, ssd_conv_w=dconv_w[:SSD_CONV], ssd_conv_b=dconv_b[0],
        ssd_dt_bias=ddtb[0, :SSD_HEADS], ssd_a_log=da_log[0, :SSD_HEADS], ssd_d=dd[0, :SSD_HEADS],
        ssd_norm_w=dssd_norm_w[0], attn_sinks=dsinks[0, :ATTN_Q_HEADS], conf_dw_w=ddw_w[:CONF_KERNEL],
        conf_dw_b=ddw_b[0], conf_ln_w=dln_w[0], conf_ln_b=dln_b[0], w_out=dw_out)
    return dx_in, grads


def _local_step(x, target, layer_params, final_norm_w):
    nbatch, seq, d = x.shape
    xt = x.reshape(nbatch * seq, d)
    saved = []
    for li, p in enumerate(layer_params):
        xt, s = _layer_fwd(xt, p, nbatch, seq, f"l{li}")
        saved.append(s)
    loss, dx, dfinal = _loss_head(xt, target.reshape(nbatch * seq, d), final_norm_w.reshape(1, d), name="loss_head")
    grads = [None] * len(layer_params)
    for li in reversed(range(len(layer_params))):
        dx, grads[li] = _layer_bwd(dx, layer_params[li], saved[li], nbatch, seq, f"l{li}")
    return loss[0, 0], dx.reshape(nbatch, seq, d), grads, dfinal[0]


MESH = pl.DeviceIdType.MESH
N_CHIPS = 4


def _mesh_pos():
    return lax.axis_index("x"), lax.axis_index("y"), lax.axis_index("c")


def _other_chips(x, y):
    return [(1 - x, y), (x, 1 - y), (1 - x, 1 - y)]


def _gather_weights(big, small, name):
    nbig, nsmall = len(big), len(small)
    n_ici = 3 * (nbig + nsmall)
    n_fwd = 3 * nbig

    def body(*refs):
        ins = refs[:nbig + nsmall]
        outs = refs[nbig + nsmall:2 * (nbig + nsmall)]
        send_sems, recv_sems = refs[2 * (nbig + nsmall):]
        x, y, c = _mesh_pos()
        me = 2 * x + y
        sibling = (x, y, 1 - c)
        chips = _other_chips(x, y)

        def ici(a, j, origin, dest):
            if a < nbig:
                src = ins[a].at[c] if origin is None else outs[a].at[origin, c]
                dst = outs[a].at[me if origin is None else origin, c]
            else:
                src = ins[a] if origin is None else outs[a].at[origin]
                dst = outs[a].at[me if origin is None else origin]
            k = a * 3 + j
            return pltpu.make_async_remote_copy(src_ref=src, dst_ref=dst, send_sem=send_sems.at[k],
                                                recv_sem=recv_sems.at[k], device_id=dest, device_id_type=MESH)

        def fwd(a, j, origin, half):
            k = n_ici + a * 3 + j
            ref = outs[a].at[origin, half]
            return pltpu.make_async_remote_copy(src_ref=ref, dst_ref=ref, send_sem=send_sems.at[k],
                                                recv_sem=recv_sems.at[k], device_id=sibling, device_id_type=MESH)

        sends = []
        for j, (px, py) in enumerate(chips):
            for a in range(nbig + nsmall):
                cp = ici(a, j, None, (px, py, c))
                cp.start()
                sends.append(cp)
        for j, (px, py) in enumerate(chips):
            origin = 2 * px + py
            for a in range(nbig):
                ici(a, j, origin, (px, py, c)).wait_recv()
                cp = fwd(a, j, origin, c)
                cp.start()
                sends.append(cp)
        for j, (px, py) in enumerate(chips):
            origin = 2 * px + py
            for a in range(nbig, nbig + nsmall):
                ici(a, j, origin, (px, py, c)).wait_recv()
            for a in range(nbig):
                fwd(a, j, origin, 1 - c).wait_recv()
        for cp in sends:
            cp.wait_send()

    out_shape = tuple(jax.ShapeDtypeStruct((N_CHIPS,) + a.shape, a.dtype) for a in list(big) + list(small))
    return pl.pallas_call(
        body, name=name, out_shape=out_shape,
        in_specs=[ANY] * (nbig + nsmall), out_specs=tuple([ANY] * (nbig + nsmall)),
        scratch_shapes=[pltpu.SemaphoreType.DMA((n_ici + n_fwd,)), pltpu.SemaphoreType.DMA((n_ici + n_fwd,))],
    )(*big, *small)


def _pair_swap_halves(arrs, name):
    n = len(arrs)

    def body(*refs):
        ins, outs = refs[:n], refs[n:2 * n]
        send_sems, recv_sems = refs[2 * n:]
        x, y, c = _mesh_pos()
        cps = [pltpu.make_async_remote_copy(src_ref=ins[a].at[1 - c], dst_ref=outs[a], send_sem=send_sems.at[a],
                                            recv_sem=recv_sems.at[a], device_id=(x, y, 1 - c), device_id_type=MESH)
               for a in range(n)]
        for cp in cps:
            cp.start()
        for cp in cps:
            cp.wait()

    return pl.pallas_call(
        body, name=name, out_shape=tuple(jax.ShapeDtypeStruct(a.shape[1:], a.dtype) for a in arrs),
        in_specs=[ANY] * n, out_specs=tuple([ANY] * n),
        scratch_shapes=[pltpu.SemaphoreType.DMA((n,)), pltpu.SemaphoreType.DMA((n,))],
    )(*arrs)


def _chip_scatter(arrs, name):
    n = len(arrs)

    def body(*refs):
        ins, outs = refs[:n], refs[n:2 * n]
        send_sems, recv_sems = refs[2 * n:]
        x, y, c = _mesh_pos()
        me = 2 * x + y
        cps = []
        for j, (px, py) in enumerate(_other_chips(x, y)):
            for a in range(n):
                cps.append(pltpu.make_async_remote_copy(
                    src_ref=ins[a].at[2 * px + py], dst_ref=outs[a].at[me], send_sem=send_sems.at[a * 3 + j],
                    recv_sem=recv_sems.at[a * 3 + j], device_id=(px, py, c), device_id_type=MESH))
        for cp in cps:
            cp.start()
        for cp in cps:
            cp.wait()

    return pl.pallas_call(
        body, name=name, out_shape=tuple(jax.ShapeDtypeStruct(a.shape, a.dtype) for a in arrs),
        in_specs=[ANY] * n, out_specs=tuple([ANY] * n),
        scratch_shapes=[pltpu.SemaphoreType.DMA((3 * n,)), pltpu.SemaphoreType.DMA((3 * n,))],
    )(*arrs)


def _pair_gather(arrs, name):
    n = len(arrs)

    def body(*refs):
        ins, outs = refs[:n], refs[n:2 * n]
        send_sems, recv_sems = refs[2 * n:]
        x, y, c = _mesh_pos()
        cps = [pltpu.make_async_remote_copy(src_ref=ins[a], dst_ref=outs[a].at[c], send_sem=send_sems.at[a],
                                            recv_sem=recv_sems.at[a], device_id=(x, y, 1 - c), device_id_type=MESH)
               for a in range(n)]
        for cp in cps:
            cp.start()
        for cp in cps:
            cp.wait()

    return pl.pallas_call(
        body, name=name, out_shape=tuple(jax.ShapeDtypeStruct((2,) + a.shape, a.dtype) for a in arrs),
        in_specs=[ANY] * n, out_specs=tuple([ANY] * n),
        scratch_shapes=[pltpu.SemaphoreType.DMA((n,)), pltpu.SemaphoreType.DMA((n,))],
    )(*arrs)


N_DEV = 8


def _allreduce_small(pack, name):
    r = pack.shape[0]

    def body(p_ref, o_ref, land, send_sems, recv_sems):
        x, y, c = _mesh_pos()
        me = 4 * x + 2 * y + c
        cps = []
        for k in range(1, N_DEV):
            peer = (x ^ (k >> 2), y ^ ((k >> 1) & 1), c ^ (k & 1))
            cps.append(pltpu.make_async_remote_copy(src_ref=p_ref, dst_ref=land.at[me], send_sem=send_sems.at[k - 1],
                                                    recv_sem=recv_sems.at[k - 1], device_id=peer, device_id_type=MESH))
        for cp in cps:
            cp.start()
        land[me] = p_ref[...]
        for cp in cps:
            cp.wait()
        total = land[0]
        for d in range(1, N_DEV):
            total = total + land[d]
        o_ref[...] = total

    vm = pl.BlockSpec(memory_space=pltpu.VMEM)
    return pl.pallas_call(
        body, name=name, out_shape=jax.ShapeDtypeStruct(pack.shape, F32),
        in_specs=[vm], out_specs=vm,
        scratch_shapes=[pltpu.VMEM((N_DEV, r, LANES), F32), pltpu.SemaphoreType.DMA((N_DEV - 1,)),
                        pltpu.SemaphoreType.DMA((N_DEV - 1,))],
    )(pack)


BIG_ROWS = 128


def _cast_mxu(w, name):
    nl, r, cdim = w.shape
    tr = BIG_ROWS

    def body(w_ref, o_ref):
        o_ref[...] = w_ref[...].astype(o_ref.dtype)

    blk = pl.BlockSpec((None, tr, cdim), lambda l, i: (l, i, 0))
    return pl.pallas_call(
        body, name=name, out_shape=jax.ShapeDtypeStruct(w.shape, MXU_DTYPE),
        grid=(nl, r // tr), in_specs=[blk], out_specs=blk,
        compiler_params=_params(("parallel", "parallel")),
    )(w)


def _pair_sum(parts, sib, which, out_dtype, name):
    _, k, r, cdim = parts.shape
    tr = BIG_ROWS

    def body(sel_ref, p_ref, s_ref, o_ref):
        o_ref[...] = (p_ref[...] + s_ref[...]).astype(o_ref.dtype)

    grid_spec = pltpu.PrefetchScalarGridSpec(
        num_scalar_prefetch=1, grid=(k, r // tr),
        in_specs=[pl.BlockSpec((None, None, tr, cdim), lambda l, i, sel: (sel[0], l, i, 0)),
                  pl.BlockSpec((None, tr, cdim), lambda l, i, sel: (l, i, 0))],
        out_specs=pl.BlockSpec((None, tr, cdim), lambda l, i, sel: (l, i, 0)))
    return pl.pallas_call(
        body, name=name, out_shape=jax.ShapeDtypeStruct((k, r, cdim), out_dtype), grid_spec=grid_spec,
        compiler_params=_params(("parallel", "parallel")),
    )(which.reshape(1).astype(jnp.int32), parts, sib)


def _sum_lead(parts, name):
    k, r, cdim = parts.shape
    tr = BIG_ROWS

    def body(p_ref, o_ref):
        total = p_ref[0].astype(F32)
        for a in range(1, k):
            total = total + p_ref[a].astype(F32)
        o_ref[...] = total

    return pl.pallas_call(
        body, name=name, out_shape=jax.ShapeDtypeStruct((r, cdim), F32),
        grid=(r // tr,), in_specs=[pl.BlockSpec((k, tr, cdim), lambda i: (0, i, 0))],
        out_specs=pl.BlockSpec((tr, cdim), lambda i: (i, 0)),
        compiler_params=_params(("parallel",)),
    )(parts)


def _adam_math(w, g, m, v):
    m2 = ADAM_B1 * m + (1.0 - ADAM_B1) * g
    v2 = ADAM_B2 * v + (1.0 - ADAM_B2) * (g * g)
    m_hat = m2 / (1.0 - ADAM_B1 ** ADAM_STEP)
    v_hat = v2 / (1.0 - ADAM_B2 ** ADAM_STEP)
    delta = -ADAM_LR * (m_hat / (jnp.sqrt(v_hat) + ADAM_EPS) + ADAM_WD * w)
    return delta, m2, v2


def _adam_big(w, g, m, v, name):
    nl, r, cdim = w.shape
    tr = BIG_ROWS

    def body(w_ref, g_ref, m_ref, v_ref, d_ref, mo_ref, vo_ref):
        delta, m2, v2 = _adam_math(w_ref[...], g_ref[...], m_ref[...], v_ref[...])
        d_ref[...] = delta
        mo_ref[...] = m2
        vo_ref[...] = v2

    blk = pl.BlockSpec((None, tr, cdim), lambda l, i: (l, i, 0))
    shp = jax.ShapeDtypeStruct(w.shape, F32)
    return pl.pallas_call(
        body, name=name, out_shape=(shp, shp, shp),
        grid=(nl, r // tr), in_specs=[blk] * 4, out_specs=(blk, blk, blk),
        compiler_params=_params(("parallel", "parallel")),
    )(w, g, m, v)


def _adam_cols_major(w, g, m, v, name):
    cdim, nl, r = w.shape
    tc = BIG_ROWS

    def body(w_ref, g_ref, m_ref, v_ref, d_ref, mo_ref, vo_ref):
        delta, m2, v2 = _adam_math(w_ref[...], g_ref[...], m_ref[...], v_ref[...])
        d_ref[...] = delta
        mo_ref[...] = m2
        vo_ref[...] = v2

    blk = pl.BlockSpec((tc, nl, r), lambda i: (i, 0, 0))
    shp = jax.ShapeDtypeStruct(w.shape, F32)
    return pl.pallas_call(
        body, name=name, out_shape=(shp, shp, shp),
        grid=(pl.cdiv(cdim, tc),), in_specs=[blk] * 4, out_specs=(blk, blk, blk),
        compiler_params=_params(("parallel",)),
    )(w, g, m, v)


def _adam_small(ws, gs, ms, vs, name):
    n = len(ws)

    def body(*refs):
        w_refs, g_refs, m_refs, v_refs = (refs[k * n:(k + 1) * n] for k in range(4))
        d_refs, mo_refs, vo_refs = (refs[(4 + k) * n:(5 + k) * n] for k in range(3))
        for a in range(n):
            delta, m2, v2 = _adam_math(w_refs[a][...], g_refs[a][...], m_refs[a][...], v_refs[a][...])
            d_refs[a][...] = delta
            mo_refs[a][...] = m2
            vo_refs[a][...] = v2

    shapes = tuple(jax.ShapeDtypeStruct(w.shape, F32) for w in ws)
    vm = pl.BlockSpec(memory_space=pltpu.VMEM)
    outs = pl.pallas_call(body, name=name, out_shape=shapes * 3, in_specs=[vm] * (4 * n),
                          out_specs=tuple([vm] * (3 * n)))(*ws, *gs, *ms, *vs)
    return outs[:n], outs[n:2 * n], outs[2 * n:]


def _pack(arrays):
    rows = []
    for a in arrays:
        flat = a.reshape(-1)
        pad = (-flat.shape[0]) % LANES
        if pad:
            flat = jnp.concatenate([flat, jnp.zeros((pad,), flat.dtype)])
        rows.append(flat.reshape(-1, LANES))
    out = jnp.concatenate(rows, axis=0)
    pad = (-out.shape[0]) % SUBLANES
    if pad:
        out = jnp.concatenate([out, jnp.zeros((pad, LANES), out.dtype)], axis=0)
    return out


def _unpack(pack, shapes):
    outs, row = [], 0
    for shp in shapes:
        n = int(np.prod(shp))
        nrows = -(-n // LANES)
        outs.append(pack[row:row + nrows].reshape(-1)[:n].reshape(shp))
        row += nrows
    return outs


SMALL = ["norm_w", "ssd_conv_b", "ssd_dt_bias", "ssd_a_log", "ssd_d", "ssd_norm_w", "attn_sinks",
         "conf_dw_b", "conf_ln_w", "conf_ln_b"]
WEIGHTS = ["norm_w", "w_in", "ssd_conv_w", "ssd_conv_b", "ssd_dt_bias", "ssd_a_log", "ssd_d", "ssd_norm_w",
           "attn_sinks", "conf_dw_w", "conf_dw_b", "conf_ln_w", "conf_ln_b", "w_out", "final_norm_w"]


def kernel(x, norm_w, w_in, ssd_conv_w, ssd_conv_b, ssd_dt_bias, ssd_a_log, ssd_d, ssd_norm_w, attn_sinks, conf_dw_w, conf_dw_b, conf_ln_w, conf_ln_b, w_out, final_norm_w, loss_target, m_norm_w, m_w_in, m_ssd_conv_w, m_ssd_conv_b, m_ssd_dt_bias, m_ssd_a_log, m_ssd_d, m_ssd_norm_w, m_attn_sinks, m_conf_dw_w, m_conf_dw_b, m_conf_ln_w, m_conf_ln_b, m_w_out, m_final_norm_w, v_norm_w, v_w_in, v_ssd_conv_w, v_ssd_conv_b, v_ssd_dt_bias, v_ssd_a_log, v_ssd_d, v_ssd_norm_w, v_attn_sinks, v_conf_dw_w, v_conf_dw_b, v_conf_ln_w, v_conf_ln_b, v_w_out, v_final_norm_w):
    w = dict(norm_w=norm_w, w_in=w_in, ssd_conv_w=ssd_conv_w, ssd_conv_b=ssd_conv_b, ssd_dt_bias=ssd_dt_bias,
             ssd_a_log=ssd_a_log, ssd_d=ssd_d, ssd_norm_w=ssd_norm_w, attn_sinks=attn_sinks, conf_dw_w=conf_dw_w,
             conf_dw_b=conf_dw_b, conf_ln_w=conf_ln_w, conf_ln_b=conf_ln_b, w_out=w_out, final_norm_w=final_norm_w)
    m = dict(norm_w=m_norm_w, w_in=m_w_in, ssd_conv_w=m_ssd_conv_w, ssd_conv_b=m_ssd_conv_b,
             ssd_dt_bias=m_ssd_dt_bias, ssd_a_log=m_ssd_a_log, ssd_d=m_ssd_d, ssd_norm_w=m_ssd_norm_w,
             attn_sinks=m_attn_sinks, conf_dw_w=m_conf_dw_w, conf_dw_b=m_conf_dw_b, conf_ln_w=m_conf_ln_w,
             conf_ln_b=m_conf_ln_b, w_out=m_w_out, final_norm_w=m_final_norm_w)
    v = dict(norm_w=v_norm_w, w_in=v_w_in, ssd_conv_w=v_ssd_conv_w, ssd_conv_b=v_ssd_conv_b,
             ssd_dt_bias=v_ssd_dt_bias, ssd_a_log=v_ssd_a_log, ssd_d=v_ssd_d, ssd_norm_w=v_ssd_norm_w,
             attn_sinks=v_attn_sinks, conf_dw_w=v_conf_dw_w, conf_dw_b=v_conf_dw_b, conf_ln_w=v_conf_ln_w,
             conf_ln_b=v_conf_ln_b, w_out=v_w_out, final_norm_w=v_final_norm_w)
    depth = w_in.shape[0]
    me = 2 * lax.axis_index("x") + lax.axis_index("y")

    assert depth == 2
    own = [_cast_mxu(w_in, name="cast_w_in"), _cast_mxu(w_out, name="cast_w_out"), ssd_conv_w, conf_dw_w]
    gathered = _gather_weights(own[:2], own[2:], name="gather_weights")
    g_in, g_out, g_conv, g_dw = [lax.dynamic_update_index_in_dim(g_all, mine, me, 0)
                                 for g_all, mine in zip(gathered, own)]
    layer_params = []
    for li in range(depth):
        w_in_p = _padded_from_chips([g_in[p, li] for p in range(N_CHIPS)])
        w_out_full = jnp.concatenate([g_out[p, li] for p in range(N_CHIPS)], axis=0)
        conv_full = jnp.concatenate([g_conv[p, li] for p in range(N_CHIPS)], axis=1)
        dw_full = jnp.concatenate([g_dw[p, li] for p in range(N_CHIPS)], axis=1)
        layer_params.append(_layer_params(li, w_in_p, w_out_full, conv_full, dw_full, w))

    loss, grad_x, grads, dfinal = _local_step(x, loss_target, layer_params, final_norm_w)

    small_list = [grads[li][n] for li in range(depth) for n in SMALL]
    small_list += [grads[li][n] for li in range(depth) for n in ("ssd_conv_w", "conf_dw_w")]
    small_list += [dfinal, loss.reshape(1)]
    small_shapes = [a.shape for a in small_list]
    reduced = _unpack(_allreduce_small(_pack(small_list), name="allreduce_small"), small_shapes)
    ns = len(SMALL)
    g = {n: jnp.stack([reduced[li * ns + i] for li in range(depth)]) for i, n in enumerate(SMALL)}
    conv_w_cols, dw_w_cols = ssd_conv_w.shape[2], conf_dw_w.shape[2]
    g["ssd_conv_w"] = jnp.stack([lax.dynamic_slice_in_dim(reduced[depth * ns + 2 * li], me * conv_w_cols,
                                                          conv_w_cols, axis=1) for li in range(depth)])
    g["conf_dw_w"] = jnp.stack([lax.dynamic_slice_in_dim(reduced[depth * ns + 2 * li + 1], me * dw_w_cols,
                                                         dw_w_cols, axis=1) for li in range(depth)])
    g["final_norm_w"] = reduced[-2]
    loss_total = reduced[-1][0]

    cols = w_in.shape[2]
    rows_out = w_out.shape[1]
    p_in = jnp.stack([jnp.stack([_chip_part_from_padded(grads[li]["w_in_p"], p, cols) for p in range(N_CHIPS)])
                      for li in range(depth)])
    p_out = jnp.stack([grads[li]["w_out"].reshape(N_CHIPS, rows_out, D_MODEL) for li in range(depth)])
    c = lax.axis_index("c")
    sib_in, sib_out = _pair_swap_halves([p_in, p_out], name="grad_pair_swap")
    s_in = _pair_sum(p_in, sib_in, c, MXU_DTYPE, name="grad_pair_sum_in")
    s_out = _pair_sum(p_out, sib_out, c, MXU_DTYPE, name="grad_pair_sum_out")
    r_in, r_out = _chip_scatter([s_in, s_out], name="grad_chip_scatter")
    r_in = lax.dynamic_update_index_in_dim(r_in, lax.dynamic_index_in_dim(s_in, me, 0, keepdims=False), me, 0)
    r_out = lax.dynamic_update_index_in_dim(r_out, lax.dynamic_index_in_dim(s_out, me, 0, keepdims=False), me, 0)
    t_in = _sum_lead(r_in, name="grad_chip_sum_in")
    t_out = _sum_lead(r_out, name="grad_chip_sum_out")
    g_w_in, g_w_out = _pair_gather([t_in, t_out], name="grad_pair_gather")
    g_w_in = lax.dynamic_update_index_in_dim(g_w_in, t_in, c, 0)
    g_w_out = lax.dynamic_update_index_in_dim(g_w_out, t_out, c, 0)

    outs_g, outs_d, outs_m, outs_v = {"w_in": g_w_in, "w_out": g_w_out}, {}, {}, {}
    to_cols, from_cols = (2, 0, 1), (1, 2, 0)
    outs_d["w_in"], outs_m["w_in"], outs_v["w_in"] = [
        jnp.transpose(a, from_cols) for a in _adam_cols_major(
            *[jnp.transpose(a, to_cols) for a in (w_in, g_w_in, m_w_in, v_w_in)], name="adam_w_in")]
    outs_d["w_out"], outs_m["w_out"], outs_v["w_out"] = _adam_big(w_out, g_w_out, m_w_out, v_w_out,
                                                                  name="adam_w_out")
    small_names = [n for n in WEIGHTS if n not in ("w_in", "w_out")]
    def as2d(a):
        return a.reshape(1, -1) if a.ndim == 1 else a

    deltas, new_ms, new_vs = _adam_small(*[[as2d(src[n]) for n in small_names] for src in (w, g, m, v)],
                                         name="adam_small")
    for n, dn, mn, vn in zip(small_names, deltas, new_ms, new_vs):
        outs_g[n], outs_d[n], outs_m[n], outs_v[n] = (g[n], dn.reshape(w[n].shape), mn.reshape(w[n].shape),
                                                      vn.reshape(w[n].shape))
    return (loss_total, grad_x, *[outs_g[n] for n in WEIGHTS], *[outs_d[n] for n in WEIGHTS],
            *[outs_m[n] for n in WEIGHTS], *[outs_v[n] for n in WEIGHTS])
```

```python
import functools
import math

import jax
import jax.numpy as jnp
import numpy as np
from jax import lax
from jax.experimental import pallas as pl
from jax.experimental.pallas import tpu as pltpu

F32 = jnp.float32
BF16 = jnp.bfloat16
MXU_DTYPE = BF16

D_MODEL = 1024
DEPTH = 2
SSD_HEADS = 16
SSD_HEAD_DIM = 64
SSD_STATE = 128
SSD_CONV = 4
CHUNK = 128
SSD_CONV_DIM = 1536
ATTN_HEAD_DIM = 64
ATTN_Q_HEADS = 8
WINDOW = 128
CONF_WIDTH = 512
CONF_KERNEL = 31
MIX_WIDTH = 2048
D_IN_PROJ = 5392
EPS = 1e-5

ADAM_LR = 0.001
ADAM_B1 = 0.9
ADAM_B2 = 0.999
ADAM_EPS = 1e-08
ADAM_WD = 0.01
ADAM_STEP = 10

LANES = 128
SUBLANES = 8
VMEM_LIMIT = 48 * 1024 * 1024

NP = 5632
OFF_ZA, OFF_Q, OFF_K, OFF_V, OFF_DT = 0, 512, 1024, 1152, 1280
ATTN_GROUP = 1536
OFF_XBC = 1536
OFF_CONF = 3072
OFF_ZS = 4096
OFF_ZC = 5120
SECTIONS = ((0, 1024, OFF_ZS), (1024, 1536, OFF_ZA), (1536, 2048, OFF_ZC), (2048, 3584, OFF_XBC),
            (3584, 3600, OFF_DT), (3600, 4368, OFF_Q), (4368, 5392, OFF_CONF))

YCAT_ATTN, YCAT_CONF = 1024, 1536
ANY = pl.BlockSpec(memory_space=pl.ANY)

NN = (((1,), (0,)), ((), ()))
NT = (((1,), (1,)), ((), ()))
TN = (((0,), (0,)), ((), ()))


def _params(sem):
    return pltpu.CompilerParams(dimension_semantics=sem, vmem_limit_bytes=VMEM_LIMIT)


def _dot(a, b, dims=NN):
    return lax.dot_general(a.astype(MXU_DTYPE), b.astype(MXU_DTYPE), dims, preferred_element_type=F32)


def _split_bf16(a, passes):
    pieces = []
    r = a
    for _ in range(passes):
        p = r.astype(BF16)
        pieces.append(p)
        r = r - p.astype(F32)
    return pieces


def _xdot(a, sel, dims=NN, passes=2):
    out = None
    for p in _split_bf16(a, passes):
        t = lax.dot_general(p, sel, dims, preferred_element_type=F32)
        out = t if out is None else out + t
    return out


def _xdot_r(sel, b, dims=NN, passes=3):
    out = None
    for p in _split_bf16(b, passes):
        t = lax.dot_general(sel, p, dims, preferred_element_type=F32)
        out = t if out is None else out + t
    return out


def _sigmoid(x):
    return 1.0 / (1.0 + jnp.exp(-x))


def _silu(x):
    return x * _sigmoid(x)


def _dsilu(x):
    s = _sigmoid(x)
    return s * (1.0 + x * (1.0 - s))


def _softplus(x):
    return jnp.maximum(x, 0.0) + jnp.log(1.0 + jnp.exp(-jnp.abs(x)))


def _rowsum8(x):
    r, c = x.shape
    return jnp.sum(x.reshape(r // SUBLANES, SUBLANES, c), axis=0)


def _iota(shape, dim):
    return lax.broadcasted_iota(jnp.int32, shape, dim)


def _matmul(a, b, form, out_dtype, tm, tn, tk, name, residual=None):
    if form == "nn":
        (m, k), n = a.shape, b.shape[1]
    elif form == "nt":
        (m, k), n = a.shape, b.shape[0]
    else:
        (k, m), n = a.shape, b.shape[1]
    tm, tn, tk = min(tm, m), min(tn, n), min(tk, k)
    assert m % tm == 0 and n % tn == 0 and k % tk == 0, (name, m, n, k, tm, tn, tk)
    if form == "nn":
        a_spec = pl.BlockSpec((tm, tk), lambda i, j, s: (i, s))
        b_spec = pl.BlockSpec((tk, tn), lambda i, j, s: (s, j))
        dims = NN
    elif form == "nt":
        (m, k), n = a.shape, b.shape[0]
        a_spec = pl.BlockSpec((tm, tk), lambda i, j, s: (i, s))
        b_spec = pl.BlockSpec((tn, tk), lambda i, j, s: (j, s))
        dims = NT
    else:
        (k, m), n = a.shape, b.shape[1]
        a_spec = pl.BlockSpec((tk, tm), lambda i, j, s: (s, i))
        b_spec = pl.BlockSpec((tk, tn), lambda i, j, s: (s, j))
        dims = TN
    nk = k // tk
    has_res = residual is not None

    def body_single(a_ref, b_ref, *rest):
        o = _dot(a_ref[...], b_ref[...], dims)
        if has_res:
            o = o + rest[0][...]
        rest[-1][...] = o.astype(out_dtype)

    def body(a_ref, b_ref, *rest):
        if has_res:
            r_ref, o_ref, acc = rest
        else:
            o_ref, acc = rest
        s = pl.program_id(2)

        @pl.when(s == 0)
        def _():
            acc[...] = jnp.zeros_like(acc)

        acc[...] += _dot(a_ref[...], b_ref[...], dims)

        @pl.when(s == nk - 1)
        def _():
            o = acc[...]
            if has_res:
                o = o + r_ref[...]
            o_ref[...] = o.astype(out_dtype)

    in_specs = [a_spec, b_spec]
    args = [a, b]
    if has_res:
        in_specs.append(pl.BlockSpec((tm, tn), lambda i, j, s: (i, j)))
        args.append(residual)
    return pl.pallas_call(
        body_single if nk == 1 else body, name=name,
        out_shape=jax.ShapeDtypeStruct((m, n), out_dtype),
        grid=(m // tm, n // tn, nk),
        in_specs=in_specs,
        out_specs=pl.BlockSpec((tm, tn), lambda i, j, s: (i, j)),
        scratch_shapes=[] if nk == 1 else [pltpu.VMEM((tm, tn), F32)],
        compiler_params=_params(("parallel", "parallel", "arbitrary")),
    )(*args)


ROW_TILE = 256


def _rmsnorm_fwd(x, w, name):
    t, d = x.shape
    tm = ROW_TILE

    def body(x_ref, w_ref, o_ref, ot_ref):
        xv = x_ref[...]
        rstd = lax.rsqrt(jnp.mean(xv * xv, axis=-1, keepdims=True) + EPS)
        h = xv * rstd * w_ref[...]
        o_ref[...] = h.astype(o_ref.dtype)
        ot_ref[...] = h.T.astype(ot_ref.dtype)

    return pl.pallas_call(
        body, name=name,
        out_shape=(jax.ShapeDtypeStruct((t, d), MXU_DTYPE), jax.ShapeDtypeStruct((d, t), MXU_DTYPE)),
        grid=(t // tm,),
        in_specs=[pl.BlockSpec((tm, d), lambda i: (i, 0)), pl.BlockSpec((1, d), lambda i: (0, 0))],
        out_specs=(pl.BlockSpec((tm, d), lambda i: (i, 0)), pl.BlockSpec((d, tm), lambda i: (0, i))),
        compiler_params=_params(("parallel",)),
    )(x, w)


def _rmsnorm_bwd(dh, x, w, dres, name):
    t, d = x.shape
    tm = ROW_TILE
    nt = t // tm

    def body(dh_ref, x_ref, w_ref, dr_ref, dx_ref, dw_ref, acc):
        i = pl.program_id(0)

        @pl.when(i == 0)
        def _():
            acc[...] = jnp.zeros_like(acc)

        xv = x_ref[...]
        rstd = lax.rsqrt(jnp.mean(xv * xv, axis=-1, keepdims=True) + EPS)
        xh = xv * rstd
        dhv = dh_ref[...]
        g = dhv * w_ref[...]
        dx_ref[...] = dr_ref[...] + rstd * (g - xh * jnp.mean(g * xh, axis=-1, keepdims=True))
        acc[...] += _rowsum8(dhv * xh)

        @pl.when(i == nt - 1)
        def _():
            dw_ref[...] = jnp.sum(acc[...], axis=0, keepdims=True)

    row = pl.BlockSpec((tm, d), lambda i: (i, 0))
    vec = pl.BlockSpec((1, d), lambda i: (0, 0))
    return pl.pallas_call(
        body, name=name,
        out_shape=(jax.ShapeDtypeStruct((t, d), F32), jax.ShapeDtypeStruct((1, d), F32)),
        grid=(nt,),
        in_specs=[row, row, vec, row],
        out_specs=(row, vec),
        scratch_shapes=[pltpu.VMEM((SUBLANES, d), F32)],
        compiler_params=_params(("arbitrary",)),
    )(dh, x, w, dres)


def _loss_head(xf, target, w, name):
    t, d = xf.shape
    tm = ROW_TILE
    nt = t // tm

    def body(x_ref, t_ref, w_ref, loss_ref, dx_ref, dw_ref, lacc, wacc):
        i = pl.program_id(0)

        @pl.when(i == 0)
        def _():
            lacc[...] = jnp.zeros_like(lacc)
            wacc[...] = jnp.zeros_like(wacc)

        xv = x_ref[...]
        rstd = lax.rsqrt(jnp.mean(xv * xv, axis=-1, keepdims=True) + EPS)
        xh = xv * rstd
        err = xh * w_ref[...] - t_ref[...]
        lacc[...] += jnp.sum(err * err)
        dy = err * (1.0 / d)
        g = dy * w_ref[...]
        dx_ref[...] = rstd * (g - xh * jnp.mean(g * xh, axis=-1, keepdims=True))
        wacc[...] += _rowsum8(dy * xh)

        @pl.when(i == nt - 1)
        def _():
            loss_ref[...] = lacc[...] * (0.5 / d)
            dw_ref[...] = jnp.sum(wacc[...], axis=0, keepdims=True)

    row = pl.BlockSpec((tm, d), lambda i: (i, 0))
    vec = pl.BlockSpec((1, d), lambda i: (0, 0))
    return pl.pallas_call(
        body, name=name,
        out_shape=(jax.ShapeDtypeStruct((SUBLANES, LANES), F32), jax.ShapeDtypeStruct((t, d), F32),
                   jax.ShapeDtypeStruct((1, d), F32)),
        grid=(nt,),
        in_specs=[row, row, vec],
        out_specs=(pl.BlockSpec((SUBLANES, LANES), lambda i: (0, 0)), row, vec),
        scratch_shapes=[pltpu.VMEM((SUBLANES, LANES), F32), pltpu.VMEM((SUBLANES, d), F32)],
        compiler_params=_params(("arbitrary",)),
    )(xf, target, w)


def _glu_fwd(proj, name):
    t = proj.shape[0]
    tm, cw = ROW_TILE, CONF_WIDTH

    def body(a_ref, g_ref, o_ref):
        o_ref[...] = a_ref[...] * _sigmoid(g_ref[...])

    return pl.pallas_call(
        body, name=name,
        out_shape=jax.ShapeDtypeStruct((t, cw), F32),
        grid=(t // tm,),
        in_specs=[pl.BlockSpec((tm, cw), lambda i: (i, OFF_CONF // cw)),
                  pl.BlockSpec((tm, cw), lambda i: (i, OFF_CONF // cw + 1))],
        out_specs=pl.BlockSpec((tm, cw), lambda i: (i, 0)),
        compiler_params=_params(("parallel",)),
    )(proj, proj)


def _glu_bwd(dc0, proj, dproj, name):
    t = proj.shape[0]
    tm, cw = ROW_TILE, CONF_WIDTH

    def body(d_ref, a_ref, g_ref, _, o_ref):
        s = _sigmoid(g_ref[...])
        dv = d_ref[...]
        o_ref[:, :cw] = (dv * s).astype(o_ref.dtype)
        o_ref[:, cw:] = (dv * a_ref[...] * s * (1.0 - s)).astype(o_ref.dtype)

    return pl.pallas_call(
        body, name=name,
        out_shape=jax.ShapeDtypeStruct(dproj.shape, dproj.dtype),
        grid=(t // tm,),
        in_specs=[pl.BlockSpec((tm, cw), lambda i: (i, 0)),
                  pl.BlockSpec((tm, cw), lambda i: (i, OFF_CONF // cw)),
                  pl.BlockSpec((tm, cw), lambda i: (i, OFF_CONF // cw + 1)), ANY],
        out_specs=pl.BlockSpec((tm, 2 * cw), lambda i: (i, OFF_CONF // (2 * cw))),
        input_output_aliases={3: 0},
        compiler_params=_params(("parallel",)),
    )(dc0, proj, proj, dproj)


def _conf_post_fwd(c1, proj, ln_w, ln_b, ycat, name):
    t = c1.shape[0]
    tm, cw = ROW_TILE, CONF_WIDTH

    def body(c_ref, z_ref, w_ref, b_ref, _, o_ref):
        cv = c_ref[...]
        xc = cv - jnp.mean(cv, axis=-1, keepdims=True)
        rstd = lax.rsqrt(jnp.mean(xc * xc, axis=-1, keepdims=True) + EPS)
        c2 = xc * rstd * w_ref[...] + b_ref[...]
        o_ref[...] = (_silu(c2) * _silu(z_ref[...])).astype(o_ref.dtype)

    vec = pl.BlockSpec((1, cw), lambda i: (0, 0))
    return pl.pallas_call(
        body, name=name,
        out_shape=jax.ShapeDtypeStruct(ycat.shape, ycat.dtype),
        grid=(t // tm,),
        in_specs=[pl.BlockSpec((tm, cw), lambda i: (i, 0)),
                  pl.BlockSpec((tm, cw), lambda i: (i, OFF_ZC // cw)), vec, vec, ANY],
        out_specs=pl.BlockSpec((tm, cw), lambda i: (i, YCAT_CONF // cw)),
        input_output_aliases={4: 0},
        compiler_params=_params(("parallel",)),
    )(c1, proj, ln_w, ln_b, ycat)


def _conf_post_bwd(dycat, c1, proj, ln_w, ln_b, dproj, name):
    t = c1.shape[0]
    tm, cw = ROW_TILE, CONF_WIDTH
    nt = t // tm

    def body(dy_ref, c_ref, z_ref, w_ref, b_ref, _, dc_ref, dz_ref, dw_ref, db_ref, wacc, bacc):
        i = pl.program_id(0)

        @pl.when(i == 0)
        def _():
            wacc[...] = jnp.zeros_like(wacc)
            bacc[...] = jnp.zeros_like(bacc)

        cv = c_ref[...]
        xc = cv - jnp.mean(cv, axis=-1, keepdims=True)
        rstd = lax.rsqrt(jnp.mean(xc * xc, axis=-1, keepdims=True) + EPS)
        xh = xc * rstd
        c2 = xh * w_ref[...] + b_ref[...]
        zv = z_ref[...]
        dy = dy_ref[...]
        dz_ref[...] = (dy * _silu(c2) * _dsilu(zv)).astype(dz_ref.dtype)
        dc2 = dy * _silu(zv) * _dsilu(c2)
        bacc[...] += _rowsum8(dc2)
        wacc[...] += _rowsum8(dc2 * xh)
        dxh = dc2 * w_ref[...]
        dc_ref[...] = rstd * (dxh - jnp.mean(dxh, axis=-1, keepdims=True)
                              - xh * jnp.mean(dxh * xh, axis=-1, keepdims=True))

        @pl.when(i == nt - 1)
        def _():
            dw_ref[...] = jnp.sum(wacc[...], axis=0, keepdims=True)
            db_ref[...] = jnp.sum(bacc[...], axis=0, keepdims=True)

    row = pl.BlockSpec((tm, cw), lambda i: (i, 0))
    vec = pl.BlockSpec((1, cw), lambda i: (0, 0))
    return pl.pallas_call(
        body, name=name,
        out_shape=(jax.ShapeDtypeStruct((t, cw), F32), jax.ShapeDtypeStruct(dproj.shape, dproj.dtype),
                   jax.ShapeDtypeStruct((1, cw), F32), jax.ShapeDtypeStruct((1, cw), F32)),
        grid=(nt,),
        in_specs=[pl.BlockSpec((tm, cw), lambda i: (i, YCAT_CONF // cw)), row,
                  pl.BlockSpec((tm, cw), lambda i: (i, OFF_ZC // cw)), vec, vec, ANY],
        out_specs=(row, pl.BlockSpec((tm, cw), lambda i: (i, OFF_ZC // cw)), vec, vec),
        input_output_aliases={5: 1},
        scratch_shapes=[pltpu.VMEM((SUBLANES, cw), F32), pltpu.VMEM((SUBLANES, cw), F32)],
        compiler_params=_params(("arbitrary",)),
    )(dycat, c1, proj, ln_w, ln_b, dproj)


CONV_TILE = 512
CONV_COLS = 512
CONV_SUB_ROWS = 128
CONV_SUB_COLS = LANES


def _conv_halo(k):
    return SUBLANES if k - 1 <= SUBLANES else 32


def _conv_subtiles(tm, cw):
    return [(r0, c0) for r0 in range(0, tm, CONV_SUB_ROWS) for c0 in range(0, cw, CONV_SUB_COLS)]


def _conv_use_shifted(k):
    return k > SUBLANES


def _conv_shift_scratch(k, rows, cw):
    return [pltpu.VMEM((SUBLANES - 1, rows - SUBLANES, cw), F32)] if _conv_use_shifted(k) else []


def _conv_fill_shifted(ext, sh):
    n = sh.shape[1]
    for b in range(1, SUBLANES):
        sh[b - 1] = ext[b:b + n, :]


def _conv_rows(ext, sh, start, rows, cs):
    b = start % SUBLANES
    if b == 0 or not sh:
        return ext[start:start + rows, cs]
    return sh[0][b - 1, start - b:start - b + rows, cs]


def _conv_fwd(src, col0, width, w, bias, k, seq, name):
    t = src.shape[0]
    tm, cw, halo = CONV_TILE, CONV_COLS, _conv_halo(k)
    sr, sc = CONV_SUB_ROWS, CONV_SUB_COLS
    p = k - 1
    cb0 = col0 // cw
    kp = w.shape[0]

    shifted = _conv_use_shifted(k)

    def body(x_ref, h_ref, w_ref, b_ref, o_ref, ext, *sh):
        i = pl.program_id(0)
        seq_start = (i * tm) % seq == 0
        ext[halo:, :] = x_ref[...]
        ext[:halo, :] = jnp.where(seq_start, 0.0, h_ref[...])
        if shifted:
            _conv_fill_shifted(ext, sh[0])
        for r0, c0 in _conv_subtiles(tm, cw):
            cs = slice(c0, c0 + sc)
            acc = jnp.zeros((sr, sc), F32) + b_ref[:, cs]
            for j in range(k):
                acc = acc + w_ref[j:j + 1, cs] * _conv_rows(ext, sh, r0 + halo - p + j, sr, cs)
            o_ref[r0:r0 + sr, cs] = acc

    return pl.pallas_call(
        body, name=name,
        out_shape=jax.ShapeDtypeStruct((t, width), F32),
        grid=(t // tm, width // cw),
        in_specs=[pl.BlockSpec((tm, cw), lambda i, j: (i, cb0 + j)),
                  pl.BlockSpec((halo, cw), lambda i, j: (jnp.maximum(i * (tm // halo) - 1, 0), cb0 + j)),
                  pl.BlockSpec((kp, cw), lambda i, j: (0, j)),
                  pl.BlockSpec((1, cw), lambda i, j: (0, j))],
        out_specs=pl.BlockSpec((tm, cw), lambda i, j: (i, j)),
        scratch_shapes=[pltpu.VMEM((halo + tm, cw), F32)] + _conv_shift_scratch(k, halo + tm, cw),
        compiler_params=_params(("parallel", "parallel")),
    )(src, src, w, bias)


def _conv_bwd(dy, src, col0, width, w, k, seq, name, into=None):
    t = src.shape[0]
    tm, cw, halo = CONV_TILE, CONV_COLS, _conv_halo(k)
    sr, sc = CONV_SUB_ROWS, CONV_SUB_COLS
    p = k - 1
    cb0 = col0 // cw
    kp = w.shape[0]
    nt = t // tm
    last_halo = t // halo - 1

    shifted = _conv_use_shifted(k)

    def body(dy_ref, dn_ref, x_ref, xp_ref, w_ref, *rest):
        if into is not None:
            rest = rest[1:]
        dx_ref, dw_ref, db_ref, dyext, xext, wacc, bacc = rest[:7]
        sh = rest[7:]
        i = pl.program_id(1)
        dysh, xsh = (sh[:1], sh[1:]) if shifted else ((), ())

        @pl.when(i == 0)
        def _():
            wacc[...] = jnp.zeros_like(wacc)
            bacc[...] = jnp.zeros_like(bacc)

        seq_start = (i * tm) % seq == 0
        seq_end = ((i + 1) * tm) % seq == 0
        dyext[:tm, :] = dy_ref[...]
        dyext[tm:, :] = jnp.where(seq_end, 0.0, dn_ref[...])
        xext[halo:, :] = x_ref[...]
        xext[:halo, :] = jnp.where(seq_start, 0.0, xp_ref[...])
        if shifted:
            _conv_fill_shifted(dyext, dysh[0])
            _conv_fill_shifted(xext, xsh[0])
        for r0, c0 in _conv_subtiles(tm, cw):
            cs = slice(c0, c0 + sc)
            dyv = dy_ref[r0:r0 + sr, cs]
            acc = jnp.zeros((sr, sc), F32)
            for j in range(k):
                acc = acc + w_ref[j:j + 1, cs] * _conv_rows(dyext, dysh, r0 + p - j, sr, cs)
                wacc[j, :, cs] += _rowsum8(dyv * _conv_rows(xext, xsh, r0 + halo - p + j, sr, cs))
            dx_ref[r0:r0 + sr, cs] = acc.astype(dx_ref.dtype)
            bacc[:, cs] += _rowsum8(dyv)

        @pl.when(i == nt - 1)
        def _():
            dw_ref[...] = jnp.zeros_like(dw_ref)
            for j in range(k):
                dw_ref[j:j + 1, :] = jnp.sum(wacc[j], axis=0, keepdims=True)
            db_ref[...] = jnp.sum(bacc[...], axis=0, keepdims=True)

    if into is None:
        dx_shape = jax.ShapeDtypeStruct((t, width), F32)
        dx_spec = pl.BlockSpec((tm, cw), lambda j, i: (i, j))
        extra_specs, extra_args, aliases = [], [], {}
    else:
        dx_shape = jax.ShapeDtypeStruct(into.shape, into.dtype)
        dx_spec = pl.BlockSpec((tm, cw), lambda j, i: (i, cb0 + j))
        extra_specs, extra_args, aliases = [ANY], [into], {5: 0}
    return pl.pallas_call(
        body, name=name,
        out_shape=(dx_shape, jax.ShapeDtypeStruct((kp, width), F32), jax.ShapeDtypeStruct((1, width), F32)),
        grid=(width // cw, nt),
        in_specs=[pl.BlockSpec((tm, cw), lambda j, i: (i, j)),
                  pl.BlockSpec((halo, cw), lambda j, i: (jnp.minimum((i + 1) * (tm // halo), last_halo), j)),
                  pl.BlockSpec((tm, cw), lambda j, i: (i, cb0 + j)),
                  pl.BlockSpec((halo, cw), lambda j, i: (jnp.maximum(i * (tm // halo) - 1, 0), cb0 + j)),
                  pl.BlockSpec((kp, cw), lambda j, i: (0, j))] + extra_specs,
        out_specs=(dx_spec,
                   pl.BlockSpec((kp, cw), lambda j, i: (0, j)),
                   pl.BlockSpec((1, cw), lambda j, i: (0, j))),
        input_output_aliases=aliases,
        scratch_shapes=[pltpu.VMEM((tm + halo, cw), F32), pltpu.VMEM((halo + tm, cw), F32),
                        pltpu.VMEM((kp, SUBLANES, cw), F32), pltpu.VMEM((SUBLANES, cw), F32)]
        + 2 * _conv_shift_scratch(k, halo + tm, cw),
        compiler_params=_params(("parallel", "arbitrary")),
    )(dy, dy, src, src, w, *extra_args)


def _head_dup(g):
    r, c = _iota((LANES, LANES), 0), _iota((LANES, LANES), 1)
    return (r == g * ATTN_HEAD_DIM + (c & (ATTN_HEAD_DIM - 1))).astype(BF16)


def _half_mask(half):
    lane = _iota((1, LANES), 1)
    return ((lane >= half * ATTN_HEAD_DIM) & (lane < (half + 1) * ATTN_HEAD_DIM)).astype(F32)


def _band_mask(first_block):
    w = WINDOW
    qi = _iota((w, 2 * w), 0)
    kj = _iota((w, 2 * w), 1) - w
    rel = qi - kj
    return (rel >= 0) & (rel < w) & (jnp.logical_not(first_block) | (kj >= 0))


def _lane_pick(x, h):
    return jnp.sum(jnp.where(_iota(x.shape, 1) == h, x, 0.0), axis=1, keepdims=True)


def _attn_specs(nb, rev):
    w = WINDOW

    def blk(i):
        return nb - 1 - i if rev else i

    def row(b, i):
        return b * nb + blk(i)

    def prow(b, i):
        return b * nb + jnp.maximum(blk(i) - 1, 0)

    q = pl.BlockSpec((w, 512), lambda b, i: (row(b, i), OFF_Q // 512))
    kc = pl.BlockSpec((w, 128), lambda b, i: (row(b, i), OFF_K // 128))
    kp = pl.BlockSpec((w, 128), lambda b, i: (prow(b, i), OFF_K // 128))
    vc = pl.BlockSpec((w, 128), lambda b, i: (row(b, i), OFF_V // 128))
    vp = pl.BlockSpec((w, 128), lambda b, i: (prow(b, i), OFF_V // 128))
    z = pl.BlockSpec((w, 512), lambda b, i: (row(b, i), OFF_ZA // 512))
    return q, kc, kp, vc, vp, z, row


def _attn_fwd(proj, sinks, ycat, nbatch, name):
    t = proj.shape[0]
    w = WINDOW
    nb = t // nbatch // w
    scale = ATTN_HEAD_DIM ** -0.5
    q_s, kc_s, kp_s, vc_s, vp_s, z_s, row = _attn_specs(nb, False)

    def body(q_ref, kc_ref, kp_ref, vc_ref, vp_ref, z_ref, sk_ref, _, y_ref, o_ref, lse_ref):
        first = pl.program_id(1) == 0
        mask = _band_mask(first)
        kk = jnp.concatenate([kp_ref[...], kc_ref[...]], axis=0).astype(MXU_DTYPE)
        vv = jnp.concatenate([vp_ref[...], vc_ref[...]], axis=0).astype(MXU_DTYPE)
        sk = sk_ref[...]
        lse_all = jnp.zeros((w, LANES), F32)
        lane = _iota((w, LANES), 1)
        for g in range(2):
            dup = _head_dup(g)
            kkd = _dot(kk, dup).astype(MXU_DTYPE)
            vvd = _dot(vv, dup)
            for jj in range(2):
                j = 2 * g + jj
                qp = q_ref[:, j * LANES:(j + 1) * LANES]
                op = jnp.zeros((w, LANES), F32)
                for half in range(2):
                    h = 2 * j + half
                    hm = _half_mask(half)
                    s = _dot(qp * hm, kkd, NT) * scale
                    s = jnp.where(mask, s, -1e30)
                    skh = _lane_pick(sk, h)
                    m = jnp.maximum(jnp.max(s, axis=1, keepdims=True), skh)
                    den = jnp.sum(jnp.exp(s - m), axis=1, keepdims=True) + jnp.exp(skh - m)
                    lse = m + jnp.log(den)
                    pr = jnp.exp(s - lse)
                    op = op + _dot(pr, vvd * hm)
                    lse_all = jnp.where(lane == h, lse, lse_all)
                o_ref[:, j * LANES:(j + 1) * LANES] = op
                y_ref[:, j * LANES:(j + 1) * LANES] = (
                    op * _silu(z_ref[:, j * LANES:(j + 1) * LANES])).astype(y_ref.dtype)
        lse_ref[...] = lse_all

    return pl.pallas_call(
        body, name=name,
        out_shape=(jax.ShapeDtypeStruct(ycat.shape, ycat.dtype), jax.ShapeDtypeStruct((t, 512), F32),
                   jax.ShapeDtypeStruct((t, LANES), F32)),
        grid=(nbatch, nb),
        in_specs=[q_s, kc_s, kp_s, vc_s, vp_s, z_s, pl.BlockSpec((1, LANES), lambda b, i: (0, 0)), ANY],
        out_specs=(pl.BlockSpec((w, 512), lambda b, i: (row(b, i), YCAT_ATTN // 512)),
                   pl.BlockSpec((w, 512), lambda b, i: (row(b, i), 0)),
                   pl.BlockSpec((w, LANES), lambda b, i: (row(b, i), 0))),
        input_output_aliases={7: 0},
        compiler_params=_params(("parallel", "parallel")),
    )(proj, proj, proj, proj, proj, proj, sinks, ycat)


def _attn_bwd(dycat, proj, o, lse, sinks, ddt, dproj, nbatch, name):
    t = proj.shape[0]
    w = WINDOW
    nb = t // nbatch // w
    scale = ATTN_HEAD_DIM ** -0.5
    q_s, kc_s, kp_s, vc_s, vp_s, z_s, row = _attn_specs(nb, True)

    def body(dy_ref, q_ref, kc_ref, kp_ref, vc_ref, vp_ref, z_ref, o_ref, lse_ref, sk_ref, ddt_ref, _,
             grp_ref, dsk_ref, kcarry, vcarry, sacc):
        b, i = pl.program_id(0), pl.program_id(1)

        @pl.when((b == 0) & (i == 0))
        def _():
            sacc[...] = jnp.zeros_like(sacc)

        @pl.when(i == 0)
        def _():
            kcarry[...] = jnp.zeros_like(kcarry)
            vcarry[...] = jnp.zeros_like(vcarry)

        first = i == nb - 1
        mask = _band_mask(first)
        kk = jnp.concatenate([kp_ref[...], kc_ref[...]], axis=0).astype(MXU_DTYPE)
        vv = jnp.concatenate([vp_ref[...], vc_ref[...]], axis=0).astype(MXU_DTYPE)
        sk = sk_ref[...]
        lse_all = lse_ref[...]
        lane1 = _iota((1, LANES), 1)
        dkk = jnp.zeros((2 * w, LANES), F32)
        dvv = jnp.zeros((2 * w, LANES), F32)
        dsk = jnp.zeros((1, LANES), F32)
        for g in range(2):
            dup = _head_dup(g)
            kkd = _dot(kk, dup).astype(MXU_DTYPE)
            vvd = _dot(vv, dup).astype(MXU_DTYPE)
            dkd = jnp.zeros((2 * w, LANES), F32)
            dvd = jnp.zeros((2 * w, LANES), F32)
            for jj in range(2):
                j = 2 * g + jj
                cols = slice(j * LANES, (j + 1) * LANES)
                qp, zp, ov, dy = q_ref[:, cols], z_ref[:, cols], o_ref[:, cols], dy_ref[:, cols]
                grp_ref[:, OFF_ZA + j * LANES:OFF_ZA + (j + 1) * LANES] = (dy * ov * _dsilu(zp)).astype(grp_ref.dtype)
                do = dy * _silu(zp)
                dq = jnp.zeros((w, LANES), F32)
                for half in range(2):
                    h = 2 * j + half
                    hm = _half_mask(half)
                    qh = qp * hm
                    doh = do * hm
                    delta = jnp.sum(doh * ov, axis=1, keepdims=True)
                    lse_h = _lane_pick(lse_all, h)
                    s = _dot(qh, kkd, NT) * scale
                    s = jnp.where(mask, s, -1e30)
                    pr = jnp.exp(s - lse_h)
                    dp = _dot(doh, vvd, NT)
                    ds = pr * (dp - delta)
                    dq = dq + _dot(ds, kkd) * hm * scale
                    dkd = dkd + _dot(ds, qh, TN) * scale
                    dvd = dvd + _dot(pr, doh, TN)
                    psink = jnp.exp(_lane_pick(sk, h) - lse_h)
                    dsk = dsk - jnp.where(lane1 == h, jnp.sum(psink * delta), 0.0)
                grp_ref[:, OFF_Q + j * LANES:OFF_Q + (j + 1) * LANES] = dq.astype(grp_ref.dtype)
            dkk = dkk + _xdot(dkd, dup, NT, passes=2)
            dvv = dvv + _xdot(dvd, dup, NT, passes=2)
        grp_ref[:, OFF_K:OFF_K + LANES] = (dkk[w:, :] + kcarry[...]).astype(grp_ref.dtype)
        grp_ref[:, OFF_V:OFF_V + LANES] = (dvv[w:, :] + vcarry[...]).astype(grp_ref.dtype)
        grp_ref[:, OFF_DT:OFF_DT + LANES] = ddt_ref[...].astype(grp_ref.dtype)
        grp_ref[:, OFF_DT + LANES:] = jnp.zeros((w, ATTN_GROUP - OFF_DT - LANES), grp_ref.dtype)
        kcarry[...] = dkk[:w, :]
        vcarry[...] = dvv[:w, :]
        sacc[...] += dsk

        @pl.when((b == nbatch - 1) & (i == nb - 1))
        def _():
            dsk_ref[...] = sacc[...]

    return pl.pallas_call(
        body, name=name,
        out_shape=(jax.ShapeDtypeStruct(dproj.shape, dproj.dtype), jax.ShapeDtypeStruct((1, LANES), F32)),
        grid=(nbatch, nb),
        in_specs=[pl.BlockSpec((w, 512), lambda b, i: (row(b, i), YCAT_ATTN // 512)),
                  q_s, kc_s, kp_s, vc_s, vp_s, z_s,
                  pl.BlockSpec((w, 512), lambda b, i: (row(b, i), 0)),
                  pl.BlockSpec((w, LANES), lambda b, i: (row(b, i), 0)),
                  pl.BlockSpec((1, LANES), lambda b, i: (0, 0)),
                  pl.BlockSpec((w, LANES), lambda b, i: (row(b, i), 0)), ANY],
        out_specs=(pl.BlockSpec((w, ATTN_GROUP), lambda b, i: (row(b, i), 0)),
                   pl.BlockSpec((1, LANES), lambda b, i: (0, 0))),
        input_output_aliases={11: 0},
        scratch_shapes=[pltpu.VMEM((w, LANES), F32), pltpu.VMEM((w, LANES), F32),
                        pltpu.VMEM((1, LANES), F32)],
        compiler_params=_params(("arbitrary", "arbitrary")),
    )(dycat, proj, proj, proj, proj, proj, proj, o, lse, sinks, ddt, dproj)


SSD_WIDTH = SSD_HEADS * SSD_HEAD_DIM
GROUP_ROWS = SSD_WIDTH // 2


def _expand_mat():
    r, c = _iota((LANES, SSD_WIDTH), 0), _iota((LANES, SSD_WIDTH), 1)
    return (r == lax.shift_right_logical(c, 6)).astype(BF16)


def _expand_mat_t():
    r, c = _iota((SSD_WIDTH, LANES), 0), _iota((SSD_WIDTH, LANES), 1)
    return (c == lax.shift_right_logical(r, 6)).astype(BF16)


def _ssd_common(u_ref, dt_ref, dtb_ref, a_ref):
    q = CHUNK
    act = _silu(u_ref[...])
    xs = act[:, :SSD_WIDTH]
    bm = act[:, SSD_WIDTH:SSD_WIDTH + 256]
    cm = act[:, SSD_WIDTH + 256:]
    dtp = _softplus(dt_ref[...] + dtb_ref[...])
    a = dtp * a_ref[...]
    tril = (_iota((q, q), 0) >= _iota((q, q), 1)).astype(BF16)
    acs = _xdot_r(tril, a)
    acs_t = acs.T
    e = _expand_mat()
    dt_x = _xdot(dtp, e)
    ea = jnp.exp(_xdot(acs, e))
    a_end = jnp.sum(jnp.where(_iota(acs.shape, 0) == q - 1, acs, 0.0), axis=0, keepdims=True)
    dec = jnp.exp(_xdot(a_end - acs, e))
    a_end_col = jnp.broadcast_to(_lane_pick(acs_t, q - 1), (LANES, LANES))
    s_scale = jnp.exp(_xdot_r(_expand_mat_t(), a_end_col))
    return act, xs, bm, cm, dtp, acs, acs_t, dt_x, ea, dec, s_scale, tril


def _decay_mat(acs, acs_t, h):
    q = CHUNK
    col = _lane_pick(acs, h)
    rowv = jnp.sum(jnp.where(_iota(acs_t.shape, 0) == h, acs_t, 0.0), axis=0, keepdims=True)
    causal = _iota((q, q), 0) >= _iota((q, q), 1)
    return jnp.exp(jnp.where(causal, col - rowv, -1e30))


GN_WIDTH = 512


def _ssd_fwd(u, proj, dtb, a_neg, d_x, norm_w, ycat, nbatch, name):
    t = u.shape[0]
    q = CHUNK
    nc = t // nbatch // q

    def body(u_ref, dt_ref, z_ref, dtb_ref, a_ref, dx_ref, nw_ref, _, y_ref, st_ref, yn_ref, state):
        c = pl.program_id(1)

        @pl.when(c == 0)
        def _():
            state[...] = jnp.zeros_like(state)

        st_ref[...] = state[...]
        act, xs, bm, cm, dtp, acs, acs_t, dt_x, ea, dec, s_scale, _ = _ssd_common(u_ref, dt_ref, dtb_ref, a_ref)
        xdt = xs * dt_x
        xdec = xdt * dec
        lo, hi = _half_mask(0), _half_mask(1)
        for g in range(2):
            bg = bm[:, g * LANES:(g + 1) * LANES]
            cg = cm[:, g * LANES:(g + 1) * LANES]
            rows = slice(g * GROUP_ROWS, (g + 1) * GROUP_ROWS)
            sg = state[rows, :]
            cb = _dot(cg, bg, NT)
            yoff = _dot(cg, sg, NT)
            for j in range(4):
                pj = g * 4 + j
                cols = slice(pj * LANES, (pj + 1) * LANES)
                xp = xdt[:, cols]
                m0 = cb * _decay_mat(acs, acs_t, 2 * pj)
                m1 = cb * _decay_mat(acs, acs_t, 2 * pj + 1)
                yp = _dot(m0, xp * lo) + _dot(m1, xp * hi)
                yp = yp + yoff[:, j * LANES:(j + 1) * LANES] * ea[:, cols]
                y_ref[:, cols] = yp + dx_ref[:, cols] * xs[:, cols]
            state[rows, :] = s_scale[rows, :] * sg + _dot(xdec[:, rows], bg, TN)
        for g in range(SSD_WIDTH // GN_WIDTH):
            cols = slice(g * GN_WIDTH, (g + 1) * GN_WIDTH)
            gg = y_ref[:, cols] * _silu(z_ref[:, cols])
            rstd = lax.rsqrt(jnp.mean(gg * gg, axis=-1, keepdims=True) + EPS)
            yn_ref[:, cols] = (gg * rstd * nw_ref[:, cols]).astype(yn_ref.dtype)

    vec = pl.BlockSpec((1, LANES), lambda b, c: (0, 0))
    wide = pl.BlockSpec((q, SSD_WIDTH), lambda b, c: (b * nc + c, 0))
    wvec = pl.BlockSpec((1, SSD_WIDTH), lambda b, c: (0, 0))
    return pl.pallas_call(
        body, name=name,
        out_shape=(jax.ShapeDtypeStruct((t, SSD_WIDTH), F32),
                   jax.ShapeDtypeStruct((nbatch * nc * SSD_WIDTH, SSD_STATE), F32),
                   jax.ShapeDtypeStruct(ycat.shape, ycat.dtype)),
        grid=(nbatch, nc),
        in_specs=[pl.BlockSpec((q, SSD_CONV_DIM), lambda b, c: (b * nc + c, 0)),
                  pl.BlockSpec((q, LANES), lambda b, c: (b * nc + c, OFF_DT // LANES)),
                  pl.BlockSpec((q, SSD_WIDTH), lambda b, c: (b * nc + c, OFF_ZS // SSD_WIDTH)),
                  vec, vec, wvec, wvec, ANY],
        out_specs=(wide, pl.BlockSpec((SSD_WIDTH, SSD_STATE), lambda b, c: (b * nc + c, 0)), wide),
        input_output_aliases={7: 2},
        scratch_shapes=[pltpu.VMEM((SSD_WIDTH, SSD_STATE), F32)],
        compiler_params=_params(("parallel", "arbitrary")),
    )(u, proj, proj, dtb, a_neg, d_x, norm_w, ycat)


def _ssd_bwd(dycat, u, proj, y, states, dtb, a_neg, d_x, norm_w, dproj, nbatch, name):
    t = u.shape[0]
    q = CHUNK
    nc = t // nbatch // q

    def body(do_ref, u_ref, dt_ref, z_ref, y_ref, st_ref, dtb_ref, a_ref, dx_ref, nw_ref, _,
             du_ref, dz_ref, ddt_ref, dal_ref, dd_ref, dtbg_ref, dnw_ref, dstate, acc_a, acc_d, acc_b, acc_w):
        b, c = pl.program_id(0), pl.program_id(1)

        @pl.when((b == 0) & (c == 0))
        def _():
            acc_a[...] = jnp.zeros_like(acc_a)
            acc_d[...] = jnp.zeros_like(acc_d)
            acc_b[...] = jnp.zeros_like(acc_b)
            acc_w[...] = jnp.zeros_like(acc_w)

        @pl.when(c == 0)
        def _():
            dstate[...] = jnp.zeros_like(dstate)

        dy_parts = []
        for g in range(SSD_WIDTH // GN_WIDTH):
            cols = slice(g * GN_WIDTH, (g + 1) * GN_WIDTH)
            yv, zv, dov = y_ref[:, cols], z_ref[:, cols], do_ref[:, cols]
            sz = _silu(zv)
            gg = yv * sz
            rstd = lax.rsqrt(jnp.mean(gg * gg, axis=-1, keepdims=True) + EPS)
            gh = gg * rstd
            acc_w[:, cols] += _rowsum8(dov * gh)
            dgn = dov * nw_ref[:, cols]
            dg = rstd * (dgn - gh * jnp.mean(dgn * gh, axis=-1, keepdims=True))
            dy_parts.append(dg * sz)
            dz_ref[:, cols] = (dg * yv * _dsilu(zv)).astype(dz_ref.dtype)

        act, xs, bm, cm, dtp, acs, acs_t, dt_x, ea, dec, s_scale, tril = _ssd_common(
            u_ref, dt_ref, dtb_ref, a_ref)
        xdt = xs * dt_x
        xdec = xdt * dec
        dyv = jnp.concatenate(dy_parts, axis=1)
        dye = dyv * ea
        lo, hi = _half_mask(0), _half_mask(1)
        et = _expand_mat_t()
        dxdt_parts, db_parts, dc_parts, dxst_parts, yoff_parts = [], [], [], [], []
        end_sum = jnp.zeros((LANES, LANES), F32)
        dal_diag = jnp.zeros((q, LANES), F32)
        lane_q = _iota((q, LANES), 1)
        for g in range(2):
            bg = bm[:, g * LANES:(g + 1) * LANES]
            cg = cm[:, g * LANES:(g + 1) * LANES]
            rows = slice(g * GROUP_ROWS, (g + 1) * GROUP_ROWS)
            sg = st_ref[rows, :]
            dsg = dstate[rows, :]
            cb = _dot(cg, bg, NT)
            yoff_parts.append(_dot(cg, sg, NT))
            dcb = jnp.zeros((q, q), F32)
            parts = []
            for j in range(4):
                pj = g * 4 + j
                cols = slice(pj * LANES, (pj + 1) * LANES)
                xp = xdt[:, cols]
                dy0, dy1 = dyv[:, cols] * lo, dyv[:, cols] * hi
                l0 = _decay_mat(acs, acs_t, 2 * pj)
                l1 = _decay_mat(acs, acs_t, 2 * pj + 1)
                g0, g1 = _dot(dy0, xp, NT), _dot(dy1, xp, NT)
                m0, m1 = cb * l0, cb * l1
                dcb = dcb + g0 * l0 + g1 * l1
                parts.append(_dot(m0, dy0, TN) + _dot(m1, dy1, TN))
                for hh, wmat in enumerate((g0 * m0, g1 * m1)):
                    sel = (lane_q == 2 * pj + hh).astype(F32)
                    dal_diag = dal_diag + _dot(wmat, sel) - _dot(wmat, sel, TN)
            dxst = _dot(bg, dsg, NT) * dec[:, rows]
            dxst_parts.append(dxst)
            dxdt_parts.append(jnp.concatenate(parts, axis=1) + dxst)
            dc_parts.append(_dot(dcb, bg) + _dot(dye[:, rows], sg))
            db_parts.append(_dot(dcb, cg, TN) + _dot(xdec[:, rows], dsg))
            s_next = s_scale[rows, :] * sg + _dot(xdec[:, rows], bg, TN)
            end_sum = end_sum + _xdot(dsg * s_next, et[rows, :], TN, passes=2)
            dstate[rows, :] = _dot(dye[:, rows], cg, TN) + s_scale[rows, :] * dsg
        dxdt = jnp.concatenate(dxdt_parts, axis=1)
        dxv = dx_ref[...]
        yoff = jnp.concatenate(yoff_parts, axis=1) * ea
        dalpha = dal_diag + _xdot(dyv * yoff - xdt * jnp.concatenate(dxst_parts, axis=1), et)
        end_row = jnp.sum(end_sum, axis=0, keepdims=True)
        dalpha = dalpha + jnp.where(_iota((q, LANES), 0) == q - 1, end_row, 0.0)
        da = _xdot_r(tril, dalpha, TN)
        ddtp = da * a_ref[...] + _xdot(dxdt * xs, et)
        acc_a[...] += _rowsum8(da * dtp)
        acc_d[...] += _rowsum8(_xdot(dyv * xs, et))
        ddt_raw = ddtp * _sigmoid(dt_ref[...] + dtb_ref[...])
        acc_b[...] += _rowsum8(ddt_raw)
        ddt_ref[...] = ddt_raw
        dxs = dxdt * dt_x + dxv * dyv
        dact = jnp.concatenate([dxs] + db_parts + dc_parts, axis=1)
        du_ref[...] = dact * _dsilu(u_ref[...])

        @pl.when((b == nbatch - 1) & (c == nc - 1))
        def _():
            dal_ref[...] = jnp.sum(acc_a[...], axis=0, keepdims=True) * a_ref[...]
            dd_ref[...] = jnp.sum(acc_d[...], axis=0, keepdims=True)
            dtbg_ref[...] = jnp.sum(acc_b[...], axis=0, keepdims=True)
            dnw_ref[...] = jnp.sum(acc_w[...], axis=0, keepdims=True)

    def rowblk(b, c):
        return b * nc + (nc - 1 - c)

    vec = pl.BlockSpec((1, LANES), lambda b, c: (0, 0))
    wvec = pl.BlockSpec((1, SSD_WIDTH), lambda b, c: (0, 0))
    wide = pl.BlockSpec((q, SSD_WIDTH), lambda b, c: (rowblk(b, c), 0))
    zblk = pl.BlockSpec((q, SSD_WIDTH), lambda b, c: (rowblk(b, c), OFF_ZS // SSD_WIDTH))
    return pl.pallas_call(
        body, name=name,
        out_shape=(jax.ShapeDtypeStruct((t, SSD_CONV_DIM), F32), jax.ShapeDtypeStruct(dproj.shape, dproj.dtype),
                   jax.ShapeDtypeStruct((t, LANES), F32),
                   jax.ShapeDtypeStruct((1, LANES), F32), jax.ShapeDtypeStruct((1, LANES), F32),
                   jax.ShapeDtypeStruct((1, LANES), F32), jax.ShapeDtypeStruct((1, SSD_WIDTH), F32)),
        grid=(nbatch, nc),
        in_specs=[wide,
                  pl.BlockSpec((q, SSD_CONV_DIM), lambda b, c: (rowblk(b, c), 0)),
                  pl.BlockSpec((q, LANES), lambda b, c: (rowblk(b, c), OFF_DT // LANES)),
                  zblk, wide,
                  pl.BlockSpec((SSD_WIDTH, SSD_STATE), lambda b, c: (rowblk(b, c), 0)),
                  vec, vec, wvec, wvec, ANY],
        out_specs=(pl.BlockSpec((q, SSD_CONV_DIM), lambda b, c: (rowblk(b, c), 0)),
                   zblk,
                   pl.BlockSpec((q, LANES), lambda b, c: (rowblk(b, c), 0)),
                   vec, vec, vec, wvec),
        input_output_aliases={10: 1},
        scratch_shapes=[pltpu.VMEM((SSD_WIDTH, SSD_STATE), F32), pltpu.VMEM((SUBLANES, LANES), F32),
                        pltpu.VMEM((SUBLANES, LANES), F32), pltpu.VMEM((SUBLANES, LANES), F32),
                        pltpu.VMEM((SUBLANES, SSD_WIDTH), F32)],
        compiler_params=_params(("arbitrary", "arbitrary")),
    )(dycat, u, proj, proj, y, states, dtb, a_neg, d_x, norm_w, dproj)


def _pad_rows(w, rows):
    return jnp.concatenate([w, jnp.zeros((rows - w.shape[0], w.shape[1]), w.dtype)], axis=0)


def _pad_lanes(v):
    return jnp.concatenate([v, jnp.zeros((LANES - v.shape[0],), v.dtype)]).reshape(1, LANES)


def _padded_from_chips(pieces):
    cols = pieces[0].shape[-1]
    lead = pieces[0].shape[:-1]
    parts, pos = [], 0
    for lo, hi, start in sorted(SECTIONS, key=lambda s: s[2]):
        if start > pos:
            parts.append(jnp.zeros(lead + (start - pos,), pieces[0].dtype))
        pos = start + hi - lo
        while lo < hi:
            p = lo // cols
            end = min(hi, (p + 1) * cols)
            parts.append(pieces[p][..., lo - p * cols:end - p * cols])
            lo = end
    if pos < NP:
        parts.append(jnp.zeros(lead + (NP - pos,), pieces[0].dtype))
    return jnp.concatenate(parts, axis=-1)


def _chip_part_from_padded(wp, p, cols):
    lo, hi = p * cols, (p + 1) * cols
    parts = []
    for rs, re, start in SECTIONS:
        a, b = max(lo, rs), min(hi, re)
        if a < b:
            parts.append(wp[..., start + a - rs:start + b - rs])
    return jnp.concatenate(parts, axis=-1)


def _layer_params(li, w_in_p, w_out, conv_w, dw_w, small):
    return dict(
        w_in_p=w_in_p, w_out=w_out,
        conv_w=_pad_rows(conv_w, SUBLANES), dw_w=_pad_rows(dw_w, 32),
        norm_w=small["norm_w"][li].reshape(1, -1),
        conv_b=small["ssd_conv_b"][li].reshape(1, -1),
        dtb=_pad_lanes(small["ssd_dt_bias"][li]),
        a_neg=_pad_lanes(-jnp.exp(small["ssd_a_log"][li])),
        d_x=jnp.repeat(small["ssd_d"][li], SSD_HEAD_DIM).reshape(1, -1),
        ssd_norm_w=small["ssd_norm_w"][li].reshape(1, -1),
        sinks=_pad_lanes(small["attn_sinks"][li]),
        dw_b=small["conf_dw_b"][li].reshape(1, -1),
        ln_w=small["conf_ln_w"][li].reshape(1, -1),
        ln_b=small["conf_ln_b"][li].reshape(1, -1),
    )


def _layer_fwd(x, p, nbatch, seq, tag):
    h, h_t = _rmsnorm_fwd(x, p["norm_w"], name=f"rmsnorm_fwd_{tag}")
    proj = _matmul(h, p["w_in_p"], "nn", F32, 1024, 512, 1024, name=f"proj_fwd_{tag}")
    u = _conv_fwd(proj, OFF_XBC, SSD_CONV_DIM, p["conv_w"], p["conv_b"], SSD_CONV, seq, name=f"ssd_conv_fwd_{tag}")
    ycat = lax.empty((x.shape[0], MIX_WIDTH), MXU_DTYPE)
    y, states, ycat = _ssd_fwd(u, proj, p["dtb"], p["a_neg"], p["d_x"], p["ssd_norm_w"], ycat, nbatch,
                               name=f"ssd_fwd_{tag}")
    ycat, o, lse = _attn_fwd(proj, p["sinks"], ycat, nbatch, name=f"attn_fwd_{tag}")
    c0 = _glu_fwd(proj, name=f"glu_fwd_{tag}")
    c1 = _conv_fwd(c0, 0, CONF_WIDTH, p["dw_w"], p["dw_b"], CONF_KERNEL, seq, name=f"conf_conv_fwd_{tag}")
    ycat = _conf_post_fwd(c1, proj, p["ln_w"], p["ln_b"], ycat, name=f"conf_post_fwd_{tag}")
    x_new = _matmul(ycat, p["w_out"], "nn", F32, 1024, 512, 2048, name=f"out_fwd_{tag}", residual=x)
    return x_new, dict(x=x, h_t=h_t, proj=proj, u=u, y=y, states=states, o=o, lse=lse, c0=c0, c1=c1, ycat=ycat)


def _layer_bwd(dx_out, p, s, nbatch, seq, tag):
    proj = s["proj"]
    dycat = _matmul(dx_out, p["w_out"], "nt", F32, 1024, 1024, 1024, name=f"out_bwd_dy_{tag}")
    dw_out = _matmul(s["ycat"], dx_out, "tn", F32, 1024, 1024, 1024, name=f"out_bwd_dw_{tag}")
    dproj = lax.empty(proj.shape, MXU_DTYPE)
    du, dproj, ddt, da_log, dd, ddtb, dssd_norm_w = _ssd_bwd(
        dycat, s["u"], proj, s["y"], s["states"], p["dtb"], p["a_neg"], p["d_x"], p["ssd_norm_w"], dproj,
        nbatch, name=f"ssd_bwd_{tag}")
    dproj, dconv_w, dconv_b = _conv_bwd(du, proj, OFF_XBC, SSD_CONV_DIM, p["conv_w"], SSD_CONV, seq,
                                        name=f"ssd_conv_bwd_{tag}", into=dproj)
    dproj, dsinks = _attn_bwd(dycat, proj, s["o"], s["lse"], p["sinks"], ddt, dproj, nbatch,
                              name=f"attn_bwd_{tag}")
    dc1, dproj, dln_w, dln_b = _conf_post_bwd(dycat, s["c1"], proj, p["ln_w"], p["ln_b"], dproj,
                                              name=f"conf_post_bwd_{tag}")
    dc0, ddw_w, ddw_b = _conv_bwd(dc1, s["c0"], 0, CONF_WIDTH, p["dw_w"], CONF_KERNEL, seq,
                                  name=f"conf_conv_bwd_{tag}")
    dproj = _glu_bwd(dc0, proj, dproj, name=f"glu_bwd_{tag}")
    dh = _matmul(dproj, p["w_in_p"], "nt", F32, 1024, 1024, 1408, name=f"proj_bwd_dh_{tag}")
    dw_in_p = _matmul(s["h_t"], dproj, "nn", F32, 1024, 512, 4096, name=f"proj_bwd_dw_{tag}")
    dx_in, dnorm_w = _rmsnorm_bwd(dh, s["x"], p["norm_w"], dx_out, name=f"rmsnorm_bwd_{tag}")
    grads = dict(
        norm_w=dnorm_w[0], w_in_p=dw_in_p, ssd_conv_w=dconv_w[:SSD_CONV], ssd_conv_b=dconv_b[0],
        ssd_dt_bias=ddtb[0, :SSD_HEADS], ssd_a_log=da_log[0, :SSD_HEADS], ssd_d=dd[0, :SSD_HEADS],
        ssd_norm_w=dssd_norm_w[0], attn_sinks=dsinks[0, :ATTN_Q_HEADS], conf_dw_w=ddw_w[:CONF_KERNEL],
        conf_dw_b=ddw_b[0], conf_ln_w=dln_w[0], conf_ln_b=dln_b[0], w_out=dw_out)
    return dx_in, grads


def _local_step(x, target, layer_params, final_norm_w):
    nbatch, seq, d = x.shape
    xt = x.reshape(nbatch * seq, d)
    saved = []
    for li, p in enumerate(layer_params):
        xt, s = _layer_fwd(xt, p, nbatch, seq, f"l{li}")
        saved.append(s)
    loss, dx, dfinal = _loss_head(xt, target.reshape(nbatch * seq, d), final_norm_w.reshape(1, d), name="loss_head")
    grads = [None] * len(layer_params)
    for li in reversed(range(len(layer_params))):
        dx, grads[li] = _layer_bwd(dx, layer_params[li], saved[li], nbatch, seq, f"l{li}")
    return loss[0, 0], dx.reshape(nbatch, seq, d), grads, dfinal[0]


MESH = pl.DeviceIdType.MESH
N_CHIPS = 4


def _mesh_pos():
    return lax.axis_index("x"), lax.axis_index("y"), lax.axis_index("c")


def _other_chips(x, y):
    return [(1 - x, y), (x, 1 - y), (1 - x, 1 - y)]


def _gather_weights(big, small, name):
    nbig, nsmall = len(big), len(small)
    n_ici = 3 * (nbig + nsmall)
    n_fwd = 3 * nbig

    def body(*refs):
        ins = refs[:nbig + nsmall]
        outs = refs[nbig + nsmall:2 * (nbig + nsmall)]
        send_sems, recv_sems = refs[2 * (nbig + nsmall):]
        x, y, c = _mesh_pos()
        me = 2 * x + y
        sibling = (x, y, 1 - c)
        chips = _other_chips(x, y)

        def ici(a, j, origin, dest):
            if a < nbig:
                src = ins[a].at[c] if origin is None else outs[a].at[origin, c]
                dst = outs[a].at[me if origin is None else origin, c]
            else:
                src = ins[a] if origin is None else outs[a].at[origin]
                dst = outs[a].at[me if origin is None else origin]
            k = a * 3 + j
            return pltpu.make_async_remote_copy(src_ref=src, dst_ref=dst, send_sem=send_sems.at[k],
                                                recv_sem=recv_sems.at[k], device_id=dest, device_id_type=MESH)

        def fwd(a, j, origin, half):
            k = n_ici + a * 3 + j
            ref = outs[a].at[origin, half]
            return pltpu.make_async_remote_copy(src_ref=ref, dst_ref=ref, send_sem=send_sems.at[k],
                                                recv_sem=recv_sems.at[k], device_id=sibling, device_id_type=MESH)

        sends = []
        for j, (px, py) in enumerate(chips):
            for a in range(nbig + nsmall):
                cp = ici(a, j, None, (px, py, c))
                cp.start()
                sends.append(cp)
        for j, (px, py) in enumerate(chips):
            origin = 2 * px + py
            for a in range(nbig):
                ici(a, j, origin, (px, py, c)).wait_recv()
                cp = fwd(a, j, origin, c)
                cp.start()
                sends.append(cp)
        for j, (px, py) in enumerate(chips):
            origin = 2 * px + py
            for a in range(nbig, nbig + nsmall):
                ici(a, j, origin, (px, py, c)).wait_recv()
            for a in range(nbig):
                fwd(a, j, origin, 1 - c).wait_recv()
        for cp in sends:
            cp.wait_send()

    out_shape = tuple(jax.ShapeDtypeStruct((N_CHIPS,) + a.shape, a.dtype) for a in list(big) + list(small))
    return pl.pallas_call(
        body, name=name, out_shape=out_shape,
        in_specs=[ANY] * (nbig + nsmall), out_specs=tuple([ANY] * (nbig + nsmall)),
        scratch_shapes=[pltpu.SemaphoreType.DMA((n_ici + n_fwd,)), pltpu.SemaphoreType.DMA((n_ici + n_fwd,))],
    )(*big, *small)


def _pair_swap_halves(arrs, name):
    n = len(arrs)

    def body(*refs):
        ins, outs = refs[:n], refs[n:2 * n]
        send_sems, recv_sems = refs[2 * n:]
        x, y, c = _mesh_pos()
        cps = [pltpu.make_async_remote_copy(src_ref=ins[a].at[1 - c], dst_ref=outs[a], send_sem=send_sems.at[a],
                                            recv_sem=recv_sems.at[a], device_id=(x, y, 1 - c), device_id_type=MESH)
               for a in range(n)]
        for cp in cps:
            cp.start()
        for cp in cps:
            cp.wait()

    return pl.pallas_call(
        body, name=name, out_shape=tuple(jax.ShapeDtypeStruct(a.shape[1:], a.dtype) for a in arrs),
        in_specs=[ANY] * n, out_specs=tuple([ANY] * n),
        scratch_shapes=[pltpu.SemaphoreType.DMA((n,)), pltpu.SemaphoreType.DMA((n,))],
    )(*arrs)


def _chip_scatter(arrs, name):
    n = len(arrs)

    def body(*refs):
        ins, outs = refs[:n], refs[n:2 * n]
        send_sems, recv_sems = refs[2 * n:]
        x, y, c = _mesh_pos()
        me = 2 * x + y
        cps = []
        for j, (px, py) in enumerate(_other_chips(x, y)):
            for a in range(n):
                cps.append(pltpu.make_async_remote_copy(
                    src_ref=ins[a].at[2 * px + py], dst_ref=outs[a].at[me], send_sem=send_sems.at[a * 3 + j],
                    recv_sem=recv_sems.at[a * 3 + j], device_id=(px, py, c), device_id_type=MESH))
        for cp in cps:
            cp.start()
        for cp in cps:
            cp.wait()

    return pl.pallas_call(
        body, name=name, out_shape=tuple(jax.ShapeDtypeStruct(a.shape, a.dtype) for a in arrs),
        in_specs=[ANY] * n, out_specs=tuple([ANY] * n),
        scratch_shapes=[pltpu.SemaphoreType.DMA((3 * n,)), pltpu.SemaphoreType.DMA((3 * n,))],
    )(*arrs)


def _pair_gather(arrs, name):
    n = len(arrs)

    def body(*refs):
        outs = refs[n:2 * n]
        send_sems, recv_sems = refs[2 * n:]
        x, y, c = _mesh_pos()
        cps = [pltpu.make_async_remote_copy(src_ref=outs[a].at[c], dst_ref=outs[a].at[c], send_sem=send_sems.at[a],
                                            recv_sem=recv_sems.at[a], device_id=(x, y, 1 - c), device_id_type=MESH)
               for a in range(n)]
        for cp in cps:
            cp.start()
        for cp in cps:
            cp.wait()

    return pl.pallas_call(
        body, name=name, out_shape=tuple(jax.ShapeDtypeStruct(a.shape, a.dtype) for a in arrs),
        in_specs=[ANY] * n, out_specs=tuple([ANY] * n),
        input_output_aliases={a: a for a in range(n)},
        scratch_shapes=[pltpu.SemaphoreType.DMA((n,)), pltpu.SemaphoreType.DMA((n,))],
    )(*arrs)


N_DEV = 8


def _allreduce_small(pack, name):
    r = pack.shape[0]

    def body(p_ref, o_ref, land, send_sems, recv_sems):
        x, y, c = _mesh_pos()
        me = 4 * x + 2 * y + c
        cps = []
        for k in range(1, N_DEV):
            peer = (x ^ (k >> 2), y ^ ((k >> 1) & 1), c ^ (k & 1))
            cps.append(pltpu.make_async_remote_copy(src_ref=p_ref, dst_ref=land.at[me], send_sem=send_sems.at[k - 1],
                                                    recv_sem=recv_sems.at[k - 1], device_id=peer, device_id_type=MESH))
        for cp in cps:
            cp.start()
        land[me] = p_ref[...]
        for cp in cps:
            cp.wait()
        total = land[0]
        for d in range(1, N_DEV):
            total = total + land[d]
        o_ref[...] = total

    vm = pl.BlockSpec(memory_space=pltpu.VMEM)
    return pl.pallas_call(
        body, name=name, out_shape=jax.ShapeDtypeStruct(pack.shape, F32),
        in_specs=[vm], out_specs=vm,
        scratch_shapes=[pltpu.VMEM((N_DEV, r, LANES), F32), pltpu.SemaphoreType.DMA((N_DEV - 1,)),
                        pltpu.SemaphoreType.DMA((N_DEV - 1,))],
    )(pack)


BIG_ROWS = 128


def _cast_mxu(w, name):
    nl, r, cdim = w.shape
    tr = BIG_ROWS

    def body(w_ref, o_ref):
        o_ref[...] = w_ref[...].astype(o_ref.dtype)

    blk = pl.BlockSpec((None, tr, cdim), lambda l, i: (l, i, 0))
    return pl.pallas_call(
        body, name=name, out_shape=jax.ShapeDtypeStruct(w.shape, MXU_DTYPE),
        grid=(nl, r // tr), in_specs=[blk], out_specs=blk,
        compiler_params=_params(("parallel", "parallel")),
    )(w)


def _pair_sum(parts, sib, which, out_dtype, name):
    _, k, r, cdim = parts.shape
    tr = BIG_ROWS

    def body(sel_ref, p_ref, s_ref, o_ref):
        o_ref[...] = (p_ref[...] + s_ref[...]).astype(o_ref.dtype)

    grid_spec = pltpu.PrefetchScalarGridSpec(
        num_scalar_prefetch=1, grid=(k, r // tr),
        in_specs=[pl.BlockSpec((None, None, tr, cdim), lambda l, i, sel: (sel[0], l, i, 0)),
                  pl.BlockSpec((None, tr, cdim), lambda l, i, sel: (l, i, 0))],
        out_specs=pl.BlockSpec((None, tr, cdim), lambda l, i, sel: (l, i, 0)))
    return pl.pallas_call(
        body, name=name, out_shape=jax.ShapeDtypeStruct((k, r, cdim), out_dtype), grid_spec=grid_spec,
        compiler_params=_params(("parallel", "parallel")),
    )(which.reshape(1).astype(jnp.int32), parts, sib)


def _sum_lead(parts, which, name):
    k, r, cdim = parts.shape
    tr = BIG_ROWS

    def body(sel_ref, p_ref, o_ref):
        total = p_ref[0].astype(F32)
        for a in range(1, k):
            total = total + p_ref[a].astype(F32)
        o_ref[...] = total

    grid_spec = pltpu.PrefetchScalarGridSpec(
        num_scalar_prefetch=1, grid=(r // tr,),
        in_specs=[pl.BlockSpec((k, tr, cdim), lambda i, sel: (0, i, 0))],
        out_specs=pl.BlockSpec((None, tr, cdim), lambda i, sel: (sel[0], i, 0)))
    return pl.pallas_call(
        body, name=name, out_shape=jax.ShapeDtypeStruct((2, r, cdim), F32), grid_spec=grid_spec,
        compiler_params=_params(("parallel",)),
    )(which.reshape(1).astype(jnp.int32), parts)


def _adam_math(w, g, m, v):
    m2 = ADAM_B1 * m + (1.0 - ADAM_B1) * g
    v2 = ADAM_B2 * v + (1.0 - ADAM_B2) * (g * g)
    m_hat = m2 / (1.0 - ADAM_B1 ** ADAM_STEP)
    v_hat = v2 / (1.0 - ADAM_B2 ** ADAM_STEP)
    delta = -ADAM_LR * (m_hat / (jnp.sqrt(v_hat) + ADAM_EPS) + ADAM_WD * w)
    return delta, m2, v2


def _adam_big(w, g, m, v, name):
    nl, r, cdim = w.shape
    tr = BIG_ROWS

    def body(w_ref, g_ref, m_ref, v_ref, d_ref, mo_ref, vo_ref):
        delta, m2, v2 = _adam_math(w_ref[...], g_ref[...], m_ref[...], v_ref[...])
        d_ref[...] = delta
        mo_ref[...] = m2
        vo_ref[...] = v2

    blk = pl.BlockSpec((None, tr, cdim), lambda l, i: (l, i, 0))
    shp = jax.ShapeDtypeStruct(w.shape, F32)
    return pl.pallas_call(
        body, name=name, out_shape=(shp, shp, shp),
        grid=(nl, r // tr), in_specs=[blk] * 4, out_specs=(blk, blk, blk),
        compiler_params=_params(("parallel", "parallel")),
    )(w, g, m, v)


def _adam_cols_major(w, g, m, v, name):
    cdim, nl, r = w.shape
    tc = BIG_ROWS

    def body(w_ref, g_ref, m_ref, v_ref, d_ref, mo_ref, vo_ref):
        delta, m2, v2 = _adam_math(w_ref[...], g_ref[...], m_ref[...], v_ref[...])
        d_ref[...] = delta
        mo_ref[...] = m2
        vo_ref[...] = v2

    blk = pl.BlockSpec((tc, nl, r), lambda i: (i, 0, 0))
    shp = jax.ShapeDtypeStruct(w.shape, F32)
    return pl.pallas_call(
        body, name=name, out_shape=(shp, shp, shp),
        grid=(pl.cdiv(cdim, tc),), in_specs=[blk] * 4, out_specs=(blk, blk, blk),
        compiler_params=_params(("parallel",)),
    )(w, g, m, v)


def _adam_small(ws, gs, ms, vs, name):
    n = len(ws)

    def body(*refs):
        w_refs, g_refs, m_refs, v_refs = (refs[k * n:(k + 1) * n] for k in range(4))
        d_refs, mo_refs, vo_refs = (refs[(4 + k) * n:(5 + k) * n] for k in range(3))
        for a in range(n):
            delta, m2, v2 = _adam_math(w_refs[a][...], g_refs[a][...], m_refs[a][...], v_refs[a][...])
            d_refs[a][...] = delta
            mo_refs[a][...] = m2
            vo_refs[a][...] = v2

    shapes = tuple(jax.ShapeDtypeStruct(w.shape, F32) for w in ws)
    vm = pl.BlockSpec(memory_space=pltpu.VMEM)
    outs = pl.pallas_call(body, name=name, out_shape=shapes * 3, in_specs=[vm] * (4 * n),
                          out_specs=tuple([vm] * (3 * n)))(*ws, *gs, *ms, *vs)
    return outs[:n], outs[n:2 * n], outs[2 * n:]


PACK_TILE = SUBLANES * LANES


def _pack(arrays):
    rows = []
    for a in arrays:
        flat = a.reshape(-1)
        pad = (-flat.shape[0]) % PACK_TILE
        if pad:
            flat = jnp.concatenate([flat, jnp.zeros((pad,), flat.dtype)])
        rows.append(flat.reshape(-1, LANES))
    return jnp.concatenate(rows, axis=0)


def _unpack(pack, shapes):
    outs, row = [], 0
    for shp in shapes:
        n = int(np.prod(shp))
        nrows = -(-n // PACK_TILE) * SUBLANES
        outs.append(pack[row:row + nrows].reshape(-1)[:n].reshape(shp))
        row += nrows
    return outs


SMALL = ["norm_w", "ssd_conv_b", "ssd_dt_bias", "ssd_a_log", "ssd_d", "ssd_norm_w", "attn_sinks",
         "conf_dw_b", "conf_ln_w", "conf_ln_b"]
WEIGHTS = ["norm_w", "w_in", "ssd_conv_w", "ssd_conv_b", "ssd_dt_bias", "ssd_a_log", "ssd_d", "ssd_norm_w",
           "attn_sinks", "conf_dw_w", "conf_dw_b", "conf_ln_w", "conf_ln_b", "w_out", "final_norm_w"]


def kernel(x, norm_w, w_in, ssd_conv_w, ssd_conv_b, ssd_dt_bias, ssd_a_log, ssd_d, ssd_norm_w, attn_sinks, conf_dw_w, conf_dw_b, conf_ln_w, conf_ln_b, w_out, final_norm_w, loss_target, m_norm_w, m_w_in, m_ssd_conv_w, m_ssd_conv_b, m_ssd_dt_bias, m_ssd_a_log, m_ssd_d, m_ssd_norm_w, m_attn_sinks, m_conf_dw_w, m_conf_dw_b, m_conf_ln_w, m_conf_ln_b, m_w_out, m_final_norm_w, v_norm_w, v_w_in, v_ssd_conv_w, v_ssd_conv_b, v_ssd_dt_bias, v_ssd_a_log, v_ssd_d, v_ssd_norm_w, v_attn_sinks, v_conf_dw_w, v_conf_dw_b, v_conf_ln_w, v_conf_ln_b, v_w_out, v_final_norm_w):
    w = dict(norm_w=norm_w, w_in=w_in, ssd_conv_w=ssd_conv_w, ssd_conv_b=ssd_conv_b, ssd_dt_bias=ssd_dt_bias,
             ssd_a_log=ssd_a_log, ssd_d=ssd_d, ssd_norm_w=ssd_norm_w, attn_sinks=attn_sinks, conf_dw_w=conf_dw_w,
             conf_dw_b=conf_dw_b, conf_ln_w=conf_ln_w, conf_ln_b=conf_ln_b, w_out=w_out, final_norm_w=final_norm_w)
    m = dict(norm_w=m_norm_w, w_in=m_w_in, ssd_conv_w=m_ssd_conv_w, ssd_conv_b=m_ssd_conv_b,
             ssd_dt_bias=m_ssd_dt_bias, ssd_a_log=m_ssd_a_log, ssd_d=m_ssd_d, ssd_norm_w=m_ssd_norm_w,
             attn_sinks=m_attn_sinks, conf_dw_w=m_conf_dw_w, conf_dw_b=m_conf_dw_b, conf_ln_w=m_conf_ln_w,
             conf_ln_b=m_conf_ln_b, w_out=m_w_out, final_norm_w=m_final_norm_w)
    v = dict(norm_w=v_norm_w, w_in=v_w_in, ssd_conv_w=v_ssd_conv_w, ssd_conv_b=v_ssd_conv_b,
             ssd_dt_bias=v_ssd_dt_bias, ssd_a_log=v_ssd_a_log, ssd_d=v_ssd_d, ssd_norm_w=v_ssd_norm_w,
             attn_sinks=v_attn_sinks, conf_dw_w=v_conf_dw_w, conf_dw_b=v_conf_dw_b, conf_ln_w=v_conf_ln_w,
             conf_ln_b=v_conf_ln_b, w_out=v_w_out, final_norm_w=v_final_norm_w)
    depth = w_in.shape[0]
    me = 2 * lax.axis_index("x") + lax.axis_index("y")

    assert depth == 2
    own = [_cast_mxu(w_in, name="cast_w_in"), _cast_mxu(w_out, name="cast_w_out"), ssd_conv_w, conf_dw_w]
    gathered = _gather_weights(own[:2], own[2:], name="gather_weights")
    g_in, g_out, g_conv, g_dw = [lax.dynamic_update_index_in_dim(g_all, mine, me, 0)
                                 for g_all, mine in zip(gathered, own)]
    layer_params = []
    for li in range(depth):
        w_in_p = _padded_from_chips([g_in[p, li] for p in range(N_CHIPS)])
        w_out_full = jnp.concatenate([g_out[p, li] for p in range(N_CHIPS)], axis=0)
        conv_full = jnp.concatenate([g_conv[p, li] for p in range(N_CHIPS)], axis=1)
        dw_full = jnp.concatenate([g_dw[p, li] for p in range(N_CHIPS)], axis=1)
        layer_params.append(_layer_params(li, w_in_p, w_out_full, conv_full, dw_full, w))

    loss, grad_x, grads, dfinal = _local_step(x, loss_target, layer_params, final_norm_w)

    small_list = [grads[li][n] for li in range(depth) for n in SMALL]
    small_list += [grads[li][n] for li in range(depth) for n in ("ssd_conv_w", "conf_dw_w")]
    small_list += [dfinal, loss.reshape(1)]
    small_shapes = [a.shape for a in small_list]
    reduced = _unpack(_allreduce_small(_pack(small_list), name="allreduce_small"), small_shapes)
    ns = len(SMALL)
    g = {n: jnp.stack([reduced[li * ns + i] for li in range(depth)]) for i, n in enumerate(SMALL)}
    conv_w_cols, dw_w_cols = ssd_conv_w.shape[2], conf_dw_w.shape[2]
    g["ssd_conv_w"] = jnp.stack([lax.dynamic_slice_in_dim(reduced[depth * ns + 2 * li], me * conv_w_cols,
                                                          conv_w_cols, axis=1) for li in range(depth)])
    g["conf_dw_w"] = jnp.stack([lax.dynamic_slice_in_dim(reduced[depth * ns + 2 * li + 1], me * dw_w_cols,
                                                         dw_w_cols, axis=1) for li in range(depth)])
    g["final_norm_w"] = reduced[-2]
    loss_total = reduced[-1][0]

    cols = w_in.shape[2]
    rows_out = w_out.shape[1]
    p_in = jnp.stack([jnp.stack([_chip_part_from_padded(grads[li]["w_in_p"], p, cols) for p in range(N_CHIPS)])
                      for li in range(depth)])
    p_out = jnp.stack([grads[li]["w_out"].reshape(N_CHIPS, rows_out, D_MODEL) for li in range(depth)])
    c = lax.axis_index("c")
    sib_in, sib_out = _pair_swap_halves([p_in, p_out], name="grad_pair_swap")
    s_in = _pair_sum(p_in, sib_in, c, MXU_DTYPE, name="grad_pair_sum_in")
    s_out = _pair_sum(p_out, sib_out, c, MXU_DTYPE, name="grad_pair_sum_out")
    r_in, r_out = _chip_scatter([s_in, s_out], name="grad_chip_scatter")
    r_in = lax.dynamic_update_index_in_dim(r_in, lax.dynamic_index_in_dim(s_in, me, 0, keepdims=False), me, 0)
    r_out = lax.dynamic_update_index_in_dim(r_out, lax.dynamic_index_in_dim(s_out, me, 0, keepdims=False), me, 0)
    t_in = _sum_lead(r_in, c, name="grad_chip_sum_in")
    t_out = _sum_lead(r_out, c, name="grad_chip_sum_out")
    g_w_in, g_w_out = _pair_gather([t_in, t_out], name="grad_pair_gather")

    outs_g, outs_d, outs_m, outs_v = {"w_in": g_w_in, "w_out": g_w_out}, {}, {}, {}
    to_cols, from_cols = (2, 0, 1), (1, 2, 0)
    outs_d["w_in"], outs_m["w_in"], outs_v["w_in"] = [
        jnp.transpose(a, from_cols) for a in _adam_cols_major(
            *[jnp.transpose(a, to_cols) for a in (w_in, g_w_in, m_w_in, v_w_in)], name="adam_w_in")]
    outs_d["w_out"], outs_m["w_out"], outs_v["w_out"] = _adam_big(w_out, g_w_out, m_w_out, v_w_out,
                                                                  name="adam_w_out")
    small_names = [n for n in WEIGHTS if n not in ("w_in", "w_out")]
    def as2d(a):
        return a.reshape(1, -1) if a.ndim == 1 else a

    deltas, new_ms, new_vs = _adam_small(*[[as2d(src[n]) for n in small_names] for src in (w, g, m, v)],
                                         name="adam_small")
    for n, dn, mn, vn in zip(small_names, deltas, new_ms, new_vs):
        outs_g[n], outs_d[n], outs_m[n], outs_v[n] = (g[n], dn.reshape(w[n].shape), mn.reshape(w[n].shape),
                                                      vn.reshape(w[n].shape))
    return (loss_total, grad_x, *[outs_g[n] for n in WEIGHTS], *[outs_d[n] for n in WEIGHTS],
            *[outs_m[n] for n in WEIGHTS], *[outs_v[n] for n in WEIGHTS])
```

```python
import functools
import math

import jax
import jax.numpy as jnp
import numpy as np
from jax import lax
from jax.experimental import pallas as pl
from jax.experimental.pallas import tpu as pltpu

F32 = jnp.float32
BF16 = jnp.bfloat16
MXU_DTYPE = BF16

D_MODEL = 1024
DEPTH = 2
SSD_HEADS = 16
SSD_HEAD_DIM = 64
SSD_STATE = 128
SSD_CONV = 4
CHUNK = 128
SSD_CONV_DIM = 1536
ATTN_HEAD_DIM = 64
ATTN_Q_HEADS = 8
WINDOW = 128
CONF_WIDTH = 512
CONF_KERNEL = 31
MIX_WIDTH = 2048
D_IN_PROJ = 5392
EPS = 1e-5

ADAM_LR = 0.001
ADAM_B1 = 0.9
ADAM_B2 = 0.999
ADAM_EPS = 1e-08
ADAM_WD = 0.01
ADAM_STEP = 10

LANES = 128
SUBLANES = 8
VMEM_LIMIT = 48 * 1024 * 1024

NP = 5632
OFF_ZA, OFF_Q, OFF_K, OFF_V, OFF_DT = 0, 512, 1024, 1152, 1280
ATTN_GROUP = 1536
OFF_XBC = 1536
OFF_CONF = 3072
OFF_ZS = 4096
OFF_ZC = 5120
SECTIONS = ((0, 1024, OFF_ZS), (1024, 1536, OFF_ZA), (1536, 2048, OFF_ZC), (2048, 3584, OFF_XBC),
            (3584, 3600, OFF_DT), (3600, 4368, OFF_Q), (4368, 5392, OFF_CONF))

YCAT_ATTN, YCAT_CONF = 1024, 1536
ANY = pl.BlockSpec(memory_space=pl.ANY)

NN = (((1,), (0,)), ((), ()))
NT = (((1,), (1,)), ((), ()))
TN = (((0,), (0,)), ((), ()))


def _params(sem):
    return pltpu.CompilerParams(dimension_semantics=sem, vmem_limit_bytes=VMEM_LIMIT)


def _dot(a, b, dims=NN):
    return lax.dot_general(a.astype(MXU_DTYPE), b.astype(MXU_DTYPE), dims, preferred_element_type=F32)


def _split_bf16(a, passes):
    pieces = []
    r = a
    for _ in range(passes):
        p = r.astype(BF16)
        pieces.append(p)
        r = r - p.astype(F32)
    return pieces


def _xdot(a, sel, dims=NN, passes=2):
    out = None
    for p in _split_bf16(a, passes):
        t = lax.dot_general(p, sel, dims, preferred_element_type=F32)
        out = t if out is None else out + t
    return out


def _xdot_r(sel, b, dims=NN, passes=3):
    out = None
    for p in _split_bf16(b, passes):
        t = lax.dot_general(sel, p, dims, preferred_element_type=F32)
        out = t if out is None else out + t
    return out


def _sigmoid(x):
    return 1.0 / (1.0 + jnp.exp(-x))


def _silu(x):
    return x * _sigmoid(x)


def _dsilu(x):
    s = _sigmoid(x)
    return s * (1.0 + x * (1.0 - s))


def _softplus(x):
    return jnp.maximum(x, 0.0) + jnp.log(1.0 + jnp.exp(-jnp.abs(x)))


def _rowsum8(x):
    r, c = x.shape
    return jnp.sum(x.reshape(r // SUBLANES, SUBLANES, c), axis=0)


def _iota(shape, dim):
    return lax.broadcasted_iota(jnp.int32, shape, dim)


def _matmul(a, b, form, out_dtype, tm, tn, tk, name, residual=None):
    if form == "nn":
        (m, k), n = a.shape, b.shape[1]
    elif form == "nt":
        (m, k), n = a.shape, b.shape[0]
    else:
        (k, m), n = a.shape, b.shape[1]
    tm, tn, tk = min(tm, m), min(tn, n), min(tk, k)
    assert m % tm == 0 and n % tn == 0 and k % tk == 0, (name, m, n, k, tm, tn, tk)
    if form == "nn":
        a_spec = pl.BlockSpec((tm, tk), lambda i, j, s: (i, s))
        b_spec = pl.BlockSpec((tk, tn), lambda i, j, s: (s, j))
        dims = NN
    elif form == "nt":
        (m, k), n = a.shape, b.shape[0]
        a_spec = pl.BlockSpec((tm, tk), lambda i, j, s: (i, s))
        b_spec = pl.BlockSpec((tn, tk), lambda i, j, s: (j, s))
        dims = NT
    else:
        (k, m), n = a.shape, b.shape[1]
        a_spec = pl.BlockSpec((tk, tm), lambda i, j, s: (s, i))
        b_spec = pl.BlockSpec((tk, tn), lambda i, j, s: (s, j))
        dims = TN
    nk = k // tk
    has_res = residual is not None

    def body_single(a_ref, b_ref, *rest):
        o = _dot(a_ref[...], b_ref[...], dims)
        if has_res:
            o = o + rest[0][...]
        rest[-1][...] = o.astype(out_dtype)

    def body(a_ref, b_ref, *rest):
        if has_res:
            r_ref, o_ref, acc = rest
        else:
            o_ref, acc = rest
        s = pl.program_id(2)

        @pl.when(s == 0)
        def _():
            acc[...] = jnp.zeros_like(acc)

        acc[...] += _dot(a_ref[...], b_ref[...], dims)

        @pl.when(s == nk - 1)
        def _():
            o = acc[...]
            if has_res:
                o = o + r_ref[...]
            o_ref[...] = o.astype(out_dtype)

    in_specs = [a_spec, b_spec]
    args = [a, b]
    if has_res:
        in_specs.append(pl.BlockSpec((tm, tn), lambda i, j, s: (i, j)))
        args.append(residual)
    return pl.pallas_call(
        body_single if nk == 1 else body, name=name,
        out_shape=jax.ShapeDtypeStruct((m, n), out_dtype),
        grid=(m // tm, n // tn, nk),
        in_specs=in_specs,
        out_specs=pl.BlockSpec((tm, tn), lambda i, j, s: (i, j)),
        scratch_shapes=[] if nk == 1 else [pltpu.VMEM((tm, tn), F32)],
        compiler_params=_params(("parallel", "parallel", "arbitrary")),
    )(*args)


ROW_TILE = 256


def _rmsnorm_fwd(x, w, name):
    t, d = x.shape
    tm = ROW_TILE

    def body(x_ref, w_ref, o_ref, ot_ref):
        xv = x_ref[...]
        rstd = lax.rsqrt(jnp.mean(xv * xv, axis=-1, keepdims=True) + EPS)
        h = xv * rstd * w_ref[...]
        o_ref[...] = h.astype(o_ref.dtype)
        ot_ref[...] = h.T.astype(ot_ref.dtype)

    return pl.pallas_call(
        body, name=name,
        out_shape=(jax.ShapeDtypeStruct((t, d), MXU_DTYPE), jax.ShapeDtypeStruct((d, t), MXU_DTYPE)),
        grid=(t // tm,),
        in_specs=[pl.BlockSpec((tm, d), lambda i: (i, 0)), pl.BlockSpec((1, d), lambda i: (0, 0))],
        out_specs=(pl.BlockSpec((tm, d), lambda i: (i, 0)), pl.BlockSpec((d, tm), lambda i: (0, i))),
        compiler_params=_params(("parallel",)),
    )(x, w)


def _rmsnorm_bwd(dh, x, w, dres, name):
    t, d = x.shape
    tm = ROW_TILE
    nt = t // tm

    def body(dh_ref, x_ref, w_ref, dr_ref, dx_ref, dw_ref, acc):
        i = pl.program_id(0)

        @pl.when(i == 0)
        def _():
            acc[...] = jnp.zeros_like(acc)

        xv = x_ref[...]
        rstd = lax.rsqrt(jnp.mean(xv * xv, axis=-1, keepdims=True) + EPS)
        xh = xv * rstd
        dhv = dh_ref[...]
        g = dhv * w_ref[...]
        dx_ref[...] = dr_ref[...] + rstd * (g - xh * jnp.mean(g * xh, axis=-1, keepdims=True))
        acc[...] += _rowsum8(dhv * xh)

        @pl.when(i == nt - 1)
        def _():
            dw_ref[...] = jnp.sum(acc[...], axis=0, keepdims=True)

    row = pl.BlockSpec((tm, d), lambda i: (i, 0))
    vec = pl.BlockSpec((1, d), lambda i: (0, 0))
    return pl.pallas_call(
        body, name=name,
        out_shape=(jax.ShapeDtypeStruct((t, d), F32), jax.ShapeDtypeStruct((1, d), F32)),
        grid=(nt,),
        in_specs=[row, row, vec, row],
        out_specs=(row, vec),
        scratch_shapes=[pltpu.VMEM((SUBLANES, d), F32)],
        compiler_params=_params(("arbitrary",)),
    )(dh, x, w, dres)


def _loss_head(xf, target, w, name):
    t, d = xf.shape
    tm = ROW_TILE
    nt = t // tm

    def body(x_ref, t_ref, w_ref, loss_ref, dx_ref, dw_ref, lacc, wacc):
        i = pl.program_id(0)

        @pl.when(i == 0)
        def _():
            lacc[...] = jnp.zeros_like(lacc)
            wacc[...] = jnp.zeros_like(wacc)

        xv = x_ref[...]
        rstd = lax.rsqrt(jnp.mean(xv * xv, axis=-1, keepdims=True) + EPS)
        xh = xv * rstd
        err = xh * w_ref[...] - t_ref[...]
        lacc[...] += jnp.sum(err * err)
        dy = err * (1.0 / d)
        g = dy * w_ref[...]
        dx_ref[...] = rstd * (g - xh * jnp.mean(g * xh, axis=-1, keepdims=True))
        wacc[...] += _rowsum8(dy * xh)

        @pl.when(i == nt - 1)
        def _():
            loss_ref[...] = lacc[...] * (0.5 / d)
            dw_ref[...] = jnp.sum(wacc[...], axis=0, keepdims=True)

    row = pl.BlockSpec((tm, d), lambda i: (i, 0))
    vec = pl.BlockSpec((1, d), lambda i: (0, 0))
    return pl.pallas_call(
        body, name=name,
        out_shape=(jax.ShapeDtypeStruct((SUBLANES, LANES), F32), jax.ShapeDtypeStruct((t, d), F32),
                   jax.ShapeDtypeStruct((1, d), F32)),
        grid=(nt,),
        in_specs=[row, row, vec],
        out_specs=(pl.BlockSpec((SUBLANES, LANES), lambda i: (0, 0)), row, vec),
        scratch_shapes=[pltpu.VMEM((SUBLANES, LANES), F32), pltpu.VMEM((SUBLANES, d), F32)],
        compiler_params=_params(("arbitrary",)),
    )(xf, target, w)


def _glu_fwd(proj, name):
    t = proj.shape[0]
    tm, cw = ROW_TILE, CONF_WIDTH

    def body(a_ref, g_ref, o_ref):
        o_ref[...] = a_ref[...] * _sigmoid(g_ref[...])

    return pl.pallas_call(
        body, name=name,
        out_shape=jax.ShapeDtypeStruct((t, cw), F32),
        grid=(t // tm,),
        in_specs=[pl.BlockSpec((tm, cw), lambda i: (i, OFF_CONF // cw)),
                  pl.BlockSpec((tm, cw), lambda i: (i, OFF_CONF // cw + 1))],
        out_specs=pl.BlockSpec((tm, cw), lambda i: (i, 0)),
        compiler_params=_params(("parallel",)),
    )(proj, proj)


def _glu_bwd(dc0, proj, dproj, name):
    t = proj.shape[0]
    tm, cw = ROW_TILE, CONF_WIDTH

    def body(d_ref, a_ref, g_ref, _, o_ref):
        s = _sigmoid(g_ref[...])
        dv = d_ref[...]
        o_ref[:, :cw] = (dv * s).astype(o_ref.dtype)
        o_ref[:, cw:] = (dv * a_ref[...] * s * (1.0 - s)).astype(o_ref.dtype)

    return pl.pallas_call(
        body, name=name,
        out_shape=jax.ShapeDtypeStruct(dproj.shape, dproj.dtype),
        grid=(t // tm,),
        in_specs=[pl.BlockSpec((tm, cw), lambda i: (i, 0)),
                  pl.BlockSpec((tm, cw), lambda i: (i, OFF_CONF // cw)),
                  pl.BlockSpec((tm, cw), lambda i: (i, OFF_CONF // cw + 1)), ANY],
        out_specs=pl.BlockSpec((tm, 2 * cw), lambda i: (i, OFF_CONF // (2 * cw))),
        input_output_aliases={3: 0},
        compiler_params=_params(("parallel",)),
    )(dc0, proj, proj, dproj)


def _conf_post_fwd(c1, proj, ln_w, ln_b, ycat, name):
    t = c1.shape[0]
    tm, cw = ROW_TILE, CONF_WIDTH

    def body(c_ref, z_ref, w_ref, b_ref, _, o_ref):
        cv = c_ref[...]
        xc = cv - jnp.mean(cv, axis=-1, keepdims=True)
        rstd = lax.rsqrt(jnp.mean(xc * xc, axis=-1, keepdims=True) + EPS)
        c2 = xc * rstd * w_ref[...] + b_ref[...]
        o_ref[...] = (_silu(c2) * _silu(z_ref[...])).astype(o_ref.dtype)

    vec = pl.BlockSpec((1, cw), lambda i: (0, 0))
    return pl.pallas_call(
        body, name=name,
        out_shape=jax.ShapeDtypeStruct(ycat.shape, ycat.dtype),
        grid=(t // tm,),
        in_specs=[pl.BlockSpec((tm, cw), lambda i: (i, 0)),
                  pl.BlockSpec((tm, cw), lambda i: (i, OFF_ZC // cw)), vec, vec, ANY],
        out_specs=pl.BlockSpec((tm, cw), lambda i: (i, YCAT_CONF // cw)),
        input_output_aliases={4: 0},
        compiler_params=_params(("parallel",)),
    )(c1, proj, ln_w, ln_b, ycat)


def _conf_post_bwd(dycat, c1, proj, ln_w, ln_b, dproj, name):
    t = c1.shape[0]
    tm, cw = ROW_TILE, CONF_WIDTH
    nt = t // tm

    def body(dy_ref, c_ref, z_ref, w_ref, b_ref, _, dc_ref, dz_ref, dw_ref, db_ref, wacc, bacc):
        i = pl.program_id(0)

        @pl.when(i == 0)
        def _():
            wacc[...] = jnp.zeros_like(wacc)
            bacc[...] = jnp.zeros_like(bacc)

        cv = c_ref[...]
        xc = cv - jnp.mean(cv, axis=-1, keepdims=True)
        rstd = lax.rsqrt(jnp.mean(xc * xc, axis=-1, keepdims=True) + EPS)
        xh = xc * rstd
        c2 = xh * w_ref[...] + b_ref[...]
        zv = z_ref[...]
        dy = dy_ref[...]
        dz_ref[...] = (dy * _silu(c2) * _dsilu(zv)).astype(dz_ref.dtype)
        dc2 = dy * _silu(zv) * _dsilu(c2)
        bacc[...] += _rowsum8(dc2)
        wacc[...] += _rowsum8(dc2 * xh)
        dxh = dc2 * w_ref[...]
        dc_ref[...] = rstd * (dxh - jnp.mean(dxh, axis=-1, keepdims=True)
                              - xh * jnp.mean(dxh * xh, axis=-1, keepdims=True))

        @pl.when(i == nt - 1)
        def _():
            dw_ref[...] = jnp.sum(wacc[...], axis=0, keepdims=True)
            db_ref[...] = jnp.sum(bacc[...], axis=0, keepdims=True)

    row = pl.BlockSpec((tm, cw), lambda i: (i, 0))
    vec = pl.BlockSpec((1, cw), lambda i: (0, 0))
    return pl.pallas_call(
        body, name=name,
        out_shape=(jax.ShapeDtypeStruct((t, cw), F32), jax.ShapeDtypeStruct(dproj.shape, dproj.dtype),
                   jax.ShapeDtypeStruct((1, cw), F32), jax.ShapeDtypeStruct((1, cw), F32)),
        grid=(nt,),
        in_specs=[pl.BlockSpec((tm, cw), lambda i: (i, YCAT_CONF // cw)), row,
                  pl.BlockSpec((tm, cw), lambda i: (i, OFF_ZC // cw)), vec, vec, ANY],
        out_specs=(row, pl.BlockSpec((tm, cw), lambda i: (i, OFF_ZC // cw)), vec, vec),
        input_output_aliases={5: 1},
        scratch_shapes=[pltpu.VMEM((SUBLANES, cw), F32), pltpu.VMEM((SUBLANES, cw), F32)],
        compiler_params=_params(("arbitrary",)),
    )(dycat, c1, proj, ln_w, ln_b, dproj)


CONV_TILE = 512
CONV_COLS = 512
CONV_SUB_ROWS = 128
CONV_SUB_COLS = LANES


def _conv_halo(k):
    return SUBLANES if k - 1 <= SUBLANES else 32


def _conv_subtiles(tm, cw):
    return [(r0, c0) for r0 in range(0, tm, CONV_SUB_ROWS) for c0 in range(0, cw, CONV_SUB_COLS)]


def _conv_use_shifted(k):
    return k > SUBLANES


def _conv_shift_scratch(k, rows, cw):
    return [pltpu.VMEM((SUBLANES - 1, rows - SUBLANES, cw), F32)] if _conv_use_shifted(k) else []


def _conv_fill_shifted(ext, sh):
    n = sh.shape[1]
    for b in range(1, SUBLANES):
        sh[b - 1] = ext[b:b + n, :]


def _conv_rows(ext, sh, start, rows, cs):
    b = start % SUBLANES
    if b == 0 or not sh:
        return ext[start:start + rows, cs]
    return sh[0][b - 1, start - b:start - b + rows, cs]


def _conv_fwd(src, col0, width, w, bias, k, seq, name):
    t = src.shape[0]
    tm, cw, halo = CONV_TILE, CONV_COLS, _conv_halo(k)
    sr, sc = CONV_SUB_ROWS, CONV_SUB_COLS
    p = k - 1
    cb0 = col0 // cw
    kp = w.shape[0]

    shifted = _conv_use_shifted(k)

    def body(x_ref, h_ref, w_ref, b_ref, o_ref, ext, *sh):
        i = pl.program_id(0)
        seq_start = (i * tm) % seq == 0
        ext[halo:, :] = x_ref[...]
        ext[:halo, :] = jnp.where(seq_start, 0.0, h_ref[...])
        if shifted:
            _conv_fill_shifted(ext, sh[0])
        for r0, c0 in _conv_subtiles(tm, cw):
            cs = slice(c0, c0 + sc)
            acc = jnp.zeros((sr, sc), F32) + b_ref[:, cs]
            for j in range(k):
                acc = acc + w_ref[j:j + 1, cs] * _conv_rows(ext, sh, r0 + halo - p + j, sr, cs)
            o_ref[r0:r0 + sr, cs] = acc

    return pl.pallas_call(
        body, name=name,
        out_shape=jax.ShapeDtypeStruct((t, width), F32),
        grid=(t // tm, width // cw),
        in_specs=[pl.BlockSpec((tm, cw), lambda i, j: (i, cb0 + j)),
                  pl.BlockSpec((halo, cw), lambda i, j: (jnp.maximum(i * (tm // halo) - 1, 0), cb0 + j)),
                  pl.BlockSpec((kp, cw), lambda i, j: (0, j)),
                  pl.BlockSpec((1, cw), lambda i, j: (0, j))],
        out_specs=pl.BlockSpec((tm, cw), lambda i, j: (i, j)),
        scratch_shapes=[pltpu.VMEM((halo + tm, cw), F32)] + _conv_shift_scratch(k, halo + tm, cw),
        compiler_params=_params(("parallel", "parallel")),
    )(src, src, w, bias)


def _conv_bwd(dy, src, col0, width, w, k, seq, name, into=None):
    t = src.shape[0]
    tm, cw, halo = CONV_TILE, CONV_COLS, _conv_halo(k)
    sr, sc = CONV_SUB_ROWS, CONV_SUB_COLS
    p = k - 1
    cb0 = col0 // cw
    kp = w.shape[0]
    nt = t // tm
    last_halo = t // halo - 1

    shifted = _conv_use_shifted(k)

    def body(dy_ref, dn_ref, x_ref, xp_ref, w_ref, *rest):
        if into is not None:
            rest = rest[1:]
        dx_ref, dw_ref, db_ref, dyext, xext, wacc, bacc = rest[:7]
        sh = rest[7:]
        i = pl.program_id(1)
        dysh, xsh = (sh[:1], sh[1:]) if shifted else ((), ())

        @pl.when(i == 0)
        def _():
            wacc[...] = jnp.zeros_like(wacc)
            bacc[...] = jnp.zeros_like(bacc)

        seq_start = (i * tm) % seq == 0
        seq_end = ((i + 1) * tm) % seq == 0
        dyext[:tm, :] = dy_ref[...]
        dyext[tm:, :] = jnp.where(seq_end, 0.0, dn_ref[...])
        xext[halo:, :] = x_ref[...]
        xext[:halo, :] = jnp.where(seq_start, 0.0, xp_ref[...])
        if shifted:
            _conv_fill_shifted(dyext, dysh[0])
            _conv_fill_shifted(xext, xsh[0])
        for r0, c0 in _conv_subtiles(tm, cw):
            cs = slice(c0, c0 + sc)
            dyv = dy_ref[r0:r0 + sr, cs]
            acc = jnp.zeros((sr, sc), F32)
            for j in range(k):
                acc = acc + w_ref[j:j + 1, cs] * _conv_rows(dyext, dysh, r0 + p - j, sr, cs)
                wacc[j, :, cs] += _rowsum8(dyv * _conv_rows(xext, xsh, r0 + halo - p + j, sr, cs))
            dx_ref[r0:r0 + sr, cs] = acc.astype(dx_ref.dtype)
            bacc[:, cs] += _rowsum8(dyv)

        @pl.when(i == nt - 1)
        def _():
            dw_ref[...] = jnp.zeros_like(dw_ref)
            for j in range(k):
                dw_ref[j:j + 1, :] = jnp.sum(wacc[j], axis=0, keepdims=True)
            db_ref[...] = jnp.sum(bacc[...], axis=0, keepdims=True)

    if into is None:
        dx_shape = jax.ShapeDtypeStruct((t, width), F32)
        dx_spec = pl.BlockSpec((tm, cw), lambda j, i: (i, j))
        extra_specs, extra_args, aliases = [], [], {}
    else:
        dx_shape = jax.ShapeDtypeStruct(into.shape, into.dtype)
        dx_spec = pl.BlockSpec((tm, cw), lambda j, i: (i, cb0 + j))
        extra_specs, extra_args, aliases = [ANY], [into], {5: 0}
    return pl.pallas_call(
        body, name=name,
        out_shape=(dx_shape, jax.ShapeDtypeStruct((kp, width), F32), jax.ShapeDtypeStruct((1, width), F32)),
        grid=(width // cw, nt),
        in_specs=[pl.BlockSpec((tm, cw), lambda j, i: (i, j)),
                  pl.BlockSpec((halo, cw), lambda j, i: (jnp.minimum((i + 1) * (tm // halo), last_halo), j)),
                  pl.BlockSpec((tm, cw), lambda j, i: (i, cb0 + j)),
                  pl.BlockSpec((halo, cw), lambda j, i: (jnp.maximum(i * (tm // halo) - 1, 0), cb0 + j)),
                  pl.BlockSpec((kp, cw), lambda j, i: (0, j))] + extra_specs,
        out_specs=(dx_spec,
                   pl.BlockSpec((kp, cw), lambda j, i: (0, j)),
                   pl.BlockSpec((1, cw), lambda j, i: (0, j))),
        input_output_aliases=aliases,
        scratch_shapes=[pltpu.VMEM((tm + halo, cw), F32), pltpu.VMEM((halo + tm, cw), F32),
                        pltpu.VMEM((kp, SUBLANES, cw), F32), pltpu.VMEM((SUBLANES, cw), F32)]
        + 2 * _conv_shift_scratch(k, halo + tm, cw),
        compiler_params=_params(("parallel", "arbitrary")),
    )(dy, dy, src, src, w, *extra_args)


def _half_mask(half):
    lane = _iota((1, LANES), 1)
    return ((lane >= half * ATTN_HEAD_DIM) & (lane < (half + 1) * ATTN_HEAD_DIM)).astype(F32)


def _stack_heads(xp, g):
    m = _half_mask(g)
    swapped = pltpu.roll(xp, ATTN_HEAD_DIM, axis=1)
    return jnp.concatenate([xp * m, swapped * m] if g == 0 else [swapped * m, xp * m], axis=0)


def _unstack_heads(both, g):
    w = both.shape[0] // 2
    top, bot = both[:w], both[w:]
    lo, hi = _half_mask(0), _half_mask(1)
    if g == 0:
        return top * lo + pltpu.roll(bot, ATTN_HEAD_DIM, axis=1) * hi
    return pltpu.roll(top, ATTN_HEAD_DIM, axis=1) * lo + bot * hi


def _band_mask(first_block):
    w = WINDOW
    qi = _iota((w, 2 * w), 0)
    kj = _iota((w, 2 * w), 1) - w
    rel = qi - kj
    return (rel >= 0) & (rel < w) & (jnp.logical_not(first_block) | (kj >= 0))


def _lane_pick(x, h):
    return jnp.sum(jnp.where(_iota(x.shape, 1) == h, x, 0.0), axis=1, keepdims=True)


def _attn_specs(nb, rev):
    w = WINDOW

    def blk(i):
        return nb - 1 - i if rev else i

    def row(b, i):
        return b * nb + blk(i)

    def prow(b, i):
        return b * nb + jnp.maximum(blk(i) - 1, 0)

    q = pl.BlockSpec((w, 512), lambda b, i: (row(b, i), OFF_Q // 512))
    kc = pl.BlockSpec((w, 128), lambda b, i: (row(b, i), OFF_K // 128))
    kp = pl.BlockSpec((w, 128), lambda b, i: (prow(b, i), OFF_K // 128))
    vc = pl.BlockSpec((w, 128), lambda b, i: (row(b, i), OFF_V // 128))
    vp = pl.BlockSpec((w, 128), lambda b, i: (prow(b, i), OFF_V // 128))
    z = pl.BlockSpec((w, 512), lambda b, i: (row(b, i), OFF_ZA // 512))
    return q, kc, kp, vc, vp, z, row


def _attn_fwd(proj, sinks, ycat, nbatch, name):
    t = proj.shape[0]
    w = WINDOW
    nb = t // nbatch // w
    scale = ATTN_HEAD_DIM ** -0.5
    q_s, kc_s, kp_s, vc_s, vp_s, z_s, row = _attn_specs(nb, False)

    def body(q_ref, kc_ref, kp_ref, vc_ref, vp_ref, z_ref, sk_ref, _, y_ref, o_ref, lse_ref):
        first = pl.program_id(1) == 0
        mask = _band_mask(first)
        kk = jnp.concatenate([kp_ref[...], kc_ref[...]], axis=0).astype(MXU_DTYPE)
        vv = jnp.concatenate([vp_ref[...], vc_ref[...]], axis=0).astype(MXU_DTYPE)
        sk = sk_ref[...]
        lane = _iota((w, LANES), 1)
        mask2 = jnp.concatenate([mask, mask], axis=0)
        scores = [_dot(_stack_heads(q_ref[:, j * LANES:(j + 1) * LANES], j // 2), kk, NT) for j in range(4)]
        lse_all = jnp.zeros((w, LANES), F32)
        for j in range(4):
            s = jnp.where(mask2, scores[j] * scale, -1e30)
            skc = jnp.concatenate([jnp.broadcast_to(_lane_pick(sk, 2 * j), (w, 1)),
                                   jnp.broadcast_to(_lane_pick(sk, 2 * j + 1), (w, 1))], axis=0)
            m = jnp.maximum(jnp.max(s, axis=1, keepdims=True), skc)
            den = jnp.sum(jnp.exp(s - m), axis=1, keepdims=True) + jnp.exp(skc - m)
            lse = m + jnp.log(den)
            lse_all = jnp.where(lane == 2 * j, lse[:w], lse_all)
            lse_all = jnp.where(lane == 2 * j + 1, lse[w:], lse_all)
            op = _unstack_heads(_dot(jnp.exp(s - lse), vv), j // 2)
            cols = slice(j * LANES, (j + 1) * LANES)
            o_ref[:, cols] = op
            y_ref[:, cols] = (op * _silu(z_ref[:, cols])).astype(y_ref.dtype)
        lse_ref[...] = lse_all

    return pl.pallas_call(
        body, name=name,
        out_shape=(jax.ShapeDtypeStruct(ycat.shape, ycat.dtype), jax.ShapeDtypeStruct((t, 512), F32),
                   jax.ShapeDtypeStruct((t, LANES), F32)),
        grid=(nbatch, nb),
        in_specs=[q_s, kc_s, kp_s, vc_s, vp_s, z_s, pl.BlockSpec((1, LANES), lambda b, i: (0, 0)), ANY],
        out_specs=(pl.BlockSpec((w, 512), lambda b, i: (row(b, i), YCAT_ATTN // 512)),
                   pl.BlockSpec((w, 512), lambda b, i: (row(b, i), 0)),
                   pl.BlockSpec((w, LANES), lambda b, i: (row(b, i), 0))),
        input_output_aliases={7: 0},
        compiler_params=_params(("parallel", "parallel")),
    )(proj, proj, proj, proj, proj, proj, sinks, ycat)


def _attn_bwd(dycat, proj, o, lse, sinks, ddt, dproj, nbatch, name):
    t = proj.shape[0]
    w = WINDOW
    nb = t // nbatch // w
    scale = ATTN_HEAD_DIM ** -0.5
    q_s, kc_s, kp_s, vc_s, vp_s, z_s, row = _attn_specs(nb, True)

    def body(dy_ref, q_ref, kc_ref, kp_ref, vc_ref, vp_ref, z_ref, o_ref, lse_ref, sk_ref, ddt_ref, _,
             grp_ref, dsk_ref, kcarry, vcarry, sacc):
        b, i = pl.program_id(0), pl.program_id(1)

        @pl.when((b == 0) & (i == 0))
        def _():
            sacc[...] = jnp.zeros_like(sacc)

        @pl.when(i == 0)
        def _():
            kcarry[...] = jnp.zeros_like(kcarry)
            vcarry[...] = jnp.zeros_like(vcarry)

        first = i == nb - 1
        mask = _band_mask(first)
        kk = jnp.concatenate([kp_ref[...], kc_ref[...]], axis=0).astype(MXU_DTYPE)
        vv = jnp.concatenate([vp_ref[...], vc_ref[...]], axis=0).astype(MXU_DTYPE)
        sk = sk_ref[...]
        lse_all = lse_ref[...]
        lane1 = _iota((1, LANES), 1)
        mask2 = jnp.concatenate([mask, mask], axis=0)
        qs, dos, deltas, lses, scores, dps = [], [], [], [], [], []
        for j in range(4):
            cols = slice(j * LANES, (j + 1) * LANES)
            qp, zp, ov, dy = q_ref[:, cols], z_ref[:, cols], o_ref[:, cols], dy_ref[:, cols]
            grp_ref[:, OFF_ZA + j * LANES:OFF_ZA + (j + 1) * LANES] = (dy * ov * _dsilu(zp)).astype(grp_ref.dtype)
            do = dy * _silu(zp)
            q2 = _stack_heads(qp, j // 2).astype(MXU_DTYPE)
            do2 = _stack_heads(do, j // 2)
            qs.append(q2)
            dos.append(do2.astype(MXU_DTYPE))
            deltas.append(jnp.sum(do2 * _stack_heads(ov, j // 2), axis=1, keepdims=True))
            lses.append(jnp.concatenate([_lane_pick(lse_all, 2 * j), _lane_pick(lse_all, 2 * j + 1)], axis=0))
            scores.append(_dot(q2, kk, NT))
            dps.append(_dot(do2, vv, NT))
        prs, dss = [], []
        dsk = jnp.zeros((1, LANES), F32)
        for j in range(4):
            pr = jnp.exp(jnp.where(mask2, scores[j] * scale, -1e30) - lses[j])
            prs.append(pr.astype(MXU_DTYPE))
            dss.append((pr * (dps[j] - deltas[j])).astype(MXU_DTYPE))
            skc = jnp.concatenate([jnp.broadcast_to(_lane_pick(sk, 2 * j), (w, 1)),
                                   jnp.broadcast_to(_lane_pick(sk, 2 * j + 1), (w, 1))], axis=0)
            sink_term = jnp.exp(skc - lses[j]) * deltas[j]
            dsk = dsk - jnp.where(lane1 == 2 * j, jnp.sum(sink_term[:w]), 0.0)
            dsk = dsk - jnp.where(lane1 == 2 * j + 1, jnp.sum(sink_term[w:]), 0.0)
        dkk = jnp.zeros((2 * w, LANES), F32)
        dvv = jnp.zeros((2 * w, LANES), F32)
        for j in range(4):
            dq = _unstack_heads(_dot(dss[j], kk) * scale, j // 2)
            grp_ref[:, OFF_Q + j * LANES:OFF_Q + (j + 1) * LANES] = dq.astype(grp_ref.dtype)
            dkk = dkk + _dot(dss[j], qs[j], TN) * scale
            dvv = dvv + _dot(prs[j], dos[j], TN)
        grp_ref[:, OFF_K:OFF_K + LANES] = (dkk[w:, :] + kcarry[...]).astype(grp_ref.dtype)
        grp_ref[:, OFF_V:OFF_V + LANES] = (dvv[w:, :] + vcarry[...]).astype(grp_ref.dtype)
        grp_ref[:, OFF_DT:OFF_DT + LANES] = ddt_ref[...].astype(grp_ref.dtype)
        grp_ref[:, OFF_DT + LANES:] = jnp.zeros((w, ATTN_GROUP - OFF_DT - LANES), grp_ref.dtype)
        kcarry[...] = dkk[:w, :]
        vcarry[...] = dvv[:w, :]
        sacc[...] += dsk

        @pl.when((b == nbatch - 1) & (i == nb - 1))
        def _():
            dsk_ref[...] = sacc[...]

    return pl.pallas_call(
        body, name=name,
        out_shape=(jax.ShapeDtypeStruct(dproj.shape, dproj.dtype), jax.ShapeDtypeStruct((1, LANES), F32)),
        grid=(nbatch, nb),
        in_specs=[pl.BlockSpec((w, 512), lambda b, i: (row(b, i), YCAT_ATTN // 512)),
                  q_s, kc_s, kp_s, vc_s, vp_s, z_s,
                  pl.BlockSpec((w, 512), lambda b, i: (row(b, i), 0)),
                  pl.BlockSpec((w, LANES), lambda b, i: (row(b, i), 0)),
                  pl.BlockSpec((1, LANES), lambda b, i: (0, 0)),
                  pl.BlockSpec((w, LANES), lambda b, i: (row(b, i), 0)), ANY],
        out_specs=(pl.BlockSpec((w, ATTN_GROUP), lambda b, i: (row(b, i), 0)),
                   pl.BlockSpec((1, LANES), lambda b, i: (0, 0))),
        input_output_aliases={11: 0},
        scratch_shapes=[pltpu.VMEM((w, LANES), F32), pltpu.VMEM((w, LANES), F32),
                        pltpu.VMEM((1, LANES), F32)],
        compiler_params=_params(("arbitrary", "arbitrary")),
    )(dycat, proj, proj, proj, proj, proj, proj, o, lse, sinks, ddt, dproj)


SSD_WIDTH = SSD_HEADS * SSD_HEAD_DIM
GROUP_ROWS = SSD_WIDTH // 2


def _expand_mat():
    r, c = _iota((LANES, SSD_WIDTH), 0), _iota((LANES, SSD_WIDTH), 1)
    return (r == lax.shift_right_logical(c, 6)).astype(BF16)


def _expand_mat_t():
    r, c = _iota((SSD_WIDTH, LANES), 0), _iota((SSD_WIDTH, LANES), 1)
    return (c == lax.shift_right_logical(r, 6)).astype(BF16)


def _ssd_common(u_ref, dt_ref, dtb_ref, a_ref):
    q = CHUNK
    act = _silu(u_ref[...])
    xs = act[:, :SSD_WIDTH]
    bm = act[:, SSD_WIDTH:SSD_WIDTH + 256]
    cm = act[:, SSD_WIDTH + 256:]
    dtp = _softplus(dt_ref[...] + dtb_ref[...])
    a = dtp * a_ref[...]
    tril = (_iota((q, q), 0) >= _iota((q, q), 1)).astype(BF16)
    acs = _xdot_r(tril, a)
    acs_t = acs.T
    e = _expand_mat()
    dt_x = _xdot(dtp, e)
    ea = jnp.exp(_xdot(acs, e))
    a_end = jnp.sum(jnp.where(_iota(acs.shape, 0) == q - 1, acs, 0.0), axis=0, keepdims=True)
    dec = jnp.exp(_xdot(a_end - acs, e))
    a_end_col = jnp.broadcast_to(_lane_pick(acs_t, q - 1), (LANES, LANES))
    s_scale = jnp.exp(_xdot_r(_expand_mat_t(), a_end_col))
    return act, xs, bm, cm, dtp, acs, acs_t, dt_x, ea, dec, s_scale, tril


def _decay_mat(acs, acs_t, h):
    q = CHUNK
    col = _lane_pick(acs, h)
    rowv = jnp.sum(jnp.where(_iota(acs_t.shape, 0) == h, acs_t, 0.0), axis=0, keepdims=True)
    causal = _iota((q, q), 0) >= _iota((q, q), 1)
    return jnp.exp(jnp.where(causal, col - rowv, -1e30))


GN_WIDTH = 512


def _ssd_fwd(u, proj, dtb, a_neg, d_x, norm_w, ycat, nbatch, name):
    t = u.shape[0]
    q = CHUNK
    nc = t // nbatch // q

    def body(u_ref, dt_ref, z_ref, dtb_ref, a_ref, dx_ref, nw_ref, _, y_ref, st_ref, yn_ref, state):
        c = pl.program_id(1)

        @pl.when(c == 0)
        def _():
            state[...] = jnp.zeros_like(state)

        st_ref[...] = state[...]
        act, xs, bm, cm, dtp, acs, acs_t, dt_x, ea, dec, s_scale, _ = _ssd_common(u_ref, dt_ref, dtb_ref, a_ref)
        xdt = xs * dt_x
        xdec = xdt * dec
        lo, hi = _half_mask(0), _half_mask(1)
        for g in range(2):
            bg = bm[:, g * LANES:(g + 1) * LANES]
            cg = cm[:, g * LANES:(g + 1) * LANES]
            rows = slice(g * GROUP_ROWS, (g + 1) * GROUP_ROWS)
            sg = state[rows, :]
            cb = _dot(cg, bg, NT)
            yoff = _dot(cg, sg, NT)
            for j in range(4):
                pj = g * 4 + j
                cols = slice(pj * LANES, (pj + 1) * LANES)
                xp = xdt[:, cols]
                m0 = cb * _decay_mat(acs, acs_t, 2 * pj)
                m1 = cb * _decay_mat(acs, acs_t, 2 * pj + 1)
                yp = _dot(m0, xp * lo) + _dot(m1, xp * hi)
                yp = yp + yoff[:, j * LANES:(j + 1) * LANES] * ea[:, cols]
                y_ref[:, cols] = yp + dx_ref[:, cols] * xs[:, cols]
            state[rows, :] = s_scale[rows, :] * sg + _dot(xdec[:, rows], bg, TN)
        for g in range(SSD_WIDTH // GN_WIDTH):
            cols = slice(g * GN_WIDTH, (g + 1) * GN_WIDTH)
            gg = y_ref[:, cols] * _silu(z_ref[:, cols])
            rstd = lax.rsqrt(jnp.mean(gg * gg, axis=-1, keepdims=True) + EPS)
            yn_ref[:, cols] = (gg * rstd * nw_ref[:, cols]).astype(yn_ref.dtype)

    vec = pl.BlockSpec((1, LANES), lambda b, c: (0, 0))
    wide = pl.BlockSpec((q, SSD_WIDTH), lambda b, c: (b * nc + c, 0))
    wvec = pl.BlockSpec((1, SSD_WIDTH), lambda b, c: (0, 0))
    return pl.pallas_call(
        body, name=name,
        out_shape=(jax.ShapeDtypeStruct((t, SSD_WIDTH), F32),
                   jax.ShapeDtypeStruct((nbatch * nc * SSD_WIDTH, SSD_STATE), F32),
                   jax.ShapeDtypeStruct(ycat.shape, ycat.dtype)),
        grid=(nbatch, nc),
        in_specs=[pl.BlockSpec((q, SSD_CONV_DIM), lambda b, c: (b * nc + c, 0)),
                  pl.BlockSpec((q, LANES), lambda b, c: (b * nc + c, OFF_DT // LANES)),
                  pl.BlockSpec((q, SSD_WIDTH), lambda b, c: (b * nc + c, OFF_ZS // SSD_WIDTH)),
                  vec, vec, wvec, wvec, ANY],
        out_specs=(wide, pl.BlockSpec((SSD_WIDTH, SSD_STATE), lambda b, c: (b * nc + c, 0)), wide),
        input_output_aliases={7: 2},
        scratch_shapes=[pltpu.VMEM((SSD_WIDTH, SSD_STATE), F32)],
        compiler_params=_params(("parallel", "arbitrary")),
    )(u, proj, proj, dtb, a_neg, d_x, norm_w, ycat)


def _ssd_bwd(dycat, u, proj, y, states, dtb, a_neg, d_x, norm_w, dproj, nbatch, name):
    t = u.shape[0]
    q = CHUNK
    nc = t // nbatch // q

    def body(do_ref, u_ref, dt_ref, z_ref, y_ref, st_ref, dtb_ref, a_ref, dx_ref, nw_ref, _,
             du_ref, dz_ref, ddt_ref, dal_ref, dd_ref, dtbg_ref, dnw_ref, dstate, acc_a, acc_d, acc_b, acc_w):
        b, c = pl.program_id(0), pl.program_id(1)

        @pl.when((b == 0) & (c == 0))
        def _():
            acc_a[...] = jnp.zeros_like(acc_a)
            acc_d[...] = jnp.zeros_like(acc_d)
            acc_b[...] = jnp.zeros_like(acc_b)
            acc_w[...] = jnp.zeros_like(acc_w)

        @pl.when(c == 0)
        def _():
            dstate[...] = jnp.zeros_like(dstate)

        dy_parts = []
        for g in range(SSD_WIDTH // GN_WIDTH):
            cols = slice(g * GN_WIDTH, (g + 1) * GN_WIDTH)
            yv, zv, dov = y_ref[:, cols], z_ref[:, cols], do_ref[:, cols]
            sz = _silu(zv)
            gg = yv * sz
            rstd = lax.rsqrt(jnp.mean(gg * gg, axis=-1, keepdims=True) + EPS)
            gh = gg * rstd
            acc_w[:, cols] += _rowsum8(dov * gh)
            dgn = dov * nw_ref[:, cols]
            dg = rstd * (dgn - gh * jnp.mean(dgn * gh, axis=-1, keepdims=True))
            dy_parts.append(dg * sz)
            dz_ref[:, cols] = (dg * yv * _dsilu(zv)).astype(dz_ref.dtype)

        act, xs, bm, cm, dtp, acs, acs_t, dt_x, ea, dec, s_scale, tril = _ssd_common(
            u_ref, dt_ref, dtb_ref, a_ref)
        xdt = xs * dt_x
        xdec = xdt * dec
        dyv = jnp.concatenate(dy_parts, axis=1)
        dye = dyv * ea
        lo, hi = _half_mask(0), _half_mask(1)
        et = _expand_mat_t()
        dxdt_parts, db_parts, dc_parts, dxst_parts, yoff_parts = [], [], [], [], []
        end_sum = jnp.zeros((LANES, LANES), F32)
        dal_diag = jnp.zeros((q, LANES), F32)
        lane_q = _iota((q, LANES), 1)
        for g in range(2):
            bg = bm[:, g * LANES:(g + 1) * LANES]
            cg = cm[:, g * LANES:(g + 1) * LANES]
            rows = slice(g * GROUP_ROWS, (g + 1) * GROUP_ROWS)
            sg = st_ref[rows, :]
            dsg = dstate[rows, :]
            cb = _dot(cg, bg, NT)
            yoff_parts.append(_dot(cg, sg, NT))
            dcb = jnp.zeros((q, q), F32)
            parts = []
            for j in range(4):
                pj = g * 4 + j
                cols = slice(pj * LANES, (pj + 1) * LANES)
                xp = xdt[:, cols]
                dy0, dy1 = dyv[:, cols] * lo, dyv[:, cols] * hi
                l0 = _decay_mat(acs, acs_t, 2 * pj)
                l1 = _decay_mat(acs, acs_t, 2 * pj + 1)
                g0, g1 = _dot(dy0, xp, NT), _dot(dy1, xp, NT)
                m0, m1 = cb * l0, cb * l1
                dcb = dcb + g0 * l0 + g1 * l1
                parts.append(_dot(m0, dy0, TN) + _dot(m1, dy1, TN))
                for hh, wmat in enumerate((g0 * m0, g1 * m1)):
                    sel = (lane_q == 2 * pj + hh).astype(F32)
                    dal_diag = dal_diag + _dot(wmat, sel) - _dot(wmat, sel, TN)
            dxst = _dot(bg, dsg, NT) * dec[:, rows]
            dxst_parts.append(dxst)
            dxdt_parts.append(jnp.concatenate(parts, axis=1) + dxst)
            dc_parts.append(_dot(dcb, bg) + _dot(dye[:, rows], sg))
            db_parts.append(_dot(dcb, cg, TN) + _dot(xdec[:, rows], dsg))
            s_next = s_scale[rows, :] * sg + _dot(xdec[:, rows], bg, TN)
            end_sum = end_sum + _xdot(dsg * s_next, et[rows, :], TN, passes=2)
            dstate[rows, :] = _dot(dye[:, rows], cg, TN) + s_scale[rows, :] * dsg
        dxdt = jnp.concatenate(dxdt_parts, axis=1)
        dxv = dx_ref[...]
        yoff = jnp.concatenate(yoff_parts, axis=1) * ea
        dalpha = dal_diag + _xdot(dyv * yoff - xdt * jnp.concatenate(dxst_parts, axis=1), et)
        end_row = jnp.sum(end_sum, axis=0, keepdims=True)
        dalpha = dalpha + jnp.where(_iota((q, LANES), 0) == q - 1, end_row, 0.0)
        da = _xdot_r(tril, dalpha, TN)
        ddtp = da * a_ref[...] + _xdot(dxdt * xs, et)
        acc_a[...] += _rowsum8(da * dtp)
        acc_d[...] += _rowsum8(_xdot(dyv * xs, et))
        ddt_raw = ddtp * _sigmoid(dt_ref[...] + dtb_ref[...])
        acc_b[...] += _rowsum8(ddt_raw)
        ddt_ref[...] = ddt_raw
        dxs = dxdt * dt_x + dxv * dyv
        dact = jnp.concatenate([dxs] + db_parts + dc_parts, axis=1)
        du_ref[...] = dact * _dsilu(u_ref[...])

        @pl.when((b == nbatch - 1) & (c == nc - 1))
        def _():
            dal_ref[...] = jnp.sum(acc_a[...], axis=0, keepdims=True) * a_ref[...]
            dd_ref[...] = jnp.sum(acc_d[...], axis=0, keepdims=True)
            dtbg_ref[...] = jnp.sum(acc_b[...], axis=0, keepdims=True)
            dnw_ref[...] = jnp.sum(acc_w[...], axis=0, keepdims=True)

    def rowblk(b, c):
        return b * nc + (nc - 1 - c)

    vec = pl.BlockSpec((1, LANES), lambda b, c: (0, 0))
    wvec = pl.BlockSpec((1, SSD_WIDTH), lambda b, c: (0, 0))
    wide = pl.BlockSpec((q, SSD_WIDTH), lambda b, c: (rowblk(b, c), 0))
    zblk = pl.BlockSpec((q, SSD_WIDTH), lambda b, c: (rowblk(b, c), OFF_ZS // SSD_WIDTH))
    return pl.pallas_call(
        body, name=name,
        out_shape=(jax.ShapeDtypeStruct((t, SSD_CONV_DIM), F32), jax.ShapeDtypeStruct(dproj.shape, dproj.dtype),
                   jax.ShapeDtypeStruct((t, LANES), F32),
                   jax.ShapeDtypeStruct((1, LANES), F32), jax.ShapeDtypeStruct((1, LANES), F32),
                   jax.ShapeDtypeStruct((1, LANES), F32), jax.ShapeDtypeStruct((1, SSD_WIDTH), F32)),
        grid=(nbatch, nc),
        in_specs=[wide,
                  pl.BlockSpec((q, SSD_CONV_DIM), lambda b, c: (rowblk(b, c), 0)),
                  pl.BlockSpec((q, LANES), lambda b, c: (rowblk(b, c), OFF_DT // LANES)),
                  zblk, wide,
                  pl.BlockSpec((SSD_WIDTH, SSD_STATE), lambda b, c: (rowblk(b, c), 0)),
                  vec, vec, wvec, wvec, ANY],
        out_specs=(pl.BlockSpec((q, SSD_CONV_DIM), lambda b, c: (rowblk(b, c), 0)),
                   zblk,
                   pl.BlockSpec((q, LANES), lambda b, c: (rowblk(b, c), 0)),
                   vec, vec, vec, wvec),
        input_output_aliases={10: 1},
        scratch_shapes=[pltpu.VMEM((SSD_WIDTH, SSD_STATE), F32), pltpu.VMEM((SUBLANES, LANES), F32),
                        pltpu.VMEM((SUBLANES, LANES), F32), pltpu.VMEM((SUBLANES, LANES), F32),
                        pltpu.VMEM((SUBLANES, SSD_WIDTH), F32)],
        compiler_params=_params(("arbitrary", "arbitrary")),
    )(dycat, u, proj, proj, y, states, dtb, a_neg, d_x, norm_w, dproj)


def _pad_rows(w, rows):
    return jnp.concatenate([w, jnp.zeros((rows - w.shape[0], w.shape[1]), w.dtype)], axis=0)


def _pad_lanes(v):
    return jnp.concatenate([v, jnp.zeros((LANES - v.shape[0],), v.dtype)]).reshape(1, LANES)


def _padded_from_chips(pieces):
    cols = pieces[0].shape[-1]
    lead = pieces[0].shape[:-1]
    parts, pos = [], 0
    for lo, hi, start in sorted(SECTIONS, key=lambda s: s[2]):
        if start > pos:
            parts.append(jnp.zeros(lead + (start - pos,), pieces[0].dtype))
        pos = start + hi - lo
        while lo < hi:
            p = lo // cols
            end = min(hi, (p + 1) * cols)
            parts.append(pieces[p][..., lo - p * cols:end - p * cols])
            lo = end
    if pos < NP:
        parts.append(jnp.zeros(lead + (NP - pos,), pieces[0].dtype))
    return jnp.concatenate(parts, axis=-1)


def _chip_part_from_padded(wp, p, cols):
    lo, hi = p * cols, (p + 1) * cols
    parts = []
    for rs, re, start in SECTIONS:
        a, b = max(lo, rs), min(hi, re)
        if a < b:
            parts.append(wp[..., start + a - rs:start + b - rs])
    return jnp.concatenate(parts, axis=-1)


def _layer_params(li, w_in_p, w_out, conv_w, dw_w, small):
    return dict(
        w_in_p=w_in_p, w_out=w_out,
        conv_w=_pad_rows(conv_w, SUBLANES), dw_w=_pad_rows(dw_w, 32),
        norm_w=small["norm_w"][li].reshape(1, -1),
        conv_b=small["ssd_conv_b"][li].reshape(1, -1),
        dtb=_pad_lanes(small["ssd_dt_bias"][li]),
        a_neg=_pad_lanes(-jnp.exp(small["ssd_a_log"][li])),
        d_x=jnp.repeat(small["ssd_d"][li], SSD_HEAD_DIM).reshape(1, -1),
        ssd_norm_w=small["ssd_norm_w"][li].reshape(1, -1),
        sinks=_pad_lanes(small["attn_sinks"][li]),
        dw_b=small["conf_dw_b"][li].reshape(1, -1),
        ln_w=small["conf_ln_w"][li].reshape(1, -1),
        ln_b=small["conf_ln_b"][li].reshape(1, -1),
    )


def _layer_fwd(x, p, nbatch, seq, tag):
    h, h_t = _rmsnorm_fwd(x, p["norm_w"], name=f"rmsnorm_fwd_{tag}")
    proj = _matmul(h, p["w_in_p"], "nn", F32, 1024, 512, 1024, name=f"proj_fwd_{tag}")
    u = _conv_fwd(proj, OFF_XBC, SSD_CONV_DIM, p["conv_w"], p["conv_b"], SSD_CONV, seq, name=f"ssd_conv_fwd_{tag}")
    ycat = lax.empty((x.shape[0], MIX_WIDTH), MXU_DTYPE)
    y, states, ycat = _ssd_fwd(u, proj, p["dtb"], p["a_neg"], p["d_x"], p["ssd_norm_w"], ycat, nbatch,
                               name=f"ssd_fwd_{tag}")
    ycat, o, lse = _attn_fwd(proj, p["sinks"], ycat, nbatch, name=f"attn_fwd_{tag}")
    c0 = _glu_fwd(proj, name=f"glu_fwd_{tag}")
    c1 = _conv_fwd(c0, 0, CONF_WIDTH, p["dw_w"], p["dw_b"], CONF_KERNEL, seq, name=f"conf_conv_fwd_{tag}")
    ycat = _conf_post_fwd(c1, proj, p["ln_w"], p["ln_b"], ycat, name=f"conf_post_fwd_{tag}")
    x_new = _matmul(ycat, p["w_out"], "nn", F32, 1024, 512, 2048, name=f"out_fwd_{tag}", residual=x)
    return x_new, dict(x=x, h_t=h_t, proj=proj, u=u, y=y, states=states, o=o, lse=lse, c0=c0, c1=c1, ycat=ycat)


def _layer_bwd(dx_out, p, s, nbatch, seq, tag):
    proj = s["proj"]
    dycat = _matmul(dx_out, p["w_out"], "nt", F32, 1024, 1024, 1024, name=f"out_bwd_dy_{tag}")
    dw_out = _matmul(s["ycat"], dx_out, "tn", F32, 1024, 1024, 1024, name=f"out_bwd_dw_{tag}")
    dproj = lax.empty(proj.shape, MXU_DTYPE)
    du, dproj, ddt, da_log, dd, ddtb, dssd_norm_w = _ssd_bwd(
        dycat, s["u"], proj, s["y"], s["states"], p["dtb"], p["a_neg"], p["d_x"], p["ssd_norm_w"], dproj,
        nbatch, name=f"ssd_bwd_{tag}")
    dproj, dconv_w, dconv_b = _conv_bwd(du, proj, OFF_XBC, SSD_CONV_DIM, p["conv_w"], SSD_CONV, seq,
                                        name=f"ssd_conv_bwd_{tag}", into=dproj)
    dproj, dsinks = _attn_bwd(dycat, proj, s["o"], s["lse"], p["sinks"], ddt, dproj, nbatch,
                              name=f"attn_bwd_{tag}")
    dc1, dproj, dln_w, dln_b = _conf_post_bwd(dycat, s["c1"], proj, p["ln_w"], p["ln_b"], dproj,
                                              name=f"conf_post_bwd_{tag}")
    dc0, ddw_w, ddw_b = _conv_bwd(dc1, s["c0"], 0, CONF_WIDTH, p["dw_w"], CONF_KERNEL, seq,
                                  name=f"conf_conv_bwd_{tag}")
    dproj = _glu_bwd(dc0, proj, dproj, name=f"glu_bwd_{tag}")
    dh = _matmul(dproj, p["w_in_p"], "nt", F32, 1024, 1024, 1408, name=f"proj_bwd_dh_{tag}")
    dw_in_p = _matmul(s["h_t"], dproj, "nn", F32, 1024, 512, 4096, name=f"proj_bwd_dw_{tag}")
    dx_in, dnorm_w = _rmsnorm_bwd(dh, s["x"], p["norm_w"], dx_out, name=f"rmsnorm_bwd_{tag}")
    grads = dict(
        norm_w=dnorm_w[0], w_in_p=dw_in_p, ssd_conv_w=dconv_w[:SSD_CONV], ssd_conv_b=dconv_b[0],
        ssd_dt_bias=ddtb[0, :SSD_HEADS], ssd_a_log=da_log[0, :SSD_HEADS], ssd_d=dd[0, :SSD_HEADS],
        ssd_norm_w=dssd_norm_w[0], attn_sinks=dsinks[0, :ATTN_Q_HEADS], conf_dw_w=ddw_w[:CONF_KERNEL],
        conf_dw_b=ddw_b[0], conf_ln_w=dln_w[0], conf_ln_b=dln_b[0], w_out=dw_out)
    return dx_in, grads


def _local_step(x, target, layer_params, final_norm_w):
    nbatch, seq, d = x.shape
    xt = x.reshape(nbatch * seq, d)
    saved = []
    for li, p in enumerate(layer_params):
        xt, s = _layer_fwd(xt, p, nbatch, seq, f"l{li}")
        saved.append(s)
    loss, dx, dfinal = _loss_head(xt, target.reshape(nbatch * seq, d), final_norm_w.reshape(1, d), name="loss_head")
    grads = [None] * len(layer_params)
    for li in reversed(range(len(layer_params))):
        dx, grads[li] = _layer_bwd(dx, layer_params[li], saved[li], nbatch, seq, f"l{li}")
    return loss[0, 0], dx.reshape(nbatch, seq, d), grads, dfinal[0]


MESH = pl.DeviceIdType.MESH
N_CHIPS = 4


def _mesh_pos():
    return lax.axis_index("x"), lax.axis_index("y"), lax.axis_index("c")


def _other_chips(x, y):
    return [(1 - x, y), (x, 1 - y), (1 - x, 1 - y)]


def _gather_weights(big, small, name):
    nbig, nsmall = len(big), len(small)
    n_ici = 3 * (nbig + nsmall)
    n_fwd = 3 * nbig

    def body(*refs):
        ins = refs[:nbig + nsmall]
        outs = refs[nbig + nsmall:2 * (nbig + nsmall)]
        send_sems, recv_sems = refs[2 * (nbig + nsmall):]
        x, y, c = _mesh_pos()
        me = 2 * x + y
        sibling = (x, y, 1 - c)
        chips = _other_chips(x, y)

        def ici(a, j, origin, dest):
            if a < nbig:
                src = ins[a].at[c] if origin is None else outs[a].at[origin, c]
                dst = outs[a].at[me if origin is None else origin, c]
            else:
                src = ins[a] if origin is None else outs[a].at[origin]
                dst = outs[a].at[me if origin is None else origin]
            k = a * 3 + j
            return pltpu.make_async_remote_copy(src_ref=src, dst_ref=dst, send_sem=send_sems.at[k],
                                                recv_sem=recv_sems.at[k], device_id=dest, device_id_type=MESH)

        def fwd(a, j, origin, half):
            k = n_ici + a * 3 + j
            ref = outs[a].at[origin, half]
            return pltpu.make_async_remote_copy(src_ref=ref, dst_ref=ref, send_sem=send_sems.at[k],
                                                recv_sem=recv_sems.at[k], device_id=sibling, device_id_type=MESH)

        sends = []
        for j, (px, py) in enumerate(chips):
            for a in range(nbig + nsmall):
                cp = ici(a, j, None, (px, py, c))
                cp.start()
                sends.append(cp)
        for j, (px, py) in enumerate(chips):
            origin = 2 * px + py
            for a in range(nbig):
                ici(a, j, origin, (px, py, c)).wait_recv()
                cp = fwd(a, j, origin, c)
                cp.start()
                sends.append(cp)
        for j, (px, py) in enumerate(chips):
            origin = 2 * px + py
            for a in range(nbig, nbig + nsmall):
                ici(a, j, origin, (px, py, c)).wait_recv()
            for a in range(nbig):
                fwd(a, j, origin, 1 - c).wait_recv()
        for cp in sends:
            cp.wait_send()

    out_shape = tuple(jax.ShapeDtypeStruct((N_CHIPS,) + a.shape, a.dtype) for a in list(big) + list(small))
    return pl.pallas_call(
        body, name=name, out_shape=out_shape,
        in_specs=[ANY] * (nbig + nsmall), out_specs=tuple([ANY] * (nbig + nsmall)),
        scratch_shapes=[pltpu.SemaphoreType.DMA((n_ici + n_fwd,)), pltpu.SemaphoreType.DMA((n_ici + n_fwd,))],
    )(*big, *small)


def _pair_swap_halves(arrs, name):
    n = len(arrs)

    def body(*refs):
        ins, outs = refs[:n], refs[n:2 * n]
        send_sems, recv_sems = refs[2 * n:]
        x, y, c = _mesh_pos()
        cps = [pltpu.make_async_remote_copy(src_ref=ins[a].at[1 - c], dst_ref=outs[a], send_sem=send_sems.at[a],
                                            recv_sem=recv_sems.at[a], device_id=(x, y, 1 - c), device_id_type=MESH)
               for a in range(n)]
        for cp in cps:
            cp.start()
        for cp in cps:
            cp.wait()

    return pl.pallas_call(
        body, name=name, out_shape=tuple(jax.ShapeDtypeStruct(a.shape[1:], a.dtype) for a in arrs),
        in_specs=[ANY] * n, out_specs=tuple([ANY] * n),
        scratch_shapes=[pltpu.SemaphoreType.DMA((n,)), pltpu.SemaphoreType.DMA((n,))],
    )(*arrs)


def _chip_scatter(arrs, name):
    n = len(arrs)

    def body(*refs):
        ins, outs = refs[:n], refs[n:2 * n]
        send_sems, recv_sems = refs[2 * n:]
        x, y, c = _mesh_pos()
        me = 2 * x + y
        cps = []
        for j, (px, py) in enumerate(_other_chips(x, y)):
            for a in range(n):
                cps.append(pltpu.make_async_remote_copy(
                    src_ref=ins[a].at[2 * px + py], dst_ref=outs[a].at[me], send_sem=send_sems.at[a * 3 + j],
                    recv_sem=recv_sems.at[a * 3 + j], device_id=(px, py, c), device_id_type=MESH))
        for cp in cps:
            cp.start()
        for cp in cps:
            cp.wait()

    return pl.pallas_call(
        body, name=name, out_shape=tuple(jax.ShapeDtypeStruct(a.shape, a.dtype) for a in arrs),
        in_specs=[ANY] * n, out_specs=tuple([ANY] * n),
        scratch_shapes=[pltpu.SemaphoreType.DMA((3 * n,)), pltpu.SemaphoreType.DMA((3 * n,))],
    )(*arrs)


def _pair_gather(arrs, name):
    n = len(arrs)

    def body(*refs):
        outs = refs[n:2 * n]
        send_sems, recv_sems = refs[2 * n:]
        x, y, c = _mesh_pos()
        cps = [pltpu.make_async_remote_copy(src_ref=outs[a].at[c], dst_ref=outs[a].at[c], send_sem=send_sems.at[a],
                                            recv_sem=recv_sems.at[a], device_id=(x, y, 1 - c), device_id_type=MESH)
               for a in range(n)]
        for cp in cps:
            cp.start()
        for cp in cps:
            cp.wait()

    return pl.pallas_call(
        body, name=name, out_shape=tuple(jax.ShapeDtypeStruct(a.shape, a.dtype) for a in arrs),
        in_specs=[ANY] * n, out_specs=tuple([ANY] * n),
        input_output_aliases={a: a for a in range(n)},
        scratch_shapes=[pltpu.SemaphoreType.DMA((n,)), pltpu.SemaphoreType.DMA((n,))],
    )(*arrs)


N_DEV = 8


def _allreduce_small(pack, name):
    r = pack.shape[0]

    def body(p_ref, o_ref, land, send_sems, recv_sems):
        x, y, c = _mesh_pos()
        me = 4 * x + 2 * y + c
        cps = []
        for k in range(1, N_DEV):
            peer = (x ^ (k >> 2), y ^ ((k >> 1) & 1), c ^ (k & 1))
            cps.append(pltpu.make_async_remote_copy(src_ref=p_ref, dst_ref=land.at[me], send_sem=send_sems.at[k - 1],
                                                    recv_sem=recv_sems.at[k - 1], device_id=peer, device_id_type=MESH))
        for cp in cps:
            cp.start()
        land[me] = p_ref[...]
        for cp in cps:
            cp.wait()
        total = land[0]
        for d in range(1, N_DEV):
            total = total + land[d]
        o_ref[...] = total

    vm = pl.BlockSpec(memory_space=pltpu.VMEM)
    return pl.pallas_call(
        body, name=name, out_shape=jax.ShapeDtypeStruct(pack.shape, F32),
        in_specs=[vm], out_specs=vm,
        scratch_shapes=[pltpu.VMEM((N_DEV, r, LANES), F32), pltpu.SemaphoreType.DMA((N_DEV - 1,)),
                        pltpu.SemaphoreType.DMA((N_DEV - 1,))],
    )(pack)


BIG_ROWS = 128


def _cast_mxu(w, name):
    nl, r, cdim = w.shape
    tr = BIG_ROWS

    def body(w_ref, o_ref):
        o_ref[...] = w_ref[...].astype(o_ref.dtype)

    blk = pl.BlockSpec((None, tr, cdim), lambda l, i: (l, i, 0))
    return pl.pallas_call(
        body, name=name, out_shape=jax.ShapeDtypeStruct(w.shape, MXU_DTYPE),
        grid=(nl, r // tr), in_specs=[blk], out_specs=blk,
        compiler_params=_params(("parallel", "parallel")),
    )(w)


def _pair_sum(parts, sib, which, out_dtype, name):
    _, k, r, cdim = parts.shape
    tr = BIG_ROWS

    def body(sel_ref, p_ref, s_ref, o_ref):
        o_ref[...] = (p_ref[...] + s_ref[...]).astype(o_ref.dtype)

    grid_spec = pltpu.PrefetchScalarGridSpec(
        num_scalar_prefetch=1, grid=(k, r // tr),
        in_specs=[pl.BlockSpec((None, None, tr, cdim), lambda l, i, sel: (sel[0], l, i, 0)),
                  pl.BlockSpec((None, tr, cdim), lambda l, i, sel: (l, i, 0))],
        out_specs=pl.BlockSpec((None, tr, cdim), lambda l, i, sel: (l, i, 0)))
    return pl.pallas_call(
        body, name=name, out_shape=jax.ShapeDtypeStruct((k, r, cdim), out_dtype), grid_spec=grid_spec,
        compiler_params=_params(("parallel", "parallel")),
    )(which.reshape(1).astype(jnp.int32), parts, sib)


def _sum_lead(parts, which, name):
    k, r, cdim = parts.shape
    tr = BIG_ROWS

    def body(sel_ref, p_ref, o_ref):
        total = p_ref[0].astype(F32)
        for a in range(1, k):
            total = total + p_ref[a].astype(F32)
        o_ref[...] = total

    grid_spec = pltpu.PrefetchScalarGridSpec(
        num_scalar_prefetch=1, grid=(r // tr,),
        in_specs=[pl.BlockSpec((k, tr, cdim), lambda i, sel: (0, i, 0))],
        out_specs=pl.BlockSpec((None, tr, cdim), lambda i, sel: (sel[0], i, 0)))
    return pl.pallas_call(
        body, name=name, out_shape=jax.ShapeDtypeStruct((2, r, cdim), F32), grid_spec=grid_spec,
        compiler_params=_params(("parallel",)),
    )(which.reshape(1).astype(jnp.int32), parts)


def _adam_math(w, g, m, v):
    m2 = ADAM_B1 * m + (1.0 - ADAM_B1) * g
    v2 = ADAM_B2 * v + (1.0 - ADAM_B2) * (g * g)
    m_hat = m2 / (1.0 - ADAM_B1 ** ADAM_STEP)
    v_hat = v2 / (1.0 - ADAM_B2 ** ADAM_STEP)
    delta = -ADAM_LR * (m_hat / (jnp.sqrt(v_hat) + ADAM_EPS) + ADAM_WD * w)
    return delta, m2, v2


def _adam_big(w, g, m, v, name):
    nl, r, cdim = w.shape
    tr = BIG_ROWS

    def body(w_ref, g_ref, m_ref, v_ref, d_ref, mo_ref, vo_ref):
        delta, m2, v2 = _adam_math(w_ref[...], g_ref[...], m_ref[...], v_ref[...])
        d_ref[...] = delta
        mo_ref[...] = m2
        vo_ref[...] = v2

    blk = pl.BlockSpec((None, tr, cdim), lambda l, i: (l, i, 0))
    shp = jax.ShapeDtypeStruct(w.shape, F32)
    return pl.pallas_call(
        body, name=name, out_shape=(shp, shp, shp),
        grid=(nl, r // tr), in_specs=[blk] * 4, out_specs=(blk, blk, blk),
        compiler_params=_params(("parallel", "parallel")),
    )(w, g, m, v)


def _adam_cols_major(w, g, m, v, name):
    cdim, nl, r = w.shape
    tc = BIG_ROWS

    def body(w_ref, g_ref, m_ref, v_ref, d_ref, mo_ref, vo_ref):
        delta, m2, v2 = _adam_math(w_ref[...], g_ref[...], m_ref[...], v_ref[...])
        d_ref[...] = delta
        mo_ref[...] = m2
        vo_ref[...] = v2

    blk = pl.BlockSpec((tc, nl, r), lambda i: (i, 0, 0))
    shp = jax.ShapeDtypeStruct(w.shape, F32)
    return pl.pallas_call(
        body, name=name, out_shape=(shp, shp, shp),
        grid=(pl.cdiv(cdim, tc),), in_specs=[blk] * 4, out_specs=(blk, blk, blk),
        compiler_params=_params(("parallel",)),
    )(w, g, m, v)


def _adam_small(ws, gs, ms, vs, name):
    n = len(ws)

    def body(*refs):
        w_refs, g_refs, m_refs, v_refs = (refs[k * n:(k + 1) * n] for k in range(4))
        d_refs, mo_refs, vo_refs = (refs[(4 + k) * n:(5 + k) * n] for k in range(3))
        for a in range(n):
            delta, m2, v2 = _adam_math(w_refs[a][...], g_refs[a][...], m_refs[a][...], v_refs[a][...])
            d_refs[a][...] = delta
            mo_refs[a][...] = m2
            vo_refs[a][...] = v2

    shapes = tuple(jax.ShapeDtypeStruct(w.shape, F32) for w in ws)
    vm = pl.BlockSpec(memory_space=pltpu.VMEM)
    outs = pl.pallas_call(body, name=name, out_shape=shapes * 3, in_specs=[vm] * (4 * n),
                          out_specs=tuple([vm] * (3 * n)))(*ws, *gs, *ms, *vs)
    return outs[:n], outs[n:2 * n], outs[2 * n:]


PACK_TILE = SUBLANES * LANES


def _pack(arrays):
    rows = []
    for a in arrays:
        flat = a.reshape(-1)
        pad = (-flat.shape[0]) % PACK_TILE
        if pad:
            flat = jnp.concatenate([flat, jnp.zeros((pad,), flat.dtype)])
        rows.append(flat.reshape(-1, LANES))
    return jnp.concatenate(rows, axis=0)


def _unpack(pack, shapes):
    outs, row = [], 0
    for shp in shapes:
        n = int(np.prod(shp))
        nrows = -(-n // PACK_TILE) * SUBLANES
        outs.append(pack[row:row + nrows].reshape(-1)[:n].reshape(shp))
        row += nrows
    return outs


SMALL = ["norm_w", "ssd_conv_b", "ssd_dt_bias", "ssd_a_log", "ssd_d", "ssd_norm_w", "attn_sinks",
         "conf_dw_b", "conf_ln_w", "conf_ln_b"]
WEIGHTS = ["norm_w", "w_in", "ssd_conv_w", "ssd_conv_b", "ssd_dt_bias", "ssd_a_log", "ssd_d", "ssd_norm_w",
           "attn_sinks", "conf_dw_w", "conf_dw_b", "conf_ln_w", "conf_ln_b", "w_out", "final_norm_w"]


def kernel(x, norm_w, w_in, ssd_conv_w, ssd_conv_b, ssd_dt_bias, ssd_a_log, ssd_d, ssd_norm_w, attn_sinks, conf_dw_w, conf_dw_b, conf_ln_w, conf_ln_b, w_out, final_norm_w, loss_target, m_norm_w, m_w_in, m_ssd_conv_w, m_ssd_conv_b, m_ssd_dt_bias, m_ssd_a_log, m_ssd_d, m_ssd_norm_w, m_attn_sinks, m_conf_dw_w, m_conf_dw_b, m_conf_ln_w, m_conf_ln_b, m_w_out, m_final_norm_w, v_norm_w, v_w_in, v_ssd_conv_w, v_ssd_conv_b, v_ssd_dt_bias, v_ssd_a_log, v_ssd_d, v_ssd_norm_w, v_attn_sinks, v_conf_dw_w, v_conf_dw_b, v_conf_ln_w, v_conf_ln_b, v_w_out, v_final_norm_w):
    w = dict(norm_w=norm_w, w_in=w_in, ssd_conv_w=ssd_conv_w, ssd_conv_b=ssd_conv_b, ssd_dt_bias=ssd_dt_bias,
             ssd_a_log=ssd_a_log, ssd_d=ssd_d, ssd_norm_w=ssd_norm_w, attn_sinks=attn_sinks, conf_dw_w=conf_dw_w,
             conf_dw_b=conf_dw_b, conf_ln_w=conf_ln_w, conf_ln_b=conf_ln_b, w_out=w_out, final_norm_w=final_norm_w)
    m = dict(norm_w=m_norm_w, w_in=m_w_in, ssd_conv_w=m_ssd_conv_w, ssd_conv_b=m_ssd_conv_b,
             ssd_dt_bias=m_ssd_dt_bias, ssd_a_log=m_ssd_a_log, ssd_d=m_ssd_d, ssd_norm_w=m_ssd_norm_w,
             attn_sinks=m_attn_sinks, conf_dw_w=m_conf_dw_w, conf_dw_b=m_conf_dw_b, conf_ln_w=m_conf_ln_w,
             conf_ln_b=m_conf_ln_b, w_out=m_w_out, final_norm_w=m_final_norm_w)
    v = dict(norm_w=v_norm_w, w_in=v_w_in, ssd_conv_w=v_ssd_conv_w, ssd_conv_b=v_ssd_conv_b,
             ssd_dt_bias=v_ssd_dt_bias, ssd_a_log=v_ssd_a_log, ssd_d=v_ssd_d, ssd_norm_w=v_ssd_norm_w,
             attn_sinks=v_attn_sinks, conf_dw_w=v_conf_dw_w, conf_dw_b=v_conf_dw_b, conf_ln_w=v_conf_ln_w,
             conf_ln_b=v_conf_ln_b, w_out=v_w_out, final_norm_w=v_final_norm_w)
    depth = w_in.shape[0]
    me = 2 * lax.axis_index("x") + lax.axis_index("y")

    assert depth == 2
    own = [_cast_mxu(w_in, name="cast_w_in"), _cast_mxu(w_out, name="cast_w_out"), ssd_conv_w, conf_dw_w]
    gathered = _gather_weights(own[:2], own[2:], name="gather_weights")
    g_in, g_out, g_conv, g_dw = [lax.dynamic_update_index_in_dim(g_all, mine, me, 0)
                                 for g_all, mine in zip(gathered, own)]
    layer_params = []
    for li in range(depth):
        w_in_p = _padded_from_chips([g_in[p, li] for p in range(N_CHIPS)])
        w_out_full = jnp.concatenate([g_out[p, li] for p in range(N_CHIPS)], axis=0)
        conv_full = jnp.concatenate([g_conv[p, li] for p in range(N_CHIPS)], axis=1)
        dw_full = jnp.concatenate([g_dw[p, li] for p in range(N_CHIPS)], axis=1)
        layer_params.append(_layer_params(li, w_in_p, w_out_full, conv_full, dw_full, w))

    loss, grad_x, grads, dfinal = _local_step(x, loss_target, layer_params, final_norm_w)

    small_list = [grads[li][n] for li in range(depth) for n in SMALL]
    small_list += [grads[li][n] for li in range(depth) for n in ("ssd_conv_w", "conf_dw_w")]
    small_list += [dfinal, loss.reshape(1)]
    small_shapes = [a.shape for a in small_list]
    reduced = _unpack(_allreduce_small(_pack(small_list), name="allreduce_small"), small_shapes)
    ns = len(SMALL)
    g = {n: jnp.stack([reduced[li * ns + i] for li in range(depth)]) for i, n in enumerate(SMALL)}
    conv_w_cols, dw_w_cols = ssd_conv_w.shape[2], conf_dw_w.shape[2]
    g["ssd_conv_w"] = jnp.stack([lax.dynamic_slice_in_dim(reduced[depth * ns + 2 * li], me * conv_w_cols,
                                                          conv_w_cols, axis=1) for li in range(depth)])
    g["conf_dw_w"] = jnp.stack([lax.dynamic_slice_in_dim(reduced[depth * ns + 2 * li + 1], me * dw_w_cols,
                                                         dw_w_cols, axis=1) for li in range(depth)])
    g["final_norm_w"] = reduced[-2]
    loss_total = reduced[-1][0]

    cols = w_in.shape[2]
    rows_out = w_out.shape[1]
    p_in = jnp.stack([jnp.stack([_chip_part_from_padded(grads[li]["w_in_p"], p, cols) for p in range(N_CHIPS)])
                      for li in range(depth)])
    p_out = jnp.stack([grads[li]["w_out"].reshape(N_CHIPS, rows_out, D_MODEL) for li in range(depth)])
    c = lax.axis_index("c")
    sib_in, sib_out = _pair_swap_halves([p_in, p_out], name="grad_pair_swap")
    s_in = _pair_sum(p_in, sib_in, c, MXU_DTYPE, name="grad_pair_sum_in")
    s_out = _pair_sum(p_out, sib_out, c, MXU_DTYPE, name="grad_pair_sum_out")
    r_in, r_out = _chip_scatter([s_in, s_out], name="grad_chip_scatter")
    r_in = lax.dynamic_update_index_in_dim(r_in, lax.dynamic_index_in_dim(s_in, me, 0, keepdims=False), me, 0)
    r_out = lax.dynamic_update_index_in_dim(r_out, lax.dynamic_index_in_dim(s_out, me, 0, keepdims=False), me, 0)
    t_in = _sum_lead(r_in, c, name="grad_chip_sum_in")
    t_out = _sum_lead(r_out, c, name="grad_chip_sum_out")
    g_w_in, g_w_out = _pair_gather([t_in, t_out], name="grad_pair_gather")

    outs_g, outs_d, outs_m, outs_v = {"w_in": g_w_in, "w_out": g_w_out}, {}, {}, {}
    to_cols, from_cols = (2, 0, 1), (1, 2, 0)
    outs_d["w_in"], outs_m["w_in"], outs_v["w_in"] = [
        jnp.transpose(a, from_cols) for a in _adam_cols_major(
            *[jnp.transpose(a, to_cols) for a in (w_in, g_w_in, m_w_in, v_w_in)], name="adam_w_in")]
    outs_d["w_out"], outs_m["w_out"], outs_v["w_out"] = _adam_big(w_out, g_w_out, m_w_out, v_w_out,
                                                                  name="adam_w_out")
    small_names = [n for n in WEIGHTS if n not in ("w_in", "w_out")]
    def as2d(a):
        return a.reshape(1, -1) if a.ndim == 1 else a

    deltas, new_ms, new_vs = _adam_small(*[[as2d(src[n]) for n in small_names] for src in (w, g, m, v)],
                                         name="adam_small")
    for n, dn, mn, vn in zip(small_names, deltas, new_ms, new_vs):
        outs_g[n], outs_d[n], outs_m[n], outs_v[n] = (g[n], dn.reshape(w[n].shape), mn.reshape(w[n].shape),
                                                      vn.reshape(w[n].shape))
    return (loss_total, grad_x, *[outs_g[n] for n in WEIGHTS], *[outs_d[n] for n in WEIGHTS],
            *[outs_m[n] for n in WEIGHTS], *[outs_v[n] for n in WEIGHTS])
```

```python
import functools
import math

import jax
import jax.numpy as jnp
import numpy as np
from jax import lax
from jax.experimental import pallas as pl
from jax.experimental.pallas import tpu as pltpu

F32 = jnp.float32
BF16 = jnp.bfloat16
MXU_DTYPE = BF16

D_MODEL = 1024
DEPTH = 2
SSD_HEADS = 16
SSD_HEAD_DIM = 64
SSD_STATE = 128
SSD_CONV = 4
CHUNK = 128
SSD_CONV_DIM = 1536
ATTN_HEAD_DIM = 64
ATTN_Q_HEADS = 8
WINDOW = 128
CONF_WIDTH = 512
CONF_KERNEL = 31
MIX_WIDTH = 2048
D_IN_PROJ = 5392
EPS = 1e-5

ADAM_LR = 0.001
ADAM_B1 = 0.9
ADAM_B2 = 0.999
ADAM_EPS = 1e-08
ADAM_WD = 0.01
ADAM_STEP = 10

LANES = 128
SUBLANES = 8
VMEM_LIMIT = 48 * 1024 * 1024

NP = 5632
OFF_ZA, OFF_Q, OFF_K, OFF_V, OFF_DT = 0, 512, 1024, 1152, 1280
ATTN_GROUP = 1536
OFF_XBC = 1536
OFF_CONF = 3072
OFF_ZS = 4096
OFF_ZC = 5120
SECTIONS = ((0, 1024, OFF_ZS), (1024, 1536, OFF_ZA), (1536, 2048, OFF_ZC), (2048, 3584, OFF_XBC),
            (3584, 3600, OFF_DT), (3600, 4368, OFF_Q), (4368, 5392, OFF_CONF))

YCAT_ATTN, YCAT_CONF = 1024, 1536
ANY = pl.BlockSpec(memory_space=pl.ANY)

NN = (((1,), (0,)), ((), ()))
NT = (((1,), (1,)), ((), ()))
TN = (((0,), (0,)), ((), ()))


def _params(sem):
    return pltpu.CompilerParams(dimension_semantics=sem, vmem_limit_bytes=VMEM_LIMIT)


def _dot(a, b, dims=NN):
    return lax.dot_general(a.astype(MXU_DTYPE), b.astype(MXU_DTYPE), dims, preferred_element_type=F32)


def _split_bf16(a, passes):
    pieces = []
    r = a
    for _ in range(passes):
        p = r.astype(BF16)
        pieces.append(p)
        r = r - p.astype(F32)
    return pieces


def _xdot(a, sel, dims=NN, passes=2):
    out = None
    for p in _split_bf16(a, passes):
        t = lax.dot_general(p, sel, dims, preferred_element_type=F32)
        out = t if out is None else out + t
    return out


def _xdot_r(sel, b, dims=NN, passes=3):
    out = None
    for p in _split_bf16(b, passes):
        t = lax.dot_general(sel, p, dims, preferred_element_type=F32)
        out = t if out is None else out + t
    return out


def _sigmoid(x):
    return 1.0 / (1.0 + jnp.exp(-x))


def _silu(x):
    return x * _sigmoid(x)


def _dsilu(x):
    s = _sigmoid(x)
    return s * (1.0 + x * (1.0 - s))


def _softplus(x):
    return jnp.maximum(x, 0.0) + jnp.log(1.0 + jnp.exp(-jnp.abs(x)))


def _rowsum8(x):
    r, c = x.shape
    return jnp.sum(x.reshape(r // SUBLANES, SUBLANES, c), axis=0)


def _iota(shape, dim):
    return lax.broadcasted_iota(jnp.int32, shape, dim)


def _matmul(a, b, form, out_dtype, tm, tn, tk, name, residual=None):
    if form == "nn":
        (m, k), n = a.shape, b.shape[1]
    elif form == "nt":
        (m, k), n = a.shape, b.shape[0]
    else:
        (k, m), n = a.shape, b.shape[1]
    tm, tn, tk = min(tm, m), min(tn, n), min(tk, k)
    assert m % tm == 0 and n % tn == 0 and k % tk == 0, (name, m, n, k, tm, tn, tk)
    if form == "nn":
        a_spec = pl.BlockSpec((tm, tk), lambda i, j, s: (i, s))
        b_spec = pl.BlockSpec((tk, tn), lambda i, j, s: (s, j))
        dims = NN
    elif form == "nt":
        (m, k), n = a.shape, b.shape[0]
        a_spec = pl.BlockSpec((tm, tk), lambda i, j, s: (i, s))
        b_spec = pl.BlockSpec((tn, tk), lambda i, j, s: (j, s))
        dims = NT
    else:
        (k, m), n = a.shape, b.shape[1]
        a_spec = pl.BlockSpec((tk, tm), lambda i, j, s: (s, i))
        b_spec = pl.BlockSpec((tk, tn), lambda i, j, s: (s, j))
        dims = TN
    nk = k // tk
    has_res = residual is not None

    def body_single(a_ref, b_ref, *rest):
        o = _dot(a_ref[...], b_ref[...], dims)
        if has_res:
            o = o + rest[0][...]
        rest[-1][...] = o.astype(out_dtype)

    def body(a_ref, b_ref, *rest):
        if has_res:
            r_ref, o_ref, acc = rest
        else:
            o_ref, acc = rest
        s = pl.program_id(2)

        @pl.when(s == 0)
        def _():
            acc[...] = jnp.zeros_like(acc)

        acc[...] += _dot(a_ref[...], b_ref[...], dims)

        @pl.when(s == nk - 1)
        def _():
            o = acc[...]
            if has_res:
                o = o + r_ref[...]
            o_ref[...] = o.astype(out_dtype)

    in_specs = [a_spec, b_spec]
    args = [a, b]
    if has_res:
        in_specs.append(pl.BlockSpec((tm, tn), lambda i, j, s: (i, j)))
        args.append(residual)
    return pl.pallas_call(
        body_single if nk == 1 else body, name=name,
        out_shape=jax.ShapeDtypeStruct((m, n), out_dtype),
        grid=(m // tm, n // tn, nk),
        in_specs=in_specs,
        out_specs=pl.BlockSpec((tm, tn), lambda i, j, s: (i, j)),
        scratch_shapes=[] if nk == 1 else [pltpu.VMEM((tm, tn), F32)],
        compiler_params=_params(("parallel", "parallel", "arbitrary")),
    )(*args)


ROW_TILE = 256


def _rmsnorm_fwd(x, w, name, after=None):
    t, d = x.shape
    tm = ROW_TILE
    deps = [] if after is None else [after]

    def body(x_ref, w_ref, *rest):
        o_ref, ot_ref = rest[len(deps):]
        xv = x_ref[...]
        rstd = lax.rsqrt(jnp.mean(xv * xv, axis=-1, keepdims=True) + EPS)
        h = xv * rstd * w_ref[...]
        o_ref[...] = h.astype(o_ref.dtype)
        ot_ref[...] = h.T.astype(ot_ref.dtype)

    return pl.pallas_call(
        body, name=name,
        out_shape=(jax.ShapeDtypeStruct((t, d), MXU_DTYPE), jax.ShapeDtypeStruct((d, t), MXU_DTYPE)),
        grid=(t // tm,),
        in_specs=[pl.BlockSpec((tm, d), lambda i: (i, 0)), pl.BlockSpec((1, d), lambda i: (0, 0))]
        + [ANY] * len(deps),
        out_specs=(pl.BlockSpec((tm, d), lambda i: (i, 0)), pl.BlockSpec((d, tm), lambda i: (0, i))),
        compiler_params=_params(("parallel",)),
    )(x, w, *deps)


def _rmsnorm_bwd(dh, x, w, dres, name):
    t, d = x.shape
    tm = ROW_TILE
    nt = t // tm

    def body(dh_ref, x_ref, w_ref, dr_ref, dx_ref, dw_ref, acc):
        i = pl.program_id(0)

        @pl.when(i == 0)
        def _():
            acc[...] = jnp.zeros_like(acc)

        xv = x_ref[...]
        rstd = lax.rsqrt(jnp.mean(xv * xv, axis=-1, keepdims=True) + EPS)
        xh = xv * rstd
        dhv = dh_ref[...]
        g = dhv * w_ref[...]
        dx_ref[...] = dr_ref[...] + rstd * (g - xh * jnp.mean(g * xh, axis=-1, keepdims=True))
        acc[...] += _rowsum8(dhv * xh)

        @pl.when(i == nt - 1)
        def _():
            dw_ref[...] = jnp.sum(acc[...], axis=0, keepdims=True)

    row = pl.BlockSpec((tm, d), lambda i: (i, 0))
    vec = pl.BlockSpec((1, d), lambda i: (0, 0))
    return pl.pallas_call(
        body, name=name,
        out_shape=(jax.ShapeDtypeStruct((t, d), F32), jax.ShapeDtypeStruct((1, d), F32)),
        grid=(nt,),
        in_specs=[row, row, vec, row],
        out_specs=(row, vec),
        scratch_shapes=[pltpu.VMEM((SUBLANES, d), F32)],
        compiler_params=_params(("arbitrary",)),
    )(dh, x, w, dres)


def _loss_head(xf, target, w, name):
    t, d = xf.shape
    tm = ROW_TILE
    nt = t // tm

    def body(x_ref, t_ref, w_ref, loss_ref, dx_ref, dw_ref, lacc, wacc):
        i = pl.program_id(0)

        @pl.when(i == 0)
        def _():
            lacc[...] = jnp.zeros_like(lacc)
            wacc[...] = jnp.zeros_like(wacc)

        xv = x_ref[...]
        rstd = lax.rsqrt(jnp.mean(xv * xv, axis=-1, keepdims=True) + EPS)
        xh = xv * rstd
        err = xh * w_ref[...] - t_ref[...]
        lacc[...] += jnp.sum(err * err)
        dy = err * (1.0 / d)
        g = dy * w_ref[...]
        dx_ref[...] = rstd * (g - xh * jnp.mean(g * xh, axis=-1, keepdims=True))
        wacc[...] += _rowsum8(dy * xh)

        @pl.when(i == nt - 1)
        def _():
            loss_ref[...] = lacc[...] * (0.5 / d)
            dw_ref[...] = jnp.sum(wacc[...], axis=0, keepdims=True)

    row = pl.BlockSpec((tm, d), lambda i: (i, 0))
    vec = pl.BlockSpec((1, d), lambda i: (0, 0))
    return pl.pallas_call(
        body, name=name,
        out_shape=(jax.ShapeDtypeStruct((SUBLANES, LANES), F32), jax.ShapeDtypeStruct((t, d), F32),
                   jax.ShapeDtypeStruct((1, d), F32)),
        grid=(nt,),
        in_specs=[row, row, vec],
        out_specs=(pl.BlockSpec((SUBLANES, LANES), lambda i: (0, 0)), row, vec),
        scratch_shapes=[pltpu.VMEM((SUBLANES, LANES), F32), pltpu.VMEM((SUBLANES, d), F32)],
        compiler_params=_params(("arbitrary",)),
    )(xf, target, w)


def _glu_fwd(proj, name):
    t = proj.shape[0]
    tm, cw = ROW_TILE, CONF_WIDTH

    def body(a_ref, g_ref, o_ref):
        o_ref[...] = a_ref[...] * _sigmoid(g_ref[...])

    return pl.pallas_call(
        body, name=name,
        out_shape=jax.ShapeDtypeStruct((t, cw), F32),
        grid=(t // tm,),
        in_specs=[pl.BlockSpec((tm, cw), lambda i: (i, OFF_CONF // cw)),
                  pl.BlockSpec((tm, cw), lambda i: (i, OFF_CONF // cw + 1))],
        out_specs=pl.BlockSpec((tm, cw), lambda i: (i, 0)),
        compiler_params=_params(("parallel",)),
    )(proj, proj)


def _glu_bwd(dc0, proj, dproj, name):
    t = proj.shape[0]
    tm, cw = ROW_TILE, CONF_WIDTH

    def body(d_ref, a_ref, g_ref, _, o_ref):
        s = _sigmoid(g_ref[...])
        dv = d_ref[...]
        o_ref[:, :cw] = (dv * s).astype(o_ref.dtype)
        o_ref[:, cw:] = (dv * a_ref[...] * s * (1.0 - s)).astype(o_ref.dtype)

    return pl.pallas_call(
        body, name=name,
        out_shape=jax.ShapeDtypeStruct(dproj.shape, dproj.dtype),
        grid=(t // tm,),
        in_specs=[pl.BlockSpec((tm, cw), lambda i: (i, 0)),
                  pl.BlockSpec((tm, cw), lambda i: (i, OFF_CONF // cw)),
                  pl.BlockSpec((tm, cw), lambda i: (i, OFF_CONF // cw + 1)), ANY],
        out_specs=pl.BlockSpec((tm, 2 * cw), lambda i: (i, OFF_CONF // (2 * cw))),
        input_output_aliases={3: 0},
        compiler_params=_params(("parallel",)),
    )(dc0, proj, proj, dproj)


def _conf_post_fwd(c1, proj, ln_w, ln_b, ycat, name):
    t = c1.shape[0]
    tm, cw = ROW_TILE, CONF_WIDTH

    def body(c_ref, z_ref, w_ref, b_ref, _, o_ref):
        cv = c_ref[...]
        xc = cv - jnp.mean(cv, axis=-1, keepdims=True)
        rstd = lax.rsqrt(jnp.mean(xc * xc, axis=-1, keepdims=True) + EPS)
        c2 = xc * rstd * w_ref[...] + b_ref[...]
        o_ref[...] = (_silu(c2) * _silu(z_ref[...])).astype(o_ref.dtype)

    vec = pl.BlockSpec((1, cw), lambda i: (0, 0))
    return pl.pallas_call(
        body, name=name,
        out_shape=jax.ShapeDtypeStruct(ycat.shape, ycat.dtype),
        grid=(t // tm,),
        in_specs=[pl.BlockSpec((tm, cw), lambda i: (i, 0)),
                  pl.BlockSpec((tm, cw), lambda i: (i, OFF_ZC // cw)), vec, vec, ANY],
        out_specs=pl.BlockSpec((tm, cw), lambda i: (i, YCAT_CONF // cw)),
        input_output_aliases={4: 0},
        compiler_params=_params(("parallel",)),
    )(c1, proj, ln_w, ln_b, ycat)


def _conf_post_bwd(dycat, c1, proj, ln_w, ln_b, dproj, name):
    t = c1.shape[0]
    tm, cw = ROW_TILE, CONF_WIDTH
    nt = t // tm

    def body(dy_ref, c_ref, z_ref, w_ref, b_ref, _, dc_ref, dz_ref, dw_ref, db_ref, wacc, bacc):
        i = pl.program_id(0)

        @pl.when(i == 0)
        def _():
            wacc[...] = jnp.zeros_like(wacc)
            bacc[...] = jnp.zeros_like(bacc)

        cv = c_ref[...]
        xc = cv - jnp.mean(cv, axis=-1, keepdims=True)
        rstd = lax.rsqrt(jnp.mean(xc * xc, axis=-1, keepdims=True) + EPS)
        xh = xc * rstd
        c2 = xh * w_ref[...] + b_ref[...]
        zv = z_ref[...]
        dy = dy_ref[...]
        dz_ref[...] = (dy * _silu(c2) * _dsilu(zv)).astype(dz_ref.dtype)
        dc2 = dy * _silu(zv) * _dsilu(c2)
        bacc[...] += _rowsum8(dc2)
        wacc[...] += _rowsum8(dc2 * xh)
        dxh = dc2 * w_ref[...]
        dc_ref[...] = rstd * (dxh - jnp.mean(dxh, axis=-1, keepdims=True)
                              - xh * jnp.mean(dxh * xh, axis=-1, keepdims=True))

        @pl.when(i == nt - 1)
        def _():
            dw_ref[...] = jnp.sum(wacc[...], axis=0, keepdims=True)
            db_ref[...] = jnp.sum(bacc[...], axis=0, keepdims=True)

    row = pl.BlockSpec((tm, cw), lambda i: (i, 0))
    vec = pl.BlockSpec((1, cw), lambda i: (0, 0))
    return pl.pallas_call(
        body, name=name,
        out_shape=(jax.ShapeDtypeStruct((t, cw), F32), jax.ShapeDtypeStruct(dproj.shape, dproj.dtype),
                   jax.ShapeDtypeStruct((1, cw), F32), jax.ShapeDtypeStruct((1, cw), F32)),
        grid=(nt,),
        in_specs=[pl.BlockSpec((tm, cw), lambda i: (i, YCAT_CONF // cw)), row,
                  pl.BlockSpec((tm, cw), lambda i: (i, OFF_ZC // cw)), vec, vec, ANY],
        out_specs=(row, pl.BlockSpec((tm, cw), lambda i: (i, OFF_ZC // cw)), vec, vec),
        input_output_aliases={5: 1},
        scratch_shapes=[pltpu.VMEM((SUBLANES, cw), F32), pltpu.VMEM((SUBLANES, cw), F32)],
        compiler_params=_params(("arbitrary",)),
    )(dycat, c1, proj, ln_w, ln_b, dproj)


CONV_TILE = 512
CONV_COLS = 512
CONV_SUB_ROWS = 128
CONV_SUB_COLS = LANES


def _conv_halo(k):
    return SUBLANES if k - 1 <= SUBLANES else 32


def _conv_subtiles(tm, cw):
    return [(r0, c0) for r0 in range(0, tm, CONV_SUB_ROWS) for c0 in range(0, cw, CONV_SUB_COLS)]


def _conv_use_shifted(k):
    return k > SUBLANES


def _conv_shift_scratch(k, rows, cw):
    return [pltpu.VMEM((SUBLANES - 1, rows - SUBLANES, cw), F32)] if _conv_use_shifted(k) else []


def _conv_fill_shifted(ext, sh):
    n = sh.shape[1]
    for b in range(1, SUBLANES):
        sh[b - 1] = ext[b:b + n, :]


def _conv_rows(ext, sh, start, rows, cs):
    b = start % SUBLANES
    if b == 0 or not sh:
        return ext[start:start + rows, cs]
    return sh[0][b - 1, start - b:start - b + rows, cs]


def _conv_fwd(src, col0, width, w, bias, k, seq, name):
    t = src.shape[0]
    tm, cw, halo = CONV_TILE, CONV_COLS, _conv_halo(k)
    sr, sc = CONV_SUB_ROWS, CONV_SUB_COLS
    p = k - 1
    cb0 = col0 // cw
    kp = w.shape[0]

    shifted = _conv_use_shifted(k)

    def body(x_ref, h_ref, w_ref, b_ref, o_ref, ext, *sh):
        i = pl.program_id(0)
        seq_start = (i * tm) % seq == 0
        ext[halo:, :] = x_ref[...]
        ext[:halo, :] = jnp.where(seq_start, 0.0, h_ref[...])
        if shifted:
            _conv_fill_shifted(ext, sh[0])
        for r0, c0 in _conv_subtiles(tm, cw):
            cs = slice(c0, c0 + sc)
            acc = jnp.zeros((sr, sc), F32) + b_ref[:, cs]
            for j in range(k):
                acc = acc + w_ref[j:j + 1, cs] * _conv_rows(ext, sh, r0 + halo - p + j, sr, cs)
            o_ref[r0:r0 + sr, cs] = acc

    return pl.pallas_call(
        body, name=name,
        out_shape=jax.ShapeDtypeStruct((t, width), F32),
        grid=(t // tm, width // cw),
        in_specs=[pl.BlockSpec((tm, cw), lambda i, j: (i, cb0 + j)),
                  pl.BlockSpec((halo, cw), lambda i, j: (jnp.maximum(i * (tm // halo) - 1, 0), cb0 + j)),
                  pl.BlockSpec((kp, cw), lambda i, j: (0, j)),
                  pl.BlockSpec((1, cw), lambda i, j: (0, j))],
        out_specs=pl.BlockSpec((tm, cw), lambda i, j: (i, j)),
        scratch_shapes=[pltpu.VMEM((halo + tm, cw), F32)] + _conv_shift_scratch(k, halo + tm, cw),
        compiler_params=_params(("parallel", "parallel")),
    )(src, src, w, bias)


def _conv_bwd(dy, src, col0, width, w, k, seq, name, into=None):
    t = src.shape[0]
    tm, cw, halo = CONV_TILE, CONV_COLS, _conv_halo(k)
    sr, sc = CONV_SUB_ROWS, CONV_SUB_COLS
    p = k - 1
    cb0 = col0 // cw
    kp = w.shape[0]
    nt = t // tm
    last_halo = t // halo - 1

    shifted = _conv_use_shifted(k)

    def body(dy_ref, dn_ref, x_ref, xp_ref, w_ref, *rest):
        if into is not None:
            rest = rest[1:]
        dx_ref, dw_ref, db_ref, dyext, xext, wacc, bacc = rest[:7]
        sh = rest[7:]
        i = pl.program_id(1)
        dysh, xsh = (sh[:1], sh[1:]) if shifted else ((), ())

        @pl.when(i == 0)
        def _():
            wacc[...] = jnp.zeros_like(wacc)
            bacc[...] = jnp.zeros_like(bacc)

        seq_start = (i * tm) % seq == 0
        seq_end = ((i + 1) * tm) % seq == 0
        dyext[:tm, :] = dy_ref[...]
        dyext[tm:, :] = jnp.where(seq_end, 0.0, dn_ref[...])
        xext[halo:, :] = x_ref[...]
        xext[:halo, :] = jnp.where(seq_start, 0.0, xp_ref[...])
        if shifted:
            _conv_fill_shifted(dyext, dysh[0])
            _conv_fill_shifted(xext, xsh[0])
        for r0, c0 in _conv_subtiles(tm, cw):
            cs = slice(c0, c0 + sc)
            dyv = dy_ref[r0:r0 + sr, cs]
            acc = jnp.zeros((sr, sc), F32)
            for j in range(k):
                acc = acc + w_ref[j:j + 1, cs] * _conv_rows(dyext, dysh, r0 + p - j, sr, cs)
                wacc[j, :, cs] += _rowsum8(dyv * _conv_rows(xext, xsh, r0 + halo - p + j, sr, cs))
            dx_ref[r0:r0 + sr, cs] = acc.astype(dx_ref.dtype)
            bacc[:, cs] += _rowsum8(dyv)

        @pl.when(i == nt - 1)
        def _():
            dw_ref[...] = jnp.zeros_like(dw_ref)
            for j in range(k):
                dw_ref[j:j + 1, :] = jnp.sum(wacc[j], axis=0, keepdims=True)
            db_ref[...] = jnp.sum(bacc[...], axis=0, keepdims=True)

    if into is None:
        dx_shape = jax.ShapeDtypeStruct((t, width), F32)
        dx_spec = pl.BlockSpec((tm, cw), lambda j, i: (i, j))
        extra_specs, extra_args, aliases = [], [], {}
    else:
        dx_shape = jax.ShapeDtypeStruct(into.shape, into.dtype)
        dx_spec = pl.BlockSpec((tm, cw), lambda j, i: (i, cb0 + j))
        extra_specs, extra_args, aliases = [ANY], [into], {5: 0}
    return pl.pallas_call(
        body, name=name,
        out_shape=(dx_shape, jax.ShapeDtypeStruct((kp, width), F32), jax.ShapeDtypeStruct((1, width), F32)),
        grid=(width // cw, nt),
        in_specs=[pl.BlockSpec((tm, cw), lambda j, i: (i, j)),
                  pl.BlockSpec((halo, cw), lambda j, i: (jnp.minimum((i + 1) * (tm // halo), last_halo), j)),
                  pl.BlockSpec((tm, cw), lambda j, i: (i, cb0 + j)),
                  pl.BlockSpec((halo, cw), lambda j, i: (jnp.maximum(i * (tm // halo) - 1, 0), cb0 + j)),
                  pl.BlockSpec((kp, cw), lambda j, i: (0, j))] + extra_specs,
        out_specs=(dx_spec,
                   pl.BlockSpec((kp, cw), lambda j, i: (0, j)),
                   pl.BlockSpec((1, cw), lambda j, i: (0, j))),
        input_output_aliases=aliases,
        scratch_shapes=[pltpu.VMEM((tm + halo, cw), F32), pltpu.VMEM((halo + tm, cw), F32),
                        pltpu.VMEM((kp, SUBLANES, cw), F32), pltpu.VMEM((SUBLANES, cw), F32)]
        + 2 * _conv_shift_scratch(k, halo + tm, cw),
        compiler_params=_params(("parallel", "arbitrary")),
    )(dy, dy, src, src, w, *extra_args)


def _half_mask(half):
    lane = _iota((1, LANES), 1)
    return ((lane >= half * ATTN_HEAD_DIM) & (lane < (half + 1) * ATTN_HEAD_DIM)).astype(F32)


def _stack_heads(xp, g):
    m = _half_mask(g)
    swapped = pltpu.roll(xp, ATTN_HEAD_DIM, axis=1)
    return jnp.concatenate([xp * m, swapped * m] if g == 0 else [swapped * m, xp * m], axis=0)


def _unstack_heads(both, g):
    w = both.shape[0] // 2
    top, bot = both[:w], both[w:]
    lo, hi = _half_mask(0), _half_mask(1)
    if g == 0:
        return top * lo + pltpu.roll(bot, ATTN_HEAD_DIM, axis=1) * hi
    return pltpu.roll(top, ATTN_HEAD_DIM, axis=1) * lo + bot * hi


def _band_mask(first_block):
    w = WINDOW
    qi = _iota((w, 2 * w), 0)
    kj = _iota((w, 2 * w), 1) - w
    rel = qi - kj
    return (rel >= 0) & (rel < w) & (jnp.logical_not(first_block) | (kj >= 0))


def _lane_pick(x, h):
    return jnp.sum(jnp.where(_iota(x.shape, 1) == h, x, 0.0), axis=1, keepdims=True)


def _attn_specs(nb, rev):
    w = WINDOW

    def blk(i):
        return nb - 1 - i if rev else i

    def row(b, i):
        return b * nb + blk(i)

    def prow(b, i):
        return b * nb + jnp.maximum(blk(i) - 1, 0)

    q = pl.BlockSpec((w, 512), lambda b, i: (row(b, i), OFF_Q // 512))
    kc = pl.BlockSpec((w, 128), lambda b, i: (row(b, i), OFF_K // 128))
    kp = pl.BlockSpec((w, 128), lambda b, i: (prow(b, i), OFF_K // 128))
    vc = pl.BlockSpec((w, 128), lambda b, i: (row(b, i), OFF_V // 128))
    vp = pl.BlockSpec((w, 128), lambda b, i: (prow(b, i), OFF_V // 128))
    z = pl.BlockSpec((w, 512), lambda b, i: (row(b, i), OFF_ZA // 512))
    return q, kc, kp, vc, vp, z, row


def _attn_fwd(proj, sinks, ycat, nbatch, name):
    t = proj.shape[0]
    w = WINDOW
    nb = t // nbatch // w
    scale = ATTN_HEAD_DIM ** -0.5
    q_s, kc_s, kp_s, vc_s, vp_s, z_s, row = _attn_specs(nb, False)

    def body(q_ref, kc_ref, kp_ref, vc_ref, vp_ref, z_ref, sk_ref, _, y_ref, o_ref, lse_ref):
        first = pl.program_id(1) == 0
        mask = _band_mask(first)
        kk = jnp.concatenate([kp_ref[...], kc_ref[...]], axis=0).astype(MXU_DTYPE)
        vv = jnp.concatenate([vp_ref[...], vc_ref[...]], axis=0).astype(MXU_DTYPE)
        sk = sk_ref[...]
        lane = _iota((w, LANES), 1)
        mask2 = jnp.concatenate([mask, mask], axis=0)
        scores = [_dot(_stack_heads(q_ref[:, j * LANES:(j + 1) * LANES], j // 2), kk, NT) for j in range(4)]
        lse_all = jnp.zeros((w, LANES), F32)
        for j in range(4):
            s = jnp.where(mask2, scores[j] * scale, -1e30)
            skc = jnp.concatenate([jnp.broadcast_to(_lane_pick(sk, 2 * j), (w, 1)),
                                   jnp.broadcast_to(_lane_pick(sk, 2 * j + 1), (w, 1))], axis=0)
            m = jnp.maximum(jnp.max(s, axis=1, keepdims=True), skc)
            den = jnp.sum(jnp.exp(s - m), axis=1, keepdims=True) + jnp.exp(skc - m)
            lse = m + jnp.log(den)
            lse_all = jnp.where(lane == 2 * j, lse[:w], lse_all)
            lse_all = jnp.where(lane == 2 * j + 1, lse[w:], lse_all)
            op = _unstack_heads(_dot(jnp.exp(s - lse), vv), j // 2)
            cols = slice(j * LANES, (j + 1) * LANES)
            o_ref[:, cols] = op
            y_ref[:, cols] = (op * _silu(z_ref[:, cols])).astype(y_ref.dtype)
        lse_ref[...] = lse_all

    return pl.pallas_call(
        body, name=name,
        out_shape=(jax.ShapeDtypeStruct(ycat.shape, ycat.dtype), jax.ShapeDtypeStruct((t, 512), F32),
                   jax.ShapeDtypeStruct((t, LANES), F32)),
        grid=(nbatch, nb),
        in_specs=[q_s, kc_s, kp_s, vc_s, vp_s, z_s, pl.BlockSpec((1, LANES), lambda b, i: (0, 0)), ANY],
        out_specs=(pl.BlockSpec((w, 512), lambda b, i: (row(b, i), YCAT_ATTN // 512)),
                   pl.BlockSpec((w, 512), lambda b, i: (row(b, i), 0)),
                   pl.BlockSpec((w, LANES), lambda b, i: (row(b, i), 0))),
        input_output_aliases={7: 0},
        compiler_params=_params(("parallel", "parallel")),
    )(proj, proj, proj, proj, proj, proj, sinks, ycat)


def _attn_bwd(dycat, proj, o, lse, sinks, ddt, dproj, nbatch, name):
    t = proj.shape[0]
    w = WINDOW
    nb = t // nbatch // w
    scale = ATTN_HEAD_DIM ** -0.5
    q_s, kc_s, kp_s, vc_s, vp_s, z_s, row = _attn_specs(nb, True)

    def body(dy_ref, q_ref, kc_ref, kp_ref, vc_ref, vp_ref, z_ref, o_ref, lse_ref, sk_ref, ddt_ref, _,
             grp_ref, dsk_ref, kcarry, vcarry, sacc):
        b, i = pl.program_id(0), pl.program_id(1)

        @pl.when((b == 0) & (i == 0))
        def _():
            sacc[...] = jnp.zeros_like(sacc)

        @pl.when(i == 0)
        def _():
            kcarry[...] = jnp.zeros_like(kcarry)
            vcarry[...] = jnp.zeros_like(vcarry)

        first = i == nb - 1
        mask = _band_mask(first)
        kk = jnp.concatenate([kp_ref[...], kc_ref[...]], axis=0).astype(MXU_DTYPE)
        vv = jnp.concatenate([vp_ref[...], vc_ref[...]], axis=0).astype(MXU_DTYPE)
        sk = sk_ref[...]
        lse_all = lse_ref[...]
        lane1 = _iota((1, LANES), 1)
        mask2 = jnp.concatenate([mask, mask], axis=0)
        qs, dos, deltas, lses, scores, dps = [], [], [], [], [], []
        for j in range(4):
            cols = slice(j * LANES, (j + 1) * LANES)
            qp, zp, ov, dy = q_ref[:, cols], z_ref[:, cols], o_ref[:, cols], dy_ref[:, cols]
            grp_ref[:, OFF_ZA + j * LANES:OFF_ZA + (j + 1) * LANES] = (dy * ov * _dsilu(zp)).astype(grp_ref.dtype)
            do = dy * _silu(zp)
            q2 = _stack_heads(qp, j // 2).astype(MXU_DTYPE)
            do2 = _stack_heads(do, j // 2)
            qs.append(q2)
            dos.append(do2.astype(MXU_DTYPE))
            deltas.append(jnp.sum(do2 * _stack_heads(ov, j // 2), axis=1, keepdims=True))
            lses.append(jnp.concatenate([_lane_pick(lse_all, 2 * j), _lane_pick(lse_all, 2 * j + 1)], axis=0))
            scores.append(_dot(q2, kk, NT))
            dps.append(_dot(do2, vv, NT))
        prs, dss = [], []
        dsk = jnp.zeros((1, LANES), F32)
        for j in range(4):
            pr = jnp.exp(jnp.where(mask2, scores[j] * scale, -1e30) - lses[j])
            prs.append(pr.astype(MXU_DTYPE))
            dss.append((pr * (dps[j] - deltas[j])).astype(MXU_DTYPE))
            skc = jnp.concatenate([jnp.broadcast_to(_lane_pick(sk, 2 * j), (w, 1)),
                                   jnp.broadcast_to(_lane_pick(sk, 2 * j + 1), (w, 1))], axis=0)
            sink_term = jnp.exp(skc - lses[j]) * deltas[j]
            dsk = dsk - jnp.where(lane1 == 2 * j, jnp.sum(sink_term[:w]), 0.0)
            dsk = dsk - jnp.where(lane1 == 2 * j + 1, jnp.sum(sink_term[w:]), 0.0)
        dkk = jnp.zeros((2 * w, LANES), F32)
        dvv = jnp.zeros((2 * w, LANES), F32)
        for j in range(4):
            dq = _unstack_heads(_dot(dss[j], kk) * scale, j // 2)
            grp_ref[:, OFF_Q + j * LANES:OFF_Q + (j + 1) * LANES] = dq.astype(grp_ref.dtype)
            dkk = dkk + _dot(dss[j], qs[j], TN) * scale
            dvv = dvv + _dot(prs[j], dos[j], TN)
        grp_ref[:, OFF_K:OFF_K + LANES] = (dkk[w:, :] + kcarry[...]).astype(grp_ref.dtype)
        grp_ref[:, OFF_V:OFF_V + LANES] = (dvv[w:, :] + vcarry[...]).astype(grp_ref.dtype)
        grp_ref[:, OFF_DT:OFF_DT + LANES] = ddt_ref[...].astype(grp_ref.dtype)
        grp_ref[:, OFF_DT + LANES:] = jnp.zeros((w, ATTN_GROUP - OFF_DT - LANES), grp_ref.dtype)
        kcarry[...] = dkk[:w, :]
        vcarry[...] = dvv[:w, :]
        sacc[...] += dsk

        @pl.when((b == nbatch - 1) & (i == nb - 1))
        def _():
            dsk_ref[...] = sacc[...]

    return pl.pallas_call(
        body, name=name,
        out_shape=(jax.ShapeDtypeStruct(dproj.shape, dproj.dtype), jax.ShapeDtypeStruct((1, LANES), F32)),
        grid=(nbatch, nb),
        in_specs=[pl.BlockSpec((w, 512), lambda b, i: (row(b, i), YCAT_ATTN // 512)),
                  q_s, kc_s, kp_s, vc_s, vp_s, z_s,
                  pl.BlockSpec((w, 512), lambda b, i: (row(b, i), 0)),
                  pl.BlockSpec((w, LANES), lambda b, i: (row(b, i), 0)),
                  pl.BlockSpec((1, LANES), lambda b, i: (0, 0)),
                  pl.BlockSpec((w, LANES), lambda b, i: (row(b, i), 0)), ANY],
        out_specs=(pl.BlockSpec((w, ATTN_GROUP), lambda b, i: (row(b, i), 0)),
                   pl.BlockSpec((1, LANES), lambda b, i: (0, 0))),
        input_output_aliases={11: 0},
        scratch_shapes=[pltpu.VMEM((w, LANES), F32), pltpu.VMEM((w, LANES), F32),
                        pltpu.VMEM((1, LANES), F32)],
        compiler_params=_params(("arbitrary", "arbitrary")),
    )(dycat, proj, proj, proj, proj, proj, proj, o, lse, sinks, ddt, dproj)


SSD_WIDTH = SSD_HEADS * SSD_HEAD_DIM
GROUP_ROWS = SSD_WIDTH // 2


def _expand_mat():
    r, c = _iota((LANES, SSD_WIDTH), 0), _iota((LANES, SSD_WIDTH), 1)
    return (r == lax.shift_right_logical(c, 6)).astype(BF16)


def _expand_mat_t():
    r, c = _iota((SSD_WIDTH, LANES), 0), _iota((SSD_WIDTH, LANES), 1)
    return (c == lax.shift_right_logical(r, 6)).astype(BF16)


def _ssd_common(u_ref, dt_ref, dtb_ref, a_ref):
    q = CHUNK
    act = _silu(u_ref[...])
    xs = act[:, :SSD_WIDTH]
    bm = act[:, SSD_WIDTH:SSD_WIDTH + 256]
    cm = act[:, SSD_WIDTH + 256:]
    dtp = _softplus(dt_ref[...] + dtb_ref[...])
    a = dtp * a_ref[...]
    tril = (_iota((q, q), 0) >= _iota((q, q), 1)).astype(BF16)
    acs = _xdot_r(tril, a)
    acs_t = acs.T
    e = _expand_mat()
    dt_x = _xdot(dtp, e)
    ea = jnp.exp(_xdot(acs, e))
    a_end = jnp.sum(jnp.where(_iota(acs.shape, 0) == q - 1, acs, 0.0), axis=0, keepdims=True)
    dec = jnp.exp(_xdot(a_end - acs, e))
    a_end_col = jnp.broadcast_to(_lane_pick(acs_t, q - 1), (LANES, LANES))
    s_scale = jnp.exp(_xdot_r(_expand_mat_t(), a_end_col))
    return act, xs, bm, cm, dtp, acs, acs_t, dt_x, ea, dec, s_scale, tril


def _decay_mat(acs, acs_t, h):
    q = CHUNK
    col = _lane_pick(acs, h)
    rowv = jnp.sum(jnp.where(_iota(acs_t.shape, 0) == h, acs_t, 0.0), axis=0, keepdims=True)
    causal = _iota((q, q), 0) >= _iota((q, q), 1)
    return jnp.exp(jnp.where(causal, col - rowv, -1e30))


GN_WIDTH = 512


def _ssd_fwd(u, proj, dtb, a_neg, d_x, norm_w, ycat, nbatch, name):
    t = u.shape[0]
    q = CHUNK
    nc = t // nbatch // q

    def body(u_ref, dt_ref, z_ref, dtb_ref, a_ref, dx_ref, nw_ref, _, y_ref, st_ref, yn_ref, state):
        c = pl.program_id(1)

        @pl.when(c == 0)
        def _():
            state[...] = jnp.zeros_like(state)

        st_ref[...] = state[...]
        act, xs, bm, cm, dtp, acs, acs_t, dt_x, ea, dec, s_scale, _ = _ssd_common(u_ref, dt_ref, dtb_ref, a_ref)
        xdt = xs * dt_x
        xdec = xdt * dec
        lo, hi = _half_mask(0), _half_mask(1)
        for g in range(2):
            bg = bm[:, g * LANES:(g + 1) * LANES]
            cg = cm[:, g * LANES:(g + 1) * LANES]
            rows = slice(g * GROUP_ROWS, (g + 1) * GROUP_ROWS)
            sg = state[rows, :]
            cb = _dot(cg, bg, NT)
            yoff = _dot(cg, sg, NT)
            for j in range(4):
                pj = g * 4 + j
                cols = slice(pj * LANES, (pj + 1) * LANES)
                xp = xdt[:, cols]
                m0 = cb * _decay_mat(acs, acs_t, 2 * pj)
                m1 = cb * _decay_mat(acs, acs_t, 2 * pj + 1)
                yp = _dot(m0, xp * lo) + _dot(m1, xp * hi)
                yp = yp + yoff[:, j * LANES:(j + 1) * LANES] * ea[:, cols]
                y_ref[:, cols] = yp + dx_ref[:, cols] * xs[:, cols]
            state[rows, :] = s_scale[rows, :] * sg + _dot(xdec[:, rows], bg, TN)
        for g in range(SSD_WIDTH // GN_WIDTH):
            cols = slice(g * GN_WIDTH, (g + 1) * GN_WIDTH)
            gg = y_ref[:, cols] * _silu(z_ref[:, cols])
            rstd = lax.rsqrt(jnp.mean(gg * gg, axis=-1, keepdims=True) + EPS)
            yn_ref[:, cols] = (gg * rstd * nw_ref[:, cols]).astype(yn_ref.dtype)

    vec = pl.BlockSpec((1, LANES), lambda b, c: (0, 0))
    wide = pl.BlockSpec((q, SSD_WIDTH), lambda b, c: (b * nc + c, 0))
    wvec = pl.BlockSpec((1, SSD_WIDTH), lambda b, c: (0, 0))
    return pl.pallas_call(
        body, name=name,
        out_shape=(jax.ShapeDtypeStruct((t, SSD_WIDTH), F32),
                   jax.ShapeDtypeStruct((nbatch * nc * SSD_WIDTH, SSD_STATE), F32),
                   jax.ShapeDtypeStruct(ycat.shape, ycat.dtype)),
        grid=(nbatch, nc),
        in_specs=[pl.BlockSpec((q, SSD_CONV_DIM), lambda b, c: (b * nc + c, 0)),
                  pl.BlockSpec((q, LANES), lambda b, c: (b * nc + c, OFF_DT // LANES)),
                  pl.BlockSpec((q, SSD_WIDTH), lambda b, c: (b * nc + c, OFF_ZS // SSD_WIDTH)),
                  vec, vec, wvec, wvec, ANY],
        out_specs=(wide, pl.BlockSpec((SSD_WIDTH, SSD_STATE), lambda b, c: (b * nc + c, 0)), wide),
        input_output_aliases={7: 2},
        scratch_shapes=[pltpu.VMEM((SSD_WIDTH, SSD_STATE), F32)],
        compiler_params=_params(("parallel", "arbitrary")),
    )(u, proj, proj, dtb, a_neg, d_x, norm_w, ycat)


def _ssd_bwd(dycat, u, proj, y, states, dtb, a_neg, d_x, norm_w, dproj, nbatch, name):
    t = u.shape[0]
    q = CHUNK
    nc = t // nbatch // q

    def body(do_ref, u_ref, dt_ref, z_ref, y_ref, st_ref, dtb_ref, a_ref, dx_ref, nw_ref, _,
             du_ref, dz_ref, ddt_ref, dal_ref, dd_ref, dtbg_ref, dnw_ref, dstate, acc_a, acc_d, acc_b, acc_w):
        b, c = pl.program_id(0), pl.program_id(1)

        @pl.when((b == 0) & (c == 0))
        def _():
            acc_a[...] = jnp.zeros_like(acc_a)
            acc_d[...] = jnp.zeros_like(acc_d)
            acc_b[...] = jnp.zeros_like(acc_b)
            acc_w[...] = jnp.zeros_like(acc_w)

        @pl.when(c == 0)
        def _():
            dstate[...] = jnp.zeros_like(dstate)

        dy_parts = []
        for g in range(SSD_WIDTH // GN_WIDTH):
            cols = slice(g * GN_WIDTH, (g + 1) * GN_WIDTH)
            yv, zv, dov = y_ref[:, cols], z_ref[:, cols], do_ref[:, cols]
            sz = _silu(zv)
            gg = yv * sz
            rstd = lax.rsqrt(jnp.mean(gg * gg, axis=-1, keepdims=True) + EPS)
            gh = gg * rstd
            acc_w[:, cols] += _rowsum8(dov * gh)
            dgn = dov * nw_ref[:, cols]
            dg = rstd * (dgn - gh * jnp.mean(dgn * gh, axis=-1, keepdims=True))
            dy_parts.append(dg * sz)
            dz_ref[:, cols] = (dg * yv * _dsilu(zv)).astype(dz_ref.dtype)

        act, xs, bm, cm, dtp, acs, acs_t, dt_x, ea, dec, s_scale, tril = _ssd_common(
            u_ref, dt_ref, dtb_ref, a_ref)
        xdt = xs * dt_x
        xdec = xdt * dec
        dyv = jnp.concatenate(dy_parts, axis=1)
        dye = dyv * ea
        lo, hi = _half_mask(0), _half_mask(1)
        et = _expand_mat_t()
        dxdt_parts, db_parts, dc_parts, dxst_parts, yoff_parts = [], [], [], [], []
        end_sum = jnp.zeros((LANES, LANES), F32)
        dal_diag = jnp.zeros((q, LANES), F32)
        lane_q = _iota((q, LANES), 1)
        for g in range(2):
            bg = bm[:, g * LANES:(g + 1) * LANES]
            cg = cm[:, g * LANES:(g + 1) * LANES]
            rows = slice(g * GROUP_ROWS, (g + 1) * GROUP_ROWS)
            sg = st_ref[rows, :]
            dsg = dstate[rows, :]
            cb = _dot(cg, bg, NT)
            yoff_parts.append(_dot(cg, sg, NT))
            dcb = jnp.zeros((q, q), F32)
            parts = []
            for j in range(4):
                pj = g * 4 + j
                cols = slice(pj * LANES, (pj + 1) * LANES)
                xp = xdt[:, cols]
                dy0, dy1 = dyv[:, cols] * lo, dyv[:, cols] * hi
                l0 = _decay_mat(acs, acs_t, 2 * pj)
                l1 = _decay_mat(acs, acs_t, 2 * pj + 1)
                g0, g1 = _dot(dy0, xp, NT), _dot(dy1, xp, NT)
                m0, m1 = cb * l0, cb * l1
                dcb = dcb + g0 * l0 + g1 * l1
                parts.append(_dot(m0, dy0, TN) + _dot(m1, dy1, TN))
                for hh, wmat in enumerate((g0 * m0, g1 * m1)):
                    sel = (lane_q == 2 * pj + hh).astype(F32)
                    dal_diag = dal_diag + _dot(wmat, sel) - _dot(wmat, sel, TN)
            dxst = _dot(bg, dsg, NT) * dec[:, rows]
            dxst_parts.append(dxst)
            dxdt_parts.append(jnp.concatenate(parts, axis=1) + dxst)
            dc_parts.append(_dot(dcb, bg) + _dot(dye[:, rows], sg))
            db_parts.append(_dot(dcb, cg, TN) + _dot(xdec[:, rows], dsg))
            s_next = s_scale[rows, :] * sg + _dot(xdec[:, rows], bg, TN)
            end_sum = end_sum + _xdot(dsg * s_next, et[rows, :], TN, passes=2)
            dstate[rows, :] = _dot(dye[:, rows], cg, TN) + s_scale[rows, :] * dsg
        dxdt = jnp.concatenate(dxdt_parts, axis=1)
        dxv = dx_ref[...]
        yoff = jnp.concatenate(yoff_parts, axis=1) * ea
        dalpha = dal_diag + _xdot(dyv * yoff - xdt * jnp.concatenate(dxst_parts, axis=1), et)
        end_row = jnp.sum(end_sum, axis=0, keepdims=True)
        dalpha = dalpha + jnp.where(_iota((q, LANES), 0) == q - 1, end_row, 0.0)
        da = _xdot_r(tril, dalpha, TN)
        ddtp = da * a_ref[...] + _xdot(dxdt * xs, et)
        acc_a[...] += _rowsum8(da * dtp)
        acc_d[...] += _rowsum8(_xdot(dyv * xs, et))
        ddt_raw = ddtp * _sigmoid(dt_ref[...] + dtb_ref[...])
        acc_b[...] += _rowsum8(ddt_raw)
        ddt_ref[...] = ddt_raw
        dxs = dxdt * dt_x + dxv * dyv
        dact = jnp.concatenate([dxs] + db_parts + dc_parts, axis=1)
        du_ref[...] = dact * _dsilu(u_ref[...])

        @pl.when((b == nbatch - 1) & (c == nc - 1))
        def _():
            dal_ref[...] = jnp.sum(acc_a[...], axis=0, keepdims=True) * a_ref[...]
            dd_ref[...] = jnp.sum(acc_d[...], axis=0, keepdims=True)
            dtbg_ref[...] = jnp.sum(acc_b[...], axis=0, keepdims=True)
            dnw_ref[...] = jnp.sum(acc_w[...], axis=0, keepdims=True)

    def rowblk(b, c):
        return b * nc + (nc - 1 - c)

    vec = pl.BlockSpec((1, LANES), lambda b, c: (0, 0))
    wvec = pl.BlockSpec((1, SSD_WIDTH), lambda b, c: (0, 0))
    wide = pl.BlockSpec((q, SSD_WIDTH), lambda b, c: (rowblk(b, c), 0))
    zblk = pl.BlockSpec((q, SSD_WIDTH), lambda b, c: (rowblk(b, c), OFF_ZS // SSD_WIDTH))
    return pl.pallas_call(
        body, name=name,
        out_shape=(jax.ShapeDtypeStruct((t, SSD_CONV_DIM), F32), jax.ShapeDtypeStruct(dproj.shape, dproj.dtype),
                   jax.ShapeDtypeStruct((t, LANES), F32),
                   jax.ShapeDtypeStruct((1, LANES), F32), jax.ShapeDtypeStruct((1, LANES), F32),
                   jax.ShapeDtypeStruct((1, LANES), F32), jax.ShapeDtypeStruct((1, SSD_WIDTH), F32)),
        grid=(nbatch, nc),
        in_specs=[wide,
                  pl.BlockSpec((q, SSD_CONV_DIM), lambda b, c: (rowblk(b, c), 0)),
                  pl.BlockSpec((q, LANES), lambda b, c: (rowblk(b, c), OFF_DT // LANES)),
                  zblk, wide,
                  pl.BlockSpec((SSD_WIDTH, SSD_STATE), lambda b, c: (rowblk(b, c), 0)),
                  vec, vec, wvec, wvec, ANY],
        out_specs=(pl.BlockSpec((q, SSD_CONV_DIM), lambda b, c: (rowblk(b, c), 0)),
                   zblk,
                   pl.BlockSpec((q, LANES), lambda b, c: (rowblk(b, c), 0)),
                   vec, vec, vec, wvec),
        input_output_aliases={10: 1},
        scratch_shapes=[pltpu.VMEM((SSD_WIDTH, SSD_STATE), F32), pltpu.VMEM((SUBLANES, LANES), F32),
                        pltpu.VMEM((SUBLANES, LANES), F32), pltpu.VMEM((SUBLANES, LANES), F32),
                        pltpu.VMEM((SUBLANES, SSD_WIDTH), F32)],
        compiler_params=_params(("arbitrary", "arbitrary")),
    )(dycat, u, proj, proj, y, states, dtb, a_neg, d_x, norm_w, dproj)


def _pad_rows(w, rows):
    return jnp.concatenate([w, jnp.zeros((rows - w.shape[0], w.shape[1]), w.dtype)], axis=0)


def _pad_lanes(v):
    return jnp.concatenate([v, jnp.zeros((LANES - v.shape[0],), v.dtype)]).reshape(1, LANES)


def _padded_from_chips(pieces):
    cols = pieces[0].shape[-1]
    lead = pieces[0].shape[:-1]
    parts, pos = [], 0
    for lo, hi, start in sorted(SECTIONS, key=lambda s: s[2]):
        if start > pos:
            parts.append(jnp.zeros(lead + (start - pos,), pieces[0].dtype))
        pos = start + hi - lo
        while lo < hi:
            p = lo // cols
            end = min(hi, (p + 1) * cols)
            parts.append(pieces[p][..., lo - p * cols:end - p * cols])
            lo = end
    if pos < NP:
        parts.append(jnp.zeros(lead + (NP - pos,), pieces[0].dtype))
    return jnp.concatenate(parts, axis=-1)


def _chip_part_from_padded(wp, p, cols):
    lo, hi = p * cols, (p + 1) * cols
    parts = []
    for rs, re, start in SECTIONS:
        a, b = max(lo, rs), min(hi, re)
        if a < b:
            parts.append(wp[..., start + a - rs:start + b - rs])
    return jnp.concatenate(parts, axis=-1)


def _layer_params(li, w_in_p, w_out, conv_w, dw_w, small):
    return dict(
        w_in_p=w_in_p, w_out=w_out,
        conv_w=_pad_rows(conv_w, SUBLANES), dw_w=_pad_rows(dw_w, 32),
        norm_w=small["norm_w"][li].reshape(1, -1),
        conv_b=small["ssd_conv_b"][li].reshape(1, -1),
        dtb=_pad_lanes(small["ssd_dt_bias"][li]),
        a_neg=_pad_lanes(-jnp.exp(small["ssd_a_log"][li])),
        d_x=jnp.repeat(small["ssd_d"][li], SSD_HEAD_DIM).reshape(1, -1),
        ssd_norm_w=small["ssd_norm_w"][li].reshape(1, -1),
        sinks=_pad_lanes(small["attn_sinks"][li]),
        dw_b=small["conf_dw_b"][li].reshape(1, -1),
        ln_w=small["conf_ln_w"][li].reshape(1, -1),
        ln_b=small["conf_ln_b"][li].reshape(1, -1),
    )


def _layer_fwd(x, p, nbatch, seq, tag, after=None):
    h, h_t = _rmsnorm_fwd(x, p["norm_w"], name=f"rmsnorm_fwd_{tag}", after=after)
    proj = _matmul(h, p["w_in_p"], "nn", F32, 1024, 512, 1024, name=f"proj_fwd_{tag}")
    u = _conv_fwd(proj, OFF_XBC, SSD_CONV_DIM, p["conv_w"], p["conv_b"], SSD_CONV, seq, name=f"ssd_conv_fwd_{tag}")
    ycat = lax.empty((x.shape[0], MIX_WIDTH), MXU_DTYPE)
    y, states, ycat = _ssd_fwd(u, proj, p["dtb"], p["a_neg"], p["d_x"], p["ssd_norm_w"], ycat, nbatch,
                               name=f"ssd_fwd_{tag}")
    ycat, o, lse = _attn_fwd(proj, p["sinks"], ycat, nbatch, name=f"attn_fwd_{tag}")
    c0 = _glu_fwd(proj, name=f"glu_fwd_{tag}")
    c1 = _conv_fwd(c0, 0, CONF_WIDTH, p["dw_w"], p["dw_b"], CONF_KERNEL, seq, name=f"conf_conv_fwd_{tag}")
    ycat = _conf_post_fwd(c1, proj, p["ln_w"], p["ln_b"], ycat, name=f"conf_post_fwd_{tag}")
    x_new = _matmul(ycat, p["w_out"], "nn", F32, 1024, 512, 2048, name=f"out_fwd_{tag}", residual=x)
    return x_new, dict(x=x, h_t=h_t, proj=proj, u=u, y=y, states=states, o=o, lse=lse, c0=c0, c1=c1, ycat=ycat)


def _layer_bwd(dx_out, p, s, nbatch, seq, tag):
    proj = s["proj"]
    dycat = _matmul(dx_out, p["w_out"], "nt", F32, 1024, 1024, 1024, name=f"out_bwd_dy_{tag}")
    dw_out = _matmul(s["ycat"], dx_out, "tn", F32, 1024, 1024, 1024, name=f"out_bwd_dw_{tag}")
    dproj = lax.empty(proj.shape, MXU_DTYPE)
    du, dproj, ddt, da_log, dd, ddtb, dssd_norm_w = _ssd_bwd(
        dycat, s["u"], proj, s["y"], s["states"], p["dtb"], p["a_neg"], p["d_x"], p["ssd_norm_w"], dproj,
        nbatch, name=f"ssd_bwd_{tag}")
    dproj, dconv_w, dconv_b = _conv_bwd(du, proj, OFF_XBC, SSD_CONV_DIM, p["conv_w"], SSD_CONV, seq,
                                        name=f"ssd_conv_bwd_{tag}", into=dproj)
    dproj, dsinks = _attn_bwd(dycat, proj, s["o"], s["lse"], p["sinks"], ddt, dproj, nbatch,
                              name=f"attn_bwd_{tag}")
    dc1, dproj, dln_w, dln_b = _conf_post_bwd(dycat, s["c1"], proj, p["ln_w"], p["ln_b"], dproj,
                                              name=f"conf_post_bwd_{tag}")
    dc0, ddw_w, ddw_b = _conv_bwd(dc1, s["c0"], 0, CONF_WIDTH, p["dw_w"], CONF_KERNEL, seq,
                                  name=f"conf_conv_bwd_{tag}")
    dproj = _glu_bwd(dc0, proj, dproj, name=f"glu_bwd_{tag}")
    dh = _matmul(dproj, p["w_in_p"], "nt", F32, 1024, 1024, 1408, name=f"proj_bwd_dh_{tag}")
    dw_in_p = _matmul(s["h_t"], dproj, "nn", F32, 1024, 512, 4096, name=f"proj_bwd_dw_{tag}")
    dx_in, dnorm_w = _rmsnorm_bwd(dh, s["x"], p["norm_w"], dx_out, name=f"rmsnorm_bwd_{tag}")
    grads = dict(
        norm_w=dnorm_w[0], w_in_p=dw_in_p, ssd_conv_w=dconv_w[:SSD_CONV], ssd_conv_b=dconv_b[0],
        ssd_dt_bias=ddtb[0, :SSD_HEADS], ssd_a_log=da_log[0, :SSD_HEADS], ssd_d=dd[0, :SSD_HEADS],
        ssd_norm_w=dssd_norm_w[0], attn_sinks=dsinks[0, :ATTN_Q_HEADS], conf_dw_w=ddw_w[:CONF_KERNEL],
        conf_dw_b=ddw_b[0], conf_ln_w=dln_w[0], conf_ln_b=dln_b[0], w_out=dw_out)
    return dx_in, grads


def _local_step(x, target, param_fns, final_norm_w, first_after=None):
    nbatch, seq, d = x.shape
    xt = x.reshape(nbatch * seq, d)
    saved, layer_params = [], []
    for li, fn in enumerate(param_fns):
        p = fn(xt)
        layer_params.append(p)
        xt, s = _layer_fwd(xt, p, nbatch, seq, f"l{li}", after=first_after if li == 0 else None)
        saved.append(s)
    loss, dx, dfinal = _loss_head(xt, target.reshape(nbatch * seq, d), final_norm_w.reshape(1, d), name="loss_head")
    grads = [None] * len(layer_params)
    for li in reversed(range(len(layer_params))):
        dx, grads[li] = _layer_bwd(dx, layer_params[li], saved[li], nbatch, seq, f"l{li}")
    return loss[0, 0], dx.reshape(nbatch, seq, d), grads, dfinal[0]


MESH = pl.DeviceIdType.MESH
N_CHIPS = 4


def _mesh_pos():
    return lax.axis_index("x"), lax.axis_index("y"), lax.axis_index("c")


def _other_chips(x, y):
    return [(1 - x, y), (x, 1 - y), (1 - x, 1 - y)]


def _gather_weights(big, small, name):
    nbig, nsmall = len(big), len(small)
    n_ici = 3 * (nbig + nsmall)
    n_fwd = 3 * nbig

    def body(*refs):
        ins = refs[:nbig + nsmall]
        outs = refs[nbig + nsmall:2 * (nbig + nsmall)]
        send_sems, recv_sems = refs[2 * (nbig + nsmall):]
        x, y, c = _mesh_pos()
        me = 2 * x + y
        sibling = (x, y, 1 - c)
        chips = _other_chips(x, y)

        def ici(a, j, origin, dest):
            if a < nbig:
                src = ins[a].at[c] if origin is None else outs[a].at[origin, c]
                dst = outs[a].at[me if origin is None else origin, c]
            else:
                src = ins[a] if origin is None else outs[a].at[origin]
                dst = outs[a].at[me if origin is None else origin]
            k = a * 3 + j
            return pltpu.make_async_remote_copy(src_ref=src, dst_ref=dst, send_sem=send_sems.at[k],
                                                recv_sem=recv_sems.at[k], device_id=dest, device_id_type=MESH)

        def fwd(a, j, origin, half):
            k = n_ici + a * 3 + j
            ref = outs[a].at[origin, half]
            return pltpu.make_async_remote_copy(src_ref=ref, dst_ref=ref, send_sem=send_sems.at[k],
                                                recv_sem=recv_sems.at[k], device_id=sibling, device_id_type=MESH)

        sends = []
        for j, (px, py) in enumerate(chips):
            for a in range(nbig + nsmall):
                cp = ici(a, j, None, (px, py, c))
                cp.start()
                sends.append(cp)
        for j, (px, py) in enumerate(chips):
            origin = 2 * px + py
            for a in range(nbig):
                ici(a, j, origin, (px, py, c)).wait_recv()
                cp = fwd(a, j, origin, c)
                cp.start()
                sends.append(cp)
        for j, (px, py) in enumerate(chips):
            origin = 2 * px + py
            for a in range(nbig, nbig + nsmall):
                ici(a, j, origin, (px, py, c)).wait_recv()
            for a in range(nbig):
                fwd(a, j, origin, 1 - c).wait_recv()
        for cp in sends:
            cp.wait_send()

    out_shape = tuple(jax.ShapeDtypeStruct((N_CHIPS,) + a.shape, a.dtype) for a in list(big) + list(small))
    return pl.pallas_call(
        body, name=name, out_shape=out_shape,
        in_specs=[ANY] * (nbig + nsmall), out_specs=tuple([ANY] * (nbig + nsmall)),
        scratch_shapes=[pltpu.SemaphoreType.DMA((n_ici + n_fwd,)), pltpu.SemaphoreType.DMA((n_ici + n_fwd,))],
    )(*big, *small)


def _pair_swap_halves(arrs, name):
    n = len(arrs)

    def body(*refs):
        ins, outs = refs[:n], refs[n:2 * n]
        send_sems, recv_sems = refs[2 * n:]
        x, y, c = _mesh_pos()
        cps = [pltpu.make_async_remote_copy(src_ref=ins[a].at[1 - c], dst_ref=outs[a], send_sem=send_sems.at[a],
                                            recv_sem=recv_sems.at[a], device_id=(x, y, 1 - c), device_id_type=MESH)
               for a in range(n)]
        for cp in cps:
            cp.start()
        for cp in cps:
            cp.wait()

    return pl.pallas_call(
        body, name=name, out_shape=tuple(jax.ShapeDtypeStruct(a.shape[1:], a.dtype) for a in arrs),
        in_specs=[ANY] * n, out_specs=tuple([ANY] * n),
        scratch_shapes=[pltpu.SemaphoreType.DMA((n,)), pltpu.SemaphoreType.DMA((n,))],
    )(*arrs)


def _chip_scatter(arrs, name):
    n = len(arrs)

    def body(*refs):
        ins, outs = refs[:n], refs[n:2 * n]
        send_sems, recv_sems = refs[2 * n:]
        x, y, c = _mesh_pos()
        me = 2 * x + y
        cps = []
        for j, (px, py) in enumerate(_other_chips(x, y)):
            for a in range(n):
                cps.append(pltpu.make_async_remote_copy(
                    src_ref=ins[a].at[2 * px + py], dst_ref=outs[a].at[me], send_sem=send_sems.at[a * 3 + j],
                    recv_sem=recv_sems.at[a * 3 + j], device_id=(px, py, c), device_id_type=MESH))
        for cp in cps:
            cp.start()
        for cp in cps:
            cp.wait()

    return pl.pallas_call(
        body, name=name, out_shape=tuple(jax.ShapeDtypeStruct(a.shape, a.dtype) for a in arrs),
        in_specs=[ANY] * n, out_specs=tuple([ANY] * n),
        scratch_shapes=[pltpu.SemaphoreType.DMA((3 * n,)), pltpu.SemaphoreType.DMA((3 * n,))],
    )(*arrs)


HBM = pl.BlockSpec(memory_space=pltpu.HBM)
SEM = pl.BlockSpec(memory_space=pltpu.SEMAPHORE)
DATAFLOW = pltpu.SideEffectType.DATAFLOW_SIDE_EFFECTING


def _shard_bcast_start(arrs, after, name):
    n = len(arrs)
    nsem = 3 * n

    def body(*refs):
        srcs, lands = refs[:n], refs[n:2 * n]
        outs = refs[2 * n + 1:]
        send_sems, recv_sems = outs[:nsem], outs[nsem:2 * nsem]
        token = outs[-1]
        x, y, c = _mesh_pos()
        me = 2 * x + y
        for j, (px, py) in enumerate(_other_chips(x, y)):
            for a in range(n):
                pltpu.make_async_remote_copy(
                    src_ref=srcs[a], dst_ref=lands[a].at[me], send_sem=send_sems[a * 3 + j],
                    recv_sem=recv_sems[a * 3 + j], device_id=(px, py, c), device_id_type=MESH).start()
        token[...] = jnp.zeros_like(token)

    lands = [lax.empty((N_CHIPS,) + a.shape, a.dtype) for a in arrs]
    out_shape = ([pltpu.SemaphoreType.DMA(())] * (2 * nsem)
                 + [pltpu.HBM(a.shape, a.dtype) for a in arrs] + [pltpu.HBM(b.shape, b.dtype) for b in lands]
                 + [jax.ShapeDtypeStruct((SUBLANES, LANES), F32)])
    outs = pl.pallas_call(
        body, name=name, out_shape=tuple(out_shape),
        in_specs=[HBM] * (2 * n) + [ANY],
        out_specs=tuple([SEM] * (2 * nsem) + [HBM] * (2 * n) + [pl.BlockSpec(memory_space=pltpu.VMEM)]),
        input_output_aliases={a: 2 * nsem + a for a in range(2 * n)},
        compiler_params=pltpu.CompilerParams(has_side_effects=DATAFLOW),
    )(*[pltpu.with_memory_space_constraint(a, pltpu.HBM) for a in list(arrs) + lands], after)
    return outs[:-1], outs[-1]


def _shard_bcast_wait(state, n, after, name):
    nsem = 3 * n

    def body(*refs):
        srcs, lands = refs[:n], refs[n:2 * n]
        send_sems, recv_sems = refs[2 * n:2 * n + nsem], refs[2 * n + nsem:2 * n + 2 * nsem]
        x, y, c = _mesh_pos()
        for j, (px, py) in enumerate(_other_chips(x, y)):
            for a in range(n):
                cp = pltpu.make_async_remote_copy(
                    src_ref=srcs[a], dst_ref=lands[a].at[2 * px + py], send_sem=send_sems[a * 3 + j],
                    recv_sem=recv_sems[a * 3 + j], device_id=(px, py, c), device_id_type=MESH)
                cp.wait_send()
                cp.wait_recv()

    sems, thru = state[:2 * nsem], state[2 * nsem:]
    outs = pl.pallas_call(
        body, name=name, out_shape=tuple(pltpu.HBM(a.shape, a.dtype) for a in thru),
        in_specs=[HBM] * (2 * n) + [SEM] * (2 * nsem) + [ANY],
        out_specs=tuple([HBM] * (2 * n)),
        input_output_aliases={a: a for a in range(2 * n)},
        compiler_params=pltpu.CompilerParams(has_side_effects=DATAFLOW),
    )(*thru, *sems, after)
    return outs[n:]


def _pair_gather(arrs, name):
    n = len(arrs)

    def body(*refs):
        outs = refs[n:2 * n]
        send_sems, recv_sems = refs[2 * n:]
        x, y, c = _mesh_pos()
        cps = [pltpu.make_async_remote_copy(src_ref=outs[a].at[c], dst_ref=outs[a].at[c], send_sem=send_sems.at[a],
                                            recv_sem=recv_sems.at[a], device_id=(x, y, 1 - c), device_id_type=MESH)
               for a in range(n)]
        for cp in cps:
            cp.start()
        for cp in cps:
            cp.wait()

    return pl.pallas_call(
        body, name=name, out_shape=tuple(jax.ShapeDtypeStruct(a.shape, a.dtype) for a in arrs),
        in_specs=[ANY] * n, out_specs=tuple([ANY] * n),
        input_output_aliases={a: a for a in range(n)},
        scratch_shapes=[pltpu.SemaphoreType.DMA((n,)), pltpu.SemaphoreType.DMA((n,))],
    )(*arrs)


N_DEV = 8


def _allreduce_small(pack, name):
    r = pack.shape[0]

    def body(p_ref, o_ref, land, send_sems, recv_sems):
        x, y, c = _mesh_pos()
        me = 4 * x + 2 * y + c
        cps = []
        for k in range(1, N_DEV):
            peer = (x ^ (k >> 2), y ^ ((k >> 1) & 1), c ^ (k & 1))
            cps.append(pltpu.make_async_remote_copy(src_ref=p_ref, dst_ref=land.at[me], send_sem=send_sems.at[k - 1],
                                                    recv_sem=recv_sems.at[k - 1], device_id=peer, device_id_type=MESH))
        for cp in cps:
            cp.start()
        land[me] = p_ref[...]
        for cp in cps:
            cp.wait()
        total = land[0]
        for d in range(1, N_DEV):
            total = total + land[d]
        o_ref[...] = total

    vm = pl.BlockSpec(memory_space=pltpu.VMEM)
    return pl.pallas_call(
        body, name=name, out_shape=jax.ShapeDtypeStruct(pack.shape, F32),
        in_specs=[vm], out_specs=vm,
        scratch_shapes=[pltpu.VMEM((N_DEV, r, LANES), F32), pltpu.SemaphoreType.DMA((N_DEV - 1,)),
                        pltpu.SemaphoreType.DMA((N_DEV - 1,))],
    )(pack)


BIG_ROWS = 128


def _cast_layer(w, layer, name):
    _, r, cdim = w.shape
    tr = BIG_ROWS

    def body(w_ref, o_ref):
        o_ref[...] = w_ref[...].astype(o_ref.dtype)

    return pl.pallas_call(
        body, name=name, out_shape=jax.ShapeDtypeStruct((r, cdim), MXU_DTYPE),
        grid=(r // tr,), in_specs=[pl.BlockSpec((None, tr, cdim), lambda i: (layer, i, 0))],
        out_specs=pl.BlockSpec((tr, cdim), lambda i: (i, 0)),
        compiler_params=_params(("parallel",)),
    )(w)


def _pair_sum(parts, sib, which, out_dtype, name):
    _, k, r, cdim = parts.shape
    tr = BIG_ROWS

    def body(sel_ref, p_ref, s_ref, o_ref):
        o_ref[...] = (p_ref[...] + s_ref[...]).astype(o_ref.dtype)

    grid_spec = pltpu.PrefetchScalarGridSpec(
        num_scalar_prefetch=1, grid=(k, r // tr),
        in_specs=[pl.BlockSpec((None, None, tr, cdim), lambda l, i, sel: (sel[0], l, i, 0)),
                  pl.BlockSpec((None, tr, cdim), lambda l, i, sel: (l, i, 0))],
        out_specs=pl.BlockSpec((None, tr, cdim), lambda l, i, sel: (l, i, 0)))
    return pl.pallas_call(
        body, name=name, out_shape=jax.ShapeDtypeStruct((k, r, cdim), out_dtype), grid_spec=grid_spec,
        compiler_params=_params(("parallel", "parallel")),
    )(which.reshape(1).astype(jnp.int32), parts, sib)


def _sum_lead(parts, which, name):
    k, r, cdim = parts.shape
    tr = BIG_ROWS

    def body(sel_ref, p_ref, o_ref):
        total = p_ref[0].astype(F32)
        for a in range(1, k):
            total = total + p_ref[a].astype(F32)
        o_ref[...] = total

    grid_spec = pltpu.PrefetchScalarGridSpec(
        num_scalar_prefetch=1, grid=(r // tr,),
        in_specs=[pl.BlockSpec((k, tr, cdim), lambda i, sel: (0, i, 0))],
        out_specs=pl.BlockSpec((None, tr, cdim), lambda i, sel: (sel[0], i, 0)))
    return pl.pallas_call(
        body, name=name, out_shape=jax.ShapeDtypeStruct((2, r, cdim), F32), grid_spec=grid_spec,
        compiler_params=_params(("parallel",)),
    )(which.reshape(1).astype(jnp.int32), parts)


def _adam_math(w, g, m, v):
    m2 = ADAM_B1 * m + (1.0 - ADAM_B1) * g
    v2 = ADAM_B2 * v + (1.0 - ADAM_B2) * (g * g)
    m_hat = m2 / (1.0 - ADAM_B1 ** ADAM_STEP)
    v_hat = v2 / (1.0 - ADAM_B2 ** ADAM_STEP)
    delta = -ADAM_LR * (m_hat / (jnp.sqrt(v_hat) + ADAM_EPS) + ADAM_WD * w)
    return delta, m2, v2


def _adam_big(w, g, m, v, name):
    nl, r, cdim = w.shape
    tr = BIG_ROWS

    def body(w_ref, g_ref, m_ref, v_ref, d_ref, mo_ref, vo_ref):
        delta, m2, v2 = _adam_math(w_ref[...], g_ref[...], m_ref[...], v_ref[...])
        d_ref[...] = delta
        mo_ref[...] = m2
        vo_ref[...] = v2

    blk = pl.BlockSpec((None, tr, cdim), lambda l, i: (l, i, 0))
    shp = jax.ShapeDtypeStruct(w.shape, F32)
    return pl.pallas_call(
        body, name=name, out_shape=(shp, shp, shp),
        grid=(nl, r // tr), in_specs=[blk] * 4, out_specs=(blk, blk, blk),
        compiler_params=_params(("parallel", "parallel")),
    )(w, g, m, v)


def _adam_cols_major(w, g, m, v, name):
    cdim, nl, r = w.shape
    tc = BIG_ROWS

    def body(w_ref, g_ref, m_ref, v_ref, d_ref, mo_ref, vo_ref):
        delta, m2, v2 = _adam_math(w_ref[...], g_ref[...], m_ref[...], v_ref[...])
        d_ref[...] = delta
        mo_ref[...] = m2
        vo_ref[...] = v2

    blk = pl.BlockSpec((tc, nl, r), lambda i: (i, 0, 0))
    shp = jax.ShapeDtypeStruct(w.shape, F32)
    return pl.pallas_call(
        body, name=name, out_shape=(shp, shp, shp),
        grid=(pl.cdiv(cdim, tc),), in_specs=[blk] * 4, out_specs=(blk, blk, blk),
        compiler_params=_params(("parallel",)),
    )(w, g, m, v)


def _adam_small(ws, gs, ms, vs, name):
    n = len(ws)

    def body(*refs):
        w_refs, g_refs, m_refs, v_refs = (refs[k * n:(k + 1) * n] for k in range(4))
        d_refs, mo_refs, vo_refs = (refs[(4 + k) * n:(5 + k) * n] for k in range(3))
        for a in range(n):
            delta, m2, v2 = _adam_math(w_refs[a][...], g_refs[a][...], m_refs[a][...], v_refs[a][...])
            d_refs[a][...] = delta
            mo_refs[a][...] = m2
            vo_refs[a][...] = v2

    shapes = tuple(jax.ShapeDtypeStruct(w.shape, F32) for w in ws)
    vm = pl.BlockSpec(memory_space=pltpu.VMEM)
    outs = pl.pallas_call(body, name=name, out_shape=shapes * 3, in_specs=[vm] * (4 * n),
                          out_specs=tuple([vm] * (3 * n)))(*ws, *gs, *ms, *vs)
    return outs[:n], outs[n:2 * n], outs[2 * n:]


PACK_TILE = SUBLANES * LANES


def _pack(arrays):
    rows = []
    for a in arrays:
        flat = a.reshape(-1)
        pad = (-flat.shape[0]) % PACK_TILE
        if pad:
            flat = jnp.concatenate([flat, jnp.zeros((pad,), flat.dtype)])
        rows.append(flat.reshape(-1, LANES))
    return jnp.concatenate(rows, axis=0)


def _unpack(pack, shapes):
    outs, row = [], 0
    for shp in shapes:
        n = int(np.prod(shp))
        nrows = -(-n // PACK_TILE) * SUBLANES
        outs.append(pack[row:row + nrows].reshape(-1)[:n].reshape(shp))
        row += nrows
    return outs


SMALL = ["norm_w", "ssd_conv_b", "ssd_dt_bias", "ssd_a_log", "ssd_d", "ssd_norm_w", "attn_sinks",
         "conf_dw_b", "conf_ln_w", "conf_ln_b"]
WEIGHTS = ["norm_w", "w_in", "ssd_conv_w", "ssd_conv_b", "ssd_dt_bias", "ssd_a_log", "ssd_d", "ssd_norm_w",
           "attn_sinks", "conf_dw_w", "conf_dw_b", "conf_ln_w", "conf_ln_b", "w_out", "final_norm_w"]


def kernel(x, norm_w, w_in, ssd_conv_w, ssd_conv_b, ssd_dt_bias, ssd_a_log, ssd_d, ssd_norm_w, attn_sinks, conf_dw_w, conf_dw_b, conf_ln_w, conf_ln_b, w_out, final_norm_w, loss_target, m_norm_w, m_w_in, m_ssd_conv_w, m_ssd_conv_b, m_ssd_dt_bias, m_ssd_a_log, m_ssd_d, m_ssd_norm_w, m_attn_sinks, m_conf_dw_w, m_conf_dw_b, m_conf_ln_w, m_conf_ln_b, m_w_out, m_final_norm_w, v_norm_w, v_w_in, v_ssd_conv_w, v_ssd_conv_b, v_ssd_dt_bias, v_ssd_a_log, v_ssd_d, v_ssd_norm_w, v_attn_sinks, v_conf_dw_w, v_conf_dw_b, v_conf_ln_w, v_conf_ln_b, v_w_out, v_final_norm_w):
    w = dict(norm_w=norm_w, w_in=w_in, ssd_conv_w=ssd_conv_w, ssd_conv_b=ssd_conv_b, ssd_dt_bias=ssd_dt_bias,
             ssd_a_log=ssd_a_log, ssd_d=ssd_d, ssd_norm_w=ssd_norm_w, attn_sinks=attn_sinks, conf_dw_w=conf_dw_w,
             conf_dw_b=conf_dw_b, conf_ln_w=conf_ln_w, conf_ln_b=conf_ln_b, w_out=w_out, final_norm_w=final_norm_w)
    m = dict(norm_w=m_norm_w, w_in=m_w_in, ssd_conv_w=m_ssd_conv_w, ssd_conv_b=m_ssd_conv_b,
             ssd_dt_bias=m_ssd_dt_bias, ssd_a_log=m_ssd_a_log, ssd_d=m_ssd_d, ssd_norm_w=m_ssd_norm_w,
             attn_sinks=m_attn_sinks, conf_dw_w=m_conf_dw_w, conf_dw_b=m_conf_dw_b, conf_ln_w=m_conf_ln_w,
             conf_ln_b=m_conf_ln_b, w_out=m_w_out, final_norm_w=m_final_norm_w)
    v = dict(norm_w=v_norm_w, w_in=v_w_in, ssd_conv_w=v_ssd_conv_w, ssd_conv_b=v_ssd_conv_b,
             ssd_dt_bias=v_ssd_dt_bias, ssd_a_log=v_ssd_a_log, ssd_d=v_ssd_d, ssd_norm_w=v_ssd_norm_w,
             attn_sinks=v_attn_sinks, conf_dw_w=v_conf_dw_w, conf_dw_b=v_conf_dw_b, conf_ln_w=v_conf_ln_w,
             conf_ln_b=v_conf_ln_b, w_out=v_w_out, final_norm_w=v_final_norm_w)
    depth = w_in.shape[0]
    me = 2 * lax.axis_index("x") + lax.axis_index("y")

    assert depth == 2
    w_in_b = [_cast_layer(w_in, li, name=f"cast_w_in_l{li}") for li in range(depth)]
    w_out_b = [_cast_layer(w_out, li, name=f"cast_w_out_l{li}") for li in range(depth)]
    own0 = [w_in_b[0].reshape((2, -1) + w_in_b[0].shape[1:]), w_out_b[0].reshape((2, -1) + w_out_b[0].shape[1:]),
            ssd_conv_w, conf_dw_w]
    gathered0 = _gather_weights(own0[:2], own0[2:], name="gather_weights_l0")
    g_in0, g_out0, g_conv, g_dw = [lax.dynamic_update_index_in_dim(g_all, mine, me, 0)
                                   for g_all, mine in zip(gathered0, own0)]
    own1 = [w_in_b[1], w_out_b[1]]
    pending1, token1 = _shard_bcast_start(own1, gathered0[0], name="gather_l1_start")

    def small_full(li):
        return (jnp.concatenate([g_conv[p, li] for p in range(N_CHIPS)], axis=1),
                jnp.concatenate([g_dw[p, li] for p in range(N_CHIPS)], axis=1))

    def params_l0(_):
        w_in_p = _padded_from_chips([g_in0[p].reshape(w_in_b[0].shape) for p in range(N_CHIPS)])
        w_out_full = g_out0.reshape(-1, g_out0.shape[-1])
        return _layer_params(0, w_in_p, w_out_full, *small_full(0), w)

    def params_l1(layer_input):
        landed = _shard_bcast_wait(pending1, len(own1), layer_input, name="gather_l1_wait")
        g_in1, g_out1 = [lax.dynamic_update_index_in_dim(g_all, mine, me, 0) for g_all, mine in zip(landed, own1)]
        w_in_p = _padded_from_chips([g_in1[p] for p in range(N_CHIPS)])
        return _layer_params(1, w_in_p, g_out1.reshape(-1, g_out1.shape[-1]), *small_full(1), w)

    loss, grad_x, grads, dfinal = _local_step(x, loss_target, [params_l0, params_l1], final_norm_w,
                                              first_after=token1)

    small_list = [grads[li][n] for li in range(depth) for n in SMALL]
    small_list += [grads[li][n] for li in range(depth) for n in ("ssd_conv_w", "conf_dw_w")]
    small_list += [dfinal, loss.reshape(1)]
    small_shapes = [a.shape for a in small_list]
    reduced = _unpack(_allreduce_small(_pack(small_list), name="allreduce_small"), small_shapes)
    ns = len(SMALL)
    g = {n: jnp.stack([reduced[li * ns + i] for li in range(depth)]) for i, n in enumerate(SMALL)}
    conv_w_cols, dw_w_cols = ssd_conv_w.shape[2], conf_dw_w.shape[2]
    g["ssd_conv_w"] = jnp.stack([lax.dynamic_slice_in_dim(reduced[depth * ns + 2 * li], me * conv_w_cols,
                                                          conv_w_cols, axis=1) for li in range(depth)])
    g["conf_dw_w"] = jnp.stack([lax.dynamic_slice_in_dim(reduced[depth * ns + 2 * li + 1], me * dw_w_cols,
                                                         dw_w_cols, axis=1) for li in range(depth)])
    g["final_norm_w"] = reduced[-2]
    loss_total = reduced[-1][0]

    cols = w_in.shape[2]
    rows_out = w_out.shape[1]
    p_in = jnp.stack([jnp.stack([_chip_part_from_padded(grads[li]["w_in_p"], p, cols) for p in range(N_CHIPS)])
                      for li in range(depth)])
    p_out = jnp.stack([grads[li]["w_out"].reshape(N_CHIPS, rows_out, D_MODEL) for li in range(depth)])
    c = lax.axis_index("c")
    sib_in, sib_out = _pair_swap_halves([p_in, p_out], name="grad_pair_swap")
    s_in = _pair_sum(p_in, sib_in, c, MXU_DTYPE, name="grad_pair_sum_in")
    s_out = _pair_sum(p_out, sib_out, c, MXU_DTYPE, name="grad_pair_sum_out")
    r_in, r_out = _chip_scatter([s_in, s_out], name="grad_chip_scatter")
    r_in = lax.dynamic_update_index_in_dim(r_in, lax.dynamic_index_in_dim(s_in, me, 0, keepdims=False), me, 0)
    r_out = lax.dynamic_update_index_in_dim(r_out, lax.dynamic_index_in_dim(s_out, me, 0, keepdims=False), me, 0)
    t_in = _sum_lead(r_in, c, name="grad_chip_sum_in")
    t_out = _sum_lead(r_out, c, name="grad_chip_sum_out")
    g_w_in, g_w_out = _pair_gather([t_in, t_out], name="grad_pair_gather")

    outs_g, outs_d, outs_m, outs_v = {"w_in": g_w_in, "w_out": g_w_out}, {}, {}, {}
    to_cols, from_cols = (2, 0, 1), (1, 2, 0)
    outs_d["w_in"], outs_m["w_in"], outs_v["w_in"] = [
        jnp.transpose(a, from_cols) for a in _adam_cols_major(
            *[jnp.transpose(a, to_cols) for a in (w_in, g_w_in, m_w_in, v_w_in)], name="adam_w_in")]
    outs_d["w_out"], outs_m["w_out"], outs_v["w_out"] = _adam_big(w_out, g_w_out, m_w_out, v_w_out,
                                                                  name="adam_w_out")
    small_names = [n for n in WEIGHTS if n not in ("w_in", "w_out")]
    def as2d(a):
        return a.reshape(1, -1) if a.ndim == 1 else a

    deltas, new_ms, new_vs = _adam_small(*[[as2d(src[n]) for n in small_names] for src in (w, g, m, v)],
                                         name="adam_small")
    for n, dn, mn, vn in zip(small_names, deltas, new_ms, new_vs):
        outs_g[n], outs_d[n], outs_m[n], outs_v[n] = (g[n], dn.reshape(w[n].shape), mn.reshape(w[n].shape),
                                                      vn.reshape(w[n].shape))
    return (loss_total, grad_x, *[outs_g[n] for n in WEIGHTS], *[outs_d[n] for n in WEIGHTS],
            *[outs_m[n] for n in WEIGHTS], *[outs_v[n] for n in WEIGHTS])
```

```python
import functools
import math

import jax
import jax.numpy as jnp
import numpy as np
from jax import lax
from jax.experimental import pallas as pl
from jax.experimental.pallas import tpu as pltpu

F32 = jnp.float32
BF16 = jnp.bfloat16
MXU_DTYPE = BF16

D_MODEL = 1024
DEPTH = 2
SSD_HEADS = 16
SSD_HEAD_DIM = 64
SSD_STATE = 128
SSD_CONV = 4
CHUNK = 128
SSD_CONV_DIM = 1536
ATTN_HEAD_DIM = 64
ATTN_Q_HEADS = 8
WINDOW = 128
CONF_WIDTH = 512
CONF_KERNEL = 31
MIX_WIDTH = 2048
D_IN_PROJ = 5392
EPS = 1e-5

ADAM_LR = 0.001
ADAM_B1 = 0.9
ADAM_B2 = 0.999
ADAM_EPS = 1e-08
ADAM_WD = 0.01
ADAM_STEP = 10

LANES = 128
SUBLANES = 8
VMEM_LIMIT = 48 * 1024 * 1024

NP = 5632
OFF_ZA, OFF_Q, OFF_K, OFF_V, OFF_DT = 0, 512, 1024, 1152, 1280
ATTN_GROUP = 1536
OFF_XBC = 1536
OFF_CONF = 3072
OFF_ZS = 4096
OFF_ZC = 5120
SECTIONS = ((0, 1024, OFF_ZS), (1024, 1536, OFF_ZA), (1536, 2048, OFF_ZC), (2048, 3584, OFF_XBC),
            (3584, 3600, OFF_DT), (3600, 4368, OFF_Q), (4368, 5392, OFF_CONF))

YCAT_ATTN, YCAT_CONF = 1024, 1536
ANY = pl.BlockSpec(memory_space=pl.ANY)

NN = (((1,), (0,)), ((), ()))
NT = (((1,), (1,)), ((), ()))
TN = (((0,), (0,)), ((), ()))


def _params(sem):
    return pltpu.CompilerParams(dimension_semantics=sem, vmem_limit_bytes=VMEM_LIMIT)


def _dot(a, b, dims=NN):
    return lax.dot_general(a.astype(MXU_DTYPE), b.astype(MXU_DTYPE), dims, preferred_element_type=F32)


def _split_bf16(a, passes):
    pieces = []
    r = a
    for _ in range(passes):
        p = r.astype(BF16)
        pieces.append(p)
        r = r - p.astype(F32)
    return pieces


def _xdot(a, sel, dims=NN, passes=2):
    out = None
    for p in _split_bf16(a, passes):
        t = lax.dot_general(p, sel, dims, preferred_element_type=F32)
        out = t if out is None else out + t
    return out


def _xdot_r(sel, b, dims=NN, passes=3):
    out = None
    for p in _split_bf16(b, passes):
        t = lax.dot_general(sel, p, dims, preferred_element_type=F32)
        out = t if out is None else out + t
    return out


def _sigmoid(x):
    return 1.0 / (1.0 + jnp.exp(-x))


def _silu(x):
    return x * _sigmoid(x)


def _dsilu(x):
    s = _sigmoid(x)
    return s * (1.0 + x * (1.0 - s))


def _softplus(x):
    return jnp.maximum(x, 0.0) + jnp.log(1.0 + jnp.exp(-jnp.abs(x)))


def _rowsum8(x):
    r, c = x.shape
    return jnp.sum(x.reshape(r // SUBLANES, SUBLANES, c), axis=0)


def _iota(shape, dim):
    return lax.broadcasted_iota(jnp.int32, shape, dim)


def _matmul(a, b, form, out_dtype, tm, tn, tk, name, residual=None, after=None):
    if form == "nn":
        (m, k), n = a.shape, b.shape[1]
    elif form == "nt":
        (m, k), n = a.shape, b.shape[0]
    else:
        (k, m), n = a.shape, b.shape[1]
    tm, tn, tk = min(tm, m), min(tn, n), min(tk, k)
    assert m % tm == 0 and n % tn == 0 and k % tk == 0, (name, m, n, k, tm, tn, tk)
    if form == "nn":
        a_spec = pl.BlockSpec((tm, tk), lambda i, j, s: (i, s))
        b_spec = pl.BlockSpec((tk, tn), lambda i, j, s: (s, j))
        dims = NN
    elif form == "nt":
        (m, k), n = a.shape, b.shape[0]
        a_spec = pl.BlockSpec((tm, tk), lambda i, j, s: (i, s))
        b_spec = pl.BlockSpec((tn, tk), lambda i, j, s: (j, s))
        dims = NT
    else:
        (k, m), n = a.shape, b.shape[1]
        a_spec = pl.BlockSpec((tk, tm), lambda i, j, s: (s, i))
        b_spec = pl.BlockSpec((tk, tn), lambda i, j, s: (s, j))
        dims = TN
    nk = k // tk
    has_res = residual is not None
    deps = [] if after is None else [after]

    def body_single(a_ref, b_ref, *rest):
        o = _dot(a_ref[...], b_ref[...], dims)
        if has_res:
            o = o + rest[0][...]
        rest[-1][...] = o.astype(out_dtype)

    def body(a_ref, b_ref, *rest):
        r_ref = rest[0] if has_res else None
        o_ref, acc = rest[-2:]
        s = pl.program_id(2)

        @pl.when(s == 0)
        def _():
            acc[...] = jnp.zeros_like(acc)

        acc[...] += _dot(a_ref[...], b_ref[...], dims)

        @pl.when(s == nk - 1)
        def _():
            o = acc[...]
            if has_res:
                o = o + r_ref[...]
            o_ref[...] = o.astype(out_dtype)

    in_specs = [a_spec, b_spec]
    args = [a, b]
    if has_res:
        in_specs.append(pl.BlockSpec((tm, tn), lambda i, j, s: (i, j)))
        args.append(residual)
    in_specs += [ANY] * len(deps)
    args += deps
    return pl.pallas_call(
        body_single if nk == 1 else body, name=name,
        out_shape=jax.ShapeDtypeStruct((m, n), out_dtype),
        grid=(m // tm, n // tn, nk),
        in_specs=in_specs,
        out_specs=pl.BlockSpec((tm, tn), lambda i, j, s: (i, j)),
        scratch_shapes=[] if nk == 1 else [pltpu.VMEM((tm, tn), F32)],
        compiler_params=_params(("parallel", "parallel", "arbitrary")),
    )(*args)


ROW_TILE = 256


def _rmsnorm_fwd(x, w, name, after=None):
    t, d = x.shape
    tm = ROW_TILE
    deps = [] if after is None else [after]

    def body(x_ref, w_ref, *rest):
        o_ref, ot_ref = rest[len(deps):]
        xv = x_ref[...]
        rstd = lax.rsqrt(jnp.mean(xv * xv, axis=-1, keepdims=True) + EPS)
        h = xv * rstd * w_ref[...]
        o_ref[...] = h.astype(o_ref.dtype)
        ot_ref[...] = h.T.astype(ot_ref.dtype)

    return pl.pallas_call(
        body, name=name,
        out_shape=(jax.ShapeDtypeStruct((t, d), MXU_DTYPE), jax.ShapeDtypeStruct((d, t), MXU_DTYPE)),
        grid=(t // tm,),
        in_specs=[pl.BlockSpec((tm, d), lambda i: (i, 0)), pl.BlockSpec((1, d), lambda i: (0, 0))]
        + [ANY] * len(deps),
        out_specs=(pl.BlockSpec((tm, d), lambda i: (i, 0)), pl.BlockSpec((d, tm), lambda i: (0, i))),
        compiler_params=_params(("parallel",)),
    )(x, w, *deps)


def _rmsnorm_bwd(dh, x, w, dres, name):
    t, d = x.shape
    tm = ROW_TILE
    nt = t // tm

    def body(dh_ref, x_ref, w_ref, dr_ref, dx_ref, dw_ref, acc):
        i = pl.program_id(0)

        @pl.when(i == 0)
        def _():
            acc[...] = jnp.zeros_like(acc)

        xv = x_ref[...]
        rstd = lax.rsqrt(jnp.mean(xv * xv, axis=-1, keepdims=True) + EPS)
        xh = xv * rstd
        dhv = dh_ref[...]
        g = dhv * w_ref[...]
        dx_ref[...] = dr_ref[...] + rstd * (g - xh * jnp.mean(g * xh, axis=-1, keepdims=True))
        acc[...] += _rowsum8(dhv * xh)

        @pl.when(i == nt - 1)
        def _():
            dw_ref[...] = jnp.sum(acc[...], axis=0, keepdims=True)

    row = pl.BlockSpec((tm, d), lambda i: (i, 0))
    vec = pl.BlockSpec((1, d), lambda i: (0, 0))
    return pl.pallas_call(
        body, name=name,
        out_shape=(jax.ShapeDtypeStruct((t, d), F32), jax.ShapeDtypeStruct((1, d), F32)),
        grid=(nt,),
        in_specs=[row, row, vec, row],
        out_specs=(row, vec),
        scratch_shapes=[pltpu.VMEM((SUBLANES, d), F32)],
        compiler_params=_params(("arbitrary",)),
    )(dh, x, w, dres)


def _loss_head(xf, target, w, name):
    t, d = xf.shape
    tm = ROW_TILE
    nt = t // tm

    def body(x_ref, t_ref, w_ref, loss_ref, dx_ref, dw_ref, lacc, wacc):
        i = pl.program_id(0)

        @pl.when(i == 0)
        def _():
            lacc[...] = jnp.zeros_like(lacc)
            wacc[...] = jnp.zeros_like(wacc)

        xv = x_ref[...]
        rstd = lax.rsqrt(jnp.mean(xv * xv, axis=-1, keepdims=True) + EPS)
        xh = xv * rstd
        err = xh * w_ref[...] - t_ref[...]
        lacc[...] += jnp.sum(err * err)
        dy = err * (1.0 / d)
        g = dy * w_ref[...]
        dx_ref[...] = rstd * (g - xh * jnp.mean(g * xh, axis=-1, keepdims=True))
        wacc[...] += _rowsum8(dy * xh)

        @pl.when(i == nt - 1)
        def _():
            loss_ref[...] = lacc[...] * (0.5 / d)
            dw_ref[...] = jnp.sum(wacc[...], axis=0, keepdims=True)

    row = pl.BlockSpec((tm, d), lambda i: (i, 0))
    vec = pl.BlockSpec((1, d), lambda i: (0, 0))
    return pl.pallas_call(
        body, name=name,
        out_shape=(jax.ShapeDtypeStruct((SUBLANES, LANES), F32), jax.ShapeDtypeStruct((t, d), F32),
                   jax.ShapeDtypeStruct((1, d), F32)),
        grid=(nt,),
        in_specs=[row, row, vec],
        out_specs=(pl.BlockSpec((SUBLANES, LANES), lambda i: (0, 0)), row, vec),
        scratch_shapes=[pltpu.VMEM((SUBLANES, LANES), F32), pltpu.VMEM((SUBLANES, d), F32)],
        compiler_params=_params(("arbitrary",)),
    )(xf, target, w)


def _glu_fwd(proj, name):
    t = proj.shape[0]
    tm, cw = ROW_TILE, CONF_WIDTH

    def body(a_ref, g_ref, o_ref):
        o_ref[...] = a_ref[...] * _sigmoid(g_ref[...])

    return pl.pallas_call(
        body, name=name,
        out_shape=jax.ShapeDtypeStruct((t, cw), F32),
        grid=(t // tm,),
        in_specs=[pl.BlockSpec((tm, cw), lambda i: (i, OFF_CONF // cw)),
                  pl.BlockSpec((tm, cw), lambda i: (i, OFF_CONF // cw + 1))],
        out_specs=pl.BlockSpec((tm, cw), lambda i: (i, 0)),
        compiler_params=_params(("parallel",)),
    )(proj, proj)


def _glu_bwd(dc0, proj, dproj, name):
    t = proj.shape[0]
    tm, cw = ROW_TILE, CONF_WIDTH

    def body(d_ref, a_ref, g_ref, _, o_ref):
        s = _sigmoid(g_ref[...])
        dv = d_ref[...]
        o_ref[:, :cw] = (dv * s).astype(o_ref.dtype)
        o_ref[:, cw:] = (dv * a_ref[...] * s * (1.0 - s)).astype(o_ref.dtype)

    return pl.pallas_call(
        body, name=name,
        out_shape=jax.ShapeDtypeStruct(dproj.shape, dproj.dtype),
        grid=(t // tm,),
        in_specs=[pl.BlockSpec((tm, cw), lambda i: (i, 0)),
                  pl.BlockSpec((tm, cw), lambda i: (i, OFF_CONF // cw)),
                  pl.BlockSpec((tm, cw), lambda i: (i, OFF_CONF // cw + 1)), ANY],
        out_specs=pl.BlockSpec((tm, 2 * cw), lambda i: (i, OFF_CONF // (2 * cw))),
        input_output_aliases={3: 0},
        compiler_params=_params(("parallel",)),
    )(dc0, proj, proj, dproj)


def _conf_post_fwd(c1, proj, ln_w, ln_b, ycat, name):
    t = c1.shape[0]
    tm, cw = ROW_TILE, CONF_WIDTH

    def body(c_ref, z_ref, w_ref, b_ref, _, o_ref):
        cv = c_ref[...]
        xc = cv - jnp.mean(cv, axis=-1, keepdims=True)
        rstd = lax.rsqrt(jnp.mean(xc * xc, axis=-1, keepdims=True) + EPS)
        c2 = xc * rstd * w_ref[...] + b_ref[...]
        o_ref[...] = (_silu(c2) * _silu(z_ref[...])).astype(o_ref.dtype)

    vec = pl.BlockSpec((1, cw), lambda i: (0, 0))
    return pl.pallas_call(
        body, name=name,
        out_shape=jax.ShapeDtypeStruct(ycat.shape, ycat.dtype),
        grid=(t // tm,),
        in_specs=[pl.BlockSpec((tm, cw), lambda i: (i, 0)),
                  pl.BlockSpec((tm, cw), lambda i: (i, OFF_ZC // cw)), vec, vec, ANY],
        out_specs=pl.BlockSpec((tm, cw), lambda i: (i, YCAT_CONF // cw)),
        input_output_aliases={4: 0},
        compiler_params=_params(("parallel",)),
    )(c1, proj, ln_w, ln_b, ycat)


def _conf_post_bwd(dycat, c1, proj, ln_w, ln_b, dproj, name):
    t = c1.shape[0]
    tm, cw = ROW_TILE, CONF_WIDTH
    nt = t // tm

    def body(dy_ref, c_ref, z_ref, w_ref, b_ref, _, dc_ref, dz_ref, dw_ref, db_ref, wacc, bacc):
        i = pl.program_id(0)

        @pl.when(i == 0)
        def _():
            wacc[...] = jnp.zeros_like(wacc)
            bacc[...] = jnp.zeros_like(bacc)

        cv = c_ref[...]
        xc = cv - jnp.mean(cv, axis=-1, keepdims=True)
        rstd = lax.rsqrt(jnp.mean(xc * xc, axis=-1, keepdims=True) + EPS)
        xh = xc * rstd
        c2 = xh * w_ref[...] + b_ref[...]
        zv = z_ref[...]
        dy = dy_ref[...]
        dz_ref[...] = (dy * _silu(c2) * _dsilu(zv)).astype(dz_ref.dtype)
        dc2 = dy * _silu(zv) * _dsilu(c2)
        bacc[...] += _rowsum8(dc2)
        wacc[...] += _rowsum8(dc2 * xh)
        dxh = dc2 * w_ref[...]
        dc_ref[...] = rstd * (dxh - jnp.mean(dxh, axis=-1, keepdims=True)
                              - xh * jnp.mean(dxh * xh, axis=-1, keepdims=True))

        @pl.when(i == nt - 1)
        def _():
            dw_ref[...] = jnp.sum(wacc[...], axis=0, keepdims=True)
            db_ref[...] = jnp.sum(bacc[...], axis=0, keepdims=True)

    row = pl.BlockSpec((tm, cw), lambda i: (i, 0))
    vec = pl.BlockSpec((1, cw), lambda i: (0, 0))
    return pl.pallas_call(
        body, name=name,
        out_shape=(jax.ShapeDtypeStruct((t, cw), F32), jax.ShapeDtypeStruct(dproj.shape, dproj.dtype),
                   jax.ShapeDtypeStruct((1, cw), F32), jax.ShapeDtypeStruct((1, cw), F32)),
        grid=(nt,),
        in_specs=[pl.BlockSpec((tm, cw), lambda i: (i, YCAT_CONF // cw)), row,
                  pl.BlockSpec((tm, cw), lambda i: (i, OFF_ZC // cw)), vec, vec, ANY],
        out_specs=(row, pl.BlockSpec((tm, cw), lambda i: (i, OFF_ZC // cw)), vec, vec),
        input_output_aliases={5: 1},
        scratch_shapes=[pltpu.VMEM((SUBLANES, cw), F32), pltpu.VMEM((SUBLANES, cw), F32)],
        compiler_params=_params(("arbitrary",)),
    )(dycat, c1, proj, ln_w, ln_b, dproj)


CONV_TILE = 512
CONV_COLS = 512
CONV_SUB_ROWS = 128
CONV_SUB_COLS = LANES


def _conv_halo(k):
    return SUBLANES if k - 1 <= SUBLANES else 32


def _conv_subtiles(tm, cw):
    return [(r0, c0) for r0 in range(0, tm, CONV_SUB_ROWS) for c0 in range(0, cw, CONV_SUB_COLS)]


def _conv_use_shifted(k):
    return k > SUBLANES


def _conv_shift_scratch(k, rows, cw):
    return [pltpu.VMEM((SUBLANES - 1, rows - SUBLANES, cw), F32)] if _conv_use_shifted(k) else []


def _conv_fill_shifted(ext, sh):
    n = sh.shape[1]
    for b in range(1, SUBLANES):
        sh[b - 1] = ext[b:b + n, :]


def _conv_rows(ext, sh, start, rows, cs):
    b = start % SUBLANES
    if b == 0 or not sh:
        return ext[start:start + rows, cs]
    return sh[0][b - 1, start - b:start - b + rows, cs]


def _conv_fwd(src, col0, width, w, bias, k, seq, name):
    t = src.shape[0]
    tm, cw, halo = CONV_TILE, CONV_COLS, _conv_halo(k)
    sr, sc = CONV_SUB_ROWS, CONV_SUB_COLS
    p = k - 1
    cb0 = col0 // cw
    kp = w.shape[0]

    shifted = _conv_use_shifted(k)

    def body(x_ref, h_ref, w_ref, b_ref, o_ref, ext, *sh):
        i = pl.program_id(0)
        seq_start = (i * tm) % seq == 0
        ext[halo:, :] = x_ref[...]
        ext[:halo, :] = jnp.where(seq_start, 0.0, h_ref[...])
        if shifted:
            _conv_fill_shifted(ext, sh[0])
        for r0, c0 in _conv_subtiles(tm, cw):
            cs = slice(c0, c0 + sc)
            acc = jnp.zeros((sr, sc), F32) + b_ref[:, cs]
            for j in range(k):
                acc = acc + w_ref[j:j + 1, cs] * _conv_rows(ext, sh, r0 + halo - p + j, sr, cs)
            o_ref[r0:r0 + sr, cs] = acc

    return pl.pallas_call(
        body, name=name,
        out_shape=jax.ShapeDtypeStruct((t, width), F32),
        grid=(t // tm, width // cw),
        in_specs=[pl.BlockSpec((tm, cw), lambda i, j: (i, cb0 + j)),
                  pl.BlockSpec((halo, cw), lambda i, j: (jnp.maximum(i * (tm // halo) - 1, 0), cb0 + j)),
                  pl.BlockSpec((kp, cw), lambda i, j: (0, j)),
                  pl.BlockSpec((1, cw), lambda i, j: (0, j))],
        out_specs=pl.BlockSpec((tm, cw), lambda i, j: (i, j)),
        scratch_shapes=[pltpu.VMEM((halo + tm, cw), F32)] + _conv_shift_scratch(k, halo + tm, cw),
        compiler_params=_params(("parallel", "parallel")),
    )(src, src, w, bias)


def _conv_bwd(dy, src, col0, width, w, k, seq, name, into=None):
    t = src.shape[0]
    tm, cw, halo = CONV_TILE, CONV_COLS, _conv_halo(k)
    sr, sc = CONV_SUB_ROWS, CONV_SUB_COLS
    p = k - 1
    cb0 = col0 // cw
    kp = w.shape[0]
    nt = t // tm
    last_halo = t // halo - 1

    shifted = _conv_use_shifted(k)

    def body(dy_ref, dn_ref, x_ref, xp_ref, w_ref, *rest):
        if into is not None:
            rest = rest[1:]
        dx_ref, dw_ref, db_ref, dyext, xext, wacc, bacc = rest[:7]
        sh = rest[7:]
        i = pl.program_id(1)
        dysh, xsh = (sh[:1], sh[1:]) if shifted else ((), ())

        @pl.when(i == 0)
        def _():
            wacc[...] = jnp.zeros_like(wacc)
            bacc[...] = jnp.zeros_like(bacc)

        seq_start = (i * tm) % seq == 0
        seq_end = ((i + 1) * tm) % seq == 0
        dyext[:tm, :] = dy_ref[...]
        dyext[tm:, :] = jnp.where(seq_end, 0.0, dn_ref[...])
        xext[halo:, :] = x_ref[...]
        xext[:halo, :] = jnp.where(seq_start, 0.0, xp_ref[...])
        if shifted:
            _conv_fill_shifted(dyext, dysh[0])
            _conv_fill_shifted(xext, xsh[0])
        for r0, c0 in _conv_subtiles(tm, cw):
            cs = slice(c0, c0 + sc)
            dyv = dy_ref[r0:r0 + sr, cs]
            acc = jnp.zeros((sr, sc), F32)
            for j in range(k):
                acc = acc + w_ref[j:j + 1, cs] * _conv_rows(dyext, dysh, r0 + p - j, sr, cs)
                wacc[j, :, cs] += _rowsum8(dyv * _conv_rows(xext, xsh, r0 + halo - p + j, sr, cs))
            dx_ref[r0:r0 + sr, cs] = acc.astype(dx_ref.dtype)
            bacc[:, cs] += _rowsum8(dyv)

        @pl.when(i == nt - 1)
        def _():
            dw_ref[...] = jnp.zeros_like(dw_ref)
            for j in range(k):
                dw_ref[j:j + 1, :] = jnp.sum(wacc[j], axis=0, keepdims=True)
            db_ref[...] = jnp.sum(bacc[...], axis=0, keepdims=True)

    if into is None:
        dx_shape = jax.ShapeDtypeStruct((t, width), F32)
        dx_spec = pl.BlockSpec((tm, cw), lambda j, i: (i, j))
        extra_specs, extra_args, aliases = [], [], {}
    else:
        dx_shape = jax.ShapeDtypeStruct(into.shape, into.dtype)
        dx_spec = pl.BlockSpec((tm, cw), lambda j, i: (i, cb0 + j))
        extra_specs, extra_args, aliases = [ANY], [into], {5: 0}
    return pl.pallas_call(
        body, name=name,
        out_shape=(dx_shape, jax.ShapeDtypeStruct((kp, width), F32), jax.ShapeDtypeStruct((1, width), F32)),
        grid=(width // cw, nt),
        in_specs=[pl.BlockSpec((tm, cw), lambda j, i: (i, j)),
                  pl.BlockSpec((halo, cw), lambda j, i: (jnp.minimum((i + 1) * (tm // halo), last_halo), j)),
                  pl.BlockSpec((tm, cw), lambda j, i: (i, cb0 + j)),
                  pl.BlockSpec((halo, cw), lambda j, i: (jnp.maximum(i * (tm // halo) - 1, 0), cb0 + j)),
                  pl.BlockSpec((kp, cw), lambda j, i: (0, j))] + extra_specs,
        out_specs=(dx_spec,
                   pl.BlockSpec((kp, cw), lambda j, i: (0, j)),
                   pl.BlockSpec((1, cw), lambda j, i: (0, j))),
        input_output_aliases=aliases,
        scratch_shapes=[pltpu.VMEM((tm + halo, cw), F32), pltpu.VMEM((halo + tm, cw), F32),
                        pltpu.VMEM((kp, SUBLANES, cw), F32), pltpu.VMEM((SUBLANES, cw), F32)]
        + 2 * _conv_shift_scratch(k, halo + tm, cw),
        compiler_params=_params(("parallel", "arbitrary")),
    )(dy, dy, src, src, w, *extra_args)


def _half_mask(half):
    lane = _iota((1, LANES), 1)
    return ((lane >= half * ATTN_HEAD_DIM) & (lane < (half + 1) * ATTN_HEAD_DIM)).astype(F32)


def _stack_heads(xp, g):
    m = _half_mask(g)
    swapped = pltpu.roll(xp, ATTN_HEAD_DIM, axis=1)
    return jnp.concatenate([xp * m, swapped * m] if g == 0 else [swapped * m, xp * m], axis=0)


def _unstack_heads(both, g):
    w = both.shape[0] // 2
    top, bot = both[:w], both[w:]
    lo, hi = _half_mask(0), _half_mask(1)
    if g == 0:
        return top * lo + pltpu.roll(bot, ATTN_HEAD_DIM, axis=1) * hi
    return pltpu.roll(top, ATTN_HEAD_DIM, axis=1) * lo + bot * hi


def _band_mask(first_block):
    w = WINDOW
    qi = _iota((w, 2 * w), 0)
    kj = _iota((w, 2 * w), 1) - w
    rel = qi - kj
    return (rel >= 0) & (rel < w) & (jnp.logical_not(first_block) | (kj >= 0))


def _lane_pick(x, h):
    return jnp.sum(jnp.where(_iota(x.shape, 1) == h, x, 0.0), axis=1, keepdims=True)


def _attn_specs(nb, rev):
    w = WINDOW

    def blk(i):
        return nb - 1 - i if rev else i

    def row(b, i):
        return b * nb + blk(i)

    def prow(b, i):
        return b * nb + jnp.maximum(blk(i) - 1, 0)

    q = pl.BlockSpec((w, 512), lambda b, i: (row(b, i), OFF_Q // 512))
    kc = pl.BlockSpec((w, 128), lambda b, i: (row(b, i), OFF_K // 128))
    kp = pl.BlockSpec((w, 128), lambda b, i: (prow(b, i), OFF_K // 128))
    vc = pl.BlockSpec((w, 128), lambda b, i: (row(b, i), OFF_V // 128))
    vp = pl.BlockSpec((w, 128), lambda b, i: (prow(b, i), OFF_V // 128))
    z = pl.BlockSpec((w, 512), lambda b, i: (row(b, i), OFF_ZA // 512))
    return q, kc, kp, vc, vp, z, row


def _attn_fwd(proj, sinks, ycat, nbatch, name):
    t = proj.shape[0]
    w = WINDOW
    nb = t // nbatch // w
    scale = ATTN_HEAD_DIM ** -0.5
    q_s, kc_s, kp_s, vc_s, vp_s, z_s, row = _attn_specs(nb, False)

    def body(q_ref, kc_ref, kp_ref, vc_ref, vp_ref, z_ref, sk_ref, _, y_ref, o_ref, lse_ref):
        first = pl.program_id(1) == 0
        mask = _band_mask(first)
        kk = jnp.concatenate([kp_ref[...], kc_ref[...]], axis=0).astype(MXU_DTYPE)
        vv = jnp.concatenate([vp_ref[...], vc_ref[...]], axis=0).astype(MXU_DTYPE)
        sk = sk_ref[...]
        lane = _iota((w, LANES), 1)
        mask2 = jnp.concatenate([mask, mask], axis=0)
        scores = [_dot(_stack_heads(q_ref[:, j * LANES:(j + 1) * LANES], j // 2), kk, NT) for j in range(4)]
        lse_all = jnp.zeros((w, LANES), F32)
        for j in range(4):
            s = jnp.where(mask2, scores[j] * scale, -1e30)
            skc = jnp.concatenate([jnp.broadcast_to(_lane_pick(sk, 2 * j), (w, 1)),
                                   jnp.broadcast_to(_lane_pick(sk, 2 * j + 1), (w, 1))], axis=0)
            m = jnp.maximum(jnp.max(s, axis=1, keepdims=True), skc)
            den = jnp.sum(jnp.exp(s - m), axis=1, keepdims=True) + jnp.exp(skc - m)
            lse = m + jnp.log(den)
            lse_all = jnp.where(lane == 2 * j, lse[:w], lse_all)
            lse_all = jnp.where(lane == 2 * j + 1, lse[w:], lse_all)
            op = _unstack_heads(_dot(jnp.exp(s - lse), vv), j // 2)
            cols = slice(j * LANES, (j + 1) * LANES)
            o_ref[:, cols] = op
            y_ref[:, cols] = (op * _silu(z_ref[:, cols])).astype(y_ref.dtype)
        lse_ref[...] = lse_all

    return pl.pallas_call(
        body, name=name,
        out_shape=(jax.ShapeDtypeStruct(ycat.shape, ycat.dtype), jax.ShapeDtypeStruct((t, 512), F32),
                   jax.ShapeDtypeStruct((t, LANES), F32)),
        grid=(nbatch, nb),
        in_specs=[q_s, kc_s, kp_s, vc_s, vp_s, z_s, pl.BlockSpec((1, LANES), lambda b, i: (0, 0)), ANY],
        out_specs=(pl.BlockSpec((w, 512), lambda b, i: (row(b, i), YCAT_ATTN // 512)),
                   pl.BlockSpec((w, 512), lambda b, i: (row(b, i), 0)),
                   pl.BlockSpec((w, LANES), lambda b, i: (row(b, i), 0))),
        input_output_aliases={7: 0},
        compiler_params=_params(("parallel", "parallel")),
    )(proj, proj, proj, proj, proj, proj, sinks, ycat)


def _attn_bwd(dycat, proj, o, lse, sinks, ddt, dproj, nbatch, name):
    t = proj.shape[0]
    w = WINDOW
    nb = t // nbatch // w
    scale = ATTN_HEAD_DIM ** -0.5
    q_s, kc_s, kp_s, vc_s, vp_s, z_s, row = _attn_specs(nb, True)

    def body(dy_ref, q_ref, kc_ref, kp_ref, vc_ref, vp_ref, z_ref, o_ref, lse_ref, sk_ref, ddt_ref, _,
             grp_ref, dsk_ref, kcarry, vcarry, sacc):
        b, i = pl.program_id(0), pl.program_id(1)

        @pl.when((b == 0) & (i == 0))
        def _():
            sacc[...] = jnp.zeros_like(sacc)

        @pl.when(i == 0)
        def _():
            kcarry[...] = jnp.zeros_like(kcarry)
            vcarry[...] = jnp.zeros_like(vcarry)

        first = i == nb - 1
        mask = _band_mask(first)
        kk = jnp.concatenate([kp_ref[...], kc_ref[...]], axis=0).astype(MXU_DTYPE)
        vv = jnp.concatenate([vp_ref[...], vc_ref[...]], axis=0).astype(MXU_DTYPE)
        sk = sk_ref[...]
        lse_all = lse_ref[...]
        lane1 = _iota((1, LANES), 1)
        mask2 = jnp.concatenate([mask, mask], axis=0)
        qs, dos, deltas, lses, scores, dps = [], [], [], [], [], []
        for j in range(4):
            cols = slice(j * LANES, (j + 1) * LANES)
            qp, zp, ov, dy = q_ref[:, cols], z_ref[:, cols], o_ref[:, cols], dy_ref[:, cols]
            grp_ref[:, OFF_ZA + j * LANES:OFF_ZA + (j + 1) * LANES] = (dy * ov * _dsilu(zp)).astype(grp_ref.dtype)
            do = dy * _silu(zp)
            q2 = _stack_heads(qp, j // 2).astype(MXU_DTYPE)
            do2 = _stack_heads(do, j // 2)
            qs.append(q2)
            dos.append(do2.astype(MXU_DTYPE))
            deltas.append(jnp.sum(do2 * _stack_heads(ov, j // 2), axis=1, keepdims=True))
            lses.append(jnp.concatenate([_lane_pick(lse_all, 2 * j), _lane_pick(lse_all, 2 * j + 1)], axis=0))
            scores.append(_dot(q2, kk, NT))
            dps.append(_dot(do2, vv, NT))
        prs, dss = [], []
        dsk = jnp.zeros((1, LANES), F32)
        for j in range(4):
            pr = jnp.exp(jnp.where(mask2, scores[j] * scale, -1e30) - lses[j])
            prs.append(pr.astype(MXU_DTYPE))
            dss.append((pr * (dps[j] - deltas[j])).astype(MXU_DTYPE))
            skc = jnp.concatenate([jnp.broadcast_to(_lane_pick(sk, 2 * j), (w, 1)),
                                   jnp.broadcast_to(_lane_pick(sk, 2 * j + 1), (w, 1))], axis=0)
            sink_term = jnp.exp(skc - lses[j]) * deltas[j]
            dsk = dsk - jnp.where(lane1 == 2 * j, jnp.sum(sink_term[:w]), 0.0)
            dsk = dsk - jnp.where(lane1 == 2 * j + 1, jnp.sum(sink_term[w:]), 0.0)
        dkk = jnp.zeros((2 * w, LANES), F32)
        dvv = jnp.zeros((2 * w, LANES), F32)
        for j in range(4):
            dq = _unstack_heads(_dot(dss[j], kk) * scale, j // 2)
            grp_ref[:, OFF_Q + j * LANES:OFF_Q + (j + 1) * LANES] = dq.astype(grp_ref.dtype)
            dkk = dkk + _dot(dss[j], qs[j], TN) * scale
            dvv = dvv + _dot(prs[j], dos[j], TN)
        grp_ref[:, OFF_K:OFF_K + LANES] = (dkk[w:, :] + kcarry[...]).astype(grp_ref.dtype)
        grp_ref[:, OFF_V:OFF_V + LANES] = (dvv[w:, :] + vcarry[...]).astype(grp_ref.dtype)
        grp_ref[:, OFF_DT:OFF_DT + LANES] = ddt_ref[...].astype(grp_ref.dtype)
        grp_ref[:, OFF_DT + LANES:] = jnp.zeros((w, ATTN_GROUP - OFF_DT - LANES), grp_ref.dtype)
        kcarry[...] = dkk[:w, :]
        vcarry[...] = dvv[:w, :]
        sacc[...] += dsk

        @pl.when((b == nbatch - 1) & (i == nb - 1))
        def _():
            dsk_ref[...] = sacc[...]

    return pl.pallas_call(
        body, name=name,
        out_shape=(jax.ShapeDtypeStruct(dproj.shape, dproj.dtype), jax.ShapeDtypeStruct((1, LANES), F32)),
        grid=(nbatch, nb),
        in_specs=[pl.BlockSpec((w, 512), lambda b, i: (row(b, i), YCAT_ATTN // 512)),
                  q_s, kc_s, kp_s, vc_s, vp_s, z_s,
                  pl.BlockSpec((w, 512), lambda b, i: (row(b, i), 0)),
                  pl.BlockSpec((w, LANES), lambda b, i: (row(b, i), 0)),
                  pl.BlockSpec((1, LANES), lambda b, i: (0, 0)),
                  pl.BlockSpec((w, LANES), lambda b, i: (row(b, i), 0)), ANY],
        out_specs=(pl.BlockSpec((w, ATTN_GROUP), lambda b, i: (row(b, i), 0)),
                   pl.BlockSpec((1, LANES), lambda b, i: (0, 0))),
        input_output_aliases={11: 0},
        scratch_shapes=[pltpu.VMEM((w, LANES), F32), pltpu.VMEM((w, LANES), F32),
                        pltpu.VMEM((1, LANES), F32)],
        compiler_params=_params(("arbitrary", "arbitrary")),
    )(dycat, proj, proj, proj, proj, proj, proj, o, lse, sinks, ddt, dproj)


SSD_WIDTH = SSD_HEADS * SSD_HEAD_DIM
GROUP_ROWS = SSD_WIDTH // 2


def _expand_mat():
    r, c = _iota((LANES, SSD_WIDTH), 0), _iota((LANES, SSD_WIDTH), 1)
    return (r == lax.shift_right_logical(c, 6)).astype(BF16)


def _expand_mat_t():
    r, c = _iota((SSD_WIDTH, LANES), 0), _iota((SSD_WIDTH, LANES), 1)
    return (c == lax.shift_right_logical(r, 6)).astype(BF16)


def _ssd_common(u_ref, dt_ref, dtb_ref, a_ref):
    q = CHUNK
    act = _silu(u_ref[...])
    xs = act[:, :SSD_WIDTH]
    bm = act[:, SSD_WIDTH:SSD_WIDTH + 256]
    cm = act[:, SSD_WIDTH + 256:]
    dtp = _softplus(dt_ref[...] + dtb_ref[...])
    a = dtp * a_ref[...]
    tril = (_iota((q, q), 0) >= _iota((q, q), 1)).astype(BF16)
    acs = _xdot_r(tril, a)
    acs_t = acs.T
    e = _expand_mat()
    dt_x = _xdot(dtp, e)
    ea = jnp.exp(_xdot(acs, e))
    a_end = jnp.sum(jnp.where(_iota(acs.shape, 0) == q - 1, acs, 0.0), axis=0, keepdims=True)
    dec = jnp.exp(_xdot(a_end - acs, e))
    a_end_col = jnp.broadcast_to(_lane_pick(acs_t, q - 1), (LANES, LANES))
    s_scale = jnp.exp(_xdot_r(_expand_mat_t(), a_end_col))
    return act, xs, bm, cm, dtp, acs, acs_t, dt_x, ea, dec, s_scale, tril


def _decay_mat(acs, acs_t, h):
    q = CHUNK
    col = _lane_pick(acs, h)
    rowv = jnp.sum(jnp.where(_iota(acs_t.shape, 0) == h, acs_t, 0.0), axis=0, keepdims=True)
    causal = _iota((q, q), 0) >= _iota((q, q), 1)
    return jnp.exp(jnp.where(causal, col - rowv, -1e30))


GN_WIDTH = 512


def _ssd_fwd(u, proj, dtb, a_neg, d_x, norm_w, ycat, nbatch, name):
    t = u.shape[0]
    q = CHUNK
    nc = t // nbatch // q

    def body(u_ref, dt_ref, z_ref, dtb_ref, a_ref, dx_ref, nw_ref, _, y_ref, st_ref, yn_ref, state):
        c = pl.program_id(1)

        @pl.when(c == 0)
        def _():
            state[...] = jnp.zeros_like(state)

        st_ref[...] = state[...]
        act, xs, bm, cm, dtp, acs, acs_t, dt_x, ea, dec, s_scale, _ = _ssd_common(u_ref, dt_ref, dtb_ref, a_ref)
        xdt = xs * dt_x
        xdec = xdt * dec
        lo, hi = _half_mask(0), _half_mask(1)
        for g in range(2):
            bg = bm[:, g * LANES:(g + 1) * LANES]
            cg = cm[:, g * LANES:(g + 1) * LANES]
            rows = slice(g * GROUP_ROWS, (g + 1) * GROUP_ROWS)
            sg = state[rows, :]
            cb = _dot(cg, bg, NT)
            yoff = _dot(cg, sg, NT)
            for j in range(4):
                pj = g * 4 + j
                cols = slice(pj * LANES, (pj + 1) * LANES)
                xp = xdt[:, cols]
                m0 = cb * _decay_mat(acs, acs_t, 2 * pj)
                m1 = cb * _decay_mat(acs, acs_t, 2 * pj + 1)
                yp = _dot(m0, xp * lo) + _dot(m1, xp * hi)
                yp = yp + yoff[:, j * LANES:(j + 1) * LANES] * ea[:, cols]
                y_ref[:, cols] = yp + dx_ref[:, cols] * xs[:, cols]
            state[rows, :] = s_scale[rows, :] * sg + _dot(xdec[:, rows], bg, TN)
        for g in range(SSD_WIDTH // GN_WIDTH):
            cols = slice(g * GN_WIDTH, (g + 1) * GN_WIDTH)
            gg = y_ref[:, cols] * _silu(z_ref[:, cols])
            rstd = lax.rsqrt(jnp.mean(gg * gg, axis=-1, keepdims=True) + EPS)
            yn_ref[:, cols] = (gg * rstd * nw_ref[:, cols]).astype(yn_ref.dtype)

    vec = pl.BlockSpec((1, LANES), lambda b, c: (0, 0))
    wide = pl.BlockSpec((q, SSD_WIDTH), lambda b, c: (b * nc + c, 0))
    wvec = pl.BlockSpec((1, SSD_WIDTH), lambda b, c: (0, 0))
    return pl.pallas_call(
        body, name=name,
        out_shape=(jax.ShapeDtypeStruct((t, SSD_WIDTH), F32),
                   jax.ShapeDtypeStruct((nbatch * nc * SSD_WIDTH, SSD_STATE), F32),
                   jax.ShapeDtypeStruct(ycat.shape, ycat.dtype)),
        grid=(nbatch, nc),
        in_specs=[pl.BlockSpec((q, SSD_CONV_DIM), lambda b, c: (b * nc + c, 0)),
                  pl.BlockSpec((q, LANES), lambda b, c: (b * nc + c, OFF_DT // LANES)),
                  pl.BlockSpec((q, SSD_WIDTH), lambda b, c: (b * nc + c, OFF_ZS // SSD_WIDTH)),
                  vec, vec, wvec, wvec, ANY],
        out_specs=(wide, pl.BlockSpec((SSD_WIDTH, SSD_STATE), lambda b, c: (b * nc + c, 0)), wide),
        input_output_aliases={7: 2},
        scratch_shapes=[pltpu.VMEM((SSD_WIDTH, SSD_STATE), F32)],
        compiler_params=_params(("parallel", "arbitrary")),
    )(u, proj, proj, dtb, a_neg, d_x, norm_w, ycat)


def _ssd_bwd(dycat, u, proj, y, states, dtb, a_neg, d_x, norm_w, dproj, nbatch, name):
    t = u.shape[0]
    q = CHUNK
    nc = t // nbatch // q

    def body(do_ref, u_ref, dt_ref, z_ref, y_ref, st_ref, dtb_ref, a_ref, dx_ref, nw_ref, _,
             du_ref, dz_ref, ddt_ref, dal_ref, dd_ref, dtbg_ref, dnw_ref, dstate, acc_a, acc_d, acc_b, acc_w):
        b, c = pl.program_id(0), pl.program_id(1)

        @pl.when((b == 0) & (c == 0))
        def _():
            acc_a[...] = jnp.zeros_like(acc_a)
            acc_d[...] = jnp.zeros_like(acc_d)
            acc_b[...] = jnp.zeros_like(acc_b)
            acc_w[...] = jnp.zeros_like(acc_w)

        @pl.when(c == 0)
        def _():
            dstate[...] = jnp.zeros_like(dstate)

        dy_parts = []
        for g in range(SSD_WIDTH // GN_WIDTH):
            cols = slice(g * GN_WIDTH, (g + 1) * GN_WIDTH)
            yv, zv, dov = y_ref[:, cols], z_ref[:, cols], do_ref[:, cols]
            sz = _silu(zv)
            gg = yv * sz
            rstd = lax.rsqrt(jnp.mean(gg * gg, axis=-1, keepdims=True) + EPS)
            gh = gg * rstd
            acc_w[:, cols] += _rowsum8(dov * gh)
            dgn = dov * nw_ref[:, cols]
            dg = rstd * (dgn - gh * jnp.mean(dgn * gh, axis=-1, keepdims=True))
            dy_parts.append(dg * sz)
            dz_ref[:, cols] = (dg * yv * _dsilu(zv)).astype(dz_ref.dtype)

        act, xs, bm, cm, dtp, acs, acs_t, dt_x, ea, dec, s_scale, tril = _ssd_common(
            u_ref, dt_ref, dtb_ref, a_ref)
        xdt = xs * dt_x
        xdec = xdt * dec
        dyv = jnp.concatenate(dy_parts, axis=1)
        dye = dyv * ea
        lo, hi = _half_mask(0), _half_mask(1)
        et = _expand_mat_t()
        dxdt_parts, db_parts, dc_parts, dxst_parts, yoff_parts = [], [], [], [], []
        end_sum = jnp.zeros((LANES, LANES), F32)
        dal_diag = jnp.zeros((q, LANES), F32)
        lane_q = _iota((q, LANES), 1)
        for g in range(2):
            bg = bm[:, g * LANES:(g + 1) * LANES]
            cg = cm[:, g * LANES:(g + 1) * LANES]
            rows = slice(g * GROUP_ROWS, (g + 1) * GROUP_ROWS)
            sg = st_ref[rows, :]
            dsg = dstate[rows, :]
            cb = _dot(cg, bg, NT)
            yoff_parts.append(_dot(cg, sg, NT))
            dcb = jnp.zeros((q, q), F32)
            parts = []
            for j in range(4):
                pj = g * 4 + j
                cols = slice(pj * LANES, (pj + 1) * LANES)
                xp = xdt[:, cols]
                dy0, dy1 = dyv[:, cols] * lo, dyv[:, cols] * hi
                l0 = _decay_mat(acs, acs_t, 2 * pj)
                l1 = _decay_mat(acs, acs_t, 2 * pj + 1)
                g0, g1 = _dot(dy0, xp, NT), _dot(dy1, xp, NT)
                m0, m1 = cb * l0, cb * l1
                dcb = dcb + g0 * l0 + g1 * l1
                parts.append(_dot(m0, dy0, TN) + _dot(m1, dy1, TN))
                for hh, wmat in enumerate((g0 * m0, g1 * m1)):
                    sel = (lane_q == 2 * pj + hh).astype(F32)
                    dal_diag = dal_diag + _dot(wmat, sel) - _dot(wmat, sel, TN)
            dxst = _dot(bg, dsg, NT) * dec[:, rows]
            dxst_parts.append(dxst)
            dxdt_parts.append(jnp.concatenate(parts, axis=1) + dxst)
            dc_parts.append(_dot(dcb, bg) + _dot(dye[:, rows], sg))
            db_parts.append(_dot(dcb, cg, TN) + _dot(xdec[:, rows], dsg))
            s_next = s_scale[rows, :] * sg + _dot(xdec[:, rows], bg, TN)
            end_sum = end_sum + _xdot(dsg * s_next, et[rows, :], TN, passes=2)
            dstate[rows, :] = _dot(dye[:, rows], cg, TN) + s_scale[rows, :] * dsg
        dxdt = jnp.concatenate(dxdt_parts, axis=1)
        dxv = dx_ref[...]
        yoff = jnp.concatenate(yoff_parts, axis=1) * ea
        dalpha = dal_diag + _xdot(dyv * yoff - xdt * jnp.concatenate(dxst_parts, axis=1), et)
        end_row = jnp.sum(end_sum, axis=0, keepdims=True)
        dalpha = dalpha + jnp.where(_iota((q, LANES), 0) == q - 1, end_row, 0.0)
        da = _xdot_r(tril, dalpha, TN)
        ddtp = da * a_ref[...] + _xdot(dxdt * xs, et)
        acc_a[...] += _rowsum8(da * dtp)
        acc_d[...] += _rowsum8(_xdot(dyv * xs, et))
        ddt_raw = ddtp * _sigmoid(dt_ref[...] + dtb_ref[...])
        acc_b[...] += _rowsum8(ddt_raw)
        ddt_ref[...] = ddt_raw
        dxs = dxdt * dt_x + dxv * dyv
        dact = jnp.concatenate([dxs] + db_parts + dc_parts, axis=1)
        du_ref[...] = dact * _dsilu(u_ref[...])

        @pl.when((b == nbatch - 1) & (c == nc - 1))
        def _():
            dal_ref[...] = jnp.sum(acc_a[...], axis=0, keepdims=True) * a_ref[...]
            dd_ref[...] = jnp.sum(acc_d[...], axis=0, keepdims=True)
            dtbg_ref[...] = jnp.sum(acc_b[...], axis=0, keepdims=True)
            dnw_ref[...] = jnp.sum(acc_w[...], axis=0, keepdims=True)

    def rowblk(b, c):
        return b * nc + (nc - 1 - c)

    vec = pl.BlockSpec((1, LANES), lambda b, c: (0, 0))
    wvec = pl.BlockSpec((1, SSD_WIDTH), lambda b, c: (0, 0))
    wide = pl.BlockSpec((q, SSD_WIDTH), lambda b, c: (rowblk(b, c), 0))
    zblk = pl.BlockSpec((q, SSD_WIDTH), lambda b, c: (rowblk(b, c), OFF_ZS // SSD_WIDTH))
    return pl.pallas_call(
        body, name=name,
        out_shape=(jax.ShapeDtypeStruct((t, SSD_CONV_DIM), F32), jax.ShapeDtypeStruct(dproj.shape, dproj.dtype),
                   jax.ShapeDtypeStruct((t, LANES), F32),
                   jax.ShapeDtypeStruct((1, LANES), F32), jax.ShapeDtypeStruct((1, LANES), F32),
                   jax.ShapeDtypeStruct((1, LANES), F32), jax.ShapeDtypeStruct((1, SSD_WIDTH), F32)),
        grid=(nbatch, nc),
        in_specs=[wide,
                  pl.BlockSpec((q, SSD_CONV_DIM), lambda b, c: (rowblk(b, c), 0)),
                  pl.BlockSpec((q, LANES), lambda b, c: (rowblk(b, c), OFF_DT // LANES)),
                  zblk, wide,
                  pl.BlockSpec((SSD_WIDTH, SSD_STATE), lambda b, c: (rowblk(b, c), 0)),
                  vec, vec, wvec, wvec, ANY],
        out_specs=(pl.BlockSpec((q, SSD_CONV_DIM), lambda b, c: (rowblk(b, c), 0)),
                   zblk,
                   pl.BlockSpec((q, LANES), lambda b, c: (rowblk(b, c), 0)),
                   vec, vec, vec, wvec),
        input_output_aliases={10: 1},
        scratch_shapes=[pltpu.VMEM((SSD_WIDTH, SSD_STATE), F32), pltpu.VMEM((SUBLANES, LANES), F32),
                        pltpu.VMEM((SUBLANES, LANES), F32), pltpu.VMEM((SUBLANES, LANES), F32),
                        pltpu.VMEM((SUBLANES, SSD_WIDTH), F32)],
        compiler_params=_params(("arbitrary", "arbitrary")),
    )(dycat, u, proj, proj, y, states, dtb, a_neg, d_x, norm_w, dproj)


def _pad_rows(w, rows):
    return jnp.concatenate([w, jnp.zeros((rows - w.shape[0], w.shape[1]), w.dtype)], axis=0)


def _pad_lanes(v):
    return jnp.concatenate([v, jnp.zeros((LANES - v.shape[0],), v.dtype)]).reshape(1, LANES)


def _padded_from_chips(pieces):
    cols = pieces[0].shape[-1]
    lead = pieces[0].shape[:-1]
    parts, pos = [], 0
    for lo, hi, start in sorted(SECTIONS, key=lambda s: s[2]):
        if start > pos:
            parts.append(jnp.zeros(lead + (start - pos,), pieces[0].dtype))
        pos = start + hi - lo
        while lo < hi:
            p = lo // cols
            end = min(hi, (p + 1) * cols)
            parts.append(pieces[p][..., lo - p * cols:end - p * cols])
            lo = end
    if pos < NP:
        parts.append(jnp.zeros(lead + (NP - pos,), pieces[0].dtype))
    return jnp.concatenate(parts, axis=-1)


def _chip_part_from_padded(wp, p, cols):
    lo, hi = p * cols, (p + 1) * cols
    parts = []
    for rs, re, start in SECTIONS:
        a, b = max(lo, rs), min(hi, re)
        if a < b:
            parts.append(wp[..., start + a - rs:start + b - rs])
    return jnp.concatenate(parts, axis=-1)


def _layer_params(li, w_in_p, w_out, conv_w, dw_w, small):
    return dict(
        w_in_p=w_in_p, w_out=w_out,
        conv_w=_pad_rows(conv_w, SUBLANES), dw_w=_pad_rows(dw_w, 32),
        norm_w=small["norm_w"][li].reshape(1, -1),
        conv_b=small["ssd_conv_b"][li].reshape(1, -1),
        dtb=_pad_lanes(small["ssd_dt_bias"][li]),
        a_neg=_pad_lanes(-jnp.exp(small["ssd_a_log"][li])),
        d_x=jnp.repeat(small["ssd_d"][li], SSD_HEAD_DIM).reshape(1, -1),
        ssd_norm_w=small["ssd_norm_w"][li].reshape(1, -1),
        sinks=_pad_lanes(small["attn_sinks"][li]),
        dw_b=small["conf_dw_b"][li].reshape(1, -1),
        ln_w=small["conf_ln_w"][li].reshape(1, -1),
        ln_b=small["conf_ln_b"][li].reshape(1, -1),
    )


def _layer_fwd(x, p, nbatch, seq, tag, after=None):
    h, h_t = _rmsnorm_fwd(x, p["norm_w"], name=f"rmsnorm_fwd_{tag}", after=after)
    proj = _matmul(h, p["w_in_p"], "nn", F32, 1024, 512, 1024, name=f"proj_fwd_{tag}")
    u = _conv_fwd(proj, OFF_XBC, SSD_CONV_DIM, p["conv_w"], p["conv_b"], SSD_CONV, seq, name=f"ssd_conv_fwd_{tag}")
    ycat = lax.empty((x.shape[0], MIX_WIDTH), MXU_DTYPE)
    y, states, ycat = _ssd_fwd(u, proj, p["dtb"], p["a_neg"], p["d_x"], p["ssd_norm_w"], ycat, nbatch,
                               name=f"ssd_fwd_{tag}")
    ycat, o, lse = _attn_fwd(proj, p["sinks"], ycat, nbatch, name=f"attn_fwd_{tag}")
    c0 = _glu_fwd(proj, name=f"glu_fwd_{tag}")
    c1 = _conv_fwd(c0, 0, CONF_WIDTH, p["dw_w"], p["dw_b"], CONF_KERNEL, seq, name=f"conf_conv_fwd_{tag}")
    ycat = _conf_post_fwd(c1, proj, p["ln_w"], p["ln_b"], ycat, name=f"conf_post_fwd_{tag}")
    x_new = _matmul(ycat, p["w_out"], "nn", F32, 1024, 512, 2048, name=f"out_fwd_{tag}", residual=x)
    return x_new, dict(x=x, h_t=h_t, proj=proj, u=u, y=y, states=states, o=o, lse=lse, c0=c0, c1=c1, ycat=ycat)


def _layer_bwd(dx_out, p, s, nbatch, seq, tag, hooks=None):
    hooks = hooks or {}
    proj = s["proj"]
    dycat = _matmul(dx_out, p["w_out"], "nt", F32, 1024, 1024, 1024, name=f"out_bwd_dy_{tag}",
                    after=hooks.get("start_token"))
    dw_out = _matmul(s["ycat"], dx_out, "tn", F32, 1024, 1024, 1024, name=f"out_bwd_dw_{tag}")
    token = hooks["after_dycat"](dycat) if "after_dycat" in hooks else None
    dtb = p["dtb"] if token is None else p["dtb"] + token[0, 0]
    dproj = lax.empty(proj.shape, MXU_DTYPE)
    du, dproj, ddt, da_log, dd, ddtb, dssd_norm_w = _ssd_bwd(
        dycat, s["u"], proj, s["y"], s["states"], dtb, p["a_neg"], p["d_x"], p["ssd_norm_w"], dproj,
        nbatch, name=f"ssd_bwd_{tag}")
    dproj, dconv_w, dconv_b = _conv_bwd(du, proj, OFF_XBC, SSD_CONV_DIM, p["conv_w"], SSD_CONV, seq,
                                        name=f"ssd_conv_bwd_{tag}", into=dproj)
    dproj, dsinks = _attn_bwd(dycat, proj, s["o"], s["lse"], p["sinks"], ddt, dproj, nbatch,
                              name=f"attn_bwd_{tag}")
    if "after_attn" in hooks:
        hooks["after_attn"](dproj)
    dc1, dproj, dln_w, dln_b = _conf_post_bwd(dycat, s["c1"], proj, p["ln_w"], p["ln_b"], dproj,
                                              name=f"conf_post_bwd_{tag}")
    dc0, ddw_w, ddw_b = _conv_bwd(dc1, s["c0"], 0, CONF_WIDTH, p["dw_w"], CONF_KERNEL, seq,
                                  name=f"conf_conv_bwd_{tag}")
    dproj = _glu_bwd(dc0, proj, dproj, name=f"glu_bwd_{tag}")
    dh = _matmul(dproj, p["w_in_p"], "nt", F32, 1024, 1024, 1408, name=f"proj_bwd_dh_{tag}")
    dw_in_p = _matmul(s["h_t"], dproj, "nn", F32, 1024, 512, 4096, name=f"proj_bwd_dw_{tag}")
    dx_in, dnorm_w = _rmsnorm_bwd(dh, s["x"], p["norm_w"], dx_out, name=f"rmsnorm_bwd_{tag}")
    grads = dict(
        norm_w=dnorm_w[0], w_in_p=dw_in_p, ssd_conv_w=dconv_w[:SSD_CONV], ssd_conv_b=dconv_b[0],
        ssd_dt_bias=ddtb[0, :SSD_HEADS], ssd_a_log=da_log[0, :SSD_HEADS], ssd_d=dd[0, :SSD_HEADS],
        ssd_norm_w=dssd_norm_w[0], attn_sinks=dsinks[0, :ATTN_Q_HEADS], conf_dw_w=ddw_w[:CONF_KERNEL],
        conf_dw_b=ddw_b[0], conf_ln_w=dln_w[0], conf_ln_b=dln_b[0], w_out=dw_out)
    return dx_in, grads


def _local_step(x, target, param_fns, final_norm_w, first_after=None, on_grads=None):
    nbatch, seq, d = x.shape
    xt = x.reshape(nbatch * seq, d)
    saved, layer_params = [], []
    for li, fn in enumerate(param_fns):
        p = fn(xt)
        layer_params.append(p)
        xt, s = _layer_fwd(xt, p, nbatch, seq, f"l{li}", after=first_after if li == 0 else None)
        saved.append(s)
    loss, dx, dfinal = _loss_head(xt, target.reshape(nbatch * seq, d), final_norm_w.reshape(1, d), name="loss_head")
    grads = [None] * len(layer_params)
    hooks = None
    for li in reversed(range(len(layer_params))):
        dx, grads[li] = _layer_bwd(dx, layer_params[li], saved[li], nbatch, seq, f"l{li}", hooks=hooks)
        hooks = on_grads(li, grads[li]) if on_grads is not None else None
    return loss[0, 0], dx.reshape(nbatch, seq, d), grads, dfinal[0]


MESH = pl.DeviceIdType.MESH
N_CHIPS = 4


def _mesh_pos():
    return lax.axis_index("x"), lax.axis_index("y"), lax.axis_index("c")


def _other_chips(x, y):
    return [(1 - x, y), (x, 1 - y), (1 - x, 1 - y)]


def _gather_weights(big, small, name):
    nbig, nsmall = len(big), len(small)
    n_ici = 3 * (nbig + nsmall)
    n_fwd = 3 * nbig

    def body(*refs):
        ins = refs[:nbig + nsmall]
        outs = refs[nbig + nsmall:2 * (nbig + nsmall)]
        send_sems, recv_sems = refs[2 * (nbig + nsmall):]
        x, y, c = _mesh_pos()
        me = 2 * x + y
        sibling = (x, y, 1 - c)
        chips = _other_chips(x, y)

        def ici(a, j, origin, dest):
            if a < nbig:
                src = ins[a].at[c] if origin is None else outs[a].at[origin, c]
                dst = outs[a].at[me if origin is None else origin, c]
            else:
                src = ins[a] if origin is None else outs[a].at[origin]
                dst = outs[a].at[me if origin is None else origin]
            k = a * 3 + j
            return pltpu.make_async_remote_copy(src_ref=src, dst_ref=dst, send_sem=send_sems.at[k],
                                                recv_sem=recv_sems.at[k], device_id=dest, device_id_type=MESH)

        def fwd(a, j, origin, half):
            k = n_ici + a * 3 + j
            ref = outs[a].at[origin, half]
            return pltpu.make_async_remote_copy(src_ref=ref, dst_ref=ref, send_sem=send_sems.at[k],
                                                recv_sem=recv_sems.at[k], device_id=sibling, device_id_type=MESH)

        sends = []
        for j, (px, py) in enumerate(chips):
            for a in range(nbig + nsmall):
                cp = ici(a, j, None, (px, py, c))
                cp.start()
                sends.append(cp)
        for j, (px, py) in enumerate(chips):
            origin = 2 * px + py
            for a in range(nbig):
                ici(a, j, origin, (px, py, c)).wait_recv()
                cp = fwd(a, j, origin, c)
                cp.start()
                sends.append(cp)
        for j, (px, py) in enumerate(chips):
            origin = 2 * px + py
            for a in range(nbig, nbig + nsmall):
                ici(a, j, origin, (px, py, c)).wait_recv()
            for a in range(nbig):
                fwd(a, j, origin, 1 - c).wait_recv()
        for cp in sends:
            cp.wait_send()

    out_shape = tuple(jax.ShapeDtypeStruct((N_CHIPS,) + a.shape, a.dtype) for a in list(big) + list(small))
    return pl.pallas_call(
        body, name=name, out_shape=out_shape,
        in_specs=[ANY] * (nbig + nsmall), out_specs=tuple([ANY] * (nbig + nsmall)),
        scratch_shapes=[pltpu.SemaphoreType.DMA((n_ici + n_fwd,)), pltpu.SemaphoreType.DMA((n_ici + n_fwd,))],
    )(*big, *small)


def _pair_swap_halves(arrs, name):
    n = len(arrs)

    def body(*refs):
        ins, outs = refs[:n], refs[n:2 * n]
        send_sems, recv_sems = refs[2 * n:]
        x, y, c = _mesh_pos()
        cps = [pltpu.make_async_remote_copy(src_ref=ins[a].at[1 - c], dst_ref=outs[a], send_sem=send_sems.at[a],
                                            recv_sem=recv_sems.at[a], device_id=(x, y, 1 - c), device_id_type=MESH)
               for a in range(n)]
        for cp in cps:
            cp.start()
        for cp in cps:
            cp.wait()

    return pl.pallas_call(
        body, name=name, out_shape=tuple(jax.ShapeDtypeStruct(a.shape[1:], a.dtype) for a in arrs),
        in_specs=[ANY] * n, out_specs=tuple([ANY] * n),
        scratch_shapes=[pltpu.SemaphoreType.DMA((n,)), pltpu.SemaphoreType.DMA((n,))],
    )(*arrs)


def _chip_scatter(arrs, name):
    n = len(arrs)

    def body(*refs):
        ins, outs = refs[:n], refs[n:2 * n]
        send_sems, recv_sems = refs[2 * n:]
        x, y, c = _mesh_pos()
        me = 2 * x + y
        cps = []
        for j, (px, py) in enumerate(_other_chips(x, y)):
            for a in range(n):
                cps.append(pltpu.make_async_remote_copy(
                    src_ref=ins[a].at[2 * px + py], dst_ref=outs[a].at[me], send_sem=send_sems.at[a * 3 + j],
                    recv_sem=recv_sems.at[a * 3 + j], device_id=(px, py, c), device_id_type=MESH))
        for cp in cps:
            cp.start()
        for cp in cps:
            cp.wait()

    return pl.pallas_call(
        body, name=name, out_shape=tuple(jax.ShapeDtypeStruct(a.shape, a.dtype) for a in arrs),
        in_specs=[ANY] * n, out_specs=tuple([ANY] * n),
        scratch_shapes=[pltpu.SemaphoreType.DMA((3 * n,)), pltpu.SemaphoreType.DMA((3 * n,))],
    )(*arrs)


HBM = pl.BlockSpec(memory_space=pltpu.HBM)
SEM = pl.BlockSpec(memory_space=pltpu.SEMAPHORE)
DATAFLOW = pltpu.SideEffectType.DATAFLOW_SIDE_EFFECTING


def _split_peers(pattern, x, y, c):
    if pattern == "swap":
        return [((x, y, 1 - c), 1 - c, None, None)]
    me = 2 * x + y
    return [((px, py, c), 2 * px + py if pattern == "scatter" else None, me, 2 * px + py)
            for px, py in _other_chips(x, y)]


def _split_land_shape(pattern, shape):
    return {"bcast": (N_CHIPS,) + shape, "scatter": shape, "swap": shape[1:]}[pattern]


def _split_copies(pattern, srcs, lands, send_sems, recv_sems, waiting):
    x, y, c = _mesh_pos()
    peers = _split_peers(pattern, x, y, c)
    cps = []
    for j, (dev, src_slot, dst_slot, my_slot) in enumerate(peers):
        for a in range(len(srcs)):
            src = srcs[a] if src_slot is None else srcs[a].at[src_slot]
            slot = my_slot if waiting else dst_slot
            dst = lands[a] if slot is None else lands[a].at[slot]
            k = a * len(peers) + j
            cps.append(pltpu.make_async_remote_copy(src_ref=src, dst_ref=dst, send_sem=send_sems[k],
                                                    recv_sem=recv_sems[k], device_id=dev, device_id_type=MESH))
    return cps


def _split_start(arrs, pattern, after, name):
    n = len(arrs)
    nsem = n * (1 if pattern == "swap" else N_CHIPS - 1)

    def body(*refs):
        srcs, lands = refs[:n], refs[n:2 * n]
        outs = refs[2 * n + 1:]
        for cp in _split_copies(pattern, srcs, lands, outs[:nsem], outs[nsem:2 * nsem], waiting=False):
            cp.start()
        outs[-1][...] = jnp.zeros_like(outs[-1])

    lands = [lax.empty(_split_land_shape(pattern, a.shape), a.dtype) for a in arrs]
    out_shape = ([pltpu.SemaphoreType.DMA(())] * (2 * nsem)
                 + [pltpu.HBM(a.shape, a.dtype) for a in arrs] + [pltpu.HBM(b.shape, b.dtype) for b in lands]
                 + [jax.ShapeDtypeStruct((SUBLANES, LANES), F32)])
    outs = pl.pallas_call(
        body, name=name, out_shape=tuple(out_shape),
        in_specs=[HBM] * (2 * n) + [ANY],
        out_specs=tuple([SEM] * (2 * nsem) + [HBM] * (2 * n) + [pl.BlockSpec(memory_space=pltpu.VMEM)]),
        input_output_aliases={a: 2 * nsem + a for a in range(2 * n)},
        compiler_params=pltpu.CompilerParams(has_side_effects=DATAFLOW),
    )(*[pltpu.with_memory_space_constraint(a, pltpu.HBM) for a in list(arrs) + lands], after)
    return outs[:-1], outs[-1]


def _split_wait(state, n, pattern, after, name):
    nsem = n * (1 if pattern == "swap" else N_CHIPS - 1)

    def body(*refs):
        srcs, lands = refs[:n], refs[n:2 * n]
        send_sems, recv_sems = refs[2 * n:2 * n + nsem], refs[2 * n + nsem:2 * n + 2 * nsem]
        for cp in _split_copies(pattern, srcs, lands, send_sems, recv_sems, waiting=True):
            cp.wait_send()
            cp.wait_recv()

    sems, thru = state[:2 * nsem], state[2 * nsem:]
    outs = pl.pallas_call(
        body, name=name, out_shape=tuple(pltpu.HBM(a.shape, a.dtype) for a in thru),
        in_specs=[HBM] * (2 * n) + [SEM] * (2 * nsem) + [ANY],
        out_specs=tuple([HBM] * (2 * n)),
        input_output_aliases={a: a for a in range(2 * n)},
        compiler_params=pltpu.CompilerParams(has_side_effects=DATAFLOW),
    )(*thru, *sems, after)
    return outs[n:]


def _pair_gather(arrs, name):
    n = len(arrs)

    def body(*refs):
        outs = refs[n:2 * n]
        send_sems, recv_sems = refs[2 * n:]
        x, y, c = _mesh_pos()
        cps = [pltpu.make_async_remote_copy(src_ref=outs[a].at[c], dst_ref=outs[a].at[c], send_sem=send_sems.at[a],
                                            recv_sem=recv_sems.at[a], device_id=(x, y, 1 - c), device_id_type=MESH)
               for a in range(n)]
        for cp in cps:
            cp.start()
        for cp in cps:
            cp.wait()

    return pl.pallas_call(
        body, name=name, out_shape=tuple(jax.ShapeDtypeStruct(a.shape, a.dtype) for a in arrs),
        in_specs=[ANY] * n, out_specs=tuple([ANY] * n),
        input_output_aliases={a: a for a in range(n)},
        scratch_shapes=[pltpu.SemaphoreType.DMA((n,)), pltpu.SemaphoreType.DMA((n,))],
    )(*arrs)


N_DEV = 8


def _allreduce_small(pack, name):
    r = pack.shape[0]

    def body(p_ref, o_ref, land, send_sems, recv_sems):
        x, y, c = _mesh_pos()
        me = 4 * x + 2 * y + c
        cps = []
        for k in range(1, N_DEV):
            peer = (x ^ (k >> 2), y ^ ((k >> 1) & 1), c ^ (k & 1))
            cps.append(pltpu.make_async_remote_copy(src_ref=p_ref, dst_ref=land.at[me], send_sem=send_sems.at[k - 1],
                                                    recv_sem=recv_sems.at[k - 1], device_id=peer, device_id_type=MESH))
        for cp in cps:
            cp.start()
        land[me] = p_ref[...]
        for cp in cps:
            cp.wait()
        total = land[0]
        for d in range(1, N_DEV):
            total = total + land[d]
        o_ref[...] = total

    vm = pl.BlockSpec(memory_space=pltpu.VMEM)
    return pl.pallas_call(
        body, name=name, out_shape=jax.ShapeDtypeStruct(pack.shape, F32),
        in_specs=[vm], out_specs=vm,
        scratch_shapes=[pltpu.VMEM((N_DEV, r, LANES), F32), pltpu.SemaphoreType.DMA((N_DEV - 1,)),
                        pltpu.SemaphoreType.DMA((N_DEV - 1,))],
    )(pack)


BIG_ROWS = 128


def _cast_layer(w, layer, name):
    _, r, cdim = w.shape
    tr = BIG_ROWS

    def body(w_ref, o_ref):
        o_ref[...] = w_ref[...].astype(o_ref.dtype)

    return pl.pallas_call(
        body, name=name, out_shape=jax.ShapeDtypeStruct((r, cdim), MXU_DTYPE),
        grid=(r // tr,), in_specs=[pl.BlockSpec((None, tr, cdim), lambda i: (layer, i, 0))],
        out_specs=pl.BlockSpec((tr, cdim), lambda i: (i, 0)),
        compiler_params=_params(("parallel",)),
    )(w)


def _pair_sum(parts, sib, which, out_dtype, name):
    _, k, r, cdim = parts.shape
    tr = BIG_ROWS

    def body(sel_ref, p_ref, s_ref, o_ref):
        o_ref[...] = (p_ref[...] + s_ref[...]).astype(o_ref.dtype)

    grid_spec = pltpu.PrefetchScalarGridSpec(
        num_scalar_prefetch=1, grid=(k, r // tr),
        in_specs=[pl.BlockSpec((None, None, tr, cdim), lambda l, i, sel: (sel[0], l, i, 0)),
                  pl.BlockSpec((None, tr, cdim), lambda l, i, sel: (l, i, 0))],
        out_specs=pl.BlockSpec((None, tr, cdim), lambda l, i, sel: (l, i, 0)))
    return pl.pallas_call(
        body, name=name, out_shape=jax.ShapeDtypeStruct((k, r, cdim), out_dtype), grid_spec=grid_spec,
        compiler_params=_params(("parallel", "parallel")),
    )(which.reshape(1).astype(jnp.int32), parts, sib)


def _sum_lead(parts, which, name):
    k, r, cdim = parts.shape
    tr = BIG_ROWS

    def body(sel_ref, p_ref, o_ref):
        total = p_ref[0].astype(F32)
        for a in range(1, k):
            total = total + p_ref[a].astype(F32)
        o_ref[...] = total

    grid_spec = pltpu.PrefetchScalarGridSpec(
        num_scalar_prefetch=1, grid=(r // tr,),
        in_specs=[pl.BlockSpec((k, tr, cdim), lambda i, sel: (0, i, 0))],
        out_specs=pl.BlockSpec((None, tr, cdim), lambda i, sel: (sel[0], i, 0)))
    return pl.pallas_call(
        body, name=name, out_shape=jax.ShapeDtypeStruct((2, r, cdim), F32), grid_spec=grid_spec,
        compiler_params=_params(("parallel",)),
    )(which.reshape(1).astype(jnp.int32), parts)


def _adam_math(w, g, m, v):
    m2 = ADAM_B1 * m + (1.0 - ADAM_B1) * g
    v2 = ADAM_B2 * v + (1.0 - ADAM_B2) * (g * g)
    m_hat = m2 / (1.0 - ADAM_B1 ** ADAM_STEP)
    v_hat = v2 / (1.0 - ADAM_B2 ** ADAM_STEP)
    delta = -ADAM_LR * (m_hat / (jnp.sqrt(v_hat) + ADAM_EPS) + ADAM_WD * w)
    return delta, m2, v2


def _adam_big(w, g, m, v, name):
    nl, r, cdim = w.shape
    tr = BIG_ROWS

    def body(w_ref, g_ref, m_ref, v_ref, d_ref, mo_ref, vo_ref):
        delta, m2, v2 = _adam_math(w_ref[...], g_ref[...], m_ref[...], v_ref[...])
        d_ref[...] = delta
        mo_ref[...] = m2
        vo_ref[...] = v2

    blk = pl.BlockSpec((None, tr, cdim), lambda l, i: (l, i, 0))
    shp = jax.ShapeDtypeStruct(w.shape, F32)
    return pl.pallas_call(
        body, name=name, out_shape=(shp, shp, shp),
        grid=(nl, r // tr), in_specs=[blk] * 4, out_specs=(blk, blk, blk),
        compiler_params=_params(("parallel", "parallel")),
    )(w, g, m, v)


def _adam_cols_major(w, g, m, v, name):
    cdim, nl, r = w.shape
    tc = BIG_ROWS

    def body(w_ref, g_ref, m_ref, v_ref, d_ref, mo_ref, vo_ref):
        delta, m2, v2 = _adam_math(w_ref[...], g_ref[...], m_ref[...], v_ref[...])
        d_ref[...] = delta
        mo_ref[...] = m2
        vo_ref[...] = v2

    blk = pl.BlockSpec((tc, nl, r), lambda i: (i, 0, 0))
    shp = jax.ShapeDtypeStruct(w.shape, F32)
    return pl.pallas_call(
        body, name=name, out_shape=(shp, shp, shp),
        grid=(pl.cdiv(cdim, tc),), in_specs=[blk] * 4, out_specs=(blk, blk, blk),
        compiler_params=_params(("parallel",)),
    )(w, g, m, v)


def _adam_small(ws, gs, ms, vs, name):
    n = len(ws)

    def body(*refs):
        w_refs, g_refs, m_refs, v_refs = (refs[k * n:(k + 1) * n] for k in range(4))
        d_refs, mo_refs, vo_refs = (refs[(4 + k) * n:(5 + k) * n] for k in range(3))
        for a in range(n):
            delta, m2, v2 = _adam_math(w_refs[a][...], g_refs[a][...], m_refs[a][...], v_refs[a][...])
            d_refs[a][...] = delta
            mo_refs[a][...] = m2
            vo_refs[a][...] = v2

    shapes = tuple(jax.ShapeDtypeStruct(w.shape, F32) for w in ws)
    vm = pl.BlockSpec(memory_space=pltpu.VMEM)
    outs = pl.pallas_call(body, name=name, out_shape=shapes * 3, in_specs=[vm] * (4 * n),
                          out_specs=tuple([vm] * (3 * n)))(*ws, *gs, *ms, *vs)
    return outs[:n], outs[n:2 * n], outs[2 * n:]


PACK_TILE = SUBLANES * LANES


def _pack(arrays):
    rows = []
    for a in arrays:
        flat = a.reshape(-1)
        pad = (-flat.shape[0]) % PACK_TILE
        if pad:
            flat = jnp.concatenate([flat, jnp.zeros((pad,), flat.dtype)])
        rows.append(flat.reshape(-1, LANES))
    return jnp.concatenate(rows, axis=0)


def _unpack(pack, shapes):
    outs, row = [], 0
    for shp in shapes:
        n = int(np.prod(shp))
        nrows = -(-n // PACK_TILE) * SUBLANES
        outs.append(pack[row:row + nrows].reshape(-1)[:n].reshape(shp))
        row += nrows
    return outs


SMALL = ["norm_w", "ssd_conv_b", "ssd_dt_bias", "ssd_a_log", "ssd_d", "ssd_norm_w", "attn_sinks",
         "conf_dw_b", "conf_ln_w", "conf_ln_b"]
WEIGHTS = ["norm_w", "w_in", "ssd_conv_w", "ssd_conv_b", "ssd_dt_bias", "ssd_a_log", "ssd_d", "ssd_norm_w",
           "attn_sinks", "conf_dw_w", "conf_dw_b", "conf_ln_w", "conf_ln_b", "w_out", "final_norm_w"]


def kernel(x, norm_w, w_in, ssd_conv_w, ssd_conv_b, ssd_dt_bias, ssd_a_log, ssd_d, ssd_norm_w, attn_sinks, conf_dw_w, conf_dw_b, conf_ln_w, conf_ln_b, w_out, final_norm_w, loss_target, m_norm_w, m_w_in, m_ssd_conv_w, m_ssd_conv_b, m_ssd_dt_bias, m_ssd_a_log, m_ssd_d, m_ssd_norm_w, m_attn_sinks, m_conf_dw_w, m_conf_dw_b, m_conf_ln_w, m_conf_ln_b, m_w_out, m_final_norm_w, v_norm_w, v_w_in, v_ssd_conv_w, v_ssd_conv_b, v_ssd_dt_bias, v_ssd_a_log, v_ssd_d, v_ssd_norm_w, v_attn_sinks, v_conf_dw_w, v_conf_dw_b, v_conf_ln_w, v_conf_ln_b, v_w_out, v_final_norm_w):
    w = dict(norm_w=norm_w, w_in=w_in, ssd_conv_w=ssd_conv_w, ssd_conv_b=ssd_conv_b, ssd_dt_bias=ssd_dt_bias,
             ssd_a_log=ssd_a_log, ssd_d=ssd_d, ssd_norm_w=ssd_norm_w, attn_sinks=attn_sinks, conf_dw_w=conf_dw_w,
             conf_dw_b=conf_dw_b, conf_ln_w=conf_ln_w, conf_ln_b=conf_ln_b, w_out=w_out, final_norm_w=final_norm_w)
    m = dict(norm_w=m_norm_w, w_in=m_w_in, ssd_conv_w=m_ssd_conv_w, ssd_conv_b=m_ssd_conv_b,
             ssd_dt_bias=m_ssd_dt_bias, ssd_a_log=m_ssd_a_log, ssd_d=m_ssd_d, ssd_norm_w=m_ssd_norm_w,
             attn_sinks=m_attn_sinks, conf_dw_w=m_conf_dw_w, conf_dw_b=m_conf_dw_b, conf_ln_w=m_conf_ln_w,
             conf_ln_b=m_conf_ln_b, w_out=m_w_out, final_norm_w=m_final_norm_w)
    v = dict(norm_w=v_norm_w, w_in=v_w_in, ssd_conv_w=v_ssd_conv_w, ssd_conv_b=v_ssd_conv_b,
             ssd_dt_bias=v_ssd_dt_bias, ssd_a_log=v_ssd_a_log, ssd_d=v_ssd_d, ssd_norm_w=v_ssd_norm_w,
             attn_sinks=v_attn_sinks, conf_dw_w=v_conf_dw_w, conf_dw_b=v_conf_dw_b, conf_ln_w=v_conf_ln_w,
             conf_ln_b=v_conf_ln_b, w_out=v_w_out, final_norm_w=v_final_norm_w)
    depth = w_in.shape[0]
    me = 2 * lax.axis_index("x") + lax.axis_index("y")

    assert depth == 2
    w_in_b = [_cast_layer(w_in, li, name=f"cast_w_in_l{li}") for li in range(depth)]
    w_out_b = [_cast_layer(w_out, li, name=f"cast_w_out_l{li}") for li in range(depth)]
    own0 = [w_in_b[0].reshape((2, -1) + w_in_b[0].shape[1:]), w_out_b[0].reshape((2, -1) + w_out_b[0].shape[1:]),
            ssd_conv_w, conf_dw_w]
    gathered0 = _gather_weights(own0[:2], own0[2:], name="gather_weights_l0")
    g_in0, g_out0, g_conv, g_dw = [lax.dynamic_update_index_in_dim(g_all, mine, me, 0)
                                   for g_all, mine in zip(gathered0, own0)]
    own1 = [w_in_b[1], w_out_b[1]]
    pending1, token1 = _split_start(own1, "bcast", gathered0[0], name="gather_l1_start")

    def small_full(li):
        return (jnp.concatenate([g_conv[p, li] for p in range(N_CHIPS)], axis=1),
                jnp.concatenate([g_dw[p, li] for p in range(N_CHIPS)], axis=1))

    def params_l0(_):
        w_in_p = _padded_from_chips([g_in0[p].reshape(w_in_b[0].shape) for p in range(N_CHIPS)])
        w_out_full = g_out0.reshape(-1, g_out0.shape[-1])
        return _layer_params(0, w_in_p, w_out_full, *small_full(0), w)

    def params_l1(layer_input):
        landed = _split_wait(pending1, len(own1), "bcast", layer_input, name="gather_l1_wait")
        g_in1, g_out1 = [lax.dynamic_update_index_in_dim(g_all, mine, me, 0) for g_all, mine in zip(landed, own1)]
        w_in_p = _padded_from_chips([g_in1[p] for p in range(N_CHIPS)])
        return _layer_params(1, w_in_p, g_out1.reshape(-1, g_out1.shape[-1]), *small_full(1), w)

    c = lax.axis_index("c")
    cols = w_in.shape[2]
    rows_out = w_out.shape[1]

    def grad_parts(g):
        dw = g["w_in_p"]
        half = dw.shape[0] // 2
        p_in = jnp.stack([jnp.stack([_chip_part_from_padded(dw[h * half:(h + 1) * half], p, cols)
                                     for p in range(N_CHIPS)]) for h in range(2)])
        p_out = g["w_out"].reshape(N_CHIPS, 2, rows_out // 2, D_MODEL).transpose(1, 0, 2, 3)
        return [p_in, p_out]

    def pair_sums(parts, sib, tag):
        return [_pair_sum(p, sb, c, MXU_DTYPE, name=f"grad_pair_sum_{k}_{tag}")
                for k, (p, sb) in enumerate(zip(parts, sib))]

    def chip_sums(landed, sent, tag):
        filled = [lax.dynamic_update_index_in_dim(r, lax.dynamic_index_in_dim(sk, me, 0, keepdims=False), me, 0)
                  for r, sk in zip(landed, sent)]
        halves = [_sum_lead(r, c, name=f"grad_chip_sum_{k}_{tag}") for k, r in enumerate(filled)]
        return _pair_gather(halves, name=f"grad_pair_gather_{tag}")

    split = {}

    def on_grads(li, g):
        if li != depth - 1:
            return None
        parts = grad_parts(g)
        swap_state, swap_token = _split_start(parts, "swap", g["w_out"], name="grad_swap_l1_start")

        def after_dycat(dycat):
            sib = _split_wait(swap_state, len(parts), "swap", dycat, name="grad_swap_l1_wait")
            split["sent"] = pair_sums(parts, sib, "l1")
            split["scatter"], token = _split_start(split["sent"], "scatter", split["sent"][0],
                                                   name="grad_scatter_l1_start")
            return token

        def after_attn(dproj):
            landed = _split_wait(split["scatter"], len(parts), "scatter", dproj, name="grad_scatter_l1_wait")
            split["reduced"] = chip_sums(landed, split["sent"], "l1")

        return {"start_token": swap_token, "after_dycat": after_dycat, "after_attn": after_attn}

    loss, grad_x, grads, dfinal = _local_step(x, loss_target, [params_l0, params_l1], final_norm_w,
                                              first_after=token1, on_grads=on_grads)

    small_list = [grads[li][n] for li in range(depth) for n in SMALL]
    small_list += [grads[li][n] for li in range(depth) for n in ("ssd_conv_w", "conf_dw_w")]
    small_list += [dfinal, loss.reshape(1)]
    small_shapes = [a.shape for a in small_list]
    reduced = _unpack(_allreduce_small(_pack(small_list), name="allreduce_small"), small_shapes)
    ns = len(SMALL)
    g = {n: jnp.stack([reduced[li * ns + i] for li in range(depth)]) for i, n in enumerate(SMALL)}
    conv_w_cols, dw_w_cols = ssd_conv_w.shape[2], conf_dw_w.shape[2]
    g["ssd_conv_w"] = jnp.stack([lax.dynamic_slice_in_dim(reduced[depth * ns + 2 * li], me * conv_w_cols,
                                                          conv_w_cols, axis=1) for li in range(depth)])
    g["conf_dw_w"] = jnp.stack([lax.dynamic_slice_in_dim(reduced[depth * ns + 2 * li + 1], me * dw_w_cols,
                                                         dw_w_cols, axis=1) for li in range(depth)])
    g["final_norm_w"] = reduced[-2]
    loss_total = reduced[-1][0]

    parts0 = grad_parts(grads[0])
    sent0 = pair_sums(parts0, _pair_swap_halves(parts0, name="grad_pair_swap_l0"), "l0")
    reduced = [chip_sums(_chip_scatter(sent0, name="grad_chip_scatter_l0"), sent0, "l0"), split["reduced"]]
    g_w_in = jnp.stack([r[0].reshape(w_in.shape[1:]) for r in reduced])
    g_w_out = jnp.stack([r[1].reshape(w_out.shape[1:]) for r in reduced])

    outs_g, outs_d, outs_m, outs_v = {"w_in": g_w_in, "w_out": g_w_out}, {}, {}, {}
    to_cols, from_cols = (2, 0, 1), (1, 2, 0)
    outs_d["w_in"], outs_m["w_in"], outs_v["w_in"] = [
        jnp.transpose(a, from_cols) for a in _adam_cols_major(
            *[jnp.transpose(a, to_cols) for a in (w_in, g_w_in, m_w_in, v_w_in)], name="adam_w_in")]
    outs_d["w_out"], outs_m["w_out"], outs_v["w_out"] = _adam_big(w_out, g_w_out, m_w_out, v_w_out,
                                                                  name="adam_w_out")
    small_names = [n for n in WEIGHTS if n not in ("w_in", "w_out")]
    def as2d(a):
        return a.reshape(1, -1) if a.ndim == 1 else a

    deltas, new_ms, new_vs = _adam_small(*[[as2d(src[n]) for n in small_names] for src in (w, g, m, v)],
                                         name="adam_small")
    for n, dn, mn, vn in zip(small_names, deltas, new_ms, new_vs):
        outs_g[n], outs_d[n], outs_m[n], outs_v[n] = (g[n], dn.reshape(w[n].shape), mn.reshape(w[n].shape),
                                                      vn.reshape(w[n].shape))
    return (loss_total, grad_x, *[outs_g[n] for n in WEIGHTS], *[outs_d[n] for n in WEIGHTS],
            *[outs_m[n] for n in WEIGHTS], *[outs_v[n] for n in WEIGHTS])
```

```python
import functools
import math

import jax
import jax.numpy as jnp
import numpy as np
from jax import lax
from jax.experimental import pallas as pl
from jax.experimental.pallas import tpu as pltpu

F32 = jnp.float32
BF16 = jnp.bfloat16
MXU_DTYPE = BF16

D_MODEL = 1024
DEPTH = 2
SSD_HEADS = 16
SSD_HEAD_DIM = 64
SSD_STATE = 128
SSD_CONV = 4
CHUNK = 128
SSD_CONV_DIM = 1536
ATTN_HEAD_DIM = 64
ATTN_Q_HEADS = 8
WINDOW = 128
CONF_WIDTH = 512
CONF_KERNEL = 31
MIX_WIDTH = 2048
D_IN_PROJ = 5392
EPS = 1e-5

ADAM_LR = 0.001
ADAM_B1 = 0.9
ADAM_B2 = 0.999
ADAM_EPS = 1e-08
ADAM_WD = 0.01
ADAM_STEP = 10

LANES = 128
SUBLANES = 8
VMEM_LIMIT = 48 * 1024 * 1024

NP = 5632
OFF_ZA, OFF_Q, OFF_K, OFF_V, OFF_DT = 0, 512, 1024, 1152, 1280
ATTN_GROUP = 1536
OFF_XBC = 1536
OFF_CONF = 3072
OFF_ZS = 4096
OFF_ZC = 5120
SECTIONS = ((0, 1024, OFF_ZS), (1024, 1536, OFF_ZA), (1536, 2048, OFF_ZC), (2048, 3584, OFF_XBC),
            (3584, 3600, OFF_DT), (3600, 4368, OFF_Q), (4368, 5392, OFF_CONF))

YCAT_ATTN, YCAT_CONF = 1024, 1536
ANY = pl.BlockSpec(memory_space=pl.ANY)

NN = (((1,), (0,)), ((), ()))
NT = (((1,), (1,)), ((), ()))
TN = (((0,), (0,)), ((), ()))


def _params(sem):
    return pltpu.CompilerParams(dimension_semantics=sem, vmem_limit_bytes=VMEM_LIMIT)


def _dot(a, b, dims=NN):
    return lax.dot_general(a.astype(MXU_DTYPE), b.astype(MXU_DTYPE), dims, preferred_element_type=F32)


def _split_bf16(a, passes):
    pieces = []
    r = a
    for _ in range(passes):
        p = r.astype(BF16)
        pieces.append(p)
        r = r - p.astype(F32)
    return pieces


def _xdot(a, sel, dims=NN, passes=2):
    out = None
    for p in _split_bf16(a, passes):
        t = lax.dot_general(p, sel, dims, preferred_element_type=F32)
        out = t if out is None else out + t
    return out


def _xdot_r(sel, b, dims=NN, passes=3):
    out = None
    for p in _split_bf16(b, passes):
        t = lax.dot_general(sel, p, dims, preferred_element_type=F32)
        out = t if out is None else out + t
    return out


def _sigmoid(x):
    return 1.0 / (1.0 + jnp.exp(-x))


def _silu(x):
    return x * _sigmoid(x)


def _dsilu(x):
    s = _sigmoid(x)
    return s * (1.0 + x * (1.0 - s))


def _softplus(x):
    return jnp.maximum(x, 0.0) + jnp.log(1.0 + jnp.exp(-jnp.abs(x)))


def _rowsum8(x):
    r, c = x.shape
    return jnp.sum(x.reshape(r // SUBLANES, SUBLANES, c), axis=0)


def _iota(shape, dim):
    return lax.broadcasted_iota(jnp.int32, shape, dim)


def _matmul(a, b, form, out_dtype, tm, tn, tk, name, residual=None, after=None):
    if form == "nn":
        (m, k), n = a.shape, b.shape[1]
    elif form == "nt":
        (m, k), n = a.shape, b.shape[0]
    else:
        (k, m), n = a.shape, b.shape[1]
    tm, tn, tk = min(tm, m), min(tn, n), min(tk, k)
    assert m % tm == 0 and n % tn == 0 and k % tk == 0, (name, m, n, k, tm, tn, tk)
    if form == "nn":
        a_spec = pl.BlockSpec((tm, tk), lambda i, j, s: (i, s))
        b_spec = pl.BlockSpec((tk, tn), lambda i, j, s: (s, j))
        dims = NN
    elif form == "nt":
        (m, k), n = a.shape, b.shape[0]
        a_spec = pl.BlockSpec((tm, tk), lambda i, j, s: (i, s))
        b_spec = pl.BlockSpec((tn, tk), lambda i, j, s: (j, s))
        dims = NT
    else:
        (k, m), n = a.shape, b.shape[1]
        a_spec = pl.BlockSpec((tk, tm), lambda i, j, s: (s, i))
        b_spec = pl.BlockSpec((tk, tn), lambda i, j, s: (s, j))
        dims = TN
    nk = k // tk
    has_res = residual is not None
    deps = [] if after is None else [after]

    def body_single(a_ref, b_ref, *rest):
        o = _dot(a_ref[...], b_ref[...], dims)
        if has_res:
            o = o + rest[0][...]
        rest[-1][...] = o.astype(out_dtype)

    def body(a_ref, b_ref, *rest):
        r_ref = rest[0] if has_res else None
        o_ref, acc = rest[-2:]
        s = pl.program_id(2)

        @pl.when(s == 0)
        def _():
            acc[...] = jnp.zeros_like(acc)

        acc[...] += _dot(a_ref[...], b_ref[...], dims)

        @pl.when(s == nk - 1)
        def _():
            o = acc[...]
            if has_res:
                o = o + r_ref[...]
            o_ref[...] = o.astype(out_dtype)

    in_specs = [a_spec, b_spec]
    args = [a, b]
    if has_res:
        in_specs.append(pl.BlockSpec((tm, tn), lambda i, j, s: (i, j)))
        args.append(residual)
    in_specs += [ANY] * len(deps)
    args += deps
    return pl.pallas_call(
        body_single if nk == 1 else body, name=name,
        out_shape=jax.ShapeDtypeStruct((m, n), out_dtype),
        grid=(m // tm, n // tn, nk),
        in_specs=in_specs,
        out_specs=pl.BlockSpec((tm, tn), lambda i, j, s: (i, j)),
        scratch_shapes=[] if nk == 1 else [pltpu.VMEM((tm, tn), F32)],
        compiler_params=_params(("parallel", "parallel", "arbitrary")),
    )(*args)


ROW_TILE = 256


def _rmsnorm_fwd(x, w, name, after=None):
    t, d = x.shape
    tm = ROW_TILE
    deps = [] if after is None else [after]

    def body(x_ref, w_ref, *rest):
        o_ref, ot_ref = rest[len(deps):]
        xv = x_ref[...]
        rstd = lax.rsqrt(jnp.mean(xv * xv, axis=-1, keepdims=True) + EPS)
        h = xv * rstd * w_ref[...]
        o_ref[...] = h.astype(o_ref.dtype)
        ot_ref[...] = h.T.astype(ot_ref.dtype)

    return pl.pallas_call(
        body, name=name,
        out_shape=(jax.ShapeDtypeStruct((t, d), MXU_DTYPE), jax.ShapeDtypeStruct((d, t), MXU_DTYPE)),
        grid=(t // tm,),
        in_specs=[pl.BlockSpec((tm, d), lambda i: (i, 0)), pl.BlockSpec((1, d), lambda i: (0, 0))]
        + [ANY] * len(deps),
        out_specs=(pl.BlockSpec((tm, d), lambda i: (i, 0)), pl.BlockSpec((d, tm), lambda i: (0, i))),
        compiler_params=_params(("parallel",)),
    )(x, w, *deps)


def _rmsnorm_bwd(dh, x, w, dres, name):
    t, d = x.shape
    tm = ROW_TILE
    nt = t // tm

    def body(dh_ref, x_ref, w_ref, dr_ref, dx_ref, dw_ref, acc):
        i = pl.program_id(0)

        @pl.when(i == 0)
        def _():
            acc[...] = jnp.zeros_like(acc)

        xv = x_ref[...]
        rstd = lax.rsqrt(jnp.mean(xv * xv, axis=-1, keepdims=True) + EPS)
        xh = xv * rstd
        dhv = dh_ref[...]
        g = dhv * w_ref[...]
        dx_ref[...] = dr_ref[...] + rstd * (g - xh * jnp.mean(g * xh, axis=-1, keepdims=True))
        acc[...] += _rowsum8(dhv * xh)

        @pl.when(i == nt - 1)
        def _():
            dw_ref[...] = jnp.sum(acc[...], axis=0, keepdims=True)

    row = pl.BlockSpec((tm, d), lambda i: (i, 0))
    vec = pl.BlockSpec((1, d), lambda i: (0, 0))
    return pl.pallas_call(
        body, name=name,
        out_shape=(jax.ShapeDtypeStruct((t, d), F32), jax.ShapeDtypeStruct((1, d), F32)),
        grid=(nt,),
        in_specs=[row, row, vec, row],
        out_specs=(row, vec),
        scratch_shapes=[pltpu.VMEM((SUBLANES, d), F32)],
        compiler_params=_params(("arbitrary",)),
    )(dh, x, w, dres)


def _loss_head(xf, target, w, name):
    t, d = xf.shape
    tm = ROW_TILE
    nt = t // tm

    def body(x_ref, t_ref, w_ref, loss_ref, dx_ref, dw_ref, lacc, wacc):
        i = pl.program_id(0)

        @pl.when(i == 0)
        def _():
            lacc[...] = jnp.zeros_like(lacc)
            wacc[...] = jnp.zeros_like(wacc)

        xv = x_ref[...]
        rstd = lax.rsqrt(jnp.mean(xv * xv, axis=-1, keepdims=True) + EPS)
        xh = xv * rstd
        err = xh * w_ref[...] - t_ref[...]
        lacc[...] += jnp.sum(err * err)
        dy = err * (1.0 / d)
        g = dy * w_ref[...]
        dx_ref[...] = rstd * (g - xh * jnp.mean(g * xh, axis=-1, keepdims=True))
        wacc[...] += _rowsum8(dy * xh)

        @pl.when(i == nt - 1)
        def _():
            loss_ref[...] = lacc[...] * (0.5 / d)
            dw_ref[...] = jnp.sum(wacc[...], axis=0, keepdims=True)

    row = pl.BlockSpec((tm, d), lambda i: (i, 0))
    vec = pl.BlockSpec((1, d), lambda i: (0, 0))
    return pl.pallas_call(
        body, name=name,
        out_shape=(jax.ShapeDtypeStruct((SUBLANES, LANES), F32), jax.ShapeDtypeStruct((t, d), F32),
                   jax.ShapeDtypeStruct((1, d), F32)),
        grid=(nt,),
        in_specs=[row, row, vec],
        out_specs=(pl.BlockSpec((SUBLANES, LANES), lambda i: (0, 0)), row, vec),
        scratch_shapes=[pltpu.VMEM((SUBLANES, LANES), F32), pltpu.VMEM((SUBLANES, d), F32)],
        compiler_params=_params(("arbitrary",)),
    )(xf, target, w)


def _glu_fwd(proj, name):
    t = proj.shape[0]
    tm, cw = ROW_TILE, CONF_WIDTH

    def body(a_ref, g_ref, o_ref):
        o_ref[...] = a_ref[...] * _sigmoid(g_ref[...])

    return pl.pallas_call(
        body, name=name,
        out_shape=jax.ShapeDtypeStruct((t, cw), F32),
        grid=(t // tm,),
        in_specs=[pl.BlockSpec((tm, cw), lambda i: (i, OFF_CONF // cw)),
                  pl.BlockSpec((tm, cw), lambda i: (i, OFF_CONF // cw + 1))],
        out_specs=pl.BlockSpec((tm, cw), lambda i: (i, 0)),
        compiler_params=_params(("parallel",)),
    )(proj, proj)


def _glu_bwd(dc0, proj, dproj, name):
    t = proj.shape[0]
    tm, cw = ROW_TILE, CONF_WIDTH

    def body(d_ref, a_ref, g_ref, _, o_ref):
        s = _sigmoid(g_ref[...])
        dv = d_ref[...]
        o_ref[:, :cw] = (dv * s).astype(o_ref.dtype)
        o_ref[:, cw:] = (dv * a_ref[...] * s * (1.0 - s)).astype(o_ref.dtype)

    return pl.pallas_call(
        body, name=name,
        out_shape=jax.ShapeDtypeStruct(dproj.shape, dproj.dtype),
        grid=(t // tm,),
        in_specs=[pl.BlockSpec((tm, cw), lambda i: (i, 0)),
                  pl.BlockSpec((tm, cw), lambda i: (i, OFF_CONF // cw)),
                  pl.BlockSpec((tm, cw), lambda i: (i, OFF_CONF // cw + 1)), ANY],
        out_specs=pl.BlockSpec((tm, 2 * cw), lambda i: (i, OFF_CONF // (2 * cw))),
        input_output_aliases={3: 0},
        compiler_params=_params(("parallel",)),
    )(dc0, proj, proj, dproj)


def _conf_post_fwd(c1, proj, ln_w, ln_b, ycat, name):
    t = c1.shape[0]
    tm, cw = ROW_TILE, CONF_WIDTH

    def body(c_ref, z_ref, w_ref, b_ref, _, o_ref):
        cv = c_ref[...]
        xc = cv - jnp.mean(cv, axis=-1, keepdims=True)
        rstd = lax.rsqrt(jnp.mean(xc * xc, axis=-1, keepdims=True) + EPS)
        c2 = xc * rstd * w_ref[...] + b_ref[...]
        o_ref[...] = (_silu(c2) * _silu(z_ref[...])).astype(o_ref.dtype)

    vec = pl.BlockSpec((1, cw), lambda i: (0, 0))
    return pl.pallas_call(
        body, name=name,
        out_shape=jax.ShapeDtypeStruct(ycat.shape, ycat.dtype),
        grid=(t // tm,),
        in_specs=[pl.BlockSpec((tm, cw), lambda i: (i, 0)),
                  pl.BlockSpec((tm, cw), lambda i: (i, OFF_ZC // cw)), vec, vec, ANY],
        out_specs=pl.BlockSpec((tm, cw), lambda i: (i, YCAT_CONF // cw)),
        input_output_aliases={4: 0},
        compiler_params=_params(("parallel",)),
    )(c1, proj, ln_w, ln_b, ycat)


def _conf_post_bwd(dycat, c1, proj, ln_w, ln_b, dproj, name):
    t = c1.shape[0]
    tm, cw = ROW_TILE, CONF_WIDTH
    nt = t // tm

    def body(dy_ref, c_ref, z_ref, w_ref, b_ref, _, dc_ref, dz_ref, dw_ref, db_ref, wacc, bacc):
        i = pl.program_id(0)

        @pl.when(i == 0)
        def _():
            wacc[...] = jnp.zeros_like(wacc)
            bacc[...] = jnp.zeros_like(bacc)

        cv = c_ref[...]
        xc = cv - jnp.mean(cv, axis=-1, keepdims=True)
        rstd = lax.rsqrt(jnp.mean(xc * xc, axis=-1, keepdims=True) + EPS)
        xh = xc * rstd
        c2 = xh * w_ref[...] + b_ref[...]
        zv = z_ref[...]
        dy = dy_ref[...]
        dz_ref[...] = (dy * _silu(c2) * _dsilu(zv)).astype(dz_ref.dtype)
        dc2 = dy * _silu(zv) * _dsilu(c2)
        bacc[...] += _rowsum8(dc2)
        wacc[...] += _rowsum8(dc2 * xh)
        dxh = dc2 * w_ref[...]
        dc_ref[...] = rstd * (dxh - jnp.mean(dxh, axis=-1, keepdims=True)
                              - xh * jnp.mean(dxh * xh, axis=-1, keepdims=True))

        @pl.when(i == nt - 1)
        def _():
            dw_ref[...] = jnp.sum(wacc[...], axis=0, keepdims=True)
            db_ref[...] = jnp.sum(bacc[...], axis=0, keepdims=True)

    row = pl.BlockSpec((tm, cw), lambda i: (i, 0))
    vec = pl.BlockSpec((1, cw), lambda i: (0, 0))
    return pl.pallas_call(
        body, name=name,
        out_shape=(jax.ShapeDtypeStruct((t, cw), F32), jax.ShapeDtypeStruct(dproj.shape, dproj.dtype),
                   jax.ShapeDtypeStruct((1, cw), F32), jax.ShapeDtypeStruct((1, cw), F32)),
        grid=(nt,),
        in_specs=[pl.BlockSpec((tm, cw), lambda i: (i, YCAT_CONF // cw)), row,
                  pl.BlockSpec((tm, cw), lambda i: (i, OFF_ZC // cw)), vec, vec, ANY],
        out_specs=(row, pl.BlockSpec((tm, cw), lambda i: (i, OFF_ZC // cw)), vec, vec),
        input_output_aliases={5: 1},
        scratch_shapes=[pltpu.VMEM((SUBLANES, cw), F32), pltpu.VMEM((SUBLANES, cw), F32)],
        compiler_params=_params(("arbitrary",)),
    )(dycat, c1, proj, ln_w, ln_b, dproj)


CONV_TILE = 512
CONV_COLS = 512
CONV_SUB_ROWS = 128
CONV_SUB_COLS = LANES


def _conv_halo(k):
    return SUBLANES if k - 1 <= SUBLANES else 32


def _conv_subtiles(tm, cw):
    return [(r0, c0) for r0 in range(0, tm, CONV_SUB_ROWS) for c0 in range(0, cw, CONV_SUB_COLS)]


def _conv_use_shifted(k):
    return k > SUBLANES


def _conv_shift_scratch(k, rows, cw):
    return [pltpu.VMEM((SUBLANES - 1, rows - SUBLANES, cw), F32)] if _conv_use_shifted(k) else []


def _conv_fill_shifted(ext, sh):
    n = sh.shape[1]
    for b in range(1, SUBLANES):
        sh[b - 1] = ext[b:b + n, :]


def _conv_rows(ext, sh, start, rows, cs):
    b = start % SUBLANES
    if b == 0 or not sh:
        return ext[start:start + rows, cs]
    return sh[0][b - 1, start - b:start - b + rows, cs]


def _conv_fwd(src, col0, width, w, bias, k, seq, name):
    t = src.shape[0]
    tm, cw, halo = CONV_TILE, CONV_COLS, _conv_halo(k)
    sr, sc = CONV_SUB_ROWS, CONV_SUB_COLS
    p = k - 1
    cb0 = col0 // cw
    kp = w.shape[0]

    shifted = _conv_use_shifted(k)

    def body(x_ref, h_ref, w_ref, b_ref, o_ref, ext, *sh):
        i = pl.program_id(0)
        seq_start = (i * tm) % seq == 0
        ext[halo:, :] = x_ref[...]
        ext[:halo, :] = jnp.where(seq_start, 0.0, h_ref[...])
        if shifted:
            _conv_fill_shifted(ext, sh[0])
        for r0, c0 in _conv_subtiles(tm, cw):
            cs = slice(c0, c0 + sc)
            acc = jnp.zeros((sr, sc), F32) + b_ref[:, cs]
            for j in range(k):
                acc = acc + w_ref[j:j + 1, cs] * _conv_rows(ext, sh, r0 + halo - p + j, sr, cs)
            o_ref[r0:r0 + sr, cs] = acc

    return pl.pallas_call(
        body, name=name,
        out_shape=jax.ShapeDtypeStruct((t, width), F32),
        grid=(t // tm, width // cw),
        in_specs=[pl.BlockSpec((tm, cw), lambda i, j: (i, cb0 + j)),
                  pl.BlockSpec((halo, cw), lambda i, j: (jnp.maximum(i * (tm // halo) - 1, 0), cb0 + j)),
                  pl.BlockSpec((kp, cw), lambda i, j: (0, j)),
                  pl.BlockSpec((1, cw), lambda i, j: (0, j))],
        out_specs=pl.BlockSpec((tm, cw), lambda i, j: (i, j)),
        scratch_shapes=[pltpu.VMEM((halo + tm, cw), F32)] + _conv_shift_scratch(k, halo + tm, cw),
        compiler_params=_params(("parallel", "parallel")),
    )(src, src, w, bias)


def _conv_bwd(dy, src, col0, width, w, k, seq, name, into=None):
    t = src.shape[0]
    tm, cw, halo = CONV_TILE, CONV_COLS, _conv_halo(k)
    sr, sc = CONV_SUB_ROWS, CONV_SUB_COLS
    p = k - 1
    cb0 = col0 // cw
    kp = w.shape[0]
    nt = t // tm
    last_halo = t // halo - 1

    shifted = _conv_use_shifted(k)

    def body(dy_ref, dn_ref, x_ref, xp_ref, w_ref, *rest):
        if into is not None:
            rest = rest[1:]
        dx_ref, dw_ref, db_ref, dyext, xext, wacc, bacc = rest[:7]
        sh = rest[7:]
        i = pl.program_id(1)
        dysh, xsh = (sh[:1], sh[1:]) if shifted else ((), ())

        @pl.when(i == 0)
        def _():
            wacc[...] = jnp.zeros_like(wacc)
            bacc[...] = jnp.zeros_like(bacc)

        seq_start = (i * tm) % seq == 0
        seq_end = ((i + 1) * tm) % seq == 0
        dyext[:tm, :] = dy_ref[...]
        dyext[tm:, :] = jnp.where(seq_end, 0.0, dn_ref[...])
        xext[halo:, :] = x_ref[...]
        xext[:halo, :] = jnp.where(seq_start, 0.0, xp_ref[...])
        if shifted:
            _conv_fill_shifted(dyext, dysh[0])
            _conv_fill_shifted(xext, xsh[0])
        for r0, c0 in _conv_subtiles(tm, cw):
            cs = slice(c0, c0 + sc)
            dyv = dy_ref[r0:r0 + sr, cs]
            acc = jnp.zeros((sr, sc), F32)
            for j in range(k):
                acc = acc + w_ref[j:j + 1, cs] * _conv_rows(dyext, dysh, r0 + p - j, sr, cs)
                wacc[j, :, cs] += _rowsum8(dyv * _conv_rows(xext, xsh, r0 + halo - p + j, sr, cs))
            dx_ref[r0:r0 + sr, cs] = acc.astype(dx_ref.dtype)
            bacc[:, cs] += _rowsum8(dyv)

        @pl.when(i == nt - 1)
        def _():
            dw_ref[...] = jnp.zeros_like(dw_ref)
            for j in range(k):
                dw_ref[j:j + 1, :] = jnp.sum(wacc[j], axis=0, keepdims=True)
            db_ref[...] = jnp.sum(bacc[...], axis=0, keepdims=True)

    if into is None:
        dx_shape = jax.ShapeDtypeStruct((t, width), F32)
        dx_spec = pl.BlockSpec((tm, cw), lambda j, i: (i, j))
        extra_specs, extra_args, aliases = [], [], {}
    else:
        dx_shape = jax.ShapeDtypeStruct(into.shape, into.dtype)
        dx_spec = pl.BlockSpec((tm, cw), lambda j, i: (i, cb0 + j))
        extra_specs, extra_args, aliases = [ANY], [into], {5: 0}
    return pl.pallas_call(
        body, name=name,
        out_shape=(dx_shape, jax.ShapeDtypeStruct((kp, width), F32), jax.ShapeDtypeStruct((1, width), F32)),
        grid=(width // cw, nt),
        in_specs=[pl.BlockSpec((tm, cw), lambda j, i: (i, j)),
                  pl.BlockSpec((halo, cw), lambda j, i: (jnp.minimum((i + 1) * (tm // halo), last_halo), j)),
                  pl.BlockSpec((tm, cw), lambda j, i: (i, cb0 + j)),
                  pl.BlockSpec((halo, cw), lambda j, i: (jnp.maximum(i * (tm // halo) - 1, 0), cb0 + j)),
                  pl.BlockSpec((kp, cw), lambda j, i: (0, j))] + extra_specs,
        out_specs=(dx_spec,
                   pl.BlockSpec((kp, cw), lambda j, i: (0, j)),
                   pl.BlockSpec((1, cw), lambda j, i: (0, j))),
        input_output_aliases=aliases,
        scratch_shapes=[pltpu.VMEM((tm + halo, cw), F32), pltpu.VMEM((halo + tm, cw), F32),
                        pltpu.VMEM((kp, SUBLANES, cw), F32), pltpu.VMEM((SUBLANES, cw), F32)]
        + 2 * _conv_shift_scratch(k, halo + tm, cw),
        compiler_params=_params(("parallel", "arbitrary")),
    )(dy, dy, src, src, w, *extra_args)


def _half_mask(half):
    lane = _iota((1, LANES), 1)
    return ((lane >= half * ATTN_HEAD_DIM) & (lane < (half + 1) * ATTN_HEAD_DIM)).astype(F32)


def _stack_heads(xp, g):
    m = _half_mask(g)
    swapped = pltpu.roll(xp, ATTN_HEAD_DIM, axis=1)
    return jnp.concatenate([xp * m, swapped * m] if g == 0 else [swapped * m, xp * m], axis=0)


def _unstack_heads(both, g):
    w = both.shape[0] // 2
    top, bot = both[:w], both[w:]
    lo, hi = _half_mask(0), _half_mask(1)
    if g == 0:
        return top * lo + pltpu.roll(bot, ATTN_HEAD_DIM, axis=1) * hi
    return pltpu.roll(top, ATTN_HEAD_DIM, axis=1) * lo + bot * hi


def _band_mask(first_block):
    w = WINDOW
    qi = _iota((w, 2 * w), 0)
    kj = _iota((w, 2 * w), 1) - w
    rel = qi - kj
    return (rel >= 0) & (rel < w) & (jnp.logical_not(first_block) | (kj >= 0))


def _lane_pick(x, h):
    return jnp.sum(jnp.where(_iota(x.shape, 1) == h, x, 0.0), axis=1, keepdims=True)


def _attn_specs(nb, rev):
    w = WINDOW

    def blk(i):
        return nb - 1 - i if rev else i

    def row(b, i):
        return b * nb + blk(i)

    def prow(b, i):
        return b * nb + jnp.maximum(blk(i) - 1, 0)

    q = pl.BlockSpec((w, 512), lambda b, i: (row(b, i), OFF_Q // 512))
    kc = pl.BlockSpec((w, 128), lambda b, i: (row(b, i), OFF_K // 128))
    kp = pl.BlockSpec((w, 128), lambda b, i: (prow(b, i), OFF_K // 128))
    vc = pl.BlockSpec((w, 128), lambda b, i: (row(b, i), OFF_V // 128))
    vp = pl.BlockSpec((w, 128), lambda b, i: (prow(b, i), OFF_V // 128))
    z = pl.BlockSpec((w, 512), lambda b, i: (row(b, i), OFF_ZA // 512))
    return q, kc, kp, vc, vp, z, row


def _attn_fwd(proj, sinks, ycat, nbatch, name):
    t = proj.shape[0]
    w = WINDOW
    nb = t // nbatch // w
    scale = ATTN_HEAD_DIM ** -0.5
    q_s, kc_s, kp_s, vc_s, vp_s, z_s, row = _attn_specs(nb, False)

    def body(q_ref, kc_ref, kp_ref, vc_ref, vp_ref, z_ref, sk_ref, _, y_ref, o_ref, lse_ref):
        first = pl.program_id(1) == 0
        mask = _band_mask(first)
        kk = jnp.concatenate([kp_ref[...], kc_ref[...]], axis=0).astype(MXU_DTYPE)
        vv = jnp.concatenate([vp_ref[...], vc_ref[...]], axis=0).astype(MXU_DTYPE)
        sk = sk_ref[...]
        lane = _iota((w, LANES), 1)
        mask2 = jnp.concatenate([mask, mask], axis=0)
        scores = [_dot(_stack_heads(q_ref[:, j * LANES:(j + 1) * LANES], j // 2), kk, NT) for j in range(4)]
        lse_all = jnp.zeros((w, LANES), F32)
        for j in range(4):
            s = jnp.where(mask2, scores[j] * scale, -1e30)
            skc = jnp.concatenate([jnp.broadcast_to(_lane_pick(sk, 2 * j), (w, 1)),
                                   jnp.broadcast_to(_lane_pick(sk, 2 * j + 1), (w, 1))], axis=0)
            m = jnp.maximum(jnp.max(s, axis=1, keepdims=True), skc)
            den = jnp.sum(jnp.exp(s - m), axis=1, keepdims=True) + jnp.exp(skc - m)
            lse = m + jnp.log(den)
            lse_all = jnp.where(lane == 2 * j, lse[:w], lse_all)
            lse_all = jnp.where(lane == 2 * j + 1, lse[w:], lse_all)
            op = _unstack_heads(_dot(jnp.exp(s - lse), vv), j // 2)
            cols = slice(j * LANES, (j + 1) * LANES)
            o_ref[:, cols] = op
            y_ref[:, cols] = (op * _silu(z_ref[:, cols])).astype(y_ref.dtype)
        lse_ref[...] = lse_all

    return pl.pallas_call(
        body, name=name,
        out_shape=(jax.ShapeDtypeStruct(ycat.shape, ycat.dtype), jax.ShapeDtypeStruct((t, 512), F32),
                   jax.ShapeDtypeStruct((t, LANES), F32)),
        grid=(nbatch, nb),
        in_specs=[q_s, kc_s, kp_s, vc_s, vp_s, z_s, pl.BlockSpec((1, LANES), lambda b, i: (0, 0)), ANY],
        out_specs=(pl.BlockSpec((w, 512), lambda b, i: (row(b, i), YCAT_ATTN // 512)),
                   pl.BlockSpec((w, 512), lambda b, i: (row(b, i), 0)),
                   pl.BlockSpec((w, LANES), lambda b, i: (row(b, i), 0))),
        input_output_aliases={7: 0},
        compiler_params=_params(("parallel", "parallel")),
    )(proj, proj, proj, proj, proj, proj, sinks, ycat)


def _attn_bwd(dycat, proj, o, lse, sinks, ddt, dproj, nbatch, name):
    t = proj.shape[0]
    w = WINDOW
    nb = t // nbatch // w
    scale = ATTN_HEAD_DIM ** -0.5
    q_s, kc_s, kp_s, vc_s, vp_s, z_s, row = _attn_specs(nb, True)

    def body(dy_ref, q_ref, kc_ref, kp_ref, vc_ref, vp_ref, z_ref, o_ref, lse_ref, sk_ref, ddt_ref, _,
             grp_ref, dsk_ref, kcarry, vcarry, sacc):
        b, i = pl.program_id(0), pl.program_id(1)

        @pl.when((b == 0) & (i == 0))
        def _():
            sacc[...] = jnp.zeros_like(sacc)

        @pl.when(i == 0)
        def _():
            kcarry[...] = jnp.zeros_like(kcarry)
            vcarry[...] = jnp.zeros_like(vcarry)

        first = i == nb - 1
        mask = _band_mask(first)
        kk = jnp.concatenate([kp_ref[...], kc_ref[...]], axis=0).astype(MXU_DTYPE)
        vv = jnp.concatenate([vp_ref[...], vc_ref[...]], axis=0).astype(MXU_DTYPE)
        sk = sk_ref[...]
        lse_all = lse_ref[...]
        lane1 = _iota((1, LANES), 1)
        mask2 = jnp.concatenate([mask, mask], axis=0)
        qs, dos, deltas, lses, scores, dps = [], [], [], [], [], []
        for j in range(4):
            cols = slice(j * LANES, (j + 1) * LANES)
            qp, zp, ov, dy = q_ref[:, cols], z_ref[:, cols], o_ref[:, cols], dy_ref[:, cols]
            grp_ref[:, OFF_ZA + j * LANES:OFF_ZA + (j + 1) * LANES] = (dy * ov * _dsilu(zp)).astype(grp_ref.dtype)
            do = dy * _silu(zp)
            q2 = _stack_heads(qp, j // 2).astype(MXU_DTYPE)
            do2 = _stack_heads(do, j // 2)
            qs.append(q2)
            dos.append(do2.astype(MXU_DTYPE))
            deltas.append(jnp.sum(do2 * _stack_heads(ov, j // 2), axis=1, keepdims=True))
            lses.append(jnp.concatenate([_lane_pick(lse_all, 2 * j), _lane_pick(lse_all, 2 * j + 1)], axis=0))
            scores.append(_dot(q2, kk, NT))
            dps.append(_dot(do2, vv, NT))
        prs, dss = [], []
        dsk = jnp.zeros((1, LANES), F32)
        for j in range(4):
            pr = jnp.exp(jnp.where(mask2, scores[j] * scale, -1e30) - lses[j])
            prs.append(pr.astype(MXU_DTYPE))
            dss.append((pr * (dps[j] - deltas[j])).astype(MXU_DTYPE))
            skc = jnp.concatenate([jnp.broadcast_to(_lane_pick(sk, 2 * j), (w, 1)),
                                   jnp.broadcast_to(_lane_pick(sk, 2 * j + 1), (w, 1))], axis=0)
            sink_term = jnp.exp(skc - lses[j]) * deltas[j]
            dsk = dsk - jnp.where(lane1 == 2 * j, jnp.sum(sink_term[:w]), 0.0)
            dsk = dsk - jnp.where(lane1 == 2 * j + 1, jnp.sum(sink_term[w:]), 0.0)
        dkk = jnp.zeros((2 * w, LANES), F32)
        dvv = jnp.zeros((2 * w, LANES), F32)
        for j in range(4):
            dq = _unstack_heads(_dot(dss[j], kk) * scale, j // 2)
            grp_ref[:, OFF_Q + j * LANES:OFF_Q + (j + 1) * LANES] = dq.astype(grp_ref.dtype)
            dkk = dkk + _dot(dss[j], qs[j], TN) * scale
            dvv = dvv + _dot(prs[j], dos[j], TN)
        grp_ref[:, OFF_K:OFF_K + LANES] = (dkk[w:, :] + kcarry[...]).astype(grp_ref.dtype)
        grp_ref[:, OFF_V:OFF_V + LANES] = (dvv[w:, :] + vcarry[...]).astype(grp_ref.dtype)
        grp_ref[:, OFF_DT:OFF_DT + LANES] = ddt_ref[...].astype(grp_ref.dtype)
        grp_ref[:, OFF_DT + LANES:] = jnp.zeros((w, ATTN_GROUP - OFF_DT - LANES), grp_ref.dtype)
        kcarry[...] = dkk[:w, :]
        vcarry[...] = dvv[:w, :]
        sacc[...] += dsk

        @pl.when((b == nbatch - 1) & (i == nb - 1))
        def _():
            dsk_ref[...] = sacc[...]

    return pl.pallas_call(
        body, name=name,
        out_shape=(jax.ShapeDtypeStruct(dproj.shape, dproj.dtype), jax.ShapeDtypeStruct((1, LANES), F32)),
        grid=(nbatch, nb),
        in_specs=[pl.BlockSpec((w, 512), lambda b, i: (row(b, i), YCAT_ATTN // 512)),
                  q_s, kc_s, kp_s, vc_s, vp_s, z_s,
                  pl.BlockSpec((w, 512), lambda b, i: (row(b, i), 0)),
                  pl.BlockSpec((w, LANES), lambda b, i: (row(b, i), 0)),
                  pl.BlockSpec((1, LANES), lambda b, i: (0, 0)),
                  pl.BlockSpec((w, LANES), lambda b, i: (row(b, i), 0)), ANY],
        out_specs=(pl.BlockSpec((w, ATTN_GROUP), lambda b, i: (row(b, i), 0)),
                   pl.BlockSpec((1, LANES), lambda b, i: (0, 0))),
        input_output_aliases={11: 0},
        scratch_shapes=[pltpu.VMEM((w, LANES), F32), pltpu.VMEM((w, LANES), F32),
                        pltpu.VMEM((1, LANES), F32)],
        compiler_params=_params(("arbitrary", "arbitrary")),
    )(dycat, proj, proj, proj, proj, proj, proj, o, lse, sinks, ddt, dproj)


SSD_WIDTH = SSD_HEADS * SSD_HEAD_DIM
GROUP_ROWS = SSD_WIDTH // 2


def _expand_mat():
    r, c = _iota((LANES, SSD_WIDTH), 0), _iota((LANES, SSD_WIDTH), 1)
    return (r == lax.shift_right_logical(c, 6)).astype(BF16)


def _expand_mat_t():
    r, c = _iota((SSD_WIDTH, LANES), 0), _iota((SSD_WIDTH, LANES), 1)
    return (c == lax.shift_right_logical(r, 6)).astype(BF16)


def _ssd_common(u_ref, dt_ref, dtb_ref, a_ref):
    q = CHUNK
    act = _silu(u_ref[...])
    xs = act[:, :SSD_WIDTH]
    bm = act[:, SSD_WIDTH:SSD_WIDTH + 256]
    cm = act[:, SSD_WIDTH + 256:]
    dtp = _softplus(dt_ref[...] + dtb_ref[...])
    a = dtp * a_ref[...]
    tril = (_iota((q, q), 0) >= _iota((q, q), 1)).astype(BF16)
    acs = _xdot_r(tril, a)
    acs_t = acs.T
    e = _expand_mat()
    dt_x = _xdot(dtp, e)
    ea = jnp.exp(_xdot(acs, e))
    a_end = jnp.sum(jnp.where(_iota(acs.shape, 0) == q - 1, acs, 0.0), axis=0, keepdims=True)
    dec = jnp.exp(_xdot(a_end - acs, e))
    a_end_col = jnp.broadcast_to(_lane_pick(acs_t, q - 1), (LANES, LANES))
    s_scale = jnp.exp(_xdot_r(_expand_mat_t(), a_end_col))
    return act, xs, bm, cm, dtp, acs, acs_t, dt_x, ea, dec, s_scale, tril


def _decay_mat(acs, acs_t, h):
    q = CHUNK
    col = _lane_pick(acs, h)
    rowv = jnp.sum(jnp.where(_iota(acs_t.shape, 0) == h, acs_t, 0.0), axis=0, keepdims=True)
    causal = _iota((q, q), 0) >= _iota((q, q), 1)
    return jnp.exp(jnp.where(causal, col - rowv, -1e30))


GN_WIDTH = 512


def _ssd_fwd(u, proj, dtb, a_neg, d_x, norm_w, ycat, nbatch, name):
    t = u.shape[0]
    q = CHUNK
    nc = t // nbatch // q

    def body(u_ref, dt_ref, z_ref, dtb_ref, a_ref, dx_ref, nw_ref, _, y_ref, st_ref, yn_ref, state):
        c = pl.program_id(1)

        @pl.when(c == 0)
        def _():
            state[...] = jnp.zeros_like(state)

        st_ref[...] = state[...]
        act, xs, bm, cm, dtp, acs, acs_t, dt_x, ea, dec, s_scale, _ = _ssd_common(u_ref, dt_ref, dtb_ref, a_ref)
        xdt = xs * dt_x
        xdec = xdt * dec
        lo, hi = _half_mask(0), _half_mask(1)
        for g in range(2):
            bg = bm[:, g * LANES:(g + 1) * LANES]
            cg = cm[:, g * LANES:(g + 1) * LANES]
            rows = slice(g * GROUP_ROWS, (g + 1) * GROUP_ROWS)
            sg = state[rows, :]
            cb = _dot(cg, bg, NT)
            yoff = _dot(cg, sg, NT)
            for j in range(4):
                pj = g * 4 + j
                cols = slice(pj * LANES, (pj + 1) * LANES)
                xp = xdt[:, cols]
                m0 = cb * _decay_mat(acs, acs_t, 2 * pj)
                m1 = cb * _decay_mat(acs, acs_t, 2 * pj + 1)
                yp = _dot(m0, xp * lo) + _dot(m1, xp * hi)
                yp = yp + yoff[:, j * LANES:(j + 1) * LANES] * ea[:, cols]
                y_ref[:, cols] = yp + dx_ref[:, cols] * xs[:, cols]
            state[rows, :] = s_scale[rows, :] * sg + _dot(xdec[:, rows], bg, TN)
        for g in range(SSD_WIDTH // GN_WIDTH):
            cols = slice(g * GN_WIDTH, (g + 1) * GN_WIDTH)
            gg = y_ref[:, cols] * _silu(z_ref[:, cols])
            rstd = lax.rsqrt(jnp.mean(gg * gg, axis=-1, keepdims=True) + EPS)
            yn_ref[:, cols] = (gg * rstd * nw_ref[:, cols]).astype(yn_ref.dtype)

    vec = pl.BlockSpec((1, LANES), lambda b, c: (0, 0))
    wide = pl.BlockSpec((q, SSD_WIDTH), lambda b, c: (b * nc + c, 0))
    wvec = pl.BlockSpec((1, SSD_WIDTH), lambda b, c: (0, 0))
    return pl.pallas_call(
        body, name=name,
        out_shape=(jax.ShapeDtypeStruct((t, SSD_WIDTH), F32),
                   jax.ShapeDtypeStruct((nbatch * nc * SSD_WIDTH, SSD_STATE), F32),
                   jax.ShapeDtypeStruct(ycat.shape, ycat.dtype)),
        grid=(nbatch, nc),
        in_specs=[pl.BlockSpec((q, SSD_CONV_DIM), lambda b, c: (b * nc + c, 0)),
                  pl.BlockSpec((q, LANES), lambda b, c: (b * nc + c, OFF_DT // LANES)),
                  pl.BlockSpec((q, SSD_WIDTH), lambda b, c: (b * nc + c, OFF_ZS // SSD_WIDTH)),
                  vec, vec, wvec, wvec, ANY],
        out_specs=(wide, pl.BlockSpec((SSD_WIDTH, SSD_STATE), lambda b, c: (b * nc + c, 0)), wide),
        input_output_aliases={7: 2},
        scratch_shapes=[pltpu.VMEM((SSD_WIDTH, SSD_STATE), F32)],
        compiler_params=_params(("parallel", "arbitrary")),
    )(u, proj, proj, dtb, a_neg, d_x, norm_w, ycat)


def _ssd_bwd(dycat, u, proj, y, states, dtb, a_neg, d_x, norm_w, dproj, nbatch, name):
    t = u.shape[0]
    q = CHUNK
    nc = t // nbatch // q

    def body(do_ref, u_ref, dt_ref, z_ref, y_ref, st_ref, dtb_ref, a_ref, dx_ref, nw_ref, _,
             du_ref, dz_ref, ddt_ref, dal_ref, dd_ref, dtbg_ref, dnw_ref, dstate, acc_a, acc_d, acc_b, acc_w):
        b, c = pl.program_id(0), pl.program_id(1)

        @pl.when((b == 0) & (c == 0))
        def _():
            acc_a[...] = jnp.zeros_like(acc_a)
            acc_d[...] = jnp.zeros_like(acc_d)
            acc_b[...] = jnp.zeros_like(acc_b)
            acc_w[...] = jnp.zeros_like(acc_w)

        @pl.when(c == 0)
        def _():
            dstate[...] = jnp.zeros_like(dstate)

        dy_parts = []
        for g in range(SSD_WIDTH // GN_WIDTH):
            cols = slice(g * GN_WIDTH, (g + 1) * GN_WIDTH)
            yv, zv, dov = y_ref[:, cols], z_ref[:, cols], do_ref[:, cols]
            sz = _silu(zv)
            gg = yv * sz
            rstd = lax.rsqrt(jnp.mean(gg * gg, axis=-1, keepdims=True) + EPS)
            gh = gg * rstd
            acc_w[:, cols] += _rowsum8(dov * gh)
            dgn = dov * nw_ref[:, cols]
            dg = rstd * (dgn - gh * jnp.mean(dgn * gh, axis=-1, keepdims=True))
            dy_parts.append(dg * sz)
            dz_ref[:, cols] = (dg * yv * _dsilu(zv)).astype(dz_ref.dtype)

        act, xs, bm, cm, dtp, acs, acs_t, dt_x, ea, dec, s_scale, tril = _ssd_common(
            u_ref, dt_ref, dtb_ref, a_ref)
        xdt = xs * dt_x
        xdec = xdt * dec
        dyv = jnp.concatenate(dy_parts, axis=1)
        dye = dyv * ea
        lo, hi = _half_mask(0), _half_mask(1)
        et = _expand_mat_t()
        dxdt_parts, db_parts, dc_parts, dxst_parts, yoff_parts = [], [], [], [], []
        end_sum = jnp.zeros((LANES, LANES), F32)
        dal_diag = jnp.zeros((q, LANES), F32)
        lane_q = _iota((q, LANES), 1)
        for g in range(2):
            bg = bm[:, g * LANES:(g + 1) * LANES]
            cg = cm[:, g * LANES:(g + 1) * LANES]
            rows = slice(g * GROUP_ROWS, (g + 1) * GROUP_ROWS)
            sg = st_ref[rows, :]
            dsg = dstate[rows, :]
            cb = _dot(cg, bg, NT)
            yoff_parts.append(_dot(cg, sg, NT))
            dcb = jnp.zeros((q, q), F32)
            parts = []
            for j in range(4):
                pj = g * 4 + j
                cols = slice(pj * LANES, (pj + 1) * LANES)
                xp = xdt[:, cols]
                dy0, dy1 = dyv[:, cols] * lo, dyv[:, cols] * hi
                l0 = _decay_mat(acs, acs_t, 2 * pj)
                l1 = _decay_mat(acs, acs_t, 2 * pj + 1)
                g0, g1 = _dot(dy0, xp, NT), _dot(dy1, xp, NT)
                m0, m1 = cb * l0, cb * l1
                dcb = dcb + g0 * l0 + g1 * l1
                parts.append(_dot(m0, dy0, TN) + _dot(m1, dy1, TN))
                for hh, wmat in enumerate((g0 * m0, g1 * m1)):
                    sel = (lane_q == 2 * pj + hh).astype(F32)
                    dal_diag = dal_diag + _dot(wmat, sel) - _dot(wmat, sel, TN)
            dxst = _dot(bg, dsg, NT) * dec[:, rows]
            dxst_parts.append(dxst)
            dxdt_parts.append(jnp.concatenate(parts, axis=1) + dxst)
            dc_parts.append(_dot(dcb, bg) + _dot(dye[:, rows], sg))
            db_parts.append(_dot(dcb, cg, TN) + _dot(xdec[:, rows], dsg))
            s_next = s_scale[rows, :] * sg + _dot(xdec[:, rows], bg, TN)
            end_sum = end_sum + _xdot(dsg * s_next, et[rows, :], TN, passes=2)
            dstate[rows, :] = _dot(dye[:, rows], cg, TN) + s_scale[rows, :] * dsg
        dxdt = jnp.concatenate(dxdt_parts, axis=1)
        dxv = dx_ref[...]
        yoff = jnp.concatenate(yoff_parts, axis=1) * ea
        dalpha = dal_diag + _xdot(dyv * yoff - xdt * jnp.concatenate(dxst_parts, axis=1), et)
        end_row = jnp.sum(end_sum, axis=0, keepdims=True)
        dalpha = dalpha + jnp.where(_iota((q, LANES), 0) == q - 1, end_row, 0.0)
        da = _xdot_r(tril, dalpha, TN)
        ddtp = da * a_ref[...] + _xdot(dxdt * xs, et)
        acc_a[...] += _rowsum8(da * dtp)
        acc_d[...] += _rowsum8(_xdot(dyv * xs, et))
        ddt_raw = ddtp * _sigmoid(dt_ref[...] + dtb_ref[...])
        acc_b[...] += _rowsum8(ddt_raw)
        ddt_ref[...] = ddt_raw
        dxs = dxdt * dt_x + dxv * dyv
        dact = jnp.concatenate([dxs] + db_parts + dc_parts, axis=1)
        du_ref[...] = dact * _dsilu(u_ref[...])

        @pl.when((b == nbatch - 1) & (c == nc - 1))
        def _():
            dal_ref[...] = jnp.sum(acc_a[...], axis=0, keepdims=True) * a_ref[...]
            dd_ref[...] = jnp.sum(acc_d[...], axis=0, keepdims=True)
            dtbg_ref[...] = jnp.sum(acc_b[...], axis=0, keepdims=True)
            dnw_ref[...] = jnp.sum(acc_w[...], axis=0, keepdims=True)

    def rowblk(b, c):
        return b * nc + (nc - 1 - c)

    vec = pl.BlockSpec((1, LANES), lambda b, c: (0, 0))
    wvec = pl.BlockSpec((1, SSD_WIDTH), lambda b, c: (0, 0))
    wide = pl.BlockSpec((q, SSD_WIDTH), lambda b, c: (rowblk(b, c), 0))
    zblk = pl.BlockSpec((q, SSD_WIDTH), lambda b, c: (rowblk(b, c), OFF_ZS // SSD_WIDTH))
    return pl.pallas_call(
        body, name=name,
        out_shape=(jax.ShapeDtypeStruct((t, SSD_CONV_DIM), F32), jax.ShapeDtypeStruct(dproj.shape, dproj.dtype),
                   jax.ShapeDtypeStruct((t, LANES), F32),
                   jax.ShapeDtypeStruct((1, LANES), F32), jax.ShapeDtypeStruct((1, LANES), F32),
                   jax.ShapeDtypeStruct((1, LANES), F32), jax.ShapeDtypeStruct((1, SSD_WIDTH), F32)),
        grid=(nbatch, nc),
        in_specs=[wide,
                  pl.BlockSpec((q, SSD_CONV_DIM), lambda b, c: (rowblk(b, c), 0)),
                  pl.BlockSpec((q, LANES), lambda b, c: (rowblk(b, c), OFF_DT // LANES)),
                  zblk, wide,
                  pl.BlockSpec((SSD_WIDTH, SSD_STATE), lambda b, c: (rowblk(b, c), 0)),
                  vec, vec, wvec, wvec, ANY],
        out_specs=(pl.BlockSpec((q, SSD_CONV_DIM), lambda b, c: (rowblk(b, c), 0)),
                   zblk,
                   pl.BlockSpec((q, LANES), lambda b, c: (rowblk(b, c), 0)),
                   vec, vec, vec, wvec),
        input_output_aliases={10: 1},
        scratch_shapes=[pltpu.VMEM((SSD_WIDTH, SSD_STATE), F32), pltpu.VMEM((SUBLANES, LANES), F32),
                        pltpu.VMEM((SUBLANES, LANES), F32), pltpu.VMEM((SUBLANES, LANES), F32),
                        pltpu.VMEM((SUBLANES, SSD_WIDTH), F32)],
        compiler_params=_params(("arbitrary", "arbitrary")),
    )(dycat, u, proj, proj, y, states, dtb, a_neg, d_x, norm_w, dproj)


def _pad_rows(w, rows):
    return jnp.concatenate([w, jnp.zeros((rows - w.shape[0], w.shape[1]), w.dtype)], axis=0)


def _pad_lanes(v):
    return jnp.concatenate([v, jnp.zeros((LANES - v.shape[0],), v.dtype)]).reshape(1, LANES)


def _padded_from_chips(pieces):
    cols = pieces[0].shape[-1]
    lead = pieces[0].shape[:-1]
    parts, pos = [], 0
    for lo, hi, start in sorted(SECTIONS, key=lambda s: s[2]):
        if start > pos:
            parts.append(jnp.zeros(lead + (start - pos,), pieces[0].dtype))
        pos = start + hi - lo
        while lo < hi:
            p = lo // cols
            end = min(hi, (p + 1) * cols)
            parts.append(pieces[p][..., lo - p * cols:end - p * cols])
            lo = end
    if pos < NP:
        parts.append(jnp.zeros(lead + (NP - pos,), pieces[0].dtype))
    return jnp.concatenate(parts, axis=-1)


def _chip_part_from_padded(wp, p, cols):
    lo, hi = p * cols, (p + 1) * cols
    parts = []
    for rs, re, start in SECTIONS:
        a, b = max(lo, rs), min(hi, re)
        if a < b:
            parts.append(wp[..., start + a - rs:start + b - rs])
    return jnp.concatenate(parts, axis=-1)


def _layer_params(li, w_in_p, w_out, conv_w, dw_w, small):
    return dict(
        w_in_p=w_in_p, w_out=w_out,
        conv_w=_pad_rows(conv_w, SUBLANES), dw_w=_pad_rows(dw_w, 32),
        norm_w=small["norm_w"][li].reshape(1, -1),
        conv_b=small["ssd_conv_b"][li].reshape(1, -1),
        dtb=_pad_lanes(small["ssd_dt_bias"][li]),
        a_neg=_pad_lanes(-jnp.exp(small["ssd_a_log"][li])),
        d_x=jnp.repeat(small["ssd_d"][li], SSD_HEAD_DIM).reshape(1, -1),
        ssd_norm_w=small["ssd_norm_w"][li].reshape(1, -1),
        sinks=_pad_lanes(small["attn_sinks"][li]),
        dw_b=small["conf_dw_b"][li].reshape(1, -1),
        ln_w=small["conf_ln_w"][li].reshape(1, -1),
        ln_b=small["conf_ln_b"][li].reshape(1, -1),
    )


def _layer_fwd(x, p, nbatch, seq, tag, after=None):
    h, h_t = _rmsnorm_fwd(x, p["norm_w"], name=f"rmsnorm_fwd_{tag}", after=after)
    proj = _matmul(h, p["w_in_p"], "nn", F32, 1024, 512, 1024, name=f"proj_fwd_{tag}")
    u = _conv_fwd(proj, OFF_XBC, SSD_CONV_DIM, p["conv_w"], p["conv_b"], SSD_CONV, seq, name=f"ssd_conv_fwd_{tag}")
    ycat = lax.empty((x.shape[0], MIX_WIDTH), MXU_DTYPE)
    y, states, ycat = _ssd_fwd(u, proj, p["dtb"], p["a_neg"], p["d_x"], p["ssd_norm_w"], ycat, nbatch,
                               name=f"ssd_fwd_{tag}")
    ycat, o, lse = _attn_fwd(proj, p["sinks"], ycat, nbatch, name=f"attn_fwd_{tag}")
    c0 = _glu_fwd(proj, name=f"glu_fwd_{tag}")
    c1 = _conv_fwd(c0, 0, CONF_WIDTH, p["dw_w"], p["dw_b"], CONF_KERNEL, seq, name=f"conf_conv_fwd_{tag}")
    ycat = _conf_post_fwd(c1, proj, p["ln_w"], p["ln_b"], ycat, name=f"conf_post_fwd_{tag}")
    x_new = _matmul(ycat, p["w_out"], "nn", F32, 1024, 512, 2048, name=f"out_fwd_{tag}", residual=x)
    return x_new, dict(x=x, h_t=h_t, proj=proj, u=u, y=y, states=states, o=o, lse=lse, c0=c0, c1=c1, ycat=ycat)


def _layer_bwd(dx_out, p, s, nbatch, seq, tag, hooks=None):
    hooks = hooks or {}
    proj = s["proj"]
    dycat = _matmul(dx_out, p["w_out"], "nt", F32, 1024, 1024, 1024, name=f"out_bwd_dy_{tag}",
                    after=hooks.get("start_token"))
    dw_out = _matmul(s["ycat"], dx_out, "tn", F32, 1024, 1024, 1024, name=f"out_bwd_dw_{tag}")
    token = hooks["after_dycat"](dycat) if "after_dycat" in hooks else None
    dtb = p["dtb"] if token is None else p["dtb"] + token[0, 0]
    dproj = lax.empty(proj.shape, MXU_DTYPE)
    du, dproj, ddt, da_log, dd, ddtb, dssd_norm_w = _ssd_bwd(
        dycat, s["u"], proj, s["y"], s["states"], dtb, p["a_neg"], p["d_x"], p["ssd_norm_w"], dproj,
        nbatch, name=f"ssd_bwd_{tag}")
    dproj, dconv_w, dconv_b = _conv_bwd(du, proj, OFF_XBC, SSD_CONV_DIM, p["conv_w"], SSD_CONV, seq,
                                        name=f"ssd_conv_bwd_{tag}", into=dproj)
    dproj, dsinks = _attn_bwd(dycat, proj, s["o"], s["lse"], p["sinks"], ddt, dproj, nbatch,
                              name=f"attn_bwd_{tag}")
    if "after_attn" in hooks:
        hooks["after_attn"](dproj)
    dc1, dproj, dln_w, dln_b = _conf_post_bwd(dycat, s["c1"], proj, p["ln_w"], p["ln_b"], dproj,
                                              name=f"conf_post_bwd_{tag}")
    dc0, ddw_w, ddw_b = _conv_bwd(dc1, s["c0"], 0, CONF_WIDTH, p["dw_w"], CONF_KERNEL, seq,
                                  name=f"conf_conv_bwd_{tag}")
    dproj = _glu_bwd(dc0, proj, dproj, name=f"glu_bwd_{tag}")
    dh = _matmul(dproj, p["w_in_p"], "nt", F32, 1024, 1024, 1408, name=f"proj_bwd_dh_{tag}")
    dw_in_p = _matmul(s["h_t"], dproj, "nn", F32, 1024, 512, 4096, name=f"proj_bwd_dw_{tag}")
    dx_in, dnorm_w = _rmsnorm_bwd(dh, s["x"], p["norm_w"], dx_out, name=f"rmsnorm_bwd_{tag}")
    grads = dict(
        norm_w=dnorm_w[0], w_in_p=dw_in_p, ssd_conv_w=dconv_w[:SSD_CONV], ssd_conv_b=dconv_b[0],
        ssd_dt_bias=ddtb[0, :SSD_HEADS], ssd_a_log=da_log[0, :SSD_HEADS], ssd_d=dd[0, :SSD_HEADS],
        ssd_norm_w=dssd_norm_w[0], attn_sinks=dsinks[0, :ATTN_Q_HEADS], conf_dw_w=ddw_w[:CONF_KERNEL],
        conf_dw_b=ddw_b[0], conf_ln_w=dln_w[0], conf_ln_b=dln_b[0], w_out=dw_out)
    return dx_in, grads


def _local_step(x, target, param_fns, final_norm_w, first_after=None, on_grads=None):
    nbatch, seq, d = x.shape
    xt = x.reshape(nbatch * seq, d)
    saved, layer_params = [], []
    for li, fn in enumerate(param_fns):
        p = fn(xt)
        layer_params.append(p)
        xt, s = _layer_fwd(xt, p, nbatch, seq, f"l{li}", after=first_after if li == 0 else None)
        saved.append(s)
    loss, dx, dfinal = _loss_head(xt, target.reshape(nbatch * seq, d), final_norm_w.reshape(1, d), name="loss_head")
    grads = [None] * len(layer_params)
    hooks = None
    for li in reversed(range(len(layer_params))):
        dx, grads[li] = _layer_bwd(dx, layer_params[li], saved[li], nbatch, seq, f"l{li}", hooks=hooks)
        hooks = on_grads(li, grads[li]) if on_grads is not None else None
    return loss[0, 0], dx.reshape(nbatch, seq, d), grads, dfinal[0]


MESH = pl.DeviceIdType.MESH
N_CHIPS = 4


def _mesh_pos():
    return lax.axis_index("x"), lax.axis_index("y"), lax.axis_index("c")


def _other_chips(x, y):
    return [(1 - x, y), (x, 1 - y), (1 - x, 1 - y)]


def _gather_weights(big, small, name):
    nbig, nsmall = len(big), len(small)
    n_ici = 3 * (nbig + nsmall)
    n_fwd = 3 * nbig

    def body(*refs):
        ins = refs[:nbig + nsmall]
        outs = refs[nbig + nsmall:2 * (nbig + nsmall)]
        send_sems, recv_sems = refs[2 * (nbig + nsmall):]
        x, y, c = _mesh_pos()
        me = 2 * x + y
        sibling = (x, y, 1 - c)
        chips = _other_chips(x, y)

        def ici(a, j, origin, dest):
            if a < nbig:
                src = ins[a].at[c] if origin is None else outs[a].at[origin, c]
                dst = outs[a].at[me if origin is None else origin, c]
            else:
                src = ins[a] if origin is None else outs[a].at[origin]
                dst = outs[a].at[me if origin is None else origin]
            k = a * 3 + j
            return pltpu.make_async_remote_copy(src_ref=src, dst_ref=dst, send_sem=send_sems.at[k],
                                                recv_sem=recv_sems.at[k], device_id=dest, device_id_type=MESH)

        def fwd(a, j, origin, half):
            k = n_ici + a * 3 + j
            ref = outs[a].at[origin, half]
            return pltpu.make_async_remote_copy(src_ref=ref, dst_ref=ref, send_sem=send_sems.at[k],
                                                recv_sem=recv_sems.at[k], device_id=sibling, device_id_type=MESH)

        sends = []
        for j, (px, py) in enumerate(chips):
            for a in range(nbig + nsmall):
                cp = ici(a, j, None, (px, py, c))
                cp.start()
                sends.append(cp)
        for j, (px, py) in enumerate(chips):
            origin = 2 * px + py
            for a in range(nbig):
                ici(a, j, origin, (px, py, c)).wait_recv()
                cp = fwd(a, j, origin, c)
                cp.start()
                sends.append(cp)
        for j, (px, py) in enumerate(chips):
            origin = 2 * px + py
            for a in range(nbig, nbig + nsmall):
                ici(a, j, origin, (px, py, c)).wait_recv()
            for a in range(nbig):
                fwd(a, j, origin, 1 - c).wait_recv()
        for cp in sends:
            cp.wait_send()

    out_shape = tuple(jax.ShapeDtypeStruct((N_CHIPS,) + a.shape, a.dtype) for a in list(big) + list(small))
    return pl.pallas_call(
        body, name=name, out_shape=out_shape,
        in_specs=[ANY] * (nbig + nsmall), out_specs=tuple([ANY] * (nbig + nsmall)),
        scratch_shapes=[pltpu.SemaphoreType.DMA((n_ici + n_fwd,)), pltpu.SemaphoreType.DMA((n_ici + n_fwd,))],
    )(*big, *small)


def _pair_swap_halves(arrs, name):
    n = len(arrs)

    def body(*refs):
        ins, outs = refs[:n], refs[n:2 * n]
        send_sems, recv_sems = refs[2 * n:]
        x, y, c = _mesh_pos()
        cps = [pltpu.make_async_remote_copy(src_ref=ins[a].at[:, 1 - c], dst_ref=outs[a], send_sem=send_sems.at[a],
                                            recv_sem=recv_sems.at[a], device_id=(x, y, 1 - c), device_id_type=MESH)
               for a in range(n)]
        for cp in cps:
            cp.start()
        for cp in cps:
            cp.wait()

    return pl.pallas_call(
        body, name=name,
        out_shape=tuple(jax.ShapeDtypeStruct(a.shape[:1] + a.shape[2:], a.dtype) for a in arrs),
        in_specs=[ANY] * n, out_specs=tuple([ANY] * n),
        scratch_shapes=[pltpu.SemaphoreType.DMA((n,)), pltpu.SemaphoreType.DMA((n,))],
    )(*arrs)


def _chip_scatter(arrs, name):
    n = len(arrs)

    def body(*refs):
        ins, outs = refs[:n], refs[n:2 * n]
        send_sems, recv_sems = refs[2 * n:]
        x, y, c = _mesh_pos()
        me = 2 * x + y
        cps = []
        for j, (px, py) in enumerate(_other_chips(x, y)):
            for a in range(n):
                cps.append(pltpu.make_async_remote_copy(
                    src_ref=ins[a].at[2 * px + py], dst_ref=outs[a].at[me], send_sem=send_sems.at[a * 3 + j],
                    recv_sem=recv_sems.at[a * 3 + j], device_id=(px, py, c), device_id_type=MESH))
        for cp in cps:
            cp.start()
        for cp in cps:
            cp.wait()

    return pl.pallas_call(
        body, name=name, out_shape=tuple(jax.ShapeDtypeStruct(a.shape, a.dtype) for a in arrs),
        in_specs=[ANY] * n, out_specs=tuple([ANY] * n),
        scratch_shapes=[pltpu.SemaphoreType.DMA((3 * n,)), pltpu.SemaphoreType.DMA((3 * n,))],
    )(*arrs)


HBM = pl.BlockSpec(memory_space=pltpu.HBM)
SEM = pl.BlockSpec(memory_space=pltpu.SEMAPHORE)
DATAFLOW = pltpu.SideEffectType.DATAFLOW_SIDE_EFFECTING


def _split_peers(pattern, x, y, c):
    if pattern == "swap":
        return [((x, y, 1 - c), 1 - c, None, None)]
    me = 2 * x + y
    return [((px, py, c), 2 * px + py if pattern == "scatter" else None, me, 2 * px + py)
            for px, py in _other_chips(x, y)]


def _split_land_shape(pattern, shape):
    return {"bcast": (N_CHIPS,) + shape, "scatter": shape, "swap": shape[:1] + shape[2:]}[pattern]


def _split_copies(pattern, srcs, lands, send_sems, recv_sems, waiting):
    x, y, c = _mesh_pos()
    peers = _split_peers(pattern, x, y, c)
    cps = []
    for j, (dev, src_slot, dst_slot, my_slot) in enumerate(peers):
        for a in range(len(srcs)):
            if src_slot is None:
                src = srcs[a]
            else:
                src = srcs[a].at[:, src_slot] if pattern == "swap" else srcs[a].at[src_slot]
            slot = my_slot if waiting else dst_slot
            dst = lands[a] if slot is None else lands[a].at[slot]
            k = a * len(peers) + j
            cps.append(pltpu.make_async_remote_copy(src_ref=src, dst_ref=dst, send_sem=send_sems[k],
                                                    recv_sem=recv_sems[k], device_id=dev, device_id_type=MESH))
    return cps


def _split_start(arrs, pattern, after, name):
    n = len(arrs)
    nsem = n * (1 if pattern == "swap" else N_CHIPS - 1)

    def body(*refs):
        srcs, lands = refs[:n], refs[n:2 * n]
        outs = refs[2 * n + 1:]
        for cp in _split_copies(pattern, srcs, lands, outs[:nsem], outs[nsem:2 * nsem], waiting=False):
            cp.start()
        outs[-1][...] = jnp.zeros_like(outs[-1])

    lands = [lax.empty(_split_land_shape(pattern, a.shape), a.dtype) for a in arrs]
    out_shape = ([pltpu.SemaphoreType.DMA(())] * (2 * nsem)
                 + [pltpu.HBM(a.shape, a.dtype) for a in arrs] + [pltpu.HBM(b.shape, b.dtype) for b in lands]
                 + [jax.ShapeDtypeStruct((SUBLANES, LANES), F32)])
    outs = pl.pallas_call(
        body, name=name, out_shape=tuple(out_shape),
        in_specs=[HBM] * (2 * n) + [ANY],
        out_specs=tuple([SEM] * (2 * nsem) + [HBM] * (2 * n) + [pl.BlockSpec(memory_space=pltpu.VMEM)]),
        input_output_aliases={a: 2 * nsem + a for a in range(2 * n)},
        compiler_params=pltpu.CompilerParams(has_side_effects=DATAFLOW),
    )(*[pltpu.with_memory_space_constraint(a, pltpu.HBM) for a in list(arrs) + lands], after)
    return outs[:-1], outs[-1]


def _split_wait(state, n, pattern, after, name):
    nsem = n * (1 if pattern == "swap" else N_CHIPS - 1)

    def body(*refs):
        srcs, lands = refs[:n], refs[n:2 * n]
        send_sems, recv_sems = refs[2 * n:2 * n + nsem], refs[2 * n + nsem:2 * n + 2 * nsem]
        for cp in _split_copies(pattern, srcs, lands, send_sems, recv_sems, waiting=True):
            cp.wait_send()
            cp.wait_recv()

    sems, thru = state[:2 * nsem], state[2 * nsem:]
    outs = pl.pallas_call(
        body, name=name, out_shape=tuple(pltpu.HBM(a.shape, a.dtype) for a in thru),
        in_specs=[HBM] * (2 * n) + [SEM] * (2 * nsem) + [ANY],
        out_specs=tuple([HBM] * (2 * n)),
        input_output_aliases={a: a for a in range(2 * n)},
        compiler_params=pltpu.CompilerParams(has_side_effects=DATAFLOW),
    )(*thru, *sems, after)
    return outs[n:]


def _pair_gather(arrs, layer, name):
    n = len(arrs)

    def body(*refs):
        outs = refs[n:2 * n]
        send_sems, recv_sems = refs[2 * n:]
        x, y, c = _mesh_pos()
        cps = [pltpu.make_async_remote_copy(src_ref=outs[a].at[layer, c], dst_ref=outs[a].at[layer, c],
                                            send_sem=send_sems.at[a], recv_sem=recv_sems.at[a],
                                            device_id=(x, y, 1 - c), device_id_type=MESH)
               for a in range(n)]
        for cp in cps:
            cp.start()
        for cp in cps:
            cp.wait()

    return pl.pallas_call(
        body, name=name, out_shape=tuple(jax.ShapeDtypeStruct(a.shape, a.dtype) for a in arrs),
        in_specs=[ANY] * n, out_specs=tuple([ANY] * n),
        input_output_aliases={a: a for a in range(n)},
        scratch_shapes=[pltpu.SemaphoreType.DMA((n,)), pltpu.SemaphoreType.DMA((n,))],
    )(*arrs)


N_DEV = 8


def _allreduce_small(pack, name):
    r = pack.shape[0]

    def body(p_ref, o_ref, land, send_sems, recv_sems):
        x, y, c = _mesh_pos()
        me = 4 * x + 2 * y + c
        cps = []
        for k in range(1, N_DEV):
            peer = (x ^ (k >> 2), y ^ ((k >> 1) & 1), c ^ (k & 1))
            cps.append(pltpu.make_async_remote_copy(src_ref=p_ref, dst_ref=land.at[me], send_sem=send_sems.at[k - 1],
                                                    recv_sem=recv_sems.at[k - 1], device_id=peer, device_id_type=MESH))
        for cp in cps:
            cp.start()
        land[me] = p_ref[...]
        for cp in cps:
            cp.wait()
        total = land[0]
        for d in range(1, N_DEV):
            total = total + land[d]
        o_ref[...] = total

    vm = pl.BlockSpec(memory_space=pltpu.VMEM)
    return pl.pallas_call(
        body, name=name, out_shape=jax.ShapeDtypeStruct(pack.shape, F32),
        in_specs=[vm], out_specs=vm,
        scratch_shapes=[pltpu.VMEM((N_DEV, r, LANES), F32), pltpu.SemaphoreType.DMA((N_DEV - 1,)),
                        pltpu.SemaphoreType.DMA((N_DEV - 1,))],
    )(pack)


BIG_ROWS = 128


def _cast_layer(w, layer, name):
    _, r, cdim = w.shape
    tr = BIG_ROWS

    def body(w_ref, o_ref):
        o_ref[...] = w_ref[...].astype(o_ref.dtype)

    return pl.pallas_call(
        body, name=name, out_shape=jax.ShapeDtypeStruct((r, cdim), MXU_DTYPE),
        grid=(r // tr,), in_specs=[pl.BlockSpec((None, tr, cdim), lambda i: (layer, i, 0))],
        out_specs=pl.BlockSpec((tr, cdim), lambda i: (i, 0)),
        compiler_params=_params(("parallel",)),
    )(w)


def _pair_sum(parts, sib, which, out_dtype, name):
    k, _, r, cdim = parts.shape
    tr = BIG_ROWS

    def body(sel_ref, p_ref, s_ref, o_ref):
        o_ref[...] = (p_ref[...] + s_ref[...]).astype(o_ref.dtype)

    grid_spec = pltpu.PrefetchScalarGridSpec(
        num_scalar_prefetch=1, grid=(k, r // tr),
        in_specs=[pl.BlockSpec((None, None, tr, cdim), lambda l, i, sel: (l, sel[0], i, 0)),
                  pl.BlockSpec((None, tr, cdim), lambda l, i, sel: (l, i, 0))],
        out_specs=pl.BlockSpec((None, tr, cdim), lambda l, i, sel: (l, i, 0)))
    return pl.pallas_call(
        body, name=name, out_shape=jax.ShapeDtypeStruct((k, r, cdim), out_dtype), grid_spec=grid_spec,
        compiler_params=_params(("parallel", "parallel")),
    )(which.reshape(1).astype(jnp.int32), parts, sib)


def _sum_lead(parts, into, layer, which, name):
    k, r, cdim = parts.shape
    tr = BIG_ROWS

    def body(sel_ref, p_ref, _, o_ref):
        total = p_ref[0].astype(F32)
        for a in range(1, k):
            total = total + p_ref[a].astype(F32)
        o_ref[...] = total

    grid_spec = pltpu.PrefetchScalarGridSpec(
        num_scalar_prefetch=1, grid=(r // tr,),
        in_specs=[pl.BlockSpec((k, tr, cdim), lambda i, sel: (0, i, 0)), ANY],
        out_specs=pl.BlockSpec((None, None, tr, cdim), lambda i, sel: (layer, sel[0], i, 0)))
    return pl.pallas_call(
        body, name=name, out_shape=jax.ShapeDtypeStruct(into.shape, F32), grid_spec=grid_spec,
        input_output_aliases={2: 0},
        compiler_params=_params(("parallel",)),
    )(which.reshape(1).astype(jnp.int32), parts, into)


def _adam_math(w, g, m, v):
    m2 = ADAM_B1 * m + (1.0 - ADAM_B1) * g
    v2 = ADAM_B2 * v + (1.0 - ADAM_B2) * (g * g)
    m_hat = m2 / (1.0 - ADAM_B1 ** ADAM_STEP)
    v_hat = v2 / (1.0 - ADAM_B2 ** ADAM_STEP)
    delta = -ADAM_LR * (m_hat / (jnp.sqrt(v_hat) + ADAM_EPS) + ADAM_WD * w)
    return delta, m2, v2


def _adam_big(w, g, m, v, name):
    nl, r, cdim = w.shape
    tr = BIG_ROWS

    def body(w_ref, g_ref, m_ref, v_ref, d_ref, mo_ref, vo_ref):
        delta, m2, v2 = _adam_math(w_ref[...], g_ref[...], m_ref[...], v_ref[...])
        d_ref[...] = delta
        mo_ref[...] = m2
        vo_ref[...] = v2

    blk = pl.BlockSpec((None, tr, cdim), lambda l, i: (l, i, 0))
    shp = jax.ShapeDtypeStruct(w.shape, F32)
    return pl.pallas_call(
        body, name=name, out_shape=(shp, shp, shp),
        grid=(nl, r // tr), in_specs=[blk] * 4, out_specs=(blk, blk, blk),
        compiler_params=_params(("parallel", "parallel")),
    )(w, g, m, v)


def _adam_cols_major(w, g, m, v, name):
    cdim, nl, r = w.shape
    tc = BIG_ROWS

    def body(w_ref, g_ref, m_ref, v_ref, d_ref, mo_ref, vo_ref):
        delta, m2, v2 = _adam_math(w_ref[...], g_ref[...], m_ref[...], v_ref[...])
        d_ref[...] = delta
        mo_ref[...] = m2
        vo_ref[...] = v2

    blk = pl.BlockSpec((tc, nl, r), lambda i: (i, 0, 0))
    shp = jax.ShapeDtypeStruct(w.shape, F32)
    return pl.pallas_call(
        body, name=name, out_shape=(shp, shp, shp),
        grid=(pl.cdiv(cdim, tc),), in_specs=[blk] * 4, out_specs=(blk, blk, blk),
        compiler_params=_params(("parallel",)),
    )(w, g, m, v)


def _adam_small(ws, gs, ms, vs, name):
    n = len(ws)

    def body(*refs):
        w_refs, g_refs, m_refs, v_refs = (refs[k * n:(k + 1) * n] for k in range(4))
        d_refs, mo_refs, vo_refs = (refs[(4 + k) * n:(5 + k) * n] for k in range(3))
        for a in range(n):
            delta, m2, v2 = _adam_math(w_refs[a][...], g_refs[a][...], m_refs[a][...], v_refs[a][...])
            d_refs[a][...] = delta
            mo_refs[a][...] = m2
            vo_refs[a][...] = v2

    shapes = tuple(jax.ShapeDtypeStruct(w.shape, F32) for w in ws)
    vm = pl.BlockSpec(memory_space=pltpu.VMEM)
    outs = pl.pallas_call(body, name=name, out_shape=shapes * 3, in_specs=[vm] * (4 * n),
                          out_specs=tuple([vm] * (3 * n)))(*ws, *gs, *ms, *vs)
    return outs[:n], outs[n:2 * n], outs[2 * n:]


PACK_TILE = SUBLANES * LANES


def _pack(arrays):
    rows = []
    for a in arrays:
        flat = a.reshape(-1)
        pad = (-flat.shape[0]) % PACK_TILE
        if pad:
            flat = jnp.concatenate([flat, jnp.zeros((pad,), flat.dtype)])
        rows.append(flat.reshape(-1, LANES))
    return jnp.concatenate(rows, axis=0)


def _unpack(pack, shapes):
    outs, row = [], 0
    for shp in shapes:
        n = int(np.prod(shp))
        nrows = -(-n // PACK_TILE) * SUBLANES
        outs.append(pack[row:row + nrows].reshape(-1)[:n].reshape(shp))
        row += nrows
    return outs


SMALL = ["norm_w", "ssd_conv_b", "ssd_dt_bias", "ssd_a_log", "ssd_d", "ssd_norm_w", "attn_sinks",
         "conf_dw_b", "conf_ln_w", "conf_ln_b"]
WEIGHTS = ["norm_w", "w_in", "ssd_conv_w", "ssd_conv_b", "ssd_dt_bias", "ssd_a_log", "ssd_d", "ssd_norm_w",
           "attn_sinks", "conf_dw_w", "conf_dw_b", "conf_ln_w", "conf_ln_b", "w_out", "final_norm_w"]


def kernel(x, norm_w, w_in, ssd_conv_w, ssd_conv_b, ssd_dt_bias, ssd_a_log, ssd_d, ssd_norm_w, attn_sinks, conf_dw_w, conf_dw_b, conf_ln_w, conf_ln_b, w_out, final_norm_w, loss_target, m_norm_w, m_w_in, m_ssd_conv_w, m_ssd_conv_b, m_ssd_dt_bias, m_ssd_a_log, m_ssd_d, m_ssd_norm_w, m_attn_sinks, m_conf_dw_w, m_conf_dw_b, m_conf_ln_w, m_conf_ln_b, m_w_out, m_final_norm_w, v_norm_w, v_w_in, v_ssd_conv_w, v_ssd_conv_b, v_ssd_dt_bias, v_ssd_a_log, v_ssd_d, v_ssd_norm_w, v_attn_sinks, v_conf_dw_w, v_conf_dw_b, v_conf_ln_w, v_conf_ln_b, v_w_out, v_final_norm_w):
    w = dict(norm_w=norm_w, w_in=w_in, ssd_conv_w=ssd_conv_w, ssd_conv_b=ssd_conv_b, ssd_dt_bias=ssd_dt_bias,
             ssd_a_log=ssd_a_log, ssd_d=ssd_d, ssd_norm_w=ssd_norm_w, attn_sinks=attn_sinks, conf_dw_w=conf_dw_w,
             conf_dw_b=conf_dw_b, conf_ln_w=conf_ln_w, conf_ln_b=conf_ln_b, w_out=w_out, final_norm_w=final_norm_w)
    m = dict(norm_w=m_norm_w, w_in=m_w_in, ssd_conv_w=m_ssd_conv_w, ssd_conv_b=m_ssd_conv_b,
             ssd_dt_bias=m_ssd_dt_bias, ssd_a_log=m_ssd_a_log, ssd_d=m_ssd_d, ssd_norm_w=m_ssd_norm_w,
             attn_sinks=m_attn_sinks, conf_dw_w=m_conf_dw_w, conf_dw_b=m_conf_dw_b, conf_ln_w=m_conf_ln_w,
             conf_ln_b=m_conf_ln_b, w_out=m_w_out, final_norm_w=m_final_norm_w)
    v = dict(norm_w=v_norm_w, w_in=v_w_in, ssd_conv_w=v_ssd_conv_w, ssd_conv_b=v_ssd_conv_b,
             ssd_dt_bias=v_ssd_dt_bias, ssd_a_log=v_ssd_a_log, ssd_d=v_ssd_d, ssd_norm_w=v_ssd_norm_w,
             attn_sinks=v_attn_sinks, conf_dw_w=v_conf_dw_w, conf_dw_b=v_conf_dw_b, conf_ln_w=v_conf_ln_w,
             conf_ln_b=v_conf_ln_b, w_out=v_w_out, final_norm_w=v_final_norm_w)
    depth = w_in.shape[0]
    me = 2 * lax.axis_index("x") + lax.axis_index("y")

    assert depth == 2
    w_in_b = [_cast_layer(w_in, li, name=f"cast_w_in_l{li}") for li in range(depth)]
    w_out_b = [_cast_layer(w_out, li, name=f"cast_w_out_l{li}") for li in range(depth)]
    own0 = [w_in_b[0].reshape((2, -1) + w_in_b[0].shape[1:]), w_out_b[0].reshape((2, -1) + w_out_b[0].shape[1:]),
            ssd_conv_w, conf_dw_w]
    gathered0 = _gather_weights(own0[:2], own0[2:], name="gather_weights_l0")
    g_in0, g_out0, g_conv, g_dw = [lax.dynamic_update_index_in_dim(g_all, mine, me, 0)
                                   for g_all, mine in zip(gathered0, own0)]
    own1 = [w_in_b[1], w_out_b[1]]
    pending1, token1 = _split_start(own1, "bcast", gathered0[0], name="gather_l1_start")

    def small_full(li):
        return (jnp.concatenate([g_conv[p, li] for p in range(N_CHIPS)], axis=1),
                jnp.concatenate([g_dw[p, li] for p in range(N_CHIPS)], axis=1))

    def params_l0(_):
        w_in_p = _padded_from_chips([g_in0[p].reshape(w_in_b[0].shape) for p in range(N_CHIPS)])
        w_out_full = g_out0.reshape(-1, g_out0.shape[-1])
        return _layer_params(0, w_in_p, w_out_full, *small_full(0), w)

    def params_l1(layer_input):
        landed = _split_wait(pending1, len(own1), "bcast", layer_input, name="gather_l1_wait")
        g_in1, g_out1 = [lax.dynamic_update_index_in_dim(g_all, mine, me, 0) for g_all, mine in zip(landed, own1)]
        w_in_p = _padded_from_chips([g_in1[p] for p in range(N_CHIPS)])
        return _layer_params(1, w_in_p, g_out1.reshape(-1, g_out1.shape[-1]), *small_full(1), w)

    c = lax.axis_index("c")
    cols = w_in.shape[2]
    rows_out = w_out.shape[1]

    def grad_parts(g):
        dw = g["w_in_p"]
        p_in = jnp.stack([_chip_part_from_padded(dw, p, cols) for p in range(N_CHIPS)])
        return [p_in.reshape(N_CHIPS, 2, dw.shape[0] // 2, cols),
                g["w_out"].reshape(N_CHIPS, 2, rows_out // 2, D_MODEL)]

    def pair_sums(parts, sib, tag):
        return [_pair_sum(p, sb, c, MXU_DTYPE, name=f"grad_pair_sum_{k}_{tag}")
                for k, (p, sb) in enumerate(zip(parts, sib))]

    split = {"reduced": [lax.empty((depth, 2, w_in.shape[1] // 2, cols), F32),
                         lax.empty((depth, 2, rows_out // 2, D_MODEL), F32)]}

    def chip_sums(landed, sent, li):
        filled = [lax.dynamic_update_index_in_dim(r, lax.dynamic_index_in_dim(sk, me, 0, keepdims=False), me, 0)
                  for r, sk in zip(landed, sent)]
        halves = [_sum_lead(r, into, li, c, name=f"grad_chip_sum_{k}_l{li}")
                  for k, (r, into) in enumerate(zip(filled, split["reduced"]))]
        split["reduced"] = list(_pair_gather(halves, li, name=f"grad_pair_gather_l{li}"))

    def on_grads(li, g):
        if li != depth - 1:
            return None
        parts = grad_parts(g)
        swap_state, swap_token = _split_start(parts, "swap", g["w_out"], name="grad_swap_l1_start")

        def after_dycat(dycat):
            sib = _split_wait(swap_state, len(parts), "swap", dycat, name="grad_swap_l1_wait")
            split["sent"] = pair_sums(parts, sib, "l1")
            split["scatter"], token = _split_start(split["sent"], "scatter", split["sent"][0],
                                                   name="grad_scatter_l1_start")
            return token

        def after_attn(dproj):
            landed = _split_wait(split["scatter"], len(parts), "scatter", dproj, name="grad_scatter_l1_wait")
            chip_sums(landed, split["sent"], depth - 1)

        return {"start_token": swap_token, "after_dycat": after_dycat, "after_attn": after_attn}

    loss, grad_x, grads, dfinal = _local_step(x, loss_target, [params_l0, params_l1], final_norm_w,
                                              first_after=token1, on_grads=on_grads)

    small_list = [grads[li][n] for li in range(depth) for n in SMALL]
    small_list += [grads[li][n] for li in range(depth) for n in ("ssd_conv_w", "conf_dw_w")]
    small_list += [dfinal, loss.reshape(1)]
    small_shapes = [a.shape for a in small_list]
    reduced = _unpack(_allreduce_small(_pack(small_list), name="allreduce_small"), small_shapes)
    ns = len(SMALL)
    g = {n: jnp.stack([reduced[li * ns + i] for li in range(depth)]) for i, n in enumerate(SMALL)}
    conv_w_cols, dw_w_cols = ssd_conv_w.shape[2], conf_dw_w.shape[2]
    g["ssd_conv_w"] = jnp.stack([lax.dynamic_slice_in_dim(reduced[depth * ns + 2 * li], me * conv_w_cols,
                                                          conv_w_cols, axis=1) for li in range(depth)])
    g["conf_dw_w"] = jnp.stack([lax.dynamic_slice_in_dim(reduced[depth * ns + 2 * li + 1], me * dw_w_cols,
                                                         dw_w_cols, axis=1) for li in range(depth)])
    g["final_norm_w"] = reduced[-2]
    loss_total = reduced[-1][0]

    parts0 = grad_parts(grads[0])
    sent0 = pair_sums(parts0, _pair_swap_halves(parts0, name="grad_pair_swap_l0"), "l0")
    chip_sums(_chip_scatter(sent0, name="grad_chip_scatter_l0"), sent0, 0)
    g_w_in = split["reduced"][0].reshape(w_in.shape)
    g_w_out = split["reduced"][1].reshape(w_out.shape)

    outs_g, outs_d, outs_m, outs_v = {"w_in": g_w_in, "w_out": g_w_out}, {}, {}, {}
    to_cols, from_cols = (2, 0, 1), (1, 2, 0)
    outs_d["w_in"], outs_m["w_in"], outs_v["w_in"] = [
        jnp.transpose(a, from_cols) for a in _adam_cols_major(
            *[jnp.transpose(a, to_cols) for a in (w_in, g_w_in, m_w_in, v_w_in)], name="adam_w_in")]
    outs_d["w_out"], outs_m["w_out"], outs_v["w_out"] = _adam_big(w_out, g_w_out, m_w_out, v_w_out,
                                                                  name="adam_w_out")
    small_names = [n for n in WEIGHTS if n not in ("w_in", "w_out")]
    def as2d(a):
        return a.reshape(1, -1) if a.ndim == 1 else a

    deltas, new_ms, new_vs = _adam_small(*[[as2d(src[n]) for n in small_names] for src in (w, g, m, v)],
                                         name="adam_small")
    for n, dn, mn, vn in zip(small_names, deltas, new_ms, new_vs):
        outs_g[n], outs_d[n], outs_m[n], outs_v[n] = (g[n], dn.reshape(w[n].shape), mn.reshape(w[n].shape),
                                                      vn.reshape(w[n].shape))
    return (loss_total, grad_x, *[outs_g[n] for n in WEIGHTS], *[outs_d[n] for n in WEIGHTS],
            *[outs_m[n] for n in WEIGHTS], *[outs_v[n] for n in WEIGHTS])
```

```python
import functools
import math

import jax
import jax.numpy as jnp
import numpy as np
from jax import lax
from jax.experimental import pallas as pl
from jax.experimental.pallas import tpu as pltpu

F32 = jnp.float32
BF16 = jnp.bfloat16
MXU_DTYPE = BF16

D_MODEL = 1024
DEPTH = 2
SSD_HEADS = 16
SSD_HEAD_DIM = 64
SSD_STATE = 128
SSD_CONV = 4
CHUNK = 128
SSD_CONV_DIM = 1536
ATTN_HEAD_DIM = 64
ATTN_Q_HEADS = 8
WINDOW = 128
CONF_WIDTH = 512
CONF_KERNEL = 31
MIX_WIDTH = 2048
D_IN_PROJ = 5392
EPS = 1e-5

ADAM_LR = 0.001
ADAM_B1 = 0.9
ADAM_B2 = 0.999
ADAM_EPS = 1e-08
ADAM_WD = 0.01
ADAM_STEP = 10

LANES = 128
SUBLANES = 8
VMEM_LIMIT = 48 * 1024 * 1024

NP = 5632
OFF_ZA, OFF_Q, OFF_K, OFF_V, OFF_DT = 0, 512, 1024, 1152, 1280
ATTN_GROUP = 1536
OFF_XBC = 1536
OFF_CONF = 3072
OFF_ZS = 4096
OFF_ZC = 5120
SECTIONS = ((0, 1024, OFF_ZS), (1024, 1536, OFF_ZA), (1536, 2048, OFF_ZC), (2048, 3584, OFF_XBC),
            (3584, 3600, OFF_DT), (3600, 4368, OFF_Q), (4368, 5392, OFF_CONF))

YCAT_ATTN, YCAT_CONF = 1024, 1536
ANY = pl.BlockSpec(memory_space=pl.ANY)

NN = (((1,), (0,)), ((), ()))
NT = (((1,), (1,)), ((), ()))
TN = (((0,), (0,)), ((), ()))


def _params(sem):
    return pltpu.CompilerParams(dimension_semantics=sem, vmem_limit_bytes=VMEM_LIMIT)


def _dot(a, b, dims=NN):
    return lax.dot_general(a.astype(MXU_DTYPE), b.astype(MXU_DTYPE), dims, preferred_element_type=F32)


def _split_bf16(a, passes):
    pieces = []
    r = a
    for _ in range(passes):
        p = r.astype(BF16)
        pieces.append(p)
        r = r - p.astype(F32)
    return pieces


def _xdot(a, sel, dims=NN, passes=2):
    out = None
    for p in _split_bf16(a, passes):
        t = lax.dot_general(p, sel, dims, preferred_element_type=F32)
        out = t if out is None else out + t
    return out


def _xdot_r(sel, b, dims=NN, passes=3):
    out = None
    for p in _split_bf16(b, passes):
        t = lax.dot_general(sel, p, dims, preferred_element_type=F32)
        out = t if out is None else out + t
    return out


def _sigmoid(x):
    return 1.0 / (1.0 + jnp.exp(-x))


def _silu(x):
    return x * _sigmoid(x)


def _dsilu(x):
    s = _sigmoid(x)
    return s * (1.0 + x * (1.0 - s))


def _softplus(x):
    return jnp.maximum(x, 0.0) + jnp.log(1.0 + jnp.exp(-jnp.abs(x)))


def _rowsum8(x):
    r, c = x.shape
    return jnp.sum(x.reshape(r // SUBLANES, SUBLANES, c), axis=0)


def _iota(shape, dim):
    return lax.broadcasted_iota(jnp.int32, shape, dim)


def _matmul(a, b, form, out_dtype, tm, tn, tk, name, residual=None, after=None):
    if form == "nn":
        (m, k), n = a.shape, b.shape[1]
    elif form == "nt":
        (m, k), n = a.shape, b.shape[0]
    else:
        (k, m), n = a.shape, b.shape[1]
    tm, tn, tk = min(tm, m), min(tn, n), min(tk, k)
    assert m % tm == 0 and n % tn == 0 and k % tk == 0, (name, m, n, k, tm, tn, tk)
    if form == "nn":
        a_spec = pl.BlockSpec((tm, tk), lambda i, j, s: (i, s))
        b_spec = pl.BlockSpec((tk, tn), lambda i, j, s: (s, j))
        dims = NN
    elif form == "nt":
        (m, k), n = a.shape, b.shape[0]
        a_spec = pl.BlockSpec((tm, tk), lambda i, j, s: (i, s))
        b_spec = pl.BlockSpec((tn, tk), lambda i, j, s: (j, s))
        dims = NT
    else:
        (k, m), n = a.shape, b.shape[1]
        a_spec = pl.BlockSpec((tk, tm), lambda i, j, s: (s, i))
        b_spec = pl.BlockSpec((tk, tn), lambda i, j, s: (s, j))
        dims = TN
    nk = k // tk
    has_res = residual is not None
    deps = [] if after is None else [after]

    def body_single(a_ref, b_ref, *rest):
        o = _dot(a_ref[...], b_ref[...], dims)
        if has_res:
            o = o + rest[0][...]
        rest[-1][...] = o.astype(out_dtype)

    def body(a_ref, b_ref, *rest):
        r_ref = rest[0] if has_res else None
        o_ref, acc = rest[-2:]
        s = pl.program_id(2)

        @pl.when(s == 0)
        def _():
            acc[...] = jnp.zeros_like(acc)

        acc[...] += _dot(a_ref[...], b_ref[...], dims)

        @pl.when(s == nk - 1)
        def _():
            o = acc[...]
            if has_res:
                o = o + r_ref[...]
            o_ref[...] = o.astype(out_dtype)

    in_specs = [a_spec, b_spec]
    args = [a, b]
    if has_res:
        in_specs.append(pl.BlockSpec((tm, tn), lambda i, j, s: (i, j)))
        args.append(residual)
    in_specs += [ANY] * len(deps)
    args += deps
    return pl.pallas_call(
        body_single if nk == 1 else body, name=name,
        out_shape=jax.ShapeDtypeStruct((m, n), out_dtype),
        grid=(m // tm, n // tn, nk),
        in_specs=in_specs,
        out_specs=pl.BlockSpec((tm, tn), lambda i, j, s: (i, j)),
        scratch_shapes=[] if nk == 1 else [pltpu.VMEM((tm, tn), F32)],
        compiler_params=_params(("parallel", "parallel", "arbitrary")),
    )(*args)


ROW_TILE = 256


def _rmsnorm_fwd(x, w, name, after=None):
    t, d = x.shape
    tm = ROW_TILE
    deps = [] if after is None else [after]

    def body(x_ref, w_ref, *rest):
        o_ref, ot_ref = rest[len(deps):]
        xv = x_ref[...]
        rstd = lax.rsqrt(jnp.mean(xv * xv, axis=-1, keepdims=True) + EPS)
        h = xv * rstd * w_ref[...]
        o_ref[...] = h.astype(o_ref.dtype)
        ot_ref[...] = h.T.astype(ot_ref.dtype)

    return pl.pallas_call(
        body, name=name,
        out_shape=(jax.ShapeDtypeStruct((t, d), MXU_DTYPE), jax.ShapeDtypeStruct((d, t), MXU_DTYPE)),
        grid=(t // tm,),
        in_specs=[pl.BlockSpec((tm, d), lambda i: (i, 0)), pl.BlockSpec((1, d), lambda i: (0, 0))]
        + [ANY] * len(deps),
        out_specs=(pl.BlockSpec((tm, d), lambda i: (i, 0)), pl.BlockSpec((d, tm), lambda i: (0, i))),
        compiler_params=_params(("parallel",)),
    )(x, w, *deps)


def _rmsnorm_bwd(dh, x, w, dres, name):
    t, d = x.shape
    tm = ROW_TILE
    nt = t // tm

    def body(dh_ref, x_ref, w_ref, dr_ref, dx_ref, dw_ref, acc):
        i = pl.program_id(0)

        @pl.when(i == 0)
        def _():
            acc[...] = jnp.zeros_like(acc)

        xv = x_ref[...]
        rstd = lax.rsqrt(jnp.mean(xv * xv, axis=-1, keepdims=True) + EPS)
        xh = xv * rstd
        dhv = dh_ref[...]
        g = dhv * w_ref[...]
        dx_ref[...] = dr_ref[...] + rstd * (g - xh * jnp.mean(g * xh, axis=-1, keepdims=True))
        acc[...] += _rowsum8(dhv * xh)

        @pl.when(i == nt - 1)
        def _():
            dw_ref[...] = jnp.sum(acc[...], axis=0, keepdims=True)

    row = pl.BlockSpec((tm, d), lambda i: (i, 0))
    vec = pl.BlockSpec((1, d), lambda i: (0, 0))
    return pl.pallas_call(
        body, name=name,
        out_shape=(jax.ShapeDtypeStruct((t, d), F32), jax.ShapeDtypeStruct((1, d), F32)),
        grid=(nt,),
        in_specs=[row, row, vec, row],
        out_specs=(row, vec),
        scratch_shapes=[pltpu.VMEM((SUBLANES, d), F32)],
        compiler_params=_params(("arbitrary",)),
    )(dh, x, w, dres)


def _loss_head(xf, target, w, name):
    t, d = xf.shape
    tm = ROW_TILE
    nt = t // tm

    def body(x_ref, t_ref, w_ref, loss_ref, dx_ref, dw_ref, lacc, wacc):
        i = pl.program_id(0)

        @pl.when(i == 0)
        def _():
            lacc[...] = jnp.zeros_like(lacc)
            wacc[...] = jnp.zeros_like(wacc)

        xv = x_ref[...]
        rstd = lax.rsqrt(jnp.mean(xv * xv, axis=-1, keepdims=True) + EPS)
        xh = xv * rstd
        err = xh * w_ref[...] - t_ref[...]
        lacc[...] += jnp.sum(err * err)
        dy = err * (1.0 / d)
        g = dy * w_ref[...]
        dx_ref[...] = rstd * (g - xh * jnp.mean(g * xh, axis=-1, keepdims=True))
        wacc[...] += _rowsum8(dy * xh)

        @pl.when(i == nt - 1)
        def _():
            loss_ref[...] = lacc[...] * (0.5 / d)
            dw_ref[...] = jnp.sum(wacc[...], axis=0, keepdims=True)

    row = pl.BlockSpec((tm, d), lambda i: (i, 0))
    vec = pl.BlockSpec((1, d), lambda i: (0, 0))
    return pl.pallas_call(
        body, name=name,
        out_shape=(jax.ShapeDtypeStruct((SUBLANES, LANES), F32), jax.ShapeDtypeStruct((t, d), F32),
                   jax.ShapeDtypeStruct((1, d), F32)),
        grid=(nt,),
        in_specs=[row, row, vec],
        out_specs=(pl.BlockSpec((SUBLANES, LANES), lambda i: (0, 0)), row, vec),
        scratch_shapes=[pltpu.VMEM((SUBLANES, LANES), F32), pltpu.VMEM((SUBLANES, d), F32)],
        compiler_params=_params(("arbitrary",)),
    )(xf, target, w)


def _glu_fwd(proj, name):
    t = proj.shape[0]
    tm, cw = ROW_TILE, CONF_WIDTH

    def body(a_ref, g_ref, o_ref):
        o_ref[...] = a_ref[...] * _sigmoid(g_ref[...])

    return pl.pallas_call(
        body, name=name,
        out_shape=jax.ShapeDtypeStruct((t, cw), F32),
        grid=(t // tm,),
        in_specs=[pl.BlockSpec((tm, cw), lambda i: (i, OFF_CONF // cw)),
                  pl.BlockSpec((tm, cw), lambda i: (i, OFF_CONF // cw + 1))],
        out_specs=pl.BlockSpec((tm, cw), lambda i: (i, 0)),
        compiler_params=_params(("parallel",)),
    )(proj, proj)


def _glu_bwd(dc0, proj, dproj, name):
    t = proj.shape[0]
    tm, cw = ROW_TILE, CONF_WIDTH

    def body(d_ref, a_ref, g_ref, _, o_ref):
        s = _sigmoid(g_ref[...])
        dv = d_ref[...]
        o_ref[:, :cw] = (dv * s).astype(o_ref.dtype)
        o_ref[:, cw:] = (dv * a_ref[...] * s * (1.0 - s)).astype(o_ref.dtype)

    return pl.pallas_call(
        body, name=name,
        out_shape=jax.ShapeDtypeStruct(dproj.shape, dproj.dtype),
        grid=(t // tm,),
        in_specs=[pl.BlockSpec((tm, cw), lambda i: (i, 0)),
                  pl.BlockSpec((tm, cw), lambda i: (i, OFF_CONF // cw)),
                  pl.BlockSpec((tm, cw), lambda i: (i, OFF_CONF // cw + 1)), ANY],
        out_specs=pl.BlockSpec((tm, 2 * cw), lambda i: (i, OFF_CONF // (2 * cw))),
        input_output_aliases={3: 0},
        compiler_params=_params(("parallel",)),
    )(dc0, proj, proj, dproj)


def _conf_post_fwd(c1, proj, ln_w, ln_b, ycat, name):
    t = c1.shape[0]
    tm, cw = ROW_TILE, CONF_WIDTH

    def body(c_ref, z_ref, w_ref, b_ref, _, o_ref):
        cv = c_ref[...]
        xc = cv - jnp.mean(cv, axis=-1, keepdims=True)
        rstd = lax.rsqrt(jnp.mean(xc * xc, axis=-1, keepdims=True) + EPS)
        c2 = xc * rstd * w_ref[...] + b_ref[...]
        o_ref[...] = (_silu(c2) * _silu(z_ref[...])).astype(o_ref.dtype)

    vec = pl.BlockSpec((1, cw), lambda i: (0, 0))
    return pl.pallas_call(
        body, name=name,
        out_shape=jax.ShapeDtypeStruct(ycat.shape, ycat.dtype),
        grid=(t // tm,),
        in_specs=[pl.BlockSpec((tm, cw), lambda i: (i, 0)),
                  pl.BlockSpec((tm, cw), lambda i: (i, OFF_ZC // cw)), vec, vec, ANY],
        out_specs=pl.BlockSpec((tm, cw), lambda i: (i, YCAT_CONF // cw)),
        input_output_aliases={4: 0},
        compiler_params=_params(("parallel",)),
    )(c1, proj, ln_w, ln_b, ycat)


def _conf_post_bwd(dycat, c1, proj, ln_w, ln_b, dproj, name):
    t = c1.shape[0]
    tm, cw = ROW_TILE, CONF_WIDTH
    nt = t // tm

    def body(dy_ref, c_ref, z_ref, w_ref, b_ref, _, dc_ref, dz_ref, dw_ref, db_ref, wacc, bacc):
        i = pl.program_id(0)

        @pl.when(i == 0)
        def _():
            wacc[...] = jnp.zeros_like(wacc)
            bacc[...] = jnp.zeros_like(bacc)

        cv = c_ref[...]
        xc = cv - jnp.mean(cv, axis=-1, keepdims=True)
        rstd = lax.rsqrt(jnp.mean(xc * xc, axis=-1, keepdims=True) + EPS)
        xh = xc * rstd
        c2 = xh * w_ref[...] + b_ref[...]
        zv = z_ref[...]
        dy = dy_ref[...]
        dz_ref[...] = (dy * _silu(c2) * _dsilu(zv)).astype(dz_ref.dtype)
        dc2 = dy * _silu(zv) * _dsilu(c2)
        bacc[...] += _rowsum8(dc2)
        wacc[...] += _rowsum8(dc2 * xh)
        dxh = dc2 * w_ref[...]
        dc_ref[...] = rstd * (dxh - jnp.mean(dxh, axis=-1, keepdims=True)
                              - xh * jnp.mean(dxh * xh, axis=-1, keepdims=True))

        @pl.when(i == nt - 1)
        def _():
            dw_ref[...] = jnp.sum(wacc[...], axis=0, keepdims=True)
            db_ref[...] = jnp.sum(bacc[...], axis=0, keepdims=True)

    row = pl.BlockSpec((tm, cw), lambda i: (i, 0))
    vec = pl.BlockSpec((1, cw), lambda i: (0, 0))
    return pl.pallas_call(
        body, name=name,
        out_shape=(jax.ShapeDtypeStruct((t, cw), F32), jax.ShapeDtypeStruct(dproj.shape, dproj.dtype),
                   jax.ShapeDtypeStruct((1, cw), F32), jax.ShapeDtypeStruct((1, cw), F32)),
        grid=(nt,),
        in_specs=[pl.BlockSpec((tm, cw), lambda i: (i, YCAT_CONF // cw)), row,
                  pl.BlockSpec((tm, cw), lambda i: (i, OFF_ZC // cw)), vec, vec, ANY],
        out_specs=(row, pl.BlockSpec((tm, cw), lambda i: (i, OFF_ZC // cw)), vec, vec),
        input_output_aliases={5: 1},
        scratch_shapes=[pltpu.VMEM((SUBLANES, cw), F32), pltpu.VMEM((SUBLANES, cw), F32)],
        compiler_params=_params(("arbitrary",)),
    )(dycat, c1, proj, ln_w, ln_b, dproj)


CONV_TILE = 512
CONV_COLS = 512
CONV_SUB_ROWS = 128
CONV_SUB_COLS = LANES


def _conv_halo(k):
    return SUBLANES if k - 1 <= SUBLANES else 32


def _conv_subtiles(tm, cw):
    return [(r0, c0) for r0 in range(0, tm, CONV_SUB_ROWS) for c0 in range(0, cw, CONV_SUB_COLS)]


def _conv_use_shifted(k):
    return k > SUBLANES


def _conv_shift_scratch(k, rows, cw):
    return [pltpu.VMEM((SUBLANES - 1, rows - SUBLANES, cw), F32)] if _conv_use_shifted(k) else []


def _conv_fill_shifted(ext, sh):
    n = sh.shape[1]
    for b in range(1, SUBLANES):
        sh[b - 1] = ext[b:b + n, :]


def _conv_rows(ext, sh, start, rows, cs):
    b = start % SUBLANES
    if b == 0 or not sh:
        return ext[start:start + rows, cs]
    return sh[0][b - 1, start - b:start - b + rows, cs]


def _conv_fwd(src, col0, width, w, bias, k, seq, name):
    t = src.shape[0]
    tm, cw, halo = CONV_TILE, CONV_COLS, _conv_halo(k)
    sr, sc = CONV_SUB_ROWS, CONV_SUB_COLS
    p = k - 1
    cb0 = col0 // cw
    kp = w.shape[0]

    shifted = _conv_use_shifted(k)

    def body(x_ref, h_ref, w_ref, b_ref, o_ref, ext, *sh):
        i = pl.program_id(0)
        seq_start = (i * tm) % seq == 0
        ext[halo:, :] = x_ref[...]
        ext[:halo, :] = jnp.where(seq_start, 0.0, h_ref[...])
        if shifted:
            _conv_fill_shifted(ext, sh[0])
        for r0, c0 in _conv_subtiles(tm, cw):
            cs = slice(c0, c0 + sc)
            acc = jnp.zeros((sr, sc), F32) + b_ref[:, cs]
            for j in range(k):
                acc = acc + w_ref[j:j + 1, cs] * _conv_rows(ext, sh, r0 + halo - p + j, sr, cs)
            o_ref[r0:r0 + sr, cs] = acc

    return pl.pallas_call(
        body, name=name,
        out_shape=jax.ShapeDtypeStruct((t, width), F32),
        grid=(t // tm, width // cw),
        in_specs=[pl.BlockSpec((tm, cw), lambda i, j: (i, cb0 + j)),
                  pl.BlockSpec((halo, cw), lambda i, j: (jnp.maximum(i * (tm // halo) - 1, 0), cb0 + j)),
                  pl.BlockSpec((kp, cw), lambda i, j: (0, j)),
                  pl.BlockSpec((1, cw), lambda i, j: (0, j))],
        out_specs=pl.BlockSpec((tm, cw), lambda i, j: (i, j)),
        scratch_shapes=[pltpu.VMEM((halo + tm, cw), F32)] + _conv_shift_scratch(k, halo + tm, cw),
        compiler_params=_params(("parallel", "parallel")),
    )(src, src, w, bias)


def _conv_bwd(dy, src, col0, width, w, k, seq, name, into=None):
    t = src.shape[0]
    tm, cw, halo = CONV_TILE, CONV_COLS, _conv_halo(k)
    sr, sc = CONV_SUB_ROWS, CONV_SUB_COLS
    p = k - 1
    cb0 = col0 // cw
    kp = w.shape[0]
    nt = t // tm
    last_halo = t // halo - 1

    shifted = _conv_use_shifted(k)

    def body(dy_ref, dn_ref, x_ref, xp_ref, w_ref, *rest):
        if into is not None:
            rest = rest[1:]
        dx_ref, dw_ref, db_ref, dyext, xext, wacc, bacc = rest[:7]
        sh = rest[7:]
        i = pl.program_id(1)
        dysh, xsh = (sh[:1], sh[1:]) if shifted else ((), ())

        @pl.when(i == 0)
        def _():
            wacc[...] = jnp.zeros_like(wacc)
            bacc[...] = jnp.zeros_like(bacc)

        seq_start = (i * tm) % seq == 0
        seq_end = ((i + 1) * tm) % seq == 0
        dyext[:tm, :] = dy_ref[...]
        dyext[tm:, :] = jnp.where(seq_end, 0.0, dn_ref[...])
        xext[halo:, :] = x_ref[...]
        xext[:halo, :] = jnp.where(seq_start, 0.0, xp_ref[...])
        if shifted:
            _conv_fill_shifted(dyext, dysh[0])
            _conv_fill_shifted(xext, xsh[0])
        for r0, c0 in _conv_subtiles(tm, cw):
            cs = slice(c0, c0 + sc)
            dyv = dy_ref[r0:r0 + sr, cs]
            acc = jnp.zeros((sr, sc), F32)
            for j in range(k):
                acc = acc + w_ref[j:j + 1, cs] * _conv_rows(dyext, dysh, r0 + p - j, sr, cs)
                wacc[j, :, cs] += _rowsum8(dyv * _conv_rows(xext, xsh, r0 + halo - p + j, sr, cs))
            dx_ref[r0:r0 + sr, cs] = acc.astype(dx_ref.dtype)
            bacc[:, cs] += _rowsum8(dyv)

        @pl.when(i == nt - 1)
        def _():
            dw_ref[...] = jnp.zeros_like(dw_ref)
            for j in range(k):
                dw_ref[j:j + 1, :] = jnp.sum(wacc[j], axis=0, keepdims=True)
            db_ref[...] = jnp.sum(bacc[...], axis=0, keepdims=True)

    if into is None:
        dx_shape = jax.ShapeDtypeStruct((t, width), F32)
        dx_spec = pl.BlockSpec((tm, cw), lambda j, i: (i, j))
        extra_specs, extra_args, aliases = [], [], {}
    else:
        dx_shape = jax.ShapeDtypeStruct(into.shape, into.dtype)
        dx_spec = pl.BlockSpec((tm, cw), lambda j, i: (i, cb0 + j))
        extra_specs, extra_args, aliases = [ANY], [into], {5: 0}
    return pl.pallas_call(
        body, name=name,
        out_shape=(dx_shape, jax.ShapeDtypeStruct((kp, width), F32), jax.ShapeDtypeStruct((1, width), F32)),
        grid=(width // cw, nt),
        in_specs=[pl.BlockSpec((tm, cw), lambda j, i: (i, j)),
                  pl.BlockSpec((halo, cw), lambda j, i: (jnp.minimum((i + 1) * (tm // halo), last_halo), j)),
                  pl.BlockSpec((tm, cw), lambda j, i: (i, cb0 + j)),
                  pl.BlockSpec((halo, cw), lambda j, i: (jnp.maximum(i * (tm // halo) - 1, 0), cb0 + j)),
                  pl.BlockSpec((kp, cw), lambda j, i: (0, j))] + extra_specs,
        out_specs=(dx_spec,
                   pl.BlockSpec((kp, cw), lambda j, i: (0, j)),
                   pl.BlockSpec((1, cw), lambda j, i: (0, j))),
        input_output_aliases=aliases,
        scratch_shapes=[pltpu.VMEM((tm + halo, cw), F32), pltpu.VMEM((halo + tm, cw), F32),
                        pltpu.VMEM((kp, SUBLANES, cw), F32), pltpu.VMEM((SUBLANES, cw), F32)]
        + 2 * _conv_shift_scratch(k, halo + tm, cw),
        compiler_params=_params(("parallel", "arbitrary")),
    )(dy, dy, src, src, w, *extra_args)


def _half_mask(half):
    lane = _iota((1, LANES), 1)
    return ((lane >= half * ATTN_HEAD_DIM) & (lane < (half + 1) * ATTN_HEAD_DIM)).astype(F32)


def _stack_heads(xp, g):
    m = _half_mask(g)
    swapped = pltpu.roll(xp, ATTN_HEAD_DIM, axis=1)
    return jnp.concatenate([xp * m, swapped * m] if g == 0 else [swapped * m, xp * m], axis=0)


def _unstack_heads(both, g):
    w = both.shape[0] // 2
    top, bot = both[:w], both[w:]
    lo, hi = _half_mask(0), _half_mask(1)
    if g == 0:
        return top * lo + pltpu.roll(bot, ATTN_HEAD_DIM, axis=1) * hi
    return pltpu.roll(top, ATTN_HEAD_DIM, axis=1) * lo + bot * hi


def _band_mask(first_block):
    w = WINDOW
    qi = _iota((w, 2 * w), 0)
    kj = _iota((w, 2 * w), 1) - w
    rel = qi - kj
    return (rel >= 0) & (rel < w) & (jnp.logical_not(first_block) | (kj >= 0))


def _lane_pick(x, h):
    return jnp.sum(jnp.where(_iota(x.shape, 1) == h, x, 0.0), axis=1, keepdims=True)


def _attn_specs(nb, rev):
    w = WINDOW

    def blk(i):
        return nb - 1 - i if rev else i

    def row(b, i):
        return b * nb + blk(i)

    def prow(b, i):
        return b * nb + jnp.maximum(blk(i) - 1, 0)

    q = pl.BlockSpec((w, 512), lambda b, i: (row(b, i), OFF_Q // 512))
    kc = pl.BlockSpec((w, 128), lambda b, i: (row(b, i), OFF_K // 128))
    kp = pl.BlockSpec((w, 128), lambda b, i: (prow(b, i), OFF_K // 128))
    vc = pl.BlockSpec((w, 128), lambda b, i: (row(b, i), OFF_V // 128))
    vp = pl.BlockSpec((w, 128), lambda b, i: (prow(b, i), OFF_V // 128))
    z = pl.BlockSpec((w, 512), lambda b, i: (row(b, i), OFF_ZA // 512))
    return q, kc, kp, vc, vp, z, row


def _attn_fwd(proj, sinks, ycat, nbatch, name):
    t = proj.shape[0]
    w = WINDOW
    nb = t // nbatch // w
    scale = ATTN_HEAD_DIM ** -0.5
    q_s, kc_s, kp_s, vc_s, vp_s, z_s, row = _attn_specs(nb, False)

    def body(q_ref, kc_ref, kp_ref, vc_ref, vp_ref, z_ref, sk_ref, _, y_ref, o_ref, lse_ref):
        first = pl.program_id(1) == 0
        mask = _band_mask(first)
        kk = jnp.concatenate([kp_ref[...], kc_ref[...]], axis=0).astype(MXU_DTYPE)
        vv = jnp.concatenate([vp_ref[...], vc_ref[...]], axis=0).astype(MXU_DTYPE)
        sk = sk_ref[...]
        lane = _iota((w, LANES), 1)
        mask2 = jnp.concatenate([mask, mask], axis=0)
        scores = [_dot(_stack_heads(q_ref[:, j * LANES:(j + 1) * LANES], j // 2), kk, NT) for j in range(4)]
        lse_all = jnp.zeros((w, LANES), F32)
        for j in range(4):
            s = jnp.where(mask2, scores[j] * scale, -1e30)
            skc = jnp.concatenate([jnp.broadcast_to(_lane_pick(sk, 2 * j), (w, 1)),
                                   jnp.broadcast_to(_lane_pick(sk, 2 * j + 1), (w, 1))], axis=0)
            m = jnp.maximum(jnp.max(s, axis=1, keepdims=True), skc)
            den = jnp.sum(jnp.exp(s - m), axis=1, keepdims=True) + jnp.exp(skc - m)
            lse = m + jnp.log(den)
            lse_all = jnp.where(lane == 2 * j, lse[:w], lse_all)
            lse_all = jnp.where(lane == 2 * j + 1, lse[w:], lse_all)
            op = _unstack_heads(_dot(jnp.exp(s - lse), vv), j // 2)
            cols = slice(j * LANES, (j + 1) * LANES)
            o_ref[:, cols] = op
            y_ref[:, cols] = (op * _silu(z_ref[:, cols])).astype(y_ref.dtype)
        lse_ref[...] = lse_all

    return pl.pallas_call(
        body, name=name,
        out_shape=(jax.ShapeDtypeStruct(ycat.shape, ycat.dtype), jax.ShapeDtypeStruct((t, 512), F32),
                   jax.ShapeDtypeStruct((t, LANES), F32)),
        grid=(nbatch, nb),
        in_specs=[q_s, kc_s, kp_s, vc_s, vp_s, z_s, pl.BlockSpec((1, LANES), lambda b, i: (0, 0)), ANY],
        out_specs=(pl.BlockSpec((w, 512), lambda b, i: (row(b, i), YCAT_ATTN // 512)),
                   pl.BlockSpec((w, 512), lambda b, i: (row(b, i), 0)),
                   pl.BlockSpec((w, LANES), lambda b, i: (row(b, i), 0))),
        input_output_aliases={7: 0},
        compiler_params=_params(("parallel", "parallel")),
    )(proj, proj, proj, proj, proj, proj, sinks, ycat)


def _attn_bwd(dycat, proj, o, lse, sinks, ddt, dproj, nbatch, name):
    t = proj.shape[0]
    w = WINDOW
    nb = t // nbatch // w
    scale = ATTN_HEAD_DIM ** -0.5
    q_s, kc_s, kp_s, vc_s, vp_s, z_s, row = _attn_specs(nb, True)

    def body(dy_ref, q_ref, kc_ref, kp_ref, vc_ref, vp_ref, z_ref, o_ref, lse_ref, sk_ref, ddt_ref, _,
             grp_ref, dsk_ref, kcarry, vcarry, sacc):
        b, i = pl.program_id(0), pl.program_id(1)

        @pl.when((b == 0) & (i == 0))
        def _():
            sacc[...] = jnp.zeros_like(sacc)

        @pl.when(i == 0)
        def _():
            kcarry[...] = jnp.zeros_like(kcarry)
            vcarry[...] = jnp.zeros_like(vcarry)

        first = i == nb - 1
        mask = _band_mask(first)
        kk = jnp.concatenate([kp_ref[...], kc_ref[...]], axis=0).astype(MXU_DTYPE)
        vv = jnp.concatenate([vp_ref[...], vc_ref[...]], axis=0).astype(MXU_DTYPE)
        sk = sk_ref[...]
        lse_all = lse_ref[...]
        lane1 = _iota((1, LANES), 1)
        mask2 = jnp.concatenate([mask, mask], axis=0)
        qs, dos, deltas, lses, scores, dps = [], [], [], [], [], []
        for j in range(4):
            cols = slice(j * LANES, (j + 1) * LANES)
            qp, zp, ov, dy = q_ref[:, cols], z_ref[:, cols], o_ref[:, cols], dy_ref[:, cols]
            grp_ref[:, OFF_ZA + j * LANES:OFF_ZA + (j + 1) * LANES] = (dy * ov * _dsilu(zp)).astype(grp_ref.dtype)
            do = dy * _silu(zp)
            q2 = _stack_heads(qp, j // 2).astype(MXU_DTYPE)
            do2 = _stack_heads(do, j // 2)
            qs.append(q2)
            dos.append(do2.astype(MXU_DTYPE))
            deltas.append(jnp.sum(do2 * _stack_heads(ov, j // 2), axis=1, keepdims=True))
            lses.append(jnp.concatenate([_lane_pick(lse_all, 2 * j), _lane_pick(lse_all, 2 * j + 1)], axis=0))
            scores.append(_dot(q2, kk, NT))
            dps.append(_dot(do2, vv, NT))
        prs, dss = [], []
        dsk = jnp.zeros((1, LANES), F32)
        for j in range(4):
            pr = jnp.exp(jnp.where(mask2, scores[j] * scale, -1e30) - lses[j])
            prs.append(pr.astype(MXU_DTYPE))
            dss.append((pr * (dps[j] - deltas[j])).astype(MXU_DTYPE))
            skc = jnp.concatenate([jnp.broadcast_to(_lane_pick(sk, 2 * j), (w, 1)),
                                   jnp.broadcast_to(_lane_pick(sk, 2 * j + 1), (w, 1))], axis=0)
            sink_term = jnp.exp(skc - lses[j]) * deltas[j]
            dsk = dsk - jnp.where(lane1 == 2 * j, jnp.sum(sink_term[:w]), 0.0)
            dsk = dsk - jnp.where(lane1 == 2 * j + 1, jnp.sum(sink_term[w:]), 0.0)
        dkk = jnp.zeros((2 * w, LANES), F32)
        dvv = jnp.zeros((2 * w, LANES), F32)
        for j in range(4):
            dq = _unstack_heads(_dot(dss[j], kk) * scale, j // 2)
            grp_ref[:, OFF_Q + j * LANES:OFF_Q + (j + 1) * LANES] = dq.astype(grp_ref.dtype)
            dkk = dkk + _dot(dss[j], qs[j], TN) * scale
            dvv = dvv + _dot(prs[j], dos[j], TN)
        grp_ref[:, OFF_K:OFF_K + LANES] = (dkk[w:, :] + kcarry[...]).astype(grp_ref.dtype)
        grp_ref[:, OFF_V:OFF_V + LANES] = (dvv[w:, :] + vcarry[...]).astype(grp_ref.dtype)
        grp_ref[:, OFF_DT:OFF_DT + LANES] = ddt_ref[...].astype(grp_ref.dtype)
        grp_ref[:, OFF_DT + LANES:] = jnp.zeros((w, ATTN_GROUP - OFF_DT - LANES), grp_ref.dtype)
        kcarry[...] = dkk[:w, :]
        vcarry[...] = dvv[:w, :]
        sacc[...] += dsk

        @pl.when((b == nbatch - 1) & (i == nb - 1))
        def _():
            dsk_ref[...] = sacc[...]

    return pl.pallas_call(
        body, name=name,
        out_shape=(jax.ShapeDtypeStruct(dproj.shape, dproj.dtype), jax.ShapeDtypeStruct((1, LANES), F32)),
        grid=(nbatch, nb),
        in_specs=[pl.BlockSpec((w, 512), lambda b, i: (row(b, i), YCAT_ATTN // 512)),
                  q_s, kc_s, kp_s, vc_s, vp_s, z_s,
                  pl.BlockSpec((w, 512), lambda b, i: (row(b, i), 0)),
                  pl.BlockSpec((w, LANES), lambda b, i: (row(b, i), 0)),
                  pl.BlockSpec((1, LANES), lambda b, i: (0, 0)),
                  pl.BlockSpec((w, LANES), lambda b, i: (row(b, i), 0)), ANY],
        out_specs=(pl.BlockSpec((w, ATTN_GROUP), lambda b, i: (row(b, i), 0)),
                   pl.BlockSpec((1, LANES), lambda b, i: (0, 0))),
        input_output_aliases={11: 0},
        scratch_shapes=[pltpu.VMEM((w, LANES), F32), pltpu.VMEM((w, LANES), F32),
                        pltpu.VMEM((1, LANES), F32)],
        compiler_params=_params(("arbitrary", "arbitrary")),
    )(dycat, proj, proj, proj, proj, proj, proj, o, lse, sinks, ddt, dproj)


SSD_WIDTH = SSD_HEADS * SSD_HEAD_DIM
GROUP_ROWS = SSD_WIDTH // 2


def _expand_mat():
    r, c = _iota((LANES, SSD_WIDTH), 0), _iota((LANES, SSD_WIDTH), 1)
    return (r == lax.shift_right_logical(c, 6)).astype(BF16)


def _expand_mat_t():
    r, c = _iota((SSD_WIDTH, LANES), 0), _iota((SSD_WIDTH, LANES), 1)
    return (c == lax.shift_right_logical(r, 6)).astype(BF16)


def _ssd_common(u_ref, dt_ref, dtb_ref, a_ref):
    q = CHUNK
    act = _silu(u_ref[...])
    xs = act[:, :SSD_WIDTH]
    bm = act[:, SSD_WIDTH:SSD_WIDTH + 256]
    cm = act[:, SSD_WIDTH + 256:]
    dtp = _softplus(dt_ref[...] + dtb_ref[...])
    a = dtp * a_ref[...]
    tril = (_iota((q, q), 0) >= _iota((q, q), 1)).astype(BF16)
    acs = _xdot_r(tril, a)
    acs_t = acs.T
    e = _expand_mat()
    dt_x = _xdot(dtp, e)
    ea = jnp.exp(_xdot(acs, e))
    a_end = jnp.sum(jnp.where(_iota(acs.shape, 0) == q - 1, acs, 0.0), axis=0, keepdims=True)
    dec = jnp.exp(_xdot(a_end - acs, e))
    a_end_col = jnp.broadcast_to(_lane_pick(acs_t, q - 1), (LANES, LANES))
    s_scale = jnp.exp(_xdot_r(_expand_mat_t(), a_end_col))
    return act, xs, bm, cm, dtp, acs, acs_t, dt_x, ea, dec, s_scale, tril


def _decay_mat(acs, acs_t, h):
    q = CHUNK
    col = _lane_pick(acs, h)
    rowv = jnp.sum(jnp.where(_iota(acs_t.shape, 0) == h, acs_t, 0.0), axis=0, keepdims=True)
    causal = _iota((q, q), 0) >= _iota((q, q), 1)
    return jnp.exp(jnp.where(causal, col - rowv, -1e30))


GN_WIDTH = 512


def _ssd_fwd(u, proj, dtb, a_neg, d_x, norm_w, ycat, nbatch, name):
    t = u.shape[0]
    q = CHUNK
    nc = t // nbatch // q

    def body(u_ref, dt_ref, z_ref, dtb_ref, a_ref, dx_ref, nw_ref, _, y_ref, st_ref, yn_ref, state):
        c = pl.program_id(1)

        @pl.when(c == 0)
        def _():
            state[...] = jnp.zeros_like(state)

        st_ref[...] = state[...]
        act, xs, bm, cm, dtp, acs, acs_t, dt_x, ea, dec, s_scale, _ = _ssd_common(u_ref, dt_ref, dtb_ref, a_ref)
        xdt = xs * dt_x
        xdec = xdt * dec
        lo, hi = _half_mask(0), _half_mask(1)
        for g in range(2):
            bg = bm[:, g * LANES:(g + 1) * LANES]
            cg = cm[:, g * LANES:(g + 1) * LANES]
            rows = slice(g * GROUP_ROWS, (g + 1) * GROUP_ROWS)
            sg = state[rows, :]
            cb = _dot(cg, bg, NT)
            yoff = _dot(cg, sg, NT)
            for j in range(4):
                pj = g * 4 + j
                cols = slice(pj * LANES, (pj + 1) * LANES)
                xp = xdt[:, cols]
                m0 = cb * _decay_mat(acs, acs_t, 2 * pj)
                m1 = cb * _decay_mat(acs, acs_t, 2 * pj + 1)
                yp = _dot(m0, xp * lo) + _dot(m1, xp * hi)
                yp = yp + yoff[:, j * LANES:(j + 1) * LANES] * ea[:, cols]
                y_ref[:, cols] = yp + dx_ref[:, cols] * xs[:, cols]
            state[rows, :] = s_scale[rows, :] * sg + _dot(xdec[:, rows], bg, TN)
        for g in range(SSD_WIDTH // GN_WIDTH):
            cols = slice(g * GN_WIDTH, (g + 1) * GN_WIDTH)
            gg = y_ref[:, cols] * _silu(z_ref[:, cols])
            rstd = lax.rsqrt(jnp.mean(gg * gg, axis=-1, keepdims=True) + EPS)
            yn_ref[:, cols] = (gg * rstd * nw_ref[:, cols]).astype(yn_ref.dtype)

    vec = pl.BlockSpec((1, LANES), lambda b, c: (0, 0))
    wide = pl.BlockSpec((q, SSD_WIDTH), lambda b, c: (b * nc + c, 0))
    wvec = pl.BlockSpec((1, SSD_WIDTH), lambda b, c: (0, 0))
    return pl.pallas_call(
        body, name=name,
        out_shape=(jax.ShapeDtypeStruct((t, SSD_WIDTH), F32),
                   jax.ShapeDtypeStruct((nbatch * nc * SSD_WIDTH, SSD_STATE), F32),
                   jax.ShapeDtypeStruct(ycat.shape, ycat.dtype)),
        grid=(nbatch, nc),
        in_specs=[pl.BlockSpec((q, SSD_CONV_DIM), lambda b, c: (b * nc + c, 0)),
                  pl.BlockSpec((q, LANES), lambda b, c: (b * nc + c, OFF_DT // LANES)),
                  pl.BlockSpec((q, SSD_WIDTH), lambda b, c: (b * nc + c, OFF_ZS // SSD_WIDTH)),
                  vec, vec, wvec, wvec, ANY],
        out_specs=(wide, pl.BlockSpec((SSD_WIDTH, SSD_STATE), lambda b, c: (b * nc + c, 0)), wide),
        input_output_aliases={7: 2},
        scratch_shapes=[pltpu.VMEM((SSD_WIDTH, SSD_STATE), F32)],
        compiler_params=_params(("parallel", "arbitrary")),
    )(u, proj, proj, dtb, a_neg, d_x, norm_w, ycat)


def _ssd_bwd(dycat, u, proj, y, states, dtb, a_neg, d_x, norm_w, dproj, nbatch, name):
    t = u.shape[0]
    q = CHUNK
    nc = t // nbatch // q

    def body(do_ref, u_ref, dt_ref, z_ref, y_ref, st_ref, dtb_ref, a_ref, dx_ref, nw_ref, _,
             du_ref, dz_ref, ddt_ref, dal_ref, dd_ref, dtbg_ref, dnw_ref, dstate, acc_a, acc_d, acc_b, acc_w):
        b, c = pl.program_id(0), pl.program_id(1)

        @pl.when((b == 0) & (c == 0))
        def _():
            acc_a[...] = jnp.zeros_like(acc_a)
            acc_d[...] = jnp.zeros_like(acc_d)
            acc_b[...] = jnp.zeros_like(acc_b)
            acc_w[...] = jnp.zeros_like(acc_w)

        @pl.when(c == 0)
        def _():
            dstate[...] = jnp.zeros_like(dstate)

        dy_parts = []
        for g in range(SSD_WIDTH // GN_WIDTH):
            cols = slice(g * GN_WIDTH, (g + 1) * GN_WIDTH)
            yv, zv, dov = y_ref[:, cols], z_ref[:, cols], do_ref[:, cols]
            sz = _silu(zv)
            gg = yv * sz
            rstd = lax.rsqrt(jnp.mean(gg * gg, axis=-1, keepdims=True) + EPS)
            gh = gg * rstd
            acc_w[:, cols] += _rowsum8(dov * gh)
            dgn = dov * nw_ref[:, cols]
            dg = rstd * (dgn - gh * jnp.mean(dgn * gh, axis=-1, keepdims=True))
            dy_parts.append(dg * sz)
            dz_ref[:, cols] = (dg * yv * _dsilu(zv)).astype(dz_ref.dtype)

        act, xs, bm, cm, dtp, acs, acs_t, dt_x, ea, dec, s_scale, tril = _ssd_common(
            u_ref, dt_ref, dtb_ref, a_ref)
        xdt = xs * dt_x
        xdec = xdt * dec
        dyv = jnp.concatenate(dy_parts, axis=1)
        dye = dyv * ea
        lo, hi = _half_mask(0), _half_mask(1)
        et = _expand_mat_t()
        dxdt_parts, db_parts, dc_parts, dxst_parts, yoff_parts = [], [], [], [], []
        end_sum = jnp.zeros((LANES, LANES), F32)
        dal_diag = jnp.zeros((q, LANES), F32)
        lane_q = _iota((q, LANES), 1)
        for g in range(2):
            bg = bm[:, g * LANES:(g + 1) * LANES]
            cg = cm[:, g * LANES:(g + 1) * LANES]
            rows = slice(g * GROUP_ROWS, (g + 1) * GROUP_ROWS)
            sg = st_ref[rows, :]
            dsg = dstate[rows, :]
            cb = _dot(cg, bg, NT)
            yoff_parts.append(_dot(cg, sg, NT))
            dcb = jnp.zeros((q, q), F32)
            parts = []
            for j in range(4):
                pj = g * 4 + j
                cols = slice(pj * LANES, (pj + 1) * LANES)
                xp = xdt[:, cols]
                dy0, dy1 = dyv[:, cols] * lo, dyv[:, cols] * hi
                l0 = _decay_mat(acs, acs_t, 2 * pj)
                l1 = _decay_mat(acs, acs_t, 2 * pj + 1)
                g0, g1 = _dot(dy0, xp, NT), _dot(dy1, xp, NT)
                m0, m1 = cb * l0, cb * l1
                dcb = dcb + g0 * l0 + g1 * l1
                parts.append(_dot(m0, dy0, TN) + _dot(m1, dy1, TN))
                for hh, wmat in enumerate((g0 * m0, g1 * m1)):
                    sel = (lane_q == 2 * pj + hh).astype(F32)
                    dal_diag = dal_diag + _dot(wmat, sel) - _dot(wmat, sel, TN)
            dxst = _dot(bg, dsg, NT) * dec[:, rows]
            dxst_parts.append(dxst)
            dxdt_parts.append(jnp.concatenate(parts, axis=1) + dxst)
            dc_parts.append(_dot(dcb, bg) + _dot(dye[:, rows], sg))
            db_parts.append(_dot(dcb, cg, TN) + _dot(xdec[:, rows], dsg))
            s_next = s_scale[rows, :] * sg + _dot(xdec[:, rows], bg, TN)
            end_sum = end_sum + _xdot(dsg * s_next, et[rows, :], TN, passes=2)
            dstate[rows, :] = _dot(dye[:, rows], cg, TN) + s_scale[rows, :] * dsg
        dxdt = jnp.concatenate(dxdt_parts, axis=1)
        dxv = dx_ref[...]
        yoff = jnp.concatenate(yoff_parts, axis=1) * ea
        dalpha = dal_diag + _xdot(dyv * yoff - xdt * jnp.concatenate(dxst_parts, axis=1), et)
        end_row = jnp.sum(end_sum, axis=0, keepdims=True)
        dalpha = dalpha + jnp.where(_iota((q, LANES), 0) == q - 1, end_row, 0.0)
        da = _xdot_r(tril, dalpha, TN)
        ddtp = da * a_ref[...] + _xdot(dxdt * xs, et)
        acc_a[...] += _rowsum8(da * dtp)
        acc_d[...] += _rowsum8(_xdot(dyv * xs, et))
        ddt_raw = ddtp * _sigmoid(dt_ref[...] + dtb_ref[...])
        acc_b[...] += _rowsum8(ddt_raw)
        ddt_ref[...] = ddt_raw
        dxs = dxdt * dt_x + dxv * dyv
        dact = jnp.concatenate([dxs] + db_parts + dc_parts, axis=1)
        du_ref[...] = dact * _dsilu(u_ref[...])

        @pl.when((b == nbatch - 1) & (c == nc - 1))
        def _():
            dal_ref[...] = jnp.sum(acc_a[...], axis=0, keepdims=True) * a_ref[...]
            dd_ref[...] = jnp.sum(acc_d[...], axis=0, keepdims=True)
            dtbg_ref[...] = jnp.sum(acc_b[...], axis=0, keepdims=True)
            dnw_ref[...] = jnp.sum(acc_w[...], axis=0, keepdims=True)

    def rowblk(b, c):
        return b * nc + (nc - 1 - c)

    vec = pl.BlockSpec((1, LANES), lambda b, c: (0, 0))
    wvec = pl.BlockSpec((1, SSD_WIDTH), lambda b, c: (0, 0))
    wide = pl.BlockSpec((q, SSD_WIDTH), lambda b, c: (rowblk(b, c), 0))
    zblk = pl.BlockSpec((q, SSD_WIDTH), lambda b, c: (rowblk(b, c), OFF_ZS // SSD_WIDTH))
    return pl.pallas_call(
        body, name=name,
        out_shape=(jax.ShapeDtypeStruct((t, SSD_CONV_DIM), F32), jax.ShapeDtypeStruct(dproj.shape, dproj.dtype),
                   jax.ShapeDtypeStruct((t, LANES), F32),
                   jax.ShapeDtypeStruct((1, LANES), F32), jax.ShapeDtypeStruct((1, LANES), F32),
                   jax.ShapeDtypeStruct((1, LANES), F32), jax.ShapeDtypeStruct((1, SSD_WIDTH), F32)),
        grid=(nbatch, nc),
        in_specs=[wide,
                  pl.BlockSpec((q, SSD_CONV_DIM), lambda b, c: (rowblk(b, c), 0)),
                  pl.BlockSpec((q, LANES), lambda b, c: (rowblk(b, c), OFF_DT // LANES)),
                  zblk, wide,
                  pl.BlockSpec((SSD_WIDTH, SSD_STATE), lambda b, c: (rowblk(b, c), 0)),
                  vec, vec, wvec, wvec, ANY],
        out_specs=(pl.BlockSpec((q, SSD_CONV_DIM), lambda b, c: (rowblk(b, c), 0)),
                   zblk,
                   pl.BlockSpec((q, LANES), lambda b, c: (rowblk(b, c), 0)),
                   vec, vec, vec, wvec),
        input_output_aliases={10: 1},
        scratch_shapes=[pltpu.VMEM((SSD_WIDTH, SSD_STATE), F32), pltpu.VMEM((SUBLANES, LANES), F32),
                        pltpu.VMEM((SUBLANES, LANES), F32), pltpu.VMEM((SUBLANES, LANES), F32),
                        pltpu.VMEM((SUBLANES, SSD_WIDTH), F32)],
        compiler_params=_params(("arbitrary", "arbitrary")),
    )(dycat, u, proj, proj, y, states, dtb, a_neg, d_x, norm_w, dproj)


def _pad_rows(w, rows):
    return jnp.concatenate([w, jnp.zeros((rows - w.shape[0], w.shape[1]), w.dtype)], axis=0)


def _pad_lanes(v):
    return jnp.concatenate([v, jnp.zeros((LANES - v.shape[0],), v.dtype)]).reshape(1, LANES)


def _padded_from_chips(pieces):
    cols = pieces[0].shape[-1]
    lead = pieces[0].shape[:-1]
    parts, pos = [], 0
    for lo, hi, start in sorted(SECTIONS, key=lambda s: s[2]):
        if start > pos:
            parts.append(jnp.zeros(lead + (start - pos,), pieces[0].dtype))
        pos = start + hi - lo
        while lo < hi:
            p = lo // cols
            end = min(hi, (p + 1) * cols)
            parts.append(pieces[p][..., lo - p * cols:end - p * cols])
            lo = end
    if pos < NP:
        parts.append(jnp.zeros(lead + (NP - pos,), pieces[0].dtype))
    return jnp.concatenate(parts, axis=-1)


def _chip_part_from_padded(wp, p, cols):
    lo, hi = p * cols, (p + 1) * cols
    parts = []
    for rs, re, start in SECTIONS:
        a, b = max(lo, rs), min(hi, re)
        if a < b:
            parts.append(wp[..., start + a - rs:start + b - rs])
    return jnp.concatenate(parts, axis=-1)


def _layer_params(li, w_in_p, w_out, conv_w, dw_w, small):
    return dict(
        w_in_p=w_in_p, w_out=w_out,
        conv_w=_pad_rows(conv_w, SUBLANES), dw_w=_pad_rows(dw_w, 32),
        norm_w=small["norm_w"][li].reshape(1, -1),
        conv_b=small["ssd_conv_b"][li].reshape(1, -1),
        dtb=_pad_lanes(small["ssd_dt_bias"][li]),
        a_neg=_pad_lanes(-jnp.exp(small["ssd_a_log"][li])),
        d_x=jnp.repeat(small["ssd_d"][li], SSD_HEAD_DIM).reshape(1, -1),
        ssd_norm_w=small["ssd_norm_w"][li].reshape(1, -1),
        sinks=_pad_lanes(small["attn_sinks"][li]),
        dw_b=small["conf_dw_b"][li].reshape(1, -1),
        ln_w=small["conf_ln_w"][li].reshape(1, -1),
        ln_b=small["conf_ln_b"][li].reshape(1, -1),
    )


def _layer_fwd(x, p, nbatch, seq, tag, after=None):
    h, h_t = _rmsnorm_fwd(x, p["norm_w"], name=f"rmsnorm_fwd_{tag}", after=after)
    proj = _matmul(h, p["w_in_p"], "nn", F32, 1024, 512, 1024, name=f"proj_fwd_{tag}")
    u = _conv_fwd(proj, OFF_XBC, SSD_CONV_DIM, p["conv_w"], p["conv_b"], SSD_CONV, seq, name=f"ssd_conv_fwd_{tag}")
    ycat = lax.empty((x.shape[0], MIX_WIDTH), MXU_DTYPE)
    y, states, ycat = _ssd_fwd(u, proj, p["dtb"], p["a_neg"], p["d_x"], p["ssd_norm_w"], ycat, nbatch,
                               name=f"ssd_fwd_{tag}")
    ycat, o, lse = _attn_fwd(proj, p["sinks"], ycat, nbatch, name=f"attn_fwd_{tag}")
    c0 = _glu_fwd(proj, name=f"glu_fwd_{tag}")
    c1 = _conv_fwd(c0, 0, CONF_WIDTH, p["dw_w"], p["dw_b"], CONF_KERNEL, seq, name=f"conf_conv_fwd_{tag}")
    ycat = _conf_post_fwd(c1, proj, p["ln_w"], p["ln_b"], ycat, name=f"conf_post_fwd_{tag}")
    x_new = _matmul(ycat, p["w_out"], "nn", F32, 1024, 512, 2048, name=f"out_fwd_{tag}", residual=x)
    return x_new, dict(x=x, h_t=h_t, proj=proj, u=u, y=y, states=states, o=o, lse=lse, c0=c0, c1=c1, ycat=ycat)


def _layer_bwd(dx_out, p, s, nbatch, seq, tag, hooks=None):
    hooks = hooks or {}
    proj = s["proj"]
    dycat = _matmul(dx_out, p["w_out"], "nt", F32, 1024, 1024, 1024, name=f"out_bwd_dy_{tag}",
                    after=hooks.get("start_token"))
    dw_out = _matmul(s["ycat"], dx_out, "tn", F32, 1024, 1024, 1024, name=f"out_bwd_dw_{tag}")
    token = hooks["after_dycat"](dycat) if "after_dycat" in hooks else None
    dtb = p["dtb"] if token is None else p["dtb"] + token[0, 0]
    dproj = lax.empty(proj.shape, MXU_DTYPE)
    du, dproj, ddt, da_log, dd, ddtb, dssd_norm_w = _ssd_bwd(
        dycat, s["u"], proj, s["y"], s["states"], dtb, p["a_neg"], p["d_x"], p["ssd_norm_w"], dproj,
        nbatch, name=f"ssd_bwd_{tag}")
    dproj, dconv_w, dconv_b = _conv_bwd(du, proj, OFF_XBC, SSD_CONV_DIM, p["conv_w"], SSD_CONV, seq,
                                        name=f"ssd_conv_bwd_{tag}", into=dproj)
    dproj, dsinks = _attn_bwd(dycat, proj, s["o"], s["lse"], p["sinks"], ddt, dproj, nbatch,
                              name=f"attn_bwd_{tag}")
    if "after_attn" in hooks:
        hooks["after_attn"](dproj)
    dc1, dproj, dln_w, dln_b = _conf_post_bwd(dycat, s["c1"], proj, p["ln_w"], p["ln_b"], dproj,
                                              name=f"conf_post_bwd_{tag}")
    dc0, ddw_w, ddw_b = _conv_bwd(dc1, s["c0"], 0, CONF_WIDTH, p["dw_w"], CONF_KERNEL, seq,
                                  name=f"conf_conv_bwd_{tag}")
    dproj = _glu_bwd(dc0, proj, dproj, name=f"glu_bwd_{tag}")
    dh = _matmul(dproj, p["w_in_p"], "nt", F32, 1024, 1024, 1408, name=f"proj_bwd_dh_{tag}")
    dw_in_p = _matmul(s["h_t"], dproj, "nn", F32, 1024, 512, 4096, name=f"proj_bwd_dw_{tag}")
    dx_in, dnorm_w = _rmsnorm_bwd(dh, s["x"], p["norm_w"], dx_out, name=f"rmsnorm_bwd_{tag}")
    grads = dict(
        norm_w=dnorm_w[0], w_in_p=dw_in_p, ssd_conv_w=dconv_w[:SSD_CONV], ssd_conv_b=dconv_b[0],
        ssd_dt_bias=ddtb[0, :SSD_HEADS], ssd_a_log=da_log[0, :SSD_HEADS], ssd_d=dd[0, :SSD_HEADS],
        ssd_norm_w=dssd_norm_w[0], attn_sinks=dsinks[0, :ATTN_Q_HEADS], conf_dw_w=ddw_w[:CONF_KERNEL],
        conf_dw_b=ddw_b[0], conf_ln_w=dln_w[0], conf_ln_b=dln_b[0], w_out=dw_out)
    return dx_in, grads


def _local_step(x, target, param_fns, final_norm_w, first_after=None, on_grads=None):
    nbatch, seq, d = x.shape
    xt = x.reshape(nbatch * seq, d)
    saved, layer_params = [], []
    for li, fn in enumerate(param_fns):
        p = fn(xt)
        layer_params.append(p)
        xt, s = _layer_fwd(xt, p, nbatch, seq, f"l{li}", after=first_after if li == 0 else None)
        saved.append(s)
    loss, dx, dfinal = _loss_head(xt, target.reshape(nbatch * seq, d), final_norm_w.reshape(1, d), name="loss_head")
    grads = [None] * len(layer_params)
    hooks = None
    for li in reversed(range(len(layer_params))):
        dx, grads[li] = _layer_bwd(dx, layer_params[li], saved[li], nbatch, seq, f"l{li}", hooks=hooks)
        hooks = on_grads(li, grads[li]) if on_grads is not None else None
    return loss[0, 0], dx.reshape(nbatch, seq, d), grads, dfinal[0]


MESH = pl.DeviceIdType.MESH
N_CHIPS = 4


def _mesh_pos():
    return lax.axis_index("x"), lax.axis_index("y"), lax.axis_index("c")


def _other_chips(x, y):
    return [(1 - x, y), (x, 1 - y), (1 - x, 1 - y)]


def _gather_weights(big, small, name):
    nbig, nsmall = len(big), len(small)
    n_ici = 3 * (nbig + nsmall)
    n_fwd = 3 * nbig

    def body(*refs):
        ins = refs[:nbig + nsmall]
        outs = refs[nbig + nsmall:2 * (nbig + nsmall)]
        send_sems, recv_sems = refs[2 * (nbig + nsmall):]
        x, y, c = _mesh_pos()
        me = 2 * x + y
        sibling = (x, y, 1 - c)
        chips = _other_chips(x, y)

        def ici(a, j, origin, dest):
            if a < nbig:
                src = ins[a].at[c] if origin is None else outs[a].at[origin, c]
                dst = outs[a].at[me if origin is None else origin, c]
            else:
                src = ins[a] if origin is None else outs[a].at[origin]
                dst = outs[a].at[me if origin is None else origin]
            k = a * 3 + j
            return pltpu.make_async_remote_copy(src_ref=src, dst_ref=dst, send_sem=send_sems.at[k],
                                                recv_sem=recv_sems.at[k], device_id=dest, device_id_type=MESH)

        def fwd(a, j, origin, half):
            k = n_ici + a * 3 + j
            ref = outs[a].at[origin, half]
            return pltpu.make_async_remote_copy(src_ref=ref, dst_ref=ref, send_sem=send_sems.at[k],
                                                recv_sem=recv_sems.at[k], device_id=sibling, device_id_type=MESH)

        sends = []
        for j, (px, py) in enumerate(chips):
            for a in range(nbig + nsmall):
                cp = ici(a, j, None, (px, py, c))
                cp.start()
                sends.append(cp)
        for j, (px, py) in enumerate(chips):
            origin = 2 * px + py
            for a in range(nbig):
                ici(a, j, origin, (px, py, c)).wait_recv()
                cp = fwd(a, j, origin, c)
                cp.start()
                sends.append(cp)
        for j, (px, py) in enumerate(chips):
            origin = 2 * px + py
            for a in range(nbig, nbig + nsmall):
                ici(a, j, origin, (px, py, c)).wait_recv()
            for a in range(nbig):
                fwd(a, j, origin, 1 - c).wait_recv()
        for cp in sends:
            cp.wait_send()

    out_shape = tuple(jax.ShapeDtypeStruct((N_CHIPS,) + a.shape, a.dtype) for a in list(big) + list(small))
    return pl.pallas_call(
        body, name=name, out_shape=out_shape,
        in_specs=[ANY] * (nbig + nsmall), out_specs=tuple([ANY] * (nbig + nsmall)),
        scratch_shapes=[pltpu.SemaphoreType.DMA((n_ici + n_fwd,)), pltpu.SemaphoreType.DMA((n_ici + n_fwd,))],
    )(*big, *small)


def _pair_swap_halves(arrs, name):
    n = len(arrs)

    def body(*refs):
        ins, outs = refs[:n], refs[n:2 * n]
        send_sems, recv_sems = refs[2 * n:]
        x, y, c = _mesh_pos()
        cps = [pltpu.make_async_remote_copy(src_ref=ins[a].at[:, 1 - c], dst_ref=outs[a], send_sem=send_sems.at[a],
                                            recv_sem=recv_sems.at[a], device_id=(x, y, 1 - c), device_id_type=MESH)
               for a in range(n)]
        for cp in cps:
            cp.start()
        for cp in cps:
            cp.wait()

    return pl.pallas_call(
        body, name=name,
        out_shape=tuple(jax.ShapeDtypeStruct(a.shape[:1] + a.shape[2:], a.dtype) for a in arrs),
        in_specs=[ANY] * n, out_specs=tuple([ANY] * n),
        scratch_shapes=[pltpu.SemaphoreType.DMA((n,)), pltpu.SemaphoreType.DMA((n,))],
    )(*arrs)


def _chip_scatter(arrs, name):
    n = len(arrs)

    def body(*refs):
        ins, outs = refs[:n], refs[n:2 * n]
        send_sems, recv_sems = refs[2 * n:]
        x, y, c = _mesh_pos()
        me = 2 * x + y
        cps = []
        for j, (px, py) in enumerate(_other_chips(x, y)):
            for a in range(n):
                cps.append(pltpu.make_async_remote_copy(
                    src_ref=ins[a].at[2 * px + py], dst_ref=outs[a].at[me], send_sem=send_sems.at[a * 3 + j],
                    recv_sem=recv_sems.at[a * 3 + j], device_id=(px, py, c), device_id_type=MESH))
        for cp in cps:
            cp.start()
        for cp in cps:
            cp.wait()

    return pl.pallas_call(
        body, name=name, out_shape=tuple(jax.ShapeDtypeStruct(a.shape, a.dtype) for a in arrs),
        in_specs=[ANY] * n, out_specs=tuple([ANY] * n),
        scratch_shapes=[pltpu.SemaphoreType.DMA((3 * n,)), pltpu.SemaphoreType.DMA((3 * n,))],
    )(*arrs)


HBM = pl.BlockSpec(memory_space=pltpu.HBM)
SEM = pl.BlockSpec(memory_space=pltpu.SEMAPHORE)
DATAFLOW = pltpu.SideEffectType.DATAFLOW_SIDE_EFFECTING


def _split_peers(pattern, x, y, c):
    if pattern == "swap":
        return [((x, y, 1 - c), 1 - c, None, None)]
    me = 2 * x + y
    return [((px, py, c), 2 * px + py if pattern == "scatter" else None, me, 2 * px + py)
            for px, py in _other_chips(x, y)]


def _split_land_shape(pattern, shape):
    return {"bcast": (N_CHIPS,) + shape, "scatter": shape, "swap": shape[:1] + shape[2:]}[pattern]


def _split_copies(pattern, srcs, lands, send_sems, recv_sems, waiting):
    x, y, c = _mesh_pos()
    peers = _split_peers(pattern, x, y, c)
    cps = []
    for j, (dev, src_slot, dst_slot, my_slot) in enumerate(peers):
        for a in range(len(srcs)):
            if src_slot is None:
                src = srcs[a]
            else:
                src = srcs[a].at[:, src_slot] if pattern == "swap" else srcs[a].at[src_slot]
            slot = my_slot if waiting else dst_slot
            dst = lands[a] if slot is None else lands[a].at[slot]
            k = a * len(peers) + j
            cps.append(pltpu.make_async_remote_copy(src_ref=src, dst_ref=dst, send_sem=send_sems[k],
                                                    recv_sem=recv_sems[k], device_id=dev, device_id_type=MESH))
    return cps


def _split_start(arrs, pattern, after, name):
    n = len(arrs)
    nsem = n * (1 if pattern == "swap" else N_CHIPS - 1)

    def body(*refs):
        srcs, lands = refs[:n], refs[n:2 * n]
        outs = refs[2 * n + 1:]
        for cp in _split_copies(pattern, srcs, lands, outs[:nsem], outs[nsem:2 * nsem], waiting=False):
            cp.start()
        outs[-1][...] = jnp.zeros_like(outs[-1])

    lands = [lax.empty(_split_land_shape(pattern, a.shape), a.dtype) for a in arrs]
    out_shape = ([pltpu.SemaphoreType.DMA(())] * (2 * nsem)
                 + [pltpu.HBM(a.shape, a.dtype) for a in arrs] + [pltpu.HBM(b.shape, b.dtype) for b in lands]
                 + [jax.ShapeDtypeStruct((SUBLANES, LANES), F32)])
    outs = pl.pallas_call(
        body, name=name, out_shape=tuple(out_shape),
        in_specs=[HBM] * (2 * n) + [ANY],
        out_specs=tuple([SEM] * (2 * nsem) + [HBM] * (2 * n) + [pl.BlockSpec(memory_space=pltpu.VMEM)]),
        input_output_aliases={a: 2 * nsem + a for a in range(2 * n)},
        compiler_params=pltpu.CompilerParams(has_side_effects=DATAFLOW),
    )(*[pltpu.with_memory_space_constraint(a, pltpu.HBM) for a in list(arrs) + lands], after)
    return outs[:-1], outs[-1]


def _split_wait(state, n, pattern, after, name):
    nsem = n * (1 if pattern == "swap" else N_CHIPS - 1)

    def body(*refs):
        srcs, lands = refs[:n], refs[n:2 * n]
        send_sems, recv_sems = refs[2 * n:2 * n + nsem], refs[2 * n + nsem:2 * n + 2 * nsem]
        for cp in _split_copies(pattern, srcs, lands, send_sems, recv_sems, waiting=True):
            cp.wait_send()
            cp.wait_recv()

    sems, thru = state[:2 * nsem], state[2 * nsem:]
    outs = pl.pallas_call(
        body, name=name, out_shape=tuple(pltpu.HBM(a.shape, a.dtype) for a in thru),
        in_specs=[HBM] * (2 * n) + [SEM] * (2 * nsem) + [ANY],
        out_specs=tuple([HBM] * (2 * n)),
        input_output_aliases={a: a for a in range(2 * n)},
        compiler_params=pltpu.CompilerParams(has_side_effects=DATAFLOW),
    )(*thru, *sems, after)
    return outs[n:]


def _pair_gather(arrs, layer, name):
    n = len(arrs)

    def body(*refs):
        outs = refs[n:2 * n]
        send_sems, recv_sems = refs[2 * n:]
        x, y, c = _mesh_pos()
        cps = [pltpu.make_async_remote_copy(src_ref=outs[a].at[layer, c], dst_ref=outs[a].at[layer, c],
                                            send_sem=send_sems.at[a], recv_sem=recv_sems.at[a],
                                            device_id=(x, y, 1 - c), device_id_type=MESH)
               for a in range(n)]
        for cp in cps:
            cp.start()
        for cp in cps:
            cp.wait()

    return pl.pallas_call(
        body, name=name, out_shape=tuple(jax.ShapeDtypeStruct(a.shape, a.dtype) for a in arrs),
        in_specs=[ANY] * n, out_specs=tuple([ANY] * n),
        input_output_aliases={a: a for a in range(n)},
        scratch_shapes=[pltpu.SemaphoreType.DMA((n,)), pltpu.SemaphoreType.DMA((n,))],
    )(*arrs)


N_DEV = 8


def _allreduce_small(pack, name):
    r = pack.shape[0]

    def body(p_ref, o_ref, land, send_sems, recv_sems):
        x, y, c = _mesh_pos()
        me = 4 * x + 2 * y + c
        cps = []
        for k in range(1, N_DEV):
            peer = (x ^ (k >> 2), y ^ ((k >> 1) & 1), c ^ (k & 1))
            cps.append(pltpu.make_async_remote_copy(src_ref=p_ref, dst_ref=land.at[me], send_sem=send_sems.at[k - 1],
                                                    recv_sem=recv_sems.at[k - 1], device_id=peer, device_id_type=MESH))
        for cp in cps:
            cp.start()
        land[me] = p_ref[...]
        for cp in cps:
            cp.wait()
        total = land[0]
        for d in range(1, N_DEV):
            total = total + land[d]
        o_ref[...] = total

    vm = pl.BlockSpec(memory_space=pltpu.VMEM)
    return pl.pallas_call(
        body, name=name, out_shape=jax.ShapeDtypeStruct(pack.shape, F32),
        in_specs=[vm], out_specs=vm,
        scratch_shapes=[pltpu.VMEM((N_DEV, r, LANES), F32), pltpu.SemaphoreType.DMA((N_DEV - 1,)),
                        pltpu.SemaphoreType.DMA((N_DEV - 1,))],
    )(pack)


BIG_ROWS = 128


def _cast_layer(w, layer, name):
    _, r, cdim = w.shape
    tr = BIG_ROWS

    def body(w_ref, o_ref):
        o_ref[...] = w_ref[...].astype(o_ref.dtype)

    return pl.pallas_call(
        body, name=name, out_shape=jax.ShapeDtypeStruct((r, cdim), MXU_DTYPE),
        grid=(r // tr,), in_specs=[pl.BlockSpec((None, tr, cdim), lambda i: (layer, i, 0))],
        out_specs=pl.BlockSpec((tr, cdim), lambda i: (i, 0)),
        compiler_params=_params(("parallel",)),
    )(w)


def _pair_sum(parts, sib, which, out_dtype, name):
    k, _, r, cdim = parts.shape
    tr = BIG_ROWS

    def body(sel_ref, p_ref, s_ref, o_ref):
        o_ref[...] = (p_ref[...] + s_ref[...]).astype(o_ref.dtype)

    grid_spec = pltpu.PrefetchScalarGridSpec(
        num_scalar_prefetch=1, grid=(k, r // tr),
        in_specs=[pl.BlockSpec((None, None, tr, cdim), lambda l, i, sel: (l, sel[0], i, 0)),
                  pl.BlockSpec((None, tr, cdim), lambda l, i, sel: (l, i, 0))],
        out_specs=pl.BlockSpec((None, tr, cdim), lambda l, i, sel: (l, i, 0)))
    return pl.pallas_call(
        body, name=name, out_shape=jax.ShapeDtypeStruct((k, r, cdim), out_dtype), grid_spec=grid_spec,
        compiler_params=_params(("parallel", "parallel")),
    )(which.reshape(1).astype(jnp.int32), parts, sib)


def _sum_lead(parts, into, layer, which, name):
    k, r, cdim = parts.shape
    tr = BIG_ROWS

    def body(sel_ref, p_ref, _, o_ref):
        total = p_ref[0].astype(F32)
        for a in range(1, k):
            total = total + p_ref[a].astype(F32)
        o_ref[...] = total

    grid_spec = pltpu.PrefetchScalarGridSpec(
        num_scalar_prefetch=1, grid=(r // tr,),
        in_specs=[pl.BlockSpec((k, tr, cdim), lambda i, sel: (0, i, 0)), ANY],
        out_specs=pl.BlockSpec((None, None, tr, cdim), lambda i, sel: (layer, sel[0], i, 0)))
    return pl.pallas_call(
        body, name=name, out_shape=jax.ShapeDtypeStruct(into.shape, F32), grid_spec=grid_spec,
        input_output_aliases={2: 0},
        compiler_params=_params(("parallel",)),
    )(which.reshape(1).astype(jnp.int32), parts, into)


def _adam_math(w, g, m, v):
    m2 = ADAM_B1 * m + (1.0 - ADAM_B1) * g
    v2 = ADAM_B2 * v + (1.0 - ADAM_B2) * (g * g)
    m_hat = m2 / (1.0 - ADAM_B1 ** ADAM_STEP)
    v_hat = v2 / (1.0 - ADAM_B2 ** ADAM_STEP)
    delta = -ADAM_LR * (m_hat / (jnp.sqrt(v_hat) + ADAM_EPS) + ADAM_WD * w)
    return delta, m2, v2


def _adam_big(w, g, m, v, name):
    nl, r, cdim = w.shape
    tr = BIG_ROWS

    def body(w_ref, g_ref, m_ref, v_ref, d_ref, mo_ref, vo_ref):
        delta, m2, v2 = _adam_math(w_ref[...], g_ref[...], m_ref[...], v_ref[...])
        d_ref[...] = delta
        mo_ref[...] = m2
        vo_ref[...] = v2

    blk = pl.BlockSpec((None, tr, cdim), lambda l, i: (l, i, 0))
    shp = jax.ShapeDtypeStruct(w.shape, F32)
    return pl.pallas_call(
        body, name=name, out_shape=(shp, shp, shp),
        grid=(nl, r // tr), in_specs=[blk] * 4, out_specs=(blk, blk, blk),
        compiler_params=_params(("parallel", "parallel")),
    )(w, g, m, v)


def _adam_cols_major(w, g, m, v, name):
    cdim, nl, r = w.shape
    tc = BIG_ROWS

    def body(w_ref, g_ref, m_ref, v_ref, d_ref, mo_ref, vo_ref):
        delta, m2, v2 = _adam_math(w_ref[...], g_ref[...], m_ref[...], v_ref[...])
        d_ref[...] = delta
        mo_ref[...] = m2
        vo_ref[...] = v2

    blk = pl.BlockSpec((tc, nl, r), lambda i: (i, 0, 0))
    shp = jax.ShapeDtypeStruct(w.shape, F32)
    return pl.pallas_call(
        body, name=name, out_shape=(shp, shp, shp),
        grid=(pl.cdiv(cdim, tc),), in_specs=[blk] * 4, out_specs=(blk, blk, blk),
        compiler_params=_params(("parallel",)),
    )(w, g, m, v)


def _adam_small(ws, gs, ms, vs, name):
    n = len(ws)

    def body(*refs):
        w_refs, g_refs, m_refs, v_refs = (refs[k * n:(k + 1) * n] for k in range(4))
        d_refs, mo_refs, vo_refs = (refs[(4 + k) * n:(5 + k) * n] for k in range(3))
        for a in range(n):
            delta, m2, v2 = _adam_math(w_refs[a][...], g_refs[a][...], m_refs[a][...], v_refs[a][...])
            d_refs[a][...] = delta
            mo_refs[a][...] = m2
            vo_refs[a][...] = v2

    shapes = tuple(jax.ShapeDtypeStruct(w.shape, F32) for w in ws)
    vm = pl.BlockSpec(memory_space=pltpu.VMEM)
    outs = pl.pallas_call(body, name=name, out_shape=shapes * 3, in_specs=[vm] * (4 * n),
                          out_specs=tuple([vm] * (3 * n)))(*ws, *gs, *ms, *vs)
    return outs[:n], outs[n:2 * n], outs[2 * n:]


PACK_TILE = SUBLANES * LANES


def _pack(arrays):
    rows = []
    for a in arrays:
        flat = a.reshape(-1)
        pad = (-flat.shape[0]) % PACK_TILE
        if pad:
            flat = jnp.concatenate([flat, jnp.zeros((pad,), flat.dtype)])
        rows.append(flat.reshape(-1, LANES))
    return jnp.concatenate(rows, axis=0)


def _unpack(pack, shapes):
    outs, row = [], 0
    for shp in shapes:
        n = int(np.prod(shp))
        nrows = -(-n // PACK_TILE) * SUBLANES
        outs.append(pack[row:row + nrows].reshape(-1)[:n].reshape(shp))
        row += nrows
    return outs


SMALL = ["norm_w", "ssd_conv_b", "ssd_dt_bias", "ssd_a_log", "ssd_d", "ssd_norm_w", "attn_sinks",
         "conf_dw_b", "conf_ln_w", "conf_ln_b"]
WEIGHTS = ["norm_w", "w_in", "ssd_conv_w", "ssd_conv_b", "ssd_dt_bias", "ssd_a_log", "ssd_d", "ssd_norm_w",
           "attn_sinks", "conf_dw_w", "conf_dw_b", "conf_ln_w", "conf_ln_b", "w_out", "final_norm_w"]


def kernel(x, norm_w, w_in, ssd_conv_w, ssd_conv_b, ssd_dt_bias, ssd_a_log, ssd_d, ssd_norm_w, attn_sinks, conf_dw_w, conf_dw_b, conf_ln_w, conf_ln_b, w_out, final_norm_w, loss_target, m_norm_w, m_w_in, m_ssd_conv_w, m_ssd_conv_b, m_ssd_dt_bias, m_ssd_a_log, m_ssd_d, m_ssd_norm_w, m_attn_sinks, m_conf_dw_w, m_conf_dw_b, m_conf_ln_w, m_conf_ln_b, m_w_out, m_final_norm_w, v_norm_w, v_w_in, v_ssd_conv_w, v_ssd_conv_b, v_ssd_dt_bias, v_ssd_a_log, v_ssd_d, v_ssd_norm_w, v_attn_sinks, v_conf_dw_w, v_conf_dw_b, v_conf_ln_w, v_conf_ln_b, v_w_out, v_final_norm_w):
    w = dict(norm_w=norm_w, w_in=w_in, ssd_conv_w=ssd_conv_w, ssd_conv_b=ssd_conv_b, ssd_dt_bias=ssd_dt_bias,
             ssd_a_log=ssd_a_log, ssd_d=ssd_d, ssd_norm_w=ssd_norm_w, attn_sinks=attn_sinks, conf_dw_w=conf_dw_w,
             conf_dw_b=conf_dw_b, conf_ln_w=conf_ln_w, conf_ln_b=conf_ln_b, w_out=w_out, final_norm_w=final_norm_w)
    m = dict(norm_w=m_norm_w, w_in=m_w_in, ssd_conv_w=m_ssd_conv_w, ssd_conv_b=m_ssd_conv_b,
             ssd_dt_bias=m_ssd_dt_bias, ssd_a_log=m_ssd_a_log, ssd_d=m_ssd_d, ssd_norm_w=m_ssd_norm_w,
             attn_sinks=m_attn_sinks, conf_dw_w=m_conf_dw_w, conf_dw_b=m_conf_dw_b, conf_ln_w=m_conf_ln_w,
             conf_ln_b=m_conf_ln_b, w_out=m_w_out, final_norm_w=m_final_norm_w)
    v = dict(norm_w=v_norm_w, w_in=v_w_in, ssd_conv_w=v_ssd_conv_w, ssd_conv_b=v_ssd_conv_b,
             ssd_dt_bias=v_ssd_dt_bias, ssd_a_log=v_ssd_a_log, ssd_d=v_ssd_d, ssd_norm_w=v_ssd_norm_w,
             attn_sinks=v_attn_sinks, conf_dw_w=v_conf_dw_w, conf_dw_b=v_conf_dw_b, conf_ln_w=v_conf_ln_w,
             conf_ln_b=v_conf_ln_b, w_out=v_w_out, final_norm_w=v_final_norm_w)
    depth = w_in.shape[0]
    me = 2 * lax.axis_index("x") + lax.axis_index("y")

    assert depth == 2
    w_in_b = [_cast_layer(w_in, li, name=f"cast_w_in_l{li}") for li in range(depth)]
    w_out_b = [_cast_layer(w_out, li, name=f"cast_w_out_l{li}") for li in range(depth)]
    own0 = [w_in_b[0].reshape((2, -1) + w_in_b[0].shape[1:]), w_out_b[0].reshape((2, -1) + w_out_b[0].shape[1:]),
            ssd_conv_w, conf_dw_w]
    gathered0 = _gather_weights(own0[:2], own0[2:], name="gather_weights_l0")
    g_in0, g_out0, g_conv, g_dw = [lax.dynamic_update_index_in_dim(g_all, mine, me, 0)
                                   for g_all, mine in zip(gathered0, own0)]
    own1 = [w_in_b[1], w_out_b[1]]
    pending1, token1 = _split_start(own1, "bcast", gathered0[0], name="gather_l1_start")

    def small_full(li):
        return (jnp.concatenate([g_conv[p, li] for p in range(N_CHIPS)], axis=1),
                jnp.concatenate([g_dw[p, li] for p in range(N_CHIPS)], axis=1))

    def params_l0(_):
        w_in_p = _padded_from_chips([g_in0[p].reshape(w_in_b[0].shape) for p in range(N_CHIPS)])
        w_out_full = g_out0.reshape(-1, g_out0.shape[-1])
        return _layer_params(0, w_in_p, w_out_full, *small_full(0), w)

    def params_l1(layer_input):
        landed = _split_wait(pending1, len(own1), "bcast", layer_input, name="gather_l1_wait")
        g_in1, g_out1 = [lax.dynamic_update_index_in_dim(g_all, mine, me, 0) for g_all, mine in zip(landed, own1)]
        w_in_p = _padded_from_chips([g_in1[p] for p in range(N_CHIPS)])
        return _layer_params(1, w_in_p, g_out1.reshape(-1, g_out1.shape[-1]), *small_full(1), w)

    c = lax.axis_index("c")
    cols = w_in.shape[2]
    rows_out = w_out.shape[1]

    def grad_parts(g):
        dw = g["w_in_p"]
        p_in = jnp.stack([_chip_part_from_padded(dw, p, cols) for p in range(N_CHIPS)])
        return [p_in.reshape(N_CHIPS, 2, dw.shape[0] // 2, cols),
                g["w_out"].reshape(N_CHIPS, 2, rows_out // 2, D_MODEL)]

    def pair_sums(parts, sib, tag):
        return [_pair_sum(p, sb, c, MXU_DTYPE, name=f"grad_pair_sum_{k}_{tag}")
                for k, (p, sb) in enumerate(zip(parts, sib))]

    split = {"reduced": [lax.empty((depth, 2, w_in.shape[1] // 2, cols), F32),
                         lax.empty((depth, 2, rows_out // 2, D_MODEL), F32)]}

    def chip_sums(landed, sent, li):
        filled = [lax.dynamic_update_index_in_dim(r, lax.dynamic_index_in_dim(sk, me, 0, keepdims=False), me, 0)
                  for r, sk in zip(landed, sent)]
        halves = [_sum_lead(r, into, li, c, name=f"grad_chip_sum_{k}_l{li}")
                  for k, (r, into) in enumerate(zip(filled, split["reduced"]))]
        split["reduced"] = list(_pair_gather(halves, li, name=f"grad_pair_gather_l{li}"))

    def on_grads(li, g):
        if li != depth - 1:
            return None
        parts = grad_parts(g)
        swap_state, swap_token = _split_start(parts, "swap", g["w_out"], name="grad_swap_l1_start")

        def after_dycat(dycat):
            sib = _split_wait(swap_state, len(parts), "swap", dycat, name="grad_swap_l1_wait")
            split["sent"] = pair_sums(parts, sib, "l1")
            split["scatter"], token = _split_start(split["sent"], "scatter", split["sent"][0],
                                                   name="grad_scatter_l1_start")
            return token

        def after_attn(dproj):
            landed = _split_wait(split["scatter"], len(parts), "scatter", dproj, name="grad_scatter_l1_wait")
            chip_sums(landed, split["sent"], depth - 1)

        return {"start_token": swap_token, "after_dycat": after_dycat, "after_attn": after_attn}

    loss, grad_x, grads, dfinal = _local_step(x, loss_target, [params_l0, params_l1], final_norm_w,
                                              first_after=token1, on_grads=on_grads)

    parts0 = grad_parts(grads[0])
    sent0 = pair_sums(parts0, _pair_swap_halves(parts0, name="grad_pair_swap_l0"), "l0")
    scatter0, token0 = _split_start(sent0, "scatter", sent0[0], name="grad_scatter_l0_start")

    small_list = [grads[li][n] for li in range(depth) for n in SMALL]
    small_list += [grads[li][n] for li in range(depth) for n in ("ssd_conv_w", "conf_dw_w")]
    small_list += [dfinal, loss.reshape(1)]
    small_shapes = [a.shape for a in small_list]
    reduced = _unpack(_allreduce_small(_pack(small_list) + token0[0, 0], name="allreduce_small"), small_shapes)
    ns = len(SMALL)
    g = {n: jnp.stack([reduced[li * ns + i] for li in range(depth)]) for i, n in enumerate(SMALL)}
    conv_w_cols, dw_w_cols = ssd_conv_w.shape[2], conf_dw_w.shape[2]
    g["ssd_conv_w"] = jnp.stack([lax.dynamic_slice_in_dim(reduced[depth * ns + 2 * li], me * conv_w_cols,
                                                          conv_w_cols, axis=1) for li in range(depth)])
    g["conf_dw_w"] = jnp.stack([lax.dynamic_slice_in_dim(reduced[depth * ns + 2 * li + 1], me * dw_w_cols,
                                                         dw_w_cols, axis=1) for li in range(depth)])
    g["final_norm_w"] = reduced[-2]
    loss_total = reduced[-1][0]

    small_names = [n for n in WEIGHTS if n not in ("w_in", "w_out")]

    def as2d(a):
        return a.reshape(1, -1) if a.ndim == 1 else a

    deltas, new_ms, new_vs = _adam_small(*[[as2d(src[n]) for n in small_names] for src in (w, g, m, v)],
                                         name="adam_small")

    chip_sums(_split_wait(scatter0, len(sent0), "scatter", deltas[0], name="grad_scatter_l0_wait"), sent0, 0)
    g_w_in = split["reduced"][0].reshape(w_in.shape)
    g_w_out = split["reduced"][1].reshape(w_out.shape)

    outs_g, outs_d, outs_m, outs_v = {"w_in": g_w_in, "w_out": g_w_out}, {}, {}, {}
    to_cols, from_cols = (2, 0, 1), (1, 2, 0)
    outs_d["w_in"], outs_m["w_in"], outs_v["w_in"] = [
        jnp.transpose(a, from_cols) for a in _adam_cols_major(
            *[jnp.transpose(a, to_cols) for a in (w_in, g_w_in, m_w_in, v_w_in)], name="adam_w_in")]
    outs_d["w_out"], outs_m["w_out"], outs_v["w_out"] = _adam_big(w_out, g_w_out, m_w_out, v_w_out,
                                                                  name="adam_w_out")
    for n, dn, mn, vn in zip(small_names, deltas, new_ms, new_vs):
        outs_g[n], outs_d[n], outs_m[n], outs_v[n] = (g[n], dn.reshape(w[n].shape), mn.reshape(w[n].shape),
                                                      vn.reshape(w[n].shape))
    return (loss_total, grad_x, *[outs_g[n] for n in WEIGHTS], *[outs_d[n] for n in WEIGHTS],
            *[outs_m[n] for n in WEIGHTS], *[outs_v[n] for n in WEIGHTS])
```

```python
import functools
import math

import jax
import jax.numpy as jnp
import numpy as np
from jax import lax
from jax.experimental import pallas as pl
from jax.experimental.pallas import tpu as pltpu

F32 = jnp.float32
BF16 = jnp.bfloat16
MXU_DTYPE = BF16

D_MODEL = 1024
DEPTH = 2
SSD_HEADS = 16
SSD_HEAD_DIM = 64
SSD_STATE = 128
SSD_CONV = 4
CHUNK = 128
SSD_CONV_DIM = 1536
ATTN_HEAD_DIM = 64
ATTN_Q_HEADS = 8
WINDOW = 128
CONF_WIDTH = 512
CONF_KERNEL = 31
MIX_WIDTH = 2048
D_IN_PROJ = 5392
EPS = 1e-5

ADAM_LR = 0.001
ADAM_B1 = 0.9
ADAM_B2 = 0.999
ADAM_EPS = 1e-08
ADAM_WD = 0.01
ADAM_STEP = 10

LANES = 128
SUBLANES = 8
VMEM_LIMIT = 48 * 1024 * 1024

NP = 5632
OFF_ZA, OFF_Q, OFF_K, OFF_V, OFF_DT = 0, 512, 1024, 1152, 1280
ATTN_GROUP = 1536
OFF_XBC = 1536
OFF_CONF = 3072
OFF_ZS = 4096
OFF_ZC = 5120
SECTIONS = ((0, 1024, OFF_ZS), (1024, 1536, OFF_ZA), (1536, 2048, OFF_ZC), (2048, 3584, OFF_XBC),
            (3584, 3600, OFF_DT), (3600, 4368, OFF_Q), (4368, 5392, OFF_CONF))

YCAT_ATTN, YCAT_CONF = 1024, 1536
ANY = pl.BlockSpec(memory_space=pl.ANY)

NN = (((1,), (0,)), ((), ()))
NT = (((1,), (1,)), ((), ()))
TN = (((0,), (0,)), ((), ()))


def _params(sem):
    return pltpu.CompilerParams(dimension_semantics=sem, vmem_limit_bytes=VMEM_LIMIT)


def _dot(a, b, dims=NN):
    return lax.dot_general(a.astype(MXU_DTYPE), b.astype(MXU_DTYPE), dims, preferred_element_type=F32)


def _split_bf16(a, passes):
    pieces = []
    r = a
    for _ in range(passes):
        p = r.astype(BF16)
        pieces.append(p)
        r = r - p.astype(F32)
    return pieces


def _xdot(a, sel, dims=NN, passes=2):
    out = None
    for p in _split_bf16(a, passes):
        t = lax.dot_general(p, sel, dims, preferred_element_type=F32)
        out = t if out is None else out + t
    return out


def _xdot_r(sel, b, dims=NN, passes=3):
    out = None
    for p in _split_bf16(b, passes):
        t = lax.dot_general(sel, p, dims, preferred_element_type=F32)
        out = t if out is None else out + t
    return out


def _sigmoid(x):
    return 1.0 / (1.0 + jnp.exp(-x))


def _silu(x):
    return x * _sigmoid(x)


def _dsilu(x):
    s = _sigmoid(x)
    return s * (1.0 + x * (1.0 - s))


def _softplus(x):
    return jnp.maximum(x, 0.0) + jnp.log(1.0 + jnp.exp(-jnp.abs(x)))


def _rowsum8(x):
    r, c = x.shape
    return jnp.sum(x.reshape(r // SUBLANES, SUBLANES, c), axis=0)


def _iota(shape, dim):
    return lax.broadcasted_iota(jnp.int32, shape, dim)


def _matmul(a, b, form, out_dtype, tm, tn, tk, name, residual=None, after=None):
    if form == "nn":
        (m, k), n = a.shape, b.shape[1]
    elif form == "nt":
        (m, k), n = a.shape, b.shape[0]
    else:
        (k, m), n = a.shape, b.shape[1]
    tm, tn, tk = min(tm, m), min(tn, n), min(tk, k)
    assert m % tm == 0 and n % tn == 0 and k % tk == 0, (name, m, n, k, tm, tn, tk)
    if form == "nn":
        a_spec = pl.BlockSpec((tm, tk), lambda i, j, s: (i, s))
        b_spec = pl.BlockSpec((tk, tn), lambda i, j, s: (s, j))
        dims = NN
    elif form == "nt":
        (m, k), n = a.shape, b.shape[0]
        a_spec = pl.BlockSpec((tm, tk), lambda i, j, s: (i, s))
        b_spec = pl.BlockSpec((tn, tk), lambda i, j, s: (j, s))
        dims = NT
    else:
        (k, m), n = a.shape, b.shape[1]
        a_spec = pl.BlockSpec((tk, tm), lambda i, j, s: (s, i))
        b_spec = pl.BlockSpec((tk, tn), lambda i, j, s: (s, j))
        dims = TN
    nk = k // tk
    has_res = residual is not None
    deps = [] if after is None else [after]

    def body_single(a_ref, b_ref, *rest):
        o = _dot(a_ref[...], b_ref[...], dims)
        if has_res:
            o = o + rest[0][...]
        rest[-1][...] = o.astype(out_dtype)

    def body(a_ref, b_ref, *rest):
        r_ref = rest[0] if has_res else None
        o_ref, acc = rest[-2:]
        s = pl.program_id(2)

        @pl.when(s == 0)
        def _():
            acc[...] = jnp.zeros_like(acc)

        acc[...] += _dot(a_ref[...], b_ref[...], dims)

        @pl.when(s == nk - 1)
        def _():
            o = acc[...]
            if has_res:
                o = o + r_ref[...]
            o_ref[...] = o.astype(out_dtype)

    in_specs = [a_spec, b_spec]
    args = [a, b]
    if has_res:
        in_specs.append(pl.BlockSpec((tm, tn), lambda i, j, s: (i, j)))
        args.append(residual)
    in_specs += [ANY] * len(deps)
    args += deps
    return pl.pallas_call(
        body_single if nk == 1 else body, name=name,
        out_shape=jax.ShapeDtypeStruct((m, n), out_dtype),
        grid=(m // tm, n // tn, nk),
        in_specs=in_specs,
        out_specs=pl.BlockSpec((tm, tn), lambda i, j, s: (i, j)),
        scratch_shapes=[] if nk == 1 else [pltpu.VMEM((tm, tn), F32)],
        compiler_params=_params(("parallel", "parallel", "arbitrary")),
    )(*args)


ROW_TILE = 256


def _rmsnorm_fwd(x, w, name, after=None):
    t, d = x.shape
    tm = ROW_TILE
    deps = [] if after is None else [after]

    def body(x_ref, w_ref, *rest):
        o_ref, ot_ref = rest[len(deps):]
        xv = x_ref[...]
        rstd = lax.rsqrt(jnp.mean(xv * xv, axis=-1, keepdims=True) + EPS)
        h = xv * rstd * w_ref[...]
        o_ref[...] = h.astype(o_ref.dtype)
        ot_ref[...] = h.T.astype(ot_ref.dtype)

    return pl.pallas_call(
        body, name=name,
        out_shape=(jax.ShapeDtypeStruct((t, d), MXU_DTYPE), jax.ShapeDtypeStruct((d, t), MXU_DTYPE)),
        grid=(t // tm,),
        in_specs=[pl.BlockSpec((tm, d), lambda i: (i, 0)), pl.BlockSpec((1, d), lambda i: (0, 0))]
        + [ANY] * len(deps),
        out_specs=(pl.BlockSpec((tm, d), lambda i: (i, 0)), pl.BlockSpec((d, tm), lambda i: (0, i))),
        compiler_params=_params(("parallel",)),
    )(x, w, *deps)


def _rmsnorm_bwd(dh, x, w, dres, name):
    t, d = x.shape
    tm = ROW_TILE
    nt = t // tm

    def body(dh_ref, x_ref, w_ref, dr_ref, dx_ref, dw_ref, acc):
        i = pl.program_id(0)

        @pl.when(i == 0)
        def _():
            acc[...] = jnp.zeros_like(acc)

        xv = x_ref[...]
        rstd = lax.rsqrt(jnp.mean(xv * xv, axis=-1, keepdims=True) + EPS)
        xh = xv * rstd
        dhv = dh_ref[...]
        g = dhv * w_ref[...]
        dx_ref[...] = dr_ref[...] + rstd * (g - xh * jnp.mean(g * xh, axis=-1, keepdims=True))
        acc[...] += _rowsum8(dhv * xh)

        @pl.when(i == nt - 1)
        def _():
            dw_ref[...] = jnp.sum(acc[...], axis=0, keepdims=True)

    row = pl.BlockSpec((tm, d), lambda i: (i, 0))
    vec = pl.BlockSpec((1, d), lambda i: (0, 0))
    return pl.pallas_call(
        body, name=name,
        out_shape=(jax.ShapeDtypeStruct((t, d), F32), jax.ShapeDtypeStruct((1, d), F32)),
        grid=(nt,),
        in_specs=[row, row, vec, row],
        out_specs=(row, vec),
        scratch_shapes=[pltpu.VMEM((SUBLANES, d), F32)],
        compiler_params=_params(("arbitrary",)),
    )(dh, x, w, dres)


def _loss_head(xf, target, w, name):
    t, d = xf.shape
    tm = ROW_TILE
    nt = t // tm

    def body(x_ref, t_ref, w_ref, loss_ref, dx_ref, dw_ref, lacc, wacc):
        i = pl.program_id(0)

        @pl.when(i == 0)
        def _():
            lacc[...] = jnp.zeros_like(lacc)
            wacc[...] = jnp.zeros_like(wacc)

        xv = x_ref[...]
        rstd = lax.rsqrt(jnp.mean(xv * xv, axis=-1, keepdims=True) + EPS)
        xh = xv * rstd
        err = xh * w_ref[...] - t_ref[...]
        lacc[...] += jnp.sum(err * err)
        dy = err * (1.0 / d)
        g = dy * w_ref[...]
        dx_ref[...] = rstd * (g - xh * jnp.mean(g * xh, axis=-1, keepdims=True))
        wacc[...] += _rowsum8(dy * xh)

        @pl.when(i == nt - 1)
        def _():
            loss_ref[...] = lacc[...] * (0.5 / d)
            dw_ref[...] = jnp.sum(wacc[...], axis=0, keepdims=True)

    row = pl.BlockSpec((tm, d), lambda i: (i, 0))
    vec = pl.BlockSpec((1, d), lambda i: (0, 0))
    return pl.pallas_call(
        body, name=name,
        out_shape=(jax.ShapeDtypeStruct((SUBLANES, LANES), F32), jax.ShapeDtypeStruct((t, d), F32),
                   jax.ShapeDtypeStruct((1, d), F32)),
        grid=(nt,),
        in_specs=[row, row, vec],
        out_specs=(pl.BlockSpec((SUBLANES, LANES), lambda i: (0, 0)), row, vec),
        scratch_shapes=[pltpu.VMEM((SUBLANES, LANES), F32), pltpu.VMEM((SUBLANES, d), F32)],
        compiler_params=_params(("arbitrary",)),
    )(xf, target, w)


def _conf_post_bwd(dycat, c1, proj, ln_w, ln_b, dproj, name):
    t = c1.shape[0]
    tm, cw = ROW_TILE, CONF_WIDTH
    nt = t // tm

    def body(dy_ref, c_ref, z_ref, w_ref, b_ref, _, dc_ref, dz_ref, dw_ref, db_ref, wacc, bacc):
        i = pl.program_id(0)

        @pl.when(i == 0)
        def _():
            wacc[...] = jnp.zeros_like(wacc)
            bacc[...] = jnp.zeros_like(bacc)

        cv = c_ref[...]
        xc = cv - jnp.mean(cv, axis=-1, keepdims=True)
        rstd = lax.rsqrt(jnp.mean(xc * xc, axis=-1, keepdims=True) + EPS)
        xh = xc * rstd
        c2 = xh * w_ref[...] + b_ref[...]
        zv = z_ref[...]
        dy = dy_ref[...]
        dz_ref[...] = (dy * _silu(c2) * _dsilu(zv)).astype(dz_ref.dtype)
        dc2 = dy * _silu(zv) * _dsilu(c2)
        bacc[...] += _rowsum8(dc2)
        wacc[...] += _rowsum8(dc2 * xh)
        dxh = dc2 * w_ref[...]
        dc_ref[...] = rstd * (dxh - jnp.mean(dxh, axis=-1, keepdims=True)
                              - xh * jnp.mean(dxh * xh, axis=-1, keepdims=True))

        @pl.when(i == nt - 1)
        def _():
            dw_ref[...] = jnp.sum(wacc[...], axis=0, keepdims=True)
            db_ref[...] = jnp.sum(bacc[...], axis=0, keepdims=True)

    row = pl.BlockSpec((tm, cw), lambda i: (i, 0))
    vec = pl.BlockSpec((1, cw), lambda i: (0, 0))
    return pl.pallas_call(
        body, name=name,
        out_shape=(jax.ShapeDtypeStruct((t, cw), F32), jax.ShapeDtypeStruct(dproj.shape, dproj.dtype),
                   jax.ShapeDtypeStruct((1, cw), F32), jax.ShapeDtypeStruct((1, cw), F32)),
        grid=(nt,),
        in_specs=[pl.BlockSpec((tm, cw), lambda i: (i, YCAT_CONF // cw)), row,
                  pl.BlockSpec((tm, cw), lambda i: (i, OFF_ZC // cw)), vec, vec, ANY],
        out_specs=(row, pl.BlockSpec((tm, cw), lambda i: (i, OFF_ZC // cw)), vec, vec),
        input_output_aliases={5: 1},
        scratch_shapes=[pltpu.VMEM((SUBLANES, cw), F32), pltpu.VMEM((SUBLANES, cw), F32)],
        compiler_params=_params(("arbitrary",)),
    )(dycat, c1, proj, ln_w, ln_b, dproj)


CONV_TILE = 512
CONV_COLS = 512
CONV_SUB_ROWS = 128
CONV_SUB_COLS = LANES


def _conv_halo(k):
    return SUBLANES if k - 1 <= SUBLANES else 32


def _conv_subtiles(tm, cw):
    return [(r0, c0) for r0 in range(0, tm, CONV_SUB_ROWS) for c0 in range(0, cw, CONV_SUB_COLS)]


def _conv_use_shifted(k):
    return k > SUBLANES


def _conv_shift_scratch(k, rows, cw):
    return [pltpu.VMEM((SUBLANES - 1, rows - SUBLANES, cw), F32)] if _conv_use_shifted(k) else []


def _conv_fill_shifted(ext, sh):
    n = sh.shape[1]
    for b in range(1, SUBLANES):
        sh[b - 1] = ext[b:b + n, :]


def _conv_rows(ext, sh, start, rows, cs):
    b = start % SUBLANES
    if b == 0 or not sh:
        return ext[start:start + rows, cs]
    return sh[0][b - 1, start - b:start - b + rows, cs]


def _conv_fwd(src, col0, width, w, bias, k, seq, name):
    t = src.shape[0]
    tm, cw, halo = CONV_TILE, CONV_COLS, _conv_halo(k)
    sr, sc = CONV_SUB_ROWS, CONV_SUB_COLS
    p = k - 1
    cb0 = col0 // cw
    kp = w.shape[0]

    shifted = _conv_use_shifted(k)

    def body(x_ref, h_ref, w_ref, b_ref, o_ref, ext, *sh):
        i = pl.program_id(0)
        seq_start = (i * tm) % seq == 0
        ext[halo:, :] = x_ref[...]
        ext[:halo, :] = jnp.where(seq_start, 0.0, h_ref[...])
        if shifted:
            _conv_fill_shifted(ext, sh[0])
        for r0, c0 in _conv_subtiles(tm, cw):
            cs = slice(c0, c0 + sc)
            acc = jnp.zeros((sr, sc), F32) + b_ref[:, cs]
            for j in range(k):
                acc = acc + w_ref[j:j + 1, cs] * _conv_rows(ext, sh, r0 + halo - p + j, sr, cs)
            o_ref[r0:r0 + sr, cs] = acc

    return pl.pallas_call(
        body, name=name,
        out_shape=jax.ShapeDtypeStruct((t, width), F32),
        grid=(t // tm, width // cw),
        in_specs=[pl.BlockSpec((tm, cw), lambda i, j: (i, cb0 + j)),
                  pl.BlockSpec((halo, cw), lambda i, j: (jnp.maximum(i * (tm // halo) - 1, 0), cb0 + j)),
                  pl.BlockSpec((kp, cw), lambda i, j: (0, j)),
                  pl.BlockSpec((1, cw), lambda i, j: (0, j))],
        out_specs=pl.BlockSpec((tm, cw), lambda i, j: (i, j)),
        scratch_shapes=[pltpu.VMEM((halo + tm, cw), F32)] + _conv_shift_scratch(k, halo + tm, cw),
        compiler_params=_params(("parallel", "parallel")),
    )(src, src, w, bias)


def _conv_bwd(dy, src, col0, width, w, k, seq, name, into=None):
    t = src.shape[0]
    tm, cw, halo = CONV_TILE, CONV_COLS, _conv_halo(k)
    sr, sc = CONV_SUB_ROWS, CONV_SUB_COLS
    p = k - 1
    cb0 = col0 // cw
    kp = w.shape[0]
    nt = t // tm
    last_halo = t // halo - 1

    shifted = _conv_use_shifted(k)

    def body(dy_ref, dn_ref, x_ref, xp_ref, w_ref, *rest):
        if into is not None:
            rest = rest[1:]
        dx_ref, dw_ref, db_ref, dyext, xext, wacc, bacc = rest[:7]
        sh = rest[7:]
        i = pl.program_id(1)
        dysh, xsh = (sh[:1], sh[1:]) if shifted else ((), ())

        @pl.when(i == 0)
        def _():
            wacc[...] = jnp.zeros_like(wacc)
            bacc[...] = jnp.zeros_like(bacc)

        seq_start = (i * tm) % seq == 0
        seq_end = ((i + 1) * tm) % seq == 0
        dyext[:tm, :] = dy_ref[...]
        dyext[tm:, :] = jnp.where(seq_end, 0.0, dn_ref[...])
        xext[halo:, :] = x_ref[...]
        xext[:halo, :] = jnp.where(seq_start, 0.0, xp_ref[...])
        if shifted:
            _conv_fill_shifted(dyext, dysh[0])
            _conv_fill_shifted(xext, xsh[0])
        for r0, c0 in _conv_subtiles(tm, cw):
            cs = slice(c0, c0 + sc)
            dyv = dy_ref[r0:r0 + sr, cs]
            acc = jnp.zeros((sr, sc), F32)
            for j in range(k):
                acc = acc + w_ref[j:j + 1, cs] * _conv_rows(dyext, dysh, r0 + p - j, sr, cs)
                wacc[j, :, cs] += _rowsum8(dyv * _conv_rows(xext, xsh, r0 + halo - p + j, sr, cs))
            dx_ref[r0:r0 + sr, cs] = acc.astype(dx_ref.dtype)
            bacc[:, cs] += _rowsum8(dyv)

        @pl.when(i == nt - 1)
        def _():
            dw_ref[...] = jnp.zeros_like(dw_ref)
            for j in range(k):
                dw_ref[j:j + 1, :] = jnp.sum(wacc[j], axis=0, keepdims=True)
            db_ref[...] = jnp.sum(bacc[...], axis=0, keepdims=True)

    if into is None:
        dx_shape = jax.ShapeDtypeStruct((t, width), F32)
        dx_spec = pl.BlockSpec((tm, cw), lambda j, i: (i, j))
        extra_specs, extra_args, aliases = [], [], {}
    else:
        dx_shape = jax.ShapeDtypeStruct(into.shape, into.dtype)
        dx_spec = pl.BlockSpec((tm, cw), lambda j, i: (i, cb0 + j))
        extra_specs, extra_args, aliases = [ANY], [into], {5: 0}
    return pl.pallas_call(
        body, name=name,
        out_shape=(dx_shape, jax.ShapeDtypeStruct((kp, width), F32), jax.ShapeDtypeStruct((1, width), F32)),
        grid=(width // cw, nt),
        in_specs=[pl.BlockSpec((tm, cw), lambda j, i: (i, j)),
                  pl.BlockSpec((halo, cw), lambda j, i: (jnp.minimum((i + 1) * (tm // halo), last_halo), j)),
                  pl.BlockSpec((tm, cw), lambda j, i: (i, cb0 + j)),
                  pl.BlockSpec((halo, cw), lambda j, i: (jnp.maximum(i * (tm // halo) - 1, 0), cb0 + j)),
                  pl.BlockSpec((kp, cw), lambda j, i: (0, j))] + extra_specs,
        out_specs=(dx_spec,
                   pl.BlockSpec((kp, cw), lambda j, i: (0, j)),
                   pl.BlockSpec((1, cw), lambda j, i: (0, j))),
        input_output_aliases=aliases,
        scratch_shapes=[pltpu.VMEM((tm + halo, cw), F32), pltpu.VMEM((halo + tm, cw), F32),
                        pltpu.VMEM((kp, SUBLANES, cw), F32), pltpu.VMEM((SUBLANES, cw), F32)]
        + 2 * _conv_shift_scratch(k, halo + tm, cw),
        compiler_params=_params(("parallel", "arbitrary")),
    )(dy, dy, src, src, w, *extra_args)


def _conf_specs(tm, cw, halo, order):
    cb = OFF_CONF // cw

    def blk(col):
        return pl.BlockSpec((tm, cw), lambda *g: (order(*g), col))

    def prev(col):
        return pl.BlockSpec((halo, cw), lambda *g: (jnp.maximum(order(*g) * (tm // halo) - 1, 0), col))

    return blk(cb), prev(cb), blk(cb + 1), prev(cb + 1)


def _glu_window(ext, a_ref, ah_ref, g_ref, gh_ref, seq_start, halo):
    ext[halo:, :] = a_ref[...] * _sigmoid(g_ref[...])
    ext[:halo, :] = jnp.where(seq_start, 0.0, ah_ref[...] * _sigmoid(gh_ref[...]))


def _conf_fwd(proj, w, bias, ln_w, ln_b, ycat, seq, name):
    t = proj.shape[0]
    k = CONF_KERNEL
    tm, cw, halo = CONV_TILE, CONF_WIDTH, _conv_halo(k)
    sr, sc = CONV_SUB_ROWS, CONV_SUB_COLS
    p = k - 1
    kp = w.shape[0]

    def body(a_ref, ah_ref, g_ref, gh_ref, z_ref, w_ref, b_ref, lw_ref, lb_ref, _, c1_ref, y_ref, ext, sh):
        i = pl.program_id(0)
        _glu_window(ext, a_ref, ah_ref, g_ref, gh_ref, (i * tm) % seq == 0, halo)
        _conv_fill_shifted(ext, sh)
        for r0, c0 in _conv_subtiles(tm, cw):
            cs = slice(c0, c0 + sc)
            acc = jnp.zeros((sr, sc), F32) + b_ref[:, cs]
            for j in range(k):
                acc = acc + w_ref[j:j + 1, cs] * _conv_rows(ext, (sh,), r0 + halo - p + j, sr, cs)
            c1_ref[r0:r0 + sr, cs] = acc
        for r0 in range(0, tm, sr):
            rows = slice(r0, r0 + sr)
            cv = c1_ref[rows, :]
            xc = cv - jnp.mean(cv, axis=-1, keepdims=True)
            rstd = lax.rsqrt(jnp.mean(xc * xc, axis=-1, keepdims=True) + EPS)
            c2 = xc * rstd * lw_ref[...] + lb_ref[...]
            y_ref[rows, :] = (_silu(c2) * _silu(z_ref[rows, :])).astype(y_ref.dtype)

    vec = pl.BlockSpec((1, cw), lambda i: (0, 0))
    row = pl.BlockSpec((tm, cw), lambda i: (i, 0))
    return pl.pallas_call(
        body, name=name,
        out_shape=(jax.ShapeDtypeStruct((t, cw), F32), jax.ShapeDtypeStruct(ycat.shape, ycat.dtype)),
        grid=(t // tm,),
        in_specs=[*_conf_specs(tm, cw, halo, lambda i: i),
                  pl.BlockSpec((tm, cw), lambda i: (i, OFF_ZC // cw)),
                  pl.BlockSpec((kp, cw), lambda i: (0, 0)), vec, vec, vec, ANY],
        out_specs=(row, pl.BlockSpec((tm, cw), lambda i: (i, YCAT_CONF // cw))),
        input_output_aliases={9: 1},
        scratch_shapes=[pltpu.VMEM((halo + tm, cw), F32)] + _conv_shift_scratch(k, halo + tm, cw),
        compiler_params=_params(("parallel",)),
    )(proj, proj, proj, proj, proj, w, bias, ln_w, ln_b, ycat)


def _conf_conv_bwd(dc1, proj, w, dproj, seq, name):
    t = proj.shape[0]
    k = CONF_KERNEL
    tm, cw, halo = CONV_TILE, CONF_WIDTH, _conv_halo(k)
    sr, sc = CONV_SUB_ROWS, CONV_SUB_COLS
    p = k - 1
    kp = w.shape[0]
    nt = t // tm
    last_halo = t // halo - 1

    def body(dy_ref, dn_ref, a_ref, ah_ref, g_ref, gh_ref, w_ref, _, dag_ref, dw_ref, db_ref,
             dyext, xext, wacc, bacc, dysh, xsh):
        i = pl.program_id(0)

        @pl.when(i == 0)
        def _():
            wacc[...] = jnp.zeros_like(wacc)
            bacc[...] = jnp.zeros_like(bacc)

        seq_end = ((i + 1) * tm) % seq == 0
        dyext[:tm, :] = dy_ref[...]
        dyext[tm:, :] = jnp.where(seq_end, 0.0, dn_ref[...])
        _glu_window(xext, a_ref, ah_ref, g_ref, gh_ref, (i * tm) % seq == 0, halo)
        _conv_fill_shifted(dyext, dysh)
        _conv_fill_shifted(xext, xsh)
        for r0, c0 in _conv_subtiles(tm, cw):
            cs = slice(c0, c0 + sc)
            rows = slice(r0, r0 + sr)
            dyv = dy_ref[rows, cs]
            acc = jnp.zeros((sr, sc), F32)
            for j in range(k):
                acc = acc + w_ref[j:j + 1, cs] * _conv_rows(dyext, (dysh,), r0 + p - j, sr, cs)
                wacc[j, :, cs] += _rowsum8(dyv * _conv_rows(xext, (xsh,), r0 + halo - p + j, sr, cs))
            bacc[:, cs] += _rowsum8(dyv)
            s = _sigmoid(g_ref[rows, cs])
            dag_ref[rows, cs] = (acc * s).astype(dag_ref.dtype)
            dag_ref[rows, cw + c0:cw + c0 + sc] = (acc * a_ref[rows, cs] * s * (1.0 - s)).astype(dag_ref.dtype)

        @pl.when(i == nt - 1)
        def _():
            dw_ref[...] = jnp.zeros_like(dw_ref)
            for j in range(k):
                dw_ref[j:j + 1, :] = jnp.sum(wacc[j], axis=0, keepdims=True)
            db_ref[...] = jnp.sum(bacc[...], axis=0, keepdims=True)

    return pl.pallas_call(
        body, name=name,
        out_shape=(jax.ShapeDtypeStruct(dproj.shape, dproj.dtype), jax.ShapeDtypeStruct((kp, cw), F32),
                   jax.ShapeDtypeStruct((1, cw), F32)),
        grid=(nt,),
        in_specs=[pl.BlockSpec((tm, cw), lambda i: (i, 0)),
                  pl.BlockSpec((halo, cw), lambda i: (jnp.minimum((i + 1) * (tm // halo), last_halo), 0)),
                  *_conf_specs(tm, cw, halo, lambda i: i),
                  pl.BlockSpec((kp, cw), lambda i: (0, 0)), ANY],
        out_specs=(pl.BlockSpec((tm, 2 * cw), lambda i: (i, OFF_CONF // (2 * cw))),
                   pl.BlockSpec((kp, cw), lambda i: (0, 0)), pl.BlockSpec((1, cw), lambda i: (0, 0))),
        input_output_aliases={7: 0},
        scratch_shapes=[pltpu.VMEM((tm + halo, cw), F32), pltpu.VMEM((halo + tm, cw), F32),
                        pltpu.VMEM((kp, SUBLANES, cw), F32), pltpu.VMEM((SUBLANES, cw), F32)]
        + 2 * _conv_shift_scratch(k, halo + tm, cw),
        compiler_params=_params(("arbitrary",)),
    )(dc1, dc1, proj, proj, proj, proj, w, dproj)


def _half_mask(half):
    lane = _iota((1, LANES), 1)
    return ((lane >= half * ATTN_HEAD_DIM) & (lane < (half + 1) * ATTN_HEAD_DIM)).astype(F32)


def _stack_heads(xp, g):
    m = _half_mask(g)
    swapped = pltpu.roll(xp, ATTN_HEAD_DIM, axis=1)
    return jnp.concatenate([xp * m, swapped * m] if g == 0 else [swapped * m, xp * m], axis=0)


def _unstack_heads(both, g):
    w = both.shape[0] // 2
    top, bot = both[:w], both[w:]
    lo, hi = _half_mask(0), _half_mask(1)
    if g == 0:
        return top * lo + pltpu.roll(bot, ATTN_HEAD_DIM, axis=1) * hi
    return pltpu.roll(top, ATTN_HEAD_DIM, axis=1) * lo + bot * hi


def _band_mask(first_block):
    w = WINDOW
    qi = _iota((w, 2 * w), 0)
    kj = _iota((w, 2 * w), 1) - w
    rel = qi - kj
    return (rel >= 0) & (rel < w) & (jnp.logical_not(first_block) | (kj >= 0))


def _lane_pick(x, h):
    return jnp.sum(jnp.where(_iota(x.shape, 1) == h, x, 0.0), axis=1, keepdims=True)


def _attn_specs(nb, rev):
    w = WINDOW

    def blk(i):
        return nb - 1 - i if rev else i

    def row(b, i):
        return b * nb + blk(i)

    def prow(b, i):
        return b * nb + jnp.maximum(blk(i) - 1, 0)

    q = pl.BlockSpec((w, 512), lambda b, i: (row(b, i), OFF_Q // 512))
    kc = pl.BlockSpec((w, 128), lambda b, i: (row(b, i), OFF_K // 128))
    kp = pl.BlockSpec((w, 128), lambda b, i: (prow(b, i), OFF_K // 128))
    vc = pl.BlockSpec((w, 128), lambda b, i: (row(b, i), OFF_V // 128))
    vp = pl.BlockSpec((w, 128), lambda b, i: (prow(b, i), OFF_V // 128))
    z = pl.BlockSpec((w, 512), lambda b, i: (row(b, i), OFF_ZA // 512))
    return q, kc, kp, vc, vp, z, row


def _attn_fwd(proj, sinks, ycat, nbatch, name):
    t = proj.shape[0]
    w = WINDOW
    nb = t // nbatch // w
    scale = ATTN_HEAD_DIM ** -0.5
    q_s, kc_s, kp_s, vc_s, vp_s, z_s, row = _attn_specs(nb, False)

    def body(q_ref, kc_ref, kp_ref, vc_ref, vp_ref, z_ref, sk_ref, _, y_ref, o_ref, lse_ref):
        first = pl.program_id(1) == 0
        mask = _band_mask(first)
        kk = jnp.concatenate([kp_ref[...], kc_ref[...]], axis=0).astype(MXU_DTYPE)
        vv = jnp.concatenate([vp_ref[...], vc_ref[...]], axis=0).astype(MXU_DTYPE)
        sk = sk_ref[...]
        lane = _iota((w, LANES), 1)
        mask2 = jnp.concatenate([mask, mask], axis=0)
        scores = [_dot(_stack_heads(q_ref[:, j * LANES:(j + 1) * LANES], j // 2), kk, NT) for j in range(4)]
        lse_all = jnp.zeros((w, LANES), F32)
        for j in range(4):
            s = jnp.where(mask2, scores[j] * scale, -1e30)
            skc = jnp.concatenate([jnp.broadcast_to(_lane_pick(sk, 2 * j), (w, 1)),
                                   jnp.broadcast_to(_lane_pick(sk, 2 * j + 1), (w, 1))], axis=0)
            m = jnp.maximum(jnp.max(s, axis=1, keepdims=True), skc)
            den = jnp.sum(jnp.exp(s - m), axis=1, keepdims=True) + jnp.exp(skc - m)
            lse = m + jnp.log(den)
            lse_all = jnp.where(lane == 2 * j, lse[:w], lse_all)
            lse_all = jnp.where(lane == 2 * j + 1, lse[w:], lse_all)
            op = _unstack_heads(_dot(jnp.exp(s - lse), vv), j // 2)
            cols = slice(j * LANES, (j + 1) * LANES)
            o_ref[:, cols] = op
            y_ref[:, cols] = (op * _silu(z_ref[:, cols])).astype(y_ref.dtype)
        lse_ref[...] = lse_all

    return pl.pallas_call(
        body, name=name,
        out_shape=(jax.ShapeDtypeStruct(ycat.shape, ycat.dtype), jax.ShapeDtypeStruct((t, 512), F32),
                   jax.ShapeDtypeStruct((t, LANES), F32)),
        grid=(nbatch, nb),
        in_specs=[q_s, kc_s, kp_s, vc_s, vp_s, z_s, pl.BlockSpec((1, LANES), lambda b, i: (0, 0)), ANY],
        out_specs=(pl.BlockSpec((w, 512), lambda b, i: (row(b, i), YCAT_ATTN // 512)),
                   pl.BlockSpec((w, 512), lambda b, i: (row(b, i), 0)),
                   pl.BlockSpec((w, LANES), lambda b, i: (row(b, i), 0))),
        input_output_aliases={7: 0},
        compiler_params=_params(("parallel", "parallel")),
    )(proj, proj, proj, proj, proj, proj, sinks, ycat)


def _attn_bwd(dycat, proj, o, lse, sinks, ddt, dproj, nbatch, name):
    t = proj.shape[0]
    w = WINDOW
    nb = t // nbatch // w
    scale = ATTN_HEAD_DIM ** -0.5
    q_s, kc_s, kp_s, vc_s, vp_s, z_s, row = _attn_specs(nb, True)

    def body(dy_ref, q_ref, kc_ref, kp_ref, vc_ref, vp_ref, z_ref, o_ref, lse_ref, sk_ref, ddt_ref, _,
             grp_ref, dsk_ref, kcarry, vcarry, sacc):
        b, i = pl.program_id(0), pl.program_id(1)

        @pl.when((b == 0) & (i == 0))
        def _():
            sacc[...] = jnp.zeros_like(sacc)

        @pl.when(i == 0)
        def _():
            kcarry[...] = jnp.zeros_like(kcarry)
            vcarry[...] = jnp.zeros_like(vcarry)

        first = i == nb - 1
        mask = _band_mask(first)
        kk = jnp.concatenate([kp_ref[...], kc_ref[...]], axis=0).astype(MXU_DTYPE)
        vv = jnp.concatenate([vp_ref[...], vc_ref[...]], axis=0).astype(MXU_DTYPE)
        sk = sk_ref[...]
        lse_all = lse_ref[...]
        lane1 = _iota((1, LANES), 1)
        mask2 = jnp.concatenate([mask, mask], axis=0)
        qs, dos, deltas, lses, scores, dps = [], [], [], [], [], []
        for j in range(4):
            cols = slice(j * LANES, (j + 1) * LANES)
            qp, zp, ov, dy = q_ref[:, cols], z_ref[:, cols], o_ref[:, cols], dy_ref[:, cols]
            grp_ref[:, OFF_ZA + j * LANES:OFF_ZA + (j + 1) * LANES] = (dy * ov * _dsilu(zp)).astype(grp_ref.dtype)
            do = dy * _silu(zp)
            q2 = _stack_heads(qp, j // 2).astype(MXU_DTYPE)
            do2 = _stack_heads(do, j // 2)
            qs.append(q2)
            dos.append(do2.astype(MXU_DTYPE))
            deltas.append(jnp.sum(do2 * _stack_heads(ov, j // 2), axis=1, keepdims=True))
            lses.append(jnp.concatenate([_lane_pick(lse_all, 2 * j), _lane_pick(lse_all, 2 * j + 1)], axis=0))
            scores.append(_dot(q2, kk, NT))
            dps.append(_dot(do2, vv, NT))
        prs, dss = [], []
        dsk = jnp.zeros((1, LANES), F32)
        for j in range(4):
            pr = jnp.exp(jnp.where(mask2, scores[j] * scale, -1e30) - lses[j])
            prs.append(pr.astype(MXU_DTYPE))
            dss.append((pr * (dps[j] - deltas[j])).astype(MXU_DTYPE))
            skc = jnp.concatenate([jnp.broadcast_to(_lane_pick(sk, 2 * j), (w, 1)),
                                   jnp.broadcast_to(_lane_pick(sk, 2 * j + 1), (w, 1))], axis=0)
            sink_term = jnp.exp(skc - lses[j]) * deltas[j]
            dsk = dsk - jnp.where(lane1 == 2 * j, jnp.sum(sink_term[:w]), 0.0)
            dsk = dsk - jnp.where(lane1 == 2 * j + 1, jnp.sum(sink_term[w:]), 0.0)
        dkk = jnp.zeros((2 * w, LANES), F32)
        dvv = jnp.zeros((2 * w, LANES), F32)
        for j in range(4):
            dq = _unstack_heads(_dot(dss[j], kk) * scale, j // 2)
            grp_ref[:, OFF_Q + j * LANES:OFF_Q + (j + 1) * LANES] = dq.astype(grp_ref.dtype)
            dkk = dkk + _dot(dss[j], qs[j], TN) * scale
            dvv = dvv + _dot(prs[j], dos[j], TN)
        grp_ref[:, OFF_K:OFF_K + LANES] = (dkk[w:, :] + kcarry[...]).astype(grp_ref.dtype)
        grp_ref[:, OFF_V:OFF_V + LANES] = (dvv[w:, :] + vcarry[...]).astype(grp_ref.dtype)
        grp_ref[:, OFF_DT:OFF_DT + LANES] = ddt_ref[...].astype(grp_ref.dtype)
        grp_ref[:, OFF_DT + LANES:] = jnp.zeros((w, ATTN_GROUP - OFF_DT - LANES), grp_ref.dtype)
        kcarry[...] = dkk[:w, :]
        vcarry[...] = dvv[:w, :]
        sacc[...] += dsk

        @pl.when((b == nbatch - 1) & (i == nb - 1))
        def _():
            dsk_ref[...] = sacc[...]

    return pl.pallas_call(
        body, name=name,
        out_shape=(jax.ShapeDtypeStruct(dproj.shape, dproj.dtype), jax.ShapeDtypeStruct((1, LANES), F32)),
        grid=(nbatch, nb),
        in_specs=[pl.BlockSpec((w, 512), lambda b, i: (row(b, i), YCAT_ATTN // 512)),
                  q_s, kc_s, kp_s, vc_s, vp_s, z_s,
                  pl.BlockSpec((w, 512), lambda b, i: (row(b, i), 0)),
                  pl.BlockSpec((w, LANES), lambda b, i: (row(b, i), 0)),
                  pl.BlockSpec((1, LANES), lambda b, i: (0, 0)),
                  pl.BlockSpec((w, LANES), lambda b, i: (row(b, i), 0)), ANY],
        out_specs=(pl.BlockSpec((w, ATTN_GROUP), lambda b, i: (row(b, i), 0)),
                   pl.BlockSpec((1, LANES), lambda b, i: (0, 0))),
        input_output_aliases={11: 0},
        scratch_shapes=[pltpu.VMEM((w, LANES), F32), pltpu.VMEM((w, LANES), F32),
                        pltpu.VMEM((1, LANES), F32)],
        compiler_params=_params(("arbitrary", "arbitrary")),
    )(dycat, proj, proj, proj, proj, proj, proj, o, lse, sinks, ddt, dproj)


SSD_WIDTH = SSD_HEADS * SSD_HEAD_DIM
GROUP_ROWS = SSD_WIDTH // 2


def _expand_mat():
    r, c = _iota((LANES, SSD_WIDTH), 0), _iota((LANES, SSD_WIDTH), 1)
    return (r == lax.shift_right_logical(c, 6)).astype(BF16)


def _expand_mat_t():
    r, c = _iota((SSD_WIDTH, LANES), 0), _iota((SSD_WIDTH, LANES), 1)
    return (c == lax.shift_right_logical(r, 6)).astype(BF16)


def _ssd_common(u_ref, dt_ref, dtb_ref, a_ref):
    q = CHUNK
    act = _silu(u_ref[...])
    xs = act[:, :SSD_WIDTH]
    bm = act[:, SSD_WIDTH:SSD_WIDTH + 256]
    cm = act[:, SSD_WIDTH + 256:]
    dtp = _softplus(dt_ref[...] + dtb_ref[...])
    a = dtp * a_ref[...]
    tril = (_iota((q, q), 0) >= _iota((q, q), 1)).astype(BF16)
    acs = _xdot_r(tril, a)
    acs_t = acs.T
    e = _expand_mat()
    dt_x = _xdot(dtp, e)
    ea = jnp.exp(_xdot(acs, e))
    a_end = jnp.sum(jnp.where(_iota(acs.shape, 0) == q - 1, acs, 0.0), axis=0, keepdims=True)
    dec = jnp.exp(_xdot(a_end - acs, e))
    a_end_col = jnp.broadcast_to(_lane_pick(acs_t, q - 1), (LANES, LANES))
    s_scale = jnp.exp(_xdot_r(_expand_mat_t(), a_end_col))
    return act, xs, bm, cm, dtp, acs, acs_t, dt_x, ea, dec, s_scale, tril


def _decay_mat(acs, acs_t, h):
    q = CHUNK
    col = _lane_pick(acs, h)
    rowv = jnp.sum(jnp.where(_iota(acs_t.shape, 0) == h, acs_t, 0.0), axis=0, keepdims=True)
    causal = _iota((q, q), 0) >= _iota((q, q), 1)
    return jnp.exp(jnp.where(causal, col - rowv, -1e30))


GN_WIDTH = 512


def _ssd_fwd(u, proj, dtb, a_neg, d_x, norm_w, ycat, nbatch, name):
    t = u.shape[0]
    q = CHUNK
    nc = t // nbatch // q

    def body(u_ref, dt_ref, z_ref, dtb_ref, a_ref, dx_ref, nw_ref, _, y_ref, st_ref, yn_ref, state):
        c = pl.program_id(1)

        @pl.when(c == 0)
        def _():
            state[...] = jnp.zeros_like(state)

        st_ref[...] = state[...]
        act, xs, bm, cm, dtp, acs, acs_t, dt_x, ea, dec, s_scale, _ = _ssd_common(u_ref, dt_ref, dtb_ref, a_ref)
        xdt = xs * dt_x
        xdec = xdt * dec
        lo, hi = _half_mask(0), _half_mask(1)
        for g in range(2):
            bg = bm[:, g * LANES:(g + 1) * LANES]
            cg = cm[:, g * LANES:(g + 1) * LANES]
            rows = slice(g * GROUP_ROWS, (g + 1) * GROUP_ROWS)
            sg = state[rows, :]
            cb = _dot(cg, bg, NT)
            yoff = _dot(cg, sg, NT)
            for j in range(4):
                pj = g * 4 + j
                cols = slice(pj * LANES, (pj + 1) * LANES)
                xp = xdt[:, cols]
                m0 = cb * _decay_mat(acs, acs_t, 2 * pj)
                m1 = cb * _decay_mat(acs, acs_t, 2 * pj + 1)
                yp = _dot(m0, xp * lo) + _dot(m1, xp * hi)
                yp = yp + yoff[:, j * LANES:(j + 1) * LANES] * ea[:, cols]
                y_ref[:, cols] = yp + dx_ref[:, cols] * xs[:, cols]
            state[rows, :] = s_scale[rows, :] * sg + _dot(xdec[:, rows], bg, TN)
        for g in range(SSD_WIDTH // GN_WIDTH):
            cols = slice(g * GN_WIDTH, (g + 1) * GN_WIDTH)
            gg = y_ref[:, cols] * _silu(z_ref[:, cols])
            rstd = lax.rsqrt(jnp.mean(gg * gg, axis=-1, keepdims=True) + EPS)
            yn_ref[:, cols] = (gg * rstd * nw_ref[:, cols]).astype(yn_ref.dtype)

    vec = pl.BlockSpec((1, LANES), lambda b, c: (0, 0))
    wide = pl.BlockSpec((q, SSD_WIDTH), lambda b, c: (b * nc + c, 0))
    wvec = pl.BlockSpec((1, SSD_WIDTH), lambda b, c: (0, 0))
    return pl.pallas_call(
        body, name=name,
        out_shape=(jax.ShapeDtypeStruct((t, SSD_WIDTH), F32),
                   jax.ShapeDtypeStruct((nbatch * nc * SSD_WIDTH, SSD_STATE), F32),
                   jax.ShapeDtypeStruct(ycat.shape, ycat.dtype)),
        grid=(nbatch, nc),
        in_specs=[pl.BlockSpec((q, SSD_CONV_DIM), lambda b, c: (b * nc + c, 0)),
                  pl.BlockSpec((q, LANES), lambda b, c: (b * nc + c, OFF_DT // LANES)),
                  pl.BlockSpec((q, SSD_WIDTH), lambda b, c: (b * nc + c, OFF_ZS // SSD_WIDTH)),
                  vec, vec, wvec, wvec, ANY],
        out_specs=(wide, pl.BlockSpec((SSD_WIDTH, SSD_STATE), lambda b, c: (b * nc + c, 0)), wide),
        input_output_aliases={7: 2},
        scratch_shapes=[pltpu.VMEM((SSD_WIDTH, SSD_STATE), F32)],
        compiler_params=_params(("parallel", "arbitrary")),
    )(u, proj, proj, dtb, a_neg, d_x, norm_w, ycat)


def _ssd_bwd(dycat, u, proj, y, states, dtb, a_neg, d_x, norm_w, dproj, nbatch, name):
    t = u.shape[0]
    q = CHUNK
    nc = t // nbatch // q

    def body(do_ref, u_ref, dt_ref, z_ref, y_ref, st_ref, dtb_ref, a_ref, dx_ref, nw_ref, _,
             du_ref, dz_ref, ddt_ref, dal_ref, dd_ref, dtbg_ref, dnw_ref, dstate, acc_a, acc_d, acc_b, acc_w):
        b, c = pl.program_id(0), pl.program_id(1)

        @pl.when((b == 0) & (c == 0))
        def _():
            acc_a[...] = jnp.zeros_like(acc_a)
            acc_d[...] = jnp.zeros_like(acc_d)
            acc_b[...] = jnp.zeros_like(acc_b)
            acc_w[...] = jnp.zeros_like(acc_w)

        @pl.when(c == 0)
        def _():
            dstate[...] = jnp.zeros_like(dstate)

        dy_parts = []
        for g in range(SSD_WIDTH // GN_WIDTH):
            cols = slice(g * GN_WIDTH, (g + 1) * GN_WIDTH)
            yv, zv, dov = y_ref[:, cols], z_ref[:, cols], do_ref[:, cols]
            sz = _silu(zv)
            gg = yv * sz
            rstd = lax.rsqrt(jnp.mean(gg * gg, axis=-1, keepdims=True) + EPS)
            gh = gg * rstd
            acc_w[:, cols] += _rowsum8(dov * gh)
            dgn = dov * nw_ref[:, cols]
            dg = rstd * (dgn - gh * jnp.mean(dgn * gh, axis=-1, keepdims=True))
            dy_parts.append(dg * sz)
            dz_ref[:, cols] = (dg * yv * _dsilu(zv)).astype(dz_ref.dtype)

        act, xs, bm, cm, dtp, acs, acs_t, dt_x, ea, dec, s_scale, tril = _ssd_common(
            u_ref, dt_ref, dtb_ref, a_ref)
        xdt = xs * dt_x
        xdec = xdt * dec
        dyv = jnp.concatenate(dy_parts, axis=1)
        dye = dyv * ea
        lo, hi = _half_mask(0), _half_mask(1)
        et = _expand_mat_t()
        dxdt_parts, db_parts, dc_parts, dxst_parts, yoff_parts = [], [], [], [], []
        end_sum = jnp.zeros((LANES, LANES), F32)
        dal_diag = jnp.zeros((q, LANES), F32)
        lane_q = _iota((q, LANES), 1)
        for g in range(2):
            bg = bm[:, g * LANES:(g + 1) * LANES]
            cg = cm[:, g * LANES:(g + 1) * LANES]
            rows = slice(g * GROUP_ROWS, (g + 1) * GROUP_ROWS)
            sg = st_ref[rows, :]
            dsg = dstate[rows, :]
            cb = _dot(cg, bg, NT)
            yoff_parts.append(_dot(cg, sg, NT))
            dcb = jnp.zeros((q, q), F32)
            parts = []
            for j in range(4):
                pj = g * 4 + j
                cols = slice(pj * LANES, (pj + 1) * LANES)
                xp = xdt[:, cols]
                dy0, dy1 = dyv[:, cols] * lo, dyv[:, cols] * hi
                l0 = _decay_mat(acs, acs_t, 2 * pj)
                l1 = _decay_mat(acs, acs_t, 2 * pj + 1)
                g0, g1 = _dot(dy0, xp, NT), _dot(dy1, xp, NT)
                m0, m1 = cb * l0, cb * l1
                dcb = dcb + g0 * l0 + g1 * l1
                parts.append(_dot(m0, dy0, TN) + _dot(m1, dy1, TN))
                for hh, wmat in enumerate((g0 * m0, g1 * m1)):
                    sel = (lane_q == 2 * pj + hh).astype(F32)
                    dal_diag = dal_diag + _dot(wmat, sel) - _dot(wmat, sel, TN)
            dxst = _dot(bg, dsg, NT) * dec[:, rows]
            dxst_parts.append(dxst)
            dxdt_parts.append(jnp.concatenate(parts, axis=1) + dxst)
            dc_parts.append(_dot(dcb, bg) + _dot(dye[:, rows], sg))
            db_parts.append(_dot(dcb, cg, TN) + _dot(xdec[:, rows], dsg))
            s_next = s_scale[rows, :] * sg + _dot(xdec[:, rows], bg, TN)
            end_sum = end_sum + _xdot(dsg * s_next, et[rows, :], TN, passes=2)
            dstate[rows, :] = _dot(dye[:, rows], cg, TN) + s_scale[rows, :] * dsg
        dxdt = jnp.concatenate(dxdt_parts, axis=1)
        dxv = dx_ref[...]
        yoff = jnp.concatenate(yoff_parts, axis=1) * ea
        dalpha = dal_diag + _xdot(dyv * yoff - xdt * jnp.concatenate(dxst_parts, axis=1), et)
        end_row = jnp.sum(end_sum, axis=0, keepdims=True)
        dalpha = dalpha + jnp.where(_iota((q, LANES), 0) == q - 1, end_row, 0.0)
        da = _xdot_r(tril, dalpha, TN)
        ddtp = da * a_ref[...] + _xdot(dxdt * xs, et)
        acc_a[...] += _rowsum8(da * dtp)
        acc_d[...] += _rowsum8(_xdot(dyv * xs, et))
        ddt_raw = ddtp * _sigmoid(dt_ref[...] + dtb_ref[...])
        acc_b[...] += _rowsum8(ddt_raw)
        ddt_ref[...] = ddt_raw
        dxs = dxdt * dt_x + dxv * dyv
        dact = jnp.concatenate([dxs] + db_parts + dc_parts, axis=1)
        du_ref[...] = dact * _dsilu(u_ref[...])

        @pl.when((b == nbatch - 1) & (c == nc - 1))
        def _():
            dal_ref[...] = jnp.sum(acc_a[...], axis=0, keepdims=True) * a_ref[...]
            dd_ref[...] = jnp.sum(acc_d[...], axis=0, keepdims=True)
            dtbg_ref[...] = jnp.sum(acc_b[...], axis=0, keepdims=True)
            dnw_ref[...] = jnp.sum(acc_w[...], axis=0, keepdims=True)

    def rowblk(b, c):
        return b * nc + (nc - 1 - c)

    vec = pl.BlockSpec((1, LANES), lambda b, c: (0, 0))
    wvec = pl.BlockSpec((1, SSD_WIDTH), lambda b, c: (0, 0))
    wide = pl.BlockSpec((q, SSD_WIDTH), lambda b, c: (rowblk(b, c), 0))
    zblk = pl.BlockSpec((q, SSD_WIDTH), lambda b, c: (rowblk(b, c), OFF_ZS // SSD_WIDTH))
    return pl.pallas_call(
        body, name=name,
        out_shape=(jax.ShapeDtypeStruct((t, SSD_CONV_DIM), F32), jax.ShapeDtypeStruct(dproj.shape, dproj.dtype),
                   jax.ShapeDtypeStruct((t, LANES), F32),
                   jax.ShapeDtypeStruct((1, LANES), F32), jax.ShapeDtypeStruct((1, LANES), F32),
                   jax.ShapeDtypeStruct((1, LANES), F32), jax.ShapeDtypeStruct((1, SSD_WIDTH), F32)),
        grid=(nbatch, nc),
        in_specs=[wide,
                  pl.BlockSpec((q, SSD_CONV_DIM), lambda b, c: (rowblk(b, c), 0)),
                  pl.BlockSpec((q, LANES), lambda b, c: (rowblk(b, c), OFF_DT // LANES)),
                  zblk, wide,
                  pl.BlockSpec((SSD_WIDTH, SSD_STATE), lambda b, c: (rowblk(b, c), 0)),
                  vec, vec, wvec, wvec, ANY],
        out_specs=(pl.BlockSpec((q, SSD_CONV_DIM), lambda b, c: (rowblk(b, c), 0)),
                   zblk,
                   pl.BlockSpec((q, LANES), lambda b, c: (rowblk(b, c), 0)),
                   vec, vec, vec, wvec),
        input_output_aliases={10: 1},
        scratch_shapes=[pltpu.VMEM((SSD_WIDTH, SSD_STATE), F32), pltpu.VMEM((SUBLANES, LANES), F32),
                        pltpu.VMEM((SUBLANES, LANES), F32), pltpu.VMEM((SUBLANES, LANES), F32),
                        pltpu.VMEM((SUBLANES, SSD_WIDTH), F32)],
        compiler_params=_params(("arbitrary", "arbitrary")),
    )(dycat, u, proj, proj, y, states, dtb, a_neg, d_x, norm_w, dproj)


def _pad_rows(w, rows):
    return jnp.concatenate([w, jnp.zeros((rows - w.shape[0], w.shape[1]), w.dtype)], axis=0)


def _pad_lanes(v):
    return jnp.concatenate([v, jnp.zeros((LANES - v.shape[0],), v.dtype)]).reshape(1, LANES)


def _padded_from_chips(pieces):
    cols = pieces[0].shape[-1]
    lead = pieces[0].shape[:-1]
    parts, pos = [], 0
    for lo, hi, start in sorted(SECTIONS, key=lambda s: s[2]):
        if start > pos:
            parts.append(jnp.zeros(lead + (start - pos,), pieces[0].dtype))
        pos = start + hi - lo
        while lo < hi:
            p = lo // cols
            end = min(hi, (p + 1) * cols)
            parts.append(pieces[p][..., lo - p * cols:end - p * cols])
            lo = end
    if pos < NP:
        parts.append(jnp.zeros(lead + (NP - pos,), pieces[0].dtype))
    return jnp.concatenate(parts, axis=-1)


def _chip_part_from_padded(wp, p, cols):
    lo, hi = p * cols, (p + 1) * cols
    parts = []
    for rs, re, start in SECTIONS:
        a, b = max(lo, rs), min(hi, re)
        if a < b:
            parts.append(wp[..., start + a - rs:start + b - rs])
    return jnp.concatenate(parts, axis=-1)


def _layer_params(li, w_in_p, w_out, conv_w, dw_w, small):
    return dict(
        w_in_p=w_in_p, w_out=w_out,
        conv_w=_pad_rows(conv_w, SUBLANES), dw_w=_pad_rows(dw_w, 32),
        norm_w=small["norm_w"][li].reshape(1, -1),
        conv_b=small["ssd_conv_b"][li].reshape(1, -1),
        dtb=_pad_lanes(small["ssd_dt_bias"][li]),
        a_neg=_pad_lanes(-jnp.exp(small["ssd_a_log"][li])),
        d_x=jnp.repeat(small["ssd_d"][li], SSD_HEAD_DIM).reshape(1, -1),
        ssd_norm_w=small["ssd_norm_w"][li].reshape(1, -1),
        sinks=_pad_lanes(small["attn_sinks"][li]),
        dw_b=small["conf_dw_b"][li].reshape(1, -1),
        ln_w=small["conf_ln_w"][li].reshape(1, -1),
        ln_b=small["conf_ln_b"][li].reshape(1, -1),
    )


def _layer_fwd(x, p, nbatch, seq, tag, after=None):
    h, h_t = _rmsnorm_fwd(x, p["norm_w"], name=f"rmsnorm_fwd_{tag}", after=after)
    proj = _matmul(h, p["w_in_p"], "nn", F32, 1024, 512, 1024, name=f"proj_fwd_{tag}")
    u = _conv_fwd(proj, OFF_XBC, SSD_CONV_DIM, p["conv_w"], p["conv_b"], SSD_CONV, seq, name=f"ssd_conv_fwd_{tag}")
    ycat = lax.empty((x.shape[0], MIX_WIDTH), MXU_DTYPE)
    y, states, ycat = _ssd_fwd(u, proj, p["dtb"], p["a_neg"], p["d_x"], p["ssd_norm_w"], ycat, nbatch,
                               name=f"ssd_fwd_{tag}")
    ycat, o, lse = _attn_fwd(proj, p["sinks"], ycat, nbatch, name=f"attn_fwd_{tag}")
    c1, ycat = _conf_fwd(proj, p["dw_w"], p["dw_b"], p["ln_w"], p["ln_b"], ycat, seq, name=f"conf_fwd_{tag}")
    x_new = _matmul(ycat, p["w_out"], "nn", F32, 1024, 512, 2048, name=f"out_fwd_{tag}", residual=x)
    return x_new, dict(x=x, h_t=h_t, proj=proj, u=u, y=y, states=states, o=o, lse=lse, c1=c1, ycat=ycat)


def _layer_bwd(dx_out, p, s, nbatch, seq, tag, hooks=None):
    hooks = hooks or {}
    proj = s["proj"]
    dycat = _matmul(dx_out, p["w_out"], "nt", F32, 1024, 1024, 1024, name=f"out_bwd_dy_{tag}",
                    after=hooks.get("start_token"))
    dw_out = _matmul(s["ycat"], dx_out, "tn", F32, 1024, 1024, 1024, name=f"out_bwd_dw_{tag}")
    token = hooks["after_dycat"](dycat) if "after_dycat" in hooks else None
    dtb = p["dtb"] if token is None else p["dtb"] + token[0, 0]
    dproj = lax.empty(proj.shape, MXU_DTYPE)
    du, dproj, ddt, da_log, dd, ddtb, dssd_norm_w = _ssd_bwd(
        dycat, s["u"], proj, s["y"], s["states"], dtb, p["a_neg"], p["d_x"], p["ssd_norm_w"], dproj,
        nbatch, name=f"ssd_bwd_{tag}")
    dproj, dconv_w, dconv_b = _conv_bwd(du, proj, OFF_XBC, SSD_CONV_DIM, p["conv_w"], SSD_CONV, seq,
                                        name=f"ssd_conv_bwd_{tag}", into=dproj)
    dproj, dsinks = _attn_bwd(dycat, proj, s["o"], s["lse"], p["sinks"], ddt, dproj, nbatch,
                              name=f"attn_bwd_{tag}")
    if "after_attn" in hooks:
        hooks["after_attn"](dproj)
    dc1, dproj, dln_w, dln_b = _conf_post_bwd(dycat, s["c1"], proj, p["ln_w"], p["ln_b"], dproj,
                                              name=f"conf_post_bwd_{tag}")
    dproj, ddw_w, ddw_b = _conf_conv_bwd(dc1, proj, p["dw_w"], dproj, seq, name=f"conf_conv_bwd_{tag}")
    dh = _matmul(dproj, p["w_in_p"], "nt", F32, 1024, 1024, 1408, name=f"proj_bwd_dh_{tag}")
    dw_in_p = _matmul(s["h_t"], dproj, "nn", F32, 1024, 512, 4096, name=f"proj_bwd_dw_{tag}")
    dx_in, dnorm_w = _rmsnorm_bwd(dh, s["x"], p["norm_w"], dx_out, name=f"rmsnorm_bwd_{tag}")
    grads = dict(
        norm_w=dnorm_w[0], w_in_p=dw_in_p, ssd_conv_w=dconv_w[:SSD_CONV], ssd_conv_b=dconv_b[0],
        ssd_dt_bias=ddtb[0, :SSD_HEADS], ssd_a_log=da_log[0, :SSD_HEADS], ssd_d=dd[0, :SSD_HEADS],
        ssd_norm_w=dssd_norm_w[0], attn_sinks=dsinks[0, :ATTN_Q_HEADS], conf_dw_w=ddw_w[:CONF_KERNEL],
        conf_dw_b=ddw_b[0], conf_ln_w=dln_w[0], conf_ln_b=dln_b[0], w_out=dw_out)
    return dx_in, grads


def _local_step(x, target, param_fns, final_norm_w, first_after=None, on_grads=None):
    nbatch, seq, d = x.shape
    xt = x.reshape(nbatch * seq, d)
    saved, layer_params = [], []
    for li, fn in enumerate(param_fns):
        p = fn(xt)
        layer_params.append(p)
        xt, s = _layer_fwd(xt, p, nbatch, seq, f"l{li}", after=first_after if li == 0 else None)
        saved.append(s)
    loss, dx, dfinal = _loss_head(xt, target.reshape(nbatch * seq, d), final_norm_w.reshape(1, d), name="loss_head")
    grads = [None] * len(layer_params)
    hooks = None
    for li in reversed(range(len(layer_params))):
        dx, grads[li] = _layer_bwd(dx, layer_params[li], saved[li], nbatch, seq, f"l{li}", hooks=hooks)
        hooks = on_grads(li, grads[li]) if on_grads is not None else None
    return loss[0, 0], dx.reshape(nbatch, seq, d), grads, dfinal[0]


MESH = pl.DeviceIdType.MESH
N_CHIPS = 4


def _mesh_pos():
    return lax.axis_index("x"), lax.axis_index("y"), lax.axis_index("c")


def _other_chips(x, y):
    return [(1 - x, y), (x, 1 - y), (1 - x, 1 - y)]


def _gather_weights(big, small, name):
    nbig, nsmall = len(big), len(small)
    n_ici = 3 * (nbig + nsmall)
    n_fwd = 3 * nbig

    def body(*refs):
        ins = refs[:nbig + nsmall]
        outs = refs[nbig + nsmall:2 * (nbig + nsmall)]
        send_sems, recv_sems = refs[2 * (nbig + nsmall):]
        x, y, c = _mesh_pos()
        me = 2 * x + y
        sibling = (x, y, 1 - c)
        chips = _other_chips(x, y)

        def ici(a, j, origin, dest):
            if a < nbig:
                src = ins[a].at[c] if origin is None else outs[a].at[origin, c]
                dst = outs[a].at[me if origin is None else origin, c]
            else:
                src = ins[a] if origin is None else outs[a].at[origin]
                dst = outs[a].at[me if origin is None else origin]
            k = a * 3 + j
            return pltpu.make_async_remote_copy(src_ref=src, dst_ref=dst, send_sem=send_sems.at[k],
                                                recv_sem=recv_sems.at[k], device_id=dest, device_id_type=MESH)

        def fwd(a, j, origin, half):
            k = n_ici + a * 3 + j
            ref = outs[a].at[origin, half]
            return pltpu.make_async_remote_copy(src_ref=ref, dst_ref=ref, send_sem=send_sems.at[k],
                                                recv_sem=recv_sems.at[k], device_id=sibling, device_id_type=MESH)

        sends = []
        for j, (px, py) in enumerate(chips):
            for a in range(nbig + nsmall):
                cp = ici(a, j, None, (px, py, c))
                cp.start()
                sends.append(cp)
        for j, (px, py) in enumerate(chips):
            origin = 2 * px + py
            for a in range(nbig):
                ici(a, j, origin, (px, py, c)).wait_recv()
                cp = fwd(a, j, origin, c)
                cp.start()
                sends.append(cp)
        for j, (px, py) in enumerate(chips):
            origin = 2 * px + py
            for a in range(nbig, nbig + nsmall):
                ici(a, j, origin, (px, py, c)).wait_recv()
            for a in range(nbig):
                fwd(a, j, origin, 1 - c).wait_recv()
        for cp in sends:
            cp.wait_send()

    out_shape = tuple(jax.ShapeDtypeStruct((N_CHIPS,) + a.shape, a.dtype) for a in list(big) + list(small))
    return pl.pallas_call(
        body, name=name, out_shape=out_shape,
        in_specs=[ANY] * (nbig + nsmall), out_specs=tuple([ANY] * (nbig + nsmall)),
        scratch_shapes=[pltpu.SemaphoreType.DMA((n_ici + n_fwd,)), pltpu.SemaphoreType.DMA((n_ici + n_fwd,))],
    )(*big, *small)


def _pair_swap_halves(arrs, name):
    n = len(arrs)

    def body(*refs):
        ins, outs = refs[:n], refs[n:2 * n]
        send_sems, recv_sems = refs[2 * n:]
        x, y, c = _mesh_pos()
        cps = [pltpu.make_async_remote_copy(src_ref=ins[a].at[:, 1 - c], dst_ref=outs[a], send_sem=send_sems.at[a],
                                            recv_sem=recv_sems.at[a], device_id=(x, y, 1 - c), device_id_type=MESH)
               for a in range(n)]
        for cp in cps:
            cp.start()
        for cp in cps:
            cp.wait()

    return pl.pallas_call(
        body, name=name,
        out_shape=tuple(jax.ShapeDtypeStruct(a.shape[:1] + a.shape[2:], a.dtype) for a in arrs),
        in_specs=[ANY] * n, out_specs=tuple([ANY] * n),
        scratch_shapes=[pltpu.SemaphoreType.DMA((n,)), pltpu.SemaphoreType.DMA((n,))],
    )(*arrs)


HBM = pl.BlockSpec(memory_space=pltpu.HBM)
SEM = pl.BlockSpec(memory_space=pltpu.SEMAPHORE)
DATAFLOW = pltpu.SideEffectType.DATAFLOW_SIDE_EFFECTING


def _split_peers(pattern, x, y, c):
    if pattern == "swap":
        return [((x, y, 1 - c), 1 - c, None, None)]
    me = 2 * x + y
    return [((px, py, c), 2 * px + py if pattern == "scatter" else None, me, 2 * px + py)
            for px, py in _other_chips(x, y)]


def _split_land_shape(pattern, shape):
    return {"bcast": (N_CHIPS,) + shape, "scatter": shape, "swap": shape[:1] + shape[2:]}[pattern]


def _split_copies(pattern, srcs, lands, send_sems, recv_sems, waiting):
    x, y, c = _mesh_pos()
    peers = _split_peers(pattern, x, y, c)
    cps = []
    for j, (dev, src_slot, dst_slot, my_slot) in enumerate(peers):
        for a in range(len(srcs)):
            if src_slot is None:
                src = srcs[a]
            else:
                src = srcs[a].at[:, src_slot] if pattern == "swap" else srcs[a].at[src_slot]
            slot = my_slot if waiting else dst_slot
            dst = lands[a] if slot is None else lands[a].at[slot]
            k = a * len(peers) + j
            cps.append(pltpu.make_async_remote_copy(src_ref=src, dst_ref=dst, send_sem=send_sems[k],
                                                    recv_sem=recv_sems[k], device_id=dev, device_id_type=MESH))
    return cps


def _split_start(arrs, pattern, after, name):
    n = len(arrs)
    nsem = n * (1 if pattern == "swap" else N_CHIPS - 1)

    def body(*refs):
        srcs, lands = refs[:n], refs[n:2 * n]
        outs = refs[2 * n + 1:]
        for cp in _split_copies(pattern, srcs, lands, outs[:nsem], outs[nsem:2 * nsem], waiting=False):
            cp.start()
        outs[-1][...] = jnp.zeros_like(outs[-1])

    lands = [lax.empty(_split_land_shape(pattern, a.shape), a.dtype) for a in arrs]
    out_shape = ([pltpu.SemaphoreType.DMA(())] * (2 * nsem)
                 + [pltpu.HBM(a.shape, a.dtype) for a in arrs] + [pltpu.HBM(b.shape, b.dtype) for b in lands]
                 + [jax.ShapeDtypeStruct((SUBLANES, LANES), F32)])
    outs = pl.pallas_call(
        body, name=name, out_shape=tuple(out_shape),
        in_specs=[HBM] * (2 * n) + [ANY],
        out_specs=tuple([SEM] * (2 * nsem) + [HBM] * (2 * n) + [pl.BlockSpec(memory_space=pltpu.VMEM)]),
        input_output_aliases={a: 2 * nsem + a for a in range(2 * n)},
        compiler_params=pltpu.CompilerParams(has_side_effects=DATAFLOW),
    )(*[pltpu.with_memory_space_constraint(a, pltpu.HBM) for a in list(arrs) + lands], after)
    return outs[:-1], outs[-1]


def _split_wait(state, n, pattern, after, name):
    nsem = n * (1 if pattern == "swap" else N_CHIPS - 1)

    def body(*refs):
        srcs, lands = refs[:n], refs[n:2 * n]
        send_sems, recv_sems = refs[2 * n:2 * n + nsem], refs[2 * n + nsem:2 * n + 2 * nsem]
        for cp in _split_copies(pattern, srcs, lands, send_sems, recv_sems, waiting=True):
            cp.wait_send()
            cp.wait_recv()

    sems, thru = state[:2 * nsem], state[2 * nsem:]
    outs = pl.pallas_call(
        body, name=name, out_shape=tuple(pltpu.HBM(a.shape, a.dtype) for a in thru),
        in_specs=[HBM] * (2 * n) + [SEM] * (2 * nsem) + [ANY],
        out_specs=tuple([HBM] * (2 * n)),
        input_output_aliases={a: a for a in range(2 * n)},
        compiler_params=pltpu.CompilerParams(has_side_effects=DATAFLOW),
    )(*thru, *sems, after)
    return outs[n:]


def _pair_gather(arrs, layer, name):
    n = len(arrs)

    def body(*refs):
        outs = refs[n:2 * n]
        send_sems, recv_sems = refs[2 * n:]
        x, y, c = _mesh_pos()
        cps = [pltpu.make_async_remote_copy(src_ref=outs[a].at[layer, c], dst_ref=outs[a].at[layer, c],
                                            send_sem=send_sems.at[a], recv_sem=recv_sems.at[a],
                                            device_id=(x, y, 1 - c), device_id_type=MESH)
               for a in range(n)]
        for cp in cps:
            cp.start()
        for cp in cps:
            cp.wait()

    return pl.pallas_call(
        body, name=name, out_shape=tuple(jax.ShapeDtypeStruct(a.shape, a.dtype) for a in arrs),
        in_specs=[ANY] * n, out_specs=tuple([ANY] * n),
        input_output_aliases={a: a for a in range(n)},
        scratch_shapes=[pltpu.SemaphoreType.DMA((n,)), pltpu.SemaphoreType.DMA((n,))],
    )(*arrs)


N_DEV = 8


def _allreduce_small(pack, name):
    r = pack.shape[0]

    def body(p_ref, o_ref, land, send_sems, recv_sems):
        x, y, c = _mesh_pos()
        me = 4 * x + 2 * y + c
        cps = []
        for k in range(1, N_DEV):
            peer = (x ^ (k >> 2), y ^ ((k >> 1) & 1), c ^ (k & 1))
            cps.append(pltpu.make_async_remote_copy(src_ref=p_ref, dst_ref=land.at[me], send_sem=send_sems.at[k - 1],
                                                    recv_sem=recv_sems.at[k - 1], device_id=peer, device_id_type=MESH))
        for cp in cps:
            cp.start()
        land[me] = p_ref[...]
        for cp in cps:
            cp.wait()
        total = land[0]
        for d in range(1, N_DEV):
            total = total + land[d]
        o_ref[...] = total

    vm = pl.BlockSpec(memory_space=pltpu.VMEM)
    return pl.pallas_call(
        body, name=name, out_shape=jax.ShapeDtypeStruct(pack.shape, F32),
        in_specs=[vm], out_specs=vm,
        scratch_shapes=[pltpu.VMEM((N_DEV, r, LANES), F32), pltpu.SemaphoreType.DMA((N_DEV - 1,)),
                        pltpu.SemaphoreType.DMA((N_DEV - 1,))],
    )(pack)


BIG_ROWS = 128


def _cast_layer(w, layer, name):
    _, r, cdim = w.shape
    tr = BIG_ROWS

    def body(w_ref, o_ref):
        o_ref[...] = w_ref[...].astype(o_ref.dtype)

    return pl.pallas_call(
        body, name=name, out_shape=jax.ShapeDtypeStruct((r, cdim), MXU_DTYPE),
        grid=(r // tr,), in_specs=[pl.BlockSpec((None, tr, cdim), lambda i: (layer, i, 0))],
        out_specs=pl.BlockSpec((tr, cdim), lambda i: (i, 0)),
        compiler_params=_params(("parallel",)),
    )(w)


def _pair_sum(parts, sib, which, out_dtype, name):
    k, _, r, cdim = parts.shape
    tr = BIG_ROWS

    def body(sel_ref, p_ref, s_ref, o_ref):
        o_ref[...] = (p_ref[...] + s_ref[...]).astype(o_ref.dtype)

    grid_spec = pltpu.PrefetchScalarGridSpec(
        num_scalar_prefetch=1, grid=(k, r // tr),
        in_specs=[pl.BlockSpec((None, None, tr, cdim), lambda l, i, sel: (l, sel[0], i, 0)),
                  pl.BlockSpec((None, tr, cdim), lambda l, i, sel: (l, i, 0))],
        out_specs=pl.BlockSpec((None, tr, cdim), lambda l, i, sel: (l, i, 0)))
    return pl.pallas_call(
        body, name=name, out_shape=jax.ShapeDtypeStruct((k, r, cdim), out_dtype), grid_spec=grid_spec,
        compiler_params=_params(("parallel", "parallel")),
    )(which.reshape(1).astype(jnp.int32), parts, sib)


def _sum_lead(parts, into, layer, which, name):
    k, r, cdim = parts.shape
    tr = BIG_ROWS

    def body(sel_ref, p_ref, _, o_ref):
        total = p_ref[0].astype(F32)
        for a in range(1, k):
            total = total + p_ref[a].astype(F32)
        o_ref[...] = total

    grid_spec = pltpu.PrefetchScalarGridSpec(
        num_scalar_prefetch=1, grid=(r // tr,),
        in_specs=[pl.BlockSpec((k, tr, cdim), lambda i, sel: (0, i, 0)), ANY],
        out_specs=pl.BlockSpec((None, None, tr, cdim), lambda i, sel: (layer, sel[0], i, 0)))
    return pl.pallas_call(
        body, name=name, out_shape=jax.ShapeDtypeStruct(into.shape, F32), grid_spec=grid_spec,
        input_output_aliases={2: 0},
        compiler_params=_params(("parallel",)),
    )(which.reshape(1).astype(jnp.int32), parts, into)


def _adam_math(w, g, m, v):
    m2 = ADAM_B1 * m + (1.0 - ADAM_B1) * g
    v2 = ADAM_B2 * v + (1.0 - ADAM_B2) * (g * g)
    m_hat = m2 / (1.0 - ADAM_B1 ** ADAM_STEP)
    v_hat = v2 / (1.0 - ADAM_B2 ** ADAM_STEP)
    delta = -ADAM_LR * (m_hat / (jnp.sqrt(v_hat) + ADAM_EPS) + ADAM_WD * w)
    return delta, m2, v2


def _adam_big(w, g, m, v, name):
    nl, r, cdim = w.shape
    tr = BIG_ROWS

    def body(w_ref, g_ref, m_ref, v_ref, d_ref, mo_ref, vo_ref):
        delta, m2, v2 = _adam_math(w_ref[...], g_ref[...], m_ref[...], v_ref[...])
        d_ref[...] = delta
        mo_ref[...] = m2
        vo_ref[...] = v2

    blk = pl.BlockSpec((None, tr, cdim), lambda l, i: (l, i, 0))
    shp = jax.ShapeDtypeStruct(w.shape, F32)
    return pl.pallas_call(
        body, name=name, out_shape=(shp, shp, shp),
        grid=(nl, r // tr), in_specs=[blk] * 4, out_specs=(blk, blk, blk),
        compiler_params=_params(("parallel", "parallel")),
    )(w, g, m, v)


def _adam_cols_major(w, g, m, v, name):
    cdim, nl, r = w.shape
    tc = BIG_ROWS

    def body(w_ref, g_ref, m_ref, v_ref, d_ref, mo_ref, vo_ref):
        delta, m2, v2 = _adam_math(w_ref[...], g_ref[...], m_ref[...], v_ref[...])
        d_ref[...] = delta
        mo_ref[...] = m2
        vo_ref[...] = v2

    blk = pl.BlockSpec((tc, nl, r), lambda i: (i, 0, 0))
    shp = jax.ShapeDtypeStruct(w.shape, F32)
    return pl.pallas_call(
        body, name=name, out_shape=(shp, shp, shp),
        grid=(pl.cdiv(cdim, tc),), in_specs=[blk] * 4, out_specs=(blk, blk, blk),
        compiler_params=_params(("parallel",)),
    )(w, g, m, v)


def _adam_small(ws, gs, ms, vs, name):
    n = len(ws)

    def body(*refs):
        w_refs, g_refs, m_refs, v_refs = (refs[k * n:(k + 1) * n] for k in range(4))
        d_refs, mo_refs, vo_refs = (refs[(4 + k) * n:(5 + k) * n] for k in range(3))
        for a in range(n):
            delta, m2, v2 = _adam_math(w_refs[a][...], g_refs[a][...], m_refs[a][...], v_refs[a][...])
            d_refs[a][...] = delta
            mo_refs[a][...] = m2
            vo_refs[a][...] = v2

    shapes = tuple(jax.ShapeDtypeStruct(w.shape, F32) for w in ws)
    vm = pl.BlockSpec(memory_space=pltpu.VMEM)
    outs = pl.pallas_call(body, name=name, out_shape=shapes * 3, in_specs=[vm] * (4 * n),
                          out_specs=tuple([vm] * (3 * n)))(*ws, *gs, *ms, *vs)
    return outs[:n], outs[n:2 * n], outs[2 * n:]


PACK_TILE = SUBLANES * LANES


def _pack(arrays):
    rows = []
    for a in arrays:
        flat = a.reshape(-1)
        pad = (-flat.shape[0]) % PACK_TILE
        if pad:
            flat = jnp.concatenate([flat, jnp.zeros((pad,), flat.dtype)])
        rows.append(flat.reshape(-1, LANES))
    return jnp.concatenate(rows, axis=0)


def _unpack(pack, shapes):
    outs, row = [], 0
    for shp in shapes:
        n = int(np.prod(shp))
        nrows = -(-n // PACK_TILE) * SUBLANES
        outs.append(pack[row:row + nrows].reshape(-1)[:n].reshape(shp))
        row += nrows
    return outs


SMALL = ["norm_w", "ssd_conv_b", "ssd_dt_bias", "ssd_a_log", "ssd_d", "ssd_norm_w", "attn_sinks",
         "conf_dw_b", "conf_ln_w", "conf_ln_b"]
WEIGHTS = ["norm_w", "w_in", "ssd_conv_w", "ssd_conv_b", "ssd_dt_bias", "ssd_a_log", "ssd_d", "ssd_norm_w",
           "attn_sinks", "conf_dw_w", "conf_dw_b", "conf_ln_w", "conf_ln_b", "w_out", "final_norm_w"]


def kernel(x, norm_w, w_in, ssd_conv_w, ssd_conv_b, ssd_dt_bias, ssd_a_log, ssd_d, ssd_norm_w, attn_sinks, conf_dw_w, conf_dw_b, conf_ln_w, conf_ln_b, w_out, final_norm_w, loss_target, m_norm_w, m_w_in, m_ssd_conv_w, m_ssd_conv_b, m_ssd_dt_bias, m_ssd_a_log, m_ssd_d, m_ssd_norm_w, m_attn_sinks, m_conf_dw_w, m_conf_dw_b, m_conf_ln_w, m_conf_ln_b, m_w_out, m_final_norm_w, v_norm_w, v_w_in, v_ssd_conv_w, v_ssd_conv_b, v_ssd_dt_bias, v_ssd_a_log, v_ssd_d, v_ssd_norm_w, v_attn_sinks, v_conf_dw_w, v_conf_dw_b, v_conf_ln_w, v_conf_ln_b, v_w_out, v_final_norm_w):
    w = dict(norm_w=norm_w, w_in=w_in, ssd_conv_w=ssd_conv_w, ssd_conv_b=ssd_conv_b, ssd_dt_bias=ssd_dt_bias,
             ssd_a_log=ssd_a_log, ssd_d=ssd_d, ssd_norm_w=ssd_norm_w, attn_sinks=attn_sinks, conf_dw_w=conf_dw_w,
             conf_dw_b=conf_dw_b, conf_ln_w=conf_ln_w, conf_ln_b=conf_ln_b, w_out=w_out, final_norm_w=final_norm_w)
    m = dict(norm_w=m_norm_w, w_in=m_w_in, ssd_conv_w=m_ssd_conv_w, ssd_conv_b=m_ssd_conv_b,
             ssd_dt_bias=m_ssd_dt_bias, ssd_a_log=m_ssd_a_log, ssd_d=m_ssd_d, ssd_norm_w=m_ssd_norm_w,
             attn_sinks=m_attn_sinks, conf_dw_w=m_conf_dw_w, conf_dw_b=m_conf_dw_b, conf_ln_w=m_conf_ln_w,
             conf_ln_b=m_conf_ln_b, w_out=m_w_out, final_norm_w=m_final_norm_w)
    v = dict(norm_w=v_norm_w, w_in=v_w_in, ssd_conv_w=v_ssd_conv_w, ssd_conv_b=v_ssd_conv_b,
             ssd_dt_bias=v_ssd_dt_bias, ssd_a_log=v_ssd_a_log, ssd_d=v_ssd_d, ssd_norm_w=v_ssd_norm_w,
             attn_sinks=v_attn_sinks, conf_dw_w=v_conf_dw_w, conf_dw_b=v_conf_dw_b, conf_ln_w=v_conf_ln_w,
             conf_ln_b=v_conf_ln_b, w_out=v_w_out, final_norm_w=v_final_norm_w)
    depth = w_in.shape[0]
    me = 2 * lax.axis_index("x") + lax.axis_index("y")

    assert depth == 2
    w_in_b = [_cast_layer(w_in, li, name=f"cast_w_in_l{li}") for li in range(depth)]
    w_out_b = [_cast_layer(w_out, li, name=f"cast_w_out_l{li}") for li in range(depth)]
    own0 = [w_in_b[0].reshape((2, -1) + w_in_b[0].shape[1:]), w_out_b[0].reshape((2, -1) + w_out_b[0].shape[1:]),
            ssd_conv_w, conf_dw_w]
    gathered0 = _gather_weights(own0[:2], own0[2:], name="gather_weights_l0")
    g_in0, g_out0, g_conv, g_dw = [lax.dynamic_update_index_in_dim(g_all, mine, me, 0)
                                   for g_all, mine in zip(gathered0, own0)]
    own1 = [w_in_b[1], w_out_b[1]]
    pending1, token1 = _split_start(own1, "bcast", gathered0[0], name="gather_l1_start")

    def small_full(li):
        return (jnp.concatenate([g_conv[p, li] for p in range(N_CHIPS)], axis=1),
                jnp.concatenate([g_dw[p, li] for p in range(N_CHIPS)], axis=1))

    def params_l0(_):
        w_in_p = _padded_from_chips([g_in0[p].reshape(w_in_b[0].shape) for p in range(N_CHIPS)])
        w_out_full = g_out0.reshape(-1, g_out0.shape[-1])
        return _layer_params(0, w_in_p, w_out_full, *small_full(0), w)

    def params_l1(layer_input):
        landed = _split_wait(pending1, len(own1), "bcast", layer_input, name="gather_l1_wait")
        g_in1, g_out1 = [lax.dynamic_update_index_in_dim(g_all, mine, me, 0) for g_all, mine in zip(landed, own1)]
        w_in_p = _padded_from_chips([g_in1[p] for p in range(N_CHIPS)])
        return _layer_params(1, w_in_p, g_out1.reshape(-1, g_out1.shape[-1]), *small_full(1), w)

    c = lax.axis_index("c")
    cols = w_in.shape[2]
    rows_out = w_out.shape[1]

    def grad_parts(g):
        dw = g["w_in_p"]
        p_in = jnp.stack([_chip_part_from_padded(dw, p, cols) for p in range(N_CHIPS)])
        return [p_in.reshape(N_CHIPS, 2, dw.shape[0] // 2, cols),
                g["w_out"].reshape(N_CHIPS, 2, rows_out // 2, D_MODEL)]

    def pair_sums(parts, sib, tag):
        return [_pair_sum(p, sb, c, MXU_DTYPE, name=f"grad_pair_sum_{k}_{tag}")
                for k, (p, sb) in enumerate(zip(parts, sib))]

    split = {"reduced": [lax.empty((depth, 2, w_in.shape[1] // 2, cols), F32),
                         lax.empty((depth, 2, rows_out // 2, D_MODEL), F32)]}

    def chip_sums(landed, sent, li):
        filled = [lax.dynamic_update_index_in_dim(r, lax.dynamic_index_in_dim(sk, me, 0, keepdims=False), me, 0)
                  for r, sk in zip(landed, sent)]
        halves = [_sum_lead(r, into, li, c, name=f"grad_chip_sum_{k}_l{li}")
                  for k, (r, into) in enumerate(zip(filled, split["reduced"]))]
        split["reduced"] = list(_pair_gather(halves, li, name=f"grad_pair_gather_l{li}"))

    def on_grads(li, g):
        if li != depth - 1:
            return None
        parts = grad_parts(g)
        swap_state, swap_token = _split_start(parts, "swap", g["w_out"], name="grad_swap_l1_start")

        def after_dycat(dycat):
            sib = _split_wait(swap_state, len(parts), "swap", dycat, name="grad_swap_l1_wait")
            split["sent"] = pair_sums(parts, sib, "l1")
            split["scatter"], token = _split_start(split["sent"], "scatter", split["sent"][0],
                                                   name="grad_scatter_l1_start")
            return token

        def after_attn(dproj):
            landed = _split_wait(split["scatter"], len(parts), "scatter", dproj, name="grad_scatter_l1_wait")
            chip_sums(landed, split["sent"], depth - 1)

        return {"start_token": swap_token, "after_dycat": after_dycat, "after_attn": after_attn}

    loss, grad_x, grads, dfinal = _local_step(x, loss_target, [params_l0, params_l1], final_norm_w,
                                              first_after=token1, on_grads=on_grads)

    parts0 = grad_parts(grads[0])
    sent0 = pair_sums(parts0, _pair_swap_halves(parts0, name="grad_pair_swap_l0"), "l0")
    scatter0, token0 = _split_start(sent0, "scatter", sent0[0], name="grad_scatter_l0_start")

    small_list = [grads[li][n] for li in range(depth) for n in SMALL]
    small_list += [grads[li][n] for li in range(depth) for n in ("ssd_conv_w", "conf_dw_w")]
    small_list += [dfinal, loss.reshape(1)]
    small_shapes = [a.shape for a in small_list]
    reduced = _unpack(_allreduce_small(_pack(small_list) + token0[0, 0], name="allreduce_small"), small_shapes)
    ns = len(SMALL)
    g = {n: jnp.stack([reduced[li * ns + i] for li in range(depth)]) for i, n in enumerate(SMALL)}
    conv_w_cols, dw_w_cols = ssd_conv_w.shape[2], conf_dw_w.shape[2]
    g["ssd_conv_w"] = jnp.stack([lax.dynamic_slice_in_dim(reduced[depth * ns + 2 * li], me * conv_w_cols,
                                                          conv_w_cols, axis=1) for li in range(depth)])
    g["conf_dw_w"] = jnp.stack([lax.dynamic_slice_in_dim(reduced[depth * ns + 2 * li + 1], me * dw_w_cols,
                                                         dw_w_cols, axis=1) for li in range(depth)])
    g["final_norm_w"] = reduced[-2]
    loss_total = reduced[-1][0]

    small_names = [n for n in WEIGHTS if n not in ("w_in", "w_out")]

    def as2d(a):
        return a.reshape(1, -1) if a.ndim == 1 else a

    deltas, new_ms, new_vs = _adam_small(*[[as2d(src[n]) for n in small_names] for src in (w, g, m, v)],
                                         name="adam_small")

    chip_sums(_split_wait(scatter0, len(sent0), "scatter", deltas[0], name="grad_scatter_l0_wait"), sent0, 0)
    g_w_in = split["reduced"][0].reshape(w_in.shape)
    g_w_out = split["reduced"][1].reshape(w_out.shape)

    outs_g, outs_d, outs_m, outs_v = {"w_in": g_w_in, "w_out": g_w_out}, {}, {}, {}
    to_cols, from_cols = (2, 0, 1), (1, 2, 0)
    outs_d["w_in"], outs_m["w_in"], outs_v["w_in"] = [
        jnp.transpose(a, from_cols) for a in _adam_cols_major(
            *[jnp.transpose(a, to_cols) for a in (w_in, g_w_in, m_w_in, v_w_in)], name="adam_w_in")]
    outs_d["w_out"], outs_m["w_out"], outs_v["w_out"] = _adam_big(w_out, g_w_out, m_w_out, v_w_out,
                                                                  name="adam_w_out")
    for n, dn, mn, vn in zip(small_names, deltas, new_ms, new_vs):
        outs_g[n], outs_d[n], outs_m[n], outs_v[n] = (g[n], dn.reshape(w[n].shape), mn.reshape(w[n].shape),
                                                      vn.reshape(w[n].shape))
    return (loss_total, grad_x, *[outs_g[n] for n in WEIGHTS], *[outs_d[n] for n in WEIGHTS],
            *[outs_m[n] for n in WEIGHTS], *[outs_v[n] for n in WEIGHTS])
```

```python
import functools
import math

import jax
import jax.numpy as jnp
import numpy as np
from jax import lax
from jax.experimental import pallas as pl
from jax.experimental.pallas import tpu as pltpu

F32 = jnp.float32
BF16 = jnp.bfloat16
MXU_DTYPE = BF16

D_MODEL = 1024
DEPTH = 2
SSD_HEADS = 16
SSD_HEAD_DIM = 64
SSD_STATE = 128
SSD_CONV = 4
CHUNK = 128
SSD_CONV_DIM = 1536
ATTN_HEAD_DIM = 64
ATTN_Q_HEADS = 8
WINDOW = 128
CONF_WIDTH = 512
CONF_KERNEL = 31
MIX_WIDTH = 2048
D_IN_PROJ = 5392
EPS = 1e-5

ADAM_LR = 0.001
ADAM_B1 = 0.9
ADAM_B2 = 0.999
ADAM_EPS = 1e-08
ADAM_WD = 0.01
ADAM_STEP = 10

LANES = 128
SUBLANES = 8
VMEM_LIMIT = 48 * 1024 * 1024

NP = 5632
OFF_ZA, OFF_Q, OFF_K, OFF_V, OFF_DT = 0, 512, 1024, 1152, 1280
ATTN_GROUP = 1536
OFF_XBC = 1536
OFF_CONF = 3072
OFF_ZS = 4096
OFF_ZC = 5120
SECTIONS = ((0, 1024, OFF_ZS), (1024, 1536, OFF_ZA), (1536, 2048, OFF_ZC), (2048, 3584, OFF_XBC),
            (3584, 3600, OFF_DT), (3600, 4368, OFF_Q), (4368, 5392, OFF_CONF))

YCAT_ATTN, YCAT_CONF = 1024, 1536
ANY = pl.BlockSpec(memory_space=pl.ANY)

NN = (((1,), (0,)), ((), ()))
NT = (((1,), (1,)), ((), ()))
TN = (((0,), (0,)), ((), ()))


def _params(sem):
    return pltpu.CompilerParams(dimension_semantics=sem, vmem_limit_bytes=VMEM_LIMIT)


def _dot(a, b, dims=NN):
    return lax.dot_general(a.astype(MXU_DTYPE), b.astype(MXU_DTYPE), dims, preferred_element_type=F32)


def _split_bf16(a, passes):
    pieces = []
    r = a
    for _ in range(passes):
        p = r.astype(BF16)
        pieces.append(p)
        r = r - p.astype(F32)
    return pieces


def _xdot(a, sel, dims=NN, passes=2):
    out = None
    for p in _split_bf16(a, passes):
        t = lax.dot_general(p, sel, dims, preferred_element_type=F32)
        out = t if out is None else out + t
    return out


def _xdot_r(sel, b, dims=NN, passes=3):
    out = None
    for p in _split_bf16(b, passes):
        t = lax.dot_general(sel, p, dims, preferred_element_type=F32)
        out = t if out is None else out + t
    return out


def _sigmoid(x):
    return 1.0 / (1.0 + jnp.exp(-x))


def _silu(x):
    return x * _sigmoid(x)


def _dsilu(x):
    s = _sigmoid(x)
    return s * (1.0 + x * (1.0 - s))


def _softplus(x):
    return jnp.maximum(x, 0.0) + jnp.log(1.0 + jnp.exp(-jnp.abs(x)))


def _rowsum8(x):
    r, c = x.shape
    return jnp.sum(x.reshape(r // SUBLANES, SUBLANES, c), axis=0)


def _iota(shape, dim):
    return lax.broadcasted_iota(jnp.int32, shape, dim)


def _matmul(a, b, form, out_dtype, tm, tn, tk, name, residual=None, after=None):
    if form == "nn":
        (m, k), n = a.shape, b.shape[1]
    elif form == "nt":
        (m, k), n = a.shape, b.shape[0]
    else:
        (k, m), n = a.shape, b.shape[1]
    tm, tn, tk = min(tm, m), min(tn, n), min(tk, k)
    assert m % tm == 0 and n % tn == 0 and k % tk == 0, (name, m, n, k, tm, tn, tk)
    if form == "nn":
        a_spec = pl.BlockSpec((tm, tk), lambda i, j, s: (i, s))
        b_spec = pl.BlockSpec((tk, tn), lambda i, j, s: (s, j))
        dims = NN
    elif form == "nt":
        (m, k), n = a.shape, b.shape[0]
        a_spec = pl.BlockSpec((tm, tk), lambda i, j, s: (i, s))
        b_spec = pl.BlockSpec((tn, tk), lambda i, j, s: (j, s))
        dims = NT
    else:
        (k, m), n = a.shape, b.shape[1]
        a_spec = pl.BlockSpec((tk, tm), lambda i, j, s: (s, i))
        b_spec = pl.BlockSpec((tk, tn), lambda i, j, s: (s, j))
        dims = TN
    nk = k // tk
    has_res = residual is not None
    deps = [] if after is None else [after]

    def body_single(a_ref, b_ref, *rest):
        o = _dot(a_ref[...], b_ref[...], dims)
        if has_res:
            o = o + rest[0][...]
        rest[-1][...] = o.astype(out_dtype)

    def body(a_ref, b_ref, *rest):
        r_ref = rest[0] if has_res else None
        o_ref, acc = rest[-2:]
        s = pl.program_id(2)

        @pl.when(s == 0)
        def _():
            acc[...] = jnp.zeros_like(acc)

        acc[...] += _dot(a_ref[...], b_ref[...], dims)

        @pl.when(s == nk - 1)
        def _():
            o = acc[...]
            if has_res:
                o = o + r_ref[...]
            o_ref[...] = o.astype(out_dtype)

    in_specs = [a_spec, b_spec]
    args = [a, b]
    if has_res:
        in_specs.append(pl.BlockSpec((tm, tn), lambda i, j, s: (i, j)))
        args.append(residual)
    in_specs += [ANY] * len(deps)
    args += deps
    return pl.pallas_call(
        body_single if nk == 1 else body, name=name,
        out_shape=jax.ShapeDtypeStruct((m, n), out_dtype),
        grid=(m // tm, n // tn, nk),
        in_specs=in_specs,
        out_specs=pl.BlockSpec((tm, tn), lambda i, j, s: (i, j)),
        scratch_shapes=[] if nk == 1 else [pltpu.VMEM((tm, tn), F32)],
        compiler_params=_params(("parallel", "parallel", "arbitrary")),
    )(*args)


ROW_TILE = 256


def _rmsnorm_fwd(x, w, name, after=None):
    t, d = x.shape
    tm = ROW_TILE
    deps = [] if after is None else [after]

    def body(x_ref, w_ref, *rest):
        o_ref, ot_ref = rest[len(deps):]
        xv = x_ref[...]
        rstd = lax.rsqrt(jnp.mean(xv * xv, axis=-1, keepdims=True) + EPS)
        h = xv * rstd * w_ref[...]
        o_ref[...] = h.astype(o_ref.dtype)
        ot_ref[...] = h.T.astype(ot_ref.dtype)

    return pl.pallas_call(
        body, name=name,
        out_shape=(jax.ShapeDtypeStruct((t, d), MXU_DTYPE), jax.ShapeDtypeStruct((d, t), MXU_DTYPE)),
        grid=(t // tm,),
        in_specs=[pl.BlockSpec((tm, d), lambda i: (i, 0)), pl.BlockSpec((1, d), lambda i: (0, 0))]
        + [ANY] * len(deps),
        out_specs=(pl.BlockSpec((tm, d), lambda i: (i, 0)), pl.BlockSpec((d, tm), lambda i: (0, i))),
        compiler_params=_params(("parallel",)),
    )(x, w, *deps)


def _rmsnorm_bwd(dh, x, w, dres, name):
    t, d = x.shape
    tm = ROW_TILE
    nt = t // tm

    def body(dh_ref, x_ref, w_ref, dr_ref, dx_ref, dw_ref, acc):
        i = pl.program_id(0)

        @pl.when(i == 0)
        def _():
            acc[...] = jnp.zeros_like(acc)

        xv = x_ref[...]
        rstd = lax.rsqrt(jnp.mean(xv * xv, axis=-1, keepdims=True) + EPS)
        xh = xv * rstd
        dhv = dh_ref[...]
        g = dhv * w_ref[...]
        dx_ref[...] = dr_ref[...] + rstd * (g - xh * jnp.mean(g * xh, axis=-1, keepdims=True))
        acc[...] += _rowsum8(dhv * xh)

        @pl.when(i == nt - 1)
        def _():
            dw_ref[...] = jnp.sum(acc[...], axis=0, keepdims=True)

    row = pl.BlockSpec((tm, d), lambda i: (i, 0))
    vec = pl.BlockSpec((1, d), lambda i: (0, 0))
    return pl.pallas_call(
        body, name=name,
        out_shape=(jax.ShapeDtypeStruct((t, d), F32), jax.ShapeDtypeStruct((1, d), F32)),
        grid=(nt,),
        in_specs=[row, row, vec, row],
        out_specs=(row, vec),
        scratch_shapes=[pltpu.VMEM((SUBLANES, d), F32)],
        compiler_params=_params(("arbitrary",)),
    )(dh, x, w, dres)


def _loss_head(xf, target, w, name):
    t, d = xf.shape
    tm = ROW_TILE
    nt = t // tm

    def body(x_ref, t_ref, w_ref, loss_ref, dx_ref, dw_ref, lacc, wacc):
        i = pl.program_id(0)

        @pl.when(i == 0)
        def _():
            lacc[...] = jnp.zeros_like(lacc)
            wacc[...] = jnp.zeros_like(wacc)

        xv = x_ref[...]
        rstd = lax.rsqrt(jnp.mean(xv * xv, axis=-1, keepdims=True) + EPS)
        xh = xv * rstd
        err = xh * w_ref[...] - t_ref[...]
        lacc[...] += jnp.sum(err * err)
        dy = err * (1.0 / d)
        g = dy * w_ref[...]
        dx_ref[...] = rstd * (g - xh * jnp.mean(g * xh, axis=-1, keepdims=True))
        wacc[...] += _rowsum8(dy * xh)

        @pl.when(i == nt - 1)
        def _():
            loss_ref[...] = lacc[...] * (0.5 / d)
            dw_ref[...] = jnp.sum(wacc[...], axis=0, keepdims=True)

    row = pl.BlockSpec((tm, d), lambda i: (i, 0))
    vec = pl.BlockSpec((1, d), lambda i: (0, 0))
    return pl.pallas_call(
        body, name=name,
        out_shape=(jax.ShapeDtypeStruct((SUBLANES, LANES), F32), jax.ShapeDtypeStruct((t, d), F32),
                   jax.ShapeDtypeStruct((1, d), F32)),
        grid=(nt,),
        in_specs=[row, row, vec],
        out_specs=(pl.BlockSpec((SUBLANES, LANES), lambda i: (0, 0)), row, vec),
        scratch_shapes=[pltpu.VMEM((SUBLANES, LANES), F32), pltpu.VMEM((SUBLANES, d), F32)],
        compiler_params=_params(("arbitrary",)),
    )(xf, target, w)


def _conf_post_bwd(dycat, c1, proj, ln_w, ln_b, dproj, name):
    t = c1.shape[0]
    tm, cw = ROW_TILE, CONF_WIDTH
    nt = t // tm

    def body(dy_ref, c_ref, z_ref, w_ref, b_ref, _, dc_ref, dz_ref, dw_ref, db_ref, wacc, bacc):
        i = pl.program_id(0)

        @pl.when(i == 0)
        def _():
            wacc[...] = jnp.zeros_like(wacc)
            bacc[...] = jnp.zeros_like(bacc)

        cv = c_ref[...]
        xc = cv - jnp.mean(cv, axis=-1, keepdims=True)
        rstd = lax.rsqrt(jnp.mean(xc * xc, axis=-1, keepdims=True) + EPS)
        xh = xc * rstd
        c2 = xh * w_ref[...] + b_ref[...]
        zv = z_ref[...]
        dy = dy_ref[...]
        dz_ref[...] = (dy * _silu(c2) * _dsilu(zv)).astype(dz_ref.dtype)
        dc2 = dy * _silu(zv) * _dsilu(c2)
        bacc[...] += _rowsum8(dc2)
        wacc[...] += _rowsum8(dc2 * xh)
        dxh = dc2 * w_ref[...]
        dc_ref[...] = rstd * (dxh - jnp.mean(dxh, axis=-1, keepdims=True)
                              - xh * jnp.mean(dxh * xh, axis=-1, keepdims=True))

        @pl.when(i == nt - 1)
        def _():
            dw_ref[...] = jnp.sum(wacc[...], axis=0, keepdims=True)
            db_ref[...] = jnp.sum(bacc[...], axis=0, keepdims=True)

    row = pl.BlockSpec((tm, cw), lambda i: (i, 0))
    vec = pl.BlockSpec((1, cw), lambda i: (0, 0))
    return pl.pallas_call(
        body, name=name,
        out_shape=(jax.ShapeDtypeStruct((t, cw), F32), jax.ShapeDtypeStruct(dproj.shape, dproj.dtype),
                   jax.ShapeDtypeStruct((1, cw), F32), jax.ShapeDtypeStruct((1, cw), F32)),
        grid=(nt,),
        in_specs=[pl.BlockSpec((tm, cw), lambda i: (i, YCAT_CONF // cw)), row,
                  pl.BlockSpec((tm, cw), lambda i: (i, OFF_ZC // cw)), vec, vec, ANY],
        out_specs=(row, pl.BlockSpec((tm, cw), lambda i: (i, OFF_ZC // cw)), vec, vec),
        input_output_aliases={5: 1},
        scratch_shapes=[pltpu.VMEM((SUBLANES, cw), F32), pltpu.VMEM((SUBLANES, cw), F32)],
        compiler_params=_params(("arbitrary",)),
    )(dycat, c1, proj, ln_w, ln_b, dproj)


CONV_TILE = 512
CONV_COLS = 512
CONV_SUB_ROWS = 128
CONV_SUB_COLS = LANES


def _conv_halo(k):
    return SUBLANES if k - 1 <= SUBLANES else 32


def _conv_subtiles(tm, cw):
    return [(r0, c0) for r0 in range(0, tm, CONV_SUB_ROWS) for c0 in range(0, cw, CONV_SUB_COLS)]


def _conv_use_shifted(k):
    return k > SUBLANES


def _conv_shift_scratch(k, rows, cw):
    return [pltpu.VMEM((SUBLANES - 1, rows - SUBLANES, cw), F32)] if _conv_use_shifted(k) else []


def _conv_fill_shifted(ext, sh):
    n = sh.shape[1]
    for b in range(1, SUBLANES):
        sh[b - 1] = ext[b:b + n, :]


def _conv_rows(ext, sh, start, rows, cs):
    b = start % SUBLANES
    if b == 0 or not sh:
        return ext[start:start + rows, cs]
    return sh[0][b - 1, start - b:start - b + rows, cs]


def _conv_fwd(src, col0, width, w, bias, k, seq, name):
    t = src.shape[0]
    tm, cw, halo = CONV_TILE, CONV_COLS, _conv_halo(k)
    sr, sc = CONV_SUB_ROWS, CONV_SUB_COLS
    p = k - 1
    cb0 = col0 // cw
    kp = w.shape[0]

    shifted = _conv_use_shifted(k)

    def body(x_ref, h_ref, w_ref, b_ref, o_ref, ext, *sh):
        i = pl.program_id(0)
        seq_start = (i * tm) % seq == 0
        ext[halo:, :] = x_ref[...]
        ext[:halo, :] = jnp.where(seq_start, 0.0, h_ref[...])
        if shifted:
            _conv_fill_shifted(ext, sh[0])
        for r0, c0 in _conv_subtiles(tm, cw):
            cs = slice(c0, c0 + sc)
            acc = jnp.zeros((sr, sc), F32) + b_ref[:, cs]
            for j in range(k):
                acc = acc + w_ref[j:j + 1, cs] * _conv_rows(ext, sh, r0 + halo - p + j, sr, cs)
            o_ref[r0:r0 + sr, cs] = acc

    return pl.pallas_call(
        body, name=name,
        out_shape=jax.ShapeDtypeStruct((t, width), F32),
        grid=(t // tm, width // cw),
        in_specs=[pl.BlockSpec((tm, cw), lambda i, j: (i, cb0 + j)),
                  pl.BlockSpec((halo, cw), lambda i, j: (jnp.maximum(i * (tm // halo) - 1, 0), cb0 + j)),
                  pl.BlockSpec((kp, cw), lambda i, j: (0, j)),
                  pl.BlockSpec((1, cw), lambda i, j: (0, j))],
        out_specs=pl.BlockSpec((tm, cw), lambda i, j: (i, j)),
        scratch_shapes=[pltpu.VMEM((halo + tm, cw), F32)] + _conv_shift_scratch(k, halo + tm, cw),
        compiler_params=_params(("parallel", "parallel")),
    )(src, src, w, bias)


def _conv_bwd(dy, src, col0, width, w, k, seq, name, into=None):
    t = src.shape[0]
    tm, cw, halo = CONV_TILE, CONV_COLS, _conv_halo(k)
    sr, sc = CONV_SUB_ROWS, CONV_SUB_COLS
    p = k - 1
    cb0 = col0 // cw
    kp = w.shape[0]
    nt = t // tm
    last_halo = t // halo - 1

    shifted = _conv_use_shifted(k)

    def body(dy_ref, dn_ref, x_ref, xp_ref, w_ref, *rest):
        if into is not None:
            rest = rest[1:]
        dx_ref, dw_ref, db_ref, dyext, xext, wacc, bacc = rest[:7]
        sh = rest[7:]
        i = pl.program_id(1)
        dysh, xsh = (sh[:1], sh[1:]) if shifted else ((), ())

        @pl.when(i == 0)
        def _():
            wacc[...] = jnp.zeros_like(wacc)
            bacc[...] = jnp.zeros_like(bacc)

        seq_start = (i * tm) % seq == 0
        seq_end = ((i + 1) * tm) % seq == 0
        dyext[:tm, :] = dy_ref[...]
        dyext[tm:, :] = jnp.where(seq_end, 0.0, dn_ref[...])
        xext[halo:, :] = x_ref[...]
        xext[:halo, :] = jnp.where(seq_start, 0.0, xp_ref[...])
        if shifted:
            _conv_fill_shifted(dyext, dysh[0])
            _conv_fill_shifted(xext, xsh[0])
        for r0, c0 in _conv_subtiles(tm, cw):
            cs = slice(c0, c0 + sc)
            dyv = dy_ref[r0:r0 + sr, cs]
            acc = jnp.zeros((sr, sc), F32)
            for j in range(k):
                acc = acc + w_ref[j:j + 1, cs] * _conv_rows(dyext, dysh, r0 + p - j, sr, cs)
                wacc[j, :, cs] += _rowsum8(dyv * _conv_rows(xext, xsh, r0 + halo - p + j, sr, cs))
            dx_ref[r0:r0 + sr, cs] = acc.astype(dx_ref.dtype)
            bacc[:, cs] += _rowsum8(dyv)

        @pl.when(i == nt - 1)
        def _():
            dw_ref[...] = jnp.zeros_like(dw_ref)
            for j in range(k):
                dw_ref[j:j + 1, :] = jnp.sum(wacc[j], axis=0, keepdims=True)
            db_ref[...] = jnp.sum(bacc[...], axis=0, keepdims=True)

    if into is None:
        dx_shape = jax.ShapeDtypeStruct((t, width), F32)
        dx_spec = pl.BlockSpec((tm, cw), lambda j, i: (i, j))
        extra_specs, extra_args, aliases = [], [], {}
    else:
        dx_shape = jax.ShapeDtypeStruct(into.shape, into.dtype)
        dx_spec = pl.BlockSpec((tm, cw), lambda j, i: (i, cb0 + j))
        extra_specs, extra_args, aliases = [ANY], [into], {5: 0}
    return pl.pallas_call(
        body, name=name,
        out_shape=(dx_shape, jax.ShapeDtypeStruct((kp, width), F32), jax.ShapeDtypeStruct((1, width), F32)),
        grid=(width // cw, nt),
        in_specs=[pl.BlockSpec((tm, cw), lambda j, i: (i, j)),
                  pl.BlockSpec((halo, cw), lambda j, i: (jnp.minimum((i + 1) * (tm // halo), last_halo), j)),
                  pl.BlockSpec((tm, cw), lambda j, i: (i, cb0 + j)),
                  pl.BlockSpec((halo, cw), lambda j, i: (jnp.maximum(i * (tm // halo) - 1, 0), cb0 + j)),
                  pl.BlockSpec((kp, cw), lambda j, i: (0, j))] + extra_specs,
        out_specs=(dx_spec,
                   pl.BlockSpec((kp, cw), lambda j, i: (0, j)),
                   pl.BlockSpec((1, cw), lambda j, i: (0, j))),
        input_output_aliases=aliases,
        scratch_shapes=[pltpu.VMEM((tm + halo, cw), F32), pltpu.VMEM((halo + tm, cw), F32),
                        pltpu.VMEM((kp, SUBLANES, cw), F32), pltpu.VMEM((SUBLANES, cw), F32)]
        + 2 * _conv_shift_scratch(k, halo + tm, cw),
        compiler_params=_params(("parallel", "arbitrary")),
    )(dy, dy, src, src, w, *extra_args)


def _conf_specs(tm, cw, halo, order):
    cb = OFF_CONF // cw

    def blk(col):
        return pl.BlockSpec((tm, cw), lambda *g: (order(*g), col))

    def prev(col):
        return pl.BlockSpec((halo, cw), lambda *g: (jnp.maximum(order(*g) * (tm // halo) - 1, 0), col))

    return blk(cb), prev(cb), blk(cb + 1), prev(cb + 1)


def _glu_window(ext, a_ref, ah_ref, g_ref, gh_ref, seq_start, halo):
    ext[halo:, :] = a_ref[...] * _sigmoid(g_ref[...])
    ext[:halo, :] = jnp.where(seq_start, 0.0, ah_ref[...] * _sigmoid(gh_ref[...]))


def _conf_fwd(proj, w, bias, ln_w, ln_b, ycat, seq, name):
    t = proj.shape[0]
    k = CONF_KERNEL
    tm, cw, halo = CONV_TILE, CONF_WIDTH, _conv_halo(k)
    sr, sc = CONV_SUB_ROWS, CONV_SUB_COLS
    p = k - 1
    kp = w.shape[0]

    def body(a_ref, ah_ref, g_ref, gh_ref, z_ref, w_ref, b_ref, lw_ref, lb_ref, _, c1_ref, y_ref, ext, sh):
        i = pl.program_id(0)
        _glu_window(ext, a_ref, ah_ref, g_ref, gh_ref, (i * tm) % seq == 0, halo)
        _conv_fill_shifted(ext, sh)
        for r0, c0 in _conv_subtiles(tm, cw):
            cs = slice(c0, c0 + sc)
            acc = jnp.zeros((sr, sc), F32) + b_ref[:, cs]
            for j in range(k):
                acc = acc + w_ref[j:j + 1, cs] * _conv_rows(ext, (sh,), r0 + halo - p + j, sr, cs)
            c1_ref[r0:r0 + sr, cs] = acc
        for r0 in range(0, tm, sr):
            rows = slice(r0, r0 + sr)
            cv = c1_ref[rows, :]
            xc = cv - jnp.mean(cv, axis=-1, keepdims=True)
            rstd = lax.rsqrt(jnp.mean(xc * xc, axis=-1, keepdims=True) + EPS)
            c2 = xc * rstd * lw_ref[...] + lb_ref[...]
            y_ref[rows, :] = (_silu(c2) * _silu(z_ref[rows, :])).astype(y_ref.dtype)

    vec = pl.BlockSpec((1, cw), lambda i: (0, 0))
    row = pl.BlockSpec((tm, cw), lambda i: (i, 0))
    return pl.pallas_call(
        body, name=name,
        out_shape=(jax.ShapeDtypeStruct((t, cw), F32), jax.ShapeDtypeStruct(ycat.shape, ycat.dtype)),
        grid=(t // tm,),
        in_specs=[*_conf_specs(tm, cw, halo, lambda i: i),
                  pl.BlockSpec((tm, cw), lambda i: (i, OFF_ZC // cw)),
                  pl.BlockSpec((kp, cw), lambda i: (0, 0)), vec, vec, vec, ANY],
        out_specs=(row, pl.BlockSpec((tm, cw), lambda i: (i, YCAT_CONF // cw))),
        input_output_aliases={9: 1},
        scratch_shapes=[pltpu.VMEM((halo + tm, cw), F32)] + _conv_shift_scratch(k, halo + tm, cw),
        compiler_params=_params(("parallel",)),
    )(proj, proj, proj, proj, proj, w, bias, ln_w, ln_b, ycat)


def _conf_conv_bwd(dc1, proj, w, dproj, seq, name):
    t = proj.shape[0]
    k = CONF_KERNEL
    tm, cw, halo = CONV_TILE, CONF_WIDTH, _conv_halo(k)
    sr, sc = CONV_SUB_ROWS, CONV_SUB_COLS
    p = k - 1
    kp = w.shape[0]
    nt = t // tm
    last_halo = t // halo - 1

    def body(dy_ref, dn_ref, a_ref, ah_ref, g_ref, gh_ref, w_ref, _, dag_ref, dw_ref, db_ref,
             dyext, xext, wacc, bacc, dysh, xsh):
        i = pl.program_id(0)

        @pl.when(i == 0)
        def _():
            wacc[...] = jnp.zeros_like(wacc)
            bacc[...] = jnp.zeros_like(bacc)

        seq_end = ((i + 1) * tm) % seq == 0
        dyext[:tm, :] = dy_ref[...]
        dyext[tm:, :] = jnp.where(seq_end, 0.0, dn_ref[...])
        _glu_window(xext, a_ref, ah_ref, g_ref, gh_ref, (i * tm) % seq == 0, halo)
        _conv_fill_shifted(dyext, dysh)
        _conv_fill_shifted(xext, xsh)
        for r0, c0 in _conv_subtiles(tm, cw):
            cs = slice(c0, c0 + sc)
            rows = slice(r0, r0 + sr)
            dyv = dy_ref[rows, cs]
            acc = jnp.zeros((sr, sc), F32)
            for j in range(k):
                acc = acc + w_ref[j:j + 1, cs] * _conv_rows(dyext, (dysh,), r0 + p - j, sr, cs)
                wacc[j, :, cs] += _rowsum8(dyv * _conv_rows(xext, (xsh,), r0 + halo - p + j, sr, cs))
            bacc[:, cs] += _rowsum8(dyv)
            s = _sigmoid(g_ref[rows, cs])
            dag_ref[rows, cs] = (acc * s).astype(dag_ref.dtype)
            dag_ref[rows, cw + c0:cw + c0 + sc] = (acc * a_ref[rows, cs] * s * (1.0 - s)).astype(dag_ref.dtype)

        @pl.when(i == nt - 1)
        def _():
            dw_ref[...] = jnp.zeros_like(dw_ref)
            for j in range(k):
                dw_ref[j:j + 1, :] = jnp.sum(wacc[j], axis=0, keepdims=True)
            db_ref[...] = jnp.sum(bacc[...], axis=0, keepdims=True)

    return pl.pallas_call(
        body, name=name,
        out_shape=(jax.ShapeDtypeStruct(dproj.shape, dproj.dtype), jax.ShapeDtypeStruct((kp, cw), F32),
                   jax.ShapeDtypeStruct((1, cw), F32)),
        grid=(nt,),
        in_specs=[pl.BlockSpec((tm, cw), lambda i: (i, 0)),
                  pl.BlockSpec((halo, cw), lambda i: (jnp.minimum((i + 1) * (tm // halo), last_halo), 0)),
                  *_conf_specs(tm, cw, halo, lambda i: i),
                  pl.BlockSpec((kp, cw), lambda i: (0, 0)), ANY],
        out_specs=(pl.BlockSpec((tm, 2 * cw), lambda i: (i, OFF_CONF // (2 * cw))),
                   pl.BlockSpec((kp, cw), lambda i: (0, 0)), pl.BlockSpec((1, cw), lambda i: (0, 0))),
        input_output_aliases={7: 0},
        scratch_shapes=[pltpu.VMEM((tm + halo, cw), F32), pltpu.VMEM((halo + tm, cw), F32),
                        pltpu.VMEM((kp, SUBLANES, cw), F32), pltpu.VMEM((SUBLANES, cw), F32)]
        + 2 * _conv_shift_scratch(k, halo + tm, cw),
        compiler_params=_params(("arbitrary",)),
    )(dc1, dc1, proj, proj, proj, proj, w, dproj)


def _half_mask(half):
    lane = _iota((1, LANES), 1)
    return ((lane >= half * ATTN_HEAD_DIM) & (lane < (half + 1) * ATTN_HEAD_DIM)).astype(F32)


def _stack_heads(xp, g):
    m = _half_mask(g)
    swapped = pltpu.roll(xp, ATTN_HEAD_DIM, axis=1)
    return jnp.concatenate([xp * m, swapped * m] if g == 0 else [swapped * m, xp * m], axis=0)


def _unstack_heads(both, g):
    w = both.shape[0] // 2
    top, bot = both[:w], both[w:]
    lo, hi = _half_mask(0), _half_mask(1)
    if g == 0:
        return top * lo + pltpu.roll(bot, ATTN_HEAD_DIM, axis=1) * hi
    return pltpu.roll(top, ATTN_HEAD_DIM, axis=1) * lo + bot * hi


def _band_mask(first_block):
    w = WINDOW
    qi = _iota((w, 2 * w), 0)
    kj = _iota((w, 2 * w), 1) - w
    rel = qi - kj
    return (rel >= 0) & (rel < w) & (jnp.logical_not(first_block) | (kj >= 0))


def _lane_pick(x, h):
    return jnp.sum(jnp.where(_iota(x.shape, 1) == h, x, 0.0), axis=1, keepdims=True)


def _attn_specs(nb, rev):
    w = WINDOW

    def blk(i):
        return nb - 1 - i if rev else i

    def row(b, i):
        return b * nb + blk(i)

    def prow(b, i):
        return b * nb + jnp.maximum(blk(i) - 1, 0)

    q = pl.BlockSpec((w, 512), lambda b, i: (row(b, i), OFF_Q // 512))
    kc = pl.BlockSpec((w, 128), lambda b, i: (row(b, i), OFF_K // 128))
    kp = pl.BlockSpec((w, 128), lambda b, i: (prow(b, i), OFF_K // 128))
    vc = pl.BlockSpec((w, 128), lambda b, i: (row(b, i), OFF_V // 128))
    vp = pl.BlockSpec((w, 128), lambda b, i: (prow(b, i), OFF_V // 128))
    z = pl.BlockSpec((w, 512), lambda b, i: (row(b, i), OFF_ZA // 512))
    return q, kc, kp, vc, vp, z, row


def _attn_fwd(proj, sinks, ycat, nbatch, name):
    t = proj.shape[0]
    w = WINDOW
    nb = t // nbatch // w
    scale = ATTN_HEAD_DIM ** -0.5
    q_s, kc_s, kp_s, vc_s, vp_s, z_s, row = _attn_specs(nb, False)

    def body(q_ref, kc_ref, kp_ref, vc_ref, vp_ref, z_ref, sk_ref, _, y_ref, o_ref, lse_ref):
        first = pl.program_id(1) == 0
        mask = _band_mask(first)
        kk = jnp.concatenate([kp_ref[...], kc_ref[...]], axis=0).astype(MXU_DTYPE)
        vv = jnp.concatenate([vp_ref[...], vc_ref[...]], axis=0).astype(MXU_DTYPE)
        sk = sk_ref[...]
        lane = _iota((w, LANES), 1)
        mask2 = jnp.concatenate([mask, mask], axis=0)
        scores = [_dot(_stack_heads(q_ref[:, j * LANES:(j + 1) * LANES], j // 2), kk, NT) for j in range(4)]
        lse_all = jnp.zeros((w, LANES), F32)
        for j in range(4):
            s = jnp.where(mask2, scores[j] * scale, -1e30)
            skc = jnp.concatenate([jnp.broadcast_to(_lane_pick(sk, 2 * j), (w, 1)),
                                   jnp.broadcast_to(_lane_pick(sk, 2 * j + 1), (w, 1))], axis=0)
            m = jnp.maximum(jnp.max(s, axis=1, keepdims=True), skc)
            den = jnp.sum(jnp.exp(s - m), axis=1, keepdims=True) + jnp.exp(skc - m)
            lse = m + jnp.log(den)
            lse_all = jnp.where(lane == 2 * j, lse[:w], lse_all)
            lse_all = jnp.where(lane == 2 * j + 1, lse[w:], lse_all)
            op = _unstack_heads(_dot(jnp.exp(s - lse), vv), j // 2)
            cols = slice(j * LANES, (j + 1) * LANES)
            o_ref[:, cols] = op
            y_ref[:, cols] = (op * _silu(z_ref[:, cols])).astype(y_ref.dtype)
        lse_ref[...] = lse_all

    return pl.pallas_call(
        body, name=name,
        out_shape=(jax.ShapeDtypeStruct(ycat.shape, ycat.dtype), jax.ShapeDtypeStruct((t, 512), F32),
                   jax.ShapeDtypeStruct((t, LANES), F32)),
        grid=(nbatch, nb),
        in_specs=[q_s, kc_s, kp_s, vc_s, vp_s, z_s, pl.BlockSpec((1, LANES), lambda b, i: (0, 0)), ANY],
        out_specs=(pl.BlockSpec((w, 512), lambda b, i: (row(b, i), YCAT_ATTN // 512)),
                   pl.BlockSpec((w, 512), lambda b, i: (row(b, i), 0)),
                   pl.BlockSpec((w, LANES), lambda b, i: (row(b, i), 0))),
        input_output_aliases={7: 0},
        compiler_params=_params(("parallel", "parallel")),
    )(proj, proj, proj, proj, proj, proj, sinks, ycat)


def _attn_bwd(dycat, proj, o, lse, sinks, ddt, dproj, nbatch, name):
    t = proj.shape[0]
    w = WINDOW
    nb = t // nbatch // w
    scale = ATTN_HEAD_DIM ** -0.5
    q_s, kc_s, kp_s, vc_s, vp_s, z_s, row = _attn_specs(nb, True)

    def body(dy_ref, q_ref, kc_ref, kp_ref, vc_ref, vp_ref, z_ref, o_ref, lse_ref, sk_ref, ddt_ref, _,
             grp_ref, dsk_ref, kcarry, vcarry, sacc):
        b, i = pl.program_id(0), pl.program_id(1)

        @pl.when((b == 0) & (i == 0))
        def _():
            sacc[...] = jnp.zeros_like(sacc)

        @pl.when(i == 0)
        def _():
            kcarry[...] = jnp.zeros_like(kcarry)
            vcarry[...] = jnp.zeros_like(vcarry)

        first = i == nb - 1
        mask = _band_mask(first)
        kk = jnp.concatenate([kp_ref[...], kc_ref[...]], axis=0).astype(MXU_DTYPE)
        vv = jnp.concatenate([vp_ref[...], vc_ref[...]], axis=0).astype(MXU_DTYPE)
        sk = sk_ref[...]
        lse_all = lse_ref[...]
        lane1 = _iota((1, LANES), 1)
        mask2 = jnp.concatenate([mask, mask], axis=0)
        qs, dos, deltas, lses, scores, dps = [], [], [], [], [], []
        for j in range(4):
            cols = slice(j * LANES, (j + 1) * LANES)
            qp, zp, ov, dy = q_ref[:, cols], z_ref[:, cols], o_ref[:, cols], dy_ref[:, cols]
            grp_ref[:, OFF_ZA + j * LANES:OFF_ZA + (j + 1) * LANES] = (dy * ov * _dsilu(zp)).astype(grp_ref.dtype)
            do = dy * _silu(zp)
            q2 = _stack_heads(qp, j // 2).astype(MXU_DTYPE)
            do2 = _stack_heads(do, j // 2)
            qs.append(q2)
            dos.append(do2.astype(MXU_DTYPE))
            deltas.append(jnp.sum(do2 * _stack_heads(ov, j // 2), axis=1, keepdims=True))
            lses.append(jnp.concatenate([_lane_pick(lse_all, 2 * j), _lane_pick(lse_all, 2 * j + 1)], axis=0))
            scores.append(_dot(q2, kk, NT))
            dps.append(_dot(do2, vv, NT))
        prs, dss = [], []
        dsk = jnp.zeros((1, LANES), F32)
        for j in range(4):
            pr = jnp.exp(jnp.where(mask2, scores[j] * scale, -1e30) - lses[j])
            prs.append(pr.astype(MXU_DTYPE))
            dss.append((pr * (dps[j] - deltas[j])).astype(MXU_DTYPE))
            skc = jnp.concatenate([jnp.broadcast_to(_lane_pick(sk, 2 * j), (w, 1)),
                                   jnp.broadcast_to(_lane_pick(sk, 2 * j + 1), (w, 1))], axis=0)
            sink_term = jnp.exp(skc - lses[j]) * deltas[j]
            dsk = dsk - jnp.where(lane1 == 2 * j, jnp.sum(sink_term[:w]), 0.0)
            dsk = dsk - jnp.where(lane1 == 2 * j + 1, jnp.sum(sink_term[w:]), 0.0)
        dkk = jnp.zeros((2 * w, LANES), F32)
        dvv = jnp.zeros((2 * w, LANES), F32)
        for j in range(4):
            dq = _unstack_heads(_dot(dss[j], kk) * scale, j // 2)
            grp_ref[:, OFF_Q + j * LANES:OFF_Q + (j + 1) * LANES] = dq.astype(grp_ref.dtype)
            dkk = dkk + _dot(dss[j], qs[j], TN) * scale
            dvv = dvv + _dot(prs[j], dos[j], TN)
        grp_ref[:, OFF_K:OFF_K + LANES] = (dkk[w:, :] + kcarry[...]).astype(grp_ref.dtype)
        grp_ref[:, OFF_V:OFF_V + LANES] = (dvv[w:, :] + vcarry[...]).astype(grp_ref.dtype)
        grp_ref[:, OFF_DT:OFF_DT + LANES] = ddt_ref[...].astype(grp_ref.dtype)
        grp_ref[:, OFF_DT + LANES:] = jnp.zeros((w, ATTN_GROUP - OFF_DT - LANES), grp_ref.dtype)
        kcarry[...] = dkk[:w, :]
        vcarry[...] = dvv[:w, :]
        sacc[...] += dsk

        @pl.when((b == nbatch - 1) & (i == nb - 1))
        def _():
            dsk_ref[...] = sacc[...]

    return pl.pallas_call(
        body, name=name,
        out_shape=(jax.ShapeDtypeStruct(dproj.shape, dproj.dtype), jax.ShapeDtypeStruct((1, LANES), F32)),
        grid=(nbatch, nb),
        in_specs=[pl.BlockSpec((w, 512), lambda b, i: (row(b, i), YCAT_ATTN // 512)),
                  q_s, kc_s, kp_s, vc_s, vp_s, z_s,
                  pl.BlockSpec((w, 512), lambda b, i: (row(b, i), 0)),
                  pl.BlockSpec((w, LANES), lambda b, i: (row(b, i), 0)),
                  pl.BlockSpec((1, LANES), lambda b, i: (0, 0)),
                  pl.BlockSpec((w, LANES), lambda b, i: (row(b, i), 0)), ANY],
        out_specs=(pl.BlockSpec((w, ATTN_GROUP), lambda b, i: (row(b, i), 0)),
                   pl.BlockSpec((1, LANES), lambda b, i: (0, 0))),
        input_output_aliases={11: 0},
        scratch_shapes=[pltpu.VMEM((w, LANES), F32), pltpu.VMEM((w, LANES), F32),
                        pltpu.VMEM((1, LANES), F32)],
        compiler_params=_params(("arbitrary", "arbitrary")),
    )(dycat, proj, proj, proj, proj, proj, proj, o, lse, sinks, ddt, dproj)


SSD_WIDTH = SSD_HEADS * SSD_HEAD_DIM
GROUP_ROWS = SSD_WIDTH // 2


def _expand_mat():
    r, c = _iota((LANES, SSD_WIDTH), 0), _iota((LANES, SSD_WIDTH), 1)
    return (r == lax.shift_right_logical(c, 6)).astype(BF16)


def _expand_mat_t():
    r, c = _iota((SSD_WIDTH, LANES), 0), _iota((SSD_WIDTH, LANES), 1)
    return (c == lax.shift_right_logical(r, 6)).astype(BF16)


def _ssd_common(u_ref, dt_ref, dtb_ref, a_ref):
    q = CHUNK
    act = _silu(u_ref[...])
    xs = act[:, :SSD_WIDTH]
    bm = act[:, SSD_WIDTH:SSD_WIDTH + 256]
    cm = act[:, SSD_WIDTH + 256:]
    dtp = _softplus(dt_ref[...] + dtb_ref[...])
    a = dtp * a_ref[...]
    tril = (_iota((q, q), 0) >= _iota((q, q), 1)).astype(BF16)
    acs = _xdot_r(tril, a)
    acs_t = acs.T
    e = _expand_mat()
    dt_x = _xdot(dtp, e)
    ea = jnp.exp(_xdot(acs, e))
    a_end = jnp.sum(jnp.where(_iota(acs.shape, 0) == q - 1, acs, 0.0), axis=0, keepdims=True)
    dec = jnp.exp(_xdot(a_end - acs, e))
    a_end_col = jnp.broadcast_to(_lane_pick(acs_t, q - 1), (LANES, LANES))
    s_scale = jnp.exp(_xdot_r(_expand_mat_t(), a_end_col))
    return act, xs, bm, cm, dtp, acs, acs_t, dt_x, ea, dec, s_scale, tril


def _decay_mat(acs, acs_t, h):
    q = CHUNK
    col = _lane_pick(acs, h)
    rowv = jnp.sum(jnp.where(_iota(acs_t.shape, 0) == h, acs_t, 0.0), axis=0, keepdims=True)
    causal = _iota((q, q), 0) >= _iota((q, q), 1)
    return jnp.exp(jnp.where(causal, col - rowv, -1e30))


GN_WIDTH = 512


def _ssd_fwd(u, proj, dtb, a_neg, d_x, norm_w, ycat, nbatch, name):
    t = u.shape[0]
    q = CHUNK
    nc = t // nbatch // q

    def body(u_ref, dt_ref, z_ref, dtb_ref, a_ref, dx_ref, nw_ref, _, y_ref, st_ref, yn_ref, state):
        c = pl.program_id(1)

        @pl.when(c == 0)
        def _():
            state[...] = jnp.zeros_like(state)

        st_ref[...] = state[...]
        act, xs, bm, cm, dtp, acs, acs_t, dt_x, ea, dec, s_scale, _ = _ssd_common(u_ref, dt_ref, dtb_ref, a_ref)
        xdt = xs * dt_x
        xdec = xdt * dec
        lo, hi = _half_mask(0), _half_mask(1)
        grp = []
        for g in range(2):
            bg = bm[:, g * LANES:(g + 1) * LANES]
            cg = cm[:, g * LANES:(g + 1) * LANES]
            rows = slice(g * GROUP_ROWS, (g + 1) * GROUP_ROWS)
            sg = state[rows, :]
            grp.append((_dot(cg, bg, NT), _dot(cg, sg, NT), rows,
                        s_scale[rows, :] * sg + _dot(xdec[:, rows], bg, TN)))
        for g in range(2):
            cb, yoff, rows, state_new = grp[g]
            for j in range(4):
                pj = g * 4 + j
                cols = slice(pj * LANES, (pj + 1) * LANES)
                xp = xdt[:, cols]
                m2 = jnp.concatenate([cb * _decay_mat(acs, acs_t, 2 * pj), cb * _decay_mat(acs, acs_t, 2 * pj + 1)],
                                     axis=1)
                yp = _dot(m2, jnp.concatenate([xp * lo, xp * hi], axis=0))
                yp = yp + yoff[:, j * LANES:(j + 1) * LANES] * ea[:, cols]
                y_ref[:, cols] = yp + dx_ref[:, cols] * xs[:, cols]
            state[rows, :] = state_new
        for g in range(SSD_WIDTH // GN_WIDTH):
            cols = slice(g * GN_WIDTH, (g + 1) * GN_WIDTH)
            gg = y_ref[:, cols] * _silu(z_ref[:, cols])
            rstd = lax.rsqrt(jnp.mean(gg * gg, axis=-1, keepdims=True) + EPS)
            yn_ref[:, cols] = (gg * rstd * nw_ref[:, cols]).astype(yn_ref.dtype)

    vec = pl.BlockSpec((1, LANES), lambda b, c: (0, 0))
    wide = pl.BlockSpec((q, SSD_WIDTH), lambda b, c: (b * nc + c, 0))
    wvec = pl.BlockSpec((1, SSD_WIDTH), lambda b, c: (0, 0))
    return pl.pallas_call(
        body, name=name,
        out_shape=(jax.ShapeDtypeStruct((t, SSD_WIDTH), F32),
                   jax.ShapeDtypeStruct((nbatch * nc * SSD_WIDTH, SSD_STATE), F32),
                   jax.ShapeDtypeStruct(ycat.shape, ycat.dtype)),
        grid=(nbatch, nc),
        in_specs=[pl.BlockSpec((q, SSD_CONV_DIM), lambda b, c: (b * nc + c, 0)),
                  pl.BlockSpec((q, LANES), lambda b, c: (b * nc + c, OFF_DT // LANES)),
                  pl.BlockSpec((q, SSD_WIDTH), lambda b, c: (b * nc + c, OFF_ZS // SSD_WIDTH)),
                  vec, vec, wvec, wvec, ANY],
        out_specs=(wide, pl.BlockSpec((SSD_WIDTH, SSD_STATE), lambda b, c: (b * nc + c, 0)), wide),
        input_output_aliases={7: 2},
        scratch_shapes=[pltpu.VMEM((SSD_WIDTH, SSD_STATE), F32)],
        compiler_params=_params(("parallel", "arbitrary")),
    )(u, proj, proj, dtb, a_neg, d_x, norm_w, ycat)


def _ssd_bwd(dycat, u, proj, y, states, dtb, a_neg, d_x, norm_w, dproj, nbatch, name):
    t = u.shape[0]
    q = CHUNK
    nc = t // nbatch // q

    def body(do_ref, u_ref, dt_ref, z_ref, y_ref, st_ref, dtb_ref, a_ref, dx_ref, nw_ref, _,
             du_ref, dz_ref, ddt_ref, dal_ref, dd_ref, dtbg_ref, dnw_ref, dstate, acc_a, acc_d, acc_b, acc_w):
        b, c = pl.program_id(0), pl.program_id(1)

        @pl.when((b == 0) & (c == 0))
        def _():
            acc_a[...] = jnp.zeros_like(acc_a)
            acc_d[...] = jnp.zeros_like(acc_d)
            acc_b[...] = jnp.zeros_like(acc_b)
            acc_w[...] = jnp.zeros_like(acc_w)

        @pl.when(c == 0)
        def _():
            dstate[...] = jnp.zeros_like(dstate)

        dy_parts = []
        for g in range(SSD_WIDTH // GN_WIDTH):
            cols = slice(g * GN_WIDTH, (g + 1) * GN_WIDTH)
            yv, zv, dov = y_ref[:, cols], z_ref[:, cols], do_ref[:, cols]
            sz = _silu(zv)
            gg = yv * sz
            rstd = lax.rsqrt(jnp.mean(gg * gg, axis=-1, keepdims=True) + EPS)
            gh = gg * rstd
            acc_w[:, cols] += _rowsum8(dov * gh)
            dgn = dov * nw_ref[:, cols]
            dg = rstd * (dgn - gh * jnp.mean(dgn * gh, axis=-1, keepdims=True))
            dy_parts.append(dg * sz)
            dz_ref[:, cols] = (dg * yv * _dsilu(zv)).astype(dz_ref.dtype)

        act, xs, bm, cm, dtp, acs, acs_t, dt_x, ea, dec, s_scale, tril = _ssd_common(
            u_ref, dt_ref, dtb_ref, a_ref)
        xdt = xs * dt_x
        xdec = xdt * dec
        dyv = jnp.concatenate(dy_parts, axis=1)
        dye = dyv * ea
        lo, hi = _half_mask(0), _half_mask(1)
        et = _expand_mat_t()
        grp = []
        for g in range(2):
            rows = slice(g * GROUP_ROWS, (g + 1) * GROUP_ROWS)
            bg = bm[:, g * LANES:(g + 1) * LANES]
            cg = cm[:, g * LANES:(g + 1) * LANES]
            sg = st_ref[rows, :]
            dsg = dstate[rows, :]
            grp.append(dict(
                rows=rows, bg=bg, cg=cg, dsg=dsg,
                cb=_dot(cg, bg, NT), yoff=_dot(cg, sg, NT), dxst=_dot(bg, dsg, NT) * dec[:, rows],
                dc_off=_dot(dye[:, rows], sg), db_off=_dot(xdec[:, rows], dsg),
                s_next=s_scale[rows, :] * sg + _dot(xdec[:, rows], bg, TN),
                dstate_new=_dot(dye[:, rows], cg, TN) + s_scale[rows, :] * dsg))
        dy2s, g2s, l2s = [], [], []
        for pj in range(SSD_HEADS // 2):
            cols = slice(pj * LANES, (pj + 1) * LANES)
            dyp = dyv[:, cols]
            dy2 = jnp.concatenate([dyp * lo, dyp * hi], axis=0).astype(MXU_DTYPE)
            dy2s.append(dy2)
            g2s.append(_dot(dy2, xdt[:, cols], NT))
            l2s.append(jnp.concatenate([_decay_mat(acs, acs_t, 2 * pj), _decay_mat(acs, acs_t, 2 * pj + 1)], axis=0))
        dal_diag = jnp.zeros((q, LANES), F32)
        lane2 = _iota((2 * q, LANES), 1)
        row2 = _iota((2 * q, LANES), 0)
        dxdt_parts, db_parts, dc_parts = [], [], []
        end_sum = jnp.zeros((LANES, LANES), F32)
        for g in range(2):
            gd = grp[g]
            cb2 = jnp.concatenate([gd["cb"], gd["cb"]], axis=0)
            dcb = jnp.zeros((q, q), F32)
            parts = []
            for j in range(4):
                pj = g * 4 + j
                gl = g2s[pj] * l2s[pj]
                dcb = dcb + gl[:q] + gl[q:]
                m2 = cb2 * l2s[pj]
                parts.append(_dot(m2, dy2s[pj], TN))
                w2 = (gl * cb2).astype(MXU_DTYPE)
                sel2 = (lane2 == 2 * pj + (row2 >= q).astype(jnp.int32)).astype(MXU_DTYPE)
                dal_diag = dal_diag + _dot(jnp.concatenate([w2[:q], w2[q:]], axis=1), sel2) - _dot(w2, sel2, TN)
            dxdt_parts.append(jnp.concatenate(parts, axis=1) + gd["dxst"])
            dc_parts.append(_dot(dcb, gd["bg"]) + gd["dc_off"])
            db_parts.append(_dot(dcb, gd["cg"], TN) + gd["db_off"])
            end_sum = end_sum + _xdot(gd["dsg"] * gd["s_next"], et[gd["rows"], :], TN, passes=2)
            dstate[gd["rows"], :] = gd["dstate_new"]
        dxst_parts = [gd["dxst"] for gd in grp]
        yoff_parts = [gd["yoff"] for gd in grp]
        dxdt = jnp.concatenate(dxdt_parts, axis=1)
        dxv = dx_ref[...]
        yoff = jnp.concatenate(yoff_parts, axis=1) * ea
        dalpha = dal_diag + _xdot(dyv * yoff - xdt * jnp.concatenate(dxst_parts, axis=1), et)
        end_row = jnp.sum(end_sum, axis=0, keepdims=True)
        dalpha = dalpha + jnp.where(_iota((q, LANES), 0) == q - 1, end_row, 0.0)
        da = _xdot_r(tril, dalpha, TN)
        ddtp = da * a_ref[...] + _xdot(dxdt * xs, et)
        acc_a[...] += _rowsum8(da * dtp)
        acc_d[...] += _rowsum8(_xdot(dyv * xs, et))
        ddt_raw = ddtp * _sigmoid(dt_ref[...] + dtb_ref[...])
        acc_b[...] += _rowsum8(ddt_raw)
        ddt_ref[...] = ddt_raw
        dxs = dxdt * dt_x + dxv * dyv
        dact = jnp.concatenate([dxs] + db_parts + dc_parts, axis=1)
        du_ref[...] = dact * _dsilu(u_ref[...])

        @pl.when((b == nbatch - 1) & (c == nc - 1))
        def _():
            dal_ref[...] = jnp.sum(acc_a[...], axis=0, keepdims=True) * a_ref[...]
            dd_ref[...] = jnp.sum(acc_d[...], axis=0, keepdims=True)
            dtbg_ref[...] = jnp.sum(acc_b[...], axis=0, keepdims=True)
            dnw_ref[...] = jnp.sum(acc_w[...], axis=0, keepdims=True)

    def rowblk(b, c):
        return b * nc + (nc - 1 - c)

    vec = pl.BlockSpec((1, LANES), lambda b, c: (0, 0))
    wvec = pl.BlockSpec((1, SSD_WIDTH), lambda b, c: (0, 0))
    wide = pl.BlockSpec((q, SSD_WIDTH), lambda b, c: (rowblk(b, c), 0))
    zblk = pl.BlockSpec((q, SSD_WIDTH), lambda b, c: (rowblk(b, c), OFF_ZS // SSD_WIDTH))
    return pl.pallas_call(
        body, name=name,
        out_shape=(jax.ShapeDtypeStruct((t, SSD_CONV_DIM), F32), jax.ShapeDtypeStruct(dproj.shape, dproj.dtype),
                   jax.ShapeDtypeStruct((t, LANES), F32),
                   jax.ShapeDtypeStruct((1, LANES), F32), jax.ShapeDtypeStruct((1, LANES), F32),
                   jax.ShapeDtypeStruct((1, LANES), F32), jax.ShapeDtypeStruct((1, SSD_WIDTH), F32)),
        grid=(nbatch, nc),
        in_specs=[wide,
                  pl.BlockSpec((q, SSD_CONV_DIM), lambda b, c: (rowblk(b, c), 0)),
                  pl.BlockSpec((q, LANES), lambda b, c: (rowblk(b, c), OFF_DT // LANES)),
                  zblk, wide,
                  pl.BlockSpec((SSD_WIDTH, SSD_STATE), lambda b, c: (rowblk(b, c), 0)),
                  vec, vec, wvec, wvec, ANY],
        out_specs=(pl.BlockSpec((q, SSD_CONV_DIM), lambda b, c: (rowblk(b, c), 0)),
                   zblk,
                   pl.BlockSpec((q, LANES), lambda b, c: (rowblk(b, c), 0)),
                   vec, vec, vec, wvec),
        input_output_aliases={10: 1},
        scratch_shapes=[pltpu.VMEM((SSD_WIDTH, SSD_STATE), F32), pltpu.VMEM((SUBLANES, LANES), F32),
                        pltpu.VMEM((SUBLANES, LANES), F32), pltpu.VMEM((SUBLANES, LANES), F32),
                        pltpu.VMEM((SUBLANES, SSD_WIDTH), F32)],
        compiler_params=_params(("arbitrary", "arbitrary")),
    )(dycat, u, proj, proj, y, states, dtb, a_neg, d_x, norm_w, dproj)


def _pad_rows(w, rows):
    return jnp.concatenate([w, jnp.zeros((rows - w.shape[0], w.shape[1]), w.dtype)], axis=0)


def _pad_lanes(v):
    return jnp.concatenate([v, jnp.zeros((LANES - v.shape[0],), v.dtype)]).reshape(1, LANES)


def _padded_from_chips(pieces):
    cols = pieces[0].shape[-1]
    lead = pieces[0].shape[:-1]
    parts, pos = [], 0
    for lo, hi, start in sorted(SECTIONS, key=lambda s: s[2]):
        if start > pos:
            parts.append(jnp.zeros(lead + (start - pos,), pieces[0].dtype))
        pos = start + hi - lo
        while lo < hi:
            p = lo // cols
            end = min(hi, (p + 1) * cols)
            parts.append(pieces[p][..., lo - p * cols:end - p * cols])
            lo = end
    if pos < NP:
        parts.append(jnp.zeros(lead + (NP - pos,), pieces[0].dtype))
    return jnp.concatenate(parts, axis=-1)


def _chip_part_from_padded(wp, p, cols):
    lo, hi = p * cols, (p + 1) * cols
    parts = []
    for rs, re, start in SECTIONS:
        a, b = max(lo, rs), min(hi, re)
        if a < b:
            parts.append(wp[..., start + a - rs:start + b - rs])
    return jnp.concatenate(parts, axis=-1)


def _layer_params(li, w_in_p, w_out, conv_w, dw_w, small):
    return dict(
        w_in_p=w_in_p, w_out=w_out,
        conv_w=_pad_rows(conv_w, SUBLANES), dw_w=_pad_rows(dw_w, 32),
        norm_w=small["norm_w"][li].reshape(1, -1),
        conv_b=small["ssd_conv_b"][li].reshape(1, -1),
        dtb=_pad_lanes(small["ssd_dt_bias"][li]),
        a_neg=_pad_lanes(-jnp.exp(small["ssd_a_log"][li])),
        d_x=jnp.repeat(small["ssd_d"][li], SSD_HEAD_DIM).reshape(1, -1),
        ssd_norm_w=small["ssd_norm_w"][li].reshape(1, -1),
        sinks=_pad_lanes(small["attn_sinks"][li]),
        dw_b=small["conf_dw_b"][li].reshape(1, -1),
        ln_w=small["conf_ln_w"][li].reshape(1, -1),
        ln_b=small["conf_ln_b"][li].reshape(1, -1),
    )


def _layer_fwd(x, p, nbatch, seq, tag, after=None):
    h, h_t = _rmsnorm_fwd(x, p["norm_w"], name=f"rmsnorm_fwd_{tag}", after=after)
    proj = _matmul(h, p["w_in_p"], "nn", F32, 1024, 512, 1024, name=f"proj_fwd_{tag}")
    u = _conv_fwd(proj, OFF_XBC, SSD_CONV_DIM, p["conv_w"], p["conv_b"], SSD_CONV, seq, name=f"ssd_conv_fwd_{tag}")
    ycat = lax.empty((x.shape[0], MIX_WIDTH), MXU_DTYPE)
    y, states, ycat = _ssd_fwd(u, proj, p["dtb"], p["a_neg"], p["d_x"], p["ssd_norm_w"], ycat, nbatch,
                               name=f"ssd_fwd_{tag}")
    ycat, o, lse = _attn_fwd(proj, p["sinks"], ycat, nbatch, name=f"attn_fwd_{tag}")
    c1, ycat = _conf_fwd(proj, p["dw_w"], p["dw_b"], p["ln_w"], p["ln_b"], ycat, seq, name=f"conf_fwd_{tag}")
    x_new = _matmul(ycat, p["w_out"], "nn", F32, 1024, 512, 2048, name=f"out_fwd_{tag}", residual=x)
    return x_new, dict(x=x, h_t=h_t, proj=proj, u=u, y=y, states=states, o=o, lse=lse, c1=c1, ycat=ycat)


def _layer_bwd(dx_out, p, s, nbatch, seq, tag, hooks=None):
    hooks = hooks or {}
    proj = s["proj"]
    dycat = _matmul(dx_out, p["w_out"], "nt", F32, 1024, 1024, 1024, name=f"out_bwd_dy_{tag}",
                    after=hooks.get("start_token"))
    dw_out = _matmul(s["ycat"], dx_out, "tn", F32, 1024, 1024, 1024, name=f"out_bwd_dw_{tag}")
    token = hooks["after_dycat"](dycat) if "after_dycat" in hooks else None
    dtb = p["dtb"] if token is None else p["dtb"] + token[0, 0]
    dproj = lax.empty(proj.shape, MXU_DTYPE)
    du, dproj, ddt, da_log, dd, ddtb, dssd_norm_w = _ssd_bwd(
        dycat, s["u"], proj, s["y"], s["states"], dtb, p["a_neg"], p["d_x"], p["ssd_norm_w"], dproj,
        nbatch, name=f"ssd_bwd_{tag}")
    dproj, dconv_w, dconv_b = _conv_bwd(du, proj, OFF_XBC, SSD_CONV_DIM, p["conv_w"], SSD_CONV, seq,
                                        name=f"ssd_conv_bwd_{tag}", into=dproj)
    dproj, dsinks = _attn_bwd(dycat, proj, s["o"], s["lse"], p["sinks"], ddt, dproj, nbatch,
                              name=f"attn_bwd_{tag}")
    if "after_attn" in hooks:
        hooks["after_attn"](dproj)
    dc1, dproj, dln_w, dln_b = _conf_post_bwd(dycat, s["c1"], proj, p["ln_w"], p["ln_b"], dproj,
                                              name=f"conf_post_bwd_{tag}")
    dproj, ddw_w, ddw_b = _conf_conv_bwd(dc1, proj, p["dw_w"], dproj, seq, name=f"conf_conv_bwd_{tag}")
    dh = _matmul(dproj, p["w_in_p"], "nt", F32, 1024, 1024, 1408, name=f"proj_bwd_dh_{tag}")
    dw_in_p = _matmul(s["h_t"], dproj, "nn", F32, 1024, 512, 4096, name=f"proj_bwd_dw_{tag}")
    dx_in, dnorm_w = _rmsnorm_bwd(dh, s["x"], p["norm_w"], dx_out, name=f"rmsnorm_bwd_{tag}")
    grads = dict(
        norm_w=dnorm_w[0], w_in_p=dw_in_p, ssd_conv_w=dconv_w[:SSD_CONV], ssd_conv_b=dconv_b[0],
        ssd_dt_bias=ddtb[0, :SSD_HEADS], ssd_a_log=da_log[0, :SSD_HEADS], ssd_d=dd[0, :SSD_HEADS],
        ssd_norm_w=dssd_norm_w[0], attn_sinks=dsinks[0, :ATTN_Q_HEADS], conf_dw_w=ddw_w[:CONF_KERNEL],
        conf_dw_b=ddw_b[0], conf_ln_w=dln_w[0], conf_ln_b=dln_b[0], w_out=dw_out)
    return dx_in, grads


def _local_step(x, target, param_fns, final_norm_w, first_after=None, on_grads=None):
    nbatch, seq, d = x.shape
    xt = x.reshape(nbatch * seq, d)
    saved, layer_params = [], []
    for li, fn in enumerate(param_fns):
        p = fn(xt)
        layer_params.append(p)
        xt, s = _layer_fwd(xt, p, nbatch, seq, f"l{li}", after=first_after if li == 0 else None)
        saved.append(s)
    loss, dx, dfinal = _loss_head(xt, target.reshape(nbatch * seq, d), final_norm_w.reshape(1, d), name="loss_head")
    grads = [None] * len(layer_params)
    hooks = None
    for li in reversed(range(len(layer_params))):
        dx, grads[li] = _layer_bwd(dx, layer_params[li], saved[li], nbatch, seq, f"l{li}", hooks=hooks)
        hooks = on_grads(li, grads[li]) if on_grads is not None else None
    return loss[0, 0], dx.reshape(nbatch, seq, d), grads, dfinal[0]


MESH = pl.DeviceIdType.MESH
N_CHIPS = 4


def _mesh_pos():
    return lax.axis_index("x"), lax.axis_index("y"), lax.axis_index("c")


def _other_chips(x, y):
    return [(1 - x, y), (x, 1 - y), (1 - x, 1 - y)]


def _gather_weights(big, small, name):
    nbig, nsmall = len(big), len(small)
    n_ici = 3 * (nbig + nsmall)
    n_fwd = 3 * nbig

    def body(*refs):
        ins = refs[:nbig + nsmall]
        outs = refs[nbig + nsmall:2 * (nbig + nsmall)]
        send_sems, recv_sems = refs[2 * (nbig + nsmall):]
        x, y, c = _mesh_pos()
        me = 2 * x + y
        sibling = (x, y, 1 - c)
        chips = _other_chips(x, y)

        def ici(a, j, origin, dest):
            if a < nbig:
                src = ins[a].at[c] if origin is None else outs[a].at[origin, c]
                dst = outs[a].at[me if origin is None else origin, c]
            else:
                src = ins[a] if origin is None else outs[a].at[origin]
                dst = outs[a].at[me if origin is None else origin]
            k = a * 3 + j
            return pltpu.make_async_remote_copy(src_ref=src, dst_ref=dst, send_sem=send_sems.at[k],
                                                recv_sem=recv_sems.at[k], device_id=dest, device_id_type=MESH)

        def fwd(a, j, origin, half):
            k = n_ici + a * 3 + j
            ref = outs[a].at[origin, half]
            return pltpu.make_async_remote_copy(src_ref=ref, dst_ref=ref, send_sem=send_sems.at[k],
                                                recv_sem=recv_sems.at[k], device_id=sibling, device_id_type=MESH)

        sends = []
        for j, (px, py) in enumerate(chips):
            for a in range(nbig + nsmall):
                cp = ici(a, j, None, (px, py, c))
                cp.start()
                sends.append(cp)
        for j, (px, py) in enumerate(chips):
            origin = 2 * px + py
            for a in range(nbig):
                ici(a, j, origin, (px, py, c)).wait_recv()
                cp = fwd(a, j, origin, c)
                cp.start()
                sends.append(cp)
        for j, (px, py) in enumerate(chips):
            origin = 2 * px + py
            for a in range(nbig, nbig + nsmall):
                ici(a, j, origin, (px, py, c)).wait_recv()
            for a in range(nbig):
                fwd(a, j, origin, 1 - c).wait_recv()
        for cp in sends:
            cp.wait_send()

    out_shape = tuple(jax.ShapeDtypeStruct((N_CHIPS,) + a.shape, a.dtype) for a in list(big) + list(small))
    return pl.pallas_call(
        body, name=name, out_shape=out_shape,
        in_specs=[ANY] * (nbig + nsmall), out_specs=tuple([ANY] * (nbig + nsmall)),
        scratch_shapes=[pltpu.SemaphoreType.DMA((n_ici + n_fwd,)), pltpu.SemaphoreType.DMA((n_ici + n_fwd,))],
    )(*big, *small)


def _pair_swap_halves(arrs, name):
    n = len(arrs)

    def body(*refs):
        ins, outs = refs[:n], refs[n:2 * n]
        send_sems, recv_sems = refs[2 * n:]
        x, y, c = _mesh_pos()
        cps = [pltpu.make_async_remote_copy(src_ref=ins[a].at[:, 1 - c], dst_ref=outs[a], send_sem=send_sems.at[a],
                                            recv_sem=recv_sems.at[a], device_id=(x, y, 1 - c), device_id_type=MESH)
               for a in range(n)]
        for cp in cps:
            cp.start()
        for cp in cps:
            cp.wait()

    return pl.pallas_call(
        body, name=name,
        out_shape=tuple(jax.ShapeDtypeStruct(a.shape[:1] + a.shape[2:], a.dtype) for a in arrs),
        in_specs=[ANY] * n, out_specs=tuple([ANY] * n),
        scratch_shapes=[pltpu.SemaphoreType.DMA((n,)), pltpu.SemaphoreType.DMA((n,))],
    )(*arrs)


HBM = pl.BlockSpec(memory_space=pltpu.HBM)
SEM = pl.BlockSpec(memory_space=pltpu.SEMAPHORE)
DATAFLOW = pltpu.SideEffectType.DATAFLOW_SIDE_EFFECTING


def _split_peers(pattern, x, y, c):
    if pattern == "swap":
        return [((x, y, 1 - c), 1 - c, None, None)]
    me = 2 * x + y
    return [((px, py, c), 2 * px + py if pattern == "scatter" else None, me, 2 * px + py)
            for px, py in _other_chips(x, y)]


def _split_land_shape(pattern, shape):
    return {"bcast": (N_CHIPS,) + shape, "scatter": shape, "swap": shape[:1] + shape[2:]}[pattern]


def _split_copies(pattern, srcs, lands, send_sems, recv_sems, waiting):
    x, y, c = _mesh_pos()
    peers = _split_peers(pattern, x, y, c)
    cps = []
    for j, (dev, src_slot, dst_slot, my_slot) in enumerate(peers):
        for a in range(len(srcs)):
            if src_slot is None:
                src = srcs[a]
            else:
                src = srcs[a].at[:, src_slot] if pattern == "swap" else srcs[a].at[src_slot]
            slot = my_slot if waiting else dst_slot
            dst = lands[a] if slot is None else lands[a].at[slot]
            k = a * len(peers) + j
            cps.append(pltpu.make_async_remote_copy(src_ref=src, dst_ref=dst, send_sem=send_sems[k],
                                                    recv_sem=recv_sems[k], device_id=dev, device_id_type=MESH))
    return cps


def _split_start(arrs, pattern, after, name):
    n = len(arrs)
    nsem = n * (1 if pattern == "swap" else N_CHIPS - 1)

    def body(*refs):
        srcs, lands = refs[:n], refs[n:2 * n]
        outs = refs[2 * n + 1:]
        for cp in _split_copies(pattern, srcs, lands, outs[:nsem], outs[nsem:2 * nsem], waiting=False):
            cp.start()
        outs[-1][...] = jnp.zeros_like(outs[-1])

    lands = [lax.empty(_split_land_shape(pattern, a.shape), a.dtype) for a in arrs]
    out_shape = ([pltpu.SemaphoreType.DMA(())] * (2 * nsem)
                 + [pltpu.HBM(a.shape, a.dtype) for a in arrs] + [pltpu.HBM(b.shape, b.dtype) for b in lands]
                 + [jax.ShapeDtypeStruct((SUBLANES, LANES), F32)])
    outs = pl.pallas_call(
        body, name=name, out_shape=tuple(out_shape),
        in_specs=[HBM] * (2 * n) + [ANY],
        out_specs=tuple([SEM] * (2 * nsem) + [HBM] * (2 * n) + [pl.BlockSpec(memory_space=pltpu.VMEM)]),
        input_output_aliases={a: 2 * nsem + a for a in range(2 * n)},
        compiler_params=pltpu.CompilerParams(has_side_effects=DATAFLOW),
    )(*[pltpu.with_memory_space_constraint(a, pltpu.HBM) for a in list(arrs) + lands], after)
    return outs[:-1], outs[-1]


def _split_wait(state, n, pattern, after, name):
    nsem = n * (1 if pattern == "swap" else N_CHIPS - 1)

    def body(*refs):
        srcs, lands = refs[:n], refs[n:2 * n]
        send_sems, recv_sems = refs[2 * n:2 * n + nsem], refs[2 * n + nsem:2 * n + 2 * nsem]
        for cp in _split_copies(pattern, srcs, lands, send_sems, recv_sems, waiting=True):
            cp.wait_send()
            cp.wait_recv()

    sems, thru = state[:2 * nsem], state[2 * nsem:]
    outs = pl.pallas_call(
        body, name=name, out_shape=tuple(pltpu.HBM(a.shape, a.dtype) for a in thru),
        in_specs=[HBM] * (2 * n) + [SEM] * (2 * nsem) + [ANY],
        out_specs=tuple([HBM] * (2 * n)),
        input_output_aliases={a: a for a in range(2 * n)},
        compiler_params=pltpu.CompilerParams(has_side_effects=DATAFLOW),
    )(*thru, *sems, after)
    return outs[n:]


def _pair_gather(arrs, layer, name):
    n = len(arrs)

    def body(*refs):
        outs = refs[n:2 * n]
        send_sems, recv_sems = refs[2 * n:]
        x, y, c = _mesh_pos()
        cps = [pltpu.make_async_remote_copy(src_ref=outs[a].at[layer, c], dst_ref=outs[a].at[layer, c],
                                            send_sem=send_sems.at[a], recv_sem=recv_sems.at[a],
                                            device_id=(x, y, 1 - c), device_id_type=MESH)
               for a in range(n)]
        for cp in cps:
            cp.start()
        for cp in cps:
            cp.wait()

    return pl.pallas_call(
        body, name=name, out_shape=tuple(jax.ShapeDtypeStruct(a.shape, a.dtype) for a in arrs),
        in_specs=[ANY] * n, out_specs=tuple([ANY] * n),
        input_output_aliases={a: a for a in range(n)},
        scratch_shapes=[pltpu.SemaphoreType.DMA((n,)), pltpu.SemaphoreType.DMA((n,))],
    )(*arrs)


N_DEV = 8


def _allreduce_small(pack, name):
    r = pack.shape[0]

    def body(p_ref, o_ref, land, send_sems, recv_sems):
        x, y, c = _mesh_pos()
        me = 4 * x + 2 * y + c
        cps = []
        for k in range(1, N_DEV):
            peer = (x ^ (k >> 2), y ^ ((k >> 1) & 1), c ^ (k & 1))
            cps.append(pltpu.make_async_remote_copy(src_ref=p_ref, dst_ref=land.at[me], send_sem=send_sems.at[k - 1],
                                                    recv_sem=recv_sems.at[k - 1], device_id=peer, device_id_type=MESH))
        for cp in cps:
            cp.start()
        land[me] = p_ref[...]
        for cp in cps:
            cp.wait()
        total = land[0]
        for d in range(1, N_DEV):
            total = total + land[d]
        o_ref[...] = total

    vm = pl.BlockSpec(memory_space=pltpu.VMEM)
    return pl.pallas_call(
        body, name=name, out_shape=jax.ShapeDtypeStruct(pack.shape, F32),
        in_specs=[vm], out_specs=vm,
        scratch_shapes=[pltpu.VMEM((N_DEV, r, LANES), F32), pltpu.SemaphoreType.DMA((N_DEV - 1,)),
                        pltpu.SemaphoreType.DMA((N_DEV - 1,))],
    )(pack)


BIG_ROWS = 128


def _cast_layer(w, layer, name):
    _, r, cdim = w.shape
    tr = BIG_ROWS

    def body(w_ref, o_ref):
        o_ref[...] = w_ref[...].astype(o_ref.dtype)

    return pl.pallas_call(
        body, name=name, out_shape=jax.ShapeDtypeStruct((r, cdim), MXU_DTYPE),
        grid=(r // tr,), in_specs=[pl.BlockSpec((None, tr, cdim), lambda i: (layer, i, 0))],
        out_specs=pl.BlockSpec((tr, cdim), lambda i: (i, 0)),
        compiler_params=_params(("parallel",)),
    )(w)


def _pair_sum(parts, sib, which, out_dtype, name):
    k, _, r, cdim = parts.shape
    tr = BIG_ROWS

    def body(sel_ref, p_ref, s_ref, o_ref):
        o_ref[...] = (p_ref[...] + s_ref[...]).astype(o_ref.dtype)

    grid_spec = pltpu.PrefetchScalarGridSpec(
        num_scalar_prefetch=1, grid=(k, r // tr),
        in_specs=[pl.BlockSpec((None, None, tr, cdim), lambda l, i, sel: (l, sel[0], i, 0)),
                  pl.BlockSpec((None, tr, cdim), lambda l, i, sel: (l, i, 0))],
        out_specs=pl.BlockSpec((None, tr, cdim), lambda l, i, sel: (l, i, 0)))
    return pl.pallas_call(
        body, name=name, out_shape=jax.ShapeDtypeStruct((k, r, cdim), out_dtype), grid_spec=grid_spec,
        compiler_params=_params(("parallel", "parallel")),
    )(which.reshape(1).astype(jnp.int32), parts, sib)


def _sum_lead(parts, into, layer, which, name):
    k, r, cdim = parts.shape
    tr = BIG_ROWS

    def body(sel_ref, p_ref, _, o_ref):
        total = p_ref[0].astype(F32)
        for a in range(1, k):
            total = total + p_ref[a].astype(F32)
        o_ref[...] = total

    grid_spec = pltpu.PrefetchScalarGridSpec(
        num_scalar_prefetch=1, grid=(r // tr,),
        in_specs=[pl.BlockSpec((k, tr, cdim), lambda i, sel: (0, i, 0)), ANY],
        out_specs=pl.BlockSpec((None, None, tr, cdim), lambda i, sel: (layer, sel[0], i, 0)))
    return pl.pallas_call(
        body, name=name, out_shape=jax.ShapeDtypeStruct(into.shape, F32), grid_spec=grid_spec,
        input_output_aliases={2: 0},
        compiler_params=_params(("parallel",)),
    )(which.reshape(1).astype(jnp.int32), parts, into)


def _adam_math(w, g, m, v):
    m2 = ADAM_B1 * m + (1.0 - ADAM_B1) * g
    v2 = ADAM_B2 * v + (1.0 - ADAM_B2) * (g * g)
    m_hat = m2 / (1.0 - ADAM_B1 ** ADAM_STEP)
    v_hat = v2 / (1.0 - ADAM_B2 ** ADAM_STEP)
    delta = -ADAM_LR * (m_hat / (jnp.sqrt(v_hat) + ADAM_EPS) + ADAM_WD * w)
    return delta, m2, v2


def _adam_big(w, g, m, v, name):
    nl, r, cdim = w.shape
    tr = BIG_ROWS

    def body(w_ref, g_ref, m_ref, v_ref, d_ref, mo_ref, vo_ref):
        delta, m2, v2 = _adam_math(w_ref[...], g_ref[...], m_ref[...], v_ref[...])
        d_ref[...] = delta
        mo_ref[...] = m2
        vo_ref[...] = v2

    blk = pl.BlockSpec((None, tr, cdim), lambda l, i: (l, i, 0))
    shp = jax.ShapeDtypeStruct(w.shape, F32)
    return pl.pallas_call(
        body, name=name, out_shape=(shp, shp, shp),
        grid=(nl, r // tr), in_specs=[blk] * 4, out_specs=(blk, blk, blk),
        compiler_params=_params(("parallel", "parallel")),
    )(w, g, m, v)


def _adam_cols_major(w, g, m, v, name):
    cdim, nl, r = w.shape
    tc = BIG_ROWS

    def body(w_ref, g_ref, m_ref, v_ref, d_ref, mo_ref, vo_ref):
        delta, m2, v2 = _adam_math(w_ref[...], g_ref[...], m_ref[...], v_ref[...])
        d_ref[...] = delta
        mo_ref[...] = m2
        vo_ref[...] = v2

    blk = pl.BlockSpec((tc, nl, r), lambda i: (i, 0, 0))
    shp = jax.ShapeDtypeStruct(w.shape, F32)
    return pl.pallas_call(
        body, name=name, out_shape=(shp, shp, shp),
        grid=(pl.cdiv(cdim, tc),), in_specs=[blk] * 4, out_specs=(blk, blk, blk),
        compiler_params=_params(("parallel",)),
    )(w, g, m, v)


def _adam_small(ws, gs, ms, vs, name):
    n = len(ws)

    def body(*refs):
        w_refs, g_refs, m_refs, v_refs = (refs[k * n:(k + 1) * n] for k in range(4))
        d_refs, mo_refs, vo_refs = (refs[(4 + k) * n:(5 + k) * n] for k in range(3))
        for a in range(n):
            delta, m2, v2 = _adam_math(w_refs[a][...], g_refs[a][...], m_refs[a][...], v_refs[a][...])
            d_refs[a][...] = delta
            mo_refs[a][...] = m2
            vo_refs[a][...] = v2

    shapes = tuple(jax.ShapeDtypeStruct(w.shape, F32) for w in ws)
    vm = pl.BlockSpec(memory_space=pltpu.VMEM)
    outs = pl.pallas_call(body, name=name, out_shape=shapes * 3, in_specs=[vm] * (4 * n),
                          out_specs=tuple([vm] * (3 * n)))(*ws, *gs, *ms, *vs)
    return outs[:n], outs[n:2 * n], outs[2 * n:]


PACK_TILE = SUBLANES * LANES


def _pack(arrays):
    rows = []
    for a in arrays:
        flat = a.reshape(-1)
        pad = (-flat.shape[0]) % PACK_TILE
        if pad:
            flat = jnp.concatenate([flat, jnp.zeros((pad,), flat.dtype)])
        rows.append(flat.reshape(-1, LANES))
    return jnp.concatenate(rows, axis=0)


def _unpack(pack, shapes):
    outs, row = [], 0
    for shp in shapes:
        n = int(np.prod(shp))
        nrows = -(-n // PACK_TILE) * SUBLANES
        outs.append(pack[row:row + nrows].reshape(-1)[:n].reshape(shp))
        row += nrows
    return outs


SMALL = ["norm_w", "ssd_conv_b", "ssd_dt_bias", "ssd_a_log", "ssd_d", "ssd_norm_w", "attn_sinks",
         "conf_dw_b", "conf_ln_w", "conf_ln_b"]
WEIGHTS = ["norm_w", "w_in", "ssd_conv_w", "ssd_conv_b", "ssd_dt_bias", "ssd_a_log", "ssd_d", "ssd_norm_w",
           "attn_sinks", "conf_dw_w", "conf_dw_b", "conf_ln_w", "conf_ln_b", "w_out", "final_norm_w"]


def kernel(x, norm_w, w_in, ssd_conv_w, ssd_conv_b, ssd_dt_bias, ssd_a_log, ssd_d, ssd_norm_w, attn_sinks, conf_dw_w, conf_dw_b, conf_ln_w, conf_ln_b, w_out, final_norm_w, loss_target, m_norm_w, m_w_in, m_ssd_conv_w, m_ssd_conv_b, m_ssd_dt_bias, m_ssd_a_log, m_ssd_d, m_ssd_norm_w, m_attn_sinks, m_conf_dw_w, m_conf_dw_b, m_conf_ln_w, m_conf_ln_b, m_w_out, m_final_norm_w, v_norm_w, v_w_in, v_ssd_conv_w, v_ssd_conv_b, v_ssd_dt_bias, v_ssd_a_log, v_ssd_d, v_ssd_norm_w, v_attn_sinks, v_conf_dw_w, v_conf_dw_b, v_conf_ln_w, v_conf_ln_b, v_w_out, v_final_norm_w):
    w = dict(norm_w=norm_w, w_in=w_in, ssd_conv_w=ssd_conv_w, ssd_conv_b=ssd_conv_b, ssd_dt_bias=ssd_dt_bias,
             ssd_a_log=ssd_a_log, ssd_d=ssd_d, ssd_norm_w=ssd_norm_w, attn_sinks=attn_sinks, conf_dw_w=conf_dw_w,
             conf_dw_b=conf_dw_b, conf_ln_w=conf_ln_w, conf_ln_b=conf_ln_b, w_out=w_out, final_norm_w=final_norm_w)
    m = dict(norm_w=m_norm_w, w_in=m_w_in, ssd_conv_w=m_ssd_conv_w, ssd_conv_b=m_ssd_conv_b,
             ssd_dt_bias=m_ssd_dt_bias, ssd_a_log=m_ssd_a_log, ssd_d=m_ssd_d, ssd_norm_w=m_ssd_norm_w,
             attn_sinks=m_attn_sinks, conf_dw_w=m_conf_dw_w, conf_dw_b=m_conf_dw_b, conf_ln_w=m_conf_ln_w,
             conf_ln_b=m_conf_ln_b, w_out=m_w_out, final_norm_w=m_final_norm_w)
    v = dict(norm_w=v_norm_w, w_in=v_w_in, ssd_conv_w=v_ssd_conv_w, ssd_conv_b=v_ssd_conv_b,
             ssd_dt_bias=v_ssd_dt_bias, ssd_a_log=v_ssd_a_log, ssd_d=v_ssd_d, ssd_norm_w=v_ssd_norm_w,
             attn_sinks=v_attn_sinks, conf_dw_w=v_conf_dw_w, conf_dw_b=v_conf_dw_b, conf_ln_w=v_conf_ln_w,
             conf_ln_b=v_conf_ln_b, w_out=v_w_out, final_norm_w=v_final_norm_w)
    depth = w_in.shape[0]
    me = 2 * lax.axis_index("x") + lax.axis_index("y")

    assert depth == 2
    w_in_b = [_cast_layer(w_in, li, name=f"cast_w_in_l{li}") for li in range(depth)]
    w_out_b = [_cast_layer(w_out, li, name=f"cast_w_out_l{li}") for li in range(depth)]
    own0 = [w_in_b[0].reshape((2, -1) + w_in_b[0].shape[1:]), w_out_b[0].reshape((2, -1) + w_out_b[0].shape[1:]),
            ssd_conv_w, conf_dw_w]
    gathered0 = _gather_weights(own0[:2], own0[2:], name="gather_weights_l0")
    g_in0, g_out0, g_conv, g_dw = [lax.dynamic_update_index_in_dim(g_all, mine, me, 0)
                                   for g_all, mine in zip(gathered0, own0)]
    own1 = [w_in_b[1], w_out_b[1]]
    pending1, token1 = _split_start(own1, "bcast", gathered0[0], name="gather_l1_start")

    def small_full(li):
        return (jnp.concatenate([g_conv[p, li] for p in range(N_CHIPS)], axis=1),
                jnp.concatenate([g_dw[p, li] for p in range(N_CHIPS)], axis=1))

    def params_l0(_):
        w_in_p = _padded_from_chips([g_in0[p].reshape(w_in_b[0].shape) for p in range(N_CHIPS)])
        w_out_full = g_out0.reshape(-1, g_out0.shape[-1])
        return _layer_params(0, w_in_p, w_out_full, *small_full(0), w)

    def params_l1(layer_input):
        landed = _split_wait(pending1, len(own1), "bcast", layer_input, name="gather_l1_wait")
        g_in1, g_out1 = [lax.dynamic_update_index_in_dim(g_all, mine, me, 0) for g_all, mine in zip(landed, own1)]
        w_in_p = _padded_from_chips([g_in1[p] for p in range(N_CHIPS)])
        return _layer_params(1, w_in_p, g_out1.reshape(-1, g_out1.shape[-1]), *small_full(1), w)

    c = lax.axis_index("c")
    cols = w_in.shape[2]
    rows_out = w_out.shape[1]

    def grad_parts(g):
        dw = g["w_in_p"]
        p_in = jnp.stack([_chip_part_from_padded(dw, p, cols) for p in range(N_CHIPS)])
        return [p_in.reshape(N_CHIPS, 2, dw.shape[0] // 2, cols),
                g["w_out"].reshape(N_CHIPS, 2, rows_out // 2, D_MODEL)]

    def pair_sums(parts, sib, tag):
        return [_pair_sum(p, sb, c, MXU_DTYPE, name=f"grad_pair_sum_{k}_{tag}")
                for k, (p, sb) in enumerate(zip(parts, sib))]

    split = {"reduced": [lax.empty((depth, 2, w_in.shape[1] // 2, cols), F32),
                         lax.empty((depth, 2, rows_out // 2, D_MODEL), F32)]}

    def chip_sums(landed, sent, li):
        filled = [lax.dynamic_update_index_in_dim(r, lax.dynamic_index_in_dim(sk, me, 0, keepdims=False), me, 0)
                  for r, sk in zip(landed, sent)]
        halves = [_sum_lead(r, into, li, c, name=f"grad_chip_sum_{k}_l{li}")
                  for k, (r, into) in enumerate(zip(filled, split["reduced"]))]
        split["reduced"] = list(_pair_gather(halves, li, name=f"grad_pair_gather_l{li}"))

    def on_grads(li, g):
        if li != depth - 1:
            return None
        parts = grad_parts(g)
        swap_state, swap_token = _split_start(parts, "swap", g["w_out"], name="grad_swap_l1_start")

        def after_dycat(dycat):
            sib = _split_wait(swap_state, len(parts), "swap", dycat, name="grad_swap_l1_wait")
            split["sent"] = pair_sums(parts, sib, "l1")
            split["scatter"], token = _split_start(split["sent"], "scatter", split["sent"][0],
                                                   name="grad_scatter_l1_start")
            return token

        def after_attn(dproj):
            landed = _split_wait(split["scatter"], len(parts), "scatter", dproj, name="grad_scatter_l1_wait")
            chip_sums(landed, split["sent"], depth - 1)

        return {"start_token": swap_token, "after_dycat": after_dycat, "after_attn": after_attn}

    loss, grad_x, grads, dfinal = _local_step(x, loss_target, [params_l0, params_l1], final_norm_w,
                                              first_after=token1, on_grads=on_grads)

    parts0 = grad_parts(grads[0])
    sent0 = pair_sums(parts0, _pair_swap_halves(parts0, name="grad_pair_swap_l0"), "l0")
    scatter0, token0 = _split_start(sent0, "scatter", sent0[0], name="grad_scatter_l0_start")

    small_list = [grads[li][n] for li in range(depth) for n in SMALL]
    small_list += [grads[li][n] for li in range(depth) for n in ("ssd_conv_w", "conf_dw_w")]
    small_list += [dfinal, loss.reshape(1)]
    small_shapes = [a.shape for a in small_list]
    reduced = _unpack(_allreduce_small(_pack(small_list) + token0[0, 0], name="allreduce_small"), small_shapes)
    ns = len(SMALL)
    g = {n: jnp.stack([reduced[li * ns + i] for li in range(depth)]) for i, n in enumerate(SMALL)}
    conv_w_cols, dw_w_cols = ssd_conv_w.shape[2], conf_dw_w.shape[2]
    g["ssd_conv_w"] = jnp.stack([lax.dynamic_slice_in_dim(reduced[depth * ns + 2 * li], me * conv_w_cols,
                                                          conv_w_cols, axis=1) for li in range(depth)])
    g["conf_dw_w"] = jnp.stack([lax.dynamic_slice_in_dim(reduced[depth * ns + 2 * li + 1], me * dw_w_cols,
                                                         dw_w_cols, axis=1) for li in range(depth)])
    g["final_norm_w"] = reduced[-2]
    loss_total = reduced[-1][0]

    small_names = [n for n in WEIGHTS if n not in ("w_in", "w_out")]

    def as2d(a):
        return a.reshape(1, -1) if a.ndim == 1 else a

    deltas, new_ms, new_vs = _adam_small(*[[as2d(src[n]) for n in small_names] for src in (w, g, m, v)],
                                         name="adam_small")

    chip_sums(_split_wait(scatter0, len(sent0), "scatter", deltas[0], name="grad_scatter_l0_wait"), sent0, 0)
    g_w_in = split["reduced"][0].reshape(w_in.shape)
    g_w_out = split["reduced"][1].reshape(w_out.shape)

    outs_g, outs_d, outs_m, outs_v = {"w_in": g_w_in, "w_out": g_w_out}, {}, {}, {}
    to_cols, from_cols = (2, 0, 1), (1, 2, 0)
    outs_d["w_in"], outs_m["w_in"], outs_v["w_in"] = [
        jnp.transpose(a, from_cols) for a in _adam_cols_major(
            *[jnp.transpose(a, to_cols) for a in (w_in, g_w_in, m_w_in, v_w_in)], name="adam_w_in")]
    outs_d["w_out"], outs_m["w_out"], outs_v["w_out"] = _adam_big(w_out, g_w_out, m_w_out, v_w_out,
                                                                  name="adam_w_out")
    for n, dn, mn, vn in zip(small_names, deltas, new_ms, new_vs):
        outs_g[n], outs_d[n], outs_m[n], outs_v[n] = (g[n], dn.reshape(w[n].shape), mn.reshape(w[n].shape),
                                                      vn.reshape(w[n].shape))
    return (loss_total, grad_x, *[outs_g[n] for n in WEIGHTS], *[outs_d[n] for n in WEIGHTS],
            *[outs_m[n] for n in WEIGHTS], *[outs_v[n] for n in WEIGHTS])
```

```python
import functools
import math

import jax
import jax.numpy as jnp
import numpy as np
from jax import lax
from jax.experimental import pallas as pl
from jax.experimental.pallas import tpu as pltpu

F32 = jnp.float32
BF16 = jnp.bfloat16
MXU_DTYPE = BF16

D_MODEL = 1024
DEPTH = 2
SSD_HEADS = 16
SSD_HEAD_DIM = 64
SSD_STATE = 128
SSD_CONV = 4
CHUNK = 128
SSD_CONV_DIM = 1536
ATTN_HEAD_DIM = 64
ATTN_Q_HEADS = 8
WINDOW = 128
CONF_WIDTH = 512
CONF_KERNEL = 31
MIX_WIDTH = 2048
D_IN_PROJ = 5392
EPS = 1e-5

ADAM_LR = 0.001
ADAM_B1 = 0.9
ADAM_B2 = 0.999
ADAM_EPS = 1e-08
ADAM_WD = 0.01
ADAM_STEP = 10

LANES = 128
SUBLANES = 8
VMEM_LIMIT = 48 * 1024 * 1024

NP = 5632
OFF_ZA, OFF_Q, OFF_K, OFF_V, OFF_DT = 0, 512, 1024, 1152, 1280
ATTN_GROUP = 1536
OFF_XBC = 1536
OFF_CONF = 3072
OFF_ZS = 4096
OFF_ZC = 5120
SECTIONS = ((0, 1024, OFF_ZS), (1024, 1536, OFF_ZA), (1536, 2048, OFF_ZC), (2048, 3584, OFF_XBC),
            (3584, 3600, OFF_DT), (3600, 4368, OFF_Q), (4368, 5392, OFF_CONF))

YCAT_ATTN, YCAT_CONF = 1024, 1536
ANY = pl.BlockSpec(memory_space=pl.ANY)

NN = (((1,), (0,)), ((), ()))
NT = (((1,), (1,)), ((), ()))
TN = (((0,), (0,)), ((), ()))


def _params(sem):
    return pltpu.CompilerParams(dimension_semantics=sem, vmem_limit_bytes=VMEM_LIMIT)


def _dot(a, b, dims=NN):
    return lax.dot_general(a.astype(MXU_DTYPE), b.astype(MXU_DTYPE), dims, preferred_element_type=F32)


def _split_bf16(a, passes):
    pieces = []
    r = a
    for _ in range(passes):
        p = r.astype(BF16)
        pieces.append(p)
        r = r - p.astype(F32)
    return pieces


def _xdot(a, sel, dims=NN, passes=2):
    out = None
    for p in _split_bf16(a, passes):
        t = lax.dot_general(p, sel, dims, preferred_element_type=F32)
        out = t if out is None else out + t
    return out


def _xdot_r(sel, b, dims=NN, passes=3):
    out = None
    for p in _split_bf16(b, passes):
        t = lax.dot_general(sel, p, dims, preferred_element_type=F32)
        out = t if out is None else out + t
    return out


def _sigmoid(x):
    return 1.0 / (1.0 + jnp.exp(-x))


def _silu(x):
    return x * _sigmoid(x)


def _dsilu(x):
    s = _sigmoid(x)
    return s * (1.0 + x * (1.0 - s))


def _softplus(x):
    return jnp.maximum(x, 0.0) + jnp.log(1.0 + jnp.exp(-jnp.abs(x)))


def _rowsum8(x):
    r, c = x.shape
    return jnp.sum(x.reshape(r // SUBLANES, SUBLANES, c), axis=0)


def _iota(shape, dim):
    return lax.broadcasted_iota(jnp.int32, shape, dim)


def _matmul(a, b, form, out_dtype, tm, tn, tk, name, residual=None, after=None):
    if form == "nn":
        (m, k), n = a.shape, b.shape[1]
    elif form == "nt":
        (m, k), n = a.shape, b.shape[0]
    else:
        (k, m), n = a.shape, b.shape[1]
    tm, tn, tk = min(tm, m), min(tn, n), min(tk, k)
    assert m % tm == 0 and n % tn == 0 and k % tk == 0, (name, m, n, k, tm, tn, tk)
    if form == "nn":
        a_spec = pl.BlockSpec((tm, tk), lambda i, j, s: (i, s))
        b_spec = pl.BlockSpec((tk, tn), lambda i, j, s: (s, j))
        dims = NN
    elif form == "nt":
        (m, k), n = a.shape, b.shape[0]
        a_spec = pl.BlockSpec((tm, tk), lambda i, j, s: (i, s))
        b_spec = pl.BlockSpec((tn, tk), lambda i, j, s: (j, s))
        dims = NT
    else:
        (k, m), n = a.shape, b.shape[1]
        a_spec = pl.BlockSpec((tk, tm), lambda i, j, s: (s, i))
        b_spec = pl.BlockSpec((tk, tn), lambda i, j, s: (s, j))
        dims = TN
    nk = k // tk
    has_res = residual is not None
    deps = [] if after is None else [after]

    def body_single(a_ref, b_ref, *rest):
        o = _dot(a_ref[...], b_ref[...], dims)
        if has_res:
            o = o + rest[0][...]
        rest[-1][...] = o.astype(out_dtype)

    def body(a_ref, b_ref, *rest):
        r_ref = rest[0] if has_res else None
        o_ref, acc = rest[-2:]
        s = pl.program_id(2)

        @pl.when(s == 0)
        def _():
            acc[...] = jnp.zeros_like(acc)

        acc[...] += _dot(a_ref[...], b_ref[...], dims)

        @pl.when(s == nk - 1)
        def _():
            o = acc[...]
            if has_res:
                o = o + r_ref[...]
            o_ref[...] = o.astype(out_dtype)

    in_specs = [a_spec, b_spec]
    args = [a, b]
    if has_res:
        in_specs.append(pl.BlockSpec((tm, tn), lambda i, j, s: (i, j)))
        args.append(residual)
    in_specs += [ANY] * len(deps)
    args += deps
    return pl.pallas_call(
        body_single if nk == 1 else body, name=name,
        out_shape=jax.ShapeDtypeStruct((m, n), out_dtype),
        grid=(m // tm, n // tn, nk),
        in_specs=in_specs,
        out_specs=pl.BlockSpec((tm, tn), lambda i, j, s: (i, j)),
        scratch_shapes=[] if nk == 1 else [pltpu.VMEM((tm, tn), F32)],
        compiler_params=_params(("parallel", "parallel", "arbitrary")),
    )(*args)


ROW_TILE = 256


def _rmsnorm_fwd(x, w, name, after=None):
    t, d = x.shape
    tm = ROW_TILE
    deps = [] if after is None else [after]

    def body(x_ref, w_ref, *rest):
        o_ref, ot_ref = rest[len(deps):]
        xv = x_ref[...]
        rstd = lax.rsqrt(jnp.mean(xv * xv, axis=-1, keepdims=True) + EPS)
        h = xv * rstd * w_ref[...]
        o_ref[...] = h.astype(o_ref.dtype)
        ot_ref[...] = h.T.astype(ot_ref.dtype)

    return pl.pallas_call(
        body, name=name,
        out_shape=(jax.ShapeDtypeStruct((t, d), MXU_DTYPE), jax.ShapeDtypeStruct((d, t), MXU_DTYPE)),
        grid=(t // tm,),
        in_specs=[pl.BlockSpec((tm, d), lambda i: (i, 0)), pl.BlockSpec((1, d), lambda i: (0, 0))]
        + [ANY] * len(deps),
        out_specs=(pl.BlockSpec((tm, d), lambda i: (i, 0)), pl.BlockSpec((d, tm), lambda i: (0, i))),
        compiler_params=_params(("parallel",)),
    )(x, w, *deps)


PROJ_BWD_TM, PROJ_BWD_TK = 1024, 1408


def _proj_bwd_dx(dproj, w_in_p, x, w, dres, name):
    t, d = x.shape
    kdim = dproj.shape[1]
    tm, tk = min(PROJ_BWD_TM, t), PROJ_BWD_TK
    nt, nk = t // tm, kdim // tk
    assert t % tm == 0 and kdim % tk == 0

    def body(a_ref, b_ref, x_ref, w_ref, dr_ref, dx_ref, dw_ref, acc, wacc):
        i, s = pl.program_id(0), pl.program_id(1)

        @pl.when((i == 0) & (s == 0))
        def _():
            wacc[...] = jnp.zeros_like(wacc)

        @pl.when(s == 0)
        def _():
            acc[...] = jnp.zeros_like(acc)

        acc[...] += _dot(a_ref[...], b_ref[...], NT)

        @pl.when(s == nk - 1)
        def _():
            xv = x_ref[...]
            rstd = lax.rsqrt(jnp.mean(xv * xv, axis=-1, keepdims=True) + EPS)
            xh = xv * rstd
            dhv = acc[...]
            g = dhv * w_ref[...]
            dx_ref[...] = dr_ref[...] + rstd * (g - xh * jnp.mean(g * xh, axis=-1, keepdims=True))
            wacc[...] += _rowsum8(dhv * xh)

        @pl.when((i == nt - 1) & (s == nk - 1))
        def _():
            dw_ref[...] = jnp.sum(wacc[...], axis=0, keepdims=True)

    row = pl.BlockSpec((tm, d), lambda i, s: (i, 0))
    vec = pl.BlockSpec((1, d), lambda i, s: (0, 0))
    return pl.pallas_call(
        body, name=name,
        out_shape=(jax.ShapeDtypeStruct((t, d), F32), jax.ShapeDtypeStruct((1, d), F32)),
        grid=(nt, nk),
        in_specs=[pl.BlockSpec((tm, tk), lambda i, s: (i, s)), pl.BlockSpec((d, tk), lambda i, s: (0, s)),
                  row, vec, row],
        out_specs=(row, vec),
        scratch_shapes=[pltpu.VMEM((tm, d), F32), pltpu.VMEM((SUBLANES, d), F32)],
        compiler_params=_params(("arbitrary", "arbitrary")),
    )(dproj, w_in_p, x, w, dres)


def _loss_head(xf, target, w, name):
    t, d = xf.shape
    tm = ROW_TILE
    nt = t // tm

    def body(x_ref, t_ref, w_ref, loss_ref, dx_ref, dw_ref, lacc, wacc):
        i = pl.program_id(0)

        @pl.when(i == 0)
        def _():
            lacc[...] = jnp.zeros_like(lacc)
            wacc[...] = jnp.zeros_like(wacc)

        xv = x_ref[...]
        rstd = lax.rsqrt(jnp.mean(xv * xv, axis=-1, keepdims=True) + EPS)
        xh = xv * rstd
        err = xh * w_ref[...] - t_ref[...]
        lacc[...] += jnp.sum(err * err)
        dy = err * (1.0 / d)
        g = dy * w_ref[...]
        dx_ref[...] = rstd * (g - xh * jnp.mean(g * xh, axis=-1, keepdims=True))
        wacc[...] += _rowsum8(dy * xh)

        @pl.when(i == nt - 1)
        def _():
            loss_ref[...] = lacc[...] * (0.5 / d)
            dw_ref[...] = jnp.sum(wacc[...], axis=0, keepdims=True)

    row = pl.BlockSpec((tm, d), lambda i: (i, 0))
    vec = pl.BlockSpec((1, d), lambda i: (0, 0))
    return pl.pallas_call(
        body, name=name,
        out_shape=(jax.ShapeDtypeStruct((SUBLANES, LANES), F32), jax.ShapeDtypeStruct((t, d), F32),
                   jax.ShapeDtypeStruct((1, d), F32)),
        grid=(nt,),
        in_specs=[row, row, vec],
        out_specs=(pl.BlockSpec((SUBLANES, LANES), lambda i: (0, 0)), row, vec),
        scratch_shapes=[pltpu.VMEM((SUBLANES, LANES), F32), pltpu.VMEM((SUBLANES, d), F32)],
        compiler_params=_params(("arbitrary",)),
    )(xf, target, w)


def _conf_post_bwd(dycat, c1, proj, ln_w, ln_b, dproj, name):
    t = c1.shape[0]
    tm, cw = ROW_TILE, CONF_WIDTH
    nt = t // tm

    def body(dy_ref, c_ref, z_ref, w_ref, b_ref, _, dc_ref, dz_ref, dw_ref, db_ref, wacc, bacc):
        i = pl.program_id(0)

        @pl.when(i == 0)
        def _():
            wacc[...] = jnp.zeros_like(wacc)
            bacc[...] = jnp.zeros_like(bacc)

        cv = c_ref[...]
        xc = cv - jnp.mean(cv, axis=-1, keepdims=True)
        rstd = lax.rsqrt(jnp.mean(xc * xc, axis=-1, keepdims=True) + EPS)
        xh = xc * rstd
        c2 = xh * w_ref[...] + b_ref[...]
        zv = z_ref[...]
        dy = dy_ref[...]
        dz_ref[...] = (dy * _silu(c2) * _dsilu(zv)).astype(dz_ref.dtype)
        dc2 = dy * _silu(zv) * _dsilu(c2)
        bacc[...] += _rowsum8(dc2)
        wacc[...] += _rowsum8(dc2 * xh)
        dxh = dc2 * w_ref[...]
        dc_ref[...] = rstd * (dxh - jnp.mean(dxh, axis=-1, keepdims=True)
                              - xh * jnp.mean(dxh * xh, axis=-1, keepdims=True))

        @pl.when(i == nt - 1)
        def _():
            dw_ref[...] = jnp.sum(wacc[...], axis=0, keepdims=True)
            db_ref[...] = jnp.sum(bacc[...], axis=0, keepdims=True)

    row = pl.BlockSpec((tm, cw), lambda i: (i, 0))
    vec = pl.BlockSpec((1, cw), lambda i: (0, 0))
    return pl.pallas_call(
        body, name=name,
        out_shape=(jax.ShapeDtypeStruct((t, cw), F32), jax.ShapeDtypeStruct(dproj.shape, dproj.dtype),
                   jax.ShapeDtypeStruct((1, cw), F32), jax.ShapeDtypeStruct((1, cw), F32)),
        grid=(nt,),
        in_specs=[pl.BlockSpec((tm, cw), lambda i: (i, YCAT_CONF // cw)), row,
                  pl.BlockSpec((tm, cw), lambda i: (i, OFF_ZC // cw)), vec, vec, ANY],
        out_specs=(row, pl.BlockSpec((tm, cw), lambda i: (i, OFF_ZC // cw)), vec, vec),
        input_output_aliases={5: 1},
        scratch_shapes=[pltpu.VMEM((SUBLANES, cw), F32), pltpu.VMEM((SUBLANES, cw), F32)],
        compiler_params=_params(("arbitrary",)),
    )(dycat, c1, proj, ln_w, ln_b, dproj)


CONV_TILE = 512
CONV_COLS = 512
CONV_SUB_ROWS = 128
CONV_SUB_COLS = LANES


def _conv_halo(k):
    return SUBLANES if k - 1 <= SUBLANES else 32


def _conv_subtiles(tm, cw):
    return [(r0, c0) for r0 in range(0, tm, CONV_SUB_ROWS) for c0 in range(0, cw, CONV_SUB_COLS)]


def _conv_use_shifted(k):
    return k > SUBLANES


def _conv_shift_scratch(k, rows, cw):
    return [pltpu.VMEM((SUBLANES - 1, rows - SUBLANES, cw), F32)] if _conv_use_shifted(k) else []


def _conv_fill_shifted(ext, sh):
    n = sh.shape[1]
    for b in range(1, SUBLANES):
        sh[b - 1] = ext[b:b + n, :]


def _conv_rows(ext, sh, start, rows, cs):
    b = start % SUBLANES
    if b == 0 or not sh:
        return ext[start:start + rows, cs]
    return sh[0][b - 1, start - b:start - b + rows, cs]


def _conv_fwd(src, col0, width, w, bias, k, seq, name):
    t = src.shape[0]
    tm, cw, halo = CONV_TILE, CONV_COLS, _conv_halo(k)
    sr, sc = CONV_SUB_ROWS, CONV_SUB_COLS
    p = k - 1
    cb0 = col0 // cw
    kp = w.shape[0]

    shifted = _conv_use_shifted(k)

    def body(x_ref, h_ref, w_ref, b_ref, o_ref, ext, *sh):
        i = pl.program_id(0)
        seq_start = (i * tm) % seq == 0
        ext[halo:, :] = x_ref[...]
        ext[:halo, :] = jnp.where(seq_start, 0.0, h_ref[...])
        if shifted:
            _conv_fill_shifted(ext, sh[0])
        for r0, c0 in _conv_subtiles(tm, cw):
            cs = slice(c0, c0 + sc)
            acc = jnp.zeros((sr, sc), F32) + b_ref[:, cs]
            for j in range(k):
                acc = acc + w_ref[j:j + 1, cs] * _conv_rows(ext, sh, r0 + halo - p + j, sr, cs)
            o_ref[r0:r0 + sr, cs] = acc

    return pl.pallas_call(
        body, name=name,
        out_shape=jax.ShapeDtypeStruct((t, width), F32),
        grid=(t // tm, width // cw),
        in_specs=[pl.BlockSpec((tm, cw), lambda i, j: (i, cb0 + j)),
                  pl.BlockSpec((halo, cw), lambda i, j: (jnp.maximum(i * (tm // halo) - 1, 0), cb0 + j)),
                  pl.BlockSpec((kp, cw), lambda i, j: (0, j)),
                  pl.BlockSpec((1, cw), lambda i, j: (0, j))],
        out_specs=pl.BlockSpec((tm, cw), lambda i, j: (i, j)),
        scratch_shapes=[pltpu.VMEM((halo + tm, cw), F32)] + _conv_shift_scratch(k, halo + tm, cw),
        compiler_params=_params(("parallel", "parallel")),
    )(src, src, w, bias)


def _conv_bwd(dy, src, col0, width, w, k, seq, name, into=None):
    t = src.shape[0]
    tm, cw, halo = CONV_TILE, CONV_COLS, _conv_halo(k)
    sr, sc = CONV_SUB_ROWS, CONV_SUB_COLS
    p = k - 1
    cb0 = col0 // cw
    kp = w.shape[0]
    nt = t // tm
    last_halo = t // halo - 1

    shifted = _conv_use_shifted(k)

    def body(dy_ref, dn_ref, x_ref, xp_ref, w_ref, *rest):
        if into is not None:
            rest = rest[1:]
        dx_ref, dw_ref, db_ref, dyext, xext, wacc, bacc = rest[:7]
        sh = rest[7:]
        i = pl.program_id(1)
        dysh, xsh = (sh[:1], sh[1:]) if shifted else ((), ())

        @pl.when(i == 0)
        def _():
            wacc[...] = jnp.zeros_like(wacc)
            bacc[...] = jnp.zeros_like(bacc)

        seq_start = (i * tm) % seq == 0
        seq_end = ((i + 1) * tm) % seq == 0
        dyext[:tm, :] = dy_ref[...]
        dyext[tm:, :] = jnp.where(seq_end, 0.0, dn_ref[...])
        xext[halo:, :] = x_ref[...]
        xext[:halo, :] = jnp.where(seq_start, 0.0, xp_ref[...])
        if shifted:
            _conv_fill_shifted(dyext, dysh[0])
            _conv_fill_shifted(xext, xsh[0])
        for r0, c0 in _conv_subtiles(tm, cw):
            cs = slice(c0, c0 + sc)
            dyv = dy_ref[r0:r0 + sr, cs]
            acc = jnp.zeros((sr, sc), F32)
            for j in range(k):
                acc = acc + w_ref[j:j + 1, cs] * _conv_rows(dyext, dysh, r0 + p - j, sr, cs)
                wacc[j, :, cs] += _rowsum8(dyv * _conv_rows(xext, xsh, r0 + halo - p + j, sr, cs))
            dx_ref[r0:r0 + sr, cs] = acc.astype(dx_ref.dtype)
            bacc[:, cs] += _rowsum8(dyv)

        @pl.when(i == nt - 1)
        def _():
            dw_ref[...] = jnp.zeros_like(dw_ref)
            for j in range(k):
                dw_ref[j:j + 1, :] = jnp.sum(wacc[j], axis=0, keepdims=True)
            db_ref[...] = jnp.sum(bacc[...], axis=0, keepdims=True)

    if into is None:
        dx_shape = jax.ShapeDtypeStruct((t, width), F32)
        dx_spec = pl.BlockSpec((tm, cw), lambda j, i: (i, j))
        extra_specs, extra_args, aliases = [], [], {}
    else:
        dx_shape = jax.ShapeDtypeStruct(into.shape, into.dtype)
        dx_spec = pl.BlockSpec((tm, cw), lambda j, i: (i, cb0 + j))
        extra_specs, extra_args, aliases = [ANY], [into], {5: 0}
    return pl.pallas_call(
        body, name=name,
        out_shape=(dx_shape, jax.ShapeDtypeStruct((kp, width), F32), jax.ShapeDtypeStruct((1, width), F32)),
        grid=(width // cw, nt),
        in_specs=[pl.BlockSpec((tm, cw), lambda j, i: (i, j)),
                  pl.BlockSpec((halo, cw), lambda j, i: (jnp.minimum((i + 1) * (tm // halo), last_halo), j)),
                  pl.BlockSpec((tm, cw), lambda j, i: (i, cb0 + j)),
                  pl.BlockSpec((halo, cw), lambda j, i: (jnp.maximum(i * (tm // halo) - 1, 0), cb0 + j)),
                  pl.BlockSpec((kp, cw), lambda j, i: (0, j))] + extra_specs,
        out_specs=(dx_spec,
                   pl.BlockSpec((kp, cw), lambda j, i: (0, j)),
                   pl.BlockSpec((1, cw), lambda j, i: (0, j))),
        input_output_aliases=aliases,
        scratch_shapes=[pltpu.VMEM((tm + halo, cw), F32), pltpu.VMEM((halo + tm, cw), F32),
                        pltpu.VMEM((kp, SUBLANES, cw), F32), pltpu.VMEM((SUBLANES, cw), F32)]
        + 2 * _conv_shift_scratch(k, halo + tm, cw),
        compiler_params=_params(("parallel", "arbitrary")),
    )(dy, dy, src, src, w, *extra_args)


def _conf_specs(tm, cw, halo, order):
    cb = OFF_CONF // cw

    def blk(col):
        return pl.BlockSpec((tm, cw), lambda *g: (order(*g), col))

    def prev(col):
        return pl.BlockSpec((halo, cw), lambda *g: (jnp.maximum(order(*g) * (tm // halo) - 1, 0), col))

    return blk(cb), prev(cb), blk(cb + 1), prev(cb + 1)


def _glu_window(ext, a_ref, ah_ref, g_ref, gh_ref, seq_start, halo):
    ext[halo:, :] = a_ref[...] * _sigmoid(g_ref[...])
    ext[:halo, :] = jnp.where(seq_start, 0.0, ah_ref[...] * _sigmoid(gh_ref[...]))


def _conf_fwd(proj, w, bias, ln_w, ln_b, ycat, seq, name):
    t = proj.shape[0]
    k = CONF_KERNEL
    tm, cw, halo = CONV_TILE, CONF_WIDTH, _conv_halo(k)
    sr, sc = CONV_SUB_ROWS, CONV_SUB_COLS
    p = k - 1
    kp = w.shape[0]

    def body(a_ref, ah_ref, g_ref, gh_ref, z_ref, w_ref, b_ref, lw_ref, lb_ref, _, c1_ref, y_ref, ext, sh):
        i = pl.program_id(0)
        _glu_window(ext, a_ref, ah_ref, g_ref, gh_ref, (i * tm) % seq == 0, halo)
        _conv_fill_shifted(ext, sh)
        for r0, c0 in _conv_subtiles(tm, cw):
            cs = slice(c0, c0 + sc)
            acc = jnp.zeros((sr, sc), F32) + b_ref[:, cs]
            for j in range(k):
                acc = acc + w_ref[j:j + 1, cs] * _conv_rows(ext, (sh,), r0 + halo - p + j, sr, cs)
            c1_ref[r0:r0 + sr, cs] = acc
        for r0 in range(0, tm, sr):
            rows = slice(r0, r0 + sr)
            cv = c1_ref[rows, :]
            xc = cv - jnp.mean(cv, axis=-1, keepdims=True)
            rstd = lax.rsqrt(jnp.mean(xc * xc, axis=-1, keepdims=True) + EPS)
            c2 = xc * rstd * lw_ref[...] + lb_ref[...]
            y_ref[rows, :] = (_silu(c2) * _silu(z_ref[rows, :])).astype(y_ref.dtype)

    vec = pl.BlockSpec((1, cw), lambda i: (0, 0))
    row = pl.BlockSpec((tm, cw), lambda i: (i, 0))
    return pl.pallas_call(
        body, name=name,
        out_shape=(jax.ShapeDtypeStruct((t, cw), F32), jax.ShapeDtypeStruct(ycat.shape, ycat.dtype)),
        grid=(t // tm,),
        in_specs=[*_conf_specs(tm, cw, halo, lambda i: i),
                  pl.BlockSpec((tm, cw), lambda i: (i, OFF_ZC // cw)),
                  pl.BlockSpec((kp, cw), lambda i: (0, 0)), vec, vec, vec, ANY],
        out_specs=(row, pl.BlockSpec((tm, cw), lambda i: (i, YCAT_CONF // cw))),
        input_output_aliases={9: 1},
        scratch_shapes=[pltpu.VMEM((halo + tm, cw), F32)] + _conv_shift_scratch(k, halo + tm, cw),
        compiler_params=_params(("parallel",)),
    )(proj, proj, proj, proj, proj, w, bias, ln_w, ln_b, ycat)


def _conf_conv_bwd(dc1, proj, w, dproj, seq, name):
    t = proj.shape[0]
    k = CONF_KERNEL
    tm, cw, halo = CONV_TILE, CONF_WIDTH, _conv_halo(k)
    sr, sc = CONV_SUB_ROWS, CONV_SUB_COLS
    p = k - 1
    kp = w.shape[0]
    nt = t // tm
    last_halo = t // halo - 1

    def body(dy_ref, dn_ref, a_ref, ah_ref, g_ref, gh_ref, w_ref, _, dag_ref, dw_ref, db_ref,
             dyext, xext, wacc, bacc, dysh, xsh):
        i = pl.program_id(0)

        @pl.when(i == 0)
        def _():
            wacc[...] = jnp.zeros_like(wacc)
            bacc[...] = jnp.zeros_like(bacc)

        seq_end = ((i + 1) * tm) % seq == 0
        dyext[:tm, :] = dy_ref[...]
        dyext[tm:, :] = jnp.where(seq_end, 0.0, dn_ref[...])
        _glu_window(xext, a_ref, ah_ref, g_ref, gh_ref, (i * tm) % seq == 0, halo)
        _conv_fill_shifted(dyext, dysh)
        _conv_fill_shifted(xext, xsh)
        for r0, c0 in _conv_subtiles(tm, cw):
            cs = slice(c0, c0 + sc)
            rows = slice(r0, r0 + sr)
            dyv = dy_ref[rows, cs]
            acc = jnp.zeros((sr, sc), F32)
            for j in range(k):
                acc = acc + w_ref[j:j + 1, cs] * _conv_rows(dyext, (dysh,), r0 + p - j, sr, cs)
                wacc[j, :, cs] += _rowsum8(dyv * _conv_rows(xext, (xsh,), r0 + halo - p + j, sr, cs))
            bacc[:, cs] += _rowsum8(dyv)
            s = _sigmoid(g_ref[rows, cs])
            dag_ref[rows, cs] = (acc * s).astype(dag_ref.dtype)
            dag_ref[rows, cw + c0:cw + c0 + sc] = (acc * a_ref[rows, cs] * s * (1.0 - s)).astype(dag_ref.dtype)

        @pl.when(i == nt - 1)
        def _():
            dw_ref[...] = jnp.zeros_like(dw_ref)
            for j in range(k):
                dw_ref[j:j + 1, :] = jnp.sum(wacc[j], axis=0, keepdims=True)
            db_ref[...] = jnp.sum(bacc[...], axis=0, keepdims=True)

    return pl.pallas_call(
        body, name=name,
        out_shape=(jax.ShapeDtypeStruct(dproj.shape, dproj.dtype), jax.ShapeDtypeStruct((kp, cw), F32),
                   jax.ShapeDtypeStruct((1, cw), F32)),
        grid=(nt,),
        in_specs=[pl.BlockSpec((tm, cw), lambda i: (i, 0)),
                  pl.BlockSpec((halo, cw), lambda i: (jnp.minimum((i + 1) * (tm // halo), last_halo), 0)),
                  *_conf_specs(tm, cw, halo, lambda i: i),
                  pl.BlockSpec((kp, cw), lambda i: (0, 0)), ANY],
        out_specs=(pl.BlockSpec((tm, 2 * cw), lambda i: (i, OFF_CONF // (2 * cw))),
                   pl.BlockSpec((kp, cw), lambda i: (0, 0)), pl.BlockSpec((1, cw), lambda i: (0, 0))),
        input_output_aliases={7: 0},
        scratch_shapes=[pltpu.VMEM((tm + halo, cw), F32), pltpu.VMEM((halo + tm, cw), F32),
                        pltpu.VMEM((kp, SUBLANES, cw), F32), pltpu.VMEM((SUBLANES, cw), F32)]
        + 2 * _conv_shift_scratch(k, halo + tm, cw),
        compiler_params=_params(("arbitrary",)),
    )(dc1, dc1, proj, proj, proj, proj, w, dproj)


def _half_mask(half):
    lane = _iota((1, LANES), 1)
    return ((lane >= half * ATTN_HEAD_DIM) & (lane < (half + 1) * ATTN_HEAD_DIM)).astype(F32)


def _stack_heads(xp, g):
    m = _half_mask(g)
    swapped = pltpu.roll(xp, ATTN_HEAD_DIM, axis=1)
    return jnp.concatenate([xp * m, swapped * m] if g == 0 else [swapped * m, xp * m], axis=0)


def _unstack_heads(both, g):
    w = both.shape[0] // 2
    top, bot = both[:w], both[w:]
    lo, hi = _half_mask(0), _half_mask(1)
    if g == 0:
        return top * lo + pltpu.roll(bot, ATTN_HEAD_DIM, axis=1) * hi
    return pltpu.roll(top, ATTN_HEAD_DIM, axis=1) * lo + bot * hi


def _band_mask(first_block):
    w = WINDOW
    qi = _iota((w, 2 * w), 0)
    kj = _iota((w, 2 * w), 1) - w
    rel = qi - kj
    return (rel >= 0) & (rel < w) & (jnp.logical_not(first_block) | (kj >= 0))


def _lane_pick(x, h):
    return jnp.sum(jnp.where(_iota(x.shape, 1) == h, x, 0.0), axis=1, keepdims=True)


def _attn_specs(nb, rev):
    w = WINDOW

    def blk(i):
        return nb - 1 - i if rev else i

    def row(b, i):
        return b * nb + blk(i)

    def prow(b, i):
        return b * nb + jnp.maximum(blk(i) - 1, 0)

    q = pl.BlockSpec((w, 512), lambda b, i: (row(b, i), OFF_Q // 512))
    kc = pl.BlockSpec((w, 128), lambda b, i: (row(b, i), OFF_K // 128))
    kp = pl.BlockSpec((w, 128), lambda b, i: (prow(b, i), OFF_K // 128))
    vc = pl.BlockSpec((w, 128), lambda b, i: (row(b, i), OFF_V // 128))
    vp = pl.BlockSpec((w, 128), lambda b, i: (prow(b, i), OFF_V // 128))
    z = pl.BlockSpec((w, 512), lambda b, i: (row(b, i), OFF_ZA // 512))
    return q, kc, kp, vc, vp, z, row


def _attn_fwd(proj, sinks, ycat, nbatch, name):
    t = proj.shape[0]
    w = WINDOW
    nb = t // nbatch // w
    scale = ATTN_HEAD_DIM ** -0.5
    q_s, kc_s, kp_s, vc_s, vp_s, z_s, row = _attn_specs(nb, False)

    def body(q_ref, kc_ref, kp_ref, vc_ref, vp_ref, z_ref, sk_ref, _, y_ref, o_ref, lse_ref):
        first = pl.program_id(1) == 0
        mask = _band_mask(first)
        kk = jnp.concatenate([kp_ref[...], kc_ref[...]], axis=0).astype(MXU_DTYPE)
        vv = jnp.concatenate([vp_ref[...], vc_ref[...]], axis=0).astype(MXU_DTYPE)
        sk = sk_ref[...]
        lane = _iota((w, LANES), 1)
        mask2 = jnp.concatenate([mask, mask], axis=0)
        scores = [_dot(_stack_heads(q_ref[:, j * LANES:(j + 1) * LANES], j // 2), kk, NT) for j in range(4)]
        lse_all = jnp.zeros((w, LANES), F32)
        for j in range(4):
            s = jnp.where(mask2, scores[j] * scale, -1e30)
            skc = jnp.concatenate([jnp.broadcast_to(_lane_pick(sk, 2 * j), (w, 1)),
                                   jnp.broadcast_to(_lane_pick(sk, 2 * j + 1), (w, 1))], axis=0)
            m = jnp.maximum(jnp.max(s, axis=1, keepdims=True), skc)
            den = jnp.sum(jnp.exp(s - m), axis=1, keepdims=True) + jnp.exp(skc - m)
            lse = m + jnp.log(den)
            lse_all = jnp.where(lane == 2 * j, lse[:w], lse_all)
            lse_all = jnp.where(lane == 2 * j + 1, lse[w:], lse_all)
            op = _unstack_heads(_dot(jnp.exp(s - lse), vv), j // 2)
            cols = slice(j * LANES, (j + 1) * LANES)
            o_ref[:, cols] = op
            y_ref[:, cols] = (op * _silu(z_ref[:, cols])).astype(y_ref.dtype)
        lse_ref[...] = lse_all

    return pl.pallas_call(
        body, name=name,
        out_shape=(jax.ShapeDtypeStruct(ycat.shape, ycat.dtype), jax.ShapeDtypeStruct((t, 512), F32),
                   jax.ShapeDtypeStruct((t, LANES), F32)),
        grid=(nbatch, nb),
        in_specs=[q_s, kc_s, kp_s, vc_s, vp_s, z_s, pl.BlockSpec((1, LANES), lambda b, i: (0, 0)), ANY],
        out_specs=(pl.BlockSpec((w, 512), lambda b, i: (row(b, i), YCAT_ATTN // 512)),
                   pl.BlockSpec((w, 512), lambda b, i: (row(b, i), 0)),
                   pl.BlockSpec((w, LANES), lambda b, i: (row(b, i), 0))),
        input_output_aliases={7: 0},
        compiler_params=_params(("parallel", "parallel")),
    )(proj, proj, proj, proj, proj, proj, sinks, ycat)


def _attn_bwd(dycat, proj, o, lse, sinks, ddt, dproj, nbatch, name):
    t = proj.shape[0]
    w = WINDOW
    nb = t // nbatch // w
    scale = ATTN_HEAD_DIM ** -0.5
    q_s, kc_s, kp_s, vc_s, vp_s, z_s, row = _attn_specs(nb, True)

    def body(dy_ref, q_ref, kc_ref, kp_ref, vc_ref, vp_ref, z_ref, o_ref, lse_ref, sk_ref, ddt_ref, _,
             grp_ref, dsk_ref, kcarry, vcarry, sacc):
        b, i = pl.program_id(0), pl.program_id(1)

        @pl.when((b == 0) & (i == 0))
        def _():
            sacc[...] = jnp.zeros_like(sacc)

        @pl.when(i == 0)
        def _():
            kcarry[...] = jnp.zeros_like(kcarry)
            vcarry[...] = jnp.zeros_like(vcarry)

        first = i == nb - 1
        mask = _band_mask(first)
        kk = jnp.concatenate([kp_ref[...], kc_ref[...]], axis=0).astype(MXU_DTYPE)
        vv = jnp.concatenate([vp_ref[...], vc_ref[...]], axis=0).astype(MXU_DTYPE)
        sk = sk_ref[...]
        lse_all = lse_ref[...]
        lane1 = _iota((1, LANES), 1)
        mask2 = jnp.concatenate([mask, mask], axis=0)
        qs, dos, deltas, lses, scores, dps = [], [], [], [], [], []
        for j in range(4):
            cols = slice(j * LANES, (j + 1) * LANES)
            qp, zp, ov, dy = q_ref[:, cols], z_ref[:, cols], o_ref[:, cols], dy_ref[:, cols]
            grp_ref[:, OFF_ZA + j * LANES:OFF_ZA + (j + 1) * LANES] = (dy * ov * _dsilu(zp)).astype(grp_ref.dtype)
            do = dy * _silu(zp)
            q2 = _stack_heads(qp, j // 2).astype(MXU_DTYPE)
            do2 = _stack_heads(do, j // 2)
            qs.append(q2)
            dos.append(do2.astype(MXU_DTYPE))
            deltas.append(jnp.sum(do2 * _stack_heads(ov, j // 2), axis=1, keepdims=True))
            lses.append(jnp.concatenate([_lane_pick(lse_all, 2 * j), _lane_pick(lse_all, 2 * j + 1)], axis=0))
            scores.append(_dot(q2, kk, NT))
            dps.append(_dot(do2, vv, NT))
        prs, dss = [], []
        dsk = jnp.zeros((1, LANES), F32)
        for j in range(4):
            pr = jnp.exp(jnp.where(mask2, scores[j] * scale, -1e30) - lses[j])
            prs.append(pr.astype(MXU_DTYPE))
            dss.append((pr * (dps[j] - deltas[j])).astype(MXU_DTYPE))
            skc = jnp.concatenate([jnp.broadcast_to(_lane_pick(sk, 2 * j), (w, 1)),
                                   jnp.broadcast_to(_lane_pick(sk, 2 * j + 1), (w, 1))], axis=0)
            sink_term = jnp.exp(skc - lses[j]) * deltas[j]
            dsk = dsk - jnp.where(lane1 == 2 * j, jnp.sum(sink_term[:w]), 0.0)
            dsk = dsk - jnp.where(lane1 == 2 * j + 1, jnp.sum(sink_term[w:]), 0.0)
        dkk = jnp.zeros((2 * w, LANES), F32)
        dvv = jnp.zeros((2 * w, LANES), F32)
        for j in range(4):
            dq = _unstack_heads(_dot(dss[j], kk) * scale, j // 2)
            grp_ref[:, OFF_Q + j * LANES:OFF_Q + (j + 1) * LANES] = dq.astype(grp_ref.dtype)
            dkk = dkk + _dot(dss[j], qs[j], TN) * scale
            dvv = dvv + _dot(prs[j], dos[j], TN)
        grp_ref[:, OFF_K:OFF_K + LANES] = (dkk[w:, :] + kcarry[...]).astype(grp_ref.dtype)
        grp_ref[:, OFF_V:OFF_V + LANES] = (dvv[w:, :] + vcarry[...]).astype(grp_ref.dtype)
        grp_ref[:, OFF_DT:OFF_DT + LANES] = ddt_ref[...].astype(grp_ref.dtype)
        grp_ref[:, OFF_DT + LANES:] = jnp.zeros((w, ATTN_GROUP - OFF_DT - LANES), grp_ref.dtype)
        kcarry[...] = dkk[:w, :]
        vcarry[...] = dvv[:w, :]
        sacc[...] += dsk

        @pl.when((b == nbatch - 1) & (i == nb - 1))
        def _():
            dsk_ref[...] = sacc[...]

    return pl.pallas_call(
        body, name=name,
        out_shape=(jax.ShapeDtypeStruct(dproj.shape, dproj.dtype), jax.ShapeDtypeStruct((1, LANES), F32)),
        grid=(nbatch, nb),
        in_specs=[pl.BlockSpec((w, 512), lambda b, i: (row(b, i), YCAT_ATTN // 512)),
                  q_s, kc_s, kp_s, vc_s, vp_s, z_s,
                  pl.BlockSpec((w, 512), lambda b, i: (row(b, i), 0)),
                  pl.BlockSpec((w, LANES), lambda b, i: (row(b, i), 0)),
                  pl.BlockSpec((1, LANES), lambda b, i: (0, 0)),
                  pl.BlockSpec((w, LANES), lambda b, i: (row(b, i), 0)), ANY],
        out_specs=(pl.BlockSpec((w, ATTN_GROUP), lambda b, i: (row(b, i), 0)),
                   pl.BlockSpec((1, LANES), lambda b, i: (0, 0))),
        input_output_aliases={11: 0},
        scratch_shapes=[pltpu.VMEM((w, LANES), F32), pltpu.VMEM((w, LANES), F32),
                        pltpu.VMEM((1, LANES), F32)],
        compiler_params=_params(("arbitrary", "arbitrary")),
    )(dycat, proj, proj, proj, proj, proj, proj, o, lse, sinks, ddt, dproj)


SSD_WIDTH = SSD_HEADS * SSD_HEAD_DIM
GROUP_ROWS = SSD_WIDTH // 2


def _expand_mat():
    r, c = _iota((LANES, SSD_WIDTH), 0), _iota((LANES, SSD_WIDTH), 1)
    return (r == lax.shift_right_logical(c, 6)).astype(BF16)


def _expand_mat_t():
    r, c = _iota((SSD_WIDTH, LANES), 0), _iota((SSD_WIDTH, LANES), 1)
    return (c == lax.shift_right_logical(r, 6)).astype(BF16)


def _ssd_common(u_ref, dt_ref, dtb_ref, a_ref):
    q = CHUNK
    act = _silu(u_ref[...])
    xs = act[:, :SSD_WIDTH]
    bm = act[:, SSD_WIDTH:SSD_WIDTH + 256]
    cm = act[:, SSD_WIDTH + 256:]
    dtp = _softplus(dt_ref[...] + dtb_ref[...])
    a = dtp * a_ref[...]
    tril = (_iota((q, q), 0) >= _iota((q, q), 1)).astype(BF16)
    acs = _xdot_r(tril, a)
    acs_t = acs.T
    e = _expand_mat()
    dt_x = _xdot(dtp, e)
    ea = jnp.exp(_xdot(acs, e))
    a_end = jnp.sum(jnp.where(_iota(acs.shape, 0) == q - 1, acs, 0.0), axis=0, keepdims=True)
    dec = jnp.exp(_xdot(a_end - acs, e))
    a_end_col = jnp.broadcast_to(_lane_pick(acs_t, q - 1), (LANES, LANES))
    s_scale = jnp.exp(_xdot_r(_expand_mat_t(), a_end_col))
    return act, xs, bm, cm, dtp, acs, acs_t, dt_x, ea, dec, s_scale, tril


def _decay_mat(acs, acs_t, h):
    q = CHUNK
    col = _lane_pick(acs, h)
    rowv = jnp.sum(jnp.where(_iota(acs_t.shape, 0) == h, acs_t, 0.0), axis=0, keepdims=True)
    causal = _iota((q, q), 0) >= _iota((q, q), 1)
    return jnp.exp(jnp.where(causal, col - rowv, -1e30))


GN_WIDTH = 512


def _ssd_fwd(u, proj, dtb, a_neg, d_x, norm_w, ycat, nbatch, name):
    t = u.shape[0]
    q = CHUNK
    nc = t // nbatch // q

    def body(u_ref, dt_ref, z_ref, dtb_ref, a_ref, dx_ref, nw_ref, _, y_ref, st_ref, yn_ref, state):
        c = pl.program_id(1)

        @pl.when(c == 0)
        def _():
            state[...] = jnp.zeros_like(state)

        st_ref[...] = state[...]
        act, xs, bm, cm, dtp, acs, acs_t, dt_x, ea, dec, s_scale, _ = _ssd_common(u_ref, dt_ref, dtb_ref, a_ref)
        xdt = xs * dt_x
        xdec = xdt * dec
        lo, hi = _half_mask(0), _half_mask(1)
        grp = []
        for g in range(2):
            bg = bm[:, g * LANES:(g + 1) * LANES]
            cg = cm[:, g * LANES:(g + 1) * LANES]
            rows = slice(g * GROUP_ROWS, (g + 1) * GROUP_ROWS)
            sg = state[rows, :]
            grp.append((_dot(cg, bg, NT), _dot(cg, sg, NT), rows,
                        s_scale[rows, :] * sg + _dot(xdec[:, rows], bg, TN)))
        for g in range(2):
            cb, yoff, rows, state_new = grp[g]
            for j in range(4):
                pj = g * 4 + j
                cols = slice(pj * LANES, (pj + 1) * LANES)
                xp = xdt[:, cols]
                m2 = jnp.concatenate([cb * _decay_mat(acs, acs_t, 2 * pj), cb * _decay_mat(acs, acs_t, 2 * pj + 1)],
                                     axis=1)
                yp = _dot(m2, jnp.concatenate([xp * lo, xp * hi], axis=0))
                yp = yp + yoff[:, j * LANES:(j + 1) * LANES] * ea[:, cols]
                y_ref[:, cols] = yp + dx_ref[:, cols] * xs[:, cols]
            state[rows, :] = state_new
        for g in range(SSD_WIDTH // GN_WIDTH):
            cols = slice(g * GN_WIDTH, (g + 1) * GN_WIDTH)
            gg = y_ref[:, cols] * _silu(z_ref[:, cols])
            rstd = lax.rsqrt(jnp.mean(gg * gg, axis=-1, keepdims=True) + EPS)
            yn_ref[:, cols] = (gg * rstd * nw_ref[:, cols]).astype(yn_ref.dtype)

    vec = pl.BlockSpec((1, LANES), lambda b, c: (0, 0))
    wide = pl.BlockSpec((q, SSD_WIDTH), lambda b, c: (b * nc + c, 0))
    wvec = pl.BlockSpec((1, SSD_WIDTH), lambda b, c: (0, 0))
    return pl.pallas_call(
        body, name=name,
        out_shape=(jax.ShapeDtypeStruct((t, SSD_WIDTH), F32),
                   jax.ShapeDtypeStruct((nbatch * nc * SSD_WIDTH, SSD_STATE), F32),
                   jax.ShapeDtypeStruct(ycat.shape, ycat.dtype)),
        grid=(nbatch, nc),
        in_specs=[pl.BlockSpec((q, SSD_CONV_DIM), lambda b, c: (b * nc + c, 0)),
                  pl.BlockSpec((q, LANES), lambda b, c: (b * nc + c, OFF_DT // LANES)),
                  pl.BlockSpec((q, SSD_WIDTH), lambda b, c: (b * nc + c, OFF_ZS // SSD_WIDTH)),
                  vec, vec, wvec, wvec, ANY],
        out_specs=(wide, pl.BlockSpec((SSD_WIDTH, SSD_STATE), lambda b, c: (b * nc + c, 0)), wide),
        input_output_aliases={7: 2},
        scratch_shapes=[pltpu.VMEM((SSD_WIDTH, SSD_STATE), F32)],
        compiler_params=_params(("parallel", "arbitrary")),
    )(u, proj, proj, dtb, a_neg, d_x, norm_w, ycat)


def _ssd_bwd(dycat, u, proj, y, states, dtb, a_neg, d_x, norm_w, dproj, nbatch, name):
    t = u.shape[0]
    q = CHUNK
    nc = t // nbatch // q

    def body(do_ref, u_ref, dt_ref, z_ref, y_ref, st_ref, dtb_ref, a_ref, dx_ref, nw_ref, _,
             du_ref, dz_ref, ddt_ref, dal_ref, dd_ref, dtbg_ref, dnw_ref, dstate, acc_a, acc_d, acc_b, acc_w):
        b, c = pl.program_id(0), pl.program_id(1)

        @pl.when((b == 0) & (c == 0))
        def _():
            acc_a[...] = jnp.zeros_like(acc_a)
            acc_d[...] = jnp.zeros_like(acc_d)
            acc_b[...] = jnp.zeros_like(acc_b)
            acc_w[...] = jnp.zeros_like(acc_w)

        @pl.when(c == 0)
        def _():
            dstate[...] = jnp.zeros_like(dstate)

        dy_parts = []
        for g in range(SSD_WIDTH // GN_WIDTH):
            cols = slice(g * GN_WIDTH, (g + 1) * GN_WIDTH)
            yv, zv, dov = y_ref[:, cols], z_ref[:, cols], do_ref[:, cols]
            sz = _silu(zv)
            gg = yv * sz
            rstd = lax.rsqrt(jnp.mean(gg * gg, axis=-1, keepdims=True) + EPS)
            gh = gg * rstd
            acc_w[:, cols] += _rowsum8(dov * gh)
            dgn = dov * nw_ref[:, cols]
            dg = rstd * (dgn - gh * jnp.mean(dgn * gh, axis=-1, keepdims=True))
            dy_parts.append(dg * sz)
            dz_ref[:, cols] = (dg * yv * _dsilu(zv)).astype(dz_ref.dtype)

        act, xs, bm, cm, dtp, acs, acs_t, dt_x, ea, dec, s_scale, tril = _ssd_common(
            u_ref, dt_ref, dtb_ref, a_ref)
        xdt = xs * dt_x
        xdec = xdt * dec
        dyv = jnp.concatenate(dy_parts, axis=1)
        dye = dyv * ea
        lo, hi = _half_mask(0), _half_mask(1)
        et = _expand_mat_t()
        grp = []
        for g in range(2):
            rows = slice(g * GROUP_ROWS, (g + 1) * GROUP_ROWS)
            bg = bm[:, g * LANES:(g + 1) * LANES]
            cg = cm[:, g * LANES:(g + 1) * LANES]
            sg = st_ref[rows, :]
            dsg = dstate[rows, :]
            grp.append(dict(
                rows=rows, bg=bg, cg=cg, dsg=dsg,
                cb=_dot(cg, bg, NT), yoff=_dot(cg, sg, NT), dxst=_dot(bg, dsg, NT) * dec[:, rows],
                dc_off=_dot(dye[:, rows], sg), db_off=_dot(xdec[:, rows], dsg),
                s_next=s_scale[rows, :] * sg + _dot(xdec[:, rows], bg, TN),
                dstate_new=_dot(dye[:, rows], cg, TN) + s_scale[rows, :] * dsg))
        dy2s, g2s, l2s = [], [], []
        for pj in range(SSD_HEADS // 2):
            cols = slice(pj * LANES, (pj + 1) * LANES)
            dyp = dyv[:, cols]
            dy2 = jnp.concatenate([dyp * lo, dyp * hi], axis=0).astype(MXU_DTYPE)
            dy2s.append(dy2)
            g2s.append(_dot(dy2, xdt[:, cols], NT))
            l2s.append(jnp.concatenate([_decay_mat(acs, acs_t, 2 * pj), _decay_mat(acs, acs_t, 2 * pj + 1)], axis=0))
        dal_diag = jnp.zeros((q, LANES), F32)
        lane2 = _iota((2 * q, LANES), 1)
        row2 = _iota((2 * q, LANES), 0)
        dxdt_parts, db_parts, dc_parts = [], [], []
        end_sum = jnp.zeros((LANES, LANES), F32)
        for g in range(2):
            gd = grp[g]
            cb2 = jnp.concatenate([gd["cb"], gd["cb"]], axis=0)
            dcb = jnp.zeros((q, q), F32)
            parts = []
            for j in range(4):
                pj = g * 4 + j
                gl = g2s[pj] * l2s[pj]
                dcb = dcb + gl[:q] + gl[q:]
                m2 = cb2 * l2s[pj]
                parts.append(_dot(m2, dy2s[pj], TN))
                w2 = (gl * cb2).astype(MXU_DTYPE)
                sel2 = (lane2 == 2 * pj + (row2 >= q).astype(jnp.int32)).astype(MXU_DTYPE)
                dal_diag = dal_diag + _dot(jnp.concatenate([w2[:q], w2[q:]], axis=1), sel2) - _dot(w2, sel2, TN)
            dxdt_parts.append(jnp.concatenate(parts, axis=1) + gd["dxst"])
            dc_parts.append(_dot(dcb, gd["bg"]) + gd["dc_off"])
            db_parts.append(_dot(dcb, gd["cg"], TN) + gd["db_off"])
            end_sum = end_sum + _xdot(gd["dsg"] * gd["s_next"], et[gd["rows"], :], TN, passes=2)
            dstate[gd["rows"], :] = gd["dstate_new"]
        dxst_parts = [gd["dxst"] for gd in grp]
        yoff_parts = [gd["yoff"] for gd in grp]
        dxdt = jnp.concatenate(dxdt_parts, axis=1)
        dxv = dx_ref[...]
        yoff = jnp.concatenate(yoff_parts, axis=1) * ea
        dalpha = dal_diag + _xdot(dyv * yoff - xdt * jnp.concatenate(dxst_parts, axis=1), et)
        end_row = jnp.sum(end_sum, axis=0, keepdims=True)
        dalpha = dalpha + jnp.where(_iota((q, LANES), 0) == q - 1, end_row, 0.0)
        da = _xdot_r(tril, dalpha, TN)
        ddtp = da * a_ref[...] + _xdot(dxdt * xs, et)
        acc_a[...] += _rowsum8(da * dtp)
        acc_d[...] += _rowsum8(_xdot(dyv * xs, et))
        ddt_raw = ddtp * _sigmoid(dt_ref[...] + dtb_ref[...])
        acc_b[...] += _rowsum8(ddt_raw)
        ddt_ref[...] = ddt_raw
        dxs = dxdt * dt_x + dxv * dyv
        dact = jnp.concatenate([dxs] + db_parts + dc_parts, axis=1)
        du_ref[...] = dact * _dsilu(u_ref[...])

        @pl.when((b == nbatch - 1) & (c == nc - 1))
        def _():
            dal_ref[...] = jnp.sum(acc_a[...], axis=0, keepdims=True) * a_ref[...]
            dd_ref[...] = jnp.sum(acc_d[...], axis=0, keepdims=True)
            dtbg_ref[...] = jnp.sum(acc_b[...], axis=0, keepdims=True)
            dnw_ref[...] = jnp.sum(acc_w[...], axis=0, keepdims=True)

    def rowblk(b, c):
        return b * nc + (nc - 1 - c)

    vec = pl.BlockSpec((1, LANES), lambda b, c: (0, 0))
    wvec = pl.BlockSpec((1, SSD_WIDTH), lambda b, c: (0, 0))
    wide = pl.BlockSpec((q, SSD_WIDTH), lambda b, c: (rowblk(b, c), 0))
    zblk = pl.BlockSpec((q, SSD_WIDTH), lambda b, c: (rowblk(b, c), OFF_ZS // SSD_WIDTH))
    return pl.pallas_call(
        body, name=name,
        out_shape=(jax.ShapeDtypeStruct((t, SSD_CONV_DIM), F32), jax.ShapeDtypeStruct(dproj.shape, dproj.dtype),
                   jax.ShapeDtypeStruct((t, LANES), F32),
                   jax.ShapeDtypeStruct((1, LANES), F32), jax.ShapeDtypeStruct((1, LANES), F32),
                   jax.ShapeDtypeStruct((1, LANES), F32), jax.ShapeDtypeStruct((1, SSD_WIDTH), F32)),
        grid=(nbatch, nc),
        in_specs=[wide,
                  pl.BlockSpec((q, SSD_CONV_DIM), lambda b, c: (rowblk(b, c), 0)),
                  pl.BlockSpec((q, LANES), lambda b, c: (rowblk(b, c), OFF_DT // LANES)),
                  zblk, wide,
                  pl.BlockSpec((SSD_WIDTH, SSD_STATE), lambda b, c: (rowblk(b, c), 0)),
                  vec, vec, wvec, wvec, ANY],
        out_specs=(pl.BlockSpec((q, SSD_CONV_DIM), lambda b, c: (rowblk(b, c), 0)),
                   zblk,
                   pl.BlockSpec((q, LANES), lambda b, c: (rowblk(b, c), 0)),
                   vec, vec, vec, wvec),
        input_output_aliases={10: 1},
        scratch_shapes=[pltpu.VMEM((SSD_WIDTH, SSD_STATE), F32), pltpu.VMEM((SUBLANES, LANES), F32),
                        pltpu.VMEM((SUBLANES, LANES), F32), pltpu.VMEM((SUBLANES, LANES), F32),
                        pltpu.VMEM((SUBLANES, SSD_WIDTH), F32)],
        compiler_params=_params(("arbitrary", "arbitrary")),
    )(dycat, u, proj, proj, y, states, dtb, a_neg, d_x, norm_w, dproj)


def _pad_rows(w, rows):
    return jnp.concatenate([w, jnp.zeros((rows - w.shape[0], w.shape[1]), w.dtype)], axis=0)


def _pad_lanes(v):
    return jnp.concatenate([v, jnp.zeros((LANES - v.shape[0],), v.dtype)]).reshape(1, LANES)


def _padded_from_chips(pieces):
    cols = pieces[0].shape[-1]
    lead = pieces[0].shape[:-1]
    parts, pos = [], 0
    for lo, hi, start in sorted(SECTIONS, key=lambda s: s[2]):
        if start > pos:
            parts.append(jnp.zeros(lead + (start - pos,), pieces[0].dtype))
        pos = start + hi - lo
        while lo < hi:
            p = lo // cols
            end = min(hi, (p + 1) * cols)
            parts.append(pieces[p][..., lo - p * cols:end - p * cols])
            lo = end
    if pos < NP:
        parts.append(jnp.zeros(lead + (NP - pos,), pieces[0].dtype))
    return jnp.concatenate(parts, axis=-1)


def _chip_part_from_padded(wp, p, cols):
    lo, hi = p * cols, (p + 1) * cols
    parts = []
    for rs, re, start in SECTIONS:
        a, b = max(lo, rs), min(hi, re)
        if a < b:
            parts.append(wp[..., start + a - rs:start + b - rs])
    return jnp.concatenate(parts, axis=-1)


def _layer_params(li, w_in_p, w_out, conv_w, dw_w, small):
    return dict(
        w_in_p=w_in_p, w_out=w_out,
        conv_w=_pad_rows(conv_w, SUBLANES), dw_w=_pad_rows(dw_w, 32),
        norm_w=small["norm_w"][li].reshape(1, -1),
        conv_b=small["ssd_conv_b"][li].reshape(1, -1),
        dtb=_pad_lanes(small["ssd_dt_bias"][li]),
        a_neg=_pad_lanes(-jnp.exp(small["ssd_a_log"][li])),
        d_x=jnp.repeat(small["ssd_d"][li], SSD_HEAD_DIM).reshape(1, -1),
        ssd_norm_w=small["ssd_norm_w"][li].reshape(1, -1),
        sinks=_pad_lanes(small["attn_sinks"][li]),
        dw_b=small["conf_dw_b"][li].reshape(1, -1),
        ln_w=small["conf_ln_w"][li].reshape(1, -1),
        ln_b=small["conf_ln_b"][li].reshape(1, -1),
    )


def _layer_fwd(x, p, nbatch, seq, tag, after=None):
    h, h_t = _rmsnorm_fwd(x, p["norm_w"], name=f"rmsnorm_fwd_{tag}", after=after)
    proj = _matmul(h, p["w_in_p"], "nn", F32, 1024, 512, 1024, name=f"proj_fwd_{tag}")
    u = _conv_fwd(proj, OFF_XBC, SSD_CONV_DIM, p["conv_w"], p["conv_b"], SSD_CONV, seq, name=f"ssd_conv_fwd_{tag}")
    ycat = lax.empty((x.shape[0], MIX_WIDTH), MXU_DTYPE)
    y, states, ycat = _ssd_fwd(u, proj, p["dtb"], p["a_neg"], p["d_x"], p["ssd_norm_w"], ycat, nbatch,
                               name=f"ssd_fwd_{tag}")
    ycat, o, lse = _attn_fwd(proj, p["sinks"], ycat, nbatch, name=f"attn_fwd_{tag}")
    c1, ycat = _conf_fwd(proj, p["dw_w"], p["dw_b"], p["ln_w"], p["ln_b"], ycat, seq, name=f"conf_fwd_{tag}")
    x_new = _matmul(ycat, p["w_out"], "nn", F32, 1024, 512, 2048, name=f"out_fwd_{tag}", residual=x)
    return x_new, dict(x=x, h_t=h_t, proj=proj, u=u, y=y, states=states, o=o, lse=lse, c1=c1, ycat=ycat)


def _layer_bwd(dx_out, p, s, nbatch, seq, tag, hooks=None):
    hooks = hooks or {}
    proj = s["proj"]
    dycat = _matmul(dx_out, p["w_out"], "nt", F32, 1024, 1024, 1024, name=f"out_bwd_dy_{tag}",
                    after=hooks.get("start_token"))
    dw_out = _matmul(s["ycat"], dx_out, "tn", F32, 1024, 1024, 1024, name=f"out_bwd_dw_{tag}")
    token = hooks["after_dycat"](dycat) if "after_dycat" in hooks else None
    dtb = p["dtb"] if token is None else p["dtb"] + token[0, 0]
    dproj = lax.empty(proj.shape, MXU_DTYPE)
    du, dproj, ddt, da_log, dd, ddtb, dssd_norm_w = _ssd_bwd(
        dycat, s["u"], proj, s["y"], s["states"], dtb, p["a_neg"], p["d_x"], p["ssd_norm_w"], dproj,
        nbatch, name=f"ssd_bwd_{tag}")
    dproj, dconv_w, dconv_b = _conv_bwd(du, proj, OFF_XBC, SSD_CONV_DIM, p["conv_w"], SSD_CONV, seq,
                                        name=f"ssd_conv_bwd_{tag}", into=dproj)
    dproj, dsinks = _attn_bwd(dycat, proj, s["o"], s["lse"], p["sinks"], ddt, dproj, nbatch,
                              name=f"attn_bwd_{tag}")
    if "after_attn" in hooks:
        hooks["after_attn"](dproj)
    dc1, dproj, dln_w, dln_b = _conf_post_bwd(dycat, s["c1"], proj, p["ln_w"], p["ln_b"], dproj,
                                              name=f"conf_post_bwd_{tag}")
    dproj, ddw_w, ddw_b = _conf_conv_bwd(dc1, proj, p["dw_w"], dproj, seq, name=f"conf_conv_bwd_{tag}")
    dw_in_p = _matmul(s["h_t"], dproj, "nn", F32, 1024, 512, 4096, name=f"proj_bwd_dw_{tag}")
    dx_in, dnorm_w = _proj_bwd_dx(dproj, p["w_in_p"], s["x"], p["norm_w"], dx_out, name=f"proj_bwd_dx_{tag}")
    grads = dict(
        norm_w=dnorm_w[0], w_in_p=dw_in_p, ssd_conv_w=dconv_w[:SSD_CONV], ssd_conv_b=dconv_b[0],
        ssd_dt_bias=ddtb[0, :SSD_HEADS], ssd_a_log=da_log[0, :SSD_HEADS], ssd_d=dd[0, :SSD_HEADS],
        ssd_norm_w=dssd_norm_w[0], attn_sinks=dsinks[0, :ATTN_Q_HEADS], conf_dw_w=ddw_w[:CONF_KERNEL],
        conf_dw_b=ddw_b[0], conf_ln_w=dln_w[0], conf_ln_b=dln_b[0], w_out=dw_out)
    return dx_in, grads


def _local_step(x, target, param_fns, final_norm_w, first_after=None, on_grads=None):
    nbatch, seq, d = x.shape
    xt = x.reshape(nbatch * seq, d)
    saved, layer_params = [], []
    for li, fn in enumerate(param_fns):
        p = fn(xt)
        layer_params.append(p)
        xt, s = _layer_fwd(xt, p, nbatch, seq, f"l{li}", after=first_after if li == 0 else None)
        saved.append(s)
    loss, dx, dfinal = _loss_head(xt, target.reshape(nbatch * seq, d), final_norm_w.reshape(1, d), name="loss_head")
    grads = [None] * len(layer_params)
    hooks = None
    for li in reversed(range(len(layer_params))):
        dx, grads[li] = _layer_bwd(dx, layer_params[li], saved[li], nbatch, seq, f"l{li}", hooks=hooks)
        hooks = on_grads(li, grads[li]) if on_grads is not None else None
    return loss[0, 0], dx.reshape(nbatch, seq, d), grads, dfinal[0]


MESH = pl.DeviceIdType.MESH
N_CHIPS = 4


def _mesh_pos():
    return lax.axis_index("x"), lax.axis_index("y"), lax.axis_index("c")


def _other_chips(x, y):
    return [(1 - x, y), (x, 1 - y), (1 - x, 1 - y)]


def _gather_weights(big, small, name):
    nbig, nsmall = len(big), len(small)
    n_ici = 3 * (nbig + nsmall)
    n_fwd = 3 * nbig

    def body(*refs):
        ins = refs[:nbig + nsmall]
        outs = refs[nbig + nsmall:2 * (nbig + nsmall)]
        send_sems, recv_sems = refs[2 * (nbig + nsmall):]
        x, y, c = _mesh_pos()
        me = 2 * x + y
        sibling = (x, y, 1 - c)
        chips = _other_chips(x, y)

        def ici(a, j, origin, dest):
            if a < nbig:
                src = ins[a].at[c] if origin is None else outs[a].at[origin, c]
                dst = outs[a].at[me if origin is None else origin, c]
            else:
                src = ins[a] if origin is None else outs[a].at[origin]
                dst = outs[a].at[me if origin is None else origin]
            k = a * 3 + j
            return pltpu.make_async_remote_copy(src_ref=src, dst_ref=dst, send_sem=send_sems.at[k],
                                                recv_sem=recv_sems.at[k], device_id=dest, device_id_type=MESH)

        def fwd(a, j, origin, half):
            k = n_ici + a * 3 + j
            ref = outs[a].at[origin, half]
            return pltpu.make_async_remote_copy(src_ref=ref, dst_ref=ref, send_sem=send_sems.at[k],
                                                recv_sem=recv_sems.at[k], device_id=sibling, device_id_type=MESH)

        sends = []
        for j, (px, py) in enumerate(chips):
            for a in range(nbig + nsmall):
                cp = ici(a, j, None, (px, py, c))
                cp.start()
                sends.append(cp)
        for j, (px, py) in enumerate(chips):
            origin = 2 * px + py
            for a in range(nbig):
                ici(a, j, origin, (px, py, c)).wait_recv()
                cp = fwd(a, j, origin, c)
                cp.start()
                sends.append(cp)
        for j, (px, py) in enumerate(chips):
            origin = 2 * px + py
            for a in range(nbig, nbig + nsmall):
                ici(a, j, origin, (px, py, c)).wait_recv()
            for a in range(nbig):
                fwd(a, j, origin, 1 - c).wait_recv()
        for cp in sends:
            cp.wait_send()

    out_shape = tuple(jax.ShapeDtypeStruct((N_CHIPS,) + a.shape, a.dtype) for a in list(big) + list(small))
    return pl.pallas_call(
        body, name=name, out_shape=out_shape,
        in_specs=[ANY] * (nbig + nsmall), out_specs=tuple([ANY] * (nbig + nsmall)),
        scratch_shapes=[pltpu.SemaphoreType.DMA((n_ici + n_fwd,)), pltpu.SemaphoreType.DMA((n_ici + n_fwd,))],
    )(*big, *small)


def _pair_swap_halves(arrs, name):
    n = len(arrs)

    def body(*refs):
        ins, outs = refs[:n], refs[n:2 * n]
        send_sems, recv_sems = refs[2 * n:]
        x, y, c = _mesh_pos()
        cps = [pltpu.make_async_remote_copy(src_ref=ins[a].at[:, 1 - c], dst_ref=outs[a], send_sem=send_sems.at[a],
                                            recv_sem=recv_sems.at[a], device_id=(x, y, 1 - c), device_id_type=MESH)
               for a in range(n)]
        for cp in cps:
            cp.start()
        for cp in cps:
            cp.wait()

    return pl.pallas_call(
        body, name=name,
        out_shape=tuple(jax.ShapeDtypeStruct(a.shape[:1] + a.shape[2:], a.dtype) for a in arrs),
        in_specs=[ANY] * n, out_specs=tuple([ANY] * n),
        scratch_shapes=[pltpu.SemaphoreType.DMA((n,)), pltpu.SemaphoreType.DMA((n,))],
    )(*arrs)


HBM = pl.BlockSpec(memory_space=pltpu.HBM)
SEM = pl.BlockSpec(memory_space=pltpu.SEMAPHORE)
DATAFLOW = pltpu.SideEffectType.DATAFLOW_SIDE_EFFECTING


def _split_peers(pattern, x, y, c):
    if pattern == "swap":
        return [((x, y, 1 - c), 1 - c, None, None)]
    me = 2 * x + y
    return [((px, py, c), 2 * px + py if pattern == "scatter" else None, me, 2 * px + py)
            for px, py in _other_chips(x, y)]


def _split_land_shape(pattern, shape):
    return {"bcast": (N_CHIPS,) + shape, "scatter": shape, "swap": shape[:1] + shape[2:]}[pattern]


def _split_copies(pattern, srcs, lands, send_sems, recv_sems, waiting):
    x, y, c = _mesh_pos()
    peers = _split_peers(pattern, x, y, c)
    cps = []
    for j, (dev, src_slot, dst_slot, my_slot) in enumerate(peers):
        for a in range(len(srcs)):
            if src_slot is None:
                src = srcs[a]
            else:
                src = srcs[a].at[:, src_slot] if pattern == "swap" else srcs[a].at[src_slot]
            slot = my_slot if waiting else dst_slot
            dst = lands[a] if slot is None else lands[a].at[slot]
            k = a * len(peers) + j
            cps.append(pltpu.make_async_remote_copy(src_ref=src, dst_ref=dst, send_sem=send_sems[k],
                                                    recv_sem=recv_sems[k], device_id=dev, device_id_type=MESH))
    return cps


def _split_start(arrs, pattern, after, name):
    n = len(arrs)
    nsem = n * (1 if pattern == "swap" else N_CHIPS - 1)

    def body(*refs):
        srcs, lands = refs[:n], refs[n:2 * n]
        outs = refs[2 * n + 1:]
        for cp in _split_copies(pattern, srcs, lands, outs[:nsem], outs[nsem:2 * nsem], waiting=False):
            cp.start()
        outs[-1][...] = jnp.zeros_like(outs[-1])

    lands = [lax.empty(_split_land_shape(pattern, a.shape), a.dtype) for a in arrs]
    out_shape = ([pltpu.SemaphoreType.DMA(())] * (2 * nsem)
                 + [pltpu.HBM(a.shape, a.dtype) for a in arrs] + [pltpu.HBM(b.shape, b.dtype) for b in lands]
                 + [jax.ShapeDtypeStruct((SUBLANES, LANES), F32)])
    outs = pl.pallas_call(
        body, name=name, out_shape=tuple(out_shape),
        in_specs=[HBM] * (2 * n) + [ANY],
        out_specs=tuple([SEM] * (2 * nsem) + [HBM] * (2 * n) + [pl.BlockSpec(memory_space=pltpu.VMEM)]),
        input_output_aliases={a: 2 * nsem + a for a in range(2 * n)},
        compiler_params=pltpu.CompilerParams(has_side_effects=DATAFLOW),
    )(*[pltpu.with_memory_space_constraint(a, pltpu.HBM) for a in list(arrs) + lands], after)
    return outs[:-1], outs[-1]


def _split_wait(state, n, pattern, after, name):
    nsem = n * (1 if pattern == "swap" else N_CHIPS - 1)

    def body(*refs):
        srcs, lands = refs[:n], refs[n:2 * n]
        send_sems, recv_sems = refs[2 * n:2 * n + nsem], refs[2 * n + nsem:2 * n + 2 * nsem]
        for cp in _split_copies(pattern, srcs, lands, send_sems, recv_sems, waiting=True):
            cp.wait_send()
            cp.wait_recv()

    sems, thru = state[:2 * nsem], state[2 * nsem:]
    outs = pl.pallas_call(
        body, name=name, out_shape=tuple(pltpu.HBM(a.shape, a.dtype) for a in thru),
        in_specs=[HBM] * (2 * n) + [SEM] * (2 * nsem) + [ANY],
        out_specs=tuple([HBM] * (2 * n)),
        input_output_aliases={a: a for a in range(2 * n)},
        compiler_params=pltpu.CompilerParams(has_side_effects=DATAFLOW),
    )(*thru, *sems, after)
    return outs[n:]


def _pair_gather(arrs, layer, name):
    n = len(arrs)

    def body(*refs):
        outs = refs[n:2 * n]
        send_sems, recv_sems = refs[2 * n:]
        x, y, c = _mesh_pos()
        cps = [pltpu.make_async_remote_copy(src_ref=outs[a].at[layer, c], dst_ref=outs[a].at[layer, c],
                                            send_sem=send_sems.at[a], recv_sem=recv_sems.at[a],
                                            device_id=(x, y, 1 - c), device_id_type=MESH)
               for a in range(n)]
        for cp in cps:
            cp.start()
        for cp in cps:
            cp.wait()

    return pl.pallas_call(
        body, name=name, out_shape=tuple(jax.ShapeDtypeStruct(a.shape, a.dtype) for a in arrs),
        in_specs=[ANY] * n, out_specs=tuple([ANY] * n),
        input_output_aliases={a: a for a in range(n)},
        scratch_shapes=[pltpu.SemaphoreType.DMA((n,)), pltpu.SemaphoreType.DMA((n,))],
    )(*arrs)


N_DEV = 8


def _allreduce_small(pack, name):
    r = pack.shape[0]

    def body(p_ref, o_ref, land, send_sems, recv_sems):
        x, y, c = _mesh_pos()
        me = 4 * x + 2 * y + c
        cps = []
        for k in range(1, N_DEV):
            peer = (x ^ (k >> 2), y ^ ((k >> 1) & 1), c ^ (k & 1))
            cps.append(pltpu.make_async_remote_copy(src_ref=p_ref, dst_ref=land.at[me], send_sem=send_sems.at[k - 1],
                                                    recv_sem=recv_sems.at[k - 1], device_id=peer, device_id_type=MESH))
        for cp in cps:
            cp.start()
        land[me] = p_ref[...]
        for cp in cps:
            cp.wait()
        total = land[0]
        for d in range(1, N_DEV):
            total = total + land[d]
        o_ref[...] = total

    vm = pl.BlockSpec(memory_space=pltpu.VMEM)
    return pl.pallas_call(
        body, name=name, out_shape=jax.ShapeDtypeStruct(pack.shape, F32),
        in_specs=[vm], out_specs=vm,
        scratch_shapes=[pltpu.VMEM((N_DEV, r, LANES), F32), pltpu.SemaphoreType.DMA((N_DEV - 1,)),
                        pltpu.SemaphoreType.DMA((N_DEV - 1,))],
    )(pack)


BIG_ROWS = 128


def _cast_layer(w, layer, name):
    _, r, cdim = w.shape
    tr = BIG_ROWS

    def body(w_ref, o_ref):
        o_ref[...] = w_ref[...].astype(o_ref.dtype)

    return pl.pallas_call(
        body, name=name, out_shape=jax.ShapeDtypeStruct((r, cdim), MXU_DTYPE),
        grid=(r // tr,), in_specs=[pl.BlockSpec((None, tr, cdim), lambda i: (layer, i, 0))],
        out_specs=pl.BlockSpec((tr, cdim), lambda i: (i, 0)),
        compiler_params=_params(("parallel",)),
    )(w)


def _cast_cols_major(w_t, name):
    cdim, nl, r = w_t.shape
    tc = LANES

    def body(w_ref, *o_refs):
        for l in range(nl):
            o_refs[l][...] = w_ref[:, l, :].T.astype(o_refs[l].dtype)

    out = pl.BlockSpec((r, tc), lambda i: (0, i))
    return pl.pallas_call(
        body, name=name, out_shape=tuple(jax.ShapeDtypeStruct((r, cdim), MXU_DTYPE) for _ in range(nl)),
        grid=(pl.cdiv(cdim, tc),), in_specs=[pl.BlockSpec((tc, nl, r), lambda i: (i, 0, 0))],
        out_specs=tuple([out] * nl),
        compiler_params=_params(("parallel",)),
    )(w_t)


def _pair_sum(parts, sib, which, out_dtype, name):
    k, _, r, cdim = parts.shape
    tr = BIG_ROWS

    def body(sel_ref, p_ref, s_ref, o_ref):
        o_ref[...] = (p_ref[...] + s_ref[...]).astype(o_ref.dtype)

    grid_spec = pltpu.PrefetchScalarGridSpec(
        num_scalar_prefetch=1, grid=(k, r // tr),
        in_specs=[pl.BlockSpec((None, None, tr, cdim), lambda l, i, sel: (l, sel[0], i, 0)),
                  pl.BlockSpec((None, tr, cdim), lambda l, i, sel: (l, i, 0))],
        out_specs=pl.BlockSpec((None, tr, cdim), lambda l, i, sel: (l, i, 0)))
    return pl.pallas_call(
        body, name=name, out_shape=jax.ShapeDtypeStruct((k, r, cdim), out_dtype), grid_spec=grid_spec,
        compiler_params=_params(("parallel", "parallel")),
    )(which.reshape(1).astype(jnp.int32), parts, sib)


def _sum_lead(parts, into, layer, which, name):
    k, r, cdim = parts.shape
    tr = BIG_ROWS

    def body(sel_ref, p_ref, _, o_ref):
        total = p_ref[0].astype(F32)
        for a in range(1, k):
            total = total + p_ref[a].astype(F32)
        o_ref[...] = total

    grid_spec = pltpu.PrefetchScalarGridSpec(
        num_scalar_prefetch=1, grid=(r // tr,),
        in_specs=[pl.BlockSpec((k, tr, cdim), lambda i, sel: (0, i, 0)), ANY],
        out_specs=pl.BlockSpec((None, None, tr, cdim), lambda i, sel: (layer, sel[0], i, 0)))
    return pl.pallas_call(
        body, name=name, out_shape=jax.ShapeDtypeStruct(into.shape, F32), grid_spec=grid_spec,
        input_output_aliases={2: 0},
        compiler_params=_params(("parallel",)),
    )(which.reshape(1).astype(jnp.int32), parts, into)


def _adam_math(w, g, m, v):
    m2 = ADAM_B1 * m + (1.0 - ADAM_B1) * g
    v2 = ADAM_B2 * v + (1.0 - ADAM_B2) * (g * g)
    m_hat = m2 / (1.0 - ADAM_B1 ** ADAM_STEP)
    v_hat = v2 / (1.0 - ADAM_B2 ** ADAM_STEP)
    delta = -ADAM_LR * (m_hat / (jnp.sqrt(v_hat) + ADAM_EPS) + ADAM_WD * w)
    return delta, m2, v2


def _adam_big(w, g, m, v, name):
    nl, r, cdim = w.shape
    tr = BIG_ROWS

    def body(w_ref, g_ref, m_ref, v_ref, d_ref, mo_ref, vo_ref):
        delta, m2, v2 = _adam_math(w_ref[...], g_ref[...], m_ref[...], v_ref[...])
        d_ref[...] = delta
        mo_ref[...] = m2
        vo_ref[...] = v2

    blk = pl.BlockSpec((None, tr, cdim), lambda l, i: (l, i, 0))
    shp = jax.ShapeDtypeStruct(w.shape, F32)
    return pl.pallas_call(
        body, name=name, out_shape=(shp, shp, shp),
        grid=(nl, r // tr), in_specs=[blk] * 4, out_specs=(blk, blk, blk),
        compiler_params=_params(("parallel", "parallel")),
    )(w, g, m, v)


def _adam_cols_major(w, g, m, v, name):
    cdim, nl, r = w.shape
    tc = BIG_ROWS

    def body(w_ref, g_ref, m_ref, v_ref, d_ref, mo_ref, vo_ref):
        delta, m2, v2 = _adam_math(w_ref[...], g_ref[...], m_ref[...], v_ref[...])
        d_ref[...] = delta
        mo_ref[...] = m2
        vo_ref[...] = v2

    blk = pl.BlockSpec((tc, nl, r), lambda i: (i, 0, 0))
    shp = jax.ShapeDtypeStruct(w.shape, F32)
    return pl.pallas_call(
        body, name=name, out_shape=(shp, shp, shp),
        grid=(pl.cdiv(cdim, tc),), in_specs=[blk] * 4, out_specs=(blk, blk, blk),
        compiler_params=_params(("parallel",)),
    )(w, g, m, v)


def _adam_small(ws, gs, ms, vs, name):
    n = len(ws)

    def body(*refs):
        w_refs, g_refs, m_refs, v_refs = (refs[k * n:(k + 1) * n] for k in range(4))
        d_refs, mo_refs, vo_refs = (refs[(4 + k) * n:(5 + k) * n] for k in range(3))
        for a in range(n):
            delta, m2, v2 = _adam_math(w_refs[a][...], g_refs[a][...], m_refs[a][...], v_refs[a][...])
            d_refs[a][...] = delta
            mo_refs[a][...] = m2
            vo_refs[a][...] = v2

    shapes = tuple(jax.ShapeDtypeStruct(w.shape, F32) for w in ws)
    vm = pl.BlockSpec(memory_space=pltpu.VMEM)
    outs = pl.pallas_call(body, name=name, out_shape=shapes * 3, in_specs=[vm] * (4 * n),
                          out_specs=tuple([vm] * (3 * n)))(*ws, *gs, *ms, *vs)
    return outs[:n], outs[n:2 * n], outs[2 * n:]


PACK_TILE = SUBLANES * LANES


def _pack(arrays):
    rows = []
    for a in arrays:
        flat = a.reshape(-1)
        pad = (-flat.shape[0]) % PACK_TILE
        if pad:
            flat = jnp.concatenate([flat, jnp.zeros((pad,), flat.dtype)])
        rows.append(flat.reshape(-1, LANES))
    return jnp.concatenate(rows, axis=0)


def _unpack(pack, shapes):
    outs, row = [], 0
    for shp in shapes:
        n = int(np.prod(shp))
        nrows = -(-n // PACK_TILE) * SUBLANES
        outs.append(pack[row:row + nrows].reshape(-1)[:n].reshape(shp))
        row += nrows
    return outs


SMALL = ["norm_w", "ssd_conv_b", "ssd_dt_bias", "ssd_a_log", "ssd_d", "ssd_norm_w", "attn_sinks",
         "conf_dw_b", "conf_ln_w", "conf_ln_b"]
WEIGHTS = ["norm_w", "w_in", "ssd_conv_w", "ssd_conv_b", "ssd_dt_bias", "ssd_a_log", "ssd_d", "ssd_norm_w",
           "attn_sinks", "conf_dw_w", "conf_dw_b", "conf_ln_w", "conf_ln_b", "w_out", "final_norm_w"]


def kernel(x, norm_w, w_in, ssd_conv_w, ssd_conv_b, ssd_dt_bias, ssd_a_log, ssd_d, ssd_norm_w, attn_sinks, conf_dw_w, conf_dw_b, conf_ln_w, conf_ln_b, w_out, final_norm_w, loss_target, m_norm_w, m_w_in, m_ssd_conv_w, m_ssd_conv_b, m_ssd_dt_bias, m_ssd_a_log, m_ssd_d, m_ssd_norm_w, m_attn_sinks, m_conf_dw_w, m_conf_dw_b, m_conf_ln_w, m_conf_ln_b, m_w_out, m_final_norm_w, v_norm_w, v_w_in, v_ssd_conv_w, v_ssd_conv_b, v_ssd_dt_bias, v_ssd_a_log, v_ssd_d, v_ssd_norm_w, v_attn_sinks, v_conf_dw_w, v_conf_dw_b, v_conf_ln_w, v_conf_ln_b, v_w_out, v_final_norm_w):
    w = dict(norm_w=norm_w, w_in=w_in, ssd_conv_w=ssd_conv_w, ssd_conv_b=ssd_conv_b, ssd_dt_bias=ssd_dt_bias,
             ssd_a_log=ssd_a_log, ssd_d=ssd_d, ssd_norm_w=ssd_norm_w, attn_sinks=attn_sinks, conf_dw_w=conf_dw_w,
             conf_dw_b=conf_dw_b, conf_ln_w=conf_ln_w, conf_ln_b=conf_ln_b, w_out=w_out, final_norm_w=final_norm_w)
    m = dict(norm_w=m_norm_w, w_in=m_w_in, ssd_conv_w=m_ssd_conv_w, ssd_conv_b=m_ssd_conv_b,
             ssd_dt_bias=m_ssd_dt_bias, ssd_a_log=m_ssd_a_log, ssd_d=m_ssd_d, ssd_norm_w=m_ssd_norm_w,
             attn_sinks=m_attn_sinks, conf_dw_w=m_conf_dw_w, conf_dw_b=m_conf_dw_b, conf_ln_w=m_conf_ln_w,
             conf_ln_b=m_conf_ln_b, w_out=m_w_out, final_norm_w=m_final_norm_w)
    v = dict(norm_w=v_norm_w, w_in=v_w_in, ssd_conv_w=v_ssd_conv_w, ssd_conv_b=v_ssd_conv_b,
             ssd_dt_bias=v_ssd_dt_bias, ssd_a_log=v_ssd_a_log, ssd_d=v_ssd_d, ssd_norm_w=v_ssd_norm_w,
             attn_sinks=v_attn_sinks, conf_dw_w=v_conf_dw_w, conf_dw_b=v_conf_dw_b, conf_ln_w=v_conf_ln_w,
             conf_ln_b=v_conf_ln_b, w_out=v_w_out, final_norm_w=v_final_norm_w)
    depth = w_in.shape[0]
    me = 2 * lax.axis_index("x") + lax.axis_index("y")

    assert depth == 2
    w_in_t = jnp.transpose(w_in, (2, 0, 1))
    w_in_b = _cast_cols_major(w_in_t, name="cast_w_in")
    w_out_b = [_cast_layer(w_out, li, name=f"cast_w_out_l{li}") for li in range(depth)]
    own0 = [w_in_b[0].reshape((2, -1) + w_in_b[0].shape[1:]), w_out_b[0].reshape((2, -1) + w_out_b[0].shape[1:]),
            ssd_conv_w, conf_dw_w]
    gathered0 = _gather_weights(own0[:2], own0[2:], name="gather_weights_l0")
    g_in0, g_out0, g_conv, g_dw = [lax.dynamic_update_index_in_dim(g_all, mine, me, 0)
                                   for g_all, mine in zip(gathered0, own0)]
    own1 = [w_in_b[1], w_out_b[1]]
    pending1, token1 = _split_start(own1, "bcast", gathered0[0], name="gather_l1_start")

    def small_full(li):
        return (jnp.concatenate([g_conv[p, li] for p in range(N_CHIPS)], axis=1),
                jnp.concatenate([g_dw[p, li] for p in range(N_CHIPS)], axis=1))

    def params_l0(_):
        w_in_p = _padded_from_chips([g_in0[p].reshape(w_in_b[0].shape) for p in range(N_CHIPS)])
        w_out_full = g_out0.reshape(-1, g_out0.shape[-1])
        return _layer_params(0, w_in_p, w_out_full, *small_full(0), w)

    def params_l1(layer_input):
        landed = _split_wait(pending1, len(own1), "bcast", layer_input, name="gather_l1_wait")
        g_in1, g_out1 = [lax.dynamic_update_index_in_dim(g_all, mine, me, 0) for g_all, mine in zip(landed, own1)]
        w_in_p = _padded_from_chips([g_in1[p] for p in range(N_CHIPS)])
        return _layer_params(1, w_in_p, g_out1.reshape(-1, g_out1.shape[-1]), *small_full(1), w)

    c = lax.axis_index("c")
    cols = w_in.shape[2]
    rows_out = w_out.shape[1]

    def grad_parts(g):
        dw = g["w_in_p"]
        p_in = jnp.stack([_chip_part_from_padded(dw, p, cols) for p in range(N_CHIPS)])
        return [p_in.reshape(N_CHIPS, 2, dw.shape[0] // 2, cols),
                g["w_out"].reshape(N_CHIPS, 2, rows_out // 2, D_MODEL)]

    def pair_sums(parts, sib, tag):
        return [_pair_sum(p, sb, c, MXU_DTYPE, name=f"grad_pair_sum_{k}_{tag}")
                for k, (p, sb) in enumerate(zip(parts, sib))]

    split = {"reduced": [lax.empty((depth, 2, w_in.shape[1] // 2, cols), F32),
                         lax.empty((depth, 2, rows_out // 2, D_MODEL), F32)]}

    def chip_sums(landed, sent, li):
        filled = [lax.dynamic_update_index_in_dim(r, lax.dynamic_index_in_dim(sk, me, 0, keepdims=False), me, 0)
                  for r, sk in zip(landed, sent)]
        halves = [_sum_lead(r, into, li, c, name=f"grad_chip_sum_{k}_l{li}")
                  for k, (r, into) in enumerate(zip(filled, split["reduced"]))]
        split["reduced"] = list(_pair_gather(halves, li, name=f"grad_pair_gather_l{li}"))

    def on_grads(li, g):
        if li != depth - 1:
            return None
        parts = grad_parts(g)
        swap_state, swap_token = _split_start(parts, "swap", g["w_out"], name="grad_swap_l1_start")

        def after_dycat(dycat):
            sib = _split_wait(swap_state, len(parts), "swap", dycat, name="grad_swap_l1_wait")
            split["sent"] = pair_sums(parts, sib, "l1")
            split["scatter"], token = _split_start(split["sent"], "scatter", split["sent"][0],
                                                   name="grad_scatter_l1_start")
            return token

        def after_attn(dproj):
            landed = _split_wait(split["scatter"], len(parts), "scatter", dproj, name="grad_scatter_l1_wait")
            chip_sums(landed, split["sent"], depth - 1)

        return {"start_token": swap_token, "after_dycat": after_dycat, "after_attn": after_attn}

    loss, grad_x, grads, dfinal = _local_step(x, loss_target, [params_l0, params_l1], final_norm_w,
                                              first_after=token1, on_grads=on_grads)

    parts0 = grad_parts(grads[0])
    sent0 = pair_sums(parts0, _pair_swap_halves(parts0, name="grad_pair_swap_l0"), "l0")
    scatter0, token0 = _split_start(sent0, "scatter", sent0[0], name="grad_scatter_l0_start")

    small_list = [grads[li][n] for li in range(depth) for n in SMALL]
    small_list += [grads[li][n] for li in range(depth) for n in ("ssd_conv_w", "conf_dw_w")]
    small_list += [dfinal, loss.reshape(1)]
    small_shapes = [a.shape for a in small_list]
    reduced = _unpack(_allreduce_small(_pack(small_list) + token0[0, 0], name="allreduce_small"), small_shapes)
    ns = len(SMALL)
    g = {n: jnp.stack([reduced[li * ns + i] for li in range(depth)]) for i, n in enumerate(SMALL)}
    conv_w_cols, dw_w_cols = ssd_conv_w.shape[2], conf_dw_w.shape[2]
    g["ssd_conv_w"] = jnp.stack([lax.dynamic_slice_in_dim(reduced[depth * ns + 2 * li], me * conv_w_cols,
                                                          conv_w_cols, axis=1) for li in range(depth)])
    g["conf_dw_w"] = jnp.stack([lax.dynamic_slice_in_dim(reduced[depth * ns + 2 * li + 1], me * dw_w_cols,
                                                         dw_w_cols, axis=1) for li in range(depth)])
    g["final_norm_w"] = reduced[-2]
    loss_total = reduced[-1][0]

    small_names = [n for n in WEIGHTS if n not in ("w_in", "w_out")]

    def as2d(a):
        return a.reshape(1, -1) if a.ndim == 1 else a

    deltas, new_ms, new_vs = _adam_small(*[[as2d(src[n]) for n in small_names] for src in (w, g, m, v)],
                                         name="adam_small")

    chip_sums(_split_wait(scatter0, len(sent0), "scatter", deltas[0], name="grad_scatter_l0_wait"), sent0, 0)
    g_w_in = split["reduced"][0].reshape(w_in.shape)
    g_w_out = split["reduced"][1].reshape(w_out.shape)

    outs_g, outs_d, outs_m, outs_v = {"w_in": g_w_in, "w_out": g_w_out}, {}, {}, {}
    to_cols, from_cols = (2, 0, 1), (1, 2, 0)
    outs_d["w_in"], outs_m["w_in"], outs_v["w_in"] = [
        jnp.transpose(a, from_cols) for a in _adam_cols_major(
            *[jnp.transpose(a, to_cols) for a in (w_in, g_w_in, m_w_in, v_w_in)], name="adam_w_in")]
    outs_d["w_out"], outs_m["w_out"], outs_v["w_out"] = _adam_big(w_out, g_w_out, m_w_out, v_w_out,
                                                                  name="adam_w_out")
    for n, dn, mn, vn in zip(small_names, deltas, new_ms, new_vs):
        outs_g[n], outs_d[n], outs_m[n], outs_v[n] = (g[n], dn.reshape(w[n].shape), mn.reshape(w[n].shape),
                                                      vn.reshape(w[n].shape))
    return (loss_total, grad_x, *[outs_g[n] for n in WEIGHTS], *[outs_d[n] for n in WEIGHTS],
            *[outs_m[n] for n in WEIGHTS], *[outs_v[n] for n in WEIGHTS])
```

```python
import functools
import math

import jax
import jax.numpy as jnp
import numpy as np
from jax import lax
from jax.experimental import pallas as pl
from jax.experimental.pallas import tpu as pltpu

F32 = jnp.float32
BF16 = jnp.bfloat16
MXU_DTYPE = BF16

D_MODEL = 1024
DEPTH = 2
SSD_HEADS = 16
SSD_HEAD_DIM = 64
SSD_STATE = 128
SSD_CONV = 4
CHUNK = 128
SSD_CONV_DIM = 1536
ATTN_HEAD_DIM = 64
ATTN_Q_HEADS = 8
WINDOW = 128
CONF_WIDTH = 512
CONF_KERNEL = 31
MIX_WIDTH = 2048
D_IN_PROJ = 5392
EPS = 1e-5

ADAM_LR = 0.001
ADAM_B1 = 0.9
ADAM_B2 = 0.999
ADAM_EPS = 1e-08
ADAM_WD = 0.01
ADAM_STEP = 10

LANES = 128
SUBLANES = 8
VMEM_LIMIT = 48 * 1024 * 1024

NP = 5632
OFF_ZA, OFF_Q, OFF_K, OFF_V, OFF_DT = 0, 512, 1024, 1152, 1280
ATTN_GROUP = 1536
OFF_XBC = 1536
OFF_CONF = 3072
OFF_ZS = 4096
OFF_ZC = 5120
SECTIONS = ((0, 1024, OFF_ZS), (1024, 1536, OFF_ZA), (1536, 2048, OFF_ZC), (2048, 3584, OFF_XBC),
            (3584, 3600, OFF_DT), (3600, 4368, OFF_Q), (4368, 5392, OFF_CONF))

YCAT_ATTN, YCAT_CONF = 1024, 1536
ANY = pl.BlockSpec(memory_space=pl.ANY)

NN = (((1,), (0,)), ((), ()))
NT = (((1,), (1,)), ((), ()))
TN = (((0,), (0,)), ((), ()))


def _params(sem):
    return pltpu.CompilerParams(dimension_semantics=sem, vmem_limit_bytes=VMEM_LIMIT)


def _dot(a, b, dims=NN):
    return lax.dot_general(a.astype(MXU_DTYPE), b.astype(MXU_DTYPE), dims, preferred_element_type=F32)


def _split_bf16(a, passes):
    pieces = []
    r = a
    for _ in range(passes):
        p = r.astype(BF16)
        pieces.append(p)
        r = r - p.astype(F32)
    return pieces


def _xdot(a, sel, dims=NN, passes=2):
    out = None
    for p in _split_bf16(a, passes):
        t = lax.dot_general(p, sel, dims, preferred_element_type=F32)
        out = t if out is None else out + t
    return out


def _xdot_r(sel, b, dims=NN, passes=3):
    out = None
    for p in _split_bf16(b, passes):
        t = lax.dot_general(sel, p, dims, preferred_element_type=F32)
        out = t if out is None else out + t
    return out


def _sigmoid(x):
    return 1.0 / (1.0 + jnp.exp(-x))


def _silu(x):
    return x * _sigmoid(x)


def _dsilu(x):
    s = _sigmoid(x)
    return s * (1.0 + x * (1.0 - s))


def _softplus(x):
    return jnp.maximum(x, 0.0) + jnp.log(1.0 + jnp.exp(-jnp.abs(x)))


def _rowsum8(x):
    r, c = x.shape
    return jnp.sum(x.reshape(r // SUBLANES, SUBLANES, c), axis=0)


def _iota(shape, dim):
    return lax.broadcasted_iota(jnp.int32, shape, dim)


def _matmul(a, b, form, out_dtype, tm, tn, tk, name, residual=None, after=None):
    if form == "nn":
        (m, k), n = a.shape, b.shape[1]
    elif form == "nt":
        (m, k), n = a.shape, b.shape[0]
    else:
        (k, m), n = a.shape, b.shape[1]
    tm, tn, tk = min(tm, m), min(tn, n), min(tk, k)
    assert m % tm == 0 and n % tn == 0 and k % tk == 0, (name, m, n, k, tm, tn, tk)
    if form == "nn":
        a_spec = pl.BlockSpec((tm, tk), lambda i, j, s: (i, s))
        b_spec = pl.BlockSpec((tk, tn), lambda i, j, s: (s, j))
        dims = NN
    elif form == "nt":
        (m, k), n = a.shape, b.shape[0]
        a_spec = pl.BlockSpec((tm, tk), lambda i, j, s: (i, s))
        b_spec = pl.BlockSpec((tn, tk), lambda i, j, s: (j, s))
        dims = NT
    else:
        (k, m), n = a.shape, b.shape[1]
        a_spec = pl.BlockSpec((tk, tm), lambda i, j, s: (s, i))
        b_spec = pl.BlockSpec((tk, tn), lambda i, j, s: (s, j))
        dims = TN
    nk = k // tk
    has_res = residual is not None
    deps = [] if after is None else [after]

    def body_single(a_ref, b_ref, *rest):
        o = _dot(a_ref[...], b_ref[...], dims)
        if has_res:
            o = o + rest[0][...]
        rest[-1][...] = o.astype(out_dtype)

    def body(a_ref, b_ref, *rest):
        r_ref = rest[0] if has_res else None
        o_ref, acc = rest[-2:]
        s = pl.program_id(2)

        @pl.when(s == 0)
        def _():
            acc[...] = jnp.zeros_like(acc)

        acc[...] += _dot(a_ref[...], b_ref[...], dims)

        @pl.when(s == nk - 1)
        def _():
            o = acc[...]
            if has_res:
                o = o + r_ref[...]
            o_ref[...] = o.astype(out_dtype)

    in_specs = [a_spec, b_spec]
    args = [a, b]
    if has_res:
        in_specs.append(pl.BlockSpec((tm, tn), lambda i, j, s: (i, j)))
        args.append(residual)
    in_specs += [ANY] * len(deps)
    args += deps
    return pl.pallas_call(
        body_single if nk == 1 else body, name=name,
        out_shape=jax.ShapeDtypeStruct((m, n), out_dtype),
        grid=(m // tm, n // tn, nk),
        in_specs=in_specs,
        out_specs=pl.BlockSpec((tm, tn), lambda i, j, s: (i, j)),
        scratch_shapes=[] if nk == 1 else [pltpu.VMEM((tm, tn), F32)],
        compiler_params=_params(("parallel", "parallel", "arbitrary")),
    )(*args)


ROW_TILE = 256


PROJ_FWD_TM, PROJ_FWD_TN = 1024, 512


def _proj_fwd(x, w, w_in_p, name, after=None):
    t, d = x.shape
    n = w_in_p.shape[1]
    tm, tn = min(PROJ_FWD_TM, t), PROJ_FWD_TN
    assert t % tm == 0 and n % tn == 0
    deps = [] if after is None else [after]

    def body(x_ref, w_ref, b_ref, *rest):
        o_ref, ot_ref, h_scr = rest[len(deps):]

        @pl.when(pl.program_id(1) == 0)
        def _():
            xv = x_ref[...]
            rstd = lax.rsqrt(jnp.mean(xv * xv, axis=-1, keepdims=True) + EPS)
            h = xv * rstd * w_ref[...]
            h_scr[...] = h.astype(h_scr.dtype)
            ot_ref[...] = h.T.astype(ot_ref.dtype)

        o_ref[...] = _dot(h_scr[...], b_ref[...])

    return pl.pallas_call(
        body, name=name,
        out_shape=(jax.ShapeDtypeStruct((t, n), F32), jax.ShapeDtypeStruct((d, t), MXU_DTYPE)),
        grid=(t // tm, n // tn),
        in_specs=[pl.BlockSpec((tm, d), lambda i, j: (i, 0)), pl.BlockSpec((1, d), lambda i, j: (0, 0)),
                  pl.BlockSpec((d, tn), lambda i, j: (0, j))] + [ANY] * len(deps),
        out_specs=(pl.BlockSpec((tm, tn), lambda i, j: (i, j)), pl.BlockSpec((d, tm), lambda i, j: (0, i))),
        scratch_shapes=[pltpu.VMEM((tm, d), MXU_DTYPE)],
        compiler_params=_params(("parallel", "arbitrary")),
    )(x, w, w_in_p, *deps)


PROJ_BWD_TM, PROJ_BWD_TK = 1024, 1408


def _proj_bwd_dx(dproj, w_in_p, x, w, dres, name):
    t, d = x.shape
    kdim = dproj.shape[1]
    tm, tk = min(PROJ_BWD_TM, t), PROJ_BWD_TK
    nt, nk = t // tm, kdim // tk
    assert t % tm == 0 and kdim % tk == 0

    def body(a_ref, b_ref, x_ref, w_ref, dr_ref, dx_ref, dw_ref, acc, wacc):
        i, s = pl.program_id(0), pl.program_id(1)

        @pl.when((i == 0) & (s == 0))
        def _():
            wacc[...] = jnp.zeros_like(wacc)

        @pl.when(s == 0)
        def _():
            acc[...] = jnp.zeros_like(acc)

        acc[...] += _dot(a_ref[...], b_ref[...], NT)

        @pl.when(s == nk - 1)
        def _():
            xv = x_ref[...]
            rstd = lax.rsqrt(jnp.mean(xv * xv, axis=-1, keepdims=True) + EPS)
            xh = xv * rstd
            dhv = acc[...]
            g = dhv * w_ref[...]
            dx_ref[...] = dr_ref[...] + rstd * (g - xh * jnp.mean(g * xh, axis=-1, keepdims=True))
            wacc[...] += _rowsum8(dhv * xh)

        @pl.when((i == nt - 1) & (s == nk - 1))
        def _():
            dw_ref[...] = jnp.sum(wacc[...], axis=0, keepdims=True)

    row = pl.BlockSpec((tm, d), lambda i, s: (i, 0))
    vec = pl.BlockSpec((1, d), lambda i, s: (0, 0))
    return pl.pallas_call(
        body, name=name,
        out_shape=(jax.ShapeDtypeStruct((t, d), F32), jax.ShapeDtypeStruct((1, d), F32)),
        grid=(nt, nk),
        in_specs=[pl.BlockSpec((tm, tk), lambda i, s: (i, s)), pl.BlockSpec((d, tk), lambda i, s: (0, s)),
                  row, vec, row],
        out_specs=(row, vec),
        scratch_shapes=[pltpu.VMEM((tm, d), F32), pltpu.VMEM((SUBLANES, d), F32)],
        compiler_params=_params(("arbitrary", "arbitrary")),
    )(dproj, w_in_p, x, w, dres)


def _loss_head(xf, target, w, name):
    t, d = xf.shape
    tm = ROW_TILE
    nt = t // tm

    def body(x_ref, t_ref, w_ref, loss_ref, dx_ref, dw_ref, lacc, wacc):
        i = pl.program_id(0)

        @pl.when(i == 0)
        def _():
            lacc[...] = jnp.zeros_like(lacc)
            wacc[...] = jnp.zeros_like(wacc)

        xv = x_ref[...]
        rstd = lax.rsqrt(jnp.mean(xv * xv, axis=-1, keepdims=True) + EPS)
        xh = xv * rstd
        err = xh * w_ref[...] - t_ref[...]
        lacc[...] += jnp.sum(err * err)
        dy = err * (1.0 / d)
        g = dy * w_ref[...]
        dx_ref[...] = rstd * (g - xh * jnp.mean(g * xh, axis=-1, keepdims=True))
        wacc[...] += _rowsum8(dy * xh)

        @pl.when(i == nt - 1)
        def _():
            loss_ref[...] = lacc[...] * (0.5 / d)
            dw_ref[...] = jnp.sum(wacc[...], axis=0, keepdims=True)

    row = pl.BlockSpec((tm, d), lambda i: (i, 0))
    vec = pl.BlockSpec((1, d), lambda i: (0, 0))
    return pl.pallas_call(
        body, name=name,
        out_shape=(jax.ShapeDtypeStruct((SUBLANES, LANES), F32), jax.ShapeDtypeStruct((t, d), F32),
                   jax.ShapeDtypeStruct((1, d), F32)),
        grid=(nt,),
        in_specs=[row, row, vec],
        out_specs=(pl.BlockSpec((SUBLANES, LANES), lambda i: (0, 0)), row, vec),
        scratch_shapes=[pltpu.VMEM((SUBLANES, LANES), F32), pltpu.VMEM((SUBLANES, d), F32)],
        compiler_params=_params(("arbitrary",)),
    )(xf, target, w)


def _conf_post_bwd(dycat, c1, proj, ln_w, ln_b, dproj, name):
    t = c1.shape[0]
    tm, cw = ROW_TILE, CONF_WIDTH
    nt = t // tm

    def body(dy_ref, c_ref, z_ref, w_ref, b_ref, _, dc_ref, dz_ref, dw_ref, db_ref, wacc, bacc):
        i = pl.program_id(0)

        @pl.when(i == 0)
        def _():
            wacc[...] = jnp.zeros_like(wacc)
            bacc[...] = jnp.zeros_like(bacc)

        cv = c_ref[...]
        xc = cv - jnp.mean(cv, axis=-1, keepdims=True)
        rstd = lax.rsqrt(jnp.mean(xc * xc, axis=-1, keepdims=True) + EPS)
        xh = xc * rstd
        c2 = xh * w_ref[...] + b_ref[...]
        zv = z_ref[...]
        dy = dy_ref[...]
        dz_ref[...] = (dy * _silu(c2) * _dsilu(zv)).astype(dz_ref.dtype)
        dc2 = dy * _silu(zv) * _dsilu(c2)
        bacc[...] += _rowsum8(dc2)
        wacc[...] += _rowsum8(dc2 * xh)
        dxh = dc2 * w_ref[...]
        dc_ref[...] = rstd * (dxh - jnp.mean(dxh, axis=-1, keepdims=True)
                              - xh * jnp.mean(dxh * xh, axis=-1, keepdims=True))

        @pl.when(i == nt - 1)
        def _():
            dw_ref[...] = jnp.sum(wacc[...], axis=0, keepdims=True)
            db_ref[...] = jnp.sum(bacc[...], axis=0, keepdims=True)

    row = pl.BlockSpec((tm, cw), lambda i: (i, 0))
    vec = pl.BlockSpec((1, cw), lambda i: (0, 0))
    return pl.pallas_call(
        body, name=name,
        out_shape=(jax.ShapeDtypeStruct((t, cw), F32), jax.ShapeDtypeStruct(dproj.shape, dproj.dtype),
                   jax.ShapeDtypeStruct((1, cw), F32), jax.ShapeDtypeStruct((1, cw), F32)),
        grid=(nt,),
        in_specs=[pl.BlockSpec((tm, cw), lambda i: (i, YCAT_CONF // cw)), row,
                  pl.BlockSpec((tm, cw), lambda i: (i, OFF_ZC // cw)), vec, vec, ANY],
        out_specs=(row, pl.BlockSpec((tm, cw), lambda i: (i, OFF_ZC // cw)), vec, vec),
        input_output_aliases={5: 1},
        scratch_shapes=[pltpu.VMEM((SUBLANES, cw), F32), pltpu.VMEM((SUBLANES, cw), F32)],
        compiler_params=_params(("arbitrary",)),
    )(dycat, c1, proj, ln_w, ln_b, dproj)


CONV_TILE = 512
CONV_COLS = 512
CONV_SUB_ROWS = 128
CONV_SUB_COLS = LANES


def _conv_halo(k):
    return SUBLANES if k - 1 <= SUBLANES else 32


def _conv_subtiles(tm, cw):
    return [(r0, c0) for r0 in range(0, tm, CONV_SUB_ROWS) for c0 in range(0, cw, CONV_SUB_COLS)]


def _conv_use_shifted(k):
    return k > SUBLANES


def _conv_shift_scratch(k, rows, cw):
    return [pltpu.VMEM((SUBLANES - 1, rows - SUBLANES, cw), F32)] if _conv_use_shifted(k) else []


def _conv_fill_shifted(ext, sh):
    n = sh.shape[1]
    for b in range(1, SUBLANES):
        sh[b - 1] = ext[b:b + n, :]


def _conv_rows(ext, sh, start, rows, cs):
    b = start % SUBLANES
    if b == 0 or not sh:
        return ext[start:start + rows, cs]
    return sh[0][b - 1, start - b:start - b + rows, cs]


def _conv_fwd(src, col0, width, w, bias, k, seq, name):
    t = src.shape[0]
    tm, cw, halo = CONV_TILE, CONV_COLS, _conv_halo(k)
    sr, sc = CONV_SUB_ROWS, CONV_SUB_COLS
    p = k - 1
    cb0 = col0 // cw
    kp = w.shape[0]

    shifted = _conv_use_shifted(k)

    def body(x_ref, h_ref, w_ref, b_ref, o_ref, ext, *sh):
        i = pl.program_id(0)
        seq_start = (i * tm) % seq == 0
        ext[halo:, :] = x_ref[...]
        ext[:halo, :] = jnp.where(seq_start, 0.0, h_ref[...])
        if shifted:
            _conv_fill_shifted(ext, sh[0])
        for r0, c0 in _conv_subtiles(tm, cw):
            cs = slice(c0, c0 + sc)
            acc = jnp.zeros((sr, sc), F32) + b_ref[:, cs]
            for j in range(k):
                acc = acc + w_ref[j:j + 1, cs] * _conv_rows(ext, sh, r0 + halo - p + j, sr, cs)
            o_ref[r0:r0 + sr, cs] = acc

    return pl.pallas_call(
        body, name=name,
        out_shape=jax.ShapeDtypeStruct((t, width), F32),
        grid=(t // tm, width // cw),
        in_specs=[pl.BlockSpec((tm, cw), lambda i, j: (i, cb0 + j)),
                  pl.BlockSpec((halo, cw), lambda i, j: (jnp.maximum(i * (tm // halo) - 1, 0), cb0 + j)),
                  pl.BlockSpec((kp, cw), lambda i, j: (0, j)),
                  pl.BlockSpec((1, cw), lambda i, j: (0, j))],
        out_specs=pl.BlockSpec((tm, cw), lambda i, j: (i, j)),
        scratch_shapes=[pltpu.VMEM((halo + tm, cw), F32)] + _conv_shift_scratch(k, halo + tm, cw),
        compiler_params=_params(("parallel", "parallel")),
    )(src, src, w, bias)


def _conv_bwd(dy, src, col0, width, w, k, seq, name, into=None):
    t = src.shape[0]
    tm, cw, halo = CONV_TILE, CONV_COLS, _conv_halo(k)
    sr, sc = CONV_SUB_ROWS, CONV_SUB_COLS
    p = k - 1
    cb0 = col0 // cw
    kp = w.shape[0]
    nt = t // tm
    last_halo = t // halo - 1

    shifted = _conv_use_shifted(k)

    def body(dy_ref, dn_ref, x_ref, xp_ref, w_ref, *rest):
        if into is not None:
            rest = rest[1:]
        dx_ref, dw_ref, db_ref, dyext, xext, wacc, bacc = rest[:7]
        sh = rest[7:]
        i = pl.program_id(1)
        dysh, xsh = (sh[:1], sh[1:]) if shifted else ((), ())

        @pl.when(i == 0)
        def _():
            wacc[...] = jnp.zeros_like(wacc)
            bacc[...] = jnp.zeros_like(bacc)

        seq_start = (i * tm) % seq == 0
        seq_end = ((i + 1) * tm) % seq == 0
        dyext[:tm, :] = dy_ref[...]
        dyext[tm:, :] = jnp.where(seq_end, 0.0, dn_ref[...])
        xext[halo:, :] = x_ref[...]
        xext[:halo, :] = jnp.where(seq_start, 0.0, xp_ref[...])
        if shifted:
            _conv_fill_shifted(dyext, dysh[0])
            _conv_fill_shifted(xext, xsh[0])
        for r0, c0 in _conv_subtiles(tm, cw):
            cs = slice(c0, c0 + sc)
            dyv = dy_ref[r0:r0 + sr, cs]
            acc = jnp.zeros((sr, sc), F32)
            for j in range(k):
                acc = acc + w_ref[j:j + 1, cs] * _conv_rows(dyext, dysh, r0 + p - j, sr, cs)
                wacc[j, :, cs] += _rowsum8(dyv * _conv_rows(xext, xsh, r0 + halo - p + j, sr, cs))
            dx_ref[r0:r0 + sr, cs] = acc.astype(dx_ref.dtype)
            bacc[:, cs] += _rowsum8(dyv)

        @pl.when(i == nt - 1)
        def _():
            dw_ref[...] = jnp.zeros_like(dw_ref)
            for j in range(k):
                dw_ref[j:j + 1, :] = jnp.sum(wacc[j], axis=0, keepdims=True)
            db_ref[...] = jnp.sum(bacc[...], axis=0, keepdims=True)

    if into is None:
        dx_shape = jax.ShapeDtypeStruct((t, width), F32)
        dx_spec = pl.BlockSpec((tm, cw), lambda j, i: (i, j))
        extra_specs, extra_args, aliases = [], [], {}
    else:
        dx_shape = jax.ShapeDtypeStruct(into.shape, into.dtype)
        dx_spec = pl.BlockSpec((tm, cw), lambda j, i: (i, cb0 + j))
        extra_specs, extra_args, aliases = [ANY], [into], {5: 0}
    return pl.pallas_call(
        body, name=name,
        out_shape=(dx_shape, jax.ShapeDtypeStruct((kp, width), F32), jax.ShapeDtypeStruct((1, width), F32)),
        grid=(width // cw, nt),
        in_specs=[pl.BlockSpec((tm, cw), lambda j, i: (i, j)),
                  pl.BlockSpec((halo, cw), lambda j, i: (jnp.minimum((i + 1) * (tm // halo), last_halo), j)),
                  pl.BlockSpec((tm, cw), lambda j, i: (i, cb0 + j)),
                  pl.BlockSpec((halo, cw), lambda j, i: (jnp.maximum(i * (tm // halo) - 1, 0), cb0 + j)),
                  pl.BlockSpec((kp, cw), lambda j, i: (0, j))] + extra_specs,
        out_specs=(dx_spec,
                   pl.BlockSpec((kp, cw), lambda j, i: (0, j)),
                   pl.BlockSpec((1, cw), lambda j, i: (0, j))),
        input_output_aliases=aliases,
        scratch_shapes=[pltpu.VMEM((tm + halo, cw), F32), pltpu.VMEM((halo + tm, cw), F32),
                        pltpu.VMEM((kp, SUBLANES, cw), F32), pltpu.VMEM((SUBLANES, cw), F32)]
        + 2 * _conv_shift_scratch(k, halo + tm, cw),
        compiler_params=_params(("parallel", "arbitrary")),
    )(dy, dy, src, src, w, *extra_args)


def _conf_specs(tm, cw, halo, order):
    cb = OFF_CONF // cw

    def blk(col):
        return pl.BlockSpec((tm, cw), lambda *g: (order(*g), col))

    def prev(col):
        return pl.BlockSpec((halo, cw), lambda *g: (jnp.maximum(order(*g) * (tm // halo) - 1, 0), col))

    return blk(cb), prev(cb), blk(cb + 1), prev(cb + 1)


def _glu_window(ext, a_ref, ah_ref, g_ref, gh_ref, seq_start, halo):
    ext[halo:, :] = a_ref[...] * _sigmoid(g_ref[...])
    ext[:halo, :] = jnp.where(seq_start, 0.0, ah_ref[...] * _sigmoid(gh_ref[...]))


def _conf_fwd(proj, w, bias, ln_w, ln_b, ycat, seq, name):
    t = proj.shape[0]
    k = CONF_KERNEL
    tm, cw, halo = CONV_TILE, CONF_WIDTH, _conv_halo(k)
    sr, sc = CONV_SUB_ROWS, CONV_SUB_COLS
    p = k - 1
    kp = w.shape[0]

    def body(a_ref, ah_ref, g_ref, gh_ref, z_ref, w_ref, b_ref, lw_ref, lb_ref, _, c1_ref, y_ref, ext, sh):
        i = pl.program_id(0)
        _glu_window(ext, a_ref, ah_ref, g_ref, gh_ref, (i * tm) % seq == 0, halo)
        _conv_fill_shifted(ext, sh)
        for r0, c0 in _conv_subtiles(tm, cw):
            cs = slice(c0, c0 + sc)
            acc = jnp.zeros((sr, sc), F32) + b_ref[:, cs]
            for j in range(k):
                acc = acc + w_ref[j:j + 1, cs] * _conv_rows(ext, (sh,), r0 + halo - p + j, sr, cs)
            c1_ref[r0:r0 + sr, cs] = acc
        for r0 in range(0, tm, sr):
            rows = slice(r0, r0 + sr)
            cv = c1_ref[rows, :]
            xc = cv - jnp.mean(cv, axis=-1, keepdims=True)
            rstd = lax.rsqrt(jnp.mean(xc * xc, axis=-1, keepdims=True) + EPS)
            c2 = xc * rstd * lw_ref[...] + lb_ref[...]
            y_ref[rows, :] = (_silu(c2) * _silu(z_ref[rows, :])).astype(y_ref.dtype)

    vec = pl.BlockSpec((1, cw), lambda i: (0, 0))
    row = pl.BlockSpec((tm, cw), lambda i: (i, 0))
    return pl.pallas_call(
        body, name=name,
        out_shape=(jax.ShapeDtypeStruct((t, cw), F32), jax.ShapeDtypeStruct(ycat.shape, ycat.dtype)),
        grid=(t // tm,),
        in_specs=[*_conf_specs(tm, cw, halo, lambda i: i),
                  pl.BlockSpec((tm, cw), lambda i: (i, OFF_ZC // cw)),
                  pl.BlockSpec((kp, cw), lambda i: (0, 0)), vec, vec, vec, ANY],
        out_specs=(row, pl.BlockSpec((tm, cw), lambda i: (i, YCAT_CONF // cw))),
        input_output_aliases={9: 1},
        scratch_shapes=[pltpu.VMEM((halo + tm, cw), F32)] + _conv_shift_scratch(k, halo + tm, cw),
        compiler_params=_params(("parallel",)),
    )(proj, proj, proj, proj, proj, w, bias, ln_w, ln_b, ycat)


def _conf_conv_bwd(dc1, proj, w, dproj, seq, name):
    t = proj.shape[0]
    k = CONF_KERNEL
    tm, cw, halo = CONV_TILE, CONF_WIDTH, _conv_halo(k)
    sr, sc = CONV_SUB_ROWS, CONV_SUB_COLS
    p = k - 1
    kp = w.shape[0]
    nt = t // tm
    last_halo = t // halo - 1

    def body(dy_ref, dn_ref, a_ref, ah_ref, g_ref, gh_ref, w_ref, _, dag_ref, dw_ref, db_ref,
             dyext, xext, wacc, bacc, dysh, xsh):
        i = pl.program_id(0)

        @pl.when(i == 0)
        def _():
            wacc[...] = jnp.zeros_like(wacc)
            bacc[...] = jnp.zeros_like(bacc)

        seq_end = ((i + 1) * tm) % seq == 0
        dyext[:tm, :] = dy_ref[...]
        dyext[tm:, :] = jnp.where(seq_end, 0.0, dn_ref[...])
        _glu_window(xext, a_ref, ah_ref, g_ref, gh_ref, (i * tm) % seq == 0, halo)
        _conv_fill_shifted(dyext, dysh)
        _conv_fill_shifted(xext, xsh)
        for r0, c0 in _conv_subtiles(tm, cw):
            cs = slice(c0, c0 + sc)
            rows = slice(r0, r0 + sr)
            dyv = dy_ref[rows, cs]
            acc = jnp.zeros((sr, sc), F32)
            for j in range(k):
                acc = acc + w_ref[j:j + 1, cs] * _conv_rows(dyext, (dysh,), r0 + p - j, sr, cs)
                wacc[j, :, cs] += _rowsum8(dyv * _conv_rows(xext, (xsh,), r0 + halo - p + j, sr, cs))
            bacc[:, cs] += _rowsum8(dyv)
            s = _sigmoid(g_ref[rows, cs])
            dag_ref[rows, cs] = (acc * s).astype(dag_ref.dtype)
            dag_ref[rows, cw + c0:cw + c0 + sc] = (acc * a_ref[rows, cs] * s * (1.0 - s)).astype(dag_ref.dtype)

        @pl.when(i == nt - 1)
        def _():
            dw_ref[...] = jnp.zeros_like(dw_ref)
            for j in range(k):
                dw_ref[j:j + 1, :] = jnp.sum(wacc[j], axis=0, keepdims=True)
            db_ref[...] = jnp.sum(bacc[...], axis=0, keepdims=True)

    return pl.pallas_call(
        body, name=name,
        out_shape=(jax.ShapeDtypeStruct(dproj.shape, dproj.dtype), jax.ShapeDtypeStruct((kp, cw), F32),
                   jax.ShapeDtypeStruct((1, cw), F32)),
        grid=(nt,),
        in_specs=[pl.BlockSpec((tm, cw), lambda i: (i, 0)),
                  pl.BlockSpec((halo, cw), lambda i: (jnp.minimum((i + 1) * (tm // halo), last_halo), 0)),
                  *_conf_specs(tm, cw, halo, lambda i: i),
                  pl.BlockSpec((kp, cw), lambda i: (0, 0)), ANY],
        out_specs=(pl.BlockSpec((tm, 2 * cw), lambda i: (i, OFF_CONF // (2 * cw))),
                   pl.BlockSpec((kp, cw), lambda i: (0, 0)), pl.BlockSpec((1, cw), lambda i: (0, 0))),
        input_output_aliases={7: 0},
        scratch_shapes=[pltpu.VMEM((tm + halo, cw), F32), pltpu.VMEM((halo + tm, cw), F32),
                        pltpu.VMEM((kp, SUBLANES, cw), F32), pltpu.VMEM((SUBLANES, cw), F32)]
        + 2 * _conv_shift_scratch(k, halo + tm, cw),
        compiler_params=_params(("arbitrary",)),
    )(dc1, dc1, proj, proj, proj, proj, w, dproj)


def _half_mask(half):
    lane = _iota((1, LANES), 1)
    return ((lane >= half * ATTN_HEAD_DIM) & (lane < (half + 1) * ATTN_HEAD_DIM)).astype(F32)


def _stack_heads(xp, g):
    m = _half_mask(g)
    swapped = pltpu.roll(xp, ATTN_HEAD_DIM, axis=1)
    return jnp.concatenate([xp * m, swapped * m] if g == 0 else [swapped * m, xp * m], axis=0)


def _unstack_heads(both, g):
    w = both.shape[0] // 2
    top, bot = both[:w], both[w:]
    lo, hi = _half_mask(0), _half_mask(1)
    if g == 0:
        return top * lo + pltpu.roll(bot, ATTN_HEAD_DIM, axis=1) * hi
    return pltpu.roll(top, ATTN_HEAD_DIM, axis=1) * lo + bot * hi


def _band_mask(first_block):
    w = WINDOW
    qi = _iota((w, 2 * w), 0)
    kj = _iota((w, 2 * w), 1) - w
    rel = qi - kj
    return (rel >= 0) & (rel < w) & (jnp.logical_not(first_block) | (kj >= 0))


def _lane_pick(x, h):
    return jnp.sum(jnp.where(_iota(x.shape, 1) == h, x, 0.0), axis=1, keepdims=True)


def _attn_specs(nb, rev):
    w = WINDOW

    def blk(i):
        return nb - 1 - i if rev else i

    def row(b, i):
        return b * nb + blk(i)

    def prow(b, i):
        return b * nb + jnp.maximum(blk(i) - 1, 0)

    q = pl.BlockSpec((w, 512), lambda b, i: (row(b, i), OFF_Q // 512))
    kc = pl.BlockSpec((w, 128), lambda b, i: (row(b, i), OFF_K // 128))
    kp = pl.BlockSpec((w, 128), lambda b, i: (prow(b, i), OFF_K // 128))
    vc = pl.BlockSpec((w, 128), lambda b, i: (row(b, i), OFF_V // 128))
    vp = pl.BlockSpec((w, 128), lambda b, i: (prow(b, i), OFF_V // 128))
    z = pl.BlockSpec((w, 512), lambda b, i: (row(b, i), OFF_ZA // 512))
    return q, kc, kp, vc, vp, z, row


def _attn_fwd(proj, sinks, ycat, nbatch, name):
    t = proj.shape[0]
    w = WINDOW
    nb = t // nbatch // w
    scale = ATTN_HEAD_DIM ** -0.5
    q_s, kc_s, kp_s, vc_s, vp_s, z_s, row = _attn_specs(nb, False)

    def body(q_ref, kc_ref, kp_ref, vc_ref, vp_ref, z_ref, sk_ref, _, y_ref, o_ref, lse_ref):
        first = pl.program_id(1) == 0
        mask = _band_mask(first)
        kk = jnp.concatenate([kp_ref[...], kc_ref[...]], axis=0).astype(MXU_DTYPE)
        vv = jnp.concatenate([vp_ref[...], vc_ref[...]], axis=0).astype(MXU_DTYPE)
        sk = sk_ref[...]
        lane = _iota((w, LANES), 1)
        mask2 = jnp.concatenate([mask, mask], axis=0)
        scores = [_dot(_stack_heads(q_ref[:, j * LANES:(j + 1) * LANES], j // 2), kk, NT) for j in range(4)]
        lse_all = jnp.zeros((w, LANES), F32)
        for j in range(4):
            s = jnp.where(mask2, scores[j] * scale, -1e30)
            skc = jnp.concatenate([jnp.broadcast_to(_lane_pick(sk, 2 * j), (w, 1)),
                                   jnp.broadcast_to(_lane_pick(sk, 2 * j + 1), (w, 1))], axis=0)
            m = jnp.maximum(jnp.max(s, axis=1, keepdims=True), skc)
            den = jnp.sum(jnp.exp(s - m), axis=1, keepdims=True) + jnp.exp(skc - m)
            lse = m + jnp.log(den)
            lse_all = jnp.where(lane == 2 * j, lse[:w], lse_all)
            lse_all = jnp.where(lane == 2 * j + 1, lse[w:], lse_all)
            op = _unstack_heads(_dot(jnp.exp(s - lse), vv), j // 2)
            cols = slice(j * LANES, (j + 1) * LANES)
            o_ref[:, cols] = op
            y_ref[:, cols] = (op * _silu(z_ref[:, cols])).astype(y_ref.dtype)
        lse_ref[...] = lse_all

    return pl.pallas_call(
        body, name=name,
        out_shape=(jax.ShapeDtypeStruct(ycat.shape, ycat.dtype), jax.ShapeDtypeStruct((t, 512), F32),
                   jax.ShapeDtypeStruct((t, LANES), F32)),
        grid=(nbatch, nb),
        in_specs=[q_s, kc_s, kp_s, vc_s, vp_s, z_s, pl.BlockSpec((1, LANES), lambda b, i: (0, 0)), ANY],
        out_specs=(pl.BlockSpec((w, 512), lambda b, i: (row(b, i), YCAT_ATTN // 512)),
                   pl.BlockSpec((w, 512), lambda b, i: (row(b, i), 0)),
                   pl.BlockSpec((w, LANES), lambda b, i: (row(b, i), 0))),
        input_output_aliases={7: 0},
        compiler_params=_params(("parallel", "parallel")),
    )(proj, proj, proj, proj, proj, proj, sinks, ycat)


def _attn_bwd(dycat, proj, o, lse, sinks, ddt, dproj, nbatch, name):
    t = proj.shape[0]
    w = WINDOW
    nb = t // nbatch // w
    scale = ATTN_HEAD_DIM ** -0.5
    q_s, kc_s, kp_s, vc_s, vp_s, z_s, row = _attn_specs(nb, True)

    def body(dy_ref, q_ref, kc_ref, kp_ref, vc_ref, vp_ref, z_ref, o_ref, lse_ref, sk_ref, ddt_ref, _,
             grp_ref, dsk_ref, kcarry, vcarry, sacc):
        b, i = pl.program_id(0), pl.program_id(1)

        @pl.when((b == 0) & (i == 0))
        def _():
            sacc[...] = jnp.zeros_like(sacc)

        @pl.when(i == 0)
        def _():
            kcarry[...] = jnp.zeros_like(kcarry)
            vcarry[...] = jnp.zeros_like(vcarry)

        first = i == nb - 1
        mask = _band_mask(first)
        kk = jnp.concatenate([kp_ref[...], kc_ref[...]], axis=0).astype(MXU_DTYPE)
        vv = jnp.concatenate([vp_ref[...], vc_ref[...]], axis=0).astype(MXU_DTYPE)
        sk = sk_ref[...]
        lse_all = lse_ref[...]
        lane1 = _iota((1, LANES), 1)
        mask2 = jnp.concatenate([mask, mask], axis=0)
        qs, dos, deltas, lses, scores, dps = [], [], [], [], [], []
        for j in range(4):
            cols = slice(j * LANES, (j + 1) * LANES)
            qp, zp, ov, dy = q_ref[:, cols], z_ref[:, cols], o_ref[:, cols], dy_ref[:, cols]
            grp_ref[:, OFF_ZA + j * LANES:OFF_ZA + (j + 1) * LANES] = (dy * ov * _dsilu(zp)).astype(grp_ref.dtype)
            do = dy * _silu(zp)
            q2 = _stack_heads(qp, j // 2).astype(MXU_DTYPE)
            do2 = _stack_heads(do, j // 2)
            qs.append(q2)
            dos.append(do2.astype(MXU_DTYPE))
            deltas.append(jnp.sum(do2 * _stack_heads(ov, j // 2), axis=1, keepdims=True))
            lses.append(jnp.concatenate([_lane_pick(lse_all, 2 * j), _lane_pick(lse_all, 2 * j + 1)], axis=0))
            scores.append(_dot(q2, kk, NT))
            dps.append(_dot(do2, vv, NT))
        prs, dss = [], []
        dsk = jnp.zeros((1, LANES), F32)
        for j in range(4):
            pr = jnp.exp(jnp.where(mask2, scores[j] * scale, -1e30) - lses[j])
            prs.append(pr.astype(MXU_DTYPE))
            dss.append((pr * (dps[j] - deltas[j])).astype(MXU_DTYPE))
            skc = jnp.concatenate([jnp.broadcast_to(_lane_pick(sk, 2 * j), (w, 1)),
                                   jnp.broadcast_to(_lane_pick(sk, 2 * j + 1), (w, 1))], axis=0)
            sink_term = jnp.exp(skc - lses[j]) * deltas[j]
            dsk = dsk - jnp.where(lane1 == 2 * j, jnp.sum(sink_term[:w]), 0.0)
            dsk = dsk - jnp.where(lane1 == 2 * j + 1, jnp.sum(sink_term[w:]), 0.0)
        dkk = jnp.zeros((2 * w, LANES), F32)
        dvv = jnp.zeros((2 * w, LANES), F32)
        for j in range(4):
            dq = _unstack_heads(_dot(dss[j], kk) * scale, j // 2)
            grp_ref[:, OFF_Q + j * LANES:OFF_Q + (j + 1) * LANES] = dq.astype(grp_ref.dtype)
            dkk = dkk + _dot(dss[j], qs[j], TN) * scale
            dvv = dvv + _dot(prs[j], dos[j], TN)
        grp_ref[:, OFF_K:OFF_K + LANES] = (dkk[w:, :] + kcarry[...]).astype(grp_ref.dtype)
        grp_ref[:, OFF_V:OFF_V + LANES] = (dvv[w:, :] + vcarry[...]).astype(grp_ref.dtype)
        grp_ref[:, OFF_DT:OFF_DT + LANES] = ddt_ref[...].astype(grp_ref.dtype)
        grp_ref[:, OFF_DT + LANES:] = jnp.zeros((w, ATTN_GROUP - OFF_DT - LANES), grp_ref.dtype)
        kcarry[...] = dkk[:w, :]
        vcarry[...] = dvv[:w, :]
        sacc[...] += dsk

        @pl.when((b == nbatch - 1) & (i == nb - 1))
        def _():
            dsk_ref[...] = sacc[...]

    return pl.pallas_call(
        body, name=name,
        out_shape=(jax.ShapeDtypeStruct(dproj.shape, dproj.dtype), jax.ShapeDtypeStruct((1, LANES), F32)),
        grid=(nbatch, nb),
        in_specs=[pl.BlockSpec((w, 512), lambda b, i: (row(b, i), YCAT_ATTN // 512)),
                  q_s, kc_s, kp_s, vc_s, vp_s, z_s,
                  pl.BlockSpec((w, 512), lambda b, i: (row(b, i), 0)),
                  pl.BlockSpec((w, LANES), lambda b, i: (row(b, i), 0)),
                  pl.BlockSpec((1, LANES), lambda b, i: (0, 0)),
                  pl.BlockSpec((w, LANES), lambda b, i: (row(b, i), 0)), ANY],
        out_specs=(pl.BlockSpec((w, ATTN_GROUP), lambda b, i: (row(b, i), 0)),
                   pl.BlockSpec((1, LANES), lambda b, i: (0, 0))),
        input_output_aliases={11: 0},
        scratch_shapes=[pltpu.VMEM((w, LANES), F32), pltpu.VMEM((w, LANES), F32),
                        pltpu.VMEM((1, LANES), F32)],
        compiler_params=_params(("arbitrary", "arbitrary")),
    )(dycat, proj, proj, proj, proj, proj, proj, o, lse, sinks, ddt, dproj)


SSD_WIDTH = SSD_HEADS * SSD_HEAD_DIM
GROUP_ROWS = SSD_WIDTH // 2


def _expand_mat():
    r, c = _iota((LANES, SSD_WIDTH), 0), _iota((LANES, SSD_WIDTH), 1)
    return (r == lax.shift_right_logical(c, 6)).astype(BF16)


def _expand_mat_t():
    r, c = _iota((SSD_WIDTH, LANES), 0), _iota((SSD_WIDTH, LANES), 1)
    return (c == lax.shift_right_logical(r, 6)).astype(BF16)


def _ssd_common(u_ref, dt_ref, dtb_ref, a_ref):
    q = CHUNK
    act = _silu(u_ref[...])
    xs = act[:, :SSD_WIDTH]
    bm = act[:, SSD_WIDTH:SSD_WIDTH + 256]
    cm = act[:, SSD_WIDTH + 256:]
    dtp = _softplus(dt_ref[...] + dtb_ref[...])
    a = dtp * a_ref[...]
    tril = (_iota((q, q), 0) >= _iota((q, q), 1)).astype(BF16)
    acs = _xdot_r(tril, a)
    acs_t = acs.T
    e = _expand_mat()
    dt_x = _xdot(dtp, e)
    ea = jnp.exp(_xdot(acs, e))
    a_end = jnp.sum(jnp.where(_iota(acs.shape, 0) == q - 1, acs, 0.0), axis=0, keepdims=True)
    dec = jnp.exp(_xdot(a_end - acs, e))
    a_end_col = jnp.broadcast_to(_lane_pick(acs_t, q - 1), (LANES, LANES))
    s_scale = jnp.exp(_xdot_r(_expand_mat_t(), a_end_col))
    return act, xs, bm, cm, dtp, acs, acs_t, dt_x, ea, dec, s_scale, tril


def _decay_mat(acs, acs_t, h):
    q = CHUNK
    col = _lane_pick(acs, h)
    rowv = jnp.sum(jnp.where(_iota(acs_t.shape, 0) == h, acs_t, 0.0), axis=0, keepdims=True)
    causal = _iota((q, q), 0) >= _iota((q, q), 1)
    return jnp.exp(jnp.where(causal, col - rowv, -1e30))


GN_WIDTH = 512


def _ssd_fwd(u, proj, dtb, a_neg, d_x, norm_w, ycat, nbatch, name):
    t = u.shape[0]
    q = CHUNK
    nc = t // nbatch // q

    def body(u_ref, dt_ref, z_ref, dtb_ref, a_ref, dx_ref, nw_ref, _, y_ref, st_ref, yn_ref, state):
        c = pl.program_id(1)

        @pl.when(c == 0)
        def _():
            state[...] = jnp.zeros_like(state)

        st_ref[...] = state[...]
        act, xs, bm, cm, dtp, acs, acs_t, dt_x, ea, dec, s_scale, _ = _ssd_common(u_ref, dt_ref, dtb_ref, a_ref)
        xdt = xs * dt_x
        xdec = xdt * dec
        lo, hi = _half_mask(0), _half_mask(1)
        grp = []
        for g in range(2):
            bg = bm[:, g * LANES:(g + 1) * LANES]
            cg = cm[:, g * LANES:(g + 1) * LANES]
            rows = slice(g * GROUP_ROWS, (g + 1) * GROUP_ROWS)
            sg = state[rows, :]
            grp.append((_dot(cg, bg, NT), _dot(cg, sg, NT), rows,
                        s_scale[rows, :] * sg + _dot(xdec[:, rows], bg, TN)))
        for g in range(2):
            cb, yoff, rows, state_new = grp[g]
            for j in range(4):
                pj = g * 4 + j
                cols = slice(pj * LANES, (pj + 1) * LANES)
                xp = xdt[:, cols]
                m2 = jnp.concatenate([cb * _decay_mat(acs, acs_t, 2 * pj), cb * _decay_mat(acs, acs_t, 2 * pj + 1)],
                                     axis=1)
                yp = _dot(m2, jnp.concatenate([xp * lo, xp * hi], axis=0))
                yp = yp + yoff[:, j * LANES:(j + 1) * LANES] * ea[:, cols]
                y_ref[:, cols] = yp + dx_ref[:, cols] * xs[:, cols]
            state[rows, :] = state_new
        for g in range(SSD_WIDTH // GN_WIDTH):
            cols = slice(g * GN_WIDTH, (g + 1) * GN_WIDTH)
            gg = y_ref[:, cols] * _silu(z_ref[:, cols])
            rstd = lax.rsqrt(jnp.mean(gg * gg, axis=-1, keepdims=True) + EPS)
            yn_ref[:, cols] = (gg * rstd * nw_ref[:, cols]).astype(yn_ref.dtype)

    vec = pl.BlockSpec((1, LANES), lambda b, c: (0, 0))
    wide = pl.BlockSpec((q, SSD_WIDTH), lambda b, c: (b * nc + c, 0))
    wvec = pl.BlockSpec((1, SSD_WIDTH), lambda b, c: (0, 0))
    return pl.pallas_call(
        body, name=name,
        out_shape=(jax.ShapeDtypeStruct((t, SSD_WIDTH), F32),
                   jax.ShapeDtypeStruct((nbatch * nc * SSD_WIDTH, SSD_STATE), F32),
                   jax.ShapeDtypeStruct(ycat.shape, ycat.dtype)),
        grid=(nbatch, nc),
        in_specs=[pl.BlockSpec((q, SSD_CONV_DIM), lambda b, c: (b * nc + c, 0)),
                  pl.BlockSpec((q, LANES), lambda b, c: (b * nc + c, OFF_DT // LANES)),
                  pl.BlockSpec((q, SSD_WIDTH), lambda b, c: (b * nc + c, OFF_ZS // SSD_WIDTH)),
                  vec, vec, wvec, wvec, ANY],
        out_specs=(wide, pl.BlockSpec((SSD_WIDTH, SSD_STATE), lambda b, c: (b * nc + c, 0)), wide),
        input_output_aliases={7: 2},
        scratch_shapes=[pltpu.VMEM((SSD_WIDTH, SSD_STATE), F32)],
        compiler_params=_params(("parallel", "arbitrary")),
    )(u, proj, proj, dtb, a_neg, d_x, norm_w, ycat)


def _ssd_bwd(dycat, u, proj, y, states, dtb, a_neg, d_x, norm_w, dproj, nbatch, name):
    t = u.shape[0]
    q = CHUNK
    nc = t // nbatch // q

    def body(do_ref, u_ref, dt_ref, z_ref, y_ref, st_ref, dtb_ref, a_ref, dx_ref, nw_ref, _,
             du_ref, dz_ref, ddt_ref, dal_ref, dd_ref, dtbg_ref, dnw_ref, dstate, acc_a, acc_d, acc_b, acc_w):
        b, c = pl.program_id(0), pl.program_id(1)

        @pl.when((b == 0) & (c == 0))
        def _():
            acc_a[...] = jnp.zeros_like(acc_a)
            acc_d[...] = jnp.zeros_like(acc_d)
            acc_b[...] = jnp.zeros_like(acc_b)
            acc_w[...] = jnp.zeros_like(acc_w)

        @pl.when(c == 0)
        def _():
            dstate[...] = jnp.zeros_like(dstate)

        dy_parts = []
        for g in range(SSD_WIDTH // GN_WIDTH):
            cols = slice(g * GN_WIDTH, (g + 1) * GN_WIDTH)
            yv, zv, dov = y_ref[:, cols], z_ref[:, cols], do_ref[:, cols]
            sz = _silu(zv)
            gg = yv * sz
            rstd = lax.rsqrt(jnp.mean(gg * gg, axis=-1, keepdims=True) + EPS)
            gh = gg * rstd
            acc_w[:, cols] += _rowsum8(dov * gh)
            dgn = dov * nw_ref[:, cols]
            dg = rstd * (dgn - gh * jnp.mean(dgn * gh, axis=-1, keepdims=True))
            dy_parts.append(dg * sz)
            dz_ref[:, cols] = (dg * yv * _dsilu(zv)).astype(dz_ref.dtype)

        act, xs, bm, cm, dtp, acs, acs_t, dt_x, ea, dec, s_scale, tril = _ssd_common(
            u_ref, dt_ref, dtb_ref, a_ref)
        xdt = xs * dt_x
        xdec = xdt * dec
        dyv = jnp.concatenate(dy_parts, axis=1)
        dye = dyv * ea
        lo, hi = _half_mask(0), _half_mask(1)
        et = _expand_mat_t()
        grp = []
        for g in range(2):
            rows = slice(g * GROUP_ROWS, (g + 1) * GROUP_ROWS)
            bg = bm[:, g * LANES:(g + 1) * LANES]
            cg = cm[:, g * LANES:(g + 1) * LANES]
            sg = st_ref[rows, :]
            dsg = dstate[rows, :]
            grp.append(dict(
                rows=rows, bg=bg, cg=cg, dsg=dsg,
                cb=_dot(cg, bg, NT), yoff=_dot(cg, sg, NT), dxst=_dot(bg, dsg, NT) * dec[:, rows],
                dc_off=_dot(dye[:, rows], sg), db_off=_dot(xdec[:, rows], dsg),
                s_next=s_scale[rows, :] * sg + _dot(xdec[:, rows], bg, TN),
                dstate_new=_dot(dye[:, rows], cg, TN) + s_scale[rows, :] * dsg))
        dy2s, g2s, l2s = [], [], []
        for pj in range(SSD_HEADS // 2):
            cols = slice(pj * LANES, (pj + 1) * LANES)
            dyp = dyv[:, cols]
            dy2 = jnp.concatenate([dyp * lo, dyp * hi], axis=0).astype(MXU_DTYPE)
            dy2s.append(dy2)
            g2s.append(_dot(dy2, xdt[:, cols], NT))
            l2s.append(jnp.concatenate([_decay_mat(acs, acs_t, 2 * pj), _decay_mat(acs, acs_t, 2 * pj + 1)], axis=0))
        dal_diag = jnp.zeros((q, LANES), F32)
        lane2 = _iota((2 * q, LANES), 1)
        row2 = _iota((2 * q, LANES), 0)
        dxdt_parts, db_parts, dc_parts = [], [], []
        end_sum = jnp.zeros((LANES, LANES), F32)
        for g in range(2):
            gd = grp[g]
            cb2 = jnp.concatenate([gd["cb"], gd["cb"]], axis=0)
            dcb = jnp.zeros((q, q), F32)
            parts = []
            for j in range(4):
                pj = g * 4 + j
                gl = g2s[pj] * l2s[pj]
                dcb = dcb + gl[:q] + gl[q:]
                m2 = cb2 * l2s[pj]
                parts.append(_dot(m2, dy2s[pj], TN))
                w2 = (gl * cb2).astype(MXU_DTYPE)
                sel2 = (lane2 == 2 * pj + (row2 >= q).astype(jnp.int32)).astype(MXU_DTYPE)
                dal_diag = dal_diag + _dot(jnp.concatenate([w2[:q], w2[q:]], axis=1), sel2) - _dot(w2, sel2, TN)
            dxdt_parts.append(jnp.concatenate(parts, axis=1) + gd["dxst"])
            dc_parts.append(_dot(dcb, gd["bg"]) + gd["dc_off"])
            db_parts.append(_dot(dcb, gd["cg"], TN) + gd["db_off"])
            end_sum = end_sum + _xdot(gd["dsg"] * gd["s_next"], et[gd["rows"], :], TN, passes=2)
            dstate[gd["rows"], :] = gd["dstate_new"]
        dxst_parts = [gd["dxst"] for gd in grp]
        yoff_parts = [gd["yoff"] for gd in grp]
        dxdt = jnp.concatenate(dxdt_parts, axis=1)
        dxv = dx_ref[...]
        yoff = jnp.concatenate(yoff_parts, axis=1) * ea
        dalpha = dal_diag + _xdot(dyv * yoff - xdt * jnp.concatenate(dxst_parts, axis=1), et)
        end_row = jnp.sum(end_sum, axis=0, keepdims=True)
        dalpha = dalpha + jnp.where(_iota((q, LANES), 0) == q - 1, end_row, 0.0)
        da = _xdot_r(tril, dalpha, TN)
        ddtp = da * a_ref[...] + _xdot(dxdt * xs, et)
        acc_a[...] += _rowsum8(da * dtp)
        acc_d[...] += _rowsum8(_xdot(dyv * xs, et))
        ddt_raw = ddtp * _sigmoid(dt_ref[...] + dtb_ref[...])
        acc_b[...] += _rowsum8(ddt_raw)
        ddt_ref[...] = ddt_raw
        dxs = dxdt * dt_x + dxv * dyv
        dact = jnp.concatenate([dxs] + db_parts + dc_parts, axis=1)
        du_ref[...] = dact * _dsilu(u_ref[...])

        @pl.when((b == nbatch - 1) & (c == nc - 1))
        def _():
            dal_ref[...] = jnp.sum(acc_a[...], axis=0, keepdims=True) * a_ref[...]
            dd_ref[...] = jnp.sum(acc_d[...], axis=0, keepdims=True)
            dtbg_ref[...] = jnp.sum(acc_b[...], axis=0, keepdims=True)
            dnw_ref[...] = jnp.sum(acc_w[...], axis=0, keepdims=True)

    def rowblk(b, c):
        return b * nc + (nc - 1 - c)

    vec = pl.BlockSpec((1, LANES), lambda b, c: (0, 0))
    wvec = pl.BlockSpec((1, SSD_WIDTH), lambda b, c: (0, 0))
    wide = pl.BlockSpec((q, SSD_WIDTH), lambda b, c: (rowblk(b, c), 0))
    zblk = pl.BlockSpec((q, SSD_WIDTH), lambda b, c: (rowblk(b, c), OFF_ZS // SSD_WIDTH))
    return pl.pallas_call(
        body, name=name,
        out_shape=(jax.ShapeDtypeStruct((t, SSD_CONV_DIM), F32), jax.ShapeDtypeStruct(dproj.shape, dproj.dtype),
                   jax.ShapeDtypeStruct((t, LANES), F32),
                   jax.ShapeDtypeStruct((1, LANES), F32), jax.ShapeDtypeStruct((1, LANES), F32),
                   jax.ShapeDtypeStruct((1, LANES), F32), jax.ShapeDtypeStruct((1, SSD_WIDTH), F32)),
        grid=(nbatch, nc),
        in_specs=[wide,
                  pl.BlockSpec((q, SSD_CONV_DIM), lambda b, c: (rowblk(b, c), 0)),
                  pl.BlockSpec((q, LANES), lambda b, c: (rowblk(b, c), OFF_DT // LANES)),
                  zblk, wide,
                  pl.BlockSpec((SSD_WIDTH, SSD_STATE), lambda b, c: (rowblk(b, c), 0)),
                  vec, vec, wvec, wvec, ANY],
        out_specs=(pl.BlockSpec((q, SSD_CONV_DIM), lambda b, c: (rowblk(b, c), 0)),
                   zblk,
                   pl.BlockSpec((q, LANES), lambda b, c: (rowblk(b, c), 0)),
                   vec, vec, vec, wvec),
        input_output_aliases={10: 1},
        scratch_shapes=[pltpu.VMEM((SSD_WIDTH, SSD_STATE), F32), pltpu.VMEM((SUBLANES, LANES), F32),
                        pltpu.VMEM((SUBLANES, LANES), F32), pltpu.VMEM((SUBLANES, LANES), F32),
                        pltpu.VMEM((SUBLANES, SSD_WIDTH), F32)],
        compiler_params=_params(("arbitrary", "arbitrary")),
    )(dycat, u, proj, proj, y, states, dtb, a_neg, d_x, norm_w, dproj)


def _pad_rows(w, rows):
    return jnp.concatenate([w, jnp.zeros((rows - w.shape[0], w.shape[1]), w.dtype)], axis=0)


def _pad_lanes(v):
    return jnp.concatenate([v, jnp.zeros((LANES - v.shape[0],), v.dtype)]).reshape(1, LANES)


def _padded_from_chips(pieces):
    cols = pieces[0].shape[-1]
    lead = pieces[0].shape[:-1]
    parts, pos = [], 0
    for lo, hi, start in sorted(SECTIONS, key=lambda s: s[2]):
        if start > pos:
            parts.append(jnp.zeros(lead + (start - pos,), pieces[0].dtype))
        pos = start + hi - lo
        while lo < hi:
            p = lo // cols
            end = min(hi, (p + 1) * cols)
            parts.append(pieces[p][..., lo - p * cols:end - p * cols])
            lo = end
    if pos < NP:
        parts.append(jnp.zeros(lead + (NP - pos,), pieces[0].dtype))
    return jnp.concatenate(parts, axis=-1)


def _chip_part_from_padded(wp, p, cols):
    lo, hi = p * cols, (p + 1) * cols
    parts = []
    for rs, re, start in SECTIONS:
        a, b = max(lo, rs), min(hi, re)
        if a < b:
            parts.append(wp[..., start + a - rs:start + b - rs])
    return jnp.concatenate(parts, axis=-1)


def _layer_params(li, w_in_p, w_out, conv_w, dw_w, small):
    return dict(
        w_in_p=w_in_p, w_out=w_out,
        conv_w=_pad_rows(conv_w, SUBLANES), dw_w=_pad_rows(dw_w, 32),
        norm_w=small["norm_w"][li].reshape(1, -1),
        conv_b=small["ssd_conv_b"][li].reshape(1, -1),
        dtb=_pad_lanes(small["ssd_dt_bias"][li]),
        a_neg=_pad_lanes(-jnp.exp(small["ssd_a_log"][li])),
        d_x=jnp.repeat(small["ssd_d"][li], SSD_HEAD_DIM).reshape(1, -1),
        ssd_norm_w=small["ssd_norm_w"][li].reshape(1, -1),
        sinks=_pad_lanes(small["attn_sinks"][li]),
        dw_b=small["conf_dw_b"][li].reshape(1, -1),
        ln_w=small["conf_ln_w"][li].reshape(1, -1),
        ln_b=small["conf_ln_b"][li].reshape(1, -1),
    )


def _layer_fwd(x, p, nbatch, seq, tag, after=None):
    proj, h_t = _proj_fwd(x, p["norm_w"], p["w_in_p"], name=f"proj_fwd_{tag}", after=after)
    u = _conv_fwd(proj, OFF_XBC, SSD_CONV_DIM, p["conv_w"], p["conv_b"], SSD_CONV, seq, name=f"ssd_conv_fwd_{tag}")
    ycat = lax.empty((x.shape[0], MIX_WIDTH), MXU_DTYPE)
    y, states, ycat = _ssd_fwd(u, proj, p["dtb"], p["a_neg"], p["d_x"], p["ssd_norm_w"], ycat, nbatch,
                               name=f"ssd_fwd_{tag}")
    ycat, o, lse = _attn_fwd(proj, p["sinks"], ycat, nbatch, name=f"attn_fwd_{tag}")
    c1, ycat = _conf_fwd(proj, p["dw_w"], p["dw_b"], p["ln_w"], p["ln_b"], ycat, seq, name=f"conf_fwd_{tag}")
    x_new = _matmul(ycat, p["w_out"], "nn", F32, 1024, 512, 2048, name=f"out_fwd_{tag}", residual=x)
    return x_new, dict(x=x, h_t=h_t, proj=proj, u=u, y=y, states=states, o=o, lse=lse, c1=c1, ycat=ycat)


def _layer_bwd(dx_out, p, s, nbatch, seq, tag, hooks=None):
    hooks = hooks or {}
    proj = s["proj"]
    dycat = _matmul(dx_out, p["w_out"], "nt", F32, 1024, 1024, 1024, name=f"out_bwd_dy_{tag}",
                    after=hooks.get("start_token"))
    dw_out = _matmul(s["ycat"], dx_out, "tn", F32, 1024, 1024, 1024, name=f"out_bwd_dw_{tag}")
    token = hooks["after_dycat"](dycat) if "after_dycat" in hooks else None
    dtb = p["dtb"] if token is None else p["dtb"] + token[0, 0]
    dproj = lax.empty(proj.shape, MXU_DTYPE)
    du, dproj, ddt, da_log, dd, ddtb, dssd_norm_w = _ssd_bwd(
        dycat, s["u"], proj, s["y"], s["states"], dtb, p["a_neg"], p["d_x"], p["ssd_norm_w"], dproj,
        nbatch, name=f"ssd_bwd_{tag}")
    dproj, dconv_w, dconv_b = _conv_bwd(du, proj, OFF_XBC, SSD_CONV_DIM, p["conv_w"], SSD_CONV, seq,
                                        name=f"ssd_conv_bwd_{tag}", into=dproj)
    dproj, dsinks = _attn_bwd(dycat, proj, s["o"], s["lse"], p["sinks"], ddt, dproj, nbatch,
                              name=f"attn_bwd_{tag}")
    if "after_attn" in hooks:
        hooks["after_attn"](dproj)
    dc1, dproj, dln_w, dln_b = _conf_post_bwd(dycat, s["c1"], proj, p["ln_w"], p["ln_b"], dproj,
                                              name=f"conf_post_bwd_{tag}")
    dproj, ddw_w, ddw_b = _conf_conv_bwd(dc1, proj, p["dw_w"], dproj, seq, name=f"conf_conv_bwd_{tag}")
    dw_in_p = _matmul(s["h_t"], dproj, "nn", F32, 1024, 512, 4096, name=f"proj_bwd_dw_{tag}")
    dx_in, dnorm_w = _proj_bwd_dx(dproj, p["w_in_p"], s["x"], p["norm_w"], dx_out, name=f"proj_bwd_dx_{tag}")
    grads = dict(
        norm_w=dnorm_w[0], w_in_p=dw_in_p, ssd_conv_w=dconv_w[:SSD_CONV], ssd_conv_b=dconv_b[0],
        ssd_dt_bias=ddtb[0, :SSD_HEADS], ssd_a_log=da_log[0, :SSD_HEADS], ssd_d=dd[0, :SSD_HEADS],
        ssd_norm_w=dssd_norm_w[0], attn_sinks=dsinks[0, :ATTN_Q_HEADS], conf_dw_w=ddw_w[:CONF_KERNEL],
        conf_dw_b=ddw_b[0], conf_ln_w=dln_w[0], conf_ln_b=dln_b[0], w_out=dw_out)
    return dx_in, grads


def _local_step(x, target, param_fns, final_norm_w, first_after=None, on_grads=None):
    nbatch, seq, d = x.shape
    xt = x.reshape(nbatch * seq, d)
    saved, layer_params = [], []
    for li, fn in enumerate(param_fns):
        p = fn(xt)
        layer_params.append(p)
        xt, s = _layer_fwd(xt, p, nbatch, seq, f"l{li}", after=first_after if li == 0 else None)
        saved.append(s)
    loss, dx, dfinal = _loss_head(xt, target.reshape(nbatch * seq, d), final_norm_w.reshape(1, d), name="loss_head")
    grads = [None] * len(layer_params)
    hooks = None
    for li in reversed(range(len(layer_params))):
        dx, grads[li] = _layer_bwd(dx, layer_params[li], saved[li], nbatch, seq, f"l{li}", hooks=hooks)
        hooks = on_grads(li, grads[li]) if on_grads is not None else None
    return loss[0, 0], dx.reshape(nbatch, seq, d), grads, dfinal[0]


MESH = pl.DeviceIdType.MESH
N_CHIPS = 4


def _mesh_pos():
    return lax.axis_index("x"), lax.axis_index("y"), lax.axis_index("c")


def _other_chips(x, y):
    return [(1 - x, y), (x, 1 - y), (1 - x, 1 - y)]


def _gather_weights(big, small, name):
    nbig, nsmall = len(big), len(small)
    n_ici = 3 * (nbig + nsmall)
    n_fwd = 3 * nbig

    def body(*refs):
        ins = refs[:nbig + nsmall]
        outs = refs[nbig + nsmall:2 * (nbig + nsmall)]
        send_sems, recv_sems = refs[2 * (nbig + nsmall):]
        x, y, c = _mesh_pos()
        me = 2 * x + y
        sibling = (x, y, 1 - c)
        chips = _other_chips(x, y)

        def ici(a, j, origin, dest):
            if a < nbig:
                src = ins[a].at[c] if origin is None else outs[a].at[origin, c]
                dst = outs[a].at[me if origin is None else origin, c]
            else:
                src = ins[a] if origin is None else outs[a].at[origin]
                dst = outs[a].at[me if origin is None else origin]
            k = a * 3 + j
            return pltpu.make_async_remote_copy(src_ref=src, dst_ref=dst, send_sem=send_sems.at[k],
                                                recv_sem=recv_sems.at[k], device_id=dest, device_id_type=MESH)

        def fwd(a, j, origin, half):
            k = n_ici + a * 3 + j
            ref = outs[a].at[origin, half]
            return pltpu.make_async_remote_copy(src_ref=ref, dst_ref=ref, send_sem=send_sems.at[k],
                                                recv_sem=recv_sems.at[k], device_id=sibling, device_id_type=MESH)

        sends = []
        for j, (px, py) in enumerate(chips):
            for a in range(nbig + nsmall):
                cp = ici(a, j, None, (px, py, c))
                cp.start()
                sends.append(cp)
        for j, (px, py) in enumerate(chips):
            origin = 2 * px + py
            for a in range(nbig):
                ici(a, j, origin, (px, py, c)).wait_recv()
                cp = fwd(a, j, origin, c)
                cp.start()
                sends.append(cp)
        for j, (px, py) in enumerate(chips):
            origin = 2 * px + py
            for a in range(nbig, nbig + nsmall):
                ici(a, j, origin, (px, py, c)).wait_recv()
            for a in range(nbig):
                fwd(a, j, origin, 1 - c).wait_recv()
        for cp in sends:
            cp.wait_send()

    out_shape = tuple(jax.ShapeDtypeStruct((N_CHIPS,) + a.shape, a.dtype) for a in list(big) + list(small))
    return pl.pallas_call(
        body, name=name, out_shape=out_shape,
        in_specs=[ANY] * (nbig + nsmall), out_specs=tuple([ANY] * (nbig + nsmall)),
        scratch_shapes=[pltpu.SemaphoreType.DMA((n_ici + n_fwd,)), pltpu.SemaphoreType.DMA((n_ici + n_fwd,))],
    )(*big, *small)


def _pair_swap_halves(arrs, name):
    n = len(arrs)

    def body(*refs):
        ins, outs = refs[:n], refs[n:2 * n]
        send_sems, recv_sems = refs[2 * n:]
        x, y, c = _mesh_pos()
        cps = [pltpu.make_async_remote_copy(src_ref=ins[a].at[:, 1 - c], dst_ref=outs[a], send_sem=send_sems.at[a],
                                            recv_sem=recv_sems.at[a], device_id=(x, y, 1 - c), device_id_type=MESH)
               for a in range(n)]
        for cp in cps:
            cp.start()
        for cp in cps:
            cp.wait()

    return pl.pallas_call(
        body, name=name,
        out_shape=tuple(jax.ShapeDtypeStruct(a.shape[:1] + a.shape[2:], a.dtype) for a in arrs),
        in_specs=[ANY] * n, out_specs=tuple([ANY] * n),
        scratch_shapes=[pltpu.SemaphoreType.DMA((n,)), pltpu.SemaphoreType.DMA((n,))],
    )(*arrs)


HBM = pl.BlockSpec(memory_space=pltpu.HBM)
SEM = pl.BlockSpec(memory_space=pltpu.SEMAPHORE)
DATAFLOW = pltpu.SideEffectType.DATAFLOW_SIDE_EFFECTING


def _split_peers(pattern, x, y, c):
    if pattern == "swap":
        return [((x, y, 1 - c), 1 - c, None, None)]
    me = 2 * x + y
    return [((px, py, c), 2 * px + py if pattern == "scatter" else None, me, 2 * px + py)
            for px, py in _other_chips(x, y)]


def _split_land_shape(pattern, shape):
    return {"bcast": (N_CHIPS,) + shape, "scatter": shape, "swap": shape[:1] + shape[2:]}[pattern]


def _split_copies(pattern, srcs, lands, send_sems, recv_sems, waiting):
    x, y, c = _mesh_pos()
    peers = _split_peers(pattern, x, y, c)
    cps = []
    for j, (dev, src_slot, dst_slot, my_slot) in enumerate(peers):
        for a in range(len(srcs)):
            if src_slot is None:
                src = srcs[a]
            else:
                src = srcs[a].at[:, src_slot] if pattern == "swap" else srcs[a].at[src_slot]
            slot = my_slot if waiting else dst_slot
            dst = lands[a] if slot is None else lands[a].at[slot]
            k = a * len(peers) + j
            cps.append(pltpu.make_async_remote_copy(src_ref=src, dst_ref=dst, send_sem=send_sems[k],
                                                    recv_sem=recv_sems[k], device_id=dev, device_id_type=MESH))
    return cps


def _split_start(arrs, pattern, after, name):
    n = len(arrs)
    nsem = n * (1 if pattern == "swap" else N_CHIPS - 1)

    def body(*refs):
        srcs, lands = refs[:n], refs[n:2 * n]
        outs = refs[2 * n + 1:]
        for cp in _split_copies(pattern, srcs, lands, outs[:nsem], outs[nsem:2 * nsem], waiting=False):
            cp.start()
        outs[-1][...] = jnp.zeros_like(outs[-1])

    lands = [lax.empty(_split_land_shape(pattern, a.shape), a.dtype) for a in arrs]
    out_shape = ([pltpu.SemaphoreType.DMA(())] * (2 * nsem)
                 + [pltpu.HBM(a.shape, a.dtype) for a in arrs] + [pltpu.HBM(b.shape, b.dtype) for b in lands]
                 + [jax.ShapeDtypeStruct((SUBLANES, LANES), F32)])
    outs = pl.pallas_call(
        body, name=name, out_shape=tuple(out_shape),
        in_specs=[HBM] * (2 * n) + [ANY],
        out_specs=tuple([SEM] * (2 * nsem) + [HBM] * (2 * n) + [pl.BlockSpec(memory_space=pltpu.VMEM)]),
        input_output_aliases={a: 2 * nsem + a for a in range(2 * n)},
        compiler_params=pltpu.CompilerParams(has_side_effects=DATAFLOW),
    )(*[pltpu.with_memory_space_constraint(a, pltpu.HBM) for a in list(arrs) + lands], after)
    return outs[:-1], outs[-1]


def _split_wait(state, n, pattern, after, name):
    nsem = n * (1 if pattern == "swap" else N_CHIPS - 1)

    def body(*refs):
        srcs, lands = refs[:n], refs[n:2 * n]
        send_sems, recv_sems = refs[2 * n:2 * n + nsem], refs[2 * n + nsem:2 * n + 2 * nsem]
        for cp in _split_copies(pattern, srcs, lands, send_sems, recv_sems, waiting=True):
            cp.wait_send()
            cp.wait_recv()

    sems, thru = state[:2 * nsem], state[2 * nsem:]
    outs = pl.pallas_call(
        body, name=name, out_shape=tuple(pltpu.HBM(a.shape, a.dtype) for a in thru),
        in_specs=[HBM] * (2 * n) + [SEM] * (2 * nsem) + [ANY],
        out_specs=tuple([HBM] * (2 * n)),
        input_output_aliases={a: a for a in range(2 * n)},
        compiler_params=pltpu.CompilerParams(has_side_effects=DATAFLOW),
    )(*thru, *sems, after)
    return outs[n:]


def _pair_gather(arrs, layer, name):
    n = len(arrs)

    def body(*refs):
        outs = refs[n:2 * n]
        send_sems, recv_sems = refs[2 * n:]
        x, y, c = _mesh_pos()
        cps = [pltpu.make_async_remote_copy(src_ref=outs[a].at[layer, c], dst_ref=outs[a].at[layer, c],
                                            send_sem=send_sems.at[a], recv_sem=recv_sems.at[a],
                                            device_id=(x, y, 1 - c), device_id_type=MESH)
               for a in range(n)]
        for cp in cps:
            cp.start()
        for cp in cps:
            cp.wait()

    return pl.pallas_call(
        body, name=name, out_shape=tuple(jax.ShapeDtypeStruct(a.shape, a.dtype) for a in arrs),
        in_specs=[ANY] * n, out_specs=tuple([ANY] * n),
        input_output_aliases={a: a for a in range(n)},
        scratch_shapes=[pltpu.SemaphoreType.DMA((n,)), pltpu.SemaphoreType.DMA((n,))],
    )(*arrs)


N_DEV = 8


def _allreduce_small(pack, name):
    r = pack.shape[0]

    def body(p_ref, o_ref, land, send_sems, recv_sems):
        x, y, c = _mesh_pos()
        me = 4 * x + 2 * y + c
        cps = []
        for k in range(1, N_DEV):
            peer = (x ^ (k >> 2), y ^ ((k >> 1) & 1), c ^ (k & 1))
            cps.append(pltpu.make_async_remote_copy(src_ref=p_ref, dst_ref=land.at[me], send_sem=send_sems.at[k - 1],
                                                    recv_sem=recv_sems.at[k - 1], device_id=peer, device_id_type=MESH))
        for cp in cps:
            cp.start()
        land[me] = p_ref[...]
        for cp in cps:
            cp.wait()
        total = land[0]
        for d in range(1, N_DEV):
            total = total + land[d]
        o_ref[...] = total

    vm = pl.BlockSpec(memory_space=pltpu.VMEM)
    return pl.pallas_call(
        body, name=name, out_shape=jax.ShapeDtypeStruct(pack.shape, F32),
        in_specs=[vm], out_specs=vm,
        scratch_shapes=[pltpu.VMEM((N_DEV, r, LANES), F32), pltpu.SemaphoreType.DMA((N_DEV - 1,)),
                        pltpu.SemaphoreType.DMA((N_DEV - 1,))],
    )(pack)


BIG_ROWS = 128


def _cast_layer(w, layer, name):
    _, r, cdim = w.shape
    tr = BIG_ROWS

    def body(w_ref, o_ref):
        o_ref[...] = w_ref[...].astype(o_ref.dtype)

    return pl.pallas_call(
        body, name=name, out_shape=jax.ShapeDtypeStruct((r, cdim), MXU_DTYPE),
        grid=(r // tr,), in_specs=[pl.BlockSpec((None, tr, cdim), lambda i: (layer, i, 0))],
        out_specs=pl.BlockSpec((tr, cdim), lambda i: (i, 0)),
        compiler_params=_params(("parallel",)),
    )(w)


def _cast_cols_major(w_t, name):
    cdim, nl, r = w_t.shape
    tc = LANES

    def body(w_ref, *o_refs):
        for l in range(nl):
            o_refs[l][...] = w_ref[:, l, :].T.astype(o_refs[l].dtype)

    out = pl.BlockSpec((r, tc), lambda i: (0, i))
    return pl.pallas_call(
        body, name=name, out_shape=tuple(jax.ShapeDtypeStruct((r, cdim), MXU_DTYPE) for _ in range(nl)),
        grid=(pl.cdiv(cdim, tc),), in_specs=[pl.BlockSpec((tc, nl, r), lambda i: (i, 0, 0))],
        out_specs=tuple([out] * nl),
        compiler_params=_params(("parallel",)),
    )(w_t)


def _pair_sum(parts, sib, which, out_dtype, name):
    k, _, r, cdim = parts.shape
    tr = BIG_ROWS

    def body(sel_ref, p_ref, s_ref, o_ref):
        o_ref[...] = (p_ref[...] + s_ref[...]).astype(o_ref.dtype)

    grid_spec = pltpu.PrefetchScalarGridSpec(
        num_scalar_prefetch=1, grid=(k, r // tr),
        in_specs=[pl.BlockSpec((None, None, tr, cdim), lambda l, i, sel: (l, sel[0], i, 0)),
                  pl.BlockSpec((None, tr, cdim), lambda l, i, sel: (l, i, 0))],
        out_specs=pl.BlockSpec((None, tr, cdim), lambda l, i, sel: (l, i, 0)))
    return pl.pallas_call(
        body, name=name, out_shape=jax.ShapeDtypeStruct((k, r, cdim), out_dtype), grid_spec=grid_spec,
        compiler_params=_params(("parallel", "parallel")),
    )(which.reshape(1).astype(jnp.int32), parts, sib)


def _sum_lead(parts, into, layer, which, name):
    k, r, cdim = parts.shape
    tr = BIG_ROWS

    def body(sel_ref, p_ref, _, o_ref):
        total = p_ref[0].astype(F32)
        for a in range(1, k):
            total = total + p_ref[a].astype(F32)
        o_ref[...] = total

    grid_spec = pltpu.PrefetchScalarGridSpec(
        num_scalar_prefetch=1, grid=(r // tr,),
        in_specs=[pl.BlockSpec((k, tr, cdim), lambda i, sel: (0, i, 0)), ANY],
        out_specs=pl.BlockSpec((None, None, tr, cdim), lambda i, sel: (layer, sel[0], i, 0)))
    return pl.pallas_call(
        body, name=name, out_shape=jax.ShapeDtypeStruct(into.shape, F32), grid_spec=grid_spec,
        input_output_aliases={2: 0},
        compiler_params=_params(("parallel",)),
    )(which.reshape(1).astype(jnp.int32), parts, into)


def _adam_math(w, g, m, v):
    m2 = ADAM_B1 * m + (1.0 - ADAM_B1) * g
    v2 = ADAM_B2 * v + (1.0 - ADAM_B2) * (g * g)
    m_hat = m2 / (1.0 - ADAM_B1 ** ADAM_STEP)
    v_hat = v2 / (1.0 - ADAM_B2 ** ADAM_STEP)
    delta = -ADAM_LR * (m_hat / (jnp.sqrt(v_hat) + ADAM_EPS) + ADAM_WD * w)
    return delta, m2, v2


def _adam_big(w, g, m, v, name):
    nl, r, cdim = w.shape
    tr = BIG_ROWS

    def body(w_ref, g_ref, m_ref, v_ref, d_ref, mo_ref, vo_ref):
        delta, m2, v2 = _adam_math(w_ref[...], g_ref[...], m_ref[...], v_ref[...])
        d_ref[...] = delta
        mo_ref[...] = m2
        vo_ref[...] = v2

    blk = pl.BlockSpec((None, tr, cdim), lambda l, i: (l, i, 0))
    shp = jax.ShapeDtypeStruct(w.shape, F32)
    return pl.pallas_call(
        body, name=name, out_shape=(shp, shp, shp),
        grid=(nl, r // tr), in_specs=[blk] * 4, out_specs=(blk, blk, blk),
        compiler_params=_params(("parallel", "parallel")),
    )(w, g, m, v)


def _adam_cols_major(w, g, m, v, name):
    cdim, nl, r = w.shape
    tc = BIG_ROWS

    def body(w_ref, g_ref, m_ref, v_ref, d_ref, mo_ref, vo_ref):
        delta, m2, v2 = _adam_math(w_ref[...], g_ref[...], m_ref[...], v_ref[...])
        d_ref[...] = delta
        mo_ref[...] = m2
        vo_ref[...] = v2

    blk = pl.BlockSpec((tc, nl, r), lambda i: (i, 0, 0))
    shp = jax.ShapeDtypeStruct(w.shape, F32)
    return pl.pallas_call(
        body, name=name, out_shape=(shp, shp, shp),
        grid=(pl.cdiv(cdim, tc),), in_specs=[blk] * 4, out_specs=(blk, blk, blk),
        compiler_params=_params(("parallel",)),
    )(w, g, m, v)


def _adam_small(ws, gs, ms, vs, name):
    n = len(ws)

    def body(*refs):
        w_refs, g_refs, m_refs, v_refs = (refs[k * n:(k + 1) * n] for k in range(4))
        d_refs, mo_refs, vo_refs = (refs[(4 + k) * n:(5 + k) * n] for k in range(3))
        for a in range(n):
            delta, m2, v2 = _adam_math(w_refs[a][...], g_refs[a][...], m_refs[a][...], v_refs[a][...])
            d_refs[a][...] = delta
            mo_refs[a][...] = m2
            vo_refs[a][...] = v2

    shapes = tuple(jax.ShapeDtypeStruct(w.shape, F32) for w in ws)
    vm = pl.BlockSpec(memory_space=pltpu.VMEM)
    outs = pl.pallas_call(body, name=name, out_shape=shapes * 3, in_specs=[vm] * (4 * n),
                          out_specs=tuple([vm] * (3 * n)))(*ws, *gs, *ms, *vs)
    return outs[:n], outs[n:2 * n], outs[2 * n:]


PACK_TILE = SUBLANES * LANES


def _pack(arrays):
    rows = []
    for a in arrays:
        flat = a.reshape(-1)
        pad = (-flat.shape[0]) % PACK_TILE
        if pad:
            flat = jnp.concatenate([flat, jnp.zeros((pad,), flat.dtype)])
        rows.append(flat.reshape(-1, LANES))
    return jnp.concatenate(rows, axis=0)


def _unpack(pack, shapes):
    outs, row = [], 0
    for shp in shapes:
        n = int(np.prod(shp))
        nrows = -(-n // PACK_TILE) * SUBLANES
        outs.append(pack[row:row + nrows].reshape(-1)[:n].reshape(shp))
        row += nrows
    return outs


SMALL = ["norm_w", "ssd_conv_b", "ssd_dt_bias", "ssd_a_log", "ssd_d", "ssd_norm_w", "attn_sinks",
         "conf_dw_b", "conf_ln_w", "conf_ln_b"]
WEIGHTS = ["norm_w", "w_in", "ssd_conv_w", "ssd_conv_b", "ssd_dt_bias", "ssd_a_log", "ssd_d", "ssd_norm_w",
           "attn_sinks", "conf_dw_w", "conf_dw_b", "conf_ln_w", "conf_ln_b", "w_out", "final_norm_w"]


def kernel(x, norm_w, w_in, ssd_conv_w, ssd_conv_b, ssd_dt_bias, ssd_a_log, ssd_d, ssd_norm_w, attn_sinks, conf_dw_w, conf_dw_b, conf_ln_w, conf_ln_b, w_out, final_norm_w, loss_target, m_norm_w, m_w_in, m_ssd_conv_w, m_ssd_conv_b, m_ssd_dt_bias, m_ssd_a_log, m_ssd_d, m_ssd_norm_w, m_attn_sinks, m_conf_dw_w, m_conf_dw_b, m_conf_ln_w, m_conf_ln_b, m_w_out, m_final_norm_w, v_norm_w, v_w_in, v_ssd_conv_w, v_ssd_conv_b, v_ssd_dt_bias, v_ssd_a_log, v_ssd_d, v_ssd_norm_w, v_attn_sinks, v_conf_dw_w, v_conf_dw_b, v_conf_ln_w, v_conf_ln_b, v_w_out, v_final_norm_w):
    w = dict(norm_w=norm_w, w_in=w_in, ssd_conv_w=ssd_conv_w, ssd_conv_b=ssd_conv_b, ssd_dt_bias=ssd_dt_bias,
             ssd_a_log=ssd_a_log, ssd_d=ssd_d, ssd_norm_w=ssd_norm_w, attn_sinks=attn_sinks, conf_dw_w=conf_dw_w,
             conf_dw_b=conf_dw_b, conf_ln_w=conf_ln_w, conf_ln_b=conf_ln_b, w_out=w_out, final_norm_w=final_norm_w)
    m = dict(norm_w=m_norm_w, w_in=m_w_in, ssd_conv_w=m_ssd_conv_w, ssd_conv_b=m_ssd_conv_b,
             ssd_dt_bias=m_ssd_dt_bias, ssd_a_log=m_ssd_a_log, ssd_d=m_ssd_d, ssd_norm_w=m_ssd_norm_w,
             attn_sinks=m_attn_sinks, conf_dw_w=m_conf_dw_w, conf_dw_b=m_conf_dw_b, conf_ln_w=m_conf_ln_w,
             conf_ln_b=m_conf_ln_b, w_out=m_w_out, final_norm_w=m_final_norm_w)
    v = dict(norm_w=v_norm_w, w_in=v_w_in, ssd_conv_w=v_ssd_conv_w, ssd_conv_b=v_ssd_conv_b,
             ssd_dt_bias=v_ssd_dt_bias, ssd_a_log=v_ssd_a_log, ssd_d=v_ssd_d, ssd_norm_w=v_ssd_norm_w,
             attn_sinks=v_attn_sinks, conf_dw_w=v_conf_dw_w, conf_dw_b=v_conf_dw_b, conf_ln_w=v_conf_ln_w,
             conf_ln_b=v_conf_ln_b, w_out=v_w_out, final_norm_w=v_final_norm_w)
    depth = w_in.shape[0]
    me = 2 * lax.axis_index("x") + lax.axis_index("y")

    assert depth == 2
    w_in_t = jnp.transpose(w_in, (2, 0, 1))
    w_in_b = _cast_cols_major(w_in_t, name="cast_w_in")
    w_out_b = [_cast_layer(w_out, li, name=f"cast_w_out_l{li}") for li in range(depth)]
    own0 = [w_in_b[0].reshape((2, -1) + w_in_b[0].shape[1:]), w_out_b[0].reshape((2, -1) + w_out_b[0].shape[1:]),
            ssd_conv_w, conf_dw_w]
    gathered0 = _gather_weights(own0[:2], own0[2:], name="gather_weights_l0")
    g_in0, g_out0, g_conv, g_dw = [lax.dynamic_update_index_in_dim(g_all, mine, me, 0)
                                   for g_all, mine in zip(gathered0, own0)]
    own1 = [w_in_b[1], w_out_b[1]]
    pending1, token1 = _split_start(own1, "bcast", gathered0[0], name="gather_l1_start")

    def small_full(li):
        return (jnp.concatenate([g_conv[p, li] for p in range(N_CHIPS)], axis=1),
                jnp.concatenate([g_dw[p, li] for p in range(N_CHIPS)], axis=1))

    def params_l0(_):
        w_in_p = _padded_from_chips([g_in0[p].reshape(w_in_b[0].shape) for p in range(N_CHIPS)])
        w_out_full = g_out0.reshape(-1, g_out0.shape[-1])
        return _layer_params(0, w_in_p, w_out_full, *small_full(0), w)

    def params_l1(layer_input):
        landed = _split_wait(pending1, len(own1), "bcast", layer_input, name="gather_l1_wait")
        g_in1, g_out1 = [lax.dynamic_update_index_in_dim(g_all, mine, me, 0) for g_all, mine in zip(landed, own1)]
        w_in_p = _padded_from_chips([g_in1[p] for p in range(N_CHIPS)])
        return _layer_params(1, w_in_p, g_out1.reshape(-1, g_out1.shape[-1]), *small_full(1), w)

    c = lax.axis_index("c")
    cols = w_in.shape[2]
    rows_out = w_out.shape[1]

    def grad_parts(g):
        dw = g["w_in_p"]
        p_in = jnp.stack([_chip_part_from_padded(dw, p, cols) for p in range(N_CHIPS)])
        return [p_in.reshape(N_CHIPS, 2, dw.shape[0] // 2, cols),
                g["w_out"].reshape(N_CHIPS, 2, rows_out // 2, D_MODEL)]

    def pair_sums(parts, sib, tag):
        return [_pair_sum(p, sb, c, MXU_DTYPE, name=f"grad_pair_sum_{k}_{tag}")
                for k, (p, sb) in enumerate(zip(parts, sib))]

    split = {"reduced": [lax.empty((depth, 2, w_in.shape[1] // 2, cols), F32),
                         lax.empty((depth, 2, rows_out // 2, D_MODEL), F32)]}

    def chip_sums(landed, sent, li):
        filled = [lax.dynamic_update_index_in_dim(r, lax.dynamic_index_in_dim(sk, me, 0, keepdims=False), me, 0)
                  for r, sk in zip(landed, sent)]
        halves = [_sum_lead(r, into, li, c, name=f"grad_chip_sum_{k}_l{li}")
                  for k, (r, into) in enumerate(zip(filled, split["reduced"]))]
        split["reduced"] = list(_pair_gather(halves, li, name=f"grad_pair_gather_l{li}"))

    def on_grads(li, g):
        if li != depth - 1:
            return None
        parts = grad_parts(g)
        swap_state, swap_token = _split_start(parts, "swap", g["w_out"], name="grad_swap_l1_start")

        def after_dycat(dycat):
            sib = _split_wait(swap_state, len(parts), "swap", dycat, name="grad_swap_l1_wait")
            split["sent"] = pair_sums(parts, sib, "l1")
            split["scatter"], token = _split_start(split["sent"], "scatter", split["sent"][0],
                                                   name="grad_scatter_l1_start")
            return token

        def after_attn(dproj):
            landed = _split_wait(split["scatter"], len(parts), "scatter", dproj, name="grad_scatter_l1_wait")
            chip_sums(landed, split["sent"], depth - 1)

        return {"start_token": swap_token, "after_dycat": after_dycat, "after_attn": after_attn}

    loss, grad_x, grads, dfinal = _local_step(x, loss_target, [params_l0, params_l1], final_norm_w,
                                              first_after=token1, on_grads=on_grads)

    parts0 = grad_parts(grads[0])
    sent0 = pair_sums(parts0, _pair_swap_halves(parts0, name="grad_pair_swap_l0"), "l0")
    scatter0, token0 = _split_start(sent0, "scatter", sent0[0], name="grad_scatter_l0_start")

    small_list = [grads[li][n] for li in range(depth) for n in SMALL]
    small_list += [grads[li][n] for li in range(depth) for n in ("ssd_conv_w", "conf_dw_w")]
    small_list += [dfinal, loss.reshape(1)]
    small_shapes = [a.shape for a in small_list]
    reduced = _unpack(_allreduce_small(_pack(small_list) + token0[0, 0], name="allreduce_small"), small_shapes)
    ns = len(SMALL)
    g = {n: jnp.stack([reduced[li * ns + i] for li in range(depth)]) for i, n in enumerate(SMALL)}
    conv_w_cols, dw_w_cols = ssd_conv_w.shape[2], conf_dw_w.shape[2]
    g["ssd_conv_w"] = jnp.stack([lax.dynamic_slice_in_dim(reduced[depth * ns + 2 * li], me * conv_w_cols,
                                                          conv_w_cols, axis=1) for li in range(depth)])
    g["conf_dw_w"] = jnp.stack([lax.dynamic_slice_in_dim(reduced[depth * ns + 2 * li + 1], me * dw_w_cols,
                                                         dw_w_cols, axis=1) for li in range(depth)])
    g["final_norm_w"] = reduced[-2]
    loss_total = reduced[-1][0]

    small_names = [n for n in WEIGHTS if n not in ("w_in", "w_out")]

    def as2d(a):
        return a.reshape(1, -1) if a.ndim == 1 else a

    deltas, new_ms, new_vs = _adam_small(*[[as2d(src[n]) for n in small_names] for src in (w, g, m, v)],
                                         name="adam_small")

    chip_sums(_split_wait(scatter0, len(sent0), "scatter", deltas[0], name="grad_scatter_l0_wait"), sent0, 0)
    g_w_in = split["reduced"][0].reshape(w_in.shape)
    g_w_out = split["reduced"][1].reshape(w_out.shape)

    outs_g, outs_d, outs_m, outs_v = {"w_in": g_w_in, "w_out": g_w_out}, {}, {}, {}
    to_cols, from_cols = (2, 0, 1), (1, 2, 0)
    outs_d["w_in"], outs_m["w_in"], outs_v["w_in"] = [
        jnp.transpose(a, from_cols) for a in _adam_cols_major(
            *[jnp.transpose(a, to_cols) for a in (w_in, g_w_in, m_w_in, v_w_in)], name="adam_w_in")]
    outs_d["w_out"], outs_m["w_out"], outs_v["w_out"] = _adam_big(w_out, g_w_out, m_w_out, v_w_out,
                                                                  name="adam_w_out")
    for n, dn, mn, vn in zip(small_names, deltas, new_ms, new_vs):
        outs_g[n], outs_d[n], outs_m[n], outs_v[n] = (g[n], dn.reshape(w[n].shape), mn.reshape(w[n].shape),
                                                      vn.reshape(w[n].shape))
    return (loss_total, grad_x, *[outs_g[n] for n in WEIGHTS], *[outs_d[n] for n in WEIGHTS],
            *[outs_m[n] for n in WEIGHTS], *[outs_v[n] for n in WEIGHTS])
```

```python
import functools
import math

import jax
import jax.numpy as jnp
import numpy as np
from jax import lax
from jax.experimental import pallas as pl
from jax.experimental.pallas import tpu as pltpu

F32 = jnp.float32
BF16 = jnp.bfloat16
MXU_DTYPE = BF16

D_MODEL = 1024
DEPTH = 2
SSD_HEADS = 16
SSD_HEAD_DIM = 64
SSD_STATE = 128
SSD_CONV = 4
CHUNK = 128
SSD_CONV_DIM = 1536
ATTN_HEAD_DIM = 64
ATTN_Q_HEADS = 8
WINDOW = 128
CONF_WIDTH = 512
CONF_KERNEL = 31
MIX_WIDTH = 2048
D_IN_PROJ = 5392
EPS = 1e-5

ADAM_LR = 0.001
ADAM_B1 = 0.9
ADAM_B2 = 0.999
ADAM_EPS = 1e-08
ADAM_WD = 0.01
ADAM_STEP = 10

LANES = 128
SUBLANES = 8
VMEM_LIMIT = 48 * 1024 * 1024

NP = 5632
OFF_ZA, OFF_Q, OFF_K, OFF_V, OFF_DT = 0, 512, 1024, 1152, 1280
ATTN_GROUP = 1536
OFF_CONF, OFF_ZC = 1536, 2560
CONF_GROUP = 1536
OFF_ZS = 3072
OFF_XBC = 4096
SECTIONS = ((0, 1024, OFF_ZS), (1024, 1536, OFF_ZA), (1536, 2048, OFF_ZC), (2048, 3584, OFF_XBC),
            (3584, 3600, OFF_DT), (3600, 4368, OFF_Q), (4368, 5392, OFF_CONF))

YCAT_ATTN, YCAT_CONF = 1024, 1536
ANY = pl.BlockSpec(memory_space=pl.ANY)

NN = (((1,), (0,)), ((), ()))
NT = (((1,), (1,)), ((), ()))
TN = (((0,), (0,)), ((), ()))


def _params(sem):
    return pltpu.CompilerParams(dimension_semantics=sem, vmem_limit_bytes=VMEM_LIMIT)


def _dot(a, b, dims=NN):
    return lax.dot_general(a.astype(MXU_DTYPE), b.astype(MXU_DTYPE), dims, preferred_element_type=F32)


def _split_bf16(a, passes):
    pieces = []
    r = a
    for _ in range(passes):
        p = r.astype(BF16)
        pieces.append(p)
        r = r - p.astype(F32)
    return pieces


def _xdot(a, sel, dims=NN, passes=2):
    out = None
    for p in _split_bf16(a, passes):
        t = lax.dot_general(p, sel, dims, preferred_element_type=F32)
        out = t if out is None else out + t
    return out


def _xdot_r(sel, b, dims=NN, passes=3):
    out = None
    for p in _split_bf16(b, passes):
        t = lax.dot_general(sel, p, dims, preferred_element_type=F32)
        out = t if out is None else out + t
    return out


def _sigmoid(x):
    return 1.0 / (1.0 + jnp.exp(-x))


def _silu(x):
    return x * _sigmoid(x)


def _dsilu(x):
    s = _sigmoid(x)
    return s * (1.0 + x * (1.0 - s))


def _softplus(x):
    return jnp.maximum(x, 0.0) + jnp.log(1.0 + jnp.exp(-jnp.abs(x)))


def _rowsum8(x):
    r, c = x.shape
    return jnp.sum(x.reshape(r // SUBLANES, SUBLANES, c), axis=0)


def _iota(shape, dim):
    return lax.broadcasted_iota(jnp.int32, shape, dim)


def _matmul(a, b, form, out_dtype, tm, tn, tk, name, residual=None, after=None):
    if form == "nn":
        (m, k), n = a.shape, b.shape[1]
    elif form == "nt":
        (m, k), n = a.shape, b.shape[0]
    else:
        (k, m), n = a.shape, b.shape[1]
    tm, tn, tk = min(tm, m), min(tn, n), min(tk, k)
    assert m % tm == 0 and n % tn == 0 and k % tk == 0, (name, m, n, k, tm, tn, tk)
    if form == "nn":
        a_spec = pl.BlockSpec((tm, tk), lambda i, j, s: (i, s))
        b_spec = pl.BlockSpec((tk, tn), lambda i, j, s: (s, j))
        dims = NN
    elif form == "nt":
        (m, k), n = a.shape, b.shape[0]
        a_spec = pl.BlockSpec((tm, tk), lambda i, j, s: (i, s))
        b_spec = pl.BlockSpec((tn, tk), lambda i, j, s: (j, s))
        dims = NT
    else:
        (k, m), n = a.shape, b.shape[1]
        a_spec = pl.BlockSpec((tk, tm), lambda i, j, s: (s, i))
        b_spec = pl.BlockSpec((tk, tn), lambda i, j, s: (s, j))
        dims = TN
    nk = k // tk
    has_res = residual is not None
    deps = [] if after is None else [after]

    def body_single(a_ref, b_ref, *rest):
        o = _dot(a_ref[...], b_ref[...], dims)
        if has_res:
            o = o + rest[0][...]
        rest[-1][...] = o.astype(out_dtype)

    def body(a_ref, b_ref, *rest):
        r_ref = rest[0] if has_res else None
        o_ref, acc = rest[-2:]
        s = pl.program_id(2)

        @pl.when(s == 0)
        def _():
            acc[...] = jnp.zeros_like(acc)

        acc[...] += _dot(a_ref[...], b_ref[...], dims)

        @pl.when(s == nk - 1)
        def _():
            o = acc[...]
            if has_res:
                o = o + r_ref[...]
            o_ref[...] = o.astype(out_dtype)

    in_specs = [a_spec, b_spec]
    args = [a, b]
    if has_res:
        in_specs.append(pl.BlockSpec((tm, tn), lambda i, j, s: (i, j)))
        args.append(residual)
    in_specs += [ANY] * len(deps)
    args += deps
    return pl.pallas_call(
        body_single if nk == 1 else body, name=name,
        out_shape=jax.ShapeDtypeStruct((m, n), out_dtype),
        grid=(m // tm, n // tn, nk),
        in_specs=in_specs,
        out_specs=pl.BlockSpec((tm, tn), lambda i, j, s: (i, j)),
        scratch_shapes=[] if nk == 1 else [pltpu.VMEM((tm, tn), F32)],
        compiler_params=_params(("parallel", "parallel", "arbitrary")),
    )(*args)


ROW_TILE = 256


PROJ_FWD_TM, PROJ_FWD_TN = 1024, 512


def _proj_fwd(x, w, w_in_p, name, after=None):
    t, d = x.shape
    n = w_in_p.shape[1]
    tm, tn = min(PROJ_FWD_TM, t), PROJ_FWD_TN
    assert t % tm == 0 and n % tn == 0
    deps = [] if after is None else [after]

    def body(x_ref, w_ref, b_ref, *rest):
        o_ref, ot_ref, h_scr = rest[len(deps):]

        @pl.when(pl.program_id(1) == 0)
        def _():
            xv = x_ref[...]
            rstd = lax.rsqrt(jnp.mean(xv * xv, axis=-1, keepdims=True) + EPS)
            h = xv * rstd * w_ref[...]
            h_scr[...] = h.astype(h_scr.dtype)
            ot_ref[...] = h.T.astype(ot_ref.dtype)

        o_ref[...] = _dot(h_scr[...], b_ref[...])

    return pl.pallas_call(
        body, name=name,
        out_shape=(jax.ShapeDtypeStruct((t, n), F32), jax.ShapeDtypeStruct((d, t), MXU_DTYPE)),
        grid=(t // tm, n // tn),
        in_specs=[pl.BlockSpec((tm, d), lambda i, j: (i, 0)), pl.BlockSpec((1, d), lambda i, j: (0, 0)),
                  pl.BlockSpec((d, tn), lambda i, j: (0, j))] + [ANY] * len(deps),
        out_specs=(pl.BlockSpec((tm, tn), lambda i, j: (i, j)), pl.BlockSpec((d, tm), lambda i, j: (0, i))),
        scratch_shapes=[pltpu.VMEM((tm, d), MXU_DTYPE)],
        compiler_params=_params(("parallel", "arbitrary")),
    )(x, w, w_in_p, *deps)


PROJ_BWD_TM, PROJ_BWD_TK = 1024, 1408


def _proj_bwd_dx(dproj, w_in_p, x, w, dres, name):
    t, d = x.shape
    kdim = dproj.shape[1]
    tm, tk = min(PROJ_BWD_TM, t), PROJ_BWD_TK
    nt, nk = t // tm, kdim // tk
    assert t % tm == 0 and kdim % tk == 0

    def body(a_ref, b_ref, x_ref, w_ref, dr_ref, dx_ref, dw_ref, acc, wacc):
        i, s = pl.program_id(0), pl.program_id(1)

        @pl.when((i == 0) & (s == 0))
        def _():
            wacc[...] = jnp.zeros_like(wacc)

        @pl.when(s == 0)
        def _():
            acc[...] = jnp.zeros_like(acc)

        acc[...] += _dot(a_ref[...], b_ref[...], NT)

        @pl.when(s == nk - 1)
        def _():
            xv = x_ref[...]
            rstd = lax.rsqrt(jnp.mean(xv * xv, axis=-1, keepdims=True) + EPS)
            xh = xv * rstd
            dhv = acc[...]
            g = dhv * w_ref[...]
            dx_ref[...] = dr_ref[...] + rstd * (g - xh * jnp.mean(g * xh, axis=-1, keepdims=True))
            wacc[...] += _rowsum8(dhv * xh)

        @pl.when((i == nt - 1) & (s == nk - 1))
        def _():
            dw_ref[...] = jnp.sum(wacc[...], axis=0, keepdims=True)

    row = pl.BlockSpec((tm, d), lambda i, s: (i, 0))
    vec = pl.BlockSpec((1, d), lambda i, s: (0, 0))
    return pl.pallas_call(
        body, name=name,
        out_shape=(jax.ShapeDtypeStruct((t, d), F32), jax.ShapeDtypeStruct((1, d), F32)),
        grid=(nt, nk),
        in_specs=[pl.BlockSpec((tm, tk), lambda i, s: (i, s)), pl.BlockSpec((d, tk), lambda i, s: (0, s)),
                  row, vec, row],
        out_specs=(row, vec),
        scratch_shapes=[pltpu.VMEM((tm, d), F32), pltpu.VMEM((SUBLANES, d), F32)],
        compiler_params=_params(("arbitrary", "arbitrary")),
    )(dproj, w_in_p, x, w, dres)


def _loss_head(xf, target, w, name):
    t, d = xf.shape
    tm = ROW_TILE
    nt = t // tm

    def body(x_ref, t_ref, w_ref, loss_ref, dx_ref, dw_ref, lacc, wacc):
        i = pl.program_id(0)

        @pl.when(i == 0)
        def _():
            lacc[...] = jnp.zeros_like(lacc)
            wacc[...] = jnp.zeros_like(wacc)

        xv = x_ref[...]
        rstd = lax.rsqrt(jnp.mean(xv * xv, axis=-1, keepdims=True) + EPS)
        xh = xv * rstd
        err = xh * w_ref[...] - t_ref[...]
        lacc[...] += jnp.sum(err * err)
        dy = err * (1.0 / d)
        g = dy * w_ref[...]
        dx_ref[...] = rstd * (g - xh * jnp.mean(g * xh, axis=-1, keepdims=True))
        wacc[...] += _rowsum8(dy * xh)

        @pl.when(i == nt - 1)
        def _():
            loss_ref[...] = lacc[...] * (0.5 / d)
            dw_ref[...] = jnp.sum(wacc[...], axis=0, keepdims=True)

    row = pl.BlockSpec((tm, d), lambda i: (i, 0))
    vec = pl.BlockSpec((1, d), lambda i: (0, 0))
    return pl.pallas_call(
        body, name=name,
        out_shape=(jax.ShapeDtypeStruct((SUBLANES, LANES), F32), jax.ShapeDtypeStruct((t, d), F32),
                   jax.ShapeDtypeStruct((1, d), F32)),
        grid=(nt,),
        in_specs=[row, row, vec],
        out_specs=(pl.BlockSpec((SUBLANES, LANES), lambda i: (0, 0)), row, vec),
        scratch_shapes=[pltpu.VMEM((SUBLANES, LANES), F32), pltpu.VMEM((SUBLANES, d), F32)],
        compiler_params=_params(("arbitrary",)),
    )(xf, target, w)


CONV_TILE = 512
CONV_COLS = 512
CONV_SUB_ROWS = 128
CONV_SUB_COLS = LANES


def _conv_halo(k):
    return SUBLANES if k - 1 <= SUBLANES else 32


def _conv_subtiles(tm, cw):
    return [(r0, c0) for r0 in range(0, tm, CONV_SUB_ROWS) for c0 in range(0, cw, CONV_SUB_COLS)]


def _conv_use_shifted(k):
    return k > SUBLANES


def _conv_shift_scratch(k, rows, cw):
    return [pltpu.VMEM((SUBLANES - 1, rows - SUBLANES, cw), F32)] if _conv_use_shifted(k) else []


def _conv_fill_shifted(ext, sh):
    n = sh.shape[1]
    for b in range(1, SUBLANES):
        sh[b - 1] = ext[b:b + n, :]


def _conv_rows(ext, sh, start, rows, cs):
    b = start % SUBLANES
    if b == 0 or not sh:
        return ext[start:start + rows, cs]
    return sh[0][b - 1, start - b:start - b + rows, cs]


def _conv_fwd(src, col0, width, w, bias, k, seq, name):
    t = src.shape[0]
    tm, cw, halo = CONV_TILE, CONV_COLS, _conv_halo(k)
    sr, sc = CONV_SUB_ROWS, CONV_SUB_COLS
    p = k - 1
    cb0 = col0 // cw
    kp = w.shape[0]

    shifted = _conv_use_shifted(k)

    def body(x_ref, h_ref, w_ref, b_ref, o_ref, ext, *sh):
        i = pl.program_id(0)
        seq_start = (i * tm) % seq == 0
        ext[halo:, :] = x_ref[...]
        ext[:halo, :] = jnp.where(seq_start, 0.0, h_ref[...])
        if shifted:
            _conv_fill_shifted(ext, sh[0])
        for r0, c0 in _conv_subtiles(tm, cw):
            cs = slice(c0, c0 + sc)
            acc = jnp.zeros((sr, sc), F32) + b_ref[:, cs]
            for j in range(k):
                acc = acc + w_ref[j:j + 1, cs] * _conv_rows(ext, sh, r0 + halo - p + j, sr, cs)
            o_ref[r0:r0 + sr, cs] = acc

    return pl.pallas_call(
        body, name=name,
        out_shape=jax.ShapeDtypeStruct((t, width), F32),
        grid=(t // tm, width // cw),
        in_specs=[pl.BlockSpec((tm, cw), lambda i, j: (i, cb0 + j)),
                  pl.BlockSpec((halo, cw), lambda i, j: (jnp.maximum(i * (tm // halo) - 1, 0), cb0 + j)),
                  pl.BlockSpec((kp, cw), lambda i, j: (0, j)),
                  pl.BlockSpec((1, cw), lambda i, j: (0, j))],
        out_specs=pl.BlockSpec((tm, cw), lambda i, j: (i, j)),
        scratch_shapes=[pltpu.VMEM((halo + tm, cw), F32)] + _conv_shift_scratch(k, halo + tm, cw),
        compiler_params=_params(("parallel", "parallel")),
    )(src, src, w, bias)


def _conv_bwd(dy, src, col0, width, w, k, seq, name, into=None):
    t = src.shape[0]
    tm, cw, halo = CONV_TILE, CONV_COLS, _conv_halo(k)
    sr, sc = CONV_SUB_ROWS, CONV_SUB_COLS
    p = k - 1
    cb0 = col0 // cw
    kp = w.shape[0]
    nt = t // tm
    last_halo = t // halo - 1

    shifted = _conv_use_shifted(k)

    def body(dy_ref, dn_ref, x_ref, xp_ref, w_ref, *rest):
        if into is not None:
            rest = rest[1:]
        dx_ref, dw_ref, db_ref, dyext, xext, wacc, bacc = rest[:7]
        sh = rest[7:]
        i = pl.program_id(1)
        dysh, xsh = (sh[:1], sh[1:]) if shifted else ((), ())

        @pl.when(i == 0)
        def _():
            wacc[...] = jnp.zeros_like(wacc)
            bacc[...] = jnp.zeros_like(bacc)

        seq_start = (i * tm) % seq == 0
        seq_end = ((i + 1) * tm) % seq == 0
        dyext[:tm, :] = dy_ref[...]
        dyext[tm:, :] = jnp.where(seq_end, 0.0, dn_ref[...])
        xext[halo:, :] = x_ref[...]
        xext[:halo, :] = jnp.where(seq_start, 0.0, xp_ref[...])
        if shifted:
            _conv_fill_shifted(dyext, dysh[0])
            _conv_fill_shifted(xext, xsh[0])
        for r0, c0 in _conv_subtiles(tm, cw):
            cs = slice(c0, c0 + sc)
            dyv = dy_ref[r0:r0 + sr, cs]
            acc = jnp.zeros((sr, sc), F32)
            for j in range(k):
                acc = acc + w_ref[j:j + 1, cs] * _conv_rows(dyext, dysh, r0 + p - j, sr, cs)
                wacc[j, :, cs] += _rowsum8(dyv * _conv_rows(xext, xsh, r0 + halo - p + j, sr, cs))
            dx_ref[r0:r0 + sr, cs] = acc.astype(dx_ref.dtype)
            bacc[:, cs] += _rowsum8(dyv)

        @pl.when(i == nt - 1)
        def _():
            dw_ref[...] = jnp.zeros_like(dw_ref)
            for j in range(k):
                dw_ref[j:j + 1, :] = jnp.sum(wacc[j], axis=0, keepdims=True)
            db_ref[...] = jnp.sum(bacc[...], axis=0, keepdims=True)

    if into is None:
        dx_shape = jax.ShapeDtypeStruct((t, width), F32)
        dx_spec = pl.BlockSpec((tm, cw), lambda j, i: (i, j))
        extra_specs, extra_args, aliases = [], [], {}
    else:
        dx_shape = jax.ShapeDtypeStruct(into.shape, into.dtype)
        dx_spec = pl.BlockSpec((tm, cw), lambda j, i: (i, cb0 + j))
        extra_specs, extra_args, aliases = [ANY], [into], {5: 0}
    return pl.pallas_call(
        body, name=name,
        out_shape=(dx_shape, jax.ShapeDtypeStruct((kp, width), F32), jax.ShapeDtypeStruct((1, width), F32)),
        grid=(width // cw, nt),
        in_specs=[pl.BlockSpec((tm, cw), lambda j, i: (i, j)),
                  pl.BlockSpec((halo, cw), lambda j, i: (jnp.minimum((i + 1) * (tm // halo), last_halo), j)),
                  pl.BlockSpec((tm, cw), lambda j, i: (i, cb0 + j)),
                  pl.BlockSpec((halo, cw), lambda j, i: (jnp.maximum(i * (tm // halo) - 1, 0), cb0 + j)),
                  pl.BlockSpec((kp, cw), lambda j, i: (0, j))] + extra_specs,
        out_specs=(dx_spec,
                   pl.BlockSpec((kp, cw), lambda j, i: (0, j)),
                   pl.BlockSpec((1, cw), lambda j, i: (0, j))),
        input_output_aliases=aliases,
        scratch_shapes=[pltpu.VMEM((tm + halo, cw), F32), pltpu.VMEM((halo + tm, cw), F32),
                        pltpu.VMEM((kp, SUBLANES, cw), F32), pltpu.VMEM((SUBLANES, cw), F32)]
        + 2 * _conv_shift_scratch(k, halo + tm, cw),
        compiler_params=_params(("parallel", "arbitrary")),
    )(dy, dy, src, src, w, *extra_args)


def _conf_specs(tm, cw, halo, order):
    cb = OFF_CONF // cw

    def blk(col):
        return pl.BlockSpec((tm, cw), lambda *g: (order(*g), col))

    def prev(col):
        return pl.BlockSpec((halo, cw), lambda *g: (jnp.maximum(order(*g) * (tm // halo) - 1, 0), col))

    return blk(cb), prev(cb), blk(cb + 1), prev(cb + 1)


def _glu_window(ext, a_ref, ah_ref, g_ref, gh_ref, seq_start, halo):
    ext[halo:, :] = a_ref[...] * _sigmoid(g_ref[...])
    ext[:halo, :] = jnp.where(seq_start, 0.0, ah_ref[...] * _sigmoid(gh_ref[...]))


def _conf_fwd(proj, w, bias, ln_w, ln_b, ycat, seq, name):
    t = proj.shape[0]
    k = CONF_KERNEL
    tm, cw, halo = CONV_TILE, CONF_WIDTH, _conv_halo(k)
    sr, sc = CONV_SUB_ROWS, CONV_SUB_COLS
    p = k - 1
    kp = w.shape[0]

    def body(a_ref, ah_ref, g_ref, gh_ref, z_ref, w_ref, b_ref, lw_ref, lb_ref, _, c1_ref, y_ref, ext, sh):
        i = pl.program_id(0)
        _glu_window(ext, a_ref, ah_ref, g_ref, gh_ref, (i * tm) % seq == 0, halo)
        _conv_fill_shifted(ext, sh)
        for r0, c0 in _conv_subtiles(tm, cw):
            cs = slice(c0, c0 + sc)
            acc = jnp.zeros((sr, sc), F32) + b_ref[:, cs]
            for j in range(k):
                acc = acc + w_ref[j:j + 1, cs] * _conv_rows(ext, (sh,), r0 + halo - p + j, sr, cs)
            c1_ref[r0:r0 + sr, cs] = acc
        for r0 in range(0, tm, sr):
            rows = slice(r0, r0 + sr)
            cv = c1_ref[rows, :]
            xc = cv - jnp.mean(cv, axis=-1, keepdims=True)
            rstd = lax.rsqrt(jnp.mean(xc * xc, axis=-1, keepdims=True) + EPS)
            c2 = xc * rstd * lw_ref[...] + lb_ref[...]
            y_ref[rows, :] = (_silu(c2) * _silu(z_ref[rows, :])).astype(y_ref.dtype)

    vec = pl.BlockSpec((1, cw), lambda i: (0, 0))
    row = pl.BlockSpec((tm, cw), lambda i: (i, 0))
    return pl.pallas_call(
        body, name=name,
        out_shape=(jax.ShapeDtypeStruct((t, cw), F32), jax.ShapeDtypeStruct(ycat.shape, ycat.dtype)),
        grid=(t // tm,),
        in_specs=[*_conf_specs(tm, cw, halo, lambda i: i),
                  pl.BlockSpec((tm, cw), lambda i: (i, OFF_ZC // cw)),
                  pl.BlockSpec((kp, cw), lambda i: (0, 0)), vec, vec, vec, ANY],
        out_specs=(row, pl.BlockSpec((tm, cw), lambda i: (i, YCAT_CONF // cw))),
        input_output_aliases={9: 1},
        scratch_shapes=[pltpu.VMEM((halo + tm, cw), F32)] + _conv_shift_scratch(k, halo + tm, cw),
        compiler_params=_params(("parallel",)),
    )(proj, proj, proj, proj, proj, w, bias, ln_w, ln_b, ycat)


def _conf_bwd(dycat, proj, c1, w, ln_w, ln_b, dproj, seq, name):
    t = proj.shape[0]
    k = CONF_KERNEL
    tm, cw, halo = CONV_TILE, CONF_WIDTH, _conv_halo(k)
    sr, sc = CONV_SUB_ROWS, CONV_SUB_COLS
    p = k - 1
    kp = w.shape[0]
    nt = t // tm
    last_halo = t // halo - 1

    def body(dy_ref, dyn_ref, c_ref, cn_ref, z_ref, zn_ref, a_ref, ah_ref, g_ref, gh_ref, w_ref, lw_ref, lb_ref, _,
             grp_ref, dw_ref, db_ref, dlw_ref, dlb_ref, dyext, xext, wacc, bacc, lwacc, lbacc, dysh, xsh):
        i = pl.program_id(0)

        @pl.when(i == 0)
        def _():
            wacc[...] = jnp.zeros_like(wacc)
            bacc[...] = jnp.zeros_like(bacc)
            lwacc[...] = jnp.zeros_like(lwacc)
            lbacc[...] = jnp.zeros_like(lbacc)

        def post_bwd(dy, cv, zv):
            xc = cv - jnp.mean(cv, axis=-1, keepdims=True)
            rstd = lax.rsqrt(jnp.mean(xc * xc, axis=-1, keepdims=True) + EPS)
            xh = xc * rstd
            c2 = xh * lw_ref[...] + lb_ref[...]
            dz = dy * _silu(c2) * _dsilu(zv)
            dc2 = dy * _silu(zv) * _dsilu(c2)
            dxh = dc2 * lw_ref[...]
            dc = rstd * (dxh - jnp.mean(dxh, axis=-1, keepdims=True)
                         - xh * jnp.mean(dxh * xh, axis=-1, keepdims=True))
            return dc, dz, dc2 * xh, dc2

        seq_end = ((i + 1) * tm) % seq == 0
        for r0 in range(0, tm, sr):
            rows = slice(r0, r0 + sr)
            dc, dz, lw_terms, lb_terms = post_bwd(dy_ref[rows, :], c_ref[rows, :], z_ref[rows, :])
            dyext[rows, :] = dc
            grp_ref[rows, 2 * cw:] = dz.astype(grp_ref.dtype)
            lwacc[...] += _rowsum8(lw_terms)
            lbacc[...] += _rowsum8(lb_terms)
        dc_next = post_bwd(dyn_ref[...], cn_ref[...], zn_ref[...])[0]
        dyext[tm:, :] = jnp.where(seq_end, 0.0, dc_next)
        _glu_window(xext, a_ref, ah_ref, g_ref, gh_ref, (i * tm) % seq == 0, halo)
        _conv_fill_shifted(dyext, dysh)
        _conv_fill_shifted(xext, xsh)
        dag_ref = grp_ref
        for r0, c0 in _conv_subtiles(tm, cw):
            cs = slice(c0, c0 + sc)
            rows = slice(r0, r0 + sr)
            dyv = dyext[rows, cs]
            acc = jnp.zeros((sr, sc), F32)
            for j in range(k):
                acc = acc + w_ref[j:j + 1, cs] * _conv_rows(dyext, (dysh,), r0 + p - j, sr, cs)
                wacc[j, :, cs] += _rowsum8(dyv * _conv_rows(xext, (xsh,), r0 + halo - p + j, sr, cs))
            bacc[:, cs] += _rowsum8(dyv)
            s = _sigmoid(g_ref[rows, cs])
            dag_ref[rows, cs] = (acc * s).astype(dag_ref.dtype)
            dag_ref[rows, cw + c0:cw + c0 + sc] = (acc * a_ref[rows, cs] * s * (1.0 - s)).astype(dag_ref.dtype)

        @pl.when(i == nt - 1)
        def _():
            dw_ref[...] = jnp.zeros_like(dw_ref)
            for j in range(k):
                dw_ref[j:j + 1, :] = jnp.sum(wacc[j], axis=0, keepdims=True)
            db_ref[...] = jnp.sum(bacc[...], axis=0, keepdims=True)
            dlw_ref[...] = jnp.sum(lwacc[...], axis=0, keepdims=True)
            dlb_ref[...] = jnp.sum(lbacc[...], axis=0, keepdims=True)

    def blk(col):
        return pl.BlockSpec((tm, cw), lambda i: (i, col))

    def nxt(col):
        return pl.BlockSpec((halo, cw), lambda i: (jnp.minimum((i + 1) * (tm // halo), last_halo), col))

    vec = pl.BlockSpec((1, cw), lambda i: (0, 0))
    return pl.pallas_call(
        body, name=name,
        out_shape=(jax.ShapeDtypeStruct(dproj.shape, dproj.dtype), jax.ShapeDtypeStruct((kp, cw), F32),
                   jax.ShapeDtypeStruct((1, cw), F32), jax.ShapeDtypeStruct((1, cw), F32),
                   jax.ShapeDtypeStruct((1, cw), F32)),
        grid=(nt,),
        in_specs=[blk(YCAT_CONF // cw), nxt(YCAT_CONF // cw), blk(0), nxt(0), blk(OFF_ZC // cw), nxt(OFF_ZC // cw),
                  *_conf_specs(tm, cw, halo, lambda i: i),
                  pl.BlockSpec((kp, cw), lambda i: (0, 0)), vec, vec, ANY],
        out_specs=(pl.BlockSpec((tm, CONF_GROUP), lambda i: (i, OFF_CONF // CONF_GROUP)),
                   pl.BlockSpec((kp, cw), lambda i: (0, 0)), vec, vec, vec),
        input_output_aliases={13: 0},
        scratch_shapes=[pltpu.VMEM((tm + halo, cw), F32), pltpu.VMEM((halo + tm, cw), F32),
                        pltpu.VMEM((kp, SUBLANES, cw), F32), pltpu.VMEM((SUBLANES, cw), F32),
                        pltpu.VMEM((SUBLANES, cw), F32), pltpu.VMEM((SUBLANES, cw), F32)]
        + 2 * _conv_shift_scratch(k, halo + tm, cw),
        compiler_params=_params(("arbitrary",)),
    )(dycat, dycat, c1, c1, proj, proj, proj, proj, proj, proj, w, ln_w, ln_b, dproj)


def _half_mask(half):
    lane = _iota((1, LANES), 1)
    return ((lane >= half * ATTN_HEAD_DIM) & (lane < (half + 1) * ATTN_HEAD_DIM)).astype(F32)


def _stack_heads(xp, g):
    m = _half_mask(g)
    swapped = pltpu.roll(xp, ATTN_HEAD_DIM, axis=1)
    return jnp.concatenate([xp * m, swapped * m] if g == 0 else [swapped * m, xp * m], axis=0)


def _unstack_heads(both, g):
    w = both.shape[0] // 2
    top, bot = both[:w], both[w:]
    lo, hi = _half_mask(0), _half_mask(1)
    if g == 0:
        return top * lo + pltpu.roll(bot, ATTN_HEAD_DIM, axis=1) * hi
    return pltpu.roll(top, ATTN_HEAD_DIM, axis=1) * lo + bot * hi


def _band_mask(first_block):
    w = WINDOW
    qi = _iota((w, 2 * w), 0)
    kj = _iota((w, 2 * w), 1) - w
    rel = qi - kj
    return (rel >= 0) & (rel < w) & (jnp.logical_not(first_block) | (kj >= 0))


def _lane_pick(x, h):
    return jnp.sum(jnp.where(_iota(x.shape, 1) == h, x, 0.0), axis=1, keepdims=True)


def _attn_specs(nb, rev):
    w = WINDOW

    def blk(i):
        return nb - 1 - i if rev else i

    def row(b, i):
        return b * nb + blk(i)

    def prow(b, i):
        return b * nb + jnp.maximum(blk(i) - 1, 0)

    q = pl.BlockSpec((w, 512), lambda b, i: (row(b, i), OFF_Q // 512))
    kc = pl.BlockSpec((w, 128), lambda b, i: (row(b, i), OFF_K // 128))
    kp = pl.BlockSpec((w, 128), lambda b, i: (prow(b, i), OFF_K // 128))
    vc = pl.BlockSpec((w, 128), lambda b, i: (row(b, i), OFF_V // 128))
    vp = pl.BlockSpec((w, 128), lambda b, i: (prow(b, i), OFF_V // 128))
    z = pl.BlockSpec((w, 512), lambda b, i: (row(b, i), OFF_ZA // 512))
    return q, kc, kp, vc, vp, z, row


def _attn_fwd(proj, sinks, ycat, nbatch, name):
    t = proj.shape[0]
    w = WINDOW
    nb = t // nbatch // w
    scale = ATTN_HEAD_DIM ** -0.5
    q_s, kc_s, kp_s, vc_s, vp_s, z_s, row = _attn_specs(nb, False)

    def body(q_ref, kc_ref, kp_ref, vc_ref, vp_ref, z_ref, sk_ref, _, y_ref, o_ref, lse_ref):
        first = pl.program_id(1) == 0
        mask = _band_mask(first)
        kk = jnp.concatenate([kp_ref[...], kc_ref[...]], axis=0).astype(MXU_DTYPE)
        vv = jnp.concatenate([vp_ref[...], vc_ref[...]], axis=0).astype(MXU_DTYPE)
        sk = sk_ref[...]
        lane = _iota((w, LANES), 1)
        mask2 = jnp.concatenate([mask, mask], axis=0)
        scores = [_dot(_stack_heads(q_ref[:, j * LANES:(j + 1) * LANES], j // 2), kk, NT) for j in range(4)]
        lse_all = jnp.zeros((w, LANES), F32)
        for j in range(4):
            s = jnp.where(mask2, scores[j] * scale, -1e30)
            skc = jnp.concatenate([jnp.broadcast_to(_lane_pick(sk, 2 * j), (w, 1)),
                                   jnp.broadcast_to(_lane_pick(sk, 2 * j + 1), (w, 1))], axis=0)
            m = jnp.maximum(jnp.max(s, axis=1, keepdims=True), skc)
            den = jnp.sum(jnp.exp(s - m), axis=1, keepdims=True) + jnp.exp(skc - m)
            lse = m + jnp.log(den)
            lse_all = jnp.where(lane == 2 * j, lse[:w], lse_all)
            lse_all = jnp.where(lane == 2 * j + 1, lse[w:], lse_all)
            op = _unstack_heads(_dot(jnp.exp(s - lse), vv), j // 2)
            cols = slice(j * LANES, (j + 1) * LANES)
            o_ref[:, cols] = op
            y_ref[:, cols] = (op * _silu(z_ref[:, cols])).astype(y_ref.dtype)
        lse_ref[...] = lse_all

    return pl.pallas_call(
        body, name=name,
        out_shape=(jax.ShapeDtypeStruct(ycat.shape, ycat.dtype), jax.ShapeDtypeStruct((t, 512), F32),
                   jax.ShapeDtypeStruct((t, LANES), F32)),
        grid=(nbatch, nb),
        in_specs=[q_s, kc_s, kp_s, vc_s, vp_s, z_s, pl.BlockSpec((1, LANES), lambda b, i: (0, 0)), ANY],
        out_specs=(pl.BlockSpec((w, 512), lambda b, i: (row(b, i), YCAT_ATTN // 512)),
                   pl.BlockSpec((w, 512), lambda b, i: (row(b, i), 0)),
                   pl.BlockSpec((w, LANES), lambda b, i: (row(b, i), 0))),
        input_output_aliases={7: 0},
        compiler_params=_params(("parallel", "parallel")),
    )(proj, proj, proj, proj, proj, proj, sinks, ycat)


def _attn_bwd(dycat, proj, o, lse, sinks, ddt, dproj, nbatch, name):
    t = proj.shape[0]
    w = WINDOW
    nb = t // nbatch // w
    scale = ATTN_HEAD_DIM ** -0.5
    q_s, kc_s, kp_s, vc_s, vp_s, z_s, row = _attn_specs(nb, True)

    def body(dy_ref, q_ref, kc_ref, kp_ref, vc_ref, vp_ref, z_ref, o_ref, lse_ref, sk_ref, ddt_ref, _,
             grp_ref, dsk_ref, kcarry, vcarry, sacc):
        b, i = pl.program_id(0), pl.program_id(1)

        @pl.when((b == 0) & (i == 0))
        def _():
            sacc[...] = jnp.zeros_like(sacc)

        @pl.when(i == 0)
        def _():
            kcarry[...] = jnp.zeros_like(kcarry)
            vcarry[...] = jnp.zeros_like(vcarry)

        first = i == nb - 1
        mask = _band_mask(first)
        kk = jnp.concatenate([kp_ref[...], kc_ref[...]], axis=0).astype(MXU_DTYPE)
        vv = jnp.concatenate([vp_ref[...], vc_ref[...]], axis=0).astype(MXU_DTYPE)
        sk = sk_ref[...]
        lse_all = lse_ref[...]
        lane1 = _iota((1, LANES), 1)
        mask2 = jnp.concatenate([mask, mask], axis=0)
        qs, dos, deltas, lses, scores, dps = [], [], [], [], [], []
        for j in range(4):
            cols = slice(j * LANES, (j + 1) * LANES)
            qp, zp, ov, dy = q_ref[:, cols], z_ref[:, cols], o_ref[:, cols], dy_ref[:, cols]
            grp_ref[:, OFF_ZA + j * LANES:OFF_ZA + (j + 1) * LANES] = (dy * ov * _dsilu(zp)).astype(grp_ref.dtype)
            do = dy * _silu(zp)
            q2 = _stack_heads(qp, j // 2).astype(MXU_DTYPE)
            do2 = _stack_heads(do, j // 2)
            qs.append(q2)
            dos.append(do2.astype(MXU_DTYPE))
            deltas.append(jnp.sum(do2 * _stack_heads(ov, j // 2), axis=1, keepdims=True))
            lses.append(jnp.concatenate([_lane_pick(lse_all, 2 * j), _lane_pick(lse_all, 2 * j + 1)], axis=0))
            scores.append(_dot(q2, kk, NT))
            dps.append(_dot(do2, vv, NT))
        prs, dss = [], []
        dsk = jnp.zeros((1, LANES), F32)
        for j in range(4):
            pr = jnp.exp(jnp.where(mask2, scores[j] * scale, -1e30) - lses[j])
            prs.append(pr.astype(MXU_DTYPE))
            dss.append((pr * (dps[j] - deltas[j])).astype(MXU_DTYPE))
            skc = jnp.concatenate([jnp.broadcast_to(_lane_pick(sk, 2 * j), (w, 1)),
                                   jnp.broadcast_to(_lane_pick(sk, 2 * j + 1), (w, 1))], axis=0)
            sink_term = jnp.exp(skc - lses[j]) * deltas[j]
            dsk = dsk - jnp.where(lane1 == 2 * j, jnp.sum(sink_term[:w]), 0.0)
            dsk = dsk - jnp.where(lane1 == 2 * j + 1, jnp.sum(sink_term[w:]), 0.0)
        dkk = jnp.zeros((2 * w, LANES), F32)
        dvv = jnp.zeros((2 * w, LANES), F32)
        for j in range(4):
            dq = _unstack_heads(_dot(dss[j], kk) * scale, j // 2)
            grp_ref[:, OFF_Q + j * LANES:OFF_Q + (j + 1) * LANES] = dq.astype(grp_ref.dtype)
            dkk = dkk + _dot(dss[j], qs[j], TN) * scale
            dvv = dvv + _dot(prs[j], dos[j], TN)
        grp_ref[:, OFF_K:OFF_K + LANES] = (dkk[w:, :] + kcarry[...]).astype(grp_ref.dtype)
        grp_ref[:, OFF_V:OFF_V + LANES] = (dvv[w:, :] + vcarry[...]).astype(grp_ref.dtype)
        grp_ref[:, OFF_DT:OFF_DT + LANES] = ddt_ref[...].astype(grp_ref.dtype)
        grp_ref[:, OFF_DT + LANES:] = jnp.zeros((w, ATTN_GROUP - OFF_DT - LANES), grp_ref.dtype)
        kcarry[...] = dkk[:w, :]
        vcarry[...] = dvv[:w, :]
        sacc[...] += dsk

        @pl.when((b == nbatch - 1) & (i == nb - 1))
        def _():
            dsk_ref[...] = sacc[...]

    return pl.pallas_call(
        body, name=name,
        out_shape=(jax.ShapeDtypeStruct(dproj.shape, dproj.dtype), jax.ShapeDtypeStruct((1, LANES), F32)),
        grid=(nbatch, nb),
        in_specs=[pl.BlockSpec((w, 512), lambda b, i: (row(b, i), YCAT_ATTN // 512)),
                  q_s, kc_s, kp_s, vc_s, vp_s, z_s,
                  pl.BlockSpec((w, 512), lambda b, i: (row(b, i), 0)),
                  pl.BlockSpec((w, LANES), lambda b, i: (row(b, i), 0)),
                  pl.BlockSpec((1, LANES), lambda b, i: (0, 0)),
                  pl.BlockSpec((w, LANES), lambda b, i: (row(b, i), 0)), ANY],
        out_specs=(pl.BlockSpec((w, ATTN_GROUP), lambda b, i: (row(b, i), 0)),
                   pl.BlockSpec((1, LANES), lambda b, i: (0, 0))),
        input_output_aliases={11: 0},
        scratch_shapes=[pltpu.VMEM((w, LANES), F32), pltpu.VMEM((w, LANES), F32),
                        pltpu.VMEM((1, LANES), F32)],
        compiler_params=_params(("arbitrary", "arbitrary")),
    )(dycat, proj, proj, proj, proj, proj, proj, o, lse, sinks, ddt, dproj)


SSD_WIDTH = SSD_HEADS * SSD_HEAD_DIM
GROUP_ROWS = SSD_WIDTH // 2


def _expand_mat():
    r, c = _iota((LANES, SSD_WIDTH), 0), _iota((LANES, SSD_WIDTH), 1)
    return (r == lax.shift_right_logical(c, 6)).astype(BF16)


def _expand_mat_t():
    r, c = _iota((SSD_WIDTH, LANES), 0), _iota((SSD_WIDTH, LANES), 1)
    return (c == lax.shift_right_logical(r, 6)).astype(BF16)


def _ssd_common(u_ref, dt_ref, dtb_ref, a_ref):
    q = CHUNK
    act = _silu(u_ref[...])
    xs = act[:, :SSD_WIDTH]
    bm = act[:, SSD_WIDTH:SSD_WIDTH + 256]
    cm = act[:, SSD_WIDTH + 256:]
    dtp = _softplus(dt_ref[...] + dtb_ref[...])
    a = dtp * a_ref[...]
    tril = (_iota((q, q), 0) >= _iota((q, q), 1)).astype(BF16)
    acs = _xdot_r(tril, a)
    acs_t = acs.T
    e = _expand_mat()
    dt_x = _xdot(dtp, e)
    ea = jnp.exp(_xdot(acs, e))
    a_end = jnp.sum(jnp.where(_iota(acs.shape, 0) == q - 1, acs, 0.0), axis=0, keepdims=True)
    dec = jnp.exp(_xdot(a_end - acs, e))
    a_end_col = jnp.broadcast_to(_lane_pick(acs_t, q - 1), (LANES, LANES))
    s_scale = jnp.exp(_xdot_r(_expand_mat_t(), a_end_col))
    return act, xs, bm, cm, dtp, acs, acs_t, dt_x, ea, dec, s_scale, tril


def _decay_mat(acs, acs_t, h):
    q = CHUNK
    col = _lane_pick(acs, h)
    rowv = jnp.sum(jnp.where(_iota(acs_t.shape, 0) == h, acs_t, 0.0), axis=0, keepdims=True)
    causal = _iota((q, q), 0) >= _iota((q, q), 1)
    return jnp.exp(jnp.where(causal, col - rowv, -1e30))


GN_WIDTH = 512


def _ssd_fwd(u, proj, dtb, a_neg, d_x, norm_w, ycat, nbatch, name):
    t = u.shape[0]
    q = CHUNK
    nc = t // nbatch // q

    def body(u_ref, dt_ref, z_ref, dtb_ref, a_ref, dx_ref, nw_ref, _, y_ref, st_ref, yn_ref, state):
        c = pl.program_id(1)

        @pl.when(c == 0)
        def _():
            state[...] = jnp.zeros_like(state)

        st_ref[...] = state[...]
        act, xs, bm, cm, dtp, acs, acs_t, dt_x, ea, dec, s_scale, _ = _ssd_common(u_ref, dt_ref, dtb_ref, a_ref)
        xdt = xs * dt_x
        xdec = xdt * dec
        lo, hi = _half_mask(0), _half_mask(1)
        grp = []
        for g in range(2):
            bg = bm[:, g * LANES:(g + 1) * LANES]
            cg = cm[:, g * LANES:(g + 1) * LANES]
            rows = slice(g * GROUP_ROWS, (g + 1) * GROUP_ROWS)
            sg = state[rows, :]
            grp.append((_dot(cg, bg, NT), _dot(cg, sg, NT), rows,
                        s_scale[rows, :] * sg + _dot(xdec[:, rows], bg, TN)))
        for g in range(2):
            cb, yoff, rows, state_new = grp[g]
            for j in range(4):
                pj = g * 4 + j
                cols = slice(pj * LANES, (pj + 1) * LANES)
                xp = xdt[:, cols]
                m2 = jnp.concatenate([cb * _decay_mat(acs, acs_t, 2 * pj), cb * _decay_mat(acs, acs_t, 2 * pj + 1)],
                                     axis=1)
                yp = _dot(m2, jnp.concatenate([xp * lo, xp * hi], axis=0))
                yp = yp + yoff[:, j * LANES:(j + 1) * LANES] * ea[:, cols]
                y_ref[:, cols] = yp + dx_ref[:, cols] * xs[:, cols]
            state[rows, :] = state_new
        for g in range(SSD_WIDTH // GN_WIDTH):
            cols = slice(g * GN_WIDTH, (g + 1) * GN_WIDTH)
            gg = y_ref[:, cols] * _silu(z_ref[:, cols])
            rstd = lax.rsqrt(jnp.mean(gg * gg, axis=-1, keepdims=True) + EPS)
            yn_ref[:, cols] = (gg * rstd * nw_ref[:, cols]).astype(yn_ref.dtype)

    vec = pl.BlockSpec((1, LANES), lambda b, c: (0, 0))
    wide = pl.BlockSpec((q, SSD_WIDTH), lambda b, c: (b * nc + c, 0))
    wvec = pl.BlockSpec((1, SSD_WIDTH), lambda b, c: (0, 0))
    return pl.pallas_call(
        body, name=name,
        out_shape=(jax.ShapeDtypeStruct((t, SSD_WIDTH), F32),
                   jax.ShapeDtypeStruct((nbatch * nc * SSD_WIDTH, SSD_STATE), F32),
                   jax.ShapeDtypeStruct(ycat.shape, ycat.dtype)),
        grid=(nbatch, nc),
        in_specs=[pl.BlockSpec((q, SSD_CONV_DIM), lambda b, c: (b * nc + c, 0)),
                  pl.BlockSpec((q, LANES), lambda b, c: (b * nc + c, OFF_DT // LANES)),
                  pl.BlockSpec((q, SSD_WIDTH), lambda b, c: (b * nc + c, OFF_ZS // SSD_WIDTH)),
                  vec, vec, wvec, wvec, ANY],
        out_specs=(wide, pl.BlockSpec((SSD_WIDTH, SSD_STATE), lambda b, c: (b * nc + c, 0)), wide),
        input_output_aliases={7: 2},
        scratch_shapes=[pltpu.VMEM((SSD_WIDTH, SSD_STATE), F32)],
        compiler_params=_params(("parallel", "arbitrary")),
    )(u, proj, proj, dtb, a_neg, d_x, norm_w, ycat)


def _ssd_bwd(dycat, u, proj, y, states, dtb, a_neg, d_x, norm_w, dproj, nbatch, name):
    t = u.shape[0]
    q = CHUNK
    nc = t // nbatch // q

    def body(do_ref, u_ref, dt_ref, z_ref, y_ref, st_ref, dtb_ref, a_ref, dx_ref, nw_ref, _,
             du_ref, dz_ref, ddt_ref, dal_ref, dd_ref, dtbg_ref, dnw_ref, dstate, acc_a, acc_d, acc_b, acc_w):
        b, c = pl.program_id(0), pl.program_id(1)

        @pl.when((b == 0) & (c == 0))
        def _():
            acc_a[...] = jnp.zeros_like(acc_a)
            acc_d[...] = jnp.zeros_like(acc_d)
            acc_b[...] = jnp.zeros_like(acc_b)
            acc_w[...] = jnp.zeros_like(acc_w)

        @pl.when(c == 0)
        def _():
            dstate[...] = jnp.zeros_like(dstate)

        dy_parts = []
        for g in range(SSD_WIDTH // GN_WIDTH):
            cols = slice(g * GN_WIDTH, (g + 1) * GN_WIDTH)
            yv, zv, dov = y_ref[:, cols], z_ref[:, cols], do_ref[:, cols]
            sz = _silu(zv)
            gg = yv * sz
            rstd = lax.rsqrt(jnp.mean(gg * gg, axis=-1, keepdims=True) + EPS)
            gh = gg * rstd
            acc_w[:, cols] += _rowsum8(dov * gh)
            dgn = dov * nw_ref[:, cols]
            dg = rstd * (dgn - gh * jnp.mean(dgn * gh, axis=-1, keepdims=True))
            dy_parts.append(dg * sz)
            dz_ref[:, cols] = (dg * yv * _dsilu(zv)).astype(dz_ref.dtype)

        act, xs, bm, cm, dtp, acs, acs_t, dt_x, ea, dec, s_scale, tril = _ssd_common(
            u_ref, dt_ref, dtb_ref, a_ref)
        xdt = xs * dt_x
        xdec = xdt * dec
        dyv = jnp.concatenate(dy_parts, axis=1)
        dye = dyv * ea
        lo, hi = _half_mask(0), _half_mask(1)
        et = _expand_mat_t()
        grp = []
        for g in range(2):
            rows = slice(g * GROUP_ROWS, (g + 1) * GROUP_ROWS)
            bg = bm[:, g * LANES:(g + 1) * LANES]
            cg = cm[:, g * LANES:(g + 1) * LANES]
            sg = st_ref[rows, :]
            dsg = dstate[rows, :]
            grp.append(dict(
                rows=rows, bg=bg, cg=cg, dsg=dsg,
                cb=_dot(cg, bg, NT), yoff=_dot(cg, sg, NT), dxst=_dot(bg, dsg, NT) * dec[:, rows],
                dc_off=_dot(dye[:, rows], sg), db_off=_dot(xdec[:, rows], dsg),
                s_next=s_scale[rows, :] * sg + _dot(xdec[:, rows], bg, TN),
                dstate_new=_dot(dye[:, rows], cg, TN) + s_scale[rows, :] * dsg))
        dy2s, g2s, l2s = [], [], []
        for pj in range(SSD_HEADS // 2):
            cols = slice(pj * LANES, (pj + 1) * LANES)
            dyp = dyv[:, cols]
            dy2 = jnp.concatenate([dyp * lo, dyp * hi], axis=0).astype(MXU_DTYPE)
            dy2s.append(dy2)
            g2s.append(_dot(dy2, xdt[:, cols], NT))
            l2s.append(jnp.concatenate([_decay_mat(acs, acs_t, 2 * pj), _decay_mat(acs, acs_t, 2 * pj + 1)], axis=0))
        dal_diag = jnp.zeros((q, LANES), F32)
        lane2 = _iota((2 * q, LANES), 1)
        row2 = _iota((2 * q, LANES), 0)
        dxdt_parts, db_parts, dc_parts = [], [], []
        end_sum = jnp.zeros((LANES, LANES), F32)
        for g in range(2):
            gd = grp[g]
            cb2 = jnp.concatenate([gd["cb"], gd["cb"]], axis=0)
            dcb = jnp.zeros((q, q), F32)
            parts = []
            for j in range(4):
                pj = g * 4 + j
                gl = g2s[pj] * l2s[pj]
                dcb = dcb + gl[:q] + gl[q:]
                m2 = cb2 * l2s[pj]
                parts.append(_dot(m2, dy2s[pj], TN))
                w2 = (gl * cb2).astype(MXU_DTYPE)
                sel2 = (lane2 == 2 * pj + (row2 >= q).astype(jnp.int32)).astype(MXU_DTYPE)
                dal_diag = dal_diag + _dot(jnp.concatenate([w2[:q], w2[q:]], axis=1), sel2) - _dot(w2, sel2, TN)
            dxdt_parts.append(jnp.concatenate(parts, axis=1) + gd["dxst"])
            dc_parts.append(_dot(dcb, gd["bg"]) + gd["dc_off"])
            db_parts.append(_dot(dcb, gd["cg"], TN) + gd["db_off"])
            end_sum = end_sum + _xdot(gd["dsg"] * gd["s_next"], et[gd["rows"], :], TN, passes=2)
            dstate[gd["rows"], :] = gd["dstate_new"]
        dxst_parts = [gd["dxst"] for gd in grp]
        yoff_parts = [gd["yoff"] for gd in grp]
        dxdt = jnp.concatenate(dxdt_parts, axis=1)
        dxv = dx_ref[...]
        yoff = jnp.concatenate(yoff_parts, axis=1) * ea
        dalpha = dal_diag + _xdot(dyv * yoff - xdt * jnp.concatenate(dxst_parts, axis=1), et)
        end_row = jnp.sum(end_sum, axis=0, keepdims=True)
        dalpha = dalpha + jnp.where(_iota((q, LANES), 0) == q - 1, end_row, 0.0)
        da = _xdot_r(tril, dalpha, TN)
        ddtp = da * a_ref[...] + _xdot(dxdt * xs, et)
        acc_a[...] += _rowsum8(da * dtp)
        acc_d[...] += _rowsum8(_xdot(dyv * xs, et))
        ddt_raw = ddtp * _sigmoid(dt_ref[...] + dtb_ref[...])
        acc_b[...] += _rowsum8(ddt_raw)
        ddt_ref[...] = ddt_raw
        dxs = dxdt * dt_x + dxv * dyv
        dact = jnp.concatenate([dxs] + db_parts + dc_parts, axis=1)
        du_ref[...] = dact * _dsilu(u_ref[...])

        @pl.when((b == nbatch - 1) & (c == nc - 1))
        def _():
            dal_ref[...] = jnp.sum(acc_a[...], axis=0, keepdims=True) * a_ref[...]
            dd_ref[...] = jnp.sum(acc_d[...], axis=0, keepdims=True)
            dtbg_ref[...] = jnp.sum(acc_b[...], axis=0, keepdims=True)
            dnw_ref[...] = jnp.sum(acc_w[...], axis=0, keepdims=True)

    def rowblk(b, c):
        return b * nc + (nc - 1 - c)

    vec = pl.BlockSpec((1, LANES), lambda b, c: (0, 0))
    wvec = pl.BlockSpec((1, SSD_WIDTH), lambda b, c: (0, 0))
    wide = pl.BlockSpec((q, SSD_WIDTH), lambda b, c: (rowblk(b, c), 0))
    zblk = pl.BlockSpec((q, SSD_WIDTH), lambda b, c: (rowblk(b, c), OFF_ZS // SSD_WIDTH))
    return pl.pallas_call(
        body, name=name,
        out_shape=(jax.ShapeDtypeStruct((t, SSD_CONV_DIM), F32), jax.ShapeDtypeStruct(dproj.shape, dproj.dtype),
                   jax.ShapeDtypeStruct((t, LANES), F32),
                   jax.ShapeDtypeStruct((1, LANES), F32), jax.ShapeDtypeStruct((1, LANES), F32),
                   jax.ShapeDtypeStruct((1, LANES), F32), jax.ShapeDtypeStruct((1, SSD_WIDTH), F32)),
        grid=(nbatch, nc),
        in_specs=[wide,
                  pl.BlockSpec((q, SSD_CONV_DIM), lambda b, c: (rowblk(b, c), 0)),
                  pl.BlockSpec((q, LANES), lambda b, c: (rowblk(b, c), OFF_DT // LANES)),
                  zblk, wide,
                  pl.BlockSpec((SSD_WIDTH, SSD_STATE), lambda b, c: (rowblk(b, c), 0)),
                  vec, vec, wvec, wvec, ANY],
        out_specs=(pl.BlockSpec((q, SSD_CONV_DIM), lambda b, c: (rowblk(b, c), 0)),
                   zblk,
                   pl.BlockSpec((q, LANES), lambda b, c: (rowblk(b, c), 0)),
                   vec, vec, vec, wvec),
        input_output_aliases={10: 1},
        scratch_shapes=[pltpu.VMEM((SSD_WIDTH, SSD_STATE), F32), pltpu.VMEM((SUBLANES, LANES), F32),
                        pltpu.VMEM((SUBLANES, LANES), F32), pltpu.VMEM((SUBLANES, LANES), F32),
                        pltpu.VMEM((SUBLANES, SSD_WIDTH), F32)],
        compiler_params=_params(("arbitrary", "arbitrary")),
    )(dycat, u, proj, proj, y, states, dtb, a_neg, d_x, norm_w, dproj)


def _pad_rows(w, rows):
    return jnp.concatenate([w, jnp.zeros((rows - w.shape[0], w.shape[1]), w.dtype)], axis=0)


def _pad_lanes(v):
    return jnp.concatenate([v, jnp.zeros((LANES - v.shape[0],), v.dtype)]).reshape(1, LANES)


def _padded_from_chips(pieces):
    cols = pieces[0].shape[-1]
    lead = pieces[0].shape[:-1]
    parts, pos = [], 0
    for lo, hi, start in sorted(SECTIONS, key=lambda s: s[2]):
        if start > pos:
            parts.append(jnp.zeros(lead + (start - pos,), pieces[0].dtype))
        pos = start + hi - lo
        while lo < hi:
            p = lo // cols
            end = min(hi, (p + 1) * cols)
            parts.append(pieces[p][..., lo - p * cols:end - p * cols])
            lo = end
    if pos < NP:
        parts.append(jnp.zeros(lead + (NP - pos,), pieces[0].dtype))
    return jnp.concatenate(parts, axis=-1)


def _chip_part_from_padded(wp, p, cols):
    lo, hi = p * cols, (p + 1) * cols
    parts = []
    for rs, re, start in SECTIONS:
        a, b = max(lo, rs), min(hi, re)
        if a < b:
            parts.append(wp[..., start + a - rs:start + b - rs])
    return jnp.concatenate(parts, axis=-1)


def _layer_params(li, w_in_p, w_out, conv_w, dw_w, small):
    return dict(
        w_in_p=w_in_p, w_out=w_out,
        conv_w=_pad_rows(conv_w, SUBLANES), dw_w=_pad_rows(dw_w, 32),
        norm_w=small["norm_w"][li].reshape(1, -1),
        conv_b=small["ssd_conv_b"][li].reshape(1, -1),
        dtb=_pad_lanes(small["ssd_dt_bias"][li]),
        a_neg=_pad_lanes(-jnp.exp(small["ssd_a_log"][li])),
        d_x=jnp.repeat(small["ssd_d"][li], SSD_HEAD_DIM).reshape(1, -1),
        ssd_norm_w=small["ssd_norm_w"][li].reshape(1, -1),
        sinks=_pad_lanes(small["attn_sinks"][li]),
        dw_b=small["conf_dw_b"][li].reshape(1, -1),
        ln_w=small["conf_ln_w"][li].reshape(1, -1),
        ln_b=small["conf_ln_b"][li].reshape(1, -1),
    )


def _layer_fwd(x, p, nbatch, seq, tag, after=None):
    proj, h_t = _proj_fwd(x, p["norm_w"], p["w_in_p"], name=f"proj_fwd_{tag}", after=after)
    u = _conv_fwd(proj, OFF_XBC, SSD_CONV_DIM, p["conv_w"], p["conv_b"], SSD_CONV, seq, name=f"ssd_conv_fwd_{tag}")
    ycat = lax.empty((x.shape[0], MIX_WIDTH), MXU_DTYPE)
    y, states, ycat = _ssd_fwd(u, proj, p["dtb"], p["a_neg"], p["d_x"], p["ssd_norm_w"], ycat, nbatch,
                               name=f"ssd_fwd_{tag}")
    ycat, o, lse = _attn_fwd(proj, p["sinks"], ycat, nbatch, name=f"attn_fwd_{tag}")
    c1, ycat = _conf_fwd(proj, p["dw_w"], p["dw_b"], p["ln_w"], p["ln_b"], ycat, seq, name=f"conf_fwd_{tag}")
    w_out = p["w_out"](ycat) if callable(p["w_out"]) else p["w_out"]
    x_new = _matmul(ycat, w_out, "nn", F32, 1024, 512, 2048, name=f"out_fwd_{tag}", residual=x)
    return x_new, dict(x=x, w_out=w_out, h_t=h_t, proj=proj, u=u, y=y, states=states, o=o, lse=lse, c1=c1, ycat=ycat)


def _layer_bwd(dx_out, p, s, nbatch, seq, tag, hooks=None):
    hooks = hooks or {}
    proj = s["proj"]
    dycat = _matmul(dx_out, s["w_out"], "nt", F32, 1024, 1024, 1024, name=f"out_bwd_dy_{tag}",
                    after=hooks.get("start_token"))
    dw_out = _matmul(s["ycat"], dx_out, "tn", F32, 1024, 1024, 1024, name=f"out_bwd_dw_{tag}")
    token = hooks["after_dycat"](dycat) if "after_dycat" in hooks else None
    dtb = p["dtb"] if token is None else p["dtb"] + token[0, 0]
    dproj = lax.empty(proj.shape, MXU_DTYPE)
    du, dproj, ddt, da_log, dd, ddtb, dssd_norm_w = _ssd_bwd(
        dycat, s["u"], proj, s["y"], s["states"], dtb, p["a_neg"], p["d_x"], p["ssd_norm_w"], dproj,
        nbatch, name=f"ssd_bwd_{tag}")
    dproj, dconv_w, dconv_b = _conv_bwd(du, proj, OFF_XBC, SSD_CONV_DIM, p["conv_w"], SSD_CONV, seq,
                                        name=f"ssd_conv_bwd_{tag}", into=dproj)
    dproj, dsinks = _attn_bwd(dycat, proj, s["o"], s["lse"], p["sinks"], ddt, dproj, nbatch,
                              name=f"attn_bwd_{tag}")
    if "after_attn" in hooks:
        hooks["after_attn"](dproj)
    dproj, ddw_w, ddw_b, dln_w, dln_b = _conf_bwd(dycat, proj, s["c1"], p["dw_w"], p["ln_w"], p["ln_b"], dproj, seq,
                                                  name=f"conf_bwd_{tag}")
    dw_in_p = _matmul(s["h_t"], dproj, "nn", F32, 1024, 512, 4096, name=f"proj_bwd_dw_{tag}")
    dx_in, dnorm_w = _proj_bwd_dx(dproj, p["w_in_p"], s["x"], p["norm_w"], dx_out, name=f"proj_bwd_dx_{tag}")
    grads = dict(
        norm_w=dnorm_w[0], w_in_p=dw_in_p, ssd_conv_w=dconv_w[:SSD_CONV], ssd_conv_b=dconv_b[0],
        ssd_dt_bias=ddtb[0, :SSD_HEADS], ssd_a_log=da_log[0, :SSD_HEADS], ssd_d=dd[0, :SSD_HEADS],
        ssd_norm_w=dssd_norm_w[0], attn_sinks=dsinks[0, :ATTN_Q_HEADS], conf_dw_w=ddw_w[:CONF_KERNEL],
        conf_dw_b=ddw_b[0], conf_ln_w=dln_w[0], conf_ln_b=dln_b[0], w_out=dw_out)
    return dx_in, grads


def _local_step(x, target, param_fns, final_norm_w, first_after=None, on_grads=None):
    nbatch, seq, d = x.shape
    xt = x.reshape(nbatch * seq, d)
    saved, layer_params = [], []
    for li, fn in enumerate(param_fns):
        p = fn(xt)
        layer_params.append(p)
        xt, s = _layer_fwd(xt, p, nbatch, seq, f"l{li}", after=first_after if li == 0 else None)
        saved.append(s)
    loss, dx, dfinal = _loss_head(xt, target.reshape(nbatch * seq, d), final_norm_w.reshape(1, d), name="loss_head")
    grads = [None] * len(layer_params)
    hooks = None
    for li in reversed(range(len(layer_params))):
        dx, grads[li] = _layer_bwd(dx, layer_params[li], saved[li], nbatch, seq, f"l{li}", hooks=hooks)
        hooks = on_grads(li, grads[li]) if on_grads is not None else None
    return loss[0, 0], dx.reshape(nbatch, seq, d), grads, dfinal[0]


MESH = pl.DeviceIdType.MESH
N_CHIPS = 4


def _mesh_pos():
    return lax.axis_index("x"), lax.axis_index("y"), lax.axis_index("c")


def _other_chips(x, y):
    return [(1 - x, y), (x, 1 - y), (1 - x, 1 - y)]


def _gather_weights(big, small, name):
    nbig, nsmall = len(big), len(small)
    n_ici = 3 * (nbig + nsmall)
    n_fwd = 3 * nbig

    def body(*refs):
        ins = refs[:nbig + nsmall]
        outs = refs[nbig + nsmall:2 * (nbig + nsmall)]
        send_sems, recv_sems = refs[2 * (nbig + nsmall):]
        x, y, c = _mesh_pos()
        me = 2 * x + y
        sibling = (x, y, 1 - c)
        chips = _other_chips(x, y)

        def ici(a, j, origin, dest):
            if a < nbig:
                src = ins[a].at[c] if origin is None else outs[a].at[origin, c]
                dst = outs[a].at[me if origin is None else origin, c]
            else:
                src = ins[a] if origin is None else outs[a].at[origin]
                dst = outs[a].at[me if origin is None else origin]
            k = a * 3 + j
            return pltpu.make_async_remote_copy(src_ref=src, dst_ref=dst, send_sem=send_sems.at[k],
                                                recv_sem=recv_sems.at[k], device_id=dest, device_id_type=MESH)

        def fwd(a, j, origin, half):
            k = n_ici + a * 3 + j
            ref = outs[a].at[origin, half]
            return pltpu.make_async_remote_copy(src_ref=ref, dst_ref=ref, send_sem=send_sems.at[k],
                                                recv_sem=recv_sems.at[k], device_id=sibling, device_id_type=MESH)

        sends = []
        for j, (px, py) in enumerate(chips):
            for a in range(nbig + nsmall):
                cp = ici(a, j, None, (px, py, c))
                cp.start()
                sends.append(cp)
        for j, (px, py) in enumerate(chips):
            origin = 2 * px + py
            for a in range(nbig):
                ici(a, j, origin, (px, py, c)).wait_recv()
                cp = fwd(a, j, origin, c)
                cp.start()
                sends.append(cp)
        for j, (px, py) in enumerate(chips):
            origin = 2 * px + py
            for a in range(nbig, nbig + nsmall):
                ici(a, j, origin, (px, py, c)).wait_recv()
            for a in range(nbig):
                fwd(a, j, origin, 1 - c).wait_recv()
        for cp in sends:
            cp.wait_send()

    out_shape = tuple(jax.ShapeDtypeStruct((N_CHIPS,) + a.shape, a.dtype) for a in list(big) + list(small))
    return pl.pallas_call(
        body, name=name, out_shape=out_shape,
        in_specs=[ANY] * (nbig + nsmall), out_specs=tuple([ANY] * (nbig + nsmall)),
        scratch_shapes=[pltpu.SemaphoreType.DMA((n_ici + n_fwd,)), pltpu.SemaphoreType.DMA((n_ici + n_fwd,))],
    )(*big, *small)


def _pair_swap_halves(arrs, name):
    n = len(arrs)

    def body(*refs):
        ins, outs = refs[:n], refs[n:2 * n]
        send_sems, recv_sems = refs[2 * n:]
        x, y, c = _mesh_pos()
        cps = [pltpu.make_async_remote_copy(src_ref=ins[a].at[:, 1 - c], dst_ref=outs[a], send_sem=send_sems.at[a],
                                            recv_sem=recv_sems.at[a], device_id=(x, y, 1 - c), device_id_type=MESH)
               for a in range(n)]
        for cp in cps:
            cp.start()
        for cp in cps:
            cp.wait()

    return pl.pallas_call(
        body, name=name,
        out_shape=tuple(jax.ShapeDtypeStruct(a.shape[:1] + a.shape[2:], a.dtype) for a in arrs),
        in_specs=[ANY] * n, out_specs=tuple([ANY] * n),
        scratch_shapes=[pltpu.SemaphoreType.DMA((n,)), pltpu.SemaphoreType.DMA((n,))],
    )(*arrs)


HBM = pl.BlockSpec(memory_space=pltpu.HBM)
SEM = pl.BlockSpec(memory_space=pltpu.SEMAPHORE)
DATAFLOW = pltpu.SideEffectType.DATAFLOW_SIDE_EFFECTING


def _split_peers(pattern, x, y, c):
    if pattern == "swap":
        return [((x, y, 1 - c), 1 - c, None, None)]
    me = 2 * x + y
    return [((px, py, c), 2 * px + py if pattern == "scatter" else None, me, 2 * px + py)
            for px, py in _other_chips(x, y)]


def _split_land_shape(pattern, shape):
    return {"bcast": (N_CHIPS,) + shape, "scatter": shape, "swap": shape[:1] + shape[2:]}[pattern]


def _split_copies(pattern, srcs, lands, send_sems, recv_sems, waiting):
    x, y, c = _mesh_pos()
    peers = _split_peers(pattern, x, y, c)
    cps = []
    for j, (dev, src_slot, dst_slot, my_slot) in enumerate(peers):
        for a in range(len(srcs)):
            if src_slot is None:
                src = srcs[a]
            else:
                src = srcs[a].at[:, src_slot] if pattern == "swap" else srcs[a].at[src_slot]
            slot = my_slot if waiting else dst_slot
            dst = lands[a] if slot is None else lands[a].at[slot]
            k = a * len(peers) + j
            cps.append(pltpu.make_async_remote_copy(src_ref=src, dst_ref=dst, send_sem=send_sems[k],
                                                    recv_sem=recv_sems[k], device_id=dev, device_id_type=MESH))
    return cps


def _split_start(arrs, pattern, after, name):
    n = len(arrs)
    nsem = n * (1 if pattern == "swap" else N_CHIPS - 1)

    def body(*refs):
        srcs, lands = refs[:n], refs[n:2 * n]
        outs = refs[2 * n + 1:]
        for cp in _split_copies(pattern, srcs, lands, outs[:nsem], outs[nsem:2 * nsem], waiting=False):
            cp.start()
        outs[-1][...] = jnp.zeros_like(outs[-1])

    lands = [lax.empty(_split_land_shape(pattern, a.shape), a.dtype) for a in arrs]
    out_shape = ([pltpu.SemaphoreType.DMA(())] * (2 * nsem)
                 + [pltpu.HBM(a.shape, a.dtype) for a in arrs] + [pltpu.HBM(b.shape, b.dtype) for b in lands]
                 + [jax.ShapeDtypeStruct((SUBLANES, LANES), F32)])
    outs = pl.pallas_call(
        body, name=name, out_shape=tuple(out_shape),
        in_specs=[HBM] * (2 * n) + [ANY],
        out_specs=tuple([SEM] * (2 * nsem) + [HBM] * (2 * n) + [pl.BlockSpec(memory_space=pltpu.VMEM)]),
        input_output_aliases={a: 2 * nsem + a for a in range(2 * n)},
        compiler_params=pltpu.CompilerParams(has_side_effects=DATAFLOW),
    )(*[pltpu.with_memory_space_constraint(a, pltpu.HBM) for a in list(arrs) + lands], after)
    return outs[:-1], outs[-1]


def _split_wait(state, n, pattern, after, name):
    nsem = n * (1 if pattern == "swap" else N_CHIPS - 1)

    def body(*refs):
        srcs, lands = refs[:n], refs[n:2 * n]
        send_sems, recv_sems = refs[2 * n:2 * n + nsem], refs[2 * n + nsem:2 * n + 2 * nsem]
        for cp in _split_copies(pattern, srcs, lands, send_sems, recv_sems, waiting=True):
            cp.wait_send()
            cp.wait_recv()

    sems, thru = state[:2 * nsem], state[2 * nsem:]
    outs = pl.pallas_call(
        body, name=name, out_shape=tuple(pltpu.HBM(a.shape, a.dtype) for a in thru),
        in_specs=[HBM] * (2 * n) + [SEM] * (2 * nsem) + [ANY],
        out_specs=tuple([HBM] * (2 * n)),
        input_output_aliases={a: a for a in range(2 * n)},
        compiler_params=pltpu.CompilerParams(has_side_effects=DATAFLOW),
    )(*thru, *sems, after)
    return outs[n:]


def _pair_gather(arrs, layer, name):
    n = len(arrs)

    def body(*refs):
        outs = refs[n:2 * n]
        send_sems, recv_sems = refs[2 * n:]
        x, y, c = _mesh_pos()
        cps = [pltpu.make_async_remote_copy(src_ref=outs[a].at[layer, c], dst_ref=outs[a].at[layer, c],
                                            send_sem=send_sems.at[a], recv_sem=recv_sems.at[a],
                                            device_id=(x, y, 1 - c), device_id_type=MESH)
               for a in range(n)]
        for cp in cps:
            cp.start()
        for cp in cps:
            cp.wait()

    return pl.pallas_call(
        body, name=name, out_shape=tuple(jax.ShapeDtypeStruct(a.shape, a.dtype) for a in arrs),
        in_specs=[ANY] * n, out_specs=tuple([ANY] * n),
        input_output_aliases={a: a for a in range(n)},
        scratch_shapes=[pltpu.SemaphoreType.DMA((n,)), pltpu.SemaphoreType.DMA((n,))],
    )(*arrs)


N_DEV = 8


def _allreduce_small(pack, name):
    r = pack.shape[0]

    def body(p_ref, o_ref, land, send_sems, recv_sems):
        x, y, c = _mesh_pos()
        me = 4 * x + 2 * y + c
        cps = []
        for k in range(1, N_DEV):
            peer = (x ^ (k >> 2), y ^ ((k >> 1) & 1), c ^ (k & 1))
            cps.append(pltpu.make_async_remote_copy(src_ref=p_ref, dst_ref=land.at[me], send_sem=send_sems.at[k - 1],
                                                    recv_sem=recv_sems.at[k - 1], device_id=peer, device_id_type=MESH))
        for cp in cps:
            cp.start()
        land[me] = p_ref[...]
        for cp in cps:
            cp.wait()
        total = land[0]
        for d in range(1, N_DEV):
            total = total + land[d]
        o_ref[...] = total

    vm = pl.BlockSpec(memory_space=pltpu.VMEM)
    return pl.pallas_call(
        body, name=name, out_shape=jax.ShapeDtypeStruct(pack.shape, F32),
        in_specs=[vm], out_specs=vm,
        scratch_shapes=[pltpu.VMEM((N_DEV, r, LANES), F32), pltpu.SemaphoreType.DMA((N_DEV - 1,)),
                        pltpu.SemaphoreType.DMA((N_DEV - 1,))],
    )(pack)


BIG_ROWS = 128


def _cast_layer(w, layer, name):
    _, r, cdim = w.shape
    tr = BIG_ROWS

    def body(w_ref, o_ref):
        o_ref[...] = w_ref[...].astype(o_ref.dtype)

    return pl.pallas_call(
        body, name=name, out_shape=jax.ShapeDtypeStruct((r, cdim), MXU_DTYPE),
        grid=(r // tr,), in_specs=[pl.BlockSpec((None, tr, cdim), lambda i: (layer, i, 0))],
        out_specs=pl.BlockSpec((tr, cdim), lambda i: (i, 0)),
        compiler_params=_params(("parallel",)),
    )(w)


def _cast_cols_major(w_t, name):
    cdim, nl, r = w_t.shape
    tc = LANES

    def body(w_ref, *o_refs):
        for l in range(nl):
            o_refs[l][...] = w_ref[:, l, :].T.astype(o_refs[l].dtype)

    out = pl.BlockSpec((r, tc), lambda i: (0, i))
    return pl.pallas_call(
        body, name=name, out_shape=tuple(jax.ShapeDtypeStruct((r, cdim), MXU_DTYPE) for _ in range(nl)),
        grid=(pl.cdiv(cdim, tc),), in_specs=[pl.BlockSpec((tc, nl, r), lambda i: (i, 0, 0))],
        out_specs=tuple([out] * nl),
        compiler_params=_params(("parallel",)),
    )(w_t)


def _pair_sum(parts, sib, which, out_dtype, name):
    k, _, r, cdim = parts.shape
    tr = BIG_ROWS

    def body(sel_ref, p_ref, s_ref, o_ref):
        o_ref[...] = (p_ref[...] + s_ref[...]).astype(o_ref.dtype)

    grid_spec = pltpu.PrefetchScalarGridSpec(
        num_scalar_prefetch=1, grid=(k, r // tr),
        in_specs=[pl.BlockSpec((None, None, tr, cdim), lambda l, i, sel: (l, sel[0], i, 0)),
                  pl.BlockSpec((None, tr, cdim), lambda l, i, sel: (l, i, 0))],
        out_specs=pl.BlockSpec((None, tr, cdim), lambda l, i, sel: (l, i, 0)))
    return pl.pallas_call(
        body, name=name, out_shape=jax.ShapeDtypeStruct((k, r, cdim), out_dtype), grid_spec=grid_spec,
        compiler_params=_params(("parallel", "parallel")),
    )(which.reshape(1).astype(jnp.int32), parts, sib)


def _sum_lead(parts, into, layer, which, name):
    k, r, cdim = parts.shape
    tr = BIG_ROWS

    def body(sel_ref, p_ref, _, o_ref):
        total = p_ref[0].astype(F32)
        for a in range(1, k):
            total = total + p_ref[a].astype(F32)
        o_ref[...] = total

    grid_spec = pltpu.PrefetchScalarGridSpec(
        num_scalar_prefetch=1, grid=(r // tr,),
        in_specs=[pl.BlockSpec((k, tr, cdim), lambda i, sel: (0, i, 0)), ANY],
        out_specs=pl.BlockSpec((None, None, tr, cdim), lambda i, sel: (layer, sel[0], i, 0)))
    return pl.pallas_call(
        body, name=name, out_shape=jax.ShapeDtypeStruct(into.shape, F32), grid_spec=grid_spec,
        input_output_aliases={2: 0},
        compiler_params=_params(("parallel",)),
    )(which.reshape(1).astype(jnp.int32), parts, into)


def _adam_math(w, g, m, v):
    m2 = ADAM_B1 * m + (1.0 - ADAM_B1) * g
    v2 = ADAM_B2 * v + (1.0 - ADAM_B2) * (g * g)
    m_hat = m2 / (1.0 - ADAM_B1 ** ADAM_STEP)
    v_hat = v2 / (1.0 - ADAM_B2 ** ADAM_STEP)
    delta = -ADAM_LR * (m_hat / (jnp.sqrt(v_hat) + ADAM_EPS) + ADAM_WD * w)
    return delta, m2, v2


def _adam_big(w, g, m, v, name):
    nl, r, cdim = w.shape
    tr = BIG_ROWS

    def body(w_ref, g_ref, m_ref, v_ref, d_ref, mo_ref, vo_ref):
        delta, m2, v2 = _adam_math(w_ref[...], g_ref[...], m_ref[...], v_ref[...])
        d_ref[...] = delta
        mo_ref[...] = m2
        vo_ref[...] = v2

    blk = pl.BlockSpec((None, tr, cdim), lambda l, i: (l, i, 0))
    shp = jax.ShapeDtypeStruct(w.shape, F32)
    return pl.pallas_call(
        body, name=name, out_shape=(shp, shp, shp),
        grid=(nl, r // tr), in_specs=[blk] * 4, out_specs=(blk, blk, blk),
        compiler_params=_params(("parallel", "parallel")),
    )(w, g, m, v)


def _adam_cols_major(w, g, m, v, name):
    cdim, nl, r = w.shape
    tc = BIG_ROWS

    def body(w_ref, g_ref, m_ref, v_ref, d_ref, mo_ref, vo_ref):
        delta, m2, v2 = _adam_math(w_ref[...], g_ref[...], m_ref[...], v_ref[...])
        d_ref[...] = delta
        mo_ref[...] = m2
        vo_ref[...] = v2

    blk = pl.BlockSpec((tc, nl, r), lambda i: (i, 0, 0))
    shp = jax.ShapeDtypeStruct(w.shape, F32)
    return pl.pallas_call(
        body, name=name, out_shape=(shp, shp, shp),
        grid=(pl.cdiv(cdim, tc),), in_specs=[blk] * 4, out_specs=(blk, blk, blk),
        compiler_params=_params(("parallel",)),
    )(w, g, m, v)


def _adam_small(ws, gs, ms, vs, name):
    n = len(ws)

    def body(*refs):
        w_refs, g_refs, m_refs, v_refs = (refs[k * n:(k + 1) * n] for k in range(4))
        d_refs, mo_refs, vo_refs = (refs[(4 + k) * n:(5 + k) * n] for k in range(3))
        for a in range(n):
            delta, m2, v2 = _adam_math(w_refs[a][...], g_refs[a][...], m_refs[a][...], v_refs[a][...])
            d_refs[a][...] = delta
            mo_refs[a][...] = m2
            vo_refs[a][...] = v2

    shapes = tuple(jax.ShapeDtypeStruct(w.shape, F32) for w in ws)
    vm = pl.BlockSpec(memory_space=pltpu.VMEM)
    outs = pl.pallas_call(body, name=name, out_shape=shapes * 3, in_specs=[vm] * (4 * n),
                          out_specs=tuple([vm] * (3 * n)))(*ws, *gs, *ms, *vs)
    return outs[:n], outs[n:2 * n], outs[2 * n:]


PACK_TILE = SUBLANES * LANES


def _pack(arrays):
    rows = []
    for a in arrays:
        flat = a.reshape(-1)
        pad = (-flat.shape[0]) % PACK_TILE
        if pad:
            flat = jnp.concatenate([flat, jnp.zeros((pad,), flat.dtype)])
        rows.append(flat.reshape(-1, LANES))
    return jnp.concatenate(rows, axis=0)


def _unpack(pack, shapes):
    outs, row = [], 0
    for shp in shapes:
        n = int(np.prod(shp))
        nrows = -(-n // PACK_TILE) * SUBLANES
        outs.append(pack[row:row + nrows].reshape(-1)[:n].reshape(shp))
        row += nrows
    return outs


SMALL = ["norm_w", "ssd_conv_b", "ssd_dt_bias", "ssd_a_log", "ssd_d", "ssd_norm_w", "attn_sinks",
         "conf_dw_b", "conf_ln_w", "conf_ln_b"]
WEIGHTS = ["norm_w", "w_in", "ssd_conv_w", "ssd_conv_b", "ssd_dt_bias", "ssd_a_log", "ssd_d", "ssd_norm_w",
           "attn_sinks", "conf_dw_w", "conf_dw_b", "conf_ln_w", "conf_ln_b", "w_out", "final_norm_w"]


def kernel(x, norm_w, w_in, ssd_conv_w, ssd_conv_b, ssd_dt_bias, ssd_a_log, ssd_d, ssd_norm_w, attn_sinks, conf_dw_w, conf_dw_b, conf_ln_w, conf_ln_b, w_out, final_norm_w, loss_target, m_norm_w, m_w_in, m_ssd_conv_w, m_ssd_conv_b, m_ssd_dt_bias, m_ssd_a_log, m_ssd_d, m_ssd_norm_w, m_attn_sinks, m_conf_dw_w, m_conf_dw_b, m_conf_ln_w, m_conf_ln_b, m_w_out, m_final_norm_w, v_norm_w, v_w_in, v_ssd_conv_w, v_ssd_conv_b, v_ssd_dt_bias, v_ssd_a_log, v_ssd_d, v_ssd_norm_w, v_attn_sinks, v_conf_dw_w, v_conf_dw_b, v_conf_ln_w, v_conf_ln_b, v_w_out, v_final_norm_w):
    w = dict(norm_w=norm_w, w_in=w_in, ssd_conv_w=ssd_conv_w, ssd_conv_b=ssd_conv_b, ssd_dt_bias=ssd_dt_bias,
             ssd_a_log=ssd_a_log, ssd_d=ssd_d, ssd_norm_w=ssd_norm_w, attn_sinks=attn_sinks, conf_dw_w=conf_dw_w,
             conf_dw_b=conf_dw_b, conf_ln_w=conf_ln_w, conf_ln_b=conf_ln_b, w_out=w_out, final_norm_w=final_norm_w)
    m = dict(norm_w=m_norm_w, w_in=m_w_in, ssd_conv_w=m_ssd_conv_w, ssd_conv_b=m_ssd_conv_b,
             ssd_dt_bias=m_ssd_dt_bias, ssd_a_log=m_ssd_a_log, ssd_d=m_ssd_d, ssd_norm_w=m_ssd_norm_w,
             attn_sinks=m_attn_sinks, conf_dw_w=m_conf_dw_w, conf_dw_b=m_conf_dw_b, conf_ln_w=m_conf_ln_w,
             conf_ln_b=m_conf_ln_b, w_out=m_w_out, final_norm_w=m_final_norm_w)
    v = dict(norm_w=v_norm_w, w_in=v_w_in, ssd_conv_w=v_ssd_conv_w, ssd_conv_b=v_ssd_conv_b,
             ssd_dt_bias=v_ssd_dt_bias, ssd_a_log=v_ssd_a_log, ssd_d=v_ssd_d, ssd_norm_w=v_ssd_norm_w,
             attn_sinks=v_attn_sinks, conf_dw_w=v_conf_dw_w, conf_dw_b=v_conf_dw_b, conf_ln_w=v_conf_ln_w,
             conf_ln_b=v_conf_ln_b, w_out=v_w_out, final_norm_w=v_final_norm_w)
    depth = w_in.shape[0]
    me = 2 * lax.axis_index("x") + lax.axis_index("y")

    assert depth == 2
    w_in_t = jnp.transpose(w_in, (2, 0, 1))
    w_in_b = _cast_cols_major(w_in_t, name="cast_w_in")
    w_out_b = [_cast_layer(w_out, li, name=f"cast_w_out_l{li}") for li in range(depth)]
    own0 = [w_in_b[0].reshape((2, -1) + w_in_b[0].shape[1:]), ssd_conv_w, conf_dw_w]
    gathered0 = _gather_weights(own0[:1], own0[1:], name="gather_weights_l0")
    g_in0, g_conv, g_dw = [lax.dynamic_update_index_in_dim(g_all, mine, me, 0)
                           for g_all, mine in zip(gathered0, own0)]
    own1 = [w_out_b[0], w_in_b[1], w_out_b[1]]
    pending1, token1 = _split_start(own1, "bcast", gathered0[0], name="gather_rest_start")
    rest = {}

    def small_full(li):
        return (jnp.concatenate([g_conv[p, li] for p in range(N_CHIPS)], axis=1),
                jnp.concatenate([g_dw[p, li] for p in range(N_CHIPS)], axis=1))

    def w_out_l0(after):
        landed = _split_wait(pending1, len(own1), "bcast", after, name="gather_rest_wait")
        rest["landed"] = [lax.dynamic_update_index_in_dim(g_all, mine, me, 0) for g_all, mine in zip(landed, own1)]
        return rest["landed"][0].reshape(-1, w_out.shape[2])

    def params_l0(_):
        w_in_p = _padded_from_chips([g_in0[p].reshape(w_in_b[0].shape) for p in range(N_CHIPS)])
        return _layer_params(0, w_in_p, w_out_l0, *small_full(0), w)

    def params_l1(_):
        _, g_in1, g_out1 = rest["landed"]
        w_in_p = _padded_from_chips([g_in1[p] for p in range(N_CHIPS)])
        return _layer_params(1, w_in_p, g_out1.reshape(-1, g_out1.shape[-1]), *small_full(1), w)

    c = lax.axis_index("c")
    cols = w_in.shape[2]
    rows_out = w_out.shape[1]

    def grad_parts(g):
        dw = g["w_in_p"]
        p_in = jnp.stack([_chip_part_from_padded(dw, p, cols) for p in range(N_CHIPS)])
        return [p_in.reshape(N_CHIPS, 2, dw.shape[0] // 2, cols),
                g["w_out"].reshape(N_CHIPS, 2, rows_out // 2, D_MODEL)]

    def pair_sums(parts, sib, tag):
        return [_pair_sum(p, sb, c, MXU_DTYPE, name=f"grad_pair_sum_{k}_{tag}")
                for k, (p, sb) in enumerate(zip(parts, sib))]

    split = {"reduced": [lax.empty((depth, 2, w_in.shape[1] // 2, cols), F32),
                         lax.empty((depth, 2, rows_out // 2, D_MODEL), F32)]}

    def chip_sums(landed, sent, li):
        filled = [lax.dynamic_update_index_in_dim(r, lax.dynamic_index_in_dim(sk, me, 0, keepdims=False), me, 0)
                  for r, sk in zip(landed, sent)]
        halves = [_sum_lead(r, into, li, c, name=f"grad_chip_sum_{k}_l{li}")
                  for k, (r, into) in enumerate(zip(filled, split["reduced"]))]
        split["reduced"] = list(_pair_gather(halves, li, name=f"grad_pair_gather_l{li}"))

    def on_grads(li, g):
        if li != depth - 1:
            return None
        parts = grad_parts(g)
        swap_state, swap_token = _split_start(parts, "swap", g["w_out"], name="grad_swap_l1_start")

        def after_dycat(dycat):
            sib = _split_wait(swap_state, len(parts), "swap", dycat, name="grad_swap_l1_wait")
            split["sent"] = pair_sums(parts, sib, "l1")
            split["scatter"], token = _split_start(split["sent"], "scatter", split["sent"][0],
                                                   name="grad_scatter_l1_start")
            return token

        def after_attn(dproj):
            landed = _split_wait(split["scatter"], len(parts), "scatter", dproj, name="grad_scatter_l1_wait")
            chip_sums(landed, split["sent"], depth - 1)

        return {"start_token": swap_token, "after_dycat": after_dycat, "after_attn": after_attn}

    loss, grad_x, grads, dfinal = _local_step(x, loss_target, [params_l0, params_l1], final_norm_w,
                                              first_after=token1, on_grads=on_grads)

    parts0 = grad_parts(grads[0])
    sent0 = pair_sums(parts0, _pair_swap_halves(parts0, name="grad_pair_swap_l0"), "l0")
    scatter0, token0 = _split_start(sent0, "scatter", sent0[0], name="grad_scatter_l0_start")

    small_list = [grads[li][n] for li in range(depth) for n in SMALL]
    small_list += [grads[li][n] for li in range(depth) for n in ("ssd_conv_w", "conf_dw_w")]
    small_list += [dfinal, loss.reshape(1)]
    small_shapes = [a.shape for a in small_list]
    reduced = _unpack(_allreduce_small(_pack(small_list) + token0[0, 0], name="allreduce_small"), small_shapes)
    ns = len(SMALL)
    g = {n: jnp.stack([reduced[li * ns + i] for li in range(depth)]) for i, n in enumerate(SMALL)}
    conv_w_cols, dw_w_cols = ssd_conv_w.shape[2], conf_dw_w.shape[2]
    g["ssd_conv_w"] = jnp.stack([lax.dynamic_slice_in_dim(reduced[depth * ns + 2 * li], me * conv_w_cols,
                                                          conv_w_cols, axis=1) for li in range(depth)])
    g["conf_dw_w"] = jnp.stack([lax.dynamic_slice_in_dim(reduced[depth * ns + 2 * li + 1], me * dw_w_cols,
                                                         dw_w_cols, axis=1) for li in range(depth)])
    g["final_norm_w"] = reduced[-2]
    loss_total = reduced[-1][0]

    small_names = [n for n in WEIGHTS if n not in ("w_in", "w_out")]

    def as2d(a):
        return a.reshape(1, -1) if a.ndim == 1 else a

    deltas, new_ms, new_vs = _adam_small(*[[as2d(src[n]) for n in small_names] for src in (w, g, m, v)],
                                         name="adam_small")

    chip_sums(_split_wait(scatter0, len(sent0), "scatter", deltas[0], name="grad_scatter_l0_wait"), sent0, 0)
    g_w_in = split["reduced"][0].reshape(w_in.shape)
    g_w_out = split["reduced"][1].reshape(w_out.shape)

    outs_g, outs_d, outs_m, outs_v = {"w_in": g_w_in, "w_out": g_w_out}, {}, {}, {}
    to_cols, from_cols = (2, 0, 1), (1, 2, 0)
    outs_d["w_in"], outs_m["w_in"], outs_v["w_in"] = [
        jnp.transpose(a, from_cols) for a in _adam_cols_major(
            *[jnp.transpose(a, to_cols) for a in (w_in, g_w_in, m_w_in, v_w_in)], name="adam_w_in")]
    outs_d["w_out"], outs_m["w_out"], outs_v["w_out"] = _adam_big(w_out, g_w_out, m_w_out, v_w_out,
                                                                  name="adam_w_out")
    for n, dn, mn, vn in zip(small_names, deltas, new_ms, new_vs):
        outs_g[n], outs_d[n], outs_m[n], outs_v[n] = (g[n], dn.reshape(w[n].shape), mn.reshape(w[n].shape),
                                                      vn.reshape(w[n].shape))
    return (loss_total, grad_x, *[outs_g[n] for n in WEIGHTS], *[outs_d[n] for n in WEIGHTS],
            *[outs_m[n] for n in WEIGHTS], *[outs_v[n] for n in WEIGHTS])
```

```python
import functools
import math

import jax
import jax.numpy as jnp
import numpy as np
from jax import lax
from jax.experimental import pallas as pl
from jax.experimental.pallas import tpu as pltpu

F32 = jnp.float32
BF16 = jnp.bfloat16
MXU_DTYPE = BF16

D_MODEL = 1024
DEPTH = 2
SSD_HEADS = 16
SSD_HEAD_DIM = 64
SSD_STATE = 128
SSD_CONV = 4
CHUNK = 128
SSD_CONV_DIM = 1536
ATTN_HEAD_DIM = 64
ATTN_Q_HEADS = 8
WINDOW = 128
CONF_WIDTH = 512
CONF_KERNEL = 31
MIX_WIDTH = 2048
D_IN_PROJ = 5392
EPS = 1e-5

ADAM_LR = 0.001
ADAM_B1 = 0.9
ADAM_B2 = 0.999
ADAM_EPS = 1e-08
ADAM_WD = 0.01
ADAM_STEP = 10

LANES = 128
SUBLANES = 8
VMEM_LIMIT = 48 * 1024 * 1024

NP = 5632
OFF_ZA, OFF_Q, OFF_K, OFF_V, OFF_DT = 0, 512, 1024, 1152, 1280
ATTN_GROUP = 1536
OFF_CONF, OFF_ZC = 1536, 2560
CONF_GROUP = 1536
OFF_ZS = 3072
OFF_XBC = 4096
SECTIONS = ((0, 1024, OFF_ZS), (1024, 1536, OFF_ZA), (1536, 2048, OFF_ZC), (2048, 3584, OFF_XBC),
            (3584, 3600, OFF_DT), (3600, 4368, OFF_Q), (4368, 5392, OFF_CONF))

YCAT_ATTN, YCAT_CONF = 1024, 1536
ANY = pl.BlockSpec(memory_space=pl.ANY)

NN = (((1,), (0,)), ((), ()))
NT = (((1,), (1,)), ((), ()))
TN = (((0,), (0,)), ((), ()))


def _params(sem):
    return pltpu.CompilerParams(dimension_semantics=sem, vmem_limit_bytes=VMEM_LIMIT)


def _dot(a, b, dims=NN):
    return lax.dot_general(a.astype(MXU_DTYPE), b.astype(MXU_DTYPE), dims, preferred_element_type=F32)


def _split_bf16(a, passes):
    pieces = []
    r = a
    for _ in range(passes):
        p = r.astype(BF16)
        pieces.append(p)
        r = r - p.astype(F32)
    return pieces


def _xdot(a, sel, dims=NN, passes=2):
    out = None
    for p in _split_bf16(a, passes):
        t = lax.dot_general(p, sel, dims, preferred_element_type=F32)
        out = t if out is None else out + t
    return out


def _xdot_r(sel, b, dims=NN, passes=3):
    out = None
    for p in _split_bf16(b, passes):
        t = lax.dot_general(sel, p, dims, preferred_element_type=F32)
        out = t if out is None else out + t
    return out


def _sigmoid(x):
    return 1.0 / (1.0 + jnp.exp(-x))


def _silu(x):
    return x * _sigmoid(x)


def _dsilu(x):
    s = _sigmoid(x)
    return s * (1.0 + x * (1.0 - s))


def _softplus(x):
    return jnp.maximum(x, 0.0) + jnp.log(1.0 + jnp.exp(-jnp.abs(x)))


def _rowsum8(x):
    r, c = x.shape
    return jnp.sum(x.reshape(r // SUBLANES, SUBLANES, c), axis=0)


def _iota(shape, dim):
    return lax.broadcasted_iota(jnp.int32, shape, dim)


def _matmul(a, b, form, out_dtype, tm, tn, tk, name, residual=None, after=None):
    if form == "nn":
        (m, k), n = a.shape, b.shape[1]
    elif form == "nt":
        (m, k), n = a.shape, b.shape[0]
    else:
        (k, m), n = a.shape, b.shape[1]
    tm, tn, tk = min(tm, m), min(tn, n), min(tk, k)
    assert m % tm == 0 and n % tn == 0 and k % tk == 0, (name, m, n, k, tm, tn, tk)
    if form == "nn":
        a_spec = pl.BlockSpec((tm, tk), lambda i, j, s: (i, s))
        b_spec = pl.BlockSpec((tk, tn), lambda i, j, s: (s, j))
        dims = NN
    elif form == "nt":
        (m, k), n = a.shape, b.shape[0]
        a_spec = pl.BlockSpec((tm, tk), lambda i, j, s: (i, s))
        b_spec = pl.BlockSpec((tn, tk), lambda i, j, s: (j, s))
        dims = NT
    else:
        (k, m), n = a.shape, b.shape[1]
        a_spec = pl.BlockSpec((tk, tm), lambda i, j, s: (s, i))
        b_spec = pl.BlockSpec((tk, tn), lambda i, j, s: (s, j))
        dims = TN
    nk = k // tk
    has_res = residual is not None
    deps = [] if after is None else [after]

    def body_single(a_ref, b_ref, *rest):
        o = _dot(a_ref[...], b_ref[...], dims)
        if has_res:
            o = o + rest[0][...]
        rest[-1][...] = o.astype(out_dtype)

    def body(a_ref, b_ref, *rest):
        r_ref = rest[0] if has_res else None
        o_ref, acc = rest[-2:]
        s = pl.program_id(2)

        @pl.when(s == 0)
        def _():
            acc[...] = jnp.zeros_like(acc)

        acc[...] += _dot(a_ref[...], b_ref[...], dims)

        @pl.when(s == nk - 1)
        def _():
            o = acc[...]
            if has_res:
                o = o + r_ref[...]
            o_ref[...] = o.astype(out_dtype)

    in_specs = [a_spec, b_spec]
    args = [a, b]
    if has_res:
        in_specs.append(pl.BlockSpec((tm, tn), lambda i, j, s: (i, j)))
        args.append(residual)
    in_specs += [ANY] * len(deps)
    args += deps
    return pl.pallas_call(
        body_single if nk == 1 else body, name=name,
        out_shape=jax.ShapeDtypeStruct((m, n), out_dtype),
        grid=(m // tm, n // tn, nk),
        in_specs=in_specs,
        out_specs=pl.BlockSpec((tm, tn), lambda i, j, s: (i, j)),
        scratch_shapes=[] if nk == 1 else [pltpu.VMEM((tm, tn), F32)],
        compiler_params=_params(("parallel", "parallel", "arbitrary")),
    )(*args)


ROW_TILE = 256


PROJ_FWD_TM, PROJ_FWD_TN = 1024, 512


def _proj_fwd(x, w, w_in_p, name, after=None):
    t, d = x.shape
    n = w_in_p.shape[1]
    tm, tn = min(PROJ_FWD_TM, t), PROJ_FWD_TN
    assert t % tm == 0 and n % tn == 0
    deps = [] if after is None else [after]

    def body(x_ref, w_ref, b_ref, *rest):
        o_ref, ot_ref, h_scr = rest[len(deps):]

        @pl.when(pl.program_id(1) == 0)
        def _():
            xv = x_ref[...]
            rstd = lax.rsqrt(jnp.mean(xv * xv, axis=-1, keepdims=True) + EPS)
            h = xv * rstd * w_ref[...]
            h_scr[...] = h.astype(h_scr.dtype)
            ot_ref[...] = h.T.astype(ot_ref.dtype)

        o_ref[...] = _dot(h_scr[...], b_ref[...])

    return pl.pallas_call(
        body, name=name,
        out_shape=(jax.ShapeDtypeStruct((t, n), F32), jax.ShapeDtypeStruct((d, t), MXU_DTYPE)),
        grid=(t // tm, n // tn),
        in_specs=[pl.BlockSpec((tm, d), lambda i, j: (i, 0)), pl.BlockSpec((1, d), lambda i, j: (0, 0)),
                  pl.BlockSpec((d, tn), lambda i, j: (0, j))] + [ANY] * len(deps),
        out_specs=(pl.BlockSpec((tm, tn), lambda i, j: (i, j)), pl.BlockSpec((d, tm), lambda i, j: (0, i))),
        scratch_shapes=[pltpu.VMEM((tm, d), MXU_DTYPE)],
        compiler_params=_params(("parallel", "arbitrary")),
    )(x, w, w_in_p, *deps)


PROJ_BWD_TM, PROJ_BWD_TK = 1024, 1408


def _proj_bwd_dx(dproj, w_in_p, x, w, dres, name):
    t, d = x.shape
    kdim = dproj.shape[1]
    tm, tk = min(PROJ_BWD_TM, t), PROJ_BWD_TK
    nt, nk = t // tm, kdim // tk
    assert t % tm == 0 and kdim % tk == 0

    def body(a_ref, b_ref, x_ref, w_ref, dr_ref, dx_ref, dw_ref, acc, wacc):
        i, s = pl.program_id(0), pl.program_id(1)

        @pl.when((i == 0) & (s == 0))
        def _():
            wacc[...] = jnp.zeros_like(wacc)

        @pl.when(s == 0)
        def _():
            acc[...] = jnp.zeros_like(acc)

        acc[...] += _dot(a_ref[...], b_ref[...], NT)

        @pl.when(s == nk - 1)
        def _():
            xv = x_ref[...]
            rstd = lax.rsqrt(jnp.mean(xv * xv, axis=-1, keepdims=True) + EPS)
            xh = xv * rstd
            dhv = acc[...]
            g = dhv * w_ref[...]
            dx_ref[...] = dr_ref[...] + rstd * (g - xh * jnp.mean(g * xh, axis=-1, keepdims=True))
            wacc[...] += _rowsum8(dhv * xh)

        @pl.when((i == nt - 1) & (s == nk - 1))
        def _():
            dw_ref[...] = jnp.sum(wacc[...], axis=0, keepdims=True)

    row = pl.BlockSpec((tm, d), lambda i, s: (i, 0))
    vec = pl.BlockSpec((1, d), lambda i, s: (0, 0))
    return pl.pallas_call(
        body, name=name,
        out_shape=(jax.ShapeDtypeStruct((t, d), F32), jax.ShapeDtypeStruct((1, d), F32)),
        grid=(nt, nk),
        in_specs=[pl.BlockSpec((tm, tk), lambda i, s: (i, s)), pl.BlockSpec((d, tk), lambda i, s: (0, s)),
                  row, vec, row],
        out_specs=(row, vec),
        scratch_shapes=[pltpu.VMEM((tm, d), F32), pltpu.VMEM((SUBLANES, d), F32)],
        compiler_params=_params(("arbitrary", "arbitrary")),
    )(dproj, w_in_p, x, w, dres)


def _loss_head(xf, target, w, name):
    t, d = xf.shape
    tm = ROW_TILE
    nt = t // tm

    def body(x_ref, t_ref, w_ref, loss_ref, dx_ref, dw_ref, lacc, wacc):
        i = pl.program_id(0)

        @pl.when(i == 0)
        def _():
            lacc[...] = jnp.zeros_like(lacc)
            wacc[...] = jnp.zeros_like(wacc)

        xv = x_ref[...]
        rstd = lax.rsqrt(jnp.mean(xv * xv, axis=-1, keepdims=True) + EPS)
        xh = xv * rstd
        err = xh * w_ref[...] - t_ref[...]
        lacc[...] += jnp.sum(err * err)
        dy = err * (1.0 / d)
        g = dy * w_ref[...]
        dx_ref[...] = rstd * (g - xh * jnp.mean(g * xh, axis=-1, keepdims=True))
        wacc[...] += _rowsum8(dy * xh)

        @pl.when(i == nt - 1)
        def _():
            loss_ref[...] = lacc[...] * (0.5 / d)
            dw_ref[...] = jnp.sum(wacc[...], axis=0, keepdims=True)

    row = pl.BlockSpec((tm, d), lambda i: (i, 0))
    vec = pl.BlockSpec((1, d), lambda i: (0, 0))
    return pl.pallas_call(
        body, name=name,
        out_shape=(jax.ShapeDtypeStruct((SUBLANES, LANES), F32), jax.ShapeDtypeStruct((t, d), F32),
                   jax.ShapeDtypeStruct((1, d), F32)),
        grid=(nt,),
        in_specs=[row, row, vec],
        out_specs=(pl.BlockSpec((SUBLANES, LANES), lambda i: (0, 0)), row, vec),
        scratch_shapes=[pltpu.VMEM((SUBLANES, LANES), F32), pltpu.VMEM((SUBLANES, d), F32)],
        compiler_params=_params(("arbitrary",)),
    )(xf, target, w)


CONV_TILE = 512
CONV_COLS = 512
CONV_SUB_ROWS = 128
CONV_SUB_COLS = LANES


def _conv_halo(k):
    return SUBLANES if k - 1 <= SUBLANES else 32


def _conv_subtiles(tm, cw):
    return [(r0, c0) for r0 in range(0, tm, CONV_SUB_ROWS) for c0 in range(0, cw, CONV_SUB_COLS)]


def _conv_use_shifted(k):
    return k > SUBLANES


def _conv_shift_scratch(k, rows, cw):
    return [pltpu.VMEM((SUBLANES - 1, rows - SUBLANES, cw), F32)] if _conv_use_shifted(k) else []


def _conv_fill_shifted(ext, sh):
    n = sh.shape[1]
    for b in range(1, SUBLANES):
        sh[b - 1] = ext[b:b + n, :]


def _conv_rows(ext, sh, start, rows, cs):
    b = start % SUBLANES
    if b == 0 or not sh:
        return ext[start:start + rows, cs]
    return sh[0][b - 1, start - b:start - b + rows, cs]


def _conv_fwd(src, col0, width, w, bias, k, seq, name):
    t = src.shape[0]
    tm, cw, halo = CONV_TILE, CONV_COLS, _conv_halo(k)
    sr, sc = CONV_SUB_ROWS, CONV_SUB_COLS
    p = k - 1
    cb0 = col0 // cw
    kp = w.shape[0]

    shifted = _conv_use_shifted(k)

    def body(x_ref, h_ref, w_ref, b_ref, o_ref, ext, *sh):
        i = pl.program_id(0)
        seq_start = (i * tm) % seq == 0
        ext[halo:, :] = x_ref[...]
        ext[:halo, :] = jnp.where(seq_start, 0.0, h_ref[...])
        if shifted:
            _conv_fill_shifted(ext, sh[0])
        for r0, c0 in _conv_subtiles(tm, cw):
            cs = slice(c0, c0 + sc)
            acc = jnp.zeros((sr, sc), F32) + b_ref[:, cs]
            for j in range(k):
                acc = acc + w_ref[j:j + 1, cs] * _conv_rows(ext, sh, r0 + halo - p + j, sr, cs)
            o_ref[r0:r0 + sr, cs] = acc

    return pl.pallas_call(
        body, name=name,
        out_shape=jax.ShapeDtypeStruct((t, width), F32),
        grid=(t // tm, width // cw),
        in_specs=[pl.BlockSpec((tm, cw), lambda i, j: (i, cb0 + j)),
                  pl.BlockSpec((halo, cw), lambda i, j: (jnp.maximum(i * (tm // halo) - 1, 0), cb0 + j)),
                  pl.BlockSpec((kp, cw), lambda i, j: (0, j)),
                  pl.BlockSpec((1, cw), lambda i, j: (0, j))],
        out_specs=pl.BlockSpec((tm, cw), lambda i, j: (i, j)),
        scratch_shapes=[pltpu.VMEM((halo + tm, cw), F32)] + _conv_shift_scratch(k, halo + tm, cw),
        compiler_params=_params(("parallel", "parallel")),
    )(src, src, w, bias)


def _conv_bwd(dy, src, col0, width, w, k, seq, name, into=None):
    t = src.shape[0]
    tm, cw, halo = CONV_TILE, CONV_COLS, _conv_halo(k)
    sr, sc = CONV_SUB_ROWS, CONV_SUB_COLS
    p = k - 1
    cb0 = col0 // cw
    kp = w.shape[0]
    nt = t // tm
    last_halo = t // halo - 1

    shifted = _conv_use_shifted(k)

    def body(dy_ref, dn_ref, x_ref, xp_ref, w_ref, *rest):
        if into is not None:
            rest = rest[1:]
        dx_ref, dw_ref, db_ref, dyext, xext, wacc, bacc = rest[:7]
        sh = rest[7:]
        i = pl.program_id(1)
        dysh, xsh = (sh[:1], sh[1:]) if shifted else ((), ())

        @pl.when(i == 0)
        def _():
            wacc[...] = jnp.zeros_like(wacc)
            bacc[...] = jnp.zeros_like(bacc)

        seq_start = (i * tm) % seq == 0
        seq_end = ((i + 1) * tm) % seq == 0
        dyext[:tm, :] = dy_ref[...]
        dyext[tm:, :] = jnp.where(seq_end, 0.0, dn_ref[...])
        xext[halo:, :] = x_ref[...]
        xext[:halo, :] = jnp.where(seq_start, 0.0, xp_ref[...])
        if shifted:
            _conv_fill_shifted(dyext, dysh[0])
            _conv_fill_shifted(xext, xsh[0])
        for r0, c0 in _conv_subtiles(tm, cw):
            cs = slice(c0, c0 + sc)
            dyv = dy_ref[r0:r0 + sr, cs]
            acc = jnp.zeros((sr, sc), F32)
            for j in range(k):
                acc = acc + w_ref[j:j + 1, cs] * _conv_rows(dyext, dysh, r0 + p - j, sr, cs)
                wacc[j, :, cs] += _rowsum8(dyv * _conv_rows(xext, xsh, r0 + halo - p + j, sr, cs))
            dx_ref[r0:r0 + sr, cs] = acc.astype(dx_ref.dtype)
            bacc[:, cs] += _rowsum8(dyv)

        @pl.when(i == nt - 1)
        def _():
            dw_ref[...] = jnp.zeros_like(dw_ref)
            for j in range(k):
                dw_ref[j:j + 1, :] = jnp.sum(wacc[j], axis=0, keepdims=True)
            db_ref[...] = jnp.sum(bacc[...], axis=0, keepdims=True)

    if into is None:
        dx_shape = jax.ShapeDtypeStruct((t, width), F32)
        dx_spec = pl.BlockSpec((tm, cw), lambda j, i: (i, j))
        extra_specs, extra_args, aliases = [], [], {}
    else:
        dx_shape = jax.ShapeDtypeStruct(into.shape, into.dtype)
        dx_spec = pl.BlockSpec((tm, cw), lambda j, i: (i, cb0 + j))
        extra_specs, extra_args, aliases = [ANY], [into], {5: 0}
    return pl.pallas_call(
        body, name=name,
        out_shape=(dx_shape, jax.ShapeDtypeStruct((kp, width), F32), jax.ShapeDtypeStruct((1, width), F32)),
        grid=(width // cw, nt),
        in_specs=[pl.BlockSpec((tm, cw), lambda j, i: (i, j)),
                  pl.BlockSpec((halo, cw), lambda j, i: (jnp.minimum((i + 1) * (tm // halo), last_halo), j)),
                  pl.BlockSpec((tm, cw), lambda j, i: (i, cb0 + j)),
                  pl.BlockSpec((halo, cw), lambda j, i: (jnp.maximum(i * (tm // halo) - 1, 0), cb0 + j)),
                  pl.BlockSpec((kp, cw), lambda j, i: (0, j))] + extra_specs,
        out_specs=(dx_spec,
                   pl.BlockSpec((kp, cw), lambda j, i: (0, j)),
                   pl.BlockSpec((1, cw), lambda j, i: (0, j))),
        input_output_aliases=aliases,
        scratch_shapes=[pltpu.VMEM((tm + halo, cw), F32), pltpu.VMEM((halo + tm, cw), F32),
                        pltpu.VMEM((kp, SUBLANES, cw), F32), pltpu.VMEM((SUBLANES, cw), F32)]
        + 2 * _conv_shift_scratch(k, halo + tm, cw),
        compiler_params=_params(("parallel", "arbitrary")),
    )(dy, dy, src, src, w, *extra_args)


def _conf_specs(tm, cw, halo, order):
    cb = OFF_CONF // cw

    def blk(col):
        return pl.BlockSpec((tm, cw), lambda *g: (order(*g), col))

    def prev(col):
        return pl.BlockSpec((halo, cw), lambda *g: (jnp.maximum(order(*g) * (tm // halo) - 1, 0), col))

    return blk(cb), prev(cb), blk(cb + 1), prev(cb + 1)


def _glu_window(ext, a_ref, ah_ref, g_ref, gh_ref, seq_start, halo):
    ext[halo:, :] = a_ref[...] * _sigmoid(g_ref[...])
    ext[:halo, :] = jnp.where(seq_start, 0.0, ah_ref[...] * _sigmoid(gh_ref[...]))


def _conf_fwd(proj, w, bias, ln_w, ln_b, ycat, seq, name):
    t = proj.shape[0]
    k = CONF_KERNEL
    tm, cw, halo = CONV_TILE, CONF_WIDTH, _conv_halo(k)
    sr, sc = CONV_SUB_ROWS, CONV_SUB_COLS
    p = k - 1
    kp = w.shape[0]

    def body(a_ref, ah_ref, g_ref, gh_ref, z_ref, w_ref, b_ref, lw_ref, lb_ref, _, c1_ref, y_ref, ext, sh):
        i = pl.program_id(0)
        _glu_window(ext, a_ref, ah_ref, g_ref, gh_ref, (i * tm) % seq == 0, halo)
        _conv_fill_shifted(ext, sh)
        for r0, c0 in _conv_subtiles(tm, cw):
            cs = slice(c0, c0 + sc)
            acc = jnp.zeros((sr, sc), F32) + b_ref[:, cs]
            for j in range(k):
                acc = acc + w_ref[j:j + 1, cs] * _conv_rows(ext, (sh,), r0 + halo - p + j, sr, cs)
            c1_ref[r0:r0 + sr, cs] = acc
        for r0 in range(0, tm, sr):
            rows = slice(r0, r0 + sr)
            cv = c1_ref[rows, :]
            xc = cv - jnp.mean(cv, axis=-1, keepdims=True)
            rstd = lax.rsqrt(jnp.mean(xc * xc, axis=-1, keepdims=True) + EPS)
            c2 = xc * rstd * lw_ref[...] + lb_ref[...]
            y_ref[rows, :] = (_silu(c2) * _silu(z_ref[rows, :])).astype(y_ref.dtype)

    vec = pl.BlockSpec((1, cw), lambda i: (0, 0))
    row = pl.BlockSpec((tm, cw), lambda i: (i, 0))
    return pl.pallas_call(
        body, name=name,
        out_shape=(jax.ShapeDtypeStruct((t, cw), F32), jax.ShapeDtypeStruct(ycat.shape, ycat.dtype)),
        grid=(t // tm,),
        in_specs=[*_conf_specs(tm, cw, halo, lambda i: i),
                  pl.BlockSpec((tm, cw), lambda i: (i, OFF_ZC // cw)),
                  pl.BlockSpec((kp, cw), lambda i: (0, 0)), vec, vec, vec, ANY],
        out_specs=(row, pl.BlockSpec((tm, cw), lambda i: (i, YCAT_CONF // cw))),
        input_output_aliases={9: 1},
        scratch_shapes=[pltpu.VMEM((halo + tm, cw), F32)] + _conv_shift_scratch(k, halo + tm, cw),
        compiler_params=_params(("parallel",)),
    )(proj, proj, proj, proj, proj, w, bias, ln_w, ln_b, ycat)


def _conf_bwd(dycat, proj, c1, w, ln_w, ln_b, dproj, seq, name):
    t = proj.shape[0]
    k = CONF_KERNEL
    tm, cw, halo = CONV_TILE, CONF_WIDTH, _conv_halo(k)
    sr, sc = CONV_SUB_ROWS, CONV_SUB_COLS
    p = k - 1
    kp = w.shape[0]
    nt = t // tm
    last_halo = t // halo - 1

    def body(dy_ref, dyn_ref, c_ref, cn_ref, z_ref, zn_ref, a_ref, ah_ref, g_ref, gh_ref, w_ref, lw_ref, lb_ref, _,
             grp_ref, dw_ref, db_ref, dlw_ref, dlb_ref, dyext, xext, wacc, bacc, lwacc, lbacc, dysh, xsh):
        i = pl.program_id(0)

        @pl.when(i == 0)
        def _():
            wacc[...] = jnp.zeros_like(wacc)
            bacc[...] = jnp.zeros_like(bacc)
            lwacc[...] = jnp.zeros_like(lwacc)
            lbacc[...] = jnp.zeros_like(lbacc)

        def post_bwd(dy, cv, zv):
            xc = cv - jnp.mean(cv, axis=-1, keepdims=True)
            rstd = lax.rsqrt(jnp.mean(xc * xc, axis=-1, keepdims=True) + EPS)
            xh = xc * rstd
            c2 = xh * lw_ref[...] + lb_ref[...]
            dz = dy * _silu(c2) * _dsilu(zv)
            dc2 = dy * _silu(zv) * _dsilu(c2)
            dxh = dc2 * lw_ref[...]
            dc = rstd * (dxh - jnp.mean(dxh, axis=-1, keepdims=True)
                         - xh * jnp.mean(dxh * xh, axis=-1, keepdims=True))
            return dc, dz, dc2 * xh, dc2

        seq_end = ((i + 1) * tm) % seq == 0
        for r0 in range(0, tm, sr):
            rows = slice(r0, r0 + sr)
            dc, dz, lw_terms, lb_terms = post_bwd(dy_ref[rows, :], c_ref[rows, :], z_ref[rows, :])
            dyext[rows, :] = dc
            grp_ref[rows, 2 * cw:] = dz.astype(grp_ref.dtype)
            lwacc[...] += _rowsum8(lw_terms)
            lbacc[...] += _rowsum8(lb_terms)
        dc_next = post_bwd(dyn_ref[...], cn_ref[...], zn_ref[...])[0]
        dyext[tm:, :] = jnp.where(seq_end, 0.0, dc_next)
        _glu_window(xext, a_ref, ah_ref, g_ref, gh_ref, (i * tm) % seq == 0, halo)
        _conv_fill_shifted(dyext, dysh)
        _conv_fill_shifted(xext, xsh)
        dag_ref = grp_ref
        for r0, c0 in _conv_subtiles(tm, cw):
            cs = slice(c0, c0 + sc)
            rows = slice(r0, r0 + sr)
            dyv = dyext[rows, cs]
            acc = jnp.zeros((sr, sc), F32)
            for j in range(k):
                acc = acc + w_ref[j:j + 1, cs] * _conv_rows(dyext, (dysh,), r0 + p - j, sr, cs)
                wacc[j, :, cs] += _rowsum8(dyv * _conv_rows(xext, (xsh,), r0 + halo - p + j, sr, cs))
            bacc[:, cs] += _rowsum8(dyv)
            s = _sigmoid(g_ref[rows, cs])
            dag_ref[rows, cs] = (acc * s).astype(dag_ref.dtype)
            dag_ref[rows, cw + c0:cw + c0 + sc] = (acc * a_ref[rows, cs] * s * (1.0 - s)).astype(dag_ref.dtype)

        @pl.when(i == nt - 1)
        def _():
            dw_ref[...] = jnp.zeros_like(dw_ref)
            for j in range(k):
                dw_ref[j:j + 1, :] = jnp.sum(wacc[j], axis=0, keepdims=True)
            db_ref[...] = jnp.sum(bacc[...], axis=0, keepdims=True)
            dlw_ref[...] = jnp.sum(lwacc[...], axis=0, keepdims=True)
            dlb_ref[...] = jnp.sum(lbacc[...], axis=0, keepdims=True)

    def blk(col):
        return pl.BlockSpec((tm, cw), lambda i: (i, col))

    def nxt(col):
        return pl.BlockSpec((halo, cw), lambda i: (jnp.minimum((i + 1) * (tm // halo), last_halo), col))

    vec = pl.BlockSpec((1, cw), lambda i: (0, 0))
    return pl.pallas_call(
        body, name=name,
        out_shape=(jax.ShapeDtypeStruct(dproj.shape, dproj.dtype), jax.ShapeDtypeStruct((kp, cw), F32),
                   jax.ShapeDtypeStruct((1, cw), F32), jax.ShapeDtypeStruct((1, cw), F32),
                   jax.ShapeDtypeStruct((1, cw), F32)),
        grid=(nt,),
        in_specs=[blk(YCAT_CONF // cw), nxt(YCAT_CONF // cw), blk(0), nxt(0), blk(OFF_ZC // cw), nxt(OFF_ZC // cw),
                  *_conf_specs(tm, cw, halo, lambda i: i),
                  pl.BlockSpec((kp, cw), lambda i: (0, 0)), vec, vec, ANY],
        out_specs=(pl.BlockSpec((tm, CONF_GROUP), lambda i: (i, OFF_CONF // CONF_GROUP)),
                   pl.BlockSpec((kp, cw), lambda i: (0, 0)), vec, vec, vec),
        input_output_aliases={13: 0},
        scratch_shapes=[pltpu.VMEM((tm + halo, cw), F32), pltpu.VMEM((halo + tm, cw), F32),
                        pltpu.VMEM((kp, SUBLANES, cw), F32), pltpu.VMEM((SUBLANES, cw), F32),
                        pltpu.VMEM((SUBLANES, cw), F32), pltpu.VMEM((SUBLANES, cw), F32)]
        + 2 * _conv_shift_scratch(k, halo + tm, cw),
        compiler_params=_params(("arbitrary",)),
    )(dycat, dycat, c1, c1, proj, proj, proj, proj, proj, proj, w, ln_w, ln_b, dproj)


def _half_mask(half):
    lane = _iota((1, LANES), 1)
    return ((lane >= half * ATTN_HEAD_DIM) & (lane < (half + 1) * ATTN_HEAD_DIM)).astype(F32)


def _stack_heads(xp, g):
    m = _half_mask(g)
    swapped = pltpu.roll(xp, ATTN_HEAD_DIM, axis=1)
    return jnp.concatenate([xp * m, swapped * m] if g == 0 else [swapped * m, xp * m], axis=0)


def _unstack_heads(both, g):
    w = both.shape[0] // 2
    top, bot = both[:w], both[w:]
    lo, hi = _half_mask(0), _half_mask(1)
    if g == 0:
        return top * lo + pltpu.roll(bot, ATTN_HEAD_DIM, axis=1) * hi
    return pltpu.roll(top, ATTN_HEAD_DIM, axis=1) * lo + bot * hi


def _band_mask(first_block):
    w = WINDOW
    qi = _iota((w, 2 * w), 0)
    kj = _iota((w, 2 * w), 1) - w
    rel = qi - kj
    return (rel >= 0) & (rel < w) & (jnp.logical_not(first_block) | (kj >= 0))


def _lane_pick(x, h):
    return jnp.sum(jnp.where(_iota(x.shape, 1) == h, x, 0.0), axis=1, keepdims=True)


def _attn_specs(nb, rev):
    w = WINDOW

    def blk(i):
        return nb - 1 - i if rev else i

    def row(b, i):
        return b * nb + blk(i)

    def prow(b, i):
        return b * nb + jnp.maximum(blk(i) - 1, 0)

    q = pl.BlockSpec((w, 512), lambda b, i: (row(b, i), OFF_Q // 512))
    kc = pl.BlockSpec((w, 128), lambda b, i: (row(b, i), OFF_K // 128))
    kp = pl.BlockSpec((w, 128), lambda b, i: (prow(b, i), OFF_K // 128))
    vc = pl.BlockSpec((w, 128), lambda b, i: (row(b, i), OFF_V // 128))
    vp = pl.BlockSpec((w, 128), lambda b, i: (prow(b, i), OFF_V // 128))
    z = pl.BlockSpec((w, 512), lambda b, i: (row(b, i), OFF_ZA // 512))
    return q, kc, kp, vc, vp, z, row


def _attn_fwd(proj, sinks, ycat, nbatch, name):
    t = proj.shape[0]
    w = WINDOW
    nb = t // nbatch // w
    scale = ATTN_HEAD_DIM ** -0.5
    q_s, kc_s, kp_s, vc_s, vp_s, z_s, row = _attn_specs(nb, False)

    def body(q_ref, kc_ref, kp_ref, vc_ref, vp_ref, z_ref, sk_ref, _, y_ref, o_ref, lse_ref):
        first = pl.program_id(1) == 0
        mask = _band_mask(first)
        kk = jnp.concatenate([kp_ref[...], kc_ref[...]], axis=0).astype(MXU_DTYPE)
        vv = jnp.concatenate([vp_ref[...], vc_ref[...]], axis=0).astype(MXU_DTYPE)
        sk = sk_ref[...]
        lane = _iota((w, LANES), 1)
        mask2 = jnp.concatenate([mask, mask], axis=0)
        scores = [_dot(_stack_heads(q_ref[:, j * LANES:(j + 1) * LANES], j // 2), kk, NT) for j in range(4)]
        lse_all = jnp.zeros((w, LANES), F32)
        for j in range(4):
            s = jnp.where(mask2, scores[j] * scale, -1e30)
            skc = jnp.concatenate([jnp.broadcast_to(_lane_pick(sk, 2 * j), (w, 1)),
                                   jnp.broadcast_to(_lane_pick(sk, 2 * j + 1), (w, 1))], axis=0)
            m = jnp.maximum(jnp.max(s, axis=1, keepdims=True), skc)
            den = jnp.sum(jnp.exp(s - m), axis=1, keepdims=True) + jnp.exp(skc - m)
            lse = m + jnp.log(den)
            lse_all = jnp.where(lane == 2 * j, lse[:w], lse_all)
            lse_all = jnp.where(lane == 2 * j + 1, lse[w:], lse_all)
            op = _unstack_heads(_dot(jnp.exp(s - lse), vv), j // 2)
            cols = slice(j * LANES, (j + 1) * LANES)
            o_ref[:, cols] = op
            y_ref[:, cols] = (op * _silu(z_ref[:, cols])).astype(y_ref.dtype)
        lse_ref[...] = lse_all

    return pl.pallas_call(
        body, name=name,
        out_shape=(jax.ShapeDtypeStruct(ycat.shape, ycat.dtype), jax.ShapeDtypeStruct((t, 512), F32),
                   jax.ShapeDtypeStruct((t, LANES), F32)),
        grid=(nbatch, nb),
        in_specs=[q_s, kc_s, kp_s, vc_s, vp_s, z_s, pl.BlockSpec((1, LANES), lambda b, i: (0, 0)), ANY],
        out_specs=(pl.BlockSpec((w, 512), lambda b, i: (row(b, i), YCAT_ATTN // 512)),
                   pl.BlockSpec((w, 512), lambda b, i: (row(b, i), 0)),
                   pl.BlockSpec((w, LANES), lambda b, i: (row(b, i), 0))),
        input_output_aliases={7: 0},
        compiler_params=_params(("parallel", "parallel")),
    )(proj, proj, proj, proj, proj, proj, sinks, ycat)


def _attn_bwd(dycat, proj, o, lse, sinks, ddt, dproj, nbatch, name):
    t = proj.shape[0]
    w = WINDOW
    nb = t // nbatch // w
    scale = ATTN_HEAD_DIM ** -0.5
    q_s, kc_s, kp_s, vc_s, vp_s, z_s, row = _attn_specs(nb, True)

    def body(dy_ref, q_ref, kc_ref, kp_ref, vc_ref, vp_ref, z_ref, o_ref, lse_ref, sk_ref, ddt_ref, _,
             grp_ref, dsk_ref, kcarry, vcarry, sacc):
        b, i = pl.program_id(0), pl.program_id(1)

        @pl.when((b == 0) & (i == 0))
        def _():
            sacc[...] = jnp.zeros_like(sacc)

        @pl.when(i == 0)
        def _():
            kcarry[...] = jnp.zeros_like(kcarry)
            vcarry[...] = jnp.zeros_like(vcarry)

        first = i == nb - 1
        mask = _band_mask(first)
        kk = jnp.concatenate([kp_ref[...], kc_ref[...]], axis=0).astype(MXU_DTYPE)
        vv = jnp.concatenate([vp_ref[...], vc_ref[...]], axis=0).astype(MXU_DTYPE)
        sk = sk_ref[...]
        lse_all = lse_ref[...]
        lane1 = _iota((1, LANES), 1)
        mask2 = jnp.concatenate([mask, mask], axis=0)
        qs, dos, deltas, lses, scores, dps = [], [], [], [], [], []
        for j in range(4):
            cols = slice(j * LANES, (j + 1) * LANES)
            qp, zp, ov, dy = q_ref[:, cols], z_ref[:, cols], o_ref[:, cols], dy_ref[:, cols]
            grp_ref[:, OFF_ZA + j * LANES:OFF_ZA + (j + 1) * LANES] = (dy * ov * _dsilu(zp)).astype(grp_ref.dtype)
            do = dy * _silu(zp)
            q2 = _stack_heads(qp, j // 2).astype(MXU_DTYPE)
            do2 = _stack_heads(do, j // 2)
            qs.append(q2)
            dos.append(do2.astype(MXU_DTYPE))
            deltas.append(jnp.sum(do2 * _stack_heads(ov, j // 2), axis=1, keepdims=True))
            lses.append(jnp.concatenate([_lane_pick(lse_all, 2 * j), _lane_pick(lse_all, 2 * j + 1)], axis=0))
            scores.append(_dot(q2, kk, NT))
            dps.append(_dot(do2, vv, NT))
        prs, dss = [], []
        dsk = jnp.zeros((1, LANES), F32)
        for j in range(4):
            pr = jnp.exp(jnp.where(mask2, scores[j] * scale, -1e30) - lses[j])
            prs.append(pr.astype(MXU_DTYPE))
            dss.append((pr * (dps[j] - deltas[j])).astype(MXU_DTYPE))
            skc = jnp.concatenate([jnp.broadcast_to(_lane_pick(sk, 2 * j), (w, 1)),
                                   jnp.broadcast_to(_lane_pick(sk, 2 * j + 1), (w, 1))], axis=0)
            sink_term = jnp.exp(skc - lses[j]) * deltas[j]
            dsk = dsk - jnp.where(lane1 == 2 * j, jnp.sum(sink_term[:w]), 0.0)
            dsk = dsk - jnp.where(lane1 == 2 * j + 1, jnp.sum(sink_term[w:]), 0.0)
        dkk = jnp.zeros((2 * w, LANES), F32)
        dvv = jnp.zeros((2 * w, LANES), F32)
        for j in range(4):
            dq = _unstack_heads(_dot(dss[j], kk) * scale, j // 2)
            grp_ref[:, OFF_Q + j * LANES:OFF_Q + (j + 1) * LANES] = dq.astype(grp_ref.dtype)
            dkk = dkk + _dot(dss[j], qs[j], TN) * scale
            dvv = dvv + _dot(prs[j], dos[j], TN)
        grp_ref[:, OFF_K:OFF_K + LANES] = (dkk[w:, :] + kcarry[...]).astype(grp_ref.dtype)
        grp_ref[:, OFF_V:OFF_V + LANES] = (dvv[w:, :] + vcarry[...]).astype(grp_ref.dtype)
        grp_ref[:, OFF_DT:OFF_DT + LANES] = ddt_ref[...].astype(grp_ref.dtype)
        grp_ref[:, OFF_DT + LANES:] = jnp.zeros((w, ATTN_GROUP - OFF_DT - LANES), grp_ref.dtype)
        kcarry[...] = dkk[:w, :]
        vcarry[...] = dvv[:w, :]
        sacc[...] += dsk

        @pl.when((b == nbatch - 1) & (i == nb - 1))
        def _():
            dsk_ref[...] = sacc[...]

    return pl.pallas_call(
        body, name=name,
        out_shape=(jax.ShapeDtypeStruct(dproj.shape, dproj.dtype), jax.ShapeDtypeStruct((1, LANES), F32)),
        grid=(nbatch, nb),
        in_specs=[pl.BlockSpec((w, 512), lambda b, i: (row(b, i), YCAT_ATTN // 512)),
                  q_s, kc_s, kp_s, vc_s, vp_s, z_s,
                  pl.BlockSpec((w, 512), lambda b, i: (row(b, i), 0)),
                  pl.BlockSpec((w, LANES), lambda b, i: (row(b, i), 0)),
                  pl.BlockSpec((1, LANES), lambda b, i: (0, 0)),
                  pl.BlockSpec((w, LANES), lambda b, i: (row(b, i), 0)), ANY],
        out_specs=(pl.BlockSpec((w, ATTN_GROUP), lambda b, i: (row(b, i), 0)),
                   pl.BlockSpec((1, LANES), lambda b, i: (0, 0))),
        input_output_aliases={11: 0},
        scratch_shapes=[pltpu.VMEM((w, LANES), F32), pltpu.VMEM((w, LANES), F32),
                        pltpu.VMEM((1, LANES), F32)],
        compiler_params=_params(("arbitrary", "arbitrary")),
    )(dycat, proj, proj, proj, proj, proj, proj, o, lse, sinks, ddt, dproj)


SSD_WIDTH = SSD_HEADS * SSD_HEAD_DIM
GROUP_ROWS = SSD_WIDTH // 2


def _expand_mat():
    r, c = _iota((LANES, SSD_WIDTH), 0), _iota((LANES, SSD_WIDTH), 1)
    return (r == lax.shift_right_logical(c, 6)).astype(BF16)


def _expand_mat_t():
    r, c = _iota((SSD_WIDTH, LANES), 0), _iota((SSD_WIDTH, LANES), 1)
    return (c == lax.shift_right_logical(r, 6)).astype(BF16)


def _ssd_common(u_ref, dt_ref, dtb_ref, a_ref):
    q = CHUNK
    act = _silu(u_ref[...])
    xs = act[:, :SSD_WIDTH]
    bm = act[:, SSD_WIDTH:SSD_WIDTH + 256]
    cm = act[:, SSD_WIDTH + 256:]
    dtp = _softplus(dt_ref[...] + dtb_ref[...])
    a = dtp * a_ref[...]
    tril = (_iota((q, q), 0) >= _iota((q, q), 1)).astype(BF16)
    acs = _xdot_r(tril, a)
    acs_t = acs.T
    e = _expand_mat()
    dt_x = _xdot(dtp, e)
    ea = jnp.exp(_xdot(acs, e))
    a_end = jnp.sum(jnp.where(_iota(acs.shape, 0) == q - 1, acs, 0.0), axis=0, keepdims=True)
    dec = jnp.exp(_xdot(a_end - acs, e))
    a_end_col = jnp.broadcast_to(_lane_pick(acs_t, q - 1), (LANES, LANES))
    s_scale = jnp.exp(_xdot_r(_expand_mat_t(), a_end_col))
    return act, xs, bm, cm, dtp, acs, acs_t, dt_x, ea, dec, s_scale, tril


def _decay_mat(acs, acs_t, h):
    q = CHUNK
    col = _lane_pick(acs, h)
    rowv = jnp.sum(jnp.where(_iota(acs_t.shape, 0) == h, acs_t, 0.0), axis=0, keepdims=True)
    causal = _iota((q, q), 0) >= _iota((q, q), 1)
    return jnp.exp(jnp.where(causal, col - rowv, -1e30))


GN_WIDTH = 512


def _ssd_fwd(u, proj, dtb, a_neg, d_x, norm_w, ycat, nbatch, name):
    t = u.shape[0]
    q = CHUNK
    nc = t // nbatch // q

    def body(u_ref, dt_ref, z_ref, dtb_ref, a_ref, dx_ref, nw_ref, _, y_ref, st_ref, yn_ref, state):
        c = pl.program_id(1)

        @pl.when(c == 0)
        def _():
            state[...] = jnp.zeros_like(state)

        st_ref[...] = state[...]
        act, xs, bm, cm, dtp, acs, acs_t, dt_x, ea, dec, s_scale, _ = _ssd_common(u_ref, dt_ref, dtb_ref, a_ref)
        xdt = xs * dt_x
        xdec = xdt * dec
        lo, hi = _half_mask(0), _half_mask(1)
        grp = []
        for g in range(2):
            bg = bm[:, g * LANES:(g + 1) * LANES]
            cg = cm[:, g * LANES:(g + 1) * LANES]
            rows = slice(g * GROUP_ROWS, (g + 1) * GROUP_ROWS)
            sg = state[rows, :]
            grp.append((_dot(cg, bg, NT), _dot(cg, sg, NT), rows,
                        s_scale[rows, :] * sg + _dot(xdec[:, rows], bg, TN)))
        for g in range(2):
            cb, yoff, rows, state_new = grp[g]
            for j in range(4):
                pj = g * 4 + j
                cols = slice(pj * LANES, (pj + 1) * LANES)
                xp = xdt[:, cols]
                m2 = jnp.concatenate([cb * _decay_mat(acs, acs_t, 2 * pj), cb * _decay_mat(acs, acs_t, 2 * pj + 1)],
                                     axis=1)
                yp = _dot(m2, jnp.concatenate([xp * lo, xp * hi], axis=0))
                yp = yp + yoff[:, j * LANES:(j + 1) * LANES] * ea[:, cols]
                y_ref[:, cols] = yp + dx_ref[:, cols] * xs[:, cols]
            state[rows, :] = state_new
        for g in range(SSD_WIDTH // GN_WIDTH):
            cols = slice(g * GN_WIDTH, (g + 1) * GN_WIDTH)
            gg = y_ref[:, cols] * _silu(z_ref[:, cols])
            rstd = lax.rsqrt(jnp.mean(gg * gg, axis=-1, keepdims=True) + EPS)
            yn_ref[:, cols] = (gg * rstd * nw_ref[:, cols]).astype(yn_ref.dtype)

    vec = pl.BlockSpec((1, LANES), lambda b, c: (0, 0))
    wide = pl.BlockSpec((q, SSD_WIDTH), lambda b, c: (b * nc + c, 0))
    wvec = pl.BlockSpec((1, SSD_WIDTH), lambda b, c: (0, 0))
    return pl.pallas_call(
        body, name=name,
        out_shape=(jax.ShapeDtypeStruct((t, SSD_WIDTH), F32),
                   jax.ShapeDtypeStruct((nbatch * nc * SSD_WIDTH, SSD_STATE), F32),
                   jax.ShapeDtypeStruct(ycat.shape, ycat.dtype)),
        grid=(nbatch, nc),
        in_specs=[pl.BlockSpec((q, SSD_CONV_DIM), lambda b, c: (b * nc + c, 0)),
                  pl.BlockSpec((q, LANES), lambda b, c: (b * nc + c, OFF_DT // LANES)),
                  pl.BlockSpec((q, SSD_WIDTH), lambda b, c: (b * nc + c, OFF_ZS // SSD_WIDTH)),
                  vec, vec, wvec, wvec, ANY],
        out_specs=(wide, pl.BlockSpec((SSD_WIDTH, SSD_STATE), lambda b, c: (b * nc + c, 0)), wide),
        input_output_aliases={7: 2},
        scratch_shapes=[pltpu.VMEM((SSD_WIDTH, SSD_STATE), F32)],
        compiler_params=_params(("parallel", "arbitrary")),
    )(u, proj, proj, dtb, a_neg, d_x, norm_w, ycat)


def _ssd_bwd(dycat, u, proj, y, states, dtb, a_neg, d_x, norm_w, dproj, nbatch, name):
    t = u.shape[0]
    q = CHUNK
    nc = t // nbatch // q

    def body(do_ref, u_ref, dt_ref, z_ref, y_ref, st_ref, dtb_ref, a_ref, dx_ref, nw_ref, _,
             du_ref, dz_ref, ddt_ref, dal_ref, dd_ref, dtbg_ref, dnw_ref, dstate, acc_a, acc_d, acc_b, acc_w):
        b, c = pl.program_id(0), pl.program_id(1)

        @pl.when((b == 0) & (c == 0))
        def _():
            acc_a[...] = jnp.zeros_like(acc_a)
            acc_d[...] = jnp.zeros_like(acc_d)
            acc_b[...] = jnp.zeros_like(acc_b)
            acc_w[...] = jnp.zeros_like(acc_w)

        @pl.when(c == 0)
        def _():
            dstate[...] = jnp.zeros_like(dstate)

        dy_parts = []
        for g in range(SSD_WIDTH // GN_WIDTH):
            cols = slice(g * GN_WIDTH, (g + 1) * GN_WIDTH)
            yv, zv, dov = y_ref[:, cols], z_ref[:, cols], do_ref[:, cols]
            sz = _silu(zv)
            gg = yv * sz
            rstd = lax.rsqrt(jnp.mean(gg * gg, axis=-1, keepdims=True) + EPS)
            gh = gg * rstd
            acc_w[:, cols] += _rowsum8(dov * gh)
            dgn = dov * nw_ref[:, cols]
            dg = rstd * (dgn - gh * jnp.mean(dgn * gh, axis=-1, keepdims=True))
            dy_parts.append(dg * sz)
            dz_ref[:, cols] = (dg * yv * _dsilu(zv)).astype(dz_ref.dtype)

        act, xs, bm, cm, dtp, acs, acs_t, dt_x, ea, dec, s_scale, tril = _ssd_common(
            u_ref, dt_ref, dtb_ref, a_ref)
        xdt = xs * dt_x
        xdec = xdt * dec
        dyv = jnp.concatenate(dy_parts, axis=1)
        dye = dyv * ea
        lo, hi = _half_mask(0), _half_mask(1)
        et = _expand_mat_t()
        grp = []
        for g in range(2):
            rows = slice(g * GROUP_ROWS, (g + 1) * GROUP_ROWS)
            bg = bm[:, g * LANES:(g + 1) * LANES]
            cg = cm[:, g * LANES:(g + 1) * LANES]
            sg = st_ref[rows, :]
            dsg = dstate[rows, :]
            grp.append(dict(
                rows=rows, bg=bg, cg=cg, dsg=dsg,
                cb=_dot(cg, bg, NT), yoff=_dot(cg, sg, NT), dxst=_dot(bg, dsg, NT) * dec[:, rows],
                dc_off=_dot(dye[:, rows], sg), db_off=_dot(xdec[:, rows], dsg),
                s_next=s_scale[rows, :] * sg + _dot(xdec[:, rows], bg, TN),
                dstate_new=_dot(dye[:, rows], cg, TN) + s_scale[rows, :] * dsg))
        dy2s, g2s, l2s = [], [], []
        for pj in range(SSD_HEADS // 2):
            cols = slice(pj * LANES, (pj + 1) * LANES)
            dyp = dyv[:, cols]
            dy2 = jnp.concatenate([dyp * lo, dyp * hi], axis=0).astype(MXU_DTYPE)
            dy2s.append(dy2)
            g2s.append(_dot(dy2, xdt[:, cols], NT))
            l2s.append(jnp.concatenate([_decay_mat(acs, acs_t, 2 * pj), _decay_mat(acs, acs_t, 2 * pj + 1)], axis=0))
        dal_diag = jnp.zeros((q, LANES), F32)
        lane2 = _iota((2 * q, LANES), 1)
        row2 = _iota((2 * q, LANES), 0)
        dxdt_parts, db_parts, dc_parts = [], [], []
        end_sum = jnp.zeros((LANES, LANES), F32)
        for g in range(2):
            gd = grp[g]
            cb2 = jnp.concatenate([gd["cb"], gd["cb"]], axis=0)
            dcb = jnp.zeros((q, q), F32)
            parts = []
            for j in range(4):
                pj = g * 4 + j
                gl = g2s[pj] * l2s[pj]
                dcb = dcb + gl[:q] + gl[q:]
                m2 = cb2 * l2s[pj]
                parts.append(_dot(m2, dy2s[pj], TN))
                w2 = (gl * cb2).astype(MXU_DTYPE)
                sel2 = (lane2 == 2 * pj + (row2 >= q).astype(jnp.int32)).astype(MXU_DTYPE)
                dal_diag = dal_diag + _dot(jnp.concatenate([w2[:q], w2[q:]], axis=1), sel2) - _dot(w2, sel2, TN)
            dxdt_parts.append(jnp.concatenate(parts, axis=1) + gd["dxst"])
            dc_parts.append(_dot(dcb, gd["bg"]) + gd["dc_off"])
            db_parts.append(_dot(dcb, gd["cg"], TN) + gd["db_off"])
            end_sum = end_sum + _xdot(gd["dsg"] * gd["s_next"], et[gd["rows"], :], TN, passes=2)
            dstate[gd["rows"], :] = gd["dstate_new"]
        dxst_parts = [gd["dxst"] for gd in grp]
        yoff_parts = [gd["yoff"] for gd in grp]
        dxdt = jnp.concatenate(dxdt_parts, axis=1)
        dxv = dx_ref[...]
        yoff = jnp.concatenate(yoff_parts, axis=1) * ea
        dalpha = dal_diag + _xdot(dyv * yoff - xdt * jnp.concatenate(dxst_parts, axis=1), et)
        end_row = jnp.sum(end_sum, axis=0, keepdims=True)
        dalpha = dalpha + jnp.where(_iota((q, LANES), 0) == q - 1, end_row, 0.0)
        da = _xdot_r(tril, dalpha, TN)
        ddtp = da * a_ref[...] + _xdot(dxdt * xs, et)
        acc_a[...] += _rowsum8(da * dtp)
        acc_d[...] += _rowsum8(_xdot(dyv * xs, et))
        ddt_raw = ddtp * _sigmoid(dt_ref[...] + dtb_ref[...])
        acc_b[...] += _rowsum8(ddt_raw)
        ddt_ref[...] = ddt_raw
        dxs = dxdt * dt_x + dxv * dyv
        dact = jnp.concatenate([dxs] + db_parts + dc_parts, axis=1)
        du_ref[...] = dact * _dsilu(u_ref[...])

        @pl.when((b == nbatch - 1) & (c == nc - 1))
        def _():
            dal_ref[...] = jnp.sum(acc_a[...], axis=0, keepdims=True) * a_ref[...]
            dd_ref[...] = jnp.sum(acc_d[...], axis=0, keepdims=True)
            dtbg_ref[...] = jnp.sum(acc_b[...], axis=0, keepdims=True)
            dnw_ref[...] = jnp.sum(acc_w[...], axis=0, keepdims=True)

    def rowblk(b, c):
        return b * nc + (nc - 1 - c)

    vec = pl.BlockSpec((1, LANES), lambda b, c: (0, 0))
    wvec = pl.BlockSpec((1, SSD_WIDTH), lambda b, c: (0, 0))
    wide = pl.BlockSpec((q, SSD_WIDTH), lambda b, c: (rowblk(b, c), 0))
    zblk = pl.BlockSpec((q, SSD_WIDTH), lambda b, c: (rowblk(b, c), OFF_ZS // SSD_WIDTH))
    return pl.pallas_call(
        body, name=name,
        out_shape=(jax.ShapeDtypeStruct((t, SSD_CONV_DIM), F32), jax.ShapeDtypeStruct(dproj.shape, dproj.dtype),
                   jax.ShapeDtypeStruct((t, LANES), F32),
                   jax.ShapeDtypeStruct((1, LANES), F32), jax.ShapeDtypeStruct((1, LANES), F32),
                   jax.ShapeDtypeStruct((1, LANES), F32), jax.ShapeDtypeStruct((1, SSD_WIDTH), F32)),
        grid=(nbatch, nc),
        in_specs=[wide,
                  pl.BlockSpec((q, SSD_CONV_DIM), lambda b, c: (rowblk(b, c), 0)),
                  pl.BlockSpec((q, LANES), lambda b, c: (rowblk(b, c), OFF_DT // LANES)),
                  zblk, wide,
                  pl.BlockSpec((SSD_WIDTH, SSD_STATE), lambda b, c: (rowblk(b, c), 0)),
                  vec, vec, wvec, wvec, ANY],
        out_specs=(pl.BlockSpec((q, SSD_CONV_DIM), lambda b, c: (rowblk(b, c), 0)),
                   zblk,
                   pl.BlockSpec((q, LANES), lambda b, c: (rowblk(b, c), 0)),
                   vec, vec, vec, wvec),
        input_output_aliases={10: 1},
        scratch_shapes=[pltpu.VMEM((SSD_WIDTH, SSD_STATE), F32), pltpu.VMEM((SUBLANES, LANES), F32),
                        pltpu.VMEM((SUBLANES, LANES), F32), pltpu.VMEM((SUBLANES, LANES), F32),
                        pltpu.VMEM((SUBLANES, SSD_WIDTH), F32)],
        compiler_params=_params(("arbitrary", "arbitrary")),
    )(dycat, u, proj, proj, y, states, dtb, a_neg, d_x, norm_w, dproj)


def _pad_rows(w, rows):
    return jnp.concatenate([w, jnp.zeros((rows - w.shape[0], w.shape[1]), w.dtype)], axis=0)


def _pad_lanes(v):
    return jnp.concatenate([v, jnp.zeros((LANES - v.shape[0],), v.dtype)]).reshape(1, LANES)


def _padded_from_chips(pieces):
    cols = pieces[0].shape[-1]
    lead = pieces[0].shape[:-1]
    parts, pos = [], 0
    for lo, hi, start in sorted(SECTIONS, key=lambda s: s[2]):
        if start > pos:
            parts.append(jnp.zeros(lead + (start - pos,), pieces[0].dtype))
        pos = start + hi - lo
        while lo < hi:
            p = lo // cols
            end = min(hi, (p + 1) * cols)
            parts.append(pieces[p][..., lo - p * cols:end - p * cols])
            lo = end
    if pos < NP:
        parts.append(jnp.zeros(lead + (NP - pos,), pieces[0].dtype))
    return jnp.concatenate(parts, axis=-1)


def _chip_part_from_padded(wp, p, cols):
    lo, hi = p * cols, (p + 1) * cols
    parts = []
    for rs, re, start in SECTIONS:
        a, b = max(lo, rs), min(hi, re)
        if a < b:
            parts.append(wp[..., start + a - rs:start + b - rs])
    return jnp.concatenate(parts, axis=-1)


def _layer_params(li, w_in_p, w_out, conv_w, dw_w, small):
    return dict(
        w_in_p=w_in_p, w_out=w_out,
        conv_w=_pad_rows(conv_w, SUBLANES), dw_w=_pad_rows(dw_w, 32),
        norm_w=small["norm_w"][li].reshape(1, -1),
        conv_b=small["ssd_conv_b"][li].reshape(1, -1),
        dtb=_pad_lanes(small["ssd_dt_bias"][li]),
        a_neg=_pad_lanes(-jnp.exp(small["ssd_a_log"][li])),
        d_x=jnp.repeat(small["ssd_d"][li], SSD_HEAD_DIM).reshape(1, -1),
        ssd_norm_w=small["ssd_norm_w"][li].reshape(1, -1),
        sinks=_pad_lanes(small["attn_sinks"][li]),
        dw_b=small["conf_dw_b"][li].reshape(1, -1),
        ln_w=small["conf_ln_w"][li].reshape(1, -1),
        ln_b=small["conf_ln_b"][li].reshape(1, -1),
    )


def _layer_fwd(x, p, nbatch, seq, tag, after=None):
    proj, h_t = _proj_fwd(x, p["norm_w"], p["w_in_p"], name=f"proj_fwd_{tag}", after=after)
    u = _conv_fwd(proj, OFF_XBC, SSD_CONV_DIM, p["conv_w"], p["conv_b"], SSD_CONV, seq, name=f"ssd_conv_fwd_{tag}")
    ycat = lax.empty((x.shape[0], MIX_WIDTH), MXU_DTYPE)
    y, states, ycat = _ssd_fwd(u, proj, p["dtb"], p["a_neg"], p["d_x"], p["ssd_norm_w"], ycat, nbatch,
                               name=f"ssd_fwd_{tag}")
    ycat, o, lse = _attn_fwd(proj, p["sinks"], ycat, nbatch, name=f"attn_fwd_{tag}")
    c1, ycat = _conf_fwd(proj, p["dw_w"], p["dw_b"], p["ln_w"], p["ln_b"], ycat, seq, name=f"conf_fwd_{tag}")
    w_out = p["w_out"](ycat) if callable(p["w_out"]) else p["w_out"]
    x_new = _matmul(ycat, w_out, "nn", F32, 1024, 512, 2048, name=f"out_fwd_{tag}", residual=x)
    return x_new, dict(x=x, w_out=w_out, h_t=h_t, proj=proj, u=u, y=y, states=states, o=o, lse=lse, c1=c1, ycat=ycat)


def _layer_bwd(dx_out, p, s, nbatch, seq, tag, hooks=None):
    hooks = hooks or {}
    proj = s["proj"]
    dycat = _matmul(dx_out, s["w_out"], "nt", F32, 1024, 1024, 1024, name=f"out_bwd_dy_{tag}",
                    after=hooks.get("start_token"))
    dw_out = _matmul(s["ycat"], dx_out, "tn", F32, 1024, 1024, 1024, name=f"out_bwd_dw_{tag}")
    token = hooks["after_dycat"](dycat) if "after_dycat" in hooks else None
    dtb = p["dtb"] if token is None else p["dtb"] + token[0, 0]
    dproj = lax.empty(proj.shape, MXU_DTYPE)
    du, dproj, ddt, da_log, dd, ddtb, dssd_norm_w = _ssd_bwd(
        dycat, s["u"], proj, s["y"], s["states"], dtb, p["a_neg"], p["d_x"], p["ssd_norm_w"], dproj,
        nbatch, name=f"ssd_bwd_{tag}")
    dproj, dconv_w, dconv_b = _conv_bwd(du, proj, OFF_XBC, SSD_CONV_DIM, p["conv_w"], SSD_CONV, seq,
                                        name=f"ssd_conv_bwd_{tag}", into=dproj)
    dproj, dsinks = _attn_bwd(dycat, proj, s["o"], s["lse"], p["sinks"], ddt, dproj, nbatch,
                              name=f"attn_bwd_{tag}")
    if "after_attn" in hooks:
        hooks["after_attn"](dproj)
    dproj, ddw_w, ddw_b, dln_w, dln_b = _conf_bwd(dycat, proj, s["c1"], p["dw_w"], p["ln_w"], p["ln_b"], dproj, seq,
                                                  name=f"conf_bwd_{tag}")
    dw_in_p = _matmul(s["h_t"], dproj, "nn", F32, 1024, 512, 4096, name=f"proj_bwd_dw_{tag}")
    token = hooks["after_dw"](dw_in_p, dw_out) if "after_dw" in hooks else None
    norm_w = p["norm_w"] if token is None else p["norm_w"] + token[0, 0]
    dx_in, dnorm_w = _proj_bwd_dx(dproj, p["w_in_p"], s["x"], norm_w, dx_out, name=f"proj_bwd_dx_{tag}")
    grads = dict(
        norm_w=dnorm_w[0], w_in_p=dw_in_p, ssd_conv_w=dconv_w[:SSD_CONV], ssd_conv_b=dconv_b[0],
        ssd_dt_bias=ddtb[0, :SSD_HEADS], ssd_a_log=da_log[0, :SSD_HEADS], ssd_d=dd[0, :SSD_HEADS],
        ssd_norm_w=dssd_norm_w[0], attn_sinks=dsinks[0, :ATTN_Q_HEADS], conf_dw_w=ddw_w[:CONF_KERNEL],
        conf_dw_b=ddw_b[0], conf_ln_w=dln_w[0], conf_ln_b=dln_b[0], w_out=dw_out)
    return dx_in, grads


def _local_step(x, target, param_fns, final_norm_w, first_after=None, bwd_hooks=None):
    nbatch, seq, d = x.shape
    xt = x.reshape(nbatch * seq, d)
    saved, layer_params = [], []
    for li, fn in enumerate(param_fns):
        p = fn(xt)
        layer_params.append(p)
        xt, s = _layer_fwd(xt, p, nbatch, seq, f"l{li}", after=first_after if li == 0 else None)
        saved.append(s)
    loss, dx, dfinal = _loss_head(xt, target.reshape(nbatch * seq, d), final_norm_w.reshape(1, d), name="loss_head")
    grads = [None] * len(layer_params)
    for li in reversed(range(len(layer_params))):
        hooks = bwd_hooks(li) if bwd_hooks is not None else None
        dx, grads[li] = _layer_bwd(dx, layer_params[li], saved[li], nbatch, seq, f"l{li}", hooks=hooks)
    return loss[0, 0], dx.reshape(nbatch, seq, d), grads, dfinal[0]


MESH = pl.DeviceIdType.MESH
N_CHIPS = 4


def _mesh_pos():
    return lax.axis_index("x"), lax.axis_index("y"), lax.axis_index("c")


def _other_chips(x, y):
    return [(1 - x, y), (x, 1 - y), (1 - x, 1 - y)]


def _gather_weights(big, small, name):
    nbig, nsmall = len(big), len(small)
    n_ici = 3 * (nbig + nsmall)
    n_fwd = 3 * nbig

    def body(*refs):
        ins = refs[:nbig + nsmall]
        outs = refs[nbig + nsmall:2 * (nbig + nsmall)]
        send_sems, recv_sems = refs[2 * (nbig + nsmall):]
        x, y, c = _mesh_pos()
        me = 2 * x + y
        sibling = (x, y, 1 - c)
        chips = _other_chips(x, y)

        def ici(a, j, origin, dest):
            if a < nbig:
                src = ins[a].at[c] if origin is None else outs[a].at[origin, c]
                dst = outs[a].at[me if origin is None else origin, c]
            else:
                src = ins[a] if origin is None else outs[a].at[origin]
                dst = outs[a].at[me if origin is None else origin]
            k = a * 3 + j
            return pltpu.make_async_remote_copy(src_ref=src, dst_ref=dst, send_sem=send_sems.at[k],
                                                recv_sem=recv_sems.at[k], device_id=dest, device_id_type=MESH)

        def fwd(a, j, origin, half):
            k = n_ici + a * 3 + j
            ref = outs[a].at[origin, half]
            return pltpu.make_async_remote_copy(src_ref=ref, dst_ref=ref, send_sem=send_sems.at[k],
                                                recv_sem=recv_sems.at[k], device_id=sibling, device_id_type=MESH)

        sends = []
        for j, (px, py) in enumerate(chips):
            for a in range(nbig + nsmall):
                cp = ici(a, j, None, (px, py, c))
                cp.start()
                sends.append(cp)
        for j, (px, py) in enumerate(chips):
            origin = 2 * px + py
            for a in range(nbig):
                ici(a, j, origin, (px, py, c)).wait_recv()
                cp = fwd(a, j, origin, c)
                cp.start()
                sends.append(cp)
        for j, (px, py) in enumerate(chips):
            origin = 2 * px + py
            for a in range(nbig, nbig + nsmall):
                ici(a, j, origin, (px, py, c)).wait_recv()
            for a in range(nbig):
                fwd(a, j, origin, 1 - c).wait_recv()
        for cp in sends:
            cp.wait_send()

    out_shape = tuple(jax.ShapeDtypeStruct((N_CHIPS,) + a.shape, a.dtype) for a in list(big) + list(small))
    return pl.pallas_call(
        body, name=name, out_shape=out_shape,
        in_specs=[ANY] * (nbig + nsmall), out_specs=tuple([ANY] * (nbig + nsmall)),
        scratch_shapes=[pltpu.SemaphoreType.DMA((n_ici + n_fwd,)), pltpu.SemaphoreType.DMA((n_ici + n_fwd,))],
    )(*big, *small)


def _pair_swap_halves(arrs, name):
    n = len(arrs)

    def body(*refs):
        ins, outs = refs[:n], refs[n:2 * n]
        send_sems, recv_sems = refs[2 * n:]
        x, y, c = _mesh_pos()
        cps = [pltpu.make_async_remote_copy(src_ref=ins[a].at[:, 1 - c], dst_ref=outs[a], send_sem=send_sems.at[a],
                                            recv_sem=recv_sems.at[a], device_id=(x, y, 1 - c), device_id_type=MESH)
               for a in range(n)]
        for cp in cps:
            cp.start()
        for cp in cps:
            cp.wait()

    return pl.pallas_call(
        body, name=name,
        out_shape=tuple(jax.ShapeDtypeStruct(a.shape[:1] + a.shape[2:], a.dtype) for a in arrs),
        in_specs=[ANY] * n, out_specs=tuple([ANY] * n),
        scratch_shapes=[pltpu.SemaphoreType.DMA((n,)), pltpu.SemaphoreType.DMA((n,))],
    )(*arrs)


HBM = pl.BlockSpec(memory_space=pltpu.HBM)
SEM = pl.BlockSpec(memory_space=pltpu.SEMAPHORE)
DATAFLOW = pltpu.SideEffectType.DATAFLOW_SIDE_EFFECTING


def _split_peers(pattern, x, y, c):
    if pattern == "swap":
        return [((x, y, 1 - c), 1 - c, None, None)]
    me = 2 * x + y
    return [((px, py, c), 2 * px + py if pattern == "scatter" else None, me, 2 * px + py)
            for px, py in _other_chips(x, y)]


def _split_land_shape(pattern, shape):
    return {"bcast": (N_CHIPS,) + shape, "scatter": shape, "swap": shape[:1] + shape[2:]}[pattern]


def _split_copies(pattern, srcs, lands, send_sems, recv_sems, waiting):
    x, y, c = _mesh_pos()
    peers = _split_peers(pattern, x, y, c)
    cps = []
    for j, (dev, src_slot, dst_slot, my_slot) in enumerate(peers):
        for a in range(len(srcs)):
            if src_slot is None:
                src = srcs[a]
            else:
                src = srcs[a].at[:, src_slot] if pattern == "swap" else srcs[a].at[src_slot]
            slot = my_slot if waiting else dst_slot
            dst = lands[a] if slot is None else lands[a].at[slot]
            k = a * len(peers) + j
            cps.append(pltpu.make_async_remote_copy(src_ref=src, dst_ref=dst, send_sem=send_sems[k],
                                                    recv_sem=recv_sems[k], device_id=dev, device_id_type=MESH))
    return cps


def _split_start(arrs, pattern, after, name):
    n = len(arrs)
    nsem = n * (1 if pattern == "swap" else N_CHIPS - 1)

    def body(*refs):
        srcs, lands = refs[:n], refs[n:2 * n]
        outs = refs[2 * n + 1:]
        for cp in _split_copies(pattern, srcs, lands, outs[:nsem], outs[nsem:2 * nsem], waiting=False):
            cp.start()
        outs[-1][...] = jnp.zeros_like(outs[-1])

    lands = [lax.empty(_split_land_shape(pattern, a.shape), a.dtype) for a in arrs]
    out_shape = ([pltpu.SemaphoreType.DMA(())] * (2 * nsem)
                 + [pltpu.HBM(a.shape, a.dtype) for a in arrs] + [pltpu.HBM(b.shape, b.dtype) for b in lands]
                 + [jax.ShapeDtypeStruct((SUBLANES, LANES), F32)])
    outs = pl.pallas_call(
        body, name=name, out_shape=tuple(out_shape),
        in_specs=[HBM] * (2 * n) + [ANY],
        out_specs=tuple([SEM] * (2 * nsem) + [HBM] * (2 * n) + [pl.BlockSpec(memory_space=pltpu.VMEM)]),
        input_output_aliases={a: 2 * nsem + a for a in range(2 * n)},
        compiler_params=pltpu.CompilerParams(has_side_effects=DATAFLOW),
    )(*[pltpu.with_memory_space_constraint(a, pltpu.HBM) for a in list(arrs) + lands], after)
    return outs[:-1], outs[-1]


def _split_wait(state, n, pattern, after, name):
    nsem = n * (1 if pattern == "swap" else N_CHIPS - 1)

    def body(*refs):
        srcs, lands = refs[:n], refs[n:2 * n]
        send_sems, recv_sems = refs[2 * n:2 * n + nsem], refs[2 * n + nsem:2 * n + 2 * nsem]
        for cp in _split_copies(pattern, srcs, lands, send_sems, recv_sems, waiting=True):
            cp.wait_send()
            cp.wait_recv()

    sems, thru = state[:2 * nsem], state[2 * nsem:]
    outs = pl.pallas_call(
        body, name=name, out_shape=tuple(pltpu.HBM(a.shape, a.dtype) for a in thru),
        in_specs=[HBM] * (2 * n) + [SEM] * (2 * nsem) + [ANY],
        out_specs=tuple([HBM] * (2 * n)),
        input_output_aliases={a: a for a in range(2 * n)},
        compiler_params=pltpu.CompilerParams(has_side_effects=DATAFLOW),
    )(*thru, *sems, after)
    return outs[n:]


def _pair_gather(arrs, layer, name):
    n = len(arrs)

    def body(*refs):
        outs = refs[n:2 * n]
        send_sems, recv_sems = refs[2 * n:]
        x, y, c = _mesh_pos()
        cps = [pltpu.make_async_remote_copy(src_ref=outs[a].at[layer, c], dst_ref=outs[a].at[layer, c],
                                            send_sem=send_sems.at[a], recv_sem=recv_sems.at[a],
                                            device_id=(x, y, 1 - c), device_id_type=MESH)
               for a in range(n)]
        for cp in cps:
            cp.start()
        for cp in cps:
            cp.wait()

    return pl.pallas_call(
        body, name=name, out_shape=tuple(jax.ShapeDtypeStruct(a.shape, a.dtype) for a in arrs),
        in_specs=[ANY] * n, out_specs=tuple([ANY] * n),
        input_output_aliases={a: a for a in range(n)},
        scratch_shapes=[pltpu.SemaphoreType.DMA((n,)), pltpu.SemaphoreType.DMA((n,))],
    )(*arrs)


N_DEV = 8


def _allreduce_small(pack, name):
    r = pack.shape[0]

    def body(p_ref, o_ref, land, send_sems, recv_sems):
        x, y, c = _mesh_pos()
        me = 4 * x + 2 * y + c
        cps = []
        for k in range(1, N_DEV):
            peer = (x ^ (k >> 2), y ^ ((k >> 1) & 1), c ^ (k & 1))
            cps.append(pltpu.make_async_remote_copy(src_ref=p_ref, dst_ref=land.at[me], send_sem=send_sems.at[k - 1],
                                                    recv_sem=recv_sems.at[k - 1], device_id=peer, device_id_type=MESH))
        for cp in cps:
            cp.start()
        land[me] = p_ref[...]
        for cp in cps:
            cp.wait()
        total = land[0]
        for d in range(1, N_DEV):
            total = total + land[d]
        o_ref[...] = total

    vm = pl.BlockSpec(memory_space=pltpu.VMEM)
    return pl.pallas_call(
        body, name=name, out_shape=jax.ShapeDtypeStruct(pack.shape, F32),
        in_specs=[vm], out_specs=vm,
        scratch_shapes=[pltpu.VMEM((N_DEV, r, LANES), F32), pltpu.SemaphoreType.DMA((N_DEV - 1,)),
                        pltpu.SemaphoreType.DMA((N_DEV - 1,))],
    )(pack)


BIG_ROWS = 128


def _cast_layer(w, layer, name):
    _, r, cdim = w.shape
    tr = BIG_ROWS

    def body(w_ref, o_ref):
        o_ref[...] = w_ref[...].astype(o_ref.dtype)

    return pl.pallas_call(
        body, name=name, out_shape=jax.ShapeDtypeStruct((r, cdim), MXU_DTYPE),
        grid=(r // tr,), in_specs=[pl.BlockSpec((None, tr, cdim), lambda i: (layer, i, 0))],
        out_specs=pl.BlockSpec((tr, cdim), lambda i: (i, 0)),
        compiler_params=_params(("parallel",)),
    )(w)


def _cast_cols_major(w_t, name):
    cdim, nl, r = w_t.shape
    tc = LANES

    def body(w_ref, *o_refs):
        for l in range(nl):
            o_refs[l][...] = w_ref[:, l, :].T.astype(o_refs[l].dtype)

    out = pl.BlockSpec((r, tc), lambda i: (0, i))
    return pl.pallas_call(
        body, name=name, out_shape=tuple(jax.ShapeDtypeStruct((r, cdim), MXU_DTYPE) for _ in range(nl)),
        grid=(pl.cdiv(cdim, tc),), in_specs=[pl.BlockSpec((tc, nl, r), lambda i: (i, 0, 0))],
        out_specs=tuple([out] * nl),
        compiler_params=_params(("parallel",)),
    )(w_t)


def _pair_sum(parts, sib, which, out_dtype, name):
    k, _, r, cdim = parts.shape
    tr = BIG_ROWS

    def body(sel_ref, p_ref, s_ref, o_ref):
        o_ref[...] = (p_ref[...] + s_ref[...]).astype(o_ref.dtype)

    grid_spec = pltpu.PrefetchScalarGridSpec(
        num_scalar_prefetch=1, grid=(k, r // tr),
        in_specs=[pl.BlockSpec((None, None, tr, cdim), lambda l, i, sel: (l, sel[0], i, 0)),
                  pl.BlockSpec((None, tr, cdim), lambda l, i, sel: (l, i, 0))],
        out_specs=pl.BlockSpec((None, tr, cdim), lambda l, i, sel: (l, i, 0)))
    return pl.pallas_call(
        body, name=name, out_shape=jax.ShapeDtypeStruct((k, r, cdim), out_dtype), grid_spec=grid_spec,
        compiler_params=_params(("parallel", "parallel")),
    )(which.reshape(1).astype(jnp.int32), parts, sib)


def _sum_lead(parts, into, layer, which, name):
    k, r, cdim = parts.shape
    tr = BIG_ROWS

    def body(sel_ref, p_ref, _, o_ref):
        total = p_ref[0].astype(F32)
        for a in range(1, k):
            total = total + p_ref[a].astype(F32)
        o_ref[...] = total

    grid_spec = pltpu.PrefetchScalarGridSpec(
        num_scalar_prefetch=1, grid=(r // tr,),
        in_specs=[pl.BlockSpec((k, tr, cdim), lambda i, sel: (0, i, 0)), ANY],
        out_specs=pl.BlockSpec((None, None, tr, cdim), lambda i, sel: (layer, sel[0], i, 0)))
    return pl.pallas_call(
        body, name=name, out_shape=jax.ShapeDtypeStruct(into.shape, F32), grid_spec=grid_spec,
        input_output_aliases={2: 0},
        compiler_params=_params(("parallel",)),
    )(which.reshape(1).astype(jnp.int32), parts, into)


def _adam_math(w, g, m, v):
    m2 = ADAM_B1 * m + (1.0 - ADAM_B1) * g
    v2 = ADAM_B2 * v + (1.0 - ADAM_B2) * (g * g)
    m_hat = m2 / (1.0 - ADAM_B1 ** ADAM_STEP)
    v_hat = v2 / (1.0 - ADAM_B2 ** ADAM_STEP)
    delta = -ADAM_LR * (m_hat / (jnp.sqrt(v_hat) + ADAM_EPS) + ADAM_WD * w)
    return delta, m2, v2


def _adam_big(w, g, m, v, name):
    nl, r, cdim = w.shape
    tr = BIG_ROWS

    def body(w_ref, g_ref, m_ref, v_ref, d_ref, mo_ref, vo_ref):
        delta, m2, v2 = _adam_math(w_ref[...], g_ref[...], m_ref[...], v_ref[...])
        d_ref[...] = delta
        mo_ref[...] = m2
        vo_ref[...] = v2

    blk = pl.BlockSpec((None, tr, cdim), lambda l, i: (l, i, 0))
    shp = jax.ShapeDtypeStruct(w.shape, F32)
    return pl.pallas_call(
        body, name=name, out_shape=(shp, shp, shp),
        grid=(nl, r // tr), in_specs=[blk] * 4, out_specs=(blk, blk, blk),
        compiler_params=_params(("parallel", "parallel")),
    )(w, g, m, v)


def _adam_cols_major(w, g, m, v, name):
    cdim, nl, r = w.shape
    tc = BIG_ROWS

    def body(w_ref, g_ref, m_ref, v_ref, d_ref, mo_ref, vo_ref):
        delta, m2, v2 = _adam_math(w_ref[...], g_ref[...], m_ref[...], v_ref[...])
        d_ref[...] = delta
        mo_ref[...] = m2
        vo_ref[...] = v2

    blk = pl.BlockSpec((tc, nl, r), lambda i: (i, 0, 0))
    shp = jax.ShapeDtypeStruct(w.shape, F32)
    return pl.pallas_call(
        body, name=name, out_shape=(shp, shp, shp),
        grid=(pl.cdiv(cdim, tc),), in_specs=[blk] * 4, out_specs=(blk, blk, blk),
        compiler_params=_params(("parallel",)),
    )(w, g, m, v)


def _adam_small(ws, gs, ms, vs, name):
    n = len(ws)

    def body(*refs):
        w_refs, g_refs, m_refs, v_refs = (refs[k * n:(k + 1) * n] for k in range(4))
        d_refs, mo_refs, vo_refs = (refs[(4 + k) * n:(5 + k) * n] for k in range(3))
        for a in range(n):
            delta, m2, v2 = _adam_math(w_refs[a][...], g_refs[a][...], m_refs[a][...], v_refs[a][...])
            d_refs[a][...] = delta
            mo_refs[a][...] = m2
            vo_refs[a][...] = v2

    shapes = tuple(jax.ShapeDtypeStruct(w.shape, F32) for w in ws)
    vm = pl.BlockSpec(memory_space=pltpu.VMEM)
    outs = pl.pallas_call(body, name=name, out_shape=shapes * 3, in_specs=[vm] * (4 * n),
                          out_specs=tuple([vm] * (3 * n)))(*ws, *gs, *ms, *vs)
    return outs[:n], outs[n:2 * n], outs[2 * n:]


PACK_TILE = SUBLANES * LANES


def _pack(arrays):
    rows = []
    for a in arrays:
        flat = a.reshape(-1)
        pad = (-flat.shape[0]) % PACK_TILE
        if pad:
            flat = jnp.concatenate([flat, jnp.zeros((pad,), flat.dtype)])
        rows.append(flat.reshape(-1, LANES))
    return jnp.concatenate(rows, axis=0)


def _unpack(pack, shapes):
    outs, row = [], 0
    for shp in shapes:
        n = int(np.prod(shp))
        nrows = -(-n // PACK_TILE) * SUBLANES
        outs.append(pack[row:row + nrows].reshape(-1)[:n].reshape(shp))
        row += nrows
    return outs


SMALL = ["norm_w", "ssd_conv_b", "ssd_dt_bias", "ssd_a_log", "ssd_d", "ssd_norm_w", "attn_sinks",
         "conf_dw_b", "conf_ln_w", "conf_ln_b"]
WEIGHTS = ["norm_w", "w_in", "ssd_conv_w", "ssd_conv_b", "ssd_dt_bias", "ssd_a_log", "ssd_d", "ssd_norm_w",
           "attn_sinks", "conf_dw_w", "conf_dw_b", "conf_ln_w", "conf_ln_b", "w_out", "final_norm_w"]


def kernel(x, norm_w, w_in, ssd_conv_w, ssd_conv_b, ssd_dt_bias, ssd_a_log, ssd_d, ssd_norm_w, attn_sinks, conf_dw_w, conf_dw_b, conf_ln_w, conf_ln_b, w_out, final_norm_w, loss_target, m_norm_w, m_w_in, m_ssd_conv_w, m_ssd_conv_b, m_ssd_dt_bias, m_ssd_a_log, m_ssd_d, m_ssd_norm_w, m_attn_sinks, m_conf_dw_w, m_conf_dw_b, m_conf_ln_w, m_conf_ln_b, m_w_out, m_final_norm_w, v_norm_w, v_w_in, v_ssd_conv_w, v_ssd_conv_b, v_ssd_dt_bias, v_ssd_a_log, v_ssd_d, v_ssd_norm_w, v_attn_sinks, v_conf_dw_w, v_conf_dw_b, v_conf_ln_w, v_conf_ln_b, v_w_out, v_final_norm_w):
    w = dict(norm_w=norm_w, w_in=w_in, ssd_conv_w=ssd_conv_w, ssd_conv_b=ssd_conv_b, ssd_dt_bias=ssd_dt_bias,
             ssd_a_log=ssd_a_log, ssd_d=ssd_d, ssd_norm_w=ssd_norm_w, attn_sinks=attn_sinks, conf_dw_w=conf_dw_w,
             conf_dw_b=conf_dw_b, conf_ln_w=conf_ln_w, conf_ln_b=conf_ln_b, w_out=w_out, final_norm_w=final_norm_w)
    m = dict(norm_w=m_norm_w, w_in=m_w_in, ssd_conv_w=m_ssd_conv_w, ssd_conv_b=m_ssd_conv_b,
             ssd_dt_bias=m_ssd_dt_bias, ssd_a_log=m_ssd_a_log, ssd_d=m_ssd_d, ssd_norm_w=m_ssd_norm_w,
             attn_sinks=m_attn_sinks, conf_dw_w=m_conf_dw_w, conf_dw_b=m_conf_dw_b, conf_ln_w=m_conf_ln_w,
             conf_ln_b=m_conf_ln_b, w_out=m_w_out, final_norm_w=m_final_norm_w)
    v = dict(norm_w=v_norm_w, w_in=v_w_in, ssd_conv_w=v_ssd_conv_w, ssd_conv_b=v_ssd_conv_b,
             ssd_dt_bias=v_ssd_dt_bias, ssd_a_log=v_ssd_a_log, ssd_d=v_ssd_d, ssd_norm_w=v_ssd_norm_w,
             attn_sinks=v_attn_sinks, conf_dw_w=v_conf_dw_w, conf_dw_b=v_conf_dw_b, conf_ln_w=v_conf_ln_w,
             conf_ln_b=v_conf_ln_b, w_out=v_w_out, final_norm_w=v_final_norm_w)
    depth = w_in.shape[0]
    me = 2 * lax.axis_index("x") + lax.axis_index("y")

    assert depth == 2
    w_in_t = jnp.transpose(w_in, (2, 0, 1))
    w_in_b = _cast_cols_major(w_in_t, name="cast_w_in")
    w_out_b = [_cast_layer(w_out, li, name=f"cast_w_out_l{li}") for li in range(depth)]
    own0 = [w_in_b[0].reshape((2, -1) + w_in_b[0].shape[1:]), ssd_conv_w, conf_dw_w]
    gathered0 = _gather_weights(own0[:1], own0[1:], name="gather_weights_l0")
    g_in0, g_conv, g_dw = [lax.dynamic_update_index_in_dim(g_all, mine, me, 0)
                           for g_all, mine in zip(gathered0, own0)]
    own1 = [w_out_b[0], w_in_b[1], w_out_b[1]]
    pending1, token1 = _split_start(own1, "bcast", gathered0[0], name="gather_rest_start")
    rest = {}

    def small_full(li):
        return (jnp.concatenate([g_conv[p, li] for p in range(N_CHIPS)], axis=1),
                jnp.concatenate([g_dw[p, li] for p in range(N_CHIPS)], axis=1))

    def w_out_l0(after):
        landed = _split_wait(pending1, len(own1), "bcast", after, name="gather_rest_wait")
        rest["landed"] = [lax.dynamic_update_index_in_dim(g_all, mine, me, 0) for g_all, mine in zip(landed, own1)]
        return rest["landed"][0].reshape(-1, w_out.shape[2])

    def params_l0(_):
        w_in_p = _padded_from_chips([g_in0[p].reshape(w_in_b[0].shape) for p in range(N_CHIPS)])
        return _layer_params(0, w_in_p, w_out_l0, *small_full(0), w)

    def params_l1(_):
        _, g_in1, g_out1 = rest["landed"]
        w_in_p = _padded_from_chips([g_in1[p] for p in range(N_CHIPS)])
        return _layer_params(1, w_in_p, g_out1.reshape(-1, g_out1.shape[-1]), *small_full(1), w)

    c = lax.axis_index("c")
    cols = w_in.shape[2]
    rows_out = w_out.shape[1]

    def grad_parts(g):
        dw = g["w_in_p"]
        p_in = jnp.stack([_chip_part_from_padded(dw, p, cols) for p in range(N_CHIPS)])
        return [p_in.reshape(N_CHIPS, 2, dw.shape[0] // 2, cols),
                g["w_out"].reshape(N_CHIPS, 2, rows_out // 2, D_MODEL)]

    def pair_sums(parts, sib, tag):
        return [_pair_sum(p, sb, c, MXU_DTYPE, name=f"grad_pair_sum_{k}_{tag}")
                for k, (p, sb) in enumerate(zip(parts, sib))]

    split = {"reduced": [lax.empty((depth, 2, w_in.shape[1] // 2, cols), F32),
                         lax.empty((depth, 2, rows_out // 2, D_MODEL), F32)]}

    def chip_sums(landed, sent, li):
        filled = [lax.dynamic_update_index_in_dim(r, lax.dynamic_index_in_dim(sk, me, 0, keepdims=False), me, 0)
                  for r, sk in zip(landed, sent)]
        halves = [_sum_lead(r, into, li, c, name=f"grad_chip_sum_{k}_l{li}")
                  for k, (r, into) in enumerate(zip(filled, split["reduced"]))]
        split["reduced"] = list(_pair_gather(halves, li, name=f"grad_pair_gather_l{li}"))

    def bwd_hooks(li):
        def after_dw(dw_in_p, dw_out):
            parts = grad_parts(dict(w_in_p=dw_in_p, w_out=dw_out))
            state, token = _split_start(parts, "swap", dw_out, name=f"grad_swap_l{li}_start")
            split[f"swap{li}"] = (parts, state)
            return token

        hooks = {"after_dw": after_dw}
        if li == depth - 2:
            parts, swap_state = split[f"swap{depth - 1}"]

            def after_dycat(dycat):
                sib = _split_wait(swap_state, len(parts), "swap", dycat, name="grad_swap_l1_wait")
                split["sent"] = pair_sums(parts, sib, "l1")
                split["scatter"], token = _split_start(split["sent"], "scatter", split["sent"][0],
                                                       name="grad_scatter_l1_start")
                return token

            def after_attn(dproj):
                landed = _split_wait(split["scatter"], len(parts), "scatter", dproj, name="grad_scatter_l1_wait")
                chip_sums(landed, split["sent"], depth - 1)

            hooks.update(after_dycat=after_dycat, after_attn=after_attn)
        return hooks

    loss, grad_x, grads, dfinal = _local_step(x, loss_target, [params_l0, params_l1], final_norm_w,
                                              first_after=token1, bwd_hooks=bwd_hooks)

    parts0, swap0 = split["swap0"]
    sent0 = pair_sums(parts0, _split_wait(swap0, len(parts0), "swap", grad_x, name="grad_swap_l0_wait"), "l0")
    scatter0, token0 = _split_start(sent0, "scatter", sent0[0], name="grad_scatter_l0_start")

    small_list = [grads[li][n] for li in range(depth) for n in SMALL]
    small_list += [grads[li][n] for li in range(depth) for n in ("ssd_conv_w", "conf_dw_w")]
    small_list += [dfinal, loss.reshape(1)]
    small_shapes = [a.shape for a in small_list]
    reduced = _unpack(_allreduce_small(_pack(small_list) + token0[0, 0], name="allreduce_small"), small_shapes)
    ns = len(SMALL)
    g = {n: jnp.stack([reduced[li * ns + i] for li in range(depth)]) for i, n in enumerate(SMALL)}
    conv_w_cols, dw_w_cols = ssd_conv_w.shape[2], conf_dw_w.shape[2]
    g["ssd_conv_w"] = jnp.stack([lax.dynamic_slice_in_dim(reduced[depth * ns + 2 * li], me * conv_w_cols,
                                                          conv_w_cols, axis=1) for li in range(depth)])
    g["conf_dw_w"] = jnp.stack([lax.dynamic_slice_in_dim(reduced[depth * ns + 2 * li + 1], me * dw_w_cols,
                                                         dw_w_cols, axis=1) for li in range(depth)])
    g["final_norm_w"] = reduced[-2]
    loss_total = reduced[-1][0]

    small_names = [n for n in WEIGHTS if n not in ("w_in", "w_out")]

    def as2d(a):
        return a.reshape(1, -1) if a.ndim == 1 else a

    deltas, new_ms, new_vs = _adam_small(*[[as2d(src[n]) for n in small_names] for src in (w, g, m, v)],
                                         name="adam_small")

    chip_sums(_split_wait(scatter0, len(sent0), "scatter", deltas[0], name="grad_scatter_l0_wait"), sent0, 0)
    g_w_in = split["reduced"][0].reshape(w_in.shape)
    g_w_out = split["reduced"][1].reshape(w_out.shape)

    outs_g, outs_d, outs_m, outs_v = {"w_in": g_w_in, "w_out": g_w_out}, {}, {}, {}
    to_cols, from_cols = (2, 0, 1), (1, 2, 0)
    outs_d["w_in"], outs_m["w_in"], outs_v["w_in"] = [
        jnp.transpose(a, from_cols) for a in _adam_cols_major(
            *[jnp.transpose(a, to_cols) for a in (w_in, g_w_in, m_w_in, v_w_in)], name="adam_w_in")]
    outs_d["w_out"], outs_m["w_out"], outs_v["w_out"] = _adam_big(w_out, g_w_out, m_w_out, v_w_out,
                                                                  name="adam_w_out")
    for n, dn, mn, vn in zip(small_names, deltas, new_ms, new_vs):
        outs_g[n], outs_d[n], outs_m[n], outs_v[n] = (g[n], dn.reshape(w[n].shape), mn.reshape(w[n].shape),
                                                      vn.reshape(w[n].shape))
    return (loss_total, grad_x, *[outs_g[n] for n in WEIGHTS], *[outs_d[n] for n in WEIGHTS],
            *[outs_m[n] for n in WEIGHTS], *[outs_v[n] for n in WEIGHTS])
```

```python
import functools
import math

import jax
import jax.numpy as jnp
import numpy as np
from jax import lax
from jax.experimental import pallas as pl
from jax.experimental.pallas import tpu as pltpu

F32 = jnp.float32
BF16 = jnp.bfloat16
MXU_DTYPE = BF16

D_MODEL = 1024
DEPTH = 2
SSD_HEADS = 16
SSD_HEAD_DIM = 64
SSD_STATE = 128
SSD_CONV = 4
CHUNK = 128
SSD_CONV_DIM = 1536
ATTN_HEAD_DIM = 64
ATTN_Q_HEADS = 8
WINDOW = 128
CONF_WIDTH = 512
CONF_KERNEL = 31
MIX_WIDTH = 2048
D_IN_PROJ = 5392
EPS = 1e-5

ADAM_LR = 0.001
ADAM_B1 = 0.9
ADAM_B2 = 0.999
ADAM_EPS = 1e-08
ADAM_WD = 0.01
ADAM_STEP = 10

LANES = 128
SUBLANES = 8
VMEM_LIMIT = 48 * 1024 * 1024

NP = 5632
OFF_ZA, OFF_Q, OFF_K, OFF_V, OFF_DT = 0, 512, 1024, 1152, 1280
ATTN_GROUP = 1536
OFF_CONF, OFF_ZC = 1536, 2560
CONF_GROUP = 1536
OFF_ZS = 3072
OFF_XBC = 4096
SECTIONS = ((0, 1024, OFF_ZS), (1024, 1536, OFF_ZA), (1536, 2048, OFF_ZC), (2048, 3584, OFF_XBC),
            (3584, 3600, OFF_DT), (3600, 4368, OFF_Q), (4368, 5392, OFF_CONF))

YCAT_ATTN, YCAT_CONF = 1024, 1536
ANY = pl.BlockSpec(memory_space=pl.ANY)

NN = (((1,), (0,)), ((), ()))
NT = (((1,), (1,)), ((), ()))
TN = (((0,), (0,)), ((), ()))


def _params(sem):
    return pltpu.CompilerParams(dimension_semantics=sem, vmem_limit_bytes=VMEM_LIMIT)


def _dot(a, b, dims=NN):
    return lax.dot_general(a.astype(MXU_DTYPE), b.astype(MXU_DTYPE), dims, preferred_element_type=F32)


def _split_bf16(a, passes):
    pieces = []
    r = a
    for _ in range(passes):
        p = r.astype(BF16)
        pieces.append(p)
        r = r - p.astype(F32)
    return pieces


def _xdot(a, sel, dims=NN, passes=2):
    out = None
    for p in _split_bf16(a, passes):
        t = lax.dot_general(p, sel, dims, preferred_element_type=F32)
        out = t if out is None else out + t
    return out


def _xdot_r(sel, b, dims=NN, passes=3):
    out = None
    for p in _split_bf16(b, passes):
        t = lax.dot_general(sel, p, dims, preferred_element_type=F32)
        out = t if out is None else out + t
    return out


def _sigmoid(x):
    return 1.0 / (1.0 + jnp.exp(-x))


def _silu(x):
    return x * _sigmoid(x)


def _dsilu(x):
    s = _sigmoid(x)
    return s * (1.0 + x * (1.0 - s))


def _softplus(x):
    return jnp.maximum(x, 0.0) + jnp.log(1.0 + jnp.exp(-jnp.abs(x)))


def _rowsum8(x):
    r, c = x.shape
    return jnp.sum(x.reshape(r // SUBLANES, SUBLANES, c), axis=0)


def _iota(shape, dim):
    return lax.broadcasted_iota(jnp.int32, shape, dim)


def _matmul(a, b, form, out_dtype, tm, tn, tk, name, residual=None, after=None):
    if form == "nn":
        (m, k), n = a.shape, b.shape[1]
    elif form == "nt":
        (m, k), n = a.shape, b.shape[0]
    else:
        (k, m), n = a.shape, b.shape[1]
    tm, tn, tk = min(tm, m), min(tn, n), min(tk, k)
    assert m % tm == 0 and n % tn == 0 and k % tk == 0, (name, m, n, k, tm, tn, tk)
    if form == "nn":
        a_spec = pl.BlockSpec((tm, tk), lambda i, j, s: (i, s))
        b_spec = pl.BlockSpec((tk, tn), lambda i, j, s: (s, j))
        dims = NN
    elif form == "nt":
        (m, k), n = a.shape, b.shape[0]
        a_spec = pl.BlockSpec((tm, tk), lambda i, j, s: (i, s))
        b_spec = pl.BlockSpec((tn, tk), lambda i, j, s: (j, s))
        dims = NT
    else:
        (k, m), n = a.shape, b.shape[1]
        a_spec = pl.BlockSpec((tk, tm), lambda i, j, s: (s, i))
        b_spec = pl.BlockSpec((tk, tn), lambda i, j, s: (s, j))
        dims = TN
    nk = k // tk
    has_res = residual is not None
    deps = [] if after is None else [after]

    def body_single(a_ref, b_ref, *rest):
        o = _dot(a_ref[...], b_ref[...], dims)
        if has_res:
            o = o + rest[0][...]
        rest[-1][...] = o.astype(out_dtype)

    def body(a_ref, b_ref, *rest):
        r_ref = rest[0] if has_res else None
        o_ref, acc = rest[-2:]
        s = pl.program_id(2)

        @pl.when(s == 0)
        def _():
            acc[...] = jnp.zeros_like(acc)

        acc[...] += _dot(a_ref[...], b_ref[...], dims)

        @pl.when(s == nk - 1)
        def _():
            o = acc[...]
            if has_res:
                o = o + r_ref[...]
            o_ref[...] = o.astype(out_dtype)

    in_specs = [a_spec, b_spec]
    args = [a, b]
    if has_res:
        in_specs.append(pl.BlockSpec((tm, tn), lambda i, j, s: (i, j)))
        args.append(residual)
    in_specs += [ANY] * len(deps)
    args += deps
    return pl.pallas_call(
        body_single if nk == 1 else body, name=name,
        out_shape=jax.ShapeDtypeStruct((m, n), out_dtype),
        grid=(m // tm, n // tn, nk),
        in_specs=in_specs,
        out_specs=pl.BlockSpec((tm, tn), lambda i, j, s: (i, j)),
        scratch_shapes=[] if nk == 1 else [pltpu.VMEM((tm, tn), F32)],
        compiler_params=_params(("parallel", "parallel", "arbitrary")),
    )(*args)


ROW_TILE = 256


PROJ_FWD_TM, PROJ_FWD_TN = 1024, 512


def _proj_fwd(x, w, w_in_p, name, after=None):
    t, d = x.shape
    n = w_in_p.shape[1]
    tm, tn = min(PROJ_FWD_TM, t), PROJ_FWD_TN
    assert t % tm == 0 and n % tn == 0
    deps = [] if after is None else [after]

    def body(x_ref, w_ref, b_ref, *rest):
        o_ref, ot_ref, h_scr = rest[len(deps):]

        @pl.when(pl.program_id(1) == 0)
        def _():
            xv = x_ref[...]
            rstd = lax.rsqrt(jnp.mean(xv * xv, axis=-1, keepdims=True) + EPS)
            h = xv * rstd * w_ref[...]
            h_scr[...] = h.astype(h_scr.dtype)
            ot_ref[...] = h.T.astype(ot_ref.dtype)

        o_ref[...] = _dot(h_scr[...], b_ref[...])

    return pl.pallas_call(
        body, name=name,
        out_shape=(jax.ShapeDtypeStruct((t, n), F32), jax.ShapeDtypeStruct((d, t), MXU_DTYPE)),
        grid=(t // tm, n // tn),
        in_specs=[pl.BlockSpec((tm, d), lambda i, j: (i, 0)), pl.BlockSpec((1, d), lambda i, j: (0, 0)),
                  pl.BlockSpec((d, tn), lambda i, j: (0, j))] + [ANY] * len(deps),
        out_specs=(pl.BlockSpec((tm, tn), lambda i, j: (i, j)), pl.BlockSpec((d, tm), lambda i, j: (0, i))),
        scratch_shapes=[pltpu.VMEM((tm, d), MXU_DTYPE)],
        compiler_params=_params(("parallel", "arbitrary")),
    )(x, w, w_in_p, *deps)


PROJ_BWD_TM, PROJ_BWD_TK = 1024, 1408


def _proj_bwd_dx(dproj, w_in_p, x, w, dres, name):
    t, d = x.shape
    kdim = dproj.shape[1]
    tm, tk = min(PROJ_BWD_TM, t), PROJ_BWD_TK
    nt, nk = t // tm, kdim // tk
    assert t % tm == 0 and kdim % tk == 0

    def body(a_ref, b_ref, x_ref, w_ref, dr_ref, dx_ref, dw_ref, acc, wacc):
        i, s = pl.program_id(0), pl.program_id(1)

        @pl.when((i == 0) & (s == 0))
        def _():
            wacc[...] = jnp.zeros_like(wacc)

        @pl.when(s == 0)
        def _():
            acc[...] = jnp.zeros_like(acc)

        acc[...] += _dot(a_ref[...], b_ref[...], NT)

        @pl.when(s == nk - 1)
        def _():
            xv = x_ref[...]
            rstd = lax.rsqrt(jnp.mean(xv * xv, axis=-1, keepdims=True) + EPS)
            xh = xv * rstd
            dhv = acc[...]
            g = dhv * w_ref[...]
            dx_ref[...] = dr_ref[...] + rstd * (g - xh * jnp.mean(g * xh, axis=-1, keepdims=True))
            wacc[...] += _rowsum8(dhv * xh)

        @pl.when((i == nt - 1) & (s == nk - 1))
        def _():
            dw_ref[...] = jnp.sum(wacc[...], axis=0, keepdims=True)

    row = pl.BlockSpec((tm, d), lambda i, s: (i, 0))
    vec = pl.BlockSpec((1, d), lambda i, s: (0, 0))
    return pl.pallas_call(
        body, name=name,
        out_shape=(jax.ShapeDtypeStruct((t, d), F32), jax.ShapeDtypeStruct((1, d), F32)),
        grid=(nt, nk),
        in_specs=[pl.BlockSpec((tm, tk), lambda i, s: (i, s)), pl.BlockSpec((d, tk), lambda i, s: (0, s)),
                  row, vec, row],
        out_specs=(row, vec),
        scratch_shapes=[pltpu.VMEM((tm, d), F32), pltpu.VMEM((SUBLANES, d), F32)],
        compiler_params=_params(("arbitrary", "arbitrary")),
    )(dproj, w_in_p, x, w, dres)


def _loss_head(xf, target, w, name):
    t, d = xf.shape
    tm = ROW_TILE
    nt = t // tm

    def body(x_ref, t_ref, w_ref, loss_ref, dx_ref, dw_ref, lacc, wacc):
        i = pl.program_id(0)

        @pl.when(i == 0)
        def _():
            lacc[...] = jnp.zeros_like(lacc)
            wacc[...] = jnp.zeros_like(wacc)

        xv = x_ref[...]
        rstd = lax.rsqrt(jnp.mean(xv * xv, axis=-1, keepdims=True) + EPS)
        xh = xv * rstd
        err = xh * w_ref[...] - t_ref[...]
        lacc[...] += jnp.sum(err * err)
        dy = err * (1.0 / d)
        g = dy * w_ref[...]
        dx_ref[...] = rstd * (g - xh * jnp.mean(g * xh, axis=-1, keepdims=True))
        wacc[...] += _rowsum8(dy * xh)

        @pl.when(i == nt - 1)
        def _():
            loss_ref[...] = lacc[...] * (0.5 / d)
            dw_ref[...] = jnp.sum(wacc[...], axis=0, keepdims=True)

    row = pl.BlockSpec((tm, d), lambda i: (i, 0))
    vec = pl.BlockSpec((1, d), lambda i: (0, 0))
    return pl.pallas_call(
        body, name=name,
        out_shape=(jax.ShapeDtypeStruct((SUBLANES, LANES), F32), jax.ShapeDtypeStruct((t, d), F32),
                   jax.ShapeDtypeStruct((1, d), F32)),
        grid=(nt,),
        in_specs=[row, row, vec],
        out_specs=(pl.BlockSpec((SUBLANES, LANES), lambda i: (0, 0)), row, vec),
        scratch_shapes=[pltpu.VMEM((SUBLANES, LANES), F32), pltpu.VMEM((SUBLANES, d), F32)],
        compiler_params=_params(("arbitrary",)),
    )(xf, target, w)


CONV_TILE = 512
CONV_COLS = 512
CONV_SUB_ROWS = 128
CONV_SUB_COLS = LANES


def _conv_halo(k):
    return SUBLANES if k - 1 <= SUBLANES else 32


def _conv_subtiles(tm, cw):
    return [(r0, c0) for r0 in range(0, tm, CONV_SUB_ROWS) for c0 in range(0, cw, CONV_SUB_COLS)]


def _conv_use_shifted(k):
    return k > SUBLANES


def _conv_shift_scratch(k, rows, cw):
    return [pltpu.VMEM((SUBLANES - 1, rows - SUBLANES, cw), F32)] if _conv_use_shifted(k) else []


def _conv_fill_shifted(ext, sh):
    n = sh.shape[1]
    for b in range(1, SUBLANES):
        sh[b - 1] = ext[b:b + n, :]


def _conv_rows(ext, sh, start, rows, cs):
    b = start % SUBLANES
    if b == 0 or not sh:
        return ext[start:start + rows, cs]
    return sh[0][b - 1, start - b:start - b + rows, cs]


def _conv_fwd(src, col0, width, w, bias, k, seq, name):
    t = src.shape[0]
    tm, cw, halo = CONV_TILE, CONV_COLS, _conv_halo(k)
    sr, sc = CONV_SUB_ROWS, CONV_SUB_COLS
    p = k - 1
    cb0 = col0 // cw
    kp = w.shape[0]

    shifted = _conv_use_shifted(k)

    def body(x_ref, h_ref, w_ref, b_ref, o_ref, ext, *sh):
        i = pl.program_id(0)
        seq_start = (i * tm) % seq == 0
        ext[halo:, :] = x_ref[...]
        ext[:halo, :] = jnp.where(seq_start, 0.0, h_ref[...])
        if shifted:
            _conv_fill_shifted(ext, sh[0])
        for r0, c0 in _conv_subtiles(tm, cw):
            cs = slice(c0, c0 + sc)
            acc = jnp.zeros((sr, sc), F32) + b_ref[:, cs]
            for j in range(k):
                acc = acc + w_ref[j:j + 1, cs] * _conv_rows(ext, sh, r0 + halo - p + j, sr, cs)
            o_ref[r0:r0 + sr, cs] = acc

    return pl.pallas_call(
        body, name=name,
        out_shape=jax.ShapeDtypeStruct((t, width), F32),
        grid=(t // tm, width // cw),
        in_specs=[pl.BlockSpec((tm, cw), lambda i, j: (i, cb0 + j)),
                  pl.BlockSpec((halo, cw), lambda i, j: (jnp.maximum(i * (tm // halo) - 1, 0), cb0 + j)),
                  pl.BlockSpec((kp, cw), lambda i, j: (0, j)),
                  pl.BlockSpec((1, cw), lambda i, j: (0, j))],
        out_specs=pl.BlockSpec((tm, cw), lambda i, j: (i, j)),
        scratch_shapes=[pltpu.VMEM((halo + tm, cw), F32)] + _conv_shift_scratch(k, halo + tm, cw),
        compiler_params=_params(("parallel", "parallel")),
    )(src, src, w, bias)


def _conv_bwd(dy, src, col0, width, w, k, seq, name, into=None):
    t = src.shape[0]
    tm, cw, halo = CONV_TILE, CONV_COLS, _conv_halo(k)
    sr, sc = CONV_SUB_ROWS, CONV_SUB_COLS
    p = k - 1
    cb0 = col0 // cw
    kp = w.shape[0]
    nt = t // tm
    last_halo = t // halo - 1

    shifted = _conv_use_shifted(k)

    def body(dy_ref, dn_ref, x_ref, xp_ref, w_ref, *rest):
        if into is not None:
            rest = rest[1:]
        dx_ref, dw_ref, db_ref, dyext, xext, wacc, bacc = rest[:7]
        sh = rest[7:]
        i = pl.program_id(1)
        dysh, xsh = (sh[:1], sh[1:]) if shifted else ((), ())

        @pl.when(i == 0)
        def _():
            wacc[...] = jnp.zeros_like(wacc)
            bacc[...] = jnp.zeros_like(bacc)

        seq_start = (i * tm) % seq == 0
        seq_end = ((i + 1) * tm) % seq == 0
        dyext[:tm, :] = dy_ref[...]
        dyext[tm:, :] = jnp.where(seq_end, 0.0, dn_ref[...])
        xext[halo:, :] = x_ref[...]
        xext[:halo, :] = jnp.where(seq_start, 0.0, xp_ref[...])
        if shifted:
            _conv_fill_shifted(dyext, dysh[0])
            _conv_fill_shifted(xext, xsh[0])
        for r0, c0 in _conv_subtiles(tm, cw):
            cs = slice(c0, c0 + sc)
            dyv = dy_ref[r0:r0 + sr, cs]
            acc = jnp.zeros((sr, sc), F32)
            for j in range(k):
                acc = acc + w_ref[j:j + 1, cs] * _conv_rows(dyext, dysh, r0 + p - j, sr, cs)
                wacc[j, :, cs] += _rowsum8(dyv * _conv_rows(xext, xsh, r0 + halo - p + j, sr, cs))
            dx_ref[r0:r0 + sr, cs] = acc.astype(dx_ref.dtype)
            bacc[:, cs] += _rowsum8(dyv)

        @pl.when(i == nt - 1)
        def _():
            dw_ref[...] = jnp.zeros_like(dw_ref)
            for j in range(k):
                dw_ref[j:j + 1, :] = jnp.sum(wacc[j], axis=0, keepdims=True)
            db_ref[...] = jnp.sum(bacc[...], axis=0, keepdims=True)

    if into is None:
        dx_shape = jax.ShapeDtypeStruct((t, width), F32)
        dx_spec = pl.BlockSpec((tm, cw), lambda j, i: (i, j))
        extra_specs, extra_args, aliases = [], [], {}
    else:
        dx_shape = jax.ShapeDtypeStruct(into.shape, into.dtype)
        dx_spec = pl.BlockSpec((tm, cw), lambda j, i: (i, cb0 + j))
        extra_specs, extra_args, aliases = [ANY], [into], {5: 0}
    return pl.pallas_call(
        body, name=name,
        out_shape=(dx_shape, jax.ShapeDtypeStruct((kp, width), F32), jax.ShapeDtypeStruct((1, width), F32)),
        grid=(width // cw, nt),
        in_specs=[pl.BlockSpec((tm, cw), lambda j, i: (i, j)),
                  pl.BlockSpec((halo, cw), lambda j, i: (jnp.minimum((i + 1) * (tm // halo), last_halo), j)),
                  pl.BlockSpec((tm, cw), lambda j, i: (i, cb0 + j)),
                  pl.BlockSpec((halo, cw), lambda j, i: (jnp.maximum(i * (tm // halo) - 1, 0), cb0 + j)),
                  pl.BlockSpec((kp, cw), lambda j, i: (0, j))] + extra_specs,
        out_specs=(dx_spec,
                   pl.BlockSpec((kp, cw), lambda j, i: (0, j)),
                   pl.BlockSpec((1, cw), lambda j, i: (0, j))),
        input_output_aliases=aliases,
        scratch_shapes=[pltpu.VMEM((tm + halo, cw), F32), pltpu.VMEM((halo + tm, cw), F32),
                        pltpu.VMEM((kp, SUBLANES, cw), F32), pltpu.VMEM((SUBLANES, cw), F32)]
        + 2 * _conv_shift_scratch(k, halo + tm, cw),
        compiler_params=_params(("parallel", "arbitrary")),
    )(dy, dy, src, src, w, *extra_args)


def _conf_specs(tm, cw, halo, order):
    cb = OFF_CONF // cw

    def blk(col):
        return pl.BlockSpec((tm, cw), lambda *g: (order(*g), col))

    def prev(col):
        return pl.BlockSpec((halo, cw), lambda *g: (jnp.maximum(order(*g) * (tm // halo) - 1, 0), col))

    return blk(cb), prev(cb), blk(cb + 1), prev(cb + 1)


def _glu_window(ext, a_ref, ah_ref, g_ref, gh_ref, seq_start, halo):
    ext[halo:, :] = a_ref[...] * _sigmoid(g_ref[...])
    ext[:halo, :] = jnp.where(seq_start, 0.0, ah_ref[...] * _sigmoid(gh_ref[...]))


def _conf_fwd(proj, w, bias, ln_w, ln_b, ycat, seq, name):
    t = proj.shape[0]
    k = CONF_KERNEL
    tm, cw, halo = CONV_TILE, CONF_WIDTH, _conv_halo(k)
    sr, sc = CONV_SUB_ROWS, CONV_SUB_COLS
    p = k - 1
    kp = w.shape[0]

    def body(a_ref, ah_ref, g_ref, gh_ref, z_ref, w_ref, b_ref, lw_ref, lb_ref, _, c1_ref, y_ref, ext, sh):
        i = pl.program_id(0)
        _glu_window(ext, a_ref, ah_ref, g_ref, gh_ref, (i * tm) % seq == 0, halo)
        _conv_fill_shifted(ext, sh)
        for r0, c0 in _conv_subtiles(tm, cw):
            cs = slice(c0, c0 + sc)
            acc = jnp.zeros((sr, sc), F32) + b_ref[:, cs]
            for j in range(k):
                acc = acc + w_ref[j:j + 1, cs] * _conv_rows(ext, (sh,), r0 + halo - p + j, sr, cs)
            c1_ref[r0:r0 + sr, cs] = acc
        for r0 in range(0, tm, sr):
            rows = slice(r0, r0 + sr)
            cv = c1_ref[rows, :]
            xc = cv - jnp.mean(cv, axis=-1, keepdims=True)
            rstd = lax.rsqrt(jnp.mean(xc * xc, axis=-1, keepdims=True) + EPS)
            c2 = xc * rstd * lw_ref[...] + lb_ref[...]
            y_ref[rows, :] = (_silu(c2) * _silu(z_ref[rows, :])).astype(y_ref.dtype)

    vec = pl.BlockSpec((1, cw), lambda i: (0, 0))
    row = pl.BlockSpec((tm, cw), lambda i: (i, 0))
    return pl.pallas_call(
        body, name=name,
        out_shape=(jax.ShapeDtypeStruct((t, cw), F32), jax.ShapeDtypeStruct(ycat.shape, ycat.dtype)),
        grid=(t // tm,),
        in_specs=[*_conf_specs(tm, cw, halo, lambda i: i),
                  pl.BlockSpec((tm, cw), lambda i: (i, OFF_ZC // cw)),
                  pl.BlockSpec((kp, cw), lambda i: (0, 0)), vec, vec, vec, ANY],
        out_specs=(row, pl.BlockSpec((tm, cw), lambda i: (i, YCAT_CONF // cw))),
        input_output_aliases={9: 1},
        scratch_shapes=[pltpu.VMEM((halo + tm, cw), F32)] + _conv_shift_scratch(k, halo + tm, cw),
        compiler_params=_params(("parallel",)),
    )(proj, proj, proj, proj, proj, w, bias, ln_w, ln_b, ycat)


def _conf_bwd(dycat, proj, c1, w, ln_w, ln_b, dproj, seq, name):
    t = proj.shape[0]
    k = CONF_KERNEL
    tm, cw, halo = CONV_TILE, CONF_WIDTH, _conv_halo(k)
    sr, sc = CONV_SUB_ROWS, CONV_SUB_COLS
    p = k - 1
    kp = w.shape[0]
    nt = t // tm
    last_halo = t // halo - 1

    def body(dy_ref, dyn_ref, c_ref, cn_ref, z_ref, zn_ref, a_ref, ah_ref, g_ref, gh_ref, w_ref, lw_ref, lb_ref, _,
             grp_ref, dw_ref, db_ref, dlw_ref, dlb_ref, dyext, xext, wacc, bacc, lwacc, lbacc, dysh, xsh):
        i = pl.program_id(0)

        @pl.when(i == 0)
        def _():
            wacc[...] = jnp.zeros_like(wacc)
            bacc[...] = jnp.zeros_like(bacc)
            lwacc[...] = jnp.zeros_like(lwacc)
            lbacc[...] = jnp.zeros_like(lbacc)

        def post_bwd(dy, cv, zv):
            xc = cv - jnp.mean(cv, axis=-1, keepdims=True)
            rstd = lax.rsqrt(jnp.mean(xc * xc, axis=-1, keepdims=True) + EPS)
            xh = xc * rstd
            c2 = xh * lw_ref[...] + lb_ref[...]
            dz = dy * _silu(c2) * _dsilu(zv)
            dc2 = dy * _silu(zv) * _dsilu(c2)
            dxh = dc2 * lw_ref[...]
            dc = rstd * (dxh - jnp.mean(dxh, axis=-1, keepdims=True)
                         - xh * jnp.mean(dxh * xh, axis=-1, keepdims=True))
            return dc, dz, dc2 * xh, dc2

        seq_end = ((i + 1) * tm) % seq == 0
        for r0 in range(0, tm, sr):
            rows = slice(r0, r0 + sr)
            dc, dz, lw_terms, lb_terms = post_bwd(dy_ref[rows, :], c_ref[rows, :], z_ref[rows, :])
            dyext[rows, :] = dc
            grp_ref[rows, 2 * cw:] = dz.astype(grp_ref.dtype)
            lwacc[...] += _rowsum8(lw_terms)
            lbacc[...] += _rowsum8(lb_terms)
        dc_next = post_bwd(dyn_ref[...], cn_ref[...], zn_ref[...])[0]
        dyext[tm:, :] = jnp.where(seq_end, 0.0, dc_next)
        _glu_window(xext, a_ref, ah_ref, g_ref, gh_ref, (i * tm) % seq == 0, halo)
        _conv_fill_shifted(dyext, dysh)
        _conv_fill_shifted(xext, xsh)
        dag_ref = grp_ref
        for r0, c0 in _conv_subtiles(tm, cw):
            cs = slice(c0, c0 + sc)
            rows = slice(r0, r0 + sr)
            dyv = dyext[rows, cs]
            acc = jnp.zeros((sr, sc), F32)
            for j in range(k):
                acc = acc + w_ref[j:j + 1, cs] * _conv_rows(dyext, (dysh,), r0 + p - j, sr, cs)
                wacc[j, :, cs] += _rowsum8(dyv * _conv_rows(xext, (xsh,), r0 + halo - p + j, sr, cs))
            bacc[:, cs] += _rowsum8(dyv)
            s = _sigmoid(g_ref[rows, cs])
            dag_ref[rows, cs] = (acc * s).astype(dag_ref.dtype)
            dag_ref[rows, cw + c0:cw + c0 + sc] = (acc * a_ref[rows, cs] * s * (1.0 - s)).astype(dag_ref.dtype)

        @pl.when(i == nt - 1)
        def _():
            dw_ref[...] = jnp.zeros_like(dw_ref)
            for j in range(k):
                dw_ref[j:j + 1, :] = jnp.sum(wacc[j], axis=0, keepdims=True)
            db_ref[...] = jnp.sum(bacc[...], axis=0, keepdims=True)
            dlw_ref[...] = jnp.sum(lwacc[...], axis=0, keepdims=True)
            dlb_ref[...] = jnp.sum(lbacc[...], axis=0, keepdims=True)

    def blk(col):
        return pl.BlockSpec((tm, cw), lambda i: (i, col))

    def nxt(col):
        return pl.BlockSpec((halo, cw), lambda i: (jnp.minimum((i + 1) * (tm // halo), last_halo), col))

    vec = pl.BlockSpec((1, cw), lambda i: (0, 0))
    return pl.pallas_call(
        body, name=name,
        out_shape=(jax.ShapeDtypeStruct(dproj.shape, dproj.dtype), jax.ShapeDtypeStruct((kp, cw), F32),
                   jax.ShapeDtypeStruct((1, cw), F32), jax.ShapeDtypeStruct((1, cw), F32),
                   jax.ShapeDtypeStruct((1, cw), F32)),
        grid=(nt,),
        in_specs=[blk(YCAT_CONF // cw), nxt(YCAT_CONF // cw), blk(0), nxt(0), blk(OFF_ZC // cw), nxt(OFF_ZC // cw),
                  *_conf_specs(tm, cw, halo, lambda i: i),
                  pl.BlockSpec((kp, cw), lambda i: (0, 0)), vec, vec, ANY],
        out_specs=(pl.BlockSpec((tm, CONF_GROUP), lambda i: (i, OFF_CONF // CONF_GROUP)),
                   pl.BlockSpec((kp, cw), lambda i: (0, 0)), vec, vec, vec),
        input_output_aliases={13: 0},
        scratch_shapes=[pltpu.VMEM((tm + halo, cw), F32), pltpu.VMEM((halo + tm, cw), F32),
                        pltpu.VMEM((kp, SUBLANES, cw), F32), pltpu.VMEM((SUBLANES, cw), F32),
                        pltpu.VMEM((SUBLANES, cw), F32), pltpu.VMEM((SUBLANES, cw), F32)]
        + 2 * _conv_shift_scratch(k, halo + tm, cw),
        compiler_params=_params(("arbitrary",)),
    )(dycat, dycat, c1, c1, proj, proj, proj, proj, proj, proj, w, ln_w, ln_b, dproj)


def _half_mask(half):
    lane = _iota((1, LANES), 1)
    return ((lane >= half * ATTN_HEAD_DIM) & (lane < (half + 1) * ATTN_HEAD_DIM)).astype(F32)


def _stack_heads(xp, g):
    m = _half_mask(g)
    swapped = pltpu.roll(xp, ATTN_HEAD_DIM, axis=1)
    return jnp.concatenate([xp * m, swapped * m] if g == 0 else [swapped * m, xp * m], axis=0)


def _unstack_heads(both, g):
    w = both.shape[0] // 2
    top, bot = both[:w], both[w:]
    lo, hi = _half_mask(0), _half_mask(1)
    if g == 0:
        return top * lo + pltpu.roll(bot, ATTN_HEAD_DIM, axis=1) * hi
    return pltpu.roll(top, ATTN_HEAD_DIM, axis=1) * lo + bot * hi


def _band_mask(first_block):
    w = WINDOW
    qi = _iota((w, 2 * w), 0)
    kj = _iota((w, 2 * w), 1) - w
    rel = qi - kj
    return (rel >= 0) & (rel < w) & (jnp.logical_not(first_block) | (kj >= 0))


def _lane_pick(x, h):
    return jnp.sum(jnp.where(_iota(x.shape, 1) == h, x, 0.0), axis=1, keepdims=True)


def _attn_specs(nb, rev):
    w = WINDOW

    def blk(i):
        return nb - 1 - i if rev else i

    def row(b, i):
        return b * nb + blk(i)

    def prow(b, i):
        return b * nb + jnp.maximum(blk(i) - 1, 0)

    q = pl.BlockSpec((w, 512), lambda b, i: (row(b, i), OFF_Q // 512))
    kc = pl.BlockSpec((w, 128), lambda b, i: (row(b, i), OFF_K // 128))
    kp = pl.BlockSpec((w, 128), lambda b, i: (prow(b, i), OFF_K // 128))
    vc = pl.BlockSpec((w, 128), lambda b, i: (row(b, i), OFF_V // 128))
    vp = pl.BlockSpec((w, 128), lambda b, i: (prow(b, i), OFF_V // 128))
    z = pl.BlockSpec((w, 512), lambda b, i: (row(b, i), OFF_ZA // 512))
    return q, kc, kp, vc, vp, z, row


def _attn_fwd(proj, sinks, ycat, nbatch, name):
    t = proj.shape[0]
    w = WINDOW
    nb = t // nbatch // w
    scale = ATTN_HEAD_DIM ** -0.5
    q_s, kc_s, kp_s, vc_s, vp_s, z_s, row = _attn_specs(nb, False)

    def body(q_ref, kc_ref, kp_ref, vc_ref, vp_ref, z_ref, sk_ref, _, y_ref, o_ref, lse_ref):
        first = pl.program_id(1) == 0
        mask = _band_mask(first)
        kk = jnp.concatenate([kp_ref[...], kc_ref[...]], axis=0).astype(MXU_DTYPE)
        vv = jnp.concatenate([vp_ref[...], vc_ref[...]], axis=0).astype(MXU_DTYPE)
        sk = sk_ref[...]
        lane = _iota((w, LANES), 1)
        mask2 = jnp.concatenate([mask, mask], axis=0)
        scores = [_dot(_stack_heads(q_ref[:, j * LANES:(j + 1) * LANES], j // 2), kk, NT) for j in range(4)]
        lse_all = jnp.zeros((w, LANES), F32)
        for j in range(4):
            s = jnp.where(mask2, scores[j] * scale, -1e30)
            skc = jnp.concatenate([jnp.broadcast_to(_lane_pick(sk, 2 * j), (w, 1)),
                                   jnp.broadcast_to(_lane_pick(sk, 2 * j + 1), (w, 1))], axis=0)
            m = jnp.maximum(jnp.max(s, axis=1, keepdims=True), skc)
            den = jnp.sum(jnp.exp(s - m), axis=1, keepdims=True) + jnp.exp(skc - m)
            lse = m + jnp.log(den)
            lse_all = jnp.where(lane == 2 * j, lse[:w], lse_all)
            lse_all = jnp.where(lane == 2 * j + 1, lse[w:], lse_all)
            op = _unstack_heads(_dot(jnp.exp(s - lse), vv), j // 2)
            cols = slice(j * LANES, (j + 1) * LANES)
            o_ref[:, cols] = op
            y_ref[:, cols] = (op * _silu(z_ref[:, cols])).astype(y_ref.dtype)
        lse_ref[...] = lse_all

    return pl.pallas_call(
        body, name=name,
        out_shape=(jax.ShapeDtypeStruct(ycat.shape, ycat.dtype), jax.ShapeDtypeStruct((t, 512), F32),
                   jax.ShapeDtypeStruct((t, LANES), F32)),
        grid=(nbatch, nb),
        in_specs=[q_s, kc_s, kp_s, vc_s, vp_s, z_s, pl.BlockSpec((1, LANES), lambda b, i: (0, 0)), ANY],
        out_specs=(pl.BlockSpec((w, 512), lambda b, i: (row(b, i), YCAT_ATTN // 512)),
                   pl.BlockSpec((w, 512), lambda b, i: (row(b, i), 0)),
                   pl.BlockSpec((w, LANES), lambda b, i: (row(b, i), 0))),
        input_output_aliases={7: 0},
        compiler_params=_params(("parallel", "parallel")),
    )(proj, proj, proj, proj, proj, proj, sinks, ycat)


def _attn_bwd(dycat, proj, o, lse, sinks, ddt, dproj, nbatch, name):
    t = proj.shape[0]
    w = WINDOW
    nb = t // nbatch // w
    scale = ATTN_HEAD_DIM ** -0.5
    q_s, kc_s, kp_s, vc_s, vp_s, z_s, row = _attn_specs(nb, True)

    def body(dy_ref, q_ref, kc_ref, kp_ref, vc_ref, vp_ref, z_ref, o_ref, lse_ref, sk_ref, ddt_ref, _,
             grp_ref, dsk_ref, kcarry, vcarry, sacc):
        b, i = pl.program_id(0), pl.program_id(1)

        @pl.when((b == 0) & (i == 0))
        def _():
            sacc[...] = jnp.zeros_like(sacc)

        @pl.when(i == 0)
        def _():
            kcarry[...] = jnp.zeros_like(kcarry)
            vcarry[...] = jnp.zeros_like(vcarry)

        first = i == nb - 1
        mask = _band_mask(first)
        kk = jnp.concatenate([kp_ref[...], kc_ref[...]], axis=0).astype(MXU_DTYPE)
        vv = jnp.concatenate([vp_ref[...], vc_ref[...]], axis=0).astype(MXU_DTYPE)
        sk = sk_ref[...]
        lse_all = lse_ref[...]
        lane1 = _iota((1, LANES), 1)
        mask2 = jnp.concatenate([mask, mask], axis=0)
        qs, dos, deltas, lses, scores, dps = [], [], [], [], [], []
        for j in range(4):
            cols = slice(j * LANES, (j + 1) * LANES)
            qp, zp, ov, dy = q_ref[:, cols], z_ref[:, cols], o_ref[:, cols], dy_ref[:, cols]
            grp_ref[:, OFF_ZA + j * LANES:OFF_ZA + (j + 1) * LANES] = (dy * ov * _dsilu(zp)).astype(grp_ref.dtype)
            do = dy * _silu(zp)
            q2 = _stack_heads(qp, j // 2).astype(MXU_DTYPE)
            do2 = _stack_heads(do, j // 2)
            qs.append(q2)
            dos.append(do2.astype(MXU_DTYPE))
            deltas.append(jnp.sum(do2 * _stack_heads(ov, j // 2), axis=1, keepdims=True))
            lses.append(jnp.concatenate([_lane_pick(lse_all, 2 * j), _lane_pick(lse_all, 2 * j + 1)], axis=0))
            scores.append(_dot(q2, kk, NT))
            dps.append(_dot(do2, vv, NT))
        prs, dss = [], []
        dsk = jnp.zeros((1, LANES), F32)
        for j in range(4):
            pr = jnp.exp(jnp.where(mask2, scores[j] * scale, -1e30) - lses[j])
            prs.append(pr.astype(MXU_DTYPE))
            dss.append((pr * (dps[j] - deltas[j])).astype(MXU_DTYPE))
            skc = jnp.concatenate([jnp.broadcast_to(_lane_pick(sk, 2 * j), (w, 1)),
                                   jnp.broadcast_to(_lane_pick(sk, 2 * j + 1), (w, 1))], axis=0)
            sink_term = jnp.exp(skc - lses[j]) * deltas[j]
            dsk = dsk - jnp.where(lane1 == 2 * j, jnp.sum(sink_term[:w]), 0.0)
            dsk = dsk - jnp.where(lane1 == 2 * j + 1, jnp.sum(sink_term[w:]), 0.0)
        dkk = jnp.zeros((2 * w, LANES), F32)
        dvv = jnp.zeros((2 * w, LANES), F32)
        for j in range(4):
            dq = _unstack_heads(_dot(dss[j], kk) * scale, j // 2)
            grp_ref[:, OFF_Q + j * LANES:OFF_Q + (j + 1) * LANES] = dq.astype(grp_ref.dtype)
            dkk = dkk + _dot(dss[j], qs[j], TN) * scale
            dvv = dvv + _dot(prs[j], dos[j], TN)
        grp_ref[:, OFF_K:OFF_K + LANES] = (dkk[w:, :] + kcarry[...]).astype(grp_ref.dtype)
        grp_ref[:, OFF_V:OFF_V + LANES] = (dvv[w:, :] + vcarry[...]).astype(grp_ref.dtype)
        grp_ref[:, OFF_DT:OFF_DT + LANES] = ddt_ref[...].astype(grp_ref.dtype)
        grp_ref[:, OFF_DT + LANES:] = jnp.zeros((w, ATTN_GROUP - OFF_DT - LANES), grp_ref.dtype)
        kcarry[...] = dkk[:w, :]
        vcarry[...] = dvv[:w, :]
        sacc[...] += dsk

        @pl.when((b == nbatch - 1) & (i == nb - 1))
        def _():
            dsk_ref[...] = sacc[...]

    return pl.pallas_call(
        body, name=name,
        out_shape=(jax.ShapeDtypeStruct(dproj.shape, dproj.dtype), jax.ShapeDtypeStruct((1, LANES), F32)),
        grid=(nbatch, nb),
        in_specs=[pl.BlockSpec((w, 512), lambda b, i: (row(b, i), YCAT_ATTN // 512)),
                  q_s, kc_s, kp_s, vc_s, vp_s, z_s,
                  pl.BlockSpec((w, 512), lambda b, i: (row(b, i), 0)),
                  pl.BlockSpec((w, LANES), lambda b, i: (row(b, i), 0)),
                  pl.BlockSpec((1, LANES), lambda b, i: (0, 0)),
                  pl.BlockSpec((w, LANES), lambda b, i: (row(b, i), 0)), ANY],
        out_specs=(pl.BlockSpec((w, ATTN_GROUP), lambda b, i: (row(b, i), 0)),
                   pl.BlockSpec((1, LANES), lambda b, i: (0, 0))),
        input_output_aliases={11: 0},
        scratch_shapes=[pltpu.VMEM((w, LANES), F32), pltpu.VMEM((w, LANES), F32),
                        pltpu.VMEM((1, LANES), F32)],
        compiler_params=_params(("arbitrary", "arbitrary")),
    )(dycat, proj, proj, proj, proj, proj, proj, o, lse, sinks, ddt, dproj)


SSD_WIDTH = SSD_HEADS * SSD_HEAD_DIM
GROUP_ROWS = SSD_WIDTH // 2


def _expand_mat():
    r, c = _iota((LANES, SSD_WIDTH), 0), _iota((LANES, SSD_WIDTH), 1)
    return (r == lax.shift_right_logical(c, 6)).astype(BF16)


def _expand_mat_t():
    r, c = _iota((SSD_WIDTH, LANES), 0), _iota((SSD_WIDTH, LANES), 1)
    return (c == lax.shift_right_logical(r, 6)).astype(BF16)


def _ssd_common(u_ref, dt_ref, dtb_ref, a_ref):
    q = CHUNK
    act = _silu(u_ref[...])
    xs = act[:, :SSD_WIDTH]
    bm = act[:, SSD_WIDTH:SSD_WIDTH + 256]
    cm = act[:, SSD_WIDTH + 256:]
    dtp = _softplus(dt_ref[...] + dtb_ref[...])
    a = dtp * a_ref[...]
    tril = (_iota((q, q), 0) >= _iota((q, q), 1)).astype(BF16)
    acs = _xdot_r(tril, a)
    acs_t = acs.T
    e = _expand_mat()
    dt_x = _xdot(dtp, e)
    ea = jnp.exp(_xdot(acs, e))
    a_end = jnp.sum(jnp.where(_iota(acs.shape, 0) == q - 1, acs, 0.0), axis=0, keepdims=True)
    dec = jnp.exp(_xdot(a_end - acs, e))
    a_end_col = jnp.broadcast_to(_lane_pick(acs_t, q - 1), (LANES, LANES))
    s_scale = jnp.exp(_xdot_r(_expand_mat_t(), a_end_col))
    return act, xs, bm, cm, dtp, acs, acs_t, dt_x, ea, dec, s_scale, tril


def _decay_mat(acs, acs_t, h):
    q = CHUNK
    col = _lane_pick(acs, h)
    rowv = jnp.sum(jnp.where(_iota(acs_t.shape, 0) == h, acs_t, 0.0), axis=0, keepdims=True)
    causal = _iota((q, q), 0) >= _iota((q, q), 1)
    return jnp.exp(jnp.where(causal, col - rowv, -1e30))


GN_WIDTH = 512


def _ssd_fwd(u, proj, dtb, a_neg, d_x, norm_w, ycat, nbatch, name):
    t = u.shape[0]
    q = CHUNK
    nc = t // nbatch // q

    def body(u_ref, dt_ref, z_ref, dtb_ref, a_ref, dx_ref, nw_ref, _, y_ref, st_ref, yn_ref, state):
        c = pl.program_id(1)

        @pl.when(c == 0)
        def _():
            state[...] = jnp.zeros_like(state)

        st_ref[...] = state[...]
        act, xs, bm, cm, dtp, acs, acs_t, dt_x, ea, dec, s_scale, _ = _ssd_common(u_ref, dt_ref, dtb_ref, a_ref)
        xdt = xs * dt_x
        xdec = xdt * dec
        lo, hi = _half_mask(0), _half_mask(1)
        grp = []
        for g in range(2):
            bg = bm[:, g * LANES:(g + 1) * LANES]
            cg = cm[:, g * LANES:(g + 1) * LANES]
            rows = slice(g * GROUP_ROWS, (g + 1) * GROUP_ROWS)
            sg = state[rows, :]
            grp.append((_dot(cg, bg, NT), _dot(cg, sg, NT), rows,
                        s_scale[rows, :] * sg + _dot(xdec[:, rows], bg, TN)))
        for g in range(2):
            cb, yoff, rows, state_new = grp[g]
            for j in range(4):
                pj = g * 4 + j
                cols = slice(pj * LANES, (pj + 1) * LANES)
                xp = xdt[:, cols]
                m2 = jnp.concatenate([cb * _decay_mat(acs, acs_t, 2 * pj), cb * _decay_mat(acs, acs_t, 2 * pj + 1)],
                                     axis=1)
                yp = _dot(m2, jnp.concatenate([xp * lo, xp * hi], axis=0))
                yp = yp + yoff[:, j * LANES:(j + 1) * LANES] * ea[:, cols]
                y_ref[:, cols] = yp + dx_ref[:, cols] * xs[:, cols]
            state[rows, :] = state_new
        for g in range(SSD_WIDTH // GN_WIDTH):
            cols = slice(g * GN_WIDTH, (g + 1) * GN_WIDTH)
            gg = y_ref[:, cols] * _silu(z_ref[:, cols])
            rstd = lax.rsqrt(jnp.mean(gg * gg, axis=-1, keepdims=True) + EPS)
            yn_ref[:, cols] = (gg * rstd * nw_ref[:, cols]).astype(yn_ref.dtype)

    vec = pl.BlockSpec((1, LANES), lambda b, c: (0, 0))
    wide = pl.BlockSpec((q, SSD_WIDTH), lambda b, c: (b * nc + c, 0))
    wvec = pl.BlockSpec((1, SSD_WIDTH), lambda b, c: (0, 0))
    return pl.pallas_call(
        body, name=name,
        out_shape=(jax.ShapeDtypeStruct((t, SSD_WIDTH), F32),
                   jax.ShapeDtypeStruct((nbatch * nc * SSD_WIDTH, SSD_STATE), F32),
                   jax.ShapeDtypeStruct(ycat.shape, ycat.dtype)),
        grid=(nbatch, nc),
        in_specs=[pl.BlockSpec((q, SSD_CONV_DIM), lambda b, c: (b * nc + c, 0)),
                  pl.BlockSpec((q, LANES), lambda b, c: (b * nc + c, OFF_DT // LANES)),
                  pl.BlockSpec((q, SSD_WIDTH), lambda b, c: (b * nc + c, OFF_ZS // SSD_WIDTH)),
                  vec, vec, wvec, wvec, ANY],
        out_specs=(wide, pl.BlockSpec((SSD_WIDTH, SSD_STATE), lambda b, c: (b * nc + c, 0)), wide),
        input_output_aliases={7: 2},
        scratch_shapes=[pltpu.VMEM((SSD_WIDTH, SSD_STATE), F32)],
        compiler_params=_params(("parallel", "arbitrary")),
    )(u, proj, proj, dtb, a_neg, d_x, norm_w, ycat)


def _ssd_bwd(dycat, u, proj, y, states, dtb, a_neg, d_x, norm_w, dproj, nbatch, name):
    t = u.shape[0]
    q = CHUNK
    nc = t // nbatch // q

    def body(do_ref, u_ref, dt_ref, z_ref, y_ref, st_ref, dtb_ref, a_ref, dx_ref, nw_ref, _,
             du_ref, dz_ref, ddt_ref, dal_ref, dd_ref, dtbg_ref, dnw_ref, dstate, acc_a, acc_d, acc_b, acc_w):
        b, c = pl.program_id(0), pl.program_id(1)

        @pl.when((b == 0) & (c == 0))
        def _():
            acc_a[...] = jnp.zeros_like(acc_a)
            acc_d[...] = jnp.zeros_like(acc_d)
            acc_b[...] = jnp.zeros_like(acc_b)
            acc_w[...] = jnp.zeros_like(acc_w)

        @pl.when(c == 0)
        def _():
            dstate[...] = jnp.zeros_like(dstate)

        dy_parts = []
        for g in range(SSD_WIDTH // GN_WIDTH):
            cols = slice(g * GN_WIDTH, (g + 1) * GN_WIDTH)
            yv, zv, dov = y_ref[:, cols], z_ref[:, cols], do_ref[:, cols]
            sz = _silu(zv)
            gg = yv * sz
            rstd = lax.rsqrt(jnp.mean(gg * gg, axis=-1, keepdims=True) + EPS)
            gh = gg * rstd
            acc_w[:, cols] += _rowsum8(dov * gh)
            dgn = dov * nw_ref[:, cols]
            dg = rstd * (dgn - gh * jnp.mean(dgn * gh, axis=-1, keepdims=True))
            dy_parts.append(dg * sz)
            dz_ref[:, cols] = (dg * yv * _dsilu(zv)).astype(dz_ref.dtype)

        act, xs, bm, cm, dtp, acs, acs_t, dt_x, ea, dec, s_scale, tril = _ssd_common(
            u_ref, dt_ref, dtb_ref, a_ref)
        xdt = xs * dt_x
        xdec = xdt * dec
        dyv = jnp.concatenate(dy_parts, axis=1)
        dye = dyv * ea
        lo, hi = _half_mask(0), _half_mask(1)
        et = _expand_mat_t()
        grp = []
        for g in range(2):
            rows = slice(g * GROUP_ROWS, (g + 1) * GROUP_ROWS)
            bg = bm[:, g * LANES:(g + 1) * LANES]
            cg = cm[:, g * LANES:(g + 1) * LANES]
            sg = st_ref[rows, :]
            dsg = dstate[rows, :]
            grp.append(dict(
                rows=rows, bg=bg, cg=cg, dsg=dsg,
                cb=_dot(cg, bg, NT), yoff=_dot(cg, sg, NT), dxst=_dot(bg, dsg, NT) * dec[:, rows],
                dc_off=_dot(dye[:, rows], sg), db_off=_dot(xdec[:, rows], dsg),
                s_next=s_scale[rows, :] * sg + _dot(xdec[:, rows], bg, TN),
                dstate_new=_dot(dye[:, rows], cg, TN) + s_scale[rows, :] * dsg))
        dy2s, g2s, l2s = [], [], []
        for pj in range(SSD_HEADS // 2):
            cols = slice(pj * LANES, (pj + 1) * LANES)
            dyp = dyv[:, cols]
            dy2 = jnp.concatenate([dyp * lo, dyp * hi], axis=0).astype(MXU_DTYPE)
            dy2s.append(dy2)
            g2s.append(_dot(dy2, xdt[:, cols], NT))
            l2s.append(jnp.concatenate([_decay_mat(acs, acs_t, 2 * pj), _decay_mat(acs, acs_t, 2 * pj + 1)], axis=0))
        dal_diag = jnp.zeros((q, LANES), F32)
        lane2 = _iota((2 * q, LANES), 1)
        row2 = _iota((2 * q, LANES), 0)
        dxdt_parts, db_parts, dc_parts = [], [], []
        end_sum = jnp.zeros((LANES, LANES), F32)
        for g in range(2):
            gd = grp[g]
            cb2 = jnp.concatenate([gd["cb"], gd["cb"]], axis=0)
            dcb = jnp.zeros((q, q), F32)
            parts = []
            for j in range(4):
                pj = g * 4 + j
                gl = g2s[pj] * l2s[pj]
                dcb = dcb + gl[:q] + gl[q:]
                m2 = cb2 * l2s[pj]
                parts.append(_dot(m2, dy2s[pj], TN))
                w2 = (gl * cb2).astype(MXU_DTYPE)
                sel2 = (lane2 == 2 * pj + (row2 >= q).astype(jnp.int32)).astype(MXU_DTYPE)
                dal_diag = dal_diag + _dot(jnp.concatenate([w2[:q], w2[q:]], axis=1), sel2) - _dot(w2, sel2, TN)
            dxdt_parts.append(jnp.concatenate(parts, axis=1) + gd["dxst"])
            dc_parts.append(_dot(dcb, gd["bg"]) + gd["dc_off"])
            db_parts.append(_dot(dcb, gd["cg"], TN) + gd["db_off"])
            end_sum = end_sum + _xdot(gd["dsg"] * gd["s_next"], et[gd["rows"], :], TN, passes=2)
            dstate[gd["rows"], :] = gd["dstate_new"]
        dxst_parts = [gd["dxst"] for gd in grp]
        yoff_parts = [gd["yoff"] for gd in grp]
        dxdt = jnp.concatenate(dxdt_parts, axis=1)
        dxv = dx_ref[...]
        yoff = jnp.concatenate(yoff_parts, axis=1) * ea
        dalpha = dal_diag + _xdot(dyv * yoff - xdt * jnp.concatenate(dxst_parts, axis=1), et)
        end_row = jnp.sum(end_sum, axis=0, keepdims=True)
        dalpha = dalpha + jnp.where(_iota((q, LANES), 0) == q - 1, end_row, 0.0)
        da = _xdot_r(tril, dalpha, TN)
        ddtp = da * a_ref[...] + _xdot(dxdt * xs, et)
        acc_a[...] += _rowsum8(da * dtp)
        acc_d[...] += _rowsum8(_xdot(dyv * xs, et))
        ddt_raw = ddtp * _sigmoid(dt_ref[...] + dtb_ref[...])
        acc_b[...] += _rowsum8(ddt_raw)
        ddt_ref[...] = ddt_raw
        dxs = dxdt * dt_x + dxv * dyv
        dact = jnp.concatenate([dxs] + db_parts + dc_parts, axis=1)
        du_ref[...] = dact * _dsilu(u_ref[...])

        @pl.when((b == nbatch - 1) & (c == nc - 1))
        def _():
            dal_ref[...] = jnp.sum(acc_a[...], axis=0, keepdims=True) * a_ref[...]
            dd_ref[...] = jnp.sum(acc_d[...], axis=0, keepdims=True)
            dtbg_ref[...] = jnp.sum(acc_b[...], axis=0, keepdims=True)
            dnw_ref[...] = jnp.sum(acc_w[...], axis=0, keepdims=True)

    def rowblk(b, c):
        return b * nc + (nc - 1 - c)

    vec = pl.BlockSpec((1, LANES), lambda b, c: (0, 0))
    wvec = pl.BlockSpec((1, SSD_WIDTH), lambda b, c: (0, 0))
    wide = pl.BlockSpec((q, SSD_WIDTH), lambda b, c: (rowblk(b, c), 0))
    zblk = pl.BlockSpec((q, SSD_WIDTH), lambda b, c: (rowblk(b, c), OFF_ZS // SSD_WIDTH))
    return pl.pallas_call(
        body, name=name,
        out_shape=(jax.ShapeDtypeStruct((t, SSD_CONV_DIM), F32), jax.ShapeDtypeStruct(dproj.shape, dproj.dtype),
                   jax.ShapeDtypeStruct((t, LANES), F32),
                   jax.ShapeDtypeStruct((1, LANES), F32), jax.ShapeDtypeStruct((1, LANES), F32),
                   jax.ShapeDtypeStruct((1, LANES), F32), jax.ShapeDtypeStruct((1, SSD_WIDTH), F32)),
        grid=(nbatch, nc),
        in_specs=[wide,
                  pl.BlockSpec((q, SSD_CONV_DIM), lambda b, c: (rowblk(b, c), 0)),
                  pl.BlockSpec((q, LANES), lambda b, c: (rowblk(b, c), OFF_DT // LANES)),
                  zblk, wide,
                  pl.BlockSpec((SSD_WIDTH, SSD_STATE), lambda b, c: (rowblk(b, c), 0)),
                  vec, vec, wvec, wvec, ANY],
        out_specs=(pl.BlockSpec((q, SSD_CONV_DIM), lambda b, c: (rowblk(b, c), 0)),
                   zblk,
                   pl.BlockSpec((q, LANES), lambda b, c: (rowblk(b, c), 0)),
                   vec, vec, vec, wvec),
        input_output_aliases={10: 1},
        scratch_shapes=[pltpu.VMEM((SSD_WIDTH, SSD_STATE), F32), pltpu.VMEM((SUBLANES, LANES), F32),
                        pltpu.VMEM((SUBLANES, LANES), F32), pltpu.VMEM((SUBLANES, LANES), F32),
                        pltpu.VMEM((SUBLANES, SSD_WIDTH), F32)],
        compiler_params=_params(("arbitrary", "arbitrary")),
    )(dycat, u, proj, proj, y, states, dtb, a_neg, d_x, norm_w, dproj)


def _pad_rows(w, rows):
    return jnp.concatenate([w, jnp.zeros((rows - w.shape[0], w.shape[1]), w.dtype)], axis=0)


def _pad_lanes(v):
    return jnp.concatenate([v, jnp.zeros((LANES - v.shape[0],), v.dtype)]).reshape(1, LANES)


def _padded_from_chips(pieces):
    cols = pieces[0].shape[-1]
    lead = pieces[0].shape[:-1]
    parts, pos = [], 0
    for lo, hi, start in sorted(SECTIONS, key=lambda s: s[2]):
        if start > pos:
            parts.append(jnp.zeros(lead + (start - pos,), pieces[0].dtype))
        pos = start + hi - lo
        while lo < hi:
            p = lo // cols
            end = min(hi, (p + 1) * cols)
            parts.append(pieces[p][..., lo - p * cols:end - p * cols])
            lo = end
    if pos < NP:
        parts.append(jnp.zeros(lead + (NP - pos,), pieces[0].dtype))
    return jnp.concatenate(parts, axis=-1)


def _chip_part_from_padded(wp, p, cols):
    lo, hi = p * cols, (p + 1) * cols
    parts = []
    for rs, re, start in SECTIONS:
        a, b = max(lo, rs), min(hi, re)
        if a < b:
            parts.append(wp[..., start + a - rs:start + b - rs])
    return jnp.concatenate(parts, axis=-1)


def _layer_params(li, w_in_p, w_out, conv_w, dw_w, small):
    return dict(
        w_in_p=w_in_p, w_out=w_out,
        conv_w=_pad_rows(conv_w, SUBLANES), dw_w=_pad_rows(dw_w, 32),
        norm_w=small["norm_w"][li].reshape(1, -1),
        conv_b=small["ssd_conv_b"][li].reshape(1, -1),
        dtb=_pad_lanes(small["ssd_dt_bias"][li]),
        a_neg=_pad_lanes(-jnp.exp(small["ssd_a_log"][li])),
        d_x=jnp.repeat(small["ssd_d"][li], SSD_HEAD_DIM).reshape(1, -1),
        ssd_norm_w=small["ssd_norm_w"][li].reshape(1, -1),
        sinks=_pad_lanes(small["attn_sinks"][li]),
        dw_b=small["conf_dw_b"][li].reshape(1, -1),
        ln_w=small["conf_ln_w"][li].reshape(1, -1),
        ln_b=small["conf_ln_b"][li].reshape(1, -1),
    )


def _layer_fwd(x, p, nbatch, seq, tag, after=None):
    proj, h_t = _proj_fwd(x, p["norm_w"], p["w_in_p"], name=f"proj_fwd_{tag}", after=after)
    u = _conv_fwd(proj, OFF_XBC, SSD_CONV_DIM, p["conv_w"], p["conv_b"], SSD_CONV, seq, name=f"ssd_conv_fwd_{tag}")
    ycat = lax.empty((x.shape[0], MIX_WIDTH), MXU_DTYPE)
    y, states, ycat = _ssd_fwd(u, proj, p["dtb"], p["a_neg"], p["d_x"], p["ssd_norm_w"], ycat, nbatch,
                               name=f"ssd_fwd_{tag}")
    ycat, o, lse = _attn_fwd(proj, p["sinks"], ycat, nbatch, name=f"attn_fwd_{tag}")
    c1, ycat = _conf_fwd(proj, p["dw_w"], p["dw_b"], p["ln_w"], p["ln_b"], ycat, seq, name=f"conf_fwd_{tag}")
    w_out = p["w_out"](ycat) if callable(p["w_out"]) else p["w_out"]
    x_new = _matmul(ycat, w_out, "nn", F32, 1024, 512, 2048, name=f"out_fwd_{tag}", residual=x)
    return x_new, dict(x=x, w_out=w_out, h_t=h_t, proj=proj, u=u, y=y, states=states, o=o, lse=lse, c1=c1, ycat=ycat)


def _layer_bwd(dx_out, p, s, nbatch, seq, tag, hooks=None):
    hooks = hooks or {}
    proj = s["proj"]
    dycat = _matmul(dx_out, s["w_out"], "nt", F32, 1024, 1024, 1024, name=f"out_bwd_dy_{tag}",
                    after=hooks.get("start_token"))
    dw_out = _matmul(s["ycat"], dx_out, "tn", F32, 1024, 1024, 1024, name=f"out_bwd_dw_{tag}")
    token = hooks["after_dycat"](dycat) if "after_dycat" in hooks else None
    dtb = p["dtb"] if token is None else p["dtb"] + token[0, 0]
    dproj = lax.empty(proj.shape, MXU_DTYPE)
    du, dproj, ddt, da_log, dd, ddtb, dssd_norm_w = _ssd_bwd(
        dycat, s["u"], proj, s["y"], s["states"], dtb, p["a_neg"], p["d_x"], p["ssd_norm_w"], dproj,
        nbatch, name=f"ssd_bwd_{tag}")
    dproj, dconv_w, dconv_b = _conv_bwd(du, proj, OFF_XBC, SSD_CONV_DIM, p["conv_w"], SSD_CONV, seq,
                                        name=f"ssd_conv_bwd_{tag}", into=dproj)
    dproj, dsinks = _attn_bwd(dycat, proj, s["o"], s["lse"], p["sinks"], ddt, dproj, nbatch,
                              name=f"attn_bwd_{tag}")
    if "after_attn" in hooks:
        hooks["after_attn"](dproj)
    dproj, ddw_w, ddw_b, dln_w, dln_b = _conf_bwd(dycat, proj, s["c1"], p["dw_w"], p["ln_w"], p["ln_b"], dproj, seq,
                                                  name=f"conf_bwd_{tag}")
    dw_in_p = _matmul(s["h_t"], dproj, "nn", F32, 1024, 512, 4096, name=f"proj_bwd_dw_{tag}")
    token = hooks["after_dw"](dw_in_p, dw_out) if "after_dw" in hooks else None
    norm_w = p["norm_w"] if token is None else p["norm_w"] + token[0, 0]
    dx_in, dnorm_w = _proj_bwd_dx(dproj, p["w_in_p"], s["x"], norm_w, dx_out, name=f"proj_bwd_dx_{tag}")
    grads = dict(
        norm_w=dnorm_w[0], w_in_p=dw_in_p, ssd_conv_w=dconv_w[:SSD_CONV], ssd_conv_b=dconv_b[0],
        ssd_dt_bias=ddtb[0, :SSD_HEADS], ssd_a_log=da_log[0, :SSD_HEADS], ssd_d=dd[0, :SSD_HEADS],
        ssd_norm_w=dssd_norm_w[0], attn_sinks=dsinks[0, :ATTN_Q_HEADS], conf_dw_w=ddw_w[:CONF_KERNEL],
        conf_dw_b=ddw_b[0], conf_ln_w=dln_w[0], conf_ln_b=dln_b[0], w_out=dw_out)
    return dx_in, grads


def _local_step(x, target, param_fns, final_norm_w, first_after=None, bwd_hooks=None):
    nbatch, seq, d = x.shape
    xt = x.reshape(nbatch * seq, d)
    saved, layer_params = [], []
    for li, fn in enumerate(param_fns):
        p = fn(xt)
        layer_params.append(p)
        xt, s = _layer_fwd(xt, p, nbatch, seq, f"l{li}", after=first_after if li == 0 else None)
        saved.append(s)
    loss, dx, dfinal = _loss_head(xt, target.reshape(nbatch * seq, d), final_norm_w.reshape(1, d), name="loss_head")
    grads = [None] * len(layer_params)
    for li in reversed(range(len(layer_params))):
        hooks = bwd_hooks(li) if bwd_hooks is not None else None
        dx, grads[li] = _layer_bwd(dx, layer_params[li], saved[li], nbatch, seq, f"l{li}", hooks=hooks)
    return loss[0, 0], dx.reshape(nbatch, seq, d), grads, dfinal[0]


MESH = pl.DeviceIdType.MESH
N_CHIPS = 4


def _mesh_pos():
    return lax.axis_index("x"), lax.axis_index("y"), lax.axis_index("c")


def _other_chips(x, y):
    return [(1 - x, y), (x, 1 - y), (1 - x, 1 - y)]


def _gather_weights(big, small, name):
    nbig, nsmall = len(big), len(small)
    n_ici = 3 * (nbig + nsmall)
    n_fwd = 3 * nbig

    def body(*refs):
        ins = refs[:nbig + nsmall]
        outs = refs[nbig + nsmall:2 * (nbig + nsmall)]
        send_sems, recv_sems = refs[2 * (nbig + nsmall):]
        x, y, c = _mesh_pos()
        me = 2 * x + y
        sibling = (x, y, 1 - c)
        chips = _other_chips(x, y)

        def ici(a, j, origin, dest):
            if a < nbig:
                src = ins[a].at[c] if origin is None else outs[a].at[origin, c]
                dst = outs[a].at[me if origin is None else origin, c]
            else:
                src = ins[a] if origin is None else outs[a].at[origin]
                dst = outs[a].at[me if origin is None else origin]
            k = a * 3 + j
            return pltpu.make_async_remote_copy(src_ref=src, dst_ref=dst, send_sem=send_sems.at[k],
                                                recv_sem=recv_sems.at[k], device_id=dest, device_id_type=MESH)

        def fwd(a, j, origin, half):
            k = n_ici + a * 3 + j
            ref = outs[a].at[origin, half]
            return pltpu.make_async_remote_copy(src_ref=ref, dst_ref=ref, send_sem=send_sems.at[k],
                                                recv_sem=recv_sems.at[k], device_id=sibling, device_id_type=MESH)

        sends = []
        for j, (px, py) in enumerate(chips):
            for a in range(nbig + nsmall):
                cp = ici(a, j, None, (px, py, c))
                cp.start()
                sends.append(cp)
        for j, (px, py) in enumerate(chips):
            origin = 2 * px + py
            for a in range(nbig):
                ici(a, j, origin, (px, py, c)).wait_recv()
                cp = fwd(a, j, origin, c)
                cp.start()
                sends.append(cp)
        for j, (px, py) in enumerate(chips):
            origin = 2 * px + py
            for a in range(nbig, nbig + nsmall):
                ici(a, j, origin, (px, py, c)).wait_recv()
            for a in range(nbig):
                fwd(a, j, origin, 1 - c).wait_recv()
        for cp in sends:
            cp.wait_send()

    out_shape = tuple(jax.ShapeDtypeStruct((N_CHIPS,) + a.shape, a.dtype) for a in list(big) + list(small))
    return pl.pallas_call(
        body, name=name, out_shape=out_shape,
        in_specs=[ANY] * (nbig + nsmall), out_specs=tuple([ANY] * (nbig + nsmall)),
        scratch_shapes=[pltpu.SemaphoreType.DMA((n_ici + n_fwd,)), pltpu.SemaphoreType.DMA((n_ici + n_fwd,))],
    )(*big, *small)


HBM = pl.BlockSpec(memory_space=pltpu.HBM)
SEM = pl.BlockSpec(memory_space=pltpu.SEMAPHORE)
DATAFLOW = pltpu.SideEffectType.DATAFLOW_SIDE_EFFECTING


def _split_peers(pattern, x, y, c):
    if pattern == "swap":
        return [((x, y, 1 - c), 1 - c, None, None)]
    me = 2 * x + y
    return [((px, py, c), 2 * px + py if pattern == "scatter" else None, me, 2 * px + py)
            for px, py in _other_chips(x, y)]


def _split_land_shape(pattern, shape):
    return {"bcast": (N_CHIPS,) + shape, "scatter": shape, "swap": shape[:1] + shape[2:]}[pattern]


def _split_copies(pattern, srcs, lands, send_sems, recv_sems, waiting):
    x, y, c = _mesh_pos()
    peers = _split_peers(pattern, x, y, c)
    cps = []
    for j, (dev, src_slot, dst_slot, my_slot) in enumerate(peers):
        for a in range(len(srcs)):
            if src_slot is None:
                src = srcs[a]
            else:
                src = srcs[a].at[:, src_slot] if pattern == "swap" else srcs[a].at[src_slot]
            slot = my_slot if waiting else dst_slot
            dst = lands[a] if slot is None else lands[a].at[slot]
            k = a * len(peers) + j
            cps.append(pltpu.make_async_remote_copy(src_ref=src, dst_ref=dst, send_sem=send_sems[k],
                                                    recv_sem=recv_sems[k], device_id=dev, device_id_type=MESH))
    return cps


def _split_start(arrs, pattern, after, name):
    n = len(arrs)
    nsem = n * (1 if pattern == "swap" else N_CHIPS - 1)

    def body(*refs):
        srcs, lands = refs[:n], refs[n:2 * n]
        outs = refs[2 * n + 1:]
        for cp in _split_copies(pattern, srcs, lands, outs[:nsem], outs[nsem:2 * nsem], waiting=False):
            cp.start()
        outs[-1][...] = jnp.zeros_like(outs[-1])

    lands = [lax.empty(_split_land_shape(pattern, a.shape), a.dtype) for a in arrs]
    out_shape = ([pltpu.SemaphoreType.DMA(())] * (2 * nsem)
                 + [pltpu.HBM(a.shape, a.dtype) for a in arrs] + [pltpu.HBM(b.shape, b.dtype) for b in lands]
                 + [jax.ShapeDtypeStruct((SUBLANES, LANES), F32)])
    outs = pl.pallas_call(
        body, name=name, out_shape=tuple(out_shape),
        in_specs=[HBM] * (2 * n) + [ANY],
        out_specs=tuple([SEM] * (2 * nsem) + [HBM] * (2 * n) + [pl.BlockSpec(memory_space=pltpu.VMEM)]),
        input_output_aliases={a: 2 * nsem + a for a in range(2 * n)},
        compiler_params=pltpu.CompilerParams(has_side_effects=DATAFLOW),
    )(*[pltpu.with_memory_space_constraint(a, pltpu.HBM) for a in list(arrs) + lands], after)
    return outs[:-1], outs[-1]


def _split_wait(state, n, pattern, after, name):
    nsem = n * (1 if pattern == "swap" else N_CHIPS - 1)

    def body(*refs):
        srcs, lands = refs[:n], refs[n:2 * n]
        send_sems, recv_sems = refs[2 * n:2 * n + nsem], refs[2 * n + nsem:2 * n + 2 * nsem]
        for cp in _split_copies(pattern, srcs, lands, send_sems, recv_sems, waiting=True):
            cp.wait_send()
            cp.wait_recv()

    sems, thru = state[:2 * nsem], state[2 * nsem:]
    outs = pl.pallas_call(
        body, name=name, out_shape=tuple(pltpu.HBM(a.shape, a.dtype) for a in thru),
        in_specs=[HBM] * (2 * n) + [SEM] * (2 * nsem) + [ANY],
        out_specs=tuple([HBM] * (2 * n)),
        input_output_aliases={a: a for a in range(2 * n)},
        compiler_params=pltpu.CompilerParams(has_side_effects=DATAFLOW),
    )(*thru, *sems, after)
    return outs[n:]


def _pair_gather(arrs, layer, name):
    n = len(arrs)

    def body(*refs):
        outs = refs[n:2 * n]
        send_sems, recv_sems = refs[2 * n:]
        x, y, c = _mesh_pos()
        cps = [pltpu.make_async_remote_copy(src_ref=outs[a].at[layer, c], dst_ref=outs[a].at[layer, c],
                                            send_sem=send_sems.at[a], recv_sem=recv_sems.at[a],
                                            device_id=(x, y, 1 - c), device_id_type=MESH)
               for a in range(n)]
        for cp in cps:
            cp.start()
        for cp in cps:
            cp.wait()

    return pl.pallas_call(
        body, name=name, out_shape=tuple(jax.ShapeDtypeStruct(a.shape, a.dtype) for a in arrs),
        in_specs=[ANY] * n, out_specs=tuple([ANY] * n),
        input_output_aliases={a: a for a in range(n)},
        scratch_shapes=[pltpu.SemaphoreType.DMA((n,)), pltpu.SemaphoreType.DMA((n,))],
    )(*arrs)


N_DEV = 8


def _allreduce_small(pack, name):
    r = pack.shape[0]

    def body(p_ref, o_ref, land, send_sems, recv_sems):
        x, y, c = _mesh_pos()
        me = 4 * x + 2 * y + c
        cps = []
        for k in range(1, N_DEV):
            peer = (x ^ (k >> 2), y ^ ((k >> 1) & 1), c ^ (k & 1))
            cps.append(pltpu.make_async_remote_copy(src_ref=p_ref, dst_ref=land.at[me], send_sem=send_sems.at[k - 1],
                                                    recv_sem=recv_sems.at[k - 1], device_id=peer, device_id_type=MESH))
        for cp in cps:
            cp.start()
        land[me] = p_ref[...]
        for cp in cps:
            cp.wait()
        total = land[0]
        for d in range(1, N_DEV):
            total = total + land[d]
        o_ref[...] = total

    vm = pl.BlockSpec(memory_space=pltpu.VMEM)
    return pl.pallas_call(
        body, name=name, out_shape=jax.ShapeDtypeStruct(pack.shape, F32),
        in_specs=[vm], out_specs=vm,
        scratch_shapes=[pltpu.VMEM((N_DEV, r, LANES), F32), pltpu.SemaphoreType.DMA((N_DEV - 1,)),
                        pltpu.SemaphoreType.DMA((N_DEV - 1,))],
    )(pack)


BIG_ROWS = 128


def _cast_layer(w, layer, name):
    _, r, cdim = w.shape
    tr = BIG_ROWS

    def body(w_ref, o_ref):
        o_ref[...] = w_ref[...].astype(o_ref.dtype)

    return pl.pallas_call(
        body, name=name, out_shape=jax.ShapeDtypeStruct((r, cdim), MXU_DTYPE),
        grid=(r // tr,), in_specs=[pl.BlockSpec((None, tr, cdim), lambda i: (layer, i, 0))],
        out_specs=pl.BlockSpec((tr, cdim), lambda i: (i, 0)),
        compiler_params=_params(("parallel",)),
    )(w)


def _cast_cols_major(w_t, name):
    cdim, nl, r = w_t.shape
    tc = LANES

    def body(w_ref, *o_refs):
        for l in range(nl):
            o_refs[l][...] = w_ref[:, l, :].T.astype(o_refs[l].dtype)

    out = pl.BlockSpec((r, tc), lambda i: (0, i))
    return pl.pallas_call(
        body, name=name, out_shape=tuple(jax.ShapeDtypeStruct((r, cdim), MXU_DTYPE) for _ in range(nl)),
        grid=(pl.cdiv(cdim, tc),), in_specs=[pl.BlockSpec((tc, nl, r), lambda i: (i, 0, 0))],
        out_specs=tuple([out] * nl),
        compiler_params=_params(("parallel",)),
    )(w_t)


def _pair_sum(parts, sib, which, out_dtype, name):
    k, _, r, cdim = parts.shape
    tr = BIG_ROWS

    def body(sel_ref, p_ref, s_ref, o_ref):
        o_ref[...] = (p_ref[...] + s_ref[...]).astype(o_ref.dtype)

    grid_spec = pltpu.PrefetchScalarGridSpec(
        num_scalar_prefetch=1, grid=(k, r // tr),
        in_specs=[pl.BlockSpec((None, None, tr, cdim), lambda l, i, sel: (l, sel[0], i, 0)),
                  pl.BlockSpec((None, tr, cdim), lambda l, i, sel: (l, i, 0))],
        out_specs=pl.BlockSpec((None, tr, cdim), lambda l, i, sel: (l, i, 0)))
    return pl.pallas_call(
        body, name=name, out_shape=jax.ShapeDtypeStruct((k, r, cdim), out_dtype), grid_spec=grid_spec,
        compiler_params=_params(("parallel", "parallel")),
    )(which.reshape(1).astype(jnp.int32), parts, sib)


def _sum_lead(parts, into, layer, which, name):
    k, r, cdim = parts.shape
    tr = BIG_ROWS

    def body(sel_ref, p_ref, _, o_ref):
        total = p_ref[0].astype(F32)
        for a in range(1, k):
            total = total + p_ref[a].astype(F32)
        o_ref[...] = total

    grid_spec = pltpu.PrefetchScalarGridSpec(
        num_scalar_prefetch=1, grid=(r // tr,),
        in_specs=[pl.BlockSpec((k, tr, cdim), lambda i, sel: (0, i, 0)), ANY],
        out_specs=pl.BlockSpec((None, None, tr, cdim), lambda i, sel: (layer, sel[0], i, 0)))
    return pl.pallas_call(
        body, name=name, out_shape=jax.ShapeDtypeStruct(into.shape, F32), grid_spec=grid_spec,
        input_output_aliases={2: 0},
        compiler_params=_params(("parallel",)),
    )(which.reshape(1).astype(jnp.int32), parts, into)


def _adam_math(w, g, m, v):
    m2 = ADAM_B1 * m + (1.0 - ADAM_B1) * g
    v2 = ADAM_B2 * v + (1.0 - ADAM_B2) * (g * g)
    m_hat = m2 / (1.0 - ADAM_B1 ** ADAM_STEP)
    v_hat = v2 / (1.0 - ADAM_B2 ** ADAM_STEP)
    delta = -ADAM_LR * (m_hat / (jnp.sqrt(v_hat) + ADAM_EPS) + ADAM_WD * w)
    return delta, m2, v2


def _adam_big(w, g, m, v, name):
    nl, r, cdim = w.shape
    tr = BIG_ROWS

    def body(w_ref, g_ref, m_ref, v_ref, d_ref, mo_ref, vo_ref):
        delta, m2, v2 = _adam_math(w_ref[...], g_ref[...], m_ref[...], v_ref[...])
        d_ref[...] = delta
        mo_ref[...] = m2
        vo_ref[...] = v2

    blk = pl.BlockSpec((None, tr, cdim), lambda l, i: (l, i, 0))
    shp = jax.ShapeDtypeStruct(w.shape, F32)
    return pl.pallas_call(
        body, name=name, out_shape=(shp, shp, shp),
        grid=(nl, r // tr), in_specs=[blk] * 4, out_specs=(blk, blk, blk),
        compiler_params=_params(("parallel", "parallel")),
    )(w, g, m, v)


def _adam_cols_major(w, g, m, v, name):
    cdim, nl, r = w.shape
    tc = BIG_ROWS

    def body(w_ref, g_ref, m_ref, v_ref, d_ref, mo_ref, vo_ref):
        delta, m2, v2 = _adam_math(w_ref[...], g_ref[...], m_ref[...], v_ref[...])
        d_ref[...] = delta
        mo_ref[...] = m2
        vo_ref[...] = v2

    blk = pl.BlockSpec((tc, nl, r), lambda i: (i, 0, 0))
    shp = jax.ShapeDtypeStruct(w.shape, F32)
    return pl.pallas_call(
        body, name=name, out_shape=(shp, shp, shp),
        grid=(pl.cdiv(cdim, tc),), in_specs=[blk] * 4, out_specs=(blk, blk, blk),
        compiler_params=_params(("parallel",)),
    )(w, g, m, v)


def _adam_small(ws, gs, ms, vs, name):
    n = len(ws)

    def body(*refs):
        w_refs, g_refs, m_refs, v_refs = (refs[k * n:(k + 1) * n] for k in range(4))
        d_refs, mo_refs, vo_refs = (refs[(4 + k) * n:(5 + k) * n] for k in range(3))
        for a in range(n):
            delta, m2, v2 = _adam_math(w_refs[a][...], g_refs[a][...], m_refs[a][...], v_refs[a][...])
            d_refs[a][...] = delta
            mo_refs[a][...] = m2
            vo_refs[a][...] = v2

    shapes = tuple(jax.ShapeDtypeStruct(w.shape, F32) for w in ws)
    vm = pl.BlockSpec(memory_space=pltpu.VMEM)
    outs = pl.pallas_call(body, name=name, out_shape=shapes * 3, in_specs=[vm] * (4 * n),
                          out_specs=tuple([vm] * (3 * n)))(*ws, *gs, *ms, *vs)
    return outs[:n], outs[n:2 * n], outs[2 * n:]


PACK_TILE = SUBLANES * LANES


def _pack(arrays):
    rows = []
    for a in arrays:
        flat = a.reshape(-1)
        pad = (-flat.shape[0]) % PACK_TILE
        if pad:
            flat = jnp.concatenate([flat, jnp.zeros((pad,), flat.dtype)])
        rows.append(flat.reshape(-1, LANES))
    return jnp.concatenate(rows, axis=0)


def _unpack(pack, shapes):
    outs, row = [], 0
    for shp in shapes:
        n = int(np.prod(shp))
        nrows = -(-n // PACK_TILE) * SUBLANES
        outs.append(pack[row:row + nrows].reshape(-1)[:n].reshape(shp))
        row += nrows
    return outs


SMALL = ["norm_w", "ssd_conv_b", "ssd_dt_bias", "ssd_a_log", "ssd_d", "ssd_norm_w", "attn_sinks",
         "conf_dw_b", "conf_ln_w", "conf_ln_b"]
WEIGHTS = ["norm_w", "w_in", "ssd_conv_w", "ssd_conv_b", "ssd_dt_bias", "ssd_a_log", "ssd_d", "ssd_norm_w",
           "attn_sinks", "conf_dw_w", "conf_dw_b", "conf_ln_w", "conf_ln_b", "w_out", "final_norm_w"]


def kernel(x, norm_w, w_in, ssd_conv_w, ssd_conv_b, ssd_dt_bias, ssd_a_log, ssd_d, ssd_norm_w, attn_sinks, conf_dw_w, conf_dw_b, conf_ln_w, conf_ln_b, w_out, final_norm_w, loss_target, m_norm_w, m_w_in, m_ssd_conv_w, m_ssd_conv_b, m_ssd_dt_bias, m_ssd_a_log, m_ssd_d, m_ssd_norm_w, m_attn_sinks, m_conf_dw_w, m_conf_dw_b, m_conf_ln_w, m_conf_ln_b, m_w_out, m_final_norm_w, v_norm_w, v_w_in, v_ssd_conv_w, v_ssd_conv_b, v_ssd_dt_bias, v_ssd_a_log, v_ssd_d, v_ssd_norm_w, v_attn_sinks, v_conf_dw_w, v_conf_dw_b, v_conf_ln_w, v_conf_ln_b, v_w_out, v_final_norm_w):
    w = dict(norm_w=norm_w, w_in=w_in, ssd_conv_w=ssd_conv_w, ssd_conv_b=ssd_conv_b, ssd_dt_bias=ssd_dt_bias,
             ssd_a_log=ssd_a_log, ssd_d=ssd_d, ssd_norm_w=ssd_norm_w, attn_sinks=attn_sinks, conf_dw_w=conf_dw_w,
             conf_dw_b=conf_dw_b, conf_ln_w=conf_ln_w, conf_ln_b=conf_ln_b, w_out=w_out, final_norm_w=final_norm_w)
    m = dict(norm_w=m_norm_w, w_in=m_w_in, ssd_conv_w=m_ssd_conv_w, ssd_conv_b=m_ssd_conv_b,
             ssd_dt_bias=m_ssd_dt_bias, ssd_a_log=m_ssd_a_log, ssd_d=m_ssd_d, ssd_norm_w=m_ssd_norm_w,
             attn_sinks=m_attn_sinks, conf_dw_w=m_conf_dw_w, conf_dw_b=m_conf_dw_b, conf_ln_w=m_conf_ln_w,
             conf_ln_b=m_conf_ln_b, w_out=m_w_out, final_norm_w=m_final_norm_w)
    v = dict(norm_w=v_norm_w, w_in=v_w_in, ssd_conv_w=v_ssd_conv_w, ssd_conv_b=v_ssd_conv_b,
             ssd_dt_bias=v_ssd_dt_bias, ssd_a_log=v_ssd_a_log, ssd_d=v_ssd_d, ssd_norm_w=v_ssd_norm_w,
             attn_sinks=v_attn_sinks, conf_dw_w=v_conf_dw_w, conf_dw_b=v_conf_dw_b, conf_ln_w=v_conf_ln_w,
             conf_ln_b=v_conf_ln_b, w_out=v_w_out, final_norm_w=v_final_norm_w)
    depth = w_in.shape[0]
    me = 2 * lax.axis_index("x") + lax.axis_index("y")

    assert depth == 2
    w_in_t = jnp.transpose(w_in, (2, 0, 1))
    w_in_b = _cast_cols_major(w_in_t, name="cast_w_in")
    w_out_b = [_cast_layer(w_out, li, name=f"cast_w_out_l{li}") for li in range(depth)]
    own0 = [w_in_b[0].reshape((2, -1) + w_in_b[0].shape[1:]), ssd_conv_w, conf_dw_w]
    gathered0 = _gather_weights(own0[:1], own0[1:], name="gather_weights_l0")
    g_in0, g_conv, g_dw = [lax.dynamic_update_index_in_dim(g_all, mine, me, 0)
                           for g_all, mine in zip(gathered0, own0)]
    own1 = [w_out_b[0], w_in_b[1], w_out_b[1]]
    pending1, token1 = _split_start(own1, "bcast", gathered0[0], name="gather_rest_start")
    rest = {}

    def small_full(li):
        return (jnp.concatenate([g_conv[p, li] for p in range(N_CHIPS)], axis=1),
                jnp.concatenate([g_dw[p, li] for p in range(N_CHIPS)], axis=1))

    def w_out_l0(after):
        landed = _split_wait(pending1, len(own1), "bcast", after, name="gather_rest_wait")
        rest["landed"] = [lax.dynamic_update_index_in_dim(g_all, mine, me, 0) for g_all, mine in zip(landed, own1)]
        return rest["landed"][0].reshape(-1, w_out.shape[2])

    def params_l0(_):
        w_in_p = _padded_from_chips([g_in0[p].reshape(w_in_b[0].shape) for p in range(N_CHIPS)])
        return _layer_params(0, w_in_p, w_out_l0, *small_full(0), w)

    def params_l1(_):
        _, g_in1, g_out1 = rest["landed"]
        w_in_p = _padded_from_chips([g_in1[p] for p in range(N_CHIPS)])
        return _layer_params(1, w_in_p, g_out1.reshape(-1, g_out1.shape[-1]), *small_full(1), w)

    c = lax.axis_index("c")
    cols = w_in.shape[2]
    rows_out = w_out.shape[1]

    def grad_parts(g):
        dw = g["w_in_p"]
        return [dw.reshape(1, 2, dw.shape[0] // 2, dw.shape[1]),
                g["w_out"].reshape(N_CHIPS, 2, rows_out // 2, D_MODEL)]

    def pair_sums(parts, sib, tag):
        s_in, s_out = [_pair_sum(p, sb, c, MXU_DTYPE, name=f"grad_pair_sum_{k}_{tag}")
                       for k, (p, sb) in enumerate(zip(parts, sib))]
        return [jnp.stack([_chip_part_from_padded(s_in[0], p, cols) for p in range(N_CHIPS)]), s_out]

    split = {"reduced": [lax.empty((depth, 2, w_in.shape[1] // 2, cols), F32),
                         lax.empty((depth, 2, rows_out // 2, D_MODEL), F32)]}

    def chip_sums(landed, sent, li):
        filled = [lax.dynamic_update_index_in_dim(r, lax.dynamic_index_in_dim(sk, me, 0, keepdims=False), me, 0)
                  for r, sk in zip(landed, sent)]
        halves = [_sum_lead(r, into, li, c, name=f"grad_chip_sum_{k}_l{li}")
                  for k, (r, into) in enumerate(zip(filled, split["reduced"]))]
        split["reduced"] = list(_pair_gather(halves, li, name=f"grad_pair_gather_l{li}"))

    def bwd_hooks(li):
        def after_dw(dw_in_p, dw_out):
            parts = grad_parts(dict(w_in_p=dw_in_p, w_out=dw_out))
            state, token = _split_start(parts, "swap", dw_out, name=f"grad_swap_l{li}_start")
            split[f"swap{li}"] = (parts, state)
            return token

        hooks = {"after_dw": after_dw}
        if li == depth - 2:
            parts, swap_state = split[f"swap{depth - 1}"]

            def after_dycat(dycat):
                sib = _split_wait(swap_state, len(parts), "swap", dycat, name="grad_swap_l1_wait")
                split["sent"] = pair_sums(parts, sib, "l1")
                split["scatter"], token = _split_start(split["sent"], "scatter", split["sent"][0],
                                                       name="grad_scatter_l1_start")
                return token

            def after_attn(dproj):
                landed = _split_wait(split["scatter"], len(parts), "scatter", dproj, name="grad_scatter_l1_wait")
                chip_sums(landed, split["sent"], depth - 1)

            hooks.update(after_dycat=after_dycat, after_attn=after_attn)
        return hooks

    loss, grad_x, grads, dfinal = _local_step(x, loss_target, [params_l0, params_l1], final_norm_w,
                                              first_after=token1, bwd_hooks=bwd_hooks)

    parts0, swap0 = split["swap0"]
    sent0 = pair_sums(parts0, _split_wait(swap0, len(parts0), "swap", grad_x, name="grad_swap_l0_wait"), "l0")
    scatter0, token0 = _split_start(sent0, "scatter", sent0[0], name="grad_scatter_l0_start")

    small_list = [grads[li][n] for li in range(depth) for n in SMALL]
    small_list += [grads[li][n] for li in range(depth) for n in ("ssd_conv_w", "conf_dw_w")]
    small_list += [dfinal, loss.reshape(1)]
    small_shapes = [a.shape for a in small_list]
    reduced = _unpack(_allreduce_small(_pack(small_list) + token0[0, 0], name="allreduce_small"), small_shapes)
    ns = len(SMALL)
    g = {n: jnp.stack([reduced[li * ns + i] for li in range(depth)]) for i, n in enumerate(SMALL)}
    conv_w_cols, dw_w_cols = ssd_conv_w.shape[2], conf_dw_w.shape[2]
    g["ssd_conv_w"] = jnp.stack([lax.dynamic_slice_in_dim(reduced[depth * ns + 2 * li], me * conv_w_cols,
                                                          conv_w_cols, axis=1) for li in range(depth)])
    g["conf_dw_w"] = jnp.stack([lax.dynamic_slice_in_dim(reduced[depth * ns + 2 * li + 1], me * dw_w_cols,
                                                         dw_w_cols, axis=1) for li in range(depth)])
    g["final_norm_w"] = reduced[-2]
    loss_total = reduced[-1][0]

    small_names = [n for n in WEIGHTS if n not in ("w_in", "w_out")]

    def as2d(a):
        return a.reshape(1, -1) if a.ndim == 1 else a

    deltas, new_ms, new_vs = _adam_small(*[[as2d(src[n]) for n in small_names] for src in (w, g, m, v)],
                                         name="adam_small")

    chip_sums(_split_wait(scatter0, len(sent0), "scatter", deltas[0], name="grad_scatter_l0_wait"), sent0, 0)
    g_w_in = split["reduced"][0].reshape(w_in.shape)
    g_w_out = split["reduced"][1].reshape(w_out.shape)

    outs_g, outs_d, outs_m, outs_v = {"w_in": g_w_in, "w_out": g_w_out}, {}, {}, {}
    to_cols, from_cols = (2, 0, 1), (1, 2, 0)
    outs_d["w_in"], outs_m["w_in"], outs_v["w_in"] = [
        jnp.transpose(a, from_cols) for a in _adam_cols_major(
            *[jnp.transpose(a, to_cols) for a in (w_in, g_w_in, m_w_in, v_w_in)], name="adam_w_in")]
    outs_d["w_out"], outs_m["w_out"], outs_v["w_out"] = _adam_big(w_out, g_w_out, m_w_out, v_w_out,
                                                                  name="adam_w_out")
    for n, dn, mn, vn in zip(small_names, deltas, new_ms, new_vs):
        outs_g[n], outs_d[n], outs_m[n], outs_v[n] = (g[n], dn.reshape(w[n].shape), mn.reshape(w[n].shape),
                                                      vn.reshape(w[n].shape))
    return (loss_total, grad_x, *[outs_g[n] for n in WEIGHTS], *[outs_d[n] for n in WEIGHTS],
            *[outs_m[n] for n in WEIGHTS], *[outs_v[n] for n in WEIGHTS])
```

```python
import functools
import math

import jax
import jax.numpy as jnp
import numpy as np
from jax import lax
from jax.experimental import pallas as pl
from jax.experimental.pallas import tpu as pltpu

F32 = jnp.float32
BF16 = jnp.bfloat16
MXU_DTYPE = BF16

D_MODEL = 1024
DEPTH = 2
SSD_HEADS = 16
SSD_HEAD_DIM = 64
SSD_STATE = 128
SSD_CONV = 4
CHUNK = 128
SSD_CONV_DIM = 1536
ATTN_HEAD_DIM = 64
ATTN_Q_HEADS = 8
WINDOW = 128
CONF_WIDTH = 512
CONF_KERNEL = 31
MIX_WIDTH = 2048
D_IN_PROJ = 5392
EPS = 1e-5

ADAM_LR = 0.001
ADAM_B1 = 0.9
ADAM_B2 = 0.999
ADAM_EPS = 1e-08
ADAM_WD = 0.01
ADAM_STEP = 10

LANES = 128
SUBLANES = 8
VMEM_LIMIT = 48 * 1024 * 1024

NP = 5632
OFF_ZA, OFF_Q, OFF_K, OFF_V, OFF_DT = 0, 512, 1024, 1152, 1280
ATTN_GROUP = 1536
OFF_CONF, OFF_ZC = 1536, 2560
CONF_GROUP = 1536
OFF_ZS = 3072
OFF_XBC = 4096
SECTIONS = ((0, 1024, OFF_ZS), (1024, 1536, OFF_ZA), (1536, 2048, OFF_ZC), (2048, 3584, OFF_XBC),
            (3584, 3600, OFF_DT), (3600, 4368, OFF_Q), (4368, 5392, OFF_CONF))

YCAT_ATTN, YCAT_CONF = 1024, 1536
ANY = pl.BlockSpec(memory_space=pl.ANY)

NN = (((1,), (0,)), ((), ()))
NT = (((1,), (1,)), ((), ()))
TN = (((0,), (0,)), ((), ()))


def _params(sem):
    return pltpu.CompilerParams(dimension_semantics=sem, vmem_limit_bytes=VMEM_LIMIT)


def _dot(a, b, dims=NN):
    return lax.dot_general(a.astype(MXU_DTYPE), b.astype(MXU_DTYPE), dims, preferred_element_type=F32)


def _split_bf16(a, passes):
    pieces = []
    r = a
    for _ in range(passes):
        p = r.astype(BF16)
        pieces.append(p)
        r = r - p.astype(F32)
    return pieces


def _xdot(a, sel, dims=NN, passes=2):
    out = None
    for p in _split_bf16(a, passes):
        t = lax.dot_general(p, sel, dims, preferred_element_type=F32)
        out = t if out is None else out + t
    return out


def _xdot_r(sel, b, dims=NN, passes=3):
    out = None
    for p in _split_bf16(b, passes):
        t = lax.dot_general(sel, p, dims, preferred_element_type=F32)
        out = t if out is None else out + t
    return out


def _sigmoid(x):
    return 1.0 / (1.0 + jnp.exp(-x))


def _silu(x):
    return x * _sigmoid(x)


def _dsilu(x):
    s = _sigmoid(x)
    return s * (1.0 + x * (1.0 - s))


def _softplus(x):
    return jnp.maximum(x, 0.0) + jnp.log(1.0 + jnp.exp(-jnp.abs(x)))


def _rowsum8(x):
    r, c = x.shape
    return jnp.sum(x.reshape(r // SUBLANES, SUBLANES, c), axis=0)


def _iota(shape, dim):
    return lax.broadcasted_iota(jnp.int32, shape, dim)


def _matmul(a, b, form, out_dtype, tm, tn, tk, name, residual=None, after=None):
    if form == "nn":
        (m, k), n = a.shape, b.shape[1]
    elif form == "nt":
        (m, k), n = a.shape, b.shape[0]
    else:
        (k, m), n = a.shape, b.shape[1]
    tm, tn, tk = min(tm, m), min(tn, n), min(tk, k)
    assert m % tm == 0 and n % tn == 0 and k % tk == 0, (name, m, n, k, tm, tn, tk)
    if form == "nn":
        a_spec = pl.BlockSpec((tm, tk), lambda i, j, s: (i, s))
        b_spec = pl.BlockSpec((tk, tn), lambda i, j, s: (s, j))
        dims = NN
    elif form == "nt":
        (m, k), n = a.shape, b.shape[0]
        a_spec = pl.BlockSpec((tm, tk), lambda i, j, s: (i, s))
        b_spec = pl.BlockSpec((tn, tk), lambda i, j, s: (j, s))
        dims = NT
    else:
        (k, m), n = a.shape, b.shape[1]
        a_spec = pl.BlockSpec((tk, tm), lambda i, j, s: (s, i))
        b_spec = pl.BlockSpec((tk, tn), lambda i, j, s: (s, j))
        dims = TN
    nk = k // tk
    has_res = residual is not None
    deps = [] if after is None else [after]

    def body_single(a_ref, b_ref, *rest):
        o = _dot(a_ref[...], b_ref[...], dims)
        if has_res:
            o = o + rest[0][...]
        rest[-1][...] = o.astype(out_dtype)

    def body(a_ref, b_ref, *rest):
        r_ref = rest[0] if has_res else None
        o_ref, acc = rest[-2:]
        s = pl.program_id(2)

        @pl.when(s == 0)
        def _():
            acc[...] = jnp.zeros_like(acc)

        acc[...] += _dot(a_ref[...], b_ref[...], dims)

        @pl.when(s == nk - 1)
        def _():
            o = acc[...]
            if has_res:
                o = o + r_ref[...]
            o_ref[...] = o.astype(out_dtype)

    in_specs = [a_spec, b_spec]
    args = [a, b]
    if has_res:
        in_specs.append(pl.BlockSpec((tm, tn), lambda i, j, s: (i, j)))
        args.append(residual)
    in_specs += [ANY] * len(deps)
    args += deps
    return pl.pallas_call(
        body_single if nk == 1 else body, name=name,
        out_shape=jax.ShapeDtypeStruct((m, n), out_dtype),
        grid=(m // tm, n // tn, nk),
        in_specs=in_specs,
        out_specs=pl.BlockSpec((tm, tn), lambda i, j, s: (i, j)),
        scratch_shapes=[] if nk == 1 else [pltpu.VMEM((tm, tn), F32)],
        compiler_params=_params(("parallel", "parallel", "arbitrary")),
    )(*args)


ROW_TILE = 256


PROJ_FWD_TM, PROJ_FWD_TN = 1024, 512


def _proj_fwd(x, w, w_in_p, name, after=None):
    t, d = x.shape
    n = w_in_p.shape[1]
    tm, tn = min(PROJ_FWD_TM, t), PROJ_FWD_TN
    assert t % tm == 0 and n % tn == 0
    deps = [] if after is None else [after]

    def body(x_ref, w_ref, b_ref, *rest):
        o_ref, ot_ref, h_scr = rest[len(deps):]

        @pl.when(pl.program_id(1) == 0)
        def _():
            xv = x_ref[...]
            rstd = lax.rsqrt(jnp.mean(xv * xv, axis=-1, keepdims=True) + EPS)
            h = xv * rstd * w_ref[...]
            h_scr[...] = h.astype(h_scr.dtype)
            ot_ref[...] = h.T.astype(ot_ref.dtype)

        o_ref[...] = _dot(h_scr[...], b_ref[...])

    return pl.pallas_call(
        body, name=name,
        out_shape=(jax.ShapeDtypeStruct((t, n), F32), jax.ShapeDtypeStruct((d, t), MXU_DTYPE)),
        grid=(t // tm, n // tn),
        in_specs=[pl.BlockSpec((tm, d), lambda i, j: (i, 0)), pl.BlockSpec((1, d), lambda i, j: (0, 0)),
                  pl.BlockSpec((d, tn), lambda i, j: (0, j))] + [ANY] * len(deps),
        out_specs=(pl.BlockSpec((tm, tn), lambda i, j: (i, j)), pl.BlockSpec((d, tm), lambda i, j: (0, i))),
        scratch_shapes=[pltpu.VMEM((tm, d), MXU_DTYPE)],
        compiler_params=_params(("parallel", "arbitrary")),
    )(x, w, w_in_p, *deps)


PROJ_BWD_TM, PROJ_BWD_TK = 1024, 1408


def _proj_bwd_dx(dproj, w_in_p, x, w, dres, name):
    t, d = x.shape
    kdim = dproj.shape[1]
    tm, tk = min(PROJ_BWD_TM, t), PROJ_BWD_TK
    nt, nk = t // tm, kdim // tk
    assert t % tm == 0 and kdim % tk == 0

    def body(a_ref, b_ref, x_ref, w_ref, dr_ref, dx_ref, dw_ref, acc, wacc):
        i, s = pl.program_id(0), pl.program_id(1)

        @pl.when((i == 0) & (s == 0))
        def _():
            wacc[...] = jnp.zeros_like(wacc)

        @pl.when(s == 0)
        def _():
            acc[...] = jnp.zeros_like(acc)

        acc[...] += _dot(a_ref[...], b_ref[...], NT)

        @pl.when(s == nk - 1)
        def _():
            xv = x_ref[...]
            rstd = lax.rsqrt(jnp.mean(xv * xv, axis=-1, keepdims=True) + EPS)
            xh = xv * rstd
            dhv = acc[...]
            g = dhv * w_ref[...]
            dx_ref[...] = dr_ref[...] + rstd * (g - xh * jnp.mean(g * xh, axis=-1, keepdims=True))
            wacc[...] += _rowsum8(dhv * xh)

        @pl.when((i == nt - 1) & (s == nk - 1))
        def _():
            dw_ref[...] = jnp.sum(wacc[...], axis=0, keepdims=True)

    row = pl.BlockSpec((tm, d), lambda i, s: (i, 0))
    vec = pl.BlockSpec((1, d), lambda i, s: (0, 0))
    return pl.pallas_call(
        body, name=name,
        out_shape=(jax.ShapeDtypeStruct((t, d), F32), jax.ShapeDtypeStruct((1, d), F32)),
        grid=(nt, nk),
        in_specs=[pl.BlockSpec((tm, tk), lambda i, s: (i, s)), pl.BlockSpec((d, tk), lambda i, s: (0, s)),
                  row, vec, row],
        out_specs=(row, vec),
        scratch_shapes=[pltpu.VMEM((tm, d), F32), pltpu.VMEM((SUBLANES, d), F32)],
        compiler_params=_params(("arbitrary", "arbitrary")),
    )(dproj, w_in_p, x, w, dres)


def _loss_head(xf, target, w, name):
    t, d = xf.shape
    tm = ROW_TILE
    nt = t // tm

    def body(x_ref, t_ref, w_ref, loss_ref, dx_ref, dw_ref, lacc, wacc):
        i = pl.program_id(0)

        @pl.when(i == 0)
        def _():
            lacc[...] = jnp.zeros_like(lacc)
            wacc[...] = jnp.zeros_like(wacc)

        xv = x_ref[...]
        rstd = lax.rsqrt(jnp.mean(xv * xv, axis=-1, keepdims=True) + EPS)
        xh = xv * rstd
        err = xh * w_ref[...] - t_ref[...]
        lacc[...] += jnp.sum(err * err)
        dy = err * (1.0 / d)
        g = dy * w_ref[...]
        dx_ref[...] = rstd * (g - xh * jnp.mean(g * xh, axis=-1, keepdims=True))
        wacc[...] += _rowsum8(dy * xh)

        @pl.when(i == nt - 1)
        def _():
            loss_ref[...] = lacc[...] * (0.5 / d)
            dw_ref[...] = jnp.sum(wacc[...], axis=0, keepdims=True)

    row = pl.BlockSpec((tm, d), lambda i: (i, 0))
    vec = pl.BlockSpec((1, d), lambda i: (0, 0))
    return pl.pallas_call(
        body, name=name,
        out_shape=(jax.ShapeDtypeStruct((SUBLANES, LANES), F32), jax.ShapeDtypeStruct((t, d), F32),
                   jax.ShapeDtypeStruct((1, d), F32)),
        grid=(nt,),
        in_specs=[row, row, vec],
        out_specs=(pl.BlockSpec((SUBLANES, LANES), lambda i: (0, 0)), row, vec),
        scratch_shapes=[pltpu.VMEM((SUBLANES, LANES), F32), pltpu.VMEM((SUBLANES, d), F32)],
        compiler_params=_params(("arbitrary",)),
    )(xf, target, w)


CONV_TILE = 512
CONV_COLS = 512
CONV_SUB_ROWS = 128
CONV_SUB_COLS = LANES


def _conv_halo(k):
    return SUBLANES if k - 1 <= SUBLANES else 32


def _conv_subtiles(tm, cw):
    return [(r0, c0) for r0 in range(0, tm, CONV_SUB_ROWS) for c0 in range(0, cw, CONV_SUB_COLS)]


def _conv_use_shifted(k):
    return k > SUBLANES


def _conv_shift_scratch(k, rows, cw):
    return [pltpu.VMEM((SUBLANES - 1, rows - SUBLANES, cw), F32)] if _conv_use_shifted(k) else []


def _conv_fill_shifted(ext, sh):
    n = sh.shape[1]
    for b in range(1, SUBLANES):
        sh[b - 1] = ext[b:b + n, :]


def _conv_rows(ext, sh, start, rows, cs):
    b = start % SUBLANES
    if b == 0 or not sh:
        return ext[start:start + rows, cs]
    return sh[0][b - 1, start - b:start - b + rows, cs]


def _conv_fwd(src, col0, width, w, bias, k, seq, name):
    t = src.shape[0]
    tm, cw, halo = CONV_TILE, CONV_COLS, _conv_halo(k)
    sr, sc = CONV_SUB_ROWS, CONV_SUB_COLS
    p = k - 1
    cb0 = col0 // cw
    kp = w.shape[0]

    shifted = _conv_use_shifted(k)

    def body(x_ref, h_ref, w_ref, b_ref, o_ref, ext, *sh):
        i = pl.program_id(0)
        seq_start = (i * tm) % seq == 0
        ext[halo:, :] = x_ref[...]
        ext[:halo, :] = jnp.where(seq_start, 0.0, h_ref[...])
        if shifted:
            _conv_fill_shifted(ext, sh[0])
        for r0, c0 in _conv_subtiles(tm, cw):
            cs = slice(c0, c0 + sc)
            acc = jnp.zeros((sr, sc), F32) + b_ref[:, cs]
            for j in range(k):
                acc = acc + w_ref[j:j + 1, cs] * _conv_rows(ext, sh, r0 + halo - p + j, sr, cs)
            o_ref[r0:r0 + sr, cs] = acc

    return pl.pallas_call(
        body, name=name,
        out_shape=jax.ShapeDtypeStruct((t, width), F32),
        grid=(t // tm, width // cw),
        in_specs=[pl.BlockSpec((tm, cw), lambda i, j: (i, cb0 + j)),
                  pl.BlockSpec((halo, cw), lambda i, j: (jnp.maximum(i * (tm // halo) - 1, 0), cb0 + j)),
                  pl.BlockSpec((kp, cw), lambda i, j: (0, j)),
                  pl.BlockSpec((1, cw), lambda i, j: (0, j))],
        out_specs=pl.BlockSpec((tm, cw), lambda i, j: (i, j)),
        scratch_shapes=[pltpu.VMEM((halo + tm, cw), F32)] + _conv_shift_scratch(k, halo + tm, cw),
        compiler_params=_params(("parallel", "parallel")),
    )(src, src, w, bias)


def _conv_bwd(dy, src, col0, width, w, k, seq, name, into=None):
    t = src.shape[0]
    tm, cw, halo = CONV_TILE, CONV_COLS, _conv_halo(k)
    sr, sc = CONV_SUB_ROWS, CONV_SUB_COLS
    p = k - 1
    cb0 = col0 // cw
    kp = w.shape[0]
    nt = t // tm
    last_halo = t // halo - 1

    shifted = _conv_use_shifted(k)

    def body(dy_ref, dn_ref, x_ref, xp_ref, w_ref, *rest):
        if into is not None:
            rest = rest[1:]
        dx_ref, dw_ref, db_ref, dyext, xext, wacc, bacc = rest[:7]
        sh = rest[7:]
        i = pl.program_id(1)
        dysh, xsh = (sh[:1], sh[1:]) if shifted else ((), ())

        @pl.when(i == 0)
        def _():
            wacc[...] = jnp.zeros_like(wacc)
            bacc[...] = jnp.zeros_like(bacc)

        seq_start = (i * tm) % seq == 0
        seq_end = ((i + 1) * tm) % seq == 0
        dyext[:tm, :] = dy_ref[...]
        dyext[tm:, :] = jnp.where(seq_end, 0.0, dn_ref[...])
        xext[halo:, :] = x_ref[...]
        xext[:halo, :] = jnp.where(seq_start, 0.0, xp_ref[...])
        if shifted:
            _conv_fill_shifted(dyext, dysh[0])
            _conv_fill_shifted(xext, xsh[0])
        for r0, c0 in _conv_subtiles(tm, cw):
            cs = slice(c0, c0 + sc)
            dyv = dy_ref[r0:r0 + sr, cs]
            acc = jnp.zeros((sr, sc), F32)
            for j in range(k):
                acc = acc + w_ref[j:j + 1, cs] * _conv_rows(dyext, dysh, r0 + p - j, sr, cs)
                wacc[j, :, cs] += _rowsum8(dyv * _conv_rows(xext, xsh, r0 + halo - p + j, sr, cs))
            dx_ref[r0:r0 + sr, cs] = acc.astype(dx_ref.dtype)
            bacc[:, cs] += _rowsum8(dyv)

        @pl.when(i == nt - 1)
        def _():
            dw_ref[...] = jnp.zeros_like(dw_ref)
            for j in range(k):
                dw_ref[j:j + 1, :] = jnp.sum(wacc[j], axis=0, keepdims=True)
            db_ref[...] = jnp.sum(bacc[...], axis=0, keepdims=True)

    if into is None:
        dx_shape = jax.ShapeDtypeStruct((t, width), F32)
        dx_spec = pl.BlockSpec((tm, cw), lambda j, i: (i, j))
        extra_specs, extra_args, aliases = [], [], {}
    else:
        dx_shape = jax.ShapeDtypeStruct(into.shape, into.dtype)
        dx_spec = pl.BlockSpec((tm, cw), lambda j, i: (i, cb0 + j))
        extra_specs, extra_args, aliases = [ANY], [into], {5: 0}
    return pl.pallas_call(
        body, name=name,
        out_shape=(dx_shape, jax.ShapeDtypeStruct((kp, width), F32), jax.ShapeDtypeStruct((1, width), F32)),
        grid=(width // cw, nt),
        in_specs=[pl.BlockSpec((tm, cw), lambda j, i: (i, j)),
                  pl.BlockSpec((halo, cw), lambda j, i: (jnp.minimum((i + 1) * (tm // halo), last_halo), j)),
                  pl.BlockSpec((tm, cw), lambda j, i: (i, cb0 + j)),
                  pl.BlockSpec((halo, cw), lambda j, i: (jnp.maximum(i * (tm // halo) - 1, 0), cb0 + j)),
                  pl.BlockSpec((kp, cw), lambda j, i: (0, j))] + extra_specs,
        out_specs=(dx_spec,
                   pl.BlockSpec((kp, cw), lambda j, i: (0, j)),
                   pl.BlockSpec((1, cw), lambda j, i: (0, j))),
        input_output_aliases=aliases,
        scratch_shapes=[pltpu.VMEM((tm + halo, cw), F32), pltpu.VMEM((halo + tm, cw), F32),
                        pltpu.VMEM((kp, SUBLANES, cw), F32), pltpu.VMEM((SUBLANES, cw), F32)]
        + 2 * _conv_shift_scratch(k, halo + tm, cw),
        compiler_params=_params(("parallel", "arbitrary")),
    )(dy, dy, src, src, w, *extra_args)


def _conf_specs(tm, cw, halo, order):
    cb = OFF_CONF // cw

    def blk(col):
        return pl.BlockSpec((tm, cw), lambda *g: (order(*g), col))

    def prev(col):
        return pl.BlockSpec((halo, cw), lambda *g: (jnp.maximum(order(*g) * (tm // halo) - 1, 0), col))

    return blk(cb), prev(cb), blk(cb + 1), prev(cb + 1)


def _glu_window(ext, a_ref, ah_ref, g_ref, gh_ref, seq_start, halo):
    ext[halo:, :] = a_ref[...] * _sigmoid(g_ref[...])
    ext[:halo, :] = jnp.where(seq_start, 0.0, ah_ref[...] * _sigmoid(gh_ref[...]))


def _conf_fwd(proj, w, bias, ln_w, ln_b, ycat, seq, name):
    t = proj.shape[0]
    k = CONF_KERNEL
    tm, cw, halo = CONV_TILE, CONF_WIDTH, _conv_halo(k)
    sr, sc = CONV_SUB_ROWS, CONV_SUB_COLS
    p = k - 1
    kp = w.shape[0]

    def body(a_ref, ah_ref, g_ref, gh_ref, z_ref, w_ref, b_ref, lw_ref, lb_ref, _, c1_ref, y_ref, ext, sh):
        i = pl.program_id(0)
        _glu_window(ext, a_ref, ah_ref, g_ref, gh_ref, (i * tm) % seq == 0, halo)
        _conv_fill_shifted(ext, sh)
        for r0, c0 in _conv_subtiles(tm, cw):
            cs = slice(c0, c0 + sc)
            acc = jnp.zeros((sr, sc), F32) + b_ref[:, cs]
            for j in range(k):
                acc = acc + w_ref[j:j + 1, cs] * _conv_rows(ext, (sh,), r0 + halo - p + j, sr, cs)
            c1_ref[r0:r0 + sr, cs] = acc
        for r0 in range(0, tm, sr):
            rows = slice(r0, r0 + sr)
            cv = c1_ref[rows, :]
            xc = cv - jnp.mean(cv, axis=-1, keepdims=True)
            rstd = lax.rsqrt(jnp.mean(xc * xc, axis=-1, keepdims=True) + EPS)
            c2 = xc * rstd * lw_ref[...] + lb_ref[...]
            y_ref[rows, :] = (_silu(c2) * _silu(z_ref[rows, :])).astype(y_ref.dtype)

    vec = pl.BlockSpec((1, cw), lambda i: (0, 0))
    row = pl.BlockSpec((tm, cw), lambda i: (i, 0))
    return pl.pallas_call(
        body, name=name,
        out_shape=(jax.ShapeDtypeStruct((t, cw), F32), jax.ShapeDtypeStruct(ycat.shape, ycat.dtype)),
        grid=(t // tm,),
        in_specs=[*_conf_specs(tm, cw, halo, lambda i: i),
                  pl.BlockSpec((tm, cw), lambda i: (i, OFF_ZC // cw)),
                  pl.BlockSpec((kp, cw), lambda i: (0, 0)), vec, vec, vec, ANY],
        out_specs=(row, pl.BlockSpec((tm, cw), lambda i: (i, YCAT_CONF // cw))),
        input_output_aliases={9: 1},
        scratch_shapes=[pltpu.VMEM((halo + tm, cw), F32)] + _conv_shift_scratch(k, halo + tm, cw),
        compiler_params=_params(("parallel",)),
    )(proj, proj, proj, proj, proj, w, bias, ln_w, ln_b, ycat)


def _conf_bwd(dycat, proj, c1, w, ln_w, ln_b, dproj, seq, name):
    t = proj.shape[0]
    k = CONF_KERNEL
    tm, cw, halo = CONV_TILE, CONF_WIDTH, _conv_halo(k)
    sr, sc = CONV_SUB_ROWS, CONV_SUB_COLS
    p = k - 1
    kp = w.shape[0]
    nt = t // tm
    last_halo = t // halo - 1

    def body(dy_ref, dyn_ref, c_ref, cn_ref, z_ref, zn_ref, a_ref, ah_ref, g_ref, gh_ref, w_ref, lw_ref, lb_ref, _,
             grp_ref, dw_ref, db_ref, dlw_ref, dlb_ref, dyext, xext, wacc, bacc, lwacc, lbacc, dysh, xsh):
        i = pl.program_id(0)

        @pl.when(i == 0)
        def _():
            wacc[...] = jnp.zeros_like(wacc)
            bacc[...] = jnp.zeros_like(bacc)
            lwacc[...] = jnp.zeros_like(lwacc)
            lbacc[...] = jnp.zeros_like(lbacc)

        def post_bwd(dy, cv, zv):
            xc = cv - jnp.mean(cv, axis=-1, keepdims=True)
            rstd = lax.rsqrt(jnp.mean(xc * xc, axis=-1, keepdims=True) + EPS)
            xh = xc * rstd
            c2 = xh * lw_ref[...] + lb_ref[...]
            dz = dy * _silu(c2) * _dsilu(zv)
            dc2 = dy * _silu(zv) * _dsilu(c2)
            dxh = dc2 * lw_ref[...]
            dc = rstd * (dxh - jnp.mean(dxh, axis=-1, keepdims=True)
                         - xh * jnp.mean(dxh * xh, axis=-1, keepdims=True))
            return dc, dz, dc2 * xh, dc2

        seq_end = ((i + 1) * tm) % seq == 0
        for r0 in range(0, tm, sr):
            rows = slice(r0, r0 + sr)
            dc, dz, lw_terms, lb_terms = post_bwd(dy_ref[rows, :], c_ref[rows, :], z_ref[rows, :])
            dyext[rows, :] = dc
            grp_ref[rows, 2 * cw:] = dz.astype(grp_ref.dtype)
            lwacc[...] += _rowsum8(lw_terms)
            lbacc[...] += _rowsum8(lb_terms)
        dc_next = post_bwd(dyn_ref[...], cn_ref[...], zn_ref[...])[0]
        dyext[tm:, :] = jnp.where(seq_end, 0.0, dc_next)
        _glu_window(xext, a_ref, ah_ref, g_ref, gh_ref, (i * tm) % seq == 0, halo)
        _conv_fill_shifted(dyext, dysh)
        _conv_fill_shifted(xext, xsh)
        dag_ref = grp_ref
        for r0, c0 in _conv_subtiles(tm, cw):
            cs = slice(c0, c0 + sc)
            rows = slice(r0, r0 + sr)
            dyv = dyext[rows, cs]
            acc = jnp.zeros((sr, sc), F32)
            for j in range(k):
                acc = acc + w_ref[j:j + 1, cs] * _conv_rows(dyext, (dysh,), r0 + p - j, sr, cs)
                wacc[j, :, cs] += _rowsum8(dyv * _conv_rows(xext, (xsh,), r0 + halo - p + j, sr, cs))
            bacc[:, cs] += _rowsum8(dyv)
            s = _sigmoid(g_ref[rows, cs])
            dag_ref[rows, cs] = (acc * s).astype(dag_ref.dtype)
            dag_ref[rows, cw + c0:cw + c0 + sc] = (acc * a_ref[rows, cs] * s * (1.0 - s)).astype(dag_ref.dtype)

        @pl.when(i == nt - 1)
        def _():
            dw_ref[...] = jnp.zeros_like(dw_ref)
            for j in range(k):
                dw_ref[j:j + 1, :] = jnp.sum(wacc[j], axis=0, keepdims=True)
            db_ref[...] = jnp.sum(bacc[...], axis=0, keepdims=True)
            dlw_ref[...] = jnp.sum(lwacc[...], axis=0, keepdims=True)
            dlb_ref[...] = jnp.sum(lbacc[...], axis=0, keepdims=True)

    def blk(col):
        return pl.BlockSpec((tm, cw), lambda i: (i, col))

    def nxt(col):
        return pl.BlockSpec((halo, cw), lambda i: (jnp.minimum((i + 1) * (tm // halo), last_halo), col))

    vec = pl.BlockSpec((1, cw), lambda i: (0, 0))
    return pl.pallas_call(
        body, name=name,
        out_shape=(jax.ShapeDtypeStruct(dproj.shape, dproj.dtype), jax.ShapeDtypeStruct((kp, cw), F32),
                   jax.ShapeDtypeStruct((1, cw), F32), jax.ShapeDtypeStruct((1, cw), F32),
                   jax.ShapeDtypeStruct((1, cw), F32)),
        grid=(nt,),
        in_specs=[blk(YCAT_CONF // cw), nxt(YCAT_CONF // cw), blk(0), nxt(0), blk(OFF_ZC // cw), nxt(OFF_ZC // cw),
                  *_conf_specs(tm, cw, halo, lambda i: i),
                  pl.BlockSpec((kp, cw), lambda i: (0, 0)), vec, vec, ANY],
        out_specs=(pl.BlockSpec((tm, CONF_GROUP), lambda i: (i, OFF_CONF // CONF_GROUP)),
                   pl.BlockSpec((kp, cw), lambda i: (0, 0)), vec, vec, vec),
        input_output_aliases={13: 0},
        scratch_shapes=[pltpu.VMEM((tm + halo, cw), F32), pltpu.VMEM((halo + tm, cw), F32),
                        pltpu.VMEM((kp, SUBLANES, cw), F32), pltpu.VMEM((SUBLANES, cw), F32),
                        pltpu.VMEM((SUBLANES, cw), F32), pltpu.VMEM((SUBLANES, cw), F32)]
        + 2 * _conv_shift_scratch(k, halo + tm, cw),
        compiler_params=_params(("arbitrary",)),
    )(dycat, dycat, c1, c1, proj, proj, proj, proj, proj, proj, w, ln_w, ln_b, dproj)


def _half_mask(half):
    lane = _iota((1, LANES), 1)
    return ((lane >= half * ATTN_HEAD_DIM) & (lane < (half + 1) * ATTN_HEAD_DIM)).astype(F32)


def _stack_heads(xp, g):
    m = _half_mask(g)
    swapped = pltpu.roll(xp, ATTN_HEAD_DIM, axis=1)
    return jnp.concatenate([xp * m, swapped * m] if g == 0 else [swapped * m, xp * m], axis=0)


def _unstack_heads(both, g):
    w = both.shape[0] // 2
    top, bot = both[:w], both[w:]
    lo, hi = _half_mask(0), _half_mask(1)
    if g == 0:
        return top * lo + pltpu.roll(bot, ATTN_HEAD_DIM, axis=1) * hi
    return pltpu.roll(top, ATTN_HEAD_DIM, axis=1) * lo + bot * hi


def _band_mask(first_block):
    w = WINDOW
    qi = _iota((w, 2 * w), 0)
    kj = _iota((w, 2 * w), 1) - w
    rel = qi - kj
    return (rel >= 0) & (rel < w) & (jnp.logical_not(first_block) | (kj >= 0))


def _lane_pick(x, h):
    return jnp.sum(jnp.where(_iota(x.shape, 1) == h, x, 0.0), axis=1, keepdims=True)


def _attn_specs(nb, rev):
    w = WINDOW

    def blk(i):
        return nb - 1 - i if rev else i

    def row(b, i):
        return b * nb + blk(i)

    def prow(b, i):
        return b * nb + jnp.maximum(blk(i) - 1, 0)

    q = pl.BlockSpec((w, 512), lambda b, i: (row(b, i), OFF_Q // 512))
    kc = pl.BlockSpec((w, 128), lambda b, i: (row(b, i), OFF_K // 128))
    kp = pl.BlockSpec((w, 128), lambda b, i: (prow(b, i), OFF_K // 128))
    vc = pl.BlockSpec((w, 128), lambda b, i: (row(b, i), OFF_V // 128))
    vp = pl.BlockSpec((w, 128), lambda b, i: (prow(b, i), OFF_V // 128))
    z = pl.BlockSpec((w, 512), lambda b, i: (row(b, i), OFF_ZA // 512))
    return q, kc, kp, vc, vp, z, row


def _attn_fwd(proj, sinks, ycat, nbatch, name):
    t = proj.shape[0]
    w = WINDOW
    nb = t // nbatch // w
    scale = ATTN_HEAD_DIM ** -0.5
    q_s, kc_s, kp_s, vc_s, vp_s, z_s, row = _attn_specs(nb, False)

    def body(q_ref, kc_ref, kp_ref, vc_ref, vp_ref, z_ref, sk_ref, _, y_ref, o_ref, lse_ref):
        first = pl.program_id(1) == 0
        mask = _band_mask(first)
        kk = jnp.concatenate([kp_ref[...], kc_ref[...]], axis=0).astype(MXU_DTYPE)
        vv = jnp.concatenate([vp_ref[...], vc_ref[...]], axis=0).astype(MXU_DTYPE)
        sk = sk_ref[...]
        lane = _iota((w, LANES), 1)
        mask2 = jnp.concatenate([mask, mask], axis=0)
        scores = [_dot(_stack_heads(q_ref[:, j * LANES:(j + 1) * LANES], j // 2), kk, NT) for j in range(4)]
        lse_all = jnp.zeros((w, LANES), F32)
        for j in range(4):
            s = jnp.where(mask2, scores[j] * scale, -1e30)
            skc = jnp.concatenate([jnp.broadcast_to(_lane_pick(sk, 2 * j), (w, 1)),
                                   jnp.broadcast_to(_lane_pick(sk, 2 * j + 1), (w, 1))], axis=0)
            m = jnp.maximum(jnp.max(s, axis=1, keepdims=True), skc)
            den = jnp.sum(jnp.exp(s - m), axis=1, keepdims=True) + jnp.exp(skc - m)
            lse = m + jnp.log(den)
            lse_all = jnp.where(lane == 2 * j, lse[:w], lse_all)
            lse_all = jnp.where(lane == 2 * j + 1, lse[w:], lse_all)
            op = _unstack_heads(_dot(jnp.exp(s - lse), vv), j // 2)
            cols = slice(j * LANES, (j + 1) * LANES)
            o_ref[:, cols] = op
            y_ref[:, cols] = (op * _silu(z_ref[:, cols])).astype(y_ref.dtype)
        lse_ref[...] = lse_all

    return pl.pallas_call(
        body, name=name,
        out_shape=(jax.ShapeDtypeStruct(ycat.shape, ycat.dtype), jax.ShapeDtypeStruct((t, 512), F32),
                   jax.ShapeDtypeStruct((t, LANES), F32)),
        grid=(nbatch, nb),
        in_specs=[q_s, kc_s, kp_s, vc_s, vp_s, z_s, pl.BlockSpec((1, LANES), lambda b, i: (0, 0)), ANY],
        out_specs=(pl.BlockSpec((w, 512), lambda b, i: (row(b, i), YCAT_ATTN // 512)),
                   pl.BlockSpec((w, 512), lambda b, i: (row(b, i), 0)),
                   pl.BlockSpec((w, LANES), lambda b, i: (row(b, i), 0))),
        input_output_aliases={7: 0},
        compiler_params=_params(("parallel", "parallel")),
    )(proj, proj, proj, proj, proj, proj, sinks, ycat)


def _attn_bwd(dycat, proj, o, lse, sinks, ddt, dproj, nbatch, name):
    t = proj.shape[0]
    w = WINDOW
    nb = t // nbatch // w
    scale = ATTN_HEAD_DIM ** -0.5
    q_s, kc_s, kp_s, vc_s, vp_s, z_s, row = _attn_specs(nb, True)

    def body(dy_ref, q_ref, kc_ref, kp_ref, vc_ref, vp_ref, z_ref, o_ref, lse_ref, sk_ref, ddt_ref, _,
             grp_ref, dsk_ref, kcarry, vcarry, sacc):
        b, i = pl.program_id(0), pl.program_id(1)

        @pl.when((b == 0) & (i == 0))
        def _():
            sacc[...] = jnp.zeros_like(sacc)

        @pl.when(i == 0)
        def _():
            kcarry[...] = jnp.zeros_like(kcarry)
            vcarry[...] = jnp.zeros_like(vcarry)

        first = i == nb - 1
        mask = _band_mask(first)
        kk = jnp.concatenate([kp_ref[...], kc_ref[...]], axis=0).astype(MXU_DTYPE)
        vv = jnp.concatenate([vp_ref[...], vc_ref[...]], axis=0).astype(MXU_DTYPE)
        sk = sk_ref[...]
        lse_all = lse_ref[...]
        lane1 = _iota((1, LANES), 1)
        mask2 = jnp.concatenate([mask, mask], axis=0)
        qs, dos, deltas, lses, scores, dps = [], [], [], [], [], []
        for j in range(4):
            cols = slice(j * LANES, (j + 1) * LANES)
            qp, zp, ov, dy = q_ref[:, cols], z_ref[:, cols], o_ref[:, cols], dy_ref[:, cols]
            grp_ref[:, OFF_ZA + j * LANES:OFF_ZA + (j + 1) * LANES] = (dy * ov * _dsilu(zp)).astype(grp_ref.dtype)
            do = dy * _silu(zp)
            q2 = _stack_heads(qp, j // 2).astype(MXU_DTYPE)
            do2 = _stack_heads(do, j // 2)
            qs.append(q2)
            dos.append(do2.astype(MXU_DTYPE))
            deltas.append(jnp.sum(do2 * _stack_heads(ov, j // 2), axis=1, keepdims=True))
            lses.append(jnp.concatenate([_lane_pick(lse_all, 2 * j), _lane_pick(lse_all, 2 * j + 1)], axis=0))
            scores.append(_dot(q2, kk, NT))
            dps.append(_dot(do2, vv, NT))
        prs, dss = [], []
        dsk = jnp.zeros((1, LANES), F32)
        for j in range(4):
            pr = jnp.exp(jnp.where(mask2, scores[j] * scale, -1e30) - lses[j])
            prs.append(pr.astype(MXU_DTYPE))
            dss.append((pr * (dps[j] - deltas[j])).astype(MXU_DTYPE))
            skc = jnp.concatenate([jnp.broadcast_to(_lane_pick(sk, 2 * j), (w, 1)),
                                   jnp.broadcast_to(_lane_pick(sk, 2 * j + 1), (w, 1))], axis=0)
            sink_term = jnp.exp(skc - lses[j]) * deltas[j]
            dsk = dsk - jnp.where(lane1 == 2 * j, jnp.sum(sink_term[:w]), 0.0)
            dsk = dsk - jnp.where(lane1 == 2 * j + 1, jnp.sum(sink_term[w:]), 0.0)
        dkk = jnp.zeros((2 * w, LANES), F32)
        dvv = jnp.zeros((2 * w, LANES), F32)
        for j in range(4):
            dq = _unstack_heads(_dot(dss[j], kk) * scale, j // 2)
            grp_ref[:, OFF_Q + j * LANES:OFF_Q + (j + 1) * LANES] = dq.astype(grp_ref.dtype)
            dkk = dkk + _dot(dss[j], qs[j], TN) * scale
            dvv = dvv + _dot(prs[j], dos[j], TN)
        grp_ref[:, OFF_K:OFF_K + LANES] = (dkk[w:, :] + kcarry[...]).astype(grp_ref.dtype)
        grp_ref[:, OFF_V:OFF_V + LANES] = (dvv[w:, :] + vcarry[...]).astype(grp_ref.dtype)
        grp_ref[:, OFF_DT:OFF_DT + LANES] = ddt_ref[...].astype(grp_ref.dtype)
        grp_ref[:, OFF_DT + LANES:] = jnp.zeros((w, ATTN_GROUP - OFF_DT - LANES), grp_ref.dtype)
        kcarry[...] = dkk[:w, :]
        vcarry[...] = dvv[:w, :]
        sacc[...] += dsk

        @pl.when((b == nbatch - 1) & (i == nb - 1))
        def _():
            dsk_ref[...] = sacc[...]

    return pl.pallas_call(
        body, name=name,
        out_shape=(jax.ShapeDtypeStruct(dproj.shape, dproj.dtype), jax.ShapeDtypeStruct((1, LANES), F32)),
        grid=(nbatch, nb),
        in_specs=[pl.BlockSpec((w, 512), lambda b, i: (row(b, i), YCAT_ATTN // 512)),
                  q_s, kc_s, kp_s, vc_s, vp_s, z_s,
                  pl.BlockSpec((w, 512), lambda b, i: (row(b, i), 0)),
                  pl.BlockSpec((w, LANES), lambda b, i: (row(b, i), 0)),
                  pl.BlockSpec((1, LANES), lambda b, i: (0, 0)),
                  pl.BlockSpec((w, LANES), lambda b, i: (row(b, i), 0)), ANY],
        out_specs=(pl.BlockSpec((w, ATTN_GROUP), lambda b, i: (row(b, i), 0)),
                   pl.BlockSpec((1, LANES), lambda b, i: (0, 0))),
        input_output_aliases={11: 0},
        scratch_shapes=[pltpu.VMEM((w, LANES), F32), pltpu.VMEM((w, LANES), F32),
                        pltpu.VMEM((1, LANES), F32)],
        compiler_params=_params(("arbitrary", "arbitrary")),
    )(dycat, proj, proj, proj, proj, proj, proj, o, lse, sinks, ddt, dproj)


SSD_WIDTH = SSD_HEADS * SSD_HEAD_DIM
GROUP_ROWS = SSD_WIDTH // 2


def _expand_mat():
    r, c = _iota((LANES, SSD_WIDTH), 0), _iota((LANES, SSD_WIDTH), 1)
    return (r == lax.shift_right_logical(c, 6)).astype(BF16)


def _expand_mat_t():
    r, c = _iota((SSD_WIDTH, LANES), 0), _iota((SSD_WIDTH, LANES), 1)
    return (c == lax.shift_right_logical(r, 6)).astype(BF16)


def _ssd_common(u_ref, dt_ref, dtb_ref, a_ref):
    q = CHUNK
    act = _silu(u_ref[...])
    xs = act[:, :SSD_WIDTH]
    bm = act[:, SSD_WIDTH:SSD_WIDTH + 256]
    cm = act[:, SSD_WIDTH + 256:]
    dtp = _softplus(dt_ref[...] + dtb_ref[...])
    a = dtp * a_ref[...]
    tril = (_iota((q, q), 0) >= _iota((q, q), 1)).astype(BF16)
    acs = _xdot_r(tril, a)
    acs_t = acs.T
    e = _expand_mat()
    dt_x = _xdot(dtp, e)
    ea = jnp.exp(_xdot(acs, e))
    a_end = jnp.sum(jnp.where(_iota(acs.shape, 0) == q - 1, acs, 0.0), axis=0, keepdims=True)
    dec = jnp.exp(_xdot(a_end - acs, e))
    a_end_col = jnp.broadcast_to(_lane_pick(acs_t, q - 1), (LANES, LANES))
    s_scale = jnp.exp(_xdot_r(_expand_mat_t(), a_end_col))
    return act, xs, bm, cm, dtp, acs, acs_t, dt_x, ea, dec, s_scale, tril


def _decay_mat(acs, acs_t, h):
    q = CHUNK
    col = _lane_pick(acs, h)
    rowv = jnp.sum(jnp.where(_iota(acs_t.shape, 0) == h, acs_t, 0.0), axis=0, keepdims=True)
    causal = _iota((q, q), 0) >= _iota((q, q), 1)
    return jnp.exp(jnp.where(causal, col - rowv, -1e30))


GN_WIDTH = 512


def _ssd_fwd(u, proj, dtb, a_neg, d_x, norm_w, ycat, nbatch, name):
    t = u.shape[0]
    q = CHUNK
    nc = t // nbatch // q

    def body(u_ref, dt_ref, z_ref, dtb_ref, a_ref, dx_ref, nw_ref, _, y_ref, st_ref, yn_ref, state):
        c = pl.program_id(1)

        @pl.when(c == 0)
        def _():
            state[...] = jnp.zeros_like(state)

        st_ref[...] = state[...]
        act, xs, bm, cm, dtp, acs, acs_t, dt_x, ea, dec, s_scale, _ = _ssd_common(u_ref, dt_ref, dtb_ref, a_ref)
        xdt = xs * dt_x
        xdec = xdt * dec
        lo, hi = _half_mask(0), _half_mask(1)
        grp = []
        for g in range(2):
            bg = bm[:, g * LANES:(g + 1) * LANES]
            cg = cm[:, g * LANES:(g + 1) * LANES]
            rows = slice(g * GROUP_ROWS, (g + 1) * GROUP_ROWS)
            sg = state[rows, :]
            grp.append((_dot(cg, bg, NT), _dot(cg, sg, NT), rows,
                        s_scale[rows, :] * sg + _dot(xdec[:, rows], bg, TN)))
        for g in range(2):
            cb, yoff, rows, state_new = grp[g]
            for j in range(4):
                pj = g * 4 + j
                cols = slice(pj * LANES, (pj + 1) * LANES)
                xp = xdt[:, cols]
                m2 = jnp.concatenate([cb * _decay_mat(acs, acs_t, 2 * pj), cb * _decay_mat(acs, acs_t, 2 * pj + 1)],
                                     axis=1)
                yp = _dot(m2, jnp.concatenate([xp * lo, xp * hi], axis=0))
                yp = yp + yoff[:, j * LANES:(j + 1) * LANES] * ea[:, cols]
                y_ref[:, cols] = yp + dx_ref[:, cols] * xs[:, cols]
            state[rows, :] = state_new
        for g in range(SSD_WIDTH // GN_WIDTH):
            cols = slice(g * GN_WIDTH, (g + 1) * GN_WIDTH)
            gg = y_ref[:, cols] * _silu(z_ref[:, cols])
            rstd = lax.rsqrt(jnp.mean(gg * gg, axis=-1, keepdims=True) + EPS)
            yn_ref[:, cols] = (gg * rstd * nw_ref[:, cols]).astype(yn_ref.dtype)

    vec = pl.BlockSpec((1, LANES), lambda b, c: (0, 0))
    wide = pl.BlockSpec((q, SSD_WIDTH), lambda b, c: (b * nc + c, 0))
    wvec = pl.BlockSpec((1, SSD_WIDTH), lambda b, c: (0, 0))
    return pl.pallas_call(
        body, name=name,
        out_shape=(jax.ShapeDtypeStruct((t, SSD_WIDTH), F32),
                   jax.ShapeDtypeStruct((nbatch * nc * SSD_WIDTH, SSD_STATE), F32),
                   jax.ShapeDtypeStruct(ycat.shape, ycat.dtype)),
        grid=(nbatch, nc),
        in_specs=[pl.BlockSpec((q, SSD_CONV_DIM), lambda b, c: (b * nc + c, 0)),
                  pl.BlockSpec((q, LANES), lambda b, c: (b * nc + c, OFF_DT // LANES)),
                  pl.BlockSpec((q, SSD_WIDTH), lambda b, c: (b * nc + c, OFF_ZS // SSD_WIDTH)),
                  vec, vec, wvec, wvec, ANY],
        out_specs=(wide, pl.BlockSpec((SSD_WIDTH, SSD_STATE), lambda b, c: (b * nc + c, 0)), wide),
        input_output_aliases={7: 2},
        scratch_shapes=[pltpu.VMEM((SSD_WIDTH, SSD_STATE), F32)],
        compiler_params=_params(("parallel", "arbitrary")),
    )(u, proj, proj, dtb, a_neg, d_x, norm_w, ycat)


def _ssd_bwd(dycat, u, proj, y, states, dtb, a_neg, d_x, norm_w, dproj, nbatch, name):
    t = u.shape[0]
    q = CHUNK
    nc = t // nbatch // q

    def body(do_ref, u_ref, dt_ref, z_ref, y_ref, st_ref, dtb_ref, a_ref, dx_ref, nw_ref, _,
             du_ref, dz_ref, ddt_ref, dal_ref, dd_ref, dtbg_ref, dnw_ref, dstate, acc_a, acc_d, acc_b, acc_w):
        b, c = pl.program_id(0), pl.program_id(1)

        @pl.when((b == 0) & (c == 0))
        def _():
            acc_a[...] = jnp.zeros_like(acc_a)
            acc_d[...] = jnp.zeros_like(acc_d)
            acc_b[...] = jnp.zeros_like(acc_b)
            acc_w[...] = jnp.zeros_like(acc_w)

        @pl.when(c == 0)
        def _():
            dstate[...] = jnp.zeros_like(dstate)

        dy_parts = []
        for g in range(SSD_WIDTH // GN_WIDTH):
            cols = slice(g * GN_WIDTH, (g + 1) * GN_WIDTH)
            yv, zv, dov = y_ref[:, cols], z_ref[:, cols], do_ref[:, cols]
            sz = _silu(zv)
            gg = yv * sz
            rstd = lax.rsqrt(jnp.mean(gg * gg, axis=-1, keepdims=True) + EPS)
            gh = gg * rstd
            acc_w[:, cols] += _rowsum8(dov * gh)
            dgn = dov * nw_ref[:, cols]
            dg = rstd * (dgn - gh * jnp.mean(dgn * gh, axis=-1, keepdims=True))
            dy_parts.append(dg * sz)
            dz_ref[:, cols] = (dg * yv * _dsilu(zv)).astype(dz_ref.dtype)

        act, xs, bm, cm, dtp, acs, acs_t, dt_x, ea, dec, s_scale, tril = _ssd_common(
            u_ref, dt_ref, dtb_ref, a_ref)
        xdt = xs * dt_x
        xdec = xdt * dec
        dyv = jnp.concatenate(dy_parts, axis=1)
        dye = dyv * ea
        lo, hi = _half_mask(0), _half_mask(1)
        et = _expand_mat_t()
        grp = []
        for g in range(2):
            rows = slice(g * GROUP_ROWS, (g + 1) * GROUP_ROWS)
            bg = bm[:, g * LANES:(g + 1) * LANES]
            cg = cm[:, g * LANES:(g + 1) * LANES]
            sg = st_ref[rows, :]
            dsg = dstate[rows, :]
            grp.append(dict(
                rows=rows, bg=bg, cg=cg, dsg=dsg,
                cb=_dot(cg, bg, NT), yoff=_dot(cg, sg, NT), dxst=_dot(bg, dsg, NT) * dec[:, rows],
                dc_off=_dot(dye[:, rows], sg), db_off=_dot(xdec[:, rows], dsg),
                s_next=s_scale[rows, :] * sg + _dot(xdec[:, rows], bg, TN),
                dstate_new=_dot(dye[:, rows], cg, TN) + s_scale[rows, :] * dsg))
        dy2s, g2s, l2s = [], [], []
        for pj in range(SSD_HEADS // 2):
            cols = slice(pj * LANES, (pj + 1) * LANES)
            dyp = dyv[:, cols]
            dy2 = jnp.concatenate([dyp * lo, dyp * hi], axis=0).astype(MXU_DTYPE)
            dy2s.append(dy2)
            g2s.append(_dot(dy2, xdt[:, cols], NT))
            l2s.append(jnp.concatenate([_decay_mat(acs, acs_t, 2 * pj), _decay_mat(acs, acs_t, 2 * pj + 1)], axis=0))
        dal_diag = jnp.zeros((q, LANES), F32)
        lane2 = _iota((2 * q, LANES), 1)
        row2 = _iota((2 * q, LANES), 0)
        dxdt_parts, db_parts, dc_parts = [], [], []
        end_sum = jnp.zeros((LANES, LANES), F32)
        for g in range(2):
            gd = grp[g]
            cb2 = jnp.concatenate([gd["cb"], gd["cb"]], axis=0)
            dcb = jnp.zeros((q, q), F32)
            parts = []
            for j in range(4):
                pj = g * 4 + j
                gl = g2s[pj] * l2s[pj]
                dcb = dcb + gl[:q] + gl[q:]
                m2 = cb2 * l2s[pj]
                parts.append(_dot(m2, dy2s[pj], TN))
                w2 = (gl * cb2).astype(MXU_DTYPE)
                sel2 = (lane2 == 2 * pj + (row2 >= q).astype(jnp.int32)).astype(MXU_DTYPE)
                dal_diag = dal_diag + _dot(jnp.concatenate([w2[:q], w2[q:]], axis=1), sel2) - _dot(w2, sel2, TN)
            dxdt_parts.append(jnp.concatenate(parts, axis=1) + gd["dxst"])
            dc_parts.append(_dot(dcb, gd["bg"]) + gd["dc_off"])
            db_parts.append(_dot(dcb, gd["cg"], TN) + gd["db_off"])
            end_sum = end_sum + _xdot(gd["dsg"] * gd["s_next"], et[gd["rows"], :], TN, passes=2)
            dstate[gd["rows"], :] = gd["dstate_new"]
        dxst_parts = [gd["dxst"] for gd in grp]
        yoff_parts = [gd["yoff"] for gd in grp]
        dxdt = jnp.concatenate(dxdt_parts, axis=1)
        dxv = dx_ref[...]
        yoff = jnp.concatenate(yoff_parts, axis=1) * ea
        dalpha = dal_diag + _xdot(dyv * yoff - xdt * jnp.concatenate(dxst_parts, axis=1), et)
        end_row = jnp.sum(end_sum, axis=0, keepdims=True)
        dalpha = dalpha + jnp.where(_iota((q, LANES), 0) == q - 1, end_row, 0.0)
        da = _xdot_r(tril, dalpha, TN)
        ddtp = da * a_ref[...] + _xdot(dxdt * xs, et)
        acc_a[...] += _rowsum8(da * dtp)
        acc_d[...] += _rowsum8(_xdot(dyv * xs, et))
        ddt_raw = ddtp * _sigmoid(dt_ref[...] + dtb_ref[...])
        acc_b[...] += _rowsum8(ddt_raw)
        ddt_ref[...] = ddt_raw
        dxs = dxdt * dt_x + dxv * dyv
        dact = jnp.concatenate([dxs] + db_parts + dc_parts, axis=1)
        du_ref[...] = dact * _dsilu(u_ref[...])

        @pl.when((b == nbatch - 1) & (c == nc - 1))
        def _():
            dal_ref[...] = jnp.sum(acc_a[...], axis=0, keepdims=True) * a_ref[...]
            dd_ref[...] = jnp.sum(acc_d[...], axis=0, keepdims=True)
            dtbg_ref[...] = jnp.sum(acc_b[...], axis=0, keepdims=True)
            dnw_ref[...] = jnp.sum(acc_w[...], axis=0, keepdims=True)

    def rowblk(b, c):
        return b * nc + (nc - 1 - c)

    vec = pl.BlockSpec((1, LANES), lambda b, c: (0, 0))
    wvec = pl.BlockSpec((1, SSD_WIDTH), lambda b, c: (0, 0))
    wide = pl.BlockSpec((q, SSD_WIDTH), lambda b, c: (rowblk(b, c), 0))
    zblk = pl.BlockSpec((q, SSD_WIDTH), lambda b, c: (rowblk(b, c), OFF_ZS // SSD_WIDTH))
    return pl.pallas_call(
        body, name=name,
        out_shape=(jax.ShapeDtypeStruct((t, SSD_CONV_DIM), F32), jax.ShapeDtypeStruct(dproj.shape, dproj.dtype),
                   jax.ShapeDtypeStruct((t, LANES), F32),
                   jax.ShapeDtypeStruct((1, LANES), F32), jax.ShapeDtypeStruct((1, LANES), F32),
                   jax.ShapeDtypeStruct((1, LANES), F32), jax.ShapeDtypeStruct((1, SSD_WIDTH), F32)),
        grid=(nbatch, nc),
        in_specs=[wide,
                  pl.BlockSpec((q, SSD_CONV_DIM), lambda b, c: (rowblk(b, c), 0)),
                  pl.BlockSpec((q, LANES), lambda b, c: (rowblk(b, c), OFF_DT // LANES)),
                  zblk, wide,
                  pl.BlockSpec((SSD_WIDTH, SSD_STATE), lambda b, c: (rowblk(b, c), 0)),
                  vec, vec, wvec, wvec, ANY],
        out_specs=(pl.BlockSpec((q, SSD_CONV_DIM), lambda b, c: (rowblk(b, c), 0)),
                   zblk,
                   pl.BlockSpec((q, LANES), lambda b, c: (rowblk(b, c), 0)),
                   vec, vec, vec, wvec),
        input_output_aliases={10: 1},
        scratch_shapes=[pltpu.VMEM((SSD_WIDTH, SSD_STATE), F32), pltpu.VMEM((SUBLANES, LANES), F32),
                        pltpu.VMEM((SUBLANES, LANES), F32), pltpu.VMEM((SUBLANES, LANES), F32),
                        pltpu.VMEM((SUBLANES, SSD_WIDTH), F32)],
        compiler_params=_params(("arbitrary", "arbitrary")),
    )(dycat, u, proj, proj, y, states, dtb, a_neg, d_x, norm_w, dproj)


def _pad_rows(w, rows):
    return jnp.concatenate([w, jnp.zeros((rows - w.shape[0], w.shape[1]), w.dtype)], axis=0)


def _pad_lanes(v):
    return jnp.concatenate([v, jnp.zeros((LANES - v.shape[0],), v.dtype)]).reshape(1, LANES)


def _padded_from_chips(pieces):
    cols = pieces[0].shape[-1]
    lead = pieces[0].shape[:-1]
    parts, pos = [], 0
    for lo, hi, start in sorted(SECTIONS, key=lambda s: s[2]):
        if start > pos:
            parts.append(jnp.zeros(lead + (start - pos,), pieces[0].dtype))
        pos = start + hi - lo
        while lo < hi:
            p = lo // cols
            end = min(hi, (p + 1) * cols)
            parts.append(pieces[p][..., lo - p * cols:end - p * cols])
            lo = end
    if pos < NP:
        parts.append(jnp.zeros(lead + (NP - pos,), pieces[0].dtype))
    return jnp.concatenate(parts, axis=-1)


def _chip_part_from_padded(wp, p, cols):
    lo, hi = p * cols, (p + 1) * cols
    parts = []
    for rs, re, start in SECTIONS:
        a, b = max(lo, rs), min(hi, re)
        if a < b:
            parts.append(wp[..., start + a - rs:start + b - rs])
    return jnp.concatenate(parts, axis=-1)


def _layer_params(li, w_in_p, w_out, conv_w, dw_w, small):
    return dict(
        w_in_p=w_in_p, w_out=w_out,
        conv_w=_pad_rows(conv_w, SUBLANES), dw_w=_pad_rows(dw_w, 32),
        norm_w=small["norm_w"][li].reshape(1, -1),
        conv_b=small["ssd_conv_b"][li].reshape(1, -1),
        dtb=_pad_lanes(small["ssd_dt_bias"][li]),
        a_neg=_pad_lanes(-jnp.exp(small["ssd_a_log"][li])),
        d_x=jnp.repeat(small["ssd_d"][li], SSD_HEAD_DIM).reshape(1, -1),
        ssd_norm_w=small["ssd_norm_w"][li].reshape(1, -1),
        sinks=_pad_lanes(small["attn_sinks"][li]),
        dw_b=small["conf_dw_b"][li].reshape(1, -1),
        ln_w=small["conf_ln_w"][li].reshape(1, -1),
        ln_b=small["conf_ln_b"][li].reshape(1, -1),
    )


def _layer_fwd(x, p, nbatch, seq, tag, after=None):
    proj, h_t = _proj_fwd(x, p["norm_w"], p["w_in_p"], name=f"proj_fwd_{tag}", after=after)
    u = _conv_fwd(proj, OFF_XBC, SSD_CONV_DIM, p["conv_w"], p["conv_b"], SSD_CONV, seq, name=f"ssd_conv_fwd_{tag}")
    ycat = lax.empty((x.shape[0], MIX_WIDTH), MXU_DTYPE)
    y, states, ycat = _ssd_fwd(u, proj, p["dtb"], p["a_neg"], p["d_x"], p["ssd_norm_w"], ycat, nbatch,
                               name=f"ssd_fwd_{tag}")
    ycat, o, lse = _attn_fwd(proj, p["sinks"], ycat, nbatch, name=f"attn_fwd_{tag}")
    c1, ycat = _conf_fwd(proj, p["dw_w"], p["dw_b"], p["ln_w"], p["ln_b"], ycat, seq, name=f"conf_fwd_{tag}")
    w_out = p["w_out"](ycat) if callable(p["w_out"]) else p["w_out"]
    x_new = _matmul(ycat, w_out, "nn", F32, 1024, 512, 2048, name=f"out_fwd_{tag}", residual=x)
    return x_new, dict(x=x, w_out=w_out, h_t=h_t, proj=proj, u=u, y=y, states=states, o=o, lse=lse, c1=c1, ycat=ycat)


def _layer_bwd(dx_out, p, s, nbatch, seq, tag, hooks=None):
    hooks = hooks or {}
    proj = s["proj"]
    dycat = _matmul(dx_out, s["w_out"], "nt", F32, 1024, 1024, 1024, name=f"out_bwd_dy_{tag}",
                    after=hooks.get("start_token"))
    dw_out = _matmul(s["ycat"], dx_out, "tn", F32, 1024, 1024, 1024, name=f"out_bwd_dw_{tag}")
    token = hooks["after_dycat"](dycat) if "after_dycat" in hooks else None
    dtb = p["dtb"] if token is None else p["dtb"] + token[0, 0]
    dproj = lax.empty(proj.shape, MXU_DTYPE)
    du, dproj, ddt, da_log, dd, ddtb, dssd_norm_w = _ssd_bwd(
        dycat, s["u"], proj, s["y"], s["states"], dtb, p["a_neg"], p["d_x"], p["ssd_norm_w"], dproj,
        nbatch, name=f"ssd_bwd_{tag}")
    dproj, dconv_w, dconv_b = _conv_bwd(du, proj, OFF_XBC, SSD_CONV_DIM, p["conv_w"], SSD_CONV, seq,
                                        name=f"ssd_conv_bwd_{tag}", into=dproj)
    dproj, dsinks = _attn_bwd(dycat, proj, s["o"], s["lse"], p["sinks"], ddt, dproj, nbatch,
                              name=f"attn_bwd_{tag}")
    if "after_attn" in hooks:
        hooks["after_attn"](dproj)
    dproj, ddw_w, ddw_b, dln_w, dln_b = _conf_bwd(dycat, proj, s["c1"], p["dw_w"], p["ln_w"], p["ln_b"], dproj, seq,
                                                  name=f"conf_bwd_{tag}")
    dw_in_p = _matmul(s["h_t"], dproj, "nn", F32, 1024, 512, 4096, name=f"proj_bwd_dw_{tag}")
    token = hooks["after_dw"](dw_in_p, dw_out) if "after_dw" in hooks else None
    norm_w = p["norm_w"] if token is None else p["norm_w"] + token[0, 0]
    dx_in, dnorm_w = _proj_bwd_dx(dproj, p["w_in_p"], s["x"], norm_w, dx_out, name=f"proj_bwd_dx_{tag}")
    grads = dict(
        norm_w=dnorm_w[0], w_in_p=dw_in_p, ssd_conv_w=dconv_w[:SSD_CONV], ssd_conv_b=dconv_b[0],
        ssd_dt_bias=ddtb[0, :SSD_HEADS], ssd_a_log=da_log[0, :SSD_HEADS], ssd_d=dd[0, :SSD_HEADS],
        ssd_norm_w=dssd_norm_w[0], attn_sinks=dsinks[0, :ATTN_Q_HEADS], conf_dw_w=ddw_w[:CONF_KERNEL],
        conf_dw_b=ddw_b[0], conf_ln_w=dln_w[0], conf_ln_b=dln_b[0], w_out=dw_out)
    return dx_in, grads


def _local_step(x, target, param_fns, final_norm_w, first_after=None, bwd_hooks=None):
    nbatch, seq, d = x.shape
    xt = x.reshape(nbatch * seq, d)
    saved, layer_params = [], []
    for li, fn in enumerate(param_fns):
        p = fn(xt)
        layer_params.append(p)
        xt, s = _layer_fwd(xt, p, nbatch, seq, f"l{li}", after=first_after if li == 0 else None)
        saved.append(s)
    loss, dx, dfinal = _loss_head(xt, target.reshape(nbatch * seq, d), final_norm_w.reshape(1, d), name="loss_head")
    grads = [None] * len(layer_params)
    for li in reversed(range(len(layer_params))):
        hooks = bwd_hooks(li) if bwd_hooks is not None else None
        dx, grads[li] = _layer_bwd(dx, layer_params[li], saved[li], nbatch, seq, f"l{li}", hooks=hooks)
    return loss[0, 0], dx.reshape(nbatch, seq, d), grads, dfinal[0]


MESH = pl.DeviceIdType.MESH
N_CHIPS = 4


def _mesh_pos():
    return lax.axis_index("x"), lax.axis_index("y"), lax.axis_index("c")


def _other_chips(x, y):
    return [(1 - x, y), (x, 1 - y), (1 - x, 1 - y)]


def _gather_weights(big, small, name):
    nbig, nsmall = len(big), len(small)
    n_ici = 3 * (nbig + nsmall)
    n_fwd = 3 * nbig

    def body(*refs):
        ins = refs[:nbig + nsmall]
        outs = refs[nbig + nsmall:2 * (nbig + nsmall)]
        send_sems, recv_sems = refs[2 * (nbig + nsmall):]
        x, y, c = _mesh_pos()
        me = 2 * x + y
        sibling = (x, y, 1 - c)
        chips = _other_chips(x, y)

        def ici(a, j, origin, dest):
            if a < nbig:
                src = ins[a].at[c] if origin is None else outs[a].at[origin, c]
                dst = outs[a].at[me if origin is None else origin, c]
            else:
                src = ins[a] if origin is None else outs[a].at[origin]
                dst = outs[a].at[me if origin is None else origin]
            k = a * 3 + j
            return pltpu.make_async_remote_copy(src_ref=src, dst_ref=dst, send_sem=send_sems.at[k],
                                                recv_sem=recv_sems.at[k], device_id=dest, device_id_type=MESH)

        def fwd(a, j, origin, half):
            k = n_ici + a * 3 + j
            ref = outs[a].at[origin, half]
            return pltpu.make_async_remote_copy(src_ref=ref, dst_ref=ref, send_sem=send_sems.at[k],
                                                recv_sem=recv_sems.at[k], device_id=sibling, device_id_type=MESH)

        sends = []
        for j, (px, py) in enumerate(chips):
            for a in range(nbig + nsmall):
                cp = ici(a, j, None, (px, py, c))
                cp.start()
                sends.append(cp)
        for j, (px, py) in enumerate(chips):
            origin = 2 * px + py
            for a in range(nbig):
                ici(a, j, origin, (px, py, c)).wait_recv()
                cp = fwd(a, j, origin, c)
                cp.start()
                sends.append(cp)
        for j, (px, py) in enumerate(chips):
            origin = 2 * px + py
            for a in range(nbig, nbig + nsmall):
                ici(a, j, origin, (px, py, c)).wait_recv()
            for a in range(nbig):
                fwd(a, j, origin, 1 - c).wait_recv()
        for cp in sends:
            cp.wait_send()

    out_shape = tuple(jax.ShapeDtypeStruct((N_CHIPS,) + a.shape, a.dtype) for a in list(big) + list(small))
    return pl.pallas_call(
        body, name=name, out_shape=out_shape,
        in_specs=[ANY] * (nbig + nsmall), out_specs=tuple([ANY] * (nbig + nsmall)),
        scratch_shapes=[pltpu.SemaphoreType.DMA((n_ici + n_fwd,)), pltpu.SemaphoreType.DMA((n_ici + n_fwd,))],
    )(*big, *small)


HBM = pl.BlockSpec(memory_space=pltpu.HBM)
SEM = pl.BlockSpec(memory_space=pltpu.SEMAPHORE)
DATAFLOW = pltpu.SideEffectType.DATAFLOW_SIDE_EFFECTING


def _split_peers(pattern, x, y, c):
    if pattern == "swap":
        return [((x, y, 1 - c), 1 - c, None, None)]
    me = 2 * x + y
    return [((px, py, c), 2 * px + py if pattern == "scatter" else None, me, 2 * px + py)
            for px, py in _other_chips(x, y)]


def _split_land_shape(pattern, shape):
    return {"bcast": (N_CHIPS,) + shape, "scatter": shape, "swap": shape[:1] + shape[2:]}[pattern]


def _split_copies(pattern, srcs, lands, send_sems, recv_sems, waiting):
    x, y, c = _mesh_pos()
    peers = _split_peers(pattern, x, y, c)
    cps = []
    for j, (dev, src_slot, dst_slot, my_slot) in enumerate(peers):
        for a in range(len(srcs)):
            if src_slot is None:
                src = srcs[a]
            else:
                src = srcs[a].at[:, src_slot] if pattern == "swap" else srcs[a].at[src_slot]
            slot = my_slot if waiting else dst_slot
            dst = lands[a] if slot is None else lands[a].at[slot]
            k = a * len(peers) + j
            cps.append(pltpu.make_async_remote_copy(src_ref=src, dst_ref=dst, send_sem=send_sems[k],
                                                    recv_sem=recv_sems[k], device_id=dev, device_id_type=MESH))
    return cps


def _split_start(arrs, pattern, after, name):
    n = len(arrs)
    nsem = n * (1 if pattern == "swap" else N_CHIPS - 1)

    def body(*refs):
        srcs, lands = refs[:n], refs[n:2 * n]
        outs = refs[2 * n + 1:]
        for cp in _split_copies(pattern, srcs, lands, outs[:nsem], outs[nsem:2 * nsem], waiting=False):
            cp.start()
        outs[-1][...] = jnp.zeros_like(outs[-1])

    lands = [lax.empty(_split_land_shape(pattern, a.shape), a.dtype) for a in arrs]
    out_shape = ([pltpu.SemaphoreType.DMA(())] * (2 * nsem)
                 + [pltpu.HBM(a.shape, a.dtype) for a in arrs] + [pltpu.HBM(b.shape, b.dtype) for b in lands]
                 + [jax.ShapeDtypeStruct((SUBLANES, LANES), F32)])
    outs = pl.pallas_call(
        body, name=name, out_shape=tuple(out_shape),
        in_specs=[HBM] * (2 * n) + [ANY],
        out_specs=tuple([SEM] * (2 * nsem) + [HBM] * (2 * n) + [pl.BlockSpec(memory_space=pltpu.VMEM)]),
        input_output_aliases={a: 2 * nsem + a for a in range(2 * n)},
        compiler_params=pltpu.CompilerParams(has_side_effects=DATAFLOW),
    )(*[pltpu.with_memory_space_constraint(a, pltpu.HBM) for a in list(arrs) + lands], after)
    return outs[:-1], outs[-1]


def _split_wait(state, n, pattern, after, name):
    nsem = n * (1 if pattern == "swap" else N_CHIPS - 1)

    def body(*refs):
        srcs, lands = refs[:n], refs[n:2 * n]
        send_sems, recv_sems = refs[2 * n:2 * n + nsem], refs[2 * n + nsem:2 * n + 2 * nsem]
        for cp in _split_copies(pattern, srcs, lands, send_sems, recv_sems, waiting=True):
            cp.wait_send()
            cp.wait_recv()

    sems, thru = state[:2 * nsem], state[2 * nsem:]
    outs = pl.pallas_call(
        body, name=name, out_shape=tuple(pltpu.HBM(a.shape, a.dtype) for a in thru),
        in_specs=[HBM] * (2 * n) + [SEM] * (2 * nsem) + [ANY],
        out_specs=tuple([HBM] * (2 * n)),
        input_output_aliases={a: a for a in range(2 * n)},
        compiler_params=pltpu.CompilerParams(has_side_effects=DATAFLOW),
    )(*thru, *sems, after)
    return outs[:n], outs[n:]


def _pair_gather(arrs, layer, name):
    n = len(arrs)

    def body(*refs):
        outs = refs[n:2 * n]
        send_sems, recv_sems = refs[2 * n:]
        x, y, c = _mesh_pos()
        cps = [pltpu.make_async_remote_copy(src_ref=outs[a].at[layer, c], dst_ref=outs[a].at[layer, c],
                                            send_sem=send_sems.at[a], recv_sem=recv_sems.at[a],
                                            device_id=(x, y, 1 - c), device_id_type=MESH)
               for a in range(n)]
        for cp in cps:
            cp.start()
        for cp in cps:
            cp.wait()

    return pl.pallas_call(
        body, name=name, out_shape=tuple(jax.ShapeDtypeStruct(a.shape, a.dtype) for a in arrs),
        in_specs=[ANY] * n, out_specs=tuple([ANY] * n),
        input_output_aliases={a: a for a in range(n)},
        scratch_shapes=[pltpu.SemaphoreType.DMA((n,)), pltpu.SemaphoreType.DMA((n,))],
    )(*arrs)


N_DEV = 8


def _allreduce_small(pack, name):
    r = pack.shape[0]

    def body(p_ref, o_ref, land, send_sems, recv_sems):
        x, y, c = _mesh_pos()
        me = 4 * x + 2 * y + c
        cps = []
        for k in range(1, N_DEV):
            peer = (x ^ (k >> 2), y ^ ((k >> 1) & 1), c ^ (k & 1))
            cps.append(pltpu.make_async_remote_copy(src_ref=p_ref, dst_ref=land.at[me], send_sem=send_sems.at[k - 1],
                                                    recv_sem=recv_sems.at[k - 1], device_id=peer, device_id_type=MESH))
        for cp in cps:
            cp.start()
        land[me] = p_ref[...]
        for cp in cps:
            cp.wait()
        total = land[0]
        for d in range(1, N_DEV):
            total = total + land[d]
        o_ref[...] = total

    vm = pl.BlockSpec(memory_space=pltpu.VMEM)
    return pl.pallas_call(
        body, name=name, out_shape=jax.ShapeDtypeStruct(pack.shape, F32),
        in_specs=[vm], out_specs=vm,
        scratch_shapes=[pltpu.VMEM((N_DEV, r, LANES), F32), pltpu.SemaphoreType.DMA((N_DEV - 1,)),
                        pltpu.SemaphoreType.DMA((N_DEV - 1,))],
    )(pack)


BIG_ROWS = 128


def _cast_layer(w, layer, name):
    _, r, cdim = w.shape
    tr = BIG_ROWS

    def body(w_ref, o_ref):
        o_ref[...] = w_ref[...].astype(o_ref.dtype)

    return pl.pallas_call(
        body, name=name, out_shape=jax.ShapeDtypeStruct((r, cdim), MXU_DTYPE),
        grid=(r // tr,), in_specs=[pl.BlockSpec((None, tr, cdim), lambda i: (layer, i, 0))],
        out_specs=pl.BlockSpec((tr, cdim), lambda i: (i, 0)),
        compiler_params=_params(("parallel",)),
    )(w)


def _cast_cols_major(w_t, name):
    cdim, nl, r = w_t.shape
    tc = LANES

    def body(w_ref, *o_refs):
        for l in range(nl):
            o_refs[l][...] = w_ref[:, l, :].T.astype(o_refs[l].dtype)

    out = pl.BlockSpec((r, tc), lambda i: (0, i))
    return pl.pallas_call(
        body, name=name, out_shape=tuple(jax.ShapeDtypeStruct((r, cdim), MXU_DTYPE) for _ in range(nl)),
        grid=(pl.cdiv(cdim, tc),), in_specs=[pl.BlockSpec((tc, nl, r), lambda i: (i, 0, 0))],
        out_specs=tuple([out] * nl),
        compiler_params=_params(("parallel",)),
    )(w_t)


def _pair_sum(parts, sib, which, out_dtype, name):
    k, _, r, cdim = parts.shape
    tr = BIG_ROWS

    def body(sel_ref, p_ref, s_ref, o_ref):
        o_ref[...] = (p_ref[...] + s_ref[...]).astype(o_ref.dtype)

    grid_spec = pltpu.PrefetchScalarGridSpec(
        num_scalar_prefetch=1, grid=(k, r // tr),
        in_specs=[pl.BlockSpec((None, None, tr, cdim), lambda l, i, sel: (l, sel[0], i, 0)),
                  pl.BlockSpec((None, tr, cdim), lambda l, i, sel: (l, i, 0))],
        out_specs=pl.BlockSpec((None, tr, cdim), lambda l, i, sel: (l, i, 0)))
    return pl.pallas_call(
        body, name=name, out_shape=jax.ShapeDtypeStruct((k, r, cdim), out_dtype), grid_spec=grid_spec,
        compiler_params=_params(("parallel", "parallel")),
    )(which.reshape(1).astype(jnp.int32), parts, sib)


def _sum_lead(parts, into, layer, which, name):
    k, r, cdim = parts.shape
    tr = BIG_ROWS

    def body(sel_ref, p_ref, _, o_ref):
        total = p_ref[0].astype(F32)
        for a in range(1, k):
            total = total + p_ref[a].astype(F32)
        o_ref[...] = total

    grid_spec = pltpu.PrefetchScalarGridSpec(
        num_scalar_prefetch=1, grid=(r // tr,),
        in_specs=[pl.BlockSpec((k, tr, cdim), lambda i, sel: (0, i, 0)), ANY],
        out_specs=pl.BlockSpec((None, None, tr, cdim), lambda i, sel: (layer, sel[0], i, 0)))
    return pl.pallas_call(
        body, name=name, out_shape=jax.ShapeDtypeStruct(into.shape, F32), grid_spec=grid_spec,
        input_output_aliases={2: 0},
        compiler_params=_params(("parallel",)),
    )(which.reshape(1).astype(jnp.int32), parts, into)


def _adam_math(w, g, m, v):
    m2 = ADAM_B1 * m + (1.0 - ADAM_B1) * g
    v2 = ADAM_B2 * v + (1.0 - ADAM_B2) * (g * g)
    m_hat = m2 / (1.0 - ADAM_B1 ** ADAM_STEP)
    v_hat = v2 / (1.0 - ADAM_B2 ** ADAM_STEP)
    delta = -ADAM_LR * (m_hat / (jnp.sqrt(v_hat) + ADAM_EPS) + ADAM_WD * w)
    return delta, m2, v2


def _adam_big(w, g, m, v, name):
    nl, r, cdim = w.shape
    tr = BIG_ROWS

    def body(w_ref, g_ref, m_ref, v_ref, d_ref, mo_ref, vo_ref):
        delta, m2, v2 = _adam_math(w_ref[...], g_ref[...], m_ref[...], v_ref[...])
        d_ref[...] = delta
        mo_ref[...] = m2
        vo_ref[...] = v2

    blk = pl.BlockSpec((None, tr, cdim), lambda l, i: (l, i, 0))
    shp = jax.ShapeDtypeStruct(w.shape, F32)
    return pl.pallas_call(
        body, name=name, out_shape=(shp, shp, shp),
        grid=(nl, r // tr), in_specs=[blk] * 4, out_specs=(blk, blk, blk),
        compiler_params=_params(("parallel", "parallel")),
    )(w, g, m, v)


def _adam_cols_major(w, g, m, v, name):
    cdim, nl, r = w.shape
    tc = BIG_ROWS

    def body(w_ref, g_ref, m_ref, v_ref, d_ref, mo_ref, vo_ref):
        delta, m2, v2 = _adam_math(w_ref[...], g_ref[...], m_ref[...], v_ref[...])
        d_ref[...] = delta
        mo_ref[...] = m2
        vo_ref[...] = v2

    blk = pl.BlockSpec((tc, nl, r), lambda i: (i, 0, 0))
    shp = jax.ShapeDtypeStruct(w.shape, F32)
    return pl.pallas_call(
        body, name=name, out_shape=(shp, shp, shp),
        grid=(pl.cdiv(cdim, tc),), in_specs=[blk] * 4, out_specs=(blk, blk, blk),
        compiler_params=_params(("parallel",)),
    )(w, g, m, v)


def _adam_small(ws, gs, ms, vs, name):
    n = len(ws)

    def body(*refs):
        w_refs, g_refs, m_refs, v_refs = (refs[k * n:(k + 1) * n] for k in range(4))
        d_refs, mo_refs, vo_refs = (refs[(4 + k) * n:(5 + k) * n] for k in range(3))
        for a in range(n):
            delta, m2, v2 = _adam_math(w_refs[a][...], g_refs[a][...], m_refs[a][...], v_refs[a][...])
            d_refs[a][...] = delta
            mo_refs[a][...] = m2
            vo_refs[a][...] = v2

    shapes = tuple(jax.ShapeDtypeStruct(w.shape, F32) for w in ws)
    vm = pl.BlockSpec(memory_space=pltpu.VMEM)
    outs = pl.pallas_call(body, name=name, out_shape=shapes * 3, in_specs=[vm] * (4 * n),
                          out_specs=tuple([vm] * (3 * n)))(*ws, *gs, *ms, *vs)
    return outs[:n], outs[n:2 * n], outs[2 * n:]


PACK_TILE = SUBLANES * LANES


def _pack(arrays):
    rows = []
    for a in arrays:
        flat = a.reshape(-1)
        pad = (-flat.shape[0]) % PACK_TILE
        if pad:
            flat = jnp.concatenate([flat, jnp.zeros((pad,), flat.dtype)])
        rows.append(flat.reshape(-1, LANES))
    return jnp.concatenate(rows, axis=0)


def _unpack(pack, shapes):
    outs, row = [], 0
    for shp in shapes:
        n = int(np.prod(shp))
        nrows = -(-n // PACK_TILE) * SUBLANES
        outs.append(pack[row:row + nrows].reshape(-1)[:n].reshape(shp))
        row += nrows
    return outs


SMALL = ["norm_w", "ssd_conv_b", "ssd_dt_bias", "ssd_a_log", "ssd_d", "ssd_norm_w", "attn_sinks",
         "conf_dw_b", "conf_ln_w", "conf_ln_b"]
WEIGHTS = ["norm_w", "w_in", "ssd_conv_w", "ssd_conv_b", "ssd_dt_bias", "ssd_a_log", "ssd_d", "ssd_norm_w",
           "attn_sinks", "conf_dw_w", "conf_dw_b", "conf_ln_w", "conf_ln_b", "w_out", "final_norm_w"]


def kernel(x, norm_w, w_in, ssd_conv_w, ssd_conv_b, ssd_dt_bias, ssd_a_log, ssd_d, ssd_norm_w, attn_sinks, conf_dw_w, conf_dw_b, conf_ln_w, conf_ln_b, w_out, final_norm_w, loss_target, m_norm_w, m_w_in, m_ssd_conv_w, m_ssd_conv_b, m_ssd_dt_bias, m_ssd_a_log, m_ssd_d, m_ssd_norm_w, m_attn_sinks, m_conf_dw_w, m_conf_dw_b, m_conf_ln_w, m_conf_ln_b, m_w_out, m_final_norm_w, v_norm_w, v_w_in, v_ssd_conv_w, v_ssd_conv_b, v_ssd_dt_bias, v_ssd_a_log, v_ssd_d, v_ssd_norm_w, v_attn_sinks, v_conf_dw_w, v_conf_dw_b, v_conf_ln_w, v_conf_ln_b, v_w_out, v_final_norm_w):
    w = dict(norm_w=norm_w, w_in=w_in, ssd_conv_w=ssd_conv_w, ssd_conv_b=ssd_conv_b, ssd_dt_bias=ssd_dt_bias,
             ssd_a_log=ssd_a_log, ssd_d=ssd_d, ssd_norm_w=ssd_norm_w, attn_sinks=attn_sinks, conf_dw_w=conf_dw_w,
             conf_dw_b=conf_dw_b, conf_ln_w=conf_ln_w, conf_ln_b=conf_ln_b, w_out=w_out, final_norm_w=final_norm_w)
    m = dict(norm_w=m_norm_w, w_in=m_w_in, ssd_conv_w=m_ssd_conv_w, ssd_conv_b=m_ssd_conv_b,
             ssd_dt_bias=m_ssd_dt_bias, ssd_a_log=m_ssd_a_log, ssd_d=m_ssd_d, ssd_norm_w=m_ssd_norm_w,
             attn_sinks=m_attn_sinks, conf_dw_w=m_conf_dw_w, conf_dw_b=m_conf_dw_b, conf_ln_w=m_conf_ln_w,
             conf_ln_b=m_conf_ln_b, w_out=m_w_out, final_norm_w=m_final_norm_w)
    v = dict(norm_w=v_norm_w, w_in=v_w_in, ssd_conv_w=v_ssd_conv_w, ssd_conv_b=v_ssd_conv_b,
             ssd_dt_bias=v_ssd_dt_bias, ssd_a_log=v_ssd_a_log, ssd_d=v_ssd_d, ssd_norm_w=v_ssd_norm_w,
             attn_sinks=v_attn_sinks, conf_dw_w=v_conf_dw_w, conf_dw_b=v_conf_dw_b, conf_ln_w=v_conf_ln_w,
             conf_ln_b=v_conf_ln_b, w_out=v_w_out, final_norm_w=v_final_norm_w)
    depth = w_in.shape[0]
    me = 2 * lax.axis_index("x") + lax.axis_index("y")

    assert depth == 2
    w_in_t = jnp.transpose(w_in, (2, 0, 1))
    w_in_b = _cast_cols_major(w_in_t, name="cast_w_in")
    w_out_b = [_cast_layer(w_out, li, name=f"cast_w_out_l{li}") for li in range(depth)]
    own0 = [w_in_b[0].reshape((2, -1) + w_in_b[0].shape[1:]), ssd_conv_w, conf_dw_w]
    gathered0 = _gather_weights(own0[:1], own0[1:], name="gather_weights_l0")
    g_in0, g_conv, g_dw = [lax.dynamic_update_index_in_dim(g_all, mine, me, 0)
                           for g_all, mine in zip(gathered0, own0)]
    own1 = [w_out_b[0], w_in_b[1], w_out_b[1]]
    pending1, token1 = _split_start(own1, "bcast", gathered0[0], name="gather_rest_start")
    rest = {}

    def small_full(li):
        return (jnp.concatenate([g_conv[p, li] for p in range(N_CHIPS)], axis=1),
                jnp.concatenate([g_dw[p, li] for p in range(N_CHIPS)], axis=1))

    def w_out_l0(after):
        mine1, landed = _split_wait(pending1, len(own1), "bcast", after, name="gather_rest_wait")
        rest["landed"] = [lax.dynamic_update_index_in_dim(g_all, mine, me, 0) for g_all, mine in zip(landed, mine1)]
        return rest["landed"][0].reshape(-1, w_out.shape[2])

    def params_l0(_):
        w_in_p = _padded_from_chips([g_in0[p].reshape(w_in_b[0].shape) for p in range(N_CHIPS)])
        return _layer_params(0, w_in_p, w_out_l0, *small_full(0), w)

    def params_l1(_):
        _, g_in1, g_out1 = rest["landed"]
        w_in_p = _padded_from_chips([g_in1[p] for p in range(N_CHIPS)])
        return _layer_params(1, w_in_p, g_out1.reshape(-1, g_out1.shape[-1]), *small_full(1), w)

    c = lax.axis_index("c")
    cols = w_in.shape[2]
    rows_out = w_out.shape[1]

    def grad_parts(g):
        dw = g["w_in_p"]
        return [dw.reshape(1, 2, dw.shape[0] // 2, dw.shape[1]),
                g["w_out"].reshape(N_CHIPS, 2, rows_out // 2, D_MODEL)]

    def pair_sums(parts, sib, tag):
        s_in, s_out = [_pair_sum(p, sb, c, MXU_DTYPE, name=f"grad_pair_sum_{k}_{tag}")
                       for k, (p, sb) in enumerate(zip(parts, sib))]
        return [jnp.stack([_chip_part_from_padded(s_in[0], p, cols) for p in range(N_CHIPS)]), s_out]

    split = {"reduced": [lax.empty((depth, 2, w_in.shape[1] // 2, cols), F32),
                         lax.empty((depth, 2, rows_out // 2, D_MODEL), F32)]}

    def chip_sums(landed, sent, li):
        filled = [lax.dynamic_update_index_in_dim(r, lax.dynamic_index_in_dim(sk, me, 0, keepdims=False), me, 0)
                  for r, sk in zip(landed, sent)]
        halves = [_sum_lead(r, into, li, c, name=f"grad_chip_sum_{k}_l{li}")
                  for k, (r, into) in enumerate(zip(filled, split["reduced"]))]
        split["reduced"] = list(_pair_gather(halves, li, name=f"grad_pair_gather_l{li}"))

    def bwd_hooks(li):
        def after_dw(dw_in_p, dw_out):
            parts = grad_parts(dict(w_in_p=dw_in_p, w_out=dw_out))
            state, token = _split_start(parts, "swap", dw_out, name=f"grad_swap_l{li}_start")
            split[f"swap{li}"] = (parts, state)
            return token

        hooks = {"after_dw": after_dw}
        if li == depth - 2:
            parts, swap_state = split[f"swap{depth - 1}"]

            def after_dycat(dycat):
                mine, sib = _split_wait(swap_state, len(parts), "swap", dycat, name="grad_swap_l1_wait")
                sent = pair_sums(mine, sib, "l1")
                split["scatter"], token = _split_start(sent, "scatter", sent[0], name="grad_scatter_l1_start")
                return token

            def after_attn(dproj):
                sent, landed = _split_wait(split["scatter"], len(parts), "scatter", dproj,
                                           name="grad_scatter_l1_wait")
                chip_sums(landed, sent, depth - 1)

            hooks.update(after_dycat=after_dycat, after_attn=after_attn)
        return hooks

    loss, grad_x, grads, dfinal = _local_step(x, loss_target, [params_l0, params_l1], final_norm_w,
                                              first_after=token1, bwd_hooks=bwd_hooks)

    parts0, swap0 = split["swap0"]
    sent0 = pair_sums(*_split_wait(swap0, len(parts0), "swap", grad_x, name="grad_swap_l0_wait"), "l0")
    scatter0, token0 = _split_start(sent0, "scatter", sent0[0], name="grad_scatter_l0_start")

    small_list = [grads[li][n] for li in range(depth) for n in SMALL]
    small_list += [grads[li][n] for li in range(depth) for n in ("ssd_conv_w", "conf_dw_w")]
    small_list += [dfinal, loss.reshape(1)]
    small_shapes = [a.shape for a in small_list]
    reduced = _unpack(_allreduce_small(_pack(small_list) + token0[0, 0], name="allreduce_small"), small_shapes)
    ns = len(SMALL)
    g = {n: jnp.stack([reduced[li * ns + i] for li in range(depth)]) for i, n in enumerate(SMALL)}
    conv_w_cols, dw_w_cols = ssd_conv_w.shape[2], conf_dw_w.shape[2]
    g["ssd_conv_w"] = jnp.stack([lax.dynamic_slice_in_dim(reduced[depth * ns + 2 * li], me * conv_w_cols,
                                                          conv_w_cols, axis=1) for li in range(depth)])
    g["conf_dw_w"] = jnp.stack([lax.dynamic_slice_in_dim(reduced[depth * ns + 2 * li + 1], me * dw_w_cols,
                                                         dw_w_cols, axis=1) for li in range(depth)])
    g["final_norm_w"] = reduced[-2]
    loss_total = reduced[-1][0]

    small_names = [n for n in WEIGHTS if n not in ("w_in", "w_out")]

    def as2d(a):
        return a.reshape(1, -1) if a.ndim == 1 else a

    deltas, new_ms, new_vs = _adam_small(*[[as2d(src[n]) for n in small_names] for src in (w, g, m, v)],
                                         name="adam_small")

    sent0, landed0 = _split_wait(scatter0, len(sent0), "scatter", deltas[0], name="grad_scatter_l0_wait")
    chip_sums(landed0, sent0, 0)
    g_w_in = split["reduced"][0].reshape(w_in.shape)
    g_w_out = split["reduced"][1].reshape(w_out.shape)

    outs_g, outs_d, outs_m, outs_v = {"w_in": g_w_in, "w_out": g_w_out}, {}, {}, {}
    to_cols, from_cols = (2, 0, 1), (1, 2, 0)
    outs_d["w_in"], outs_m["w_in"], outs_v["w_in"] = [
        jnp.transpose(a, from_cols) for a in _adam_cols_major(
            *[jnp.transpose(a, to_cols) for a in (w_in, g_w_in, m_w_in, v_w_in)], name="adam_w_in")]
    outs_d["w_out"], outs_m["w_out"], outs_v["w_out"] = _adam_big(w_out, g_w_out, m_w_out, v_w_out,
                                                                  name="adam_w_out")
    for n, dn, mn, vn in zip(small_names, deltas, new_ms, new_vs):
        outs_g[n], outs_d[n], outs_m[n], outs_v[n] = (g[n], dn.reshape(w[n].shape), mn.reshape(w[n].shape),
                                                      vn.reshape(w[n].shape))
    return (loss_total, grad_x, *[outs_g[n] for n in WEIGHTS], *[outs_d[n] for n in WEIGHTS],
            *[outs_m[n] for n in WEIGHTS], *[outs_v[n] for n in WEIGHTS])
```

```python
import functools
import math

import jax
import jax.numpy as jnp
import numpy as np
from jax import lax
from jax.experimental import pallas as pl
from jax.experimental.pallas import tpu as pltpu

F32 = jnp.float32
BF16 = jnp.bfloat16
MXU_DTYPE = BF16

D_MODEL = 1024
DEPTH = 2
SSD_HEADS = 16
SSD_HEAD_DIM = 64
SSD_STATE = 128
SSD_CONV = 4
CHUNK = 128
SSD_CONV_DIM = 1536
ATTN_HEAD_DIM = 64
ATTN_Q_HEADS = 8
WINDOW = 128
CONF_WIDTH = 512
CONF_KERNEL = 31
MIX_WIDTH = 2048
D_IN_PROJ = 5392
EPS = 1e-5

ADAM_LR = 0.001
ADAM_B1 = 0.9
ADAM_B2 = 0.999
ADAM_EPS = 1e-08
ADAM_WD = 0.01
ADAM_STEP = 10

LANES = 128
SUBLANES = 8
VMEM_LIMIT = 48 * 1024 * 1024

NP = 5632
OFF_ZA, OFF_Q, OFF_K, OFF_V, OFF_DT = 0, 512, 1024, 1152, 1280
ATTN_GROUP = 1536
OFF_CONF, OFF_ZC = 1536, 2560
CONF_GROUP = 1536
OFF_ZS = 3072
OFF_XBC = 4096
SECTIONS = ((0, 1024, OFF_ZS), (1024, 1536, OFF_ZA), (1536, 2048, OFF_ZC), (2048, 3584, OFF_XBC),
            (3584, 3600, OFF_DT), (3600, 4368, OFF_Q), (4368, 5392, OFF_CONF))

YCAT_ATTN, YCAT_CONF = 1024, 1536
ANY = pl.BlockSpec(memory_space=pl.ANY)

NN = (((1,), (0,)), ((), ()))
NT = (((1,), (1,)), ((), ()))
TN = (((0,), (0,)), ((), ()))


def _params(sem):
    return pltpu.CompilerParams(dimension_semantics=sem, vmem_limit_bytes=VMEM_LIMIT)


def _dot(a, b, dims=NN):
    return lax.dot_general(a.astype(MXU_DTYPE), b.astype(MXU_DTYPE), dims, preferred_element_type=F32)


def _split_bf16(a, passes):
    pieces = []
    r = a
    for _ in range(passes):
        p = r.astype(BF16)
        pieces.append(p)
        r = r - p.astype(F32)
    return pieces


def _xdot(a, sel, dims=NN, passes=2):
    out = None
    for p in _split_bf16(a, passes):
        t = lax.dot_general(p, sel, dims, preferred_element_type=F32)
        out = t if out is None else out + t
    return out


def _xdot_r(sel, b, dims=NN, passes=3):
    out = None
    for p in _split_bf16(b, passes):
        t = lax.dot_general(sel, p, dims, preferred_element_type=F32)
        out = t if out is None else out + t
    return out


def _sigmoid(x):
    return 1.0 / (1.0 + jnp.exp(-x))


def _silu(x):
    return x * _sigmoid(x)


def _dsilu(x):
    s = _sigmoid(x)
    return s * (1.0 + x * (1.0 - s))


def _softplus(x):
    return jnp.maximum(x, 0.0) + jnp.log(1.0 + jnp.exp(-jnp.abs(x)))


def _rowsum8(x):
    r, c = x.shape
    return jnp.sum(x.reshape(r // SUBLANES, SUBLANES, c), axis=0)


def _iota(shape, dim):
    return lax.broadcasted_iota(jnp.int32, shape, dim)


def _matmul(a, b, form, out_dtype, tm, tn, tk, name, residual=None, after=None):
    if form == "nn":
        (m, k), n = a.shape, b.shape[1]
    elif form == "nt":
        (m, k), n = a.shape, b.shape[0]
    else:
        (k, m), n = a.shape, b.shape[1]
    tm, tn, tk = min(tm, m), min(tn, n), min(tk, k)
    assert m % tm == 0 and n % tn == 0 and k % tk == 0, (name, m, n, k, tm, tn, tk)
    if form == "nn":
        a_spec = pl.BlockSpec((tm, tk), lambda i, j, s: (i, s))
        b_spec = pl.BlockSpec((tk, tn), lambda i, j, s: (s, j))
        dims = NN
    elif form == "nt":
        (m, k), n = a.shape, b.shape[0]
        a_spec = pl.BlockSpec((tm, tk), lambda i, j, s: (i, s))
        b_spec = pl.BlockSpec((tn, tk), lambda i, j, s: (j, s))
        dims = NT
    else:
        (k, m), n = a.shape, b.shape[1]
        a_spec = pl.BlockSpec((tk, tm), lambda i, j, s: (s, i))
        b_spec = pl.BlockSpec((tk, tn), lambda i, j, s: (s, j))
        dims = TN
    nk = k // tk
    has_res = residual is not None
    deps = [] if after is None else [after]

    def body_single(a_ref, b_ref, *rest):
        o = _dot(a_ref[...], b_ref[...], dims)
        if has_res:
            o = o + rest[0][...]
        rest[-1][...] = o.astype(out_dtype)

    def body(a_ref, b_ref, *rest):
        r_ref = rest[0] if has_res else None
        o_ref, acc = rest[-2:]
        s = pl.program_id(2)

        @pl.when(s == 0)
        def _():
            acc[...] = jnp.zeros_like(acc)

        acc[...] += _dot(a_ref[...], b_ref[...], dims)

        @pl.when(s == nk - 1)
        def _():
            o = acc[...]
            if has_res:
                o = o + r_ref[...]
            o_ref[...] = o.astype(out_dtype)

    in_specs = [a_spec, b_spec]
    args = [a, b]
    if has_res:
        in_specs.append(pl.BlockSpec((tm, tn), lambda i, j, s: (i, j)))
        args.append(residual)
    in_specs += [ANY] * len(deps)
    args += deps
    return pl.pallas_call(
        body_single if nk == 1 else body, name=name,
        out_shape=jax.ShapeDtypeStruct((m, n), out_dtype),
        grid=(m // tm, n // tn, nk),
        in_specs=in_specs,
        out_specs=pl.BlockSpec((tm, tn), lambda i, j, s: (i, j)),
        scratch_shapes=[] if nk == 1 else [pltpu.VMEM((tm, tn), F32)],
        compiler_params=_params(("parallel", "parallel", "arbitrary")),
    )(*args)


ROW_TILE = 256


PROJ_FWD_TM, PROJ_FWD_TN = 1024, 512


def _proj_fwd(x, w, w_in_p, name, after=None):
    t, d = x.shape
    n = w_in_p.shape[1]
    tm, tn = min(PROJ_FWD_TM, t), PROJ_FWD_TN
    assert t % tm == 0 and n % tn == 0
    deps = [] if after is None else [after]

    def body(x_ref, w_ref, b_ref, *rest):
        o_ref, ot_ref, h_scr = rest[len(deps):]

        @pl.when(pl.program_id(1) == 0)
        def _():
            xv = x_ref[...]
            rstd = lax.rsqrt(jnp.mean(xv * xv, axis=-1, keepdims=True) + EPS)
            h = xv * rstd * w_ref[...]
            h_scr[...] = h.astype(h_scr.dtype)
            ot_ref[...] = h.T.astype(ot_ref.dtype)

        o_ref[...] = _dot(h_scr[...], b_ref[...])

    return pl.pallas_call(
        body, name=name,
        out_shape=(jax.ShapeDtypeStruct((t, n), F32), jax.ShapeDtypeStruct((d, t), MXU_DTYPE)),
        grid=(t // tm, n // tn),
        in_specs=[pl.BlockSpec((tm, d), lambda i, j: (i, 0)), pl.BlockSpec((1, d), lambda i, j: (0, 0)),
                  pl.BlockSpec((d, tn), lambda i, j: (0, j))] + [ANY] * len(deps),
        out_specs=(pl.BlockSpec((tm, tn), lambda i, j: (i, j)), pl.BlockSpec((d, tm), lambda i, j: (0, i))),
        scratch_shapes=[pltpu.VMEM((tm, d), MXU_DTYPE)],
        compiler_params=_params(("parallel", "arbitrary")),
    )(x, w, w_in_p, *deps)


PROJ_BWD_TM, PROJ_BWD_TK = 1024, 1408


def _proj_bwd_dx(dproj, w_in_p, x, w, dres, name):
    t, d = x.shape
    kdim = dproj.shape[1]
    tm, tk = min(PROJ_BWD_TM, t), PROJ_BWD_TK
    nt, nk = t // tm, kdim // tk
    assert t % tm == 0 and kdim % tk == 0

    def body(a_ref, b_ref, x_ref, w_ref, dr_ref, dx_ref, dw_ref, acc, wacc):
        i, s = pl.program_id(0), pl.program_id(1)

        @pl.when((i == 0) & (s == 0))
        def _():
            wacc[...] = jnp.zeros_like(wacc)

        @pl.when(s == 0)
        def _():
            acc[...] = jnp.zeros_like(acc)

        acc[...] += _dot(a_ref[...], b_ref[...], NT)

        @pl.when(s == nk - 1)
        def _():
            xv = x_ref[...]
            rstd = lax.rsqrt(jnp.mean(xv * xv, axis=-1, keepdims=True) + EPS)
            xh = xv * rstd
            dhv = acc[...]
            g = dhv * w_ref[...]
            dx_ref[...] = dr_ref[...] + rstd * (g - xh * jnp.mean(g * xh, axis=-1, keepdims=True))
            wacc[...] += _rowsum8(dhv * xh)

        @pl.when((i == nt - 1) & (s == nk - 1))
        def _():
            dw_ref[...] = jnp.sum(wacc[...], axis=0, keepdims=True)

    row = pl.BlockSpec((tm, d), lambda i, s: (i, 0))
    vec = pl.BlockSpec((1, d), lambda i, s: (0, 0))
    return pl.pallas_call(
        body, name=name,
        out_shape=(jax.ShapeDtypeStruct((t, d), F32), jax.ShapeDtypeStruct((1, d), F32)),
        grid=(nt, nk),
        in_specs=[pl.BlockSpec((tm, tk), lambda i, s: (i, s)), pl.BlockSpec((d, tk), lambda i, s: (0, s)),
                  row, vec, row],
        out_specs=(row, vec),
        scratch_shapes=[pltpu.VMEM((tm, d), F32), pltpu.VMEM((SUBLANES, d), F32)],
        compiler_params=_params(("arbitrary", "arbitrary")),
    )(dproj, w_in_p, x, w, dres)


def _loss_head(xf, target, w, name):
    t, d = xf.shape
    tm = ROW_TILE
    nt = t // tm

    def body(x_ref, t_ref, w_ref, loss_ref, dx_ref, dw_ref, lacc, wacc):
        i = pl.program_id(0)

        @pl.when(i == 0)
        def _():
            lacc[...] = jnp.zeros_like(lacc)
            wacc[...] = jnp.zeros_like(wacc)

        xv = x_ref[...]
        rstd = lax.rsqrt(jnp.mean(xv * xv, axis=-1, keepdims=True) + EPS)
        xh = xv * rstd
        err = xh * w_ref[...] - t_ref[...]
        lacc[...] += jnp.sum(err * err)
        dy = err * (1.0 / d)
        g = dy * w_ref[...]
        dx_ref[...] = rstd * (g - xh * jnp.mean(g * xh, axis=-1, keepdims=True))
        wacc[...] += _rowsum8(dy * xh)

        @pl.when(i == nt - 1)
        def _():
            loss_ref[...] = lacc[...] * (0.5 / d)
            dw_ref[...] = jnp.sum(wacc[...], axis=0, keepdims=True)

    row = pl.BlockSpec((tm, d), lambda i: (i, 0))
    vec = pl.BlockSpec((1, d), lambda i: (0, 0))
    return pl.pallas_call(
        body, name=name,
        out_shape=(jax.ShapeDtypeStruct((SUBLANES, LANES), F32), jax.ShapeDtypeStruct((t, d), F32),
                   jax.ShapeDtypeStruct((1, d), F32)),
        grid=(nt,),
        in_specs=[row, row, vec],
        out_specs=(pl.BlockSpec((SUBLANES, LANES), lambda i: (0, 0)), row, vec),
        scratch_shapes=[pltpu.VMEM((SUBLANES, LANES), F32), pltpu.VMEM((SUBLANES, d), F32)],
        compiler_params=_params(("arbitrary",)),
    )(xf, target, w)


CONV_TILE = 512
CONV_COLS = 512
CONV_SUB_ROWS = 128
CONV_SUB_COLS = LANES


def _conv_halo(k):
    return SUBLANES if k - 1 <= SUBLANES else 32


def _conv_subtiles(tm, cw):
    return [(r0, c0) for r0 in range(0, tm, CONV_SUB_ROWS) for c0 in range(0, cw, CONV_SUB_COLS)]


def _conv_use_shifted(k):
    return k > SUBLANES


def _conv_shift_scratch(k, rows, cw):
    return [pltpu.VMEM((SUBLANES - 1, rows - SUBLANES, cw), F32)] if _conv_use_shifted(k) else []


def _conv_fill_shifted(ext, sh):
    n = sh.shape[1]
    for b in range(1, SUBLANES):
        sh[b - 1] = ext[b:b + n, :]


def _conv_rows(ext, sh, start, rows, cs):
    b = start % SUBLANES
    if b == 0 or not sh:
        return ext[start:start + rows, cs]
    return sh[0][b - 1, start - b:start - b + rows, cs]


def _conv_fwd(src, col0, width, w, bias, k, seq, name):
    t = src.shape[0]
    tm, cw, halo = CONV_TILE, CONV_COLS, _conv_halo(k)
    sr, sc = CONV_SUB_ROWS, CONV_SUB_COLS
    p = k - 1
    cb0 = col0 // cw
    kp = w.shape[0]

    shifted = _conv_use_shifted(k)

    def body(x_ref, h_ref, w_ref, b_ref, o_ref, ext, *sh):
        i = pl.program_id(0)
        seq_start = (i * tm) % seq == 0
        ext[halo:, :] = x_ref[...]
        ext[:halo, :] = jnp.where(seq_start, 0.0, h_ref[...])
        if shifted:
            _conv_fill_shifted(ext, sh[0])
        for r0, c0 in _conv_subtiles(tm, cw):
            cs = slice(c0, c0 + sc)
            acc = jnp.zeros((sr, sc), F32) + b_ref[:, cs]
            for j in range(k):
                acc = acc + w_ref[j:j + 1, cs] * _conv_rows(ext, sh, r0 + halo - p + j, sr, cs)
            o_ref[r0:r0 + sr, cs] = acc

    return pl.pallas_call(
        body, name=name,
        out_shape=jax.ShapeDtypeStruct((t, width), F32),
        grid=(t // tm, width // cw),
        in_specs=[pl.BlockSpec((tm, cw), lambda i, j: (i, cb0 + j)),
                  pl.BlockSpec((halo, cw), lambda i, j: (jnp.maximum(i * (tm // halo) - 1, 0), cb0 + j)),
                  pl.BlockSpec((kp, cw), lambda i, j: (0, j)),
                  pl.BlockSpec((1, cw), lambda i, j: (0, j))],
        out_specs=pl.BlockSpec((tm, cw), lambda i, j: (i, j)),
        scratch_shapes=[pltpu.VMEM((halo + tm, cw), F32)] + _conv_shift_scratch(k, halo + tm, cw),
        compiler_params=_params(("parallel", "parallel")),
    )(src, src, w, bias)


def _conv_bwd(dy, src, col0, width, w, k, seq, name, into=None):
    t = src.shape[0]
    tm, cw, halo = CONV_TILE, CONV_COLS, _conv_halo(k)
    sr, sc = CONV_SUB_ROWS, CONV_SUB_COLS
    p = k - 1
    cb0 = col0 // cw
    kp = w.shape[0]
    nt = t // tm
    last_halo = t // halo - 1

    shifted = _conv_use_shifted(k)

    def body(dy_ref, dn_ref, x_ref, xp_ref, w_ref, *rest):
        if into is not None:
            rest = rest[1:]
        dx_ref, dw_ref, db_ref, dyext, xext, wacc, bacc = rest[:7]
        sh = rest[7:]
        i = pl.program_id(1)
        dysh, xsh = (sh[:1], sh[1:]) if shifted else ((), ())

        @pl.when(i == 0)
        def _():
            wacc[...] = jnp.zeros_like(wacc)
            bacc[...] = jnp.zeros_like(bacc)

        seq_start = (i * tm) % seq == 0
        seq_end = ((i + 1) * tm) % seq == 0
        dyext[:tm, :] = dy_ref[...]
        dyext[tm:, :] = jnp.where(seq_end, 0.0, dn_ref[...])
        xext[halo:, :] = x_ref[...]
        xext[:halo, :] = jnp.where(seq_start, 0.0, xp_ref[...])
        if shifted:
            _conv_fill_shifted(dyext, dysh[0])
            _conv_fill_shifted(xext, xsh[0])
        for r0, c0 in _conv_subtiles(tm, cw):
            cs = slice(c0, c0 + sc)
            dyv = dy_ref[r0:r0 + sr, cs]
            acc = jnp.zeros((sr, sc), F32)
            for j in range(k):
                acc = acc + w_ref[j:j + 1, cs] * _conv_rows(dyext, dysh, r0 + p - j, sr, cs)
                wacc[j, :, cs] += _rowsum8(dyv * _conv_rows(xext, xsh, r0 + halo - p + j, sr, cs))
            dx_ref[r0:r0 + sr, cs] = acc.astype(dx_ref.dtype)
            bacc[:, cs] += _rowsum8(dyv)

        @pl.when(i == nt - 1)
        def _():
            dw_ref[...] = jnp.zeros_like(dw_ref)
            for j in range(k):
                dw_ref[j:j + 1, :] = jnp.sum(wacc[j], axis=0, keepdims=True)
            db_ref[...] = jnp.sum(bacc[...], axis=0, keepdims=True)

    if into is None:
        dx_shape = jax.ShapeDtypeStruct((t, width), F32)
        dx_spec = pl.BlockSpec((tm, cw), lambda j, i: (i, j))
        extra_specs, extra_args, aliases = [], [], {}
    else:
        dx_shape = jax.ShapeDtypeStruct(into.shape, into.dtype)
        dx_spec = pl.BlockSpec((tm, cw), lambda j, i: (i, cb0 + j))
        extra_specs, extra_args, aliases = [ANY], [into], {5: 0}
    return pl.pallas_call(
        body, name=name,
        out_shape=(dx_shape, jax.ShapeDtypeStruct((kp, width), F32), jax.ShapeDtypeStruct((1, width), F32)),
        grid=(width // cw, nt),
        in_specs=[pl.BlockSpec((tm, cw), lambda j, i: (i, j)),
                  pl.BlockSpec((halo, cw), lambda j, i: (jnp.minimum((i + 1) * (tm // halo), last_halo), j)),
                  pl.BlockSpec((tm, cw), lambda j, i: (i, cb0 + j)),
                  pl.BlockSpec((halo, cw), lambda j, i: (jnp.maximum(i * (tm // halo) - 1, 0), cb0 + j)),
                  pl.BlockSpec((kp, cw), lambda j, i: (0, j))] + extra_specs,
        out_specs=(dx_spec,
                   pl.BlockSpec((kp, cw), lambda j, i: (0, j)),
                   pl.BlockSpec((1, cw), lambda j, i: (0, j))),
        input_output_aliases=aliases,
        scratch_shapes=[pltpu.VMEM((tm + halo, cw), F32), pltpu.VMEM((halo + tm, cw), F32),
                        pltpu.VMEM((kp, SUBLANES, cw), F32), pltpu.VMEM((SUBLANES, cw), F32)]
        + 2 * _conv_shift_scratch(k, halo + tm, cw),
        compiler_params=_params(("parallel", "arbitrary")),
    )(dy, dy, src, src, w, *extra_args)


def _conf_specs(tm, cw, halo, order):
    cb = OFF_CONF // cw

    def blk(col):
        return pl.BlockSpec((tm, cw), lambda *g: (order(*g), col))

    def prev(col):
        return pl.BlockSpec((halo, cw), lambda *g: (jnp.maximum(order(*g) * (tm // halo) - 1, 0), col))

    return blk(cb), prev(cb), blk(cb + 1), prev(cb + 1)


def _glu_window(ext, a_ref, ah_ref, g_ref, gh_ref, seq_start, halo):
    ext[halo:, :] = a_ref[...] * _sigmoid(g_ref[...])
    ext[:halo, :] = jnp.where(seq_start, 0.0, ah_ref[...] * _sigmoid(gh_ref[...]))


def _conf_fwd(proj, w, bias, ln_w, ln_b, ycat, seq, name):
    t = proj.shape[0]
    k = CONF_KERNEL
    tm, cw, halo = CONV_TILE, CONF_WIDTH, _conv_halo(k)
    sr, sc = CONV_SUB_ROWS, CONV_SUB_COLS
    p = k - 1
    kp = w.shape[0]

    def body(a_ref, ah_ref, g_ref, gh_ref, z_ref, w_ref, b_ref, lw_ref, lb_ref, _, c1_ref, y_ref, ext, sh):
        i = pl.program_id(0)
        _glu_window(ext, a_ref, ah_ref, g_ref, gh_ref, (i * tm) % seq == 0, halo)
        _conv_fill_shifted(ext, sh)
        for r0, c0 in _conv_subtiles(tm, cw):
            cs = slice(c0, c0 + sc)
            acc = jnp.zeros((sr, sc), F32) + b_ref[:, cs]
            for j in range(k):
                acc = acc + w_ref[j:j + 1, cs] * _conv_rows(ext, (sh,), r0 + halo - p + j, sr, cs)
            c1_ref[r0:r0 + sr, cs] = acc
        for r0 in range(0, tm, sr):
            rows = slice(r0, r0 + sr)
            cv = c1_ref[rows, :]
            xc = cv - jnp.mean(cv, axis=-1, keepdims=True)
            rstd = lax.rsqrt(jnp.mean(xc * xc, axis=-1, keepdims=True) + EPS)
            c2 = xc * rstd * lw_ref[...] + lb_ref[...]
            y_ref[rows, :] = (_silu(c2) * _silu(z_ref[rows, :])).astype(y_ref.dtype)

    vec = pl.BlockSpec((1, cw), lambda i: (0, 0))
    row = pl.BlockSpec((tm, cw), lambda i: (i, 0))
    return pl.pallas_call(
        body, name=name,
        out_shape=(jax.ShapeDtypeStruct((t, cw), F32), jax.ShapeDtypeStruct(ycat.shape, ycat.dtype)),
        grid=(t // tm,),
        in_specs=[*_conf_specs(tm, cw, halo, lambda i: i),
                  pl.BlockSpec((tm, cw), lambda i: (i, OFF_ZC // cw)),
                  pl.BlockSpec((kp, cw), lambda i: (0, 0)), vec, vec, vec, ANY],
        out_specs=(row, pl.BlockSpec((tm, cw), lambda i: (i, YCAT_CONF // cw))),
        input_output_aliases={9: 1},
        scratch_shapes=[pltpu.VMEM((halo + tm, cw), F32)] + _conv_shift_scratch(k, halo + tm, cw),
        compiler_params=_params(("parallel",)),
    )(proj, proj, proj, proj, proj, w, bias, ln_w, ln_b, ycat)


def _conf_bwd(dycat, proj, c1, w, ln_w, ln_b, dproj, seq, name):
    t = proj.shape[0]
    k = CONF_KERNEL
    tm, cw, halo = CONV_TILE, CONF_WIDTH, _conv_halo(k)
    sr, sc = CONV_SUB_ROWS, CONV_SUB_COLS
    p = k - 1
    kp = w.shape[0]
    nt = t // tm
    last_halo = t // halo - 1

    def body(dy_ref, dyn_ref, c_ref, cn_ref, z_ref, zn_ref, a_ref, ah_ref, g_ref, gh_ref, w_ref, lw_ref, lb_ref, _,
             grp_ref, dw_ref, db_ref, dlw_ref, dlb_ref, dyext, xext, wacc, bacc, lwacc, lbacc, dysh, xsh):
        i = pl.program_id(0)

        @pl.when(i == 0)
        def _():
            wacc[...] = jnp.zeros_like(wacc)
            bacc[...] = jnp.zeros_like(bacc)
            lwacc[...] = jnp.zeros_like(lwacc)
            lbacc[...] = jnp.zeros_like(lbacc)

        def post_bwd(dy, cv, zv):
            xc = cv - jnp.mean(cv, axis=-1, keepdims=True)
            rstd = lax.rsqrt(jnp.mean(xc * xc, axis=-1, keepdims=True) + EPS)
            xh = xc * rstd
            c2 = xh * lw_ref[...] + lb_ref[...]
            dz = dy * _silu(c2) * _dsilu(zv)
            dc2 = dy * _silu(zv) * _dsilu(c2)
            dxh = dc2 * lw_ref[...]
            dc = rstd * (dxh - jnp.mean(dxh, axis=-1, keepdims=True)
                         - xh * jnp.mean(dxh * xh, axis=-1, keepdims=True))
            return dc, dz, dc2 * xh, dc2

        seq_end = ((i + 1) * tm) % seq == 0
        for r0 in range(0, tm, sr):
            rows = slice(r0, r0 + sr)
            dc, dz, lw_terms, lb_terms = post_bwd(dy_ref[rows, :], c_ref[rows, :], z_ref[rows, :])
            dyext[rows, :] = dc
            grp_ref[rows, 2 * cw:] = dz.astype(grp_ref.dtype)
            lwacc[...] += _rowsum8(lw_terms)
            lbacc[...] += _rowsum8(lb_terms)
        dc_next = post_bwd(dyn_ref[...], cn_ref[...], zn_ref[...])[0]
        dyext[tm:, :] = jnp.where(seq_end, 0.0, dc_next)
        _glu_window(xext, a_ref, ah_ref, g_ref, gh_ref, (i * tm) % seq == 0, halo)
        _conv_fill_shifted(dyext, dysh)
        _conv_fill_shifted(xext, xsh)
        dag_ref = grp_ref
        for r0, c0 in _conv_subtiles(tm, cw):
            cs = slice(c0, c0 + sc)
            rows = slice(r0, r0 + sr)
            dyv = dyext[rows, cs]
            acc = jnp.zeros((sr, sc), F32)
            for j in range(k):
                acc = acc + w_ref[j:j + 1, cs] * _conv_rows(dyext, (dysh,), r0 + p - j, sr, cs)
                wacc[j, :, cs] += _rowsum8(dyv * _conv_rows(xext, (xsh,), r0 + halo - p + j, sr, cs))
            bacc[:, cs] += _rowsum8(dyv)
            s = _sigmoid(g_ref[rows, cs])
            dag_ref[rows, cs] = (acc * s).astype(dag_ref.dtype)
            dag_ref[rows, cw + c0:cw + c0 + sc] = (acc * a_ref[rows, cs] * s * (1.0 - s)).astype(dag_ref.dtype)

        @pl.when(i == nt - 1)
        def _():
            dw_ref[...] = jnp.zeros_like(dw_ref)
            for j in range(k):
                dw_ref[j:j + 1, :] = jnp.sum(wacc[j], axis=0, keepdims=True)
            db_ref[...] = jnp.sum(bacc[...], axis=0, keepdims=True)
            dlw_ref[...] = jnp.sum(lwacc[...], axis=0, keepdims=True)
            dlb_ref[...] = jnp.sum(lbacc[...], axis=0, keepdims=True)

    def blk(col):
        return pl.BlockSpec((tm, cw), lambda i: (i, col))

    def nxt(col):
        return pl.BlockSpec((halo, cw), lambda i: (jnp.minimum((i + 1) * (tm // halo), last_halo), col))

    vec = pl.BlockSpec((1, cw), lambda i: (0, 0))
    return pl.pallas_call(
        body, name=name,
        out_shape=(jax.ShapeDtypeStruct(dproj.shape, dproj.dtype), jax.ShapeDtypeStruct((kp, cw), F32),
                   jax.ShapeDtypeStruct((1, cw), F32), jax.ShapeDtypeStruct((1, cw), F32),
                   jax.ShapeDtypeStruct((1, cw), F32)),
        grid=(nt,),
        in_specs=[blk(YCAT_CONF // cw), nxt(YCAT_CONF // cw), blk(0), nxt(0), blk(OFF_ZC // cw), nxt(OFF_ZC // cw),
                  *_conf_specs(tm, cw, halo, lambda i: i),
                  pl.BlockSpec((kp, cw), lambda i: (0, 0)), vec, vec, ANY],
        out_specs=(pl.BlockSpec((tm, CONF_GROUP), lambda i: (i, OFF_CONF // CONF_GROUP)),
                   pl.BlockSpec((kp, cw), lambda i: (0, 0)), vec, vec, vec),
        input_output_aliases={13: 0},
        scratch_shapes=[pltpu.VMEM((tm + halo, cw), F32), pltpu.VMEM((halo + tm, cw), F32),
                        pltpu.VMEM((kp, SUBLANES, cw), F32), pltpu.VMEM((SUBLANES, cw), F32),
                        pltpu.VMEM((SUBLANES, cw), F32), pltpu.VMEM((SUBLANES, cw), F32)]
        + 2 * _conv_shift_scratch(k, halo + tm, cw),
        compiler_params=_params(("arbitrary",)),
    )(dycat, dycat, c1, c1, proj, proj, proj, proj, proj, proj, w, ln_w, ln_b, dproj)


def _half_mask(half):
    lane = _iota((1, LANES), 1)
    return ((lane >= half * ATTN_HEAD_DIM) & (lane < (half + 1) * ATTN_HEAD_DIM)).astype(F32)


def _stack_heads(xp, g):
    m = _half_mask(g)
    swapped = pltpu.roll(xp, ATTN_HEAD_DIM, axis=1)
    return jnp.concatenate([xp * m, swapped * m] if g == 0 else [swapped * m, xp * m], axis=0)


def _unstack_heads(both, g):
    w = both.shape[0] // 2
    top, bot = both[:w], both[w:]
    lo, hi = _half_mask(0), _half_mask(1)
    if g == 0:
        return top * lo + pltpu.roll(bot, ATTN_HEAD_DIM, axis=1) * hi
    return pltpu.roll(top, ATTN_HEAD_DIM, axis=1) * lo + bot * hi


def _band_mask(first_block):
    w = WINDOW
    qi = _iota((w, 2 * w), 0)
    kj = _iota((w, 2 * w), 1) - w
    rel = qi - kj
    return (rel >= 0) & (rel < w) & (jnp.logical_not(first_block) | (kj >= 0))


def _lane_pick(x, h):
    return jnp.sum(jnp.where(_iota(x.shape, 1) == h, x, 0.0), axis=1, keepdims=True)


def _attn_specs(nb, rev):
    w = WINDOW

    def blk(i):
        return nb - 1 - i if rev else i

    def row(b, i):
        return b * nb + blk(i)

    def prow(b, i):
        return b * nb + jnp.maximum(blk(i) - 1, 0)

    q = pl.BlockSpec((w, 512), lambda b, i: (row(b, i), OFF_Q // 512))
    kc = pl.BlockSpec((w, 128), lambda b, i: (row(b, i), OFF_K // 128))
    kp = pl.BlockSpec((w, 128), lambda b, i: (prow(b, i), OFF_K // 128))
    vc = pl.BlockSpec((w, 128), lambda b, i: (row(b, i), OFF_V // 128))
    vp = pl.BlockSpec((w, 128), lambda b, i: (prow(b, i), OFF_V // 128))
    z = pl.BlockSpec((w, 512), lambda b, i: (row(b, i), OFF_ZA // 512))
    return q, kc, kp, vc, vp, z, row


def _attn_fwd(proj, sinks, ycat, nbatch, name):
    t = proj.shape[0]
    w = WINDOW
    nb = t // nbatch // w
    scale = ATTN_HEAD_DIM ** -0.5
    q_s, kc_s, kp_s, vc_s, vp_s, z_s, row = _attn_specs(nb, False)

    def body(q_ref, kc_ref, kp_ref, vc_ref, vp_ref, z_ref, sk_ref, _, y_ref, o_ref, lse_ref):
        first = pl.program_id(1) == 0
        mask = _band_mask(first)
        kk = jnp.concatenate([kp_ref[...], kc_ref[...]], axis=0).astype(MXU_DTYPE)
        vv = jnp.concatenate([vp_ref[...], vc_ref[...]], axis=0).astype(MXU_DTYPE)
        sk = sk_ref[...]
        lane = _iota((w, LANES), 1)
        mask2 = jnp.concatenate([mask, mask], axis=0)
        scores = [_dot(_stack_heads(q_ref[:, j * LANES:(j + 1) * LANES], j // 2), kk, NT) for j in range(4)]
        lse_all = jnp.zeros((w, LANES), F32)
        for j in range(4):
            s = jnp.where(mask2, scores[j] * scale, -1e30)
            skc = jnp.concatenate([jnp.broadcast_to(_lane_pick(sk, 2 * j), (w, 1)),
                                   jnp.broadcast_to(_lane_pick(sk, 2 * j + 1), (w, 1))], axis=0)
            m = jnp.maximum(jnp.max(s, axis=1, keepdims=True), skc)
            den = jnp.sum(jnp.exp(s - m), axis=1, keepdims=True) + jnp.exp(skc - m)
            lse = m + jnp.log(den)
            lse_all = jnp.where(lane == 2 * j, lse[:w], lse_all)
            lse_all = jnp.where(lane == 2 * j + 1, lse[w:], lse_all)
            op = _unstack_heads(_dot(jnp.exp(s - lse), vv), j // 2)
            cols = slice(j * LANES, (j + 1) * LANES)
            o_ref[:, cols] = op
            y_ref[:, cols] = (op * _silu(z_ref[:, cols])).astype(y_ref.dtype)
        lse_ref[...] = lse_all

    return pl.pallas_call(
        body, name=name,
        out_shape=(jax.ShapeDtypeStruct(ycat.shape, ycat.dtype), jax.ShapeDtypeStruct((t, 512), F32),
                   jax.ShapeDtypeStruct((t, LANES), F32)),
        grid=(nbatch, nb),
        in_specs=[q_s, kc_s, kp_s, vc_s, vp_s, z_s, pl.BlockSpec((1, LANES), lambda b, i: (0, 0)), ANY],
        out_specs=(pl.BlockSpec((w, 512), lambda b, i: (row(b, i), YCAT_ATTN // 512)),
                   pl.BlockSpec((w, 512), lambda b, i: (row(b, i), 0)),
                   pl.BlockSpec((w, LANES), lambda b, i: (row(b, i), 0))),
        input_output_aliases={7: 0},
        compiler_params=_params(("parallel", "parallel")),
    )(proj, proj, proj, proj, proj, proj, sinks, ycat)


def _attn_bwd(dycat, proj, o, lse, sinks, ddt, dproj, nbatch, name):
    t = proj.shape[0]
    w = WINDOW
    nb = t // nbatch // w
    scale = ATTN_HEAD_DIM ** -0.5
    q_s, kc_s, kp_s, vc_s, vp_s, z_s, row = _attn_specs(nb, True)

    def body(dy_ref, q_ref, kc_ref, kp_ref, vc_ref, vp_ref, z_ref, o_ref, lse_ref, sk_ref, ddt_ref, _,
             grp_ref, dsk_ref, kcarry, vcarry, sacc):
        b, i = pl.program_id(0), pl.program_id(1)

        @pl.when((b == 0) & (i == 0))
        def _():
            sacc[...] = jnp.zeros_like(sacc)

        @pl.when(i == 0)
        def _():
            kcarry[...] = jnp.zeros_like(kcarry)
            vcarry[...] = jnp.zeros_like(vcarry)

        first = i == nb - 1
        mask = _band_mask(first)
        kk = jnp.concatenate([kp_ref[...], kc_ref[...]], axis=0).astype(MXU_DTYPE)
        vv = jnp.concatenate([vp_ref[...], vc_ref[...]], axis=0).astype(MXU_DTYPE)
        sk = sk_ref[...]
        lse_all = lse_ref[...]
        lane1 = _iota((1, LANES), 1)
        mask2 = jnp.concatenate([mask, mask], axis=0)
        qs, dos, deltas, lses, scores, dps = [], [], [], [], [], []
        for j in range(4):
            cols = slice(j * LANES, (j + 1) * LANES)
            qp, zp, ov, dy = q_ref[:, cols], z_ref[:, cols], o_ref[:, cols], dy_ref[:, cols]
            grp_ref[:, OFF_ZA + j * LANES:OFF_ZA + (j + 1) * LANES] = (dy * ov * _dsilu(zp)).astype(grp_ref.dtype)
            do = dy * _silu(zp)
            q2 = _stack_heads(qp, j // 2).astype(MXU_DTYPE)
            do2 = _stack_heads(do, j // 2)
            qs.append(q2)
            dos.append(do2.astype(MXU_DTYPE))
            deltas.append(jnp.sum(do2 * _stack_heads(ov, j // 2), axis=1, keepdims=True))
            lses.append(jnp.concatenate([_lane_pick(lse_all, 2 * j), _lane_pick(lse_all, 2 * j + 1)], axis=0))
            scores.append(_dot(q2, kk, NT))
            dps.append(_dot(do2, vv, NT))
        prs, dss = [], []
        dsk = jnp.zeros((1, LANES), F32)
        for j in range(4):
            pr = jnp.exp(jnp.where(mask2, scores[j] * scale, -1e30) - lses[j])
            prs.append(pr.astype(MXU_DTYPE))
            dss.append((pr * (dps[j] - deltas[j])).astype(MXU_DTYPE))
            skc = jnp.concatenate([jnp.broadcast_to(_lane_pick(sk, 2 * j), (w, 1)),
                                   jnp.broadcast_to(_lane_pick(sk, 2 * j + 1), (w, 1))], axis=0)
            sink_term = jnp.exp(skc - lses[j]) * deltas[j]
            dsk = dsk - jnp.where(lane1 == 2 * j, jnp.sum(sink_term[:w]), 0.0)
            dsk = dsk - jnp.where(lane1 == 2 * j + 1, jnp.sum(sink_term[w:]), 0.0)
        dkk = jnp.zeros((2 * w, LANES), F32)
        dvv = jnp.zeros((2 * w, LANES), F32)
        for j in range(4):
            dq = _unstack_heads(_dot(dss[j], kk) * scale, j // 2)
            grp_ref[:, OFF_Q + j * LANES:OFF_Q + (j + 1) * LANES] = dq.astype(grp_ref.dtype)
            dkk = dkk + _dot(dss[j], qs[j], TN) * scale
            dvv = dvv + _dot(prs[j], dos[j], TN)
        grp_ref[:, OFF_K:OFF_K + LANES] = (dkk[w:, :] + kcarry[...]).astype(grp_ref.dtype)
        grp_ref[:, OFF_V:OFF_V + LANES] = (dvv[w:, :] + vcarry[...]).astype(grp_ref.dtype)
        grp_ref[:, OFF_DT:OFF_DT + LANES] = ddt_ref[...].astype(grp_ref.dtype)
        grp_ref[:, OFF_DT + LANES:] = jnp.zeros((w, ATTN_GROUP - OFF_DT - LANES), grp_ref.dtype)
        kcarry[...] = dkk[:w, :]
        vcarry[...] = dvv[:w, :]
        sacc[...] += dsk

        @pl.when((b == nbatch - 1) & (i == nb - 1))
        def _():
            dsk_ref[...] = sacc[...]

    return pl.pallas_call(
        body, name=name,
        out_shape=(jax.ShapeDtypeStruct(dproj.shape, dproj.dtype), jax.ShapeDtypeStruct((1, LANES), F32)),
        grid=(nbatch, nb),
        in_specs=[pl.BlockSpec((w, 512), lambda b, i: (row(b, i), YCAT_ATTN // 512)),
                  q_s, kc_s, kp_s, vc_s, vp_s, z_s,
                  pl.BlockSpec((w, 512), lambda b, i: (row(b, i), 0)),
                  pl.BlockSpec((w, LANES), lambda b, i: (row(b, i), 0)),
                  pl.BlockSpec((1, LANES), lambda b, i: (0, 0)),
                  pl.BlockSpec((w, LANES), lambda b, i: (row(b, i), 0)), ANY],
        out_specs=(pl.BlockSpec((w, ATTN_GROUP), lambda b, i: (row(b, i), 0)),
                   pl.BlockSpec((1, LANES), lambda b, i: (0, 0))),
        input_output_aliases={11: 0},
        scratch_shapes=[pltpu.VMEM((w, LANES), F32), pltpu.VMEM((w, LANES), F32),
                        pltpu.VMEM((1, LANES), F32)],
        compiler_params=_params(("arbitrary", "arbitrary")),
    )(dycat, proj, proj, proj, proj, proj, proj, o, lse, sinks, ddt, dproj)


SSD_WIDTH = SSD_HEADS * SSD_HEAD_DIM
GROUP_ROWS = SSD_WIDTH // 2


def _expand_mat():
    r, c = _iota((LANES, SSD_WIDTH), 0), _iota((LANES, SSD_WIDTH), 1)
    return (r == lax.shift_right_logical(c, 6)).astype(BF16)


def _expand_mat_t():
    r, c = _iota((SSD_WIDTH, LANES), 0), _iota((SSD_WIDTH, LANES), 1)
    return (c == lax.shift_right_logical(r, 6)).astype(BF16)


def _ssd_common(u_ref, dt_ref, dtb_ref, a_ref):
    q = CHUNK
    act = _silu(u_ref[...])
    xs = act[:, :SSD_WIDTH]
    bm = act[:, SSD_WIDTH:SSD_WIDTH + 256]
    cm = act[:, SSD_WIDTH + 256:]
    dtp = _softplus(dt_ref[...] + dtb_ref[...])
    a = dtp * a_ref[...]
    tril = (_iota((q, q), 0) >= _iota((q, q), 1)).astype(BF16)
    acs = _xdot_r(tril, a)
    acs_t = acs.T
    e = _expand_mat()
    dt_x = _xdot(dtp, e)
    ea = jnp.exp(_xdot(acs, e))
    a_end = jnp.sum(jnp.where(_iota(acs.shape, 0) == q - 1, acs, 0.0), axis=0, keepdims=True)
    dec = jnp.exp(_xdot(a_end - acs, e))
    a_end_col = jnp.broadcast_to(_lane_pick(acs_t, q - 1), (LANES, LANES))
    s_scale = jnp.exp(_xdot_r(_expand_mat_t(), a_end_col))
    return act, xs, bm, cm, dtp, acs, acs_t, dt_x, ea, dec, s_scale, tril


def _decay_mat(acs, acs_t, h):
    q = CHUNK
    col = _lane_pick(acs, h)
    rowv = jnp.sum(jnp.where(_iota(acs_t.shape, 0) == h, acs_t, 0.0), axis=0, keepdims=True)
    causal = _iota((q, q), 0) >= _iota((q, q), 1)
    return jnp.exp(jnp.where(causal, col - rowv, -1e30))


GN_WIDTH = 512


def _ssd_fwd(u, proj, dtb, a_neg, d_x, norm_w, ycat, nbatch, name):
    t = u.shape[0]
    q = CHUNK
    nc = t // nbatch // q

    def body(u_ref, dt_ref, z_ref, dtb_ref, a_ref, dx_ref, nw_ref, _, y_ref, st_ref, yn_ref, state):
        c = pl.program_id(1)

        @pl.when(c == 0)
        def _():
            state[...] = jnp.zeros_like(state)

        st_ref[...] = state[...]
        act, xs, bm, cm, dtp, acs, acs_t, dt_x, ea, dec, s_scale, _ = _ssd_common(u_ref, dt_ref, dtb_ref, a_ref)
        xdt = xs * dt_x
        xdec = xdt * dec
        lo, hi = _half_mask(0), _half_mask(1)
        grp = []
        for g in range(2):
            bg = bm[:, g * LANES:(g + 1) * LANES]
            cg = cm[:, g * LANES:(g + 1) * LANES]
            rows = slice(g * GROUP_ROWS, (g + 1) * GROUP_ROWS)
            sg = state[rows, :]
            grp.append((_dot(cg, bg, NT), _dot(cg, sg, NT), rows,
                        s_scale[rows, :] * sg + _dot(xdec[:, rows], bg, TN)))
        for g in range(2):
            cb, yoff, rows, state_new = grp[g]
            for j in range(4):
                pj = g * 4 + j
                cols = slice(pj * LANES, (pj + 1) * LANES)
                xp = xdt[:, cols]
                m2 = jnp.concatenate([cb * _decay_mat(acs, acs_t, 2 * pj), cb * _decay_mat(acs, acs_t, 2 * pj + 1)],
                                     axis=1)
                yp = _dot(m2, jnp.concatenate([xp * lo, xp * hi], axis=0))
                yp = yp + yoff[:, j * LANES:(j + 1) * LANES] * ea[:, cols]
                y_ref[:, cols] = yp + dx_ref[:, cols] * xs[:, cols]
            state[rows, :] = state_new
        for g in range(SSD_WIDTH // GN_WIDTH):
            cols = slice(g * GN_WIDTH, (g + 1) * GN_WIDTH)
            gg = y_ref[:, cols] * _silu(z_ref[:, cols])
            rstd = lax.rsqrt(jnp.mean(gg * gg, axis=-1, keepdims=True) + EPS)
            yn_ref[:, cols] = (gg * rstd * nw_ref[:, cols]).astype(yn_ref.dtype)

    vec = pl.BlockSpec((1, LANES), lambda b, c: (0, 0))
    wide = pl.BlockSpec((q, SSD_WIDTH), lambda b, c: (b * nc + c, 0))
    wvec = pl.BlockSpec((1, SSD_WIDTH), lambda b, c: (0, 0))
    return pl.pallas_call(
        body, name=name,
        out_shape=(jax.ShapeDtypeStruct((t, SSD_WIDTH), F32),
                   jax.ShapeDtypeStruct((nbatch * nc * SSD_WIDTH, SSD_STATE), F32),
                   jax.ShapeDtypeStruct(ycat.shape, ycat.dtype)),
        grid=(nbatch, nc),
        in_specs=[pl.BlockSpec((q, SSD_CONV_DIM), lambda b, c: (b * nc + c, 0)),
                  pl.BlockSpec((q, LANES), lambda b, c: (b * nc + c, OFF_DT // LANES)),
                  pl.BlockSpec((q, SSD_WIDTH), lambda b, c: (b * nc + c, OFF_ZS // SSD_WIDTH)),
                  vec, vec, wvec, wvec, ANY],
        out_specs=(wide, pl.BlockSpec((SSD_WIDTH, SSD_STATE), lambda b, c: (b * nc + c, 0)), wide),
        input_output_aliases={7: 2},
        scratch_shapes=[pltpu.VMEM((SSD_WIDTH, SSD_STATE), F32)],
        compiler_params=_params(("parallel", "arbitrary")),
    )(u, proj, proj, dtb, a_neg, d_x, norm_w, ycat)


def _ssd_bwd(dycat, u, proj, y, states, dtb, a_neg, d_x, norm_w, dproj, nbatch, name):
    t = u.shape[0]
    q = CHUNK
    nc = t // nbatch // q

    def body(do_ref, u_ref, dt_ref, z_ref, y_ref, st_ref, dtb_ref, a_ref, dx_ref, nw_ref, _,
             du_ref, dz_ref, ddt_ref, dal_ref, dd_ref, dtbg_ref, dnw_ref, dstate, acc_a, acc_d, acc_b, acc_w):
        b, c = pl.program_id(0), pl.program_id(1)

        @pl.when((b == 0) & (c == 0))
        def _():
            acc_a[...] = jnp.zeros_like(acc_a)
            acc_d[...] = jnp.zeros_like(acc_d)
            acc_b[...] = jnp.zeros_like(acc_b)
            acc_w[...] = jnp.zeros_like(acc_w)

        @pl.when(c == 0)
        def _():
            dstate[...] = jnp.zeros_like(dstate)

        dy_parts = []
        for g in range(SSD_WIDTH // GN_WIDTH):
            cols = slice(g * GN_WIDTH, (g + 1) * GN_WIDTH)
            yv, zv, dov = y_ref[:, cols], z_ref[:, cols], do_ref[:, cols]
            sz = _silu(zv)
            gg = yv * sz
            rstd = lax.rsqrt(jnp.mean(gg * gg, axis=-1, keepdims=True) + EPS)
            gh = gg * rstd
            acc_w[:, cols] += _rowsum8(dov * gh)
            dgn = dov * nw_ref[:, cols]
            dg = rstd * (dgn - gh * jnp.mean(dgn * gh, axis=-1, keepdims=True))
            dy_parts.append(dg * sz)
            dz_ref[:, cols] = (dg * yv * _dsilu(zv)).astype(dz_ref.dtype)

        act, xs, bm, cm, dtp, acs, acs_t, dt_x, ea, dec, s_scale, tril = _ssd_common(
            u_ref, dt_ref, dtb_ref, a_ref)
        xdt = xs * dt_x
        xdec = xdt * dec
        dyv = jnp.concatenate(dy_parts, axis=1)
        dye = dyv * ea
        lo, hi = _half_mask(0), _half_mask(1)
        et = _expand_mat_t()
        grp = []
        for g in range(2):
            rows = slice(g * GROUP_ROWS, (g + 1) * GROUP_ROWS)
            bg = bm[:, g * LANES:(g + 1) * LANES]
            cg = cm[:, g * LANES:(g + 1) * LANES]
            sg = st_ref[rows, :]
            dsg = dstate[rows, :]
            grp.append(dict(
                rows=rows, bg=bg, cg=cg, dsg=dsg,
                cb=_dot(cg, bg, NT), yoff=_dot(cg, sg, NT), dxst=_dot(bg, dsg, NT) * dec[:, rows],
                dc_off=_dot(dye[:, rows], sg), db_off=_dot(xdec[:, rows], dsg),
                s_next=s_scale[rows, :] * sg + _dot(xdec[:, rows], bg, TN),
                dstate_new=_dot(dye[:, rows], cg, TN) + s_scale[rows, :] * dsg))
        dy2s, g2s, l2s = [], [], []
        for pj in range(SSD_HEADS // 2):
            cols = slice(pj * LANES, (pj + 1) * LANES)
            dyp = dyv[:, cols]
            dy2 = jnp.concatenate([dyp * lo, dyp * hi], axis=0).astype(MXU_DTYPE)
            dy2s.append(dy2)
            g2s.append(_dot(dy2, xdt[:, cols], NT))
            l2s.append(jnp.concatenate([_decay_mat(acs, acs_t, 2 * pj), _decay_mat(acs, acs_t, 2 * pj + 1)], axis=0))
        dal_diag = jnp.zeros((q, LANES), F32)
        lane2 = _iota((2 * q, LANES), 1)
        row2 = _iota((2 * q, LANES), 0)
        dxdt_parts, db_parts, dc_parts = [], [], []
        end_sum = jnp.zeros((LANES, LANES), F32)
        for g in range(2):
            gd = grp[g]
            cb2 = jnp.concatenate([gd["cb"], gd["cb"]], axis=0)
            dcb = jnp.zeros((q, q), F32)
            parts = []
            for j in range(4):
                pj = g * 4 + j
                gl = g2s[pj] * l2s[pj]
                dcb = dcb + gl[:q] + gl[q:]
                m2 = cb2 * l2s[pj]
                parts.append(_dot(m2, dy2s[pj], TN))
                w2 = (gl * cb2).astype(MXU_DTYPE)
                sel2 = (lane2 == 2 * pj + (row2 >= q).astype(jnp.int32)).astype(MXU_DTYPE)
                dal_diag = dal_diag + _dot(jnp.concatenate([w2[:q], w2[q:]], axis=1), sel2) - _dot(w2, sel2, TN)
            dxdt_parts.append(jnp.concatenate(parts, axis=1) + gd["dxst"])
            dc_parts.append(_dot(dcb, gd["bg"]) + gd["dc_off"])
            db_parts.append(_dot(dcb, gd["cg"], TN) + gd["db_off"])
            end_sum = end_sum + _xdot(gd["dsg"] * gd["s_next"], et[gd["rows"], :], TN, passes=2)
            dstate[gd["rows"], :] = gd["dstate_new"]
        dxst_parts = [gd["dxst"] for gd in grp]
        yoff_parts = [gd["yoff"] for gd in grp]
        dxdt = jnp.concatenate(dxdt_parts, axis=1)
        dxv = dx_ref[...]
        yoff = jnp.concatenate(yoff_parts, axis=1) * ea
        dalpha = dal_diag + _xdot(dyv * yoff - xdt * jnp.concatenate(dxst_parts, axis=1), et)
        end_row = jnp.sum(end_sum, axis=0, keepdims=True)
        dalpha = dalpha + jnp.where(_iota((q, LANES), 0) == q - 1, end_row, 0.0)
        da = _xdot_r(tril, dalpha, TN)
        ddtp = da * a_ref[...] + _xdot(dxdt * xs, et)
        acc_a[...] += _rowsum8(da * dtp)
        acc_d[...] += _rowsum8(_xdot(dyv * xs, et))
        ddt_raw = ddtp * _sigmoid(dt_ref[...] + dtb_ref[...])
        acc_b[...] += _rowsum8(ddt_raw)
        ddt_ref[...] = ddt_raw
        dxs = dxdt * dt_x + dxv * dyv
        dact = jnp.concatenate([dxs] + db_parts + dc_parts, axis=1)
        du_ref[...] = dact * _dsilu(u_ref[...])

        @pl.when((b == nbatch - 1) & (c == nc - 1))
        def _():
            dal_ref[...] = jnp.sum(acc_a[...], axis=0, keepdims=True) * a_ref[...]
            dd_ref[...] = jnp.sum(acc_d[...], axis=0, keepdims=True)
            dtbg_ref[...] = jnp.sum(acc_b[...], axis=0, keepdims=True)
            dnw_ref[...] = jnp.sum(acc_w[...], axis=0, keepdims=True)

    def rowblk(b, c):
        return b * nc + (nc - 1 - c)

    vec = pl.BlockSpec((1, LANES), lambda b, c: (0, 0))
    wvec = pl.BlockSpec((1, SSD_WIDTH), lambda b, c: (0, 0))
    wide = pl.BlockSpec((q, SSD_WIDTH), lambda b, c: (rowblk(b, c), 0))
    zblk = pl.BlockSpec((q, SSD_WIDTH), lambda b, c: (rowblk(b, c), OFF_ZS // SSD_WIDTH))
    return pl.pallas_call(
        body, name=name,
        out_shape=(jax.ShapeDtypeStruct((t, SSD_CONV_DIM), F32), jax.ShapeDtypeStruct(dproj.shape, dproj.dtype),
                   jax.ShapeDtypeStruct((t, LANES), F32),
                   jax.ShapeDtypeStruct((1, LANES), F32), jax.ShapeDtypeStruct((1, LANES), F32),
                   jax.ShapeDtypeStruct((1, LANES), F32), jax.ShapeDtypeStruct((1, SSD_WIDTH), F32)),
        grid=(nbatch, nc),
        in_specs=[wide,
                  pl.BlockSpec((q, SSD_CONV_DIM), lambda b, c: (rowblk(b, c), 0)),
                  pl.BlockSpec((q, LANES), lambda b, c: (rowblk(b, c), OFF_DT // LANES)),
                  zblk, wide,
                  pl.BlockSpec((SSD_WIDTH, SSD_STATE), lambda b, c: (rowblk(b, c), 0)),
                  vec, vec, wvec, wvec, ANY],
        out_specs=(pl.BlockSpec((q, SSD_CONV_DIM), lambda b, c: (rowblk(b, c), 0)),
                   zblk,
                   pl.BlockSpec((q, LANES), lambda b, c: (rowblk(b, c), 0)),
                   vec, vec, vec, wvec),
        input_output_aliases={10: 1},
        scratch_shapes=[pltpu.VMEM((SSD_WIDTH, SSD_STATE), F32), pltpu.VMEM((SUBLANES, LANES), F32),
                        pltpu.VMEM((SUBLANES, LANES), F32), pltpu.VMEM((SUBLANES, LANES), F32),
                        pltpu.VMEM((SUBLANES, SSD_WIDTH), F32)],
        compiler_params=_params(("arbitrary", "arbitrary")),
    )(dycat, u, proj, proj, y, states, dtb, a_neg, d_x, norm_w, dproj)


def _pad_rows(w, rows):
    return jnp.concatenate([w, jnp.zeros((rows - w.shape[0], w.shape[1]), w.dtype)], axis=0)


def _pad_lanes(v):
    return jnp.concatenate([v, jnp.zeros((LANES - v.shape[0],), v.dtype)]).reshape(1, LANES)


def _padded_from_chips(pieces):
    cols = pieces[0].shape[-1]
    lead = pieces[0].shape[:-1]
    parts, pos = [], 0
    for lo, hi, start in sorted(SECTIONS, key=lambda s: s[2]):
        if start > pos:
            parts.append(jnp.zeros(lead + (start - pos,), pieces[0].dtype))
        pos = start + hi - lo
        while lo < hi:
            p = lo // cols
            end = min(hi, (p + 1) * cols)
            parts.append(pieces[p][..., lo - p * cols:end - p * cols])
            lo = end
    if pos < NP:
        parts.append(jnp.zeros(lead + (NP - pos,), pieces[0].dtype))
    return jnp.concatenate(parts, axis=-1)


def _chip_part_from_padded(wp, p, cols):
    lo, hi = p * cols, (p + 1) * cols
    parts = []
    for rs, re, start in SECTIONS:
        a, b = max(lo, rs), min(hi, re)
        if a < b:
            parts.append(wp[..., start + a - rs:start + b - rs])
    return jnp.concatenate(parts, axis=-1)


def _layer_params(li, w_in_p, w_out, conv_w, dw_w, small):
    return dict(
        w_in_p=w_in_p, w_out=w_out,
        conv_w=_pad_rows(conv_w, SUBLANES), dw_w=_pad_rows(dw_w, 32),
        norm_w=small["norm_w"][li].reshape(1, -1),
        conv_b=small["ssd_conv_b"][li].reshape(1, -1),
        dtb=_pad_lanes(small["ssd_dt_bias"][li]),
        a_neg=_pad_lanes(-jnp.exp(small["ssd_a_log"][li])),
        d_x=jnp.repeat(small["ssd_d"][li], SSD_HEAD_DIM).reshape(1, -1),
        ssd_norm_w=small["ssd_norm_w"][li].reshape(1, -1),
        sinks=_pad_lanes(small["attn_sinks"][li]),
        dw_b=small["conf_dw_b"][li].reshape(1, -1),
        ln_w=small["conf_ln_w"][li].reshape(1, -1),
        ln_b=small["conf_ln_b"][li].reshape(1, -1),
    )


def _layer_fwd(x, p, nbatch, seq, tag, after=None):
    proj, h_t = _proj_fwd(x, p["norm_w"], p["w_in_p"], name=f"proj_fwd_{tag}", after=after)
    u = _conv_fwd(proj, OFF_XBC, SSD_CONV_DIM, p["conv_w"], p["conv_b"], SSD_CONV, seq, name=f"ssd_conv_fwd_{tag}")
    ycat = lax.empty((x.shape[0], MIX_WIDTH), MXU_DTYPE)
    y, states, ycat = _ssd_fwd(u, proj, p["dtb"], p["a_neg"], p["d_x"], p["ssd_norm_w"], ycat, nbatch,
                               name=f"ssd_fwd_{tag}")
    ycat, o, lse = _attn_fwd(proj, p["sinks"], ycat, nbatch, name=f"attn_fwd_{tag}")
    c1, ycat = _conf_fwd(proj, p["dw_w"], p["dw_b"], p["ln_w"], p["ln_b"], ycat, seq, name=f"conf_fwd_{tag}")
    w_out = p["w_out"](ycat) if callable(p["w_out"]) else p["w_out"]
    x_new = _matmul(ycat, w_out, "nn", F32, 1024, 512, 2048, name=f"out_fwd_{tag}", residual=x)
    return x_new, dict(x=x, w_out=w_out, h_t=h_t, proj=proj, u=u, y=y, states=states, o=o, lse=lse, c1=c1, ycat=ycat)


def _layer_bwd(dx_out, p, s, nbatch, seq, tag, hooks=None):
    hooks = hooks or {}
    proj = s["proj"]
    dycat = _matmul(dx_out, s["w_out"], "nt", F32, 1024, 1024, 1024, name=f"out_bwd_dy_{tag}",
                    after=hooks.get("start_token"))
    dw_out = _matmul(s["ycat"], dx_out, "tn", F32, 1024, 1024, 1024, name=f"out_bwd_dw_{tag}")
    token = hooks["after_dycat"](dycat) if "after_dycat" in hooks else None
    dtb = p["dtb"] if token is None else p["dtb"] + token[0, 0]
    dproj = lax.empty(proj.shape, MXU_DTYPE)
    du, dproj, ddt, da_log, dd, ddtb, dssd_norm_w = _ssd_bwd(
        dycat, s["u"], proj, s["y"], s["states"], dtb, p["a_neg"], p["d_x"], p["ssd_norm_w"], dproj,
        nbatch, name=f"ssd_bwd_{tag}")
    dproj, dconv_w, dconv_b = _conv_bwd(du, proj, OFF_XBC, SSD_CONV_DIM, p["conv_w"], SSD_CONV, seq,
                                        name=f"ssd_conv_bwd_{tag}", into=dproj)
    dproj, dsinks = _attn_bwd(dycat, proj, s["o"], s["lse"], p["sinks"], ddt, dproj, nbatch,
                              name=f"attn_bwd_{tag}")
    if "after_attn" in hooks:
        hooks["after_attn"](dproj)
    dproj, ddw_w, ddw_b, dln_w, dln_b = _conf_bwd(dycat, proj, s["c1"], p["dw_w"], p["ln_w"], p["ln_b"], dproj, seq,
                                                  name=f"conf_bwd_{tag}")
    dw_in_p = _matmul(s["h_t"], dproj, "nn", F32, 1024, 512, 4096, name=f"proj_bwd_dw_{tag}")
    token = hooks["after_dw"](dw_in_p, dw_out) if "after_dw" in hooks else None
    norm_w = p["norm_w"] if token is None else p["norm_w"] + token[0, 0]
    dx_in, dnorm_w = _proj_bwd_dx(dproj, p["w_in_p"], s["x"], norm_w, dx_out, name=f"proj_bwd_dx_{tag}")
    grads = dict(
        norm_w=dnorm_w[0], w_in_p=dw_in_p, ssd_conv_w=dconv_w[:SSD_CONV], ssd_conv_b=dconv_b[0],
        ssd_dt_bias=ddtb[0, :SSD_HEADS], ssd_a_log=da_log[0, :SSD_HEADS], ssd_d=dd[0, :SSD_HEADS],
        ssd_norm_w=dssd_norm_w[0], attn_sinks=dsinks[0, :ATTN_Q_HEADS], conf_dw_w=ddw_w[:CONF_KERNEL],
        conf_dw_b=ddw_b[0], conf_ln_w=dln_w[0], conf_ln_b=dln_b[0], w_out=dw_out)
    return dx_in, grads


def _local_step(x, target, param_fns, final_norm_w, first_after=None, bwd_hooks=None):
    nbatch, seq, d = x.shape
    xt = x.reshape(nbatch * seq, d)
    saved, layer_params = [], []
    for li, fn in enumerate(param_fns):
        p = fn(xt)
        layer_params.append(p)
        xt, s = _layer_fwd(xt, p, nbatch, seq, f"l{li}", after=first_after if li == 0 else None)
        saved.append(s)
    loss, dx, dfinal = _loss_head(xt, target.reshape(nbatch * seq, d), final_norm_w.reshape(1, d), name="loss_head")
    grads = [None] * len(layer_params)
    for li in reversed(range(len(layer_params))):
        hooks = bwd_hooks(li) if bwd_hooks is not None else None
        dx, grads[li] = _layer_bwd(dx, layer_params[li], saved[li], nbatch, seq, f"l{li}", hooks=hooks)
    return loss[0, 0], dx.reshape(nbatch, seq, d), grads, dfinal[0]


MESH = pl.DeviceIdType.MESH
N_CHIPS = 4


def _mesh_pos():
    return lax.axis_index("x"), lax.axis_index("y"), lax.axis_index("c")


def _other_chips(x, y):
    return [(1 - x, y), (x, 1 - y), (1 - x, 1 - y)]


def _gather_weights(big, small, name):
    nbig, nsmall = len(big), len(small)
    n_ici = 3 * (nbig + nsmall)
    n_fwd = 3 * nbig

    def body(*refs):
        ins = refs[:nbig + nsmall]
        outs = refs[nbig + nsmall:2 * (nbig + nsmall)]
        send_sems, recv_sems = refs[2 * (nbig + nsmall):]
        x, y, c = _mesh_pos()
        me = 2 * x + y
        sibling = (x, y, 1 - c)
        chips = _other_chips(x, y)

        def ici(a, j, origin, dest):
            if a < nbig:
                src = ins[a].at[c] if origin is None else outs[a].at[origin, c]
                dst = outs[a].at[me if origin is None else origin, c]
            else:
                src = ins[a] if origin is None else outs[a].at[origin]
                dst = outs[a].at[me if origin is None else origin]
            k = a * 3 + j
            return pltpu.make_async_remote_copy(src_ref=src, dst_ref=dst, send_sem=send_sems.at[k],
                                                recv_sem=recv_sems.at[k], device_id=dest, device_id_type=MESH)

        def fwd(a, j, origin, half):
            k = n_ici + a * 3 + j
            ref = outs[a].at[origin, half]
            return pltpu.make_async_remote_copy(src_ref=ref, dst_ref=ref, send_sem=send_sems.at[k],
                                                recv_sem=recv_sems.at[k], device_id=sibling, device_id_type=MESH)

        sends = []
        for j, (px, py) in enumerate(chips):
            for a in range(nbig + nsmall):
                cp = ici(a, j, None, (px, py, c))
                cp.start()
                sends.append(cp)
        for j, (px, py) in enumerate(chips):
            origin = 2 * px + py
            for a in range(nbig):
                ici(a, j, origin, (px, py, c)).wait_recv()
                cp = fwd(a, j, origin, c)
                cp.start()
                sends.append(cp)
        for j, (px, py) in enumerate(chips):
            origin = 2 * px + py
            for a in range(nbig, nbig + nsmall):
                ici(a, j, origin, (px, py, c)).wait_recv()
            for a in range(nbig):
                fwd(a, j, origin, 1 - c).wait_recv()
        for cp in sends:
            cp.wait_send()

    out_shape = tuple(jax.ShapeDtypeStruct((N_CHIPS,) + a.shape, a.dtype) for a in list(big) + list(small))
    return pl.pallas_call(
        body, name=name, out_shape=out_shape,
        in_specs=[ANY] * (nbig + nsmall), out_specs=tuple([ANY] * (nbig + nsmall)),
        scratch_shapes=[pltpu.SemaphoreType.DMA((n_ici + n_fwd,)), pltpu.SemaphoreType.DMA((n_ici + n_fwd,))],
    )(*big, *small)


HBM = pl.BlockSpec(memory_space=pltpu.HBM)
SEM = pl.BlockSpec(memory_space=pltpu.SEMAPHORE)
DATAFLOW = pltpu.SideEffectType.DATAFLOW_SIDE_EFFECTING


def _split_peers(pattern, x, y, c):
    if pattern == "swap":
        return [((x, y, 1 - c), 1 - c, None, None)]
    me = 2 * x + y
    return [((px, py, c), 2 * px + py if pattern == "scatter" else None, me, 2 * px + py)
            for px, py in _other_chips(x, y)]


def _split_land_shape(pattern, shape):
    return {"bcast": (N_CHIPS,) + shape, "scatter": shape, "swap": shape[:1] + shape[2:]}[pattern]


def _split_copies(pattern, srcs, lands, send_sems, recv_sems, waiting):
    x, y, c = _mesh_pos()
    peers = _split_peers(pattern, x, y, c)
    cps = []
    for j, (dev, src_slot, dst_slot, my_slot) in enumerate(peers):
        for a in range(len(srcs)):
            if src_slot is None:
                src = srcs[a]
            else:
                src = srcs[a].at[:, src_slot] if pattern == "swap" else srcs[a].at[src_slot]
            slot = my_slot if waiting else dst_slot
            dst = lands[a] if slot is None else lands[a].at[slot]
            k = a * len(peers) + j
            cps.append(pltpu.make_async_remote_copy(src_ref=src, dst_ref=dst, send_sem=send_sems[k],
                                                    recv_sem=recv_sems[k], device_id=dev, device_id_type=MESH))
    return cps


def _split_start(arrs, pattern, after, name):
    n = len(arrs)
    nsem = n * (1 if pattern == "swap" else N_CHIPS - 1)
    deps = [] if after is None else [after]

    def body(*refs):
        srcs, lands = refs[:n], refs[n:2 * n]
        outs = refs[2 * n + len(deps):]
        for cp in _split_copies(pattern, srcs, lands, outs[:nsem], outs[nsem:2 * nsem], waiting=False):
            cp.start()
        outs[-1][...] = jnp.zeros_like(outs[-1])

    lands = [lax.empty(_split_land_shape(pattern, a.shape), a.dtype) for a in arrs]
    out_shape = ([pltpu.SemaphoreType.DMA(())] * (2 * nsem)
                 + [pltpu.HBM(a.shape, a.dtype) for a in arrs] + [pltpu.HBM(b.shape, b.dtype) for b in lands]
                 + [jax.ShapeDtypeStruct((SUBLANES, LANES), F32)])
    outs = pl.pallas_call(
        body, name=name, out_shape=tuple(out_shape),
        in_specs=[HBM] * (2 * n) + [ANY] * len(deps),
        out_specs=tuple([SEM] * (2 * nsem) + [HBM] * (2 * n) + [pl.BlockSpec(memory_space=pltpu.VMEM)]),
        input_output_aliases={a: 2 * nsem + a for a in range(2 * n)},
        compiler_params=pltpu.CompilerParams(has_side_effects=DATAFLOW),
    )(*[pltpu.with_memory_space_constraint(a, pltpu.HBM) for a in list(arrs) + lands], *deps)
    return outs[:-1], outs[-1]


def _split_wait(state, n, pattern, after, name):
    nsem = n * (1 if pattern == "swap" else N_CHIPS - 1)

    def body(*refs):
        srcs, lands = refs[:n], refs[n:2 * n]
        send_sems, recv_sems = refs[2 * n:2 * n + nsem], refs[2 * n + nsem:2 * n + 2 * nsem]
        for cp in _split_copies(pattern, srcs, lands, send_sems, recv_sems, waiting=True):
            cp.wait_send()
            cp.wait_recv()

    sems, thru = state[:2 * nsem], state[2 * nsem:]
    outs = pl.pallas_call(
        body, name=name, out_shape=tuple(pltpu.HBM(a.shape, a.dtype) for a in thru),
        in_specs=[HBM] * (2 * n) + [SEM] * (2 * nsem) + [ANY],
        out_specs=tuple([HBM] * (2 * n)),
        input_output_aliases={a: a for a in range(2 * n)},
        compiler_params=pltpu.CompilerParams(has_side_effects=DATAFLOW),
    )(*thru, *sems, after)
    return outs[:n], outs[n:]


def _pair_gather(arrs, layer, name):
    n = len(arrs)

    def body(*refs):
        outs = refs[n:2 * n]
        send_sems, recv_sems = refs[2 * n:]
        x, y, c = _mesh_pos()
        cps = [pltpu.make_async_remote_copy(src_ref=outs[a].at[layer, c], dst_ref=outs[a].at[layer, c],
                                            send_sem=send_sems.at[a], recv_sem=recv_sems.at[a],
                                            device_id=(x, y, 1 - c), device_id_type=MESH)
               for a in range(n)]
        for cp in cps:
            cp.start()
        for cp in cps:
            cp.wait()

    return pl.pallas_call(
        body, name=name, out_shape=tuple(jax.ShapeDtypeStruct(a.shape, a.dtype) for a in arrs),
        in_specs=[ANY] * n, out_specs=tuple([ANY] * n),
        input_output_aliases={a: a for a in range(n)},
        scratch_shapes=[pltpu.SemaphoreType.DMA((n,)), pltpu.SemaphoreType.DMA((n,))],
    )(*arrs)


N_DEV = 8


def _allreduce_small(pack, name):
    r = pack.shape[0]

    def body(p_ref, o_ref, land, send_sems, recv_sems):
        x, y, c = _mesh_pos()
        me = 4 * x + 2 * y + c
        cps = []
        for k in range(1, N_DEV):
            peer = (x ^ (k >> 2), y ^ ((k >> 1) & 1), c ^ (k & 1))
            cps.append(pltpu.make_async_remote_copy(src_ref=p_ref, dst_ref=land.at[me], send_sem=send_sems.at[k - 1],
                                                    recv_sem=recv_sems.at[k - 1], device_id=peer, device_id_type=MESH))
        for cp in cps:
            cp.start()
        land[me] = p_ref[...]
        for cp in cps:
            cp.wait()
        total = land[0]
        for d in range(1, N_DEV):
            total = total + land[d]
        o_ref[...] = total

    vm = pl.BlockSpec(memory_space=pltpu.VMEM)
    return pl.pallas_call(
        body, name=name, out_shape=jax.ShapeDtypeStruct(pack.shape, F32),
        in_specs=[vm], out_specs=vm,
        scratch_shapes=[pltpu.VMEM((N_DEV, r, LANES), F32), pltpu.SemaphoreType.DMA((N_DEV - 1,)),
                        pltpu.SemaphoreType.DMA((N_DEV - 1,))],
    )(pack)


BIG_ROWS = 128


def _cast_layer(w, layer, name):
    _, r, cdim = w.shape
    tr = BIG_ROWS

    def body(w_ref, o_ref):
        o_ref[...] = w_ref[...].astype(o_ref.dtype)

    return pl.pallas_call(
        body, name=name, out_shape=jax.ShapeDtypeStruct((r, cdim), MXU_DTYPE),
        grid=(r // tr,), in_specs=[pl.BlockSpec((None, tr, cdim), lambda i: (layer, i, 0))],
        out_specs=pl.BlockSpec((tr, cdim), lambda i: (i, 0)),
        compiler_params=_params(("parallel",)),
    )(w)


def _cast_cols_major(w_t, name):
    cdim, nl, r = w_t.shape
    tc = LANES

    def body(w_ref, *o_refs):
        for l in range(nl):
            o_refs[l][...] = w_ref[:, l, :].T.astype(o_refs[l].dtype)

    out = pl.BlockSpec((r, tc), lambda i: (0, i))
    return pl.pallas_call(
        body, name=name, out_shape=tuple(jax.ShapeDtypeStruct((r, cdim), MXU_DTYPE) for _ in range(nl)),
        grid=(pl.cdiv(cdim, tc),), in_specs=[pl.BlockSpec((tc, nl, r), lambda i: (i, 0, 0))],
        out_specs=tuple([out] * nl),
        compiler_params=_params(("parallel",)),
    )(w_t)


def _pair_sum(parts, sib, which, out_dtype, name):
    k, _, r, cdim = parts.shape
    tr = BIG_ROWS

    def body(sel_ref, p_ref, s_ref, o_ref):
        o_ref[...] = (p_ref[...] + s_ref[...]).astype(o_ref.dtype)

    grid_spec = pltpu.PrefetchScalarGridSpec(
        num_scalar_prefetch=1, grid=(k, r // tr),
        in_specs=[pl.BlockSpec((None, None, tr, cdim), lambda l, i, sel: (l, sel[0], i, 0)),
                  pl.BlockSpec((None, tr, cdim), lambda l, i, sel: (l, i, 0))],
        out_specs=pl.BlockSpec((None, tr, cdim), lambda l, i, sel: (l, i, 0)))
    return pl.pallas_call(
        body, name=name, out_shape=jax.ShapeDtypeStruct((k, r, cdim), out_dtype), grid_spec=grid_spec,
        compiler_params=_params(("parallel", "parallel")),
    )(which.reshape(1).astype(jnp.int32), parts, sib)


def _sum_lead(parts, into, layer, which, name):
    k, r, cdim = parts.shape
    tr = BIG_ROWS

    def body(sel_ref, p_ref, _, o_ref):
        total = p_ref[0].astype(F32)
        for a in range(1, k):
            total = total + p_ref[a].astype(F32)
        o_ref[...] = total

    grid_spec = pltpu.PrefetchScalarGridSpec(
        num_scalar_prefetch=1, grid=(r // tr,),
        in_specs=[pl.BlockSpec((k, tr, cdim), lambda i, sel: (0, i, 0)), ANY],
        out_specs=pl.BlockSpec((None, None, tr, cdim), lambda i, sel: (layer, sel[0], i, 0)))
    return pl.pallas_call(
        body, name=name, out_shape=jax.ShapeDtypeStruct(into.shape, F32), grid_spec=grid_spec,
        input_output_aliases={2: 0},
        compiler_params=_params(("parallel",)),
    )(which.reshape(1).astype(jnp.int32), parts, into)


def _adam_math(w, g, m, v):
    m2 = ADAM_B1 * m + (1.0 - ADAM_B1) * g
    v2 = ADAM_B2 * v + (1.0 - ADAM_B2) * (g * g)
    m_hat = m2 / (1.0 - ADAM_B1 ** ADAM_STEP)
    v_hat = v2 / (1.0 - ADAM_B2 ** ADAM_STEP)
    delta = -ADAM_LR * (m_hat / (jnp.sqrt(v_hat) + ADAM_EPS) + ADAM_WD * w)
    return delta, m2, v2


def _adam_big(w, g, m, v, name):
    nl, r, cdim = w.shape
    tr = BIG_ROWS

    def body(w_ref, g_ref, m_ref, v_ref, d_ref, mo_ref, vo_ref):
        delta, m2, v2 = _adam_math(w_ref[...], g_ref[...], m_ref[...], v_ref[...])
        d_ref[...] = delta
        mo_ref[...] = m2
        vo_ref[...] = v2

    blk = pl.BlockSpec((None, tr, cdim), lambda l, i: (l, i, 0))
    shp = jax.ShapeDtypeStruct(w.shape, F32)
    return pl.pallas_call(
        body, name=name, out_shape=(shp, shp, shp),
        grid=(nl, r // tr), in_specs=[blk] * 4, out_specs=(blk, blk, blk),
        compiler_params=_params(("parallel", "parallel")),
    )(w, g, m, v)


def _adam_cols_major(w, g, m, v, name):
    cdim, nl, r = w.shape
    tc = BIG_ROWS

    def body(w_ref, g_ref, m_ref, v_ref, d_ref, mo_ref, vo_ref):
        delta, m2, v2 = _adam_math(w_ref[...], g_ref[...], m_ref[...], v_ref[...])
        d_ref[...] = delta
        mo_ref[...] = m2
        vo_ref[...] = v2

    blk = pl.BlockSpec((tc, nl, r), lambda i: (i, 0, 0))
    shp = jax.ShapeDtypeStruct(w.shape, F32)
    return pl.pallas_call(
        body, name=name, out_shape=(shp, shp, shp),
        grid=(pl.cdiv(cdim, tc),), in_specs=[blk] * 4, out_specs=(blk, blk, blk),
        compiler_params=_params(("parallel",)),
    )(w, g, m, v)


def _adam_small(ws, gs, ms, vs, name):
    n = len(ws)

    def body(*refs):
        w_refs, g_refs, m_refs, v_refs = (refs[k * n:(k + 1) * n] for k in range(4))
        d_refs, mo_refs, vo_refs = (refs[(4 + k) * n:(5 + k) * n] for k in range(3))
        for a in range(n):
            delta, m2, v2 = _adam_math(w_refs[a][...], g_refs[a][...], m_refs[a][...], v_refs[a][...])
            d_refs[a][...] = delta
            mo_refs[a][...] = m2
            vo_refs[a][...] = v2

    shapes = tuple(jax.ShapeDtypeStruct(w.shape, F32) for w in ws)
    vm = pl.BlockSpec(memory_space=pltpu.VMEM)
    outs = pl.pallas_call(body, name=name, out_shape=shapes * 3, in_specs=[vm] * (4 * n),
                          out_specs=tuple([vm] * (3 * n)))(*ws, *gs, *ms, *vs)
    return outs[:n], outs[n:2 * n], outs[2 * n:]


PACK_TILE = SUBLANES * LANES


def _pack(arrays):
    rows = []
    for a in arrays:
        flat = a.reshape(-1)
        pad = (-flat.shape[0]) % PACK_TILE
        if pad:
            flat = jnp.concatenate([flat, jnp.zeros((pad,), flat.dtype)])
        rows.append(flat.reshape(-1, LANES))
    return jnp.concatenate(rows, axis=0)


def _unpack(pack, shapes):
    outs, row = [], 0
    for shp in shapes:
        n = int(np.prod(shp))
        nrows = -(-n // PACK_TILE) * SUBLANES
        outs.append(pack[row:row + nrows].reshape(-1)[:n].reshape(shp))
        row += nrows
    return outs


SMALL = ["norm_w", "ssd_conv_b", "ssd_dt_bias", "ssd_a_log", "ssd_d", "ssd_norm_w", "attn_sinks",
         "conf_dw_b", "conf_ln_w", "conf_ln_b"]
WEIGHTS = ["norm_w", "w_in", "ssd_conv_w", "ssd_conv_b", "ssd_dt_bias", "ssd_a_log", "ssd_d", "ssd_norm_w",
           "attn_sinks", "conf_dw_w", "conf_dw_b", "conf_ln_w", "conf_ln_b", "w_out", "final_norm_w"]


def kernel(x, norm_w, w_in, ssd_conv_w, ssd_conv_b, ssd_dt_bias, ssd_a_log, ssd_d, ssd_norm_w, attn_sinks, conf_dw_w, conf_dw_b, conf_ln_w, conf_ln_b, w_out, final_norm_w, loss_target, m_norm_w, m_w_in, m_ssd_conv_w, m_ssd_conv_b, m_ssd_dt_bias, m_ssd_a_log, m_ssd_d, m_ssd_norm_w, m_attn_sinks, m_conf_dw_w, m_conf_dw_b, m_conf_ln_w, m_conf_ln_b, m_w_out, m_final_norm_w, v_norm_w, v_w_in, v_ssd_conv_w, v_ssd_conv_b, v_ssd_dt_bias, v_ssd_a_log, v_ssd_d, v_ssd_norm_w, v_attn_sinks, v_conf_dw_w, v_conf_dw_b, v_conf_ln_w, v_conf_ln_b, v_w_out, v_final_norm_w):
    w = dict(norm_w=norm_w, w_in=w_in, ssd_conv_w=ssd_conv_w, ssd_conv_b=ssd_conv_b, ssd_dt_bias=ssd_dt_bias,
             ssd_a_log=ssd_a_log, ssd_d=ssd_d, ssd_norm_w=ssd_norm_w, attn_sinks=attn_sinks, conf_dw_w=conf_dw_w,
             conf_dw_b=conf_dw_b, conf_ln_w=conf_ln_w, conf_ln_b=conf_ln_b, w_out=w_out, final_norm_w=final_norm_w)
    m = dict(norm_w=m_norm_w, w_in=m_w_in, ssd_conv_w=m_ssd_conv_w, ssd_conv_b=m_ssd_conv_b,
             ssd_dt_bias=m_ssd_dt_bias, ssd_a_log=m_ssd_a_log, ssd_d=m_ssd_d, ssd_norm_w=m_ssd_norm_w,
             attn_sinks=m_attn_sinks, conf_dw_w=m_conf_dw_w, conf_dw_b=m_conf_dw_b, conf_ln_w=m_conf_ln_w,
             conf_ln_b=m_conf_ln_b, w_out=m_w_out, final_norm_w=m_final_norm_w)
    v = dict(norm_w=v_norm_w, w_in=v_w_in, ssd_conv_w=v_ssd_conv_w, ssd_conv_b=v_ssd_conv_b,
             ssd_dt_bias=v_ssd_dt_bias, ssd_a_log=v_ssd_a_log, ssd_d=v_ssd_d, ssd_norm_w=v_ssd_norm_w,
             attn_sinks=v_attn_sinks, conf_dw_w=v_conf_dw_w, conf_dw_b=v_conf_dw_b, conf_ln_w=v_conf_ln_w,
             conf_ln_b=v_conf_ln_b, w_out=v_w_out, final_norm_w=v_final_norm_w)
    depth = w_in.shape[0]
    me = 2 * lax.axis_index("x") + lax.axis_index("y")

    assert depth == 2
    w_in_t = jnp.transpose(w_in, (2, 0, 1))
    w_in_b = _cast_cols_major(w_in_t, name="cast_w_in")
    w_out_b = [_cast_layer(w_out, li, name=f"cast_w_out_l{li}") for li in range(depth)]
    own0 = [w_in_b[0].reshape((2, -1) + w_in_b[0].shape[1:]), ssd_conv_w, conf_dw_w]
    gathered0 = _gather_weights(own0[:1], own0[1:], name="gather_weights_l0")
    g_in0, g_conv, g_dw = [lax.dynamic_update_index_in_dim(g_all, mine, me, 0)
                           for g_all, mine in zip(gathered0, own0)]
    own1 = [w_out_b[0], w_in_b[1], w_out_b[1]]
    pending1, token1 = _split_start(own1, "bcast", gathered0[0], name="gather_rest_start")
    rest = {}

    def small_full(li):
        return (jnp.concatenate([g_conv[p, li] for p in range(N_CHIPS)], axis=1),
                jnp.concatenate([g_dw[p, li] for p in range(N_CHIPS)], axis=1))

    def w_out_l0(after):
        mine1, landed = _split_wait(pending1, len(own1), "bcast", after, name="gather_rest_wait")
        rest["landed"] = [lax.dynamic_update_index_in_dim(g_all, mine, me, 0) for g_all, mine in zip(landed, mine1)]
        return rest["landed"][0].reshape(-1, w_out.shape[2])

    def params_l0(_):
        w_in_p = _padded_from_chips([g_in0[p].reshape(w_in_b[0].shape) for p in range(N_CHIPS)])
        return _layer_params(0, w_in_p, w_out_l0, *small_full(0), w)

    def params_l1(_):
        _, g_in1, g_out1 = rest["landed"]
        w_in_p = _padded_from_chips([g_in1[p] for p in range(N_CHIPS)])
        return _layer_params(1, w_in_p, g_out1.reshape(-1, g_out1.shape[-1]), *small_full(1), w)

    c = lax.axis_index("c")
    cols = w_in.shape[2]
    rows_out = w_out.shape[1]

    def grad_parts(g):
        dw = g["w_in_p"]
        return [dw.reshape(1, 2, dw.shape[0] // 2, dw.shape[1]),
                g["w_out"].reshape(N_CHIPS, 2, rows_out // 2, D_MODEL)]

    def pair_sums(parts, sib, tag):
        s_in, s_out = [_pair_sum(p, sb, c, MXU_DTYPE, name=f"grad_pair_sum_{k}_{tag}")
                       for k, (p, sb) in enumerate(zip(parts, sib))]
        return [jnp.stack([_chip_part_from_padded(s_in[0], p, cols) for p in range(N_CHIPS)]), s_out]

    split = {"reduced": [lax.empty((depth, 2, w_in.shape[1] // 2, cols), F32),
                         lax.empty((depth, 2, rows_out // 2, D_MODEL), F32)]}

    def chip_sums(landed, sent, li):
        filled = [lax.dynamic_update_index_in_dim(r, lax.dynamic_index_in_dim(sk, me, 0, keepdims=False), me, 0)
                  for r, sk in zip(landed, sent)]
        halves = [_sum_lead(r, into, li, c, name=f"grad_chip_sum_{k}_l{li}")
                  for k, (r, into) in enumerate(zip(filled, split["reduced"]))]
        split["reduced"] = list(_pair_gather(halves, li, name=f"grad_pair_gather_l{li}"))

    def bwd_hooks(li):
        def after_dw(dw_in_p, dw_out):
            parts = grad_parts(dict(w_in_p=dw_in_p, w_out=dw_out))
            state, token = _split_start(parts, "swap", None, name=f"grad_swap_l{li}_start")
            split[f"swap{li}"] = (parts, state)
            return token

        hooks = {"after_dw": after_dw}
        if li == depth - 2:
            parts, swap_state = split[f"swap{depth - 1}"]

            def after_dycat(dycat):
                mine, sib = _split_wait(swap_state, len(parts), "swap", dycat, name="grad_swap_l1_wait")
                sent = pair_sums(mine, sib, "l1")
                split["scatter"], token = _split_start(sent, "scatter", None, name="grad_scatter_l1_start")
                return token

            def after_attn(dproj):
                sent, landed = _split_wait(split["scatter"], len(parts), "scatter", dproj,
                                           name="grad_scatter_l1_wait")
                chip_sums(landed, sent, depth - 1)

            hooks.update(after_dycat=after_dycat, after_attn=after_attn)
        return hooks

    loss, grad_x, grads, dfinal = _local_step(x, loss_target, [params_l0, params_l1], final_norm_w,
                                              first_after=token1, bwd_hooks=bwd_hooks)

    parts0, swap0 = split["swap0"]
    sent0 = pair_sums(*_split_wait(swap0, len(parts0), "swap", grad_x, name="grad_swap_l0_wait"), "l0")
    scatter0, token0 = _split_start(sent0, "scatter", None, name="grad_scatter_l0_start")

    small_list = [grads[li][n] for li in range(depth) for n in SMALL]
    small_list += [grads[li][n] for li in range(depth) for n in ("ssd_conv_w", "conf_dw_w")]
    small_list += [dfinal, loss.reshape(1)]
    small_shapes = [a.shape for a in small_list]
    reduced = _unpack(_allreduce_small(_pack(small_list) + token0[0, 0], name="allreduce_small"), small_shapes)
    ns = len(SMALL)
    g = {n: jnp.stack([reduced[li * ns + i] for li in range(depth)]) for i, n in enumerate(SMALL)}
    conv_w_cols, dw_w_cols = ssd_conv_w.shape[2], conf_dw_w.shape[2]
    g["ssd_conv_w"] = jnp.stack([lax.dynamic_slice_in_dim(reduced[depth * ns + 2 * li], me * conv_w_cols,
                                                          conv_w_cols, axis=1) for li in range(depth)])
    g["conf_dw_w"] = jnp.stack([lax.dynamic_slice_in_dim(reduced[depth * ns + 2 * li + 1], me * dw_w_cols,
                                                         dw_w_cols, axis=1) for li in range(depth)])
    g["final_norm_w"] = reduced[-2]
    loss_total = reduced[-1][0]

    small_names = [n for n in WEIGHTS if n not in ("w_in", "w_out")]

    def as2d(a):
        return a.reshape(1, -1) if a.ndim == 1 else a

    deltas, new_ms, new_vs = _adam_small(*[[as2d(src[n]) for n in small_names] for src in (w, g, m, v)],
                                         name="adam_small")

    sent0, landed0 = _split_wait(scatter0, len(sent0), "scatter", deltas[0], name="grad_scatter_l0_wait")
    chip_sums(landed0, sent0, 0)
    g_w_in = split["reduced"][0].reshape(w_in.shape)
    g_w_out = split["reduced"][1].reshape(w_out.shape)

    outs_g, outs_d, outs_m, outs_v = {"w_in": g_w_in, "w_out": g_w_out}, {}, {}, {}
    to_cols, from_cols = (2, 0, 1), (1, 2, 0)
    outs_d["w_in"], outs_m["w_in"], outs_v["w_in"] = [
        jnp.transpose(a, from_cols) for a in _adam_cols_major(
            *[jnp.transpose(a, to_cols) for a in (w_in, g_w_in, m_w_in, v_w_in)], name="adam_w_in")]
    outs_d["w_out"], outs_m["w_out"], outs_v["w_out"] = _adam_big(w_out, g_w_out, m_w_out, v_w_out,
                                                                  name="adam_w_out")
    for n, dn, mn, vn in zip(small_names, deltas, new_ms, new_vs):
        outs_g[n], outs_d[n], outs_m[n], outs_v[n] = (g[n], dn.reshape(w[n].shape), mn.reshape(w[n].shape),
                                                      vn.reshape(w[n].shape))
    return (loss_total, grad_x, *[outs_g[n] for n in WEIGHTS], *[outs_d[n] for n in WEIGHTS],
            *[outs_m[n] for n in WEIGHTS], *[outs_v[n] for n in WEIGHTS])
```

```python
import functools
import math

import jax
import jax.numpy as jnp
import numpy as np
from jax import lax
from jax.experimental import pallas as pl
from jax.experimental.pallas import tpu as pltpu

F32 = jnp.float32
BF16 = jnp.bfloat16
MXU_DTYPE = BF16

D_MODEL = 1024
DEPTH = 2
SSD_HEADS = 16
SSD_HEAD_DIM = 64
SSD_STATE = 128
SSD_CONV = 4
CHUNK = 128
SSD_CONV_DIM = 1536
ATTN_HEAD_DIM = 64
ATTN_Q_HEADS = 8
WINDOW = 128
CONF_WIDTH = 512
CONF_KERNEL = 31
MIX_WIDTH = 2048
D_IN_PROJ = 5392
EPS = 1e-5

ADAM_LR = 0.001
ADAM_B1 = 0.9
ADAM_B2 = 0.999
ADAM_EPS = 1e-08
ADAM_WD = 0.01
ADAM_STEP = 10

LANES = 128
SUBLANES = 8
VMEM_LIMIT = 48 * 1024 * 1024

NP = 5632
OFF_ZA, OFF_Q, OFF_K, OFF_V, OFF_DT = 0, 512, 1024, 1152, 1280
ATTN_GROUP = 1536
OFF_CONF, OFF_ZC = 1536, 2560
CONF_GROUP = 1536
OFF_ZS = 3072
OFF_XBC = 4096
SECTIONS = ((0, 1024, OFF_ZS), (1024, 1536, OFF_ZA), (1536, 2048, OFF_ZC), (2048, 3584, OFF_XBC),
            (3584, 3600, OFF_DT), (3600, 4368, OFF_Q), (4368, 5392, OFF_CONF))

YCAT_ATTN, YCAT_CONF = 1024, 1536
ANY = pl.BlockSpec(memory_space=pl.ANY)

NN = (((1,), (0,)), ((), ()))
NT = (((1,), (1,)), ((), ()))
TN = (((0,), (0,)), ((), ()))


def _params(sem):
    return pltpu.CompilerParams(dimension_semantics=sem, vmem_limit_bytes=VMEM_LIMIT)


def _dot(a, b, dims=NN):
    return lax.dot_general(a.astype(MXU_DTYPE), b.astype(MXU_DTYPE), dims, preferred_element_type=F32)


def _split_bf16(a, passes):
    pieces = []
    r = a
    for _ in range(passes):
        p = r.astype(BF16)
        pieces.append(p)
        r = r - p.astype(F32)
    return pieces


def _xdot(a, sel, dims=NN, passes=2):
    out = None
    for p in _split_bf16(a, passes):
        t = lax.dot_general(p, sel, dims, preferred_element_type=F32)
        out = t if out is None else out + t
    return out


def _xdot_r(sel, b, dims=NN, passes=3):
    out = None
    for p in _split_bf16(b, passes):
        t = lax.dot_general(sel, p, dims, preferred_element_type=F32)
        out = t if out is None else out + t
    return out


def _sigmoid(x):
    return 1.0 / (1.0 + jnp.exp(-x))


def _silu(x):
    return x * _sigmoid(x)


def _dsilu(x):
    s = _sigmoid(x)
    return s * (1.0 + x * (1.0 - s))


def _softplus(x):
    return jnp.maximum(x, 0.0) + jnp.log(1.0 + jnp.exp(-jnp.abs(x)))


def _rowsum8(x):
    r, c = x.shape
    return jnp.sum(x.reshape(r // SUBLANES, SUBLANES, c), axis=0)


def _iota(shape, dim):
    return lax.broadcasted_iota(jnp.int32, shape, dim)


def _matmul(a, b, form, out_dtype, tm, tn, tk, name, residual=None, after=None):
    if form == "nn":
        (m, k), n = a.shape, b.shape[1]
    elif form == "nt":
        (m, k), n = a.shape, b.shape[0]
    else:
        (k, m), n = a.shape, b.shape[1]
    tm, tn, tk = min(tm, m), min(tn, n), min(tk, k)
    assert m % tm == 0 and n % tn == 0 and k % tk == 0, (name, m, n, k, tm, tn, tk)
    if form == "nn":
        a_spec = pl.BlockSpec((tm, tk), lambda i, j, s: (i, s))
        b_spec = pl.BlockSpec((tk, tn), lambda i, j, s: (s, j))
        dims = NN
    elif form == "nt":
        (m, k), n = a.shape, b.shape[0]
        a_spec = pl.BlockSpec((tm, tk), lambda i, j, s: (i, s))
        b_spec = pl.BlockSpec((tn, tk), lambda i, j, s: (j, s))
        dims = NT
    else:
        (k, m), n = a.shape, b.shape[1]
        a_spec = pl.BlockSpec((tk, tm), lambda i, j, s: (s, i))
        b_spec = pl.BlockSpec((tk, tn), lambda i, j, s: (s, j))
        dims = TN
    nk = k // tk
    has_res = residual is not None
    deps = [] if after is None else [after]

    def body_single(a_ref, b_ref, *rest):
        o = _dot(a_ref[...], b_ref[...], dims)
        if has_res:
            o = o + rest[0][...]
        rest[-1][...] = o.astype(out_dtype)

    def body(a_ref, b_ref, *rest):
        r_ref = rest[0] if has_res else None
        o_ref, acc = rest[-2:]
        s = pl.program_id(2)

        @pl.when(s == 0)
        def _():
            acc[...] = jnp.zeros_like(acc)

        acc[...] += _dot(a_ref[...], b_ref[...], dims)

        @pl.when(s == nk - 1)
        def _():
            o = acc[...]
            if has_res:
                o = o + r_ref[...]
            o_ref[...] = o.astype(out_dtype)

    in_specs = [a_spec, b_spec]
    args = [a, b]
    if has_res:
        in_specs.append(pl.BlockSpec((tm, tn), lambda i, j, s: (i, j)))
        args.append(residual)
    in_specs += [ANY] * len(deps)
    args += deps
    return pl.pallas_call(
        body_single if nk == 1 else body, name=name,
        out_shape=jax.ShapeDtypeStruct((m, n), out_dtype),
        grid=(m // tm, n // tn, nk),
        in_specs=in_specs,
        out_specs=pl.BlockSpec((tm, tn), lambda i, j, s: (i, j)),
        scratch_shapes=[] if nk == 1 else [pltpu.VMEM((tm, tn), F32)],
        compiler_params=_params(("parallel", "parallel", "arbitrary")),
    )(*args)


ROW_TILE = 256


PROJ_FWD_TM, PROJ_FWD_TN = 1024, 512


def _proj_fwd(x, w, w_in_p, name, after=None):
    t, d = x.shape
    n = w_in_p.shape[1]
    tm, tn = min(PROJ_FWD_TM, t), PROJ_FWD_TN
    assert t % tm == 0 and n % tn == 0
    deps = [] if after is None else [after]

    def body(x_ref, w_ref, b_ref, *rest):
        o_ref, ot_ref, h_scr = rest[len(deps):]

        @pl.when(pl.program_id(1) == 0)
        def _():
            xv = x_ref[...]
            rstd = lax.rsqrt(jnp.mean(xv * xv, axis=-1, keepdims=True) + EPS)
            h = xv * rstd * w_ref[...]
            h_scr[...] = h.astype(h_scr.dtype)
            ot_ref[...] = h.T.astype(ot_ref.dtype)

        o_ref[...] = _dot(h_scr[...], b_ref[...])

    return pl.pallas_call(
        body, name=name,
        out_shape=(jax.ShapeDtypeStruct((t, n), F32), jax.ShapeDtypeStruct((d, t), MXU_DTYPE)),
        grid=(t // tm, n // tn),
        in_specs=[pl.BlockSpec((tm, d), lambda i, j: (i, 0)), pl.BlockSpec((1, d), lambda i, j: (0, 0)),
                  pl.BlockSpec((d, tn), lambda i, j: (0, j))] + [ANY] * len(deps),
        out_specs=(pl.BlockSpec((tm, tn), lambda i, j: (i, j)), pl.BlockSpec((d, tm), lambda i, j: (0, i))),
        scratch_shapes=[pltpu.VMEM((tm, d), MXU_DTYPE)],
        compiler_params=_params(("parallel", "arbitrary")),
    )(x, w, w_in_p, *deps)


PROJ_BWD_TM, PROJ_BWD_TK = 1024, 1408


def _proj_bwd_dx(dproj, w_in_p, x, w, dres, name):
    t, d = x.shape
    kdim = dproj.shape[1]
    tm, tk = min(PROJ_BWD_TM, t), PROJ_BWD_TK
    nt, nk = t // tm, kdim // tk
    assert t % tm == 0 and kdim % tk == 0

    def body(a_ref, b_ref, x_ref, w_ref, dr_ref, dx_ref, dw_ref, acc, wacc):
        i, s = pl.program_id(0), pl.program_id(1)

        @pl.when((i == 0) & (s == 0))
        def _():
            wacc[...] = jnp.zeros_like(wacc)

        @pl.when(s == 0)
        def _():
            acc[...] = jnp.zeros_like(acc)

        acc[...] += _dot(a_ref[...], b_ref[...], NT)

        @pl.when(s == nk - 1)
        def _():
            xv = x_ref[...]
            rstd = lax.rsqrt(jnp.mean(xv * xv, axis=-1, keepdims=True) + EPS)
            xh = xv * rstd
            dhv = acc[...]
            g = dhv * w_ref[...]
            dx_ref[...] = dr_ref[...] + rstd * (g - xh * jnp.mean(g * xh, axis=-1, keepdims=True))
            wacc[...] += _rowsum8(dhv * xh)

        @pl.when((i == nt - 1) & (s == nk - 1))
        def _():
            dw_ref[...] = jnp.sum(wacc[...], axis=0, keepdims=True)

    row = pl.BlockSpec((tm, d), lambda i, s: (i, 0))
    vec = pl.BlockSpec((1, d), lambda i, s: (0, 0))
    return pl.pallas_call(
        body, name=name,
        out_shape=(jax.ShapeDtypeStruct((t, d), F32), jax.ShapeDtypeStruct((1, d), F32)),
        grid=(nt, nk),
        in_specs=[pl.BlockSpec((tm, tk), lambda i, s: (i, s)), pl.BlockSpec((d, tk), lambda i, s: (0, s)),
                  row, vec, row],
        out_specs=(row, vec),
        scratch_shapes=[pltpu.VMEM((tm, d), F32), pltpu.VMEM((SUBLANES, d), F32)],
        compiler_params=_params(("arbitrary", "arbitrary")),
    )(dproj, w_in_p, x, w, dres)


def _loss_head(xf, target, w, name):
    t, d = xf.shape
    tm = ROW_TILE
    nt = t // tm

    def body(x_ref, t_ref, w_ref, loss_ref, dx_ref, dw_ref, lacc, wacc):
        i = pl.program_id(0)

        @pl.when(i == 0)
        def _():
            lacc[...] = jnp.zeros_like(lacc)
            wacc[...] = jnp.zeros_like(wacc)

        xv = x_ref[...]
        rstd = lax.rsqrt(jnp.mean(xv * xv, axis=-1, keepdims=True) + EPS)
        xh = xv * rstd
        err = xh * w_ref[...] - t_ref[...]
        lacc[...] += jnp.sum(err * err)
        dy = err * (1.0 / d)
        g = dy * w_ref[...]
        dx_ref[...] = rstd * (g - xh * jnp.mean(g * xh, axis=-1, keepdims=True))
        wacc[...] += _rowsum8(dy * xh)

        @pl.when(i == nt - 1)
        def _():
            loss_ref[...] = lacc[...] * (0.5 / d)
            dw_ref[...] = jnp.sum(wacc[...], axis=0, keepdims=True)

    row = pl.BlockSpec((tm, d), lambda i: (i, 0))
    vec = pl.BlockSpec((1, d), lambda i: (0, 0))
    return pl.pallas_call(
        body, name=name,
        out_shape=(jax.ShapeDtypeStruct((SUBLANES, LANES), F32), jax.ShapeDtypeStruct((t, d), F32),
                   jax.ShapeDtypeStruct((1, d), F32)),
        grid=(nt,),
        in_specs=[row, row, vec],
        out_specs=(pl.BlockSpec((SUBLANES, LANES), lambda i: (0, 0)), row, vec),
        scratch_shapes=[pltpu.VMEM((SUBLANES, LANES), F32), pltpu.VMEM((SUBLANES, d), F32)],
        compiler_params=_params(("arbitrary",)),
    )(xf, target, w)


CONV_TILE = 512
CONV_COLS = 512
CONV_SUB_ROWS = 128
CONV_SUB_COLS = LANES


def _conv_halo(k):
    return SUBLANES if k - 1 <= SUBLANES else 32


def _conv_subtiles(tm, cw):
    return [(r0, c0) for r0 in range(0, tm, CONV_SUB_ROWS) for c0 in range(0, cw, CONV_SUB_COLS)]


def _conv_use_shifted(k):
    return k > SUBLANES


def _conv_shift_scratch(k, rows, cw):
    return [pltpu.VMEM((SUBLANES - 1, rows - SUBLANES, cw), F32)] if _conv_use_shifted(k) else []


def _conv_fill_shifted(ext, sh):
    n = sh.shape[1]
    for b in range(1, SUBLANES):
        sh[b - 1] = ext[b:b + n, :]


def _conv_rows(ext, sh, start, rows, cs):
    b = start % SUBLANES
    if b == 0 or not sh:
        return ext[start:start + rows, cs]
    return sh[0][b - 1, start - b:start - b + rows, cs]


def _conv_fwd(src, col0, width, w, bias, k, seq, name):
    t = src.shape[0]
    tm, cw, halo = CONV_TILE, CONV_COLS, _conv_halo(k)
    sr, sc = CONV_SUB_ROWS, CONV_SUB_COLS
    p = k - 1
    cb0 = col0 // cw
    kp = w.shape[0]

    shifted = _conv_use_shifted(k)

    def body(x_ref, h_ref, w_ref, b_ref, o_ref, ext, *sh):
        i = pl.program_id(0)
        seq_start = (i * tm) % seq == 0
        ext[halo:, :] = x_ref[...]
        ext[:halo, :] = jnp.where(seq_start, 0.0, h_ref[...])
        if shifted:
            _conv_fill_shifted(ext, sh[0])
        for r0, c0 in _conv_subtiles(tm, cw):
            cs = slice(c0, c0 + sc)
            acc = jnp.zeros((sr, sc), F32) + b_ref[:, cs]
            for j in range(k):
                acc = acc + w_ref[j:j + 1, cs] * _conv_rows(ext, sh, r0 + halo - p + j, sr, cs)
            o_ref[r0:r0 + sr, cs] = acc

    return pl.pallas_call(
        body, name=name,
        out_shape=jax.ShapeDtypeStruct((t, width), F32),
        grid=(t // tm, width // cw),
        in_specs=[pl.BlockSpec((tm, cw), lambda i, j: (i, cb0 + j)),
                  pl.BlockSpec((halo, cw), lambda i, j: (jnp.maximum(i * (tm // halo) - 1, 0), cb0 + j)),
                  pl.BlockSpec((kp, cw), lambda i, j: (0, j)),
                  pl.BlockSpec((1, cw), lambda i, j: (0, j))],
        out_specs=pl.BlockSpec((tm, cw), lambda i, j: (i, j)),
        scratch_shapes=[pltpu.VMEM((halo + tm, cw), F32)] + _conv_shift_scratch(k, halo + tm, cw),
        compiler_params=_params(("parallel", "parallel")),
    )(src, src, w, bias)


def _conv_bwd(dy, src, col0, width, w, k, seq, name, into=None):
    t = src.shape[0]
    tm, cw, halo = CONV_TILE, CONV_COLS, _conv_halo(k)
    sr, sc = CONV_SUB_ROWS, CONV_SUB_COLS
    p = k - 1
    cb0 = col0 // cw
    kp = w.shape[0]
    nt = t // tm
    last_halo = t // halo - 1

    shifted = _conv_use_shifted(k)

    def body(dy_ref, dn_ref, x_ref, xp_ref, w_ref, *rest):
        if into is not None:
            rest = rest[1:]
        dx_ref, dw_ref, db_ref, dyext, xext, wacc, bacc = rest[:7]
        sh = rest[7:]
        i = pl.program_id(1)
        dysh, xsh = (sh[:1], sh[1:]) if shifted else ((), ())

        @pl.when(i == 0)
        def _():
            wacc[...] = jnp.zeros_like(wacc)
            bacc[...] = jnp.zeros_like(bacc)

        seq_start = (i * tm) % seq == 0
        seq_end = ((i + 1) * tm) % seq == 0
        dyext[:tm, :] = dy_ref[...]
        dyext[tm:, :] = jnp.where(seq_end, 0.0, dn_ref[...])
        xext[halo:, :] = x_ref[...]
        xext[:halo, :] = jnp.where(seq_start, 0.0, xp_ref[...])
        if shifted:
            _conv_fill_shifted(dyext, dysh[0])
            _conv_fill_shifted(xext, xsh[0])
        for r0, c0 in _conv_subtiles(tm, cw):
            cs = slice(c0, c0 + sc)
            dyv = dy_ref[r0:r0 + sr, cs]
            acc = jnp.zeros((sr, sc), F32)
            for j in range(k):
                acc = acc + w_ref[j:j + 1, cs] * _conv_rows(dyext, dysh, r0 + p - j, sr, cs)
                wacc[j, :, cs] += _rowsum8(dyv * _conv_rows(xext, xsh, r0 + halo - p + j, sr, cs))
            dx_ref[r0:r0 + sr, cs] = acc.astype(dx_ref.dtype)
            bacc[:, cs] += _rowsum8(dyv)

        @pl.when(i == nt - 1)
        def _():
            dw_ref[...] = jnp.zeros_like(dw_ref)
            for j in range(k):
                dw_ref[j:j + 1, :] = jnp.sum(wacc[j], axis=0, keepdims=True)
            db_ref[...] = jnp.sum(bacc[...], axis=0, keepdims=True)

    if into is None:
        dx_shape = jax.ShapeDtypeStruct((t, width), F32)
        dx_spec = pl.BlockSpec((tm, cw), lambda j, i: (i, j))
        extra_specs, extra_args, aliases = [], [], {}
    else:
        dx_shape = jax.ShapeDtypeStruct(into.shape, into.dtype)
        dx_spec = pl.BlockSpec((tm, cw), lambda j, i: (i, cb0 + j))
        extra_specs, extra_args, aliases = [ANY], [into], {5: 0}
    return pl.pallas_call(
        body, name=name,
        out_shape=(dx_shape, jax.ShapeDtypeStruct((kp, width), F32), jax.ShapeDtypeStruct((1, width), F32)),
        grid=(width // cw, nt),
        in_specs=[pl.BlockSpec((tm, cw), lambda j, i: (i, j)),
                  pl.BlockSpec((halo, cw), lambda j, i: (jnp.minimum((i + 1) * (tm // halo), last_halo), j)),
                  pl.BlockSpec((tm, cw), lambda j, i: (i, cb0 + j)),
                  pl.BlockSpec((halo, cw), lambda j, i: (jnp.maximum(i * (tm // halo) - 1, 0), cb0 + j)),
                  pl.BlockSpec((kp, cw), lambda j, i: (0, j))] + extra_specs,
        out_specs=(dx_spec,
                   pl.BlockSpec((kp, cw), lambda j, i: (0, j)),
                   pl.BlockSpec((1, cw), lambda j, i: (0, j))),
        input_output_aliases=aliases,
        scratch_shapes=[pltpu.VMEM((tm + halo, cw), F32), pltpu.VMEM((halo + tm, cw), F32),
                        pltpu.VMEM((kp, SUBLANES, cw), F32), pltpu.VMEM((SUBLANES, cw), F32)]
        + 2 * _conv_shift_scratch(k, halo + tm, cw),
        compiler_params=_params(("parallel", "arbitrary")),
    )(dy, dy, src, src, w, *extra_args)


def _conf_specs(tm, cw, halo, order):
    cb = OFF_CONF // cw

    def blk(col):
        return pl.BlockSpec((tm, cw), lambda *g: (order(*g), col))

    def prev(col):
        return pl.BlockSpec((halo, cw), lambda *g: (jnp.maximum(order(*g) * (tm // halo) - 1, 0), col))

    return blk(cb), prev(cb), blk(cb + 1), prev(cb + 1)


def _glu_window(ext, a_ref, ah_ref, g_ref, gh_ref, seq_start, halo):
    ext[halo:, :] = a_ref[...] * _sigmoid(g_ref[...])
    ext[:halo, :] = jnp.where(seq_start, 0.0, ah_ref[...] * _sigmoid(gh_ref[...]))


def _conf_fwd(proj, w, bias, ln_w, ln_b, ycat, seq, name):
    t = proj.shape[0]
    k = CONF_KERNEL
    tm, cw, halo = CONV_TILE, CONF_WIDTH, _conv_halo(k)
    sr, sc = CONV_SUB_ROWS, CONV_SUB_COLS
    p = k - 1
    kp = w.shape[0]

    def body(a_ref, ah_ref, g_ref, gh_ref, z_ref, w_ref, b_ref, lw_ref, lb_ref, _, c1_ref, y_ref, ext, sh):
        i = pl.program_id(0)
        _glu_window(ext, a_ref, ah_ref, g_ref, gh_ref, (i * tm) % seq == 0, halo)
        _conv_fill_shifted(ext, sh)
        for r0, c0 in _conv_subtiles(tm, cw):
            cs = slice(c0, c0 + sc)
            acc = jnp.zeros((sr, sc), F32) + b_ref[:, cs]
            for j in range(k):
                acc = acc + w_ref[j:j + 1, cs] * _conv_rows(ext, (sh,), r0 + halo - p + j, sr, cs)
            c1_ref[r0:r0 + sr, cs] = acc
        for r0 in range(0, tm, sr):
            rows = slice(r0, r0 + sr)
            cv = c1_ref[rows, :]
            xc = cv - jnp.mean(cv, axis=-1, keepdims=True)
            rstd = lax.rsqrt(jnp.mean(xc * xc, axis=-1, keepdims=True) + EPS)
            c2 = xc * rstd * lw_ref[...] + lb_ref[...]
            y_ref[rows, :] = (_silu(c2) * _silu(z_ref[rows, :])).astype(y_ref.dtype)

    vec = pl.BlockSpec((1, cw), lambda i: (0, 0))
    row = pl.BlockSpec((tm, cw), lambda i: (i, 0))
    return pl.pallas_call(
        body, name=name,
        out_shape=(jax.ShapeDtypeStruct((t, cw), F32), jax.ShapeDtypeStruct(ycat.shape, ycat.dtype)),
        grid=(t // tm,),
        in_specs=[*_conf_specs(tm, cw, halo, lambda i: i),
                  pl.BlockSpec((tm, cw), lambda i: (i, OFF_ZC // cw)),
                  pl.BlockSpec((kp, cw), lambda i: (0, 0)), vec, vec, vec, ANY],
        out_specs=(row, pl.BlockSpec((tm, cw), lambda i: (i, YCAT_CONF // cw))),
        input_output_aliases={9: 1},
        scratch_shapes=[pltpu.VMEM((halo + tm, cw), F32)] + _conv_shift_scratch(k, halo + tm, cw),
        compiler_params=_params(("parallel",)),
    )(proj, proj, proj, proj, proj, w, bias, ln_w, ln_b, ycat)


def _conf_bwd(dycat, proj, c1, w, ln_w, ln_b, dproj, seq, name):
    t = proj.shape[0]
    k = CONF_KERNEL
    tm, cw, halo = CONV_TILE, CONF_WIDTH, _conv_halo(k)
    sr, sc = CONV_SUB_ROWS, CONV_SUB_COLS
    p = k - 1
    kp = w.shape[0]
    nt = t // tm
    last_halo = t // halo - 1

    def body(dy_ref, dyn_ref, c_ref, cn_ref, z_ref, zn_ref, a_ref, ah_ref, g_ref, gh_ref, w_ref, lw_ref, lb_ref, _,
             grp_ref, dw_ref, db_ref, dlw_ref, dlb_ref, dyext, xext, wacc, bacc, lwacc, lbacc, dysh, xsh):
        i = pl.program_id(0)

        @pl.when(i == 0)
        def _():
            wacc[...] = jnp.zeros_like(wacc)
            bacc[...] = jnp.zeros_like(bacc)
            lwacc[...] = jnp.zeros_like(lwacc)
            lbacc[...] = jnp.zeros_like(lbacc)

        def post_bwd(dy, cv, zv):
            xc = cv - jnp.mean(cv, axis=-1, keepdims=True)
            rstd = lax.rsqrt(jnp.mean(xc * xc, axis=-1, keepdims=True) + EPS)
            xh = xc * rstd
            c2 = xh * lw_ref[...] + lb_ref[...]
            dz = dy * _silu(c2) * _dsilu(zv)
            dc2 = dy * _silu(zv) * _dsilu(c2)
            dxh = dc2 * lw_ref[...]
            dc = rstd * (dxh - jnp.mean(dxh, axis=-1, keepdims=True)
                         - xh * jnp.mean(dxh * xh, axis=-1, keepdims=True))
            return dc, dz, dc2 * xh, dc2

        seq_end = ((i + 1) * tm) % seq == 0
        for r0 in range(0, tm, sr):
            rows = slice(r0, r0 + sr)
            dc, dz, lw_terms, lb_terms = post_bwd(dy_ref[rows, :], c_ref[rows, :], z_ref[rows, :])
            dyext[rows, :] = dc
            grp_ref[rows, 2 * cw:] = dz.astype(grp_ref.dtype)
            lwacc[...] += _rowsum8(lw_terms)
            lbacc[...] += _rowsum8(lb_terms)
        dc_next = post_bwd(dyn_ref[...], cn_ref[...], zn_ref[...])[0]
        dyext[tm:, :] = jnp.where(seq_end, 0.0, dc_next)
        _glu_window(xext, a_ref, ah_ref, g_ref, gh_ref, (i * tm) % seq == 0, halo)
        _conv_fill_shifted(dyext, dysh)
        _conv_fill_shifted(xext, xsh)
        dag_ref = grp_ref
        for r0, c0 in _conv_subtiles(tm, cw):
            cs = slice(c0, c0 + sc)
            rows = slice(r0, r0 + sr)
            dyv = dyext[rows, cs]
            acc = jnp.zeros((sr, sc), F32)
            for j in range(k):
                acc = acc + w_ref[j:j + 1, cs] * _conv_rows(dyext, (dysh,), r0 + p - j, sr, cs)
                wacc[j, :, cs] += _rowsum8(dyv * _conv_rows(xext, (xsh,), r0 + halo - p + j, sr, cs))
            bacc[:, cs] += _rowsum8(dyv)
            s = _sigmoid(g_ref[rows, cs])
            dag_ref[rows, cs] = (acc * s).astype(dag_ref.dtype)
            dag_ref[rows, cw + c0:cw + c0 + sc] = (acc * a_ref[rows, cs] * s * (1.0 - s)).astype(dag_ref.dtype)

        @pl.when(i == nt - 1)
        def _():
            dw_ref[...] = jnp.zeros_like(dw_ref)
            for j in range(k):
                dw_ref[j:j + 1, :] = jnp.sum(wacc[j], axis=0, keepdims=True)
            db_ref[...] = jnp.sum(bacc[...], axis=0, keepdims=True)
            dlw_ref[...] = jnp.sum(lwacc[...], axis=0, keepdims=True)
            dlb_ref[...] = jnp.sum(lbacc[...], axis=0, keepdims=True)

    def blk(col):
        return pl.BlockSpec((tm, cw), lambda i: (i, col))

    def nxt(col):
        return pl.BlockSpec((halo, cw), lambda i: (jnp.minimum((i + 1) * (tm // halo), last_halo), col))

    vec = pl.BlockSpec((1, cw), lambda i: (0, 0))
    return pl.pallas_call(
        body, name=name,
        out_shape=(jax.ShapeDtypeStruct(dproj.shape, dproj.dtype), jax.ShapeDtypeStruct((kp, cw), F32),
                   jax.ShapeDtypeStruct((1, cw), F32), jax.ShapeDtypeStruct((1, cw), F32),
                   jax.ShapeDtypeStruct((1, cw), F32)),
        grid=(nt,),
        in_specs=[blk(YCAT_CONF // cw), nxt(YCAT_CONF // cw), blk(0), nxt(0), blk(OFF_ZC // cw), nxt(OFF_ZC // cw),
                  *_conf_specs(tm, cw, halo, lambda i: i),
                  pl.BlockSpec((kp, cw), lambda i: (0, 0)), vec, vec, ANY],
        out_specs=(pl.BlockSpec((tm, CONF_GROUP), lambda i: (i, OFF_CONF // CONF_GROUP)),
                   pl.BlockSpec((kp, cw), lambda i: (0, 0)), vec, vec, vec),
        input_output_aliases={13: 0},
        scratch_shapes=[pltpu.VMEM((tm + halo, cw), F32), pltpu.VMEM((halo + tm, cw), F32),
                        pltpu.VMEM((kp, SUBLANES, cw), F32), pltpu.VMEM((SUBLANES, cw), F32),
                        pltpu.VMEM((SUBLANES, cw), F32), pltpu.VMEM((SUBLANES, cw), F32)]
        + 2 * _conv_shift_scratch(k, halo + tm, cw),
        compiler_params=_params(("arbitrary",)),
    )(dycat, dycat, c1, c1, proj, proj, proj, proj, proj, proj, w, ln_w, ln_b, dproj)


def _half_mask(half):
    lane = _iota((1, LANES), 1)
    return ((lane >= half * ATTN_HEAD_DIM) & (lane < (half + 1) * ATTN_HEAD_DIM)).astype(F32)


def _stack_heads(xp, g):
    m = _half_mask(g)
    swapped = pltpu.roll(xp, ATTN_HEAD_DIM, axis=1)
    return jnp.concatenate([xp * m, swapped * m] if g == 0 else [swapped * m, xp * m], axis=0)


def _unstack_heads(both, g):
    w = both.shape[0] // 2
    top, bot = both[:w], both[w:]
    lo, hi = _half_mask(0), _half_mask(1)
    if g == 0:
        return top * lo + pltpu.roll(bot, ATTN_HEAD_DIM, axis=1) * hi
    return pltpu.roll(top, ATTN_HEAD_DIM, axis=1) * lo + bot * hi


def _band_mask(first_block):
    w = WINDOW
    qi = _iota((w, 2 * w), 0)
    kj = _iota((w, 2 * w), 1) - w
    rel = qi - kj
    return (rel >= 0) & (rel < w) & (jnp.logical_not(first_block) | (kj >= 0))


def _lane_pick(x, h):
    return jnp.sum(jnp.where(_iota(x.shape, 1) == h, x, 0.0), axis=1, keepdims=True)


def _attn_specs(nb, rev):
    w = WINDOW

    def blk(i):
        return nb - 1 - i if rev else i

    def row(b, i):
        return b * nb + blk(i)

    def prow(b, i):
        return b * nb + jnp.maximum(blk(i) - 1, 0)

    q = pl.BlockSpec((w, 512), lambda b, i: (row(b, i), OFF_Q // 512))
    kc = pl.BlockSpec((w, 128), lambda b, i: (row(b, i), OFF_K // 128))
    kp = pl.BlockSpec((w, 128), lambda b, i: (prow(b, i), OFF_K // 128))
    vc = pl.BlockSpec((w, 128), lambda b, i: (row(b, i), OFF_V // 128))
    vp = pl.BlockSpec((w, 128), lambda b, i: (prow(b, i), OFF_V // 128))
    z = pl.BlockSpec((w, 512), lambda b, i: (row(b, i), OFF_ZA // 512))
    return q, kc, kp, vc, vp, z, row


def _attn_fwd(proj, sinks, ycat, nbatch, name):
    t = proj.shape[0]
    w = WINDOW
    nb = t // nbatch // w
    scale = ATTN_HEAD_DIM ** -0.5
    q_s, kc_s, kp_s, vc_s, vp_s, z_s, row = _attn_specs(nb, False)

    def body(q_ref, kc_ref, kp_ref, vc_ref, vp_ref, z_ref, sk_ref, _, y_ref, o_ref, lse_ref):
        first = pl.program_id(1) == 0
        mask = _band_mask(first)
        kk = jnp.concatenate([kp_ref[...], kc_ref[...]], axis=0).astype(MXU_DTYPE)
        vv = jnp.concatenate([vp_ref[...], vc_ref[...]], axis=0).astype(MXU_DTYPE)
        sk = sk_ref[...]
        lane = _iota((w, LANES), 1)
        mask2 = jnp.concatenate([mask, mask], axis=0)
        scores = [_dot(_stack_heads(q_ref[:, j * LANES:(j + 1) * LANES], j // 2), kk, NT) for j in range(4)]
        lse_all = jnp.zeros((w, LANES), F32)
        for j in range(4):
            s = jnp.where(mask2, scores[j] * scale, -1e30)
            skc = jnp.concatenate([jnp.broadcast_to(_lane_pick(sk, 2 * j), (w, 1)),
                                   jnp.broadcast_to(_lane_pick(sk, 2 * j + 1), (w, 1))], axis=0)
            m = jnp.maximum(jnp.max(s, axis=1, keepdims=True), skc)
            den = jnp.sum(jnp.exp(s - m), axis=1, keepdims=True) + jnp.exp(skc - m)
            lse = m + jnp.log(den)
            lse_all = jnp.where(lane == 2 * j, lse[:w], lse_all)
            lse_all = jnp.where(lane == 2 * j + 1, lse[w:], lse_all)
            op = _unstack_heads(_dot(jnp.exp(s - lse), vv), j // 2)
            cols = slice(j * LANES, (j + 1) * LANES)
            o_ref[:, cols] = op
            y_ref[:, cols] = (op * _silu(z_ref[:, cols])).astype(y_ref.dtype)
        lse_ref[...] = lse_all

    return pl.pallas_call(
        body, name=name,
        out_shape=(jax.ShapeDtypeStruct(ycat.shape, ycat.dtype), jax.ShapeDtypeStruct((t, 512), F32),
                   jax.ShapeDtypeStruct((t, LANES), F32)),
        grid=(nbatch, nb),
        in_specs=[q_s, kc_s, kp_s, vc_s, vp_s, z_s, pl.BlockSpec((1, LANES), lambda b, i: (0, 0)), ANY],
        out_specs=(pl.BlockSpec((w, 512), lambda b, i: (row(b, i), YCAT_ATTN // 512)),
                   pl.BlockSpec((w, 512), lambda b, i: (row(b, i), 0)),
                   pl.BlockSpec((w, LANES), lambda b, i: (row(b, i), 0))),
        input_output_aliases={7: 0},
        compiler_params=_params(("parallel", "parallel")),
    )(proj, proj, proj, proj, proj, proj, sinks, ycat)


def _attn_bwd(dycat, proj, o, lse, sinks, ddt, dproj, nbatch, name):
    t = proj.shape[0]
    w = WINDOW
    nb = t // nbatch // w
    scale = ATTN_HEAD_DIM ** -0.5
    q_s, kc_s, kp_s, vc_s, vp_s, z_s, row = _attn_specs(nb, True)

    def body(dy_ref, q_ref, kc_ref, kp_ref, vc_ref, vp_ref, z_ref, o_ref, lse_ref, sk_ref, ddt_ref, _,
             grp_ref, dsk_ref, kcarry, vcarry, sacc):
        b, i = pl.program_id(0), pl.program_id(1)

        @pl.when((b == 0) & (i == 0))
        def _():
            sacc[...] = jnp.zeros_like(sacc)

        @pl.when(i == 0)
        def _():
            kcarry[...] = jnp.zeros_like(kcarry)
            vcarry[...] = jnp.zeros_like(vcarry)

        first = i == nb - 1
        mask = _band_mask(first)
        kk = jnp.concatenate([kp_ref[...], kc_ref[...]], axis=0).astype(MXU_DTYPE)
        vv = jnp.concatenate([vp_ref[...], vc_ref[...]], axis=0).astype(MXU_DTYPE)
        sk = sk_ref[...]
        lse_all = lse_ref[...]
        lane1 = _iota((1, LANES), 1)
        mask2 = jnp.concatenate([mask, mask], axis=0)
        qs, dos, deltas, lses, scores, dps = [], [], [], [], [], []
        for j in range(4):
            cols = slice(j * LANES, (j + 1) * LANES)
            qp, zp, ov, dy = q_ref[:, cols], z_ref[:, cols], o_ref[:, cols], dy_ref[:, cols]
            grp_ref[:, OFF_ZA + j * LANES:OFF_ZA + (j + 1) * LANES] = (dy * ov * _dsilu(zp)).astype(grp_ref.dtype)
            do = dy * _silu(zp)
            q2 = _stack_heads(qp, j // 2).astype(MXU_DTYPE)
            do2 = _stack_heads(do, j // 2)
            qs.append(q2)
            dos.append(do2.astype(MXU_DTYPE))
            deltas.append(jnp.sum(do2 * _stack_heads(ov, j // 2), axis=1, keepdims=True))
            lses.append(jnp.concatenate([_lane_pick(lse_all, 2 * j), _lane_pick(lse_all, 2 * j + 1)], axis=0))
            scores.append(_dot(q2, kk, NT))
            dps.append(_dot(do2, vv, NT))
        prs, dss = [], []
        dsk = jnp.zeros((1, LANES), F32)
        for j in range(4):
            pr = jnp.exp(jnp.where(mask2, scores[j] * scale, -1e30) - lses[j])
            prs.append(pr.astype(MXU_DTYPE))
            dss.append((pr * (dps[j] - deltas[j])).astype(MXU_DTYPE))
            skc = jnp.concatenate([jnp.broadcast_to(_lane_pick(sk, 2 * j), (w, 1)),
                                   jnp.broadcast_to(_lane_pick(sk, 2 * j + 1), (w, 1))], axis=0)
            sink_term = jnp.exp(skc - lses[j]) * deltas[j]
            dsk = dsk - jnp.where(lane1 == 2 * j, jnp.sum(sink_term[:w]), 0.0)
            dsk = dsk - jnp.where(lane1 == 2 * j + 1, jnp.sum(sink_term[w:]), 0.0)
        dkk = jnp.zeros((2 * w, LANES), F32)
        dvv = jnp.zeros((2 * w, LANES), F32)
        for j in range(4):
            dq = _unstack_heads(_dot(dss[j], kk) * scale, j // 2)
            grp_ref[:, OFF_Q + j * LANES:OFF_Q + (j + 1) * LANES] = dq.astype(grp_ref.dtype)
            dkk = dkk + _dot(dss[j], qs[j], TN) * scale
            dvv = dvv + _dot(prs[j], dos[j], TN)
        grp_ref[:, OFF_K:OFF_K + LANES] = (dkk[w:, :] + kcarry[...]).astype(grp_ref.dtype)
        grp_ref[:, OFF_V:OFF_V + LANES] = (dvv[w:, :] + vcarry[...]).astype(grp_ref.dtype)
        grp_ref[:, OFF_DT:OFF_DT + LANES] = ddt_ref[...].astype(grp_ref.dtype)
        grp_ref[:, OFF_DT + LANES:] = jnp.zeros((w, ATTN_GROUP - OFF_DT - LANES), grp_ref.dtype)
        kcarry[...] = dkk[:w, :]
        vcarry[...] = dvv[:w, :]
        sacc[...] += dsk

        @pl.when((b == nbatch - 1) & (i == nb - 1))
        def _():
            dsk_ref[...] = sacc[...]

    return pl.pallas_call(
        body, name=name,
        out_shape=(jax.ShapeDtypeStruct(dproj.shape, dproj.dtype), jax.ShapeDtypeStruct((1, LANES), F32)),
        grid=(nbatch, nb),
        in_specs=[pl.BlockSpec((w, 512), lambda b, i: (row(b, i), YCAT_ATTN // 512)),
                  q_s, kc_s, kp_s, vc_s, vp_s, z_s,
                  pl.BlockSpec((w, 512), lambda b, i: (row(b, i), 0)),
                  pl.BlockSpec((w, LANES), lambda b, i: (row(b, i), 0)),
                  pl.BlockSpec((1, LANES), lambda b, i: (0, 0)),
                  pl.BlockSpec((w, LANES), lambda b, i: (row(b, i), 0)), ANY],
        out_specs=(pl.BlockSpec((w, ATTN_GROUP), lambda b, i: (row(b, i), 0)),
                   pl.BlockSpec((1, LANES), lambda b, i: (0, 0))),
        input_output_aliases={11: 0},
        scratch_shapes=[pltpu.VMEM((w, LANES), F32), pltpu.VMEM((w, LANES), F32),
                        pltpu.VMEM((1, LANES), F32)],
        compiler_params=_params(("arbitrary", "arbitrary")),
    )(dycat, proj, proj, proj, proj, proj, proj, o, lse, sinks, ddt, dproj)


SSD_WIDTH = SSD_HEADS * SSD_HEAD_DIM
GROUP_ROWS = SSD_WIDTH // 2


def _expand_mat():
    r, c = _iota((LANES, SSD_WIDTH), 0), _iota((LANES, SSD_WIDTH), 1)
    return (r == lax.shift_right_logical(c, 6)).astype(BF16)


def _expand_mat_t():
    r, c = _iota((SSD_WIDTH, LANES), 0), _iota((SSD_WIDTH, LANES), 1)
    return (c == lax.shift_right_logical(r, 6)).astype(BF16)


def _ssd_common(u_ref, dt_ref, dtb_ref, a_ref):
    q = CHUNK
    act = _silu(u_ref[...])
    xs = act[:, :SSD_WIDTH]
    bm = act[:, SSD_WIDTH:SSD_WIDTH + 256]
    cm = act[:, SSD_WIDTH + 256:]
    dtp = _softplus(dt_ref[...] + dtb_ref[...])
    a = dtp * a_ref[...]
    tril = (_iota((q, q), 0) >= _iota((q, q), 1)).astype(BF16)
    acs = _xdot_r(tril, a)
    acs_t = acs.T
    e = _expand_mat()
    dt_x = _xdot(dtp, e)
    ea = jnp.exp(_xdot(acs, e))
    a_end = jnp.sum(jnp.where(_iota(acs.shape, 0) == q - 1, acs, 0.0), axis=0, keepdims=True)
    dec = jnp.exp(_xdot(a_end - acs, e))
    a_end_col = jnp.broadcast_to(_lane_pick(acs_t, q - 1), (LANES, LANES))
    s_scale = jnp.exp(_xdot_r(_expand_mat_t(), a_end_col))
    return act, xs, bm, cm, dtp, acs, acs_t, dt_x, ea, dec, s_scale, tril


def _decay_mat(acs, acs_t, h):
    q = CHUNK
    col = _lane_pick(acs, h)
    rowv = jnp.sum(jnp.where(_iota(acs_t.shape, 0) == h, acs_t, 0.0), axis=0, keepdims=True)
    causal = _iota((q, q), 0) >= _iota((q, q), 1)
    return jnp.exp(jnp.where(causal, col - rowv, -1e30))


GN_WIDTH = 512


def _ssd_fwd(u, proj, dtb, a_neg, d_x, norm_w, ycat, nbatch, name):
    t = u.shape[0]
    q = CHUNK
    nc = t // nbatch // q

    def body(u_ref, dt_ref, z_ref, dtb_ref, a_ref, dx_ref, nw_ref, _, y_ref, st_ref, yn_ref, state):
        c = pl.program_id(1)

        @pl.when(c == 0)
        def _():
            state[...] = jnp.zeros_like(state)

        st_ref[...] = state[...]
        act, xs, bm, cm, dtp, acs, acs_t, dt_x, ea, dec, s_scale, _ = _ssd_common(u_ref, dt_ref, dtb_ref, a_ref)
        xdt = xs * dt_x
        xdec = xdt * dec
        lo, hi = _half_mask(0), _half_mask(1)
        grp = []
        for g in range(2):
            bg = bm[:, g * LANES:(g + 1) * LANES]
            cg = cm[:, g * LANES:(g + 1) * LANES]
            rows = slice(g * GROUP_ROWS, (g + 1) * GROUP_ROWS)
            sg = state[rows, :]
            grp.append((_dot(cg, bg, NT), _dot(cg, sg, NT), rows,
                        s_scale[rows, :] * sg + _dot(xdec[:, rows], bg, TN)))
        for g in range(2):
            cb, yoff, rows, state_new = grp[g]
            for j in range(4):
                pj = g * 4 + j
                cols = slice(pj * LANES, (pj + 1) * LANES)
                xp = xdt[:, cols]
                m2 = jnp.concatenate([cb * _decay_mat(acs, acs_t, 2 * pj), cb * _decay_mat(acs, acs_t, 2 * pj + 1)],
                                     axis=1)
                yp = _dot(m2, jnp.concatenate([xp * lo, xp * hi], axis=0))
                yp = yp + yoff[:, j * LANES:(j + 1) * LANES] * ea[:, cols]
                y_ref[:, cols] = yp + dx_ref[:, cols] * xs[:, cols]
            state[rows, :] = state_new
        for g in range(SSD_WIDTH // GN_WIDTH):
            cols = slice(g * GN_WIDTH, (g + 1) * GN_WIDTH)
            gg = y_ref[:, cols] * _silu(z_ref[:, cols])
            rstd = lax.rsqrt(jnp.mean(gg * gg, axis=-1, keepdims=True) + EPS)
            yn_ref[:, cols] = (gg * rstd * nw_ref[:, cols]).astype(yn_ref.dtype)

    vec = pl.BlockSpec((1, LANES), lambda b, c: (0, 0))
    wide = pl.BlockSpec((q, SSD_WIDTH), lambda b, c: (b * nc + c, 0))
    wvec = pl.BlockSpec((1, SSD_WIDTH), lambda b, c: (0, 0))
    return pl.pallas_call(
        body, name=name,
        out_shape=(jax.ShapeDtypeStruct((t, SSD_WIDTH), F32),
                   jax.ShapeDtypeStruct((nbatch * nc * SSD_WIDTH, SSD_STATE), F32),
                   jax.ShapeDtypeStruct(ycat.shape, ycat.dtype)),
        grid=(nbatch, nc),
        in_specs=[pl.BlockSpec((q, SSD_CONV_DIM), lambda b, c: (b * nc + c, 0)),
                  pl.BlockSpec((q, LANES), lambda b, c: (b * nc + c, OFF_DT // LANES)),
                  pl.BlockSpec((q, SSD_WIDTH), lambda b, c: (b * nc + c, OFF_ZS // SSD_WIDTH)),
                  vec, vec, wvec, wvec, ANY],
        out_specs=(wide, pl.BlockSpec((SSD_WIDTH, SSD_STATE), lambda b, c: (b * nc + c, 0)), wide),
        input_output_aliases={7: 2},
        scratch_shapes=[pltpu.VMEM((SSD_WIDTH, SSD_STATE), F32)],
        compiler_params=_params(("parallel", "arbitrary")),
    )(u, proj, proj, dtb, a_neg, d_x, norm_w, ycat)


def _ssd_bwd(dycat, u, proj, y, states, dtb, a_neg, d_x, norm_w, dproj, nbatch, name):
    t = u.shape[0]
    q = CHUNK
    nc = t // nbatch // q

    def body(do_ref, u_ref, dt_ref, z_ref, y_ref, st_ref, dtb_ref, a_ref, dx_ref, nw_ref, _,
             du_ref, dz_ref, ddt_ref, dal_ref, dd_ref, dtbg_ref, dnw_ref, dstate, acc_a, acc_d, acc_b, acc_w):
        b, c = pl.program_id(0), pl.program_id(1)

        @pl.when((b == 0) & (c == 0))
        def _():
            acc_a[...] = jnp.zeros_like(acc_a)
            acc_d[...] = jnp.zeros_like(acc_d)
            acc_b[...] = jnp.zeros_like(acc_b)
            acc_w[...] = jnp.zeros_like(acc_w)

        @pl.when(c == 0)
        def _():
            dstate[...] = jnp.zeros_like(dstate)

        dy_parts = []
        for g in range(SSD_WIDTH // GN_WIDTH):
            cols = slice(g * GN_WIDTH, (g + 1) * GN_WIDTH)
            yv, zv, dov = y_ref[:, cols], z_ref[:, cols], do_ref[:, cols]
            sz = _silu(zv)
            gg = yv * sz
            rstd = lax.rsqrt(jnp.mean(gg * gg, axis=-1, keepdims=True) + EPS)
            gh = gg * rstd
            acc_w[:, cols] += _rowsum8(dov * gh)
            dgn = dov * nw_ref[:, cols]
            dg = rstd * (dgn - gh * jnp.mean(dgn * gh, axis=-1, keepdims=True))
            dy_parts.append(dg * sz)
            dz_ref[:, cols] = (dg * yv * _dsilu(zv)).astype(dz_ref.dtype)

        act, xs, bm, cm, dtp, acs, acs_t, dt_x, ea, dec, s_scale, tril = _ssd_common(
            u_ref, dt_ref, dtb_ref, a_ref)
        xdt = xs * dt_x
        xdec = xdt * dec
        dyv = jnp.concatenate(dy_parts, axis=1)
        dye = dyv * ea
        lo, hi = _half_mask(0), _half_mask(1)
        et = _expand_mat_t()
        grp = []
        for g in range(2):
            rows = slice(g * GROUP_ROWS, (g + 1) * GROUP_ROWS)
            bg = bm[:, g * LANES:(g + 1) * LANES]
            cg = cm[:, g * LANES:(g + 1) * LANES]
            sg = st_ref[rows, :]
            dsg = dstate[rows, :]
            grp.append(dict(
                rows=rows, bg=bg, cg=cg, dsg=dsg,
                cb=_dot(cg, bg, NT), yoff=_dot(cg, sg, NT), dxst=_dot(bg, dsg, NT) * dec[:, rows],
                dc_off=_dot(dye[:, rows], sg), db_off=_dot(xdec[:, rows], dsg),
                s_carried=s_scale[rows, :] * sg,
                dstate_new=_dot(dye[:, rows], cg, TN) + s_scale[rows, :] * dsg))
        dy2s, g2s, l2s = [], [], []
        for pj in range(SSD_HEADS // 2):
            cols = slice(pj * LANES, (pj + 1) * LANES)
            dyp = dyv[:, cols]
            dy2 = jnp.concatenate([dyp * lo, dyp * hi], axis=0).astype(MXU_DTYPE)
            dy2s.append(dy2)
            g2s.append(_dot(dy2, xdt[:, cols], NT))
            l2s.append(jnp.concatenate([_decay_mat(acs, acs_t, 2 * pj), _decay_mat(acs, acs_t, 2 * pj + 1)], axis=0))
        dal_diag = jnp.zeros((q, LANES), F32)
        lane2 = _iota((2 * q, LANES), 1)
        row2 = _iota((2 * q, LANES), 0)
        dxdt_parts, db_parts, dc_parts = [], [], []
        end_sum = jnp.zeros((LANES, LANES), F32)
        for g in range(2):
            gd = grp[g]
            cb2 = jnp.concatenate([gd["cb"], gd["cb"]], axis=0)
            dcb = jnp.zeros((q, q), F32)
            parts = []
            for j in range(4):
                pj = g * 4 + j
                gl = g2s[pj] * l2s[pj]
                dcb = dcb + gl[:q] + gl[q:]
                m2 = cb2 * l2s[pj]
                parts.append(_dot(m2, dy2s[pj], TN))
                w2 = (gl * cb2).astype(MXU_DTYPE)
                sel2 = (lane2 == 2 * pj + (row2 >= q).astype(jnp.int32)).astype(MXU_DTYPE)
                dal_diag = dal_diag + _dot(jnp.concatenate([w2[:q], w2[q:]], axis=1), sel2) - _dot(w2, sel2, TN)
            dxdt_parts.append(jnp.concatenate(parts, axis=1) + gd["dxst"])
            dc_parts.append(_dot(dcb, gd["bg"]) + gd["dc_off"])
            db_parts.append(_dot(dcb, gd["cg"], TN) + gd["db_off"])
            end_sum = end_sum + _xdot(gd["dsg"] * gd["s_carried"], et[gd["rows"], :], TN, passes=2)
            dstate[gd["rows"], :] = gd["dstate_new"]
        dxst_parts = [gd["dxst"] for gd in grp]
        yoff_parts = [gd["yoff"] for gd in grp]
        dxdt = jnp.concatenate(dxdt_parts, axis=1)
        dxv = dx_ref[...]
        yoff = jnp.concatenate(yoff_parts, axis=1) * ea
        st_term = _xdot(xdt * jnp.concatenate(dxst_parts, axis=1), et)
        dalpha = dal_diag + _xdot(dyv * yoff, et) - st_term
        end_row = jnp.sum(end_sum, axis=0, keepdims=True) + jnp.sum(st_term, axis=0, keepdims=True)
        dalpha = dalpha + jnp.where(_iota((q, LANES), 0) == q - 1, end_row, 0.0)
        da = _xdot_r(tril, dalpha, TN)
        ddtp = da * a_ref[...] + _xdot(dxdt * xs, et)
        acc_a[...] += _rowsum8(da * dtp)
        acc_d[...] += _rowsum8(_xdot(dyv * xs, et))
        ddt_raw = ddtp * _sigmoid(dt_ref[...] + dtb_ref[...])
        acc_b[...] += _rowsum8(ddt_raw)
        ddt_ref[...] = ddt_raw
        dxs = dxdt * dt_x + dxv * dyv
        dact = jnp.concatenate([dxs] + db_parts + dc_parts, axis=1)
        du_ref[...] = dact * _dsilu(u_ref[...])

        @pl.when((b == nbatch - 1) & (c == nc - 1))
        def _():
            dal_ref[...] = jnp.sum(acc_a[...], axis=0, keepdims=True) * a_ref[...]
            dd_ref[...] = jnp.sum(acc_d[...], axis=0, keepdims=True)
            dtbg_ref[...] = jnp.sum(acc_b[...], axis=0, keepdims=True)
            dnw_ref[...] = jnp.sum(acc_w[...], axis=0, keepdims=True)

    def rowblk(b, c):
        return b * nc + (nc - 1 - c)

    vec = pl.BlockSpec((1, LANES), lambda b, c: (0, 0))
    wvec = pl.BlockSpec((1, SSD_WIDTH), lambda b, c: (0, 0))
    wide = pl.BlockSpec((q, SSD_WIDTH), lambda b, c: (rowblk(b, c), 0))
    zblk = pl.BlockSpec((q, SSD_WIDTH), lambda b, c: (rowblk(b, c), OFF_ZS // SSD_WIDTH))
    return pl.pallas_call(
        body, name=name,
        out_shape=(jax.ShapeDtypeStruct((t, SSD_CONV_DIM), F32), jax.ShapeDtypeStruct(dproj.shape, dproj.dtype),
                   jax.ShapeDtypeStruct((t, LANES), F32),
                   jax.ShapeDtypeStruct((1, LANES), F32), jax.ShapeDtypeStruct((1, LANES), F32),
                   jax.ShapeDtypeStruct((1, LANES), F32), jax.ShapeDtypeStruct((1, SSD_WIDTH), F32)),
        grid=(nbatch, nc),
        in_specs=[wide,
                  pl.BlockSpec((q, SSD_CONV_DIM), lambda b, c: (rowblk(b, c), 0)),
                  pl.BlockSpec((q, LANES), lambda b, c: (rowblk(b, c), OFF_DT // LANES)),
                  zblk, wide,
                  pl.BlockSpec((SSD_WIDTH, SSD_STATE), lambda b, c: (rowblk(b, c), 0)),
                  vec, vec, wvec, wvec, ANY],
        out_specs=(pl.BlockSpec((q, SSD_CONV_DIM), lambda b, c: (rowblk(b, c), 0)),
                   zblk,
                   pl.BlockSpec((q, LANES), lambda b, c: (rowblk(b, c), 0)),
                   vec, vec, vec, wvec),
        input_output_aliases={10: 1},
        scratch_shapes=[pltpu.VMEM((SSD_WIDTH, SSD_STATE), F32), pltpu.VMEM((SUBLANES, LANES), F32),
                        pltpu.VMEM((SUBLANES, LANES), F32), pltpu.VMEM((SUBLANES, LANES), F32),
                        pltpu.VMEM((SUBLANES, SSD_WIDTH), F32)],
        compiler_params=_params(("arbitrary", "arbitrary")),
    )(dycat, u, proj, proj, y, states, dtb, a_neg, d_x, norm_w, dproj)


def _pad_rows(w, rows):
    return jnp.concatenate([w, jnp.zeros((rows - w.shape[0], w.shape[1]), w.dtype)], axis=0)


def _pad_lanes(v):
    return jnp.concatenate([v, jnp.zeros((LANES - v.shape[0],), v.dtype)]).reshape(1, LANES)


def _padded_from_chips(pieces):
    cols = pieces[0].shape[-1]
    lead = pieces[0].shape[:-1]
    parts, pos = [], 0
    for lo, hi, start in sorted(SECTIONS, key=lambda s: s[2]):
        if start > pos:
            parts.append(jnp.zeros(lead + (start - pos,), pieces[0].dtype))
        pos = start + hi - lo
        while lo < hi:
            p = lo // cols
            end = min(hi, (p + 1) * cols)
            parts.append(pieces[p][..., lo - p * cols:end - p * cols])
            lo = end
    if pos < NP:
        parts.append(jnp.zeros(lead + (NP - pos,), pieces[0].dtype))
    return jnp.concatenate(parts, axis=-1)


def _chip_part_from_padded(wp, p, cols):
    lo, hi = p * cols, (p + 1) * cols
    parts = []
    for rs, re, start in SECTIONS:
        a, b = max(lo, rs), min(hi, re)
        if a < b:
            parts.append(wp[..., start + a - rs:start + b - rs])
    return jnp.concatenate(parts, axis=-1)


def _layer_params(li, w_in_p, w_out, conv_w, dw_w, small):
    return dict(
        w_in_p=w_in_p, w_out=w_out,
        conv_w=_pad_rows(conv_w, SUBLANES), dw_w=_pad_rows(dw_w, 32),
        norm_w=small["norm_w"][li].reshape(1, -1),
        conv_b=small["ssd_conv_b"][li].reshape(1, -1),
        dtb=_pad_lanes(small["ssd_dt_bias"][li]),
        a_neg=_pad_lanes(-jnp.exp(small["ssd_a_log"][li])),
        d_x=jnp.repeat(small["ssd_d"][li], SSD_HEAD_DIM).reshape(1, -1),
        ssd_norm_w=small["ssd_norm_w"][li].reshape(1, -1),
        sinks=_pad_lanes(small["attn_sinks"][li]),
        dw_b=small["conf_dw_b"][li].reshape(1, -1),
        ln_w=small["conf_ln_w"][li].reshape(1, -1),
        ln_b=small["conf_ln_b"][li].reshape(1, -1),
    )


def _layer_fwd(x, p, nbatch, seq, tag, after=None):
    proj, h_t = _proj_fwd(x, p["norm_w"], p["w_in_p"], name=f"proj_fwd_{tag}", after=after)
    u = _conv_fwd(proj, OFF_XBC, SSD_CONV_DIM, p["conv_w"], p["conv_b"], SSD_CONV, seq, name=f"ssd_conv_fwd_{tag}")
    ycat = lax.empty((x.shape[0], MIX_WIDTH), MXU_DTYPE)
    y, states, ycat = _ssd_fwd(u, proj, p["dtb"], p["a_neg"], p["d_x"], p["ssd_norm_w"], ycat, nbatch,
                               name=f"ssd_fwd_{tag}")
    ycat, o, lse = _attn_fwd(proj, p["sinks"], ycat, nbatch, name=f"attn_fwd_{tag}")
    c1, ycat = _conf_fwd(proj, p["dw_w"], p["dw_b"], p["ln_w"], p["ln_b"], ycat, seq, name=f"conf_fwd_{tag}")
    w_out = p["w_out"](ycat) if callable(p["w_out"]) else p["w_out"]
    x_new = _matmul(ycat, w_out, "nn", F32, 1024, 512, 2048, name=f"out_fwd_{tag}", residual=x)
    return x_new, dict(x=x, w_out=w_out, h_t=h_t, proj=proj, u=u, y=y, states=states, o=o, lse=lse, c1=c1, ycat=ycat)


def _layer_bwd(dx_out, p, s, nbatch, seq, tag, hooks=None):
    hooks = hooks or {}
    proj = s["proj"]
    dycat = _matmul(dx_out, s["w_out"], "nt", F32, 1024, 1024, 1024, name=f"out_bwd_dy_{tag}",
                    after=hooks.get("start_token"))
    dw_out = _matmul(s["ycat"], dx_out, "tn", F32, 1024, 1024, 1024, name=f"out_bwd_dw_{tag}")
    token = hooks["after_dycat"](dycat) if "after_dycat" in hooks else None
    dtb = p["dtb"] if token is None else p["dtb"] + token[0, 0]
    dproj = lax.empty(proj.shape, MXU_DTYPE)
    du, dproj, ddt, da_log, dd, ddtb, dssd_norm_w = _ssd_bwd(
        dycat, s["u"], proj, s["y"], s["states"], dtb, p["a_neg"], p["d_x"], p["ssd_norm_w"], dproj,
        nbatch, name=f"ssd_bwd_{tag}")
    dproj, dconv_w, dconv_b = _conv_bwd(du, proj, OFF_XBC, SSD_CONV_DIM, p["conv_w"], SSD_CONV, seq,
                                        name=f"ssd_conv_bwd_{tag}", into=dproj)
    dproj, dsinks = _attn_bwd(dycat, proj, s["o"], s["lse"], p["sinks"], ddt, dproj, nbatch,
                              name=f"attn_bwd_{tag}")
    if "after_attn" in hooks:
        hooks["after_attn"](dproj)
    dproj, ddw_w, ddw_b, dln_w, dln_b = _conf_bwd(dycat, proj, s["c1"], p["dw_w"], p["ln_w"], p["ln_b"], dproj, seq,
                                                  name=f"conf_bwd_{tag}")
    dw_in_p = _matmul(s["h_t"], dproj, "nn", F32, 1024, 512, 4096, name=f"proj_bwd_dw_{tag}")
    token = hooks["after_dw"](dw_in_p, dw_out) if "after_dw" in hooks else None
    norm_w = p["norm_w"] if token is None else p["norm_w"] + token[0, 0]
    dx_in, dnorm_w = _proj_bwd_dx(dproj, p["w_in_p"], s["x"], norm_w, dx_out, name=f"proj_bwd_dx_{tag}")
    grads = dict(
        norm_w=dnorm_w[0], w_in_p=dw_in_p, ssd_conv_w=dconv_w[:SSD_CONV], ssd_conv_b=dconv_b[0],
        ssd_dt_bias=ddtb[0, :SSD_HEADS], ssd_a_log=da_log[0, :SSD_HEADS], ssd_d=dd[0, :SSD_HEADS],
        ssd_norm_w=dssd_norm_w[0], attn_sinks=dsinks[0, :ATTN_Q_HEADS], conf_dw_w=ddw_w[:CONF_KERNEL],
        conf_dw_b=ddw_b[0], conf_ln_w=dln_w[0], conf_ln_b=dln_b[0], w_out=dw_out)
    return dx_in, grads


def _local_step(x, target, param_fns, final_norm_w, first_after=None, bwd_hooks=None):
    nbatch, seq, d = x.shape
    xt = x.reshape(nbatch * seq, d)
    saved, layer_params = [], []
    for li, fn in enumerate(param_fns):
        p = fn(xt)
        layer_params.append(p)
        xt, s = _layer_fwd(xt, p, nbatch, seq, f"l{li}", after=first_after if li == 0 else None)
        saved.append(s)
    loss, dx, dfinal = _loss_head(xt, target.reshape(nbatch * seq, d), final_norm_w.reshape(1, d), name="loss_head")
    grads = [None] * len(layer_params)
    for li in reversed(range(len(layer_params))):
        hooks = bwd_hooks(li) if bwd_hooks is not None else None
        dx, grads[li] = _layer_bwd(dx, layer_params[li], saved[li], nbatch, seq, f"l{li}", hooks=hooks)
    return loss[0, 0], dx.reshape(nbatch, seq, d), grads, dfinal[0]


MESH = pl.DeviceIdType.MESH
N_CHIPS = 4


def _mesh_pos():
    return lax.axis_index("x"), lax.axis_index("y"), lax.axis_index("c")


def _other_chips(x, y):
    return [(1 - x, y), (x, 1 - y), (1 - x, 1 - y)]


def _gather_weights(big, small, name):
    nbig, nsmall = len(big), len(small)
    n_ici = 3 * (nbig + nsmall)
    n_fwd = 3 * nbig

    def body(*refs):
        ins = refs[:nbig + nsmall]
        outs = refs[nbig + nsmall:2 * (nbig + nsmall)]
        send_sems, recv_sems = refs[2 * (nbig + nsmall):]
        x, y, c = _mesh_pos()
        me = 2 * x + y
        sibling = (x, y, 1 - c)
        chips = _other_chips(x, y)

        def ici(a, j, origin, dest):
            if a < nbig:
                src = ins[a].at[c] if origin is None else outs[a].at[origin, c]
                dst = outs[a].at[me if origin is None else origin, c]
            else:
                src = ins[a] if origin is None else outs[a].at[origin]
                dst = outs[a].at[me if origin is None else origin]
            k = a * 3 + j
            return pltpu.make_async_remote_copy(src_ref=src, dst_ref=dst, send_sem=send_sems.at[k],
                                                recv_sem=recv_sems.at[k], device_id=dest, device_id_type=MESH)

        def fwd(a, j, origin, half):
            k = n_ici + a * 3 + j
            ref = outs[a].at[origin, half]
            return pltpu.make_async_remote_copy(src_ref=ref, dst_ref=ref, send_sem=send_sems.at[k],
                                                recv_sem=recv_sems.at[k], device_id=sibling, device_id_type=MESH)

        sends = []
        for j, (px, py) in enumerate(chips):
            for a in range(nbig + nsmall):
                cp = ici(a, j, None, (px, py, c))
                cp.start()
                sends.append(cp)
        for j, (px, py) in enumerate(chips):
            origin = 2 * px + py
            for a in range(nbig):
                ici(a, j, origin, (px, py, c)).wait_recv()
                cp = fwd(a, j, origin, c)
                cp.start()
                sends.append(cp)
        for j, (px, py) in enumerate(chips):
            origin = 2 * px + py
            for a in range(nbig, nbig + nsmall):
                ici(a, j, origin, (px, py, c)).wait_recv()
            for a in range(nbig):
                fwd(a, j, origin, 1 - c).wait_recv()
        for cp in sends:
            cp.wait_send()

    out_shape = tuple(jax.ShapeDtypeStruct((N_CHIPS,) + a.shape, a.dtype) for a in list(big) + list(small))
    return pl.pallas_call(
        body, name=name, out_shape=out_shape,
        in_specs=[ANY] * (nbig + nsmall), out_specs=tuple([ANY] * (nbig + nsmall)),
        scratch_shapes=[pltpu.SemaphoreType.DMA((n_ici + n_fwd,)), pltpu.SemaphoreType.DMA((n_ici + n_fwd,))],
    )(*big, *small)


HBM = pl.BlockSpec(memory_space=pltpu.HBM)
SEM = pl.BlockSpec(memory_space=pltpu.SEMAPHORE)
DATAFLOW = pltpu.SideEffectType.DATAFLOW_SIDE_EFFECTING


def _split_peers(pattern, x, y, c):
    if pattern == "swap":
        return [((x, y, 1 - c), 1 - c, None, None)]
    me = 2 * x + y
    return [((px, py, c), 2 * px + py if pattern == "scatter" else None, me, 2 * px + py)
            for px, py in _other_chips(x, y)]


def _split_land_shape(pattern, shape):
    return {"bcast": (N_CHIPS,) + shape, "scatter": shape, "swap": shape[:1] + shape[2:]}[pattern]


def _split_copies(pattern, srcs, lands, send_sems, recv_sems, waiting):
    x, y, c = _mesh_pos()
    peers = _split_peers(pattern, x, y, c)
    cps = []
    for j, (dev, src_slot, dst_slot, my_slot) in enumerate(peers):
        for a in range(len(srcs)):
            if src_slot is None:
                src = srcs[a]
            else:
                src = srcs[a].at[:, src_slot] if pattern == "swap" else srcs[a].at[src_slot]
            slot = my_slot if waiting else dst_slot
            dst = lands[a] if slot is None else lands[a].at[slot]
            k = a * len(peers) + j
            cps.append(pltpu.make_async_remote_copy(src_ref=src, dst_ref=dst, send_sem=send_sems[k],
                                                    recv_sem=recv_sems[k], device_id=dev, device_id_type=MESH))
    return cps


def _split_start(arrs, pattern, after, name):
    n = len(arrs)
    nsem = n * (1 if pattern == "swap" else N_CHIPS - 1)
    deps = [] if after is None else [after]

    def body(*refs):
        srcs, lands = refs[:n], refs[n:2 * n]
        outs = refs[2 * n + len(deps):]
        for cp in _split_copies(pattern, srcs, lands, outs[:nsem], outs[nsem:2 * nsem], waiting=False):
            cp.start()
        outs[-1][...] = jnp.zeros_like(outs[-1])

    lands = [lax.empty(_split_land_shape(pattern, a.shape), a.dtype) for a in arrs]
    out_shape = ([pltpu.SemaphoreType.DMA(())] * (2 * nsem)
                 + [pltpu.HBM(a.shape, a.dtype) for a in arrs] + [pltpu.HBM(b.shape, b.dtype) for b in lands]
                 + [jax.ShapeDtypeStruct((SUBLANES, LANES), F32)])
    outs = pl.pallas_call(
        body, name=name, out_shape=tuple(out_shape),
        in_specs=[HBM] * (2 * n) + [ANY] * len(deps),
        out_specs=tuple([SEM] * (2 * nsem) + [HBM] * (2 * n) + [pl.BlockSpec(memory_space=pltpu.VMEM)]),
        input_output_aliases={a: 2 * nsem + a for a in range(2 * n)},
        compiler_params=pltpu.CompilerParams(has_side_effects=DATAFLOW),
    )(*[pltpu.with_memory_space_constraint(a, pltpu.HBM) for a in list(arrs) + lands], *deps)
    return outs[:-1], outs[-1]


def _split_wait(state, n, pattern, after, name):
    nsem = n * (1 if pattern == "swap" else N_CHIPS - 1)

    def body(*refs):
        srcs, lands = refs[:n], refs[n:2 * n]
        send_sems, recv_sems = refs[2 * n:2 * n + nsem], refs[2 * n + nsem:2 * n + 2 * nsem]
        for cp in _split_copies(pattern, srcs, lands, send_sems, recv_sems, waiting=True):
            cp.wait_send()
            cp.wait_recv()

    sems, thru = state[:2 * nsem], state[2 * nsem:]
    outs = pl.pallas_call(
        body, name=name, out_shape=tuple(pltpu.HBM(a.shape, a.dtype) for a in thru),
        in_specs=[HBM] * (2 * n) + [SEM] * (2 * nsem) + [ANY],
        out_specs=tuple([HBM] * (2 * n)),
        input_output_aliases={a: a for a in range(2 * n)},
        compiler_params=pltpu.CompilerParams(has_side_effects=DATAFLOW),
    )(*thru, *sems, after)
    return outs[:n], outs[n:]


def _pair_gather(arrs, layer, name):
    n = len(arrs)

    def body(*refs):
        outs = refs[n:2 * n]
        send_sems, recv_sems = refs[2 * n:]
        x, y, c = _mesh_pos()
        cps = [pltpu.make_async_remote_copy(src_ref=outs[a].at[layer, c], dst_ref=outs[a].at[layer, c],
                                            send_sem=send_sems.at[a], recv_sem=recv_sems.at[a],
                                            device_id=(x, y, 1 - c), device_id_type=MESH)
               for a in range(n)]
        for cp in cps:
            cp.start()
        for cp in cps:
            cp.wait()

    return pl.pallas_call(
        body, name=name, out_shape=tuple(jax.ShapeDtypeStruct(a.shape, a.dtype) for a in arrs),
        in_specs=[ANY] * n, out_specs=tuple([ANY] * n),
        input_output_aliases={a: a for a in range(n)},
        scratch_shapes=[pltpu.SemaphoreType.DMA((n,)), pltpu.SemaphoreType.DMA((n,))],
    )(*arrs)


N_DEV = 8


def _allreduce_small(pack, name):
    r = pack.shape[0]

    def body(p_ref, o_ref, land, send_sems, recv_sems):
        x, y, c = _mesh_pos()
        me = 4 * x + 2 * y + c
        cps = []
        for k in range(1, N_DEV):
            peer = (x ^ (k >> 2), y ^ ((k >> 1) & 1), c ^ (k & 1))
            cps.append(pltpu.make_async_remote_copy(src_ref=p_ref, dst_ref=land.at[me], send_sem=send_sems.at[k - 1],
                                                    recv_sem=recv_sems.at[k - 1], device_id=peer, device_id_type=MESH))
        for cp in cps:
            cp.start()
        land[me] = p_ref[...]
        for cp in cps:
            cp.wait()
        total = land[0]
        for d in range(1, N_DEV):
            total = total + land[d]
        o_ref[...] = total

    vm = pl.BlockSpec(memory_space=pltpu.VMEM)
    return pl.pallas_call(
        body, name=name, out_shape=jax.ShapeDtypeStruct(pack.shape, F32),
        in_specs=[vm], out_specs=vm,
        scratch_shapes=[pltpu.VMEM((N_DEV, r, LANES), F32), pltpu.SemaphoreType.DMA((N_DEV - 1,)),
                        pltpu.SemaphoreType.DMA((N_DEV - 1,))],
    )(pack)


BIG_ROWS = 128


def _cast_layer(w, layer, name):
    _, r, cdim = w.shape
    tr = BIG_ROWS

    def body(w_ref, o_ref):
        o_ref[...] = w_ref[...].astype(o_ref.dtype)

    return pl.pallas_call(
        body, name=name, out_shape=jax.ShapeDtypeStruct((r, cdim), MXU_DTYPE),
        grid=(r // tr,), in_specs=[pl.BlockSpec((None, tr, cdim), lambda i: (layer, i, 0))],
        out_specs=pl.BlockSpec((tr, cdim), lambda i: (i, 0)),
        compiler_params=_params(("parallel",)),
    )(w)


def _cast_cols_major(w_t, name):
    cdim, nl, r = w_t.shape
    tc = LANES

    def body(w_ref, *o_refs):
        for l in range(nl):
            o_refs[l][...] = w_ref[:, l, :].T.astype(o_refs[l].dtype)

    out = pl.BlockSpec((r, tc), lambda i: (0, i))
    return pl.pallas_call(
        body, name=name, out_shape=tuple(jax.ShapeDtypeStruct((r, cdim), MXU_DTYPE) for _ in range(nl)),
        grid=(pl.cdiv(cdim, tc),), in_specs=[pl.BlockSpec((tc, nl, r), lambda i: (i, 0, 0))],
        out_specs=tuple([out] * nl),
        compiler_params=_params(("parallel",)),
    )(w_t)


def _pair_sum(parts, sib, which, out_dtype, name):
    k, _, r, cdim = parts.shape
    tr = BIG_ROWS

    def body(sel_ref, p_ref, s_ref, o_ref):
        o_ref[...] = (p_ref[...] + s_ref[...]).astype(o_ref.dtype)

    grid_spec = pltpu.PrefetchScalarGridSpec(
        num_scalar_prefetch=1, grid=(k, r // tr),
        in_specs=[pl.BlockSpec((None, None, tr, cdim), lambda l, i, sel: (l, sel[0], i, 0)),
                  pl.BlockSpec((None, tr, cdim), lambda l, i, sel: (l, i, 0))],
        out_specs=pl.BlockSpec((None, tr, cdim), lambda l, i, sel: (l, i, 0)))
    return pl.pallas_call(
        body, name=name, out_shape=jax.ShapeDtypeStruct((k, r, cdim), out_dtype), grid_spec=grid_spec,
        compiler_params=_params(("parallel", "parallel")),
    )(which.reshape(1).astype(jnp.int32), parts, sib)


def _sum_lead(parts, into, layer, which, name):
    k, r, cdim = parts.shape
    tr = BIG_ROWS

    def body(sel_ref, p_ref, _, o_ref):
        total = p_ref[0].astype(F32)
        for a in range(1, k):
            total = total + p_ref[a].astype(F32)
        o_ref[...] = total

    grid_spec = pltpu.PrefetchScalarGridSpec(
        num_scalar_prefetch=1, grid=(r // tr,),
        in_specs=[pl.BlockSpec((k, tr, cdim), lambda i, sel: (0, i, 0)), ANY],
        out_specs=pl.BlockSpec((None, None, tr, cdim), lambda i, sel: (layer, sel[0], i, 0)))
    return pl.pallas_call(
        body, name=name, out_shape=jax.ShapeDtypeStruct(into.shape, F32), grid_spec=grid_spec,
        input_output_aliases={2: 0},
        compiler_params=_params(("parallel",)),
    )(which.reshape(1).astype(jnp.int32), parts, into)


def _adam_math(w, g, m, v):
    m2 = ADAM_B1 * m + (1.0 - ADAM_B1) * g
    v2 = ADAM_B2 * v + (1.0 - ADAM_B2) * (g * g)
    m_hat = m2 / (1.0 - ADAM_B1 ** ADAM_STEP)
    v_hat = v2 / (1.0 - ADAM_B2 ** ADAM_STEP)
    delta = -ADAM_LR * (m_hat / (jnp.sqrt(v_hat) + ADAM_EPS) + ADAM_WD * w)
    return delta, m2, v2


def _adam_big(w, g, m, v, name):
    nl, r, cdim = w.shape
    tr = BIG_ROWS

    def body(w_ref, g_ref, m_ref, v_ref, d_ref, mo_ref, vo_ref):
        delta, m2, v2 = _adam_math(w_ref[...], g_ref[...], m_ref[...], v_ref[...])
        d_ref[...] = delta
        mo_ref[...] = m2
        vo_ref[...] = v2

    blk = pl.BlockSpec((None, tr, cdim), lambda l, i: (l, i, 0))
    shp = jax.ShapeDtypeStruct(w.shape, F32)
    return pl.pallas_call(
        body, name=name, out_shape=(shp, shp, shp),
        grid=(nl, r // tr), in_specs=[blk] * 4, out_specs=(blk, blk, blk),
        compiler_params=_params(("parallel", "parallel")),
    )(w, g, m, v)


def _adam_cols_major(w, g, m, v, name):
    cdim, nl, r = w.shape
    tc = BIG_ROWS

    def body(w_ref, g_ref, m_ref, v_ref, d_ref, mo_ref, vo_ref):
        delta, m2, v2 = _adam_math(w_ref[...], g_ref[...], m_ref[...], v_ref[...])
        d_ref[...] = delta
        mo_ref[...] = m2
        vo_ref[...] = v2

    blk = pl.BlockSpec((tc, nl, r), lambda i: (i, 0, 0))
    shp = jax.ShapeDtypeStruct(w.shape, F32)
    return pl.pallas_call(
        body, name=name, out_shape=(shp, shp, shp),
        grid=(pl.cdiv(cdim, tc),), in_specs=[blk] * 4, out_specs=(blk, blk, blk),
        compiler_params=_params(("parallel",)),
    )(w, g, m, v)


def _adam_small(ws, gs, ms, vs, name):
    n = len(ws)

    def body(*refs):
        w_refs, g_refs, m_refs, v_refs = (refs[k * n:(k + 1) * n] for k in range(4))
        d_refs, mo_refs, vo_refs = (refs[(4 + k) * n:(5 + k) * n] for k in range(3))
        for a in range(n):
            delta, m2, v2 = _adam_math(w_refs[a][...], g_refs[a][...], m_refs[a][...], v_refs[a][...])
            d_refs[a][...] = delta
            mo_refs[a][...] = m2
            vo_refs[a][...] = v2

    shapes = tuple(jax.ShapeDtypeStruct(w.shape, F32) for w in ws)
    vm = pl.BlockSpec(memory_space=pltpu.VMEM)
    outs = pl.pallas_call(body, name=name, out_shape=shapes * 3, in_specs=[vm] * (4 * n),
                          out_specs=tuple([vm] * (3 * n)))(*ws, *gs, *ms, *vs)
    return outs[:n], outs[n:2 * n], outs[2 * n:]


PACK_TILE = SUBLANES * LANES


def _pack(arrays):
    rows = []
    for a in arrays:
        flat = a.reshape(-1)
        pad = (-flat.shape[0]) % PACK_TILE
        if pad:
            flat = jnp.concatenate([flat, jnp.zeros((pad,), flat.dtype)])
        rows.append(flat.reshape(-1, LANES))
    return jnp.concatenate(rows, axis=0)


def _unpack(pack, shapes):
    outs, row = [], 0
    for shp in shapes:
        n = int(np.prod(shp))
        nrows = -(-n // PACK_TILE) * SUBLANES
        outs.append(pack[row:row + nrows].reshape(-1)[:n].reshape(shp))
        row += nrows
    return outs


SMALL = ["norm_w", "ssd_conv_b", "ssd_dt_bias", "ssd_a_log", "ssd_d", "ssd_norm_w", "attn_sinks",
         "conf_dw_b", "conf_ln_w", "conf_ln_b"]
WEIGHTS = ["norm_w", "w_in", "ssd_conv_w", "ssd_conv_b", "ssd_dt_bias", "ssd_a_log", "ssd_d", "ssd_norm_w",
           "attn_sinks", "conf_dw_w", "conf_dw_b", "conf_ln_w", "conf_ln_b", "w_out", "final_norm_w"]


def kernel(x, norm_w, w_in, ssd_conv_w, ssd_conv_b, ssd_dt_bias, ssd_a_log, ssd_d, ssd_norm_w, attn_sinks, conf_dw_w, conf_dw_b, conf_ln_w, conf_ln_b, w_out, final_norm_w, loss_target, m_norm_w, m_w_in, m_ssd_conv_w, m_ssd_conv_b, m_ssd_dt_bias, m_ssd_a_log, m_ssd_d, m_ssd_norm_w, m_attn_sinks, m_conf_dw_w, m_conf_dw_b, m_conf_ln_w, m_conf_ln_b, m_w_out, m_final_norm_w, v_norm_w, v_w_in, v_ssd_conv_w, v_ssd_conv_b, v_ssd_dt_bias, v_ssd_a_log, v_ssd_d, v_ssd_norm_w, v_attn_sinks, v_conf_dw_w, v_conf_dw_b, v_conf_ln_w, v_conf_ln_b, v_w_out, v_final_norm_w):
    w = dict(norm_w=norm_w, w_in=w_in, ssd_conv_w=ssd_conv_w, ssd_conv_b=ssd_conv_b, ssd_dt_bias=ssd_dt_bias,
             ssd_a_log=ssd_a_log, ssd_d=ssd_d, ssd_norm_w=ssd_norm_w, attn_sinks=attn_sinks, conf_dw_w=conf_dw_w,
             conf_dw_b=conf_dw_b, conf_ln_w=conf_ln_w, conf_ln_b=conf_ln_b, w_out=w_out, final_norm_w=final_norm_w)
    m = dict(norm_w=m_norm_w, w_in=m_w_in, ssd_conv_w=m_ssd_conv_w, ssd_conv_b=m_ssd_conv_b,
             ssd_dt_bias=m_ssd_dt_bias, ssd_a_log=m_ssd_a_log, ssd_d=m_ssd_d, ssd_norm_w=m_ssd_norm_w,
             attn_sinks=m_attn_sinks, conf_dw_w=m_conf_dw_w, conf_dw_b=m_conf_dw_b, conf_ln_w=m_conf_ln_w,
             conf_ln_b=m_conf_ln_b, w_out=m_w_out, final_norm_w=m_final_norm_w)
    v = dict(norm_w=v_norm_w, w_in=v_w_in, ssd_conv_w=v_ssd_conv_w, ssd_conv_b=v_ssd_conv_b,
             ssd_dt_bias=v_ssd_dt_bias, ssd_a_log=v_ssd_a_log, ssd_d=v_ssd_d, ssd_norm_w=v_ssd_norm_w,
             attn_sinks=v_attn_sinks, conf_dw_w=v_conf_dw_w, conf_dw_b=v_conf_dw_b, conf_ln_w=v_conf_ln_w,
             conf_ln_b=v_conf_ln_b, w_out=v_w_out, final_norm_w=v_final_norm_w)
    depth = w_in.shape[0]
    me = 2 * lax.axis_index("x") + lax.axis_index("y")

    assert depth == 2
    w_in_t = jnp.transpose(w_in, (2, 0, 1))
    w_in_b = _cast_cols_major(w_in_t, name="cast_w_in")
    w_out_b = [_cast_layer(w_out, li, name=f"cast_w_out_l{li}") for li in range(depth)]
    own0 = [w_in_b[0].reshape((2, -1) + w_in_b[0].shape[1:]), ssd_conv_w, conf_dw_w]
    gathered0 = _gather_weights(own0[:1], own0[1:], name="gather_weights_l0")
    g_in0, g_conv, g_dw = [lax.dynamic_update_index_in_dim(g_all, mine, me, 0)
                           for g_all, mine in zip(gathered0, own0)]
    own1 = [w_out_b[0], w_in_b[1], w_out_b[1]]
    pending1, token1 = _split_start(own1, "bcast", gathered0[0], name="gather_rest_start")
    rest = {}

    def small_full(li):
        return (jnp.concatenate([g_conv[p, li] for p in range(N_CHIPS)], axis=1),
                jnp.concatenate([g_dw[p, li] for p in range(N_CHIPS)], axis=1))

    def w_out_l0(after):
        mine1, landed = _split_wait(pending1, len(own1), "bcast", after, name="gather_rest_wait")
        rest["landed"] = [lax.dynamic_update_index_in_dim(g_all, mine, me, 0) for g_all, mine in zip(landed, mine1)]
        return rest["landed"][0].reshape(-1, w_out.shape[2])

    def params_l0(_):
        w_in_p = _padded_from_chips([g_in0[p].reshape(w_in_b[0].shape) for p in range(N_CHIPS)])
        return _layer_params(0, w_in_p, w_out_l0, *small_full(0), w)

    def params_l1(_):
        _, g_in1, g_out1 = rest["landed"]
        w_in_p = _padded_from_chips([g_in1[p] for p in range(N_CHIPS)])
        return _layer_params(1, w_in_p, g_out1.reshape(-1, g_out1.shape[-1]), *small_full(1), w)

    c = lax.axis_index("c")
    cols = w_in.shape[2]
    rows_out = w_out.shape[1]

    def grad_parts(g):
        dw = g["w_in_p"]
        return [dw.reshape(1, 2, dw.shape[0] // 2, dw.shape[1]),
                g["w_out"].reshape(N_CHIPS, 2, rows_out // 2, D_MODEL)]

    def pair_sums(parts, sib, tag):
        s_in, s_out = [_pair_sum(p, sb, c, MXU_DTYPE, name=f"grad_pair_sum_{k}_{tag}")
                       for k, (p, sb) in enumerate(zip(parts, sib))]
        return [jnp.stack([_chip_part_from_padded(s_in[0], p, cols) for p in range(N_CHIPS)]), s_out]

    split = {"reduced": [lax.empty((depth, 2, w_in.shape[1] // 2, cols), F32),
                         lax.empty((depth, 2, rows_out // 2, D_MODEL), F32)]}

    def chip_sums(landed, sent, li):
        filled = [lax.dynamic_update_index_in_dim(r, lax.dynamic_index_in_dim(sk, me, 0, keepdims=False), me, 0)
                  for r, sk in zip(landed, sent)]
        halves = [_sum_lead(r, into, li, c, name=f"grad_chip_sum_{k}_l{li}")
                  for k, (r, into) in enumerate(zip(filled, split["reduced"]))]
        split["reduced"] = list(_pair_gather(halves, li, name=f"grad_pair_gather_l{li}"))

    def bwd_hooks(li):
        def after_dw(dw_in_p, dw_out):
            parts = grad_parts(dict(w_in_p=dw_in_p, w_out=dw_out))
            state, token = _split_start(parts, "swap", None, name=f"grad_swap_l{li}_start")
            split[f"swap{li}"] = (parts, state)
            return token

        hooks = {"after_dw": after_dw}
        if li == depth - 2:
            parts, swap_state = split[f"swap{depth - 1}"]

            def after_dycat(dycat):
                mine, sib = _split_wait(swap_state, len(parts), "swap", dycat, name="grad_swap_l1_wait")
                sent = pair_sums(mine, sib, "l1")
                split["scatter"], token = _split_start(sent, "scatter", None, name="grad_scatter_l1_start")
                return token

            def after_attn(dproj):
                sent, landed = _split_wait(split["scatter"], len(parts), "scatter", dproj,
                                           name="grad_scatter_l1_wait")
                chip_sums(landed, sent, depth - 1)

            hooks.update(after_dycat=after_dycat, after_attn=after_attn)
        return hooks

    loss, grad_x, grads, dfinal = _local_step(x, loss_target, [params_l0, params_l1], final_norm_w,
                                              first_after=token1, bwd_hooks=bwd_hooks)

    parts0, swap0 = split["swap0"]
    sent0 = pair_sums(*_split_wait(swap0, len(parts0), "swap", grad_x, name="grad_swap_l0_wait"), "l0")
    scatter0, token0 = _split_start(sent0, "scatter", None, name="grad_scatter_l0_start")

    small_list = [grads[li][n] for li in range(depth) for n in SMALL]
    small_list += [grads[li][n] for li in range(depth) for n in ("ssd_conv_w", "conf_dw_w")]
    small_list += [dfinal, loss.reshape(1)]
    small_shapes = [a.shape for a in small_list]
    reduced = _unpack(_allreduce_small(_pack(small_list) + token0[0, 0], name="allreduce_small"), small_shapes)
    ns = len(SMALL)
    g = {n: jnp.stack([reduced[li * ns + i] for li in range(depth)]) for i, n in enumerate(SMALL)}
    conv_w_cols, dw_w_cols = ssd_conv_w.shape[2], conf_dw_w.shape[2]
    g["ssd_conv_w"] = jnp.stack([lax.dynamic_slice_in_dim(reduced[depth * ns + 2 * li], me * conv_w_cols,
                                                          conv_w_cols, axis=1) for li in range(depth)])
    g["conf_dw_w"] = jnp.stack([lax.dynamic_slice_in_dim(reduced[depth * ns + 2 * li + 1], me * dw_w_cols,
                                                         dw_w_cols, axis=1) for li in range(depth)])
    g["final_norm_w"] = reduced[-2]
    loss_total = reduced[-1][0]

    small_names = [n for n in WEIGHTS if n not in ("w_in", "w_out")]

    def as2d(a):
        return a.reshape(1, -1) if a.ndim == 1 else a

    deltas, new_ms, new_vs = _adam_small(*[[as2d(src[n]) for n in small_names] for src in (w, g, m, v)],
                                         name="adam_small")

    sent0, landed0 = _split_wait(scatter0, len(sent0), "scatter", deltas[0], name="grad_scatter_l0_wait")
    chip_sums(landed0, sent0, 0)
    g_w_in = split["reduced"][0].reshape(w_in.shape)
    g_w_out = split["reduced"][1].reshape(w_out.shape)

    outs_g, outs_d, outs_m, outs_v = {"w_in": g_w_in, "w_out": g_w_out}, {}, {}, {}
    to_cols, from_cols = (2, 0, 1), (1, 2, 0)
    outs_d["w_in"], outs_m["w_in"], outs_v["w_in"] = [
        jnp.transpose(a, from_cols) for a in _adam_cols_major(
            *[jnp.transpose(a, to_cols) for a in (w_in, g_w_in, m_w_in, v_w_in)], name="adam_w_in")]
    outs_d["w_out"], outs_m["w_out"], outs_v["w_out"] = _adam_big(w_out, g_w_out, m_w_out, v_w_out,
                                                                  name="adam_w_out")
    for n, dn, mn, vn in zip(small_names, deltas, new_ms, new_vs):
        outs_g[n], outs_d[n], outs_m[n], outs_v[n] = (g[n], dn.reshape(w[n].shape), mn.reshape(w[n].shape),
                                                      vn.reshape(w[n].shape))
    return (loss_total, grad_x, *[outs_g[n] for n in WEIGHTS], *[outs_d[n] for n in WEIGHTS],
            *[outs_m[n] for n in WEIGHTS], *[outs_v[n] for n in WEIGHTS])
```

```python
import functools
import math

import jax
import jax.numpy as jnp
import numpy as np
from jax import lax
from jax.experimental import pallas as pl
from jax.experimental.pallas import tpu as pltpu

F32 = jnp.float32
BF16 = jnp.bfloat16
MXU_DTYPE = BF16

D_MODEL = 1024
DEPTH = 2
SSD_HEADS = 16
SSD_HEAD_DIM = 64
SSD_STATE = 128
SSD_CONV = 4
CHUNK = 128
SSD_CONV_DIM = 1536
ATTN_HEAD_DIM = 64
ATTN_Q_HEADS = 8
WINDOW = 128
CONF_WIDTH = 512
CONF_KERNEL = 31
MIX_WIDTH = 2048
D_IN_PROJ = 5392
EPS = 1e-5

ADAM_LR = 0.001
ADAM_B1 = 0.9
ADAM_B2 = 0.999
ADAM_EPS = 1e-08
ADAM_WD = 0.01
ADAM_STEP = 10

LANES = 128
SUBLANES = 8
VMEM_LIMIT = 48 * 1024 * 1024

NP = 5632
OFF_ZA, OFF_Q, OFF_K, OFF_V, OFF_DT = 0, 512, 1024, 1152, 1280
ATTN_GROUP = 1536
OFF_CONF, OFF_ZC = 1536, 2560
CONF_GROUP = 1536
OFF_ZS = 3072
OFF_XBC = 4096
SECTIONS = ((0, 1024, OFF_ZS), (1024, 1536, OFF_ZA), (1536, 2048, OFF_ZC), (2048, 3584, OFF_XBC),
            (3584, 3600, OFF_DT), (3600, 4368, OFF_Q), (4368, 5392, OFF_CONF))

YCAT_ATTN, YCAT_CONF = 1024, 1536
ANY = pl.BlockSpec(memory_space=pl.ANY)

NN = (((1,), (0,)), ((), ()))
NT = (((1,), (1,)), ((), ()))
TN = (((0,), (0,)), ((), ()))


def _params(sem):
    return pltpu.CompilerParams(dimension_semantics=sem, vmem_limit_bytes=VMEM_LIMIT)


def _dot(a, b, dims=NN):
    return lax.dot_general(a.astype(MXU_DTYPE), b.astype(MXU_DTYPE), dims, preferred_element_type=F32)


def _split_bf16(a, passes):
    pieces = []
    r = a
    for _ in range(passes):
        p = r.astype(BF16)
        pieces.append(p)
        r = r - p.astype(F32)
    return pieces


def _xdot(a, sel, dims=NN, passes=2):
    out = None
    for p in _split_bf16(a, passes):
        t = lax.dot_general(p, sel, dims, preferred_element_type=F32)
        out = t if out is None else out + t
    return out


def _xdot_r(sel, b, dims=NN, passes=3):
    out = None
    for p in _split_bf16(b, passes):
        t = lax.dot_general(sel, p, dims, preferred_element_type=F32)
        out = t if out is None else out + t
    return out


def _sigmoid(x):
    return 1.0 / (1.0 + jnp.exp(-x))


def _silu(x):
    return x * _sigmoid(x)


def _dsilu(x):
    s = _sigmoid(x)
    return s * (1.0 + x * (1.0 - s))


def _softplus(x):
    return jnp.maximum(x, 0.0) + jnp.log(1.0 + jnp.exp(-jnp.abs(x)))


def _rowsum8(x):
    r, c = x.shape
    return jnp.sum(x.reshape(r // SUBLANES, SUBLANES, c), axis=0)


def _iota(shape, dim):
    return lax.broadcasted_iota(jnp.int32, shape, dim)


def _matmul(a, b, form, out_dtype, tm, tn, tk, name, residual=None, after=None):
    if form == "nn":
        (m, k), n = a.shape, b.shape[1]
    elif form == "nt":
        (m, k), n = a.shape, b.shape[0]
    else:
        (k, m), n = a.shape, b.shape[1]
    tm, tn, tk = min(tm, m), min(tn, n), min(tk, k)
    assert m % tm == 0 and n % tn == 0 and k % tk == 0, (name, m, n, k, tm, tn, tk)
    if form == "nn":
        a_spec = pl.BlockSpec((tm, tk), lambda i, j, s: (i, s))
        b_spec = pl.BlockSpec((tk, tn), lambda i, j, s: (s, j))
        dims = NN
    elif form == "nt":
        (m, k), n = a.shape, b.shape[0]
        a_spec = pl.BlockSpec((tm, tk), lambda i, j, s: (i, s))
        b_spec = pl.BlockSpec((tn, tk), lambda i, j, s: (j, s))
        dims = NT
    else:
        (k, m), n = a.shape, b.shape[1]
        a_spec = pl.BlockSpec((tk, tm), lambda i, j, s: (s, i))
        b_spec = pl.BlockSpec((tk, tn), lambda i, j, s: (s, j))
        dims = TN
    nk = k // tk
    has_res = residual is not None
    deps = [] if after is None else [after]

    def body_single(a_ref, b_ref, *rest):
        o = _dot(a_ref[...], b_ref[...], dims)
        if has_res:
            o = o + rest[0][...]
        rest[-1][...] = o.astype(out_dtype)

    def body(a_ref, b_ref, *rest):
        r_ref = rest[0] if has_res else None
        o_ref, acc = rest[-2:]
        s = pl.program_id(2)

        @pl.when(s == 0)
        def _():
            acc[...] = jnp.zeros_like(acc)

        acc[...] += _dot(a_ref[...], b_ref[...], dims)

        @pl.when(s == nk - 1)
        def _():
            o = acc[...]
            if has_res:
                o = o + r_ref[...]
            o_ref[...] = o.astype(out_dtype)

    in_specs = [a_spec, b_spec]
    args = [a, b]
    if has_res:
        in_specs.append(pl.BlockSpec((tm, tn), lambda i, j, s: (i, j)))
        args.append(residual)
    in_specs += [ANY] * len(deps)
    args += deps
    return pl.pallas_call(
        body_single if nk == 1 else body, name=name,
        out_shape=jax.ShapeDtypeStruct((m, n), out_dtype),
        grid=(m // tm, n // tn, nk),
        in_specs=in_specs,
        out_specs=pl.BlockSpec((tm, tn), lambda i, j, s: (i, j)),
        scratch_shapes=[] if nk == 1 else [pltpu.VMEM((tm, tn), F32)],
        compiler_params=_params(("parallel", "parallel", "arbitrary")),
    )(*args)


ROW_TILE = 256


PROJ_FWD_TM, PROJ_FWD_TN = 1024, 512


def _proj_fwd(x, w, w_in_p, name, after=None):
    t, d = x.shape
    n = w_in_p.shape[1]
    tm, tn = min(PROJ_FWD_TM, t), PROJ_FWD_TN
    assert t % tm == 0 and n % tn == 0
    deps = [] if after is None else [after]

    def body(x_ref, w_ref, b_ref, *rest):
        o_ref, ot_ref, h_scr = rest[len(deps):]

        @pl.when(pl.program_id(1) == 0)
        def _():
            xv = x_ref[...]
            rstd = lax.rsqrt(jnp.mean(xv * xv, axis=-1, keepdims=True) + EPS)
            h = xv * rstd * w_ref[...]
            h_scr[...] = h.astype(h_scr.dtype)
            ot_ref[...] = h.T.astype(ot_ref.dtype)

        o_ref[...] = _dot(h_scr[...], b_ref[...])

    return pl.pallas_call(
        body, name=name,
        out_shape=(jax.ShapeDtypeStruct((t, n), F32), jax.ShapeDtypeStruct((d, t), MXU_DTYPE)),
        grid=(t // tm, n // tn),
        in_specs=[pl.BlockSpec((tm, d), lambda i, j: (i, 0)), pl.BlockSpec((1, d), lambda i, j: (0, 0)),
                  pl.BlockSpec((d, tn), lambda i, j: (0, j))] + [ANY] * len(deps),
        out_specs=(pl.BlockSpec((tm, tn), lambda i, j: (i, j)), pl.BlockSpec((d, tm), lambda i, j: (0, i))),
        scratch_shapes=[pltpu.VMEM((tm, d), MXU_DTYPE)],
        compiler_params=_params(("parallel", "arbitrary")),
    )(x, w, w_in_p, *deps)


PROJ_BWD_TM, PROJ_BWD_TK = 1024, 1408


def _proj_bwd_dx(dproj, w_in_p, x, w, dres, name):
    t, d = x.shape
    kdim = dproj.shape[1]
    tm, tk = min(PROJ_BWD_TM, t), PROJ_BWD_TK
    nt, nk = t // tm, kdim // tk
    assert t % tm == 0 and kdim % tk == 0

    def body(a_ref, b_ref, x_ref, w_ref, dr_ref, dx_ref, dw_ref, acc, wacc):
        i, s = pl.program_id(0), pl.program_id(1)

        @pl.when((i == 0) & (s == 0))
        def _():
            wacc[...] = jnp.zeros_like(wacc)

        @pl.when(s == 0)
        def _():
            acc[...] = jnp.zeros_like(acc)

        acc[...] += _dot(a_ref[...], b_ref[...], NT)

        @pl.when(s == nk - 1)
        def _():
            xv = x_ref[...]
            rstd = lax.rsqrt(jnp.mean(xv * xv, axis=-1, keepdims=True) + EPS)
            xh = xv * rstd
            dhv = acc[...]
            g = dhv * w_ref[...]
            dx_ref[...] = dr_ref[...] + rstd * (g - xh * jnp.mean(g * xh, axis=-1, keepdims=True))
            wacc[...] += _rowsum8(dhv * xh)

        @pl.when((i == nt - 1) & (s == nk - 1))
        def _():
            dw_ref[...] = jnp.sum(wacc[...], axis=0, keepdims=True)

    row = pl.BlockSpec((tm, d), lambda i, s: (i, 0))
    vec = pl.BlockSpec((1, d), lambda i, s: (0, 0))
    return pl.pallas_call(
        body, name=name,
        out_shape=(jax.ShapeDtypeStruct((t, d), F32), jax.ShapeDtypeStruct((1, d), F32)),
        grid=(nt, nk),
        in_specs=[pl.BlockSpec((tm, tk), lambda i, s: (i, s)), pl.BlockSpec((d, tk), lambda i, s: (0, s)),
                  row, vec, row],
        out_specs=(row, vec),
        scratch_shapes=[pltpu.VMEM((tm, d), F32), pltpu.VMEM((SUBLANES, d), F32)],
        compiler_params=_params(("arbitrary", "arbitrary")),
    )(dproj, w_in_p, x, w, dres)


def _loss_head(xf, target, w, name):
    t, d = xf.shape
    tm = ROW_TILE
    nt = t // tm

    def body(x_ref, t_ref, w_ref, loss_ref, dx_ref, dw_ref, lacc, wacc):
        i = pl.program_id(0)

        @pl.when(i == 0)
        def _():
            lacc[...] = jnp.zeros_like(lacc)
            wacc[...] = jnp.zeros_like(wacc)

        xv = x_ref[...]
        rstd = lax.rsqrt(jnp.mean(xv * xv, axis=-1, keepdims=True) + EPS)
        xh = xv * rstd
        err = xh * w_ref[...] - t_ref[...]
        lacc[...] += jnp.sum(err * err)
        dy = err * (1.0 / d)
        g = dy * w_ref[...]
        dx_ref[...] = rstd * (g - xh * jnp.mean(g * xh, axis=-1, keepdims=True))
        wacc[...] += _rowsum8(dy * xh)

        @pl.when(i == nt - 1)
        def _():
            loss_ref[...] = lacc[...] * (0.5 / d)
            dw_ref[...] = jnp.sum(wacc[...], axis=0, keepdims=True)

    row = pl.BlockSpec((tm, d), lambda i: (i, 0))
    vec = pl.BlockSpec((1, d), lambda i: (0, 0))
    return pl.pallas_call(
        body, name=name,
        out_shape=(jax.ShapeDtypeStruct((SUBLANES, LANES), F32), jax.ShapeDtypeStruct((t, d), F32),
                   jax.ShapeDtypeStruct((1, d), F32)),
        grid=(nt,),
        in_specs=[row, row, vec],
        out_specs=(pl.BlockSpec((SUBLANES, LANES), lambda i: (0, 0)), row, vec),
        scratch_shapes=[pltpu.VMEM((SUBLANES, LANES), F32), pltpu.VMEM((SUBLANES, d), F32)],
        compiler_params=_params(("arbitrary",)),
    )(xf, target, w)


CONV_TILE = 512
CONV_COLS = 512
CONV_SUB_ROWS = 128
CONV_SUB_COLS = LANES


def _conv_halo(k):
    return SUBLANES if k - 1 <= SUBLANES else 32


def _conv_subtiles(tm, cw):
    return [(r0, c0) for r0 in range(0, tm, CONV_SUB_ROWS) for c0 in range(0, cw, CONV_SUB_COLS)]


def _conv_use_shifted(k):
    return k > SUBLANES


def _conv_shift_scratch(k, rows, cw):
    return [pltpu.VMEM((SUBLANES - 1, rows - SUBLANES, cw), F32)] if _conv_use_shifted(k) else []


def _conv_fill_shifted(ext, sh):
    n = sh.shape[1]
    for b in range(1, SUBLANES):
        sh[b - 1] = ext[b:b + n, :]


def _conv_rows(ext, sh, start, rows, cs):
    b = start % SUBLANES
    if b == 0 or not sh:
        return ext[start:start + rows, cs]
    return sh[0][b - 1, start - b:start - b + rows, cs]


def _conv_fwd(src, col0, width, w, bias, k, seq, name):
    t = src.shape[0]
    tm, cw, halo = CONV_TILE, CONV_COLS, _conv_halo(k)
    sr, sc = CONV_SUB_ROWS, CONV_SUB_COLS
    p = k - 1
    cb0 = col0 // cw
    kp = w.shape[0]

    shifted = _conv_use_shifted(k)

    def body(x_ref, h_ref, w_ref, b_ref, o_ref, ext, *sh):
        i = pl.program_id(0)
        seq_start = (i * tm) % seq == 0
        ext[halo:, :] = x_ref[...]
        ext[:halo, :] = jnp.where(seq_start, 0.0, h_ref[...])
        if shifted:
            _conv_fill_shifted(ext, sh[0])
        for r0, c0 in _conv_subtiles(tm, cw):
            cs = slice(c0, c0 + sc)
            acc = jnp.zeros((sr, sc), F32) + b_ref[:, cs]
            for j in range(k):
                acc = acc + w_ref[j:j + 1, cs] * _conv_rows(ext, sh, r0 + halo - p + j, sr, cs)
            o_ref[r0:r0 + sr, cs] = acc

    return pl.pallas_call(
        body, name=name,
        out_shape=jax.ShapeDtypeStruct((t, width), F32),
        grid=(t // tm, width // cw),
        in_specs=[pl.BlockSpec((tm, cw), lambda i, j: (i, cb0 + j)),
                  pl.BlockSpec((halo, cw), lambda i, j: (jnp.maximum(i * (tm // halo) - 1, 0), cb0 + j)),
                  pl.BlockSpec((kp, cw), lambda i, j: (0, j)),
                  pl.BlockSpec((1, cw), lambda i, j: (0, j))],
        out_specs=pl.BlockSpec((tm, cw), lambda i, j: (i, j)),
        scratch_shapes=[pltpu.VMEM((halo + tm, cw), F32)] + _conv_shift_scratch(k, halo + tm, cw),
        compiler_params=_params(("parallel", "parallel")),
    )(src, src, w, bias)


def _conv_bwd(dy, src, col0, width, w, k, seq, name, into=None):
    t = src.shape[0]
    tm, cw, halo = CONV_TILE, CONV_COLS, _conv_halo(k)
    sr, sc = CONV_SUB_ROWS, CONV_SUB_COLS
    p = k - 1
    cb0 = col0 // cw
    kp = w.shape[0]
    nt = t // tm
    last_halo = t // halo - 1

    shifted = _conv_use_shifted(k)

    def body(dy_ref, dn_ref, x_ref, xp_ref, w_ref, *rest):
        if into is not None:
            rest = rest[1:]
        dx_ref, dw_ref, db_ref, dyext, xext, wacc, bacc = rest[:7]
        sh = rest[7:]
        i = pl.program_id(1)
        dysh, xsh = (sh[:1], sh[1:]) if shifted else ((), ())

        @pl.when(i == 0)
        def _():
            wacc[...] = jnp.zeros_like(wacc)
            bacc[...] = jnp.zeros_like(bacc)

        seq_start = (i * tm) % seq == 0
        seq_end = ((i + 1) * tm) % seq == 0
        dyext[:tm, :] = dy_ref[...]
        dyext[tm:, :] = jnp.where(seq_end, 0.0, dn_ref[...])
        xext[halo:, :] = x_ref[...]
        xext[:halo, :] = jnp.where(seq_start, 0.0, xp_ref[...])
        if shifted:
            _conv_fill_shifted(dyext, dysh[0])
            _conv_fill_shifted(xext, xsh[0])
        for r0, c0 in _conv_subtiles(tm, cw):
            cs = slice(c0, c0 + sc)
            dyv = dy_ref[r0:r0 + sr, cs]
            acc = jnp.zeros((sr, sc), F32)
            for j in range(k):
                acc = acc + w_ref[j:j + 1, cs] * _conv_rows(dyext, dysh, r0 + p - j, sr, cs)
                wacc[j, :, cs] += _rowsum8(dyv * _conv_rows(xext, xsh, r0 + halo - p + j, sr, cs))
            dx_ref[r0:r0 + sr, cs] = acc.astype(dx_ref.dtype)
            bacc[:, cs] += _rowsum8(dyv)

        @pl.when(i == nt - 1)
        def _():
            dw_ref[...] = jnp.zeros_like(dw_ref)
            for j in range(k):
                dw_ref[j:j + 1, :] = jnp.sum(wacc[j], axis=0, keepdims=True)
            db_ref[...] = jnp.sum(bacc[...], axis=0, keepdims=True)

    if into is None:
        dx_shape = jax.ShapeDtypeStruct((t, width), F32)
        dx_spec = pl.BlockSpec((tm, cw), lambda j, i: (i, j))
        extra_specs, extra_args, aliases = [], [], {}
    else:
        dx_shape = jax.ShapeDtypeStruct(into.shape, into.dtype)
        dx_spec = pl.BlockSpec((tm, cw), lambda j, i: (i, cb0 + j))
        extra_specs, extra_args, aliases = [ANY], [into], {5: 0}
    return pl.pallas_call(
        body, name=name,
        out_shape=(dx_shape, jax.ShapeDtypeStruct((kp, width), F32), jax.ShapeDtypeStruct((1, width), F32)),
        grid=(width // cw, nt),
        in_specs=[pl.BlockSpec((tm, cw), lambda j, i: (i, j)),
                  pl.BlockSpec((halo, cw), lambda j, i: (jnp.minimum((i + 1) * (tm // halo), last_halo), j)),
                  pl.BlockSpec((tm, cw), lambda j, i: (i, cb0 + j)),
                  pl.BlockSpec((halo, cw), lambda j, i: (jnp.maximum(i * (tm // halo) - 1, 0), cb0 + j)),
                  pl.BlockSpec((kp, cw), lambda j, i: (0, j))] + extra_specs,
        out_specs=(dx_spec,
                   pl.BlockSpec((kp, cw), lambda j, i: (0, j)),
                   pl.BlockSpec((1, cw), lambda j, i: (0, j))),
        input_output_aliases=aliases,
        scratch_shapes=[pltpu.VMEM((tm + halo, cw), F32), pltpu.VMEM((halo + tm, cw), F32),
                        pltpu.VMEM((kp, SUBLANES, cw), F32), pltpu.VMEM((SUBLANES, cw), F32)]
        + 2 * _conv_shift_scratch(k, halo + tm, cw),
        compiler_params=_params(("parallel", "arbitrary")),
    )(dy, dy, src, src, w, *extra_args)


def _conf_specs(tm, cw, halo, order):
    cb = OFF_CONF // cw

    def blk(col):
        return pl.BlockSpec((tm, cw), lambda *g: (order(*g), col))

    def prev(col):
        return pl.BlockSpec((halo, cw), lambda *g: (jnp.maximum(order(*g) * (tm // halo) - 1, 0), col))

    return blk(cb), prev(cb), blk(cb + 1), prev(cb + 1)


def _glu_window(ext, a_ref, ah_ref, g_ref, gh_ref, seq_start, halo):
    ext[halo:, :] = a_ref[...] * _sigmoid(g_ref[...])
    ext[:halo, :] = jnp.where(seq_start, 0.0, ah_ref[...] * _sigmoid(gh_ref[...]))


def _conf_fwd(proj, w, bias, ln_w, ln_b, ycat, seq, name):
    t = proj.shape[0]
    k = CONF_KERNEL
    tm, cw, halo = CONV_TILE, CONF_WIDTH, _conv_halo(k)
    sr, sc = CONV_SUB_ROWS, CONV_SUB_COLS
    p = k - 1
    kp = w.shape[0]

    def body(a_ref, ah_ref, g_ref, gh_ref, z_ref, w_ref, b_ref, lw_ref, lb_ref, _, c1_ref, y_ref, ext, sh):
        i = pl.program_id(0)
        _glu_window(ext, a_ref, ah_ref, g_ref, gh_ref, (i * tm) % seq == 0, halo)
        _conv_fill_shifted(ext, sh)
        for r0, c0 in _conv_subtiles(tm, cw):
            cs = slice(c0, c0 + sc)
            acc = jnp.zeros((sr, sc), F32) + b_ref[:, cs]
            for j in range(k):
                acc = acc + w_ref[j:j + 1, cs] * _conv_rows(ext, (sh,), r0 + halo - p + j, sr, cs)
            c1_ref[r0:r0 + sr, cs] = acc
        for r0 in range(0, tm, sr):
            rows = slice(r0, r0 + sr)
            cv = c1_ref[rows, :]
            xc = cv - jnp.mean(cv, axis=-1, keepdims=True)
            rstd = lax.rsqrt(jnp.mean(xc * xc, axis=-1, keepdims=True) + EPS)
            c2 = xc * rstd * lw_ref[...] + lb_ref[...]
            y_ref[rows, :] = (_silu(c2) * _silu(z_ref[rows, :])).astype(y_ref.dtype)

    vec = pl.BlockSpec((1, cw), lambda i: (0, 0))
    row = pl.BlockSpec((tm, cw), lambda i: (i, 0))
    return pl.pallas_call(
        body, name=name,
        out_shape=(jax.ShapeDtypeStruct((t, cw), F32), jax.ShapeDtypeStruct(ycat.shape, ycat.dtype)),
        grid=(t // tm,),
        in_specs=[*_conf_specs(tm, cw, halo, lambda i: i),
                  pl.BlockSpec((tm, cw), lambda i: (i, OFF_ZC // cw)),
                  pl.BlockSpec((kp, cw), lambda i: (0, 0)), vec, vec, vec, ANY],
        out_specs=(row, pl.BlockSpec((tm, cw), lambda i: (i, YCAT_CONF // cw))),
        input_output_aliases={9: 1},
        scratch_shapes=[pltpu.VMEM((halo + tm, cw), F32)] + _conv_shift_scratch(k, halo + tm, cw),
        compiler_params=_params(("parallel",)),
    )(proj, proj, proj, proj, proj, w, bias, ln_w, ln_b, ycat)


def _conf_bwd(dycat, proj, c1, w, ln_w, ln_b, dproj, seq, name):
    t = proj.shape[0]
    k = CONF_KERNEL
    tm, cw, halo = CONV_TILE, CONF_WIDTH, _conv_halo(k)
    sr, sc = CONV_SUB_ROWS, CONV_SUB_COLS
    p = k - 1
    kp = w.shape[0]
    nt = t // tm
    last_halo = t // halo - 1

    def body(dy_ref, dyn_ref, c_ref, cn_ref, z_ref, zn_ref, a_ref, ah_ref, g_ref, gh_ref, w_ref, lw_ref, lb_ref, _,
             grp_ref, dw_ref, db_ref, dlw_ref, dlb_ref, dyext, xext, wacc, bacc, lwacc, lbacc, dysh, xsh):
        i = pl.program_id(0)

        @pl.when(i == 0)
        def _():
            wacc[...] = jnp.zeros_like(wacc)
            bacc[...] = jnp.zeros_like(bacc)
            lwacc[...] = jnp.zeros_like(lwacc)
            lbacc[...] = jnp.zeros_like(lbacc)

        def post_bwd(dy, cv, zv):
            xc = cv - jnp.mean(cv, axis=-1, keepdims=True)
            rstd = lax.rsqrt(jnp.mean(xc * xc, axis=-1, keepdims=True) + EPS)
            xh = xc * rstd
            c2 = xh * lw_ref[...] + lb_ref[...]
            dz = dy * _silu(c2) * _dsilu(zv)
            dc2 = dy * _silu(zv) * _dsilu(c2)
            dxh = dc2 * lw_ref[...]
            dc = rstd * (dxh - jnp.mean(dxh, axis=-1, keepdims=True)
                         - xh * jnp.mean(dxh * xh, axis=-1, keepdims=True))
            return dc, dz, dc2 * xh, dc2

        seq_end = ((i + 1) * tm) % seq == 0
        for r0 in range(0, tm, sr):
            rows = slice(r0, r0 + sr)
            dc, dz, lw_terms, lb_terms = post_bwd(dy_ref[rows, :], c_ref[rows, :], z_ref[rows, :])
            dyext[rows, :] = dc
            grp_ref[rows, 2 * cw:] = dz.astype(grp_ref.dtype)
            lwacc[...] += _rowsum8(lw_terms)
            lbacc[...] += _rowsum8(lb_terms)
        dc_next = post_bwd(dyn_ref[...], cn_ref[...], zn_ref[...])[0]
        dyext[tm:, :] = jnp.where(seq_end, 0.0, dc_next)
        _glu_window(xext, a_ref, ah_ref, g_ref, gh_ref, (i * tm) % seq == 0, halo)
        _conv_fill_shifted(dyext, dysh)
        _conv_fill_shifted(xext, xsh)
        dag_ref = grp_ref
        for r0, c0 in _conv_subtiles(tm, cw):
            cs = slice(c0, c0 + sc)
            rows = slice(r0, r0 + sr)
            dyv = dyext[rows, cs]
            acc = jnp.zeros((sr, sc), F32)
            for j in range(k):
                acc = acc + w_ref[j:j + 1, cs] * _conv_rows(dyext, (dysh,), r0 + p - j, sr, cs)
                wacc[j, :, cs] += _rowsum8(dyv * _conv_rows(xext, (xsh,), r0 + halo - p + j, sr, cs))
            bacc[:, cs] += _rowsum8(dyv)
            s = _sigmoid(g_ref[rows, cs])
            dag_ref[rows, cs] = (acc * s).astype(dag_ref.dtype)
            dag_ref[rows, cw + c0:cw + c0 + sc] = (acc * a_ref[rows, cs] * s * (1.0 - s)).astype(dag_ref.dtype)

        @pl.when(i == nt - 1)
        def _():
            dw_ref[...] = jnp.zeros_like(dw_ref)
            for j in range(k):
                dw_ref[j:j + 1, :] = jnp.sum(wacc[j], axis=0, keepdims=True)
            db_ref[...] = jnp.sum(bacc[...], axis=0, keepdims=True)
            dlw_ref[...] = jnp.sum(lwacc[...], axis=0, keepdims=True)
            dlb_ref[...] = jnp.sum(lbacc[...], axis=0, keepdims=True)

    def blk(col):
        return pl.BlockSpec((tm, cw), lambda i: (i, col))

    def nxt(col):
        return pl.BlockSpec((halo, cw), lambda i: (jnp.minimum((i + 1) * (tm // halo), last_halo), col))

    vec = pl.BlockSpec((1, cw), lambda i: (0, 0))
    return pl.pallas_call(
        body, name=name,
        out_shape=(jax.ShapeDtypeStruct(dproj.shape, dproj.dtype), jax.ShapeDtypeStruct((kp, cw), F32),
                   jax.ShapeDtypeStruct((1, cw), F32), jax.ShapeDtypeStruct((1, cw), F32),
                   jax.ShapeDtypeStruct((1, cw), F32)),
        grid=(nt,),
        in_specs=[blk(YCAT_CONF // cw), nxt(YCAT_CONF // cw), blk(0), nxt(0), blk(OFF_ZC // cw), nxt(OFF_ZC // cw),
                  *_conf_specs(tm, cw, halo, lambda i: i),
                  pl.BlockSpec((kp, cw), lambda i: (0, 0)), vec, vec, ANY],
        out_specs=(pl.BlockSpec((tm, CONF_GROUP), lambda i: (i, OFF_CONF // CONF_GROUP)),
                   pl.BlockSpec((kp, cw), lambda i: (0, 0)), vec, vec, vec),
        input_output_aliases={13: 0},
        scratch_shapes=[pltpu.VMEM((tm + halo, cw), F32), pltpu.VMEM((halo + tm, cw), F32),
                        pltpu.VMEM((kp, SUBLANES, cw), F32), pltpu.VMEM((SUBLANES, cw), F32),
                        pltpu.VMEM((SUBLANES, cw), F32), pltpu.VMEM((SUBLANES, cw), F32)]
        + 2 * _conv_shift_scratch(k, halo + tm, cw),
        compiler_params=_params(("arbitrary",)),
    )(dycat, dycat, c1, c1, proj, proj, proj, proj, proj, proj, w, ln_w, ln_b, dproj)


def _half_mask(half):
    lane = _iota((1, LANES), 1)
    return ((lane >= half * ATTN_HEAD_DIM) & (lane < (half + 1) * ATTN_HEAD_DIM)).astype(F32)


def _stack_heads(xp, g):
    m = _half_mask(g)
    swapped = pltpu.roll(xp, ATTN_HEAD_DIM, axis=1)
    return jnp.concatenate([xp * m, swapped * m] if g == 0 else [swapped * m, xp * m], axis=0)


def _unstack_heads(both, g):
    w = both.shape[0] // 2
    top, bot = both[:w], both[w:]
    lo, hi = _half_mask(0), _half_mask(1)
    if g == 0:
        return top * lo + pltpu.roll(bot, ATTN_HEAD_DIM, axis=1) * hi
    return pltpu.roll(top, ATTN_HEAD_DIM, axis=1) * lo + bot * hi


def _band_mask(first_block):
    w = WINDOW
    qi = _iota((w, 2 * w), 0)
    kj = _iota((w, 2 * w), 1) - w
    rel = qi - kj
    return (rel >= 0) & (rel < w) & (jnp.logical_not(first_block) | (kj >= 0))


def _lane_pick(x, h):
    return jnp.sum(jnp.where(_iota(x.shape, 1) == h, x, 0.0), axis=1, keepdims=True)


def _attn_specs(nb, rev):
    w = WINDOW

    def blk(i):
        return nb - 1 - i if rev else i

    def row(b, i):
        return b * nb + blk(i)

    def prow(b, i):
        return b * nb + jnp.maximum(blk(i) - 1, 0)

    q = pl.BlockSpec((w, 512), lambda b, i: (row(b, i), OFF_Q // 512))
    kc = pl.BlockSpec((w, 128), lambda b, i: (row(b, i), OFF_K // 128))
    kp = pl.BlockSpec((w, 128), lambda b, i: (prow(b, i), OFF_K // 128))
    vc = pl.BlockSpec((w, 128), lambda b, i: (row(b, i), OFF_V // 128))
    vp = pl.BlockSpec((w, 128), lambda b, i: (prow(b, i), OFF_V // 128))
    z = pl.BlockSpec((w, 512), lambda b, i: (row(b, i), OFF_ZA // 512))
    return q, kc, kp, vc, vp, z, row


def _attn_fwd(proj, sinks, ycat, nbatch, name):
    t = proj.shape[0]
    w = WINDOW
    nb = t // nbatch // w
    scale = ATTN_HEAD_DIM ** -0.5
    q_s, kc_s, kp_s, vc_s, vp_s, z_s, row = _attn_specs(nb, False)

    def body(q_ref, kc_ref, kp_ref, vc_ref, vp_ref, z_ref, sk_ref, _, y_ref, o_ref, lse_ref):
        first = pl.program_id(1) == 0
        mask = _band_mask(first)
        kk = jnp.concatenate([kp_ref[...], kc_ref[...]], axis=0).astype(MXU_DTYPE)
        vv = jnp.concatenate([vp_ref[...], vc_ref[...]], axis=0).astype(MXU_DTYPE)
        sk = sk_ref[...]
        lane = _iota((w, LANES), 1)
        mask2 = jnp.concatenate([mask, mask], axis=0)
        scores = [_dot(_stack_heads(q_ref[:, j * LANES:(j + 1) * LANES], j // 2), kk, NT) for j in range(4)]
        lse_all = jnp.zeros((w, LANES), F32)
        for j in range(4):
            s = jnp.where(mask2, scores[j] * scale, -1e30)
            skc = jnp.concatenate([jnp.broadcast_to(_lane_pick(sk, 2 * j), (w, 1)),
                                   jnp.broadcast_to(_lane_pick(sk, 2 * j + 1), (w, 1))], axis=0)
            m = jnp.maximum(jnp.max(s, axis=1, keepdims=True), skc)
            den = jnp.sum(jnp.exp(s - m), axis=1, keepdims=True) + jnp.exp(skc - m)
            lse = m + jnp.log(den)
            lse_all = jnp.where(lane == 2 * j, lse[:w], lse_all)
            lse_all = jnp.where(lane == 2 * j + 1, lse[w:], lse_all)
            op = _unstack_heads(_dot(jnp.exp(s - lse), vv), j // 2)
            cols = slice(j * LANES, (j + 1) * LANES)
            o_ref[:, cols] = op
            y_ref[:, cols] = (op * _silu(z_ref[:, cols])).astype(y_ref.dtype)
        lse_ref[...] = lse_all

    return pl.pallas_call(
        body, name=name,
        out_shape=(jax.ShapeDtypeStruct(ycat.shape, ycat.dtype), jax.ShapeDtypeStruct((t, 512), F32),
                   jax.ShapeDtypeStruct((t, LANES), F32)),
        grid=(nbatch, nb),
        in_specs=[q_s, kc_s, kp_s, vc_s, vp_s, z_s, pl.BlockSpec((1, LANES), lambda b, i: (0, 0)), ANY],
        out_specs=(pl.BlockSpec((w, 512), lambda b, i: (row(b, i), YCAT_ATTN // 512)),
                   pl.BlockSpec((w, 512), lambda b, i: (row(b, i), 0)),
                   pl.BlockSpec((w, LANES), lambda b, i: (row(b, i), 0))),
        input_output_aliases={7: 0},
        compiler_params=_params(("parallel", "parallel")),
    )(proj, proj, proj, proj, proj, proj, sinks, ycat)


def _attn_bwd(dycat, proj, o, lse, sinks, ddt, dproj, nbatch, name):
    t = proj.shape[0]
    w = WINDOW
    nb = t // nbatch // w
    scale = ATTN_HEAD_DIM ** -0.5
    q_s, kc_s, kp_s, vc_s, vp_s, z_s, row = _attn_specs(nb, True)

    def body(dy_ref, q_ref, kc_ref, kp_ref, vc_ref, vp_ref, z_ref, o_ref, lse_ref, sk_ref, ddt_ref, _,
             grp_ref, dsk_ref, kcarry, vcarry, sacc):
        b, i = pl.program_id(0), pl.program_id(1)

        @pl.when((b == 0) & (i == 0))
        def _():
            sacc[...] = jnp.zeros_like(sacc)

        @pl.when(i == 0)
        def _():
            kcarry[...] = jnp.zeros_like(kcarry)
            vcarry[...] = jnp.zeros_like(vcarry)

        first = i == nb - 1
        mask = _band_mask(first)
        kk = jnp.concatenate([kp_ref[...], kc_ref[...]], axis=0).astype(MXU_DTYPE)
        vv = jnp.concatenate([vp_ref[...], vc_ref[...]], axis=0).astype(MXU_DTYPE)
        sk = sk_ref[...]
        lse_all = lse_ref[...]
        lane1 = _iota((1, LANES), 1)
        mask2 = jnp.concatenate([mask, mask], axis=0)
        qs, dos, deltas, lses, scores, dps = [], [], [], [], [], []
        for j in range(4):
            cols = slice(j * LANES, (j + 1) * LANES)
            qp, zp, ov, dy = q_ref[:, cols], z_ref[:, cols], o_ref[:, cols], dy_ref[:, cols]
            grp_ref[:, OFF_ZA + j * LANES:OFF_ZA + (j + 1) * LANES] = (dy * ov * _dsilu(zp)).astype(grp_ref.dtype)
            do = dy * _silu(zp)
            q2 = _stack_heads(qp, j // 2).astype(MXU_DTYPE)
            do2 = _stack_heads(do, j // 2)
            qs.append(q2)
            dos.append(do2.astype(MXU_DTYPE))
            deltas.append(jnp.sum(do2 * _stack_heads(ov, j // 2), axis=1, keepdims=True))
            lses.append(jnp.concatenate([_lane_pick(lse_all, 2 * j), _lane_pick(lse_all, 2 * j + 1)], axis=0))
            scores.append(_dot(q2, kk, NT))
            dps.append(_dot(do2, vv, NT))
        prs, dss = [], []
        dsk = jnp.zeros((1, LANES), F32)
        for j in range(4):
            pr = jnp.exp(jnp.where(mask2, scores[j] * scale, -1e30) - lses[j])
            prs.append(pr.astype(MXU_DTYPE))
            dss.append((pr * (dps[j] - deltas[j])).astype(MXU_DTYPE))
            skc = jnp.concatenate([jnp.broadcast_to(_lane_pick(sk, 2 * j), (w, 1)),
                                   jnp.broadcast_to(_lane_pick(sk, 2 * j + 1), (w, 1))], axis=0)
            sink_term = jnp.exp(skc - lses[j]) * deltas[j]
            dsk = dsk - jnp.where(lane1 == 2 * j, jnp.sum(sink_term[:w]), 0.0)
            dsk = dsk - jnp.where(lane1 == 2 * j + 1, jnp.sum(sink_term[w:]), 0.0)
        dkk = jnp.zeros((2 * w, LANES), F32)
        dvv = jnp.zeros((2 * w, LANES), F32)
        for j in range(4):
            dq = _unstack_heads(_dot(dss[j], kk) * scale, j // 2)
            grp_ref[:, OFF_Q + j * LANES:OFF_Q + (j + 1) * LANES] = dq.astype(grp_ref.dtype)
            dkk = dkk + _dot(dss[j], qs[j], TN) * scale
            dvv = dvv + _dot(prs[j], dos[j], TN)
        grp_ref[:, OFF_K:OFF_K + LANES] = (dkk[w:, :] + kcarry[...]).astype(grp_ref.dtype)
        grp_ref[:, OFF_V:OFF_V + LANES] = (dvv[w:, :] + vcarry[...]).astype(grp_ref.dtype)
        grp_ref[:, OFF_DT:OFF_DT + LANES] = ddt_ref[...].astype(grp_ref.dtype)
        grp_ref[:, OFF_DT + LANES:] = jnp.zeros((w, ATTN_GROUP - OFF_DT - LANES), grp_ref.dtype)
        kcarry[...] = dkk[:w, :]
        vcarry[...] = dvv[:w, :]
        sacc[...] += dsk

        @pl.when((b == nbatch - 1) & (i == nb - 1))
        def _():
            dsk_ref[...] = sacc[...]

    return pl.pallas_call(
        body, name=name,
        out_shape=(jax.ShapeDtypeStruct(dproj.shape, dproj.dtype), jax.ShapeDtypeStruct((1, LANES), F32)),
        grid=(nbatch, nb),
        in_specs=[pl.BlockSpec((w, 512), lambda b, i: (row(b, i), YCAT_ATTN // 512)),
                  q_s, kc_s, kp_s, vc_s, vp_s, z_s,
                  pl.BlockSpec((w, 512), lambda b, i: (row(b, i), 0)),
                  pl.BlockSpec((w, LANES), lambda b, i: (row(b, i), 0)),
                  pl.BlockSpec((1, LANES), lambda b, i: (0, 0)),
                  pl.BlockSpec((w, LANES), lambda b, i: (row(b, i), 0)), ANY],
        out_specs=(pl.BlockSpec((w, ATTN_GROUP), lambda b, i: (row(b, i), 0)),
                   pl.BlockSpec((1, LANES), lambda b, i: (0, 0))),
        input_output_aliases={11: 0},
        scratch_shapes=[pltpu.VMEM((w, LANES), F32), pltpu.VMEM((w, LANES), F32),
                        pltpu.VMEM((1, LANES), F32)],
        compiler_params=_params(("arbitrary", "arbitrary")),
    )(dycat, proj, proj, proj, proj, proj, proj, o, lse, sinks, ddt, dproj)


SSD_WIDTH = SSD_HEADS * SSD_HEAD_DIM
GROUP_ROWS = SSD_WIDTH // 2


def _expand_mat():
    r, c = _iota((LANES, SSD_WIDTH), 0), _iota((LANES, SSD_WIDTH), 1)
    return (r == lax.shift_right_logical(c, 6)).astype(BF16)


def _expand_mat_t():
    r, c = _iota((SSD_WIDTH, LANES), 0), _iota((SSD_WIDTH, LANES), 1)
    return (c == lax.shift_right_logical(r, 6)).astype(BF16)


def _ssd_common(u_ref, dt_ref, dtb_ref, a_ref, stack_broadcasts=False):
    q = CHUNK
    act = _silu(u_ref[...])
    xs = act[:, :SSD_WIDTH]
    bm = act[:, SSD_WIDTH:SSD_WIDTH + 256]
    cm = act[:, SSD_WIDTH + 256:]
    dtp = _softplus(dt_ref[...] + dtb_ref[...])
    a = dtp * a_ref[...]
    tril = (_iota((q, q), 0) >= _iota((q, q), 1)).astype(BF16)
    acs = _xdot_r(tril, a)
    acs_t = acs.T
    e = _expand_mat()
    a_end = jnp.sum(jnp.where(_iota(acs.shape, 0) == q - 1, acs, 0.0), axis=0, keepdims=True)
    if stack_broadcasts:
        spread = _xdot(jnp.concatenate([dtp, acs, a_end - acs], axis=0), e)
        dt_x, ea, dec = spread[:q], jnp.exp(spread[q:2 * q]), jnp.exp(spread[2 * q:])
    else:
        dt_x = _xdot(dtp, e)
        ea = jnp.exp(_xdot(acs, e))
        dec = jnp.exp(_xdot(a_end - acs, e))
    a_end_col = jnp.broadcast_to(_lane_pick(acs_t, q - 1), (LANES, LANES))
    s_scale = jnp.exp(_xdot_r(_expand_mat_t(), a_end_col))
    return act, xs, bm, cm, dtp, acs, acs_t, dt_x, ea, dec, s_scale, tril


def _decay_mat(acs, acs_t, h):
    q = CHUNK
    col = _lane_pick(acs, h)
    rowv = jnp.sum(jnp.where(_iota(acs_t.shape, 0) == h, acs_t, 0.0), axis=0, keepdims=True)
    causal = _iota((q, q), 0) >= _iota((q, q), 1)
    return jnp.exp(jnp.where(causal, col - rowv, -1e30))


GN_WIDTH = 512


def _ssd_fwd(u, proj, dtb, a_neg, d_x, norm_w, ycat, nbatch, name):
    t = u.shape[0]
    q = CHUNK
    nc = t // nbatch // q

    def body(u_ref, dt_ref, z_ref, dtb_ref, a_ref, dx_ref, nw_ref, _, y_ref, st_ref, yn_ref, state):
        c = pl.program_id(1)

        @pl.when(c == 0)
        def _():
            state[...] = jnp.zeros_like(state)

        st_ref[...] = state[...]
        act, xs, bm, cm, dtp, acs, acs_t, dt_x, ea, dec, s_scale, _ = _ssd_common(u_ref, dt_ref, dtb_ref, a_ref)
        xdt = xs * dt_x
        xdec = xdt * dec
        lo, hi = _half_mask(0), _half_mask(1)
        grp = []
        for g in range(2):
            bg = bm[:, g * LANES:(g + 1) * LANES]
            cg = cm[:, g * LANES:(g + 1) * LANES]
            rows = slice(g * GROUP_ROWS, (g + 1) * GROUP_ROWS)
            sg = state[rows, :]
            grp.append((_dot(cg, bg, NT), _dot(cg, sg, NT), rows,
                        s_scale[rows, :] * sg + _dot(xdec[:, rows], bg, TN)))
        for g in range(2):
            cb, yoff, rows, state_new = grp[g]
            for j in range(4):
                pj = g * 4 + j
                cols = slice(pj * LANES, (pj + 1) * LANES)
                xp = xdt[:, cols]
                m2 = jnp.concatenate([cb * _decay_mat(acs, acs_t, 2 * pj), cb * _decay_mat(acs, acs_t, 2 * pj + 1)],
                                     axis=1)
                yp = _dot(m2, jnp.concatenate([xp * lo, xp * hi], axis=0))
                yp = yp + yoff[:, j * LANES:(j + 1) * LANES] * ea[:, cols]
                y_ref[:, cols] = yp + dx_ref[:, cols] * xs[:, cols]
            state[rows, :] = state_new
        for g in range(SSD_WIDTH // GN_WIDTH):
            cols = slice(g * GN_WIDTH, (g + 1) * GN_WIDTH)
            gg = y_ref[:, cols] * _silu(z_ref[:, cols])
            rstd = lax.rsqrt(jnp.mean(gg * gg, axis=-1, keepdims=True) + EPS)
            yn_ref[:, cols] = (gg * rstd * nw_ref[:, cols]).astype(yn_ref.dtype)

    vec = pl.BlockSpec((1, LANES), lambda b, c: (0, 0))
    wide = pl.BlockSpec((q, SSD_WIDTH), lambda b, c: (b * nc + c, 0))
    wvec = pl.BlockSpec((1, SSD_WIDTH), lambda b, c: (0, 0))
    return pl.pallas_call(
        body, name=name,
        out_shape=(jax.ShapeDtypeStruct((t, SSD_WIDTH), F32),
                   jax.ShapeDtypeStruct((nbatch * nc * SSD_WIDTH, SSD_STATE), F32),
                   jax.ShapeDtypeStruct(ycat.shape, ycat.dtype)),
        grid=(nbatch, nc),
        in_specs=[pl.BlockSpec((q, SSD_CONV_DIM), lambda b, c: (b * nc + c, 0)),
                  pl.BlockSpec((q, LANES), lambda b, c: (b * nc + c, OFF_DT // LANES)),
                  pl.BlockSpec((q, SSD_WIDTH), lambda b, c: (b * nc + c, OFF_ZS // SSD_WIDTH)),
                  vec, vec, wvec, wvec, ANY],
        out_specs=(wide, pl.BlockSpec((SSD_WIDTH, SSD_STATE), lambda b, c: (b * nc + c, 0)), wide),
        input_output_aliases={7: 2},
        scratch_shapes=[pltpu.VMEM((SSD_WIDTH, SSD_STATE), F32)],
        compiler_params=_params(("parallel", "arbitrary")),
    )(u, proj, proj, dtb, a_neg, d_x, norm_w, ycat)


def _ssd_bwd(dycat, u, proj, y, states, dtb, a_neg, d_x, norm_w, dproj, nbatch, name):
    t = u.shape[0]
    q = CHUNK
    nc = t // nbatch // q

    def body(do_ref, u_ref, dt_ref, z_ref, y_ref, st_ref, dtb_ref, a_ref, dx_ref, nw_ref, _,
             du_ref, dz_ref, ddt_ref, dal_ref, dd_ref, dtbg_ref, dnw_ref, dstate, acc_a, acc_d, acc_b, acc_w):
        b, c = pl.program_id(0), pl.program_id(1)

        @pl.when((b == 0) & (c == 0))
        def _():
            acc_a[...] = jnp.zeros_like(acc_a)
            acc_d[...] = jnp.zeros_like(acc_d)
            acc_b[...] = jnp.zeros_like(acc_b)
            acc_w[...] = jnp.zeros_like(acc_w)

        @pl.when(c == 0)
        def _():
            dstate[...] = jnp.zeros_like(dstate)

        dy_parts = []
        for g in range(SSD_WIDTH // GN_WIDTH):
            cols = slice(g * GN_WIDTH, (g + 1) * GN_WIDTH)
            yv, zv, dov = y_ref[:, cols], z_ref[:, cols], do_ref[:, cols]
            sz = _silu(zv)
            gg = yv * sz
            rstd = lax.rsqrt(jnp.mean(gg * gg, axis=-1, keepdims=True) + EPS)
            gh = gg * rstd
            acc_w[:, cols] += _rowsum8(dov * gh)
            dgn = dov * nw_ref[:, cols]
            dg = rstd * (dgn - gh * jnp.mean(dgn * gh, axis=-1, keepdims=True))
            dy_parts.append(dg * sz)
            dz_ref[:, cols] = (dg * yv * _dsilu(zv)).astype(dz_ref.dtype)

        act, xs, bm, cm, dtp, acs, acs_t, dt_x, ea, dec, s_scale, tril = _ssd_common(
            u_ref, dt_ref, dtb_ref, a_ref, stack_broadcasts=True)
        xdt = xs * dt_x
        xdec = xdt * dec
        dyv = jnp.concatenate(dy_parts, axis=1)
        dye = dyv * ea
        lo, hi = _half_mask(0), _half_mask(1)
        et = _expand_mat_t()
        grp = []
        for g in range(2):
            rows = slice(g * GROUP_ROWS, (g + 1) * GROUP_ROWS)
            bg = bm[:, g * LANES:(g + 1) * LANES]
            cg = cm[:, g * LANES:(g + 1) * LANES]
            sg = st_ref[rows, :]
            dsg = dstate[rows, :]
            grp.append(dict(
                rows=rows, bg=bg, cg=cg, dsg=dsg,
                cb=_dot(cg, bg, NT), yoff=_dot(cg, sg, NT), dxst=_dot(bg, dsg, NT) * dec[:, rows],
                dc_off=_dot(dye[:, rows], sg), db_off=_dot(xdec[:, rows], dsg),
                s_carried=s_scale[rows, :] * sg,
                dstate_new=_dot(dye[:, rows], cg, TN) + s_scale[rows, :] * dsg))
        dy2s, g2s, l2s = [], [], []
        for pj in range(SSD_HEADS // 2):
            cols = slice(pj * LANES, (pj + 1) * LANES)
            dyp = dyv[:, cols]
            dy2 = jnp.concatenate([dyp * lo, dyp * hi], axis=0).astype(MXU_DTYPE)
            dy2s.append(dy2)
            g2s.append(_dot(dy2, xdt[:, cols], NT))
            l2s.append(jnp.concatenate([_decay_mat(acs, acs_t, 2 * pj), _decay_mat(acs, acs_t, 2 * pj + 1)], axis=0))
        dal_diag = jnp.zeros((q, LANES), F32)
        lane2 = _iota((2 * q, LANES), 1)
        row2 = _iota((2 * q, LANES), 0)
        dxdt_parts, db_parts, dc_parts = [], [], []
        end_sum = jnp.zeros((LANES, LANES), F32)
        for g in range(2):
            gd = grp[g]
            cb2 = jnp.concatenate([gd["cb"], gd["cb"]], axis=0)
            dcb = jnp.zeros((q, q), F32)
            parts = []
            for j in range(4):
                pj = g * 4 + j
                gl = g2s[pj] * l2s[pj]
                dcb = dcb + gl[:q] + gl[q:]
                m2 = cb2 * l2s[pj]
                parts.append(_dot(m2, dy2s[pj], TN))
                w2 = (gl * cb2).astype(MXU_DTYPE)
                sel2 = (lane2 == 2 * pj + (row2 >= q).astype(jnp.int32)).astype(MXU_DTYPE)
                dal_diag = dal_diag + _dot(jnp.concatenate([w2[:q], w2[q:]], axis=1), sel2) - _dot(w2, sel2, TN)
            dxdt_parts.append(jnp.concatenate(parts, axis=1) + gd["dxst"])
            dc_parts.append(_dot(dcb, gd["bg"]) + gd["dc_off"])
            db_parts.append(_dot(dcb, gd["cg"], TN) + gd["db_off"])
            end_sum = end_sum + _xdot(gd["dsg"] * gd["s_carried"], et[gd["rows"], :], TN, passes=2)
            dstate[gd["rows"], :] = gd["dstate_new"]
        dxst_parts = [gd["dxst"] for gd in grp]
        yoff_parts = [gd["yoff"] for gd in grp]
        dxdt = jnp.concatenate(dxdt_parts, axis=1)
        dxv = dx_ref[...]
        yoff = jnp.concatenate(yoff_parts, axis=1) * ea
        per_head = _xdot(jnp.concatenate([dyv * yoff, xdt * jnp.concatenate(dxst_parts, axis=1),
                                          dxdt * xs, dyv * xs], axis=0), et)
        off_term, st_term, dx_term, d_term = (per_head[k * q:(k + 1) * q] for k in range(4))
        dalpha = dal_diag + off_term - st_term
        end_row = jnp.sum(end_sum, axis=0, keepdims=True) + jnp.sum(st_term, axis=0, keepdims=True)
        dalpha = dalpha + jnp.where(_iota((q, LANES), 0) == q - 1, end_row, 0.0)
        da = _xdot_r(tril, dalpha, TN)
        ddtp = da * a_ref[...] + dx_term
        acc_a[...] += _rowsum8(da * dtp)
        acc_d[...] += _rowsum8(d_term)
        ddt_raw = ddtp * _sigmoid(dt_ref[...] + dtb_ref[...])
        acc_b[...] += _rowsum8(ddt_raw)
        ddt_ref[...] = ddt_raw
        dxs = dxdt * dt_x + dxv * dyv
        dact = jnp.concatenate([dxs] + db_parts + dc_parts, axis=1)
        du_ref[...] = dact * _dsilu(u_ref[...])

        @pl.when((b == nbatch - 1) & (c == nc - 1))
        def _():
            dal_ref[...] = jnp.sum(acc_a[...], axis=0, keepdims=True) * a_ref[...]
            dd_ref[...] = jnp.sum(acc_d[...], axis=0, keepdims=True)
            dtbg_ref[...] = jnp.sum(acc_b[...], axis=0, keepdims=True)
            dnw_ref[...] = jnp.sum(acc_w[...], axis=0, keepdims=True)

    def rowblk(b, c):
        return b * nc + (nc - 1 - c)

    vec = pl.BlockSpec((1, LANES), lambda b, c: (0, 0))
    wvec = pl.BlockSpec((1, SSD_WIDTH), lambda b, c: (0, 0))
    wide = pl.BlockSpec((q, SSD_WIDTH), lambda b, c: (rowblk(b, c), 0))
    zblk = pl.BlockSpec((q, SSD_WIDTH), lambda b, c: (rowblk(b, c), OFF_ZS // SSD_WIDTH))
    return pl.pallas_call(
        body, name=name,
        out_shape=(jax.ShapeDtypeStruct((t, SSD_CONV_DIM), F32), jax.ShapeDtypeStruct(dproj.shape, dproj.dtype),
                   jax.ShapeDtypeStruct((t, LANES), F32),
                   jax.ShapeDtypeStruct((1, LANES), F32), jax.ShapeDtypeStruct((1, LANES), F32),
                   jax.ShapeDtypeStruct((1, LANES), F32), jax.ShapeDtypeStruct((1, SSD_WIDTH), F32)),
        grid=(nbatch, nc),
        in_specs=[wide,
                  pl.BlockSpec((q, SSD_CONV_DIM), lambda b, c: (rowblk(b, c), 0)),
                  pl.BlockSpec((q, LANES), lambda b, c: (rowblk(b, c), OFF_DT // LANES)),
                  zblk, wide,
                  pl.BlockSpec((SSD_WIDTH, SSD_STATE), lambda b, c: (rowblk(b, c), 0)),
                  vec, vec, wvec, wvec, ANY],
        out_specs=(pl.BlockSpec((q, SSD_CONV_DIM), lambda b, c: (rowblk(b, c), 0)),
                   zblk,
                   pl.BlockSpec((q, LANES), lambda b, c: (rowblk(b, c), 0)),
                   vec, vec, vec, wvec),
        input_output_aliases={10: 1},
        scratch_shapes=[pltpu.VMEM((SSD_WIDTH, SSD_STATE), F32), pltpu.VMEM((SUBLANES, LANES), F32),
                        pltpu.VMEM((SUBLANES, LANES), F32), pltpu.VMEM((SUBLANES, LANES), F32),
                        pltpu.VMEM((SUBLANES, SSD_WIDTH), F32)],
        compiler_params=_params(("arbitrary", "arbitrary")),
    )(dycat, u, proj, proj, y, states, dtb, a_neg, d_x, norm_w, dproj)


def _pad_rows(w, rows):
    return jnp.concatenate([w, jnp.zeros((rows - w.shape[0], w.shape[1]), w.dtype)], axis=0)


def _pad_lanes(v):
    return jnp.concatenate([v, jnp.zeros((LANES - v.shape[0],), v.dtype)]).reshape(1, LANES)


def _padded_from_chips(pieces):
    cols = pieces[0].shape[-1]
    lead = pieces[0].shape[:-1]
    parts, pos = [], 0
    for lo, hi, start in sorted(SECTIONS, key=lambda s: s[2]):
        if start > pos:
            parts.append(jnp.zeros(lead + (start - pos,), pieces[0].dtype))
        pos = start + hi - lo
        while lo < hi:
            p = lo // cols
            end = min(hi, (p + 1) * cols)
            parts.append(pieces[p][..., lo - p * cols:end - p * cols])
            lo = end
    if pos < NP:
        parts.append(jnp.zeros(lead + (NP - pos,), pieces[0].dtype))
    return jnp.concatenate(parts, axis=-1)


def _chip_part_from_padded(wp, p, cols):
    lo, hi = p * cols, (p + 1) * cols
    parts = []
    for rs, re, start in SECTIONS:
        a, b = max(lo, rs), min(hi, re)
        if a < b:
            parts.append(wp[..., start + a - rs:start + b - rs])
    return jnp.concatenate(parts, axis=-1)


def _layer_params(li, w_in_p, w_out, conv_w, dw_w, small):
    return dict(
        w_in_p=w_in_p, w_out=w_out,
        conv_w=_pad_rows(conv_w, SUBLANES), dw_w=_pad_rows(dw_w, 32),
        norm_w=small["norm_w"][li].reshape(1, -1),
        conv_b=small["ssd_conv_b"][li].reshape(1, -1),
        dtb=_pad_lanes(small["ssd_dt_bias"][li]),
        a_neg=_pad_lanes(-jnp.exp(small["ssd_a_log"][li])),
        d_x=jnp.repeat(small["ssd_d"][li], SSD_HEAD_DIM).reshape(1, -1),
        ssd_norm_w=small["ssd_norm_w"][li].reshape(1, -1),
        sinks=_pad_lanes(small["attn_sinks"][li]),
        dw_b=small["conf_dw_b"][li].reshape(1, -1),
        ln_w=small["conf_ln_w"][li].reshape(1, -1),
        ln_b=small["conf_ln_b"][li].reshape(1, -1),
    )


def _layer_fwd(x, p, nbatch, seq, tag, after=None):
    proj, h_t = _proj_fwd(x, p["norm_w"], p["w_in_p"], name=f"proj_fwd_{tag}", after=after)
    u = _conv_fwd(proj, OFF_XBC, SSD_CONV_DIM, p["conv_w"], p["conv_b"], SSD_CONV, seq, name=f"ssd_conv_fwd_{tag}")
    ycat = lax.empty((x.shape[0], MIX_WIDTH), MXU_DTYPE)
    y, states, ycat = _ssd_fwd(u, proj, p["dtb"], p["a_neg"], p["d_x"], p["ssd_norm_w"], ycat, nbatch,
                               name=f"ssd_fwd_{tag}")
    ycat, o, lse = _attn_fwd(proj, p["sinks"], ycat, nbatch, name=f"attn_fwd_{tag}")
    c1, ycat = _conf_fwd(proj, p["dw_w"], p["dw_b"], p["ln_w"], p["ln_b"], ycat, seq, name=f"conf_fwd_{tag}")
    w_out = p["w_out"](ycat) if callable(p["w_out"]) else p["w_out"]
    x_new = _matmul(ycat, w_out, "nn", F32, 1024, 512, 2048, name=f"out_fwd_{tag}", residual=x)
    return x_new, dict(x=x, w_out=w_out, h_t=h_t, proj=proj, u=u, y=y, states=states, o=o, lse=lse, c1=c1, ycat=ycat)


def _layer_bwd(dx_out, p, s, nbatch, seq, tag, hooks=None):
    hooks = hooks or {}
    proj = s["proj"]
    dycat = _matmul(dx_out, s["w_out"], "nt", F32, 1024, 1024, 1024, name=f"out_bwd_dy_{tag}",
                    after=hooks.get("start_token"))
    dw_out = _matmul(s["ycat"], dx_out, "tn", F32, 1024, 1024, 1024, name=f"out_bwd_dw_{tag}")
    token = hooks["after_dycat"](dycat) if "after_dycat" in hooks else None
    dtb = p["dtb"] if token is None else p["dtb"] + token[0, 0]
    dproj = lax.empty(proj.shape, MXU_DTYPE)
    du, dproj, ddt, da_log, dd, ddtb, dssd_norm_w = _ssd_bwd(
        dycat, s["u"], proj, s["y"], s["states"], dtb, p["a_neg"], p["d_x"], p["ssd_norm_w"], dproj,
        nbatch, name=f"ssd_bwd_{tag}")
    dproj, dconv_w, dconv_b = _conv_bwd(du, proj, OFF_XBC, SSD_CONV_DIM, p["conv_w"], SSD_CONV, seq,
                                        name=f"ssd_conv_bwd_{tag}", into=dproj)
    dproj, dsinks = _attn_bwd(dycat, proj, s["o"], s["lse"], p["sinks"], ddt, dproj, nbatch,
                              name=f"attn_bwd_{tag}")
    if "after_attn" in hooks:
        hooks["after_attn"](dproj)
    dproj, ddw_w, ddw_b, dln_w, dln_b = _conf_bwd(dycat, proj, s["c1"], p["dw_w"], p["ln_w"], p["ln_b"], dproj, seq,
                                                  name=f"conf_bwd_{tag}")
    dw_in_p = _matmul(s["h_t"], dproj, "nn", F32, 1024, 512, 4096, name=f"proj_bwd_dw_{tag}")
    token = hooks["after_dw"](dw_in_p, dw_out) if "after_dw" in hooks else None
    norm_w = p["norm_w"] if token is None else p["norm_w"] + token[0, 0]
    dx_in, dnorm_w = _proj_bwd_dx(dproj, p["w_in_p"], s["x"], norm_w, dx_out, name=f"proj_bwd_dx_{tag}")
    grads = dict(
        norm_w=dnorm_w[0], w_in_p=dw_in_p, ssd_conv_w=dconv_w[:SSD_CONV], ssd_conv_b=dconv_b[0],
        ssd_dt_bias=ddtb[0, :SSD_HEADS], ssd_a_log=da_log[0, :SSD_HEADS], ssd_d=dd[0, :SSD_HEADS],
        ssd_norm_w=dssd_norm_w[0], attn_sinks=dsinks[0, :ATTN_Q_HEADS], conf_dw_w=ddw_w[:CONF_KERNEL],
        conf_dw_b=ddw_b[0], conf_ln_w=dln_w[0], conf_ln_b=dln_b[0], w_out=dw_out)
    return dx_in, grads


def _local_step(x, target, param_fns, final_norm_w, first_after=None, bwd_hooks=None):
    nbatch, seq, d = x.shape
    xt = x.reshape(nbatch * seq, d)
    saved, layer_params = [], []
    for li, fn in enumerate(param_fns):
        p = fn(xt)
        layer_params.append(p)
        xt, s = _layer_fwd(xt, p, nbatch, seq, f"l{li}", after=first_after if li == 0 else None)
        saved.append(s)
    loss, dx, dfinal = _loss_head(xt, target.reshape(nbatch * seq, d), final_norm_w.reshape(1, d), name="loss_head")
    grads = [None] * len(layer_params)
    for li in reversed(range(len(layer_params))):
        hooks = bwd_hooks(li) if bwd_hooks is not None else None
        dx, grads[li] = _layer_bwd(dx, layer_params[li], saved[li], nbatch, seq, f"l{li}", hooks=hooks)
    return loss[0, 0], dx.reshape(nbatch, seq, d), grads, dfinal[0]


MESH = pl.DeviceIdType.MESH
N_CHIPS = 4


def _mesh_pos():
    return lax.axis_index("x"), lax.axis_index("y"), lax.axis_index("c")


def _other_chips(x, y):
    return [(1 - x, y), (x, 1 - y), (1 - x, 1 - y)]


def _gather_weights(big, small, name):
    nbig, nsmall = len(big), len(small)
    n_ici = 3 * (nbig + nsmall)
    n_fwd = 3 * nbig

    def body(*refs):
        ins = refs[:nbig + nsmall]
        outs = refs[nbig + nsmall:2 * (nbig + nsmall)]
        send_sems, recv_sems = refs[2 * (nbig + nsmall):]
        x, y, c = _mesh_pos()
        me = 2 * x + y
        sibling = (x, y, 1 - c)
        chips = _other_chips(x, y)

        def ici(a, j, origin, dest):
            if a < nbig:
                src = ins[a].at[c] if origin is None else outs[a].at[origin, c]
                dst = outs[a].at[me if origin is None else origin, c]
            else:
                src = ins[a] if origin is None else outs[a].at[origin]
                dst = outs[a].at[me if origin is None else origin]
            k = a * 3 + j
            return pltpu.make_async_remote_copy(src_ref=src, dst_ref=dst, send_sem=send_sems.at[k],
                                                recv_sem=recv_sems.at[k], device_id=dest, device_id_type=MESH)

        def fwd(a, j, origin, half):
            k = n_ici + a * 3 + j
            ref = outs[a].at[origin, half]
            return pltpu.make_async_remote_copy(src_ref=ref, dst_ref=ref, send_sem=send_sems.at[k],
                                                recv_sem=recv_sems.at[k], device_id=sibling, device_id_type=MESH)

        sends = []
        for j, (px, py) in enumerate(chips):
            for a in range(nbig + nsmall):
                cp = ici(a, j, None, (px, py, c))
                cp.start()
                sends.append(cp)
        for j, (px, py) in enumerate(chips):
            origin = 2 * px + py
            for a in range(nbig):
                ici(a, j, origin, (px, py, c)).wait_recv()
                cp = fwd(a, j, origin, c)
                cp.start()
                sends.append(cp)
        for j, (px, py) in enumerate(chips):
            origin = 2 * px + py
            for a in range(nbig, nbig + nsmall):
                ici(a, j, origin, (px, py, c)).wait_recv()
            for a in range(nbig):
                fwd(a, j, origin, 1 - c).wait_recv()
        for cp in sends:
            cp.wait_send()

    out_shape = tuple(jax.ShapeDtypeStruct((N_CHIPS,) + a.shape, a.dtype) for a in list(big) + list(small))
    return pl.pallas_call(
        body, name=name, out_shape=out_shape,
        in_specs=[ANY] * (nbig + nsmall), out_specs=tuple([ANY] * (nbig + nsmall)),
        scratch_shapes=[pltpu.SemaphoreType.DMA((n_ici + n_fwd,)), pltpu.SemaphoreType.DMA((n_ici + n_fwd,))],
    )(*big, *small)


HBM = pl.BlockSpec(memory_space=pltpu.HBM)
SEM = pl.BlockSpec(memory_space=pltpu.SEMAPHORE)
DATAFLOW = pltpu.SideEffectType.DATAFLOW_SIDE_EFFECTING


def _split_peers(pattern, x, y, c):
    if pattern == "swap":
        return [((x, y, 1 - c), 1 - c, None, None)]
    me = 2 * x + y
    return [((px, py, c), 2 * px + py if pattern == "scatter" else None, me, 2 * px + py)
            for px, py in _other_chips(x, y)]


def _split_land_shape(pattern, shape):
    return {"bcast": (N_CHIPS,) + shape, "scatter": shape, "swap": shape[:1] + shape[2:]}[pattern]


def _split_copies(pattern, srcs, lands, send_sems, recv_sems, waiting):
    x, y, c = _mesh_pos()
    peers = _split_peers(pattern, x, y, c)
    cps = []
    for j, (dev, src_slot, dst_slot, my_slot) in enumerate(peers):
        for a in range(len(srcs)):
            if src_slot is None:
                src = srcs[a]
            else:
                src = srcs[a].at[:, src_slot] if pattern == "swap" else srcs[a].at[src_slot]
            slot = my_slot if waiting else dst_slot
            dst = lands[a] if slot is None else lands[a].at[slot]
            k = a * len(peers) + j
            cps.append(pltpu.make_async_remote_copy(src_ref=src, dst_ref=dst, send_sem=send_sems[k],
                                                    recv_sem=recv_sems[k], device_id=dev, device_id_type=MESH))
    return cps


def _split_start(arrs, pattern, after, name):
    n = len(arrs)
    nsem = n * (1 if pattern == "swap" else N_CHIPS - 1)
    deps = [] if after is None else [after]

    def body(*refs):
        srcs, lands = refs[:n], refs[n:2 * n]
        outs = refs[2 * n + len(deps):]
        for cp in _split_copies(pattern, srcs, lands, outs[:nsem], outs[nsem:2 * nsem], waiting=False):
            cp.start()
        outs[-1][...] = jnp.zeros_like(outs[-1])

    lands = [lax.empty(_split_land_shape(pattern, a.shape), a.dtype) for a in arrs]
    out_shape = ([pltpu.SemaphoreType.DMA(())] * (2 * nsem)
                 + [pltpu.HBM(a.shape, a.dtype) for a in arrs] + [pltpu.HBM(b.shape, b.dtype) for b in lands]
                 + [jax.ShapeDtypeStruct((SUBLANES, LANES), F32)])
    outs = pl.pallas_call(
        body, name=name, out_shape=tuple(out_shape),
        in_specs=[HBM] * (2 * n) + [ANY] * len(deps),
        out_specs=tuple([SEM] * (2 * nsem) + [HBM] * (2 * n) + [pl.BlockSpec(memory_space=pltpu.VMEM)]),
        input_output_aliases={a: 2 * nsem + a for a in range(2 * n)},
        compiler_params=pltpu.CompilerParams(has_side_effects=DATAFLOW),
    )(*[pltpu.with_memory_space_constraint(a, pltpu.HBM) for a in list(arrs) + lands], *deps)
    return outs[:-1], outs[-1]


def _split_wait(state, n, pattern, after, name):
    nsem = n * (1 if pattern == "swap" else N_CHIPS - 1)

    def body(*refs):
        srcs, lands = refs[:n], refs[n:2 * n]
        send_sems, recv_sems = refs[2 * n:2 * n + nsem], refs[2 * n + nsem:2 * n + 2 * nsem]
        for cp in _split_copies(pattern, srcs, lands, send_sems, recv_sems, waiting=True):
            cp.wait_send()
            cp.wait_recv()

    sems, thru = state[:2 * nsem], state[2 * nsem:]
    outs = pl.pallas_call(
        body, name=name, out_shape=tuple(pltpu.HBM(a.shape, a.dtype) for a in thru),
        in_specs=[HBM] * (2 * n) + [SEM] * (2 * nsem) + [ANY],
        out_specs=tuple([HBM] * (2 * n)),
        input_output_aliases={a: a for a in range(2 * n)},
        compiler_params=pltpu.CompilerParams(has_side_effects=DATAFLOW),
    )(*thru, *sems, after)
    return outs[:n], outs[n:]


def _pair_gather(arrs, layer, name):
    n = len(arrs)

    def body(*refs):
        outs = refs[n:2 * n]
        send_sems, recv_sems = refs[2 * n:]
        x, y, c = _mesh_pos()
        cps = [pltpu.make_async_remote_copy(src_ref=outs[a].at[layer, c], dst_ref=outs[a].at[layer, c],
                                            send_sem=send_sems.at[a], recv_sem=recv_sems.at[a],
                                            device_id=(x, y, 1 - c), device_id_type=MESH)
               for a in range(n)]
        for cp in cps:
            cp.start()
        for cp in cps:
            cp.wait()

    return pl.pallas_call(
        body, name=name, out_shape=tuple(jax.ShapeDtypeStruct(a.shape, a.dtype) for a in arrs),
        in_specs=[ANY] * n, out_specs=tuple([ANY] * n),
        input_output_aliases={a: a for a in range(n)},
        scratch_shapes=[pltpu.SemaphoreType.DMA((n,)), pltpu.SemaphoreType.DMA((n,))],
    )(*arrs)


N_DEV = 8


def _allreduce_small(pack, name):
    r = pack.shape[0]

    def body(p_ref, o_ref, land, send_sems, recv_sems):
        x, y, c = _mesh_pos()
        me = 4 * x + 2 * y + c
        cps = []
        for k in range(1, N_DEV):
            peer = (x ^ (k >> 2), y ^ ((k >> 1) & 1), c ^ (k & 1))
            cps.append(pltpu.make_async_remote_copy(src_ref=p_ref, dst_ref=land.at[me], send_sem=send_sems.at[k - 1],
                                                    recv_sem=recv_sems.at[k - 1], device_id=peer, device_id_type=MESH))
        for cp in cps:
            cp.start()
        land[me] = p_ref[...]
        for cp in cps:
            cp.wait()
        total = land[0]
        for d in range(1, N_DEV):
            total = total + land[d]
        o_ref[...] = total

    vm = pl.BlockSpec(memory_space=pltpu.VMEM)
    return pl.pallas_call(
        body, name=name, out_shape=jax.ShapeDtypeStruct(pack.shape, F32),
        in_specs=[vm], out_specs=vm,
        scratch_shapes=[pltpu.VMEM((N_DEV, r, LANES), F32), pltpu.SemaphoreType.DMA((N_DEV - 1,)),
                        pltpu.SemaphoreType.DMA((N_DEV - 1,))],
    )(pack)


BIG_ROWS = 128


def _cast_layer(w, layer, name):
    _, r, cdim = w.shape
    tr = BIG_ROWS

    def body(w_ref, o_ref):
        o_ref[...] = w_ref[...].astype(o_ref.dtype)

    return pl.pallas_call(
        body, name=name, out_shape=jax.ShapeDtypeStruct((r, cdim), MXU_DTYPE),
        grid=(r // tr,), in_specs=[pl.BlockSpec((None, tr, cdim), lambda i: (layer, i, 0))],
        out_specs=pl.BlockSpec((tr, cdim), lambda i: (i, 0)),
        compiler_params=_params(("parallel",)),
    )(w)


def _cast_cols_major(w_t, name):
    cdim, nl, r = w_t.shape
    tc = LANES

    def body(w_ref, *o_refs):
        for l in range(nl):
            o_refs[l][...] = w_ref[:, l, :].T.astype(o_refs[l].dtype)

    out = pl.BlockSpec((r, tc), lambda i: (0, i))
    return pl.pallas_call(
        body, name=name, out_shape=tuple(jax.ShapeDtypeStruct((r, cdim), MXU_DTYPE) for _ in range(nl)),
        grid=(pl.cdiv(cdim, tc),), in_specs=[pl.BlockSpec((tc, nl, r), lambda i: (i, 0, 0))],
        out_specs=tuple([out] * nl),
        compiler_params=_params(("parallel",)),
    )(w_t)


def _pair_sum(parts, sib, which, out_dtype, name):
    k, _, r, cdim = parts.shape
    tr = BIG_ROWS

    def body(sel_ref, p_ref, s_ref, o_ref):
        o_ref[...] = (p_ref[...] + s_ref[...]).astype(o_ref.dtype)

    grid_spec = pltpu.PrefetchScalarGridSpec(
        num_scalar_prefetch=1, grid=(k, r // tr),
        in_specs=[pl.BlockSpec((None, None, tr, cdim), lambda l, i, sel: (l, sel[0], i, 0)),
                  pl.BlockSpec((None, tr, cdim), lambda l, i, sel: (l, i, 0))],
        out_specs=pl.BlockSpec((None, tr, cdim), lambda l, i, sel: (l, i, 0)))
    return pl.pallas_call(
        body, name=name, out_shape=jax.ShapeDtypeStruct((k, r, cdim), out_dtype), grid_spec=grid_spec,
        compiler_params=_params(("parallel", "parallel")),
    )(which.reshape(1).astype(jnp.int32), parts, sib)


def _sum_lead(parts, into, layer, which, name):
    k, r, cdim = parts.shape
    tr = BIG_ROWS

    def body(sel_ref, p_ref, _, o_ref):
        total = p_ref[0].astype(F32)
        for a in range(1, k):
            total = total + p_ref[a].astype(F32)
        o_ref[...] = total

    grid_spec = pltpu.PrefetchScalarGridSpec(
        num_scalar_prefetch=1, grid=(r // tr,),
        in_specs=[pl.BlockSpec((k, tr, cdim), lambda i, sel: (0, i, 0)), ANY],
        out_specs=pl.BlockSpec((None, None, tr, cdim), lambda i, sel: (layer, sel[0], i, 0)))
    return pl.pallas_call(
        body, name=name, out_shape=jax.ShapeDtypeStruct(into.shape, F32), grid_spec=grid_spec,
        input_output_aliases={2: 0},
        compiler_params=_params(("parallel",)),
    )(which.reshape(1).astype(jnp.int32), parts, into)


def _adam_math(w, g, m, v):
    m2 = ADAM_B1 * m + (1.0 - ADAM_B1) * g
    v2 = ADAM_B2 * v + (1.0 - ADAM_B2) * (g * g)
    m_hat = m2 / (1.0 - ADAM_B1 ** ADAM_STEP)
    v_hat = v2 / (1.0 - ADAM_B2 ** ADAM_STEP)
    delta = -ADAM_LR * (m_hat / (jnp.sqrt(v_hat) + ADAM_EPS) + ADAM_WD * w)
    return delta, m2, v2


def _adam_big(w, g, m, v, name):
    nl, r, cdim = w.shape
    tr = BIG_ROWS

    def body(w_ref, g_ref, m_ref, v_ref, d_ref, mo_ref, vo_ref):
        delta, m2, v2 = _adam_math(w_ref[...], g_ref[...], m_ref[...], v_ref[...])
        d_ref[...] = delta
        mo_ref[...] = m2
        vo_ref[...] = v2

    blk = pl.BlockSpec((None, tr, cdim), lambda l, i: (l, i, 0))
    shp = jax.ShapeDtypeStruct(w.shape, F32)
    return pl.pallas_call(
        body, name=name, out_shape=(shp, shp, shp),
        grid=(nl, r // tr), in_specs=[blk] * 4, out_specs=(blk, blk, blk),
        compiler_params=_params(("parallel", "parallel")),
    )(w, g, m, v)


def _adam_cols_major(w, g, m, v, name):
    cdim, nl, r = w.shape
    tc = BIG_ROWS

    def body(w_ref, g_ref, m_ref, v_ref, d_ref, mo_ref, vo_ref):
        delta, m2, v2 = _adam_math(w_ref[...], g_ref[...], m_ref[...], v_ref[...])
        d_ref[...] = delta
        mo_ref[...] = m2
        vo_ref[...] = v2

    blk = pl.BlockSpec((tc, nl, r), lambda i: (i, 0, 0))
    shp = jax.ShapeDtypeStruct(w.shape, F32)
    return pl.pallas_call(
        body, name=name, out_shape=(shp, shp, shp),
        grid=(pl.cdiv(cdim, tc),), in_specs=[blk] * 4, out_specs=(blk, blk, blk),
        compiler_params=_params(("parallel",)),
    )(w, g, m, v)


def _adam_small(ws, gs, ms, vs, name):
    n = len(ws)

    def body(*refs):
        w_refs, g_refs, m_refs, v_refs = (refs[k * n:(k + 1) * n] for k in range(4))
        d_refs, mo_refs, vo_refs = (refs[(4 + k) * n:(5 + k) * n] for k in range(3))
        for a in range(n):
            delta, m2, v2 = _adam_math(w_refs[a][...], g_refs[a][...], m_refs[a][...], v_refs[a][...])
            d_refs[a][...] = delta
            mo_refs[a][...] = m2
            vo_refs[a][...] = v2

    shapes = tuple(jax.ShapeDtypeStruct(w.shape, F32) for w in ws)
    vm = pl.BlockSpec(memory_space=pltpu.VMEM)
    outs = pl.pallas_call(body, name=name, out_shape=shapes * 3, in_specs=[vm] * (4 * n),
                          out_specs=tuple([vm] * (3 * n)))(*ws, *gs, *ms, *vs)
    return outs[:n], outs[n:2 * n], outs[2 * n:]


PACK_TILE = SUBLANES * LANES


def _pack(arrays):
    rows = []
    for a in arrays:
        flat = a.reshape(-1)
        pad = (-flat.shape[0]) % PACK_TILE
        if pad:
            flat = jnp.concatenate([flat, jnp.zeros((pad,), flat.dtype)])
        rows.append(flat.reshape(-1, LANES))
    return jnp.concatenate(rows, axis=0)


def _unpack(pack, shapes):
    outs, row = [], 0
    for shp in shapes:
        n = int(np.prod(shp))
        nrows = -(-n // PACK_TILE) * SUBLANES
        outs.append(pack[row:row + nrows].reshape(-1)[:n].reshape(shp))
        row += nrows
    return outs


SMALL = ["norm_w", "ssd_conv_b", "ssd_dt_bias", "ssd_a_log", "ssd_d", "ssd_norm_w", "attn_sinks",
         "conf_dw_b", "conf_ln_w", "conf_ln_b"]
WEIGHTS = ["norm_w", "w_in", "ssd_conv_w", "ssd_conv_b", "ssd_dt_bias", "ssd_a_log", "ssd_d", "ssd_norm_w",
           "attn_sinks", "conf_dw_w", "conf_dw_b", "conf_ln_w", "conf_ln_b", "w_out", "final_norm_w"]


def kernel(x, norm_w, w_in, ssd_conv_w, ssd_conv_b, ssd_dt_bias, ssd_a_log, ssd_d, ssd_norm_w, attn_sinks, conf_dw_w, conf_dw_b, conf_ln_w, conf_ln_b, w_out, final_norm_w, loss_target, m_norm_w, m_w_in, m_ssd_conv_w, m_ssd_conv_b, m_ssd_dt_bias, m_ssd_a_log, m_ssd_d, m_ssd_norm_w, m_attn_sinks, m_conf_dw_w, m_conf_dw_b, m_conf_ln_w, m_conf_ln_b, m_w_out, m_final_norm_w, v_norm_w, v_w_in, v_ssd_conv_w, v_ssd_conv_b, v_ssd_dt_bias, v_ssd_a_log, v_ssd_d, v_ssd_norm_w, v_attn_sinks, v_conf_dw_w, v_conf_dw_b, v_conf_ln_w, v_conf_ln_b, v_w_out, v_final_norm_w):
    w = dict(norm_w=norm_w, w_in=w_in, ssd_conv_w=ssd_conv_w, ssd_conv_b=ssd_conv_b, ssd_dt_bias=ssd_dt_bias,
             ssd_a_log=ssd_a_log, ssd_d=ssd_d, ssd_norm_w=ssd_norm_w, attn_sinks=attn_sinks, conf_dw_w=conf_dw_w,
             conf_dw_b=conf_dw_b, conf_ln_w=conf_ln_w, conf_ln_b=conf_ln_b, w_out=w_out, final_norm_w=final_norm_w)
    m = dict(norm_w=m_norm_w, w_in=m_w_in, ssd_conv_w=m_ssd_conv_w, ssd_conv_b=m_ssd_conv_b,
             ssd_dt_bias=m_ssd_dt_bias, ssd_a_log=m_ssd_a_log, ssd_d=m_ssd_d, ssd_norm_w=m_ssd_norm_w,
             attn_sinks=m_attn_sinks, conf_dw_w=m_conf_dw_w, conf_dw_b=m_conf_dw_b, conf_ln_w=m_conf_ln_w,
             conf_ln_b=m_conf_ln_b, w_out=m_w_out, final_norm_w=m_final_norm_w)
    v = dict(norm_w=v_norm_w, w_in=v_w_in, ssd_conv_w=v_ssd_conv_w, ssd_conv_b=v_ssd_conv_b,
             ssd_dt_bias=v_ssd_dt_bias, ssd_a_log=v_ssd_a_log, ssd_d=v_ssd_d, ssd_norm_w=v_ssd_norm_w,
             attn_sinks=v_attn_sinks, conf_dw_w=v_conf_dw_w, conf_dw_b=v_conf_dw_b, conf_ln_w=v_conf_ln_w,
             conf_ln_b=v_conf_ln_b, w_out=v_w_out, final_norm_w=v_final_norm_w)
    depth = w_in.shape[0]
    me = 2 * lax.axis_index("x") + lax.axis_index("y")

    assert depth == 2
    w_in_t = jnp.transpose(w_in, (2, 0, 1))
    w_in_b = _cast_cols_major(w_in_t, name="cast_w_in")
    w_out_b = [_cast_layer(w_out, li, name=f"cast_w_out_l{li}") for li in range(depth)]
    own0 = [w_in_b[0].reshape((2, -1) + w_in_b[0].shape[1:]), ssd_conv_w, conf_dw_w]
    gathered0 = _gather_weights(own0[:1], own0[1:], name="gather_weights_l0")
    g_in0, g_conv, g_dw = [lax.dynamic_update_index_in_dim(g_all, mine, me, 0)
                           for g_all, mine in zip(gathered0, own0)]
    own1 = [w_out_b[0], w_in_b[1], w_out_b[1]]
    pending1, token1 = _split_start(own1, "bcast", gathered0[0], name="gather_rest_start")
    rest = {}

    def small_full(li):
        return (jnp.concatenate([g_conv[p, li] for p in range(N_CHIPS)], axis=1),
                jnp.concatenate([g_dw[p, li] for p in range(N_CHIPS)], axis=1))

    def w_out_l0(after):
        mine1, landed = _split_wait(pending1, len(own1), "bcast", after, name="gather_rest_wait")
        rest["landed"] = [lax.dynamic_update_index_in_dim(g_all, mine, me, 0) for g_all, mine in zip(landed, mine1)]
        return rest["landed"][0].reshape(-1, w_out.shape[2])

    def params_l0(_):
        w_in_p = _padded_from_chips([g_in0[p].reshape(w_in_b[0].shape) for p in range(N_CHIPS)])
        return _layer_params(0, w_in_p, w_out_l0, *small_full(0), w)

    def params_l1(_):
        _, g_in1, g_out1 = rest["landed"]
        w_in_p = _padded_from_chips([g_in1[p] for p in range(N_CHIPS)])
        return _layer_params(1, w_in_p, g_out1.reshape(-1, g_out1.shape[-1]), *small_full(1), w)

    c = lax.axis_index("c")
    cols = w_in.shape[2]
    rows_out = w_out.shape[1]

    def grad_parts(g):
        dw = g["w_in_p"]
        return [dw.reshape(1, 2, dw.shape[0] // 2, dw.shape[1]),
                g["w_out"].reshape(N_CHIPS, 2, rows_out // 2, D_MODEL)]

    def pair_sums(parts, sib, tag):
        s_in, s_out = [_pair_sum(p, sb, c, MXU_DTYPE, name=f"grad_pair_sum_{k}_{tag}")
                       for k, (p, sb) in enumerate(zip(parts, sib))]
        return [jnp.stack([_chip_part_from_padded(s_in[0], p, cols) for p in range(N_CHIPS)]), s_out]

    split = {"reduced": [lax.empty((depth, 2, w_in.shape[1] // 2, cols), F32),
                         lax.empty((depth, 2, rows_out // 2, D_MODEL), F32)]}

    def chip_sums(landed, sent, li):
        filled = [lax.dynamic_update_index_in_dim(r, lax.dynamic_index_in_dim(sk, me, 0, keepdims=False), me, 0)
                  for r, sk in zip(landed, sent)]
        halves = [_sum_lead(r, into, li, c, name=f"grad_chip_sum_{k}_l{li}")
                  for k, (r, into) in enumerate(zip(filled, split["reduced"]))]
        split["reduced"] = list(_pair_gather(halves, li, name=f"grad_pair_gather_l{li}"))

    def bwd_hooks(li):
        def after_dw(dw_in_p, dw_out):
            parts = grad_parts(dict(w_in_p=dw_in_p, w_out=dw_out))
            state, token = _split_start(parts, "swap", None, name=f"grad_swap_l{li}_start")
            split[f"swap{li}"] = (parts, state)
            return token

        hooks = {"after_dw": after_dw}
        if li == depth - 2:
            parts, swap_state = split[f"swap{depth - 1}"]

            def after_dycat(dycat):
                mine, sib = _split_wait(swap_state, len(parts), "swap", dycat, name="grad_swap_l1_wait")
                sent = pair_sums(mine, sib, "l1")
                split["scatter"], token = _split_start(sent, "scatter", None, name="grad_scatter_l1_start")
                return token

            def after_attn(dproj):
                sent, landed = _split_wait(split["scatter"], len(parts), "scatter", dproj,
                                           name="grad_scatter_l1_wait")
                chip_sums(landed, sent, depth - 1)

            hooks.update(after_dycat=after_dycat, after_attn=after_attn)
        return hooks

    loss, grad_x, grads, dfinal = _local_step(x, loss_target, [params_l0, params_l1], final_norm_w,
                                              first_after=token1, bwd_hooks=bwd_hooks)

    parts0, swap0 = split["swap0"]
    sent0 = pair_sums(*_split_wait(swap0, len(parts0), "swap", grad_x, name="grad_swap_l0_wait"), "l0")
    scatter0, token0 = _split_start(sent0, "scatter", None, name="grad_scatter_l0_start")

    small_list = [grads[li][n] for li in range(depth) for n in SMALL]
    small_list += [grads[li][n] for li in range(depth) for n in ("ssd_conv_w", "conf_dw_w")]
    small_list += [dfinal, loss.reshape(1)]
    small_shapes = [a.shape for a in small_list]
    reduced = _unpack(_allreduce_small(_pack(small_list) + token0[0, 0], name="allreduce_small"), small_shapes)
    ns = len(SMALL)
    g = {n: jnp.stack([reduced[li * ns + i] for li in range(depth)]) for i, n in enumerate(SMALL)}
    conv_w_cols, dw_w_cols = ssd_conv_w.shape[2], conf_dw_w.shape[2]
    g["ssd_conv_w"] = jnp.stack([lax.dynamic_slice_in_dim(reduced[depth * ns + 2 * li], me * conv_w_cols,
                                                          conv_w_cols, axis=1) for li in range(depth)])
    g["conf_dw_w"] = jnp.stack([lax.dynamic_slice_in_dim(reduced[depth * ns + 2 * li + 1], me * dw_w_cols,
                                                         dw_w_cols, axis=1) for li in range(depth)])
    g["final_norm_w"] = reduced[-2]
    loss_total = reduced[-1][0]

    small_names = [n for n in WEIGHTS if n not in ("w_in", "w_out")]

    def as2d(a):
        return a.reshape(1, -1) if a.ndim == 1 else a

    deltas, new_ms, new_vs = _adam_small(*[[as2d(src[n]) for n in small_names] for src in (w, g, m, v)],
                                         name="adam_small")

    sent0, landed0 = _split_wait(scatter0, len(sent0), "scatter", deltas[0], name="grad_scatter_l0_wait")
    chip_sums(landed0, sent0, 0)
    g_w_in = split["reduced"][0].reshape(w_in.shape)
    g_w_out = split["reduced"][1].reshape(w_out.shape)

    outs_g, outs_d, outs_m, outs_v = {"w_in": g_w_in, "w_out": g_w_out}, {}, {}, {}
    to_cols, from_cols = (2, 0, 1), (1, 2, 0)
    outs_d["w_in"], outs_m["w_in"], outs_v["w_in"] = [
        jnp.transpose(a, from_cols) for a in _adam_cols_major(
            *[jnp.transpose(a, to_cols) for a in (w_in, g_w_in, m_w_in, v_w_in)], name="adam_w_in")]
    outs_d["w_out"], outs_m["w_out"], outs_v["w_out"] = _adam_big(w_out, g_w_out, m_w_out, v_w_out,
                                                                  name="adam_w_out")
    for n, dn, mn, vn in zip(small_names, deltas, new_ms, new_vs):
        outs_g[n], outs_d[n], outs_m[n], outs_v[n] = (g[n], dn.reshape(w[n].shape), mn.reshape(w[n].shape),
                                                      vn.reshape(w[n].shape))
    return (loss_total, grad_x, *[outs_g[n] for n in WEIGHTS], *[outs_d[n] for n in WEIGHTS],
            *[outs_m[n] for n in WEIGHTS], *[outs_v[n] for n in WEIGHTS])
```

```python
import functools
import math

import jax
import jax.numpy as jnp
import numpy as np
from jax import lax
from jax.experimental import pallas as pl
from jax.experimental.pallas import tpu as pltpu

F32 = jnp.float32
BF16 = jnp.bfloat16
MXU_DTYPE = BF16

D_MODEL = 1024
DEPTH = 2
SSD_HEADS = 16
SSD_HEAD_DIM = 64
SSD_STATE = 128
SSD_CONV = 4
CHUNK = 128
SSD_CONV_DIM = 1536
ATTN_HEAD_DIM = 64
ATTN_Q_HEADS = 8
WINDOW = 128
CONF_WIDTH = 512
CONF_KERNEL = 31
MIX_WIDTH = 2048
D_IN_PROJ = 5392
EPS = 1e-5

ADAM_LR = 0.001
ADAM_B1 = 0.9
ADAM_B2 = 0.999
ADAM_EPS = 1e-08
ADAM_WD = 0.01
ADAM_STEP = 10

LANES = 128
SUBLANES = 8
VMEM_LIMIT = 48 * 1024 * 1024

NP = 5632
OFF_ZA, OFF_Q, OFF_K, OFF_V, OFF_DT = 0, 512, 1024, 1152, 1280
ATTN_GROUP = 1536
OFF_CONF, OFF_ZC = 1536, 2560
CONF_GROUP = 1536
OFF_ZS = 3072
OFF_XBC = 4096
SECTIONS = ((0, 1024, OFF_ZS), (1024, 1536, OFF_ZA), (1536, 2048, OFF_ZC), (2048, 3584, OFF_XBC),
            (3584, 3600, OFF_DT), (3600, 4368, OFF_Q), (4368, 5392, OFF_CONF))

YCAT_ATTN, YCAT_CONF = 1024, 1536
ANY = pl.BlockSpec(memory_space=pl.ANY)

NN = (((1,), (0,)), ((), ()))
NT = (((1,), (1,)), ((), ()))
TN = (((0,), (0,)), ((), ()))


def _params(sem):
    return pltpu.CompilerParams(dimension_semantics=sem, vmem_limit_bytes=VMEM_LIMIT)


def _dot(a, b, dims=NN):
    return lax.dot_general(a.astype(MXU_DTYPE), b.astype(MXU_DTYPE), dims, preferred_element_type=F32)


def _split_bf16(a, passes):
    pieces = []
    r = a
    for _ in range(passes):
        p = r.astype(BF16)
        pieces.append(p)
        r = r - p.astype(F32)
    return pieces


def _xdot(a, sel, dims=NN, passes=2):
    out = None
    for p in _split_bf16(a, passes):
        t = lax.dot_general(p, sel, dims, preferred_element_type=F32)
        out = t if out is None else out + t
    return out


def _xdot_r(sel, b, dims=NN, passes=3):
    out = None
    for p in _split_bf16(b, passes):
        t = lax.dot_general(sel, p, dims, preferred_element_type=F32)
        out = t if out is None else out + t
    return out


def _sigmoid(x):
    return 1.0 / (1.0 + jnp.exp(-x))


def _silu(x):
    return x * _sigmoid(x)


def _dsilu(x):
    s = _sigmoid(x)
    return s * (1.0 + x * (1.0 - s))


def _softplus(x):
    return jnp.maximum(x, 0.0) + jnp.log(1.0 + jnp.exp(-jnp.abs(x)))


def _rowsum8(x):
    r, c = x.shape
    return jnp.sum(x.reshape(r // SUBLANES, SUBLANES, c), axis=0)


def _iota(shape, dim):
    return lax.broadcasted_iota(jnp.int32, shape, dim)


def _matmul(a, b, form, out_dtype, tm, tn, tk, name, residual=None, after=None):
    if form == "nn":
        (m, k), n = a.shape, b.shape[1]
    elif form == "nt":
        (m, k), n = a.shape, b.shape[0]
    else:
        (k, m), n = a.shape, b.shape[1]
    tm, tn, tk = min(tm, m), min(tn, n), min(tk, k)
    assert m % tm == 0 and n % tn == 0 and k % tk == 0, (name, m, n, k, tm, tn, tk)
    if form == "nn":
        a_spec = pl.BlockSpec((tm, tk), lambda i, j, s: (i, s))
        b_spec = pl.BlockSpec((tk, tn), lambda i, j, s: (s, j))
        dims = NN
    elif form == "nt":
        (m, k), n = a.shape, b.shape[0]
        a_spec = pl.BlockSpec((tm, tk), lambda i, j, s: (i, s))
        b_spec = pl.BlockSpec((tn, tk), lambda i, j, s: (j, s))
        dims = NT
    else:
        (k, m), n = a.shape, b.shape[1]
        a_spec = pl.BlockSpec((tk, tm), lambda i, j, s: (s, i))
        b_spec = pl.BlockSpec((tk, tn), lambda i, j, s: (s, j))
        dims = TN
    nk = k // tk
    has_res = residual is not None
    deps = [] if after is None else [after]

    def body_single(a_ref, b_ref, *rest):
        o = _dot(a_ref[...], b_ref[...], dims)
        if has_res:
            o = o + rest[0][...]
        rest[-1][...] = o.astype(out_dtype)

    def body(a_ref, b_ref, *rest):
        r_ref = rest[0] if has_res else None
        o_ref, acc = rest[-2:]
        s = pl.program_id(2)

        @pl.when(s == 0)
        def _():
            acc[...] = jnp.zeros_like(acc)

        acc[...] += _dot(a_ref[...], b_ref[...], dims)

        @pl.when(s == nk - 1)
        def _():
            o = acc[...]
            if has_res:
                o = o + r_ref[...]
            o_ref[...] = o.astype(out_dtype)

    in_specs = [a_spec, b_spec]
    args = [a, b]
    if has_res:
        in_specs.append(pl.BlockSpec((tm, tn), lambda i, j, s: (i, j)))
        args.append(residual)
    in_specs += [ANY] * len(deps)
    args += deps
    return pl.pallas_call(
        body_single if nk == 1 else body, name=name,
        out_shape=jax.ShapeDtypeStruct((m, n), out_dtype),
        grid=(m // tm, n // tn, nk),
        in_specs=in_specs,
        out_specs=pl.BlockSpec((tm, tn), lambda i, j, s: (i, j)),
        scratch_shapes=[] if nk == 1 else [pltpu.VMEM((tm, tn), F32)],
        compiler_params=_params(("parallel", "parallel", "arbitrary")),
    )(*args)


ROW_TILE = 256


PROJ_FWD_TM, PROJ_FWD_TN = 1024, 512


def _proj_fwd(x, w, w_in_p, name, after=None):
    t, d = x.shape
    n = w_in_p.shape[1]
    tm, tn = min(PROJ_FWD_TM, t), PROJ_FWD_TN
    assert t % tm == 0 and n % tn == 0
    deps = [] if after is None else [after]

    def body(x_ref, w_ref, b_ref, *rest):
        o_ref, ot_ref, h_scr = rest[len(deps):]

        @pl.when(pl.program_id(1) == 0)
        def _():
            xv = x_ref[...]
            rstd = lax.rsqrt(jnp.mean(xv * xv, axis=-1, keepdims=True) + EPS)
            h = xv * rstd * w_ref[...]
            h_scr[...] = h.astype(h_scr.dtype)
            ot_ref[...] = h.T.astype(ot_ref.dtype)

        o_ref[...] = _dot(h_scr[...], b_ref[...])

    return pl.pallas_call(
        body, name=name,
        out_shape=(jax.ShapeDtypeStruct((t, n), F32), jax.ShapeDtypeStruct((d, t), MXU_DTYPE)),
        grid=(t // tm, n // tn),
        in_specs=[pl.BlockSpec((tm, d), lambda i, j: (i, 0)), pl.BlockSpec((1, d), lambda i, j: (0, 0)),
                  pl.BlockSpec((d, tn), lambda i, j: (0, j))] + [ANY] * len(deps),
        out_specs=(pl.BlockSpec((tm, tn), lambda i, j: (i, j)), pl.BlockSpec((d, tm), lambda i, j: (0, i))),
        scratch_shapes=[pltpu.VMEM((tm, d), MXU_DTYPE)],
        compiler_params=_params(("parallel", "arbitrary")),
    )(x, w, w_in_p, *deps)


PROJ_BWD_TM, PROJ_BWD_TK = 1024, 1408


def _proj_bwd_dx(dproj, w_in_p, x, w, dres, name):
    t, d = x.shape
    kdim = dproj.shape[1]
    tm, tk = min(PROJ_BWD_TM, t), PROJ_BWD_TK
    nt, nk = t // tm, kdim // tk
    assert t % tm == 0 and kdim % tk == 0

    def body(a_ref, b_ref, x_ref, w_ref, dr_ref, dx_ref, dw_ref, acc, wacc):
        i, s = pl.program_id(0), pl.program_id(1)

        @pl.when((i == 0) & (s == 0))
        def _():
            wacc[...] = jnp.zeros_like(wacc)

        @pl.when(s == 0)
        def _():
            acc[...] = jnp.zeros_like(acc)

        acc[...] += _dot(a_ref[...], b_ref[...], NT)

        @pl.when(s == nk - 1)
        def _():
            xv = x_ref[...]
            rstd = lax.rsqrt(jnp.mean(xv * xv, axis=-1, keepdims=True) + EPS)
            xh = xv * rstd
            dhv = acc[...]
            g = dhv * w_ref[...]
            dx_ref[...] = dr_ref[...] + rstd * (g - xh * jnp.mean(g * xh, axis=-1, keepdims=True))
            wacc[...] += _rowsum8(dhv * xh)

        @pl.when((i == nt - 1) & (s == nk - 1))
        def _():
            dw_ref[...] = jnp.sum(wacc[...], axis=0, keepdims=True)

    row = pl.BlockSpec((tm, d), lambda i, s: (i, 0))
    vec = pl.BlockSpec((1, d), lambda i, s: (0, 0))
    return pl.pallas_call(
        body, name=name,
        out_shape=(jax.ShapeDtypeStruct((t, d), F32), jax.ShapeDtypeStruct((1, d), F32)),
        grid=(nt, nk),
        in_specs=[pl.BlockSpec((tm, tk), lambda i, s: (i, s)), pl.BlockSpec((d, tk), lambda i, s: (0, s)),
                  row, vec, row],
        out_specs=(row, vec),
        scratch_shapes=[pltpu.VMEM((tm, d), F32), pltpu.VMEM((SUBLANES, d), F32)],
        compiler_params=_params(("arbitrary", "arbitrary")),
    )(dproj, w_in_p, x, w, dres)


def _loss_head(xf, target, w, name):
    t, d = xf.shape
    tm = ROW_TILE
    nt = t // tm

    def body(x_ref, t_ref, w_ref, loss_ref, dx_ref, dw_ref, lacc, wacc):
        i = pl.program_id(0)

        @pl.when(i == 0)
        def _():
            lacc[...] = jnp.zeros_like(lacc)
            wacc[...] = jnp.zeros_like(wacc)

        xv = x_ref[...]
        rstd = lax.rsqrt(jnp.mean(xv * xv, axis=-1, keepdims=True) + EPS)
        xh = xv * rstd
        err = xh * w_ref[...] - t_ref[...]
        lacc[...] += jnp.sum(err * err)
        dy = err * (1.0 / d)
        g = dy * w_ref[...]
        dx_ref[...] = rstd * (g - xh * jnp.mean(g * xh, axis=-1, keepdims=True))
        wacc[...] += _rowsum8(dy * xh)

        @pl.when(i == nt - 1)
        def _():
            loss_ref[...] = lacc[...] * (0.5 / d)
            dw_ref[...] = jnp.sum(wacc[...], axis=0, keepdims=True)

    row = pl.BlockSpec((tm, d), lambda i: (i, 0))
    vec = pl.BlockSpec((1, d), lambda i: (0, 0))
    return pl.pallas_call(
        body, name=name,
        out_shape=(jax.ShapeDtypeStruct((SUBLANES, LANES), F32), jax.ShapeDtypeStruct((t, d), F32),
                   jax.ShapeDtypeStruct((1, d), F32)),
        grid=(nt,),
        in_specs=[row, row, vec],
        out_specs=(pl.BlockSpec((SUBLANES, LANES), lambda i: (0, 0)), row, vec),
        scratch_shapes=[pltpu.VMEM((SUBLANES, LANES), F32), pltpu.VMEM((SUBLANES, d), F32)],
        compiler_params=_params(("arbitrary",)),
    )(xf, target, w)


CONV_TILE = 512
CONV_COLS = 512
CONV_SUB_ROWS = 128
CONV_SUB_COLS = LANES


def _conv_halo(k):
    return SUBLANES if k - 1 <= SUBLANES else 32


def _conv_subtiles(tm, cw):
    return [(r0, c0) for r0 in range(0, tm, CONV_SUB_ROWS) for c0 in range(0, cw, CONV_SUB_COLS)]


def _conv_use_shifted(k):
    return k > SUBLANES


def _conv_shift_scratch(k, rows, cw):
    return [pltpu.VMEM((SUBLANES - 1, rows - SUBLANES, cw), F32)] if _conv_use_shifted(k) else []


def _conv_fill_shifted(ext, sh):
    n = sh.shape[1]
    for b in range(1, SUBLANES):
        sh[b - 1] = ext[b:b + n, :]


def _conv_rows(ext, sh, start, rows, cs):
    b = start % SUBLANES
    if b == 0 or not sh:
        return ext[start:start + rows, cs]
    return sh[0][b - 1, start - b:start - b + rows, cs]


def _conv_fwd(src, col0, width, w, bias, k, seq, name):
    t = src.shape[0]
    tm, cw, halo = CONV_TILE, CONV_COLS, _conv_halo(k)
    sr, sc = CONV_SUB_ROWS, CONV_SUB_COLS
    p = k - 1
    cb0 = col0 // cw
    kp = w.shape[0]

    shifted = _conv_use_shifted(k)

    def body(x_ref, h_ref, w_ref, b_ref, o_ref, ext, *sh):
        i = pl.program_id(0)
        seq_start = (i * tm) % seq == 0
        ext[halo:, :] = x_ref[...]
        ext[:halo, :] = jnp.where(seq_start, 0.0, h_ref[...])
        if shifted:
            _conv_fill_shifted(ext, sh[0])
        for r0, c0 in _conv_subtiles(tm, cw):
            cs = slice(c0, c0 + sc)
            acc = jnp.zeros((sr, sc), F32) + b_ref[:, cs]
            for j in range(k):
                acc = acc + w_ref[j:j + 1, cs] * _conv_rows(ext, sh, r0 + halo - p + j, sr, cs)
            o_ref[r0:r0 + sr, cs] = acc

    return pl.pallas_call(
        body, name=name,
        out_shape=jax.ShapeDtypeStruct((t, width), F32),
        grid=(t // tm, width // cw),
        in_specs=[pl.BlockSpec((tm, cw), lambda i, j: (i, cb0 + j)),
                  pl.BlockSpec((halo, cw), lambda i, j: (jnp.maximum(i * (tm // halo) - 1, 0), cb0 + j)),
                  pl.BlockSpec((kp, cw), lambda i, j: (0, j)),
                  pl.BlockSpec((1, cw), lambda i, j: (0, j))],
        out_specs=pl.BlockSpec((tm, cw), lambda i, j: (i, j)),
        scratch_shapes=[pltpu.VMEM((halo + tm, cw), F32)] + _conv_shift_scratch(k, halo + tm, cw),
        compiler_params=_params(("parallel", "parallel")),
    )(src, src, w, bias)


def _conv_bwd(dy, src, col0, width, w, k, seq, name, into=None):
    t = src.shape[0]
    tm, cw, halo = CONV_TILE, CONV_COLS, _conv_halo(k)
    sr, sc = CONV_SUB_ROWS, CONV_SUB_COLS
    p = k - 1
    cb0 = col0 // cw
    kp = w.shape[0]
    nt = t // tm
    last_halo = t // halo - 1

    shifted = _conv_use_shifted(k)

    def body(dy_ref, dn_ref, x_ref, xp_ref, w_ref, *rest):
        if into is not None:
            rest = rest[1:]
        dx_ref, dw_ref, db_ref, dyext, xext, wacc, bacc = rest[:7]
        sh = rest[7:]
        i = pl.program_id(1)
        dysh, xsh = (sh[:1], sh[1:]) if shifted else ((), ())

        @pl.when(i == 0)
        def _():
            wacc[...] = jnp.zeros_like(wacc)
            bacc[...] = jnp.zeros_like(bacc)

        seq_start = (i * tm) % seq == 0
        seq_end = ((i + 1) * tm) % seq == 0
        dyext[:tm, :] = dy_ref[...]
        dyext[tm:, :] = jnp.where(seq_end, 0.0, dn_ref[...])
        xext[halo:, :] = x_ref[...]
        xext[:halo, :] = jnp.where(seq_start, 0.0, xp_ref[...])
        if shifted:
            _conv_fill_shifted(dyext, dysh[0])
            _conv_fill_shifted(xext, xsh[0])
        for r0, c0 in _conv_subtiles(tm, cw):
            cs = slice(c0, c0 + sc)
            dyv = dy_ref[r0:r0 + sr, cs]
            acc = jnp.zeros((sr, sc), F32)
            for j in range(k):
                acc = acc + w_ref[j:j + 1, cs] * _conv_rows(dyext, dysh, r0 + p - j, sr, cs)
                wacc[j, :, cs] += _rowsum8(dyv * _conv_rows(xext, xsh, r0 + halo - p + j, sr, cs))
            dx_ref[r0:r0 + sr, cs] = acc.astype(dx_ref.dtype)
            bacc[:, cs] += _rowsum8(dyv)

        @pl.when(i == nt - 1)
        def _():
            dw_ref[...] = jnp.zeros_like(dw_ref)
            for j in range(k):
                dw_ref[j:j + 1, :] = jnp.sum(wacc[j], axis=0, keepdims=True)
            db_ref[...] = jnp.sum(bacc[...], axis=0, keepdims=True)

    if into is None:
        dx_shape = jax.ShapeDtypeStruct((t, width), F32)
        dx_spec = pl.BlockSpec((tm, cw), lambda j, i: (i, j))
        extra_specs, extra_args, aliases = [], [], {}
    else:
        dx_shape = jax.ShapeDtypeStruct(into.shape, into.dtype)
        dx_spec = pl.BlockSpec((tm, cw), lambda j, i: (i, cb0 + j))
        extra_specs, extra_args, aliases = [ANY], [into], {5: 0}
    return pl.pallas_call(
        body, name=name,
        out_shape=(dx_shape, jax.ShapeDtypeStruct((kp, width), F32), jax.ShapeDtypeStruct((1, width), F32)),
        grid=(width // cw, nt),
        in_specs=[pl.BlockSpec((tm, cw), lambda j, i: (i, j)),
                  pl.BlockSpec((halo, cw), lambda j, i: (jnp.minimum((i + 1) * (tm // halo), last_halo), j)),
                  pl.BlockSpec((tm, cw), lambda j, i: (i, cb0 + j)),
                  pl.BlockSpec((halo, cw), lambda j, i: (jnp.maximum(i * (tm // halo) - 1, 0), cb0 + j)),
                  pl.BlockSpec((kp, cw), lambda j, i: (0, j))] + extra_specs,
        out_specs=(dx_spec,
                   pl.BlockSpec((kp, cw), lambda j, i: (0, j)),
                   pl.BlockSpec((1, cw), lambda j, i: (0, j))),
        input_output_aliases=aliases,
        scratch_shapes=[pltpu.VMEM((tm + halo, cw), F32), pltpu.VMEM((halo + tm, cw), F32),
                        pltpu.VMEM((kp, SUBLANES, cw), F32), pltpu.VMEM((SUBLANES, cw), F32)]
        + 2 * _conv_shift_scratch(k, halo + tm, cw),
        compiler_params=_params(("parallel", "arbitrary")),
    )(dy, dy, src, src, w, *extra_args)


def _conf_specs(tm, cw, halo, order):
    cb = OFF_CONF // cw

    def blk(col):
        return pl.BlockSpec((tm, cw), lambda *g: (order(*g), col))

    def prev(col):
        return pl.BlockSpec((halo, cw), lambda *g: (jnp.maximum(order(*g) * (tm // halo) - 1, 0), col))

    return blk(cb), prev(cb), blk(cb + 1), prev(cb + 1)


def _glu_window(ext, a_ref, ah_ref, g_ref, gh_ref, seq_start, halo):
    ext[halo:, :] = a_ref[...] * _sigmoid(g_ref[...])
    ext[:halo, :] = jnp.where(seq_start, 0.0, ah_ref[...] * _sigmoid(gh_ref[...]))


def _conf_fwd(proj, w, bias, ln_w, ln_b, ycat, seq, name):
    t = proj.shape[0]
    k = CONF_KERNEL
    tm, cw, halo = CONV_TILE, CONF_WIDTH, _conv_halo(k)
    sr, sc = CONV_SUB_ROWS, CONV_SUB_COLS
    p = k - 1
    kp = w.shape[0]

    def body(a_ref, ah_ref, g_ref, gh_ref, z_ref, w_ref, b_ref, lw_ref, lb_ref, _, c1_ref, y_ref, ext, sh):
        i = pl.program_id(0)
        _glu_window(ext, a_ref, ah_ref, g_ref, gh_ref, (i * tm) % seq == 0, halo)
        _conv_fill_shifted(ext, sh)
        for r0, c0 in _conv_subtiles(tm, cw):
            cs = slice(c0, c0 + sc)
            acc = jnp.zeros((sr, sc), F32) + b_ref[:, cs]
            for j in range(k):
                acc = acc + w_ref[j:j + 1, cs] * _conv_rows(ext, (sh,), r0 + halo - p + j, sr, cs)
            c1_ref[r0:r0 + sr, cs] = acc
        for r0 in range(0, tm, sr):
            rows = slice(r0, r0 + sr)
            cv = c1_ref[rows, :]
            xc = cv - jnp.mean(cv, axis=-1, keepdims=True)
            rstd = lax.rsqrt(jnp.mean(xc * xc, axis=-1, keepdims=True) + EPS)
            c2 = xc * rstd * lw_ref[...] + lb_ref[...]
            y_ref[rows, :] = (_silu(c2) * _silu(z_ref[rows, :])).astype(y_ref.dtype)

    vec = pl.BlockSpec((1, cw), lambda i: (0, 0))
    row = pl.BlockSpec((tm, cw), lambda i: (i, 0))
    return pl.pallas_call(
        body, name=name,
        out_shape=(jax.ShapeDtypeStruct((t, cw), F32), jax.ShapeDtypeStruct(ycat.shape, ycat.dtype)),
        grid=(t // tm,),
        in_specs=[*_conf_specs(tm, cw, halo, lambda i: i),
                  pl.BlockSpec((tm, cw), lambda i: (i, OFF_ZC // cw)),
                  pl.BlockSpec((kp, cw), lambda i: (0, 0)), vec, vec, vec, ANY],
        out_specs=(row, pl.BlockSpec((tm, cw), lambda i: (i, YCAT_CONF // cw))),
        input_output_aliases={9: 1},
        scratch_shapes=[pltpu.VMEM((halo + tm, cw), F32)] + _conv_shift_scratch(k, halo + tm, cw),
        compiler_params=_params(("parallel",)),
    )(proj, proj, proj, proj, proj, w, bias, ln_w, ln_b, ycat)


def _conf_bwd(dycat, proj, c1, w, ln_w, ln_b, dproj, seq, name):
    t = proj.shape[0]
    k = CONF_KERNEL
    tm, cw, halo = CONV_TILE, CONF_WIDTH, _conv_halo(k)
    sr, sc = CONV_SUB_ROWS, CONV_SUB_COLS
    p = k - 1
    kp = w.shape[0]
    nt = t // tm
    last_halo = t // halo - 1

    def body(dy_ref, dyn_ref, c_ref, cn_ref, z_ref, zn_ref, a_ref, ah_ref, g_ref, gh_ref, w_ref, lw_ref, lb_ref, _,
             grp_ref, dw_ref, db_ref, dlw_ref, dlb_ref, dyext, xext, wacc, bacc, lwacc, lbacc, dysh, xsh):
        i = pl.program_id(0)

        @pl.when(i == 0)
        def _():
            wacc[...] = jnp.zeros_like(wacc)
            bacc[...] = jnp.zeros_like(bacc)
            lwacc[...] = jnp.zeros_like(lwacc)
            lbacc[...] = jnp.zeros_like(lbacc)

        def post_bwd(dy, cv, zv):
            xc = cv - jnp.mean(cv, axis=-1, keepdims=True)
            rstd = lax.rsqrt(jnp.mean(xc * xc, axis=-1, keepdims=True) + EPS)
            xh = xc * rstd
            c2 = xh * lw_ref[...] + lb_ref[...]
            dz = dy * _silu(c2) * _dsilu(zv)
            dc2 = dy * _silu(zv) * _dsilu(c2)
            dxh = dc2 * lw_ref[...]
            dc = rstd * (dxh - jnp.mean(dxh, axis=-1, keepdims=True)
                         - xh * jnp.mean(dxh * xh, axis=-1, keepdims=True))
            return dc, dz, dc2 * xh, dc2

        seq_end = ((i + 1) * tm) % seq == 0
        for r0 in range(0, tm, sr):
            rows = slice(r0, r0 + sr)
            dc, dz, lw_terms, lb_terms = post_bwd(dy_ref[rows, :], c_ref[rows, :], z_ref[rows, :])
            dyext[rows, :] = dc
            grp_ref[rows, 2 * cw:] = dz.astype(grp_ref.dtype)
            lwacc[...] += _rowsum8(lw_terms)
            lbacc[...] += _rowsum8(lb_terms)
        dc_next = post_bwd(dyn_ref[...], cn_ref[...], zn_ref[...])[0]
        dyext[tm:, :] = jnp.where(seq_end, 0.0, dc_next)
        _glu_window(xext, a_ref, ah_ref, g_ref, gh_ref, (i * tm) % seq == 0, halo)
        _conv_fill_shifted(dyext, dysh)
        _conv_fill_shifted(xext, xsh)
        dag_ref = grp_ref
        for r0, c0 in _conv_subtiles(tm, cw):
            cs = slice(c0, c0 + sc)
            rows = slice(r0, r0 + sr)
            dyv = dyext[rows, cs]
            acc = jnp.zeros((sr, sc), F32)
            for j in range(k):
                acc = acc + w_ref[j:j + 1, cs] * _conv_rows(dyext, (dysh,), r0 + p - j, sr, cs)
                wacc[j, :, cs] += _rowsum8(dyv * _conv_rows(xext, (xsh,), r0 + halo - p + j, sr, cs))
            bacc[:, cs] += _rowsum8(dyv)
            s = _sigmoid(g_ref[rows, cs])
            dag_ref[rows, cs] = (acc * s).astype(dag_ref.dtype)
            dag_ref[rows, cw + c0:cw + c0 + sc] = (acc * a_ref[rows, cs] * s * (1.0 - s)).astype(dag_ref.dtype)

        @pl.when(i == nt - 1)
        def _():
            dw_ref[...] = jnp.zeros_like(dw_ref)
            for j in range(k):
                dw_ref[j:j + 1, :] = jnp.sum(wacc[j], axis=0, keepdims=True)
            db_ref[...] = jnp.sum(bacc[...], axis=0, keepdims=True)
            dlw_ref[...] = jnp.sum(lwacc[...], axis=0, keepdims=True)
            dlb_ref[...] = jnp.sum(lbacc[...], axis=0, keepdims=True)

    def blk(col):
        return pl.BlockSpec((tm, cw), lambda i: (i, col))

    def nxt(col):
        return pl.BlockSpec((halo, cw), lambda i: (jnp.minimum((i + 1) * (tm // halo), last_halo), col))

    vec = pl.BlockSpec((1, cw), lambda i: (0, 0))
    return pl.pallas_call(
        body, name=name,
        out_shape=(jax.ShapeDtypeStruct(dproj.shape, dproj.dtype), jax.ShapeDtypeStruct((kp, cw), F32),
                   jax.ShapeDtypeStruct((1, cw), F32), jax.ShapeDtypeStruct((1, cw), F32),
                   jax.ShapeDtypeStruct((1, cw), F32)),
        grid=(nt,),
        in_specs=[blk(YCAT_CONF // cw), nxt(YCAT_CONF // cw), blk(0), nxt(0), blk(OFF_ZC // cw), nxt(OFF_ZC // cw),
                  *_conf_specs(tm, cw, halo, lambda i: i),
                  pl.BlockSpec((kp, cw), lambda i: (0, 0)), vec, vec, ANY],
        out_specs=(pl.BlockSpec((tm, CONF_GROUP), lambda i: (i, OFF_CONF // CONF_GROUP)),
                   pl.BlockSpec((kp, cw), lambda i: (0, 0)), vec, vec, vec),
        input_output_aliases={13: 0},
        scratch_shapes=[pltpu.VMEM((tm + halo, cw), F32), pltpu.VMEM((halo + tm, cw), F32),
                        pltpu.VMEM((kp, SUBLANES, cw), F32), pltpu.VMEM((SUBLANES, cw), F32),
                        pltpu.VMEM((SUBLANES, cw), F32), pltpu.VMEM((SUBLANES, cw), F32)]
        + 2 * _conv_shift_scratch(k, halo + tm, cw),
        compiler_params=_params(("arbitrary",)),
    )(dycat, dycat, c1, c1, proj, proj, proj, proj, proj, proj, w, ln_w, ln_b, dproj)


def _half_mask(half):
    lane = _iota((1, LANES), 1)
    return ((lane >= half * ATTN_HEAD_DIM) & (lane < (half + 1) * ATTN_HEAD_DIM)).astype(F32)


def _stack_heads(xp, g):
    m = _half_mask(g)
    swapped = pltpu.roll(xp, ATTN_HEAD_DIM, axis=1)
    return jnp.concatenate([xp * m, swapped * m] if g == 0 else [swapped * m, xp * m], axis=0)


def _unstack_heads(both, g):
    w = both.shape[0] // 2
    top, bot = both[:w], both[w:]
    lo, hi = _half_mask(0), _half_mask(1)
    if g == 0:
        return top * lo + pltpu.roll(bot, ATTN_HEAD_DIM, axis=1) * hi
    return pltpu.roll(top, ATTN_HEAD_DIM, axis=1) * lo + bot * hi


def _band_mask(first_block):
    w = WINDOW
    qi = _iota((w, 2 * w), 0)
    kj = _iota((w, 2 * w), 1) - w
    rel = qi - kj
    return (rel >= 0) & (rel < w) & (jnp.logical_not(first_block) | (kj >= 0))


def _lane_pick(x, h):
    return jnp.sum(jnp.where(_iota(x.shape, 1) == h, x, 0.0), axis=1, keepdims=True)


def _attn_specs(nb, rev):
    w = WINDOW

    def blk(i):
        return nb - 1 - i if rev else i

    def row(b, i):
        return b * nb + blk(i)

    def prow(b, i):
        return b * nb + jnp.maximum(blk(i) - 1, 0)

    q = pl.BlockSpec((w, 512), lambda b, i: (row(b, i), OFF_Q // 512))
    kc = pl.BlockSpec((w, 128), lambda b, i: (row(b, i), OFF_K // 128))
    kp = pl.BlockSpec((w, 128), lambda b, i: (prow(b, i), OFF_K // 128))
    vc = pl.BlockSpec((w, 128), lambda b, i: (row(b, i), OFF_V // 128))
    vp = pl.BlockSpec((w, 128), lambda b, i: (prow(b, i), OFF_V // 128))
    z = pl.BlockSpec((w, 512), lambda b, i: (row(b, i), OFF_ZA // 512))
    return q, kc, kp, vc, vp, z, row


def _attn_fwd(proj, sinks, ycat, nbatch, name):
    t = proj.shape[0]
    w = WINDOW
    nb = t // nbatch // w
    scale = ATTN_HEAD_DIM ** -0.5
    q_s, kc_s, kp_s, vc_s, vp_s, z_s, row = _attn_specs(nb, False)

    def body(q_ref, kc_ref, kp_ref, vc_ref, vp_ref, z_ref, sk_ref, _, y_ref, o_ref, lse_ref):
        first = pl.program_id(1) == 0
        mask = _band_mask(first)
        kk = jnp.concatenate([kp_ref[...], kc_ref[...]], axis=0).astype(MXU_DTYPE)
        vv = jnp.concatenate([vp_ref[...], vc_ref[...]], axis=0).astype(MXU_DTYPE)
        sk = sk_ref[...]
        lane = _iota((w, LANES), 1)
        mask2 = jnp.concatenate([mask, mask], axis=0)
        scores = [_dot(_stack_heads(q_ref[:, j * LANES:(j + 1) * LANES], j // 2), kk, NT) for j in range(4)]
        lse_all = jnp.zeros((w, LANES), F32)
        for j in range(4):
            s = jnp.where(mask2, scores[j] * scale, -1e30)
            skc = jnp.concatenate([jnp.broadcast_to(_lane_pick(sk, 2 * j), (w, 1)),
                                   jnp.broadcast_to(_lane_pick(sk, 2 * j + 1), (w, 1))], axis=0)
            m = jnp.maximum(jnp.max(s, axis=1, keepdims=True), skc)
            den = jnp.sum(jnp.exp(s - m), axis=1, keepdims=True) + jnp.exp(skc - m)
            lse = m + jnp.log(den)
            lse_all = jnp.where(lane == 2 * j, lse[:w], lse_all)
            lse_all = jnp.where(lane == 2 * j + 1, lse[w:], lse_all)
            op = _unstack_heads(_dot(jnp.exp(s - lse), vv), j // 2)
            cols = slice(j * LANES, (j + 1) * LANES)
            o_ref[:, cols] = op
            y_ref[:, cols] = (op * _silu(z_ref[:, cols])).astype(y_ref.dtype)
        lse_ref[...] = lse_all

    return pl.pallas_call(
        body, name=name,
        out_shape=(jax.ShapeDtypeStruct(ycat.shape, ycat.dtype), jax.ShapeDtypeStruct((t, 512), F32),
                   jax.ShapeDtypeStruct((t, LANES), F32)),
        grid=(nbatch, nb),
        in_specs=[q_s, kc_s, kp_s, vc_s, vp_s, z_s, pl.BlockSpec((1, LANES), lambda b, i: (0, 0)), ANY],
        out_specs=(pl.BlockSpec((w, 512), lambda b, i: (row(b, i), YCAT_ATTN // 512)),
                   pl.BlockSpec((w, 512), lambda b, i: (row(b, i), 0)),
                   pl.BlockSpec((w, LANES), lambda b, i: (row(b, i), 0))),
        input_output_aliases={7: 0},
        compiler_params=_params(("parallel", "parallel")),
    )(proj, proj, proj, proj, proj, proj, sinks, ycat)


def _attn_bwd(dycat, proj, o, lse, sinks, ddt, dproj, nbatch, name):
    t = proj.shape[0]
    w = WINDOW
    nb = t // nbatch // w
    scale = ATTN_HEAD_DIM ** -0.5
    q_s, kc_s, kp_s, vc_s, vp_s, z_s, row = _attn_specs(nb, True)

    def body(dy_ref, q_ref, kc_ref, kp_ref, vc_ref, vp_ref, z_ref, o_ref, lse_ref, sk_ref, ddt_ref, _,
             grp_ref, dsk_ref, kcarry, vcarry, sacc):
        b, i = pl.program_id(0), pl.program_id(1)

        @pl.when((b == 0) & (i == 0))
        def _():
            sacc[...] = jnp.zeros_like(sacc)

        @pl.when(i == 0)
        def _():
            kcarry[...] = jnp.zeros_like(kcarry)
            vcarry[...] = jnp.zeros_like(vcarry)

        first = i == nb - 1
        mask = _band_mask(first)
        kk = jnp.concatenate([kp_ref[...], kc_ref[...]], axis=0).astype(MXU_DTYPE)
        vv = jnp.concatenate([vp_ref[...], vc_ref[...]], axis=0).astype(MXU_DTYPE)
        sk = sk_ref[...]
        lse_all = lse_ref[...]
        lane1 = _iota((1, LANES), 1)
        mask2 = jnp.concatenate([mask, mask], axis=0)
        qs, dos, deltas, lses, scores, dps = [], [], [], [], [], []
        for j in range(4):
            cols = slice(j * LANES, (j + 1) * LANES)
            qp, zp, ov, dy = q_ref[:, cols], z_ref[:, cols], o_ref[:, cols], dy_ref[:, cols]
            grp_ref[:, OFF_ZA + j * LANES:OFF_ZA + (j + 1) * LANES] = (dy * ov * _dsilu(zp)).astype(grp_ref.dtype)
            do = dy * _silu(zp)
            q2 = _stack_heads(qp, j // 2).astype(MXU_DTYPE)
            do2 = _stack_heads(do, j // 2)
            qs.append(q2)
            dos.append(do2.astype(MXU_DTYPE))
            deltas.append(jnp.sum(do2 * _stack_heads(ov, j // 2), axis=1, keepdims=True))
            lses.append(jnp.concatenate([_lane_pick(lse_all, 2 * j), _lane_pick(lse_all, 2 * j + 1)], axis=0))
            scores.append(_dot(q2, kk, NT))
            dps.append(_dot(do2, vv, NT))
        prs, dss = [], []
        dsk = jnp.zeros((1, LANES), F32)
        for j in range(4):
            pr = jnp.exp(jnp.where(mask2, scores[j] * scale, -1e30) - lses[j])
            prs.append(pr.astype(MXU_DTYPE))
            dss.append((pr * (dps[j] - deltas[j])).astype(MXU_DTYPE))
            skc = jnp.concatenate([jnp.broadcast_to(_lane_pick(sk, 2 * j), (w, 1)),
                                   jnp.broadcast_to(_lane_pick(sk, 2 * j + 1), (w, 1))], axis=0)
            sink_term = jnp.exp(skc - lses[j]) * deltas[j]
            dsk = dsk - jnp.where(lane1 == 2 * j, jnp.sum(sink_term[:w]), 0.0)
            dsk = dsk - jnp.where(lane1 == 2 * j + 1, jnp.sum(sink_term[w:]), 0.0)
        dkk = jnp.zeros((2 * w, LANES), F32)
        dvv = jnp.zeros((2 * w, LANES), F32)
        for j in range(4):
            dq = _unstack_heads(_dot(dss[j], kk) * scale, j // 2)
            grp_ref[:, OFF_Q + j * LANES:OFF_Q + (j + 1) * LANES] = dq.astype(grp_ref.dtype)
            dkk = dkk + _dot(dss[j], qs[j], TN) * scale
            dvv = dvv + _dot(prs[j], dos[j], TN)
        grp_ref[:, OFF_K:OFF_K + LANES] = (dkk[w:, :] + kcarry[...]).astype(grp_ref.dtype)
        grp_ref[:, OFF_V:OFF_V + LANES] = (dvv[w:, :] + vcarry[...]).astype(grp_ref.dtype)
        grp_ref[:, OFF_DT:OFF_DT + LANES] = ddt_ref[...].astype(grp_ref.dtype)
        grp_ref[:, OFF_DT + LANES:] = jnp.zeros((w, ATTN_GROUP - OFF_DT - LANES), grp_ref.dtype)
        kcarry[...] = dkk[:w, :]
        vcarry[...] = dvv[:w, :]
        sacc[...] += dsk

        @pl.when((b == nbatch - 1) & (i == nb - 1))
        def _():
            dsk_ref[...] = sacc[...]

    return pl.pallas_call(
        body, name=name,
        out_shape=(jax.ShapeDtypeStruct(dproj.shape, dproj.dtype), jax.ShapeDtypeStruct((1, LANES), F32)),
        grid=(nbatch, nb),
        in_specs=[pl.BlockSpec((w, 512), lambda b, i: (row(b, i), YCAT_ATTN // 512)),
                  q_s, kc_s, kp_s, vc_s, vp_s, z_s,
                  pl.BlockSpec((w, 512), lambda b, i: (row(b, i), 0)),
                  pl.BlockSpec((w, LANES), lambda b, i: (row(b, i), 0)),
                  pl.BlockSpec((1, LANES), lambda b, i: (0, 0)),
                  pl.BlockSpec((w, LANES), lambda b, i: (row(b, i), 0)), ANY],
        out_specs=(pl.BlockSpec((w, ATTN_GROUP), lambda b, i: (row(b, i), 0)),
                   pl.BlockSpec((1, LANES), lambda b, i: (0, 0))),
        input_output_aliases={11: 0},
        scratch_shapes=[pltpu.VMEM((w, LANES), F32), pltpu.VMEM((w, LANES), F32),
                        pltpu.VMEM((1, LANES), F32)],
        compiler_params=_params(("arbitrary", "arbitrary")),
    )(dycat, proj, proj, proj, proj, proj, proj, o, lse, sinks, ddt, dproj)


SSD_WIDTH = SSD_HEADS * SSD_HEAD_DIM
GROUP_ROWS = SSD_WIDTH // 2


def _expand_mat():
    r, c = _iota((LANES, SSD_WIDTH), 0), _iota((LANES, SSD_WIDTH), 1)
    return (r == lax.shift_right_logical(c, 6)).astype(BF16)


def _expand_mat_t():
    r, c = _iota((SSD_WIDTH, LANES), 0), _iota((SSD_WIDTH, LANES), 1)
    return (c == lax.shift_right_logical(r, 6)).astype(BF16)


def _ssd_common(u_ref, dt_ref, dtb_ref, a_ref, stack_broadcasts=False):
    q = CHUNK
    act = _silu(u_ref[...])
    xs = act[:, :SSD_WIDTH]
    bm = act[:, SSD_WIDTH:SSD_WIDTH + 256]
    cm = act[:, SSD_WIDTH + 256:]
    dtp = _softplus(dt_ref[...] + dtb_ref[...])
    a = dtp * a_ref[...]
    tril = (_iota((q, q), 0) >= _iota((q, q), 1)).astype(BF16)
    acs = _xdot_r(tril, a)
    acs_t = acs.T
    e = _expand_mat()
    a_end = jnp.sum(jnp.where(_iota(acs.shape, 0) == q - 1, acs, 0.0), axis=0, keepdims=True)
    if stack_broadcasts:
        spread = _xdot(jnp.concatenate([dtp, acs, a_end - acs], axis=0), e)
        dt_x, ea, dec = spread[:q], jnp.exp(spread[q:2 * q]), jnp.exp(spread[2 * q:])
    else:
        dt_x = _xdot(dtp, e)
        ea = jnp.exp(_xdot(acs, e))
        dec = jnp.exp(_xdot(a_end - acs, e))
    a_end_col = jnp.broadcast_to(_lane_pick(acs_t, q - 1), (LANES, LANES))
    s_scale = jnp.exp(_xdot_r(_expand_mat_t(), a_end_col))
    return act, xs, bm, cm, dtp, acs, acs_t, dt_x, ea, dec, s_scale, tril


def _decay_mat(acs, acs_t, h):
    q = CHUNK
    col = _lane_pick(acs, h)
    rowv = jnp.sum(jnp.where(_iota(acs_t.shape, 0) == h, acs_t, 0.0), axis=0, keepdims=True)
    causal = _iota((q, q), 0) >= _iota((q, q), 1)
    return jnp.exp(jnp.where(causal, col - rowv, -1e30))


GN_WIDTH = 512


def _ssd_fwd(u, proj, dtb, a_neg, d_x, norm_w, ycat, nbatch, name):
    t = u.shape[0]
    q = CHUNK
    nc = t // nbatch // q

    def body(u_ref, dt_ref, z_ref, dtb_ref, a_ref, dx_ref, nw_ref, _, y_ref, st_ref, yn_ref, state):
        c = pl.program_id(1)

        @pl.when(c == 0)
        def _():
            state[...] = jnp.zeros_like(state)

        st_ref[...] = state[...]
        act, xs, bm, cm, dtp, acs, acs_t, dt_x, ea, dec, s_scale, _ = _ssd_common(u_ref, dt_ref, dtb_ref, a_ref)
        xdt = xs * dt_x
        xdec = xdt * dec
        lo, hi = _half_mask(0), _half_mask(1)
        grp = []
        for g in range(2):
            bg = bm[:, g * LANES:(g + 1) * LANES]
            cg = cm[:, g * LANES:(g + 1) * LANES]
            rows = slice(g * GROUP_ROWS, (g + 1) * GROUP_ROWS)
            sg = state[rows, :]
            grp.append((_dot(cg, bg, NT), _dot(cg, sg, NT), rows,
                        s_scale[rows, :] * sg + _dot(xdec[:, rows], bg, TN)))
        yps = []
        for pj in range(SSD_HEADS // 2):
            cb = grp[pj // 4][0]
            xp = xdt[:, pj * LANES:(pj + 1) * LANES]
            m2 = jnp.concatenate([cb * _decay_mat(acs, acs_t, 2 * pj), cb * _decay_mat(acs, acs_t, 2 * pj + 1)],
                                 axis=1)
            yps.append(_dot(m2, jnp.concatenate([xp * lo, xp * hi], axis=0)))
        for g in range(2):
            _, yoff, rows, state_new = grp[g]
            for j in range(4):
                pj = g * 4 + j
                cols = slice(pj * LANES, (pj + 1) * LANES)
                yp = yps[pj] + yoff[:, j * LANES:(j + 1) * LANES] * ea[:, cols]
                y_ref[:, cols] = yp + dx_ref[:, cols] * xs[:, cols]
            state[rows, :] = state_new
        for g in range(SSD_WIDTH // GN_WIDTH):
            cols = slice(g * GN_WIDTH, (g + 1) * GN_WIDTH)
            gg = y_ref[:, cols] * _silu(z_ref[:, cols])
            rstd = lax.rsqrt(jnp.mean(gg * gg, axis=-1, keepdims=True) + EPS)
            yn_ref[:, cols] = (gg * rstd * nw_ref[:, cols]).astype(yn_ref.dtype)

    vec = pl.BlockSpec((1, LANES), lambda b, c: (0, 0))
    wide = pl.BlockSpec((q, SSD_WIDTH), lambda b, c: (b * nc + c, 0))
    wvec = pl.BlockSpec((1, SSD_WIDTH), lambda b, c: (0, 0))
    return pl.pallas_call(
        body, name=name,
        out_shape=(jax.ShapeDtypeStruct((t, SSD_WIDTH), F32),
                   jax.ShapeDtypeStruct((nbatch * nc * SSD_WIDTH, SSD_STATE), F32),
                   jax.ShapeDtypeStruct(ycat.shape, ycat.dtype)),
        grid=(nbatch, nc),
        in_specs=[pl.BlockSpec((q, SSD_CONV_DIM), lambda b, c: (b * nc + c, 0)),
                  pl.BlockSpec((q, LANES), lambda b, c: (b * nc + c, OFF_DT // LANES)),
                  pl.BlockSpec((q, SSD_WIDTH), lambda b, c: (b * nc + c, OFF_ZS // SSD_WIDTH)),
                  vec, vec, wvec, wvec, ANY],
        out_specs=(wide, pl.BlockSpec((SSD_WIDTH, SSD_STATE), lambda b, c: (b * nc + c, 0)), wide),
        input_output_aliases={7: 2},
        scratch_shapes=[pltpu.VMEM((SSD_WIDTH, SSD_STATE), F32)],
        compiler_params=_params(("parallel", "arbitrary")),
    )(u, proj, proj, dtb, a_neg, d_x, norm_w, ycat)


def _ssd_bwd(dycat, u, proj, y, states, dtb, a_neg, d_x, norm_w, dproj, nbatch, name):
    t = u.shape[0]
    q = CHUNK
    nc = t // nbatch // q

    def body(do_ref, u_ref, dt_ref, z_ref, y_ref, st_ref, dtb_ref, a_ref, dx_ref, nw_ref, _,
             du_ref, dz_ref, ddt_ref, dal_ref, dd_ref, dtbg_ref, dnw_ref, dstate, acc_a, acc_d, acc_b, acc_w):
        b, c = pl.program_id(0), pl.program_id(1)

        @pl.when((b == 0) & (c == 0))
        def _():
            acc_a[...] = jnp.zeros_like(acc_a)
            acc_d[...] = jnp.zeros_like(acc_d)
            acc_b[...] = jnp.zeros_like(acc_b)
            acc_w[...] = jnp.zeros_like(acc_w)

        @pl.when(c == 0)
        def _():
            dstate[...] = jnp.zeros_like(dstate)

        dy_parts = []
        for g in range(SSD_WIDTH // GN_WIDTH):
            cols = slice(g * GN_WIDTH, (g + 1) * GN_WIDTH)
            yv, zv, dov = y_ref[:, cols], z_ref[:, cols], do_ref[:, cols]
            sz = _silu(zv)
            gg = yv * sz
            rstd = lax.rsqrt(jnp.mean(gg * gg, axis=-1, keepdims=True) + EPS)
            gh = gg * rstd
            acc_w[:, cols] += _rowsum8(dov * gh)
            dgn = dov * nw_ref[:, cols]
            dg = rstd * (dgn - gh * jnp.mean(dgn * gh, axis=-1, keepdims=True))
            dy_parts.append(dg * sz)
            dz_ref[:, cols] = (dg * yv * _dsilu(zv)).astype(dz_ref.dtype)

        act, xs, bm, cm, dtp, acs, acs_t, dt_x, ea, dec, s_scale, tril = _ssd_common(
            u_ref, dt_ref, dtb_ref, a_ref, stack_broadcasts=True)
        xdt = xs * dt_x
        xdec = xdt * dec
        dyv = jnp.concatenate(dy_parts, axis=1)
        dye = dyv * ea
        lo, hi = _half_mask(0), _half_mask(1)
        et = _expand_mat_t()
        grp = []
        for g in range(2):
            rows = slice(g * GROUP_ROWS, (g + 1) * GROUP_ROWS)
            bg = bm[:, g * LANES:(g + 1) * LANES]
            cg = cm[:, g * LANES:(g + 1) * LANES]
            sg = st_ref[rows, :]
            dsg = dstate[rows, :]
            grp.append(dict(
                rows=rows, bg=bg, cg=cg, dsg=dsg,
                cb=_dot(cg, bg, NT), yoff=_dot(cg, sg, NT), dxst=_dot(bg, dsg, NT) * dec[:, rows],
                dc_off=_dot(dye[:, rows], sg), db_off=_dot(xdec[:, rows], dsg),
                s_carried=s_scale[rows, :] * sg,
                dstate_new=_dot(dye[:, rows], cg, TN) + s_scale[rows, :] * dsg))
        dy2s, g2s, l2s = [], [], []
        for pj in range(SSD_HEADS // 2):
            cols = slice(pj * LANES, (pj + 1) * LANES)
            dyp = dyv[:, cols]
            dy2 = jnp.concatenate([dyp * lo, dyp * hi], axis=0).astype(MXU_DTYPE)
            dy2s.append(dy2)
            g2s.append(_dot(dy2, xdt[:, cols], NT))
            l2s.append(jnp.concatenate([_decay_mat(acs, acs_t, 2 * pj), _decay_mat(acs, acs_t, 2 * pj + 1)], axis=0))
        dal_diag = jnp.zeros((q, LANES), F32)
        lane2 = _iota((2 * q, LANES), 1)
        row2 = _iota((2 * q, LANES), 0)
        dxdt_parts, db_parts, dc_parts = [], [], []
        end_sum = jnp.zeros((LANES, LANES), F32)
        for g in range(2):
            gd = grp[g]
            cb2 = jnp.concatenate([gd["cb"], gd["cb"]], axis=0)
            dcb = jnp.zeros((q, q), F32)
            parts = []
            for j in range(4):
                pj = g * 4 + j
                gl = g2s[pj] * l2s[pj]
                dcb = dcb + gl[:q] + gl[q:]
                m2 = cb2 * l2s[pj]
                parts.append(_dot(m2, dy2s[pj], TN))
                w2 = (gl * cb2).astype(MXU_DTYPE)
                sel2 = (lane2 == 2 * pj + (row2 >= q).astype(jnp.int32)).astype(MXU_DTYPE)
                dal_diag = dal_diag + _dot(jnp.concatenate([w2[:q], w2[q:]], axis=1), sel2) - _dot(w2, sel2, TN)
            dxdt_parts.append(jnp.concatenate(parts, axis=1) + gd["dxst"])
            dc_parts.append(_dot(dcb, gd["bg"]) + gd["dc_off"])
            db_parts.append(_dot(dcb, gd["cg"], TN) + gd["db_off"])
            end_sum = end_sum + _xdot(gd["dsg"] * gd["s_carried"], et[gd["rows"], :], TN, passes=2)
            dstate[gd["rows"], :] = gd["dstate_new"]
        dxst_parts = [gd["dxst"] for gd in grp]
        yoff_parts = [gd["yoff"] for gd in grp]
        dxdt = jnp.concatenate(dxdt_parts, axis=1)
        dxv = dx_ref[...]
        yoff = jnp.concatenate(yoff_parts, axis=1) * ea
        per_head = _xdot(jnp.concatenate([dyv * yoff, xdt * jnp.concatenate(dxst_parts, axis=1),
                                          dxdt * xs, dyv * xs], axis=0), et)
        off_term, st_term, dx_term, d_term = (per_head[k * q:(k + 1) * q] for k in range(4))
        dalpha = dal_diag + off_term - st_term
        end_row = jnp.sum(end_sum, axis=0, keepdims=True) + jnp.sum(st_term, axis=0, keepdims=True)
        dalpha = dalpha + jnp.where(_iota((q, LANES), 0) == q - 1, end_row, 0.0)
        da = _xdot_r(tril, dalpha, TN)
        ddtp = da * a_ref[...] + dx_term
        acc_a[...] += _rowsum8(da * dtp)
        acc_d[...] += _rowsum8(d_term)
        ddt_raw = ddtp * _sigmoid(dt_ref[...] + dtb_ref[...])
        acc_b[...] += _rowsum8(ddt_raw)
        ddt_ref[...] = ddt_raw
        dxs = dxdt * dt_x + dxv * dyv
        dact = jnp.concatenate([dxs] + db_parts + dc_parts, axis=1)
        du_ref[...] = dact * _dsilu(u_ref[...])

        @pl.when((b == nbatch - 1) & (c == nc - 1))
        def _():
            dal_ref[...] = jnp.sum(acc_a[...], axis=0, keepdims=True) * a_ref[...]
            dd_ref[...] = jnp.sum(acc_d[...], axis=0, keepdims=True)
            dtbg_ref[...] = jnp.sum(acc_b[...], axis=0, keepdims=True)
            dnw_ref[...] = jnp.sum(acc_w[...], axis=0, keepdims=True)

    def rowblk(b, c):
        return b * nc + (nc - 1 - c)

    vec = pl.BlockSpec((1, LANES), lambda b, c: (0, 0))
    wvec = pl.BlockSpec((1, SSD_WIDTH), lambda b, c: (0, 0))
    wide = pl.BlockSpec((q, SSD_WIDTH), lambda b, c: (rowblk(b, c), 0))
    zblk = pl.BlockSpec((q, SSD_WIDTH), lambda b, c: (rowblk(b, c), OFF_ZS // SSD_WIDTH))
    return pl.pallas_call(
        body, name=name,
        out_shape=(jax.ShapeDtypeStruct((t, SSD_CONV_DIM), F32), jax.ShapeDtypeStruct(dproj.shape, dproj.dtype),
                   jax.ShapeDtypeStruct((t, LANES), F32),
                   jax.ShapeDtypeStruct((1, LANES), F32), jax.ShapeDtypeStruct((1, LANES), F32),
                   jax.ShapeDtypeStruct((1, LANES), F32), jax.ShapeDtypeStruct((1, SSD_WIDTH), F32)),
        grid=(nbatch, nc),
        in_specs=[wide,
                  pl.BlockSpec((q, SSD_CONV_DIM), lambda b, c: (rowblk(b, c), 0)),
                  pl.BlockSpec((q, LANES), lambda b, c: (rowblk(b, c), OFF_DT // LANES)),
                  zblk, wide,
                  pl.BlockSpec((SSD_WIDTH, SSD_STATE), lambda b, c: (rowblk(b, c), 0)),
                  vec, vec, wvec, wvec, ANY],
        out_specs=(pl.BlockSpec((q, SSD_CONV_DIM), lambda b, c: (rowblk(b, c), 0)),
                   zblk,
                   pl.BlockSpec((q, LANES), lambda b, c: (rowblk(b, c), 0)),
                   vec, vec, vec, wvec),
        input_output_aliases={10: 1},
        scratch_shapes=[pltpu.VMEM((SSD_WIDTH, SSD_STATE), F32), pltpu.VMEM((SUBLANES, LANES), F32),
                        pltpu.VMEM((SUBLANES, LANES), F32), pltpu.VMEM((SUBLANES, LANES), F32),
                        pltpu.VMEM((SUBLANES, SSD_WIDTH), F32)],
        compiler_params=_params(("arbitrary", "arbitrary")),
    )(dycat, u, proj, proj, y, states, dtb, a_neg, d_x, norm_w, dproj)


def _pad_rows(w, rows):
    return jnp.concatenate([w, jnp.zeros((rows - w.shape[0], w.shape[1]), w.dtype)], axis=0)


def _pad_lanes(v):
    return jnp.concatenate([v, jnp.zeros((LANES - v.shape[0],), v.dtype)]).reshape(1, LANES)


def _padded_from_chips(pieces):
    cols = pieces[0].shape[-1]
    lead = pieces[0].shape[:-1]
    parts, pos = [], 0
    for lo, hi, start in sorted(SECTIONS, key=lambda s: s[2]):
        if start > pos:
            parts.append(jnp.zeros(lead + (start - pos,), pieces[0].dtype))
        pos = start + hi - lo
        while lo < hi:
            p = lo // cols
            end = min(hi, (p + 1) * cols)
            parts.append(pieces[p][..., lo - p * cols:end - p * cols])
            lo = end
    if pos < NP:
        parts.append(jnp.zeros(lead + (NP - pos,), pieces[0].dtype))
    return jnp.concatenate(parts, axis=-1)


def _chip_part_from_padded(wp, p, cols):
    lo, hi = p * cols, (p + 1) * cols
    parts = []
    for rs, re, start in SECTIONS:
        a, b = max(lo, rs), min(hi, re)
        if a < b:
            parts.append(wp[..., start + a - rs:start + b - rs])
    return jnp.concatenate(parts, axis=-1)


def _layer_params(li, w_in_p, w_out, conv_w, dw_w, small):
    return dict(
        w_in_p=w_in_p, w_out=w_out,
        conv_w=_pad_rows(conv_w, SUBLANES), dw_w=_pad_rows(dw_w, 32),
        norm_w=small["norm_w"][li].reshape(1, -1),
        conv_b=small["ssd_conv_b"][li].reshape(1, -1),
        dtb=_pad_lanes(small["ssd_dt_bias"][li]),
        a_neg=_pad_lanes(-jnp.exp(small["ssd_a_log"][li])),
        d_x=jnp.repeat(small["ssd_d"][li], SSD_HEAD_DIM).reshape(1, -1),
        ssd_norm_w=small["ssd_norm_w"][li].reshape(1, -1),
        sinks=_pad_lanes(small["attn_sinks"][li]),
        dw_b=small["conf_dw_b"][li].reshape(1, -1),
        ln_w=small["conf_ln_w"][li].reshape(1, -1),
        ln_b=small["conf_ln_b"][li].reshape(1, -1),
    )


def _layer_fwd(x, p, nbatch, seq, tag, after=None):
    proj, h_t = _proj_fwd(x, p["norm_w"], p["w_in_p"], name=f"proj_fwd_{tag}", after=after)
    u = _conv_fwd(proj, OFF_XBC, SSD_CONV_DIM, p["conv_w"], p["conv_b"], SSD_CONV, seq, name=f"ssd_conv_fwd_{tag}")
    ycat = lax.empty((x.shape[0], MIX_WIDTH), MXU_DTYPE)
    y, states, ycat = _ssd_fwd(u, proj, p["dtb"], p["a_neg"], p["d_x"], p["ssd_norm_w"], ycat, nbatch,
                               name=f"ssd_fwd_{tag}")
    ycat, o, lse = _attn_fwd(proj, p["sinks"], ycat, nbatch, name=f"attn_fwd_{tag}")
    c1, ycat = _conf_fwd(proj, p["dw_w"], p["dw_b"], p["ln_w"], p["ln_b"], ycat, seq, name=f"conf_fwd_{tag}")
    w_out = p["w_out"](ycat) if callable(p["w_out"]) else p["w_out"]
    x_new = _matmul(ycat, w_out, "nn", F32, 1024, 512, 2048, name=f"out_fwd_{tag}", residual=x)
    return x_new, dict(x=x, w_out=w_out, h_t=h_t, proj=proj, u=u, y=y, states=states, o=o, lse=lse, c1=c1, ycat=ycat)


def _layer_bwd(dx_out, p, s, nbatch, seq, tag, hooks=None):
    hooks = hooks or {}
    proj = s["proj"]
    dycat = _matmul(dx_out, s["w_out"], "nt", F32, 1024, 1024, 1024, name=f"out_bwd_dy_{tag}",
                    after=hooks.get("start_token"))
    dw_out = _matmul(s["ycat"], dx_out, "tn", F32, 1024, 1024, 1024, name=f"out_bwd_dw_{tag}")
    token = hooks["after_dycat"](dycat) if "after_dycat" in hooks else None
    dtb = p["dtb"] if token is None else p["dtb"] + token[0, 0]
    dproj = lax.empty(proj.shape, MXU_DTYPE)
    du, dproj, ddt, da_log, dd, ddtb, dssd_norm_w = _ssd_bwd(
        dycat, s["u"], proj, s["y"], s["states"], dtb, p["a_neg"], p["d_x"], p["ssd_norm_w"], dproj,
        nbatch, name=f"ssd_bwd_{tag}")
    dproj, dconv_w, dconv_b = _conv_bwd(du, proj, OFF_XBC, SSD_CONV_DIM, p["conv_w"], SSD_CONV, seq,
                                        name=f"ssd_conv_bwd_{tag}", into=dproj)
    dproj, dsinks = _attn_bwd(dycat, proj, s["o"], s["lse"], p["sinks"], ddt, dproj, nbatch,
                              name=f"attn_bwd_{tag}")
    if "after_attn" in hooks:
        hooks["after_attn"](dproj)
    dproj, ddw_w, ddw_b, dln_w, dln_b = _conf_bwd(dycat, proj, s["c1"], p["dw_w"], p["ln_w"], p["ln_b"], dproj, seq,
                                                  name=f"conf_bwd_{tag}")
    dw_in_p = _matmul(s["h_t"], dproj, "nn", F32, 1024, 512, 4096, name=f"proj_bwd_dw_{tag}")
    token = hooks["after_dw"](dw_in_p, dw_out) if "after_dw" in hooks else None
    norm_w = p["norm_w"] if token is None else p["norm_w"] + token[0, 0]
    dx_in, dnorm_w = _proj_bwd_dx(dproj, p["w_in_p"], s["x"], norm_w, dx_out, name=f"proj_bwd_dx_{tag}")
    grads = dict(
        norm_w=dnorm_w[0], w_in_p=dw_in_p, ssd_conv_w=dconv_w[:SSD_CONV], ssd_conv_b=dconv_b[0],
        ssd_dt_bias=ddtb[0, :SSD_HEADS], ssd_a_log=da_log[0, :SSD_HEADS], ssd_d=dd[0, :SSD_HEADS],
        ssd_norm_w=dssd_norm_w[0], attn_sinks=dsinks[0, :ATTN_Q_HEADS], conf_dw_w=ddw_w[:CONF_KERNEL],
        conf_dw_b=ddw_b[0], conf_ln_w=dln_w[0], conf_ln_b=dln_b[0], w_out=dw_out)
    return dx_in, grads


def _local_step(x, target, param_fns, final_norm_w, first_after=None, bwd_hooks=None):
    nbatch, seq, d = x.shape
    xt = x.reshape(nbatch * seq, d)
    saved, layer_params = [], []
    for li, fn in enumerate(param_fns):
        p = fn(xt)
        layer_params.append(p)
        xt, s = _layer_fwd(xt, p, nbatch, seq, f"l{li}", after=first_after if li == 0 else None)
        saved.append(s)
    loss, dx, dfinal = _loss_head(xt, target.reshape(nbatch * seq, d), final_norm_w.reshape(1, d), name="loss_head")
    grads = [None] * len(layer_params)
    for li in reversed(range(len(layer_params))):
        hooks = bwd_hooks(li) if bwd_hooks is not None else None
        dx, grads[li] = _layer_bwd(dx, layer_params[li], saved[li], nbatch, seq, f"l{li}", hooks=hooks)
    return loss[0, 0], dx.reshape(nbatch, seq, d), grads, dfinal[0]


MESH = pl.DeviceIdType.MESH
N_CHIPS = 4


def _mesh_pos():
    return lax.axis_index("x"), lax.axis_index("y"), lax.axis_index("c")


def _other_chips(x, y):
    return [(1 - x, y), (x, 1 - y), (1 - x, 1 - y)]


def _gather_weights(big, small, name):
    nbig, nsmall = len(big), len(small)
    n_ici = 3 * (nbig + nsmall)
    n_fwd = 3 * nbig

    def body(*refs):
        ins = refs[:nbig + nsmall]
        outs = refs[nbig + nsmall:2 * (nbig + nsmall)]
        send_sems, recv_sems = refs[2 * (nbig + nsmall):]
        x, y, c = _mesh_pos()
        me = 2 * x + y
        sibling = (x, y, 1 - c)
        chips = _other_chips(x, y)

        def ici(a, j, origin, dest):
            if a < nbig:
                src = ins[a].at[c] if origin is None else outs[a].at[origin, c]
                dst = outs[a].at[me if origin is None else origin, c]
            else:
                src = ins[a] if origin is None else outs[a].at[origin]
                dst = outs[a].at[me if origin is None else origin]
            k = a * 3 + j
            return pltpu.make_async_remote_copy(src_ref=src, dst_ref=dst, send_sem=send_sems.at[k],
                                                recv_sem=recv_sems.at[k], device_id=dest, device_id_type=MESH)

        def fwd(a, j, origin, half):
            k = n_ici + a * 3 + j
            ref = outs[a].at[origin, half]
            return pltpu.make_async_remote_copy(src_ref=ref, dst_ref=ref, send_sem=send_sems.at[k],
                                                recv_sem=recv_sems.at[k], device_id=sibling, device_id_type=MESH)

        sends = []
        for j, (px, py) in enumerate(chips):
            for a in range(nbig + nsmall):
                cp = ici(a, j, None, (px, py, c))
                cp.start()
                sends.append(cp)
        for j, (px, py) in enumerate(chips):
            origin = 2 * px + py
            for a in range(nbig):
                ici(a, j, origin, (px, py, c)).wait_recv()
                cp = fwd(a, j, origin, c)
                cp.start()
                sends.append(cp)
        for j, (px, py) in enumerate(chips):
            origin = 2 * px + py
            for a in range(nbig, nbig + nsmall):
                ici(a, j, origin, (px, py, c)).wait_recv()
            for a in range(nbig):
                fwd(a, j, origin, 1 - c).wait_recv()
        for cp in sends:
            cp.wait_send()

    out_shape = tuple(jax.ShapeDtypeStruct((N_CHIPS,) + a.shape, a.dtype) for a in list(big) + list(small))
    return pl.pallas_call(
        body, name=name, out_shape=out_shape,
        in_specs=[ANY] * (nbig + nsmall), out_specs=tuple([ANY] * (nbig + nsmall)),
        scratch_shapes=[pltpu.SemaphoreType.DMA((n_ici + n_fwd,)), pltpu.SemaphoreType.DMA((n_ici + n_fwd,))],
    )(*big, *small)


HBM = pl.BlockSpec(memory_space=pltpu.HBM)
SEM = pl.BlockSpec(memory_space=pltpu.SEMAPHORE)
DATAFLOW = pltpu.SideEffectType.DATAFLOW_SIDE_EFFECTING


def _split_peers(pattern, x, y, c):
    if pattern == "swap":
        return [((x, y, 1 - c), 1 - c, None, None)]
    me = 2 * x + y
    return [((px, py, c), 2 * px + py if pattern == "scatter" else None, me, 2 * px + py)
            for px, py in _other_chips(x, y)]


def _split_land_shape(pattern, shape):
    return {"bcast": (N_CHIPS,) + shape, "scatter": shape, "swap": shape[:1] + shape[2:]}[pattern]


def _split_copies(pattern, srcs, lands, send_sems, recv_sems, waiting):
    x, y, c = _mesh_pos()
    peers = _split_peers(pattern, x, y, c)
    cps = []
    for j, (dev, src_slot, dst_slot, my_slot) in enumerate(peers):
        for a in range(len(srcs)):
            if src_slot is None:
                src = srcs[a]
            else:
                src = srcs[a].at[:, src_slot] if pattern == "swap" else srcs[a].at[src_slot]
            slot = my_slot if waiting else dst_slot
            dst = lands[a] if slot is None else lands[a].at[slot]
            k = a * len(peers) + j
            cps.append(pltpu.make_async_remote_copy(src_ref=src, dst_ref=dst, send_sem=send_sems[k],
                                                    recv_sem=recv_sems[k], device_id=dev, device_id_type=MESH))
    return cps


def _split_start(arrs, pattern, after, name):
    n = len(arrs)
    nsem = n * (1 if pattern == "swap" else N_CHIPS - 1)
    deps = [] if after is None else [after]

    def body(*refs):
        srcs, lands = refs[:n], refs[n:2 * n]
        outs = refs[2 * n + len(deps):]
        for cp in _split_copies(pattern, srcs, lands, outs[:nsem], outs[nsem:2 * nsem], waiting=False):
            cp.start()
        outs[-1][...] = jnp.zeros_like(outs[-1])

    lands = [lax.empty(_split_land_shape(pattern, a.shape), a.dtype) for a in arrs]
    out_shape = ([pltpu.SemaphoreType.DMA(())] * (2 * nsem)
                 + [pltpu.HBM(a.shape, a.dtype) for a in arrs] + [pltpu.HBM(b.shape, b.dtype) for b in lands]
                 + [jax.ShapeDtypeStruct((SUBLANES, LANES), F32)])
    outs = pl.pallas_call(
        body, name=name, out_shape=tuple(out_shape),
        in_specs=[HBM] * (2 * n) + [ANY] * len(deps),
        out_specs=tuple([SEM] * (2 * nsem) + [HBM] * (2 * n) + [pl.BlockSpec(memory_space=pltpu.VMEM)]),
        input_output_aliases={a: 2 * nsem + a for a in range(2 * n)},
        compiler_params=pltpu.CompilerParams(has_side_effects=DATAFLOW),
    )(*[pltpu.with_memory_space_constraint(a, pltpu.HBM) for a in list(arrs) + lands], *deps)
    return outs[:-1], outs[-1]


def _split_wait(state, n, pattern, after, name):
    nsem = n * (1 if pattern == "swap" else N_CHIPS - 1)

    def body(*refs):
        srcs, lands = refs[:n], refs[n:2 * n]
        send_sems, recv_sems = refs[2 * n:2 * n + nsem], refs[2 * n + nsem:2 * n + 2 * nsem]
        for cp in _split_copies(pattern, srcs, lands, send_sems, recv_sems, waiting=True):
            cp.wait_send()
            cp.wait_recv()

    sems, thru = state[:2 * nsem], state[2 * nsem:]
    outs = pl.pallas_call(
        body, name=name, out_shape=tuple(pltpu.HBM(a.shape, a.dtype) for a in thru),
        in_specs=[HBM] * (2 * n) + [SEM] * (2 * nsem) + [ANY],
        out_specs=tuple([HBM] * (2 * n)),
        input_output_aliases={a: a for a in range(2 * n)},
        compiler_params=pltpu.CompilerParams(has_side_effects=DATAFLOW),
    )(*thru, *sems, after)
    return outs[:n], outs[n:]


def _pair_gather(arrs, layer, name):
    n = len(arrs)

    def body(*refs):
        outs = refs[n:2 * n]
        send_sems, recv_sems = refs[2 * n:]
        x, y, c = _mesh_pos()
        cps = [pltpu.make_async_remote_copy(src_ref=outs[a].at[layer, c], dst_ref=outs[a].at[layer, c],
                                            send_sem=send_sems.at[a], recv_sem=recv_sems.at[a],
                                            device_id=(x, y, 1 - c), device_id_type=MESH)
               for a in range(n)]
        for cp in cps:
            cp.start()
        for cp in cps:
            cp.wait()

    return pl.pallas_call(
        body, name=name, out_shape=tuple(jax.ShapeDtypeStruct(a.shape, a.dtype) for a in arrs),
        in_specs=[ANY] * n, out_specs=tuple([ANY] * n),
        input_output_aliases={a: a for a in range(n)},
        scratch_shapes=[pltpu.SemaphoreType.DMA((n,)), pltpu.SemaphoreType.DMA((n,))],
    )(*arrs)


N_DEV = 8


def _allreduce_small(pack, name):
    r = pack.shape[0]

    def body(p_ref, o_ref, land, send_sems, recv_sems):
        x, y, c = _mesh_pos()
        me = 4 * x + 2 * y + c
        cps = []
        for k in range(1, N_DEV):
            peer = (x ^ (k >> 2), y ^ ((k >> 1) & 1), c ^ (k & 1))
            cps.append(pltpu.make_async_remote_copy(src_ref=p_ref, dst_ref=land.at[me], send_sem=send_sems.at[k - 1],
                                                    recv_sem=recv_sems.at[k - 1], device_id=peer, device_id_type=MESH))
        for cp in cps:
            cp.start()
        land[me] = p_ref[...]
        for cp in cps:
            cp.wait()
        total = land[0]
        for d in range(1, N_DEV):
            total = total + land[d]
        o_ref[...] = total

    vm = pl.BlockSpec(memory_space=pltpu.VMEM)
    return pl.pallas_call(
        body, name=name, out_shape=jax.ShapeDtypeStruct(pack.shape, F32),
        in_specs=[vm], out_specs=vm,
        scratch_shapes=[pltpu.VMEM((N_DEV, r, LANES), F32), pltpu.SemaphoreType.DMA((N_DEV - 1,)),
                        pltpu.SemaphoreType.DMA((N_DEV - 1,))],
    )(pack)


BIG_ROWS = 128


def _cast_layer(w, layer, name):
    _, r, cdim = w.shape
    tr = BIG_ROWS

    def body(w_ref, o_ref):
        o_ref[...] = w_ref[...].astype(o_ref.dtype)

    return pl.pallas_call(
        body, name=name, out_shape=jax.ShapeDtypeStruct((r, cdim), MXU_DTYPE),
        grid=(r // tr,), in_specs=[pl.BlockSpec((None, tr, cdim), lambda i: (layer, i, 0))],
        out_specs=pl.BlockSpec((tr, cdim), lambda i: (i, 0)),
        compiler_params=_params(("parallel",)),
    )(w)


def _cast_cols_major(w_t, name):
    cdim, nl, r = w_t.shape
    tc = LANES

    def body(w_ref, *o_refs):
        for l in range(nl):
            o_refs[l][...] = w_ref[:, l, :].T.astype(o_refs[l].dtype)

    out = pl.BlockSpec((r, tc), lambda i: (0, i))
    return pl.pallas_call(
        body, name=name, out_shape=tuple(jax.ShapeDtypeStruct((r, cdim), MXU_DTYPE) for _ in range(nl)),
        grid=(pl.cdiv(cdim, tc),), in_specs=[pl.BlockSpec((tc, nl, r), lambda i: (i, 0, 0))],
        out_specs=tuple([out] * nl),
        compiler_params=_params(("parallel",)),
    )(w_t)


def _pair_sum(parts, sib, which, out_dtype, name):
    k, _, r, cdim = parts.shape
    tr = BIG_ROWS

    def body(sel_ref, p_ref, s_ref, o_ref):
        o_ref[...] = (p_ref[...] + s_ref[...]).astype(o_ref.dtype)

    grid_spec = pltpu.PrefetchScalarGridSpec(
        num_scalar_prefetch=1, grid=(k, r // tr),
        in_specs=[pl.BlockSpec((None, None, tr, cdim), lambda l, i, sel: (l, sel[0], i, 0)),
                  pl.BlockSpec((None, tr, cdim), lambda l, i, sel: (l, i, 0))],
        out_specs=pl.BlockSpec((None, tr, cdim), lambda l, i, sel: (l, i, 0)))
    return pl.pallas_call(
        body, name=name, out_shape=jax.ShapeDtypeStruct((k, r, cdim), out_dtype), grid_spec=grid_spec,
        compiler_params=_params(("parallel", "parallel")),
    )(which.reshape(1).astype(jnp.int32), parts, sib)


def _sum_lead(parts, into, layer, which, name):
    k, r, cdim = parts.shape
    tr = BIG_ROWS

    def body(sel_ref, p_ref, _, o_ref):
        total = p_ref[0].astype(F32)
        for a in range(1, k):
            total = total + p_ref[a].astype(F32)
        o_ref[...] = total

    grid_spec = pltpu.PrefetchScalarGridSpec(
        num_scalar_prefetch=1, grid=(r // tr,),
        in_specs=[pl.BlockSpec((k, tr, cdim), lambda i, sel: (0, i, 0)), ANY],
        out_specs=pl.BlockSpec((None, None, tr, cdim), lambda i, sel: (layer, sel[0], i, 0)))
    return pl.pallas_call(
        body, name=name, out_shape=jax.ShapeDtypeStruct(into.shape, F32), grid_spec=grid_spec,
        input_output_aliases={2: 0},
        compiler_params=_params(("parallel",)),
    )(which.reshape(1).astype(jnp.int32), parts, into)


def _adam_math(w, g, m, v):
    m2 = ADAM_B1 * m + (1.0 - ADAM_B1) * g
    v2 = ADAM_B2 * v + (1.0 - ADAM_B2) * (g * g)
    m_hat = m2 / (1.0 - ADAM_B1 ** ADAM_STEP)
    v_hat = v2 / (1.0 - ADAM_B2 ** ADAM_STEP)
    delta = -ADAM_LR * (m_hat / (jnp.sqrt(v_hat) + ADAM_EPS) + ADAM_WD * w)
    return delta, m2, v2


def _adam_big(w, g, m, v, name):
    nl, r, cdim = w.shape
    tr = BIG_ROWS

    def body(w_ref, g_ref, m_ref, v_ref, d_ref, mo_ref, vo_ref):
        delta, m2, v2 = _adam_math(w_ref[...], g_ref[...], m_ref[...], v_ref[...])
        d_ref[...] = delta
        mo_ref[...] = m2
        vo_ref[...] = v2

    blk = pl.BlockSpec((None, tr, cdim), lambda l, i: (l, i, 0))
    shp = jax.ShapeDtypeStruct(w.shape, F32)
    return pl.pallas_call(
        body, name=name, out_shape=(shp, shp, shp),
        grid=(nl, r // tr), in_specs=[blk] * 4, out_specs=(blk, blk, blk),
        compiler_params=_params(("parallel", "parallel")),
    )(w, g, m, v)


def _adam_cols_major(w, g, m, v, name):
    cdim, nl, r = w.shape
    tc = BIG_ROWS

    def body(w_ref, g_ref, m_ref, v_ref, d_ref, mo_ref, vo_ref):
        delta, m2, v2 = _adam_math(w_ref[...], g_ref[...], m_ref[...], v_ref[...])
        d_ref[...] = delta
        mo_ref[...] = m2
        vo_ref[...] = v2

    blk = pl.BlockSpec((tc, nl, r), lambda i: (i, 0, 0))
    shp = jax.ShapeDtypeStruct(w.shape, F32)
    return pl.pallas_call(
        body, name=name, out_shape=(shp, shp, shp),
        grid=(pl.cdiv(cdim, tc),), in_specs=[blk] * 4, out_specs=(blk, blk, blk),
        compiler_params=_params(("parallel",)),
    )(w, g, m, v)


def _adam_small(ws, gs, ms, vs, name):
    n = len(ws)

    def body(*refs):
        w_refs, g_refs, m_refs, v_refs = (refs[k * n:(k + 1) * n] for k in range(4))
        d_refs, mo_refs, vo_refs = (refs[(4 + k) * n:(5 + k) * n] for k in range(3))
        for a in range(n):
            delta, m2, v2 = _adam_math(w_refs[a][...], g_refs[a][...], m_refs[a][...], v_refs[a][...])
            d_refs[a][...] = delta
            mo_refs[a][...] = m2
            vo_refs[a][...] = v2

    shapes = tuple(jax.ShapeDtypeStruct(w.shape, F32) for w in ws)
    vm = pl.BlockSpec(memory_space=pltpu.VMEM)
    outs = pl.pallas_call(body, name=name, out_shape=shapes * 3, in_specs=[vm] * (4 * n),
                          out_specs=tuple([vm] * (3 * n)))(*ws, *gs, *ms, *vs)
    return outs[:n], outs[n:2 * n], outs[2 * n:]


PACK_TILE = SUBLANES * LANES


def _pack(arrays):
    rows = []
    for a in arrays:
        flat = a.reshape(-1)
        pad = (-flat.shape[0]) % PACK_TILE
        if pad:
            flat = jnp.concatenate([flat, jnp.zeros((pad,), flat.dtype)])
        rows.append(flat.reshape(-1, LANES))
    return jnp.concatenate(rows, axis=0)


def _unpack(pack, shapes):
    outs, row = [], 0
    for shp in shapes:
        n = int(np.prod(shp))
        nrows = -(-n // PACK_TILE) * SUBLANES
        outs.append(pack[row:row + nrows].reshape(-1)[:n].reshape(shp))
        row += nrows
    return outs


SMALL = ["norm_w", "ssd_conv_b", "ssd_dt_bias", "ssd_a_log", "ssd_d", "ssd_norm_w", "attn_sinks",
         "conf_dw_b", "conf_ln_w", "conf_ln_b"]
WEIGHTS = ["norm_w", "w_in", "ssd_conv_w", "ssd_conv_b", "ssd_dt_bias", "ssd_a_log", "ssd_d", "ssd_norm_w",
           "attn_sinks", "conf_dw_w", "conf_dw_b", "conf_ln_w", "conf_ln_b", "w_out", "final_norm_w"]


def kernel(x, norm_w, w_in, ssd_conv_w, ssd_conv_b, ssd_dt_bias, ssd_a_log, ssd_d, ssd_norm_w, attn_sinks, conf_dw_w, conf_dw_b, conf_ln_w, conf_ln_b, w_out, final_norm_w, loss_target, m_norm_w, m_w_in, m_ssd_conv_w, m_ssd_conv_b, m_ssd_dt_bias, m_ssd_a_log, m_ssd_d, m_ssd_norm_w, m_attn_sinks, m_conf_dw_w, m_conf_dw_b, m_conf_ln_w, m_conf_ln_b, m_w_out, m_final_norm_w, v_norm_w, v_w_in, v_ssd_conv_w, v_ssd_conv_b, v_ssd_dt_bias, v_ssd_a_log, v_ssd_d, v_ssd_norm_w, v_attn_sinks, v_conf_dw_w, v_conf_dw_b, v_conf_ln_w, v_conf_ln_b, v_w_out, v_final_norm_w):
    w = dict(norm_w=norm_w, w_in=w_in, ssd_conv_w=ssd_conv_w, ssd_conv_b=ssd_conv_b, ssd_dt_bias=ssd_dt_bias,
             ssd_a_log=ssd_a_log, ssd_d=ssd_d, ssd_norm_w=ssd_norm_w, attn_sinks=attn_sinks, conf_dw_w=conf_dw_w,
             conf_dw_b=conf_dw_b, conf_ln_w=conf_ln_w, conf_ln_b=conf_ln_b, w_out=w_out, final_norm_w=final_norm_w)
    m = dict(norm_w=m_norm_w, w_in=m_w_in, ssd_conv_w=m_ssd_conv_w, ssd_conv_b=m_ssd_conv_b,
             ssd_dt_bias=m_ssd_dt_bias, ssd_a_log=m_ssd_a_log, ssd_d=m_ssd_d, ssd_norm_w=m_ssd_norm_w,
             attn_sinks=m_attn_sinks, conf_dw_w=m_conf_dw_w, conf_dw_b=m_conf_dw_b, conf_ln_w=m_conf_ln_w,
             conf_ln_b=m_conf_ln_b, w_out=m_w_out, final_norm_w=m_final_norm_w)
    v = dict(norm_w=v_norm_w, w_in=v_w_in, ssd_conv_w=v_ssd_conv_w, ssd_conv_b=v_ssd_conv_b,
             ssd_dt_bias=v_ssd_dt_bias, ssd_a_log=v_ssd_a_log, ssd_d=v_ssd_d, ssd_norm_w=v_ssd_norm_w,
             attn_sinks=v_attn_sinks, conf_dw_w=v_conf_dw_w, conf_dw_b=v_conf_dw_b, conf_ln_w=v_conf_ln_w,
             conf_ln_b=v_conf_ln_b, w_out=v_w_out, final_norm_w=v_final_norm_w)
    depth = w_in.shape[0]
    me = 2 * lax.axis_index("x") + lax.axis_index("y")

    assert depth == 2
    w_in_t = jnp.transpose(w_in, (2, 0, 1))
    w_in_b = _cast_cols_major(w_in_t, name="cast_w_in")
    w_out_b = [_cast_layer(w_out, li, name=f"cast_w_out_l{li}") for li in range(depth)]
    own0 = [w_in_b[0].reshape((2, -1) + w_in_b[0].shape[1:]), ssd_conv_w, conf_dw_w]
    gathered0 = _gather_weights(own0[:1], own0[1:], name="gather_weights_l0")
    g_in0, g_conv, g_dw = [lax.dynamic_update_index_in_dim(g_all, mine, me, 0)
                           for g_all, mine in zip(gathered0, own0)]
    own1 = [w_out_b[0], w_in_b[1], w_out_b[1]]
    pending1, token1 = _split_start(own1, "bcast", gathered0[0], name="gather_rest_start")
    rest = {}

    def small_full(li):
        return (jnp.concatenate([g_conv[p, li] for p in range(N_CHIPS)], axis=1),
                jnp.concatenate([g_dw[p, li] for p in range(N_CHIPS)], axis=1))

    def w_out_l0(after):
        mine1, landed = _split_wait(pending1, len(own1), "bcast", after, name="gather_rest_wait")
        rest["landed"] = [lax.dynamic_update_index_in_dim(g_all, mine, me, 0) for g_all, mine in zip(landed, mine1)]
        return rest["landed"][0].reshape(-1, w_out.shape[2])

    def params_l0(_):
        w_in_p = _padded_from_chips([g_in0[p].reshape(w_in_b[0].shape) for p in range(N_CHIPS)])
        return _layer_params(0, w_in_p, w_out_l0, *small_full(0), w)

    def params_l1(_):
        _, g_in1, g_out1 = rest["landed"]
        w_in_p = _padded_from_chips([g_in1[p] for p in range(N_CHIPS)])
        return _layer_params(1, w_in_p, g_out1.reshape(-1, g_out1.shape[-1]), *small_full(1), w)

    c = lax.axis_index("c")
    cols = w_in.shape[2]
    rows_out = w_out.shape[1]

    def grad_parts(g):
        dw = g["w_in_p"]
        return [dw.reshape(1, 2, dw.shape[0] // 2, dw.shape[1]),
                g["w_out"].reshape(N_CHIPS, 2, rows_out // 2, D_MODEL)]

    def pair_sums(parts, sib, tag):
        s_in, s_out = [_pair_sum(p, sb, c, MXU_DTYPE, name=f"grad_pair_sum_{k}_{tag}")
                       for k, (p, sb) in enumerate(zip(parts, sib))]
        return [jnp.stack([_chip_part_from_padded(s_in[0], p, cols) for p in range(N_CHIPS)]), s_out]

    split = {"reduced": [lax.empty((depth, 2, w_in.shape[1] // 2, cols), F32),
                         lax.empty((depth, 2, rows_out // 2, D_MODEL), F32)]}

    def chip_sums(landed, sent, li, which=(0, 1)):
        filled = [lax.dynamic_update_index_in_dim(r, lax.dynamic_index_in_dim(sk, me, 0, keepdims=False), me, 0)
                  for r, sk in zip(landed, sent)]
        tag = "".join(str(k) for k in which)
        halves = [_sum_lead(r, split["reduced"][k], li, c, name=f"grad_chip_sum_{k}_l{li}")
                  for k, r in zip(which, filled)]
        for k, buf in zip(which, _pair_gather(halves, li, name=f"grad_pair_gather_{tag}_l{li}")):
            split["reduced"][k] = buf

    def bwd_hooks(li):
        def after_dw(dw_in_p, dw_out):
            parts = grad_parts(dict(w_in_p=dw_in_p, w_out=dw_out))
            state, token = _split_start(parts, "swap", None, name=f"grad_swap_l{li}_start")
            split[f"swap{li}"] = (parts, state)
            return token

        hooks = {"after_dw": after_dw}
        if li == depth - 2:
            parts, swap_state = split[f"swap{depth - 1}"]

            def after_dycat(dycat):
                mine, sib = _split_wait(swap_state, len(parts), "swap", dycat, name="grad_swap_l1_wait")
                sent = pair_sums(mine, sib, "l1")
                split["scatter"], token = _split_start(sent, "scatter", None, name="grad_scatter_l1_start")
                return token

            def after_attn(dproj):
                sent, landed = _split_wait(split["scatter"], len(parts), "scatter", dproj,
                                           name="grad_scatter_l1_wait")
                chip_sums(landed, sent, depth - 1)

            hooks.update(after_dycat=after_dycat, after_attn=after_attn)
        return hooks

    loss, grad_x, grads, dfinal = _local_step(x, loss_target, [params_l0, params_l1], final_norm_w,
                                              first_after=token1, bwd_hooks=bwd_hooks)

    parts0, swap0 = split["swap0"]
    sent0 = pair_sums(*_split_wait(swap0, len(parts0), "swap", grad_x, name="grad_swap_l0_wait"), "l0")
    scatter0_out, token_out = _split_start(sent0[1:], "scatter", None, name="grad_scatter_out_l0_start")
    scatter0_in, token0 = _split_start(sent0[:1], "scatter", token_out, name="grad_scatter_in_l0_start")

    small_list = [grads[li][n] for li in range(depth) for n in SMALL]
    small_list += [grads[li][n] for li in range(depth) for n in ("ssd_conv_w", "conf_dw_w")]
    small_list += [dfinal, loss.reshape(1)]
    small_shapes = [a.shape for a in small_list]
    reduced = _unpack(_allreduce_small(_pack(small_list) + token0[0, 0], name="allreduce_small"), small_shapes)
    ns = len(SMALL)
    g = {n: jnp.stack([reduced[li * ns + i] for li in range(depth)]) for i, n in enumerate(SMALL)}
    conv_w_cols, dw_w_cols = ssd_conv_w.shape[2], conf_dw_w.shape[2]
    g["ssd_conv_w"] = jnp.stack([lax.dynamic_slice_in_dim(reduced[depth * ns + 2 * li], me * conv_w_cols,
                                                          conv_w_cols, axis=1) for li in range(depth)])
    g["conf_dw_w"] = jnp.stack([lax.dynamic_slice_in_dim(reduced[depth * ns + 2 * li + 1], me * dw_w_cols,
                                                         dw_w_cols, axis=1) for li in range(depth)])
    g["final_norm_w"] = reduced[-2]
    loss_total = reduced[-1][0]

    small_names = [n for n in WEIGHTS if n not in ("w_in", "w_out")]

    def as2d(a):
        return a.reshape(1, -1) if a.ndim == 1 else a

    deltas, new_ms, new_vs = _adam_small(*[[as2d(src[n]) for n in small_names] for src in (w, g, m, v)],
                                         name="adam_small")

    sent_out, landed_out = _split_wait(scatter0_out, 1, "scatter", deltas[0], name="grad_scatter_out_l0_wait")
    chip_sums(landed_out, sent_out, 0, which=(1,))
    g_w_out = split["reduced"][1].reshape(w_out.shape)
    outs_g, outs_d, outs_m, outs_v = {"w_out": g_w_out}, {}, {}, {}
    outs_d["w_out"], outs_m["w_out"], outs_v["w_out"] = _adam_big(w_out, g_w_out, m_w_out, v_w_out,
                                                                  name="adam_w_out")
    sent_in, landed_in = _split_wait(scatter0_in, 1, "scatter", outs_d["w_out"], name="grad_scatter_in_l0_wait")
    chip_sums(landed_in, sent_in, 0, which=(0,))
    g_w_in = split["reduced"][0].reshape(w_in.shape)
    outs_g["w_in"] = g_w_in
    to_cols, from_cols = (2, 0, 1), (1, 2, 0)
    outs_d["w_in"], outs_m["w_in"], outs_v["w_in"] = [
        jnp.transpose(a, from_cols) for a in _adam_cols_major(
            *[jnp.transpose(a, to_cols) for a in (w_in, g_w_in, m_w_in, v_w_in)], name="adam_w_in")]
    for n, dn, mn, vn in zip(small_names, deltas, new_ms, new_vs):
        outs_g[n], outs_d[n], outs_m[n], outs_v[n] = (g[n], dn.reshape(w[n].shape), mn.reshape(w[n].shape),
                                                      vn.reshape(w[n].shape))
    return (loss_total, grad_x, *[outs_g[n] for n in WEIGHTS], *[outs_d[n] for n in WEIGHTS],
            *[outs_m[n] for n in WEIGHTS], *[outs_v[n] for n in WEIGHTS])
```

```python
import functools
import math

import jax
import jax.numpy as jnp
import numpy as np
from jax import lax
from jax.experimental import pallas as pl
from jax.experimental.pallas import tpu as pltpu

F32 = jnp.float32
BF16 = jnp.bfloat16
MXU_DTYPE = BF16

D_MODEL = 1024
DEPTH = 2
SSD_HEADS = 16
SSD_HEAD_DIM = 64
SSD_STATE = 128
SSD_CONV = 4
CHUNK = 128
SSD_CONV_DIM = 1536
ATTN_HEAD_DIM = 64
ATTN_Q_HEADS = 8
WINDOW = 128
CONF_WIDTH = 512
CONF_KERNEL = 31
MIX_WIDTH = 2048
D_IN_PROJ = 5392
EPS = 1e-5

ADAM_LR = 0.001
ADAM_B1 = 0.9
ADAM_B2 = 0.999
ADAM_EPS = 1e-08
ADAM_WD = 0.01
ADAM_STEP = 10

LANES = 128
SUBLANES = 8
VMEM_LIMIT = 48 * 1024 * 1024

NP = 5632
OFF_ZA, OFF_Q, OFF_K, OFF_V, OFF_DT = 0, 512, 1024, 1152, 1280
ATTN_GROUP = 1536
OFF_CONF, OFF_ZC = 1536, 2560
CONF_GROUP = 1536
OFF_ZS = 3072
OFF_XBC = 4096
SECTIONS = ((0, 1024, OFF_ZS), (1024, 1536, OFF_ZA), (1536, 2048, OFF_ZC), (2048, 3584, OFF_XBC),
            (3584, 3600, OFF_DT), (3600, 4368, OFF_Q), (4368, 5392, OFF_CONF))

YCAT_ATTN, YCAT_CONF = 1024, 1536
ANY = pl.BlockSpec(memory_space=pl.ANY)

NN = (((1,), (0,)), ((), ()))
NT = (((1,), (1,)), ((), ()))
TN = (((0,), (0,)), ((), ()))


def _params(sem):
    return pltpu.CompilerParams(dimension_semantics=sem, vmem_limit_bytes=VMEM_LIMIT)


def _dot(a, b, dims=NN):
    return lax.dot_general(a.astype(MXU_DTYPE), b.astype(MXU_DTYPE), dims, preferred_element_type=F32)


def _split_bf16(a, passes):
    pieces = []
    r = a
    for _ in range(passes):
        p = r.astype(BF16)
        pieces.append(p)
        r = r - p.astype(F32)
    return pieces


def _xdot(a, sel, dims=NN, passes=2):
    out = None
    for p in _split_bf16(a, passes):
        t = lax.dot_general(p, sel, dims, preferred_element_type=F32)
        out = t if out is None else out + t
    return out


def _xdot_r(sel, b, dims=NN, passes=3):
    out = None
    for p in _split_bf16(b, passes):
        t = lax.dot_general(sel, p, dims, preferred_element_type=F32)
        out = t if out is None else out + t
    return out


def _sigmoid(x):
    return 1.0 / (1.0 + jnp.exp(-x))


def _silu(x):
    return x * _sigmoid(x)


def _dsilu(x):
    s = _sigmoid(x)
    return s * (1.0 + x * (1.0 - s))


def _softplus(x):
    return jnp.maximum(x, 0.0) + jnp.log(1.0 + jnp.exp(-jnp.abs(x)))


def _rowsum8(x):
    r, c = x.shape
    return jnp.sum(x.reshape(r // SUBLANES, SUBLANES, c), axis=0)


def _iota(shape, dim):
    return lax.broadcasted_iota(jnp.int32, shape, dim)


def _matmul(a, b, form, out_dtype, tm, tn, tk, name, residual=None, after=None):
    if form == "nn":
        (m, k), n = a.shape, b.shape[1]
    elif form == "nt":
        (m, k), n = a.shape, b.shape[0]
    else:
        (k, m), n = a.shape, b.shape[1]
    tm, tn, tk = min(tm, m), min(tn, n), min(tk, k)
    assert m % tm == 0 and n % tn == 0 and k % tk == 0, (name, m, n, k, tm, tn, tk)
    if form == "nn":
        a_spec = pl.BlockSpec((tm, tk), lambda i, j, s: (i, s))
        b_spec = pl.BlockSpec((tk, tn), lambda i, j, s: (s, j))
        dims = NN
    elif form == "nt":
        (m, k), n = a.shape, b.shape[0]
        a_spec = pl.BlockSpec((tm, tk), lambda i, j, s: (i, s))
        b_spec = pl.BlockSpec((tn, tk), lambda i, j, s: (j, s))
        dims = NT
    else:
        (k, m), n = a.shape, b.shape[1]
        a_spec = pl.BlockSpec((tk, tm), lambda i, j, s: (s, i))
        b_spec = pl.BlockSpec((tk, tn), lambda i, j, s: (s, j))
        dims = TN
    nk = k // tk
    has_res = residual is not None
    deps = [] if after is None else [after]

    def body_single(a_ref, b_ref, *rest):
        o = _dot(a_ref[...], b_ref[...], dims)
        if has_res:
            o = o + rest[0][...]
        rest[-1][...] = o.astype(out_dtype)

    def body(a_ref, b_ref, *rest):
        r_ref = rest[0] if has_res else None
        o_ref, acc = rest[-2:]
        s = pl.program_id(2)

        @pl.when(s == 0)
        def _():
            acc[...] = jnp.zeros_like(acc)

        acc[...] += _dot(a_ref[...], b_ref[...], dims)

        @pl.when(s == nk - 1)
        def _():
            o = acc[...]
            if has_res:
                o = o + r_ref[...]
            o_ref[...] = o.astype(out_dtype)

    in_specs = [a_spec, b_spec]
    args = [a, b]
    if has_res:
        in_specs.append(pl.BlockSpec((tm, tn), lambda i, j, s: (i, j)))
        args.append(residual)
    in_specs += [ANY] * len(deps)
    args += deps
    return pl.pallas_call(
        body_single if nk == 1 else body, name=name,
        out_shape=jax.ShapeDtypeStruct((m, n), out_dtype),
        grid=(m // tm, n // tn, nk),
        in_specs=in_specs,
        out_specs=pl.BlockSpec((tm, tn), lambda i, j, s: (i, j)),
        scratch_shapes=[] if nk == 1 else [pltpu.VMEM((tm, tn), F32)],
        compiler_params=_params(("parallel", "parallel", "arbitrary")),
    )(*args)


ROW_TILE = 256


PROJ_FWD_TM, PROJ_FWD_TN = 1024, 512


def _proj_fwd(x, w, w_in_p, name, after=None):
    t, d = x.shape
    n = w_in_p.shape[1]
    tm, tn = min(PROJ_FWD_TM, t), PROJ_FWD_TN
    assert t % tm == 0 and n % tn == 0
    deps = [] if after is None else [after]

    def body(x_ref, w_ref, b_ref, *rest):
        o_ref, ot_ref, h_scr = rest[len(deps):]

        @pl.when(pl.program_id(1) == 0)
        def _():
            xv = x_ref[...]
            rstd = lax.rsqrt(jnp.mean(xv * xv, axis=-1, keepdims=True) + EPS)
            h = xv * rstd * w_ref[...]
            h_scr[...] = h.astype(h_scr.dtype)
            ot_ref[...] = h.T.astype(ot_ref.dtype)

        o_ref[...] = _dot(h_scr[...], b_ref[...])

    return pl.pallas_call(
        body, name=name,
        out_shape=(jax.ShapeDtypeStruct((t, n), F32), jax.ShapeDtypeStruct((d, t), MXU_DTYPE)),
        grid=(t // tm, n // tn),
        in_specs=[pl.BlockSpec((tm, d), lambda i, j: (i, 0)), pl.BlockSpec((1, d), lambda i, j: (0, 0)),
                  pl.BlockSpec((d, tn), lambda i, j: (0, j))] + [ANY] * len(deps),
        out_specs=(pl.BlockSpec((tm, tn), lambda i, j: (i, j)), pl.BlockSpec((d, tm), lambda i, j: (0, i))),
        scratch_shapes=[pltpu.VMEM((tm, d), MXU_DTYPE)],
        compiler_params=_params(("parallel", "arbitrary")),
    )(x, w, w_in_p, *deps)


PROJ_BWD_TM, PROJ_BWD_TK = 1024, 1408


def _proj_bwd_dx(dproj, w_in_p, x, w, dres, name):
    t, d = x.shape
    kdim = dproj.shape[1]
    tm, tk = min(PROJ_BWD_TM, t), PROJ_BWD_TK
    nt, nk = t // tm, kdim // tk
    assert t % tm == 0 and kdim % tk == 0

    def body(a_ref, b_ref, x_ref, w_ref, dr_ref, dx_ref, dw_ref, acc, wacc):
        i, s = pl.program_id(0), pl.program_id(1)

        @pl.when((i == 0) & (s == 0))
        def _():
            wacc[...] = jnp.zeros_like(wacc)

        @pl.when(s == 0)
        def _():
            acc[...] = jnp.zeros_like(acc)

        acc[...] += _dot(a_ref[...], b_ref[...], NT)

        @pl.when(s == nk - 1)
        def _():
            xv = x_ref[...]
            rstd = lax.rsqrt(jnp.mean(xv * xv, axis=-1, keepdims=True) + EPS)
            xh = xv * rstd
            dhv = acc[...]
            g = dhv * w_ref[...]
            dx_ref[...] = dr_ref[...] + rstd * (g - xh * jnp.mean(g * xh, axis=-1, keepdims=True))
            wacc[...] += _rowsum8(dhv * xh)

        @pl.when((i == nt - 1) & (s == nk - 1))
        def _():
            dw_ref[...] = jnp.sum(wacc[...], axis=0, keepdims=True)

    row = pl.BlockSpec((tm, d), lambda i, s: (i, 0))
    vec = pl.BlockSpec((1, d), lambda i, s: (0, 0))
    return pl.pallas_call(
        body, name=name,
        out_shape=(jax.ShapeDtypeStruct((t, d), F32), jax.ShapeDtypeStruct((1, d), F32)),
        grid=(nt, nk),
        in_specs=[pl.BlockSpec((tm, tk), lambda i, s: (i, s)), pl.BlockSpec((d, tk), lambda i, s: (0, s)),
                  row, vec, row],
        out_specs=(row, vec),
        scratch_shapes=[pltpu.VMEM((tm, d), F32), pltpu.VMEM((SUBLANES, d), F32)],
        compiler_params=_params(("arbitrary", "arbitrary")),
    )(dproj, w_in_p, x, w, dres)


def _loss_head(xf, target, w, name):
    t, d = xf.shape
    tm = ROW_TILE
    nt = t // tm

    def body(x_ref, t_ref, w_ref, loss_ref, dx_ref, dw_ref, lacc, wacc):
        i = pl.program_id(0)

        @pl.when(i == 0)
        def _():
            lacc[...] = jnp.zeros_like(lacc)
            wacc[...] = jnp.zeros_like(wacc)

        xv = x_ref[...]
        rstd = lax.rsqrt(jnp.mean(xv * xv, axis=-1, keepdims=True) + EPS)
        xh = xv * rstd
        err = xh * w_ref[...] - t_ref[...]
        lacc[...] += jnp.sum(err * err)
        dy = err * (1.0 / d)
        g = dy * w_ref[...]
        dx_ref[...] = rstd * (g - xh * jnp.mean(g * xh, axis=-1, keepdims=True))
        wacc[...] += _rowsum8(dy * xh)

        @pl.when(i == nt - 1)
        def _():
            loss_ref[...] = lacc[...] * (0.5 / d)
            dw_ref[...] = jnp.sum(wacc[...], axis=0, keepdims=True)

    row = pl.BlockSpec((tm, d), lambda i: (i, 0))
    vec = pl.BlockSpec((1, d), lambda i: (0, 0))
    return pl.pallas_call(
        body, name=name,
        out_shape=(jax.ShapeDtypeStruct((SUBLANES, LANES), F32), jax.ShapeDtypeStruct((t, d), F32),
                   jax.ShapeDtypeStruct((1, d), F32)),
        grid=(nt,),
        in_specs=[row, row, vec],
        out_specs=(pl.BlockSpec((SUBLANES, LANES), lambda i: (0, 0)), row, vec),
        scratch_shapes=[pltpu.VMEM((SUBLANES, LANES), F32), pltpu.VMEM((SUBLANES, d), F32)],
        compiler_params=_params(("arbitrary",)),
    )(xf, target, w)


CONV_TILE = 512
CONV_COLS = 512
CONV_SUB_ROWS = 128
CONV_SUB_COLS = LANES


def _conv_halo(k):
    return SUBLANES if k - 1 <= SUBLANES else 32


def _conv_subtiles(tm, cw):
    return [(r0, c0) for r0 in range(0, tm, CONV_SUB_ROWS) for c0 in range(0, cw, CONV_SUB_COLS)]


def _conv_use_shifted(k):
    return k > SUBLANES


def _conv_shift_scratch(k, rows, cw):
    return [pltpu.VMEM((SUBLANES - 1, rows - SUBLANES, cw), F32)] if _conv_use_shifted(k) else []


def _conv_fill_shifted(ext, sh):
    n = sh.shape[1]
    for b in range(1, SUBLANES):
        sh[b - 1] = ext[b:b + n, :]


def _conv_rows(ext, sh, start, rows, cs):
    b = start % SUBLANES
    if b == 0 or not sh:
        return ext[start:start + rows, cs]
    return sh[0][b - 1, start - b:start - b + rows, cs]


def _conv_fwd(src, col0, width, w, bias, k, seq, name):
    t = src.shape[0]
    tm, cw, halo = CONV_TILE, CONV_COLS, _conv_halo(k)
    sr, sc = CONV_SUB_ROWS, CONV_SUB_COLS
    p = k - 1
    cb0 = col0 // cw
    kp = w.shape[0]

    shifted = _conv_use_shifted(k)

    def body(x_ref, h_ref, w_ref, b_ref, o_ref, ext, *sh):
        i = pl.program_id(0)
        seq_start = (i * tm) % seq == 0
        ext[halo:, :] = x_ref[...]
        ext[:halo, :] = jnp.where(seq_start, 0.0, h_ref[...])
        if shifted:
            _conv_fill_shifted(ext, sh[0])
        for r0, c0 in _conv_subtiles(tm, cw):
            cs = slice(c0, c0 + sc)
            acc = jnp.zeros((sr, sc), F32) + b_ref[:, cs]
            for j in range(k):
                acc = acc + w_ref[j:j + 1, cs] * _conv_rows(ext, sh, r0 + halo - p + j, sr, cs)
            o_ref[r0:r0 + sr, cs] = acc

    return pl.pallas_call(
        body, name=name,
        out_shape=jax.ShapeDtypeStruct((t, width), F32),
        grid=(t // tm, width // cw),
        in_specs=[pl.BlockSpec((tm, cw), lambda i, j: (i, cb0 + j)),
                  pl.BlockSpec((halo, cw), lambda i, j: (jnp.maximum(i * (tm // halo) - 1, 0), cb0 + j)),
                  pl.BlockSpec((kp, cw), lambda i, j: (0, j)),
                  pl.BlockSpec((1, cw), lambda i, j: (0, j))],
        out_specs=pl.BlockSpec((tm, cw), lambda i, j: (i, j)),
        scratch_shapes=[pltpu.VMEM((halo + tm, cw), F32)] + _conv_shift_scratch(k, halo + tm, cw),
        compiler_params=_params(("parallel", "parallel")),
    )(src, src, w, bias)


def _conv_bwd(dy, src, col0, width, w, k, seq, name, into=None):
    t = src.shape[0]
    tm, cw, halo = CONV_TILE, CONV_COLS, _conv_halo(k)
    sr, sc = CONV_SUB_ROWS, CONV_SUB_COLS
    p = k - 1
    cb0 = col0 // cw
    kp = w.shape[0]
    nt = t // tm
    last_halo = t // halo - 1

    shifted = _conv_use_shifted(k)

    def body(dy_ref, dn_ref, x_ref, xp_ref, w_ref, *rest):
        if into is not None:
            rest = rest[1:]
        dx_ref, dw_ref, db_ref, dyext, xext, wacc, bacc = rest[:7]
        sh = rest[7:]
        i = pl.program_id(1)
        dysh, xsh = (sh[:1], sh[1:]) if shifted else ((), ())

        @pl.when(i == 0)
        def _():
            wacc[...] = jnp.zeros_like(wacc)
            bacc[...] = jnp.zeros_like(bacc)

        seq_start = (i * tm) % seq == 0
        seq_end = ((i + 1) * tm) % seq == 0
        dyext[:tm, :] = dy_ref[...]
        dyext[tm:, :] = jnp.where(seq_end, 0.0, dn_ref[...])
        xext[halo:, :] = x_ref[...]
        xext[:halo, :] = jnp.where(seq_start, 0.0, xp_ref[...])
        if shifted:
            _conv_fill_shifted(dyext, dysh[0])
            _conv_fill_shifted(xext, xsh[0])
        for r0, c0 in _conv_subtiles(tm, cw):
            cs = slice(c0, c0 + sc)
            dyv = dy_ref[r0:r0 + sr, cs]
            acc = jnp.zeros((sr, sc), F32)
            for j in range(k):
                acc = acc + w_ref[j:j + 1, cs] * _conv_rows(dyext, dysh, r0 + p - j, sr, cs)
                wacc[j, :, cs] += _rowsum8(dyv * _conv_rows(xext, xsh, r0 + halo - p + j, sr, cs))
            dx_ref[r0:r0 + sr, cs] = acc.astype(dx_ref.dtype)
            bacc[:, cs] += _rowsum8(dyv)

        @pl.when(i == nt - 1)
        def _():
            dw_ref[...] = jnp.zeros_like(dw_ref)
            for j in range(k):
                dw_ref[j:j + 1, :] = jnp.sum(wacc[j], axis=0, keepdims=True)
            db_ref[...] = jnp.sum(bacc[...], axis=0, keepdims=True)

    if into is None:
        dx_shape = jax.ShapeDtypeStruct((t, width), F32)
        dx_spec = pl.BlockSpec((tm, cw), lambda j, i: (i, j))
        extra_specs, extra_args, aliases = [], [], {}
    else:
        dx_shape = jax.ShapeDtypeStruct(into.shape, into.dtype)
        dx_spec = pl.BlockSpec((tm, cw), lambda j, i: (i, cb0 + j))
        extra_specs, extra_args, aliases = [ANY], [into], {5: 0}
    return pl.pallas_call(
        body, name=name,
        out_shape=(dx_shape, jax.ShapeDtypeStruct((kp, width), F32), jax.ShapeDtypeStruct((1, width), F32)),
        grid=(width // cw, nt),
        in_specs=[pl.BlockSpec((tm, cw), lambda j, i: (i, j)),
                  pl.BlockSpec((halo, cw), lambda j, i: (jnp.minimum((i + 1) * (tm // halo), last_halo), j)),
                  pl.BlockSpec((tm, cw), lambda j, i: (i, cb0 + j)),
                  pl.BlockSpec((halo, cw), lambda j, i: (jnp.maximum(i * (tm // halo) - 1, 0), cb0 + j)),
                  pl.BlockSpec((kp, cw), lambda j, i: (0, j))] + extra_specs,
        out_specs=(dx_spec,
                   pl.BlockSpec((kp, cw), lambda j, i: (0, j)),
                   pl.BlockSpec((1, cw), lambda j, i: (0, j))),
        input_output_aliases=aliases,
        scratch_shapes=[pltpu.VMEM((tm + halo, cw), F32), pltpu.VMEM((halo + tm, cw), F32),
                        pltpu.VMEM((kp, SUBLANES, cw), F32), pltpu.VMEM((SUBLANES, cw), F32)]
        + 2 * _conv_shift_scratch(k, halo + tm, cw),
        compiler_params=_params(("parallel", "arbitrary")),
    )(dy, dy, src, src, w, *extra_args)


def _conf_specs(tm, cw, halo, order):
    cb = OFF_CONF // cw

    def blk(col):
        return pl.BlockSpec((tm, cw), lambda *g: (order(*g), col))

    def prev(col):
        return pl.BlockSpec((halo, cw), lambda *g: (jnp.maximum(order(*g) * (tm // halo) - 1, 0), col))

    return blk(cb), prev(cb), blk(cb + 1), prev(cb + 1)


def _glu_window(ext, a_ref, ah_ref, g_ref, gh_ref, seq_start, halo):
    ext[halo:, :] = a_ref[...] * _sigmoid(g_ref[...])
    ext[:halo, :] = jnp.where(seq_start, 0.0, ah_ref[...] * _sigmoid(gh_ref[...]))


def _conf_fwd(proj, w, bias, ln_w, ln_b, ycat, seq, name):
    t = proj.shape[0]
    k = CONF_KERNEL
    tm, cw, halo = CONV_TILE, CONF_WIDTH, _conv_halo(k)
    sr, sc = CONV_SUB_ROWS, CONV_SUB_COLS
    p = k - 1
    kp = w.shape[0]

    def body(a_ref, ah_ref, g_ref, gh_ref, z_ref, w_ref, b_ref, lw_ref, lb_ref, _, c1_ref, y_ref, ext, sh):
        i = pl.program_id(0)
        _glu_window(ext, a_ref, ah_ref, g_ref, gh_ref, (i * tm) % seq == 0, halo)
        _conv_fill_shifted(ext, sh)
        for r0, c0 in _conv_subtiles(tm, cw):
            cs = slice(c0, c0 + sc)
            acc = jnp.zeros((sr, sc), F32) + b_ref[:, cs]
            for j in range(k):
                acc = acc + w_ref[j:j + 1, cs] * _conv_rows(ext, (sh,), r0 + halo - p + j, sr, cs)
            c1_ref[r0:r0 + sr, cs] = acc
        for r0 in range(0, tm, sr):
            rows = slice(r0, r0 + sr)
            cv = c1_ref[rows, :]
            xc = cv - jnp.mean(cv, axis=-1, keepdims=True)
            rstd = lax.rsqrt(jnp.mean(xc * xc, axis=-1, keepdims=True) + EPS)
            c2 = xc * rstd * lw_ref[...] + lb_ref[...]
            y_ref[rows, :] = (_silu(c2) * _silu(z_ref[rows, :])).astype(y_ref.dtype)

    vec = pl.BlockSpec((1, cw), lambda i: (0, 0))
    row = pl.BlockSpec((tm, cw), lambda i: (i, 0))
    return pl.pallas_call(
        body, name=name,
        out_shape=(jax.ShapeDtypeStruct((t, cw), F32), jax.ShapeDtypeStruct(ycat.shape, ycat.dtype)),
        grid=(t // tm,),
        in_specs=[*_conf_specs(tm, cw, halo, lambda i: i),
                  pl.BlockSpec((tm, cw), lambda i: (i, OFF_ZC // cw)),
                  pl.BlockSpec((kp, cw), lambda i: (0, 0)), vec, vec, vec, ANY],
        out_specs=(row, pl.BlockSpec((tm, cw), lambda i: (i, YCAT_CONF // cw))),
        input_output_aliases={9: 1},
        scratch_shapes=[pltpu.VMEM((halo + tm, cw), F32)] + _conv_shift_scratch(k, halo + tm, cw),
        compiler_params=_params(("parallel",)),
    )(proj, proj, proj, proj, proj, w, bias, ln_w, ln_b, ycat)


def _conf_bwd(dycat, proj, c1, w, ln_w, ln_b, dproj, seq, name):
    t = proj.shape[0]
    k = CONF_KERNEL
    tm, cw, halo = CONV_TILE, CONF_WIDTH, _conv_halo(k)
    sr, sc = CONV_SUB_ROWS, CONV_SUB_COLS
    p = k - 1
    kp = w.shape[0]
    nt = t // tm
    last_halo = t // halo - 1

    def body(dy_ref, dyn_ref, c_ref, cn_ref, z_ref, zn_ref, a_ref, ah_ref, g_ref, gh_ref, w_ref, lw_ref, lb_ref, _,
             grp_ref, dw_ref, db_ref, dlw_ref, dlb_ref, dyext, xext, wacc, bacc, lwacc, lbacc, dysh, xsh):
        i = pl.program_id(0)

        @pl.when(i == 0)
        def _():
            wacc[...] = jnp.zeros_like(wacc)
            bacc[...] = jnp.zeros_like(bacc)
            lwacc[...] = jnp.zeros_like(lwacc)
            lbacc[...] = jnp.zeros_like(lbacc)

        def post_bwd(dy, cv, zv):
            xc = cv - jnp.mean(cv, axis=-1, keepdims=True)
            rstd = lax.rsqrt(jnp.mean(xc * xc, axis=-1, keepdims=True) + EPS)
            xh = xc * rstd
            c2 = xh * lw_ref[...] + lb_ref[...]
            dz = dy * _silu(c2) * _dsilu(zv)
            dc2 = dy * _silu(zv) * _dsilu(c2)
            dxh = dc2 * lw_ref[...]
            dc = rstd * (dxh - jnp.mean(dxh, axis=-1, keepdims=True)
                         - xh * jnp.mean(dxh * xh, axis=-1, keepdims=True))
            return dc, dz, dc2 * xh, dc2

        seq_end = ((i + 1) * tm) % seq == 0
        for r0 in range(0, tm, sr):
            rows = slice(r0, r0 + sr)
            dc, dz, lw_terms, lb_terms = post_bwd(dy_ref[rows, :], c_ref[rows, :], z_ref[rows, :])
            dyext[rows, :] = dc
            grp_ref[rows, 2 * cw:] = dz.astype(grp_ref.dtype)
            lwacc[...] += _rowsum8(lw_terms)
            lbacc[...] += _rowsum8(lb_terms)
        dc_next = post_bwd(dyn_ref[...], cn_ref[...], zn_ref[...])[0]
        dyext[tm:, :] = jnp.where(seq_end, 0.0, dc_next)
        _glu_window(xext, a_ref, ah_ref, g_ref, gh_ref, (i * tm) % seq == 0, halo)
        _conv_fill_shifted(dyext, dysh)
        _conv_fill_shifted(xext, xsh)
        dag_ref = grp_ref
        for r0, c0 in _conv_subtiles(tm, cw):
            cs = slice(c0, c0 + sc)
            rows = slice(r0, r0 + sr)
            dyv = dyext[rows, cs]
            acc = jnp.zeros((sr, sc), F32)
            for j in range(k):
                acc = acc + w_ref[j:j + 1, cs] * _conv_rows(dyext, (dysh,), r0 + p - j, sr, cs)
                wacc[j, :, cs] += _rowsum8(dyv * _conv_rows(xext, (xsh,), r0 + halo - p + j, sr, cs))
            bacc[:, cs] += _rowsum8(dyv)
            s = _sigmoid(g_ref[rows, cs])
            dag_ref[rows, cs] = (acc * s).astype(dag_ref.dtype)
            dag_ref[rows, cw + c0:cw + c0 + sc] = (acc * a_ref[rows, cs] * s * (1.0 - s)).astype(dag_ref.dtype)

        @pl.when(i == nt - 1)
        def _():
            dw_ref[...] = jnp.zeros_like(dw_ref)
            for j in range(k):
                dw_ref[j:j + 1, :] = jnp.sum(wacc[j], axis=0, keepdims=True)
            db_ref[...] = jnp.sum(bacc[...], axis=0, keepdims=True)
            dlw_ref[...] = jnp.sum(lwacc[...], axis=0, keepdims=True)
            dlb_ref[...] = jnp.sum(lbacc[...], axis=0, keepdims=True)

    def blk(col):
        return pl.BlockSpec((tm, cw), lambda i: (i, col))

    def nxt(col):
        return pl.BlockSpec((halo, cw), lambda i: (jnp.minimum((i + 1) * (tm // halo), last_halo), col))

    vec = pl.BlockSpec((1, cw), lambda i: (0, 0))
    return pl.pallas_call(
        body, name=name,
        out_shape=(jax.ShapeDtypeStruct(dproj.shape, dproj.dtype), jax.ShapeDtypeStruct((kp, cw), F32),
                   jax.ShapeDtypeStruct((1, cw), F32), jax.ShapeDtypeStruct((1, cw), F32),
                   jax.ShapeDtypeStruct((1, cw), F32)),
        grid=(nt,),
        in_specs=[blk(YCAT_CONF // cw), nxt(YCAT_CONF // cw), blk(0), nxt(0), blk(OFF_ZC // cw), nxt(OFF_ZC // cw),
                  *_conf_specs(tm, cw, halo, lambda i: i),
                  pl.BlockSpec((kp, cw), lambda i: (0, 0)), vec, vec, ANY],
        out_specs=(pl.BlockSpec((tm, CONF_GROUP), lambda i: (i, OFF_CONF // CONF_GROUP)),
                   pl.BlockSpec((kp, cw), lambda i: (0, 0)), vec, vec, vec),
        input_output_aliases={13: 0},
        scratch_shapes=[pltpu.VMEM((tm + halo, cw), F32), pltpu.VMEM((halo + tm, cw), F32),
                        pltpu.VMEM((kp, SUBLANES, cw), F32), pltpu.VMEM((SUBLANES, cw), F32),
                        pltpu.VMEM((SUBLANES, cw), F32), pltpu.VMEM((SUBLANES, cw), F32)]
        + 2 * _conv_shift_scratch(k, halo + tm, cw),
        compiler_params=_params(("arbitrary",)),
    )(dycat, dycat, c1, c1, proj, proj, proj, proj, proj, proj, w, ln_w, ln_b, dproj)


def _half_mask(half):
    lane = _iota((1, LANES), 1)
    return ((lane >= half * ATTN_HEAD_DIM) & (lane < (half + 1) * ATTN_HEAD_DIM)).astype(F32)


def _stack_heads(xp, g):
    m = _half_mask(g)
    swapped = pltpu.roll(xp, ATTN_HEAD_DIM, axis=1)
    return jnp.concatenate([xp * m, swapped * m] if g == 0 else [swapped * m, xp * m], axis=0)


def _unstack_heads(both, g):
    w = both.shape[0] // 2
    top, bot = both[:w], both[w:]
    lo, hi = _half_mask(0), _half_mask(1)
    if g == 0:
        return top * lo + pltpu.roll(bot, ATTN_HEAD_DIM, axis=1) * hi
    return pltpu.roll(top, ATTN_HEAD_DIM, axis=1) * lo + bot * hi


def _band_mask(first_block):
    w = WINDOW
    qi = _iota((w, 2 * w), 0)
    kj = _iota((w, 2 * w), 1) - w
    rel = qi - kj
    return (rel >= 0) & (rel < w) & (jnp.logical_not(first_block) | (kj >= 0))


def _lane_pick(x, h):
    return jnp.sum(jnp.where(_iota(x.shape, 1) == h, x, 0.0), axis=1, keepdims=True)


def _attn_specs(nb, rev):
    w = WINDOW

    def blk(i):
        return nb - 1 - i if rev else i

    def row(b, i):
        return b * nb + blk(i)

    def prow(b, i):
        return b * nb + jnp.maximum(blk(i) - 1, 0)

    q = pl.BlockSpec((w, 512), lambda b, i: (row(b, i), OFF_Q // 512))
    kc = pl.BlockSpec((w, 128), lambda b, i: (row(b, i), OFF_K // 128))
    kp = pl.BlockSpec((w, 128), lambda b, i: (prow(b, i), OFF_K // 128))
    vc = pl.BlockSpec((w, 128), lambda b, i: (row(b, i), OFF_V // 128))
    vp = pl.BlockSpec((w, 128), lambda b, i: (prow(b, i), OFF_V // 128))
    z = pl.BlockSpec((w, 512), lambda b, i: (row(b, i), OFF_ZA // 512))
    return q, kc, kp, vc, vp, z, row


def _attn_fwd(proj, sinks, ycat, nbatch, name):
    t = proj.shape[0]
    w = WINDOW
    nb = t // nbatch // w
    scale = ATTN_HEAD_DIM ** -0.5
    q_s, kc_s, kp_s, vc_s, vp_s, z_s, row = _attn_specs(nb, False)

    def body(q_ref, kc_ref, kp_ref, vc_ref, vp_ref, z_ref, sk_ref, _, y_ref, o_ref, lse_ref):
        first = pl.program_id(1) == 0
        mask = _band_mask(first)
        kk = jnp.concatenate([kp_ref[...], kc_ref[...]], axis=0).astype(MXU_DTYPE)
        vv = jnp.concatenate([vp_ref[...], vc_ref[...]], axis=0).astype(MXU_DTYPE)
        sk = sk_ref[...]
        lane = _iota((w, LANES), 1)
        mask2 = jnp.concatenate([mask, mask], axis=0)
        scores = [_dot(_stack_heads(q_ref[:, j * LANES:(j + 1) * LANES], j // 2), kk, NT) for j in range(4)]
        lse_all = jnp.zeros((w, LANES), F32)
        for j in range(4):
            s = jnp.where(mask2, scores[j] * scale, -1e30)
            skc = jnp.concatenate([jnp.broadcast_to(_lane_pick(sk, 2 * j), (w, 1)),
                                   jnp.broadcast_to(_lane_pick(sk, 2 * j + 1), (w, 1))], axis=0)
            m = jnp.maximum(jnp.max(s, axis=1, keepdims=True), skc)
            den = jnp.sum(jnp.exp(s - m), axis=1, keepdims=True) + jnp.exp(skc - m)
            lse = m + jnp.log(den)
            lse_all = jnp.where(lane == 2 * j, lse[:w], lse_all)
            lse_all = jnp.where(lane == 2 * j + 1, lse[w:], lse_all)
            op = _unstack_heads(_dot(jnp.exp(s - lse), vv), j // 2)
            cols = slice(j * LANES, (j + 1) * LANES)
            o_ref[:, cols] = op
            y_ref[:, cols] = (op * _silu(z_ref[:, cols])).astype(y_ref.dtype)
        lse_ref[...] = lse_all

    return pl.pallas_call(
        body, name=name,
        out_shape=(jax.ShapeDtypeStruct(ycat.shape, ycat.dtype), jax.ShapeDtypeStruct((t, 512), F32),
                   jax.ShapeDtypeStruct((t, LANES), F32)),
        grid=(nbatch, nb),
        in_specs=[q_s, kc_s, kp_s, vc_s, vp_s, z_s, pl.BlockSpec((1, LANES), lambda b, i: (0, 0)), ANY],
        out_specs=(pl.BlockSpec((w, 512), lambda b, i: (row(b, i), YCAT_ATTN // 512)),
                   pl.BlockSpec((w, 512), lambda b, i: (row(b, i), 0)),
                   pl.BlockSpec((w, LANES), lambda b, i: (row(b, i), 0))),
        input_output_aliases={7: 0},
        compiler_params=_params(("parallel", "parallel")),
    )(proj, proj, proj, proj, proj, proj, sinks, ycat)


def _attn_bwd(dycat, proj, o, lse, sinks, ddt, dproj, nbatch, name):
    t = proj.shape[0]
    w = WINDOW
    nb = t // nbatch // w
    scale = ATTN_HEAD_DIM ** -0.5
    q_s, kc_s, kp_s, vc_s, vp_s, z_s, row = _attn_specs(nb, True)

    def body(dy_ref, q_ref, kc_ref, kp_ref, vc_ref, vp_ref, z_ref, o_ref, lse_ref, sk_ref, ddt_ref, _,
             grp_ref, dsk_ref, kcarry, vcarry, sacc):
        b, i = pl.program_id(0), pl.program_id(1)

        @pl.when((b == 0) & (i == 0))
        def _():
            sacc[...] = jnp.zeros_like(sacc)

        @pl.when(i == 0)
        def _():
            kcarry[...] = jnp.zeros_like(kcarry)
            vcarry[...] = jnp.zeros_like(vcarry)

        first = i == nb - 1
        mask = _band_mask(first)
        kk = jnp.concatenate([kp_ref[...], kc_ref[...]], axis=0).astype(MXU_DTYPE)
        vv = jnp.concatenate([vp_ref[...], vc_ref[...]], axis=0).astype(MXU_DTYPE)
        sk = sk_ref[...]
        lse_all = lse_ref[...]
        lane1 = _iota((1, LANES), 1)
        mask2 = jnp.concatenate([mask, mask], axis=0)
        qs, dos, deltas, lses, scores, dps = [], [], [], [], [], []
        for j in range(4):
            cols = slice(j * LANES, (j + 1) * LANES)
            qp, zp, ov, dy = q_ref[:, cols], z_ref[:, cols], o_ref[:, cols], dy_ref[:, cols]
            grp_ref[:, OFF_ZA + j * LANES:OFF_ZA + (j + 1) * LANES] = (dy * ov * _dsilu(zp)).astype(grp_ref.dtype)
            do = dy * _silu(zp)
            q2 = _stack_heads(qp, j // 2).astype(MXU_DTYPE)
            do2 = _stack_heads(do, j // 2)
            qs.append(q2)
            dos.append(do2.astype(MXU_DTYPE))
            deltas.append(jnp.sum(do2 * _stack_heads(ov, j // 2), axis=1, keepdims=True))
            lses.append(jnp.concatenate([_lane_pick(lse_all, 2 * j), _lane_pick(lse_all, 2 * j + 1)], axis=0))
            scores.append(_dot(q2, kk, NT))
            dps.append(_dot(do2, vv, NT))
        prs, dss = [], []
        dsk = jnp.zeros((1, LANES), F32)
        for j in range(4):
            pr = jnp.exp(jnp.where(mask2, scores[j] * scale, -1e30) - lses[j])
            prs.append(pr.astype(MXU_DTYPE))
            dss.append((pr * (dps[j] - deltas[j])).astype(MXU_DTYPE))
            skc = jnp.concatenate([jnp.broadcast_to(_lane_pick(sk, 2 * j), (w, 1)),
                                   jnp.broadcast_to(_lane_pick(sk, 2 * j + 1), (w, 1))], axis=0)
            sink_term = jnp.exp(skc - lses[j]) * deltas[j]
            dsk = dsk - jnp.where(lane1 == 2 * j, jnp.sum(sink_term[:w]), 0.0)
            dsk = dsk - jnp.where(lane1 == 2 * j + 1, jnp.sum(sink_term[w:]), 0.0)
        dkk = jnp.zeros((2 * w, LANES), F32)
        dvv = jnp.zeros((2 * w, LANES), F32)
        for j in range(4):
            dq = _unstack_heads(_dot(dss[j], kk) * scale, j // 2)
            grp_ref[:, OFF_Q + j * LANES:OFF_Q + (j + 1) * LANES] = dq.astype(grp_ref.dtype)
            dkk = dkk + _dot(dss[j], qs[j], TN) * scale
            dvv = dvv + _dot(prs[j], dos[j], TN)
        grp_ref[:, OFF_K:OFF_K + LANES] = (dkk[w:, :] + kcarry[...]).astype(grp_ref.dtype)
        grp_ref[:, OFF_V:OFF_V + LANES] = (dvv[w:, :] + vcarry[...]).astype(grp_ref.dtype)
        grp_ref[:, OFF_DT:OFF_DT + LANES] = ddt_ref[...].astype(grp_ref.dtype)
        grp_ref[:, OFF_DT + LANES:] = jnp.zeros((w, ATTN_GROUP - OFF_DT - LANES), grp_ref.dtype)
        kcarry[...] = dkk[:w, :]
        vcarry[...] = dvv[:w, :]
        sacc[...] += dsk

        @pl.when((b == nbatch - 1) & (i == nb - 1))
        def _():
            dsk_ref[...] = sacc[...]

    return pl.pallas_call(
        body, name=name,
        out_shape=(jax.ShapeDtypeStruct(dproj.shape, dproj.dtype), jax.ShapeDtypeStruct((1, LANES), F32)),
        grid=(nbatch, nb),
        in_specs=[pl.BlockSpec((w, 512), lambda b, i: (row(b, i), YCAT_ATTN // 512)),
                  q_s, kc_s, kp_s, vc_s, vp_s, z_s,
                  pl.BlockSpec((w, 512), lambda b, i: (row(b, i), 0)),
                  pl.BlockSpec((w, LANES), lambda b, i: (row(b, i), 0)),
                  pl.BlockSpec((1, LANES), lambda b, i: (0, 0)),
                  pl.BlockSpec((w, LANES), lambda b, i: (row(b, i), 0)), ANY],
        out_specs=(pl.BlockSpec((w, ATTN_GROUP), lambda b, i: (row(b, i), 0)),
                   pl.BlockSpec((1, LANES), lambda b, i: (0, 0))),
        input_output_aliases={11: 0},
        scratch_shapes=[pltpu.VMEM((w, LANES), F32), pltpu.VMEM((w, LANES), F32),
                        pltpu.VMEM((1, LANES), F32)],
        compiler_params=_params(("arbitrary", "arbitrary")),
    )(dycat, proj, proj, proj, proj, proj, proj, o, lse, sinks, ddt, dproj)


SSD_WIDTH = SSD_HEADS * SSD_HEAD_DIM
GROUP_ROWS = SSD_WIDTH // 2


def _expand_mat():
    r, c = _iota((LANES, SSD_WIDTH), 0), _iota((LANES, SSD_WIDTH), 1)
    return (r == lax.shift_right_logical(c, 6)).astype(BF16)


def _expand_mat_t():
    r, c = _iota((SSD_WIDTH, LANES), 0), _iota((SSD_WIDTH, LANES), 1)
    return (c == lax.shift_right_logical(r, 6)).astype(BF16)


def _ssd_common(u_ref, dt_ref, dtb_ref, a_ref, stack_broadcasts=False):
    q = CHUNK
    act = _silu(u_ref[...])
    xs = act[:, :SSD_WIDTH]
    bm = act[:, SSD_WIDTH:SSD_WIDTH + 256]
    cm = act[:, SSD_WIDTH + 256:]
    dtp = _softplus(dt_ref[...] + dtb_ref[...])
    a = dtp * a_ref[...]
    tril = (_iota((q, q), 0) >= _iota((q, q), 1)).astype(BF16)
    acs = _xdot_r(tril, a)
    acs_t = acs.T
    e = _expand_mat()
    a_end = jnp.sum(jnp.where(_iota(acs.shape, 0) == q - 1, acs, 0.0), axis=0, keepdims=True)
    if stack_broadcasts:
        spread = _xdot(jnp.concatenate([dtp, acs, a_end - acs], axis=0), e)
        dt_x, ea, dec = spread[:q], jnp.exp(spread[q:2 * q]), jnp.exp(spread[2 * q:])
    else:
        dt_x = _xdot(dtp, e)
        ea = jnp.exp(_xdot(acs, e))
        dec = jnp.exp(_xdot(a_end - acs, e))
    a_end_col = jnp.broadcast_to(_lane_pick(acs_t, q - 1), (LANES, LANES))
    s_scale = jnp.exp(_xdot_r(_expand_mat_t(), a_end_col))
    return act, xs, bm, cm, dtp, acs, acs_t, dt_x, ea, dec, s_scale, tril


def _decay_mat(acs, acs_t, h):
    q = CHUNK
    col = _lane_pick(acs, h)
    rowv = jnp.sum(jnp.where(_iota(acs_t.shape, 0) == h, acs_t, 0.0), axis=0, keepdims=True)
    causal = _iota((q, q), 0) >= _iota((q, q), 1)
    return jnp.exp(jnp.where(causal, col - rowv, -1e30))


GN_WIDTH = 512


def _ssd_fwd(u, proj, dtb, a_neg, d_x, norm_w, ycat, nbatch, name):
    t = u.shape[0]
    q = CHUNK
    nc = t // nbatch // q

    def body(u_ref, dt_ref, z_ref, dtb_ref, a_ref, dx_ref, nw_ref, _, y_ref, st_ref, yn_ref, state):
        c = pl.program_id(1)

        @pl.when(c == 0)
        def _():
            state[...] = jnp.zeros_like(state)

        st_ref[...] = state[...]
        act, xs, bm, cm, dtp, acs, acs_t, dt_x, ea, dec, s_scale, _ = _ssd_common(u_ref, dt_ref, dtb_ref, a_ref)
        xdt = xs * dt_x
        xdec = xdt * dec
        lo, hi = _half_mask(0), _half_mask(1)
        grp = []
        for g in range(2):
            bg = bm[:, g * LANES:(g + 1) * LANES]
            cg = cm[:, g * LANES:(g + 1) * LANES]
            rows = slice(g * GROUP_ROWS, (g + 1) * GROUP_ROWS)
            sg = state[rows, :]
            grp.append((_dot(cg, bg, NT), _dot(cg, sg, NT), rows,
                        s_scale[rows, :] * sg + _dot(xdec[:, rows], bg, TN)))
        yps = []
        for pj in range(SSD_HEADS // 2):
            cb = grp[pj // 4][0]
            xp = xdt[:, pj * LANES:(pj + 1) * LANES]
            m2 = jnp.concatenate([cb * _decay_mat(acs, acs_t, 2 * pj), cb * _decay_mat(acs, acs_t, 2 * pj + 1)],
                                 axis=1)
            yps.append(_dot(m2, jnp.concatenate([xp * lo, xp * hi], axis=0)))
        for g in range(2):
            _, yoff, rows, state_new = grp[g]
            for j in range(4):
                pj = g * 4 + j
                cols = slice(pj * LANES, (pj + 1) * LANES)
                yp = yps[pj] + yoff[:, j * LANES:(j + 1) * LANES] * ea[:, cols]
                y_ref[:, cols] = yp + dx_ref[:, cols] * xs[:, cols]
            state[rows, :] = state_new
        for g in range(SSD_WIDTH // GN_WIDTH):
            cols = slice(g * GN_WIDTH, (g + 1) * GN_WIDTH)
            gg = y_ref[:, cols] * _silu(z_ref[:, cols])
            rstd = lax.rsqrt(jnp.mean(gg * gg, axis=-1, keepdims=True) + EPS)
            yn_ref[:, cols] = (gg * rstd * nw_ref[:, cols]).astype(yn_ref.dtype)

    vec = pl.BlockSpec((1, LANES), lambda b, c: (0, 0))
    wide = pl.BlockSpec((q, SSD_WIDTH), lambda b, c: (b * nc + c, 0))
    wvec = pl.BlockSpec((1, SSD_WIDTH), lambda b, c: (0, 0))
    return pl.pallas_call(
        body, name=name,
        out_shape=(jax.ShapeDtypeStruct((t, SSD_WIDTH), F32),
                   jax.ShapeDtypeStruct((nbatch * nc * SSD_WIDTH, SSD_STATE), F32),
                   jax.ShapeDtypeStruct(ycat.shape, ycat.dtype)),
        grid=(nbatch, nc),
        in_specs=[pl.BlockSpec((q, SSD_CONV_DIM), lambda b, c: (b * nc + c, 0)),
                  pl.BlockSpec((q, LANES), lambda b, c: (b * nc + c, OFF_DT // LANES)),
                  pl.BlockSpec((q, SSD_WIDTH), lambda b, c: (b * nc + c, OFF_ZS // SSD_WIDTH)),
                  vec, vec, wvec, wvec, ANY],
        out_specs=(wide, pl.BlockSpec((SSD_WIDTH, SSD_STATE), lambda b, c: (b * nc + c, 0)), wide),
        input_output_aliases={7: 2},
        scratch_shapes=[pltpu.VMEM((SSD_WIDTH, SSD_STATE), F32)],
        compiler_params=_params(("parallel", "arbitrary")),
    )(u, proj, proj, dtb, a_neg, d_x, norm_w, ycat)


def _ssd_bwd(dycat, u, proj, y, states, dtb, a_neg, d_x, norm_w, dproj, nbatch, name):
    t = u.shape[0]
    q = CHUNK
    nc = t // nbatch // q

    def body(do_ref, u_ref, dt_ref, z_ref, y_ref, st_ref, dtb_ref, a_ref, dx_ref, nw_ref, _,
             du_ref, dz_ref, ddt_ref, dal_ref, dd_ref, dtbg_ref, dnw_ref, dstate, acc_a, acc_d, acc_b, acc_w):
        b, c = pl.program_id(0), pl.program_id(1)

        @pl.when((b == 0) & (c == 0))
        def _():
            acc_a[...] = jnp.zeros_like(acc_a)
            acc_d[...] = jnp.zeros_like(acc_d)
            acc_b[...] = jnp.zeros_like(acc_b)
            acc_w[...] = jnp.zeros_like(acc_w)

        @pl.when(c == 0)
        def _():
            dstate[...] = jnp.zeros_like(dstate)

        dy_parts = []
        for g in range(SSD_WIDTH // GN_WIDTH):
            cols = slice(g * GN_WIDTH, (g + 1) * GN_WIDTH)
            yv, zv, dov = y_ref[:, cols], z_ref[:, cols], do_ref[:, cols]
            sz = _silu(zv)
            gg = yv * sz
            rstd = lax.rsqrt(jnp.mean(gg * gg, axis=-1, keepdims=True) + EPS)
            gh = gg * rstd
            acc_w[:, cols] += _rowsum8(dov * gh)
            dgn = dov * nw_ref[:, cols]
            dg = rstd * (dgn - gh * jnp.mean(dgn * gh, axis=-1, keepdims=True))
            dy_parts.append(dg * sz)
            dz_ref[:, cols] = (dg * yv * _dsilu(zv)).astype(dz_ref.dtype)

        act, xs, bm, cm, dtp, acs, acs_t, dt_x, ea, dec, s_scale, tril = _ssd_common(
            u_ref, dt_ref, dtb_ref, a_ref, stack_broadcasts=True)
        xdt = xs * dt_x
        xdec = xdt * dec
        dyv = jnp.concatenate(dy_parts, axis=1)
        dye = dyv * ea
        lo, hi = _half_mask(0), _half_mask(1)
        et = _expand_mat_t()
        grp = []
        for g in range(2):
            rows = slice(g * GROUP_ROWS, (g + 1) * GROUP_ROWS)
            bg = bm[:, g * LANES:(g + 1) * LANES]
            cg = cm[:, g * LANES:(g + 1) * LANES]
            sg = st_ref[rows, :]
            dsg = dstate[rows, :]
            grp.append(dict(
                rows=rows, bg=bg, cg=cg, dsg=dsg,
                cb=_dot(cg, bg, NT), yoff=_dot(cg, sg, NT), dxst=_dot(bg, dsg, NT) * dec[:, rows],
                dc_off=_dot(dye[:, rows], sg), db_off=_dot(xdec[:, rows], dsg),
                s_carried=s_scale[rows, :] * sg,
                dstate_new=_dot(dye[:, rows], cg, TN) + s_scale[rows, :] * dsg))
        dy2s, g2s, l2s = [], [], []
        for pj in range(SSD_HEADS // 2):
            cols = slice(pj * LANES, (pj + 1) * LANES)
            dyp = dyv[:, cols]
            dy2 = jnp.concatenate([dyp * lo, dyp * hi], axis=0).astype(MXU_DTYPE)
            dy2s.append(dy2)
            g2s.append(_dot(dy2, xdt[:, cols], NT))
            l2s.append(jnp.concatenate([_decay_mat(acs, acs_t, 2 * pj), _decay_mat(acs, acs_t, 2 * pj + 1)], axis=0))
        dal_diag = jnp.zeros((q, LANES), F32)
        lane2 = _iota((2 * q, LANES), 1)
        row2 = _iota((2 * q, LANES), 0)
        dxdt_parts, db_parts, dc_parts = [], [], []
        end_sum = jnp.zeros((LANES, LANES), F32)
        for g in range(2):
            gd = grp[g]
            cb2 = jnp.concatenate([gd["cb"], gd["cb"]], axis=0)
            dcb = jnp.zeros((q, q), F32)
            parts = []
            for j in range(4):
                pj = g * 4 + j
                gl = g2s[pj] * l2s[pj]
                dcb = dcb + gl[:q] + gl[q:]
                m2 = cb2 * l2s[pj]
                parts.append(_dot(m2, dy2s[pj], TN))
                w2 = (gl * cb2).astype(MXU_DTYPE)
                sel2 = (lane2 == 2 * pj + (row2 >= q).astype(jnp.int32)).astype(MXU_DTYPE)
                dal_diag = dal_diag + _dot(jnp.concatenate([w2[:q], w2[q:]], axis=1), sel2) - _dot(w2, sel2, TN)
            dxdt_parts.append(jnp.concatenate(parts, axis=1) + gd["dxst"])
            dc_parts.append(_dot(dcb, gd["bg"]) + gd["dc_off"])
            db_parts.append(_dot(dcb, gd["cg"], TN) + gd["db_off"])
            end_sum = end_sum + _xdot(gd["dsg"] * gd["s_carried"], et[gd["rows"], :], TN, passes=2)
            dstate[gd["rows"], :] = gd["dstate_new"]
        dxst_parts = [gd["dxst"] for gd in grp]
        yoff_parts = [gd["yoff"] for gd in grp]
        dxdt = jnp.concatenate(dxdt_parts, axis=1)
        dxv = dx_ref[...]
        yoff = jnp.concatenate(yoff_parts, axis=1) * ea
        per_head = _xdot(jnp.concatenate([dyv * yoff, xdt * jnp.concatenate(dxst_parts, axis=1),
                                          dxdt * xs, dyv * xs], axis=0), et)
        off_term, st_term, dx_term, d_term = (per_head[k * q:(k + 1) * q] for k in range(4))
        dalpha = dal_diag + off_term - st_term
        end_row = jnp.sum(end_sum, axis=0, keepdims=True) + jnp.sum(st_term, axis=0, keepdims=True)
        dalpha = dalpha + jnp.where(_iota((q, LANES), 0) == q - 1, end_row, 0.0)
        da = _xdot_r(tril, dalpha, TN)
        ddtp = da * a_ref[...] + dx_term
        acc_a[...] += _rowsum8(da * dtp)
        acc_d[...] += _rowsum8(d_term)
        ddt_raw = ddtp * _sigmoid(dt_ref[...] + dtb_ref[...])
        acc_b[...] += _rowsum8(ddt_raw)
        ddt_ref[...] = ddt_raw
        dxs = dxdt * dt_x + dxv * dyv
        dact = jnp.concatenate([dxs] + db_parts + dc_parts, axis=1)
        du_ref[...] = dact * _dsilu(u_ref[...])

        @pl.when((b == nbatch - 1) & (c == nc - 1))
        def _():
            dal_ref[...] = jnp.sum(acc_a[...], axis=0, keepdims=True) * a_ref[...]
            dd_ref[...] = jnp.sum(acc_d[...], axis=0, keepdims=True)
            dtbg_ref[...] = jnp.sum(acc_b[...], axis=0, keepdims=True)
            dnw_ref[...] = jnp.sum(acc_w[...], axis=0, keepdims=True)

    def rowblk(b, c):
        return b * nc + (nc - 1 - c)

    vec = pl.BlockSpec((1, LANES), lambda b, c: (0, 0))
    wvec = pl.BlockSpec((1, SSD_WIDTH), lambda b, c: (0, 0))
    wide = pl.BlockSpec((q, SSD_WIDTH), lambda b, c: (rowblk(b, c), 0))
    zblk = pl.BlockSpec((q, SSD_WIDTH), lambda b, c: (rowblk(b, c), OFF_ZS // SSD_WIDTH))
    return pl.pallas_call(
        body, name=name,
        out_shape=(jax.ShapeDtypeStruct((t, SSD_CONV_DIM), F32), jax.ShapeDtypeStruct(dproj.shape, dproj.dtype),
                   jax.ShapeDtypeStruct((t, LANES), F32),
                   jax.ShapeDtypeStruct((1, LANES), F32), jax.ShapeDtypeStruct((1, LANES), F32),
                   jax.ShapeDtypeStruct((1, LANES), F32), jax.ShapeDtypeStruct((1, SSD_WIDTH), F32)),
        grid=(nbatch, nc),
        in_specs=[wide,
                  pl.BlockSpec((q, SSD_CONV_DIM), lambda b, c: (rowblk(b, c), 0)),
                  pl.BlockSpec((q, LANES), lambda b, c: (rowblk(b, c), OFF_DT // LANES)),
                  zblk, wide,
                  pl.BlockSpec((SSD_WIDTH, SSD_STATE), lambda b, c: (rowblk(b, c), 0)),
                  vec, vec, wvec, wvec, ANY],
        out_specs=(pl.BlockSpec((q, SSD_CONV_DIM), lambda b, c: (rowblk(b, c), 0)),
                   zblk,
                   pl.BlockSpec((q, LANES), lambda b, c: (rowblk(b, c), 0)),
                   vec, vec, vec, wvec),
        input_output_aliases={10: 1},
        scratch_shapes=[pltpu.VMEM((SSD_WIDTH, SSD_STATE), F32), pltpu.VMEM((SUBLANES, LANES), F32),
                        pltpu.VMEM((SUBLANES, LANES), F32), pltpu.VMEM((SUBLANES, LANES), F32),
                        pltpu.VMEM((SUBLANES, SSD_WIDTH), F32)],
        compiler_params=_params(("arbitrary", "arbitrary")),
    )(dycat, u, proj, proj, y, states, dtb, a_neg, d_x, norm_w, dproj)


def _pad_rows(w, rows):
    return jnp.concatenate([w, jnp.zeros((rows - w.shape[0], w.shape[1]), w.dtype)], axis=0)


def _pad_lanes(v):
    return jnp.concatenate([v, jnp.zeros((LANES - v.shape[0],), v.dtype)]).reshape(1, LANES)


def _padded_from_chips(pieces):
    cols = pieces[0].shape[-1]
    lead = pieces[0].shape[:-1]
    parts, pos = [], 0
    for lo, hi, start in sorted(SECTIONS, key=lambda s: s[2]):
        if start > pos:
            parts.append(jnp.zeros(lead + (start - pos,), pieces[0].dtype))
        pos = start + hi - lo
        while lo < hi:
            p = lo // cols
            end = min(hi, (p + 1) * cols)
            parts.append(pieces[p][..., lo - p * cols:end - p * cols])
            lo = end
    if pos < NP:
        parts.append(jnp.zeros(lead + (NP - pos,), pieces[0].dtype))
    return jnp.concatenate(parts, axis=-1)


def _chip_part_from_padded(wp, p, cols):
    lo, hi = p * cols, (p + 1) * cols
    parts = []
    for rs, re, start in SECTIONS:
        a, b = max(lo, rs), min(hi, re)
        if a < b:
            parts.append(wp[..., start + a - rs:start + b - rs])
    return jnp.concatenate(parts, axis=-1)


def _layer_params(li, w_in_p, w_out, conv_w, dw_w, small):
    return dict(
        w_in_p=w_in_p, w_out=w_out,
        conv_w=_pad_rows(conv_w, SUBLANES), dw_w=_pad_rows(dw_w, 32),
        norm_w=small["norm_w"][li].reshape(1, -1),
        conv_b=small["ssd_conv_b"][li].reshape(1, -1),
        dtb=_pad_lanes(small["ssd_dt_bias"][li]),
        a_neg=_pad_lanes(-jnp.exp(small["ssd_a_log"][li])),
        d_x=jnp.repeat(small["ssd_d"][li], SSD_HEAD_DIM).reshape(1, -1),
        ssd_norm_w=small["ssd_norm_w"][li].reshape(1, -1),
        sinks=_pad_lanes(small["attn_sinks"][li]),
        dw_b=small["conf_dw_b"][li].reshape(1, -1),
        ln_w=small["conf_ln_w"][li].reshape(1, -1),
        ln_b=small["conf_ln_b"][li].reshape(1, -1),
    )


def _layer_fwd(x, p, nbatch, seq, tag, after=None):
    proj, h_t = _proj_fwd(x, p["norm_w"], p["w_in_p"], name=f"proj_fwd_{tag}", after=after)
    u = _conv_fwd(proj, OFF_XBC, SSD_CONV_DIM, p["conv_w"], p["conv_b"], SSD_CONV, seq, name=f"ssd_conv_fwd_{tag}")
    ycat = lax.empty((x.shape[0], MIX_WIDTH), MXU_DTYPE)
    y, states, ycat = _ssd_fwd(u, proj, p["dtb"], p["a_neg"], p["d_x"], p["ssd_norm_w"], ycat, nbatch,
                               name=f"ssd_fwd_{tag}")
    ycat, o, lse = _attn_fwd(proj, p["sinks"], ycat, nbatch, name=f"attn_fwd_{tag}")
    c1, ycat = _conf_fwd(proj, p["dw_w"], p["dw_b"], p["ln_w"], p["ln_b"], ycat, seq, name=f"conf_fwd_{tag}")
    w_out = p["w_out"](ycat) if callable(p["w_out"]) else p["w_out"]
    x_new = _matmul(ycat, w_out, "nn", F32, 1024, 512, 2048, name=f"out_fwd_{tag}", residual=x)
    return x_new, dict(x=x, w_out=w_out, h_t=h_t, proj=proj, u=u, y=y, states=states, o=o, lse=lse, c1=c1, ycat=ycat)


def _layer_bwd(dx_out, p, s, nbatch, seq, tag, hooks=None):
    hooks = hooks or {}
    proj = s["proj"]
    dycat = _matmul(dx_out, s["w_out"], "nt", F32, 1024, 1024, 1024, name=f"out_bwd_dy_{tag}",
                    after=hooks.get("start_token"))
    dw_out = _matmul(s["ycat"], dx_out, "tn", F32, 1024, 1024, 1024, name=f"out_bwd_dw_{tag}")
    token = hooks["after_dycat"](dycat) if "after_dycat" in hooks else None
    dtb = p["dtb"] if token is None else p["dtb"] + token[0, 0]
    dproj = lax.empty(proj.shape, MXU_DTYPE)
    du, dproj, ddt, da_log, dd, ddtb, dssd_norm_w = _ssd_bwd(
        dycat, s["u"], proj, s["y"], s["states"], dtb, p["a_neg"], p["d_x"], p["ssd_norm_w"], dproj,
        nbatch, name=f"ssd_bwd_{tag}")
    dproj, dconv_w, dconv_b = _conv_bwd(du, proj, OFF_XBC, SSD_CONV_DIM, p["conv_w"], SSD_CONV, seq,
                                        name=f"ssd_conv_bwd_{tag}", into=dproj)
    dproj, dsinks = _attn_bwd(dycat, proj, s["o"], s["lse"], p["sinks"], ddt, dproj, nbatch,
                              name=f"attn_bwd_{tag}")
    if "after_attn" in hooks:
        hooks["after_attn"](dproj)
    dproj, ddw_w, ddw_b, dln_w, dln_b = _conf_bwd(dycat, proj, s["c1"], p["dw_w"], p["ln_w"], p["ln_b"], dproj, seq,
                                                  name=f"conf_bwd_{tag}")
    dw_in_p = _matmul(s["h_t"], dproj, "nn", F32, 1024, 512, 4096, name=f"proj_bwd_dw_{tag}")
    token = hooks["after_dw"](dw_in_p, dw_out) if "after_dw" in hooks else None
    norm_w = p["norm_w"] if token is None else p["norm_w"] + token[0, 0]
    dx_in, dnorm_w = _proj_bwd_dx(dproj, p["w_in_p"], s["x"], norm_w, dx_out, name=f"proj_bwd_dx_{tag}")
    grads = dict(
        norm_w=dnorm_w[0], w_in_p=dw_in_p, ssd_conv_w=dconv_w[:SSD_CONV], ssd_conv_b=dconv_b[0],
        ssd_dt_bias=ddtb[0, :SSD_HEADS], ssd_a_log=da_log[0, :SSD_HEADS], ssd_d=dd[0, :SSD_HEADS],
        ssd_norm_w=dssd_norm_w[0], attn_sinks=dsinks[0, :ATTN_Q_HEADS], conf_dw_w=ddw_w[:CONF_KERNEL],
        conf_dw_b=ddw_b[0], conf_ln_w=dln_w[0], conf_ln_b=dln_b[0], w_out=dw_out)
    return dx_in, grads


def _local_step(x, target, param_fns, final_norm_w, first_after=None, bwd_hooks=None):
    nbatch, seq, d = x.shape
    xt = x.reshape(nbatch * seq, d)
    saved, layer_params = [], []
    for li, fn in enumerate(param_fns):
        p = fn(xt)
        layer_params.append(p)
        xt, s = _layer_fwd(xt, p, nbatch, seq, f"l{li}", after=first_after if li == 0 else None)
        saved.append(s)
    loss, dx, dfinal = _loss_head(xt, target.reshape(nbatch * seq, d), final_norm_w.reshape(1, d), name="loss_head")
    grads = [None] * len(layer_params)
    for li in reversed(range(len(layer_params))):
        hooks = bwd_hooks(li) if bwd_hooks is not None else None
        dx, grads[li] = _layer_bwd(dx, layer_params[li], saved[li], nbatch, seq, f"l{li}", hooks=hooks)
    return loss[0, 0], dx.reshape(nbatch, seq, d), grads, dfinal[0]


MESH = pl.DeviceIdType.MESH
N_CHIPS = 4


def _mesh_pos():
    return lax.axis_index("x"), lax.axis_index("y"), lax.axis_index("c")


def _other_chips(x, y):
    return [(1 - x, y), (x, 1 - y), (1 - x, 1 - y)]


def _gather_weights(big, small, name):
    nbig, nsmall = len(big), len(small)
    n_ici = 3 * (nbig + nsmall)
    n_fwd = 3 * nbig

    def body(*refs):
        ins = refs[:nbig + nsmall]
        outs = refs[nbig + nsmall:2 * (nbig + nsmall)]
        send_sems, recv_sems = refs[2 * (nbig + nsmall):]
        x, y, c = _mesh_pos()
        me = 2 * x + y
        sibling = (x, y, 1 - c)
        chips = _other_chips(x, y)

        def ici(a, j, origin, dest):
            if a < nbig:
                src = ins[a].at[c] if origin is None else outs[a].at[origin, c]
                dst = outs[a].at[me if origin is None else origin, c]
            else:
                src = ins[a] if origin is None else outs[a].at[origin]
                dst = outs[a].at[me if origin is None else origin]
            k = a * 3 + j
            return pltpu.make_async_remote_copy(src_ref=src, dst_ref=dst, send_sem=send_sems.at[k],
                                                recv_sem=recv_sems.at[k], device_id=dest, device_id_type=MESH)

        def fwd(a, j, origin, half):
            k = n_ici + a * 3 + j
            ref = outs[a].at[origin, half]
            return pltpu.make_async_remote_copy(src_ref=ref, dst_ref=ref, send_sem=send_sems.at[k],
                                                recv_sem=recv_sems.at[k], device_id=sibling, device_id_type=MESH)

        sends = []
        for j, (px, py) in enumerate(chips):
            for a in range(nbig + nsmall):
                cp = ici(a, j, None, (px, py, c))
                cp.start()
                sends.append(cp)
        for j, (px, py) in enumerate(chips):
            origin = 2 * px + py
            for a in range(nbig):
                ici(a, j, origin, (px, py, c)).wait_recv()
                cp = fwd(a, j, origin, c)
                cp.start()
                sends.append(cp)
        for j, (px, py) in enumerate(chips):
            origin = 2 * px + py
            for a in range(nbig, nbig + nsmall):
                ici(a, j, origin, (px, py, c)).wait_recv()
            for a in range(nbig):
                fwd(a, j, origin, 1 - c).wait_recv()
        for cp in sends:
            cp.wait_send()

    out_shape = tuple(jax.ShapeDtypeStruct((N_CHIPS,) + a.shape, a.dtype) for a in list(big) + list(small))
    return pl.pallas_call(
        body, name=name, out_shape=out_shape,
        in_specs=[ANY] * (nbig + nsmall), out_specs=tuple([ANY] * (nbig + nsmall)),
        scratch_shapes=[pltpu.SemaphoreType.DMA((n_ici + n_fwd,)), pltpu.SemaphoreType.DMA((n_ici + n_fwd,))],
    )(*big, *small)


HBM = pl.BlockSpec(memory_space=pltpu.HBM)
SEM = pl.BlockSpec(memory_space=pltpu.SEMAPHORE)
DATAFLOW = pltpu.SideEffectType.DATAFLOW_SIDE_EFFECTING


def _split_peers(pattern, x, y, c):
    if pattern == "swap":
        return [((x, y, 1 - c), 1 - c, None, None)]
    me = 2 * x + y
    return [((px, py, c), 2 * px + py if pattern == "scatter" else None, me, 2 * px + py)
            for px, py in _other_chips(x, y)]


def _split_land_shape(pattern, shape):
    return {"bcast": (N_CHIPS,) + shape, "scatter": shape, "swap": shape[:1] + shape[2:]}[pattern]


def _split_copies(pattern, srcs, lands, send_sems, recv_sems, waiting):
    x, y, c = _mesh_pos()
    peers = _split_peers(pattern, x, y, c)
    cps = []
    for j, (dev, src_slot, dst_slot, my_slot) in enumerate(peers):
        for a in range(len(srcs)):
            if src_slot is None:
                src = srcs[a]
            else:
                src = srcs[a].at[:, src_slot] if pattern == "swap" else srcs[a].at[src_slot]
            slot = my_slot if waiting else dst_slot
            dst = lands[a] if slot is None else lands[a].at[slot]
            k = a * len(peers) + j
            cps.append(pltpu.make_async_remote_copy(src_ref=src, dst_ref=dst, send_sem=send_sems[k],
                                                    recv_sem=recv_sems[k], device_id=dev, device_id_type=MESH))
    return cps


def _split_start(arrs, pattern, after, name):
    n = len(arrs)
    nsem = n * (1 if pattern == "swap" else N_CHIPS - 1)
    deps = [] if after is None else [after]

    def body(*refs):
        srcs, lands = refs[:n], refs[n:2 * n]
        outs = refs[2 * n + len(deps):]
        for cp in _split_copies(pattern, srcs, lands, outs[:nsem], outs[nsem:2 * nsem], waiting=False):
            cp.start()
        outs[-1][...] = jnp.zeros_like(outs[-1])

    lands = [lax.empty(_split_land_shape(pattern, a.shape), a.dtype) for a in arrs]
    out_shape = ([pltpu.SemaphoreType.DMA(())] * (2 * nsem)
                 + [pltpu.HBM(a.shape, a.dtype) for a in arrs] + [pltpu.HBM(b.shape, b.dtype) for b in lands]
                 + [jax.ShapeDtypeStruct((SUBLANES, LANES), F32)])
    outs = pl.pallas_call(
        body, name=name, out_shape=tuple(out_shape),
        in_specs=[HBM] * (2 * n) + [ANY] * len(deps),
        out_specs=tuple([SEM] * (2 * nsem) + [HBM] * (2 * n) + [pl.BlockSpec(memory_space=pltpu.VMEM)]),
        input_output_aliases={a: 2 * nsem + a for a in range(2 * n)},
        compiler_params=pltpu.CompilerParams(has_side_effects=DATAFLOW),
    )(*[pltpu.with_memory_space_constraint(a, pltpu.HBM) for a in list(arrs) + lands], *deps)
    return outs[:-1], outs[-1]


def _split_wait(state, n, pattern, after, name):
    nsem = n * (1 if pattern == "swap" else N_CHIPS - 1)

    def body(*refs):
        srcs, lands = refs[:n], refs[n:2 * n]
        send_sems, recv_sems = refs[2 * n:2 * n + nsem], refs[2 * n + nsem:2 * n + 2 * nsem]
        for cp in _split_copies(pattern, srcs, lands, send_sems, recv_sems, waiting=True):
            cp.wait_send()
            cp.wait_recv()

    sems, thru = state[:2 * nsem], state[2 * nsem:]
    outs = pl.pallas_call(
        body, name=name, out_shape=tuple(pltpu.HBM(a.shape, a.dtype) for a in thru),
        in_specs=[HBM] * (2 * n) + [SEM] * (2 * nsem) + [ANY],
        out_specs=tuple([HBM] * (2 * n)),
        input_output_aliases={a: a for a in range(2 * n)},
        compiler_params=pltpu.CompilerParams(has_side_effects=DATAFLOW),
    )(*thru, *sems, after)
    return outs[:n], outs[n:]


def _pair_gather(arrs, layer, name):
    n = len(arrs)

    def body(*refs):
        outs = refs[n:2 * n]
        send_sems, recv_sems = refs[2 * n:]
        x, y, c = _mesh_pos()
        cps = [pltpu.make_async_remote_copy(src_ref=outs[a].at[layer, c], dst_ref=outs[a].at[layer, c],
                                            send_sem=send_sems.at[a], recv_sem=recv_sems.at[a],
                                            device_id=(x, y, 1 - c), device_id_type=MESH)
               for a in range(n)]
        for cp in cps:
            cp.start()
        for cp in cps:
            cp.wait()

    return pl.pallas_call(
        body, name=name, out_shape=tuple(jax.ShapeDtypeStruct(a.shape, a.dtype) for a in arrs),
        in_specs=[ANY] * n, out_specs=tuple([ANY] * n),
        input_output_aliases={a: a for a in range(n)},
        scratch_shapes=[pltpu.SemaphoreType.DMA((n,)), pltpu.SemaphoreType.DMA((n,))],
    )(*arrs)


N_DEV = 8


def _allreduce_small(pack, name):
    r = pack.shape[0]

    def body(p_ref, o_ref, land, send_sems, recv_sems):
        x, y, c = _mesh_pos()
        me = 4 * x + 2 * y + c
        cps = []
        for k in range(1, N_DEV):
            peer = (x ^ (k >> 2), y ^ ((k >> 1) & 1), c ^ (k & 1))
            cps.append(pltpu.make_async_remote_copy(src_ref=p_ref, dst_ref=land.at[me], send_sem=send_sems.at[k - 1],
                                                    recv_sem=recv_sems.at[k - 1], device_id=peer, device_id_type=MESH))
        for cp in cps:
            cp.start()
        land[me] = p_ref[...]
        for cp in cps:
            cp.wait()
        total = land[0]
        for d in range(1, N_DEV):
            total = total + land[d]
        o_ref[...] = total

    vm = pl.BlockSpec(memory_space=pltpu.VMEM)
    return pl.pallas_call(
        body, name=name, out_shape=jax.ShapeDtypeStruct(pack.shape, F32),
        in_specs=[vm], out_specs=vm,
        scratch_shapes=[pltpu.VMEM((N_DEV, r, LANES), F32), pltpu.SemaphoreType.DMA((N_DEV - 1,)),
                        pltpu.SemaphoreType.DMA((N_DEV - 1,))],
    )(pack)


BIG_ROWS = 128


def _cast_layer(w, layer, name):
    _, r, cdim = w.shape
    tr = BIG_ROWS

    def body(w_ref, o_ref):
        o_ref[...] = w_ref[...].astype(o_ref.dtype)

    return pl.pallas_call(
        body, name=name, out_shape=jax.ShapeDtypeStruct((r, cdim), MXU_DTYPE),
        grid=(r // tr,), in_specs=[pl.BlockSpec((None, tr, cdim), lambda i: (layer, i, 0))],
        out_specs=pl.BlockSpec((tr, cdim), lambda i: (i, 0)),
        compiler_params=_params(("parallel",)),
    )(w)


def _cast_cols_major(w_t, name):
    cdim, nl, r = w_t.shape
    tc = LANES

    def body(w_ref, *o_refs):
        for l in range(nl):
            o_refs[l][...] = w_ref[:, l, :].T.astype(o_refs[l].dtype)

    out = pl.BlockSpec((r, tc), lambda i: (0, i))
    return pl.pallas_call(
        body, name=name, out_shape=tuple(jax.ShapeDtypeStruct((r, cdim), MXU_DTYPE) for _ in range(nl)),
        grid=(pl.cdiv(cdim, tc),), in_specs=[pl.BlockSpec((tc, nl, r), lambda i: (i, 0, 0))],
        out_specs=tuple([out] * nl),
        compiler_params=_params(("parallel",)),
    )(w_t)


def _pair_sum(parts, sib, which, out_dtype, name):
    k, _, r, cdim = parts.shape
    tr = BIG_ROWS

    def body(sel_ref, p_ref, s_ref, o_ref):
        o_ref[...] = (p_ref[...] + s_ref[...]).astype(o_ref.dtype)

    grid_spec = pltpu.PrefetchScalarGridSpec(
        num_scalar_prefetch=1, grid=(k, r // tr),
        in_specs=[pl.BlockSpec((None, None, tr, cdim), lambda l, i, sel: (l, sel[0], i, 0)),
                  pl.BlockSpec((None, tr, cdim), lambda l, i, sel: (l, i, 0))],
        out_specs=pl.BlockSpec((None, tr, cdim), lambda l, i, sel: (l, i, 0)))
    return pl.pallas_call(
        body, name=name, out_shape=jax.ShapeDtypeStruct((k, r, cdim), out_dtype), grid_spec=grid_spec,
        compiler_params=_params(("parallel", "parallel")),
    )(which.reshape(1).astype(jnp.int32), parts, sib)


def _sum_lead(parts, into, layer, which, name):
    k, r, cdim = parts.shape
    tr = BIG_ROWS

    def body(sel_ref, p_ref, _, o_ref):
        total = p_ref[0].astype(F32)
        for a in range(1, k):
            total = total + p_ref[a].astype(F32)
        o_ref[...] = total

    grid_spec = pltpu.PrefetchScalarGridSpec(
        num_scalar_prefetch=1, grid=(r // tr,),
        in_specs=[pl.BlockSpec((k, tr, cdim), lambda i, sel: (0, i, 0)), ANY],
        out_specs=pl.BlockSpec((None, None, tr, cdim), lambda i, sel: (layer, sel[0], i, 0)))
    return pl.pallas_call(
        body, name=name, out_shape=jax.ShapeDtypeStruct(into.shape, F32), grid_spec=grid_spec,
        input_output_aliases={2: 0},
        compiler_params=_params(("parallel",)),
    )(which.reshape(1).astype(jnp.int32), parts, into)


def _adam_math(w, g, m, v):
    m2 = ADAM_B1 * m + (1.0 - ADAM_B1) * g
    v2 = ADAM_B2 * v + (1.0 - ADAM_B2) * (g * g)
    m_hat = m2 / (1.0 - ADAM_B1 ** ADAM_STEP)
    v_hat = v2 / (1.0 - ADAM_B2 ** ADAM_STEP)
    delta = -ADAM_LR * (m_hat / (jnp.sqrt(v_hat) + ADAM_EPS) + ADAM_WD * w)
    return delta, m2, v2


def _adam_big(w, g, m, v, name):
    nl, r, cdim = w.shape
    tr = BIG_ROWS

    def body(w_ref, g_ref, m_ref, v_ref, d_ref, mo_ref, vo_ref):
        delta, m2, v2 = _adam_math(w_ref[...], g_ref[...], m_ref[...], v_ref[...])
        d_ref[...] = delta
        mo_ref[...] = m2
        vo_ref[...] = v2

    blk = pl.BlockSpec((None, tr, cdim), lambda l, i: (l, i, 0))
    shp = jax.ShapeDtypeStruct(w.shape, F32)
    return pl.pallas_call(
        body, name=name, out_shape=(shp, shp, shp),
        grid=(nl, r // tr), in_specs=[blk] * 4, out_specs=(blk, blk, blk),
        compiler_params=_params(("parallel", "parallel")),
    )(w, g, m, v)


def _adam_cols_major(w, g, m, v, name):
    cdim, nl, r = w.shape
    tc = BIG_ROWS

    def body(w_ref, g_ref, m_ref, v_ref, d_ref, mo_ref, vo_ref):
        delta, m2, v2 = _adam_math(w_ref[...], g_ref[...], m_ref[...], v_ref[...])
        d_ref[...] = delta
        mo_ref[...] = m2
        vo_ref[...] = v2

    blk = pl.BlockSpec((tc, nl, r), lambda i: (i, 0, 0))
    shp = jax.ShapeDtypeStruct(w.shape, F32)
    return pl.pallas_call(
        body, name=name, out_shape=(shp, shp, shp),
        grid=(pl.cdiv(cdim, tc),), in_specs=[blk] * 4, out_specs=(blk, blk, blk),
        compiler_params=_params(("parallel",)),
    )(w, g, m, v)


def _adam_small(ws, gs, ms, vs, name):
    n = len(ws)

    def body(*refs):
        w_refs, g_refs, m_refs, v_refs = (refs[k * n:(k + 1) * n] for k in range(4))
        d_refs, mo_refs, vo_refs = (refs[(4 + k) * n:(5 + k) * n] for k in range(3))
        for a in range(n):
            delta, m2, v2 = _adam_math(w_refs[a][...], g_refs[a][...], m_refs[a][...], v_refs[a][...])
            d_refs[a][...] = delta
            mo_refs[a][...] = m2
            vo_refs[a][...] = v2

    shapes = tuple(jax.ShapeDtypeStruct(w.shape, F32) for w in ws)
    vm = pl.BlockSpec(memory_space=pltpu.VMEM)
    outs = pl.pallas_call(body, name=name, out_shape=shapes * 3, in_specs=[vm] * (4 * n),
                          out_specs=tuple([vm] * (3 * n)))(*ws, *gs, *ms, *vs)
    return outs[:n], outs[n:2 * n], outs[2 * n:]


PACK_TILE = SUBLANES * LANES


def _pack(arrays):
    rows = []
    for a in arrays:
        flat = a.reshape(-1)
        pad = (-flat.shape[0]) % PACK_TILE
        if pad:
            flat = jnp.concatenate([flat, jnp.zeros((pad,), flat.dtype)])
        rows.append(flat.reshape(-1, LANES))
    return jnp.concatenate(rows, axis=0)


def _unpack(pack, shapes):
    outs, row = [], 0
    for shp in shapes:
        n = int(np.prod(shp))
        nrows = -(-n // PACK_TILE) * SUBLANES
        outs.append(pack[row:row + nrows].reshape(-1)[:n].reshape(shp))
        row += nrows
    return outs


SMALL = ["norm_w", "ssd_conv_b", "ssd_dt_bias", "ssd_a_log", "ssd_d", "ssd_norm_w", "attn_sinks",
         "conf_dw_b", "conf_ln_w", "conf_ln_b"]
WEIGHTS = ["norm_w", "w_in", "ssd_conv_w", "ssd_conv_b", "ssd_dt_bias", "ssd_a_log", "ssd_d", "ssd_norm_w",
           "attn_sinks", "conf_dw_w", "conf_dw_b", "conf_ln_w", "conf_ln_b", "w_out", "final_norm_w"]


def kernel(x, norm_w, w_in, ssd_conv_w, ssd_conv_b, ssd_dt_bias, ssd_a_log, ssd_d, ssd_norm_w, attn_sinks, conf_dw_w, conf_dw_b, conf_ln_w, conf_ln_b, w_out, final_norm_w, loss_target, m_norm_w, m_w_in, m_ssd_conv_w, m_ssd_conv_b, m_ssd_dt_bias, m_ssd_a_log, m_ssd_d, m_ssd_norm_w, m_attn_sinks, m_conf_dw_w, m_conf_dw_b, m_conf_ln_w, m_conf_ln_b, m_w_out, m_final_norm_w, v_norm_w, v_w_in, v_ssd_conv_w, v_ssd_conv_b, v_ssd_dt_bias, v_ssd_a_log, v_ssd_d, v_ssd_norm_w, v_attn_sinks, v_conf_dw_w, v_conf_dw_b, v_conf_ln_w, v_conf_ln_b, v_w_out, v_final_norm_w):
    w = dict(norm_w=norm_w, w_in=w_in, ssd_conv_w=ssd_conv_w, ssd_conv_b=ssd_conv_b, ssd_dt_bias=ssd_dt_bias,
             ssd_a_log=ssd_a_log, ssd_d=ssd_d, ssd_norm_w=ssd_norm_w, attn_sinks=attn_sinks, conf_dw_w=conf_dw_w,
             conf_dw_b=conf_dw_b, conf_ln_w=conf_ln_w, conf_ln_b=conf_ln_b, w_out=w_out, final_norm_w=final_norm_w)
    m = dict(norm_w=m_norm_w, w_in=m_w_in, ssd_conv_w=m_ssd_conv_w, ssd_conv_b=m_ssd_conv_b,
             ssd_dt_bias=m_ssd_dt_bias, ssd_a_log=m_ssd_a_log, ssd_d=m_ssd_d, ssd_norm_w=m_ssd_norm_w,
             attn_sinks=m_attn_sinks, conf_dw_w=m_conf_dw_w, conf_dw_b=m_conf_dw_b, conf_ln_w=m_conf_ln_w,
             conf_ln_b=m_conf_ln_b, w_out=m_w_out, final_norm_w=m_final_norm_w)
    v = dict(norm_w=v_norm_w, w_in=v_w_in, ssd_conv_w=v_ssd_conv_w, ssd_conv_b=v_ssd_conv_b,
             ssd_dt_bias=v_ssd_dt_bias, ssd_a_log=v_ssd_a_log, ssd_d=v_ssd_d, ssd_norm_w=v_ssd_norm_w,
             attn_sinks=v_attn_sinks, conf_dw_w=v_conf_dw_w, conf_dw_b=v_conf_dw_b, conf_ln_w=v_conf_ln_w,
             conf_ln_b=v_conf_ln_b, w_out=v_w_out, final_norm_w=v_final_norm_w)
    depth = w_in.shape[0]
    me = 2 * lax.axis_index("x") + lax.axis_index("y")

    assert depth == 2
    w_in_t = jnp.transpose(w_in, (2, 0, 1))
    w_in_b = _cast_cols_major(w_in_t, name="cast_w_in")
    w_out_b = [_cast_layer(w_out, li, name=f"cast_w_out_l{li}") for li in range(depth)]
    own0 = [w_in_b[0].reshape((2, -1) + w_in_b[0].shape[1:]), ssd_conv_w, conf_dw_w]
    gathered0 = _gather_weights(own0[:1], own0[1:], name="gather_weights_l0")
    g_in0, g_conv, g_dw = [lax.dynamic_update_index_in_dim(g_all, mine, me, 0)
                           for g_all, mine in zip(gathered0, own0)]
    own1 = [w_out_b[0], w_in_b[1], w_out_b[1]]
    pending1, token1 = _split_start(own1, "bcast", gathered0[0], name="gather_rest_start")
    rest = {}

    def small_full(li):
        return (jnp.concatenate([g_conv[p, li] for p in range(N_CHIPS)], axis=1),
                jnp.concatenate([g_dw[p, li] for p in range(N_CHIPS)], axis=1))

    def w_out_l0(after):
        mine1, landed = _split_wait(pending1, len(own1), "bcast", after, name="gather_rest_wait")
        rest["landed"] = [lax.dynamic_update_index_in_dim(g_all, mine, me, 0) for g_all, mine in zip(landed, mine1)]
        return rest["landed"][0].reshape(-1, w_out.shape[2])

    def params_l0(_):
        w_in_p = _padded_from_chips([g_in0[p].reshape(w_in_b[0].shape) for p in range(N_CHIPS)])
        return _layer_params(0, w_in_p, w_out_l0, *small_full(0), w)

    def params_l1(_):
        _, g_in1, g_out1 = rest["landed"]
        w_in_p = _padded_from_chips([g_in1[p] for p in range(N_CHIPS)])
        return _layer_params(1, w_in_p, g_out1.reshape(-1, g_out1.shape[-1]), *small_full(1), w)

    c = lax.axis_index("c")
    cols = w_in.shape[2]
    rows_out = w_out.shape[1]

    def grad_parts(g):
        dw = g["w_in_p"]
        return [dw.reshape(1, 2, dw.shape[0] // 2, dw.shape[1]),
                g["w_out"].reshape(N_CHIPS, 2, rows_out // 2, D_MODEL)]

    def pair_sums(parts, sib, tag):
        s_in, s_out = [_pair_sum(p, sb, c, MXU_DTYPE, name=f"grad_pair_sum_{k}_{tag}")
                       for k, (p, sb) in enumerate(zip(parts, sib))]
        return [jnp.stack([_chip_part_from_padded(s_in[0], p, cols) for p in range(N_CHIPS)]), s_out]

    split = {"reduced": [lax.empty((depth, 2, w_in.shape[1] // 2, cols), F32),
                         lax.empty((depth, 2, rows_out // 2, D_MODEL), F32)]}

    def chip_sums(landed, sent, li, which=(0, 1)):
        filled = [lax.dynamic_update_index_in_dim(r, lax.dynamic_index_in_dim(sk, me, 0, keepdims=False), me, 0)
                  for r, sk in zip(landed, sent)]
        tag = "".join(str(k) for k in which)
        halves = [_sum_lead(r, split["reduced"][k], li, c, name=f"grad_chip_sum_{k}_l{li}")
                  for k, r in zip(which, filled)]
        for k, buf in zip(which, _pair_gather(halves, li, name=f"grad_pair_gather_{tag}_l{li}")):
            split["reduced"][k] = buf

    def bwd_hooks(li):
        def after_dw(dw_in_p, dw_out):
            parts = grad_parts(dict(w_in_p=dw_in_p, w_out=dw_out))
            state, token = _split_start(parts, "swap", None, name=f"grad_swap_l{li}_start")
            if li > 0:
                split[f"swap{li}"] = (parts, state)
                return token
            mine, sib = _split_wait(state, len(parts), "swap", token, name="grad_swap_l0_wait")
            split["scatter0"], token = _split_start(pair_sums(mine, sib, "l0"), "scatter", None,
                                                    name="grad_scatter_l0_start")
            return token

        hooks = {"after_dw": after_dw}
        if li == depth - 2:
            parts, swap_state = split[f"swap{depth - 1}"]

            def after_dycat(dycat):
                mine, sib = _split_wait(swap_state, len(parts), "swap", dycat, name="grad_swap_l1_wait")
                sent = pair_sums(mine, sib, "l1")
                split["scatter"], token = _split_start(sent, "scatter", None, name="grad_scatter_l1_start")
                return token

            def after_attn(dproj):
                sent, landed = _split_wait(split["scatter"], len(parts), "scatter", dproj,
                                           name="grad_scatter_l1_wait")
                chip_sums(landed, sent, depth - 1)

            hooks.update(after_dycat=after_dycat, after_attn=after_attn)
        return hooks

    loss, grad_x, grads, dfinal = _local_step(x, loss_target, [params_l0, params_l1], final_norm_w,
                                              first_after=token1, bwd_hooks=bwd_hooks)

    small_list = [grads[li][n] for li in range(depth) for n in SMALL]
    small_list += [grads[li][n] for li in range(depth) for n in ("ssd_conv_w", "conf_dw_w")]
    small_list += [dfinal, loss.reshape(1)]
    small_shapes = [a.shape for a in small_list]
    reduced = _unpack(_allreduce_small(_pack(small_list), name="allreduce_small"), small_shapes)
    ns = len(SMALL)
    g = {n: jnp.stack([reduced[li * ns + i] for li in range(depth)]) for i, n in enumerate(SMALL)}
    conv_w_cols, dw_w_cols = ssd_conv_w.shape[2], conf_dw_w.shape[2]
    g["ssd_conv_w"] = jnp.stack([lax.dynamic_slice_in_dim(reduced[depth * ns + 2 * li], me * conv_w_cols,
                                                          conv_w_cols, axis=1) for li in range(depth)])
    g["conf_dw_w"] = jnp.stack([lax.dynamic_slice_in_dim(reduced[depth * ns + 2 * li + 1], me * dw_w_cols,
                                                         dw_w_cols, axis=1) for li in range(depth)])
    g["final_norm_w"] = reduced[-2]
    loss_total = reduced[-1][0]

    small_names = [n for n in WEIGHTS if n not in ("w_in", "w_out")]

    def as2d(a):
        return a.reshape(1, -1) if a.ndim == 1 else a

    deltas, new_ms, new_vs = _adam_small(*[[as2d(src[n]) for n in small_names] for src in (w, g, m, v)],
                                         name="adam_small")

    sent0, landed0 = _split_wait(split["scatter0"], 2, "scatter", deltas[0], name="grad_scatter_l0_wait")
    chip_sums(landed0, sent0, 0)
    g_w_in = split["reduced"][0].reshape(w_in.shape)
    g_w_out = split["reduced"][1].reshape(w_out.shape)
    outs_g, outs_d, outs_m, outs_v = {"w_in": g_w_in, "w_out": g_w_out}, {}, {}, {}
    outs_d["w_out"], outs_m["w_out"], outs_v["w_out"] = _adam_big(w_out, g_w_out, m_w_out, v_w_out,
                                                                  name="adam_w_out")
    to_cols, from_cols = (2, 0, 1), (1, 2, 0)
    outs_d["w_in"], outs_m["w_in"], outs_v["w_in"] = [
        jnp.transpose(a, from_cols) for a in _adam_cols_major(
            *[jnp.transpose(a, to_cols) for a in (w_in, g_w_in, m_w_in, v_w_in)], name="adam_w_in")]
    for n, dn, mn, vn in zip(small_names, deltas, new_ms, new_vs):
        outs_g[n], outs_d[n], outs_m[n], outs_v[n] = (g[n], dn.reshape(w[n].shape), mn.reshape(w[n].shape),
                                                      vn.reshape(w[n].shape))
    return (loss_total, grad_x, *[outs_g[n] for n in WEIGHTS], *[outs_d[n] for n in WEIGHTS],
            *[outs_m[n] for n in WEIGHTS], *[outs_v[n] for n in WEIGHTS])
```

```python
import functools
import math

import jax
import jax.numpy as jnp
import numpy as np
from jax import lax
from jax.experimental import pallas as pl
from jax.experimental.pallas import tpu as pltpu

F32 = jnp.float32
BF16 = jnp.bfloat16
MXU_DTYPE = BF16

D_MODEL = 1024
DEPTH = 2
SSD_HEADS = 16
SSD_HEAD_DIM = 64
SSD_STATE = 128
SSD_CONV = 4
CHUNK = 128
SSD_CONV_DIM = 1536
ATTN_HEAD_DIM = 64
ATTN_Q_HEADS = 8
WINDOW = 128
CONF_WIDTH = 512
CONF_KERNEL = 31
MIX_WIDTH = 2048
D_IN_PROJ = 5392
EPS = 1e-5

ADAM_LR = 0.001
ADAM_B1 = 0.9
ADAM_B2 = 0.999
ADAM_EPS = 1e-08
ADAM_WD = 0.01
ADAM_STEP = 10

LANES = 128
SUBLANES = 8
VMEM_LIMIT = 48 * 1024 * 1024

NP = 5632
OFF_ZA, OFF_Q, OFF_K, OFF_V, OFF_DT = 0, 512, 1024, 1152, 1280
ATTN_GROUP = 1536
OFF_CONF, OFF_ZC = 1536, 2560
CONF_GROUP = 1536
OFF_ZS = 3072
OFF_XBC = 4096
SECTIONS = ((0, 1024, OFF_ZS), (1024, 1536, OFF_ZA), (1536, 2048, OFF_ZC), (2048, 3584, OFF_XBC),
            (3584, 3600, OFF_DT), (3600, 4368, OFF_Q), (4368, 5392, OFF_CONF))

YCAT_ATTN, YCAT_CONF = 1024, 1536
ANY = pl.BlockSpec(memory_space=pl.ANY)

NN = (((1,), (0,)), ((), ()))
NT = (((1,), (1,)), ((), ()))
TN = (((0,), (0,)), ((), ()))


def _params(sem):
    return pltpu.CompilerParams(dimension_semantics=sem, vmem_limit_bytes=VMEM_LIMIT)


def _dot(a, b, dims=NN):
    return lax.dot_general(a.astype(MXU_DTYPE), b.astype(MXU_DTYPE), dims, preferred_element_type=F32)


def _split_bf16(a, passes):
    pieces = []
    r = a
    for _ in range(passes):
        p = r.astype(BF16)
        pieces.append(p)
        r = r - p.astype(F32)
    return pieces


def _xdot(a, sel, dims=NN, passes=2):
    out = None
    for p in _split_bf16(a, passes):
        t = lax.dot_general(p, sel, dims, preferred_element_type=F32)
        out = t if out is None else out + t
    return out


def _xdot_r(sel, b, dims=NN, passes=3):
    out = None
    for p in _split_bf16(b, passes):
        t = lax.dot_general(sel, p, dims, preferred_element_type=F32)
        out = t if out is None else out + t
    return out


def _sigmoid(x):
    return 1.0 / (1.0 + jnp.exp(-x))


def _silu(x):
    return x * _sigmoid(x)


def _dsilu(x):
    s = _sigmoid(x)
    return s * (1.0 + x * (1.0 - s))


def _softplus(x):
    return jnp.maximum(x, 0.0) + jnp.log(1.0 + jnp.exp(-jnp.abs(x)))


def _rowsum8(x):
    r, c = x.shape
    return jnp.sum(x.reshape(r // SUBLANES, SUBLANES, c), axis=0)


def _iota(shape, dim):
    return lax.broadcasted_iota(jnp.int32, shape, dim)


def _matmul(a, b, form, out_dtype, tm, tn, tk, name, residual=None, after=None):
    if form == "nn":
        (m, k), n = a.shape, b.shape[1]
    elif form == "nt":
        (m, k), n = a.shape, b.shape[0]
    else:
        (k, m), n = a.shape, b.shape[1]
    tm, tn, tk = min(tm, m), min(tn, n), min(tk, k)
    assert m % tm == 0 and n % tn == 0 and k % tk == 0, (name, m, n, k, tm, tn, tk)
    if form == "nn":
        a_spec = pl.BlockSpec((tm, tk), lambda i, j, s: (i, s))
        b_spec = pl.BlockSpec((tk, tn), lambda i, j, s: (s, j))
        dims = NN
    elif form == "nt":
        (m, k), n = a.shape, b.shape[0]
        a_spec = pl.BlockSpec((tm, tk), lambda i, j, s: (i, s))
        b_spec = pl.BlockSpec((tn, tk), lambda i, j, s: (j, s))
        dims = NT
    else:
        (k, m), n = a.shape, b.shape[1]
        a_spec = pl.BlockSpec((tk, tm), lambda i, j, s: (s, i))
        b_spec = pl.BlockSpec((tk, tn), lambda i, j, s: (s, j))
        dims = TN
    nk = k // tk
    has_res = residual is not None
    deps = [] if after is None else [after]

    def body_single(a_ref, b_ref, *rest):
        o = _dot(a_ref[...], b_ref[...], dims)
        if has_res:
            o = o + rest[0][...]
        rest[-1][...] = o.astype(out_dtype)

    def body(a_ref, b_ref, *rest):
        r_ref = rest[0] if has_res else None
        o_ref, acc = rest[-2:]
        s = pl.program_id(2)

        @pl.when(s == 0)
        def _():
            acc[...] = jnp.zeros_like(acc)

        acc[...] += _dot(a_ref[...], b_ref[...], dims)

        @pl.when(s == nk - 1)
        def _():
            o = acc[...]
            if has_res:
                o = o + r_ref[...]
            o_ref[...] = o.astype(out_dtype)

    in_specs = [a_spec, b_spec]
    args = [a, b]
    if has_res:
        in_specs.append(pl.BlockSpec((tm, tn), lambda i, j, s: (i, j)))
        args.append(residual)
    in_specs += [ANY] * len(deps)
    args += deps
    return pl.pallas_call(
        body_single if nk == 1 else body, name=name,
        out_shape=jax.ShapeDtypeStruct((m, n), out_dtype),
        grid=(m // tm, n // tn, nk),
        in_specs=in_specs,
        out_specs=pl.BlockSpec((tm, tn), lambda i, j, s: (i, j)),
        scratch_shapes=[] if nk == 1 else [pltpu.VMEM((tm, tn), F32)],
        compiler_params=_params(("parallel", "parallel", "arbitrary")),
    )(*args)


ROW_TILE = 256


PROJ_FWD_TM, PROJ_FWD_TN = 1024, 512


def _proj_fwd(x, w, w_in_p, name, after=None):
    t, d = x.shape
    n = w_in_p.shape[1]
    tm, tn = min(PROJ_FWD_TM, t), PROJ_FWD_TN
    assert t % tm == 0 and n % tn == 0
    deps = [] if after is None else [after]

    def body(x_ref, w_ref, b_ref, *rest):
        o_ref, ot_ref, h_scr = rest[len(deps):]

        @pl.when(pl.program_id(1) == 0)
        def _():
            xv = x_ref[...]
            rstd = lax.rsqrt(jnp.mean(xv * xv, axis=-1, keepdims=True) + EPS)
            h = xv * rstd * w_ref[...]
            h_scr[...] = h.astype(h_scr.dtype)
            ot_ref[...] = h.T.astype(ot_ref.dtype)

        o_ref[...] = _dot(h_scr[...], b_ref[...])

    return pl.pallas_call(
        body, name=name,
        out_shape=(jax.ShapeDtypeStruct((t, n), F32), jax.ShapeDtypeStruct((d, t), MXU_DTYPE)),
        grid=(t // tm, n // tn),
        in_specs=[pl.BlockSpec((tm, d), lambda i, j: (i, 0)), pl.BlockSpec((1, d), lambda i, j: (0, 0)),
                  pl.BlockSpec((d, tn), lambda i, j: (0, j))] + [ANY] * len(deps),
        out_specs=(pl.BlockSpec((tm, tn), lambda i, j: (i, j)), pl.BlockSpec((d, tm), lambda i, j: (0, i))),
        scratch_shapes=[pltpu.VMEM((tm, d), MXU_DTYPE)],
        compiler_params=_params(("parallel", "arbitrary")),
    )(x, w, w_in_p, *deps)


PROJ_BWD_TM, PROJ_BWD_TK = 1024, 1408


def _proj_bwd_dx(dproj, w_in_p, x, w, dres, name):
    t, d = x.shape
    kdim = dproj.shape[1]
    tm, tk = min(PROJ_BWD_TM, t), PROJ_BWD_TK
    nt, nk = t // tm, kdim // tk
    assert t % tm == 0 and kdim % tk == 0

    def body(a_ref, b_ref, x_ref, w_ref, dr_ref, dx_ref, dw_ref, acc, wacc):
        i, s = pl.program_id(0), pl.program_id(1)

        @pl.when((i == 0) & (s == 0))
        def _():
            wacc[...] = jnp.zeros_like(wacc)

        @pl.when(s == 0)
        def _():
            acc[...] = jnp.zeros_like(acc)

        acc[...] += _dot(a_ref[...], b_ref[...], NT)

        @pl.when(s == nk - 1)
        def _():
            xv = x_ref[...]
            rstd = lax.rsqrt(jnp.mean(xv * xv, axis=-1, keepdims=True) + EPS)
            xh = xv * rstd
            dhv = acc[...]
            g = dhv * w_ref[...]
            dx_ref[...] = dr_ref[...] + rstd * (g - xh * jnp.mean(g * xh, axis=-1, keepdims=True))
            wacc[...] += _rowsum8(dhv * xh)

        @pl.when((i == nt - 1) & (s == nk - 1))
        def _():
            dw_ref[...] = jnp.sum(wacc[...], axis=0, keepdims=True)

    row = pl.BlockSpec((tm, d), lambda i, s: (i, 0))
    vec = pl.BlockSpec((1, d), lambda i, s: (0, 0))
    return pl.pallas_call(
        body, name=name,
        out_shape=(jax.ShapeDtypeStruct((t, d), F32), jax.ShapeDtypeStruct((1, d), F32)),
        grid=(nt, nk),
        in_specs=[pl.BlockSpec((tm, tk), lambda i, s: (i, s)), pl.BlockSpec((d, tk), lambda i, s: (0, s)),
                  row, vec, row],
        out_specs=(row, vec),
        scratch_shapes=[pltpu.VMEM((tm, d), F32), pltpu.VMEM((SUBLANES, d), F32)],
        compiler_params=_params(("arbitrary", "arbitrary")),
    )(dproj, w_in_p, x, w, dres)


def _loss_head(xf, target, w, name):
    t, d = xf.shape
    tm = ROW_TILE
    nt = t // tm

    def body(x_ref, t_ref, w_ref, loss_ref, dx_ref, dw_ref, lacc, wacc):
        i = pl.program_id(0)

        @pl.when(i == 0)
        def _():
            lacc[...] = jnp.zeros_like(lacc)
            wacc[...] = jnp.zeros_like(wacc)

        xv = x_ref[...]
        rstd = lax.rsqrt(jnp.mean(xv * xv, axis=-1, keepdims=True) + EPS)
        xh = xv * rstd
        err = xh * w_ref[...] - t_ref[...]
        lacc[...] += jnp.sum(err * err)
        dy = err * (1.0 / d)
        g = dy * w_ref[...]
        dx_ref[...] = rstd * (g - xh * jnp.mean(g * xh, axis=-1, keepdims=True))
        wacc[...] += _rowsum8(dy * xh)

        @pl.when(i == nt - 1)
        def _():
            loss_ref[...] = lacc[...] * (0.5 / d)
            dw_ref[...] = jnp.sum(wacc[...], axis=0, keepdims=True)

    row = pl.BlockSpec((tm, d), lambda i: (i, 0))
    vec = pl.BlockSpec((1, d), lambda i: (0, 0))
    return pl.pallas_call(
        body, name=name,
        out_shape=(jax.ShapeDtypeStruct((SUBLANES, LANES), F32), jax.ShapeDtypeStruct((t, d), F32),
                   jax.ShapeDtypeStruct((1, d), F32)),
        grid=(nt,),
        in_specs=[row, row, vec],
        out_specs=(pl.BlockSpec((SUBLANES, LANES), lambda i: (0, 0)), row, vec),
        scratch_shapes=[pltpu.VMEM((SUBLANES, LANES), F32), pltpu.VMEM((SUBLANES, d), F32)],
        compiler_params=_params(("arbitrary",)),
    )(xf, target, w)


CONV_TILE = 512
CONV_TILE_SHORT = 1024
CONV_COLS = 512
CONV_SUB_ROWS = 128
CONV_SUB_COLS = LANES


def _conv_halo(k):
    return SUBLANES if k - 1 <= SUBLANES else 32


def _conv_tile(k, t):
    return min(CONV_TILE if _conv_use_shifted(k) else CONV_TILE_SHORT, t)


def _conv_subtiles(tm, cw):
    return [(r0, c0) for r0 in range(0, tm, CONV_SUB_ROWS) for c0 in range(0, cw, CONV_SUB_COLS)]


def _conv_use_shifted(k):
    return k > SUBLANES


def _conv_shift_scratch(k, rows, cw):
    return [pltpu.VMEM((SUBLANES - 1, rows - SUBLANES, cw), F32)] if _conv_use_shifted(k) else []


def _conv_fill_shifted(ext, sh):
    n = sh.shape[1]
    for b in range(1, SUBLANES):
        sh[b - 1] = ext[b:b + n, :]


def _conv_rows(ext, sh, start, rows, cs):
    b = start % SUBLANES
    if b == 0 or not sh:
        return ext[start:start + rows, cs]
    return sh[0][b - 1, start - b:start - b + rows, cs]


def _conv_fwd(src, col0, width, w, bias, k, seq, name):
    t = src.shape[0]
    tm, cw, halo = _conv_tile(k, src.shape[0]), CONV_COLS, _conv_halo(k)
    sr, sc = CONV_SUB_ROWS, CONV_SUB_COLS
    p = k - 1
    cb0 = col0 // cw
    kp = w.shape[0]

    shifted = _conv_use_shifted(k)

    def body(x_ref, h_ref, w_ref, b_ref, o_ref, ext, *sh):
        i = pl.program_id(0)
        seq_start = (i * tm) % seq == 0
        ext[halo:, :] = x_ref[...]
        ext[:halo, :] = jnp.where(seq_start, 0.0, h_ref[...])
        if shifted:
            _conv_fill_shifted(ext, sh[0])
        for r0, c0 in _conv_subtiles(tm, cw):
            cs = slice(c0, c0 + sc)
            acc = jnp.zeros((sr, sc), F32) + b_ref[:, cs]
            for j in range(k):
                acc = acc + w_ref[j:j + 1, cs] * _conv_rows(ext, sh, r0 + halo - p + j, sr, cs)
            o_ref[r0:r0 + sr, cs] = acc

    return pl.pallas_call(
        body, name=name,
        out_shape=jax.ShapeDtypeStruct((t, width), F32),
        grid=(t // tm, width // cw),
        in_specs=[pl.BlockSpec((tm, cw), lambda i, j: (i, cb0 + j)),
                  pl.BlockSpec((halo, cw), lambda i, j: (jnp.maximum(i * (tm // halo) - 1, 0), cb0 + j)),
                  pl.BlockSpec((kp, cw), lambda i, j: (0, j)),
                  pl.BlockSpec((1, cw), lambda i, j: (0, j))],
        out_specs=pl.BlockSpec((tm, cw), lambda i, j: (i, j)),
        scratch_shapes=[pltpu.VMEM((halo + tm, cw), F32)] + _conv_shift_scratch(k, halo + tm, cw),
        compiler_params=_params(("parallel", "parallel")),
    )(src, src, w, bias)


def _conv_bwd(dy, src, col0, width, w, k, seq, name, into=None):
    t = src.shape[0]
    tm, cw, halo = _conv_tile(k, src.shape[0]), CONV_COLS, _conv_halo(k)
    sr, sc = CONV_SUB_ROWS, CONV_SUB_COLS
    p = k - 1
    cb0 = col0 // cw
    kp = w.shape[0]
    nt = t // tm
    last_halo = t // halo - 1

    shifted = _conv_use_shifted(k)

    def body(dy_ref, dn_ref, x_ref, xp_ref, w_ref, *rest):
        if into is not None:
            rest = rest[1:]
        dx_ref, dw_ref, db_ref, dyext, xext, wacc, bacc = rest[:7]
        sh = rest[7:]
        i = pl.program_id(1)
        dysh, xsh = (sh[:1], sh[1:]) if shifted else ((), ())

        @pl.when(i == 0)
        def _():
            wacc[...] = jnp.zeros_like(wacc)
            bacc[...] = jnp.zeros_like(bacc)

        seq_start = (i * tm) % seq == 0
        seq_end = ((i + 1) * tm) % seq == 0
        dyext[:tm, :] = dy_ref[...]
        dyext[tm:, :] = jnp.where(seq_end, 0.0, dn_ref[...])
        xext[halo:, :] = x_ref[...]
        xext[:halo, :] = jnp.where(seq_start, 0.0, xp_ref[...])
        if shifted:
            _conv_fill_shifted(dyext, dysh[0])
            _conv_fill_shifted(xext, xsh[0])
        for r0, c0 in _conv_subtiles(tm, cw):
            cs = slice(c0, c0 + sc)
            dyv = dy_ref[r0:r0 + sr, cs]
            acc = jnp.zeros((sr, sc), F32)
            for j in range(k):
                acc = acc + w_ref[j:j + 1, cs] * _conv_rows(dyext, dysh, r0 + p - j, sr, cs)
                wacc[j, :, cs] += _rowsum8(dyv * _conv_rows(xext, xsh, r0 + halo - p + j, sr, cs))
            dx_ref[r0:r0 + sr, cs] = acc.astype(dx_ref.dtype)
            bacc[:, cs] += _rowsum8(dyv)

        @pl.when(i == nt - 1)
        def _():
            dw_ref[...] = jnp.zeros_like(dw_ref)
            for j in range(k):
                dw_ref[j:j + 1, :] = jnp.sum(wacc[j], axis=0, keepdims=True)
            db_ref[...] = jnp.sum(bacc[...], axis=0, keepdims=True)

    if into is None:
        dx_shape = jax.ShapeDtypeStruct((t, width), F32)
        dx_spec = pl.BlockSpec((tm, cw), lambda j, i: (i, j))
        extra_specs, extra_args, aliases = [], [], {}
    else:
        dx_shape = jax.ShapeDtypeStruct(into.shape, into.dtype)
        dx_spec = pl.BlockSpec((tm, cw), lambda j, i: (i, cb0 + j))
        extra_specs, extra_args, aliases = [ANY], [into], {5: 0}
    return pl.pallas_call(
        body, name=name,
        out_shape=(dx_shape, jax.ShapeDtypeStruct((kp, width), F32), jax.ShapeDtypeStruct((1, width), F32)),
        grid=(width // cw, nt),
        in_specs=[pl.BlockSpec((tm, cw), lambda j, i: (i, j)),
                  pl.BlockSpec((halo, cw), lambda j, i: (jnp.minimum((i + 1) * (tm // halo), last_halo), j)),
                  pl.BlockSpec((tm, cw), lambda j, i: (i, cb0 + j)),
                  pl.BlockSpec((halo, cw), lambda j, i: (jnp.maximum(i * (tm // halo) - 1, 0), cb0 + j)),
                  pl.BlockSpec((kp, cw), lambda j, i: (0, j))] + extra_specs,
        out_specs=(dx_spec,
                   pl.BlockSpec((kp, cw), lambda j, i: (0, j)),
                   pl.BlockSpec((1, cw), lambda j, i: (0, j))),
        input_output_aliases=aliases,
        scratch_shapes=[pltpu.VMEM((tm + halo, cw), F32), pltpu.VMEM((halo + tm, cw), F32),
                        pltpu.VMEM((kp, SUBLANES, cw), F32), pltpu.VMEM((SUBLANES, cw), F32)]
        + 2 * _conv_shift_scratch(k, halo + tm, cw),
        compiler_params=_params(("parallel", "arbitrary")),
    )(dy, dy, src, src, w, *extra_args)


def _conf_specs(tm, cw, halo, order):
    cb = OFF_CONF // cw

    def blk(col):
        return pl.BlockSpec((tm, cw), lambda *g: (order(*g), col))

    def prev(col):
        return pl.BlockSpec((halo, cw), lambda *g: (jnp.maximum(order(*g) * (tm // halo) - 1, 0), col))

    return blk(cb), prev(cb), blk(cb + 1), prev(cb + 1)


def _glu_window(ext, a_ref, ah_ref, g_ref, gh_ref, seq_start, halo):
    ext[halo:, :] = a_ref[...] * _sigmoid(g_ref[...])
    ext[:halo, :] = jnp.where(seq_start, 0.0, ah_ref[...] * _sigmoid(gh_ref[...]))


def _conf_fwd(proj, w, bias, ln_w, ln_b, ycat, seq, name):
    t = proj.shape[0]
    k = CONF_KERNEL
    tm, cw, halo = CONV_TILE, CONF_WIDTH, _conv_halo(k)
    sr, sc = CONV_SUB_ROWS, CONV_SUB_COLS
    p = k - 1
    kp = w.shape[0]

    def body(a_ref, ah_ref, g_ref, gh_ref, z_ref, w_ref, b_ref, lw_ref, lb_ref, _, c1_ref, y_ref, ext, sh):
        i = pl.program_id(0)
        _glu_window(ext, a_ref, ah_ref, g_ref, gh_ref, (i * tm) % seq == 0, halo)
        _conv_fill_shifted(ext, sh)
        for r0, c0 in _conv_subtiles(tm, cw):
            cs = slice(c0, c0 + sc)
            acc = jnp.zeros((sr, sc), F32) + b_ref[:, cs]
            for j in range(k):
                acc = acc + w_ref[j:j + 1, cs] * _conv_rows(ext, (sh,), r0 + halo - p + j, sr, cs)
            c1_ref[r0:r0 + sr, cs] = acc
        for r0 in range(0, tm, sr):
            rows = slice(r0, r0 + sr)
            cv = c1_ref[rows, :]
            xc = cv - jnp.mean(cv, axis=-1, keepdims=True)
            rstd = lax.rsqrt(jnp.mean(xc * xc, axis=-1, keepdims=True) + EPS)
            c2 = xc * rstd * lw_ref[...] + lb_ref[...]
            y_ref[rows, :] = (_silu(c2) * _silu(z_ref[rows, :])).astype(y_ref.dtype)

    vec = pl.BlockSpec((1, cw), lambda i: (0, 0))
    row = pl.BlockSpec((tm, cw), lambda i: (i, 0))
    return pl.pallas_call(
        body, name=name,
        out_shape=(jax.ShapeDtypeStruct((t, cw), F32), jax.ShapeDtypeStruct(ycat.shape, ycat.dtype)),
        grid=(t // tm,),
        in_specs=[*_conf_specs(tm, cw, halo, lambda i: i),
                  pl.BlockSpec((tm, cw), lambda i: (i, OFF_ZC // cw)),
                  pl.BlockSpec((kp, cw), lambda i: (0, 0)), vec, vec, vec, ANY],
        out_specs=(row, pl.BlockSpec((tm, cw), lambda i: (i, YCAT_CONF // cw))),
        input_output_aliases={9: 1},
        scratch_shapes=[pltpu.VMEM((halo + tm, cw), F32)] + _conv_shift_scratch(k, halo + tm, cw),
        compiler_params=_params(("parallel",)),
    )(proj, proj, proj, proj, proj, w, bias, ln_w, ln_b, ycat)


def _conf_bwd(dycat, proj, c1, w, ln_w, ln_b, dproj, seq, name):
    t = proj.shape[0]
    k = CONF_KERNEL
    tm, cw, halo = CONV_TILE, CONF_WIDTH, _conv_halo(k)
    sr, sc = CONV_SUB_ROWS, CONV_SUB_COLS
    p = k - 1
    kp = w.shape[0]
    nt = t // tm
    last_halo = t // halo - 1

    def body(dy_ref, dyn_ref, c_ref, cn_ref, z_ref, zn_ref, a_ref, ah_ref, g_ref, gh_ref, w_ref, lw_ref, lb_ref, _,
             grp_ref, dw_ref, db_ref, dlw_ref, dlb_ref, dyext, xext, wacc, bacc, lwacc, lbacc, dysh, xsh):
        i = pl.program_id(0)

        @pl.when(i == 0)
        def _():
            wacc[...] = jnp.zeros_like(wacc)
            bacc[...] = jnp.zeros_like(bacc)
            lwacc[...] = jnp.zeros_like(lwacc)
            lbacc[...] = jnp.zeros_like(lbacc)

        def post_bwd(dy, cv, zv):
            xc = cv - jnp.mean(cv, axis=-1, keepdims=True)
            rstd = lax.rsqrt(jnp.mean(xc * xc, axis=-1, keepdims=True) + EPS)
            xh = xc * rstd
            c2 = xh * lw_ref[...] + lb_ref[...]
            dz = dy * _silu(c2) * _dsilu(zv)
            dc2 = dy * _silu(zv) * _dsilu(c2)
            dxh = dc2 * lw_ref[...]
            dc = rstd * (dxh - jnp.mean(dxh, axis=-1, keepdims=True)
                         - xh * jnp.mean(dxh * xh, axis=-1, keepdims=True))
            return dc, dz, dc2 * xh, dc2

        seq_end = ((i + 1) * tm) % seq == 0
        for r0 in range(0, tm, sr):
            rows = slice(r0, r0 + sr)
            dc, dz, lw_terms, lb_terms = post_bwd(dy_ref[rows, :], c_ref[rows, :], z_ref[rows, :])
            dyext[rows, :] = dc
            grp_ref[rows, 2 * cw:] = dz.astype(grp_ref.dtype)
            lwacc[...] += _rowsum8(lw_terms)
            lbacc[...] += _rowsum8(lb_terms)
        dc_next = post_bwd(dyn_ref[...], cn_ref[...], zn_ref[...])[0]
        dyext[tm:, :] = jnp.where(seq_end, 0.0, dc_next)
        _glu_window(xext, a_ref, ah_ref, g_ref, gh_ref, (i * tm) % seq == 0, halo)
        _conv_fill_shifted(dyext, dysh)
        _conv_fill_shifted(xext, xsh)
        dag_ref = grp_ref
        for r0, c0 in _conv_subtiles(tm, cw):
            cs = slice(c0, c0 + sc)
            rows = slice(r0, r0 + sr)
            dyv = dyext[rows, cs]
            acc = jnp.zeros((sr, sc), F32)
            for j in range(k):
                acc = acc + w_ref[j:j + 1, cs] * _conv_rows(dyext, (dysh,), r0 + p - j, sr, cs)
                wacc[j, :, cs] += _rowsum8(dyv * _conv_rows(xext, (xsh,), r0 + halo - p + j, sr, cs))
            bacc[:, cs] += _rowsum8(dyv)
            s = _sigmoid(g_ref[rows, cs])
            dag_ref[rows, cs] = (acc * s).astype(dag_ref.dtype)
            dag_ref[rows, cw + c0:cw + c0 + sc] = (acc * a_ref[rows, cs] * s * (1.0 - s)).astype(dag_ref.dtype)

        @pl.when(i == nt - 1)
        def _():
            dw_ref[...] = jnp.zeros_like(dw_ref)
            for j in range(k):
                dw_ref[j:j + 1, :] = jnp.sum(wacc[j], axis=0, keepdims=True)
            db_ref[...] = jnp.sum(bacc[...], axis=0, keepdims=True)
            dlw_ref[...] = jnp.sum(lwacc[...], axis=0, keepdims=True)
            dlb_ref[...] = jnp.sum(lbacc[...], axis=0, keepdims=True)

    def blk(col):
        return pl.BlockSpec((tm, cw), lambda i: (i, col))

    def nxt(col):
        return pl.BlockSpec((halo, cw), lambda i: (jnp.minimum((i + 1) * (tm // halo), last_halo), col))

    vec = pl.BlockSpec((1, cw), lambda i: (0, 0))
    return pl.pallas_call(
        body, name=name,
        out_shape=(jax.ShapeDtypeStruct(dproj.shape, dproj.dtype), jax.ShapeDtypeStruct((kp, cw), F32),
                   jax.ShapeDtypeStruct((1, cw), F32), jax.ShapeDtypeStruct((1, cw), F32),
                   jax.ShapeDtypeStruct((1, cw), F32)),
        grid=(nt,),
        in_specs=[blk(YCAT_CONF // cw), nxt(YCAT_CONF // cw), blk(0), nxt(0), blk(OFF_ZC // cw), nxt(OFF_ZC // cw),
                  *_conf_specs(tm, cw, halo, lambda i: i),
                  pl.BlockSpec((kp, cw), lambda i: (0, 0)), vec, vec, ANY],
        out_specs=(pl.BlockSpec((tm, CONF_GROUP), lambda i: (i, OFF_CONF // CONF_GROUP)),
                   pl.BlockSpec((kp, cw), lambda i: (0, 0)), vec, vec, vec),
        input_output_aliases={13: 0},
        scratch_shapes=[pltpu.VMEM((tm + halo, cw), F32), pltpu.VMEM((halo + tm, cw), F32),
                        pltpu.VMEM((kp, SUBLANES, cw), F32), pltpu.VMEM((SUBLANES, cw), F32),
                        pltpu.VMEM((SUBLANES, cw), F32), pltpu.VMEM((SUBLANES, cw), F32)]
        + 2 * _conv_shift_scratch(k, halo + tm, cw),
        compiler_params=_params(("arbitrary",)),
    )(dycat, dycat, c1, c1, proj, proj, proj, proj, proj, proj, w, ln_w, ln_b, dproj)


def _half_mask(half):
    lane = _iota((1, LANES), 1)
    return ((lane >= half * ATTN_HEAD_DIM) & (lane < (half + 1) * ATTN_HEAD_DIM)).astype(F32)


def _stack_heads(xp, g):
    m = _half_mask(g)
    swapped = pltpu.roll(xp, ATTN_HEAD_DIM, axis=1)
    return jnp.concatenate([xp * m, swapped * m] if g == 0 else [swapped * m, xp * m], axis=0)


def _unstack_heads(both, g):
    w = both.shape[0] // 2
    top, bot = both[:w], both[w:]
    lo, hi = _half_mask(0), _half_mask(1)
    if g == 0:
        return top * lo + pltpu.roll(bot, ATTN_HEAD_DIM, axis=1) * hi
    return pltpu.roll(top, ATTN_HEAD_DIM, axis=1) * lo + bot * hi


def _band_mask(first_block):
    w = WINDOW
    qi = _iota((w, 2 * w), 0)
    kj = _iota((w, 2 * w), 1) - w
    rel = qi - kj
    return (rel >= 0) & (rel < w) & (jnp.logical_not(first_block) | (kj >= 0))


def _lane_pick(x, h):
    return jnp.sum(jnp.where(_iota(x.shape, 1) == h, x, 0.0), axis=1, keepdims=True)


def _attn_specs(nb, rev):
    w = WINDOW

    def blk(i):
        return nb - 1 - i if rev else i

    def row(b, i):
        return b * nb + blk(i)

    def prow(b, i):
        return b * nb + jnp.maximum(blk(i) - 1, 0)

    q = pl.BlockSpec((w, 512), lambda b, i: (row(b, i), OFF_Q // 512))
    kc = pl.BlockSpec((w, 128), lambda b, i: (row(b, i), OFF_K // 128))
    kp = pl.BlockSpec((w, 128), lambda b, i: (prow(b, i), OFF_K // 128))
    vc = pl.BlockSpec((w, 128), lambda b, i: (row(b, i), OFF_V // 128))
    vp = pl.BlockSpec((w, 128), lambda b, i: (prow(b, i), OFF_V // 128))
    z = pl.BlockSpec((w, 512), lambda b, i: (row(b, i), OFF_ZA // 512))
    return q, kc, kp, vc, vp, z, row


def _attn_fwd(proj, sinks, ycat, nbatch, name):
    t = proj.shape[0]
    w = WINDOW
    nb = t // nbatch // w
    scale = ATTN_HEAD_DIM ** -0.5
    q_s, kc_s, kp_s, vc_s, vp_s, z_s, row = _attn_specs(nb, False)

    def body(q_ref, kc_ref, kp_ref, vc_ref, vp_ref, z_ref, sk_ref, _, y_ref, o_ref, lse_ref):
        first = pl.program_id(1) == 0
        mask = _band_mask(first)
        kk = jnp.concatenate([kp_ref[...], kc_ref[...]], axis=0).astype(MXU_DTYPE)
        vv = jnp.concatenate([vp_ref[...], vc_ref[...]], axis=0).astype(MXU_DTYPE)
        sk = sk_ref[...]
        lane = _iota((w, LANES), 1)
        mask2 = jnp.concatenate([mask, mask], axis=0)
        scores = [_dot(_stack_heads(q_ref[:, j * LANES:(j + 1) * LANES], j // 2), kk, NT) for j in range(4)]
        lse_all = jnp.zeros((w, LANES), F32)
        for j in range(4):
            s = jnp.where(mask2, scores[j] * scale, -1e30)
            skc = jnp.concatenate([jnp.broadcast_to(_lane_pick(sk, 2 * j), (w, 1)),
                                   jnp.broadcast_to(_lane_pick(sk, 2 * j + 1), (w, 1))], axis=0)
            m = jnp.maximum(jnp.max(s, axis=1, keepdims=True), skc)
            den = jnp.sum(jnp.exp(s - m), axis=1, keepdims=True) + jnp.exp(skc - m)
            lse = m + jnp.log(den)
            lse_all = jnp.where(lane == 2 * j, lse[:w], lse_all)
            lse_all = jnp.where(lane == 2 * j + 1, lse[w:], lse_all)
            op = _unstack_heads(_dot(jnp.exp(s - lse), vv), j // 2)
            cols = slice(j * LANES, (j + 1) * LANES)
            o_ref[:, cols] = op
            y_ref[:, cols] = (op * _silu(z_ref[:, cols])).astype(y_ref.dtype)
        lse_ref[...] = lse_all

    return pl.pallas_call(
        body, name=name,
        out_shape=(jax.ShapeDtypeStruct(ycat.shape, ycat.dtype), jax.ShapeDtypeStruct((t, 512), F32),
                   jax.ShapeDtypeStruct((t, LANES), F32)),
        grid=(nbatch, nb),
        in_specs=[q_s, kc_s, kp_s, vc_s, vp_s, z_s, pl.BlockSpec((1, LANES), lambda b, i: (0, 0)), ANY],
        out_specs=(pl.BlockSpec((w, 512), lambda b, i: (row(b, i), YCAT_ATTN // 512)),
                   pl.BlockSpec((w, 512), lambda b, i: (row(b, i), 0)),
                   pl.BlockSpec((w, LANES), lambda b, i: (row(b, i), 0))),
        input_output_aliases={7: 0},
        compiler_params=_params(("parallel", "parallel")),
    )(proj, proj, proj, proj, proj, proj, sinks, ycat)


def _attn_bwd(dycat, proj, o, lse, sinks, ddt, dproj, nbatch, name):
    t = proj.shape[0]
    w = WINDOW
    nb = t // nbatch // w
    scale = ATTN_HEAD_DIM ** -0.5
    q_s, kc_s, kp_s, vc_s, vp_s, z_s, row = _attn_specs(nb, True)

    def body(dy_ref, q_ref, kc_ref, kp_ref, vc_ref, vp_ref, z_ref, o_ref, lse_ref, sk_ref, ddt_ref, _,
             grp_ref, dsk_ref, kcarry, vcarry, sacc):
        b, i = pl.program_id(0), pl.program_id(1)

        @pl.when((b == 0) & (i == 0))
        def _():
            sacc[...] = jnp.zeros_like(sacc)

        @pl.when(i == 0)
        def _():
            kcarry[...] = jnp.zeros_like(kcarry)
            vcarry[...] = jnp.zeros_like(vcarry)

        first = i == nb - 1
        mask = _band_mask(first)
        kk = jnp.concatenate([kp_ref[...], kc_ref[...]], axis=0).astype(MXU_DTYPE)
        vv = jnp.concatenate([vp_ref[...], vc_ref[...]], axis=0).astype(MXU_DTYPE)
        sk = sk_ref[...]
        lse_all = lse_ref[...]
        lane1 = _iota((1, LANES), 1)
        mask2 = jnp.concatenate([mask, mask], axis=0)
        qs, dos, deltas, lses, scores, dps = [], [], [], [], [], []
        for j in range(4):
            cols = slice(j * LANES, (j + 1) * LANES)
            qp, zp, ov, dy = q_ref[:, cols], z_ref[:, cols], o_ref[:, cols], dy_ref[:, cols]
            grp_ref[:, OFF_ZA + j * LANES:OFF_ZA + (j + 1) * LANES] = (dy * ov * _dsilu(zp)).astype(grp_ref.dtype)
            do = dy * _silu(zp)
            q2 = _stack_heads(qp, j // 2).astype(MXU_DTYPE)
            do2 = _stack_heads(do, j // 2)
            qs.append(q2)
            dos.append(do2.astype(MXU_DTYPE))
            deltas.append(jnp.sum(do2 * _stack_heads(ov, j // 2), axis=1, keepdims=True))
            lses.append(jnp.concatenate([_lane_pick(lse_all, 2 * j), _lane_pick(lse_all, 2 * j + 1)], axis=0))
            scores.append(_dot(q2, kk, NT))
            dps.append(_dot(do2, vv, NT))
        prs, dss = [], []
        dsk = jnp.zeros((1, LANES), F32)
        for j in range(4):
            pr = jnp.exp(jnp.where(mask2, scores[j] * scale, -1e30) - lses[j])
            prs.append(pr.astype(MXU_DTYPE))
            dss.append((pr * (dps[j] - deltas[j])).astype(MXU_DTYPE))
            skc = jnp.concatenate([jnp.broadcast_to(_lane_pick(sk, 2 * j), (w, 1)),
                                   jnp.broadcast_to(_lane_pick(sk, 2 * j + 1), (w, 1))], axis=0)
            sink_term = jnp.exp(skc - lses[j]) * deltas[j]
            dsk = dsk - jnp.where(lane1 == 2 * j, jnp.sum(sink_term[:w]), 0.0)
            dsk = dsk - jnp.where(lane1 == 2 * j + 1, jnp.sum(sink_term[w:]), 0.0)
        dkk = jnp.zeros((2 * w, LANES), F32)
        dvv = jnp.zeros((2 * w, LANES), F32)
        for j in range(4):
            dq = _unstack_heads(_dot(dss[j], kk) * scale, j // 2)
            grp_ref[:, OFF_Q + j * LANES:OFF_Q + (j + 1) * LANES] = dq.astype(grp_ref.dtype)
            dkk = dkk + _dot(dss[j], qs[j], TN) * scale
            dvv = dvv + _dot(prs[j], dos[j], TN)
        grp_ref[:, OFF_K:OFF_K + LANES] = (dkk[w:, :] + kcarry[...]).astype(grp_ref.dtype)
        grp_ref[:, OFF_V:OFF_V + LANES] = (dvv[w:, :] + vcarry[...]).astype(grp_ref.dtype)
        grp_ref[:, OFF_DT:OFF_DT + LANES] = ddt_ref[...].astype(grp_ref.dtype)
        grp_ref[:, OFF_DT + LANES:] = jnp.zeros((w, ATTN_GROUP - OFF_DT - LANES), grp_ref.dtype)
        kcarry[...] = dkk[:w, :]
        vcarry[...] = dvv[:w, :]
        sacc[...] += dsk

        @pl.when((b == nbatch - 1) & (i == nb - 1))
        def _():
            dsk_ref[...] = sacc[...]

    return pl.pallas_call(
        body, name=name,
        out_shape=(jax.ShapeDtypeStruct(dproj.shape, dproj.dtype), jax.ShapeDtypeStruct((1, LANES), F32)),
        grid=(nbatch, nb),
        in_specs=[pl.BlockSpec((w, 512), lambda b, i: (row(b, i), YCAT_ATTN // 512)),
                  q_s, kc_s, kp_s, vc_s, vp_s, z_s,
                  pl.BlockSpec((w, 512), lambda b, i: (row(b, i), 0)),
                  pl.BlockSpec((w, LANES), lambda b, i: (row(b, i), 0)),
                  pl.BlockSpec((1, LANES), lambda b, i: (0, 0)),
                  pl.BlockSpec((w, LANES), lambda b, i: (row(b, i), 0)), ANY],
        out_specs=(pl.BlockSpec((w, ATTN_GROUP), lambda b, i: (row(b, i), 0)),
                   pl.BlockSpec((1, LANES), lambda b, i: (0, 0))),
        input_output_aliases={11: 0},
        scratch_shapes=[pltpu.VMEM((w, LANES), F32), pltpu.VMEM((w, LANES), F32),
                        pltpu.VMEM((1, LANES), F32)],
        compiler_params=_params(("arbitrary", "arbitrary")),
    )(dycat, proj, proj, proj, proj, proj, proj, o, lse, sinks, ddt, dproj)


SSD_WIDTH = SSD_HEADS * SSD_HEAD_DIM
GROUP_ROWS = SSD_WIDTH // 2


def _expand_mat():
    r, c = _iota((LANES, SSD_WIDTH), 0), _iota((LANES, SSD_WIDTH), 1)
    return (r == lax.shift_right_logical(c, 6)).astype(BF16)


def _expand_mat_t():
    r, c = _iota((SSD_WIDTH, LANES), 0), _iota((SSD_WIDTH, LANES), 1)
    return (c == lax.shift_right_logical(r, 6)).astype(BF16)


def _ssd_common(u_ref, dt_ref, dtb_ref, a_ref, stack_broadcasts=False):
    q = CHUNK
    act = _silu(u_ref[...])
    xs = act[:, :SSD_WIDTH]
    bm = act[:, SSD_WIDTH:SSD_WIDTH + 256]
    cm = act[:, SSD_WIDTH + 256:]
    dtp = _softplus(dt_ref[...] + dtb_ref[...])
    a = dtp * a_ref[...]
    tril = (_iota((q, q), 0) >= _iota((q, q), 1)).astype(BF16)
    acs = _xdot_r(tril, a)
    acs_t = acs.T
    e = _expand_mat()
    a_end = jnp.sum(jnp.where(_iota(acs.shape, 0) == q - 1, acs, 0.0), axis=0, keepdims=True)
    if stack_broadcasts:
        spread = _xdot(jnp.concatenate([dtp, acs, a_end - acs], axis=0), e)
        dt_x, ea, dec = spread[:q], jnp.exp(spread[q:2 * q]), jnp.exp(spread[2 * q:])
    else:
        dt_x = _xdot(dtp, e)
        ea = jnp.exp(_xdot(acs, e))
        dec = jnp.exp(_xdot(a_end - acs, e))
    a_end_col = jnp.broadcast_to(_lane_pick(acs_t, q - 1), (LANES, LANES))
    s_scale = jnp.exp(_xdot_r(_expand_mat_t(), a_end_col))
    return act, xs, bm, cm, dtp, acs, acs_t, dt_x, ea, dec, s_scale, tril


def _decay_mat(acs, acs_t, h):
    q = CHUNK
    col = _lane_pick(acs, h)
    rowv = jnp.sum(jnp.where(_iota(acs_t.shape, 0) == h, acs_t, 0.0), axis=0, keepdims=True)
    causal = _iota((q, q), 0) >= _iota((q, q), 1)
    return jnp.exp(jnp.where(causal, col - rowv, -1e30))


GN_WIDTH = 512


def _ssd_fwd(u, proj, dtb, a_neg, d_x, norm_w, ycat, nbatch, name):
    t = u.shape[0]
    q = CHUNK
    nc = t // nbatch // q

    def body(u_ref, dt_ref, z_ref, dtb_ref, a_ref, dx_ref, nw_ref, _, y_ref, st_ref, yn_ref, state):
        c = pl.program_id(1)

        @pl.when(c == 0)
        def _():
            state[...] = jnp.zeros_like(state)

        st_ref[...] = state[...]
        act, xs, bm, cm, dtp, acs, acs_t, dt_x, ea, dec, s_scale, _ = _ssd_common(u_ref, dt_ref, dtb_ref, a_ref)
        xdt = xs * dt_x
        xdec = xdt * dec
        lo, hi = _half_mask(0), _half_mask(1)
        grp = []
        for g in range(2):
            bg = bm[:, g * LANES:(g + 1) * LANES]
            cg = cm[:, g * LANES:(g + 1) * LANES]
            rows = slice(g * GROUP_ROWS, (g + 1) * GROUP_ROWS)
            sg = state[rows, :]
            grp.append((_dot(cg, bg, NT), _dot(cg, sg, NT), rows,
                        s_scale[rows, :] * sg + _dot(xdec[:, rows], bg, TN)))
        yps = []
        for pj in range(SSD_HEADS // 2):
            cb = grp[pj // 4][0]
            xp = xdt[:, pj * LANES:(pj + 1) * LANES]
            m2 = jnp.concatenate([cb * _decay_mat(acs, acs_t, 2 * pj), cb * _decay_mat(acs, acs_t, 2 * pj + 1)],
                                 axis=1)
            yps.append(_dot(m2, jnp.concatenate([xp * lo, xp * hi], axis=0)))
        for g in range(2):
            _, yoff, rows, state_new = grp[g]
            for j in range(4):
                pj = g * 4 + j
                cols = slice(pj * LANES, (pj + 1) * LANES)
                yp = yps[pj] + yoff[:, j * LANES:(j + 1) * LANES] * ea[:, cols]
                y_ref[:, cols] = yp + dx_ref[:, cols] * xs[:, cols]
            state[rows, :] = state_new
        for g in range(SSD_WIDTH // GN_WIDTH):
            cols = slice(g * GN_WIDTH, (g + 1) * GN_WIDTH)
            gg = y_ref[:, cols] * _silu(z_ref[:, cols])
            rstd = lax.rsqrt(jnp.mean(gg * gg, axis=-1, keepdims=True) + EPS)
            yn_ref[:, cols] = (gg * rstd * nw_ref[:, cols]).astype(yn_ref.dtype)

    vec = pl.BlockSpec((1, LANES), lambda b, c: (0, 0))
    wide = pl.BlockSpec((q, SSD_WIDTH), lambda b, c: (b * nc + c, 0))
    wvec = pl.BlockSpec((1, SSD_WIDTH), lambda b, c: (0, 0))
    return pl.pallas_call(
        body, name=name,
        out_shape=(jax.ShapeDtypeStruct((t, SSD_WIDTH), F32),
                   jax.ShapeDtypeStruct((nbatch * nc * SSD_WIDTH, SSD_STATE), F32),
                   jax.ShapeDtypeStruct(ycat.shape, ycat.dtype)),
        grid=(nbatch, nc),
        in_specs=[pl.BlockSpec((q, SSD_CONV_DIM), lambda b, c: (b * nc + c, 0)),
                  pl.BlockSpec((q, LANES), lambda b, c: (b * nc + c, OFF_DT // LANES)),
                  pl.BlockSpec((q, SSD_WIDTH), lambda b, c: (b * nc + c, OFF_ZS // SSD_WIDTH)),
                  vec, vec, wvec, wvec, ANY],
        out_specs=(wide, pl.BlockSpec((SSD_WIDTH, SSD_STATE), lambda b, c: (b * nc + c, 0)), wide),
        input_output_aliases={7: 2},
        scratch_shapes=[pltpu.VMEM((SSD_WIDTH, SSD_STATE), F32)],
        compiler_params=_params(("parallel", "arbitrary")),
    )(u, proj, proj, dtb, a_neg, d_x, norm_w, ycat)


def _ssd_bwd(dycat, u, proj, y, states, dtb, a_neg, d_x, norm_w, dproj, nbatch, name):
    t = u.shape[0]
    q = CHUNK
    nc = t // nbatch // q

    def body(do_ref, u_ref, dt_ref, z_ref, y_ref, st_ref, dtb_ref, a_ref, dx_ref, nw_ref, _,
             du_ref, dz_ref, ddt_ref, dal_ref, dd_ref, dtbg_ref, dnw_ref, dstate, acc_a, acc_d, acc_b, acc_w):
        b, c = pl.program_id(0), pl.program_id(1)

        @pl.when((b == 0) & (c == 0))
        def _():
            acc_a[...] = jnp.zeros_like(acc_a)
            acc_d[...] = jnp.zeros_like(acc_d)
            acc_b[...] = jnp.zeros_like(acc_b)
            acc_w[...] = jnp.zeros_like(acc_w)

        @pl.when(c == 0)
        def _():
            dstate[...] = jnp.zeros_like(dstate)

        dy_parts = []
        for g in range(SSD_WIDTH // GN_WIDTH):
            cols = slice(g * GN_WIDTH, (g + 1) * GN_WIDTH)
            yv, zv, dov = y_ref[:, cols], z_ref[:, cols], do_ref[:, cols]
            sz = _silu(zv)
            gg = yv * sz
            rstd = lax.rsqrt(jnp.mean(gg * gg, axis=-1, keepdims=True) + EPS)
            gh = gg * rstd
            acc_w[:, cols] += _rowsum8(dov * gh)
            dgn = dov * nw_ref[:, cols]
            dg = rstd * (dgn - gh * jnp.mean(dgn * gh, axis=-1, keepdims=True))
            dy_parts.append(dg * sz)
            dz_ref[:, cols] = (dg * yv * _dsilu(zv)).astype(dz_ref.dtype)

        act, xs, bm, cm, dtp, acs, acs_t, dt_x, ea, dec, s_scale, tril = _ssd_common(
            u_ref, dt_ref, dtb_ref, a_ref, stack_broadcasts=True)
        xdt = xs * dt_x
        xdec = xdt * dec
        dyv = jnp.concatenate(dy_parts, axis=1)
        dye = dyv * ea
        lo, hi = _half_mask(0), _half_mask(1)
        et = _expand_mat_t()
        grp = []
        for g in range(2):
            rows = slice(g * GROUP_ROWS, (g + 1) * GROUP_ROWS)
            bg = bm[:, g * LANES:(g + 1) * LANES]
            cg = cm[:, g * LANES:(g + 1) * LANES]
            sg = st_ref[rows, :]
            dsg = dstate[rows, :]
            grp.append(dict(
                rows=rows, bg=bg, cg=cg, dsg=dsg,
                cb=_dot(cg, bg, NT), yoff=_dot(cg, sg, NT), dxst=_dot(bg, dsg, NT) * dec[:, rows],
                dc_off=_dot(dye[:, rows], sg), db_off=_dot(xdec[:, rows], dsg),
                s_carried=s_scale[rows, :] * sg,
                dstate_new=_dot(dye[:, rows], cg, TN) + s_scale[rows, :] * dsg))
        dy2s, g2s, l2s = [], [], []
        for pj in range(SSD_HEADS // 2):
            cols = slice(pj * LANES, (pj + 1) * LANES)
            dyp = dyv[:, cols]
            dy2 = jnp.concatenate([dyp * lo, dyp * hi], axis=0).astype(MXU_DTYPE)
            dy2s.append(dy2)
            g2s.append(_dot(dy2, xdt[:, cols], NT))
            l2s.append(jnp.concatenate([_decay_mat(acs, acs_t, 2 * pj), _decay_mat(acs, acs_t, 2 * pj + 1)], axis=0))
        dal_diag = jnp.zeros((q, LANES), F32)
        lane2 = _iota((2 * q, LANES), 1)
        row2 = _iota((2 * q, LANES), 0)
        dxdt_parts, db_parts, dc_parts = [], [], []
        end_sum = jnp.zeros((LANES, LANES), F32)
        for g in range(2):
            gd = grp[g]
            cb2 = jnp.concatenate([gd["cb"], gd["cb"]], axis=0)
            dcb = jnp.zeros((q, q), F32)
            parts = []
            for j in range(4):
                pj = g * 4 + j
                gl = g2s[pj] * l2s[pj]
                dcb = dcb + gl[:q] + gl[q:]
                m2 = cb2 * l2s[pj]
                parts.append(_dot(m2, dy2s[pj], TN))
                w2 = (gl * cb2).astype(MXU_DTYPE)
                sel2 = (lane2 == 2 * pj + (row2 >= q).astype(jnp.int32)).astype(MXU_DTYPE)
                dal_diag = dal_diag + _dot(jnp.concatenate([w2[:q], w2[q:]], axis=1), sel2) - _dot(w2, sel2, TN)
            dxdt_parts.append(jnp.concatenate(parts, axis=1) + gd["dxst"])
            dc_parts.append(_dot(dcb, gd["bg"]) + gd["dc_off"])
            db_parts.append(_dot(dcb, gd["cg"], TN) + gd["db_off"])
            end_sum = end_sum + _xdot(gd["dsg"] * gd["s_carried"], et[gd["rows"], :], TN, passes=2)
            dstate[gd["rows"], :] = gd["dstate_new"]
        dxst_parts = [gd["dxst"] for gd in grp]
        yoff_parts = [gd["yoff"] for gd in grp]
        dxdt = jnp.concatenate(dxdt_parts, axis=1)
        dxv = dx_ref[...]
        yoff = jnp.concatenate(yoff_parts, axis=1) * ea
        per_head = _xdot(jnp.concatenate([dyv * yoff, xdt * jnp.concatenate(dxst_parts, axis=1),
                                          dxdt * xs, dyv * xs], axis=0), et)
        off_term, st_term, dx_term, d_term = (per_head[k * q:(k + 1) * q] for k in range(4))
        dalpha = dal_diag + off_term - st_term
        end_row = jnp.sum(end_sum, axis=0, keepdims=True) + jnp.sum(st_term, axis=0, keepdims=True)
        dalpha = dalpha + jnp.where(_iota((q, LANES), 0) == q - 1, end_row, 0.0)
        da = _xdot_r(tril, dalpha, TN)
        ddtp = da * a_ref[...] + dx_term
        acc_a[...] += _rowsum8(da * dtp)
        acc_d[...] += _rowsum8(d_term)
        ddt_raw = ddtp * _sigmoid(dt_ref[...] + dtb_ref[...])
        acc_b[...] += _rowsum8(ddt_raw)
        ddt_ref[...] = ddt_raw
        dxs = dxdt * dt_x + dxv * dyv
        dact = jnp.concatenate([dxs] + db_parts + dc_parts, axis=1)
        du_ref[...] = dact * _dsilu(u_ref[...])

        @pl.when((b == nbatch - 1) & (c == nc - 1))
        def _():
            dal_ref[...] = jnp.sum(acc_a[...], axis=0, keepdims=True) * a_ref[...]
            dd_ref[...] = jnp.sum(acc_d[...], axis=0, keepdims=True)
            dtbg_ref[...] = jnp.sum(acc_b[...], axis=0, keepdims=True)
            dnw_ref[...] = jnp.sum(acc_w[...], axis=0, keepdims=True)

    def rowblk(b, c):
        return b * nc + (nc - 1 - c)

    vec = pl.BlockSpec((1, LANES), lambda b, c: (0, 0))
    wvec = pl.BlockSpec((1, SSD_WIDTH), lambda b, c: (0, 0))
    wide = pl.BlockSpec((q, SSD_WIDTH), lambda b, c: (rowblk(b, c), 0))
    zblk = pl.BlockSpec((q, SSD_WIDTH), lambda b, c: (rowblk(b, c), OFF_ZS // SSD_WIDTH))
    return pl.pallas_call(
        body, name=name,
        out_shape=(jax.ShapeDtypeStruct((t, SSD_CONV_DIM), F32), jax.ShapeDtypeStruct(dproj.shape, dproj.dtype),
                   jax.ShapeDtypeStruct((t, LANES), F32),
                   jax.ShapeDtypeStruct((1, LANES), F32), jax.ShapeDtypeStruct((1, LANES), F32),
                   jax.ShapeDtypeStruct((1, LANES), F32), jax.ShapeDtypeStruct((1, SSD_WIDTH), F32)),
        grid=(nbatch, nc),
        in_specs=[wide,
                  pl.BlockSpec((q, SSD_CONV_DIM), lambda b, c: (rowblk(b, c), 0)),
                  pl.BlockSpec((q, LANES), lambda b, c: (rowblk(b, c), OFF_DT // LANES)),
                  zblk, wide,
                  pl.BlockSpec((SSD_WIDTH, SSD_STATE), lambda b, c: (rowblk(b, c), 0)),
                  vec, vec, wvec, wvec, ANY],
        out_specs=(pl.BlockSpec((q, SSD_CONV_DIM), lambda b, c: (rowblk(b, c), 0)),
                   zblk,
                   pl.BlockSpec((q, LANES), lambda b, c: (rowblk(b, c), 0)),
                   vec, vec, vec, wvec),
        input_output_aliases={10: 1},
        scratch_shapes=[pltpu.VMEM((SSD_WIDTH, SSD_STATE), F32), pltpu.VMEM((SUBLANES, LANES), F32),
                        pltpu.VMEM((SUBLANES, LANES), F32), pltpu.VMEM((SUBLANES, LANES), F32),
                        pltpu.VMEM((SUBLANES, SSD_WIDTH), F32)],
        compiler_params=_params(("arbitrary", "arbitrary")),
    )(dycat, u, proj, proj, y, states, dtb, a_neg, d_x, norm_w, dproj)


def _pad_rows(w, rows):
    return jnp.concatenate([w, jnp.zeros((rows - w.shape[0], w.shape[1]), w.dtype)], axis=0)


def _pad_lanes(v):
    return jnp.concatenate([v, jnp.zeros((LANES - v.shape[0],), v.dtype)]).reshape(1, LANES)


def _padded_from_chips(pieces):
    cols = pieces[0].shape[-1]
    lead = pieces[0].shape[:-1]
    parts, pos = [], 0
    for lo, hi, start in sorted(SECTIONS, key=lambda s: s[2]):
        if start > pos:
            parts.append(jnp.zeros(lead + (start - pos,), pieces[0].dtype))
        pos = start + hi - lo
        while lo < hi:
            p = lo // cols
            end = min(hi, (p + 1) * cols)
            parts.append(pieces[p][..., lo - p * cols:end - p * cols])
            lo = end
    if pos < NP:
        parts.append(jnp.zeros(lead + (NP - pos,), pieces[0].dtype))
    return jnp.concatenate(parts, axis=-1)


def _chip_part_from_padded(wp, p, cols):
    lo, hi = p * cols, (p + 1) * cols
    parts = []
    for rs, re, start in SECTIONS:
        a, b = max(lo, rs), min(hi, re)
        if a < b:
            parts.append(wp[..., start + a - rs:start + b - rs])
    return jnp.concatenate(parts, axis=-1)


def _layer_params(li, w_in_p, w_out, conv_w, dw_w, small):
    return dict(
        w_in_p=w_in_p, w_out=w_out,
        conv_w=_pad_rows(conv_w, SUBLANES), dw_w=_pad_rows(dw_w, 32),
        norm_w=small["norm_w"][li].reshape(1, -1),
        conv_b=small["ssd_conv_b"][li].reshape(1, -1),
        dtb=_pad_lanes(small["ssd_dt_bias"][li]),
        a_neg=_pad_lanes(-jnp.exp(small["ssd_a_log"][li])),
        d_x=jnp.repeat(small["ssd_d"][li], SSD_HEAD_DIM).reshape(1, -1),
        ssd_norm_w=small["ssd_norm_w"][li].reshape(1, -1),
        sinks=_pad_lanes(small["attn_sinks"][li]),
        dw_b=small["conf_dw_b"][li].reshape(1, -1),
        ln_w=small["conf_ln_w"][li].reshape(1, -1),
        ln_b=small["conf_ln_b"][li].reshape(1, -1),
    )


def _layer_fwd(x, p, nbatch, seq, tag, after=None):
    proj, h_t = _proj_fwd(x, p["norm_w"], p["w_in_p"], name=f"proj_fwd_{tag}", after=after)
    u = _conv_fwd(proj, OFF_XBC, SSD_CONV_DIM, p["conv_w"], p["conv_b"], SSD_CONV, seq, name=f"ssd_conv_fwd_{tag}")
    ycat = lax.empty((x.shape[0], MIX_WIDTH), MXU_DTYPE)
    y, states, ycat = _ssd_fwd(u, proj, p["dtb"], p["a_neg"], p["d_x"], p["ssd_norm_w"], ycat, nbatch,
                               name=f"ssd_fwd_{tag}")
    ycat, o, lse = _attn_fwd(proj, p["sinks"], ycat, nbatch, name=f"attn_fwd_{tag}")
    c1, ycat = _conf_fwd(proj, p["dw_w"], p["dw_b"], p["ln_w"], p["ln_b"], ycat, seq, name=f"conf_fwd_{tag}")
    w_out = p["w_out"](ycat) if callable(p["w_out"]) else p["w_out"]
    x_new = _matmul(ycat, w_out, "nn", F32, 1024, 512, 2048, name=f"out_fwd_{tag}", residual=x)
    return x_new, dict(x=x, w_out=w_out, h_t=h_t, proj=proj, u=u, y=y, states=states, o=o, lse=lse, c1=c1, ycat=ycat)


def _layer_bwd(dx_out, p, s, nbatch, seq, tag, hooks=None):
    hooks = hooks or {}
    proj = s["proj"]
    dycat = _matmul(dx_out, s["w_out"], "nt", F32, 1024, 1024, 1024, name=f"out_bwd_dy_{tag}",
                    after=hooks.get("start_token"))
    dw_out = _matmul(s["ycat"], dx_out, "tn", F32, 1024, 1024, 1024, name=f"out_bwd_dw_{tag}")
    token = hooks["after_dycat"](dycat) if "after_dycat" in hooks else None
    dtb = p["dtb"] if token is None else p["dtb"] + token[0, 0]
    dproj = lax.empty(proj.shape, MXU_DTYPE)
    du, dproj, ddt, da_log, dd, ddtb, dssd_norm_w = _ssd_bwd(
        dycat, s["u"], proj, s["y"], s["states"], dtb, p["a_neg"], p["d_x"], p["ssd_norm_w"], dproj,
        nbatch, name=f"ssd_bwd_{tag}")
    dproj, dconv_w, dconv_b = _conv_bwd(du, proj, OFF_XBC, SSD_CONV_DIM, p["conv_w"], SSD_CONV, seq,
                                        name=f"ssd_conv_bwd_{tag}", into=dproj)
    dproj, dsinks = _attn_bwd(dycat, proj, s["o"], s["lse"], p["sinks"], ddt, dproj, nbatch,
                              name=f"attn_bwd_{tag}")
    if "after_attn" in hooks:
        hooks["after_attn"](dproj)
    dproj, ddw_w, ddw_b, dln_w, dln_b = _conf_bwd(dycat, proj, s["c1"], p["dw_w"], p["ln_w"], p["ln_b"], dproj, seq,
                                                  name=f"conf_bwd_{tag}")
    dw_in_p = _matmul(s["h_t"], dproj, "nn", F32, 1024, 512, 4096, name=f"proj_bwd_dw_{tag}")
    token = hooks["after_dw"](dw_in_p, dw_out) if "after_dw" in hooks else None
    norm_w = p["norm_w"] if token is None else p["norm_w"] + token[0, 0]
    dx_in, dnorm_w = _proj_bwd_dx(dproj, p["w_in_p"], s["x"], norm_w, dx_out, name=f"proj_bwd_dx_{tag}")
    grads = dict(
        norm_w=dnorm_w[0], w_in_p=dw_in_p, ssd_conv_w=dconv_w[:SSD_CONV], ssd_conv_b=dconv_b[0],
        ssd_dt_bias=ddtb[0, :SSD_HEADS], ssd_a_log=da_log[0, :SSD_HEADS], ssd_d=dd[0, :SSD_HEADS],
        ssd_norm_w=dssd_norm_w[0], attn_sinks=dsinks[0, :ATTN_Q_HEADS], conf_dw_w=ddw_w[:CONF_KERNEL],
        conf_dw_b=ddw_b[0], conf_ln_w=dln_w[0], conf_ln_b=dln_b[0], w_out=dw_out)
    return dx_in, grads


def _local_step(x, target, param_fns, final_norm_w, first_after=None, bwd_hooks=None):
    nbatch, seq, d = x.shape
    xt = x.reshape(nbatch * seq, d)
    saved, layer_params = [], []
    for li, fn in enumerate(param_fns):
        p = fn(xt)
        layer_params.append(p)
        xt, s = _layer_fwd(xt, p, nbatch, seq, f"l{li}", after=first_after if li == 0 else None)
        saved.append(s)
    loss, dx, dfinal = _loss_head(xt, target.reshape(nbatch * seq, d), final_norm_w.reshape(1, d), name="loss_head")
    grads = [None] * len(layer_params)
    for li in reversed(range(len(layer_params))):
        hooks = bwd_hooks(li) if bwd_hooks is not None else None
        dx, grads[li] = _layer_bwd(dx, layer_params[li], saved[li], nbatch, seq, f"l{li}", hooks=hooks)
    return loss[0, 0], dx.reshape(nbatch, seq, d), grads, dfinal[0]


MESH = pl.DeviceIdType.MESH
N_CHIPS = 4


def _mesh_pos():
    return lax.axis_index("x"), lax.axis_index("y"), lax.axis_index("c")


def _other_chips(x, y):
    return [(1 - x, y), (x, 1 - y), (1 - x, 1 - y)]


def _gather_weights(big, small, name):
    nbig, nsmall = len(big), len(small)
    n_ici = 3 * (nbig + nsmall)
    n_fwd = 3 * nbig

    def body(*refs):
        ins = refs[:nbig + nsmall]
        outs = refs[nbig + nsmall:2 * (nbig + nsmall)]
        send_sems, recv_sems = refs[2 * (nbig + nsmall):]
        x, y, c = _mesh_pos()
        me = 2 * x + y
        sibling = (x, y, 1 - c)
        chips = _other_chips(x, y)

        def ici(a, j, origin, dest):
            if a < nbig:
                src = ins[a].at[c] if origin is None else outs[a].at[origin, c]
                dst = outs[a].at[me if origin is None else origin, c]
            else:
                src = ins[a] if origin is None else outs[a].at[origin]
                dst = outs[a].at[me if origin is None else origin]
            k = a * 3 + j
            return pltpu.make_async_remote_copy(src_ref=src, dst_ref=dst, send_sem=send_sems.at[k],
                                                recv_sem=recv_sems.at[k], device_id=dest, device_id_type=MESH)

        def fwd(a, j, origin, half):
            k = n_ici + a * 3 + j
            ref = outs[a].at[origin, half]
            return pltpu.make_async_remote_copy(src_ref=ref, dst_ref=ref, send_sem=send_sems.at[k],
                                                recv_sem=recv_sems.at[k], device_id=sibling, device_id_type=MESH)

        sends = []
        for j, (px, py) in enumerate(chips):
            for a in range(nbig + nsmall):
                cp = ici(a, j, None, (px, py, c))
                cp.start()
                sends.append(cp)
        for j, (px, py) in enumerate(chips):
            origin = 2 * px + py
            for a in range(nbig):
                ici(a, j, origin, (px, py, c)).wait_recv()
                cp = fwd(a, j, origin, c)
                cp.start()
                sends.append(cp)
        for j, (px, py) in enumerate(chips):
            origin = 2 * px + py
            for a in range(nbig, nbig + nsmall):
                ici(a, j, origin, (px, py, c)).wait_recv()
            for a in range(nbig):
                fwd(a, j, origin, 1 - c).wait_recv()
        for cp in sends:
            cp.wait_send()

    out_shape = tuple(jax.ShapeDtypeStruct((N_CHIPS,) + a.shape, a.dtype) for a in list(big) + list(small))
    return pl.pallas_call(
        body, name=name, out_shape=out_shape,
        in_specs=[ANY] * (nbig + nsmall), out_specs=tuple([ANY] * (nbig + nsmall)),
        scratch_shapes=[pltpu.SemaphoreType.DMA((n_ici + n_fwd,)), pltpu.SemaphoreType.DMA((n_ici + n_fwd,))],
    )(*big, *small)


HBM = pl.BlockSpec(memory_space=pltpu.HBM)
SEM = pl.BlockSpec(memory_space=pltpu.SEMAPHORE)
DATAFLOW = pltpu.SideEffectType.DATAFLOW_SIDE_EFFECTING


def _split_peers(pattern, x, y, c):
    if pattern == "swap":
        return [((x, y, 1 - c), 1 - c, None, None)]
    me = 2 * x + y
    return [((px, py, c), 2 * px + py if pattern == "scatter" else None, me, 2 * px + py)
            for px, py in _other_chips(x, y)]


def _split_land_shape(pattern, shape):
    return {"bcast": (N_CHIPS,) + shape, "scatter": shape, "swap": shape[:1] + shape[2:]}[pattern]


def _split_copies(pattern, srcs, lands, send_sems, recv_sems, waiting):
    x, y, c = _mesh_pos()
    peers = _split_peers(pattern, x, y, c)
    cps = []
    for j, (dev, src_slot, dst_slot, my_slot) in enumerate(peers):
        for a in range(len(srcs)):
            if src_slot is None:
                src = srcs[a]
            else:
                src = srcs[a].at[:, src_slot] if pattern == "swap" else srcs[a].at[src_slot]
            slot = my_slot if waiting else dst_slot
            dst = lands[a] if slot is None else lands[a].at[slot]
            k = a * len(peers) + j
            cps.append(pltpu.make_async_remote_copy(src_ref=src, dst_ref=dst, send_sem=send_sems[k],
                                                    recv_sem=recv_sems[k], device_id=dev, device_id_type=MESH))
    return cps


def _split_start(arrs, pattern, after, name):
    n = len(arrs)
    nsem = n * (1 if pattern == "swap" else N_CHIPS - 1)
    deps = [] if after is None else [after]

    def body(*refs):
        srcs, lands = refs[:n], refs[n:2 * n]
        outs = refs[2 * n + len(deps):]
        for cp in _split_copies(pattern, srcs, lands, outs[:nsem], outs[nsem:2 * nsem], waiting=False):
            cp.start()
        outs[-1][...] = jnp.zeros_like(outs[-1])

    lands = [lax.empty(_split_land_shape(pattern, a.shape), a.dtype) for a in arrs]
    out_shape = ([pltpu.SemaphoreType.DMA(())] * (2 * nsem)
                 + [pltpu.HBM(a.shape, a.dtype) for a in arrs] + [pltpu.HBM(b.shape, b.dtype) for b in lands]
                 + [jax.ShapeDtypeStruct((SUBLANES, LANES), F32)])
    outs = pl.pallas_call(
        body, name=name, out_shape=tuple(out_shape),
        in_specs=[HBM] * (2 * n) + [ANY] * len(deps),
        out_specs=tuple([SEM] * (2 * nsem) + [HBM] * (2 * n) + [pl.BlockSpec(memory_space=pltpu.VMEM)]),
        input_output_aliases={a: 2 * nsem + a for a in range(2 * n)},
        compiler_params=pltpu.CompilerParams(has_side_effects=DATAFLOW),
    )(*[pltpu.with_memory_space_constraint(a, pltpu.HBM) for a in list(arrs) + lands], *deps)
    return outs[:-1], outs[-1]


def _split_wait(state, n, pattern, after, name):
    nsem = n * (1 if pattern == "swap" else N_CHIPS - 1)

    def body(*refs):
        srcs, lands = refs[:n], refs[n:2 * n]
        send_sems, recv_sems = refs[2 * n:2 * n + nsem], refs[2 * n + nsem:2 * n + 2 * nsem]
        for cp in _split_copies(pattern, srcs, lands, send_sems, recv_sems, waiting=True):
            cp.wait_send()
            cp.wait_recv()

    sems, thru = state[:2 * nsem], state[2 * nsem:]
    outs = pl.pallas_call(
        body, name=name, out_shape=tuple(pltpu.HBM(a.shape, a.dtype) for a in thru),
        in_specs=[HBM] * (2 * n) + [SEM] * (2 * nsem) + [ANY],
        out_specs=tuple([HBM] * (2 * n)),
        input_output_aliases={a: a for a in range(2 * n)},
        compiler_params=pltpu.CompilerParams(has_side_effects=DATAFLOW),
    )(*thru, *sems, after)
    return outs[:n], outs[n:]


def _pair_gather(arrs, layer, name):
    n = len(arrs)

    def body(*refs):
        outs = refs[n:2 * n]
        send_sems, recv_sems = refs[2 * n:]
        x, y, c = _mesh_pos()
        cps = [pltpu.make_async_remote_copy(src_ref=outs[a].at[layer, c], dst_ref=outs[a].at[layer, c],
                                            send_sem=send_sems.at[a], recv_sem=recv_sems.at[a],
                                            device_id=(x, y, 1 - c), device_id_type=MESH)
               for a in range(n)]
        for cp in cps:
            cp.start()
        for cp in cps:
            cp.wait()

    return pl.pallas_call(
        body, name=name, out_shape=tuple(jax.ShapeDtypeStruct(a.shape, a.dtype) for a in arrs),
        in_specs=[ANY] * n, out_specs=tuple([ANY] * n),
        input_output_aliases={a: a for a in range(n)},
        scratch_shapes=[pltpu.SemaphoreType.DMA((n,)), pltpu.SemaphoreType.DMA((n,))],
    )(*arrs)


N_DEV = 8


def _allreduce_small(pack, name):
    r = pack.shape[0]

    def body(p_ref, o_ref, land, send_sems, recv_sems):
        x, y, c = _mesh_pos()
        me = 4 * x + 2 * y + c
        cps = []
        for k in range(1, N_DEV):
            peer = (x ^ (k >> 2), y ^ ((k >> 1) & 1), c ^ (k & 1))
            cps.append(pltpu.make_async_remote_copy(src_ref=p_ref, dst_ref=land.at[me], send_sem=send_sems.at[k - 1],
                                                    recv_sem=recv_sems.at[k - 1], device_id=peer, device_id_type=MESH))
        for cp in cps:
            cp.start()
        land[me] = p_ref[...]
        for cp in cps:
            cp.wait()
        total = land[0]
        for d in range(1, N_DEV):
            total = total + land[d]
        o_ref[...] = total

    vm = pl.BlockSpec(memory_space=pltpu.VMEM)
    return pl.pallas_call(
        body, name=name, out_shape=jax.ShapeDtypeStruct(pack.shape, F32),
        in_specs=[vm], out_specs=vm,
        scratch_shapes=[pltpu.VMEM((N_DEV, r, LANES), F32), pltpu.SemaphoreType.DMA((N_DEV - 1,)),
                        pltpu.SemaphoreType.DMA((N_DEV - 1,))],
    )(pack)


BIG_ROWS = 128


def _cast_layer(w, layer, name):
    _, r, cdim = w.shape
    tr = BIG_ROWS

    def body(w_ref, o_ref):
        o_ref[...] = w_ref[...].astype(o_ref.dtype)

    return pl.pallas_call(
        body, name=name, out_shape=jax.ShapeDtypeStruct((r, cdim), MXU_DTYPE),
        grid=(r // tr,), in_specs=[pl.BlockSpec((None, tr, cdim), lambda i: (layer, i, 0))],
        out_specs=pl.BlockSpec((tr, cdim), lambda i: (i, 0)),
        compiler_params=_params(("parallel",)),
    )(w)


def _cast_cols_major(w_t, name):
    cdim, nl, r = w_t.shape
    tc = LANES

    def body(w_ref, *o_refs):
        for l in range(nl):
            o_refs[l][...] = w_ref[:, l, :].T.astype(o_refs[l].dtype)

    out = pl.BlockSpec((r, tc), lambda i: (0, i))
    return pl.pallas_call(
        body, name=name, out_shape=tuple(jax.ShapeDtypeStruct((r, cdim), MXU_DTYPE) for _ in range(nl)),
        grid=(pl.cdiv(cdim, tc),), in_specs=[pl.BlockSpec((tc, nl, r), lambda i: (i, 0, 0))],
        out_specs=tuple([out] * nl),
        compiler_params=_params(("parallel",)),
    )(w_t)


def _pair_sum(parts, sib, which, out_dtype, name):
    k, _, r, cdim = parts.shape
    tr = BIG_ROWS

    def body(sel_ref, p_ref, s_ref, o_ref):
        o_ref[...] = (p_ref[...] + s_ref[...]).astype(o_ref.dtype)

    grid_spec = pltpu.PrefetchScalarGridSpec(
        num_scalar_prefetch=1, grid=(k, r // tr),
        in_specs=[pl.BlockSpec((None, None, tr, cdim), lambda l, i, sel: (l, sel[0], i, 0)),
                  pl.BlockSpec((None, tr, cdim), lambda l, i, sel: (l, i, 0))],
        out_specs=pl.BlockSpec((None, tr, cdim), lambda l, i, sel: (l, i, 0)))
    return pl.pallas_call(
        body, name=name, out_shape=jax.ShapeDtypeStruct((k, r, cdim), out_dtype), grid_spec=grid_spec,
        compiler_params=_params(("parallel", "parallel")),
    )(which.reshape(1).astype(jnp.int32), parts, sib)


def _sum_lead(parts, into, layer, which, name):
    k, r, cdim = parts.shape
    tr = BIG_ROWS

    def body(sel_ref, p_ref, _, o_ref):
        total = p_ref[0].astype(F32)
        for a in range(1, k):
            total = total + p_ref[a].astype(F32)
        o_ref[...] = total

    grid_spec = pltpu.PrefetchScalarGridSpec(
        num_scalar_prefetch=1, grid=(r // tr,),
        in_specs=[pl.BlockSpec((k, tr, cdim), lambda i, sel: (0, i, 0)), ANY],
        out_specs=pl.BlockSpec((None, None, tr, cdim), lambda i, sel: (layer, sel[0], i, 0)))
    return pl.pallas_call(
        body, name=name, out_shape=jax.ShapeDtypeStruct(into.shape, F32), grid_spec=grid_spec,
        input_output_aliases={2: 0},
        compiler_params=_params(("parallel",)),
    )(which.reshape(1).astype(jnp.int32), parts, into)


def _adam_math(w, g, m, v):
    m2 = ADAM_B1 * m + (1.0 - ADAM_B1) * g
    v2 = ADAM_B2 * v + (1.0 - ADAM_B2) * (g * g)
    m_hat = m2 / (1.0 - ADAM_B1 ** ADAM_STEP)
    v_hat = v2 / (1.0 - ADAM_B2 ** ADAM_STEP)
    delta = -ADAM_LR * (m_hat / (jnp.sqrt(v_hat) + ADAM_EPS) + ADAM_WD * w)
    return delta, m2, v2


def _adam_big(w, g, m, v, name):
    nl, r, cdim = w.shape
    tr = BIG_ROWS

    def body(w_ref, g_ref, m_ref, v_ref, d_ref, mo_ref, vo_ref):
        delta, m2, v2 = _adam_math(w_ref[...], g_ref[...], m_ref[...], v_ref[...])
        d_ref[...] = delta
        mo_ref[...] = m2
        vo_ref[...] = v2

    blk = pl.BlockSpec((None, tr, cdim), lambda l, i: (l, i, 0))
    shp = jax.ShapeDtypeStruct(w.shape, F32)
    return pl.pallas_call(
        body, name=name, out_shape=(shp, shp, shp),
        grid=(nl, r // tr), in_specs=[blk] * 4, out_specs=(blk, blk, blk),
        compiler_params=_params(("parallel", "parallel")),
    )(w, g, m, v)


def _adam_cols_major(w, g, m, v, name):
    cdim, nl, r = w.shape
    tc = BIG_ROWS

    def body(w_ref, g_ref, m_ref, v_ref, d_ref, mo_ref, vo_ref):
        delta, m2, v2 = _adam_math(w_ref[...], g_ref[...], m_ref[...], v_ref[...])
        d_ref[...] = delta
        mo_ref[...] = m2
        vo_ref[...] = v2

    blk = pl.BlockSpec((tc, nl, r), lambda i: (i, 0, 0))
    shp = jax.ShapeDtypeStruct(w.shape, F32)
    return pl.pallas_call(
        body, name=name, out_shape=(shp, shp, shp),
        grid=(pl.cdiv(cdim, tc),), in_specs=[blk] * 4, out_specs=(blk, blk, blk),
        compiler_params=_params(("parallel",)),
    )(w, g, m, v)


def _adam_small(ws, gs, ms, vs, name):
    n = len(ws)

    def body(*refs):
        w_refs, g_refs, m_refs, v_refs = (refs[k * n:(k + 1) * n] for k in range(4))
        d_refs, mo_refs, vo_refs = (refs[(4 + k) * n:(5 + k) * n] for k in range(3))
        for a in range(n):
            delta, m2, v2 = _adam_math(w_refs[a][...], g_refs[a][...], m_refs[a][...], v_refs[a][...])
            d_refs[a][...] = delta
            mo_refs[a][...] = m2
            vo_refs[a][...] = v2

    shapes = tuple(jax.ShapeDtypeStruct(w.shape, F32) for w in ws)
    vm = pl.BlockSpec(memory_space=pltpu.VMEM)
    outs = pl.pallas_call(body, name=name, out_shape=shapes * 3, in_specs=[vm] * (4 * n),
                          out_specs=tuple([vm] * (3 * n)))(*ws, *gs, *ms, *vs)
    return outs[:n], outs[n:2 * n], outs[2 * n:]


PACK_TILE = SUBLANES * LANES


def _pack(arrays):
    rows = []
    for a in arrays:
        flat = a.reshape(-1)
        pad = (-flat.shape[0]) % PACK_TILE
        if pad:
            flat = jnp.concatenate([flat, jnp.zeros((pad,), flat.dtype)])
        rows.append(flat.reshape(-1, LANES))
    return jnp.concatenate(rows, axis=0)


def _unpack(pack, shapes):
    outs, row = [], 0
    for shp in shapes:
        n = int(np.prod(shp))
        nrows = -(-n // PACK_TILE) * SUBLANES
        outs.append(pack[row:row + nrows].reshape(-1)[:n].reshape(shp))
        row += nrows
    return outs


SMALL = ["norm_w", "ssd_conv_b", "ssd_dt_bias", "ssd_a_log", "ssd_d", "ssd_norm_w", "attn_sinks",
         "conf_dw_b", "conf_ln_w", "conf_ln_b"]
WEIGHTS = ["norm_w", "w_in", "ssd_conv_w", "ssd_conv_b", "ssd_dt_bias", "ssd_a_log", "ssd_d", "ssd_norm_w",
           "attn_sinks", "conf_dw_w", "conf_dw_b", "conf_ln_w", "conf_ln_b", "w_out", "final_norm_w"]


def kernel(x, norm_w, w_in, ssd_conv_w, ssd_conv_b, ssd_dt_bias, ssd_a_log, ssd_d, ssd_norm_w, attn_sinks, conf_dw_w, conf_dw_b, conf_ln_w, conf_ln_b, w_out, final_norm_w, loss_target, m_norm_w, m_w_in, m_ssd_conv_w, m_ssd_conv_b, m_ssd_dt_bias, m_ssd_a_log, m_ssd_d, m_ssd_norm_w, m_attn_sinks, m_conf_dw_w, m_conf_dw_b, m_conf_ln_w, m_conf_ln_b, m_w_out, m_final_norm_w, v_norm_w, v_w_in, v_ssd_conv_w, v_ssd_conv_b, v_ssd_dt_bias, v_ssd_a_log, v_ssd_d, v_ssd_norm_w, v_attn_sinks, v_conf_dw_w, v_conf_dw_b, v_conf_ln_w, v_conf_ln_b, v_w_out, v_final_norm_w):
    w = dict(norm_w=norm_w, w_in=w_in, ssd_conv_w=ssd_conv_w, ssd_conv_b=ssd_conv_b, ssd_dt_bias=ssd_dt_bias,
             ssd_a_log=ssd_a_log, ssd_d=ssd_d, ssd_norm_w=ssd_norm_w, attn_sinks=attn_sinks, conf_dw_w=conf_dw_w,
             conf_dw_b=conf_dw_b, conf_ln_w=conf_ln_w, conf_ln_b=conf_ln_b, w_out=w_out, final_norm_w=final_norm_w)
    m = dict(norm_w=m_norm_w, w_in=m_w_in, ssd_conv_w=m_ssd_conv_w, ssd_conv_b=m_ssd_conv_b,
             ssd_dt_bias=m_ssd_dt_bias, ssd_a_log=m_ssd_a_log, ssd_d=m_ssd_d, ssd_norm_w=m_ssd_norm_w,
             attn_sinks=m_attn_sinks, conf_dw_w=m_conf_dw_w, conf_dw_b=m_conf_dw_b, conf_ln_w=m_conf_ln_w,
             conf_ln_b=m_conf_ln_b, w_out=m_w_out, final_norm_w=m_final_norm_w)
    v = dict(norm_w=v_norm_w, w_in=v_w_in, ssd_conv_w=v_ssd_conv_w, ssd_conv_b=v_ssd_conv_b,
             ssd_dt_bias=v_ssd_dt_bias, ssd_a_log=v_ssd_a_log, ssd_d=v_ssd_d, ssd_norm_w=v_ssd_norm_w,
             attn_sinks=v_attn_sinks, conf_dw_w=v_conf_dw_w, conf_dw_b=v_conf_dw_b, conf_ln_w=v_conf_ln_w,
             conf_ln_b=v_conf_ln_b, w_out=v_w_out, final_norm_w=v_final_norm_w)
    depth = w_in.shape[0]
    me = 2 * lax.axis_index("x") + lax.axis_index("y")

    assert depth == 2
    w_in_t = jnp.transpose(w_in, (2, 0, 1))
    w_in_b = _cast_cols_major(w_in_t, name="cast_w_in")
    w_out_b = [_cast_layer(w_out, li, name=f"cast_w_out_l{li}") for li in range(depth)]
    own0 = [w_in_b[0].reshape((2, -1) + w_in_b[0].shape[1:]), ssd_conv_w, conf_dw_w]
    gathered0 = _gather_weights(own0[:1], own0[1:], name="gather_weights_l0")
    g_in0, g_conv, g_dw = [lax.dynamic_update_index_in_dim(g_all, mine, me, 0)
                           for g_all, mine in zip(gathered0, own0)]
    own1 = [w_out_b[0], w_in_b[1], w_out_b[1]]
    pending1, token1 = _split_start(own1, "bcast", gathered0[0], name="gather_rest_start")
    rest = {}

    def small_full(li):
        return (jnp.concatenate([g_conv[p, li] for p in range(N_CHIPS)], axis=1),
                jnp.concatenate([g_dw[p, li] for p in range(N_CHIPS)], axis=1))

    def w_out_l0(after):
        mine1, landed = _split_wait(pending1, len(own1), "bcast", after, name="gather_rest_wait")
        rest["landed"] = [lax.dynamic_update_index_in_dim(g_all, mine, me, 0) for g_all, mine in zip(landed, mine1)]
        return rest["landed"][0].reshape(-1, w_out.shape[2])

    def params_l0(_):
        w_in_p = _padded_from_chips([g_in0[p].reshape(w_in_b[0].shape) for p in range(N_CHIPS)])
        return _layer_params(0, w_in_p, w_out_l0, *small_full(0), w)

    def params_l1(_):
        _, g_in1, g_out1 = rest["landed"]
        w_in_p = _padded_from_chips([g_in1[p] for p in range(N_CHIPS)])
        return _layer_params(1, w_in_p, g_out1.reshape(-1, g_out1.shape[-1]), *small_full(1), w)

    c = lax.axis_index("c")
    cols = w_in.shape[2]
    rows_out = w_out.shape[1]

    def grad_parts(g):
        dw = g["w_in_p"]
        return [dw.reshape(1, 2, dw.shape[0] // 2, dw.shape[1]),
                g["w_out"].reshape(N_CHIPS, 2, rows_out // 2, D_MODEL)]

    def pair_sums(parts, sib, tag):
        s_in, s_out = [_pair_sum(p, sb, c, MXU_DTYPE, name=f"grad_pair_sum_{k}_{tag}")
                       for k, (p, sb) in enumerate(zip(parts, sib))]
        return [jnp.stack([_chip_part_from_padded(s_in[0], p, cols) for p in range(N_CHIPS)]), s_out]

    split = {"reduced": [lax.empty((depth, 2, w_in.shape[1] // 2, cols), F32),
                         lax.empty((depth, 2, rows_out // 2, D_MODEL), F32)]}

    def chip_sums(landed, sent, li, which=(0, 1)):
        filled = [lax.dynamic_update_index_in_dim(r, lax.dynamic_index_in_dim(sk, me, 0, keepdims=False), me, 0)
                  for r, sk in zip(landed, sent)]
        tag = "".join(str(k) for k in which)
        halves = [_sum_lead(r, split["reduced"][k], li, c, name=f"grad_chip_sum_{k}_l{li}")
                  for k, r in zip(which, filled)]
        for k, buf in zip(which, _pair_gather(halves, li, name=f"grad_pair_gather_{tag}_l{li}")):
            split["reduced"][k] = buf

    def bwd_hooks(li):
        def after_dw(dw_in_p, dw_out):
            parts = grad_parts(dict(w_in_p=dw_in_p, w_out=dw_out))
            state, token = _split_start(parts, "swap", None, name=f"grad_swap_l{li}_start")
            if li > 0:
                split[f"swap{li}"] = (parts, state)
                return token
            mine, sib = _split_wait(state, len(parts), "swap", token, name="grad_swap_l0_wait")
            split["scatter0"], token = _split_start(pair_sums(mine, sib, "l0"), "scatter", None,
                                                    name="grad_scatter_l0_start")
            return token

        hooks = {"after_dw": after_dw}
        if li == depth - 2:
            parts, swap_state = split[f"swap{depth - 1}"]

            def after_dycat(dycat):
                mine, sib = _split_wait(swap_state, len(parts), "swap", dycat, name="grad_swap_l1_wait")
                sent = pair_sums(mine, sib, "l1")
                split["scatter"], token = _split_start(sent, "scatter", None, name="grad_scatter_l1_start")
                return token

            def after_attn(dproj):
                sent, landed = _split_wait(split["scatter"], len(parts), "scatter", dproj,
                                           name="grad_scatter_l1_wait")
                chip_sums(landed, sent, depth - 1)

            hooks.update(after_dycat=after_dycat, after_attn=after_attn)
        return hooks

    loss, grad_x, grads, dfinal = _local_step(x, loss_target, [params_l0, params_l1], final_norm_w,
                                              first_after=token1, bwd_hooks=bwd_hooks)

    small_list = [grads[li][n] for li in range(depth) for n in SMALL]
    small_list += [grads[li][n] for li in range(depth) for n in ("ssd_conv_w", "conf_dw_w")]
    small_list += [dfinal, loss.reshape(1)]
    small_shapes = [a.shape for a in small_list]
    reduced = _unpack(_allreduce_small(_pack(small_list), name="allreduce_small"), small_shapes)
    ns = len(SMALL)
    g = {n: jnp.stack([reduced[li * ns + i] for li in range(depth)]) for i, n in enumerate(SMALL)}
    conv_w_cols, dw_w_cols = ssd_conv_w.shape[2], conf_dw_w.shape[2]
    g["ssd_conv_w"] = jnp.stack([lax.dynamic_slice_in_dim(reduced[depth * ns + 2 * li], me * conv_w_cols,
                                                          conv_w_cols, axis=1) for li in range(depth)])
    g["conf_dw_w"] = jnp.stack([lax.dynamic_slice_in_dim(reduced[depth * ns + 2 * li + 1], me * dw_w_cols,
                                                         dw_w_cols, axis=1) for li in range(depth)])
    g["final_norm_w"] = reduced[-2]
    loss_total = reduced[-1][0]

    small_names = [n for n in WEIGHTS if n not in ("w_in", "w_out")]

    def as2d(a):
        return a.reshape(1, -1) if a.ndim == 1 else a

    deltas, new_ms, new_vs = _adam_small(*[[as2d(src[n]) for n in small_names] for src in (w, g, m, v)],
                                         name="adam_small")

    sent0, landed0 = _split_wait(split["scatter0"], 2, "scatter", deltas[0], name="grad_scatter_l0_wait")
    chip_sums(landed0, sent0, 0)
    g_w_in = split["reduced"][0].reshape(w_in.shape)
    g_w_out = split["reduced"][1].reshape(w_out.shape)
    outs_g, outs_d, outs_m, outs_v = {"w_in": g_w_in, "w_out": g_w_out}, {}, {}, {}
    outs_d["w_out"], outs_m["w_out"], outs_v["w_out"] = _adam_big(w_out, g_w_out, m_w_out, v_w_out,
                                                                  name="adam_w_out")
    to_cols, from_cols = (2, 0, 1), (1, 2, 0)
    outs_d["w_in"], outs_m["w_in"], outs_v["w_in"] = [
        jnp.transpose(a, from_cols) for a in _adam_cols_major(
            *[jnp.transpose(a, to_cols) for a in (w_in, g_w_in, m_w_in, v_w_in)], name="adam_w_in")]
    for n, dn, mn, vn in zip(small_names, deltas, new_ms, new_vs):
        outs_g[n], outs_d[n], outs_m[n], outs_v[n] = (g[n], dn.reshape(w[n].shape), mn.reshape(w[n].shape),
                                                      vn.reshape(w[n].shape))
    return (loss_total, grad_x, *[outs_g[n] for n in WEIGHTS], *[outs_d[n] for n in WEIGHTS],
            *[outs_m[n] for n in WEIGHTS], *[outs_v[n] for n in WEIGHTS])
```

```python
import functools
import math

import jax
import jax.numpy as jnp
import numpy as np
from jax import lax
from jax.experimental import pallas as pl
from jax.experimental.pallas import tpu as pltpu

F32 = jnp.float32
BF16 = jnp.bfloat16
MXU_DTYPE = BF16

D_MODEL = 1024
DEPTH = 2
SSD_HEADS = 16
SSD_HEAD_DIM = 64
SSD_STATE = 128
SSD_CONV = 4
CHUNK = 128
SSD_CONV_DIM = 1536
ATTN_HEAD_DIM = 64
ATTN_Q_HEADS = 8
WINDOW = 128
CONF_WIDTH = 512
CONF_KERNEL = 31
MIX_WIDTH = 2048
D_IN_PROJ = 5392
EPS = 1e-5

ADAM_LR = 0.001
ADAM_B1 = 0.9
ADAM_B2 = 0.999
ADAM_EPS = 1e-08
ADAM_WD = 0.01
ADAM_STEP = 10

LANES = 128
SUBLANES = 8
VMEM_LIMIT = 48 * 1024 * 1024

NP = 5632
OFF_ZA, OFF_Q, OFF_K, OFF_V, OFF_DT = 0, 512, 1024, 1152, 1280
ATTN_GROUP = 1536
OFF_CONF, OFF_ZC = 1536, 2560
CONF_GROUP = 1536
OFF_ZS = 3072
OFF_XBC = 4096
SECTIONS = ((0, 1024, OFF_ZS), (1024, 1536, OFF_ZA), (1536, 2048, OFF_ZC), (2048, 3584, OFF_XBC),
            (3584, 3600, OFF_DT), (3600, 4368, OFF_Q), (4368, 5392, OFF_CONF))

YCAT_ATTN, YCAT_CONF = 1024, 1536
ANY = pl.BlockSpec(memory_space=pl.ANY)

NN = (((1,), (0,)), ((), ()))
NT = (((1,), (1,)), ((), ()))
TN = (((0,), (0,)), ((), ()))


def _params(sem):
    return pltpu.CompilerParams(dimension_semantics=sem, vmem_limit_bytes=VMEM_LIMIT)


def _dot(a, b, dims=NN):
    return lax.dot_general(a.astype(MXU_DTYPE), b.astype(MXU_DTYPE), dims, preferred_element_type=F32)


def _split_bf16(a, passes):
    pieces = []
    r = a
    for _ in range(passes):
        p = r.astype(BF16)
        pieces.append(p)
        r = r - p.astype(F32)
    return pieces


def _xdot(a, sel, dims=NN, passes=2):
    out = None
    for p in _split_bf16(a, passes):
        t = lax.dot_general(p, sel, dims, preferred_element_type=F32)
        out = t if out is None else out + t
    return out


def _xdot_r(sel, b, dims=NN, passes=3):
    out = None
    for p in _split_bf16(b, passes):
        t = lax.dot_general(sel, p, dims, preferred_element_type=F32)
        out = t if out is None else out + t
    return out


def _sigmoid(x):
    return 1.0 / (1.0 + jnp.exp(-x))


def _silu(x):
    return x * _sigmoid(x)


def _dsilu(x):
    s = _sigmoid(x)
    return s * (1.0 + x * (1.0 - s))


def _softplus(x):
    return jnp.maximum(x, 0.0) + jnp.log(1.0 + jnp.exp(-jnp.abs(x)))


def _rowsum8(x):
    r, c = x.shape
    return jnp.sum(x.reshape(r // SUBLANES, SUBLANES, c), axis=0)


def _iota(shape, dim):
    return lax.broadcasted_iota(jnp.int32, shape, dim)


def _matmul(a, b, form, out_dtype, tm, tn, tk, name, residual=None, after=None):
    if form == "nn":
        (m, k), n = a.shape, b.shape[1]
    elif form == "nt":
        (m, k), n = a.shape, b.shape[0]
    else:
        (k, m), n = a.shape, b.shape[1]
    tm, tn, tk = min(tm, m), min(tn, n), min(tk, k)
    assert m % tm == 0 and n % tn == 0 and k % tk == 0, (name, m, n, k, tm, tn, tk)
    if form == "nn":
        a_spec = pl.BlockSpec((tm, tk), lambda i, j, s: (i, s))
        b_spec = pl.BlockSpec((tk, tn), lambda i, j, s: (s, j))
        dims = NN
    elif form == "nt":
        (m, k), n = a.shape, b.shape[0]
        a_spec = pl.BlockSpec((tm, tk), lambda i, j, s: (i, s))
        b_spec = pl.BlockSpec((tn, tk), lambda i, j, s: (j, s))
        dims = NT
    else:
        (k, m), n = a.shape, b.shape[1]
        a_spec = pl.BlockSpec((tk, tm), lambda i, j, s: (s, i))
        b_spec = pl.BlockSpec((tk, tn), lambda i, j, s: (s, j))
        dims = TN
    nk = k // tk
    has_res = residual is not None
    deps = [] if after is None else [after]

    def body_single(a_ref, b_ref, *rest):
        o = _dot(a_ref[...], b_ref[...], dims)
        if has_res:
            o = o + rest[0][...]
        rest[-1][...] = o.astype(out_dtype)

    def body(a_ref, b_ref, *rest):
        r_ref = rest[0] if has_res else None
        o_ref, acc = rest[-2:]
        s = pl.program_id(2)

        @pl.when(s == 0)
        def _():
            acc[...] = jnp.zeros_like(acc)

        acc[...] += _dot(a_ref[...], b_ref[...], dims)

        @pl.when(s == nk - 1)
        def _():
            o = acc[...]
            if has_res:
                o = o + r_ref[...]
            o_ref[...] = o.astype(out_dtype)

    in_specs = [a_spec, b_spec]
    args = [a, b]
    if has_res:
        in_specs.append(pl.BlockSpec((tm, tn), lambda i, j, s: (i, j)))
        args.append(residual)
    in_specs += [ANY] * len(deps)
    args += deps
    return pl.pallas_call(
        body_single if nk == 1 else body, name=name,
        out_shape=jax.ShapeDtypeStruct((m, n), out_dtype),
        grid=(m // tm, n // tn, nk),
        in_specs=in_specs,
        out_specs=pl.BlockSpec((tm, tn), lambda i, j, s: (i, j)),
        scratch_shapes=[] if nk == 1 else [pltpu.VMEM((tm, tn), F32)],
        compiler_params=_params(("parallel", "parallel", "arbitrary")),
    )(*args)


ROW_TILE = 256


PROJ_FWD_TM, PROJ_FWD_TN = 1024, 512


def _proj_fwd(x, w, w_in_p, name, after=None):
    t, d = x.shape
    n = w_in_p.shape[1]
    tm, tn = min(PROJ_FWD_TM, t), PROJ_FWD_TN
    assert t % tm == 0 and n % tn == 0
    deps = [] if after is None else [after]

    def body(x_ref, w_ref, b_ref, *rest):
        o_ref, ot_ref, h_scr = rest[len(deps):]

        @pl.when(pl.program_id(1) == 0)
        def _():
            xv = x_ref[...]
            rstd = lax.rsqrt(jnp.mean(xv * xv, axis=-1, keepdims=True) + EPS)
            h = xv * rstd * w_ref[...]
            h_scr[...] = h.astype(h_scr.dtype)
            ot_ref[...] = h.T.astype(ot_ref.dtype)

        o_ref[...] = _dot(h_scr[...], b_ref[...])

    return pl.pallas_call(
        body, name=name,
        out_shape=(jax.ShapeDtypeStruct((t, n), F32), jax.ShapeDtypeStruct((d, t), MXU_DTYPE)),
        grid=(t // tm, n // tn),
        in_specs=[pl.BlockSpec((tm, d), lambda i, j: (i, 0)), pl.BlockSpec((1, d), lambda i, j: (0, 0)),
                  pl.BlockSpec((d, tn), lambda i, j: (0, j))] + [ANY] * len(deps),
        out_specs=(pl.BlockSpec((tm, tn), lambda i, j: (i, j)), pl.BlockSpec((d, tm), lambda i, j: (0, i))),
        scratch_shapes=[pltpu.VMEM((tm, d), MXU_DTYPE)],
        compiler_params=_params(("parallel", "arbitrary")),
    )(x, w, w_in_p, *deps)


PROJ_BWD_TM, PROJ_BWD_TK = 1024, 1408


def _proj_bwd_dx(dproj, w_in_p, x, w, dres, name):
    t, d = x.shape
    kdim = dproj.shape[1]
    tm, tk = min(PROJ_BWD_TM, t), PROJ_BWD_TK
    nt, nk = t // tm, kdim // tk
    assert t % tm == 0 and kdim % tk == 0

    def body(a_ref, b_ref, x_ref, w_ref, dr_ref, dx_ref, dw_ref, acc, wacc):
        i, s = pl.program_id(0), pl.program_id(1)

        @pl.when((i == 0) & (s == 0))
        def _():
            wacc[...] = jnp.zeros_like(wacc)

        @pl.when(s == 0)
        def _():
            acc[...] = jnp.zeros_like(acc)

        acc[...] += _dot(a_ref[...], b_ref[...], NT)

        @pl.when(s == nk - 1)
        def _():
            xv = x_ref[...]
            rstd = lax.rsqrt(jnp.mean(xv * xv, axis=-1, keepdims=True) + EPS)
            xh = xv * rstd
            dhv = acc[...]
            g = dhv * w_ref[...]
            dx_ref[...] = dr_ref[...] + rstd * (g - xh * jnp.mean(g * xh, axis=-1, keepdims=True))
            wacc[...] += _rowsum8(dhv * xh)

        @pl.when((i == nt - 1) & (s == nk - 1))
        def _():
            dw_ref[...] = jnp.sum(wacc[...], axis=0, keepdims=True)

    row = pl.BlockSpec((tm, d), lambda i, s: (i, 0))
    vec = pl.BlockSpec((1, d), lambda i, s: (0, 0))
    return pl.pallas_call(
        body, name=name,
        out_shape=(jax.ShapeDtypeStruct((t, d), F32), jax.ShapeDtypeStruct((1, d), F32)),
        grid=(nt, nk),
        in_specs=[pl.BlockSpec((tm, tk), lambda i, s: (i, s)), pl.BlockSpec((d, tk), lambda i, s: (0, s)),
                  row, vec, row],
        out_specs=(row, vec),
        scratch_shapes=[pltpu.VMEM((tm, d), F32), pltpu.VMEM((SUBLANES, d), F32)],
        compiler_params=_params(("arbitrary", "arbitrary")),
    )(dproj, w_in_p, x, w, dres)


def _loss_head(xf, target, w, name):
    t, d = xf.shape
    tm = ROW_TILE
    nt = t // tm

    def body(x_ref, t_ref, w_ref, loss_ref, dx_ref, dw_ref, lacc, wacc):
        i = pl.program_id(0)

        @pl.when(i == 0)
        def _():
            lacc[...] = jnp.zeros_like(lacc)
            wacc[...] = jnp.zeros_like(wacc)

        xv = x_ref[...]
        rstd = lax.rsqrt(jnp.mean(xv * xv, axis=-1, keepdims=True) + EPS)
        xh = xv * rstd
        err = xh * w_ref[...] - t_ref[...]
        lacc[...] += jnp.sum(err * err)
        dy = err * (1.0 / d)
        g = dy * w_ref[...]
        dx_ref[...] = rstd * (g - xh * jnp.mean(g * xh, axis=-1, keepdims=True))
        wacc[...] += _rowsum8(dy * xh)

        @pl.when(i == nt - 1)
        def _():
            loss_ref[...] = lacc[...] * (0.5 / d)
            dw_ref[...] = jnp.sum(wacc[...], axis=0, keepdims=True)

    row = pl.BlockSpec((tm, d), lambda i: (i, 0))
    vec = pl.BlockSpec((1, d), lambda i: (0, 0))
    return pl.pallas_call(
        body, name=name,
        out_shape=(jax.ShapeDtypeStruct((SUBLANES, LANES), F32), jax.ShapeDtypeStruct((t, d), F32),
                   jax.ShapeDtypeStruct((1, d), F32)),
        grid=(nt,),
        in_specs=[row, row, vec],
        out_specs=(pl.BlockSpec((SUBLANES, LANES), lambda i: (0, 0)), row, vec),
        scratch_shapes=[pltpu.VMEM((SUBLANES, LANES), F32), pltpu.VMEM((SUBLANES, d), F32)],
        compiler_params=_params(("arbitrary",)),
    )(xf, target, w)


CONV_TILE = 512
CONV_TILE_SHORT = 1024
CONV_COLS = 512
CONV_SUB_ROWS = 128
CONV_SUB_COLS = LANES


def _conv_halo(k):
    return SUBLANES if k - 1 <= SUBLANES else 32


def _conv_tile(k, t):
    return min(CONV_TILE if _conv_use_shifted(k) else CONV_TILE_SHORT, t)


def _conv_subtiles(tm, cw):
    return [(r0, c0) for r0 in range(0, tm, CONV_SUB_ROWS) for c0 in range(0, cw, CONV_SUB_COLS)]


def _conv_use_shifted(k):
    return k > SUBLANES


def _conv_shift_scratch(k, rows, cw):
    return [pltpu.VMEM((SUBLANES - 1, rows - SUBLANES, cw), F32)] if _conv_use_shifted(k) else []


def _conv_fill_shifted(ext, sh):
    n = sh.shape[1]
    for b in range(1, SUBLANES):
        sh[b - 1] = ext[b:b + n, :]


def _conv_rows(ext, sh, start, rows, cs):
    b = start % SUBLANES
    if b == 0 or not sh:
        return ext[start:start + rows, cs]
    return sh[0][b - 1, start - b:start - b + rows, cs]


def _conv_fwd(src, col0, width, w, bias, k, seq, name):
    t = src.shape[0]
    tm, cw, halo = _conv_tile(k, src.shape[0]), CONV_COLS, _conv_halo(k)
    sr, sc = CONV_SUB_ROWS, CONV_SUB_COLS
    p = k - 1
    cb0 = col0 // cw
    kp = w.shape[0]

    shifted = _conv_use_shifted(k)

    def body(x_ref, h_ref, w_ref, b_ref, o_ref, ext, *sh):
        i = pl.program_id(0)
        seq_start = (i * tm) % seq == 0
        ext[halo:, :] = x_ref[...]
        ext[:halo, :] = jnp.where(seq_start, 0.0, h_ref[...])
        if shifted:
            _conv_fill_shifted(ext, sh[0])
        for r0, c0 in _conv_subtiles(tm, cw):
            cs = slice(c0, c0 + sc)
            acc = jnp.zeros((sr, sc), F32) + b_ref[:, cs]
            for j in range(k):
                acc = acc + w_ref[j:j + 1, cs] * _conv_rows(ext, sh, r0 + halo - p + j, sr, cs)
            o_ref[r0:r0 + sr, cs] = acc

    return pl.pallas_call(
        body, name=name,
        out_shape=jax.ShapeDtypeStruct((t, width), F32),
        grid=(t // tm, width // cw),
        in_specs=[pl.BlockSpec((tm, cw), lambda i, j: (i, cb0 + j)),
                  pl.BlockSpec((halo, cw), lambda i, j: (jnp.maximum(i * (tm // halo) - 1, 0), cb0 + j)),
                  pl.BlockSpec((kp, cw), lambda i, j: (0, j)),
                  pl.BlockSpec((1, cw), lambda i, j: (0, j))],
        out_specs=pl.BlockSpec((tm, cw), lambda i, j: (i, j)),
        scratch_shapes=[pltpu.VMEM((halo + tm, cw), F32)] + _conv_shift_scratch(k, halo + tm, cw),
        compiler_params=_params(("parallel", "parallel")),
    )(src, src, w, bias)


def _conv_bwd(dy, src, col0, width, w, k, seq, name, into=None):
    t = src.shape[0]
    tm, cw, halo = _conv_tile(k, src.shape[0]), CONV_COLS, _conv_halo(k)
    sr, sc = CONV_SUB_ROWS, CONV_SUB_COLS
    p = k - 1
    cb0 = col0 // cw
    kp = w.shape[0]
    nt = t // tm
    last_halo = t // halo - 1

    shifted = _conv_use_shifted(k)

    def body(dy_ref, dn_ref, x_ref, xp_ref, w_ref, *rest):
        if into is not None:
            rest = rest[1:]
        dx_ref, dw_ref, db_ref, dyext, xext, wacc, bacc = rest[:7]
        sh = rest[7:]
        i = pl.program_id(1)
        dysh, xsh = (sh[:1], sh[1:]) if shifted else ((), ())

        @pl.when(i == 0)
        def _():
            wacc[...] = jnp.zeros_like(wacc)
            bacc[...] = jnp.zeros_like(bacc)

        seq_start = (i * tm) % seq == 0
        seq_end = ((i + 1) * tm) % seq == 0
        dyext[:tm, :] = dy_ref[...]
        dyext[tm:, :] = jnp.where(seq_end, 0.0, dn_ref[...])
        xext[halo:, :] = x_ref[...]
        xext[:halo, :] = jnp.where(seq_start, 0.0, xp_ref[...])
        if shifted:
            _conv_fill_shifted(dyext, dysh[0])
            _conv_fill_shifted(xext, xsh[0])
        for r0, c0 in _conv_subtiles(tm, cw):
            cs = slice(c0, c0 + sc)
            dyv = dy_ref[r0:r0 + sr, cs]
            acc = jnp.zeros((sr, sc), F32)
            for j in range(k):
                acc = acc + w_ref[j:j + 1, cs] * _conv_rows(dyext, dysh, r0 + p - j, sr, cs)
                wacc[j, :, cs] += _rowsum8(dyv * _conv_rows(xext, xsh, r0 + halo - p + j, sr, cs))
            dx_ref[r0:r0 + sr, cs] = acc.astype(dx_ref.dtype)
            bacc[:, cs] += _rowsum8(dyv)

        @pl.when(i == nt - 1)
        def _():
            dw_ref[...] = jnp.zeros_like(dw_ref)
            for j in range(k):
                dw_ref[j:j + 1, :] = jnp.sum(wacc[j], axis=0, keepdims=True)
            db_ref[...] = jnp.sum(bacc[...], axis=0, keepdims=True)

    if into is None:
        dx_shape = jax.ShapeDtypeStruct((t, width), F32)
        dx_spec = pl.BlockSpec((tm, cw), lambda j, i: (i, j))
        extra_specs, extra_args, aliases = [], [], {}
    else:
        dx_shape = jax.ShapeDtypeStruct(into.shape, into.dtype)
        dx_spec = pl.BlockSpec((tm, cw), lambda j, i: (i, cb0 + j))
        extra_specs, extra_args, aliases = [ANY], [into], {5: 0}
    return pl.pallas_call(
        body, name=name,
        out_shape=(dx_shape, jax.ShapeDtypeStruct((kp, width), F32), jax.ShapeDtypeStruct((1, width), F32)),
        grid=(width // cw, nt),
        in_specs=[pl.BlockSpec((tm, cw), lambda j, i: (i, j)),
                  pl.BlockSpec((halo, cw), lambda j, i: (jnp.minimum((i + 1) * (tm // halo), last_halo), j)),
                  pl.BlockSpec((tm, cw), lambda j, i: (i, cb0 + j)),
                  pl.BlockSpec((halo, cw), lambda j, i: (jnp.maximum(i * (tm // halo) - 1, 0), cb0 + j)),
                  pl.BlockSpec((kp, cw), lambda j, i: (0, j))] + extra_specs,
        out_specs=(dx_spec,
                   pl.BlockSpec((kp, cw), lambda j, i: (0, j)),
                   pl.BlockSpec((1, cw), lambda j, i: (0, j))),
        input_output_aliases=aliases,
        scratch_shapes=[pltpu.VMEM((tm + halo, cw), F32), pltpu.VMEM((halo + tm, cw), F32),
                        pltpu.VMEM((kp, SUBLANES, cw), F32), pltpu.VMEM((SUBLANES, cw), F32)]
        + 2 * _conv_shift_scratch(k, halo + tm, cw),
        compiler_params=_params(("parallel", "arbitrary")),
    )(dy, dy, src, src, w, *extra_args)


def _conf_specs(tm, cw, halo, order):
    cb = OFF_CONF // cw

    def blk(col):
        return pl.BlockSpec((tm, cw), lambda *g: (order(*g), col))

    def prev(col):
        return pl.BlockSpec((halo, cw), lambda *g: (jnp.maximum(order(*g) * (tm // halo) - 1, 0), col))

    return blk(cb), prev(cb), blk(cb + 1), prev(cb + 1)


def _glu_window(ext, a_ref, ah_ref, g_ref, gh_ref, seq_start, halo):
    ext[halo:, :] = a_ref[...] * _sigmoid(g_ref[...])
    ext[:halo, :] = jnp.where(seq_start, 0.0, ah_ref[...] * _sigmoid(gh_ref[...]))


def _conf_fwd(proj, w, bias, ln_w, ln_b, ycat, seq, name):
    t = proj.shape[0]
    k = CONF_KERNEL
    tm, cw, halo = CONV_TILE, CONF_WIDTH, _conv_halo(k)
    sr, sc = CONV_SUB_ROWS, CONV_SUB_COLS
    p = k - 1
    kp = w.shape[0]

    def body(a_ref, ah_ref, g_ref, gh_ref, z_ref, w_ref, b_ref, lw_ref, lb_ref, _, c1_ref, y_ref, ext, sh):
        i = pl.program_id(0)
        _glu_window(ext, a_ref, ah_ref, g_ref, gh_ref, (i * tm) % seq == 0, halo)
        _conv_fill_shifted(ext, sh)
        for r0, c0 in _conv_subtiles(tm, cw):
            cs = slice(c0, c0 + sc)
            acc = jnp.zeros((sr, sc), F32) + b_ref[:, cs]
            for j in range(k):
                acc = acc + w_ref[j:j + 1, cs] * _conv_rows(ext, (sh,), r0 + halo - p + j, sr, cs)
            c1_ref[r0:r0 + sr, cs] = acc
        for r0 in range(0, tm, sr):
            rows = slice(r0, r0 + sr)
            cv = c1_ref[rows, :]
            xc = cv - jnp.mean(cv, axis=-1, keepdims=True)
            rstd = lax.rsqrt(jnp.mean(xc * xc, axis=-1, keepdims=True) + EPS)
            c2 = xc * rstd * lw_ref[...] + lb_ref[...]
            y_ref[rows, :] = (_silu(c2) * _silu(z_ref[rows, :])).astype(y_ref.dtype)

    vec = pl.BlockSpec((1, cw), lambda i: (0, 0))
    row = pl.BlockSpec((tm, cw), lambda i: (i, 0))
    return pl.pallas_call(
        body, name=name,
        out_shape=(jax.ShapeDtypeStruct((t, cw), F32), jax.ShapeDtypeStruct(ycat.shape, ycat.dtype)),
        grid=(t // tm,),
        in_specs=[*_conf_specs(tm, cw, halo, lambda i: i),
                  pl.BlockSpec((tm, cw), lambda i: (i, OFF_ZC // cw)),
                  pl.BlockSpec((kp, cw), lambda i: (0, 0)), vec, vec, vec, ANY],
        out_specs=(row, pl.BlockSpec((tm, cw), lambda i: (i, YCAT_CONF // cw))),
        input_output_aliases={9: 1},
        scratch_shapes=[pltpu.VMEM((halo + tm, cw), F32)] + _conv_shift_scratch(k, halo + tm, cw),
        compiler_params=_params(("parallel",)),
    )(proj, proj, proj, proj, proj, w, bias, ln_w, ln_b, ycat)


def _conf_bwd(dycat, proj, c1, w, ln_w, ln_b, dproj, seq, name):
    t = proj.shape[0]
    k = CONF_KERNEL
    tm, cw, halo = CONV_TILE, CONF_WIDTH, _conv_halo(k)
    sr, sc = CONV_SUB_ROWS, CONV_SUB_COLS
    p = k - 1
    kp = w.shape[0]
    nt = t // tm
    last_halo = t // halo - 1

    def body(dy_ref, dyn_ref, c_ref, cn_ref, z_ref, zn_ref, a_ref, ah_ref, g_ref, gh_ref, w_ref, lw_ref, lb_ref, _,
             grp_ref, dw_ref, db_ref, dlw_ref, dlb_ref, dyext, xext, wacc, bacc, lwacc, lbacc, dysh, xsh):
        i = pl.program_id(0)

        @pl.when(i == 0)
        def _():
            wacc[...] = jnp.zeros_like(wacc)
            bacc[...] = jnp.zeros_like(bacc)
            lwacc[...] = jnp.zeros_like(lwacc)
            lbacc[...] = jnp.zeros_like(lbacc)

        def post_bwd(dy, cv, zv):
            xc = cv - jnp.mean(cv, axis=-1, keepdims=True)
            rstd = lax.rsqrt(jnp.mean(xc * xc, axis=-1, keepdims=True) + EPS)
            xh = xc * rstd
            c2 = xh * lw_ref[...] + lb_ref[...]
            dz = dy * _silu(c2) * _dsilu(zv)
            dc2 = dy * _silu(zv) * _dsilu(c2)
            dxh = dc2 * lw_ref[...]
            dc = rstd * (dxh - jnp.mean(dxh, axis=-1, keepdims=True)
                         - xh * jnp.mean(dxh * xh, axis=-1, keepdims=True))
            return dc, dz, dc2 * xh, dc2

        seq_end = ((i + 1) * tm) % seq == 0
        for r0 in range(0, tm, sr):
            rows = slice(r0, r0 + sr)
            dc, dz, lw_terms, lb_terms = post_bwd(dy_ref[rows, :], c_ref[rows, :], z_ref[rows, :])
            dyext[rows, :] = dc
            grp_ref[rows, 2 * cw:] = dz.astype(grp_ref.dtype)
            lwacc[...] += _rowsum8(lw_terms)
            lbacc[...] += _rowsum8(lb_terms)
        dc_next = post_bwd(dyn_ref[...], cn_ref[...], zn_ref[...])[0]
        dyext[tm:, :] = jnp.where(seq_end, 0.0, dc_next)
        _glu_window(xext, a_ref, ah_ref, g_ref, gh_ref, (i * tm) % seq == 0, halo)
        _conv_fill_shifted(dyext, dysh)
        _conv_fill_shifted(xext, xsh)
        dag_ref = grp_ref
        for r0, c0 in _conv_subtiles(tm, cw):
            cs = slice(c0, c0 + sc)
            rows = slice(r0, r0 + sr)
            dyv = dyext[rows, cs]
            acc = jnp.zeros((sr, sc), F32)
            for j in range(k):
                acc = acc + w_ref[j:j + 1, cs] * _conv_rows(dyext, (dysh,), r0 + p - j, sr, cs)
                wacc[j, :, cs] += _rowsum8(dyv * _conv_rows(xext, (xsh,), r0 + halo - p + j, sr, cs))
            bacc[:, cs] += _rowsum8(dyv)
            s = _sigmoid(g_ref[rows, cs])
            dag_ref[rows, cs] = (acc * s).astype(dag_ref.dtype)
            dag_ref[rows, cw + c0:cw + c0 + sc] = (acc * a_ref[rows, cs] * s * (1.0 - s)).astype(dag_ref.dtype)

        @pl.when(i == nt - 1)
        def _():
            dw_ref[...] = jnp.zeros_like(dw_ref)
            for j in range(k):
                dw_ref[j:j + 1, :] = jnp.sum(wacc[j], axis=0, keepdims=True)
            db_ref[...] = jnp.sum(bacc[...], axis=0, keepdims=True)
            dlw_ref[...] = jnp.sum(lwacc[...], axis=0, keepdims=True)
            dlb_ref[...] = jnp.sum(lbacc[...], axis=0, keepdims=True)

    def blk(col):
        return pl.BlockSpec((tm, cw), lambda i: (i, col))

    def nxt(col):
        return pl.BlockSpec((halo, cw), lambda i: (jnp.minimum((i + 1) * (tm // halo), last_halo), col))

    vec = pl.BlockSpec((1, cw), lambda i: (0, 0))
    return pl.pallas_call(
        body, name=name,
        out_shape=(jax.ShapeDtypeStruct(dproj.shape, dproj.dtype), jax.ShapeDtypeStruct((kp, cw), F32),
                   jax.ShapeDtypeStruct((1, cw), F32), jax.ShapeDtypeStruct((1, cw), F32),
                   jax.ShapeDtypeStruct((1, cw), F32)),
        grid=(nt,),
        in_specs=[blk(YCAT_CONF // cw), nxt(YCAT_CONF // cw), blk(0), nxt(0), blk(OFF_ZC // cw), nxt(OFF_ZC // cw),
                  *_conf_specs(tm, cw, halo, lambda i: i),
                  pl.BlockSpec((kp, cw), lambda i: (0, 0)), vec, vec, ANY],
        out_specs=(pl.BlockSpec((tm, CONF_GROUP), lambda i: (i, OFF_CONF // CONF_GROUP)),
                   pl.BlockSpec((kp, cw), lambda i: (0, 0)), vec, vec, vec),
        input_output_aliases={13: 0},
        scratch_shapes=[pltpu.VMEM((tm + halo, cw), F32), pltpu.VMEM((halo + tm, cw), F32),
                        pltpu.VMEM((kp, SUBLANES, cw), F32), pltpu.VMEM((SUBLANES, cw), F32),
                        pltpu.VMEM((SUBLANES, cw), F32), pltpu.VMEM((SUBLANES, cw), F32)]
        + 2 * _conv_shift_scratch(k, halo + tm, cw),
        compiler_params=_params(("arbitrary",)),
    )(dycat, dycat, c1, c1, proj, proj, proj, proj, proj, proj, w, ln_w, ln_b, dproj)


def _half_mask(half):
    lane = _iota((1, LANES), 1)
    return ((lane >= half * ATTN_HEAD_DIM) & (lane < (half + 1) * ATTN_HEAD_DIM)).astype(F32)


def _stack_heads(xp, g):
    m = _half_mask(g)
    swapped = pltpu.roll(xp, ATTN_HEAD_DIM, axis=1)
    return jnp.concatenate([xp * m, swapped * m] if g == 0 else [swapped * m, xp * m], axis=0)


def _unstack_heads(both, g):
    w = both.shape[0] // 2
    top, bot = both[:w], both[w:]
    lo, hi = _half_mask(0), _half_mask(1)
    if g == 0:
        return top * lo + pltpu.roll(bot, ATTN_HEAD_DIM, axis=1) * hi
    return pltpu.roll(top, ATTN_HEAD_DIM, axis=1) * lo + bot * hi


def _band_mask(first_block):
    w = WINDOW
    qi = _iota((w, 2 * w), 0)
    kj = _iota((w, 2 * w), 1) - w
    rel = qi - kj
    return (rel >= 0) & (rel < w) & (jnp.logical_not(first_block) | (kj >= 0))


def _lane_pick(x, h):
    return jnp.sum(jnp.where(_iota(x.shape, 1) == h, x, 0.0), axis=1, keepdims=True)


def _attn_specs(nb, rev):
    w = WINDOW

    def blk(i):
        return nb - 1 - i if rev else i

    def row(b, i):
        return b * nb + blk(i)

    def prow(b, i):
        return b * nb + jnp.maximum(blk(i) - 1, 0)

    q = pl.BlockSpec((w, 512), lambda b, i: (row(b, i), OFF_Q // 512))
    kc = pl.BlockSpec((w, 128), lambda b, i: (row(b, i), OFF_K // 128))
    kp = pl.BlockSpec((w, 128), lambda b, i: (prow(b, i), OFF_K // 128))
    vc = pl.BlockSpec((w, 128), lambda b, i: (row(b, i), OFF_V // 128))
    vp = pl.BlockSpec((w, 128), lambda b, i: (prow(b, i), OFF_V // 128))
    z = pl.BlockSpec((w, 512), lambda b, i: (row(b, i), OFF_ZA // 512))
    return q, kc, kp, vc, vp, z, row


def _attn_fwd(proj, sinks, ycat, nbatch, name):
    t = proj.shape[0]
    w = WINDOW
    nb = t // nbatch // w
    scale = ATTN_HEAD_DIM ** -0.5
    q_s, kc_s, kp_s, vc_s, vp_s, z_s, row = _attn_specs(nb, False)

    def body(q_ref, kc_ref, kp_ref, vc_ref, vp_ref, z_ref, sk_ref, _, y_ref, o_ref, lse_ref):
        first = pl.program_id(1) == 0
        mask = _band_mask(first)
        kk = jnp.concatenate([kp_ref[...], kc_ref[...]], axis=0).astype(MXU_DTYPE)
        vv = jnp.concatenate([vp_ref[...], vc_ref[...]], axis=0).astype(MXU_DTYPE)
        sk = sk_ref[...]
        lane = _iota((w, LANES), 1)
        mask2 = jnp.concatenate([mask, mask], axis=0)
        scores = [_dot(_stack_heads(q_ref[:, j * LANES:(j + 1) * LANES], j // 2), kk, NT) for j in range(4)]
        lse_all = jnp.zeros((w, LANES), F32)
        for j in range(4):
            s = jnp.where(mask2, scores[j] * scale, -1e30)
            skc = jnp.concatenate([jnp.broadcast_to(_lane_pick(sk, 2 * j), (w, 1)),
                                   jnp.broadcast_to(_lane_pick(sk, 2 * j + 1), (w, 1))], axis=0)
            m = jnp.maximum(jnp.max(s, axis=1, keepdims=True), skc)
            den = jnp.sum(jnp.exp(s - m), axis=1, keepdims=True) + jnp.exp(skc - m)
            lse = m + jnp.log(den)
            lse_all = jnp.where(lane == 2 * j, lse[:w], lse_all)
            lse_all = jnp.where(lane == 2 * j + 1, lse[w:], lse_all)
            op = _unstack_heads(_dot(jnp.exp(s - lse), vv), j // 2)
            cols = slice(j * LANES, (j + 1) * LANES)
            o_ref[:, cols] = op
            y_ref[:, cols] = (op * _silu(z_ref[:, cols])).astype(y_ref.dtype)
        lse_ref[...] = lse_all

    return pl.pallas_call(
        body, name=name,
        out_shape=(jax.ShapeDtypeStruct(ycat.shape, ycat.dtype), jax.ShapeDtypeStruct((t, 512), F32),
                   jax.ShapeDtypeStruct((t, LANES), F32)),
        grid=(nbatch, nb),
        in_specs=[q_s, kc_s, kp_s, vc_s, vp_s, z_s, pl.BlockSpec((1, LANES), lambda b, i: (0, 0)), ANY],
        out_specs=(pl.BlockSpec((w, 512), lambda b, i: (row(b, i), YCAT_ATTN // 512)),
                   pl.BlockSpec((w, 512), lambda b, i: (row(b, i), 0)),
                   pl.BlockSpec((w, LANES), lambda b, i: (row(b, i), 0))),
        input_output_aliases={7: 0},
        compiler_params=_params(("parallel", "parallel")),
    )(proj, proj, proj, proj, proj, proj, sinks, ycat)


def _attn_bwd(dycat, proj, o, lse, sinks, ddt, dproj, nbatch, name):
    t = proj.shape[0]
    w = WINDOW
    nb = t // nbatch // w
    scale = ATTN_HEAD_DIM ** -0.5
    q_s, kc_s, kp_s, vc_s, vp_s, z_s, row = _attn_specs(nb, True)

    def body(dy_ref, q_ref, kc_ref, kp_ref, vc_ref, vp_ref, z_ref, o_ref, lse_ref, sk_ref, ddt_ref, _,
             grp_ref, dsk_ref, kcarry, vcarry, sacc):
        b, i = pl.program_id(0), pl.program_id(1)

        @pl.when((b == 0) & (i == 0))
        def _():
            sacc[...] = jnp.zeros_like(sacc)

        @pl.when(i == 0)
        def _():
            kcarry[...] = jnp.zeros_like(kcarry)
            vcarry[...] = jnp.zeros_like(vcarry)

        first = i == nb - 1
        mask = _band_mask(first)
        kk = jnp.concatenate([kp_ref[...], kc_ref[...]], axis=0).astype(MXU_DTYPE)
        vv = jnp.concatenate([vp_ref[...], vc_ref[...]], axis=0).astype(MXU_DTYPE)
        sk = sk_ref[...]
        lse_all = lse_ref[...]
        lane1 = _iota((1, LANES), 1)
        mask2 = jnp.concatenate([mask, mask], axis=0)
        qs, dos, deltas, lses, scores, dps = [], [], [], [], [], []
        for j in range(4):
            cols = slice(j * LANES, (j + 1) * LANES)
            qp, zp, ov, dy = q_ref[:, cols], z_ref[:, cols], o_ref[:, cols], dy_ref[:, cols]
            grp_ref[:, OFF_ZA + j * LANES:OFF_ZA + (j + 1) * LANES] = (dy * ov * _dsilu(zp)).astype(grp_ref.dtype)
            do = dy * _silu(zp)
            q2 = _stack_heads(qp, j // 2).astype(MXU_DTYPE)
            do2 = _stack_heads(do, j // 2)
            qs.append(q2)
            dos.append(do2.astype(MXU_DTYPE))
            deltas.append(jnp.sum(do2 * _stack_heads(ov, j // 2), axis=1, keepdims=True))
            lses.append(jnp.concatenate([_lane_pick(lse_all, 2 * j), _lane_pick(lse_all, 2 * j + 1)], axis=0))
            scores.append(_dot(q2, kk, NT))
            dps.append(_dot(do2, vv, NT))
        prs, dss = [], []
        dsk = jnp.zeros((1, LANES), F32)
        for j in range(4):
            pr = jnp.exp(jnp.where(mask2, scores[j] * scale, -1e30) - lses[j])
            prs.append(pr.astype(MXU_DTYPE))
            dss.append((pr * (dps[j] - deltas[j])).astype(MXU_DTYPE))
            skc = jnp.concatenate([jnp.broadcast_to(_lane_pick(sk, 2 * j), (w, 1)),
                                   jnp.broadcast_to(_lane_pick(sk, 2 * j + 1), (w, 1))], axis=0)
            sink_term = jnp.exp(skc - lses[j]) * deltas[j]
            dsk = dsk - jnp.where(lane1 == 2 * j, jnp.sum(sink_term[:w]), 0.0)
            dsk = dsk - jnp.where(lane1 == 2 * j + 1, jnp.sum(sink_term[w:]), 0.0)
        dkk = jnp.zeros((2 * w, LANES), F32)
        dvv = jnp.zeros((2 * w, LANES), F32)
        for j in range(4):
            dq = _unstack_heads(_dot(dss[j], kk) * scale, j // 2)
            grp_ref[:, OFF_Q + j * LANES:OFF_Q + (j + 1) * LANES] = dq.astype(grp_ref.dtype)
            dkk = dkk + _dot(dss[j], qs[j], TN) * scale
            dvv = dvv + _dot(prs[j], dos[j], TN)
        grp_ref[:, OFF_K:OFF_K + LANES] = (dkk[w:, :] + kcarry[...]).astype(grp_ref.dtype)
        grp_ref[:, OFF_V:OFF_V + LANES] = (dvv[w:, :] + vcarry[...]).astype(grp_ref.dtype)
        grp_ref[:, OFF_DT:OFF_DT + LANES] = ddt_ref[...].astype(grp_ref.dtype)
        grp_ref[:, OFF_DT + LANES:] = jnp.zeros((w, ATTN_GROUP - OFF_DT - LANES), grp_ref.dtype)
        kcarry[...] = dkk[:w, :]
        vcarry[...] = dvv[:w, :]
        sacc[...] += dsk

        @pl.when((b == nbatch - 1) & (i == nb - 1))
        def _():
            dsk_ref[...] = sacc[...]

    return pl.pallas_call(
        body, name=name,
        out_shape=(jax.ShapeDtypeStruct(dproj.shape, dproj.dtype), jax.ShapeDtypeStruct((1, LANES), F32)),
        grid=(nbatch, nb),
        in_specs=[pl.BlockSpec((w, 512), lambda b, i: (row(b, i), YCAT_ATTN // 512)),
                  q_s, kc_s, kp_s, vc_s, vp_s, z_s,
                  pl.BlockSpec((w, 512), lambda b, i: (row(b, i), 0)),
                  pl.BlockSpec((w, LANES), lambda b, i: (row(b, i), 0)),
                  pl.BlockSpec((1, LANES), lambda b, i: (0, 0)),
                  pl.BlockSpec((w, LANES), lambda b, i: (row(b, i), 0)), ANY],
        out_specs=(pl.BlockSpec((w, ATTN_GROUP), lambda b, i: (row(b, i), 0)),
                   pl.BlockSpec((1, LANES), lambda b, i: (0, 0))),
        input_output_aliases={11: 0},
        scratch_shapes=[pltpu.VMEM((w, LANES), F32), pltpu.VMEM((w, LANES), F32),
                        pltpu.VMEM((1, LANES), F32)],
        compiler_params=_params(("arbitrary", "arbitrary")),
    )(dycat, proj, proj, proj, proj, proj, proj, o, lse, sinks, ddt, dproj)


SSD_WIDTH = SSD_HEADS * SSD_HEAD_DIM
GROUP_ROWS = SSD_WIDTH // 2


def _expand_mat():
    r, c = _iota((LANES, SSD_WIDTH), 0), _iota((LANES, SSD_WIDTH), 1)
    return (r == lax.shift_right_logical(c, 6)).astype(BF16)


def _expand_mat_t():
    r, c = _iota((SSD_WIDTH, LANES), 0), _iota((SSD_WIDTH, LANES), 1)
    return (c == lax.shift_right_logical(r, 6)).astype(BF16)


def _ssd_common(u_ref, dt_ref, dtb_ref, a_ref, stack_broadcasts=False):
    q = CHUNK
    act = _silu(u_ref[...])
    xs = act[:, :SSD_WIDTH]
    bm = act[:, SSD_WIDTH:SSD_WIDTH + 256]
    cm = act[:, SSD_WIDTH + 256:]
    dtp = _softplus(dt_ref[...] + dtb_ref[...])
    a = dtp * a_ref[...]
    tril = (_iota((q, q), 0) >= _iota((q, q), 1)).astype(BF16)
    acs = _xdot_r(tril, a)
    acs_t = acs.T
    e = _expand_mat()
    a_end = jnp.sum(jnp.where(_iota(acs.shape, 0) == q - 1, acs, 0.0), axis=0, keepdims=True)
    if stack_broadcasts:
        spread = _xdot(jnp.concatenate([dtp, acs, a_end - acs], axis=0), e)
        dt_x, ea, dec = spread[:q], jnp.exp(spread[q:2 * q]), jnp.exp(spread[2 * q:])
    else:
        dt_x = _xdot(dtp, e)
        ea = jnp.exp(_xdot(acs, e))
        dec = jnp.exp(_xdot(a_end - acs, e))
    a_end_col = jnp.broadcast_to(_lane_pick(acs_t, q - 1), (LANES, LANES))
    s_scale = jnp.exp(_xdot_r(_expand_mat_t(), a_end_col))
    return act, xs, bm, cm, dtp, acs, acs_t, dt_x, ea, dec, s_scale, tril


def _decay_mat(acs, acs_t, h):
    q = CHUNK
    col = _lane_pick(acs, h)
    rowv = jnp.sum(jnp.where(_iota(acs_t.shape, 0) == h, acs_t, 0.0), axis=0, keepdims=True)
    causal = _iota((q, q), 0) >= _iota((q, q), 1)
    return jnp.exp(jnp.where(causal, col - rowv, -1e30))


GN_WIDTH = 512


def _ssd_fwd(u, proj, dtb, a_neg, d_x, norm_w, ycat, nbatch, name):
    t = u.shape[0]
    q = CHUNK
    nc = t // nbatch // q

    def body(u_ref, dt_ref, z_ref, dtb_ref, a_ref, dx_ref, nw_ref, _, y_ref, st_ref, yn_ref, state):
        c = pl.program_id(1)

        @pl.when(c == 0)
        def _():
            state[...] = jnp.zeros_like(state)

        st_ref[...] = state[...]
        act, xs, bm, cm, dtp, acs, acs_t, dt_x, ea, dec, s_scale, _ = _ssd_common(u_ref, dt_ref, dtb_ref, a_ref)
        xdt = xs * dt_x
        xdec = xdt * dec
        lo, hi = _half_mask(0), _half_mask(1)
        grp = []
        for g in range(2):
            bg = bm[:, g * LANES:(g + 1) * LANES]
            cg = cm[:, g * LANES:(g + 1) * LANES]
            rows = slice(g * GROUP_ROWS, (g + 1) * GROUP_ROWS)
            sg = state[rows, :]
            grp.append((_dot(cg, bg, NT), _dot(cg, sg, NT), rows,
                        s_scale[rows, :] * sg + _dot(xdec[:, rows], bg, TN)))
        yps = []
        for pj in range(SSD_HEADS // 2):
            cb = grp[pj // 4][0]
            xp = xdt[:, pj * LANES:(pj + 1) * LANES]
            m2 = jnp.concatenate([cb * _decay_mat(acs, acs_t, 2 * pj), cb * _decay_mat(acs, acs_t, 2 * pj + 1)],
                                 axis=1)
            yps.append(_dot(m2, jnp.concatenate([xp * lo, xp * hi], axis=0)))
        for g in range(2):
            _, yoff, rows, state_new = grp[g]
            for j in range(4):
                pj = g * 4 + j
                cols = slice(pj * LANES, (pj + 1) * LANES)
                yp = yps[pj] + yoff[:, j * LANES:(j + 1) * LANES] * ea[:, cols]
                y_ref[:, cols] = yp + dx_ref[:, cols] * xs[:, cols]
            state[rows, :] = state_new
        for g in range(SSD_WIDTH // GN_WIDTH):
            cols = slice(g * GN_WIDTH, (g + 1) * GN_WIDTH)
            gg = y_ref[:, cols] * _silu(z_ref[:, cols])
            rstd = lax.rsqrt(jnp.mean(gg * gg, axis=-1, keepdims=True) + EPS)
            yn_ref[:, cols] = (gg * rstd * nw_ref[:, cols]).astype(yn_ref.dtype)

    vec = pl.BlockSpec((1, LANES), lambda b, c: (0, 0))
    wide = pl.BlockSpec((q, SSD_WIDTH), lambda b, c: (b * nc + c, 0))
    wvec = pl.BlockSpec((1, SSD_WIDTH), lambda b, c: (0, 0))
    return pl.pallas_call(
        body, name=name,
        out_shape=(jax.ShapeDtypeStruct((t, SSD_WIDTH), F32),
                   jax.ShapeDtypeStruct((nbatch * nc * SSD_WIDTH, SSD_STATE), F32),
                   jax.ShapeDtypeStruct(ycat.shape, ycat.dtype)),
        grid=(nbatch, nc),
        in_specs=[pl.BlockSpec((q, SSD_CONV_DIM), lambda b, c: (b * nc + c, 0)),
                  pl.BlockSpec((q, LANES), lambda b, c: (b * nc + c, OFF_DT // LANES)),
                  pl.BlockSpec((q, SSD_WIDTH), lambda b, c: (b * nc + c, OFF_ZS // SSD_WIDTH)),
                  vec, vec, wvec, wvec, ANY],
        out_specs=(wide, pl.BlockSpec((SSD_WIDTH, SSD_STATE), lambda b, c: (b * nc + c, 0)), wide),
        input_output_aliases={7: 2},
        scratch_shapes=[pltpu.VMEM((SSD_WIDTH, SSD_STATE), F32)],
        compiler_params=_params(("parallel", "arbitrary")),
    )(u, proj, proj, dtb, a_neg, d_x, norm_w, ycat)


def _ssd_bwd(dycat, u, proj, y, states, dtb, a_neg, d_x, norm_w, dproj, nbatch, name):
    t = u.shape[0]
    q = CHUNK
    nc = t // nbatch // q

    def body(do_ref, u_ref, dt_ref, z_ref, y_ref, st_ref, dtb_ref, a_ref, dx_ref, nw_ref, _,
             du_ref, dz_ref, ddt_ref, dal_ref, dd_ref, dtbg_ref, dnw_ref, dstate, acc_a, acc_d, acc_b, acc_w):
        b, c = pl.program_id(0), pl.program_id(1)

        @pl.when((b == 0) & (c == 0))
        def _():
            acc_a[...] = jnp.zeros_like(acc_a)
            acc_d[...] = jnp.zeros_like(acc_d)
            acc_b[...] = jnp.zeros_like(acc_b)
            acc_w[...] = jnp.zeros_like(acc_w)

        @pl.when(c == 0)
        def _():
            dstate[...] = jnp.zeros_like(dstate)

        dy_parts = []
        for g in range(SSD_WIDTH // GN_WIDTH):
            cols = slice(g * GN_WIDTH, (g + 1) * GN_WIDTH)
            yv, zv, dov = y_ref[:, cols], z_ref[:, cols], do_ref[:, cols]
            sz = _silu(zv)
            gg = yv * sz
            rstd = lax.rsqrt(jnp.mean(gg * gg, axis=-1, keepdims=True) + EPS)
            gh = gg * rstd
            acc_w[:, cols] += _rowsum8(dov * gh)
            dgn = dov * nw_ref[:, cols]
            dg = rstd * (dgn - gh * jnp.mean(dgn * gh, axis=-1, keepdims=True))
            dy_parts.append(dg * sz)
            dz_ref[:, cols] = (dg * yv * _dsilu(zv)).astype(dz_ref.dtype)

        act, xs, bm, cm, dtp, acs, acs_t, dt_x, ea, dec, s_scale, tril = _ssd_common(
            u_ref, dt_ref, dtb_ref, a_ref, stack_broadcasts=True)
        xdt = xs * dt_x
        xdec = xdt * dec
        dyv = jnp.concatenate(dy_parts, axis=1)
        dye = dyv * ea
        lo, hi = _half_mask(0), _half_mask(1)
        et = _expand_mat_t()
        grp = []
        for g in range(2):
            rows = slice(g * GROUP_ROWS, (g + 1) * GROUP_ROWS)
            bg = bm[:, g * LANES:(g + 1) * LANES]
            cg = cm[:, g * LANES:(g + 1) * LANES]
            sg = st_ref[rows, :]
            dsg = dstate[rows, :]
            grp.append(dict(
                rows=rows, bg=bg, cg=cg, dsg=dsg,
                cb=_dot(cg, bg, NT), yoff=_dot(cg, sg, NT), dxst=_dot(bg, dsg, NT) * dec[:, rows],
                dc_off=_dot(dye[:, rows], sg), db_off=_dot(xdec[:, rows], dsg),
                s_carried=s_scale[rows, :] * sg,
                dstate_new=_dot(dye[:, rows], cg, TN) + s_scale[rows, :] * dsg))
        dy2s, g2s, l2s = [], [], []
        for pj in range(SSD_HEADS // 2):
            cols = slice(pj * LANES, (pj + 1) * LANES)
            dyp = dyv[:, cols]
            dy2 = jnp.concatenate([dyp * lo, dyp * hi], axis=0).astype(MXU_DTYPE)
            dy2s.append(dy2)
            g2s.append(_dot(dy2, xdt[:, cols], NT))
            l2s.append(jnp.concatenate([_decay_mat(acs, acs_t, 2 * pj), _decay_mat(acs, acs_t, 2 * pj + 1)], axis=0))
        dal_diag = jnp.zeros((q, LANES), F32)
        lane2 = _iota((2 * q, LANES), 1)
        row2 = _iota((2 * q, LANES), 0)
        dxdt_parts, db_parts, dc_parts = [], [], []
        end_sum = jnp.zeros((LANES, LANES), F32)
        for g in range(2):
            gd = grp[g]
            cb2 = jnp.concatenate([gd["cb"], gd["cb"]], axis=0)
            dcb = jnp.zeros((q, q), F32)
            parts = []
            for j in range(4):
                pj = g * 4 + j
                gl = g2s[pj] * l2s[pj]
                dcb = dcb + gl[:q] + gl[q:]
                m2 = cb2 * l2s[pj]
                parts.append(_dot(m2, dy2s[pj], TN))
                w2 = (gl * cb2).astype(MXU_DTYPE)
                sel2 = (lane2 == 2 * pj + (row2 >= q).astype(jnp.int32)).astype(MXU_DTYPE)
                dal_diag = dal_diag + _dot(jnp.concatenate([w2[:q], w2[q:]], axis=1), sel2) - _dot(w2, sel2, TN)
            dxdt_parts.append(jnp.concatenate(parts, axis=1) + gd["dxst"])
            dc_parts.append(_dot(dcb, gd["bg"]) + gd["dc_off"])
            db_parts.append(_dot(dcb, gd["cg"], TN) + gd["db_off"])
            end_sum = end_sum + _xdot(gd["dsg"] * gd["s_carried"], et[gd["rows"], :], TN, passes=2)
            dstate[gd["rows"], :] = gd["dstate_new"]
        dxst_parts = [gd["dxst"] for gd in grp]
        yoff_parts = [gd["yoff"] for gd in grp]
        dxdt = jnp.concatenate(dxdt_parts, axis=1)
        dxv = dx_ref[...]
        yoff = jnp.concatenate(yoff_parts, axis=1) * ea
        per_head = _xdot(jnp.concatenate([dyv * yoff, xdt * jnp.concatenate(dxst_parts, axis=1),
                                          dxdt * xs, dyv * xs], axis=0), et)
        off_term, st_term, dx_term, d_term = (per_head[k * q:(k + 1) * q] for k in range(4))
        dalpha = dal_diag + off_term - st_term
        end_row = jnp.sum(end_sum, axis=0, keepdims=True) + jnp.sum(st_term, axis=0, keepdims=True)
        dalpha = dalpha + jnp.where(_iota((q, LANES), 0) == q - 1, end_row, 0.0)
        da = _xdot_r(tril, dalpha, TN)
        ddtp = da * a_ref[...] + dx_term
        acc_a[...] += _rowsum8(da * dtp)
        acc_d[...] += _rowsum8(d_term)
        ddt_raw = ddtp * _sigmoid(dt_ref[...] + dtb_ref[...])
        acc_b[...] += _rowsum8(ddt_raw)
        ddt_ref[...] = ddt_raw
        dxs = dxdt * dt_x + dxv * dyv
        dact = jnp.concatenate([dxs] + db_parts + dc_parts, axis=1)
        du_ref[...] = dact * _dsilu(u_ref[...])

        @pl.when((b == nbatch - 1) & (c == nc - 1))
        def _():
            dal_ref[...] = jnp.sum(acc_a[...], axis=0, keepdims=True) * a_ref[...]
            dd_ref[...] = jnp.sum(acc_d[...], axis=0, keepdims=True)
            dtbg_ref[...] = jnp.sum(acc_b[...], axis=0, keepdims=True)
            dnw_ref[...] = jnp.sum(acc_w[...], axis=0, keepdims=True)

    def rowblk(b, c):
        return b * nc + (nc - 1 - c)

    vec = pl.BlockSpec((1, LANES), lambda b, c: (0, 0))
    wvec = pl.BlockSpec((1, SSD_WIDTH), lambda b, c: (0, 0))
    wide = pl.BlockSpec((q, SSD_WIDTH), lambda b, c: (rowblk(b, c), 0))
    zblk = pl.BlockSpec((q, SSD_WIDTH), lambda b, c: (rowblk(b, c), OFF_ZS // SSD_WIDTH))
    return pl.pallas_call(
        body, name=name,
        out_shape=(jax.ShapeDtypeStruct((t, SSD_CONV_DIM), F32), jax.ShapeDtypeStruct(dproj.shape, dproj.dtype),
                   jax.ShapeDtypeStruct((t, LANES), F32),
                   jax.ShapeDtypeStruct((1, LANES), F32), jax.ShapeDtypeStruct((1, LANES), F32),
                   jax.ShapeDtypeStruct((1, LANES), F32), jax.ShapeDtypeStruct((1, SSD_WIDTH), F32)),
        grid=(nbatch, nc),
        in_specs=[wide,
                  pl.BlockSpec((q, SSD_CONV_DIM), lambda b, c: (rowblk(b, c), 0)),
                  pl.BlockSpec((q, LANES), lambda b, c: (rowblk(b, c), OFF_DT // LANES)),
                  zblk, wide,
                  pl.BlockSpec((SSD_WIDTH, SSD_STATE), lambda b, c: (rowblk(b, c), 0)),
                  vec, vec, wvec, wvec, ANY],
        out_specs=(pl.BlockSpec((q, SSD_CONV_DIM), lambda b, c: (rowblk(b, c), 0)),
                   zblk,
                   pl.BlockSpec((q, LANES), lambda b, c: (rowblk(b, c), 0)),
                   vec, vec, vec, wvec),
        input_output_aliases={10: 1},
        scratch_shapes=[pltpu.VMEM((SSD_WIDTH, SSD_STATE), F32), pltpu.VMEM((SUBLANES, LANES), F32),
                        pltpu.VMEM((SUBLANES, LANES), F32), pltpu.VMEM((SUBLANES, LANES), F32),
                        pltpu.VMEM((SUBLANES, SSD_WIDTH), F32)],
        compiler_params=_params(("arbitrary", "arbitrary")),
    )(dycat, u, proj, proj, y, states, dtb, a_neg, d_x, norm_w, dproj)


def _pad_rows(w, rows):
    return jnp.concatenate([w, jnp.zeros((rows - w.shape[0], w.shape[1]), w.dtype)], axis=0)


def _pad_lanes(v):
    return jnp.concatenate([v, jnp.zeros((LANES - v.shape[0],), v.dtype)]).reshape(1, LANES)


def _padded_from_chips(pieces):
    cols = pieces[0].shape[-1]
    lead = pieces[0].shape[:-1]
    parts, pos = [], 0
    for lo, hi, start in sorted(SECTIONS, key=lambda s: s[2]):
        if start > pos:
            parts.append(jnp.zeros(lead + (start - pos,), pieces[0].dtype))
        pos = start + hi - lo
        while lo < hi:
            p = lo // cols
            end = min(hi, (p + 1) * cols)
            parts.append(pieces[p][..., lo - p * cols:end - p * cols])
            lo = end
    if pos < NP:
        parts.append(jnp.zeros(lead + (NP - pos,), pieces[0].dtype))
    return jnp.concatenate(parts, axis=-1)


def _chip_part_from_padded(wp, p, cols):
    lo, hi = p * cols, (p + 1) * cols
    parts = []
    for rs, re, start in SECTIONS:
        a, b = max(lo, rs), min(hi, re)
        if a < b:
            parts.append(wp[..., start + a - rs:start + b - rs])
    return jnp.concatenate(parts, axis=-1)


def _layer_params(li, w_in_p, w_out, conv_w, dw_w, small):
    return dict(
        w_in_p=w_in_p, w_out=w_out,
        conv_w=_pad_rows(conv_w, SUBLANES), dw_w=_pad_rows(dw_w, 32),
        norm_w=small["norm_w"][li].reshape(1, -1),
        conv_b=small["ssd_conv_b"][li].reshape(1, -1),
        dtb=_pad_lanes(small["ssd_dt_bias"][li]),
        a_neg=_pad_lanes(-jnp.exp(small["ssd_a_log"][li])),
        d_x=jnp.repeat(small["ssd_d"][li], SSD_HEAD_DIM).reshape(1, -1),
        ssd_norm_w=small["ssd_norm_w"][li].reshape(1, -1),
        sinks=_pad_lanes(small["attn_sinks"][li]),
        dw_b=small["conf_dw_b"][li].reshape(1, -1),
        ln_w=small["conf_ln_w"][li].reshape(1, -1),
        ln_b=small["conf_ln_b"][li].reshape(1, -1),
    )


def _layer_fwd(x, p, nbatch, seq, tag, after=None):
    proj, h_t = _proj_fwd(x, p["norm_w"], p["w_in_p"], name=f"proj_fwd_{tag}", after=after)
    u = _conv_fwd(proj, OFF_XBC, SSD_CONV_DIM, p["conv_w"], p["conv_b"], SSD_CONV, seq, name=f"ssd_conv_fwd_{tag}")
    ycat = lax.empty((x.shape[0], MIX_WIDTH), MXU_DTYPE)
    y, states, ycat = _ssd_fwd(u, proj, p["dtb"], p["a_neg"], p["d_x"], p["ssd_norm_w"], ycat, nbatch,
                               name=f"ssd_fwd_{tag}")
    ycat, o, lse = _attn_fwd(proj, p["sinks"], ycat, nbatch, name=f"attn_fwd_{tag}")
    c1, ycat = _conf_fwd(proj, p["dw_w"], p["dw_b"], p["ln_w"], p["ln_b"], ycat, seq, name=f"conf_fwd_{tag}")
    w_out = p["w_out"](ycat) if callable(p["w_out"]) else p["w_out"]
    x_new = _matmul(ycat, w_out, "nn", F32, 1024, 512, 2048, name=f"out_fwd_{tag}", residual=x)
    return x_new, dict(x=x, w_out=w_out, h_t=h_t, proj=proj, u=u, y=y, states=states, o=o, lse=lse, c1=c1, ycat=ycat)


def _layer_bwd(dx_out, p, s, nbatch, seq, tag, hooks=None):
    hooks = hooks or {}
    proj = s["proj"]
    dycat = _matmul(dx_out, s["w_out"], "nt", F32, 1024, 1024, 1024, name=f"out_bwd_dy_{tag}",
                    after=hooks.get("start_token"))
    dw_out = _matmul(s["ycat"], dx_out, "tn", F32, 1024, 1024, 1024, name=f"out_bwd_dw_{tag}")
    token = hooks["after_dycat"](dycat) if "after_dycat" in hooks else None
    dtb = p["dtb"] if token is None else p["dtb"] + token[0, 0]
    dproj = lax.empty(proj.shape, MXU_DTYPE)
    du, dproj, ddt, da_log, dd, ddtb, dssd_norm_w = _ssd_bwd(
        dycat, s["u"], proj, s["y"], s["states"], dtb, p["a_neg"], p["d_x"], p["ssd_norm_w"], dproj,
        nbatch, name=f"ssd_bwd_{tag}")
    dproj, dconv_w, dconv_b = _conv_bwd(du, proj, OFF_XBC, SSD_CONV_DIM, p["conv_w"], SSD_CONV, seq,
                                        name=f"ssd_conv_bwd_{tag}", into=dproj)
    dproj, dsinks = _attn_bwd(dycat, proj, s["o"], s["lse"], p["sinks"], ddt, dproj, nbatch,
                              name=f"attn_bwd_{tag}")
    if "after_attn" in hooks:
        hooks["after_attn"](dproj)
    dproj, ddw_w, ddw_b, dln_w, dln_b = _conf_bwd(dycat, proj, s["c1"], p["dw_w"], p["ln_w"], p["ln_b"], dproj, seq,
                                                  name=f"conf_bwd_{tag}")
    dw_in_p = _matmul(s["h_t"], dproj, "nn", F32, 1024, 512, 4096, name=f"proj_bwd_dw_{tag}")
    token = hooks["after_dw"](dw_in_p, dw_out) if "after_dw" in hooks else None
    norm_w = p["norm_w"] if token is None else p["norm_w"] + token[0, 0]
    dx_in, dnorm_w = _proj_bwd_dx(dproj, p["w_in_p"], s["x"], norm_w, dx_out, name=f"proj_bwd_dx_{tag}")
    grads = dict(
        norm_w=dnorm_w[0], w_in_p=dw_in_p, ssd_conv_w=dconv_w[:SSD_CONV], ssd_conv_b=dconv_b[0],
        ssd_dt_bias=ddtb[0, :SSD_HEADS], ssd_a_log=da_log[0, :SSD_HEADS], ssd_d=dd[0, :SSD_HEADS],
        ssd_norm_w=dssd_norm_w[0], attn_sinks=dsinks[0, :ATTN_Q_HEADS], conf_dw_w=ddw_w[:CONF_KERNEL],
        conf_dw_b=ddw_b[0], conf_ln_w=dln_w[0], conf_ln_b=dln_b[0], w_out=dw_out)
    return dx_in, grads


def _local_step(x, target, param_fns, final_norm_w, first_after=None, bwd_hooks=None):
    nbatch, seq, d = x.shape
    xt = x.reshape(nbatch * seq, d)
    saved, layer_params = [], []
    for li, fn in enumerate(param_fns):
        p = fn(xt)
        layer_params.append(p)
        xt, s = _layer_fwd(xt, p, nbatch, seq, f"l{li}", after=first_after if li == 0 else None)
        saved.append(s)
    loss, dx, dfinal = _loss_head(xt, target.reshape(nbatch * seq, d), final_norm_w.reshape(1, d), name="loss_head")
    grads = [None] * len(layer_params)
    for li in reversed(range(len(layer_params))):
        hooks = bwd_hooks(li) if bwd_hooks is not None else None
        dx, grads[li] = _layer_bwd(dx, layer_params[li], saved[li], nbatch, seq, f"l{li}", hooks=hooks)
    return loss[0, 0], dx.reshape(nbatch, seq, d), grads, dfinal[0]


MESH = pl.DeviceIdType.MESH
N_CHIPS = 4


def _mesh_pos():
    return lax.axis_index("x"), lax.axis_index("y"), lax.axis_index("c")


def _other_chips(x, y):
    return [(1 - x, y), (x, 1 - y), (1 - x, 1 - y)]


def _gather_weights(big, small, name):
    nbig, nsmall = len(big), len(small)
    n_ici = 3 * (nbig + nsmall)
    n_fwd = 3 * nbig

    def body(*refs):
        ins = refs[:nbig + nsmall]
        outs = refs[nbig + nsmall:2 * (nbig + nsmall)]
        send_sems, recv_sems = refs[2 * (nbig + nsmall):]
        x, y, c = _mesh_pos()
        me = 2 * x + y
        sibling = (x, y, 1 - c)
        chips = _other_chips(x, y)

        def ici(a, j, origin, dest):
            if a < nbig:
                src = ins[a].at[c] if origin is None else outs[a].at[origin, c]
                dst = outs[a].at[me if origin is None else origin, c]
            else:
                src = ins[a] if origin is None else outs[a].at[origin]
                dst = outs[a].at[me if origin is None else origin]
            k = a * 3 + j
            return pltpu.make_async_remote_copy(src_ref=src, dst_ref=dst, send_sem=send_sems.at[k],
                                                recv_sem=recv_sems.at[k], device_id=dest, device_id_type=MESH)

        def fwd(a, j, origin, half):
            k = n_ici + a * 3 + j
            ref = outs[a].at[origin, half]
            return pltpu.make_async_remote_copy(src_ref=ref, dst_ref=ref, send_sem=send_sems.at[k],
                                                recv_sem=recv_sems.at[k], device_id=sibling, device_id_type=MESH)

        sends = []
        for j, (px, py) in enumerate(chips):
            for a in range(nbig + nsmall):
                cp = ici(a, j, None, (px, py, c))
                cp.start()
                sends.append(cp)
        for j, (px, py) in enumerate(chips):
            origin = 2 * px + py
            for a in range(nbig):
                ici(a, j, origin, (px, py, c)).wait_recv()
                cp = fwd(a, j, origin, c)
                cp.start()
                sends.append(cp)
        for j, (px, py) in enumerate(chips):
            origin = 2 * px + py
            for a in range(nbig, nbig + nsmall):
                ici(a, j, origin, (px, py, c)).wait_recv()
            for a in range(nbig):
                fwd(a, j, origin, 1 - c).wait_recv()
        for cp in sends:
            cp.wait_send()

    out_shape = tuple(jax.ShapeDtypeStruct((N_CHIPS,) + a.shape, a.dtype) for a in list(big) + list(small))
    return pl.pallas_call(
        body, name=name, out_shape=out_shape,
        in_specs=[ANY] * (nbig + nsmall), out_specs=tuple([ANY] * (nbig + nsmall)),
        scratch_shapes=[pltpu.SemaphoreType.DMA((n_ici + n_fwd,)), pltpu.SemaphoreType.DMA((n_ici + n_fwd,))],
    )(*big, *small)


HBM = pl.BlockSpec(memory_space=pltpu.HBM)
SEM = pl.BlockSpec(memory_space=pltpu.SEMAPHORE)
DATAFLOW = pltpu.SideEffectType.DATAFLOW_SIDE_EFFECTING


def _split_peers(pattern, x, y, c):
    if pattern == "swap":
        return [((x, y, 1 - c), 1 - c, None, None)]
    me = 2 * x + y
    return [((px, py, c), 2 * px + py if pattern == "scatter" else None, me, 2 * px + py)
            for px, py in _other_chips(x, y)]


def _split_land_shape(pattern, shape):
    return {"bcast": (N_CHIPS,) + shape, "scatter": shape, "swap": shape[:1] + shape[2:]}[pattern]


def _split_copies(pattern, srcs, lands, send_sems, recv_sems, waiting):
    x, y, c = _mesh_pos()
    peers = _split_peers(pattern, x, y, c)
    cps = []
    for j, (dev, src_slot, dst_slot, my_slot) in enumerate(peers):
        for a in range(len(srcs)):
            if src_slot is None:
                src = srcs[a]
            else:
                src = srcs[a].at[:, src_slot] if pattern == "swap" else srcs[a].at[src_slot]
            slot = my_slot if waiting else dst_slot
            dst = lands[a] if slot is None else lands[a].at[slot]
            k = a * len(peers) + j
            cps.append(pltpu.make_async_remote_copy(src_ref=src, dst_ref=dst, send_sem=send_sems[k],
                                                    recv_sem=recv_sems[k], device_id=dev, device_id_type=MESH))
    return cps


def _split_start(arrs, pattern, after, name):
    n = len(arrs)
    nsem = n * (1 if pattern == "swap" else N_CHIPS - 1)
    deps = [] if after is None else [after]

    def body(*refs):
        srcs, lands = refs[:n], refs[n:2 * n]
        outs = refs[2 * n + len(deps):]
        for cp in _split_copies(pattern, srcs, lands, outs[:nsem], outs[nsem:2 * nsem], waiting=False):
            cp.start()
        outs[-1][...] = jnp.zeros_like(outs[-1])

    lands = [lax.empty(_split_land_shape(pattern, a.shape), a.dtype) for a in arrs]
    out_shape = ([pltpu.SemaphoreType.DMA(())] * (2 * nsem)
                 + [pltpu.HBM(a.shape, a.dtype) for a in arrs] + [pltpu.HBM(b.shape, b.dtype) for b in lands]
                 + [jax.ShapeDtypeStruct((SUBLANES, LANES), F32)])
    outs = pl.pallas_call(
        body, name=name, out_shape=tuple(out_shape),
        in_specs=[HBM] * (2 * n) + [ANY] * len(deps),
        out_specs=tuple([SEM] * (2 * nsem) + [HBM] * (2 * n) + [pl.BlockSpec(memory_space=pltpu.VMEM)]),
        input_output_aliases={a: 2 * nsem + a for a in range(2 * n)},
        compiler_params=pltpu.CompilerParams(has_side_effects=DATAFLOW),
    )(*[pltpu.with_memory_space_constraint(a, pltpu.HBM) for a in list(arrs) + lands], *deps)
    return outs[:-1], outs[-1]


def _split_wait(state, n, pattern, after, name):
    nsem = n * (1 if pattern == "swap" else N_CHIPS - 1)

    def body(*refs):
        srcs, lands = refs[:n], refs[n:2 * n]
        send_sems, recv_sems = refs[2 * n:2 * n + nsem], refs[2 * n + nsem:2 * n + 2 * nsem]
        for cp in _split_copies(pattern, srcs, lands, send_sems, recv_sems, waiting=True):
            cp.wait_send()
            cp.wait_recv()

    sems, thru = state[:2 * nsem], state[2 * nsem:]
    outs = pl.pallas_call(
        body, name=name, out_shape=tuple(pltpu.HBM(a.shape, a.dtype) for a in thru),
        in_specs=[HBM] * (2 * n) + [SEM] * (2 * nsem) + [ANY],
        out_specs=tuple([HBM] * (2 * n)),
        input_output_aliases={a: a for a in range(2 * n)},
        compiler_params=pltpu.CompilerParams(has_side_effects=DATAFLOW),
    )(*thru, *sems, after)
    return outs[:n], outs[n:]


def _pair_gather(arrs, layer, name):
    n = len(arrs)

    def body(*refs):
        outs = refs[n:2 * n]
        send_sems, recv_sems = refs[2 * n:]
        x, y, c = _mesh_pos()
        cps = [pltpu.make_async_remote_copy(src_ref=outs[a].at[layer, c], dst_ref=outs[a].at[layer, c],
                                            send_sem=send_sems.at[a], recv_sem=recv_sems.at[a],
                                            device_id=(x, y, 1 - c), device_id_type=MESH)
               for a in range(n)]
        for cp in cps:
            cp.start()
        for cp in cps:
            cp.wait()

    return pl.pallas_call(
        body, name=name, out_shape=tuple(jax.ShapeDtypeStruct(a.shape, a.dtype) for a in arrs),
        in_specs=[ANY] * n, out_specs=tuple([ANY] * n),
        input_output_aliases={a: a for a in range(n)},
        scratch_shapes=[pltpu.SemaphoreType.DMA((n,)), pltpu.SemaphoreType.DMA((n,))],
    )(*arrs)


N_DEV = 8


def _allreduce_small(pack, name):
    r = pack.shape[0]

    def body(p_ref, o_ref, land, send_sems, recv_sems):
        x, y, c = _mesh_pos()
        me = 4 * x + 2 * y + c
        cps = []
        for k in range(1, N_DEV):
            peer = (x ^ (k >> 2), y ^ ((k >> 1) & 1), c ^ (k & 1))
            cps.append(pltpu.make_async_remote_copy(src_ref=p_ref, dst_ref=land.at[me], send_sem=send_sems.at[k - 1],
                                                    recv_sem=recv_sems.at[k - 1], device_id=peer, device_id_type=MESH))
        for cp in cps:
            cp.start()
        land[me] = p_ref[...]
        for cp in cps:
            cp.wait()
        total = land[0]
        for d in range(1, N_DEV):
            total = total + land[d]
        o_ref[...] = total

    vm = pl.BlockSpec(memory_space=pltpu.VMEM)
    return pl.pallas_call(
        body, name=name, out_shape=jax.ShapeDtypeStruct(pack.shape, F32),
        in_specs=[vm], out_specs=vm,
        scratch_shapes=[pltpu.VMEM((N_DEV, r, LANES), F32), pltpu.SemaphoreType.DMA((N_DEV - 1,)),
                        pltpu.SemaphoreType.DMA((N_DEV - 1,))],
    )(pack)


BIG_ROWS = 256


def _cast_layer(w, layer, name):
    _, r, cdim = w.shape
    tr = BIG_ROWS

    def body(w_ref, o_ref):
        o_ref[...] = w_ref[...].astype(o_ref.dtype)

    return pl.pallas_call(
        body, name=name, out_shape=jax.ShapeDtypeStruct((r, cdim), MXU_DTYPE),
        grid=(r // tr,), in_specs=[pl.BlockSpec((None, tr, cdim), lambda i: (layer, i, 0))],
        out_specs=pl.BlockSpec((tr, cdim), lambda i: (i, 0)),
        compiler_params=_params(("parallel",)),
    )(w)


def _cast_cols_major(w_t, name):
    cdim, nl, r = w_t.shape
    tc = LANES

    def body(w_ref, *o_refs):
        for l in range(nl):
            o_refs[l][...] = w_ref[:, l, :].T.astype(o_refs[l].dtype)

    out = pl.BlockSpec((r, tc), lambda i: (0, i))
    return pl.pallas_call(
        body, name=name, out_shape=tuple(jax.ShapeDtypeStruct((r, cdim), MXU_DTYPE) for _ in range(nl)),
        grid=(pl.cdiv(cdim, tc),), in_specs=[pl.BlockSpec((tc, nl, r), lambda i: (i, 0, 0))],
        out_specs=tuple([out] * nl),
        compiler_params=_params(("parallel",)),
    )(w_t)


def _pair_sum(parts, sib, which, out_dtype, name):
    k, _, r, cdim = parts.shape
    tr = BIG_ROWS

    def body(sel_ref, p_ref, s_ref, o_ref):
        o_ref[...] = (p_ref[...] + s_ref[...]).astype(o_ref.dtype)

    grid_spec = pltpu.PrefetchScalarGridSpec(
        num_scalar_prefetch=1, grid=(k, r // tr),
        in_specs=[pl.BlockSpec((None, None, tr, cdim), lambda l, i, sel: (l, sel[0], i, 0)),
                  pl.BlockSpec((None, tr, cdim), lambda l, i, sel: (l, i, 0))],
        out_specs=pl.BlockSpec((None, tr, cdim), lambda l, i, sel: (l, i, 0)))
    return pl.pallas_call(
        body, name=name, out_shape=jax.ShapeDtypeStruct((k, r, cdim), out_dtype), grid_spec=grid_spec,
        compiler_params=_params(("parallel", "parallel")),
    )(which.reshape(1).astype(jnp.int32), parts, sib)


def _sum_lead(parts, into, layer, which, name):
    k, r, cdim = parts.shape
    tr = BIG_ROWS

    def body(sel_ref, p_ref, _, o_ref):
        total = p_ref[0].astype(F32)
        for a in range(1, k):
            total = total + p_ref[a].astype(F32)
        o_ref[...] = total

    grid_spec = pltpu.PrefetchScalarGridSpec(
        num_scalar_prefetch=1, grid=(r // tr,),
        in_specs=[pl.BlockSpec((k, tr, cdim), lambda i, sel: (0, i, 0)), ANY],
        out_specs=pl.BlockSpec((None, None, tr, cdim), lambda i, sel: (layer, sel[0], i, 0)))
    return pl.pallas_call(
        body, name=name, out_shape=jax.ShapeDtypeStruct(into.shape, F32), grid_spec=grid_spec,
        input_output_aliases={2: 0},
        compiler_params=_params(("parallel",)),
    )(which.reshape(1).astype(jnp.int32), parts, into)


def _adam_math(w, g, m, v):
    m2 = ADAM_B1 * m + (1.0 - ADAM_B1) * g
    v2 = ADAM_B2 * v + (1.0 - ADAM_B2) * (g * g)
    m_hat = m2 / (1.0 - ADAM_B1 ** ADAM_STEP)
    v_hat = v2 / (1.0 - ADAM_B2 ** ADAM_STEP)
    delta = -ADAM_LR * (m_hat / (jnp.sqrt(v_hat) + ADAM_EPS) + ADAM_WD * w)
    return delta, m2, v2


def _adam_big(w, g, m, v, name):
    nl, r, cdim = w.shape
    tr = BIG_ROWS

    def body(w_ref, g_ref, m_ref, v_ref, d_ref, mo_ref, vo_ref):
        delta, m2, v2 = _adam_math(w_ref[...], g_ref[...], m_ref[...], v_ref[...])
        d_ref[...] = delta
        mo_ref[...] = m2
        vo_ref[...] = v2

    blk = pl.BlockSpec((None, tr, cdim), lambda l, i: (l, i, 0))
    shp = jax.ShapeDtypeStruct(w.shape, F32)
    return pl.pallas_call(
        body, name=name, out_shape=(shp, shp, shp),
        grid=(nl, r // tr), in_specs=[blk] * 4, out_specs=(blk, blk, blk),
        compiler_params=_params(("parallel", "parallel")),
    )(w, g, m, v)


def _adam_cols_major(w, g, m, v, name):
    cdim, nl, r = w.shape
    tc = BIG_ROWS

    def body(w_ref, g_ref, m_ref, v_ref, d_ref, mo_ref, vo_ref):
        delta, m2, v2 = _adam_math(w_ref[...], g_ref[...], m_ref[...], v_ref[...])
        d_ref[...] = delta
        mo_ref[...] = m2
        vo_ref[...] = v2

    blk = pl.BlockSpec((tc, nl, r), lambda i: (i, 0, 0))
    shp = jax.ShapeDtypeStruct(w.shape, F32)
    return pl.pallas_call(
        body, name=name, out_shape=(shp, shp, shp),
        grid=(pl.cdiv(cdim, tc),), in_specs=[blk] * 4, out_specs=(blk, blk, blk),
        compiler_params=_params(("parallel",)),
    )(w, g, m, v)


def _adam_small(ws, gs, ms, vs, name):
    n = len(ws)

    def body(*refs):
        w_refs, g_refs, m_refs, v_refs = (refs[k * n:(k + 1) * n] for k in range(4))
        d_refs, mo_refs, vo_refs = (refs[(4 + k) * n:(5 + k) * n] for k in range(3))
        for a in range(n):
            delta, m2, v2 = _adam_math(w_refs[a][...], g_refs[a][...], m_refs[a][...], v_refs[a][...])
            d_refs[a][...] = delta
            mo_refs[a][...] = m2
            vo_refs[a][...] = v2

    shapes = tuple(jax.ShapeDtypeStruct(w.shape, F32) for w in ws)
    vm = pl.BlockSpec(memory_space=pltpu.VMEM)
    outs = pl.pallas_call(body, name=name, out_shape=shapes * 3, in_specs=[vm] * (4 * n),
                          out_specs=tuple([vm] * (3 * n)))(*ws, *gs, *ms, *vs)
    return outs[:n], outs[n:2 * n], outs[2 * n:]


PACK_TILE = SUBLANES * LANES


def _pack(arrays):
    rows = []
    for a in arrays:
        flat = a.reshape(-1)
        pad = (-flat.shape[0]) % PACK_TILE
        if pad:
            flat = jnp.concatenate([flat, jnp.zeros((pad,), flat.dtype)])
        rows.append(flat.reshape(-1, LANES))
    return jnp.concatenate(rows, axis=0)


def _unpack(pack, shapes):
    outs, row = [], 0
    for shp in shapes:
        n = int(np.prod(shp))
        nrows = -(-n // PACK_TILE) * SUBLANES
        outs.append(pack[row:row + nrows].reshape(-1)[:n].reshape(shp))
        row += nrows
    return outs


SMALL = ["norm_w", "ssd_conv_b", "ssd_dt_bias", "ssd_a_log", "ssd_d", "ssd_norm_w", "attn_sinks",
         "conf_dw_b", "conf_ln_w", "conf_ln_b"]
WEIGHTS = ["norm_w", "w_in", "ssd_conv_w", "ssd_conv_b", "ssd_dt_bias", "ssd_a_log", "ssd_d", "ssd_norm_w",
           "attn_sinks", "conf_dw_w", "conf_dw_b", "conf_ln_w", "conf_ln_b", "w_out", "final_norm_w"]


def kernel(x, norm_w, w_in, ssd_conv_w, ssd_conv_b, ssd_dt_bias, ssd_a_log, ssd_d, ssd_norm_w, attn_sinks, conf_dw_w, conf_dw_b, conf_ln_w, conf_ln_b, w_out, final_norm_w, loss_target, m_norm_w, m_w_in, m_ssd_conv_w, m_ssd_conv_b, m_ssd_dt_bias, m_ssd_a_log, m_ssd_d, m_ssd_norm_w, m_attn_sinks, m_conf_dw_w, m_conf_dw_b, m_conf_ln_w, m_conf_ln_b, m_w_out, m_final_norm_w, v_norm_w, v_w_in, v_ssd_conv_w, v_ssd_conv_b, v_ssd_dt_bias, v_ssd_a_log, v_ssd_d, v_ssd_norm_w, v_attn_sinks, v_conf_dw_w, v_conf_dw_b, v_conf_ln_w, v_conf_ln_b, v_w_out, v_final_norm_w):
    w = dict(norm_w=norm_w, w_in=w_in, ssd_conv_w=ssd_conv_w, ssd_conv_b=ssd_conv_b, ssd_dt_bias=ssd_dt_bias,
             ssd_a_log=ssd_a_log, ssd_d=ssd_d, ssd_norm_w=ssd_norm_w, attn_sinks=attn_sinks, conf_dw_w=conf_dw_w,
             conf_dw_b=conf_dw_b, conf_ln_w=conf_ln_w, conf_ln_b=conf_ln_b, w_out=w_out, final_norm_w=final_norm_w)
    m = dict(norm_w=m_norm_w, w_in=m_w_in, ssd_conv_w=m_ssd_conv_w, ssd_conv_b=m_ssd_conv_b,
             ssd_dt_bias=m_ssd_dt_bias, ssd_a_log=m_ssd_a_log, ssd_d=m_ssd_d, ssd_norm_w=m_ssd_norm_w,
             attn_sinks=m_attn_sinks, conf_dw_w=m_conf_dw_w, conf_dw_b=m_conf_dw_b, conf_ln_w=m_conf_ln_w,
             conf_ln_b=m_conf_ln_b, w_out=m_w_out, final_norm_w=m_final_norm_w)
    v = dict(norm_w=v_norm_w, w_in=v_w_in, ssd_conv_w=v_ssd_conv_w, ssd_conv_b=v_ssd_conv_b,
             ssd_dt_bias=v_ssd_dt_bias, ssd_a_log=v_ssd_a_log, ssd_d=v_ssd_d, ssd_norm_w=v_ssd_norm_w,
             attn_sinks=v_attn_sinks, conf_dw_w=v_conf_dw_w, conf_dw_b=v_conf_dw_b, conf_ln_w=v_conf_ln_w,
             conf_ln_b=v_conf_ln_b, w_out=v_w_out, final_norm_w=v_final_norm_w)
    depth = w_in.shape[0]
    me = 2 * lax.axis_index("x") + lax.axis_index("y")

    assert depth == 2
    w_in_t = jnp.transpose(w_in, (2, 0, 1))
    w_in_b = _cast_cols_major(w_in_t, name="cast_w_in")
    w_out_b = [_cast_layer(w_out, li, name=f"cast_w_out_l{li}") for li in range(depth)]
    own0 = [w_in_b[0].reshape((2, -1) + w_in_b[0].shape[1:]), ssd_conv_w, conf_dw_w]
    gathered0 = _gather_weights(own0[:1], own0[1:], name="gather_weights_l0")
    g_in0, g_conv, g_dw = [lax.dynamic_update_index_in_dim(g_all, mine, me, 0)
                           for g_all, mine in zip(gathered0, own0)]
    own1 = [w_out_b[0], w_in_b[1], w_out_b[1]]
    pending1, token1 = _split_start(own1, "bcast", gathered0[0], name="gather_rest_start")
    rest = {}

    def small_full(li):
        return (jnp.concatenate([g_conv[p, li] for p in range(N_CHIPS)], axis=1),
                jnp.concatenate([g_dw[p, li] for p in range(N_CHIPS)], axis=1))

    def w_out_l0(after):
        mine1, landed = _split_wait(pending1, len(own1), "bcast", after, name="gather_rest_wait")
        rest["landed"] = [lax.dynamic_update_index_in_dim(g_all, mine, me, 0) for g_all, mine in zip(landed, mine1)]
        return rest["landed"][0].reshape(-1, w_out.shape[2])

    def params_l0(_):
        w_in_p = _padded_from_chips([g_in0[p].reshape(w_in_b[0].shape) for p in range(N_CHIPS)])
        return _layer_params(0, w_in_p, w_out_l0, *small_full(0), w)

    def params_l1(_):
        _, g_in1, g_out1 = rest["landed"]
        w_in_p = _padded_from_chips([g_in1[p] for p in range(N_CHIPS)])
        return _layer_params(1, w_in_p, g_out1.reshape(-1, g_out1.shape[-1]), *small_full(1), w)

    c = lax.axis_index("c")
    cols = w_in.shape[2]
    rows_out = w_out.shape[1]

    def grad_parts(g):
        dw = g["w_in_p"]
        return [dw.reshape(1, 2, dw.shape[0] // 2, dw.shape[1]),
                g["w_out"].reshape(N_CHIPS, 2, rows_out // 2, D_MODEL)]

    def pair_sums(parts, sib, tag):
        s_in, s_out = [_pair_sum(p, sb, c, MXU_DTYPE, name=f"grad_pair_sum_{k}_{tag}")
                       for k, (p, sb) in enumerate(zip(parts, sib))]
        return [jnp.stack([_chip_part_from_padded(s_in[0], p, cols) for p in range(N_CHIPS)]), s_out]

    split = {"reduced": [lax.empty((depth, 2, w_in.shape[1] // 2, cols), F32),
                         lax.empty((depth, 2, rows_out // 2, D_MODEL), F32)]}

    def chip_sums(landed, sent, li, which=(0, 1)):
        filled = [lax.dynamic_update_index_in_dim(r, lax.dynamic_index_in_dim(sk, me, 0, keepdims=False), me, 0)
                  for r, sk in zip(landed, sent)]
        tag = "".join(str(k) for k in which)
        halves = [_sum_lead(r, split["reduced"][k], li, c, name=f"grad_chip_sum_{k}_l{li}")
                  for k, r in zip(which, filled)]
        for k, buf in zip(which, _pair_gather(halves, li, name=f"grad_pair_gather_{tag}_l{li}")):
            split["reduced"][k] = buf

    def bwd_hooks(li):
        def after_dw(dw_in_p, dw_out):
            parts = grad_parts(dict(w_in_p=dw_in_p, w_out=dw_out))
            state, token = _split_start(parts, "swap", None, name=f"grad_swap_l{li}_start")
            if li > 0:
                split[f"swap{li}"] = (parts, state)
                return token
            mine, sib = _split_wait(state, len(parts), "swap", token, name="grad_swap_l0_wait")
            split["scatter0"], token = _split_start(pair_sums(mine, sib, "l0"), "scatter", None,
                                                    name="grad_scatter_l0_start")
            return token

        hooks = {"after_dw": after_dw}
        if li == depth - 2:
            parts, swap_state = split[f"swap{depth - 1}"]

            def after_dycat(dycat):
                mine, sib = _split_wait(swap_state, len(parts), "swap", dycat, name="grad_swap_l1_wait")
                sent = pair_sums(mine, sib, "l1")
                split["scatter"], token = _split_start(sent, "scatter", None, name="grad_scatter_l1_start")
                return token

            def after_attn(dproj):
                sent, landed = _split_wait(split["scatter"], len(parts), "scatter", dproj,
                                           name="grad_scatter_l1_wait")
                chip_sums(landed, sent, depth - 1)

            hooks.update(after_dycat=after_dycat, after_attn=after_attn)
        return hooks

    loss, grad_x, grads, dfinal = _local_step(x, loss_target, [params_l0, params_l1], final_norm_w,
                                              first_after=token1, bwd_hooks=bwd_hooks)

    small_list = [grads[li][n] for li in range(depth) for n in SMALL]
    small_list += [grads[li][n] for li in range(depth) for n in ("ssd_conv_w", "conf_dw_w")]
    small_list += [dfinal, loss.reshape(1)]
    small_shapes = [a.shape for a in small_list]
    reduced = _unpack(_allreduce_small(_pack(small_list), name="allreduce_small"), small_shapes)
    ns = len(SMALL)
    g = {n: jnp.stack([reduced[li * ns + i] for li in range(depth)]) for i, n in enumerate(SMALL)}
    conv_w_cols, dw_w_cols = ssd_conv_w.shape[2], conf_dw_w.shape[2]
    g["ssd_conv_w"] = jnp.stack([lax.dynamic_slice_in_dim(reduced[depth * ns + 2 * li], me * conv_w_cols,
                                                          conv_w_cols, axis=1) for li in range(depth)])
    g["conf_dw_w"] = jnp.stack([lax.dynamic_slice_in_dim(reduced[depth * ns + 2 * li + 1], me * dw_w_cols,
                                                         dw_w_cols, axis=1) for li in range(depth)])
    g["final_norm_w"] = reduced[-2]
    loss_total = reduced[-1][0]

    small_names = [n for n in WEIGHTS if n not in ("w_in", "w_out")]

    def as2d(a):
        return a.reshape(1, -1) if a.ndim == 1 else a

    deltas, new_ms, new_vs = _adam_small(*[[as2d(src[n]) for n in small_names] for src in (w, g, m, v)],
                                         name="adam_small")

    sent0, landed0 = _split_wait(split["scatter0"], 2, "scatter", deltas[0], name="grad_scatter_l0_wait")
    chip_sums(landed0, sent0, 0)
    g_w_in = split["reduced"][0].reshape(w_in.shape)
    g_w_out = split["reduced"][1].reshape(w_out.shape)
    outs_g, outs_d, outs_m, outs_v = {"w_in": g_w_in, "w_out": g_w_out}, {}, {}, {}
    outs_d["w_out"], outs_m["w_out"], outs_v["w_out"] = _adam_big(w_out, g_w_out, m_w_out, v_w_out,
                                                                  name="adam_w_out")
    to_cols, from_cols = (2, 0, 1), (1, 2, 0)
    outs_d["w_in"], outs_m["w_in"], outs_v["w_in"] = [
        jnp.transpose(a, from_cols) for a in _adam_cols_major(
            *[jnp.transpose(a, to_cols) for a in (w_in, g_w_in, m_w_in, v_w_in)], name="adam_w_in")]
    for n, dn, mn, vn in zip(small_names, deltas, new_ms, new_vs):
        outs_g[n], outs_d[n], outs_m[n], outs_v[n] = (g[n], dn.reshape(w[n].shape), mn.reshape(w[n].shape),
                                                      vn.reshape(w[n].shape))
    return (loss_total, grad_x, *[outs_g[n] for n in WEIGHTS], *[outs_d[n] for n in WEIGHTS],
            *[outs_m[n] for n in WEIGHTS], *[outs_v[n] for n in WEIGHTS])
```

```python
import functools
import math

import jax
import jax.numpy as jnp
import numpy as np
from jax import lax
from jax.experimental import pallas as pl
from jax.experimental.pallas import tpu as pltpu

F32 = jnp.float32
BF16 = jnp.bfloat16
MXU_DTYPE = BF16

D_MODEL = 1024
DEPTH = 2
SSD_HEADS = 16
SSD_HEAD_DIM = 64
SSD_STATE = 128
SSD_CONV = 4
CHUNK = 128
SSD_CONV_DIM = 1536
ATTN_HEAD_DIM = 64
ATTN_Q_HEADS = 8
WINDOW = 128
CONF_WIDTH = 512
CONF_KERNEL = 31
MIX_WIDTH = 2048
D_IN_PROJ = 5392
EPS = 1e-5

ADAM_LR = 0.001
ADAM_B1 = 0.9
ADAM_B2 = 0.999
ADAM_EPS = 1e-08
ADAM_WD = 0.01
ADAM_STEP = 10

LANES = 128
SUBLANES = 8
VMEM_LIMIT = 48 * 1024 * 1024

NP = 5632
OFF_ZA, OFF_Q, OFF_K, OFF_V, OFF_DT = 0, 512, 1024, 1152, 1280
ATTN_GROUP = 1536
OFF_CONF, OFF_ZC = 1536, 2560
CONF_GROUP = 1536
OFF_ZS = 3072
OFF_XBC = 4096
SECTIONS = ((0, 1024, OFF_ZS), (1024, 1536, OFF_ZA), (1536, 2048, OFF_ZC), (2048, 3584, OFF_XBC),
            (3584, 3600, OFF_DT), (3600, 4368, OFF_Q), (4368, 5392, OFF_CONF))

YCAT_ATTN, YCAT_CONF = 1024, 1536
ANY = pl.BlockSpec(memory_space=pl.ANY)

NN = (((1,), (0,)), ((), ()))
NT = (((1,), (1,)), ((), ()))
TN = (((0,), (0,)), ((), ()))


def _params(sem):
    return pltpu.CompilerParams(dimension_semantics=sem, vmem_limit_bytes=VMEM_LIMIT)


def _dot(a, b, dims=NN):
    return lax.dot_general(a.astype(MXU_DTYPE), b.astype(MXU_DTYPE), dims, preferred_element_type=F32)


def _split_bf16(a, passes):
    pieces = []
    r = a
    for _ in range(passes):
        p = r.astype(BF16)
        pieces.append(p)
        r = r - p.astype(F32)
    return pieces


def _xdot(a, sel, dims=NN, passes=2):
    out = None
    for p in _split_bf16(a, passes):
        t = lax.dot_general(p, sel, dims, preferred_element_type=F32)
        out = t if out is None else out + t
    return out


def _xdot_r(sel, b, dims=NN, passes=3):
    out = None
    for p in _split_bf16(b, passes):
        t = lax.dot_general(sel, p, dims, preferred_element_type=F32)
        out = t if out is None else out + t
    return out


def _sigmoid(x):
    return 1.0 / (1.0 + jnp.exp(-x))


def _silu(x):
    return x * _sigmoid(x)


def _dsilu(x):
    s = _sigmoid(x)
    return s * (1.0 + x * (1.0 - s))


def _softplus(x):
    return jnp.maximum(x, 0.0) + jnp.log(1.0 + jnp.exp(-jnp.abs(x)))


def _rowsum8(x):
    r, c = x.shape
    return jnp.sum(x.reshape(r // SUBLANES, SUBLANES, c), axis=0)


def _iota(shape, dim):
    return lax.broadcasted_iota(jnp.int32, shape, dim)


def _matmul(a, b, form, out_dtype, tm, tn, tk, name, residual=None, after=None):
    if form == "nn":
        (m, k), n = a.shape, b.shape[1]
    elif form == "nt":
        (m, k), n = a.shape, b.shape[0]
    else:
        (k, m), n = a.shape, b.shape[1]
    tm, tn, tk = min(tm, m), min(tn, n), min(tk, k)
    assert m % tm == 0 and n % tn == 0 and k % tk == 0, (name, m, n, k, tm, tn, tk)
    if form == "nn":
        a_spec = pl.BlockSpec((tm, tk), lambda i, j, s: (i, s))
        b_spec = pl.BlockSpec((tk, tn), lambda i, j, s: (s, j))
        dims = NN
    elif form == "nt":
        (m, k), n = a.shape, b.shape[0]
        a_spec = pl.BlockSpec((tm, tk), lambda i, j, s: (i, s))
        b_spec = pl.BlockSpec((tn, tk), lambda i, j, s: (j, s))
        dims = NT
    else:
        (k, m), n = a.shape, b.shape[1]
        a_spec = pl.BlockSpec((tk, tm), lambda i, j, s: (s, i))
        b_spec = pl.BlockSpec((tk, tn), lambda i, j, s: (s, j))
        dims = TN
    nk = k // tk
    has_res = residual is not None
    deps = [] if after is None else [after]

    def body_single(a_ref, b_ref, *rest):
        o = _dot(a_ref[...], b_ref[...], dims)
        if has_res:
            o = o + rest[0][...]
        rest[-1][...] = o.astype(out_dtype)

    def body(a_ref, b_ref, *rest):
        r_ref = rest[0] if has_res else None
        o_ref, acc = rest[-2:]
        s = pl.program_id(2)

        @pl.when(s == 0)
        def _():
            acc[...] = jnp.zeros_like(acc)

        acc[...] += _dot(a_ref[...], b_ref[...], dims)

        @pl.when(s == nk - 1)
        def _():
            o = acc[...]
            if has_res:
                o = o + r_ref[...]
            o_ref[...] = o.astype(out_dtype)

    in_specs = [a_spec, b_spec]
    args = [a, b]
    if has_res:
        in_specs.append(pl.BlockSpec((tm, tn), lambda i, j, s: (i, j)))
        args.append(residual)
    in_specs += [ANY] * len(deps)
    args += deps
    return pl.pallas_call(
        body_single if nk == 1 else body, name=name,
        out_shape=jax.ShapeDtypeStruct((m, n), out_dtype),
        grid=(m // tm, n // tn, nk),
        in_specs=in_specs,
        out_specs=pl.BlockSpec((tm, tn), lambda i, j, s: (i, j)),
        scratch_shapes=[] if nk == 1 else [pltpu.VMEM((tm, tn), F32)],
        compiler_params=_params(("parallel", "parallel", "arbitrary")),
    )(*args)


ROW_TILE = 256


PROJ_FWD_TM, PROJ_FWD_TN = 2048, 512


def _proj_fwd(x, w, w_in_p, name, after=None):
    t, d = x.shape
    n = w_in_p.shape[1]
    tm, tn = min(PROJ_FWD_TM, t), PROJ_FWD_TN
    assert t % tm == 0 and n % tn == 0
    deps = [] if after is None else [after]

    def body(x_ref, w_ref, b_ref, *rest):
        o_ref, ot_ref, h_scr = rest[len(deps):]

        @pl.when(pl.program_id(1) == 0)
        def _():
            xv = x_ref[...]
            rstd = lax.rsqrt(jnp.mean(xv * xv, axis=-1, keepdims=True) + EPS)
            h = xv * rstd * w_ref[...]
            h_scr[...] = h.astype(h_scr.dtype)
            ot_ref[...] = h.T.astype(ot_ref.dtype)

        o_ref[...] = _dot(h_scr[...], b_ref[...])

    return pl.pallas_call(
        body, name=name,
        out_shape=(jax.ShapeDtypeStruct((t, n), F32), jax.ShapeDtypeStruct((d, t), MXU_DTYPE)),
        grid=(t // tm, n // tn),
        in_specs=[pl.BlockSpec((tm, d), lambda i, j: (i, 0)), pl.BlockSpec((1, d), lambda i, j: (0, 0)),
                  pl.BlockSpec((d, tn), lambda i, j: (0, j))] + [ANY] * len(deps),
        out_specs=(pl.BlockSpec((tm, tn), lambda i, j: (i, j)), pl.BlockSpec((d, tm), lambda i, j: (0, i))),
        scratch_shapes=[pltpu.VMEM((tm, d), MXU_DTYPE)],
        compiler_params=_params(("parallel", "arbitrary")),
    )(x, w, w_in_p, *deps)


PROJ_BWD_TM, PROJ_BWD_TK = 1024, 1408


def _proj_bwd_dx(dproj, w_in_p, x, w, dres, name):
    t, d = x.shape
    kdim = dproj.shape[1]
    tm, tk = min(PROJ_BWD_TM, t), PROJ_BWD_TK
    nt, nk = t // tm, kdim // tk
    assert t % tm == 0 and kdim % tk == 0

    def body(a_ref, b_ref, x_ref, w_ref, dr_ref, dx_ref, dw_ref, acc, wacc):
        i, s = pl.program_id(0), pl.program_id(1)

        @pl.when((i == 0) & (s == 0))
        def _():
            wacc[...] = jnp.zeros_like(wacc)

        @pl.when(s == 0)
        def _():
            acc[...] = jnp.zeros_like(acc)

        acc[...] += _dot(a_ref[...], b_ref[...], NT)

        @pl.when(s == nk - 1)
        def _():
            xv = x_ref[...]
            rstd = lax.rsqrt(jnp.mean(xv * xv, axis=-1, keepdims=True) + EPS)
            xh = xv * rstd
            dhv = acc[...]
            g = dhv * w_ref[...]
            dx_ref[...] = dr_ref[...] + rstd * (g - xh * jnp.mean(g * xh, axis=-1, keepdims=True))
            wacc[...] += _rowsum8(dhv * xh)

        @pl.when((i == nt - 1) & (s == nk - 1))
        def _():
            dw_ref[...] = jnp.sum(wacc[...], axis=0, keepdims=True)

    row = pl.BlockSpec((tm, d), lambda i, s: (i, 0))
    vec = pl.BlockSpec((1, d), lambda i, s: (0, 0))
    return pl.pallas_call(
        body, name=name,
        out_shape=(jax.ShapeDtypeStruct((t, d), F32), jax.ShapeDtypeStruct((1, d), F32)),
        grid=(nt, nk),
        in_specs=[pl.BlockSpec((tm, tk), lambda i, s: (i, s)), pl.BlockSpec((d, tk), lambda i, s: (0, s)),
                  row, vec, row],
        out_specs=(row, vec),
        scratch_shapes=[pltpu.VMEM((tm, d), F32), pltpu.VMEM((SUBLANES, d), F32)],
        compiler_params=_params(("arbitrary", "arbitrary")),
    )(dproj, w_in_p, x, w, dres)


def _loss_head(xf, target, w, name):
    t, d = xf.shape
    tm = ROW_TILE
    nt = t // tm

    def body(x_ref, t_ref, w_ref, loss_ref, dx_ref, dw_ref, lacc, wacc):
        i = pl.program_id(0)

        @pl.when(i == 0)
        def _():
            lacc[...] = jnp.zeros_like(lacc)
            wacc[...] = jnp.zeros_like(wacc)

        xv = x_ref[...]
        rstd = lax.rsqrt(jnp.mean(xv * xv, axis=-1, keepdims=True) + EPS)
        xh = xv * rstd
        err = xh * w_ref[...] - t_ref[...]
        lacc[...] += jnp.sum(err * err)
        dy = err * (1.0 / d)
        g = dy * w_ref[...]
        dx_ref[...] = rstd * (g - xh * jnp.mean(g * xh, axis=-1, keepdims=True))
        wacc[...] += _rowsum8(dy * xh)

        @pl.when(i == nt - 1)
        def _():
            loss_ref[...] = lacc[...] * (0.5 / d)
            dw_ref[...] = jnp.sum(wacc[...], axis=0, keepdims=True)

    row = pl.BlockSpec((tm, d), lambda i: (i, 0))
    vec = pl.BlockSpec((1, d), lambda i: (0, 0))
    return pl.pallas_call(
        body, name=name,
        out_shape=(jax.ShapeDtypeStruct((SUBLANES, LANES), F32), jax.ShapeDtypeStruct((t, d), F32),
                   jax.ShapeDtypeStruct((1, d), F32)),
        grid=(nt,),
        in_specs=[row, row, vec],
        out_specs=(pl.BlockSpec((SUBLANES, LANES), lambda i: (0, 0)), row, vec),
        scratch_shapes=[pltpu.VMEM((SUBLANES, LANES), F32), pltpu.VMEM((SUBLANES, d), F32)],
        compiler_params=_params(("arbitrary",)),
    )(xf, target, w)


CONV_TILE = 512
CONV_TILE_SHORT = 1024
CONV_COLS = 512
CONV_SUB_ROWS = 128
CONV_SUB_COLS = LANES


def _conv_halo(k):
    return SUBLANES if k - 1 <= SUBLANES else 32


def _conv_tile(k, t):
    return min(CONV_TILE if _conv_use_shifted(k) else CONV_TILE_SHORT, t)


def _conv_subtiles(tm, cw):
    return [(r0, c0) for r0 in range(0, tm, CONV_SUB_ROWS) for c0 in range(0, cw, CONV_SUB_COLS)]


def _conv_use_shifted(k):
    return k > SUBLANES


def _conv_shift_scratch(k, rows, cw):
    return [pltpu.VMEM((SUBLANES - 1, rows - SUBLANES, cw), F32)] if _conv_use_shifted(k) else []


def _conv_fill_shifted(ext, sh):
    n = sh.shape[1]
    for b in range(1, SUBLANES):
        sh[b - 1] = ext[b:b + n, :]


def _conv_rows(ext, sh, start, rows, cs):
    b = start % SUBLANES
    if b == 0 or not sh:
        return ext[start:start + rows, cs]
    return sh[0][b - 1, start - b:start - b + rows, cs]


def _conv_fwd(src, col0, width, w, bias, k, seq, name):
    t = src.shape[0]
    tm, cw, halo = _conv_tile(k, src.shape[0]), CONV_COLS, _conv_halo(k)
    sr, sc = CONV_SUB_ROWS, CONV_SUB_COLS
    p = k - 1
    cb0 = col0 // cw
    kp = w.shape[0]

    shifted = _conv_use_shifted(k)

    def body(x_ref, h_ref, w_ref, b_ref, o_ref, ext, *sh):
        i = pl.program_id(0)
        seq_start = (i * tm) % seq == 0
        ext[halo:, :] = x_ref[...]
        ext[:halo, :] = jnp.where(seq_start, 0.0, h_ref[...])
        if shifted:
            _conv_fill_shifted(ext, sh[0])
        for r0, c0 in _conv_subtiles(tm, cw):
            cs = slice(c0, c0 + sc)
            acc = jnp.zeros((sr, sc), F32) + b_ref[:, cs]
            for j in range(k):
                acc = acc + w_ref[j:j + 1, cs] * _conv_rows(ext, sh, r0 + halo - p + j, sr, cs)
            o_ref[r0:r0 + sr, cs] = acc

    return pl.pallas_call(
        body, name=name,
        out_shape=jax.ShapeDtypeStruct((t, width), F32),
        grid=(t // tm, width // cw),
        in_specs=[pl.BlockSpec((tm, cw), lambda i, j: (i, cb0 + j)),
                  pl.BlockSpec((halo, cw), lambda i, j: (jnp.maximum(i * (tm // halo) - 1, 0), cb0 + j)),
                  pl.BlockSpec((kp, cw), lambda i, j: (0, j)),
                  pl.BlockSpec((1, cw), lambda i, j: (0, j))],
        out_specs=pl.BlockSpec((tm, cw), lambda i, j: (i, j)),
        scratch_shapes=[pltpu.VMEM((halo + tm, cw), F32)] + _conv_shift_scratch(k, halo + tm, cw),
        compiler_params=_params(("parallel", "parallel")),
    )(src, src, w, bias)


def _conv_bwd(dy, src, col0, width, w, k, seq, name, into=None):
    t = src.shape[0]
    tm, cw, halo = _conv_tile(k, src.shape[0]), CONV_COLS, _conv_halo(k)
    sr, sc = CONV_SUB_ROWS, CONV_SUB_COLS
    p = k - 1
    cb0 = col0 // cw
    kp = w.shape[0]
    nt = t // tm
    last_halo = t // halo - 1

    shifted = _conv_use_shifted(k)

    def body(dy_ref, dn_ref, x_ref, xp_ref, w_ref, *rest):
        if into is not None:
            rest = rest[1:]
        dx_ref, dw_ref, db_ref, dyext, xext, wacc, bacc = rest[:7]
        sh = rest[7:]
        i = pl.program_id(1)
        dysh, xsh = (sh[:1], sh[1:]) if shifted else ((), ())

        @pl.when(i == 0)
        def _():
            wacc[...] = jnp.zeros_like(wacc)
            bacc[...] = jnp.zeros_like(bacc)

        seq_start = (i * tm) % seq == 0
        seq_end = ((i + 1) * tm) % seq == 0
        dyext[:tm, :] = dy_ref[...]
        dyext[tm:, :] = jnp.where(seq_end, 0.0, dn_ref[...])
        xext[halo:, :] = x_ref[...]
        xext[:halo, :] = jnp.where(seq_start, 0.0, xp_ref[...])
        if shifted:
            _conv_fill_shifted(dyext, dysh[0])
            _conv_fill_shifted(xext, xsh[0])
        for r0, c0 in _conv_subtiles(tm, cw):
            cs = slice(c0, c0 + sc)
            dyv = dy_ref[r0:r0 + sr, cs]
            acc = jnp.zeros((sr, sc), F32)
            for j in range(k):
                acc = acc + w_ref[j:j + 1, cs] * _conv_rows(dyext, dysh, r0 + p - j, sr, cs)
                wacc[j, :, cs] += _rowsum8(dyv * _conv_rows(xext, xsh, r0 + halo - p + j, sr, cs))
            dx_ref[r0:r0 + sr, cs] = acc.astype(dx_ref.dtype)
            bacc[:, cs] += _rowsum8(dyv)

        @pl.when(i == nt - 1)
        def _():
            dw_ref[...] = jnp.zeros_like(dw_ref)
            for j in range(k):
                dw_ref[j:j + 1, :] = jnp.sum(wacc[j], axis=0, keepdims=True)
            db_ref[...] = jnp.sum(bacc[...], axis=0, keepdims=True)

    if into is None:
        dx_shape = jax.ShapeDtypeStruct((t, width), F32)
        dx_spec = pl.BlockSpec((tm, cw), lambda j, i: (i, j))
        extra_specs, extra_args, aliases = [], [], {}
    else:
        dx_shape = jax.ShapeDtypeStruct(into.shape, into.dtype)
        dx_spec = pl.BlockSpec((tm, cw), lambda j, i: (i, cb0 + j))
        extra_specs, extra_args, aliases = [ANY], [into], {5: 0}
    return pl.pallas_call(
        body, name=name,
        out_shape=(dx_shape, jax.ShapeDtypeStruct((kp, width), F32), jax.ShapeDtypeStruct((1, width), F32)),
        grid=(width // cw, nt),
        in_specs=[pl.BlockSpec((tm, cw), lambda j, i: (i, j)),
                  pl.BlockSpec((halo, cw), lambda j, i: (jnp.minimum((i + 1) * (tm // halo), last_halo), j)),
                  pl.BlockSpec((tm, cw), lambda j, i: (i, cb0 + j)),
                  pl.BlockSpec((halo, cw), lambda j, i: (jnp.maximum(i * (tm // halo) - 1, 0), cb0 + j)),
                  pl.BlockSpec((kp, cw), lambda j, i: (0, j))] + extra_specs,
        out_specs=(dx_spec,
                   pl.BlockSpec((kp, cw), lambda j, i: (0, j)),
                   pl.BlockSpec((1, cw), lambda j, i: (0, j))),
        input_output_aliases=aliases,
        scratch_shapes=[pltpu.VMEM((tm + halo, cw), F32), pltpu.VMEM((halo + tm, cw), F32),
                        pltpu.VMEM((kp, SUBLANES, cw), F32), pltpu.VMEM((SUBLANES, cw), F32)]
        + 2 * _conv_shift_scratch(k, halo + tm, cw),
        compiler_params=_params(("parallel", "arbitrary")),
    )(dy, dy, src, src, w, *extra_args)


def _conf_specs(tm, cw, halo, order):
    cb = OFF_CONF // cw

    def blk(col):
        return pl.BlockSpec((tm, cw), lambda *g: (order(*g), col))

    def prev(col):
        return pl.BlockSpec((halo, cw), lambda *g: (jnp.maximum(order(*g) * (tm // halo) - 1, 0), col))

    return blk(cb), prev(cb), blk(cb + 1), prev(cb + 1)


def _glu_window(ext, a_ref, ah_ref, g_ref, gh_ref, seq_start, halo):
    ext[halo:, :] = a_ref[...] * _sigmoid(g_ref[...])
    ext[:halo, :] = jnp.where(seq_start, 0.0, ah_ref[...] * _sigmoid(gh_ref[...]))


def _conf_fwd(proj, w, bias, ln_w, ln_b, ycat, seq, name):
    t = proj.shape[0]
    k = CONF_KERNEL
    tm, cw, halo = CONV_TILE, CONF_WIDTH, _conv_halo(k)
    sr, sc = CONV_SUB_ROWS, CONV_SUB_COLS
    p = k - 1
    kp = w.shape[0]

    def body(a_ref, ah_ref, g_ref, gh_ref, z_ref, w_ref, b_ref, lw_ref, lb_ref, _, c1_ref, y_ref, ext, sh):
        i = pl.program_id(0)
        _glu_window(ext, a_ref, ah_ref, g_ref, gh_ref, (i * tm) % seq == 0, halo)
        _conv_fill_shifted(ext, sh)
        for r0, c0 in _conv_subtiles(tm, cw):
            cs = slice(c0, c0 + sc)
            acc = jnp.zeros((sr, sc), F32) + b_ref[:, cs]
            for j in range(k):
                acc = acc + w_ref[j:j + 1, cs] * _conv_rows(ext, (sh,), r0 + halo - p + j, sr, cs)
            c1_ref[r0:r0 + sr, cs] = acc
        for r0 in range(0, tm, sr):
            rows = slice(r0, r0 + sr)
            cv = c1_ref[rows, :]
            xc = cv - jnp.mean(cv, axis=-1, keepdims=True)
            rstd = lax.rsqrt(jnp.mean(xc * xc, axis=-1, keepdims=True) + EPS)
            c2 = xc * rstd * lw_ref[...] + lb_ref[...]
            y_ref[rows, :] = (_silu(c2) * _silu(z_ref[rows, :])).astype(y_ref.dtype)

    vec = pl.BlockSpec((1, cw), lambda i: (0, 0))
    row = pl.BlockSpec((tm, cw), lambda i: (i, 0))
    return pl.pallas_call(
        body, name=name,
        out_shape=(jax.ShapeDtypeStruct((t, cw), F32), jax.ShapeDtypeStruct(ycat.shape, ycat.dtype)),
        grid=(t // tm,),
        in_specs=[*_conf_specs(tm, cw, halo, lambda i: i),
                  pl.BlockSpec((tm, cw), lambda i: (i, OFF_ZC // cw)),
                  pl.BlockSpec((kp, cw), lambda i: (0, 0)), vec, vec, vec, ANY],
        out_specs=(row, pl.BlockSpec((tm, cw), lambda i: (i, YCAT_CONF // cw))),
        input_output_aliases={9: 1},
        scratch_shapes=[pltpu.VMEM((halo + tm, cw), F32)] + _conv_shift_scratch(k, halo + tm, cw),
        compiler_params=_params(("parallel",)),
    )(proj, proj, proj, proj, proj, w, bias, ln_w, ln_b, ycat)


def _conf_bwd(dycat, proj, c1, w, ln_w, ln_b, dproj, seq, name):
    t = proj.shape[0]
    k = CONF_KERNEL
    tm, cw, halo = CONV_TILE, CONF_WIDTH, _conv_halo(k)
    sr, sc = CONV_SUB_ROWS, CONV_SUB_COLS
    p = k - 1
    kp = w.shape[0]
    nt = t // tm
    last_halo = t // halo - 1

    def body(dy_ref, dyn_ref, c_ref, cn_ref, z_ref, zn_ref, a_ref, ah_ref, g_ref, gh_ref, w_ref, lw_ref, lb_ref, _,
             grp_ref, dw_ref, db_ref, dlw_ref, dlb_ref, dyext, xext, wacc, bacc, lwacc, lbacc, dysh, xsh):
        i = pl.program_id(0)

        @pl.when(i == 0)
        def _():
            wacc[...] = jnp.zeros_like(wacc)
            bacc[...] = jnp.zeros_like(bacc)
            lwacc[...] = jnp.zeros_like(lwacc)
            lbacc[...] = jnp.zeros_like(lbacc)

        def post_bwd(dy, cv, zv):
            xc = cv - jnp.mean(cv, axis=-1, keepdims=True)
            rstd = lax.rsqrt(jnp.mean(xc * xc, axis=-1, keepdims=True) + EPS)
            xh = xc * rstd
            c2 = xh * lw_ref[...] + lb_ref[...]
            dz = dy * _silu(c2) * _dsilu(zv)
            dc2 = dy * _silu(zv) * _dsilu(c2)
            dxh = dc2 * lw_ref[...]
            dc = rstd * (dxh - jnp.mean(dxh, axis=-1, keepdims=True)
                         - xh * jnp.mean(dxh * xh, axis=-1, keepdims=True))
            return dc, dz, dc2 * xh, dc2

        seq_end = ((i + 1) * tm) % seq == 0
        for r0 in range(0, tm, sr):
            rows = slice(r0, r0 + sr)
            dc, dz, lw_terms, lb_terms = post_bwd(dy_ref[rows, :], c_ref[rows, :], z_ref[rows, :])
            dyext[rows, :] = dc
            grp_ref[rows, 2 * cw:] = dz.astype(grp_ref.dtype)
            lwacc[...] += _rowsum8(lw_terms)
            lbacc[...] += _rowsum8(lb_terms)
        dc_next = post_bwd(dyn_ref[...], cn_ref[...], zn_ref[...])[0]
        dyext[tm:, :] = jnp.where(seq_end, 0.0, dc_next)
        _glu_window(xext, a_ref, ah_ref, g_ref, gh_ref, (i * tm) % seq == 0, halo)
        _conv_fill_shifted(dyext, dysh)
        _conv_fill_shifted(xext, xsh)
        dag_ref = grp_ref
        for r0, c0 in _conv_subtiles(tm, cw):
            cs = slice(c0, c0 + sc)
            rows = slice(r0, r0 + sr)
            dyv = dyext[rows, cs]
            acc = jnp.zeros((sr, sc), F32)
            for j in range(k):
                acc = acc + w_ref[j:j + 1, cs] * _conv_rows(dyext, (dysh,), r0 + p - j, sr, cs)
                wacc[j, :, cs] += _rowsum8(dyv * _conv_rows(xext, (xsh,), r0 + halo - p + j, sr, cs))
            bacc[:, cs] += _rowsum8(dyv)
            s = _sigmoid(g_ref[rows, cs])
            dag_ref[rows, cs] = (acc * s).astype(dag_ref.dtype)
            dag_ref[rows, cw + c0:cw + c0 + sc] = (acc * a_ref[rows, cs] * s * (1.0 - s)).astype(dag_ref.dtype)

        @pl.when(i == nt - 1)
        def _():
            dw_ref[...] = jnp.zeros_like(dw_ref)
            for j in range(k):
                dw_ref[j:j + 1, :] = jnp.sum(wacc[j], axis=0, keepdims=True)
            db_ref[...] = jnp.sum(bacc[...], axis=0, keepdims=True)
            dlw_ref[...] = jnp.sum(lwacc[...], axis=0, keepdims=True)
            dlb_ref[...] = jnp.sum(lbacc[...], axis=0, keepdims=True)

    def blk(col):
        return pl.BlockSpec((tm, cw), lambda i: (i, col))

    def nxt(col):
        return pl.BlockSpec((halo, cw), lambda i: (jnp.minimum((i + 1) * (tm // halo), last_halo), col))

    vec = pl.BlockSpec((1, cw), lambda i: (0, 0))
    return pl.pallas_call(
        body, name=name,
        out_shape=(jax.ShapeDtypeStruct(dproj.shape, dproj.dtype), jax.ShapeDtypeStruct((kp, cw), F32),
                   jax.ShapeDtypeStruct((1, cw), F32), jax.ShapeDtypeStruct((1, cw), F32),
                   jax.ShapeDtypeStruct((1, cw), F32)),
        grid=(nt,),
        in_specs=[blk(YCAT_CONF // cw), nxt(YCAT_CONF // cw), blk(0), nxt(0), blk(OFF_ZC // cw), nxt(OFF_ZC // cw),
                  *_conf_specs(tm, cw, halo, lambda i: i),
                  pl.BlockSpec((kp, cw), lambda i: (0, 0)), vec, vec, ANY],
        out_specs=(pl.BlockSpec((tm, CONF_GROUP), lambda i: (i, OFF_CONF // CONF_GROUP)),
                   pl.BlockSpec((kp, cw), lambda i: (0, 0)), vec, vec, vec),
        input_output_aliases={13: 0},
        scratch_shapes=[pltpu.VMEM((tm + halo, cw), F32), pltpu.VMEM((halo + tm, cw), F32),
                        pltpu.VMEM((kp, SUBLANES, cw), F32), pltpu.VMEM((SUBLANES, cw), F32),
                        pltpu.VMEM((SUBLANES, cw), F32), pltpu.VMEM((SUBLANES, cw), F32)]
        + 2 * _conv_shift_scratch(k, halo + tm, cw),
        compiler_params=_params(("arbitrary",)),
    )(dycat, dycat, c1, c1, proj, proj, proj, proj, proj, proj, w, ln_w, ln_b, dproj)


def _half_mask(half):
    lane = _iota((1, LANES), 1)
    return ((lane >= half * ATTN_HEAD_DIM) & (lane < (half + 1) * ATTN_HEAD_DIM)).astype(F32)


def _stack_heads(xp, g):
    m = _half_mask(g)
    swapped = pltpu.roll(xp, ATTN_HEAD_DIM, axis=1)
    return jnp.concatenate([xp * m, swapped * m] if g == 0 else [swapped * m, xp * m], axis=0)


def _unstack_heads(both, g):
    w = both.shape[0] // 2
    top, bot = both[:w], both[w:]
    lo, hi = _half_mask(0), _half_mask(1)
    if g == 0:
        return top * lo + pltpu.roll(bot, ATTN_HEAD_DIM, axis=1) * hi
    return pltpu.roll(top, ATTN_HEAD_DIM, axis=1) * lo + bot * hi


def _band_mask(first_block):
    w = WINDOW
    qi = _iota((w, 2 * w), 0)
    kj = _iota((w, 2 * w), 1) - w
    rel = qi - kj
    return (rel >= 0) & (rel < w) & (jnp.logical_not(first_block) | (kj >= 0))


def _lane_pick(x, h):
    return jnp.sum(jnp.where(_iota(x.shape, 1) == h, x, 0.0), axis=1, keepdims=True)


def _attn_specs(nb, rev):
    w = WINDOW

    def blk(i):
        return nb - 1 - i if rev else i

    def row(b, i):
        return b * nb + blk(i)

    def prow(b, i):
        return b * nb + jnp.maximum(blk(i) - 1, 0)

    q = pl.BlockSpec((w, 512), lambda b, i: (row(b, i), OFF_Q // 512))
    kc = pl.BlockSpec((w, 128), lambda b, i: (row(b, i), OFF_K // 128))
    kp = pl.BlockSpec((w, 128), lambda b, i: (prow(b, i), OFF_K // 128))
    vc = pl.BlockSpec((w, 128), lambda b, i: (row(b, i), OFF_V // 128))
    vp = pl.BlockSpec((w, 128), lambda b, i: (prow(b, i), OFF_V // 128))
    z = pl.BlockSpec((w, 512), lambda b, i: (row(b, i), OFF_ZA // 512))
    return q, kc, kp, vc, vp, z, row


def _attn_fwd(proj, sinks, ycat, nbatch, name):
    t = proj.shape[0]
    w = WINDOW
    nb = t // nbatch // w
    scale = ATTN_HEAD_DIM ** -0.5
    q_s, kc_s, kp_s, vc_s, vp_s, z_s, row = _attn_specs(nb, False)

    def body(q_ref, kc_ref, kp_ref, vc_ref, vp_ref, z_ref, sk_ref, _, y_ref, o_ref, lse_ref):
        first = pl.program_id(1) == 0
        mask = _band_mask(first)
        kk = jnp.concatenate([kp_ref[...], kc_ref[...]], axis=0).astype(MXU_DTYPE)
        vv = jnp.concatenate([vp_ref[...], vc_ref[...]], axis=0).astype(MXU_DTYPE)
        sk = sk_ref[...]
        lane = _iota((w, LANES), 1)
        mask2 = jnp.concatenate([mask, mask], axis=0)
        scores = [_dot(_stack_heads(q_ref[:, j * LANES:(j + 1) * LANES], j // 2), kk, NT) for j in range(4)]
        lse_all = jnp.zeros((w, LANES), F32)
        for j in range(4):
            s = jnp.where(mask2, scores[j] * scale, -1e30)
            skc = jnp.concatenate([jnp.broadcast_to(_lane_pick(sk, 2 * j), (w, 1)),
                                   jnp.broadcast_to(_lane_pick(sk, 2 * j + 1), (w, 1))], axis=0)
            m = jnp.maximum(jnp.max(s, axis=1, keepdims=True), skc)
            den = jnp.sum(jnp.exp(s - m), axis=1, keepdims=True) + jnp.exp(skc - m)
            lse = m + jnp.log(den)
            lse_all = jnp.where(lane == 2 * j, lse[:w], lse_all)
            lse_all = jnp.where(lane == 2 * j + 1, lse[w:], lse_all)
            op = _unstack_heads(_dot(jnp.exp(s - lse), vv), j // 2)
            cols = slice(j * LANES, (j + 1) * LANES)
            o_ref[:, cols] = op
            y_ref[:, cols] = (op * _silu(z_ref[:, cols])).astype(y_ref.dtype)
        lse_ref[...] = lse_all

    return pl.pallas_call(
        body, name=name,
        out_shape=(jax.ShapeDtypeStruct(ycat.shape, ycat.dtype), jax.ShapeDtypeStruct((t, 512), F32),
                   jax.ShapeDtypeStruct((t, LANES), F32)),
        grid=(nbatch, nb),
        in_specs=[q_s, kc_s, kp_s, vc_s, vp_s, z_s, pl.BlockSpec((1, LANES), lambda b, i: (0, 0)), ANY],
        out_specs=(pl.BlockSpec((w, 512), lambda b, i: (row(b, i), YCAT_ATTN // 512)),
                   pl.BlockSpec((w, 512), lambda b, i: (row(b, i), 0)),
                   pl.BlockSpec((w, LANES), lambda b, i: (row(b, i), 0))),
        input_output_aliases={7: 0},
        compiler_params=_params(("parallel", "parallel")),
    )(proj, proj, proj, proj, proj, proj, sinks, ycat)


def _attn_bwd(dycat, proj, o, lse, sinks, ddt, dproj, nbatch, name):
    t = proj.shape[0]
    w = WINDOW
    nb = t // nbatch // w
    scale = ATTN_HEAD_DIM ** -0.5
    q_s, kc_s, kp_s, vc_s, vp_s, z_s, row = _attn_specs(nb, True)

    def body(dy_ref, q_ref, kc_ref, kp_ref, vc_ref, vp_ref, z_ref, o_ref, lse_ref, sk_ref, ddt_ref, _,
             grp_ref, dsk_ref, kcarry, vcarry, sacc):
        b, i = pl.program_id(0), pl.program_id(1)

        @pl.when((b == 0) & (i == 0))
        def _():
            sacc[...] = jnp.zeros_like(sacc)

        @pl.when(i == 0)
        def _():
            kcarry[...] = jnp.zeros_like(kcarry)
            vcarry[...] = jnp.zeros_like(vcarry)

        first = i == nb - 1
        mask = _band_mask(first)
        kk = jnp.concatenate([kp_ref[...], kc_ref[...]], axis=0).astype(MXU_DTYPE)
        vv = jnp.concatenate([vp_ref[...], vc_ref[...]], axis=0).astype(MXU_DTYPE)
        sk = sk_ref[...]
        lse_all = lse_ref[...]
        lane1 = _iota((1, LANES), 1)
        mask2 = jnp.concatenate([mask, mask], axis=0)
        qs, dos, deltas, lses, scores, dps = [], [], [], [], [], []
        for j in range(4):
            cols = slice(j * LANES, (j + 1) * LANES)
            qp, zp, ov, dy = q_ref[:, cols], z_ref[:, cols], o_ref[:, cols], dy_ref[:, cols]
            grp_ref[:, OFF_ZA + j * LANES:OFF_ZA + (j + 1) * LANES] = (dy * ov * _dsilu(zp)).astype(grp_ref.dtype)
            do = dy * _silu(zp)
            q2 = _stack_heads(qp, j // 2).astype(MXU_DTYPE)
            do2 = _stack_heads(do, j // 2)
            qs.append(q2)
            dos.append(do2.astype(MXU_DTYPE))
            deltas.append(jnp.sum(do2 * _stack_heads(ov, j // 2), axis=1, keepdims=True))
            lses.append(jnp.concatenate([_lane_pick(lse_all, 2 * j), _lane_pick(lse_all, 2 * j + 1)], axis=0))
            scores.append(_dot(q2, kk, NT))
            dps.append(_dot(do2, vv, NT))
        prs, dss = [], []
        dsk = jnp.zeros((1, LANES), F32)
        for j in range(4):
            pr = jnp.exp(jnp.where(mask2, scores[j] * scale, -1e30) - lses[j])
            prs.append(pr.astype(MXU_DTYPE))
            dss.append((pr * (dps[j] - deltas[j])).astype(MXU_DTYPE))
            skc = jnp.concatenate([jnp.broadcast_to(_lane_pick(sk, 2 * j), (w, 1)),
                                   jnp.broadcast_to(_lane_pick(sk, 2 * j + 1), (w, 1))], axis=0)
            sink_term = jnp.exp(skc - lses[j]) * deltas[j]
            dsk = dsk - jnp.where(lane1 == 2 * j, jnp.sum(sink_term[:w]), 0.0)
            dsk = dsk - jnp.where(lane1 == 2 * j + 1, jnp.sum(sink_term[w:]), 0.0)
        dkk = jnp.zeros((2 * w, LANES), F32)
        dvv = jnp.zeros((2 * w, LANES), F32)
        for j in range(4):
            dq = _unstack_heads(_dot(dss[j], kk) * scale, j // 2)
            grp_ref[:, OFF_Q + j * LANES:OFF_Q + (j + 1) * LANES] = dq.astype(grp_ref.dtype)
            dkk = dkk + _dot(dss[j], qs[j], TN) * scale
            dvv = dvv + _dot(prs[j], dos[j], TN)
        grp_ref[:, OFF_K:OFF_K + LANES] = (dkk[w:, :] + kcarry[...]).astype(grp_ref.dtype)
        grp_ref[:, OFF_V:OFF_V + LANES] = (dvv[w:, :] + vcarry[...]).astype(grp_ref.dtype)
        grp_ref[:, OFF_DT:OFF_DT + LANES] = ddt_ref[...].astype(grp_ref.dtype)
        grp_ref[:, OFF_DT + LANES:] = jnp.zeros((w, ATTN_GROUP - OFF_DT - LANES), grp_ref.dtype)
        kcarry[...] = dkk[:w, :]
        vcarry[...] = dvv[:w, :]
        sacc[...] += dsk

        @pl.when((b == nbatch - 1) & (i == nb - 1))
        def _():
            dsk_ref[...] = sacc[...]

    return pl.pallas_call(
        body, name=name,
        out_shape=(jax.ShapeDtypeStruct(dproj.shape, dproj.dtype), jax.ShapeDtypeStruct((1, LANES), F32)),
        grid=(nbatch, nb),
        in_specs=[pl.BlockSpec((w, 512), lambda b, i: (row(b, i), YCAT_ATTN // 512)),
                  q_s, kc_s, kp_s, vc_s, vp_s, z_s,
                  pl.BlockSpec((w, 512), lambda b, i: (row(b, i), 0)),
                  pl.BlockSpec((w, LANES), lambda b, i: (row(b, i), 0)),
                  pl.BlockSpec((1, LANES), lambda b, i: (0, 0)),
                  pl.BlockSpec((w, LANES), lambda b, i: (row(b, i), 0)), ANY],
        out_specs=(pl.BlockSpec((w, ATTN_GROUP), lambda b, i: (row(b, i), 0)),
                   pl.BlockSpec((1, LANES), lambda b, i: (0, 0))),
        input_output_aliases={11: 0},
        scratch_shapes=[pltpu.VMEM((w, LANES), F32), pltpu.VMEM((w, LANES), F32),
                        pltpu.VMEM((1, LANES), F32)],
        compiler_params=_params(("arbitrary", "arbitrary")),
    )(dycat, proj, proj, proj, proj, proj, proj, o, lse, sinks, ddt, dproj)


SSD_WIDTH = SSD_HEADS * SSD_HEAD_DIM
GROUP_ROWS = SSD_WIDTH // 2


def _expand_mat():
    r, c = _iota((LANES, SSD_WIDTH), 0), _iota((LANES, SSD_WIDTH), 1)
    return (r == lax.shift_right_logical(c, 6)).astype(BF16)


def _expand_mat_t():
    r, c = _iota((SSD_WIDTH, LANES), 0), _iota((SSD_WIDTH, LANES), 1)
    return (c == lax.shift_right_logical(r, 6)).astype(BF16)


def _ssd_common(u_ref, dt_ref, dtb_ref, a_ref, stack_broadcasts=False):
    q = CHUNK
    act = _silu(u_ref[...])
    xs = act[:, :SSD_WIDTH]
    bm = act[:, SSD_WIDTH:SSD_WIDTH + 256]
    cm = act[:, SSD_WIDTH + 256:]
    dtp = _softplus(dt_ref[...] + dtb_ref[...])
    a = dtp * a_ref[...]
    tril = (_iota((q, q), 0) >= _iota((q, q), 1)).astype(BF16)
    acs = _xdot_r(tril, a)
    acs_t = acs.T
    e = _expand_mat()
    a_end = jnp.sum(jnp.where(_iota(acs.shape, 0) == q - 1, acs, 0.0), axis=0, keepdims=True)
    if stack_broadcasts:
        spread = _xdot(jnp.concatenate([dtp, acs, a_end - acs], axis=0), e)
        dt_x, ea, dec = spread[:q], jnp.exp(spread[q:2 * q]), jnp.exp(spread[2 * q:])
    else:
        dt_x = _xdot(dtp, e)
        ea = jnp.exp(_xdot(acs, e))
        dec = jnp.exp(_xdot(a_end - acs, e))
    a_end_col = jnp.broadcast_to(_lane_pick(acs_t, q - 1), (LANES, LANES))
    s_scale = jnp.exp(_xdot_r(_expand_mat_t(), a_end_col))
    return act, xs, bm, cm, dtp, acs, acs_t, dt_x, ea, dec, s_scale, tril


def _decay_mat(acs, acs_t, h):
    q = CHUNK
    col = _lane_pick(acs, h)
    rowv = jnp.sum(jnp.where(_iota(acs_t.shape, 0) == h, acs_t, 0.0), axis=0, keepdims=True)
    causal = _iota((q, q), 0) >= _iota((q, q), 1)
    return jnp.exp(jnp.where(causal, col - rowv, -1e30))


GN_WIDTH = 512


def _ssd_fwd(u, proj, dtb, a_neg, d_x, norm_w, ycat, nbatch, name):
    t = u.shape[0]
    q = CHUNK
    nc = t // nbatch // q

    def body(u_ref, dt_ref, z_ref, dtb_ref, a_ref, dx_ref, nw_ref, _, y_ref, st_ref, yn_ref, state):
        c = pl.program_id(1)

        @pl.when(c == 0)
        def _():
            state[...] = jnp.zeros_like(state)

        st_ref[...] = state[...]
        act, xs, bm, cm, dtp, acs, acs_t, dt_x, ea, dec, s_scale, _ = _ssd_common(u_ref, dt_ref, dtb_ref, a_ref)
        xdt = xs * dt_x
        xdec = xdt * dec
        lo, hi = _half_mask(0), _half_mask(1)
        grp = []
        for g in range(2):
            bg = bm[:, g * LANES:(g + 1) * LANES]
            cg = cm[:, g * LANES:(g + 1) * LANES]
            rows = slice(g * GROUP_ROWS, (g + 1) * GROUP_ROWS)
            sg = state[rows, :]
            grp.append((_dot(cg, bg, NT), _dot(cg, sg, NT), rows,
                        s_scale[rows, :] * sg + _dot(xdec[:, rows], bg, TN)))
        yps = []
        for pj in range(SSD_HEADS // 2):
            cb = grp[pj // 4][0]
            xp = xdt[:, pj * LANES:(pj + 1) * LANES]
            m2 = jnp.concatenate([cb * _decay_mat(acs, acs_t, 2 * pj), cb * _decay_mat(acs, acs_t, 2 * pj + 1)],
                                 axis=1)
            yps.append(_dot(m2, jnp.concatenate([xp * lo, xp * hi], axis=0)))
        for g in range(2):
            _, yoff, rows, state_new = grp[g]
            for j in range(4):
                pj = g * 4 + j
                cols = slice(pj * LANES, (pj + 1) * LANES)
                yp = yps[pj] + yoff[:, j * LANES:(j + 1) * LANES] * ea[:, cols]
                y_ref[:, cols] = yp + dx_ref[:, cols] * xs[:, cols]
            state[rows, :] = state_new
        for g in range(SSD_WIDTH // GN_WIDTH):
            cols = slice(g * GN_WIDTH, (g + 1) * GN_WIDTH)
            gg = y_ref[:, cols] * _silu(z_ref[:, cols])
            rstd = lax.rsqrt(jnp.mean(gg * gg, axis=-1, keepdims=True) + EPS)
            yn_ref[:, cols] = (gg * rstd * nw_ref[:, cols]).astype(yn_ref.dtype)

    vec = pl.BlockSpec((1, LANES), lambda b, c: (0, 0))
    wide = pl.BlockSpec((q, SSD_WIDTH), lambda b, c: (b * nc + c, 0))
    wvec = pl.BlockSpec((1, SSD_WIDTH), lambda b, c: (0, 0))
    return pl.pallas_call(
        body, name=name,
        out_shape=(jax.ShapeDtypeStruct((t, SSD_WIDTH), F32),
                   jax.ShapeDtypeStruct((nbatch * nc * SSD_WIDTH, SSD_STATE), F32),
                   jax.ShapeDtypeStruct(ycat.shape, ycat.dtype)),
        grid=(nbatch, nc),
        in_specs=[pl.BlockSpec((q, SSD_CONV_DIM), lambda b, c: (b * nc + c, 0)),
                  pl.BlockSpec((q, LANES), lambda b, c: (b * nc + c, OFF_DT // LANES)),
                  pl.BlockSpec((q, SSD_WIDTH), lambda b, c: (b * nc + c, OFF_ZS // SSD_WIDTH)),
                  vec, vec, wvec, wvec, ANY],
        out_specs=(wide, pl.BlockSpec((SSD_WIDTH, SSD_STATE), lambda b, c: (b * nc + c, 0)), wide),
        input_output_aliases={7: 2},
        scratch_shapes=[pltpu.VMEM((SSD_WIDTH, SSD_STATE), F32)],
        compiler_params=_params(("parallel", "arbitrary")),
    )(u, proj, proj, dtb, a_neg, d_x, norm_w, ycat)


def _ssd_bwd(dycat, u, proj, y, states, dtb, a_neg, d_x, norm_w, dproj, nbatch, name):
    t = u.shape[0]
    q = CHUNK
    nc = t // nbatch // q

    def body(do_ref, u_ref, dt_ref, z_ref, y_ref, st_ref, dtb_ref, a_ref, dx_ref, nw_ref, _,
             du_ref, dz_ref, ddt_ref, dal_ref, dd_ref, dtbg_ref, dnw_ref, dstate, acc_a, acc_d, acc_b, acc_w):
        b, c = pl.program_id(0), pl.program_id(1)

        @pl.when((b == 0) & (c == 0))
        def _():
            acc_a[...] = jnp.zeros_like(acc_a)
            acc_d[...] = jnp.zeros_like(acc_d)
            acc_b[...] = jnp.zeros_like(acc_b)
            acc_w[...] = jnp.zeros_like(acc_w)

        @pl.when(c == 0)
        def _():
            dstate[...] = jnp.zeros_like(dstate)

        dy_parts = []
        for g in range(SSD_WIDTH // GN_WIDTH):
            cols = slice(g * GN_WIDTH, (g + 1) * GN_WIDTH)
            yv, zv, dov = y_ref[:, cols], z_ref[:, cols], do_ref[:, cols]
            sz = _silu(zv)
            gg = yv * sz
            rstd = lax.rsqrt(jnp.mean(gg * gg, axis=-1, keepdims=True) + EPS)
            gh = gg * rstd
            acc_w[:, cols] += _rowsum8(dov * gh)
            dgn = dov * nw_ref[:, cols]
            dg = rstd * (dgn - gh * jnp.mean(dgn * gh, axis=-1, keepdims=True))
            dy_parts.append(dg * sz)
            dz_ref[:, cols] = (dg * yv * _dsilu(zv)).astype(dz_ref.dtype)

        act, xs, bm, cm, dtp, acs, acs_t, dt_x, ea, dec, s_scale, tril = _ssd_common(
            u_ref, dt_ref, dtb_ref, a_ref, stack_broadcasts=True)
        xdt = xs * dt_x
        xdec = xdt * dec
        dyv = jnp.concatenate(dy_parts, axis=1)
        dye = dyv * ea
        lo, hi = _half_mask(0), _half_mask(1)
        et = _expand_mat_t()
        grp = []
        for g in range(2):
            rows = slice(g * GROUP_ROWS, (g + 1) * GROUP_ROWS)
            bg = bm[:, g * LANES:(g + 1) * LANES]
            cg = cm[:, g * LANES:(g + 1) * LANES]
            sg = st_ref[rows, :]
            dsg = dstate[rows, :]
            grp.append(dict(
                rows=rows, bg=bg, cg=cg, dsg=dsg,
                cb=_dot(cg, bg, NT), yoff=_dot(cg, sg, NT), dxst=_dot(bg, dsg, NT) * dec[:, rows],
                dc_off=_dot(dye[:, rows], sg), db_off=_dot(xdec[:, rows], dsg),
                s_carried=s_scale[rows, :] * sg,
                dstate_new=_dot(dye[:, rows], cg, TN) + s_scale[rows, :] * dsg))
        dy2s, g2s, l2s = [], [], []
        for pj in range(SSD_HEADS // 2):
            cols = slice(pj * LANES, (pj + 1) * LANES)
            dyp = dyv[:, cols]
            dy2 = jnp.concatenate([dyp * lo, dyp * hi], axis=0).astype(MXU_DTYPE)
            dy2s.append(dy2)
            g2s.append(_dot(dy2, xdt[:, cols], NT))
            l2s.append(jnp.concatenate([_decay_mat(acs, acs_t, 2 * pj), _decay_mat(acs, acs_t, 2 * pj + 1)], axis=0))
        dal_diag = jnp.zeros((q, LANES), F32)
        lane2 = _iota((2 * q, LANES), 1)
        row2 = _iota((2 * q, LANES), 0)
        dxdt_parts, db_parts, dc_parts = [], [], []
        end_sum = jnp.zeros((LANES, LANES), F32)
        for g in range(2):
            gd = grp[g]
            cb2 = jnp.concatenate([gd["cb"], gd["cb"]], axis=0)
            dcb = jnp.zeros((q, q), F32)
            parts = []
            for j in range(4):
                pj = g * 4 + j
                gl = g2s[pj] * l2s[pj]
                dcb = dcb + gl[:q] + gl[q:]
                m2 = cb2 * l2s[pj]
                parts.append(_dot(m2, dy2s[pj], TN))
                w2 = (gl * cb2).astype(MXU_DTYPE)
                sel2 = (lane2 == 2 * pj + (row2 >= q).astype(jnp.int32)).astype(MXU_DTYPE)
                dal_diag = dal_diag + _dot(jnp.concatenate([w2[:q], w2[q:]], axis=1), sel2) - _dot(w2, sel2, TN)
            dxdt_parts.append(jnp.concatenate(parts, axis=1) + gd["dxst"])
            dc_parts.append(_dot(dcb, gd["bg"]) + gd["dc_off"])
            db_parts.append(_dot(dcb, gd["cg"], TN) + gd["db_off"])
            end_sum = end_sum + _xdot(gd["dsg"] * gd["s_carried"], et[gd["rows"], :], TN, passes=2)
            dstate[gd["rows"], :] = gd["dstate_new"]
        dxst_parts = [gd["dxst"] for gd in grp]
        yoff_parts = [gd["yoff"] for gd in grp]
        dxdt = jnp.concatenate(dxdt_parts, axis=1)
        dxv = dx_ref[...]
        yoff = jnp.concatenate(yoff_parts, axis=1) * ea
        per_head = _xdot(jnp.concatenate([dyv * yoff, xdt * jnp.concatenate(dxst_parts, axis=1),
                                          dxdt * xs, dyv * xs], axis=0), et)
        off_term, st_term, dx_term, d_term = (per_head[k * q:(k + 1) * q] for k in range(4))
        dalpha = dal_diag + off_term - st_term
        end_row = jnp.sum(end_sum, axis=0, keepdims=True) + jnp.sum(st_term, axis=0, keepdims=True)
        dalpha = dalpha + jnp.where(_iota((q, LANES), 0) == q - 1, end_row, 0.0)
        da = _xdot_r(tril, dalpha, TN)
        ddtp = da * a_ref[...] + dx_term
        acc_a[...] += _rowsum8(da * dtp)
        acc_d[...] += _rowsum8(d_term)
        ddt_raw = ddtp * _sigmoid(dt_ref[...] + dtb_ref[...])
        acc_b[...] += _rowsum8(ddt_raw)
        ddt_ref[...] = ddt_raw
        dxs = dxdt * dt_x + dxv * dyv
        dact = jnp.concatenate([dxs] + db_parts + dc_parts, axis=1)
        du_ref[...] = dact * _dsilu(u_ref[...])

        @pl.when((b == nbatch - 1) & (c == nc - 1))
        def _():
            dal_ref[...] = jnp.sum(acc_a[...], axis=0, keepdims=True) * a_ref[...]
            dd_ref[...] = jnp.sum(acc_d[...], axis=0, keepdims=True)
            dtbg_ref[...] = jnp.sum(acc_b[...], axis=0, keepdims=True)
            dnw_ref[...] = jnp.sum(acc_w[...], axis=0, keepdims=True)

    def rowblk(b, c):
        return b * nc + (nc - 1 - c)

    vec = pl.BlockSpec((1, LANES), lambda b, c: (0, 0))
    wvec = pl.BlockSpec((1, SSD_WIDTH), lambda b, c: (0, 0))
    wide = pl.BlockSpec((q, SSD_WIDTH), lambda b, c: (rowblk(b, c), 0))
    zblk = pl.BlockSpec((q, SSD_WIDTH), lambda b, c: (rowblk(b, c), OFF_ZS // SSD_WIDTH))
    return pl.pallas_call(
        body, name=name,
        out_shape=(jax.ShapeDtypeStruct((t, SSD_CONV_DIM), F32), jax.ShapeDtypeStruct(dproj.shape, dproj.dtype),
                   jax.ShapeDtypeStruct((t, LANES), F32),
                   jax.ShapeDtypeStruct((1, LANES), F32), jax.ShapeDtypeStruct((1, LANES), F32),
                   jax.ShapeDtypeStruct((1, LANES), F32), jax.ShapeDtypeStruct((1, SSD_WIDTH), F32)),
        grid=(nbatch, nc),
        in_specs=[wide,
                  pl.BlockSpec((q, SSD_CONV_DIM), lambda b, c: (rowblk(b, c), 0)),
                  pl.BlockSpec((q, LANES), lambda b, c: (rowblk(b, c), OFF_DT // LANES)),
                  zblk, wide,
                  pl.BlockSpec((SSD_WIDTH, SSD_STATE), lambda b, c: (rowblk(b, c), 0)),
                  vec, vec, wvec, wvec, ANY],
        out_specs=(pl.BlockSpec((q, SSD_CONV_DIM), lambda b, c: (rowblk(b, c), 0)),
                   zblk,
                   pl.BlockSpec((q, LANES), lambda b, c: (rowblk(b, c), 0)),
                   vec, vec, vec, wvec),
        input_output_aliases={10: 1},
        scratch_shapes=[pltpu.VMEM((SSD_WIDTH, SSD_STATE), F32), pltpu.VMEM((SUBLANES, LANES), F32),
                        pltpu.VMEM((SUBLANES, LANES), F32), pltpu.VMEM((SUBLANES, LANES), F32),
                        pltpu.VMEM((SUBLANES, SSD_WIDTH), F32)],
        compiler_params=_params(("arbitrary", "arbitrary")),
    )(dycat, u, proj, proj, y, states, dtb, a_neg, d_x, norm_w, dproj)


def _pad_rows(w, rows):
    return jnp.concatenate([w, jnp.zeros((rows - w.shape[0], w.shape[1]), w.dtype)], axis=0)


def _pad_lanes(v):
    return jnp.concatenate([v, jnp.zeros((LANES - v.shape[0],), v.dtype)]).reshape(1, LANES)


def _padded_from_chips(pieces):
    cols = pieces[0].shape[-1]
    lead = pieces[0].shape[:-1]
    parts, pos = [], 0
    for lo, hi, start in sorted(SECTIONS, key=lambda s: s[2]):
        if start > pos:
            parts.append(jnp.zeros(lead + (start - pos,), pieces[0].dtype))
        pos = start + hi - lo
        while lo < hi:
            p = lo // cols
            end = min(hi, (p + 1) * cols)
            parts.append(pieces[p][..., lo - p * cols:end - p * cols])
            lo = end
    if pos < NP:
        parts.append(jnp.zeros(lead + (NP - pos,), pieces[0].dtype))
    return jnp.concatenate(parts, axis=-1)


def _chip_part_from_padded(wp, p, cols):
    lo, hi = p * cols, (p + 1) * cols
    parts = []
    for rs, re, start in SECTIONS:
        a, b = max(lo, rs), min(hi, re)
        if a < b:
            parts.append(wp[..., start + a - rs:start + b - rs])
    return jnp.concatenate(parts, axis=-1)


def _layer_params(li, w_in_p, w_out, conv_w, dw_w, small):
    return dict(
        w_in_p=w_in_p, w_out=w_out,
        conv_w=_pad_rows(conv_w, SUBLANES), dw_w=_pad_rows(dw_w, 32),
        norm_w=small["norm_w"][li].reshape(1, -1),
        conv_b=small["ssd_conv_b"][li].reshape(1, -1),
        dtb=_pad_lanes(small["ssd_dt_bias"][li]),
        a_neg=_pad_lanes(-jnp.exp(small["ssd_a_log"][li])),
        d_x=jnp.repeat(small["ssd_d"][li], SSD_HEAD_DIM).reshape(1, -1),
        ssd_norm_w=small["ssd_norm_w"][li].reshape(1, -1),
        sinks=_pad_lanes(small["attn_sinks"][li]),
        dw_b=small["conf_dw_b"][li].reshape(1, -1),
        ln_w=small["conf_ln_w"][li].reshape(1, -1),
        ln_b=small["conf_ln_b"][li].reshape(1, -1),
    )


def _layer_fwd(x, p, nbatch, seq, tag, after=None):
    proj, h_t = _proj_fwd(x, p["norm_w"], p["w_in_p"], name=f"proj_fwd_{tag}", after=after)
    u = _conv_fwd(proj, OFF_XBC, SSD_CONV_DIM, p["conv_w"], p["conv_b"], SSD_CONV, seq, name=f"ssd_conv_fwd_{tag}")
    ycat = lax.empty((x.shape[0], MIX_WIDTH), MXU_DTYPE)
    y, states, ycat = _ssd_fwd(u, proj, p["dtb"], p["a_neg"], p["d_x"], p["ssd_norm_w"], ycat, nbatch,
                               name=f"ssd_fwd_{tag}")
    ycat, o, lse = _attn_fwd(proj, p["sinks"], ycat, nbatch, name=f"attn_fwd_{tag}")
    c1, ycat = _conf_fwd(proj, p["dw_w"], p["dw_b"], p["ln_w"], p["ln_b"], ycat, seq, name=f"conf_fwd_{tag}")
    w_out = p["w_out"](ycat) if callable(p["w_out"]) else p["w_out"]
    x_new = _matmul(ycat, w_out, "nn", F32, 1024, 512, 2048, name=f"out_fwd_{tag}", residual=x)
    return x_new, dict(x=x, w_out=w_out, h_t=h_t, proj=proj, u=u, y=y, states=states, o=o, lse=lse, c1=c1, ycat=ycat)


def _layer_bwd(dx_out, p, s, nbatch, seq, tag, hooks=None):
    hooks = hooks or {}
    proj = s["proj"]
    dycat = _matmul(dx_out, s["w_out"], "nt", F32, 1024, 1024, 1024, name=f"out_bwd_dy_{tag}",
                    after=hooks.get("start_token"))
    dw_out = _matmul(s["ycat"], dx_out, "tn", F32, 1024, 1024, 1024, name=f"out_bwd_dw_{tag}")
    token = hooks["after_dycat"](dycat) if "after_dycat" in hooks else None
    dtb = p["dtb"] if token is None else p["dtb"] + token[0, 0]
    dproj = lax.empty(proj.shape, MXU_DTYPE)
    du, dproj, ddt, da_log, dd, ddtb, dssd_norm_w = _ssd_bwd(
        dycat, s["u"], proj, s["y"], s["states"], dtb, p["a_neg"], p["d_x"], p["ssd_norm_w"], dproj,
        nbatch, name=f"ssd_bwd_{tag}")
    dproj, dconv_w, dconv_b = _conv_bwd(du, proj, OFF_XBC, SSD_CONV_DIM, p["conv_w"], SSD_CONV, seq,
                                        name=f"ssd_conv_bwd_{tag}", into=dproj)
    dproj, dsinks = _attn_bwd(dycat, proj, s["o"], s["lse"], p["sinks"], ddt, dproj, nbatch,
                              name=f"attn_bwd_{tag}")
    if "after_attn" in hooks:
        hooks["after_attn"](dproj)
    dproj, ddw_w, ddw_b, dln_w, dln_b = _conf_bwd(dycat, proj, s["c1"], p["dw_w"], p["ln_w"], p["ln_b"], dproj, seq,
                                                  name=f"conf_bwd_{tag}")
    dw_in_p = _matmul(s["h_t"], dproj, "nn", F32, 1024, 512, 4096, name=f"proj_bwd_dw_{tag}")
    token = hooks["after_dw"](dw_in_p, dw_out) if "after_dw" in hooks else None
    norm_w = p["norm_w"] if token is None else p["norm_w"] + token[0, 0]
    dx_in, dnorm_w = _proj_bwd_dx(dproj, p["w_in_p"], s["x"], norm_w, dx_out, name=f"proj_bwd_dx_{tag}")
    grads = dict(
        norm_w=dnorm_w[0], w_in_p=dw_in_p, ssd_conv_w=dconv_w[:SSD_CONV], ssd_conv_b=dconv_b[0],
        ssd_dt_bias=ddtb[0, :SSD_HEADS], ssd_a_log=da_log[0, :SSD_HEADS], ssd_d=dd[0, :SSD_HEADS],
        ssd_norm_w=dssd_norm_w[0], attn_sinks=dsinks[0, :ATTN_Q_HEADS], conf_dw_w=ddw_w[:CONF_KERNEL],
        conf_dw_b=ddw_b[0], conf_ln_w=dln_w[0], conf_ln_b=dln_b[0], w_out=dw_out)
    return dx_in, grads


def _local_step(x, target, param_fns, final_norm_w, first_after=None, bwd_hooks=None):
    nbatch, seq, d = x.shape
    xt = x.reshape(nbatch * seq, d)
    saved, layer_params = [], []
    for li, fn in enumerate(param_fns):
        p = fn(xt)
        layer_params.append(p)
        xt, s = _layer_fwd(xt, p, nbatch, seq, f"l{li}", after=first_after if li == 0 else None)
        saved.append(s)
    loss, dx, dfinal = _loss_head(xt, target.reshape(nbatch * seq, d), final_norm_w.reshape(1, d), name="loss_head")
    grads = [None] * len(layer_params)
    for li in reversed(range(len(layer_params))):
        hooks = bwd_hooks(li) if bwd_hooks is not None else None
        dx, grads[li] = _layer_bwd(dx, layer_params[li], saved[li], nbatch, seq, f"l{li}", hooks=hooks)
    return loss[0, 0], dx.reshape(nbatch, seq, d), grads, dfinal[0]


MESH = pl.DeviceIdType.MESH
N_CHIPS = 4


def _mesh_pos():
    return lax.axis_index("x"), lax.axis_index("y"), lax.axis_index("c")


def _other_chips(x, y):
    return [(1 - x, y), (x, 1 - y), (1 - x, 1 - y)]


def _gather_weights(big, small, name):
    nbig, nsmall = len(big), len(small)
    n_ici = 3 * (nbig + nsmall)
    n_fwd = 3 * nbig

    def body(*refs):
        ins = refs[:nbig + nsmall]
        outs = refs[nbig + nsmall:2 * (nbig + nsmall)]
        send_sems, recv_sems = refs[2 * (nbig + nsmall):]
        x, y, c = _mesh_pos()
        me = 2 * x + y
        sibling = (x, y, 1 - c)
        chips = _other_chips(x, y)

        def ici(a, j, origin, dest):
            if a < nbig:
                src = ins[a].at[c] if origin is None else outs[a].at[origin, c]
                dst = outs[a].at[me if origin is None else origin, c]
            else:
                src = ins[a] if origin is None else outs[a].at[origin]
                dst = outs[a].at[me if origin is None else origin]
            k = a * 3 + j
            return pltpu.make_async_remote_copy(src_ref=src, dst_ref=dst, send_sem=send_sems.at[k],
                                                recv_sem=recv_sems.at[k], device_id=dest, device_id_type=MESH)

        def fwd(a, j, origin, half):
            k = n_ici + a * 3 + j
            ref = outs[a].at[origin, half]
            return pltpu.make_async_remote_copy(src_ref=ref, dst_ref=ref, send_sem=send_sems.at[k],
                                                recv_sem=recv_sems.at[k], device_id=sibling, device_id_type=MESH)

        sends = []
        for j, (px, py) in enumerate(chips):
            for a in range(nbig + nsmall):
                cp = ici(a, j, None, (px, py, c))
                cp.start()
                sends.append(cp)
        for j, (px, py) in enumerate(chips):
            origin = 2 * px + py
            for a in range(nbig):
                ici(a, j, origin, (px, py, c)).wait_recv()
                cp = fwd(a, j, origin, c)
                cp.start()
                sends.append(cp)
        for j, (px, py) in enumerate(chips):
            origin = 2 * px + py
            for a in range(nbig, nbig + nsmall):
                ici(a, j, origin, (px, py, c)).wait_recv()
            for a in range(nbig):
                fwd(a, j, origin, 1 - c).wait_recv()
        for cp in sends:
            cp.wait_send()

    out_shape = tuple(jax.ShapeDtypeStruct((N_CHIPS,) + a.shape, a.dtype) for a in list(big) + list(small))
    return pl.pallas_call(
        body, name=name, out_shape=out_shape,
        in_specs=[ANY] * (nbig + nsmall), out_specs=tuple([ANY] * (nbig + nsmall)),
        scratch_shapes=[pltpu.SemaphoreType.DMA((n_ici + n_fwd,)), pltpu.SemaphoreType.DMA((n_ici + n_fwd,))],
    )(*big, *small)


HBM = pl.BlockSpec(memory_space=pltpu.HBM)
SEM = pl.BlockSpec(memory_space=pltpu.SEMAPHORE)
DATAFLOW = pltpu.SideEffectType.DATAFLOW_SIDE_EFFECTING


def _split_peers(pattern, x, y, c):
    if pattern == "swap":
        return [((x, y, 1 - c), 1 - c, None, None)]
    me = 2 * x + y
    return [((px, py, c), 2 * px + py if pattern == "scatter" else None, me, 2 * px + py)
            for px, py in _other_chips(x, y)]


def _split_land_shape(pattern, shape):
    return {"bcast": (N_CHIPS,) + shape, "scatter": shape, "swap": shape[:1] + shape[2:]}[pattern]


def _split_copies(pattern, srcs, lands, send_sems, recv_sems, waiting):
    x, y, c = _mesh_pos()
    peers = _split_peers(pattern, x, y, c)
    cps = []
    for j, (dev, src_slot, dst_slot, my_slot) in enumerate(peers):
        for a in range(len(srcs)):
            if src_slot is None:
                src = srcs[a]
            else:
                src = srcs[a].at[:, src_slot] if pattern == "swap" else srcs[a].at[src_slot]
            slot = my_slot if waiting else dst_slot
            dst = lands[a] if slot is None else lands[a].at[slot]
            k = a * len(peers) + j
            cps.append(pltpu.make_async_remote_copy(src_ref=src, dst_ref=dst, send_sem=send_sems[k],
                                                    recv_sem=recv_sems[k], device_id=dev, device_id_type=MESH))
    return cps


def _split_start(arrs, pattern, after, name):
    n = len(arrs)
    nsem = n * (1 if pattern == "swap" else N_CHIPS - 1)
    deps = [] if after is None else [after]

    def body(*refs):
        srcs, lands = refs[:n], refs[n:2 * n]
        outs = refs[2 * n + len(deps):]
        for cp in _split_copies(pattern, srcs, lands, outs[:nsem], outs[nsem:2 * nsem], waiting=False):
            cp.start()
        outs[-1][...] = jnp.zeros_like(outs[-1])

    lands = [lax.empty(_split_land_shape(pattern, a.shape), a.dtype) for a in arrs]
    out_shape = ([pltpu.SemaphoreType.DMA(())] * (2 * nsem)
                 + [pltpu.HBM(a.shape, a.dtype) for a in arrs] + [pltpu.HBM(b.shape, b.dtype) for b in lands]
                 + [jax.ShapeDtypeStruct((SUBLANES, LANES), F32)])
    outs = pl.pallas_call(
        body, name=name, out_shape=tuple(out_shape),
        in_specs=[HBM] * (2 * n) + [ANY] * len(deps),
        out_specs=tuple([SEM] * (2 * nsem) + [HBM] * (2 * n) + [pl.BlockSpec(memory_space=pltpu.VMEM)]),
        input_output_aliases={a: 2 * nsem + a for a in range(2 * n)},
        compiler_params=pltpu.CompilerParams(has_side_effects=DATAFLOW),
    )(*[pltpu.with_memory_space_constraint(a, pltpu.HBM) for a in list(arrs) + lands], *deps)
    return outs[:-1], outs[-1]


def _split_wait(state, n, pattern, after, name):
    nsem = n * (1 if pattern == "swap" else N_CHIPS - 1)

    def body(*refs):
        srcs, lands = refs[:n], refs[n:2 * n]
        send_sems, recv_sems = refs[2 * n:2 * n + nsem], refs[2 * n + nsem:2 * n + 2 * nsem]
        for cp in _split_copies(pattern, srcs, lands, send_sems, recv_sems, waiting=True):
            cp.wait_send()
            cp.wait_recv()

    sems, thru = state[:2 * nsem], state[2 * nsem:]
    outs = pl.pallas_call(
        body, name=name, out_shape=tuple(pltpu.HBM(a.shape, a.dtype) for a in thru),
        in_specs=[HBM] * (2 * n) + [SEM] * (2 * nsem) + [ANY],
        out_specs=tuple([HBM] * (2 * n)),
        input_output_aliases={a: a for a in range(2 * n)},
        compiler_params=pltpu.CompilerParams(has_side_effects=DATAFLOW),
    )(*thru, *sems, after)
    return outs[:n], outs[n:]


def _pair_gather(arrs, layer, name):
    n = len(arrs)

    def body(*refs):
        outs = refs[n:2 * n]
        send_sems, recv_sems = refs[2 * n:]
        x, y, c = _mesh_pos()
        cps = [pltpu.make_async_remote_copy(src_ref=outs[a].at[layer, c], dst_ref=outs[a].at[layer, c],
                                            send_sem=send_sems.at[a], recv_sem=recv_sems.at[a],
                                            device_id=(x, y, 1 - c), device_id_type=MESH)
               for a in range(n)]
        for cp in cps:
            cp.start()
        for cp in cps:
            cp.wait()

    return pl.pallas_call(
        body, name=name, out_shape=tuple(jax.ShapeDtypeStruct(a.shape, a.dtype) for a in arrs),
        in_specs=[ANY] * n, out_specs=tuple([ANY] * n),
        input_output_aliases={a: a for a in range(n)},
        scratch_shapes=[pltpu.SemaphoreType.DMA((n,)), pltpu.SemaphoreType.DMA((n,))],
    )(*arrs)


N_DEV = 8


def _allreduce_small(pack, name):
    r = pack.shape[0]

    def body(p_ref, o_ref, land, send_sems, recv_sems):
        x, y, c = _mesh_pos()
        me = 4 * x + 2 * y + c
        cps = []
        for k in range(1, N_DEV):
            peer = (x ^ (k >> 2), y ^ ((k >> 1) & 1), c ^ (k & 1))
            cps.append(pltpu.make_async_remote_copy(src_ref=p_ref, dst_ref=land.at[me], send_sem=send_sems.at[k - 1],
                                                    recv_sem=recv_sems.at[k - 1], device_id=peer, device_id_type=MESH))
        for cp in cps:
            cp.start()
        land[me] = p_ref[...]
        for cp in cps:
            cp.wait()
        total = land[0]
        for d in range(1, N_DEV):
            total = total + land[d]
        o_ref[...] = total

    vm = pl.BlockSpec(memory_space=pltpu.VMEM)
    return pl.pallas_call(
        body, name=name, out_shape=jax.ShapeDtypeStruct(pack.shape, F32),
        in_specs=[vm], out_specs=vm,
        scratch_shapes=[pltpu.VMEM((N_DEV, r, LANES), F32), pltpu.SemaphoreType.DMA((N_DEV - 1,)),
                        pltpu.SemaphoreType.DMA((N_DEV - 1,))],
    )(pack)


BIG_ROWS = 256


def _cast_layer(w, layer, name):
    _, r, cdim = w.shape
    tr = BIG_ROWS

    def body(w_ref, o_ref):
        o_ref[...] = w_ref[...].astype(o_ref.dtype)

    return pl.pallas_call(
        body, name=name, out_shape=jax.ShapeDtypeStruct((r, cdim), MXU_DTYPE),
        grid=(r // tr,), in_specs=[pl.BlockSpec((None, tr, cdim), lambda i: (layer, i, 0))],
        out_specs=pl.BlockSpec((tr, cdim), lambda i: (i, 0)),
        compiler_params=_params(("parallel",)),
    )(w)


def _cast_cols_major(w_t, name):
    cdim, nl, r = w_t.shape
    tc = LANES

    def body(w_ref, *o_refs):
        for l in range(nl):
            o_refs[l][...] = w_ref[:, l, :].T.astype(o_refs[l].dtype)

    out = pl.BlockSpec((r, tc), lambda i: (0, i))
    return pl.pallas_call(
        body, name=name, out_shape=tuple(jax.ShapeDtypeStruct((r, cdim), MXU_DTYPE) for _ in range(nl)),
        grid=(pl.cdiv(cdim, tc),), in_specs=[pl.BlockSpec((tc, nl, r), lambda i: (i, 0, 0))],
        out_specs=tuple([out] * nl),
        compiler_params=_params(("parallel",)),
    )(w_t)


def _pair_sum(parts, sib, which, out_dtype, name):
    k, _, r, cdim = parts.shape
    tr = BIG_ROWS

    def body(sel_ref, p_ref, s_ref, o_ref):
        o_ref[...] = (p_ref[...] + s_ref[...]).astype(o_ref.dtype)

    grid_spec = pltpu.PrefetchScalarGridSpec(
        num_scalar_prefetch=1, grid=(k, r // tr),
        in_specs=[pl.BlockSpec((None, None, tr, cdim), lambda l, i, sel: (l, sel[0], i, 0)),
                  pl.BlockSpec((None, tr, cdim), lambda l, i, sel: (l, i, 0))],
        out_specs=pl.BlockSpec((None, tr, cdim), lambda l, i, sel: (l, i, 0)))
    return pl.pallas_call(
        body, name=name, out_shape=jax.ShapeDtypeStruct((k, r, cdim), out_dtype), grid_spec=grid_spec,
        compiler_params=_params(("parallel", "parallel")),
    )(which.reshape(1).astype(jnp.int32), parts, sib)


def _sum_lead(parts, into, layer, which, name):
    k, r, cdim = parts.shape
    tr = BIG_ROWS

    def body(sel_ref, p_ref, _, o_ref):
        total = p_ref[0].astype(F32)
        for a in range(1, k):
            total = total + p_ref[a].astype(F32)
        o_ref[...] = total

    grid_spec = pltpu.PrefetchScalarGridSpec(
        num_scalar_prefetch=1, grid=(r // tr,),
        in_specs=[pl.BlockSpec((k, tr, cdim), lambda i, sel: (0, i, 0)), ANY],
        out_specs=pl.BlockSpec((None, None, tr, cdim), lambda i, sel: (layer, sel[0], i, 0)))
    return pl.pallas_call(
        body, name=name, out_shape=jax.ShapeDtypeStruct(into.shape, F32), grid_spec=grid_spec,
        input_output_aliases={2: 0},
        compiler_params=_params(("parallel",)),
    )(which.reshape(1).astype(jnp.int32), parts, into)


def _adam_math(w, g, m, v):
    m2 = ADAM_B1 * m + (1.0 - ADAM_B1) * g
    v2 = ADAM_B2 * v + (1.0 - ADAM_B2) * (g * g)
    m_hat = m2 / (1.0 - ADAM_B1 ** ADAM_STEP)
    v_hat = v2 / (1.0 - ADAM_B2 ** ADAM_STEP)
    delta = -ADAM_LR * (m_hat / (jnp.sqrt(v_hat) + ADAM_EPS) + ADAM_WD * w)
    return delta, m2, v2


def _adam_big(w, g, m, v, name):
    nl, r, cdim = w.shape
    tr = BIG_ROWS

    def body(w_ref, g_ref, m_ref, v_ref, d_ref, mo_ref, vo_ref):
        delta, m2, v2 = _adam_math(w_ref[...], g_ref[...], m_ref[...], v_ref[...])
        d_ref[...] = delta
        mo_ref[...] = m2
        vo_ref[...] = v2

    blk = pl.BlockSpec((None, tr, cdim), lambda l, i: (l, i, 0))
    shp = jax.ShapeDtypeStruct(w.shape, F32)
    return pl.pallas_call(
        body, name=name, out_shape=(shp, shp, shp),
        grid=(nl, r // tr), in_specs=[blk] * 4, out_specs=(blk, blk, blk),
        compiler_params=_params(("parallel", "parallel")),
    )(w, g, m, v)


def _adam_cols_major(w, g, m, v, name):
    cdim, nl, r = w.shape
    tc = BIG_ROWS

    def body(w_ref, g_ref, m_ref, v_ref, d_ref, mo_ref, vo_ref):
        delta, m2, v2 = _adam_math(w_ref[...], g_ref[...], m_ref[...], v_ref[...])
        d_ref[...] = delta
        mo_ref[...] = m2
        vo_ref[...] = v2

    blk = pl.BlockSpec((tc, nl, r), lambda i: (i, 0, 0))
    shp = jax.ShapeDtypeStruct(w.shape, F32)
    return pl.pallas_call(
        body, name=name, out_shape=(shp, shp, shp),
        grid=(pl.cdiv(cdim, tc),), in_specs=[blk] * 4, out_specs=(blk, blk, blk),
        compiler_params=_params(("parallel",)),
    )(w, g, m, v)


def _adam_small(ws, gs, ms, vs, name):
    n = len(ws)

    def body(*refs):
        w_refs, g_refs, m_refs, v_refs = (refs[k * n:(k + 1) * n] for k in range(4))
        d_refs, mo_refs, vo_refs = (refs[(4 + k) * n:(5 + k) * n] for k in range(3))
        for a in range(n):
            delta, m2, v2 = _adam_math(w_refs[a][...], g_refs[a][...], m_refs[a][...], v_refs[a][...])
            d_refs[a][...] = delta
            mo_refs[a][...] = m2
            vo_refs[a][...] = v2

    shapes = tuple(jax.ShapeDtypeStruct(w.shape, F32) for w in ws)
    vm = pl.BlockSpec(memory_space=pltpu.VMEM)
    outs = pl.pallas_call(body, name=name, out_shape=shapes * 3, in_specs=[vm] * (4 * n),
                          out_specs=tuple([vm] * (3 * n)))(*ws, *gs, *ms, *vs)
    return outs[:n], outs[n:2 * n], outs[2 * n:]


PACK_TILE = SUBLANES * LANES


def _pack(arrays):
    rows = []
    for a in arrays:
        flat = a.reshape(-1)
        pad = (-flat.shape[0]) % PACK_TILE
        if pad:
            flat = jnp.concatenate([flat, jnp.zeros((pad,), flat.dtype)])
        rows.append(flat.reshape(-1, LANES))
    return jnp.concatenate(rows, axis=0)


def _unpack(pack, shapes):
    outs, row = [], 0
    for shp in shapes:
        n = int(np.prod(shp))
        nrows = -(-n // PACK_TILE) * SUBLANES
        outs.append(pack[row:row + nrows].reshape(-1)[:n].reshape(shp))
        row += nrows
    return outs


SMALL = ["norm_w", "ssd_conv_b", "ssd_dt_bias", "ssd_a_log", "ssd_d", "ssd_norm_w", "attn_sinks",
         "conf_dw_b", "conf_ln_w", "conf_ln_b"]
WEIGHTS = ["norm_w", "w_in", "ssd_conv_w", "ssd_conv_b", "ssd_dt_bias", "ssd_a_log", "ssd_d", "ssd_norm_w",
           "attn_sinks", "conf_dw_w", "conf_dw_b", "conf_ln_w", "conf_ln_b", "w_out", "final_norm_w"]


def kernel(x, norm_w, w_in, ssd_conv_w, ssd_conv_b, ssd_dt_bias, ssd_a_log, ssd_d, ssd_norm_w, attn_sinks, conf_dw_w, conf_dw_b, conf_ln_w, conf_ln_b, w_out, final_norm_w, loss_target, m_norm_w, m_w_in, m_ssd_conv_w, m_ssd_conv_b, m_ssd_dt_bias, m_ssd_a_log, m_ssd_d, m_ssd_norm_w, m_attn_sinks, m_conf_dw_w, m_conf_dw_b, m_conf_ln_w, m_conf_ln_b, m_w_out, m_final_norm_w, v_norm_w, v_w_in, v_ssd_conv_w, v_ssd_conv_b, v_ssd_dt_bias, v_ssd_a_log, v_ssd_d, v_ssd_norm_w, v_attn_sinks, v_conf_dw_w, v_conf_dw_b, v_conf_ln_w, v_conf_ln_b, v_w_out, v_final_norm_w):
    w = dict(norm_w=norm_w, w_in=w_in, ssd_conv_w=ssd_conv_w, ssd_conv_b=ssd_conv_b, ssd_dt_bias=ssd_dt_bias,
             ssd_a_log=ssd_a_log, ssd_d=ssd_d, ssd_norm_w=ssd_norm_w, attn_sinks=attn_sinks, conf_dw_w=conf_dw_w,
             conf_dw_b=conf_dw_b, conf_ln_w=conf_ln_w, conf_ln_b=conf_ln_b, w_out=w_out, final_norm_w=final_norm_w)
    m = dict(norm_w=m_norm_w, w_in=m_w_in, ssd_conv_w=m_ssd_conv_w, ssd_conv_b=m_ssd_conv_b,
             ssd_dt_bias=m_ssd_dt_bias, ssd_a_log=m_ssd_a_log, ssd_d=m_ssd_d, ssd_norm_w=m_ssd_norm_w,
             attn_sinks=m_attn_sinks, conf_dw_w=m_conf_dw_w, conf_dw_b=m_conf_dw_b, conf_ln_w=m_conf_ln_w,
             conf_ln_b=m_conf_ln_b, w_out=m_w_out, final_norm_w=m_final_norm_w)
    v = dict(norm_w=v_norm_w, w_in=v_w_in, ssd_conv_w=v_ssd_conv_w, ssd_conv_b=v_ssd_conv_b,
             ssd_dt_bias=v_ssd_dt_bias, ssd_a_log=v_ssd_a_log, ssd_d=v_ssd_d, ssd_norm_w=v_ssd_norm_w,
             attn_sinks=v_attn_sinks, conf_dw_w=v_conf_dw_w, conf_dw_b=v_conf_dw_b, conf_ln_w=v_conf_ln_w,
             conf_ln_b=v_conf_ln_b, w_out=v_w_out, final_norm_w=v_final_norm_w)
    depth = w_in.shape[0]
    me = 2 * lax.axis_index("x") + lax.axis_index("y")

    assert depth == 2
    w_in_t = jnp.transpose(w_in, (2, 0, 1))
    w_in_b = _cast_cols_major(w_in_t, name="cast_w_in")
    w_out_b = [_cast_layer(w_out, li, name=f"cast_w_out_l{li}") for li in range(depth)]
    own0 = [w_in_b[0].reshape((2, -1) + w_in_b[0].shape[1:]), ssd_conv_w, conf_dw_w]
    gathered0 = _gather_weights(own0[:1], own0[1:], name="gather_weights_l0")
    g_in0, g_conv, g_dw = [lax.dynamic_update_index_in_dim(g_all, mine, me, 0)
                           for g_all, mine in zip(gathered0, own0)]
    own1 = [w_out_b[0], w_in_b[1], w_out_b[1]]
    pending1, token1 = _split_start(own1, "bcast", gathered0[0], name="gather_rest_start")
    rest = {}

    def small_full(li):
        return (jnp.concatenate([g_conv[p, li] for p in range(N_CHIPS)], axis=1),
                jnp.concatenate([g_dw[p, li] for p in range(N_CHIPS)], axis=1))

    def w_out_l0(after):
        mine1, landed = _split_wait(pending1, len(own1), "bcast", after, name="gather_rest_wait")
        rest["landed"] = [lax.dynamic_update_index_in_dim(g_all, mine, me, 0) for g_all, mine in zip(landed, mine1)]
        return rest["landed"][0].reshape(-1, w_out.shape[2])

    def params_l0(_):
        w_in_p = _padded_from_chips([g_in0[p].reshape(w_in_b[0].shape) for p in range(N_CHIPS)])
        return _layer_params(0, w_in_p, w_out_l0, *small_full(0), w)

    def params_l1(_):
        _, g_in1, g_out1 = rest["landed"]
        w_in_p = _padded_from_chips([g_in1[p] for p in range(N_CHIPS)])
        return _layer_params(1, w_in_p, g_out1.reshape(-1, g_out1.shape[-1]), *small_full(1), w)

    c = lax.axis_index("c")
    cols = w_in.shape[2]
    rows_out = w_out.shape[1]

    def grad_parts(g):
        dw = g["w_in_p"]
        return [dw.reshape(1, 2, dw.shape[0] // 2, dw.shape[1]),
                g["w_out"].reshape(N_CHIPS, 2, rows_out // 2, D_MODEL)]

    def pair_sums(parts, sib, tag):
        s_in, s_out = [_pair_sum(p, sb, c, MXU_DTYPE, name=f"grad_pair_sum_{k}_{tag}")
                       for k, (p, sb) in enumerate(zip(parts, sib))]
        return [jnp.stack([_chip_part_from_padded(s_in[0], p, cols) for p in range(N_CHIPS)]), s_out]

    split = {"reduced": [lax.empty((depth, 2, w_in.shape[1] // 2, cols), F32),
                         lax.empty((depth, 2, rows_out // 2, D_MODEL), F32)]}

    def chip_sums(landed, sent, li, which=(0, 1)):
        filled = [lax.dynamic_update_index_in_dim(r, lax.dynamic_index_in_dim(sk, me, 0, keepdims=False), me, 0)
                  for r, sk in zip(landed, sent)]
        tag = "".join(str(k) for k in which)
        halves = [_sum_lead(r, split["reduced"][k], li, c, name=f"grad_chip_sum_{k}_l{li}")
                  for k, r in zip(which, filled)]
        for k, buf in zip(which, _pair_gather(halves, li, name=f"grad_pair_gather_{tag}_l{li}")):
            split["reduced"][k] = buf

    def bwd_hooks(li):
        def after_dw(dw_in_p, dw_out):
            parts = grad_parts(dict(w_in_p=dw_in_p, w_out=dw_out))
            state, token = _split_start(parts, "swap", None, name=f"grad_swap_l{li}_start")
            if li > 0:
                split[f"swap{li}"] = (parts, state)
                return token
            mine, sib = _split_wait(state, len(parts), "swap", token, name="grad_swap_l0_wait")
            split["scatter0"], token = _split_start(pair_sums(mine, sib, "l0"), "scatter", None,
                                                    name="grad_scatter_l0_start")
            return token

        hooks = {"after_dw": after_dw}
        if li == depth - 2:
            parts, swap_state = split[f"swap{depth - 1}"]

            def after_dycat(dycat):
                mine, sib = _split_wait(swap_state, len(parts), "swap", dycat, name="grad_swap_l1_wait")
                sent = pair_sums(mine, sib, "l1")
                split["scatter"], token = _split_start(sent, "scatter", None, name="grad_scatter_l1_start")
                return token

            def after_attn(dproj):
                sent, landed = _split_wait(split["scatter"], len(parts), "scatter", dproj,
                                           name="grad_scatter_l1_wait")
                chip_sums(landed, sent, depth - 1)

            hooks.update(after_dycat=after_dycat, after_attn=after_attn)
        return hooks

    loss, grad_x, grads, dfinal = _local_step(x, loss_target, [params_l0, params_l1], final_norm_w,
                                              first_after=token1, bwd_hooks=bwd_hooks)

    small_list = [grads[li][n] for li in range(depth) for n in SMALL]
    small_list += [grads[li][n] for li in range(depth) for n in ("ssd_conv_w", "conf_dw_w")]
    small_list += [dfinal, loss.reshape(1)]
    small_shapes = [a.shape for a in small_list]
    reduced = _unpack(_allreduce_small(_pack(small_list), name="allreduce_small"), small_shapes)
    ns = len(SMALL)
    g = {n: jnp.stack([reduced[li * ns + i] for li in range(depth)]) for i, n in enumerate(SMALL)}
    conv_w_cols, dw_w_cols = ssd_conv_w.shape[2], conf_dw_w.shape[2]
    g["ssd_conv_w"] = jnp.stack([lax.dynamic_slice_in_dim(reduced[depth * ns + 2 * li], me * conv_w_cols,
                                                          conv_w_cols, axis=1) for li in range(depth)])
    g["conf_dw_w"] = jnp.stack([lax.dynamic_slice_in_dim(reduced[depth * ns + 2 * li + 1], me * dw_w_cols,
                                                         dw_w_cols, axis=1) for li in range(depth)])
    g["final_norm_w"] = reduced[-2]
    loss_total = reduced[-1][0]

    small_names = [n for n in WEIGHTS if n not in ("w_in", "w_out")]

    def as2d(a):
        return a.reshape(1, -1) if a.ndim == 1 else a

    deltas, new_ms, new_vs = _adam_small(*[[as2d(src[n]) for n in small_names] for src in (w, g, m, v)],
                                         name="adam_small")

    sent0, landed0 = _split_wait(split["scatter0"], 2, "scatter", deltas[0], name="grad_scatter_l0_wait")
    chip_sums(landed0, sent0, 0)
    g_w_in = split["reduced"][0].reshape(w_in.shape)
    g_w_out = split["reduced"][1].reshape(w_out.shape)
    outs_g, outs_d, outs_m, outs_v = {"w_in": g_w_in, "w_out": g_w_out}, {}, {}, {}
    outs_d["w_out"], outs_m["w_out"], outs_v["w_out"] = _adam_big(w_out, g_w_out, m_w_out, v_w_out,
                                                                  name="adam_w_out")
    to_cols, from_cols = (2, 0, 1), (1, 2, 0)
    outs_d["w_in"], outs_m["w_in"], outs_v["w_in"] = [
        jnp.transpose(a, from_cols) for a in _adam_cols_major(
            *[jnp.transpose(a, to_cols) for a in (w_in, g_w_in, m_w_in, v_w_in)], name="adam_w_in")]
    for n, dn, mn, vn in zip(small_names, deltas, new_ms, new_vs):
        outs_g[n], outs_d[n], outs_m[n], outs_v[n] = (g[n], dn.reshape(w[n].shape), mn.reshape(w[n].shape),
                                                      vn.reshape(w[n].shape))
    return (loss_total, grad_x, *[outs_g[n] for n in WEIGHTS], *[outs_d[n] for n in WEIGHTS],
            *[outs_m[n] for n in WEIGHTS], *[outs_v[n] for n in WEIGHTS])
```

```python
import functools
import math

import jax
import jax.numpy as jnp
import numpy as np
from jax import lax
from jax.experimental import pallas as pl
from jax.experimental.pallas import tpu as pltpu

F32 = jnp.float32
BF16 = jnp.bfloat16
MXU_DTYPE = BF16

D_MODEL = 1024
DEPTH = 2
SSD_HEADS = 16
SSD_HEAD_DIM = 64
SSD_STATE = 128
SSD_CONV = 4
CHUNK = 128
SSD_CONV_DIM = 1536
ATTN_HEAD_DIM = 64
ATTN_Q_HEADS = 8
WINDOW = 128
CONF_WIDTH = 512
CONF_KERNEL = 31
MIX_WIDTH = 2048
D_IN_PROJ = 5392
EPS = 1e-5

ADAM_LR = 0.001
ADAM_B1 = 0.9
ADAM_B2 = 0.999
ADAM_EPS = 1e-08
ADAM_WD = 0.01
ADAM_STEP = 10

LANES = 128
SUBLANES = 8
VMEM_LIMIT = 48 * 1024 * 1024

NP = 5632
OFF_ZA, OFF_Q, OFF_K, OFF_V, OFF_DT = 0, 512, 1024, 1152, 1280
ATTN_GROUP = 1536
OFF_CONF, OFF_ZC = 1536, 2560
CONF_GROUP = 1536
OFF_ZS = 3072
OFF_XBC = 4096
SECTIONS = ((0, 1024, OFF_ZS), (1024, 1536, OFF_ZA), (1536, 2048, OFF_ZC), (2048, 3584, OFF_XBC),
            (3584, 3600, OFF_DT), (3600, 4368, OFF_Q), (4368, 5392, OFF_CONF))

YCAT_ATTN, YCAT_CONF = 1024, 1536
ANY = pl.BlockSpec(memory_space=pl.ANY)

NN = (((1,), (0,)), ((), ()))
NT = (((1,), (1,)), ((), ()))
TN = (((0,), (0,)), ((), ()))


def _params(sem):
    return pltpu.CompilerParams(dimension_semantics=sem, vmem_limit_bytes=VMEM_LIMIT)


def _dot(a, b, dims=NN):
    return lax.dot_general(a.astype(MXU_DTYPE), b.astype(MXU_DTYPE), dims, preferred_element_type=F32)


def _split_bf16(a, passes):
    pieces = []
    r = a
    for _ in range(passes):
        p = r.astype(BF16)
        pieces.append(p)
        r = r - p.astype(F32)
    return pieces


def _xdot(a, sel, dims=NN, passes=2):
    out = None
    for p in _split_bf16(a, passes):
        t = lax.dot_general(p, sel, dims, preferred_element_type=F32)
        out = t if out is None else out + t
    return out


def _xdot_r(sel, b, dims=NN, passes=3):
    out = None
    for p in _split_bf16(b, passes):
        t = lax.dot_general(sel, p, dims, preferred_element_type=F32)
        out = t if out is None else out + t
    return out


def _sigmoid(x):
    return 1.0 / (1.0 + jnp.exp(-x))


def _silu(x):
    return x * _sigmoid(x)


def _dsilu(x):
    s = _sigmoid(x)
    return s * (1.0 + x * (1.0 - s))


def _softplus(x):
    return jnp.maximum(x, 0.0) + jnp.log(1.0 + jnp.exp(-jnp.abs(x)))


def _rowsum8(x):
    r, c = x.shape
    return jnp.sum(x.reshape(r // SUBLANES, SUBLANES, c), axis=0)


def _iota(shape, dim):
    return lax.broadcasted_iota(jnp.int32, shape, dim)


def _matmul(a, b, form, out_dtype, tm, tn, tk, name, residual=None, after=None):
    if form == "nn":
        (m, k), n = a.shape, b.shape[1]
    elif form == "nt":
        (m, k), n = a.shape, b.shape[0]
    else:
        (k, m), n = a.shape, b.shape[1]
    tm, tn, tk = min(tm, m), min(tn, n), min(tk, k)
    assert m % tm == 0 and n % tn == 0 and k % tk == 0, (name, m, n, k, tm, tn, tk)
    if form == "nn":
        a_spec = pl.BlockSpec((tm, tk), lambda i, j, s: (i, s))
        b_spec = pl.BlockSpec((tk, tn), lambda i, j, s: (s, j))
        dims = NN
    elif form == "nt":
        (m, k), n = a.shape, b.shape[0]
        a_spec = pl.BlockSpec((tm, tk), lambda i, j, s: (i, s))
        b_spec = pl.BlockSpec((tn, tk), lambda i, j, s: (j, s))
        dims = NT
    else:
        (k, m), n = a.shape, b.shape[1]
        a_spec = pl.BlockSpec((tk, tm), lambda i, j, s: (s, i))
        b_spec = pl.BlockSpec((tk, tn), lambda i, j, s: (s, j))
        dims = TN
    nk = k // tk
    has_res = residual is not None
    deps = [] if after is None else [after]

    def body_single(a_ref, b_ref, *rest):
        o = _dot(a_ref[...], b_ref[...], dims)
        if has_res:
            o = o + rest[0][...]
        rest[-1][...] = o.astype(out_dtype)

    def body(a_ref, b_ref, *rest):
        r_ref = rest[0] if has_res else None
        o_ref, acc = rest[-2:]
        s = pl.program_id(2)

        @pl.when(s == 0)
        def _():
            acc[...] = jnp.zeros_like(acc)

        acc[...] += _dot(a_ref[...], b_ref[...], dims)

        @pl.when(s == nk - 1)
        def _():
            o = acc[...]
            if has_res:
                o = o + r_ref[...]
            o_ref[...] = o.astype(out_dtype)

    in_specs = [a_spec, b_spec]
    args = [a, b]
    if has_res:
        in_specs.append(pl.BlockSpec((tm, tn), lambda i, j, s: (i, j)))
        args.append(residual)
    in_specs += [ANY] * len(deps)
    args += deps
    return pl.pallas_call(
        body_single if nk == 1 else body, name=name,
        out_shape=jax.ShapeDtypeStruct((m, n), out_dtype),
        grid=(m // tm, n // tn, nk),
        in_specs=in_specs,
        out_specs=pl.BlockSpec((tm, tn), lambda i, j, s: (i, j)),
        scratch_shapes=[] if nk == 1 else [pltpu.VMEM((tm, tn), F32)],
        compiler_params=_params(("parallel", "parallel", "arbitrary")),
    )(*args)


ROW_TILE = 256


PROJ_FWD_TM, PROJ_FWD_TN = 2048, 512


def _proj_fwd(x, w, w_in_p, name, after=None):
    t, d = x.shape
    n = w_in_p.shape[1]
    tm, tn = min(PROJ_FWD_TM, t), PROJ_FWD_TN
    assert t % tm == 0 and n % tn == 0
    deps = [] if after is None else [after]

    def body(x_ref, w_ref, b_ref, *rest):
        o_ref, ot_ref, h_scr = rest[len(deps):]

        @pl.when(pl.program_id(1) == 0)
        def _():
            xv = x_ref[...]
            rstd = lax.rsqrt(jnp.mean(xv * xv, axis=-1, keepdims=True) + EPS)
            h = xv * rstd * w_ref[...]
            h_scr[...] = h.astype(h_scr.dtype)
            ot_ref[...] = h.T.astype(ot_ref.dtype)

        o_ref[...] = _dot(h_scr[...], b_ref[...])

    return pl.pallas_call(
        body, name=name,
        out_shape=(jax.ShapeDtypeStruct((t, n), F32), jax.ShapeDtypeStruct((d, t), MXU_DTYPE)),
        grid=(t // tm, n // tn),
        in_specs=[pl.BlockSpec((tm, d), lambda i, j: (i, 0)), pl.BlockSpec((1, d), lambda i, j: (0, 0)),
                  pl.BlockSpec((d, tn), lambda i, j: (0, j))] + [ANY] * len(deps),
        out_specs=(pl.BlockSpec((tm, tn), lambda i, j: (i, j)), pl.BlockSpec((d, tm), lambda i, j: (0, i))),
        scratch_shapes=[pltpu.VMEM((tm, d), MXU_DTYPE)],
        compiler_params=_params(("parallel", "arbitrary")),
    )(x, w, w_in_p, *deps)


PROJ_BWD_TM, PROJ_BWD_TK = 1024, 1408


def _proj_bwd_dx(dproj, w_in_p, x, w, dres, name):
    t, d = x.shape
    kdim = dproj.shape[1]
    tm, tk = min(PROJ_BWD_TM, t), PROJ_BWD_TK
    nt, nk = t // tm, kdim // tk
    assert t % tm == 0 and kdim % tk == 0

    def body(a_ref, b_ref, x_ref, w_ref, dr_ref, dx_ref, dw_ref, acc, wacc):
        i, s = pl.program_id(0), pl.program_id(1)

        @pl.when((i == 0) & (s == 0))
        def _():
            wacc[...] = jnp.zeros_like(wacc)

        @pl.when(s == 0)
        def _():
            acc[...] = jnp.zeros_like(acc)

        acc[...] += _dot(a_ref[...], b_ref[...], NT)

        @pl.when(s == nk - 1)
        def _():
            xv = x_ref[...]
            rstd = lax.rsqrt(jnp.mean(xv * xv, axis=-1, keepdims=True) + EPS)
            xh = xv * rstd
            dhv = acc[...]
            g = dhv * w_ref[...]
            dx_ref[...] = dr_ref[...] + rstd * (g - xh * jnp.mean(g * xh, axis=-1, keepdims=True))
            wacc[...] += _rowsum8(dhv * xh)

        @pl.when((i == nt - 1) & (s == nk - 1))
        def _():
            dw_ref[...] = jnp.sum(wacc[...], axis=0, keepdims=True)

    row = pl.BlockSpec((tm, d), lambda i, s: (i, 0))
    vec = pl.BlockSpec((1, d), lambda i, s: (0, 0))
    return pl.pallas_call(
        body, name=name,
        out_shape=(jax.ShapeDtypeStruct((t, d), F32), jax.ShapeDtypeStruct((1, d), F32)),
        grid=(nt, nk),
        in_specs=[pl.BlockSpec((tm, tk), lambda i, s: (i, s)), pl.BlockSpec((d, tk), lambda i, s: (0, s)),
                  row, vec, row],
        out_specs=(row, vec),
        scratch_shapes=[pltpu.VMEM((tm, d), F32), pltpu.VMEM((SUBLANES, d), F32)],
        compiler_params=_params(("arbitrary", "arbitrary")),
    )(dproj, w_in_p, x, w, dres)


def _loss_head(xf, target, w, name):
    t, d = xf.shape
    tm = ROW_TILE
    nt = t // tm

    def body(x_ref, t_ref, w_ref, loss_ref, dx_ref, dw_ref, lacc, wacc):
        i = pl.program_id(0)

        @pl.when(i == 0)
        def _():
            lacc[...] = jnp.zeros_like(lacc)
            wacc[...] = jnp.zeros_like(wacc)

        xv = x_ref[...]
        rstd = lax.rsqrt(jnp.mean(xv * xv, axis=-1, keepdims=True) + EPS)
        xh = xv * rstd
        err = xh * w_ref[...] - t_ref[...]
        lacc[...] += jnp.sum(err * err)
        dy = err * (1.0 / d)
        g = dy * w_ref[...]
        dx_ref[...] = rstd * (g - xh * jnp.mean(g * xh, axis=-1, keepdims=True))
        wacc[...] += _rowsum8(dy * xh)

        @pl.when(i == nt - 1)
        def _():
            loss_ref[...] = lacc[...] * (0.5 / d)
            dw_ref[...] = jnp.sum(wacc[...], axis=0, keepdims=True)

    row = pl.BlockSpec((tm, d), lambda i: (i, 0))
    vec = pl.BlockSpec((1, d), lambda i: (0, 0))
    return pl.pallas_call(
        body, name=name,
        out_shape=(jax.ShapeDtypeStruct((SUBLANES, LANES), F32), jax.ShapeDtypeStruct((t, d), F32),
                   jax.ShapeDtypeStruct((1, d), F32)),
        grid=(nt,),
        in_specs=[row, row, vec],
        out_specs=(pl.BlockSpec((SUBLANES, LANES), lambda i: (0, 0)), row, vec),
        scratch_shapes=[pltpu.VMEM((SUBLANES, LANES), F32), pltpu.VMEM((SUBLANES, d), F32)],
        compiler_params=_params(("arbitrary",)),
    )(xf, target, w)


CONV_TILE = 512
CONV_TILE_SHORT = 1024
CONV_COLS = 512
CONV_SUB_ROWS = 128
CONV_SUB_COLS = LANES


def _conv_halo(k):
    return SUBLANES if k - 1 <= SUBLANES else 32


def _conv_tile(k, t):
    return min(CONV_TILE if _conv_use_shifted(k) else CONV_TILE_SHORT, t)


def _conv_subtiles(tm, cw):
    return [(r0, c0) for r0 in range(0, tm, CONV_SUB_ROWS) for c0 in range(0, cw, CONV_SUB_COLS)]


def _conv_use_shifted(k):
    return k > SUBLANES


def _conv_shift_scratch(k, rows, cw):
    return [pltpu.VMEM((SUBLANES - 1, rows - SUBLANES, cw), F32)] if _conv_use_shifted(k) else []


def _conv_fill_shifted(ext, sh):
    n = sh.shape[1]
    for b in range(1, SUBLANES):
        sh[b - 1] = ext[b:b + n, :]


def _conv_rows(ext, sh, start, rows, cs):
    b = start % SUBLANES
    if b == 0 or not sh:
        return ext[start:start + rows, cs]
    return sh[0][b - 1, start - b:start - b + rows, cs]


def _conv_fwd(src, col0, width, w, bias, k, seq, name):
    t = src.shape[0]
    tm, cw, halo = _conv_tile(k, src.shape[0]), CONV_COLS, _conv_halo(k)
    sr, sc = CONV_SUB_ROWS, CONV_SUB_COLS
    p = k - 1
    cb0 = col0 // cw
    kp = w.shape[0]

    shifted = _conv_use_shifted(k)

    def body(x_ref, h_ref, w_ref, b_ref, o_ref, ext, *sh):
        i = pl.program_id(0)
        seq_start = (i * tm) % seq == 0
        ext[halo:, :] = x_ref[...]
        ext[:halo, :] = jnp.where(seq_start, 0.0, h_ref[...])
        if shifted:
            _conv_fill_shifted(ext, sh[0])
        for r0, c0 in _conv_subtiles(tm, cw):
            cs = slice(c0, c0 + sc)
            acc = jnp.zeros((sr, sc), F32) + b_ref[:, cs]
            for j in range(k):
                acc = acc + w_ref[j:j + 1, cs] * _conv_rows(ext, sh, r0 + halo - p + j, sr, cs)
            o_ref[r0:r0 + sr, cs] = acc

    return pl.pallas_call(
        body, name=name,
        out_shape=jax.ShapeDtypeStruct((t, width), F32),
        grid=(t // tm, width // cw),
        in_specs=[pl.BlockSpec((tm, cw), lambda i, j: (i, cb0 + j)),
                  pl.BlockSpec((halo, cw), lambda i, j: (jnp.maximum(i * (tm // halo) - 1, 0), cb0 + j)),
                  pl.BlockSpec((kp, cw), lambda i, j: (0, j)),
                  pl.BlockSpec((1, cw), lambda i, j: (0, j))],
        out_specs=pl.BlockSpec((tm, cw), lambda i, j: (i, j)),
        scratch_shapes=[pltpu.VMEM((halo + tm, cw), F32)] + _conv_shift_scratch(k, halo + tm, cw),
        compiler_params=_params(("parallel", "parallel")),
    )(src, src, w, bias)


def _conv_bwd(dy, src, col0, width, w, k, seq, name, into=None):
    t = src.shape[0]
    tm, cw, halo = _conv_tile(k, src.shape[0]), CONV_COLS, _conv_halo(k)
    sr, sc = CONV_SUB_ROWS, CONV_SUB_COLS
    p = k - 1
    cb0 = col0 // cw
    kp = w.shape[0]
    nt = t // tm
    last_halo = t // halo - 1

    shifted = _conv_use_shifted(k)

    def body(dy_ref, dn_ref, x_ref, xp_ref, w_ref, *rest):
        if into is not None:
            rest = rest[1:]
        dx_ref, dw_ref, db_ref, dyext, xext, wacc, bacc = rest[:7]
        sh = rest[7:]
        i = pl.program_id(1)
        dysh, xsh = (sh[:1], sh[1:]) if shifted else ((), ())

        @pl.when(i == 0)
        def _():
            wacc[...] = jnp.zeros_like(wacc)
            bacc[...] = jnp.zeros_like(bacc)

        seq_start = (i * tm) % seq == 0
        seq_end = ((i + 1) * tm) % seq == 0
        dyext[:tm, :] = dy_ref[...]
        dyext[tm:, :] = jnp.where(seq_end, 0.0, dn_ref[...])
        xext[halo:, :] = x_ref[...]
        xext[:halo, :] = jnp.where(seq_start, 0.0, xp_ref[...])
        if shifted:
            _conv_fill_shifted(dyext, dysh[0])
            _conv_fill_shifted(xext, xsh[0])
        for r0, c0 in _conv_subtiles(tm, cw):
            cs = slice(c0, c0 + sc)
            dyv = dy_ref[r0:r0 + sr, cs]
            acc = jnp.zeros((sr, sc), F32)
            for j in range(k):
                acc = acc + w_ref[j:j + 1, cs] * _conv_rows(dyext, dysh, r0 + p - j, sr, cs)
                wacc[j, :, cs] += _rowsum8(dyv * _conv_rows(xext, xsh, r0 + halo - p + j, sr, cs))
            dx_ref[r0:r0 + sr, cs] = acc.astype(dx_ref.dtype)
            bacc[:, cs] += _rowsum8(dyv)

        @pl.when(i == nt - 1)
        def _():
            dw_ref[...] = jnp.zeros_like(dw_ref)
            for j in range(k):
                dw_ref[j:j + 1, :] = jnp.sum(wacc[j], axis=0, keepdims=True)
            db_ref[...] = jnp.sum(bacc[...], axis=0, keepdims=True)

    if into is None:
        dx_shape = jax.ShapeDtypeStruct((t, width), F32)
        dx_spec = pl.BlockSpec((tm, cw), lambda j, i: (i, j))
        extra_specs, extra_args, aliases = [], [], {}
    else:
        dx_shape = jax.ShapeDtypeStruct(into.shape, into.dtype)
        dx_spec = pl.BlockSpec((tm, cw), lambda j, i: (i, cb0 + j))
        extra_specs, extra_args, aliases = [ANY], [into], {5: 0}
    return pl.pallas_call(
        body, name=name,
        out_shape=(dx_shape, jax.ShapeDtypeStruct((kp, width), F32), jax.ShapeDtypeStruct((1, width), F32)),
        grid=(width // cw, nt),
        in_specs=[pl.BlockSpec((tm, cw), lambda j, i: (i, j)),
                  pl.BlockSpec((halo, cw), lambda j, i: (jnp.minimum((i + 1) * (tm // halo), last_halo), j)),
                  pl.BlockSpec((tm, cw), lambda j, i: (i, cb0 + j)),
                  pl.BlockSpec((halo, cw), lambda j, i: (jnp.maximum(i * (tm // halo) - 1, 0), cb0 + j)),
                  pl.BlockSpec((kp, cw), lambda j, i: (0, j))] + extra_specs,
        out_specs=(dx_spec,
                   pl.BlockSpec((kp, cw), lambda j, i: (0, j)),
                   pl.BlockSpec((1, cw), lambda j, i: (0, j))),
        input_output_aliases=aliases,
        scratch_shapes=[pltpu.VMEM((tm + halo, cw), F32), pltpu.VMEM((halo + tm, cw), F32),
                        pltpu.VMEM((kp, SUBLANES, cw), F32), pltpu.VMEM((SUBLANES, cw), F32)]
        + 2 * _conv_shift_scratch(k, halo + tm, cw),
        compiler_params=_params(("parallel", "arbitrary")),
    )(dy, dy, src, src, w, *extra_args)


def _conf_specs(tm, cw, halo, order):
    cb = OFF_CONF // cw

    def blk(col):
        return pl.BlockSpec((tm, cw), lambda *g: (order(*g), col))

    def prev(col):
        return pl.BlockSpec((halo, cw), lambda *g: (jnp.maximum(order(*g) * (tm // halo) - 1, 0), col))

    return blk(cb), prev(cb), blk(cb + 1), prev(cb + 1)


def _glu_window(ext, a_ref, ah_ref, g_ref, gh_ref, seq_start, halo):
    ext[halo:, :] = a_ref[...] * _sigmoid(g_ref[...])
    ext[:halo, :] = jnp.where(seq_start, 0.0, ah_ref[...] * _sigmoid(gh_ref[...]))


def _conf_fwd(proj, w, bias, ln_w, ln_b, ycat, seq, name):
    t = proj.shape[0]
    k = CONF_KERNEL
    tm, cw, halo = CONV_TILE, CONF_WIDTH, _conv_halo(k)
    sr, sc = CONV_SUB_ROWS, CONV_SUB_COLS
    p = k - 1
    kp = w.shape[0]

    def body(a_ref, ah_ref, g_ref, gh_ref, z_ref, w_ref, b_ref, lw_ref, lb_ref, _, c1_ref, y_ref, ext, sh):
        i = pl.program_id(0)
        _glu_window(ext, a_ref, ah_ref, g_ref, gh_ref, (i * tm) % seq == 0, halo)
        _conv_fill_shifted(ext, sh)
        for r0, c0 in _conv_subtiles(tm, cw):
            cs = slice(c0, c0 + sc)
            acc = jnp.zeros((sr, sc), F32) + b_ref[:, cs]
            for j in range(k):
                acc = acc + w_ref[j:j + 1, cs] * _conv_rows(ext, (sh,), r0 + halo - p + j, sr, cs)
            c1_ref[r0:r0 + sr, cs] = acc
        for r0 in range(0, tm, sr):
            rows = slice(r0, r0 + sr)
            cv = c1_ref[rows, :]
            xc = cv - jnp.mean(cv, axis=-1, keepdims=True)
            rstd = lax.rsqrt(jnp.mean(xc * xc, axis=-1, keepdims=True) + EPS)
            c2 = xc * rstd * lw_ref[...] + lb_ref[...]
            y_ref[rows, :] = (_silu(c2) * _silu(z_ref[rows, :])).astype(y_ref.dtype)

    vec = pl.BlockSpec((1, cw), lambda i: (0, 0))
    row = pl.BlockSpec((tm, cw), lambda i: (i, 0))
    return pl.pallas_call(
        body, name=name,
        out_shape=(jax.ShapeDtypeStruct((t, cw), F32), jax.ShapeDtypeStruct(ycat.shape, ycat.dtype)),
        grid=(t // tm,),
        in_specs=[*_conf_specs(tm, cw, halo, lambda i: i),
                  pl.BlockSpec((tm, cw), lambda i: (i, OFF_ZC // cw)),
                  pl.BlockSpec((kp, cw), lambda i: (0, 0)), vec, vec, vec, ANY],
        out_specs=(row, pl.BlockSpec((tm, cw), lambda i: (i, YCAT_CONF // cw))),
        input_output_aliases={9: 1},
        scratch_shapes=[pltpu.VMEM((halo + tm, cw), F32)] + _conv_shift_scratch(k, halo + tm, cw),
        compiler_params=_params(("parallel",)),
    )(proj, proj, proj, proj, proj, w, bias, ln_w, ln_b, ycat)


def _conf_bwd(dycat, proj, c1, w, ln_w, ln_b, dproj, seq, name):
    t = proj.shape[0]
    k = CONF_KERNEL
    tm, cw, halo = CONV_TILE, CONF_WIDTH, _conv_halo(k)
    sr, sc = CONV_SUB_ROWS, CONV_SUB_COLS
    p = k - 1
    kp = w.shape[0]
    nt = t // tm
    last_halo = t // halo - 1

    def body(dy_ref, dyn_ref, c_ref, cn_ref, z_ref, zn_ref, a_ref, ah_ref, g_ref, gh_ref, w_ref, lw_ref, lb_ref, _,
             grp_ref, dw_ref, db_ref, dlw_ref, dlb_ref, dyext, xext, wacc, bacc, lwacc, lbacc, dysh, xsh):
        i = pl.program_id(0)

        @pl.when(i == 0)
        def _():
            wacc[...] = jnp.zeros_like(wacc)
            bacc[...] = jnp.zeros_like(bacc)
            lwacc[...] = jnp.zeros_like(lwacc)
            lbacc[...] = jnp.zeros_like(lbacc)

        def post_bwd(dy, cv, zv):
            xc = cv - jnp.mean(cv, axis=-1, keepdims=True)
            rstd = lax.rsqrt(jnp.mean(xc * xc, axis=-1, keepdims=True) + EPS)
            xh = xc * rstd
            c2 = xh * lw_ref[...] + lb_ref[...]
            dz = dy * _silu(c2) * _dsilu(zv)
            dc2 = dy * _silu(zv) * _dsilu(c2)
            dxh = dc2 * lw_ref[...]
            dc = rstd * (dxh - jnp.mean(dxh, axis=-1, keepdims=True)
                         - xh * jnp.mean(dxh * xh, axis=-1, keepdims=True))
            return dc, dz, dc2 * xh, dc2

        seq_end = ((i + 1) * tm) % seq == 0
        for r0 in range(0, tm, sr):
            rows = slice(r0, r0 + sr)
            dc, dz, lw_terms, lb_terms = post_bwd(dy_ref[rows, :], c_ref[rows, :], z_ref[rows, :])
            dyext[rows, :] = dc
            grp_ref[rows, 2 * cw:] = dz.astype(grp_ref.dtype)
            lwacc[...] += _rowsum8(lw_terms)
            lbacc[...] += _rowsum8(lb_terms)
        dc_next = post_bwd(dyn_ref[...], cn_ref[...], zn_ref[...])[0]
        dyext[tm:, :] = jnp.where(seq_end, 0.0, dc_next)
        _glu_window(xext, a_ref, ah_ref, g_ref, gh_ref, (i * tm) % seq == 0, halo)
        _conv_fill_shifted(dyext, dysh)
        _conv_fill_shifted(xext, xsh)
        dag_ref = grp_ref
        for r0, c0 in _conv_subtiles(tm, cw):
            cs = slice(c0, c0 + sc)
            rows = slice(r0, r0 + sr)
            dyv = dyext[rows, cs]
            acc = jnp.zeros((sr, sc), F32)
            for j in range(k):
                acc = acc + w_ref[j:j + 1, cs] * _conv_rows(dyext, (dysh,), r0 + p - j, sr, cs)
                wacc[j, :, cs] += _rowsum8(dyv * _conv_rows(xext, (xsh,), r0 + halo - p + j, sr, cs))
            bacc[:, cs] += _rowsum8(dyv)
            s = _sigmoid(g_ref[rows, cs])
            dag_ref[rows, cs] = (acc * s).astype(dag_ref.dtype)
            dag_ref[rows, cw + c0:cw + c0 + sc] = (acc * a_ref[rows, cs] * s * (1.0 - s)).astype(dag_ref.dtype)

        @pl.when(i == nt - 1)
        def _():
            dw_ref[...] = jnp.zeros_like(dw_ref)
            for j in range(k):
                dw_ref[j:j + 1, :] = jnp.sum(wacc[j], axis=0, keepdims=True)
            db_ref[...] = jnp.sum(bacc[...], axis=0, keepdims=True)
            dlw_ref[...] = jnp.sum(lwacc[...], axis=0, keepdims=True)
            dlb_ref[...] = jnp.sum(lbacc[...], axis=0, keepdims=True)

    def blk(col):
        return pl.BlockSpec((tm, cw), lambda i: (i, col))

    def nxt(col):
        return pl.BlockSpec((halo, cw), lambda i: (jnp.minimum((i + 1) * (tm // halo), last_halo), col))

    vec = pl.BlockSpec((1, cw), lambda i: (0, 0))
    return pl.pallas_call(
        body, name=name,
        out_shape=(jax.ShapeDtypeStruct(dproj.shape, dproj.dtype), jax.ShapeDtypeStruct((kp, cw), F32),
                   jax.ShapeDtypeStruct((1, cw), F32), jax.ShapeDtypeStruct((1, cw), F32),
                   jax.ShapeDtypeStruct((1, cw), F32)),
        grid=(nt,),
        in_specs=[blk(YCAT_CONF // cw), nxt(YCAT_CONF // cw), blk(0), nxt(0), blk(OFF_ZC // cw), nxt(OFF_ZC // cw),
                  *_conf_specs(tm, cw, halo, lambda i: i),
                  pl.BlockSpec((kp, cw), lambda i: (0, 0)), vec, vec, ANY],
        out_specs=(pl.BlockSpec((tm, CONF_GROUP), lambda i: (i, OFF_CONF // CONF_GROUP)),
                   pl.BlockSpec((kp, cw), lambda i: (0, 0)), vec, vec, vec),
        input_output_aliases={13: 0},
        scratch_shapes=[pltpu.VMEM((tm + halo, cw), F32), pltpu.VMEM((halo + tm, cw), F32),
                        pltpu.VMEM((kp, SUBLANES, cw), F32), pltpu.VMEM((SUBLANES, cw), F32),
                        pltpu.VMEM((SUBLANES, cw), F32), pltpu.VMEM((SUBLANES, cw), F32)]
        + 2 * _conv_shift_scratch(k, halo + tm, cw),
        compiler_params=_params(("arbitrary",)),
    )(dycat, dycat, c1, c1, proj, proj, proj, proj, proj, proj, w, ln_w, ln_b, dproj)


def _half_mask(half):
    lane = _iota((1, LANES), 1)
    return ((lane >= half * ATTN_HEAD_DIM) & (lane < (half + 1) * ATTN_HEAD_DIM)).astype(F32)


def _stack_heads(xp, g):
    m = _half_mask(g)
    swapped = pltpu.roll(xp, ATTN_HEAD_DIM, axis=1)
    return jnp.concatenate([xp * m, swapped * m] if g == 0 else [swapped * m, xp * m], axis=0)


def _unstack_heads(both, g):
    w = both.shape[0] // 2
    top, bot = both[:w], both[w:]
    lo, hi = _half_mask(0), _half_mask(1)
    if g == 0:
        return top * lo + pltpu.roll(bot, ATTN_HEAD_DIM, axis=1) * hi
    return pltpu.roll(top, ATTN_HEAD_DIM, axis=1) * lo + bot * hi


def _band_mask(first_block):
    w = WINDOW
    qi = _iota((w, 2 * w), 0)
    kj = _iota((w, 2 * w), 1) - w
    rel = qi - kj
    return (rel >= 0) & (rel < w) & (jnp.logical_not(first_block) | (kj >= 0))


def _lane_pick(x, h):
    return jnp.sum(jnp.where(_iota(x.shape, 1) == h, x, 0.0), axis=1, keepdims=True)


def _attn_specs(nb, rev):
    w = WINDOW

    def blk(i):
        return nb - 1 - i if rev else i

    def row(b, i):
        return b * nb + blk(i)

    def prow(b, i):
        return b * nb + jnp.maximum(blk(i) - 1, 0)

    q = pl.BlockSpec((w, 512), lambda b, i: (row(b, i), OFF_Q // 512))
    kc = pl.BlockSpec((w, 128), lambda b, i: (row(b, i), OFF_K // 128))
    kp = pl.BlockSpec((w, 128), lambda b, i: (prow(b, i), OFF_K // 128))
    vc = pl.BlockSpec((w, 128), lambda b, i: (row(b, i), OFF_V // 128))
    vp = pl.BlockSpec((w, 128), lambda b, i: (prow(b, i), OFF_V // 128))
    z = pl.BlockSpec((w, 512), lambda b, i: (row(b, i), OFF_ZA // 512))
    return q, kc, kp, vc, vp, z, row


def _attn_fwd(proj, sinks, ycat, nbatch, name):
    t = proj.shape[0]
    w = WINDOW
    nb = t // nbatch // w
    scale = ATTN_HEAD_DIM ** -0.5
    q_s, kc_s, kp_s, vc_s, vp_s, z_s, row = _attn_specs(nb, False)

    def body(q_ref, kc_ref, kp_ref, vc_ref, vp_ref, z_ref, sk_ref, _, y_ref, o_ref, lse_ref):
        first = pl.program_id(1) == 0
        mask = _band_mask(first)
        kk = jnp.concatenate([kp_ref[...], kc_ref[...]], axis=0).astype(MXU_DTYPE)
        vv = jnp.concatenate([vp_ref[...], vc_ref[...]], axis=0).astype(MXU_DTYPE)
        sk = sk_ref[...]
        lane = _iota((w, LANES), 1)
        mask2 = jnp.concatenate([mask, mask], axis=0)
        scores = [_dot(_stack_heads(q_ref[:, j * LANES:(j + 1) * LANES], j // 2), kk, NT) for j in range(4)]
        lse_all = jnp.zeros((w, LANES), F32)
        for j in range(4):
            s = jnp.where(mask2, scores[j] * scale, -1e30)
            skc = jnp.concatenate([jnp.broadcast_to(_lane_pick(sk, 2 * j), (w, 1)),
                                   jnp.broadcast_to(_lane_pick(sk, 2 * j + 1), (w, 1))], axis=0)
            m = jnp.maximum(jnp.max(s, axis=1, keepdims=True), skc)
            den = jnp.sum(jnp.exp(s - m), axis=1, keepdims=True) + jnp.exp(skc - m)
            lse = m + jnp.log(den)
            lse_all = jnp.where(lane == 2 * j, lse[:w], lse_all)
            lse_all = jnp.where(lane == 2 * j + 1, lse[w:], lse_all)
            op = _unstack_heads(_dot(jnp.exp(s - lse), vv), j // 2)
            cols = slice(j * LANES, (j + 1) * LANES)
            o_ref[:, cols] = op
            y_ref[:, cols] = (op * _silu(z_ref[:, cols])).astype(y_ref.dtype)
        lse_ref[...] = lse_all

    return pl.pallas_call(
        body, name=name,
        out_shape=(jax.ShapeDtypeStruct(ycat.shape, ycat.dtype), jax.ShapeDtypeStruct((t, 512), F32),
                   jax.ShapeDtypeStruct((t, LANES), F32)),
        grid=(nbatch, nb),
        in_specs=[q_s, kc_s, kp_s, vc_s, vp_s, z_s, pl.BlockSpec((1, LANES), lambda b, i: (0, 0)), ANY],
        out_specs=(pl.BlockSpec((w, 512), lambda b, i: (row(b, i), YCAT_ATTN // 512)),
                   pl.BlockSpec((w, 512), lambda b, i: (row(b, i), 0)),
                   pl.BlockSpec((w, LANES), lambda b, i: (row(b, i), 0))),
        input_output_aliases={7: 0},
        compiler_params=_params(("parallel", "parallel")),
    )(proj, proj, proj, proj, proj, proj, sinks, ycat)


def _attn_bwd(dycat, proj, o, lse, sinks, ddt, dproj, nbatch, name):
    t = proj.shape[0]
    w = WINDOW
    nb = t // nbatch // w
    scale = ATTN_HEAD_DIM ** -0.5
    q_s, kc_s, kp_s, vc_s, vp_s, z_s, row = _attn_specs(nb, True)

    def body(dy_ref, q_ref, kc_ref, kp_ref, vc_ref, vp_ref, z_ref, o_ref, lse_ref, sk_ref, ddt_ref, _,
             grp_ref, dsk_ref, kcarry, vcarry, sacc):
        b, i = pl.program_id(0), pl.program_id(1)

        @pl.when((b == 0) & (i == 0))
        def _():
            sacc[...] = jnp.zeros_like(sacc)

        @pl.when(i == 0)
        def _():
            kcarry[...] = jnp.zeros_like(kcarry)
            vcarry[...] = jnp.zeros_like(vcarry)

        first = i == nb - 1
        mask = _band_mask(first)
        kk = jnp.concatenate([kp_ref[...], kc_ref[...]], axis=0).astype(MXU_DTYPE)
        vv = jnp.concatenate([vp_ref[...], vc_ref[...]], axis=0).astype(MXU_DTYPE)
        sk = sk_ref[...]
        lse_all = lse_ref[...]
        lane1 = _iota((1, LANES), 1)
        mask2 = jnp.concatenate([mask, mask], axis=0)
        qs, dos, deltas, lses, scores, dps = [], [], [], [], [], []
        for j in range(4):
            cols = slice(j * LANES, (j + 1) * LANES)
            qp, zp, ov, dy = q_ref[:, cols], z_ref[:, cols], o_ref[:, cols], dy_ref[:, cols]
            grp_ref[:, OFF_ZA + j * LANES:OFF_ZA + (j + 1) * LANES] = (dy * ov * _dsilu(zp)).astype(grp_ref.dtype)
            do = dy * _silu(zp)
            q2 = _stack_heads(qp, j // 2).astype(MXU_DTYPE)
            do2 = _stack_heads(do, j // 2)
            qs.append(q2)
            dos.append(do2.astype(MXU_DTYPE))
            deltas.append(jnp.sum(do2 * _stack_heads(ov, j // 2), axis=1, keepdims=True))
            lses.append(jnp.concatenate([_lane_pick(lse_all, 2 * j), _lane_pick(lse_all, 2 * j + 1)], axis=0))
            scores.append(_dot(q2, kk, NT))
            dps.append(_dot(do2, vv, NT))
        prs, dss = [], []
        dsk = jnp.zeros((1, LANES), F32)
        for j in range(4):
            pr = jnp.exp(jnp.where(mask2, scores[j] * scale, -1e30) - lses[j])
            prs.append(pr.astype(MXU_DTYPE))
            dss.append((pr * (dps[j] - deltas[j])).astype(MXU_DTYPE))
            skc = jnp.concatenate([jnp.broadcast_to(_lane_pick(sk, 2 * j), (w, 1)),
                                   jnp.broadcast_to(_lane_pick(sk, 2 * j + 1), (w, 1))], axis=0)
            sink_term = jnp.exp(skc - lses[j]) * deltas[j]
            dsk = dsk - jnp.where(lane1 == 2 * j, jnp.sum(sink_term[:w]), 0.0)
            dsk = dsk - jnp.where(lane1 == 2 * j + 1, jnp.sum(sink_term[w:]), 0.0)
        dkk = jnp.zeros((2 * w, LANES), F32)
        dvv = jnp.zeros((2 * w, LANES), F32)
        for j in range(4):
            dq = _unstack_heads(_dot(dss[j], kk) * scale, j // 2)
            grp_ref[:, OFF_Q + j * LANES:OFF_Q + (j + 1) * LANES] = dq.astype(grp_ref.dtype)
            dkk = dkk + _dot(dss[j], qs[j], TN) * scale
            dvv = dvv + _dot(prs[j], dos[j], TN)
        grp_ref[:, OFF_K:OFF_K + LANES] = (dkk[w:, :] + kcarry[...]).astype(grp_ref.dtype)
        grp_ref[:, OFF_V:OFF_V + LANES] = (dvv[w:, :] + vcarry[...]).astype(grp_ref.dtype)
        grp_ref[:, OFF_DT:OFF_DT + LANES] = ddt_ref[...].astype(grp_ref.dtype)
        grp_ref[:, OFF_DT + LANES:] = jnp.zeros((w, ATTN_GROUP - OFF_DT - LANES), grp_ref.dtype)
        kcarry[...] = dkk[:w, :]
        vcarry[...] = dvv[:w, :]
        sacc[...] += dsk

        @pl.when((b == nbatch - 1) & (i == nb - 1))
        def _():
            dsk_ref[...] = sacc[...]

    return pl.pallas_call(
        body, name=name,
        out_shape=(jax.ShapeDtypeStruct(dproj.shape, dproj.dtype), jax.ShapeDtypeStruct((1, LANES), F32)),
        grid=(nbatch, nb),
        in_specs=[pl.BlockSpec((w, 512), lambda b, i: (row(b, i), YCAT_ATTN // 512)),
                  q_s, kc_s, kp_s, vc_s, vp_s, z_s,
                  pl.BlockSpec((w, 512), lambda b, i: (row(b, i), 0)),
                  pl.BlockSpec((w, LANES), lambda b, i: (row(b, i), 0)),
                  pl.BlockSpec((1, LANES), lambda b, i: (0, 0)),
                  pl.BlockSpec((w, LANES), lambda b, i: (row(b, i), 0)), ANY],
        out_specs=(pl.BlockSpec((w, ATTN_GROUP), lambda b, i: (row(b, i), 0)),
                   pl.BlockSpec((1, LANES), lambda b, i: (0, 0))),
        input_output_aliases={11: 0},
        scratch_shapes=[pltpu.VMEM((w, LANES), F32), pltpu.VMEM((w, LANES), F32),
                        pltpu.VMEM((1, LANES), F32)],
        compiler_params=_params(("arbitrary", "arbitrary")),
    )(dycat, proj, proj, proj, proj, proj, proj, o, lse, sinks, ddt, dproj)


SSD_WIDTH = SSD_HEADS * SSD_HEAD_DIM
GROUP_ROWS = SSD_WIDTH // 2


def _expand_mat():
    r, c = _iota((LANES, SSD_WIDTH), 0), _iota((LANES, SSD_WIDTH), 1)
    return (r == lax.shift_right_logical(c, 6)).astype(BF16)


def _expand_mat_t():
    r, c = _iota((SSD_WIDTH, LANES), 0), _iota((SSD_WIDTH, LANES), 1)
    return (c == lax.shift_right_logical(r, 6)).astype(BF16)


def _ssd_common(u_ref, dt_ref, dtb_ref, a_ref, stack_broadcasts=False):
    q = CHUNK
    act = _silu(u_ref[...])
    xs = act[:, :SSD_WIDTH]
    bm = act[:, SSD_WIDTH:SSD_WIDTH + 256]
    cm = act[:, SSD_WIDTH + 256:]
    dtp = _softplus(dt_ref[...] + dtb_ref[...])
    a = dtp * a_ref[...]
    tril = (_iota((q, q), 0) >= _iota((q, q), 1)).astype(BF16)
    acs = _xdot_r(tril, a)
    acs_t = acs.T
    e = _expand_mat()
    a_end = jnp.sum(jnp.where(_iota(acs.shape, 0) == q - 1, acs, 0.0), axis=0, keepdims=True)
    if stack_broadcasts:
        spread = _xdot(jnp.concatenate([dtp, acs, a_end - acs], axis=0), e)
        dt_x, ea, dec = spread[:q], jnp.exp(spread[q:2 * q]), jnp.exp(spread[2 * q:])
    else:
        dt_x = _xdot(dtp, e)
        ea = jnp.exp(_xdot(acs, e))
        dec = jnp.exp(_xdot(a_end - acs, e))
    a_end_col = jnp.broadcast_to(_lane_pick(acs_t, q - 1), (LANES, LANES))
    s_scale = jnp.exp(_xdot_r(_expand_mat_t(), a_end_col))
    return act, xs, bm, cm, dtp, acs, acs_t, dt_x, ea, dec, s_scale, tril


def _decay_mat(acs, acs_t, h):
    q = CHUNK
    col = _lane_pick(acs, h)
    rowv = jnp.sum(jnp.where(_iota(acs_t.shape, 0) == h, acs_t, 0.0), axis=0, keepdims=True)
    causal = _iota((q, q), 0) >= _iota((q, q), 1)
    return jnp.exp(jnp.where(causal, col - rowv, -1e30))


GN_WIDTH = 512


def _ssd_fwd(u, proj, dtb, a_neg, d_x, norm_w, ycat, nbatch, name):
    t = u.shape[0]
    q = CHUNK
    nc = t // nbatch // q

    def body(u_ref, dt_ref, z_ref, dtb_ref, a_ref, dx_ref, nw_ref, _, y_ref, st_ref, yn_ref, state):
        c = pl.program_id(1)

        @pl.when(c == 0)
        def _():
            state[...] = jnp.zeros_like(state)

        st_ref[...] = state[...]
        act, xs, bm, cm, dtp, acs, acs_t, dt_x, ea, dec, s_scale, _ = _ssd_common(u_ref, dt_ref, dtb_ref, a_ref)
        xdt = xs * dt_x
        xdec = xdt * dec
        lo, hi = _half_mask(0), _half_mask(1)
        grp = []
        for g in range(2):
            bg = bm[:, g * LANES:(g + 1) * LANES]
            cg = cm[:, g * LANES:(g + 1) * LANES]
            rows = slice(g * GROUP_ROWS, (g + 1) * GROUP_ROWS)
            sg = state[rows, :]
            grp.append((_dot(cg, bg, NT), _dot(cg, sg, NT), rows,
                        s_scale[rows, :] * sg + _dot(xdec[:, rows], bg, TN)))
        yps = []
        for pj in range(SSD_HEADS // 2):
            cb = grp[pj // 4][0]
            xp = xdt[:, pj * LANES:(pj + 1) * LANES]
            m2 = jnp.concatenate([cb * _decay_mat(acs, acs_t, 2 * pj), cb * _decay_mat(acs, acs_t, 2 * pj + 1)],
                                 axis=1)
            yps.append(_dot(m2, jnp.concatenate([xp * lo, xp * hi], axis=0)))
        for g in range(2):
            _, yoff, rows, state_new = grp[g]
            for j in range(4):
                pj = g * 4 + j
                cols = slice(pj * LANES, (pj + 1) * LANES)
                yp = yps[pj] + yoff[:, j * LANES:(j + 1) * LANES] * ea[:, cols]
                y_ref[:, cols] = yp + dx_ref[:, cols] * xs[:, cols]
            state[rows, :] = state_new
        for g in range(SSD_WIDTH // GN_WIDTH):
            cols = slice(g * GN_WIDTH, (g + 1) * GN_WIDTH)
            gg = y_ref[:, cols] * _silu(z_ref[:, cols])
            rstd = lax.rsqrt(jnp.mean(gg * gg, axis=-1, keepdims=True) + EPS)
            yn_ref[:, cols] = (gg * rstd * nw_ref[:, cols]).astype(yn_ref.dtype)

    vec = pl.BlockSpec((1, LANES), lambda b, c: (0, 0))
    wide = pl.BlockSpec((q, SSD_WIDTH), lambda b, c: (b * nc + c, 0))
    wvec = pl.BlockSpec((1, SSD_WIDTH), lambda b, c: (0, 0))
    return pl.pallas_call(
        body, name=name,
        out_shape=(jax.ShapeDtypeStruct((t, SSD_WIDTH), F32),
                   jax.ShapeDtypeStruct((nbatch * nc * SSD_WIDTH, SSD_STATE), F32),
                   jax.ShapeDtypeStruct(ycat.shape, ycat.dtype)),
        grid=(nbatch, nc),
        in_specs=[pl.BlockSpec((q, SSD_CONV_DIM), lambda b, c: (b * nc + c, 0)),
                  pl.BlockSpec((q, LANES), lambda b, c: (b * nc + c, OFF_DT // LANES)),
                  pl.BlockSpec((q, SSD_WIDTH), lambda b, c: (b * nc + c, OFF_ZS // SSD_WIDTH)),
                  vec, vec, wvec, wvec, ANY],
        out_specs=(wide, pl.BlockSpec((SSD_WIDTH, SSD_STATE), lambda b, c: (b * nc + c, 0)), wide),
        input_output_aliases={7: 2},
        scratch_shapes=[pltpu.VMEM((SSD_WIDTH, SSD_STATE), F32)],
        compiler_params=_params(("parallel", "arbitrary")),
    )(u, proj, proj, dtb, a_neg, d_x, norm_w, ycat)


def _ssd_bwd(dycat, u, proj, y, states, dtb, a_neg, d_x, norm_w, dproj, nbatch, name):
    t = u.shape[0]
    q = CHUNK
    nc = t // nbatch // q

    def body(do_ref, u_ref, dt_ref, z_ref, y_ref, st_ref, dtb_ref, a_ref, dx_ref, nw_ref, _,
             du_ref, dz_ref, ddt_ref, dal_ref, dd_ref, dtbg_ref, dnw_ref, dstate, acc_a, acc_d, acc_b, acc_w):
        b, c = pl.program_id(0), pl.program_id(1)

        @pl.when((b == 0) & (c == 0))
        def _():
            acc_a[...] = jnp.zeros_like(acc_a)
            acc_d[...] = jnp.zeros_like(acc_d)
            acc_b[...] = jnp.zeros_like(acc_b)
            acc_w[...] = jnp.zeros_like(acc_w)

        @pl.when(c == 0)
        def _():
            dstate[...] = jnp.zeros_like(dstate)

        dy_parts = []
        for g in range(SSD_WIDTH // GN_WIDTH):
            cols = slice(g * GN_WIDTH, (g + 1) * GN_WIDTH)
            yv, zv, dov = y_ref[:, cols], z_ref[:, cols], do_ref[:, cols]
            sz = _silu(zv)
            gg = yv * sz
            rstd = lax.rsqrt(jnp.mean(gg * gg, axis=-1, keepdims=True) + EPS)
            gh = gg * rstd
            acc_w[:, cols] += _rowsum8(dov * gh)
            dgn = dov * nw_ref[:, cols]
            dg = rstd * (dgn - gh * jnp.mean(dgn * gh, axis=-1, keepdims=True))
            dy_parts.append(dg * sz)
            dz_ref[:, cols] = (dg * yv * _dsilu(zv)).astype(dz_ref.dtype)

        act, xs, bm, cm, dtp, acs, acs_t, dt_x, ea, dec, s_scale, tril = _ssd_common(
            u_ref, dt_ref, dtb_ref, a_ref, stack_broadcasts=True)
        xdt = xs * dt_x
        xdec = xdt * dec
        dyv = jnp.concatenate(dy_parts, axis=1)
        dye = dyv * ea
        lo, hi = _half_mask(0), _half_mask(1)
        et = _expand_mat_t()
        grp = []
        for g in range(2):
            rows = slice(g * GROUP_ROWS, (g + 1) * GROUP_ROWS)
            bg = bm[:, g * LANES:(g + 1) * LANES]
            cg = cm[:, g * LANES:(g + 1) * LANES]
            sg = st_ref[rows, :]
            dsg = dstate[rows, :]
            grp.append(dict(
                rows=rows, bg=bg, cg=cg, dsg=dsg,
                cb=_dot(cg, bg, NT), yoff=_dot(cg, sg, NT), dxst=_dot(bg, dsg, NT) * dec[:, rows],
                dc_off=_dot(dye[:, rows], sg), db_off=_dot(xdec[:, rows], dsg),
                s_carried=s_scale[rows, :] * sg,
                dstate_new=_dot(dye[:, rows], cg, TN) + s_scale[rows, :] * dsg))
        dy2s, g2s, l2s = [], [], []
        for pj in range(SSD_HEADS // 2):
            cols = slice(pj * LANES, (pj + 1) * LANES)
            dyp = dyv[:, cols]
            dy2 = jnp.concatenate([dyp * lo, dyp * hi], axis=0).astype(MXU_DTYPE)
            dy2s.append(dy2)
            g2s.append(_dot(dy2, xdt[:, cols], NT))
            l2s.append(jnp.concatenate([_decay_mat(acs, acs_t, 2 * pj), _decay_mat(acs, acs_t, 2 * pj + 1)], axis=0))
        dal_diag = jnp.zeros((q, LANES), F32)
        lane2 = _iota((2 * q, LANES), 1)
        row2 = _iota((2 * q, LANES), 0)
        dxdt_parts, db_parts, dc_parts = [], [], []
        end_sum = jnp.zeros((LANES, LANES), F32)
        for g in range(2):
            gd = grp[g]
            cb2 = jnp.concatenate([gd["cb"], gd["cb"]], axis=0)
            dcb = jnp.zeros((q, q), F32)
            parts = []
            for j in range(4):
                pj = g * 4 + j
                gl = g2s[pj] * l2s[pj]
                dcb = dcb + gl[:q] + gl[q:]
                m2 = cb2 * l2s[pj]
                parts.append(_dot(m2, dy2s[pj], TN))
                w2 = (gl * cb2).astype(MXU_DTYPE)
                sel2 = (lane2 == 2 * pj + (row2 >= q).astype(jnp.int32)).astype(MXU_DTYPE)
                dal_diag = dal_diag + _dot(jnp.concatenate([w2[:q], w2[q:]], axis=1), sel2) - _dot(w2, sel2, TN)
            dxdt_parts.append(jnp.concatenate(parts, axis=1) + gd["dxst"])
            dc_parts.append(_dot(dcb, gd["bg"]) + gd["dc_off"])
            db_parts.append(_dot(dcb, gd["cg"], TN) + gd["db_off"])
            end_sum = end_sum + _xdot(gd["dsg"] * gd["s_carried"], et[gd["rows"], :], TN, passes=2)
            dstate[gd["rows"], :] = gd["dstate_new"]
        dxst_parts = [gd["dxst"] for gd in grp]
        yoff_parts = [gd["yoff"] for gd in grp]
        dxdt = jnp.concatenate(dxdt_parts, axis=1)
        dxv = dx_ref[...]
        yoff = jnp.concatenate(yoff_parts, axis=1) * ea
        per_head = _xdot(jnp.concatenate([dyv * yoff, xdt * jnp.concatenate(dxst_parts, axis=1),
                                          dxdt * xs, dyv * xs], axis=0), et)
        off_term, st_term, dx_term, d_term = (per_head[k * q:(k + 1) * q] for k in range(4))
        dalpha = dal_diag + off_term - st_term
        end_row = jnp.sum(end_sum, axis=0, keepdims=True) + jnp.sum(st_term, axis=0, keepdims=True)
        dalpha = dalpha + jnp.where(_iota((q, LANES), 0) == q - 1, end_row, 0.0)
        da = _xdot_r(tril, dalpha, TN)
        ddtp = da * a_ref[...] + dx_term
        acc_a[...] += _rowsum8(da * dtp)
        acc_d[...] += _rowsum8(d_term)
        ddt_raw = ddtp * _sigmoid(dt_ref[...] + dtb_ref[...])
        acc_b[...] += _rowsum8(ddt_raw)
        ddt_ref[...] = ddt_raw
        dxs = dxdt * dt_x + dxv * dyv
        dact = jnp.concatenate([dxs] + db_parts + dc_parts, axis=1)
        du_ref[...] = dact * _dsilu(u_ref[...])

        @pl.when((b == nbatch - 1) & (c == nc - 1))
        def _():
            dal_ref[...] = jnp.sum(acc_a[...], axis=0, keepdims=True) * a_ref[...]
            dd_ref[...] = jnp.sum(acc_d[...], axis=0, keepdims=True)
            dtbg_ref[...] = jnp.sum(acc_b[...], axis=0, keepdims=True)
            dnw_ref[...] = jnp.sum(acc_w[...], axis=0, keepdims=True)

    def rowblk(b, c):
        return b * nc + (nc - 1 - c)

    vec = pl.BlockSpec((1, LANES), lambda b, c: (0, 0))
    wvec = pl.BlockSpec((1, SSD_WIDTH), lambda b, c: (0, 0))
    wide = pl.BlockSpec((q, SSD_WIDTH), lambda b, c: (rowblk(b, c), 0))
    zblk = pl.BlockSpec((q, SSD_WIDTH), lambda b, c: (rowblk(b, c), OFF_ZS // SSD_WIDTH))
    return pl.pallas_call(
        body, name=name,
        out_shape=(jax.ShapeDtypeStruct((t, SSD_CONV_DIM), F32), jax.ShapeDtypeStruct(dproj.shape, dproj.dtype),
                   jax.ShapeDtypeStruct((t, LANES), F32),
                   jax.ShapeDtypeStruct((1, LANES), F32), jax.ShapeDtypeStruct((1, LANES), F32),
                   jax.ShapeDtypeStruct((1, LANES), F32), jax.ShapeDtypeStruct((1, SSD_WIDTH), F32)),
        grid=(nbatch, nc),
        in_specs=[wide,
                  pl.BlockSpec((q, SSD_CONV_DIM), lambda b, c: (rowblk(b, c), 0)),
                  pl.BlockSpec((q, LANES), lambda b, c: (rowblk(b, c), OFF_DT // LANES)),
                  zblk, wide,
                  pl.BlockSpec((SSD_WIDTH, SSD_STATE), lambda b, c: (rowblk(b, c), 0)),
                  vec, vec, wvec, wvec, ANY],
        out_specs=(pl.BlockSpec((q, SSD_CONV_DIM), lambda b, c: (rowblk(b, c), 0)),
                   zblk,
                   pl.BlockSpec((q, LANES), lambda b, c: (rowblk(b, c), 0)),
                   vec, vec, vec, wvec),
        input_output_aliases={10: 1},
        scratch_shapes=[pltpu.VMEM((SSD_WIDTH, SSD_STATE), F32), pltpu.VMEM((SUBLANES, LANES), F32),
                        pltpu.VMEM((SUBLANES, LANES), F32), pltpu.VMEM((SUBLANES, LANES), F32),
                        pltpu.VMEM((SUBLANES, SSD_WIDTH), F32)],
        compiler_params=_params(("arbitrary", "arbitrary")),
    )(dycat, u, proj, proj, y, states, dtb, a_neg, d_x, norm_w, dproj)


def _pad_rows(w, rows):
    return jnp.concatenate([w, jnp.zeros((rows - w.shape[0], w.shape[1]), w.dtype)], axis=0)


def _pad_lanes(v):
    return jnp.concatenate([v, jnp.zeros((LANES - v.shape[0],), v.dtype)]).reshape(1, LANES)


def _padded_from_chips(pieces):
    cols = pieces[0].shape[-1]
    lead = pieces[0].shape[:-1]
    parts, pos = [], 0
    for lo, hi, start in sorted(SECTIONS, key=lambda s: s[2]):
        if start > pos:
            parts.append(jnp.zeros(lead + (start - pos,), pieces[0].dtype))
        pos = start + hi - lo
        while lo < hi:
            p = lo // cols
            end = min(hi, (p + 1) * cols)
            parts.append(pieces[p][..., lo - p * cols:end - p * cols])
            lo = end
    if pos < NP:
        parts.append(jnp.zeros(lead + (NP - pos,), pieces[0].dtype))
    return jnp.concatenate(parts, axis=-1)


def _chip_part_from_padded(wp, p, cols):
    lo, hi = p * cols, (p + 1) * cols
    parts = []
    for rs, re, start in SECTIONS:
        a, b = max(lo, rs), min(hi, re)
        if a < b:
            parts.append(wp[..., start + a - rs:start + b - rs])
    return jnp.concatenate(parts, axis=-1)


def _layer_params(li, w_in_p, w_out, conv_w, dw_w, small):
    return dict(
        w_in_p=w_in_p, w_out=w_out,
        conv_w=_pad_rows(conv_w, SUBLANES), dw_w=_pad_rows(dw_w, 32),
        norm_w=small["norm_w"][li].reshape(1, -1),
        conv_b=small["ssd_conv_b"][li].reshape(1, -1),
        dtb=_pad_lanes(small["ssd_dt_bias"][li]),
        a_neg=_pad_lanes(-jnp.exp(small["ssd_a_log"][li])),
        d_x=jnp.repeat(small["ssd_d"][li], SSD_HEAD_DIM).reshape(1, -1),
        ssd_norm_w=small["ssd_norm_w"][li].reshape(1, -1),
        sinks=_pad_lanes(small["attn_sinks"][li]),
        dw_b=small["conf_dw_b"][li].reshape(1, -1),
        ln_w=small["conf_ln_w"][li].reshape(1, -1),
        ln_b=small["conf_ln_b"][li].reshape(1, -1),
    )


def _layer_fwd(x, p, nbatch, seq, tag, after=None):
    proj, h_t = _proj_fwd(x, p["norm_w"], p["w_in_p"], name=f"proj_fwd_{tag}", after=after)
    u = _conv_fwd(proj, OFF_XBC, SSD_CONV_DIM, p["conv_w"], p["conv_b"], SSD_CONV, seq, name=f"ssd_conv_fwd_{tag}")
    ycat = lax.empty((x.shape[0], MIX_WIDTH), MXU_DTYPE)
    y, states, ycat = _ssd_fwd(u, proj, p["dtb"], p["a_neg"], p["d_x"], p["ssd_norm_w"], ycat, nbatch,
                               name=f"ssd_fwd_{tag}")
    ycat, o, lse = _attn_fwd(proj, p["sinks"], ycat, nbatch, name=f"attn_fwd_{tag}")
    c1, ycat = _conf_fwd(proj, p["dw_w"], p["dw_b"], p["ln_w"], p["ln_b"], ycat, seq, name=f"conf_fwd_{tag}")
    w_out = p["w_out"](ycat) if callable(p["w_out"]) else p["w_out"]
    x_new = _matmul(ycat, w_out, "nn", F32, 2048, 512, 2048, name=f"out_fwd_{tag}", residual=x)
    return x_new, dict(x=x, w_out=w_out, h_t=h_t, proj=proj, u=u, y=y, states=states, o=o, lse=lse, c1=c1, ycat=ycat)


def _layer_bwd(dx_out, p, s, nbatch, seq, tag, hooks=None):
    hooks = hooks or {}
    proj = s["proj"]
    dycat = _matmul(dx_out, s["w_out"], "nt", F32, 1024, 1024, 1024, name=f"out_bwd_dy_{tag}",
                    after=hooks.get("start_token"))
    dw_out = _matmul(s["ycat"], dx_out, "tn", F32, 1024, 1024, 1024, name=f"out_bwd_dw_{tag}")
    token = hooks["after_dycat"](dycat) if "after_dycat" in hooks else None
    dtb = p["dtb"] if token is None else p["dtb"] + token[0, 0]
    dproj = lax.empty(proj.shape, MXU_DTYPE)
    du, dproj, ddt, da_log, dd, ddtb, dssd_norm_w = _ssd_bwd(
        dycat, s["u"], proj, s["y"], s["states"], dtb, p["a_neg"], p["d_x"], p["ssd_norm_w"], dproj,
        nbatch, name=f"ssd_bwd_{tag}")
    dproj, dconv_w, dconv_b = _conv_bwd(du, proj, OFF_XBC, SSD_CONV_DIM, p["conv_w"], SSD_CONV, seq,
                                        name=f"ssd_conv_bwd_{tag}", into=dproj)
    dproj, dsinks = _attn_bwd(dycat, proj, s["o"], s["lse"], p["sinks"], ddt, dproj, nbatch,
                              name=f"attn_bwd_{tag}")
    if "after_attn" in hooks:
        hooks["after_attn"](dproj)
    dproj, ddw_w, ddw_b, dln_w, dln_b = _conf_bwd(dycat, proj, s["c1"], p["dw_w"], p["ln_w"], p["ln_b"], dproj, seq,
                                                  name=f"conf_bwd_{tag}")
    dw_in_p = _matmul(s["h_t"], dproj, "nn", F32, 1024, 512, 4096, name=f"proj_bwd_dw_{tag}")
    token = hooks["after_dw"](dw_in_p, dw_out) if "after_dw" in hooks else None
    norm_w = p["norm_w"] if token is None else p["norm_w"] + token[0, 0]
    dx_in, dnorm_w = _proj_bwd_dx(dproj, p["w_in_p"], s["x"], norm_w, dx_out, name=f"proj_bwd_dx_{tag}")
    grads = dict(
        norm_w=dnorm_w[0], w_in_p=dw_in_p, ssd_conv_w=dconv_w[:SSD_CONV], ssd_conv_b=dconv_b[0],
        ssd_dt_bias=ddtb[0, :SSD_HEADS], ssd_a_log=da_log[0, :SSD_HEADS], ssd_d=dd[0, :SSD_HEADS],
        ssd_norm_w=dssd_norm_w[0], attn_sinks=dsinks[0, :ATTN_Q_HEADS], conf_dw_w=ddw_w[:CONF_KERNEL],
        conf_dw_b=ddw_b[0], conf_ln_w=dln_w[0], conf_ln_b=dln_b[0], w_out=dw_out)
    return dx_in, grads


def _local_step(x, target, param_fns, final_norm_w, first_after=None, bwd_hooks=None):
    nbatch, seq, d = x.shape
    xt = x.reshape(nbatch * seq, d)
    saved, layer_params = [], []
    for li, fn in enumerate(param_fns):
        p = fn(xt)
        layer_params.append(p)
        xt, s = _layer_fwd(xt, p, nbatch, seq, f"l{li}", after=first_after if li == 0 else None)
        saved.append(s)
    loss, dx, dfinal = _loss_head(xt, target.reshape(nbatch * seq, d), final_norm_w.reshape(1, d), name="loss_head")
    grads = [None] * len(layer_params)
    for li in reversed(range(len(layer_params))):
        hooks = bwd_hooks(li) if bwd_hooks is not None else None
        dx, grads[li] = _layer_bwd(dx, layer_params[li], saved[li], nbatch, seq, f"l{li}", hooks=hooks)
    return loss[0, 0], dx.reshape(nbatch, seq, d), grads, dfinal[0]


MESH = pl.DeviceIdType.MESH
N_CHIPS = 4


def _mesh_pos():
    return lax.axis_index("x"), lax.axis_index("y"), lax.axis_index("c")


def _other_chips(x, y):
    return [(1 - x, y), (x, 1 - y), (1 - x, 1 - y)]


def _gather_weights(big, small, name):
    nbig, nsmall = len(big), len(small)
    n_ici = 3 * (nbig + nsmall)
    n_fwd = 3 * nbig

    def body(*refs):
        ins = refs[:nbig + nsmall]
        outs = refs[nbig + nsmall:2 * (nbig + nsmall)]
        send_sems, recv_sems = refs[2 * (nbig + nsmall):]
        x, y, c = _mesh_pos()
        me = 2 * x + y
        sibling = (x, y, 1 - c)
        chips = _other_chips(x, y)

        def ici(a, j, origin, dest):
            if a < nbig:
                src = ins[a].at[c] if origin is None else outs[a].at[origin, c]
                dst = outs[a].at[me if origin is None else origin, c]
            else:
                src = ins[a] if origin is None else outs[a].at[origin]
                dst = outs[a].at[me if origin is None else origin]
            k = a * 3 + j
            return pltpu.make_async_remote_copy(src_ref=src, dst_ref=dst, send_sem=send_sems.at[k],
                                                recv_sem=recv_sems.at[k], device_id=dest, device_id_type=MESH)

        def fwd(a, j, origin, half):
            k = n_ici + a * 3 + j
            ref = outs[a].at[origin, half]
            return pltpu.make_async_remote_copy(src_ref=ref, dst_ref=ref, send_sem=send_sems.at[k],
                                                recv_sem=recv_sems.at[k], device_id=sibling, device_id_type=MESH)

        sends = []
        for j, (px, py) in enumerate(chips):
            for a in range(nbig + nsmall):
                cp = ici(a, j, None, (px, py, c))
                cp.start()
                sends.append(cp)
        for j, (px, py) in enumerate(chips):
            origin = 2 * px + py
            for a in range(nbig):
                ici(a, j, origin, (px, py, c)).wait_recv()
                cp = fwd(a, j, origin, c)
                cp.start()
                sends.append(cp)
        for j, (px, py) in enumerate(chips):
            origin = 2 * px + py
            for a in range(nbig, nbig + nsmall):
                ici(a, j, origin, (px, py, c)).wait_recv()
            for a in range(nbig):
                fwd(a, j, origin, 1 - c).wait_recv()
        for cp in sends:
            cp.wait_send()

    out_shape = tuple(jax.ShapeDtypeStruct((N_CHIPS,) + a.shape, a.dtype) for a in list(big) + list(small))
    return pl.pallas_call(
        body, name=name, out_shape=out_shape,
        in_specs=[ANY] * (nbig + nsmall), out_specs=tuple([ANY] * (nbig + nsmall)),
        scratch_shapes=[pltpu.SemaphoreType.DMA((n_ici + n_fwd,)), pltpu.SemaphoreType.DMA((n_ici + n_fwd,))],
    )(*big, *small)


HBM = pl.BlockSpec(memory_space=pltpu.HBM)
SEM = pl.BlockSpec(memory_space=pltpu.SEMAPHORE)
DATAFLOW = pltpu.SideEffectType.DATAFLOW_SIDE_EFFECTING


def _split_peers(pattern, x, y, c):
    if pattern == "swap":
        return [((x, y, 1 - c), 1 - c, None, None)]
    me = 2 * x + y
    return [((px, py, c), 2 * px + py if pattern == "scatter" else None, me, 2 * px + py)
            for px, py in _other_chips(x, y)]


def _split_land_shape(pattern, shape):
    return {"bcast": (N_CHIPS,) + shape, "scatter": shape, "swap": shape[:1] + shape[2:]}[pattern]


def _split_copies(pattern, srcs, lands, send_sems, recv_sems, waiting):
    x, y, c = _mesh_pos()
    peers = _split_peers(pattern, x, y, c)
    cps = []
    for j, (dev, src_slot, dst_slot, my_slot) in enumerate(peers):
        for a in range(len(srcs)):
            if src_slot is None:
                src = srcs[a]
            else:
                src = srcs[a].at[:, src_slot] if pattern == "swap" else srcs[a].at[src_slot]
            slot = my_slot if waiting else dst_slot
            dst = lands[a] if slot is None else lands[a].at[slot]
            k = a * len(peers) + j
            cps.append(pltpu.make_async_remote_copy(src_ref=src, dst_ref=dst, send_sem=send_sems[k],
                                                    recv_sem=recv_sems[k], device_id=dev, device_id_type=MESH))
    return cps


def _split_start(arrs, pattern, after, name):
    n = len(arrs)
    nsem = n * (1 if pattern == "swap" else N_CHIPS - 1)
    deps = [] if after is None else [after]

    def body(*refs):
        srcs, lands = refs[:n], refs[n:2 * n]
        outs = refs[2 * n + len(deps):]
        for cp in _split_copies(pattern, srcs, lands, outs[:nsem], outs[nsem:2 * nsem], waiting=False):
            cp.start()
        outs[-1][...] = jnp.zeros_like(outs[-1])

    lands = [lax.empty(_split_land_shape(pattern, a.shape), a.dtype) for a in arrs]
    out_shape = ([pltpu.SemaphoreType.DMA(())] * (2 * nsem)
                 + [pltpu.HBM(a.shape, a.dtype) for a in arrs] + [pltpu.HBM(b.shape, b.dtype) for b in lands]
                 + [jax.ShapeDtypeStruct((SUBLANES, LANES), F32)])
    outs = pl.pallas_call(
        body, name=name, out_shape=tuple(out_shape),
        in_specs=[HBM] * (2 * n) + [ANY] * len(deps),
        out_specs=tuple([SEM] * (2 * nsem) + [HBM] * (2 * n) + [pl.BlockSpec(memory_space=pltpu.VMEM)]),
        input_output_aliases={a: 2 * nsem + a for a in range(2 * n)},
        compiler_params=pltpu.CompilerParams(has_side_effects=DATAFLOW),
    )(*[pltpu.with_memory_space_constraint(a, pltpu.HBM) for a in list(arrs) + lands], *deps)
    return outs[:-1], outs[-1]


def _split_wait(state, n, pattern, after, name):
    nsem = n * (1 if pattern == "swap" else N_CHIPS - 1)

    def body(*refs):
        srcs, lands = refs[:n], refs[n:2 * n]
        send_sems, recv_sems = refs[2 * n:2 * n + nsem], refs[2 * n + nsem:2 * n + 2 * nsem]
        for cp in _split_copies(pattern, srcs, lands, send_sems, recv_sems, waiting=True):
            cp.wait_send()
            cp.wait_recv()

    sems, thru = state[:2 * nsem], state[2 * nsem:]
    outs = pl.pallas_call(
        body, name=name, out_shape=tuple(pltpu.HBM(a.shape, a.dtype) for a in thru),
        in_specs=[HBM] * (2 * n) + [SEM] * (2 * nsem) + [ANY],
        out_specs=tuple([HBM] * (2 * n)),
        input_output_aliases={a: a for a in range(2 * n)},
        compiler_params=pltpu.CompilerParams(has_side_effects=DATAFLOW),
    )(*thru, *sems, after)
    return outs[:n], outs[n:]


def _pair_gather(arrs, layer, name):
    n = len(arrs)

    def body(*refs):
        outs = refs[n:2 * n]
        send_sems, recv_sems = refs[2 * n:]
        x, y, c = _mesh_pos()
        cps = [pltpu.make_async_remote_copy(src_ref=outs[a].at[layer, c], dst_ref=outs[a].at[layer, c],
                                            send_sem=send_sems.at[a], recv_sem=recv_sems.at[a],
                                            device_id=(x, y, 1 - c), device_id_type=MESH)
               for a in range(n)]
        for cp in cps:
            cp.start()
        for cp in cps:
            cp.wait()

    return pl.pallas_call(
        body, name=name, out_shape=tuple(jax.ShapeDtypeStruct(a.shape, a.dtype) for a in arrs),
        in_specs=[ANY] * n, out_specs=tuple([ANY] * n),
        input_output_aliases={a: a for a in range(n)},
        scratch_shapes=[pltpu.SemaphoreType.DMA((n,)), pltpu.SemaphoreType.DMA((n,))],
    )(*arrs)


N_DEV = 8


def _allreduce_small(pack, name):
    r = pack.shape[0]

    def body(p_ref, o_ref, land, send_sems, recv_sems):
        x, y, c = _mesh_pos()
        me = 4 * x + 2 * y + c
        cps = []
        for k in range(1, N_DEV):
            peer = (x ^ (k >> 2), y ^ ((k >> 1) & 1), c ^ (k & 1))
            cps.append(pltpu.make_async_remote_copy(src_ref=p_ref, dst_ref=land.at[me], send_sem=send_sems.at[k - 1],
                                                    recv_sem=recv_sems.at[k - 1], device_id=peer, device_id_type=MESH))
        for cp in cps:
            cp.start()
        land[me] = p_ref[...]
        for cp in cps:
            cp.wait()
        total = land[0]
        for d in range(1, N_DEV):
            total = total + land[d]
        o_ref[...] = total

    vm = pl.BlockSpec(memory_space=pltpu.VMEM)
    return pl.pallas_call(
        body, name=name, out_shape=jax.ShapeDtypeStruct(pack.shape, F32),
        in_specs=[vm], out_specs=vm,
        scratch_shapes=[pltpu.VMEM((N_DEV, r, LANES), F32), pltpu.SemaphoreType.DMA((N_DEV - 1,)),
                        pltpu.SemaphoreType.DMA((N_DEV - 1,))],
    )(pack)


BIG_ROWS = 256


def _cast_layer(w, layer, name):
    _, r, cdim = w.shape
    tr = BIG_ROWS

    def body(w_ref, o_ref):
        o_ref[...] = w_ref[...].astype(o_ref.dtype)

    return pl.pallas_call(
        body, name=name, out_shape=jax.ShapeDtypeStruct((r, cdim), MXU_DTYPE),
        grid=(r // tr,), in_specs=[pl.BlockSpec((None, tr, cdim), lambda i: (layer, i, 0))],
        out_specs=pl.BlockSpec((tr, cdim), lambda i: (i, 0)),
        compiler_params=_params(("parallel",)),
    )(w)


def _cast_cols_major(w_t, name):
    cdim, nl, r = w_t.shape
    tc = LANES

    def body(w_ref, *o_refs):
        for l in range(nl):
            o_refs[l][...] = w_ref[:, l, :].T.astype(o_refs[l].dtype)

    out = pl.BlockSpec((r, tc), lambda i: (0, i))
    return pl.pallas_call(
        body, name=name, out_shape=tuple(jax.ShapeDtypeStruct((r, cdim), MXU_DTYPE) for _ in range(nl)),
        grid=(pl.cdiv(cdim, tc),), in_specs=[pl.BlockSpec((tc, nl, r), lambda i: (i, 0, 0))],
        out_specs=tuple([out] * nl),
        compiler_params=_params(("parallel",)),
    )(w_t)


def _pair_sum(parts, sib, which, out_dtype, name):
    k, _, r, cdim = parts.shape
    tr = BIG_ROWS

    def body(sel_ref, p_ref, s_ref, o_ref):
        o_ref[...] = (p_ref[...] + s_ref[...]).astype(o_ref.dtype)

    grid_spec = pltpu.PrefetchScalarGridSpec(
        num_scalar_prefetch=1, grid=(k, r // tr),
        in_specs=[pl.BlockSpec((None, None, tr, cdim), lambda l, i, sel: (l, sel[0], i, 0)),
                  pl.BlockSpec((None, tr, cdim), lambda l, i, sel: (l, i, 0))],
        out_specs=pl.BlockSpec((None, tr, cdim), lambda l, i, sel: (l, i, 0)))
    return pl.pallas_call(
        body, name=name, out_shape=jax.ShapeDtypeStruct((k, r, cdim), out_dtype), grid_spec=grid_spec,
        compiler_params=_params(("parallel", "parallel")),
    )(which.reshape(1).astype(jnp.int32), parts, sib)


def _sum_lead(parts, into, layer, which, name):
    k, r, cdim = parts.shape
    tr = BIG_ROWS

    def body(sel_ref, p_ref, _, o_ref):
        total = p_ref[0].astype(F32)
        for a in range(1, k):
            total = total + p_ref[a].astype(F32)
        o_ref[...] = total

    grid_spec = pltpu.PrefetchScalarGridSpec(
        num_scalar_prefetch=1, grid=(r // tr,),
        in_specs=[pl.BlockSpec((k, tr, cdim), lambda i, sel: (0, i, 0)), ANY],
        out_specs=pl.BlockSpec((None, None, tr, cdim), lambda i, sel: (layer, sel[0], i, 0)))
    return pl.pallas_call(
        body, name=name, out_shape=jax.ShapeDtypeStruct(into.shape, F32), grid_spec=grid_spec,
        input_output_aliases={2: 0},
        compiler_params=_params(("parallel",)),
    )(which.reshape(1).astype(jnp.int32), parts, into)


def _adam_math(w, g, m, v):
    m2 = ADAM_B1 * m + (1.0 - ADAM_B1) * g
    v2 = ADAM_B2 * v + (1.0 - ADAM_B2) * (g * g)
    m_hat = m2 / (1.0 - ADAM_B1 ** ADAM_STEP)
    v_hat = v2 / (1.0 - ADAM_B2 ** ADAM_STEP)
    delta = -ADAM_LR * (m_hat / (jnp.sqrt(v_hat) + ADAM_EPS) + ADAM_WD * w)
    return delta, m2, v2


def _adam_big(w, g, m, v, name):
    nl, r, cdim = w.shape
    tr = BIG_ROWS

    def body(w_ref, g_ref, m_ref, v_ref, d_ref, mo_ref, vo_ref):
        delta, m2, v2 = _adam_math(w_ref[...], g_ref[...], m_ref[...], v_ref[...])
        d_ref[...] = delta
        mo_ref[...] = m2
        vo_ref[...] = v2

    blk = pl.BlockSpec((None, tr, cdim), lambda l, i: (l, i, 0))
    shp = jax.ShapeDtypeStruct(w.shape, F32)
    return pl.pallas_call(
        body, name=name, out_shape=(shp, shp, shp),
        grid=(nl, r // tr), in_specs=[blk] * 4, out_specs=(blk, blk, blk),
        compiler_params=_params(("parallel", "parallel")),
    )(w, g, m, v)


def _adam_cols_major(w, g, m, v, name):
    cdim, nl, r = w.shape
    tc = BIG_ROWS

    def body(w_ref, g_ref, m_ref, v_ref, d_ref, mo_ref, vo_ref):
        delta, m2, v2 = _adam_math(w_ref[...], g_ref[...], m_ref[...], v_ref[...])
        d_ref[...] = delta
        mo_ref[...] = m2
        vo_ref[...] = v2

    blk = pl.BlockSpec((tc, nl, r), lambda i: (i, 0, 0))
    shp = jax.ShapeDtypeStruct(w.shape, F32)
    return pl.pallas_call(
        body, name=name, out_shape=(shp, shp, shp),
        grid=(pl.cdiv(cdim, tc),), in_specs=[blk] * 4, out_specs=(blk, blk, blk),
        compiler_params=_params(("parallel",)),
    )(w, g, m, v)


def _adam_small(ws, gs, ms, vs, name):
    n = len(ws)

    def body(*refs):
        w_refs, g_refs, m_refs, v_refs = (refs[k * n:(k + 1) * n] for k in range(4))
        d_refs, mo_refs, vo_refs = (refs[(4 + k) * n:(5 + k) * n] for k in range(3))
        for a in range(n):
            delta, m2, v2 = _adam_math(w_refs[a][...], g_refs[a][...], m_refs[a][...], v_refs[a][...])
            d_refs[a][...] = delta
            mo_refs[a][...] = m2
            vo_refs[a][...] = v2

    shapes = tuple(jax.ShapeDtypeStruct(w.shape, F32) for w in ws)
    vm = pl.BlockSpec(memory_space=pltpu.VMEM)
    outs = pl.pallas_call(body, name=name, out_shape=shapes * 3, in_specs=[vm] * (4 * n),
                          out_specs=tuple([vm] * (3 * n)))(*ws, *gs, *ms, *vs)
    return outs[:n], outs[n:2 * n], outs[2 * n:]


PACK_TILE = SUBLANES * LANES


def _pack(arrays):
    rows = []
    for a in arrays:
        flat = a.reshape(-1)
        pad = (-flat.shape[0]) % PACK_TILE
        if pad:
            flat = jnp.concatenate([flat, jnp.zeros((pad,), flat.dtype)])
        rows.append(flat.reshape(-1, LANES))
    return jnp.concatenate(rows, axis=0)


def _unpack(pack, shapes):
    outs, row = [], 0
    for shp in shapes:
        n = int(np.prod(shp))
        nrows = -(-n // PACK_TILE) * SUBLANES
        outs.append(pack[row:row + nrows].reshape(-1)[:n].reshape(shp))
        row += nrows
    return outs


SMALL = ["norm_w", "ssd_conv_b", "ssd_dt_bias", "ssd_a_log", "ssd_d", "ssd_norm_w", "attn_sinks",
         "conf_dw_b", "conf_ln_w", "conf_ln_b"]
WEIGHTS = ["norm_w", "w_in", "ssd_conv_w", "ssd_conv_b", "ssd_dt_bias", "ssd_a_log", "ssd_d", "ssd_norm_w",
           "attn_sinks", "conf_dw_w", "conf_dw_b", "conf_ln_w", "conf_ln_b", "w_out", "final_norm_w"]


def kernel(x, norm_w, w_in, ssd_conv_w, ssd_conv_b, ssd_dt_bias, ssd_a_log, ssd_d, ssd_norm_w, attn_sinks, conf_dw_w, conf_dw_b, conf_ln_w, conf_ln_b, w_out, final_norm_w, loss_target, m_norm_w, m_w_in, m_ssd_conv_w, m_ssd_conv_b, m_ssd_dt_bias, m_ssd_a_log, m_ssd_d, m_ssd_norm_w, m_attn_sinks, m_conf_dw_w, m_conf_dw_b, m_conf_ln_w, m_conf_ln_b, m_w_out, m_final_norm_w, v_norm_w, v_w_in, v_ssd_conv_w, v_ssd_conv_b, v_ssd_dt_bias, v_ssd_a_log, v_ssd_d, v_ssd_norm_w, v_attn_sinks, v_conf_dw_w, v_conf_dw_b, v_conf_ln_w, v_conf_ln_b, v_w_out, v_final_norm_w):
    w = dict(norm_w=norm_w, w_in=w_in, ssd_conv_w=ssd_conv_w, ssd_conv_b=ssd_conv_b, ssd_dt_bias=ssd_dt_bias,
             ssd_a_log=ssd_a_log, ssd_d=ssd_d, ssd_norm_w=ssd_norm_w, attn_sinks=attn_sinks, conf_dw_w=conf_dw_w,
             conf_dw_b=conf_dw_b, conf_ln_w=conf_ln_w, conf_ln_b=conf_ln_b, w_out=w_out, final_norm_w=final_norm_w)
    m = dict(norm_w=m_norm_w, w_in=m_w_in, ssd_conv_w=m_ssd_conv_w, ssd_conv_b=m_ssd_conv_b,
             ssd_dt_bias=m_ssd_dt_bias, ssd_a_log=m_ssd_a_log, ssd_d=m_ssd_d, ssd_norm_w=m_ssd_norm_w,
             attn_sinks=m_attn_sinks, conf_dw_w=m_conf_dw_w, conf_dw_b=m_conf_dw_b, conf_ln_w=m_conf_ln_w,
             conf_ln_b=m_conf_ln_b, w_out=m_w_out, final_norm_w=m_final_norm_w)
    v = dict(norm_w=v_norm_w, w_in=v_w_in, ssd_conv_w=v_ssd_conv_w, ssd_conv_b=v_ssd_conv_b,
             ssd_dt_bias=v_ssd_dt_bias, ssd_a_log=v_ssd_a_log, ssd_d=v_ssd_d, ssd_norm_w=v_ssd_norm_w,
             attn_sinks=v_attn_sinks, conf_dw_w=v_conf_dw_w, conf_dw_b=v_conf_dw_b, conf_ln_w=v_conf_ln_w,
             conf_ln_b=v_conf_ln_b, w_out=v_w_out, final_norm_w=v_final_norm_w)
    depth = w_in.shape[0]
    me = 2 * lax.axis_index("x") + lax.axis_index("y")

    assert depth == 2
    w_in_t = jnp.transpose(w_in, (2, 0, 1))
    w_in_b = _cast_cols_major(w_in_t, name="cast_w_in")
    w_out_b = [_cast_layer(w_out, li, name=f"cast_w_out_l{li}") for li in range(depth)]
    own0 = [w_in_b[0].reshape((2, -1) + w_in_b[0].shape[1:]), ssd_conv_w, conf_dw_w]
    gathered0 = _gather_weights(own0[:1], own0[1:], name="gather_weights_l0")
    g_in0, g_conv, g_dw = [lax.dynamic_update_index_in_dim(g_all, mine, me, 0)
                           for g_all, mine in zip(gathered0, own0)]
    own1 = [w_out_b[0], w_in_b[1], w_out_b[1]]
    pending1, token1 = _split_start(own1, "bcast", gathered0[0], name="gather_rest_start")
    rest = {}

    def small_full(li):
        return (jnp.concatenate([g_conv[p, li] for p in range(N_CHIPS)], axis=1),
                jnp.concatenate([g_dw[p, li] for p in range(N_CHIPS)], axis=1))

    def w_out_l0(after):
        mine1, landed = _split_wait(pending1, len(own1), "bcast", after, name="gather_rest_wait")
        rest["landed"] = [lax.dynamic_update_index_in_dim(g_all, mine, me, 0) for g_all, mine in zip(landed, mine1)]
        return rest["landed"][0].reshape(-1, w_out.shape[2])

    def params_l0(_):
        w_in_p = _padded_from_chips([g_in0[p].reshape(w_in_b[0].shape) for p in range(N_CHIPS)])
        return _layer_params(0, w_in_p, w_out_l0, *small_full(0), w)

    def params_l1(_):
        _, g_in1, g_out1 = rest["landed"]
        w_in_p = _padded_from_chips([g_in1[p] for p in range(N_CHIPS)])
        return _layer_params(1, w_in_p, g_out1.reshape(-1, g_out1.shape[-1]), *small_full(1), w)

    c = lax.axis_index("c")
    cols = w_in.shape[2]
    rows_out = w_out.shape[1]

    def grad_parts(g):
        dw = g["w_in_p"]
        return [dw.reshape(1, 2, dw.shape[0] // 2, dw.shape[1]),
                g["w_out"].reshape(N_CHIPS, 2, rows_out // 2, D_MODEL)]

    def pair_sums(parts, sib, tag):
        s_in, s_out = [_pair_sum(p, sb, c, MXU_DTYPE, name=f"grad_pair_sum_{k}_{tag}")
                       for k, (p, sb) in enumerate(zip(parts, sib))]
        return [jnp.stack([_chip_part_from_padded(s_in[0], p, cols) for p in range(N_CHIPS)]), s_out]

    split = {"reduced": [lax.empty((depth, 2, w_in.shape[1] // 2, cols), F32),
                         lax.empty((depth, 2, rows_out // 2, D_MODEL), F32)]}

    def chip_sums(landed, sent, li, which=(0, 1)):
        filled = [lax.dynamic_update_index_in_dim(r, lax.dynamic_index_in_dim(sk, me, 0, keepdims=False), me, 0)
                  for r, sk in zip(landed, sent)]
        tag = "".join(str(k) for k in which)
        halves = [_sum_lead(r, split["reduced"][k], li, c, name=f"grad_chip_sum_{k}_l{li}")
                  for k, r in zip(which, filled)]
        for k, buf in zip(which, _pair_gather(halves, li, name=f"grad_pair_gather_{tag}_l{li}")):
            split["reduced"][k] = buf

    def bwd_hooks(li):
        def after_dw(dw_in_p, dw_out):
            parts = grad_parts(dict(w_in_p=dw_in_p, w_out=dw_out))
            state, token = _split_start(parts, "swap", None, name=f"grad_swap_l{li}_start")
            if li > 0:
                split[f"swap{li}"] = (parts, state)
                return token
            mine, sib = _split_wait(state, len(parts), "swap", token, name="grad_swap_l0_wait")
            split["scatter0"], token = _split_start(pair_sums(mine, sib, "l0"), "scatter", None,
                                                    name="grad_scatter_l0_start")
            return token

        hooks = {"after_dw": after_dw}
        if li == depth - 2:
            parts, swap_state = split[f"swap{depth - 1}"]

            def after_dycat(dycat):
                mine, sib = _split_wait(swap_state, len(parts), "swap", dycat, name="grad_swap_l1_wait")
                sent = pair_sums(mine, sib, "l1")
                split["scatter"], token = _split_start(sent, "scatter", None, name="grad_scatter_l1_start")
                return token

            def after_attn(dproj):
                sent, landed = _split_wait(split["scatter"], len(parts), "scatter", dproj,
                                           name="grad_scatter_l1_wait")
                chip_sums(landed, sent, depth - 1)

            hooks.update(after_dycat=after_dycat, after_attn=after_attn)
        return hooks

    loss, grad_x, grads, dfinal = _local_step(x, loss_target, [params_l0, params_l1], final_norm_w,
                                              first_after=token1, bwd_hooks=bwd_hooks)

    small_list = [grads[li][n] for li in range(depth) for n in SMALL]
    small_list += [grads[li][n] for li in range(depth) for n in ("ssd_conv_w", "conf_dw_w")]
    small_list += [dfinal, loss.reshape(1)]
    small_shapes = [a.shape for a in small_list]
    reduced = _unpack(_allreduce_small(_pack(small_list), name="allreduce_small"), small_shapes)
    ns = len(SMALL)
    g = {n: jnp.stack([reduced[li * ns + i] for li in range(depth)]) for i, n in enumerate(SMALL)}
    conv_w_cols, dw_w_cols = ssd_conv_w.shape[2], conf_dw_w.shape[2]
    g["ssd_conv_w"] = jnp.stack([lax.dynamic_slice_in_dim(reduced[depth * ns + 2 * li], me * conv_w_cols,
                                                          conv_w_cols, axis=1) for li in range(depth)])
    g["conf_dw_w"] = jnp.stack([lax.dynamic_slice_in_dim(reduced[depth * ns + 2 * li + 1], me * dw_w_cols,
                                                         dw_w_cols, axis=1) for li in range(depth)])
    g["final_norm_w"] = reduced[-2]
    loss_total = reduced[-1][0]

    small_names = [n for n in WEIGHTS if n not in ("w_in", "w_out")]

    def as2d(a):
        return a.reshape(1, -1) if a.ndim == 1 else a

    deltas, new_ms, new_vs = _adam_small(*[[as2d(src[n]) for n in small_names] for src in (w, g, m, v)],
                                         name="adam_small")

    sent0, landed0 = _split_wait(split["scatter0"], 2, "scatter", deltas[0], name="grad_scatter_l0_wait")
    chip_sums(landed0, sent0, 0)
    g_w_in = split["reduced"][0].reshape(w_in.shape)
    g_w_out = split["reduced"][1].reshape(w_out.shape)
    outs_g, outs_d, outs_m, outs_v = {"w_in": g_w_in, "w_out": g_w_out}, {}, {}, {}
    outs_d["w_out"], outs_m["w_out"], outs_v["w_out"] = _adam_big(w_out, g_w_out, m_w_out, v_w_out,
                                                                  name="adam_w_out")
    to_cols, from_cols = (2, 0, 1), (1, 2, 0)
    outs_d["w_in"], outs_m["w_in"], outs_v["w_in"] = [
        jnp.transpose(a, from_cols) for a in _adam_cols_major(
            *[jnp.transpose(a, to_cols) for a in (w_in, g_w_in, m_w_in, v_w_in)], name="adam_w_in")]
    for n, dn, mn, vn in zip(small_names, deltas, new_ms, new_vs):
        outs_g[n], outs_d[n], outs_m[n], outs_v[n] = (g[n], dn.reshape(w[n].shape), mn.reshape(w[n].shape),
                                                      vn.reshape(w[n].shape))
    return (loss_total, grad_x, *[outs_g[n] for n in WEIGHTS], *[outs_d[n] for n in WEIGHTS],
            *[outs_m[n] for n in WEIGHTS], *[outs_v[n] for n in WEIGHTS])
```
